```python
import jax, jax.numpy as jnp
from jax import lax
import numpy as np

D_MODEL = 1024
BATCH = 8
SEQ = 2048
DEPTH = 2

CHUNK = 64
HEAD_DIM = 64
N_HEADS_SB = 4
N_HEADS_CH = 8
N_HEADS_FOX = 4
W_SB = N_HEADS_SB * HEAD_DIM
W_CH = N_HEADS_CH * HEAD_DIM
W_FOX = N_HEADS_FOX * HEAD_DIM
LEFT_CHUNKS = 8
BAND = (LEFT_CHUNKS + 1) * CHUNK
MAX_REL = 128
N_REL = 2 * MAX_REL + 1
Q_BLOCK = 128
N_BRANCH = 3
D_FF = ((8 * D_MODEL // 3 + 127) // 128) * 128
QKV_WIDTH = 3 * (W_SB + W_CH + W_FOX)
FORGET_OFFSET = QKV_WIDTH
IN_WIDTH = QKV_WIDTH + N_HEADS_FOX + N_BRANCH * D_MODEL
SPLIT_SIZES = (W_SB, W_SB, W_SB, W_CH, W_CH, W_CH, W_FOX, W_FOX, W_FOX, N_HEADS_FOX, D_MODEL, D_MODEL, D_MODEL)
SPLIT_POINTS = tuple(int(v) for v in np.cumsum(SPLIT_SIZES)[:-1])
FORGET_BIAS_INIT = 4.0
RMS_EPS = 1e-6
NEG = -1e30

kernel_name = "hybrid_stickbreak_chunkrel_fox_macaron"


def rmsnorm(x, g):
    xf = x.astype(jnp.float32)
    y = xf * lax.rsqrt(jnp.mean(xf * xf, axis=-1, keepdims=True) + RMS_EPS)
    return (y * g.astype(jnp.float32)).astype(x.dtype)


def swiglu(h, w_in, w_out):
    gate, up = jnp.split(h @ w_in, 2, axis=-1)
    return (jax.nn.silu(gate) * up) @ w_out


def split_heads(t, n_heads):
    b, s, _ = t.shape
    return t.reshape(b, s, n_heads, HEAD_DIM).transpose(0, 2, 1, 3)


def merge_heads(t):
    b, h, s, d = t.shape
    return t.transpose(0, 2, 1, 3).reshape(b, s, h * d)


def stick_breaking_attention(q, k, v):
    T = q.shape[2]
    scale = HEAD_DIM ** -0.5
    outs = []
    for start in range(0, T, Q_BLOCK):
        end = start + Q_BLOCK
        z = jnp.einsum('bhqd,bhkd->bhqk', q[:, :, start:end], k[:, :, :end]).astype(jnp.float32) * scale
        strict = jnp.arange(end)[None, :] < jnp.arange(start, end)[:, None]
        log_beta = jax.nn.log_sigmoid(z)
        log_fail = jnp.where(strict, jax.nn.log_sigmoid(-z), 0.0)
        between = lax.cumsum(log_fail, axis=3, reverse=True) - log_fail
        w = jnp.where(strict, jnp.exp(log_beta + between), 0.0)
        outs.append(jnp.einsum('bhqk,bhkd->bhqd', w.astype(v.dtype), v[:, :, :end]))
    return jnp.concatenate(outs, axis=2)


def forgetting_attention(q, k, v, log_f):
    T = q.shape[2]
    scale = HEAD_DIM ** -0.5
    F = jnp.cumsum(log_f, axis=-1)
    outs = []
    for start in range(0, T, Q_BLOCK):
        end = start + Q_BLOCK
        z = jnp.einsum('bhqd,bhkd->bhqk', q[:, :, start:end], k[:, :, :end]).astype(jnp.float32) * scale
        z = z + F[:, :, start:end, None] - F[:, :, None, :end]
        causal = jnp.arange(end)[None, :] <= jnp.arange(start, end)[:, None]
        p = jax.nn.softmax(jnp.where(causal, z, NEG), axis=-1)
        outs.append(jnp.einsum('bhqk,bhkd->bhqd', p.astype(v.dtype), v[:, :, :end]))
    return jnp.concatenate(outs, axis=2)


def chunked_relpos_attention(q, k, v, rel_table):
    B, H, T, Dh = q.shape
    nc = T // CHUNK
    scale = Dh ** -0.5
    qc = q.reshape(B, H, nc, CHUNK, Dh)
    pad = ((0, 0), (0, 0), (LEFT_CHUNKS * CHUNK, 0), (0, 0))
    kp = jnp.pad(k, pad).reshape(B, H, nc + LEFT_CHUNKS, CHUNK, Dh)
    vp = jnp.pad(v, pad).reshape(B, H, nc + LEFT_CHUNKS, CHUNK, Dh)
    band_idx = jnp.arange(nc)[:, None] + jnp.arange(LEFT_CHUNKS + 1)[None, :]
    k_band = kp[:, :, band_idx].reshape(B, H, nc, BAND, Dh)
    v_band = vp[:, :, band_idx].reshape(B, H, nc, BAND, Dh)
    z = jnp.einsum('bhcqd,bhckd->bhcqk', qc, k_band).astype(jnp.float32) * scale
    rel = (jnp.arange(CHUNK)[:, None] + LEFT_CHUNKS * CHUNK) - jnp.arange(BAND)[None, :]
    rel = jnp.clip(rel, -MAX_REL, MAX_REL) + MAX_REL
    bias = rel_table[rel].astype(jnp.float32).transpose(2, 0, 1)
    z = z + bias[None, :, None]
    key_abs = (jnp.arange(nc)[:, None] - LEFT_CHUNKS) * CHUNK + jnp.arange(BAND)[None, :]
    valid = key_abs >= 0
    p = jax.nn.softmax(jnp.where(valid[None, None, :, None, :], z, NEG), axis=-1)
    o = jnp.einsum('bhcqk,bhckd->bhcqd', p.astype(v.dtype), v_band)
    return o.reshape(B, H, T, Dh)


def hybrid_layer(x, g_ffn1, w_ffn1_in, w_ffn1_out, g_mix, w_in, b_in, rel_bias,
                 w_br_sb, w_br_ch, w_br_fox, w_out, g_ffn2, w_ffn2_in, w_ffn2_out):
    x = x + 0.5 * swiglu(rmsnorm(x, g_ffn1), w_ffn1_in, w_ffn1_out)
    h = rmsnorm(x, g_mix)
    proj = h @ w_in + b_in
    (q_a, k_a, v_a, q_b, k_b, v_b, q_c, k_c, v_c,
     f_logit, g_a, g_b, g_c) = jnp.split(proj, list(SPLIT_POINTS), axis=-1)
    o_a = stick_breaking_attention(split_heads(q_a, N_HEADS_SB), split_heads(k_a, N_HEADS_SB),
                                   split_heads(v_a, N_HEADS_SB))
    o_b = chunked_relpos_attention(split_heads(q_b, N_HEADS_CH), split_heads(k_b, N_HEADS_CH),
                                   split_heads(v_b, N_HEADS_CH), rel_bias)
    log_f = jax.nn.log_sigmoid(f_logit.astype(jnp.float32)).transpose(0, 2, 1)
    o_c = forgetting_attention(split_heads(q_c, N_HEADS_FOX), split_heads(k_c, N_HEADS_FOX),
                               split_heads(v_c, N_HEADS_FOX), log_f)
    merged = (jax.nn.sigmoid(g_a) * (merge_heads(o_a) @ w_br_sb)
              + jax.nn.sigmoid(g_b) * (merge_heads(o_b) @ w_br_ch)
              + jax.nn.sigmoid(g_c) * (merge_heads(o_c) @ w_br_fox))
    x = x + merged @ w_out
    x = x + 0.5 * swiglu(rmsnorm(x, g_ffn2), w_ffn2_in, w_ffn2_out)
    return x


def _fwd_setup_inputs(seed: int = 0) -> dict:
    key = jax.random.key(seed)
    ks = jax.random.split(key, 18)

    def dense(k, shape, fan_in):
        return jax.random.normal(k, shape, jnp.float32) * fan_in ** -0.5

    def gain(k, shape):
        return 1.0 + 0.05 * jax.random.normal(k, shape, jnp.float32)

    b_in = 0.02 * jax.random.normal(ks[6], (DEPTH, IN_WIDTH), jnp.float32)
    b_in = b_in.at[:, FORGET_OFFSET:FORGET_OFFSET + N_HEADS_FOX].add(FORGET_BIAS_INIT)
    return {
        "x": jax.random.normal(ks[0], (BATCH, SEQ, D_MODEL), jnp.float32),
        "g_ffn1": gain(ks[1], (DEPTH, D_MODEL)),
        "w_ffn1_in": dense(ks[2], (DEPTH, D_MODEL, 2 * D_FF), D_MODEL),
        "w_ffn1_out": dense(ks[3], (DEPTH, D_FF, D_MODEL), D_FF),
        "g_mix": gain(ks[4], (DEPTH, D_MODEL)),
        "w_in": dense(ks[5], (DEPTH, D_MODEL, IN_WIDTH), D_MODEL),
        "b_in": b_in,
        "rel_bias": 0.1 * jax.random.normal(ks[7], (DEPTH, N_REL, N_HEADS_CH), jnp.float32),
        "w_br_sb": dense(ks[8], (DEPTH, W_SB, D_MODEL), W_SB),
        "w_br_ch": dense(ks[9], (DEPTH, W_CH, D_MODEL), W_CH),
        "w_br_fox": dense(ks[10], (DEPTH, W_FOX, D_MODEL), W_FOX),
        "w_out": dense(ks[11], (DEPTH, D_MODEL, D_MODEL), D_MODEL),
        "g_ffn2": gain(ks[12], (DEPTH, D_MODEL)),
        "w_ffn2_in": dense(ks[13], (DEPTH, D_MODEL, 2 * D_FF), D_MODEL),
        "w_ffn2_out": dense(ks[14], (DEPTH, D_FF, D_MODEL), D_FF),
        "g_final": gain(ks[15], (D_MODEL,)),
    }


def _fwd_reference(x, g_ffn1, w_ffn1_in, w_ffn1_out, g_mix, w_in, b_in, rel_bias,
              w_br_sb, w_br_ch, w_br_fox, w_out, g_ffn2, w_ffn2_in, w_ffn2_out, g_final):
    for layer in range(DEPTH):
        x = hybrid_layer(x, g_ffn1[layer], w_ffn1_in[layer], w_ffn1_out[layer], g_mix[layer],
                         w_in[layer], b_in[layer], rel_bias[layer], w_br_sb[layer], w_br_ch[layer],
                         w_br_fox[layer], w_out[layer], g_ffn2[layer], w_ffn2_in[layer],
                         w_ffn2_out[layer])
    return rmsnorm(x, g_final)


import jax as _jax
import jax.numpy as _jnp

TWIN_FORMAT = 'train_step'
FWD_PARAMS = ['x', 'g_ffn1', 'w_ffn1_in', 'w_ffn1_out', 'g_mix', 'w_in', 'b_in', 'rel_bias', 'w_br_sb', 'w_br_ch', 'w_br_fox', 'w_out', 'g_ffn2', 'w_ffn2_in', 'w_ffn2_out', 'g_final']
TWIN_WEIGHTS = ['g_ffn1', 'w_ffn1_in', 'w_ffn1_out', 'g_mix', 'w_in', 'b_in', 'rel_bias', 'w_br_sb', 'w_br_ch', 'w_br_fox', 'w_out', 'g_ffn2', 'w_ffn2_in', 'w_ffn2_out', 'g_final']
TWIN_DIFF_INPUT = 'x'
TWIN_INPUTS = ['x', 'g_ffn1', 'w_ffn1_in', 'w_ffn1_out', 'g_mix', 'w_in', 'b_in', 'rel_bias', 'w_br_sb', 'w_br_ch', 'w_br_fox', 'w_out', 'g_ffn2', 'w_ffn2_in', 'w_ffn2_out', 'g_final', 'loss_target', 'm_g_ffn1', 'm_w_ffn1_in', 'm_w_ffn1_out', 'm_g_mix', 'm_w_in', 'm_b_in', 'm_rel_bias', 'm_w_br_sb', 'm_w_br_ch', 'm_w_br_fox', 'm_w_out', 'm_g_ffn2', 'm_w_ffn2_in', 'm_w_ffn2_out', 'm_g_final', 'v_g_ffn1', 'v_w_ffn1_in', 'v_w_ffn1_out', 'v_g_mix', 'v_w_in', 'v_b_in', 'v_rel_bias', 'v_w_br_sb', 'v_w_br_ch', 'v_w_br_fox', 'v_w_out', 'v_g_ffn2', 'v_w_ffn2_in', 'v_w_ffn2_out', 'v_g_final']
TWIN_OUTPUTS = ['loss', 'grad_x', 'grad_g_ffn1', 'grad_w_ffn1_in', 'grad_w_ffn1_out', 'grad_g_mix', 'grad_w_in', 'grad_b_in', 'grad_rel_bias', 'grad_w_br_sb', 'grad_w_br_ch', 'grad_w_br_fox', 'grad_w_out', 'grad_g_ffn2', 'grad_w_ffn2_in', 'grad_w_ffn2_out', 'grad_g_final', 'delta_g_ffn1', 'delta_w_ffn1_in', 'delta_w_ffn1_out', 'delta_g_mix', 'delta_w_in', 'delta_b_in', 'delta_rel_bias', 'delta_w_br_sb', 'delta_w_br_ch', 'delta_w_br_fox', 'delta_w_out', 'delta_g_ffn2', 'delta_w_ffn2_in', 'delta_w_ffn2_out', 'delta_g_final', 'new_m_g_ffn1', 'new_m_w_ffn1_in', 'new_m_w_ffn1_out', 'new_m_g_mix', 'new_m_w_in', 'new_m_b_in', 'new_m_rel_bias', 'new_m_w_br_sb', 'new_m_w_br_ch', 'new_m_w_br_fox', 'new_m_w_out', 'new_m_g_ffn2', 'new_m_w_ffn2_in', 'new_m_w_ffn2_out', 'new_m_g_final', 'new_v_g_ffn1', 'new_v_w_ffn1_in', 'new_v_w_ffn1_out', 'new_v_g_mix', 'new_v_w_in', 'new_v_b_in', 'new_v_rel_bias', 'new_v_w_br_sb', 'new_v_w_br_ch', 'new_v_w_br_fox', 'new_v_w_out', 'new_v_g_ffn2', 'new_v_w_ffn2_in', 'new_v_w_ffn2_out', 'new_v_g_final']
TWIN_LEAF_KINDS = {'loss': 'loss', 'grad_x': 'grad_x', 'grad_g_ffn1': 'grad_w', 'grad_w_ffn1_in': 'grad_w', 'grad_w_ffn1_out': 'grad_w', 'grad_g_mix': 'grad_w', 'grad_w_in': 'grad_w', 'grad_b_in': 'grad_w', 'grad_rel_bias': 'grad_w', 'grad_w_br_sb': 'grad_w', 'grad_w_br_ch': 'grad_w', 'grad_w_br_fox': 'grad_w', 'grad_w_out': 'grad_w', 'grad_g_ffn2': 'grad_w', 'grad_w_ffn2_in': 'grad_w', 'grad_w_ffn2_out': 'grad_w', 'grad_g_final': 'grad_w', 'delta_g_ffn1': 'delta_w', 'delta_w_ffn1_in': 'delta_w', 'delta_w_ffn1_out': 'delta_w', 'delta_g_mix': 'delta_w', 'delta_w_in': 'delta_w', 'delta_b_in': 'delta_w', 'delta_rel_bias': 'delta_w', 'delta_w_br_sb': 'delta_w', 'delta_w_br_ch': 'delta_w', 'delta_w_br_fox': 'delta_w', 'delta_w_out': 'delta_w', 'delta_g_ffn2': 'delta_w', 'delta_w_ffn2_in': 'delta_w', 'delta_w_ffn2_out': 'delta_w', 'delta_g_final': 'delta_w', 'new_m_g_ffn1': 'new_m', 'new_m_w_ffn1_in': 'new_m', 'new_m_w_ffn1_out': 'new_m', 'new_m_g_mix': 'new_m', 'new_m_w_in': 'new_m', 'new_m_b_in': 'new_m', 'new_m_rel_bias': 'new_m', 'new_m_w_br_sb': 'new_m', 'new_m_w_br_ch': 'new_m', 'new_m_w_br_fox': 'new_m', 'new_m_w_out': 'new_m', 'new_m_g_ffn2': 'new_m', 'new_m_w_ffn2_in': 'new_m', 'new_m_w_ffn2_out': 'new_m', 'new_m_g_final': 'new_m', 'new_v_g_ffn1': 'new_v', 'new_v_w_ffn1_in': 'new_v', 'new_v_w_ffn1_out': 'new_v', 'new_v_g_mix': 'new_v', 'new_v_w_in': 'new_v', 'new_v_b_in': 'new_v', 'new_v_rel_bias': 'new_v', 'new_v_w_br_sb': 'new_v', 'new_v_w_br_ch': 'new_v', 'new_v_w_br_fox': 'new_v', 'new_v_w_out': 'new_v', 'new_v_g_ffn2': 'new_v', 'new_v_w_ffn2_in': 'new_v', 'new_v_w_ffn2_out': 'new_v', 'new_v_g_final': 'new_v'}


def _forward(args):
    return _fwd_reference(*[args[k] for k in FWD_PARAMS])


def _output_shape():
    out = _jax.eval_shape(lambda: _forward(_fwd_setup_inputs(0)))
    return out.shape, out.dtype

N_MICROBATCH = 1
ADAM_LR = 0.001
ADAM_B1 = 0.9
ADAM_B2 = 0.999
ADAM_EPS = 1e-08
ADAM_WD = 0.01
ADAM_STEP = 10
PER_EXAMPLE_BATCH_AXIS = {'x': 0, 'loss_target': 0}
SHARED_INPUTS = []
_WEIGHT_DTYPES = {'g_ffn1': _jnp.float32, 'w_ffn1_in': _jnp.float32, 'w_ffn1_out': _jnp.float32, 'g_mix': _jnp.float32, 'w_in': _jnp.float32, 'b_in': _jnp.float32, 'rel_bias': _jnp.float32, 'w_br_sb': _jnp.float32, 'w_br_ch': _jnp.float32, 'w_br_fox': _jnp.float32, 'w_out': _jnp.float32, 'g_ffn2': _jnp.float32, 'w_ffn2_in': _jnp.float32, 'w_ffn2_out': _jnp.float32, 'g_final': _jnp.float32}
MOMENT_SCALE = {'g_ffn1': 5.642383e-02, 'w_ffn1_in': 2.363370e-02, 'w_ffn1_out': 3.852799e-02, 'g_mix': 5.943317e-02, 'w_in': 2.439351e-02, 'b_in': 5.078516e-02, 'rel_bias': 6.199917e-03, 'w_br_sb': 4.168760e-02, 'w_br_ch': 1.015904e-02, 'w_br_fox': 1.653454e-02, 'w_out': 4.547053e-02, 'g_ffn2': 4.840139e-02, 'w_ffn2_in': 2.024611e-02, 'w_ffn2_out': 3.313750e-02, 'g_final': 1.599555e+01}


def _to_microbatches(a, axis):
    t = _jnp.moveaxis(a, axis, 0)
    t = t.reshape((N_MICROBATCH, t.shape[0] // N_MICROBATCH) + t.shape[1:])
    return _jnp.moveaxis(t, 1, axis + 1)


def setup_inputs(seed: int = 0) -> dict:
    inp = _fwd_setup_inputs(seed)
    key = _jax.random.fold_in(_jax.random.key(seed), 7919)
    shape, _ = _output_shape()
    out = dict(inp)
    out["loss_target"] = _jax.random.normal(_jax.random.fold_in(key, 0), shape, _jnp.float32)
    for i, name in enumerate(TWIN_WEIGHTS):
        w = inp[name].astype(_jnp.float32)
        if MOMENT_SCALE is None:
            s = _jnp.sqrt(_jnp.mean(_jnp.square(w)) + 1e-30)
        else:
            s = MOMENT_SCALE[name]
        km, kv = _jax.random.split(_jax.random.fold_in(key, i + 1))
        out[name] = w
        out["m_" + name] = s * _jax.random.normal(km, w.shape, _jnp.float32)
        out["v_" + name] = (s * s) * _jax.random.uniform(kv, w.shape, _jnp.float32, 0.5, 1.5)
    if N_MICROBATCH > 1:
        for name, axis in PER_EXAMPLE_BATCH_AXIS.items():
            out[name] = _to_microbatches(out[name], axis)
    return {'x': out['x'], 'g_ffn1': out['g_ffn1'], 'w_ffn1_in': out['w_ffn1_in'], 'w_ffn1_out': out['w_ffn1_out'], 'g_mix': out['g_mix'], 'w_in': out['w_in'], 'b_in': out['b_in'], 'rel_bias': out['rel_bias'], 'w_br_sb': out['w_br_sb'], 'w_br_ch': out['w_br_ch'], 'w_br_fox': out['w_br_fox'], 'w_out': out['w_out'], 'g_ffn2': out['g_ffn2'], 'w_ffn2_in': out['w_ffn2_in'], 'w_ffn2_out': out['w_ffn2_out'], 'g_final': out['g_final'], 'loss_target': out['loss_target'], 'm_g_ffn1': out['m_g_ffn1'], 'm_w_ffn1_in': out['m_w_ffn1_in'], 'm_w_ffn1_out': out['m_w_ffn1_out'], 'm_g_mix': out['m_g_mix'], 'm_w_in': out['m_w_in'], 'm_b_in': out['m_b_in'], 'm_rel_bias': out['m_rel_bias'], 'm_w_br_sb': out['m_w_br_sb'], 'm_w_br_ch': out['m_w_br_ch'], 'm_w_br_fox': out['m_w_br_fox'], 'm_w_out': out['m_w_out'], 'm_g_ffn2': out['m_g_ffn2'], 'm_w_ffn2_in': out['m_w_ffn2_in'], 'm_w_ffn2_out': out['m_w_ffn2_out'], 'm_g_final': out['m_g_final'], 'v_g_ffn1': out['v_g_ffn1'], 'v_w_ffn1_in': out['v_w_ffn1_in'], 'v_w_ffn1_out': out['v_w_ffn1_out'], 'v_g_mix': out['v_g_mix'], 'v_w_in': out['v_w_in'], 'v_b_in': out['v_b_in'], 'v_rel_bias': out['v_rel_bias'], 'v_w_br_sb': out['v_w_br_sb'], 'v_w_br_ch': out['v_w_br_ch'], 'v_w_br_fox': out['v_w_br_fox'], 'v_w_out': out['v_w_out'], 'v_g_ffn2': out['v_g_ffn2'], 'v_w_ffn2_in': out['v_w_ffn2_in'], 'v_w_ffn2_out': out['v_w_ffn2_out'], 'v_g_final': out['v_g_final']}


def _loss(weights, diff, rest, loss_target):
    with _jax.named_scope("forward"):
        args = {**rest, TWIN_DIFF_INPUT: diff, **{k: w.astype(_WEIGHT_DTYPES[k]) for k, w in weights.items()}}
        y = _forward(args)
    with _jax.named_scope("loss_head"):
        err = _jnp.square(y.astype(_jnp.float32) - loss_target)
        return 0.5 * _jnp.sum(_jnp.mean(err, axis=-1)) if err.ndim else 0.5 * err


def _adamw(w, g, m, v):
    m = ADAM_B1 * m + (1.0 - ADAM_B1) * g
    v = ADAM_B2 * v + (1.0 - ADAM_B2) * _jnp.square(g)
    m_hat = m / (1.0 - ADAM_B1 ** ADAM_STEP)
    v_hat = v / (1.0 - ADAM_B2 ** ADAM_STEP)
    delta = -ADAM_LR * (m_hat / (_jnp.sqrt(v_hat) + ADAM_EPS) + ADAM_WD * w)
    return delta, m, v


def reference(x, g_ffn1, w_ffn1_in, w_ffn1_out, g_mix, w_in, b_in, rel_bias, w_br_sb, w_br_ch, w_br_fox, w_out, g_ffn2, w_ffn2_in, w_ffn2_out, g_final, loss_target, m_g_ffn1, m_w_ffn1_in, m_w_ffn1_out, m_g_mix, m_w_in, m_b_in, m_rel_bias, m_w_br_sb, m_w_br_ch, m_w_br_fox, m_w_out, m_g_ffn2, m_w_ffn2_in, m_w_ffn2_out, m_g_final, v_g_ffn1, v_w_ffn1_in, v_w_ffn1_out, v_g_mix, v_w_in, v_b_in, v_rel_bias, v_w_br_sb, v_w_br_ch, v_w_br_fox, v_w_out, v_g_ffn2, v_w_ffn2_in, v_w_ffn2_out, v_g_final):
    given = dict(x=x, g_ffn1=g_ffn1, w_ffn1_in=w_ffn1_in, w_ffn1_out=w_ffn1_out, g_mix=g_mix, w_in=w_in, b_in=b_in, rel_bias=rel_bias, w_br_sb=w_br_sb, w_br_ch=w_br_ch, w_br_fox=w_br_fox, w_out=w_out, g_ffn2=g_ffn2, w_ffn2_in=w_ffn2_in, w_ffn2_out=w_ffn2_out, g_final=g_final, loss_target=loss_target, m_g_ffn1=m_g_ffn1, m_w_ffn1_in=m_w_ffn1_in, m_w_ffn1_out=m_w_ffn1_out, m_g_mix=m_g_mix, m_w_in=m_w_in, m_b_in=m_b_in, m_rel_bias=m_rel_bias, m_w_br_sb=m_w_br_sb, m_w_br_ch=m_w_br_ch, m_w_br_fox=m_w_br_fox, m_w_out=m_w_out, m_g_ffn2=m_g_ffn2, m_w_ffn2_in=m_w_ffn2_in, m_w_ffn2_out=m_w_ffn2_out, m_g_final=m_g_final, v_g_ffn1=v_g_ffn1, v_w_ffn1_in=v_w_ffn1_in, v_w_ffn1_out=v_w_ffn1_out, v_g_mix=v_g_mix, v_w_in=v_w_in, v_b_in=v_b_in, v_rel_bias=v_rel_bias, v_w_br_sb=v_w_br_sb, v_w_br_ch=v_w_br_ch, v_w_br_fox=v_w_br_fox, v_w_out=v_w_out, v_g_ffn2=v_g_ffn2, v_w_ffn2_in=v_w_ffn2_in, v_w_ffn2_out=v_w_ffn2_out, v_g_final=v_g_final)
    weights = {n: given[n] for n in TWIN_WEIGHTS}
    shared = {n: given[n] for n in SHARED_INPUTS}
    per_example = {n: given[n] for n in ['x']}
    grad_fn = _jax.value_and_grad(_loss, argnums=(0, 1))

    def one_microbatch(ex, loss_target):
        ex = dict(ex)
        diff = ex.pop(TWIN_DIFF_INPUT)
        return grad_fn(weights, diff, {**shared, **ex}, loss_target)

    if N_MICROBATCH == 1:
        loss, (grad_w, grad_x) = one_microbatch(per_example, given["loss_target"])
    else:
        def body(carry, xs):
            loss_sum, grad_sum = carry
            l_k, (gw_k, gx_k) = one_microbatch(xs[0], xs[1])
            with _jax.named_scope("update"):
                return (loss_sum + l_k, _jax.tree.map(_jnp.add, grad_sum, gw_k)), gx_k

        init = (_jnp.zeros((), _jnp.float32), _jax.tree.map(_jnp.zeros_like, weights))
        (loss, grad_w), grad_x = _jax.lax.scan(body, init, (per_example, given["loss_target"]))
    with _jax.named_scope("update"):
        delta_w, new_m, new_v = {}, {}, {}
        for n in TWIN_WEIGHTS:
            delta_w[n], new_m[n], new_v[n] = _adamw(weights[n], grad_w[n], given["m_" + n], given["v_" + n])
    return (loss, grad_x, *[grad_w[n] for n in TWIN_WEIGHTS], *[delta_w[n] for n in TWIN_WEIGHTS],
            *[new_m[n] for n in TWIN_WEIGHTS], *[new_v[n] for n in TWIN_WEIGHTS])
```

```python
import functools

import numpy as np
import jax
import jax.numpy as jnp
from jax import lax
from jax.experimental import pallas as pl
from jax.experimental.pallas import tpu as pltpu

F32 = jnp.float32
BF16 = jnp.bfloat16

D_MODEL = 1024
DEPTH = 2
CHUNK = 64
HEAD_DIM = 64
H_SB, H_CH, H_FOX = 4, 8, 4
W_SB, W_CH, W_FOX = H_SB * HEAD_DIM, H_CH * HEAD_DIM, H_FOX * HEAD_DIM
LEFT_CHUNKS = 8
MAX_REL = 128
N_REL = 2 * MAX_REL + 1
D_FF = 2816
QKV_WIDTH = 3 * (W_SB + W_CH + W_FOX)
GATE_WIDTH = 3 * D_MODEL
IN_WIDTH = QKV_WIDTH + H_FOX + GATE_WIDTH
F_PAD = 128
P_WIDTH = QKV_WIDTH + GATE_WIDTH + F_PAD
RMS_EPS = 1e-6
NEG = -1e30
SCALE = HEAD_DIM ** -0.5
QB = 128
BAND = (LEFT_CHUNKS + 2) * CHUNK
BAND_PAD = BAND - CHUNK

ADAM_LR, ADAM_B1, ADAM_B2, ADAM_EPS, ADAM_WD, ADAM_STEP = 0.001, 0.9, 0.999, 1e-08, 0.01, 10

N_CHIPS = 4
PACK_COLS = 1024
VMEM_BUDGET = 52 * 1024 * 1024

NT = (((1,), (1,)), ((), ()))
TN = (((0,), (0,)), ((), ()))
NN = (((1,), (0,)), ((), ()))
MESH = pl.DeviceIdType.MESH


def _pcall(body, **kw):
    return pl.pallas_call(body, **kw)


def _pick(n, cap, mult=128):
    best = None
    d = mult
    while d <= min(n, cap):
        if n % d == 0:
            best = d
        d += mult
    return n if best is None else best


def _nbytes(shape, dtype):
    return int(np.prod(shape)) * jnp.dtype(dtype).itemsize


def _mm(a, b, *, mode, name, out_dtype=F32, alpha=1.0, bias=None, res=None, tm_cap=1024, tn_cap=512,
        tk_cap=2816):
    if mode == "tn":
        K, M = a.shape
    else:
        M, K = a.shape
    N = b.shape[0] if mode == "nt" else b.shape[1]
    tm = _pick(M, tm_cap)
    tn = _pick(N, tn_cap)
    tk = K if K <= tk_cap else _pick(K, tk_cap)
    nk = K // tk
    dn = {"nn": NN, "nt": NT, "tn": TN}[mode]

    a_spec = (pl.BlockSpec((tk, tm), lambda i, j, k: (k, i)) if mode == "tn"
              else pl.BlockSpec((tm, tk), lambda i, j, k: (i, k)))
    b_spec = (pl.BlockSpec((tn, tk), lambda i, j, k: (j, k)) if mode == "nt"
              else pl.BlockSpec((tk, tn), lambda i, j, k: (k, j)))
    in_specs = [a_spec, b_spec]
    args = [a, b]
    if bias is not None:
        in_specs.append(pl.BlockSpec((1, tn), lambda i, j, k: (0, j)))
        args.append(bias)
    if res is not None:
        in_specs.append(pl.BlockSpec((tm, tn), lambda i, j, k: (i, j)))
        args.append(res)

    est = 2 * (_nbytes((tm, tk), a.dtype) + _nbytes((tk, tn), b.dtype) + _nbytes((tm, tn), out_dtype))
    est += _nbytes((tm, tn), F32) * (3 if res is not None else 1)
    assert est < VMEM_BUDGET, (name, est)

    def body(*refs):
        a_ref, b_ref = refs[0], refs[1]
        pos = 2
        bias_ref = res_ref = None
        if bias is not None:
            bias_ref = refs[pos]
            pos += 1
        if res is not None:
            res_ref = refs[pos]
            pos += 1
        o_ref = refs[pos]
        acc_ref = refs[pos + 1] if nk > 1 else None

        part = lax.dot_general(a_ref[...].astype(BF16), b_ref[...].astype(BF16), dn,
                               preferred_element_type=F32)

        def finish(acc):
            if alpha != 1.0:
                acc = acc * alpha
            if bias_ref is not None:
                acc = acc + bias_ref[...]
            if res_ref is not None:
                acc = acc + res_ref[...]
            o_ref[...] = acc.astype(out_dtype)

        if nk == 1:
            finish(part)
        else:
            k = pl.program_id(2)

            @pl.when(k == 0)
            def _():
                acc_ref[...] = part

            @pl.when(k > 0)
            def _():
                acc_ref[...] += part

            @pl.when(k == nk - 1)
            def _():
                finish(acc_ref[...])

    return _pcall(
        body, name=name, grid=(M // tm, N // tn, nk), in_specs=in_specs,
        out_specs=pl.BlockSpec((tm, tn), lambda i, j, k: (i, j)),
        out_shape=jax.ShapeDtypeStruct((M, N), out_dtype),
        scratch_shapes=[pltpu.VMEM((tm, tn), F32)] if nk > 1 else [],
        compiler_params=pltpu.CompilerParams(dimension_semantics=("parallel", "parallel", "arbitrary")),
    )(*args)


def _rmsnorm_fwd(x, g, name):
    T, Dm = x.shape
    tm = _pick(T, 256, 8)

    def body(x_ref, g_ref, h_ref):
        xv = x_ref[...]
        rstd = lax.rsqrt(jnp.mean(xv * xv, axis=-1, keepdims=True) + RMS_EPS)
        h_ref[...] = (xv * rstd * g_ref[...]).astype(BF16)

    return _pcall(
        body, name=name, grid=(T // tm,),
        in_specs=[pl.BlockSpec((tm, Dm), lambda i: (i, 0)), pl.BlockSpec((1, Dm), lambda i: (0, 0))],
        out_specs=pl.BlockSpec((tm, Dm), lambda i: (i, 0)),
        out_shape=jax.ShapeDtypeStruct((T, Dm), BF16),
    )(x, g)


def _rmsnorm_bwd(x, g, dh, dres, name):
    T, Dm = x.shape
    tm = _pick(T, 256, 8)

    def body(x_ref, g_ref, dh_ref, dres_ref, dx_ref, dg_ref):
        xv = x_ref[...]
        rstd = lax.rsqrt(jnp.mean(xv * xv, axis=-1, keepdims=True) + RMS_EPS)
        xhat = xv * rstd
        dhv = dh_ref[...]
        dyg = dhv * g_ref[...]
        dx_ref[...] = dres_ref[...] + rstd * (dyg - xhat * jnp.mean(dyg * xhat, axis=-1, keepdims=True))
        part = jnp.sum(dhv * xhat, axis=0, keepdims=True)

        @pl.when(pl.program_id(0) == 0)
        def _():
            dg_ref[...] = part

        @pl.when(pl.program_id(0) > 0)
        def _():
            dg_ref[...] += part

    row = pl.BlockSpec((tm, Dm), lambda i: (i, 0))
    vec = pl.BlockSpec((1, Dm), lambda i: (0, 0))
    return _pcall(
        body, name=name, grid=(T // tm,), in_specs=[row, vec, row, row], out_specs=[row, vec],
        out_shape=[jax.ShapeDtypeStruct((T, Dm), F32), jax.ShapeDtypeStruct((1, Dm), F32)],
        compiler_params=pltpu.CompilerParams(dimension_semantics=("arbitrary",)),
    )(x, g, dh, dres)


def _final_loss(x, g, tgt, name):
    T, Dm = x.shape
    tm = _pick(T, 256, 8)

    def body(x_ref, g_ref, t_ref, dx_ref, dg_ref, loss_ref):
        xv = x_ref[...]
        gv = g_ref[...]
        rstd = lax.rsqrt(jnp.mean(xv * xv, axis=-1, keepdims=True) + RMS_EPS)
        xhat = xv * rstd
        err = xhat * gv - t_ref[...]
        part_loss = 0.5 * jnp.sum(jnp.mean(err * err, axis=-1, keepdims=True), axis=0, keepdims=True)
        dy = err * (1.0 / Dm)
        dyg = dy * gv
        dx_ref[...] = rstd * (dyg - xhat * jnp.mean(dyg * xhat, axis=-1, keepdims=True))
        part_g = jnp.sum(dy * xhat, axis=0, keepdims=True)
        part_l = jnp.broadcast_to(part_loss, (1, 128))

        @pl.when(pl.program_id(0) == 0)
        def _():
            dg_ref[...] = part_g
            loss_ref[...] = part_l

        @pl.when(pl.program_id(0) > 0)
        def _():
            dg_ref[...] += part_g
            loss_ref[...] += part_l

    row = pl.BlockSpec((tm, Dm), lambda i: (i, 0))
    vec = pl.BlockSpec((1, Dm), lambda i: (0, 0))
    return _pcall(
        body, name=name, grid=(T // tm,), in_specs=[row, vec, row],
        out_specs=[row, vec, pl.BlockSpec((1, 128), lambda i: (0, 0))],
        out_shape=[jax.ShapeDtypeStruct((T, Dm), F32), jax.ShapeDtypeStruct((1, Dm), F32),
                   jax.ShapeDtypeStruct((1, 128), F32)],
        compiler_params=pltpu.CompilerParams(dimension_semantics=("arbitrary",)),
    )(x, g, tgt)


def _sigmoid(z):
    return 1.0 / (1.0 + jnp.exp(-z))


def _swiglu_fwd(gu, name):
    T = gu.shape[0]
    tm = _pick(T, 128, 8)

    def body(gu_ref, a_ref):
        gv = gu_ref[:, :D_FF]
        uv = gu_ref[:, D_FF:]
        a_ref[...] = (gv * _sigmoid(gv) * uv).astype(BF16)

    return _pcall(
        body, name=name, grid=(T // tm,), in_specs=[pl.BlockSpec((tm, 2 * D_FF), lambda i: (i, 0))],
        out_specs=pl.BlockSpec((tm, D_FF), lambda i: (i, 0)),
        out_shape=jax.ShapeDtypeStruct((T, D_FF), BF16),
    )(gu)


def _swiglu_bwd(gu, da, name):
    T = gu.shape[0]
    tm = _pick(T, 128, 8)

    def body(gu_ref, da_ref, d_ref):
        gv = gu_ref[:, :D_FF]
        uv = gu_ref[:, D_FF:]
        dav = da_ref[...]
        sg = _sigmoid(gv)
        d_ref[:, :D_FF] = (dav * uv * (sg + gv * sg * (1.0 - sg))).astype(BF16)
        d_ref[:, D_FF:] = (dav * gv * sg).astype(BF16)

    return _pcall(
        body, name=name, grid=(T // tm,),
        in_specs=[pl.BlockSpec((tm, 2 * D_FF), lambda i: (i, 0)), pl.BlockSpec((tm, D_FF), lambda i: (i, 0))],
        out_specs=pl.BlockSpec((tm, 2 * D_FF), lambda i: (i, 0)),
        out_shape=jax.ShapeDtypeStruct((T, 2 * D_FF), BF16),
    )(gu, da)


GATE_BLOCK0 = QKV_WIDTH // D_MODEL


def _gate_fwd(p, ya, yb, yc, name):
    T = p.shape[0]
    tm = _pick(T, 256, 8)

    def body(ga_ref, gb_ref, gc_ref, ya_ref, yb_ref, yc_ref, o_ref):
        o_ref[...] = (_sigmoid(ga_ref[...]) * ya_ref[...] + _sigmoid(gb_ref[...]) * yb_ref[...]
                      + _sigmoid(gc_ref[...]) * yc_ref[...]).astype(BF16)

    gate = [pl.BlockSpec((tm, D_MODEL), functools.partial(lambda i, kk: (i, GATE_BLOCK0 + kk), kk=kk))
            for kk in range(3)]
    row = pl.BlockSpec((tm, D_MODEL), lambda i: (i, 0))
    return _pcall(
        body, name=name, grid=(T // tm,), in_specs=gate + [row, row, row], out_specs=row,
        out_shape=jax.ShapeDtypeStruct((T, D_MODEL), BF16),
    )(p, p, p, ya, yb, yc)


def _gate_bwd(p, ya, yb, yc, dm, name):
    T = p.shape[0]
    tm = _pick(T, 256, 8)

    def body(ga_ref, gb_ref, gc_ref, ya_ref, yb_ref, yc_ref, dm_ref, da_ref, db_ref, dc_ref, dg_ref):
        dmv = dm_ref[...]
        for kk, (g_ref, y_ref, d_ref) in enumerate(((ga_ref, ya_ref, da_ref), (gb_ref, yb_ref, db_ref),
                                                    (gc_ref, yc_ref, dc_ref))):
            s = _sigmoid(g_ref[...])
            d_ref[...] = (s * dmv).astype(BF16)
            dg_ref[:, kk * D_MODEL:(kk + 1) * D_MODEL] = dmv * y_ref[...] * s * (1.0 - s)

    gate = [pl.BlockSpec((tm, D_MODEL), functools.partial(lambda i, kk: (i, GATE_BLOCK0 + kk), kk=kk))
            for kk in range(3)]
    row = pl.BlockSpec((tm, D_MODEL), lambda i: (i, 0))
    return _pcall(
        body, name=name, grid=(T // tm,), in_specs=gate + [row, row, row, row],
        out_specs=[row, row, row, pl.BlockSpec((tm, GATE_WIDTH), lambda i: (i, 0))],
        out_shape=[jax.ShapeDtypeStruct((T, D_MODEL), BF16)] * 3 + [jax.ShapeDtypeStruct((T, GATE_WIDTH), F32)],
    )(p, p, p, ya, yb, yc, dm)


def _colsum(a, name):
    T, N = a.shape
    tm = _pick(T, 256, 8)

    def body(a_ref, o_ref):
        part = jnp.sum(a_ref[...], axis=0, keepdims=True)

        @pl.when(pl.program_id(0) == 0)
        def _():
            o_ref[...] = part

        @pl.when(pl.program_id(0) > 0)
        def _():
            o_ref[...] += part

    return _pcall(
        body, name=name, grid=(T // tm,), in_specs=[pl.BlockSpec((tm, N), lambda i: (i, 0))],
        out_specs=pl.BlockSpec((1, N), lambda i: (0, 0)), out_shape=jax.ShapeDtypeStruct((1, N), F32),
        compiler_params=pltpu.CompilerParams(dimension_semantics=("arbitrary",)),
    )(a)


def _adamw(w, g, m, v, name):
    R, C = w.shape
    tr = 256 if R % 256 == 0 else R
    c1 = 1.0 / (1.0 - ADAM_B1 ** ADAM_STEP)
    c2 = 1.0 / (1.0 - ADAM_B2 ** ADAM_STEP)

    def body(w_ref, g_ref, m_ref, v_ref, d_ref, nm_ref, nv_ref):
        gv = g_ref[...]
        mn = ADAM_B1 * m_ref[...] + (1.0 - ADAM_B1) * gv
        vn = ADAM_B2 * v_ref[...] + (1.0 - ADAM_B2) * (gv * gv)
        nm_ref[...] = mn
        nv_ref[...] = vn
        d_ref[...] = -ADAM_LR * ((mn * c1) / (jnp.sqrt(vn * c2) + ADAM_EPS) + ADAM_WD * w_ref[...])

    spec = pl.BlockSpec((tr, C), lambda i: (i, 0))
    return _pcall(
        body, name=name, grid=(R // tr,), in_specs=[spec] * 4, out_specs=[spec] * 3,
        out_shape=[jax.ShapeDtypeStruct((R, C), F32)] * 3,
    )(w, g, m, v)


def _log_sigmoid(z):
    return jnp.minimum(z, 0.0) - jnp.log(1.0 + jnp.exp(-jnp.abs(z)))


def _dot3(x, m01):
    x1 = x.astype(BF16)
    r1 = x - x1.astype(F32)
    x2 = r1.astype(BF16)
    x3 = (r1 - x2.astype(F32)).astype(BF16)
    return (jnp.dot(x1, m01, preferred_element_type=F32) + jnp.dot(x2, m01, preferred_element_type=F32)
            + jnp.dot(x3, m01, preferred_element_type=F32))


def _dot_nt(a, b):
    return lax.dot_general(a, b, NT, preferred_element_type=F32)


def _dot_tn(a, b):
    return lax.dot_general(a, b, TN, preferred_element_type=F32)


def _iota2(shape):
    return lax.broadcasted_iota(jnp.int32, shape, 0), lax.broadcasted_iota(jnp.int32, shape, 1)


def _head_spec(T):
    return pl.BlockSpec((None, T, HEAD_DIM), lambda h: (h, 0, 0))


def _sb_weights(qb, kb, t0, s0, racc, m_gt, row, col):
    z = _dot_nt(qb, kb) * SCALE
    strict = (s0 + col) < (t0 + row)
    lb = _log_sigmoid(z)
    lf = jnp.where(strict, lb - z, 0.0)
    between = _dot3(lf, m_gt) + racc
    w = jnp.where(strict, jnp.exp(lb + between), 0.0)
    return strict, lb, lf, w


def _sb_fwd(q, k, v, name):
    H, T, _ = q.shape
    nb = T // QB

    def body(q_ref, k_ref, v_ref, o_ref):
        row, col = _iota2((QB, QB))
        m_gt = (row > col).astype(BF16)

        def qblock(i, _):
            t0 = pl.multiple_of(i * QB, QB)
            qb = q_ref[pl.ds(t0, QB), :].astype(BF16)

            def kblock(jj, carry):
                acc, racc = carry
                s0 = pl.multiple_of((i - jj) * QB, QB)
                kb = k_ref[pl.ds(s0, QB), :].astype(BF16)
                vb = v_ref[pl.ds(s0, QB), :].astype(BF16)
                _, _, lf, w = _sb_weights(qb, kb, t0, s0, racc, m_gt, row, col)
                acc = acc + jnp.dot(w.astype(BF16), vb, preferred_element_type=F32)
                return acc, racc + jnp.sum(lf, axis=1, keepdims=True)

            acc, _ = lax.fori_loop(0, i + 1, kblock,
                                   (jnp.zeros((QB, HEAD_DIM), F32), jnp.zeros((QB, 1), F32)))
            o_ref[pl.ds(t0, QB), :] = acc
            return 0

        lax.fori_loop(0, nb, qblock, 0)

    spec = _head_spec(T)
    return _pcall(body, name=name, grid=(H,), in_specs=[spec] * 3, out_specs=spec,
                  out_shape=jax.ShapeDtypeStruct((H, T, HEAD_DIM), F32))(q, k, v)


def _sb_bwd(q, k, v, do, name):
    H, T, _ = q.shape
    nb = T // QB

    def body(q_ref, k_ref, v_ref, do_ref, dq_ref, dk_ref, dv_ref, racc_ref):
        row, col = _iota2((QB, QB))
        m_gt = (row > col).astype(BF16)
        m_lt = (row < col).astype(BF16)
        dk_ref[...] = jnp.zeros_like(dk_ref)
        dv_ref[...] = jnp.zeros_like(dv_ref)

        def qblock(i, _):
            t0 = pl.multiple_of(i * QB, QB)
            qb = q_ref[pl.ds(t0, QB), :].astype(BF16)
            dob = do_ref[pl.ds(t0, QB), :].astype(BF16)

            def suffix(jj, racc):
                j = i - jj
                s0 = pl.multiple_of(j * QB, QB)
                kb = k_ref[pl.ds(s0, QB), :].astype(BF16)
                z = _dot_nt(qb, kb) * SCALE
                lf = jnp.where((s0 + col) < (t0 + row), _log_sigmoid(z) - z, 0.0)
                racc_ref[j] = racc
                return racc + jnp.sum(lf, axis=1, keepdims=True)

            lax.fori_loop(0, i + 1, suffix, jnp.zeros((QB, 1), F32))

            def kblock(j, carry):
                dq, cacc = carry
                s0 = pl.multiple_of(j * QB, QB)
                kb = k_ref[pl.ds(s0, QB), :].astype(BF16)
                vb = v_ref[pl.ds(s0, QB), :].astype(BF16)
                strict, lb, _, w = _sb_weights(qb, kb, t0, s0, racc_ref[j], m_gt, row, col)
                e = _dot_nt(dob, vb) * w
                c_left = _dot3(e, m_lt) + cacc
                sig = jnp.exp(lb)
                dz = jnp.where(strict, e * (1.0 - sig) - c_left * sig, 0.0).astype(BF16)
                dq = dq + jnp.dot(dz, kb, preferred_element_type=F32)
                dk_ref[pl.ds(s0, QB), :] += _dot_tn(dz, qb) * SCALE
                dv_ref[pl.ds(s0, QB), :] += _dot_tn(w.astype(BF16), dob)
                return dq, cacc + jnp.sum(e, axis=1, keepdims=True)

            dq, _ = lax.fori_loop(0, i + 1, kblock,
                                  (jnp.zeros((QB, HEAD_DIM), F32), jnp.zeros((QB, 1), F32)))
            dq_ref[pl.ds(t0, QB), :] = dq * SCALE
            return 0

        lax.fori_loop(0, nb, qblock, 0)

    spec = _head_spec(T)
    return _pcall(body, name=name, grid=(H,), in_specs=[spec] * 4, out_specs=[spec] * 3,
                  out_shape=[jax.ShapeDtypeStruct((H, T, HEAD_DIM), F32)] * 3,
                  scratch_shapes=[pltpu.VMEM((nb, QB, 1), F32)])(q, k, v, do)


def _fox_logits(qb, kb, fq, fk, t0, s0, row, col):
    z = _dot_nt(qb, kb) * SCALE + fq - fk
    return jnp.where((s0 + col) <= (t0 + row), z, NEG)


def _fox_specs(T):
    nb = T // QB
    return (pl.BlockSpec((None, T, 1), lambda h: (h, 0, 0)), pl.BlockSpec((None, nb, QB), lambda h: (h, 0, 0)))


def _fox_fwd(q, k, v, fq, fk, name):
    H, T, _ = q.shape
    nb = T // QB

    def body(q_ref, k_ref, v_ref, fq_ref, fk_ref, o_ref, lse_ref):
        row, col = _iota2((QB, QB))

        def qblock(i, _):
            t0 = pl.multiple_of(i * QB, QB)
            qb = q_ref[pl.ds(t0, QB), :].astype(BF16)
            fq = fq_ref[pl.ds(t0, QB), :]

            def kblock(j, carry):
                acc, m, l = carry
                s0 = pl.multiple_of(j * QB, QB)
                kb = k_ref[pl.ds(s0, QB), :].astype(BF16)
                vb = v_ref[pl.ds(s0, QB), :].astype(BF16)
                z = _fox_logits(qb, kb, fq, fk_ref[pl.ds(j, 1), :], t0, s0, row, col)
                m_new = jnp.maximum(m, jnp.max(z, axis=1, keepdims=True))
                a = jnp.exp(m - m_new)
                p = jnp.exp(z - m_new)
                acc = a * acc + jnp.dot(p.astype(BF16), vb, preferred_element_type=F32)
                return acc, m_new, a * l + jnp.sum(p, axis=1, keepdims=True)

            acc, m, l = lax.fori_loop(0, i + 1, kblock,
                                      (jnp.zeros((QB, HEAD_DIM), F32), jnp.full((QB, 1), NEG, F32),
                                       jnp.zeros((QB, 1), F32)))
            o_ref[pl.ds(t0, QB), :] = acc / l
            lse_ref[pl.ds(t0, QB), :] = m + jnp.log(l)
            return 0

        lax.fori_loop(0, nb, qblock, 0)

    spec = _head_spec(T)
    fq_spec, fk_spec = _fox_specs(T)
    return _pcall(body, name=name, grid=(H,), in_specs=[spec] * 3 + [fq_spec, fk_spec],
                  out_specs=[spec, fq_spec],
                  out_shape=[jax.ShapeDtypeStruct((H, T, HEAD_DIM), F32), jax.ShapeDtypeStruct((H, T, 1), F32)],
                  )(q, k, v, fq, fk)


def _fox_bwd(q, k, v, fq, fk, lse, do, name):
    H, T, _ = q.shape
    nb = T // QB

    def body(q_ref, k_ref, v_ref, fq_ref, fk_ref, lse_ref, do_ref, dq_ref, dk_ref, dv_ref, dfk_ref):
        row, col = _iota2((QB, QB))
        dk_ref[...] = jnp.zeros_like(dk_ref)
        dv_ref[...] = jnp.zeros_like(dv_ref)
        dfk_ref[...] = jnp.zeros_like(dfk_ref)

        def qblock(i, _):
            t0 = pl.multiple_of(i * QB, QB)
            qb = q_ref[pl.ds(t0, QB), :].astype(BF16)
            fq = fq_ref[pl.ds(t0, QB), :]
            lse = lse_ref[pl.ds(t0, QB), :]
            dob = do_ref[pl.ds(t0, QB), :].astype(BF16)

            def probs(j):
                s0 = pl.multiple_of(j * QB, QB)
                kb = k_ref[pl.ds(s0, QB), :].astype(BF16)
                vb = v_ref[pl.ds(s0, QB), :].astype(BF16)
                z = _fox_logits(qb, kb, fq, fk_ref[pl.ds(j, 1), :], t0, s0, row, col)
                return s0, kb, jnp.exp(z - lse), _dot_nt(dob, vb)

            def row_dot(j, delta):
                _, _, p, dp = probs(j)
                return delta + jnp.sum(p * dp, axis=1, keepdims=True)

            delta = lax.fori_loop(0, i + 1, row_dot, jnp.zeros((QB, 1), F32))

            def kblock(j, dq):
                s0, kb, p, dp = probs(j)
                dz = p * (dp - delta)
                dzb = dz.astype(BF16)
                dk_ref[pl.ds(s0, QB), :] += _dot_tn(dzb, qb) * SCALE
                dv_ref[pl.ds(s0, QB), :] += _dot_tn(p.astype(BF16), dob)
                dfk_ref[pl.ds(j, 1), :] += -jnp.sum(dz, axis=0, keepdims=True)
                return dq + jnp.dot(dzb, kb, preferred_element_type=F32)

            dq = lax.fori_loop(0, i + 1, kblock, jnp.zeros((QB, HEAD_DIM), F32))
            dq_ref[pl.ds(t0, QB), :] = dq * SCALE
            return 0

        lax.fori_loop(0, nb, qblock, 0)

    spec = _head_spec(T)
    fq_spec, fk_spec = _fox_specs(T)
    return _pcall(body, name=name, grid=(H,),
                  in_specs=[spec] * 3 + [fq_spec, fk_spec, fq_spec, spec],
                  out_specs=[spec, spec, spec, fk_spec],
                  out_shape=[jax.ShapeDtypeStruct((H, T, HEAD_DIM), F32)] * 3
                  + [jax.ShapeDtypeStruct((H, nb, QB), F32)],
                  )(q, k, v, fq, fk, lse, do)


def _forget_cumsum(flog, name):
    R, T = flog.shape
    nb = T // QB

    def body(x_ref, o_ref):
        row, col = _iota2((QB, QB))
        m_le = (row <= col).astype(BF16)
        carry = jnp.zeros((R, 1), F32)
        for b in range(nb):
            lf = _log_sigmoid(x_ref[:, b * QB:(b + 1) * QB])
            o_ref[:, b * QB:(b + 1) * QB] = _dot3(lf, m_le) + carry
            carry = carry + jnp.sum(lf, axis=1, keepdims=True)

    spec = pl.BlockSpec((R, T), lambda: (0, 0))
    return _pcall(body, name=name, in_specs=[spec], out_specs=spec,
                  out_shape=jax.ShapeDtypeStruct((R, T), F32))(flog)


def _forget_cumsum_bwd(flog, df, name):
    R, T = flog.shape
    nb = T // QB

    def body(x_ref, df_ref, o_ref):
        row, col = _iota2((QB, QB))
        m_ge = (row >= col).astype(BF16)
        carry = jnp.zeros((R, 1), F32)
        for b in reversed(range(nb)):
            dfb = df_ref[:, b * QB:(b + 1) * QB]
            dlog = _dot3(dfb, m_ge) + carry
            o_ref[:, b * QB:(b + 1) * QB] = dlog * _sigmoid(-x_ref[:, b * QB:(b + 1) * QB])
            carry = carry + jnp.sum(dfb, axis=1, keepdims=True)

    spec = pl.BlockSpec((R, T), lambda: (0, 0))
    return _pcall(body, name=name, in_specs=[spec, spec], out_specs=spec,
                  out_shape=jax.ShapeDtypeStruct((R, T), F32))(flog, df)


def _chunk_probs(qc, kb, bias, c, col):
    z = _dot_nt(qc, kb) * SCALE + bias
    z = jnp.where((c - (LEFT_CHUNKS + 1)) * CHUNK + col >= jnp.maximum(0, (c - LEFT_CHUNKS) * CHUNK), z, NEG)
    p = jnp.exp(z - jnp.max(z, axis=1, keepdims=True))
    return p, jnp.sum(p, axis=1, keepdims=True)


def _chunk_specs(T):
    return (_head_spec(T), pl.BlockSpec((None, T + BAND_PAD, HEAD_DIM), lambda h: (h, 0, 0)),
            pl.BlockSpec((None, CHUNK, BAND), lambda h: (h, 0, 0)))


def _chunk_fwd(q, kp, vp, bias, name):
    H, T, _ = q.shape
    nc = T // CHUNK

    def body(q_ref, kp_ref, vp_ref, bias_ref, o_ref):
        _, col = _iota2((CHUNK, BAND))

        def chunk(c, _):
            t0 = pl.multiple_of(c * CHUNK, CHUNK)
            qc = q_ref[pl.ds(t0, CHUNK), :].astype(BF16)
            kb = kp_ref[pl.ds(t0, BAND), :].astype(BF16)
            vb = vp_ref[pl.ds(t0, BAND), :].astype(BF16)
            p, l = _chunk_probs(qc, kb, bias_ref[...], c, col)
            o_ref[pl.ds(t0, CHUNK), :] = jnp.dot(p.astype(BF16), vb, preferred_element_type=F32) / l
            return 0

        lax.fori_loop(0, nc, chunk, 0)

    q_spec, kp_spec, b_spec = _chunk_specs(T)
    return _pcall(body, name=name, grid=(H,), in_specs=[q_spec, kp_spec, kp_spec, b_spec], out_specs=q_spec,
                  out_shape=jax.ShapeDtypeStruct((H, T, HEAD_DIM), F32))(q, kp, vp, bias)


def _chunk_bwd(q, kp, vp, bias, do, name):
    H, T, _ = q.shape
    nc = T // CHUNK

    def body(q_ref, kp_ref, vp_ref, bias_ref, do_ref, dq_ref, dkp_ref, dvp_ref, db_ref):
        _, col = _iota2((CHUNK, BAND))
        dkp_ref[...] = jnp.zeros_like(dkp_ref)
        dvp_ref[...] = jnp.zeros_like(dvp_ref)
        db_ref[...] = jnp.zeros_like(db_ref)

        def chunk(c, _):
            t0 = pl.multiple_of(c * CHUNK, CHUNK)
            qc = q_ref[pl.ds(t0, CHUNK), :].astype(BF16)
            kb = kp_ref[pl.ds(t0, BAND), :].astype(BF16)
            vb = vp_ref[pl.ds(t0, BAND), :].astype(BF16)
            dob = do_ref[pl.ds(t0, CHUNK), :].astype(BF16)
            p, l = _chunk_probs(qc, kb, bias_ref[...], c, col)
            p = p / l
            dp = _dot_nt(dob, vb)
            dz = p * (dp - jnp.sum(p * dp, axis=1, keepdims=True))
            dzb = dz.astype(BF16)
            dq_ref[pl.ds(t0, CHUNK), :] = jnp.dot(dzb, kb, preferred_element_type=F32) * SCALE
            dkp_ref[pl.ds(t0, BAND), :] += _dot_tn(dzb, qc) * SCALE
            dvp_ref[pl.ds(t0, BAND), :] += _dot_tn(p.astype(BF16), dob)
            db_ref[...] += dz
            return 0

        lax.fori_loop(0, nc, chunk, 0)

    q_spec, kp_spec, b_spec = _chunk_specs(T)
    return _pcall(body, name=name, grid=(H,), in_specs=[q_spec, kp_spec, kp_spec, b_spec, q_spec],
                  out_specs=[q_spec, kp_spec, kp_spec, b_spec],
                  out_shape=[jax.ShapeDtypeStruct((H, T, HEAD_DIM), F32),
                             jax.ShapeDtypeStruct((H, T + BAND_PAD, HEAD_DIM), F32),
                             jax.ShapeDtypeStruct((H, T + BAND_PAD, HEAD_DIM), F32),
                             jax.ShapeDtypeStruct((H, CHUNK, BAND), F32)])(q, kp, vp, bias, do)


HBM_SPEC = pl.BlockSpec(memory_space=pltpu.HBM)


def _position():
    return lax.axis_index("x"), lax.axis_index("y"), lax.axis_index("c")


def _other_chips(x, y):
    chips = [(1 - x, y), (x, 1 - y), (1 - x, 1 - y)]
    return [(chip, 2 * chip[0] + chip[1]) for chip in chips]


def _gather_weights(pack, name):
    R, C = pack.shape
    half = R // 2

    def body(pack_ref, out_ref, send_sems, recv_sems, local_sem):
        x, y, c = _position()
        me = 2 * x + y
        sibling = (x, y, 1 - c)
        others = _other_chips(x, y)

        def rows(chip, h):
            return out_ref.at[chip, pl.ds(pl.multiple_of(h * half, 16), half), :]

        def copy(k, src, dst, to):
            return pltpu.make_async_remote_copy(src_ref=src, dst_ref=dst, send_sem=send_sems.at[k],
                                                recv_sem=recv_sems.at[k], device_id=to, device_id_type=MESH)

        mine = pltpu.make_async_copy(pack_ref, out_ref.at[me], local_sem)
        mine.start()
        my_half = pack_ref.at[pl.ds(pl.multiple_of(c * half, 16), half), :]
        first = [copy(k, my_half, rows(me, c), (*chip, c)) for k, (chip, _) in enumerate(others)]
        for cp in first:
            cp.start()
        passed = [copy(3 + k, rows(idx, c), rows(idx, c), sibling) for k, (_, idx) in enumerate(others)]
        for k, (_, idx) in enumerate(others):
            copy(k, my_half, rows(idx, c), sibling).wait_recv()
            passed[k].start()
        for k, (_, idx) in enumerate(others):
            copy(3 + k, my_half, rows(idx, 1 - c), sibling).wait_recv()
        for cp in first + passed:
            cp.wait_send()
        mine.wait()

    return _pcall(
        body, name=name, in_specs=[HBM_SPEC], out_specs=HBM_SPEC,
        out_shape=jax.ShapeDtypeStruct((N_CHIPS, R, C), pack.dtype),
        scratch_shapes=[pltpu.SemaphoreType.DMA((6,)), pltpu.SemaphoreType.DMA((6,)), pltpu.SemaphoreType.DMA],
    )(pack)


def _swap_sibling(send, name):
    def body(send_ref, recv_ref, send_sem, recv_sem):
        x, y, c = _position()
        cp = pltpu.make_async_remote_copy(src_ref=send_ref, dst_ref=recv_ref, send_sem=send_sem,
                                          recv_sem=recv_sem, device_id=(x, y, 1 - c), device_id_type=MESH)
        cp.start()
        cp.wait()

    return _pcall(
        body, name=name, in_specs=[HBM_SPEC], out_specs=HBM_SPEC,
        out_shape=jax.ShapeDtypeStruct(send.shape, send.dtype),
        scratch_shapes=[pltpu.SemaphoreType.DMA, pltpu.SemaphoreType.DMA],
    )(send)


def _scatter_chips(parts, name):
    def body(parts_ref, out_ref, send_sems, recv_sems, local_sem):
        x, y, c = _position()
        me = 2 * x + y
        mine = pltpu.make_async_copy(parts_ref.at[me], out_ref.at[me], local_sem)
        mine.start()
        sends = []
        for k, (chip, idx) in enumerate(_other_chips(x, y)):
            cp = pltpu.make_async_remote_copy(src_ref=parts_ref.at[idx], dst_ref=out_ref.at[me],
                                              send_sem=send_sems.at[k], recv_sem=recv_sems.at[k],
                                              device_id=(*chip, c), device_id_type=MESH)
            cp.start()
            sends.append(cp)
        for k, (chip, idx) in enumerate(_other_chips(x, y)):
            pltpu.make_async_remote_copy(src_ref=parts_ref.at[idx], dst_ref=out_ref.at[idx],
                                         send_sem=send_sems.at[k], recv_sem=recv_sems.at[k],
                                         device_id=(*chip, c), device_id_type=MESH).wait_recv()
        for cp in sends:
            cp.wait_send()
        mine.wait()

    return _pcall(
        body, name=name, in_specs=[HBM_SPEC], out_specs=HBM_SPEC,
        out_shape=jax.ShapeDtypeStruct(parts.shape, parts.dtype),
        scratch_shapes=[pltpu.SemaphoreType.DMA((3,)), pltpu.SemaphoreType.DMA((3,)), pltpu.SemaphoreType.DMA],
    )(parts)


def _allreduce_small(v, name):
    R, C = v.shape

    def body(v_ref, o_ref, slots, send_sems, recv_sems):
        x, y, c = _position()
        me = 4 * x + 2 * y + c
        slots[0] = v_ref[...]
        sends = []
        for r in range(1, 8):
            fx, fy, fc = (r >> 2) & 1, (r >> 1) & 1, r & 1
            to = (x ^ fx, y ^ fy, c ^ fc)
            cp = pltpu.make_async_remote_copy(src_ref=v_ref, dst_ref=slots.at[r], send_sem=send_sems.at[r - 1],
                                              recv_sem=recv_sems.at[r - 1], device_id=to, device_id_type=MESH)
            cp.start()
            sends.append(cp)
        for cp in sends:
            cp.wait()
        acc = slots[me]
        for d in range(1, 8):
            acc = acc + slots[d ^ me]
        o_ref[...] = acc

    vmem = pl.BlockSpec(memory_space=pltpu.VMEM)
    return _pcall(
        body, name=name, in_specs=[vmem], out_specs=vmem, out_shape=jax.ShapeDtypeStruct((R, C), F32),
        scratch_shapes=[pltpu.VMEM((8, R, C), F32), pltpu.SemaphoreType.DMA((7,)), pltpu.SemaphoreType.DMA((7,))],
    )(v)


def _add_pair(a, b, name):
    N, R, C = a.shape
    tr = _pick(R, 512, 16)

    def body(a_ref, b_ref, o_ref):
        o_ref[...] = (a_ref[...].astype(F32) + b_ref[...].astype(F32)).astype(BF16)

    spec = pl.BlockSpec((None, tr, C), lambda n, i: (n, i, 0))
    return _pcall(body, name=name, grid=(N, R // tr), in_specs=[spec, spec], out_specs=spec,
                  out_shape=jax.ShapeDtypeStruct((N, R, C), BF16))(a, b)


def _sum_chips(parts, name):
    N, R, C = parts.shape
    tr = _pick(R, 512, 16)

    def body(p_ref, o_ref):
        acc = p_ref[0].astype(F32)
        for n in range(1, N):
            acc = acc + p_ref[n].astype(F32)
        o_ref[...] = acc

    return _pcall(body, name=name, grid=(R // tr,), in_specs=[pl.BlockSpec((N, tr, C), lambda i: (0, i, 0))],
                  out_specs=pl.BlockSpec((tr, C), lambda i: (i, 0)),
                  out_shape=jax.ShapeDtypeStruct((R, C), F32))(parts)


BIG = ("w_ffn1_in", "w_ffn1_out", "w_in", "w_br_sb", "w_br_ch", "w_br_fox", "w_out", "w_ffn2_in", "w_ffn2_out")
ROW_SHARDED = ("w_ffn1_out", "w_out", "w_ffn2_out")
SMALL = ("g_ffn1", "g_mix", "b_in", "rel_bias", "g_ffn2", "g_final")
SHARD_SHAPES = {
    "w_ffn1_in": (D_MODEL, 2 * D_FF // N_CHIPS), "w_ffn1_out": (D_FF // N_CHIPS, D_MODEL),
    "w_in": (D_MODEL, IN_WIDTH // N_CHIPS), "w_br_sb": (W_SB, D_MODEL // N_CHIPS),
    "w_br_ch": (W_CH, D_MODEL // N_CHIPS), "w_br_fox": (W_FOX, D_MODEL // N_CHIPS),
    "w_out": (D_MODEL // N_CHIPS, D_MODEL), "w_ffn2_in": (D_MODEL, 2 * D_FF // N_CHIPS),
    "w_ffn2_out": (D_FF // N_CHIPS, D_MODEL),
}
PACK_ROWS = sum(DEPTH * s[0] * s[1] // PACK_COLS for s in SHARD_SHAPES.values())
PACK_ROWS_PAD = -(-PACK_ROWS // 32) * 32


def _pack_shards(shards, dtype):
    flat = [shards[n].astype(dtype).reshape(-1, PACK_COLS) for n in BIG]
    flat.append(jnp.zeros((PACK_ROWS_PAD - PACK_ROWS, PACK_COLS), dtype))
    return jnp.concatenate(flat, axis=0)


def _unpack_shards(pack):
    out, off = {}, 0
    for n in BIG:
        r, c = SHARD_SHAPES[n]
        rows = DEPTH * r * c // PACK_COLS
        out[n] = pack[..., off:off + rows, :].reshape(pack.shape[:-2] + (DEPTH, r, c))
        off += rows
    return out


def _full_weight(gathered, name, layer):
    return jnp.concatenate([gathered[name][j, layer] for j in range(N_CHIPS)],
                           axis=0 if name in ROW_SHARDED else 1)


def _split_grad(name, g):
    r, c = SHARD_SHAPES[name]
    if name in ROW_SHARDED:
        return g.reshape(N_CHIPS, r, c)
    return jnp.transpose(g.reshape(g.shape[0], N_CHIPS, c), (1, 0, 2))


SMALL_SIZES = {"g_ffn1": DEPTH * D_MODEL, "g_mix": DEPTH * D_MODEL, "b_in": DEPTH * IN_WIDTH,
               "rel_bias": DEPTH * N_REL * H_CH, "g_ffn2": DEPTH * D_MODEL, "g_final": D_MODEL}
SMALL_TOTAL = sum(SMALL_SIZES.values()) + 1
SMALL_ROWS = -(-SMALL_TOTAL // (8 * 128)) * 8


def _pack_small(vals, extra):
    flat = [vals[n].reshape(-1) for n in SMALL] + [extra.reshape(-1)]
    flat.append(jnp.zeros((SMALL_ROWS * 128 - SMALL_TOTAL,), F32))
    return jnp.concatenate(flat).reshape(SMALL_ROWS, 128)


def _unpack_small(pack, shapes):
    flat, out, off = pack.reshape(-1), {}, 0
    for n in SMALL:
        out[n] = flat[off:off + SMALL_SIZES[n]].reshape(shapes[n])
        off += SMALL_SIZES[n]
    return out, flat[off]


def _to_heads(t, n_heads):
    return jnp.transpose(t.reshape(t.shape[0], n_heads, HEAD_DIM), (1, 0, 2))


def _from_heads(t):
    return jnp.transpose(t, (1, 0, 2)).reshape(t.shape[1], t.shape[0] * HEAD_DIM)


def _pad_keys(t):
    return jnp.pad(t, ((0, 0), (BAND_PAD, 0), (0, 0)))


def _rel_index():
    rel = (np.arange(CHUNK)[:, None] + (LEFT_CHUNKS + 1) * CHUNK) - np.arange(BAND)[None, :]
    return np.clip(rel, -MAX_REL, MAX_REL) + MAX_REL


def _pad_w_in(w):
    qkv, f, gates = w[:, :QKV_WIDTH], w[:, QKV_WIDTH:QKV_WIDTH + H_FOX], w[:, QKV_WIDTH + H_FOX:]
    return jnp.concatenate([qkv, gates, f, jnp.zeros((w.shape[0], F_PAD - H_FOX), w.dtype)], axis=1)


def _unpad_w_in(w):
    return jnp.concatenate([w[:, :QKV_WIDTH], w[:, QKV_WIDTH + GATE_WIDTH:QKV_WIDTH + GATE_WIDTH + H_FOX],
                            w[:, QKV_WIDTH:QKV_WIDTH + GATE_WIDTH]], axis=1)


QKV_SPLITS = np.cumsum([0, W_SB, W_SB, W_SB, W_CH, W_CH, W_CH, W_FOX, W_FOX, W_FOX])


def _ffn_fwd(x, g, w_in, w_out, tag):
    h = _rmsnorm_fwd(x, g, f"{tag}_norm")
    gu = _mm(h, w_in, mode="nn", name=f"{tag}_in")
    a = _swiglu_fwd(gu, f"{tag}_act")
    y = _mm(a, w_out, mode="nn", name=f"{tag}_out", alpha=0.5, res=x)
    return y, (h, gu, a)


def _ffn_bwd(x, g, w_in, w_out, saved, dy, tag):
    h, gu, a = saved
    da = _mm(dy, w_out, mode="nt", name=f"{tag}_da", alpha=0.5, tn_cap=1408)
    dw_out = _mm(a, dy, mode="tn", name=f"{tag}_dwout", alpha=0.5, tm_cap=512)
    dgu = _swiglu_bwd(gu, da, f"{tag}_dact")
    dw_in = _mm(h, dgu, mode="tn", name=f"{tag}_dwin", tm_cap=512)
    dh = _mm(dgu, w_in, mode="nt", name=f"{tag}_dh", tk_cap=1408)
    dx, dg = _rmsnorm_bwd(x, g, dh, dy, f"{tag}_dnorm")
    return dx, dg, dw_in, dw_out


def _mixer_fwd(x, lw, tag):
    T = x.shape[0]
    h = _rmsnorm_fwd(x, lw["g_mix"], f"{tag}_norm")
    p = _mm(h, lw["w_in"], mode="nn", name=f"{tag}_proj", bias=lw["b_in"], tn_cap=896)
    parts = [p[:, QKV_SPLITS[i]:QKV_SPLITS[i + 1]] for i in range(9)]
    qa, ka, va = (_to_heads(t, H_SB) for t in parts[0:3])
    qb, kb, vb = (_to_heads(t, H_CH) for t in parts[3:6])
    qc, kc, vc = (_to_heads(t, H_FOX) for t in parts[6:9])
    flog = jnp.pad(p[:, QKV_WIDTH + GATE_WIDTH:QKV_WIDTH + GATE_WIDTH + H_FOX].T, ((0, 8 - H_FOX), (0, 0)))
    fcum = _forget_cumsum(flog, f"{tag}_fcum")[:H_FOX]
    fq, fk = fcum.reshape(H_FOX, T, 1), fcum.reshape(H_FOX, T // QB, QB)
    kbp, vbp = _pad_keys(kb), _pad_keys(vb)
    oa = _sb_fwd(qa, ka, va, f"{tag}_sb")
    ob = _chunk_fwd(qb, kbp, vbp, lw["bias"], f"{tag}_ch")
    oc, lse = _fox_fwd(qc, kc, vc, fq, fk, f"{tag}_fox")
    oam, obm, ocm = _from_heads(oa), _from_heads(ob), _from_heads(oc)
    ya = _mm(oam, lw["w_br_sb"], mode="nn", name=f"{tag}_bra", tn_cap=1024)
    yb = _mm(obm, lw["w_br_ch"], mode="nn", name=f"{tag}_brb", tn_cap=1024)
    yc = _mm(ocm, lw["w_br_fox"], mode="nn", name=f"{tag}_brc", tn_cap=1024)
    merged = _gate_fwd(p, ya, yb, yc, f"{tag}_gate")
    y = _mm(merged, lw["w_out"], mode="nn", name=f"{tag}_out", res=x)
    saved = (h, p, (qa, ka, va), (qb, kbp, vbp), (qc, kc, vc), flog, fq, fk, lse, (oam, obm, ocm),
             (ya, yb, yc), merged)
    return y, saved


def _mixer_bwd(x, lw, saved, dy, tag):
    (h, p, (qa, ka, va), (qb, kbp, vbp), (qc, kc, vc), flog, fq, fk, lse, (oam, obm, ocm),
     (ya, yb, yc), merged) = saved
    T = x.shape[0]
    grads = {}
    dmerged = _mm(dy, lw["w_out"], mode="nt", name=f"{tag}_dmerged", tn_cap=1024)
    grads["w_out"] = _mm(merged, dy, mode="tn", name=f"{tag}_dwout", tm_cap=512)
    dya, dyb, dyc, dgate = _gate_bwd(p, ya, yb, yc, dmerged, f"{tag}_dgate")
    doa = _mm(dya, lw["w_br_sb"], mode="nt", name=f"{tag}_doa")
    dob = _mm(dyb, lw["w_br_ch"], mode="nt", name=f"{tag}_dob")
    doc = _mm(dyc, lw["w_br_fox"], mode="nt", name=f"{tag}_doc")
    grads["w_br_sb"] = _mm(oam, dya, mode="tn", name=f"{tag}_dwbra", tm_cap=512)
    grads["w_br_ch"] = _mm(obm, dyb, mode="tn", name=f"{tag}_dwbrb", tm_cap=512)
    grads["w_br_fox"] = _mm(ocm, dyc, mode="tn", name=f"{tag}_dwbrc", tm_cap=512)
    dqa, dka, dva = _sb_bwd(qa, ka, va, _to_heads(doa, H_SB), f"{tag}_dsb")
    dqb, dkbp, dvbp, dbias = _chunk_bwd(qb, kbp, vbp, lw["bias"], _to_heads(dob, H_CH), f"{tag}_dch")
    dqc, dkc, dvc, dfk = _fox_bwd(qc, kc, vc, fq, fk, lse, _to_heads(doc, H_FOX), f"{tag}_dfox")
    dflog = _forget_cumsum_bwd(flog, jnp.pad(dfk.reshape(H_FOX, T), ((0, 8 - H_FOX), (0, 0))), f"{tag}_dfcum")
    dqkv = [_from_heads(t) for t in (dqa, dka, dva, dqb, dkbp[:, BAND_PAD:], dvbp[:, BAND_PAD:], dqc, dkc, dvc)]
    dp = jnp.concatenate(dqkv + [dgate, jnp.pad(dflog[:H_FOX].T, ((0, 0), (0, F_PAD - H_FOX)))], axis=1)
    grads["b_in"] = _colsum(dp, f"{tag}_dbin")
    dpb = dp.astype(BF16)
    grads["w_in"] = _mm(h, dpb, mode="tn", name=f"{tag}_dwin", tm_cap=512, tn_cap=896)
    dh = _mm(dpb, lw["w_in"], mode="nt", name=f"{tag}_dh", tk_cap=896)
    dx, grads["g_mix"] = _rmsnorm_bwd(x, lw["g_mix"], dh, dy, f"{tag}_dnorm")
    grads["rel_bias"] = jnp.zeros((N_REL, H_CH), F32).at[_rel_index()[:, CHUNK:]].add(
        jnp.transpose(dbias[:, :, CHUNK:], (1, 2, 0)))
    return dx, grads


def kernel(x, g_ffn1, w_ffn1_in, w_ffn1_out, g_mix, w_in, b_in, rel_bias, w_br_sb, w_br_ch, w_br_fox, w_out, g_ffn2, w_ffn2_in, w_ffn2_out, g_final, loss_target, m_g_ffn1, m_w_ffn1_in, m_w_ffn1_out, m_g_mix, m_w_in, m_b_in, m_rel_bias, m_w_br_sb, m_w_br_ch, m_w_br_fox, m_w_out, m_g_ffn2, m_w_ffn2_in, m_w_ffn2_out, m_g_final, v_g_ffn1, v_w_ffn1_in, v_w_ffn1_out, v_g_mix, v_w_in, v_b_in, v_rel_bias, v_w_br_sb, v_w_br_ch, v_w_br_fox, v_w_out, v_g_ffn2, v_w_ffn2_in, v_w_ffn2_out, v_g_final):
    names = ("g_ffn1", "w_ffn1_in", "w_ffn1_out", "g_mix", "w_in", "b_in", "rel_bias", "w_br_sb", "w_br_ch",
             "w_br_fox", "w_out", "g_ffn2", "w_ffn2_in", "w_ffn2_out", "g_final")
    w = dict(zip(names, (g_ffn1, w_ffn1_in, w_ffn1_out, g_mix, w_in, b_in, rel_bias, w_br_sb, w_br_ch,
                         w_br_fox, w_out, g_ffn2, w_ffn2_in, w_ffn2_out, g_final)))
    m = dict(zip(names, (m_g_ffn1, m_w_ffn1_in, m_w_ffn1_out, m_g_mix, m_w_in, m_b_in, m_rel_bias, m_w_br_sb,
                         m_w_br_ch, m_w_br_fox, m_w_out, m_g_ffn2, m_w_ffn2_in, m_w_ffn2_out, m_g_final)))
    v = dict(zip(names, (v_g_ffn1, v_w_ffn1_in, v_w_ffn1_out, v_g_mix, v_w_in, v_b_in, v_rel_bias, v_w_br_sb,
                         v_w_br_ch, v_w_br_fox, v_w_out, v_g_ffn2, v_w_ffn2_in, v_w_ffn2_out, v_g_final)))
    c = lax.axis_index("c")
    xs = x[0]
    tgt = loss_target[0]

    gathered = _unpack_shards(_gather_weights(_pack_shards(w, BF16), "gather_weights"))
    rel_idx = _rel_index()
    layers = []
    for l in range(DEPTH):
        lw = {n: _full_weight(gathered, n, l) for n in BIG}
        lw["w_in"] = _pad_w_in(lw["w_in"])
        lw["b_in"] = _pad_w_in(w["b_in"][l][None, :])
        lw["bias"] = jnp.transpose(w["rel_bias"][l][rel_idx], (2, 0, 1))
        for n in ("g_ffn1", "g_mix", "g_ffn2"):
            lw[n] = w[n][l][None, :]
        layers.append(lw)

    saved = []
    act = xs
    for l, lw in enumerate(layers):
        x0 = act
        x1, s1 = _ffn_fwd(x0, lw["g_ffn1"], lw["w_ffn1_in"], lw["w_ffn1_out"], f"l{l}_ffn1")
        x2, s2 = _mixer_fwd(x1, lw, f"l{l}_mix")
        act, s3 = _ffn_fwd(x2, lw["g_ffn2"], lw["w_ffn2_in"], lw["w_ffn2_out"], f"l{l}_ffn2")
        saved.append((x0, s1, x1, s2, x2, s3))

    dact, dg_final, loss_row = _final_loss(act, w["g_final"][None, :], tgt, "final_loss")

    big_grads = {n: [None] * DEPTH for n in BIG}
    small_grads = {n: [None] * DEPTH for n in ("g_ffn1", "g_mix", "b_in", "rel_bias", "g_ffn2")}
    for l in reversed(range(DEPTH)):
        lw = layers[l]
        x0, s1, x1, s2, x2, s3 = saved[l]
        dx2, small_grads["g_ffn2"][l], big_grads["w_ffn2_in"][l], big_grads["w_ffn2_out"][l] = _ffn_bwd(
            x2, lw["g_ffn2"], lw["w_ffn2_in"], lw["w_ffn2_out"], s3, dact, f"l{l}_ffn2")
        dx1, mg = _mixer_bwd(x1, lw, s2, dx2, f"l{l}_mix")
        for n in ("w_br_sb", "w_br_ch", "w_br_fox", "w_out"):
            big_grads[n][l] = mg[n]
        big_grads["w_in"][l] = _unpad_w_in(mg["w_in"])
        small_grads["b_in"][l] = _unpad_w_in(mg["b_in"])[0]
        small_grads["g_mix"][l] = mg["g_mix"][0]
        small_grads["rel_bias"][l] = mg["rel_bias"]
        dact, dg1, big_grads["w_ffn1_in"][l], big_grads["w_ffn1_out"][l] = _ffn_bwd(
            x0, lw["g_ffn1"], lw["w_ffn1_in"], lw["w_ffn1_out"], s1, dx1, f"l{l}_ffn1")
        small_grads["g_ffn1"][l] = dg1[0]
        small_grads["g_ffn2"][l] = small_grads["g_ffn2"][l][0]
    grad_x = dact[None]

    small = {n: jnp.stack(small_grads[n]) for n in small_grads}
    small["g_final"] = dg_final[0]
    small_sum, loss = _unpack_small(_allreduce_small(_pack_small(small, loss_row[0, :1]), "allreduce_small"),
                                    {n: w[n].shape for n in SMALL})

    by_chip = {n: jnp.stack([_split_grad(n, big_grads[n][l]) for l in range(DEPTH)], axis=1) for n in BIG}
    gpack = jnp.stack([_pack_shards({n: by_chip[n][j] for n in BIG}, BF16) for j in range(N_CHIPS)])
    half = PACK_ROWS_PAD // 2
    keep = lax.dynamic_slice_in_dim(gpack, c * half, half, axis=1)
    give = lax.dynamic_slice_in_dim(gpack, (1 - c) * half, half, axis=1)
    pair = _add_pair(keep, _swap_sibling(give, "grad_swap"), "grad_pair_sum")
    mine = _sum_chips(_scatter_chips(pair, "grad_scatter"), "grad_chip_sum")
    theirs = _swap_sibling(mine, "grad_share")
    lo = jnp.where(c == 0, mine, theirs)
    hi = jnp.where(c == 0, theirs, mine)
    grads = _unpack_shards(jnp.concatenate([lo, hi], axis=0))
    grads.update(small_sum)

    delta, new_m, new_v = {}, {}, {}
    for n in BIG:
        shape = w[n].shape
        flat = lambda t: t.reshape(-1, shape[-1])
        d, nm, nv = _adamw(flat(w[n]), flat(grads[n]), flat(m[n]), flat(v[n]), f"adamw_{n}")
        delta[n], new_m[n], new_v[n] = d.reshape(shape), nm.reshape(shape), nv.reshape(shape)
    zero = jnp.zeros((1,), F32)
    sd, sm, sv = _adamw(_pack_small(w, zero), _pack_small(grads, zero), _pack_small(m, zero), _pack_small(v, zero),
                        "adamw_small")
    shapes = {n: w[n].shape for n in SMALL}
    for dst, src in ((delta, sd), (new_m, sm), (new_v, sv)):
        dst.update(_unpack_small(src, shapes)[0])

    return (loss, grad_x, *[grads[n] for n in names], *[delta[n] for n in names],
            *[new_m[n] for n in names], *[new_v[n] for n in names])
```

```python
import functools

import numpy as np
import jax
import jax.numpy as jnp
from jax import lax
from jax.experimental import pallas as pl
from jax.experimental.pallas import tpu as pltpu

F32 = jnp.float32
BF16 = jnp.bfloat16

D_MODEL = 1024
DEPTH = 2
CHUNK = 64
HEAD_DIM = 64
H_SB, H_CH, H_FOX = 4, 8, 4
W_SB, W_CH, W_FOX = H_SB * HEAD_DIM, H_CH * HEAD_DIM, H_FOX * HEAD_DIM
LEFT_CHUNKS = 8
MAX_REL = 128
N_REL = 2 * MAX_REL + 1
D_FF = 2816
QKV_WIDTH = 3 * (W_SB + W_CH + W_FOX)
GATE_WIDTH = 3 * D_MODEL
IN_WIDTH = QKV_WIDTH + H_FOX + GATE_WIDTH
F_PAD = 128
P_WIDTH = QKV_WIDTH + GATE_WIDTH + F_PAD
RMS_EPS = 1e-6
NEG = -1e30
SCALE = HEAD_DIM ** -0.5
QB = 256
BAND = (LEFT_CHUNKS + 2) * CHUNK
BAND_PAD = BAND - CHUNK

ADAM_LR, ADAM_B1, ADAM_B2, ADAM_EPS, ADAM_WD, ADAM_STEP = 0.001, 0.9, 0.999, 1e-08, 0.01, 10

N_CHIPS = 4
PACK_COLS = 1024
VMEM_BUDGET = 52 * 1024 * 1024

NT = (((1,), (1,)), ((), ()))
TN = (((0,), (0,)), ((), ()))
NN = (((1,), (0,)), ((), ()))
MESH = pl.DeviceIdType.MESH


def _pcall(body, **kw):
    return pl.pallas_call(body, **kw)


def _pick(n, cap, mult=128):
    best = None
    d = mult
    while d <= min(n, cap):
        if n % d == 0:
            best = d
        d += mult
    return n if best is None else best


def _nbytes(shape, dtype):
    return int(np.prod(shape)) * jnp.dtype(dtype).itemsize


def _mm(a, b, *, mode, name, out_dtype=F32, alpha=1.0, bias=None, res=None, tm_cap=1024, tn_cap=512,
        tk_cap=2816):
    if mode == "tn":
        K, M = a.shape
    else:
        M, K = a.shape
    N = b.shape[0] if mode == "nt" else b.shape[1]
    tm = _pick(M, tm_cap)
    tn = _pick(N, tn_cap)
    tk = K if K <= tk_cap else _pick(K, tk_cap)
    nk = K // tk
    dn = {"nn": NN, "nt": NT, "tn": TN}[mode]

    a_spec = (pl.BlockSpec((tk, tm), lambda i, j, k: (k, i)) if mode == "tn"
              else pl.BlockSpec((tm, tk), lambda i, j, k: (i, k)))
    b_spec = (pl.BlockSpec((tn, tk), lambda i, j, k: (j, k)) if mode == "nt"
              else pl.BlockSpec((tk, tn), lambda i, j, k: (k, j)))
    in_specs = [a_spec, b_spec]
    args = [a, b]
    if bias is not None:
        in_specs.append(pl.BlockSpec((1, tn), lambda i, j, k: (0, j)))
        args.append(bias)
    if res is not None:
        in_specs.append(pl.BlockSpec((tm, tn), lambda i, j, k: (i, j)))
        args.append(res)

    est = 2 * (_nbytes((tm, tk), a.dtype) + _nbytes((tk, tn), b.dtype) + _nbytes((tm, tn), out_dtype))
    est += _nbytes((tm, tn), F32) * (3 if res is not None else 1)
    assert est < VMEM_BUDGET, (name, est)

    def body(*refs):
        a_ref, b_ref = refs[0], refs[1]
        pos = 2
        bias_ref = res_ref = None
        if bias is not None:
            bias_ref = refs[pos]
            pos += 1
        if res is not None:
            res_ref = refs[pos]
            pos += 1
        o_ref = refs[pos]
        acc_ref = refs[pos + 1] if nk > 1 else None

        part = lax.dot_general(a_ref[...].astype(BF16), b_ref[...].astype(BF16), dn,
                               preferred_element_type=F32)

        def finish(acc):
            if alpha != 1.0:
                acc = acc * alpha
            if bias_ref is not None:
                acc = acc + bias_ref[...]
            if res_ref is not None:
                acc = acc + res_ref[...]
            o_ref[...] = acc.astype(out_dtype)

        if nk == 1:
            finish(part)
        else:
            k = pl.program_id(2)

            @pl.when(k == 0)
            def _():
                acc_ref[...] = part

            @pl.when(k > 0)
            def _():
                acc_ref[...] += part

            @pl.when(k == nk - 1)
            def _():
                finish(acc_ref[...])

    return _pcall(
        body, name=name, grid=(M // tm, N // tn, nk), in_specs=in_specs,
        out_specs=pl.BlockSpec((tm, tn), lambda i, j, k: (i, j)),
        out_shape=jax.ShapeDtypeStruct((M, N), out_dtype),
        scratch_shapes=[pltpu.VMEM((tm, tn), F32)] if nk > 1 else [],
        compiler_params=pltpu.CompilerParams(dimension_semantics=("parallel", "parallel", "arbitrary")),
    )(*args)


def _rmsnorm_fwd(x, g, name):
    T, Dm = x.shape
    tm = _pick(T, 256, 8)

    def body(x_ref, g_ref, h_ref):
        xv = x_ref[...]
        rstd = lax.rsqrt(jnp.mean(xv * xv, axis=-1, keepdims=True) + RMS_EPS)
        h_ref[...] = (xv * rstd * g_ref[...]).astype(BF16)

    return _pcall(
        body, name=name, grid=(T // tm,),
        in_specs=[pl.BlockSpec((tm, Dm), lambda i: (i, 0)), pl.BlockSpec((1, Dm), lambda i: (0, 0))],
        out_specs=pl.BlockSpec((tm, Dm), lambda i: (i, 0)),
        out_shape=jax.ShapeDtypeStruct((T, Dm), BF16),
    )(x, g)


def _rmsnorm_bwd(x, g, dh, dres, name):
    T, Dm = x.shape
    tm = _pick(T, 256, 8)

    def body(x_ref, g_ref, dh_ref, dres_ref, dx_ref, dg_ref):
        xv = x_ref[...]
        rstd = lax.rsqrt(jnp.mean(xv * xv, axis=-1, keepdims=True) + RMS_EPS)
        xhat = xv * rstd
        dhv = dh_ref[...]
        dyg = dhv * g_ref[...]
        dx_ref[...] = dres_ref[...] + rstd * (dyg - xhat * jnp.mean(dyg * xhat, axis=-1, keepdims=True))
        part = jnp.sum(dhv * xhat, axis=0, keepdims=True)

        @pl.when(pl.program_id(0) == 0)
        def _():
            dg_ref[...] = part

        @pl.when(pl.program_id(0) > 0)
        def _():
            dg_ref[...] += part

    row = pl.BlockSpec((tm, Dm), lambda i: (i, 0))
    vec = pl.BlockSpec((1, Dm), lambda i: (0, 0))
    return _pcall(
        body, name=name, grid=(T // tm,), in_specs=[row, vec, row, row], out_specs=[row, vec],
        out_shape=[jax.ShapeDtypeStruct((T, Dm), F32), jax.ShapeDtypeStruct((1, Dm), F32)],
        compiler_params=pltpu.CompilerParams(dimension_semantics=("arbitrary",)),
    )(x, g, dh, dres)


def _final_loss(x, g, tgt, name):
    T, Dm = x.shape
    tm = _pick(T, 256, 8)

    def body(x_ref, g_ref, t_ref, dx_ref, dg_ref, loss_ref):
        xv = x_ref[...]
        gv = g_ref[...]
        rstd = lax.rsqrt(jnp.mean(xv * xv, axis=-1, keepdims=True) + RMS_EPS)
        xhat = xv * rstd
        err = xhat * gv - t_ref[...]
        part_loss = 0.5 * jnp.sum(jnp.mean(err * err, axis=-1, keepdims=True), axis=0, keepdims=True)
        dy = err * (1.0 / Dm)
        dyg = dy * gv
        dx_ref[...] = rstd * (dyg - xhat * jnp.mean(dyg * xhat, axis=-1, keepdims=True))
        part_g = jnp.sum(dy * xhat, axis=0, keepdims=True)
        part_l = jnp.broadcast_to(part_loss, (1, 128))

        @pl.when(pl.program_id(0) == 0)
        def _():
            dg_ref[...] = part_g
            loss_ref[...] = part_l

        @pl.when(pl.program_id(0) > 0)
        def _():
            dg_ref[...] += part_g
            loss_ref[...] += part_l

    row = pl.BlockSpec((tm, Dm), lambda i: (i, 0))
    vec = pl.BlockSpec((1, Dm), lambda i: (0, 0))
    return _pcall(
        body, name=name, grid=(T // tm,), in_specs=[row, vec, row],
        out_specs=[row, vec, pl.BlockSpec((1, 128), lambda i: (0, 0))],
        out_shape=[jax.ShapeDtypeStruct((T, Dm), F32), jax.ShapeDtypeStruct((1, Dm), F32),
                   jax.ShapeDtypeStruct((1, 128), F32)],
        compiler_params=pltpu.CompilerParams(dimension_semantics=("arbitrary",)),
    )(x, g, tgt)


def _sigmoid(z):
    return 1.0 / (1.0 + jnp.exp(-z))


def _swiglu_fwd(gu, name):
    T = gu.shape[0]
    tm = _pick(T, 128, 8)

    def body(gu_ref, a_ref):
        gv = gu_ref[:, :D_FF]
        uv = gu_ref[:, D_FF:]
        a_ref[...] = (gv * _sigmoid(gv) * uv).astype(BF16)

    return _pcall(
        body, name=name, grid=(T // tm,), in_specs=[pl.BlockSpec((tm, 2 * D_FF), lambda i: (i, 0))],
        out_specs=pl.BlockSpec((tm, D_FF), lambda i: (i, 0)),
        out_shape=jax.ShapeDtypeStruct((T, D_FF), BF16),
    )(gu)


def _swiglu_bwd(gu, da, name):
    T = gu.shape[0]
    tm = _pick(T, 128, 8)

    def body(gu_ref, da_ref, d_ref):
        gv = gu_ref[:, :D_FF]
        uv = gu_ref[:, D_FF:]
        dav = da_ref[...]
        sg = _sigmoid(gv)
        d_ref[:, :D_FF] = (dav * uv * (sg + gv * sg * (1.0 - sg))).astype(BF16)
        d_ref[:, D_FF:] = (dav * gv * sg).astype(BF16)

    return _pcall(
        body, name=name, grid=(T // tm,),
        in_specs=[pl.BlockSpec((tm, 2 * D_FF), lambda i: (i, 0)), pl.BlockSpec((tm, D_FF), lambda i: (i, 0))],
        out_specs=pl.BlockSpec((tm, 2 * D_FF), lambda i: (i, 0)),
        out_shape=jax.ShapeDtypeStruct((T, 2 * D_FF), BF16),
    )(gu, da)


GATE_BLOCK0 = QKV_WIDTH // D_MODEL


def _gate_fwd(p, ya, yb, yc, name):
    T = p.shape[0]
    tm = _pick(T, 256, 8)

    def body(ga_ref, gb_ref, gc_ref, ya_ref, yb_ref, yc_ref, o_ref):
        o_ref[...] = (_sigmoid(ga_ref[...]) * ya_ref[...] + _sigmoid(gb_ref[...]) * yb_ref[...]
                      + _sigmoid(gc_ref[...]) * yc_ref[...]).astype(BF16)

    gate = [pl.BlockSpec((tm, D_MODEL), functools.partial(lambda i, kk: (i, GATE_BLOCK0 + kk), kk=kk))
            for kk in range(3)]
    row = pl.BlockSpec((tm, D_MODEL), lambda i: (i, 0))
    return _pcall(
        body, name=name, grid=(T // tm,), in_specs=gate + [row, row, row], out_specs=row,
        out_shape=jax.ShapeDtypeStruct((T, D_MODEL), BF16),
    )(p, p, p, ya, yb, yc)


def _gate_bwd(p, ya, yb, yc, dm, name):
    T = p.shape[0]
    tm = _pick(T, 256, 8)

    def body(ga_ref, gb_ref, gc_ref, ya_ref, yb_ref, yc_ref, dm_ref, da_ref, db_ref, dc_ref, dg_ref):
        dmv = dm_ref[...]
        for kk, (g_ref, y_ref, d_ref) in enumerate(((ga_ref, ya_ref, da_ref), (gb_ref, yb_ref, db_ref),
                                                    (gc_ref, yc_ref, dc_ref))):
            s = _sigmoid(g_ref[...])
            d_ref[...] = (s * dmv).astype(BF16)
            dg_ref[:, kk * D_MODEL:(kk + 1) * D_MODEL] = dmv * y_ref[...] * s * (1.0 - s)

    gate = [pl.BlockSpec((tm, D_MODEL), functools.partial(lambda i, kk: (i, GATE_BLOCK0 + kk), kk=kk))
            for kk in range(3)]
    row = pl.BlockSpec((tm, D_MODEL), lambda i: (i, 0))
    return _pcall(
        body, name=name, grid=(T // tm,), in_specs=gate + [row, row, row, row],
        out_specs=[row, row, row, pl.BlockSpec((tm, GATE_WIDTH), lambda i: (i, 0))],
        out_shape=[jax.ShapeDtypeStruct((T, D_MODEL), BF16)] * 3 + [jax.ShapeDtypeStruct((T, GATE_WIDTH), F32)],
    )(p, p, p, ya, yb, yc, dm)


def _colsum(a, name):
    T, N = a.shape
    tm = _pick(T, 256, 8)

    def body(a_ref, o_ref):
        part = jnp.sum(a_ref[...], axis=0, keepdims=True)

        @pl.when(pl.program_id(0) == 0)
        def _():
            o_ref[...] = part

        @pl.when(pl.program_id(0) > 0)
        def _():
            o_ref[...] += part

    return _pcall(
        body, name=name, grid=(T // tm,), in_specs=[pl.BlockSpec((tm, N), lambda i: (i, 0))],
        out_specs=pl.BlockSpec((1, N), lambda i: (0, 0)), out_shape=jax.ShapeDtypeStruct((1, N), F32),
        compiler_params=pltpu.CompilerParams(dimension_semantics=("arbitrary",)),
    )(a)


def _adamw(w, g, m, v, name):
    R, C = w.shape
    tr = 256 if R % 256 == 0 else R
    c1 = 1.0 / (1.0 - ADAM_B1 ** ADAM_STEP)
    c2 = 1.0 / (1.0 - ADAM_B2 ** ADAM_STEP)

    def body(w_ref, g_ref, m_ref, v_ref, d_ref, nm_ref, nv_ref):
        gv = g_ref[...]
        mn = ADAM_B1 * m_ref[...] + (1.0 - ADAM_B1) * gv
        vn = ADAM_B2 * v_ref[...] + (1.0 - ADAM_B2) * (gv * gv)
        nm_ref[...] = mn
        nv_ref[...] = vn
        d_ref[...] = -ADAM_LR * ((mn * c1) / (jnp.sqrt(vn * c2) + ADAM_EPS) + ADAM_WD * w_ref[...])

    spec = pl.BlockSpec((tr, C), lambda i: (i, 0))
    return _pcall(
        body, name=name, grid=(R // tr,), in_specs=[spec] * 4, out_specs=[spec] * 3,
        out_shape=[jax.ShapeDtypeStruct((R, C), F32)] * 3,
    )(w, g, m, v)


def _log_sigmoid(z):
    return jnp.minimum(z, 0.0) - jnp.log(1.0 + jnp.exp(-jnp.abs(z)))


def _dot3(x, m01):
    x1 = x.astype(BF16)
    r1 = x - x1.astype(F32)
    x2 = r1.astype(BF16)
    x3 = (r1 - x2.astype(F32)).astype(BF16)
    n = x.shape[0]
    if n % 16:
        return (jnp.dot(x1, m01, preferred_element_type=F32) + jnp.dot(x2, m01, preferred_element_type=F32)
                + jnp.dot(x3, m01, preferred_element_type=F32))
    y = jnp.dot(jnp.concatenate([x1, x2, x3], axis=0), m01, preferred_element_type=F32)
    return y[:n] + y[n:2 * n] + y[2 * n:]


def _dot_nt(a, b):
    return lax.dot_general(a, b, NT, preferred_element_type=F32)


def _dot_tn(a, b):
    return lax.dot_general(a, b, TN, preferred_element_type=F32)


def _iota2(shape):
    return lax.broadcasted_iota(jnp.int32, shape, 0), lax.broadcasted_iota(jnp.int32, shape, 1)


HEADS_PER_STEP = 4
HEADS = range(HEADS_PER_STEP)


CHUNK_HEADS_PER_STEP = 2
CHUNK_HEADS = range(CHUNK_HEADS_PER_STEP)


def _head_spec(T, rows=None, width=HEAD_DIM, heads=HEADS_PER_STEP):
    return pl.BlockSpec((heads, T if rows is None else rows, width), lambda g: (g, 0, 0))


def _sb_weights(qb, kb, t0, s0, racc, m_gt, row, col):
    z = _dot_nt(qb, kb) * SCALE
    strict = (s0 + col) < (t0 + row)
    lb = _log_sigmoid(z)
    lf = jnp.where(strict, lb - z, 0.0)
    between = _dot3(lf, m_gt) + racc
    w = jnp.where(strict, jnp.exp(lb + between), 0.0)
    return strict, lb, lf, w


def _sb_fwd(q, k, v, name):
    H, T, _ = q.shape
    nb = T // QB

    def body(q_ref, k_ref, v_ref, o_ref):
        row, col = _iota2((QB, QB))
        m_gt = (row > col).astype(BF16)

        def qblock(i, _):
            t0 = pl.multiple_of(i * QB, QB)
            qbs = [q_ref[h, pl.ds(t0, QB), :].astype(BF16) for h in HEADS]

            def kblock(jj, carry):
                s0 = pl.multiple_of((i - jj) * QB, QB)
                out = []
                for h in HEADS:
                    acc, racc = carry[h]
                    kb = k_ref[h, pl.ds(s0, QB), :].astype(BF16)
                    vb = v_ref[h, pl.ds(s0, QB), :].astype(BF16)
                    _, _, lf, w = _sb_weights(qbs[h], kb, t0, s0, racc, m_gt, row, col)
                    acc = acc + jnp.dot(w.astype(BF16), vb, preferred_element_type=F32)
                    out.append((acc, racc + jnp.sum(lf, axis=1, keepdims=True)))
                return tuple(out)

            res = lax.fori_loop(0, i + 1, kblock,
                                tuple((jnp.zeros((QB, HEAD_DIM), F32), jnp.zeros((QB, 1), F32)) for _ in HEADS))
            for h in HEADS:
                o_ref[h, pl.ds(t0, QB), :] = res[h][0]
            return 0

        lax.fori_loop(0, nb, qblock, 0)

    spec = _head_spec(T)
    return _pcall(body, name=name, grid=(H // HEADS_PER_STEP,), in_specs=[spec] * 3, out_specs=spec,
                  out_shape=jax.ShapeDtypeStruct((H, T, HEAD_DIM), F32))(q, k, v)


def _sb_bwd(q, k, v, do, name):
    H, T, _ = q.shape
    nb = T // QB

    def body(q_ref, k_ref, v_ref, do_ref, dq_ref, dk_ref, dv_ref, racc_ref):
        row, col = _iota2((QB, QB))
        m_gt = (row > col).astype(BF16)
        m_lt = (row < col).astype(BF16)
        dk_ref[...] = jnp.zeros_like(dk_ref)
        dv_ref[...] = jnp.zeros_like(dv_ref)

        def qblock(i, _):
            t0 = pl.multiple_of(i * QB, QB)
            qbs = [q_ref[h, pl.ds(t0, QB), :].astype(BF16) for h in HEADS]
            dobs = [do_ref[h, pl.ds(t0, QB), :].astype(BF16) for h in HEADS]

            def suffix(jj, raccs):
                j = i - jj
                s0 = pl.multiple_of(j * QB, QB)
                out = []
                for h in HEADS:
                    kb = k_ref[h, pl.ds(s0, QB), :].astype(BF16)
                    z = _dot_nt(qbs[h], kb) * SCALE
                    lf = jnp.where((s0 + col) < (t0 + row), _log_sigmoid(z) - z, 0.0)
                    racc_ref[h, j] = raccs[h]
                    out.append(raccs[h] + jnp.sum(lf, axis=1, keepdims=True))
                return tuple(out)

            lax.fori_loop(0, i + 1, suffix, tuple(jnp.zeros((QB, 1), F32) for _ in HEADS))

            def kblock(j, carry):
                s0 = pl.multiple_of(j * QB, QB)
                out = []
                for h in HEADS:
                    dq, cacc = carry[h]
                    kb = k_ref[h, pl.ds(s0, QB), :].astype(BF16)
                    vb = v_ref[h, pl.ds(s0, QB), :].astype(BF16)
                    strict, lb, _, w = _sb_weights(qbs[h], kb, t0, s0, racc_ref[h, j], m_gt, row, col)
                    e = _dot_nt(dobs[h], vb) * w
                    c_left = _dot3(e, m_lt) + cacc
                    sig = jnp.exp(lb)
                    dz = jnp.where(strict, e * (1.0 - sig) - c_left * sig, 0.0).astype(BF16)
                    dq = dq + jnp.dot(dz, kb, preferred_element_type=F32)
                    dk_ref[h, pl.ds(s0, QB), :] += _dot_tn(dz, qbs[h]) * SCALE
                    dv_ref[h, pl.ds(s0, QB), :] += _dot_tn(w.astype(BF16), dobs[h])
                    out.append((dq, cacc + jnp.sum(e, axis=1, keepdims=True)))
                return tuple(out)

            res = lax.fori_loop(0, i + 1, kblock,
                                tuple((jnp.zeros((QB, HEAD_DIM), F32), jnp.zeros((QB, 1), F32)) for _ in HEADS))
            for h in HEADS:
                dq_ref[h, pl.ds(t0, QB), :] = res[h][0] * SCALE
            return 0

        lax.fori_loop(0, nb, qblock, 0)

    spec = _head_spec(T)
    return _pcall(body, name=name, grid=(H // HEADS_PER_STEP,), in_specs=[spec] * 4, out_specs=[spec] * 3,
                  out_shape=[jax.ShapeDtypeStruct((H, T, HEAD_DIM), F32)] * 3,
                  scratch_shapes=[pltpu.VMEM((HEADS_PER_STEP, nb, QB, 1), F32)])(q, k, v, do)


def _fox_logits(qb, kb, fq, fk, t0, s0, row, col):
    z = _dot_nt(qb, kb) * SCALE + fq - fk
    return jnp.where((s0 + col) <= (t0 + row), z, NEG)


def _fox_specs(T):
    return _head_spec(T, width=1), _head_spec(T, rows=T // QB, width=QB)


def _fox_fwd(q, k, v, fq, fk, name):
    H, T, _ = q.shape
    nb = T // QB

    def body(q_ref, k_ref, v_ref, fq_ref, fk_ref, o_ref, lse_ref):
        row, col = _iota2((QB, QB))

        def qblock(i, _):
            t0 = pl.multiple_of(i * QB, QB)
            qbs = [q_ref[h, pl.ds(t0, QB), :].astype(BF16) for h in HEADS]
            fqs = [fq_ref[h, pl.ds(t0, QB), :] for h in HEADS]

            def kblock(j, carry):
                s0 = pl.multiple_of(j * QB, QB)
                out = []
                for h in HEADS:
                    acc, m, l = carry[h]
                    kb = k_ref[h, pl.ds(s0, QB), :].astype(BF16)
                    vb = v_ref[h, pl.ds(s0, QB), :].astype(BF16)
                    z = _fox_logits(qbs[h], kb, fqs[h], fk_ref[h, pl.ds(j, 1), :], t0, s0, row, col)
                    m_new = jnp.maximum(m, jnp.max(z, axis=1, keepdims=True))
                    a = jnp.exp(m - m_new)
                    p = jnp.exp(z - m_new)
                    acc = a * acc + jnp.dot(p.astype(BF16), vb, preferred_element_type=F32)
                    out.append((acc, m_new, a * l + jnp.sum(p, axis=1, keepdims=True)))
                return tuple(out)

            res = lax.fori_loop(0, i + 1, kblock,
                                tuple((jnp.zeros((QB, HEAD_DIM), F32), jnp.full((QB, 1), NEG, F32),
                                       jnp.zeros((QB, 1), F32)) for _ in HEADS))
            for h in HEADS:
                acc, m, l = res[h]
                o_ref[h, pl.ds(t0, QB), :] = acc / l
                lse_ref[h, pl.ds(t0, QB), :] = m + jnp.log(l)
            return 0

        lax.fori_loop(0, nb, qblock, 0)

    spec = _head_spec(T)
    fq_spec, fk_spec = _fox_specs(T)
    return _pcall(body, name=name, grid=(H // HEADS_PER_STEP,), in_specs=[spec] * 3 + [fq_spec, fk_spec],
                  out_specs=[spec, fq_spec],
                  out_shape=[jax.ShapeDtypeStruct((H, T, HEAD_DIM), F32), jax.ShapeDtypeStruct((H, T, 1), F32)],
                  )(q, k, v, fq, fk)


def _fox_bwd(q, k, v, fq, fk, lse, do, name):
    H, T, _ = q.shape
    nb = T // QB

    def body(q_ref, k_ref, v_ref, fq_ref, fk_ref, lse_ref, do_ref, dq_ref, dk_ref, dv_ref, dfk_ref):
        row, col = _iota2((QB, QB))
        dk_ref[...] = jnp.zeros_like(dk_ref)
        dv_ref[...] = jnp.zeros_like(dv_ref)
        dfk_ref[...] = jnp.zeros_like(dfk_ref)

        def qblock(i, _):
            t0 = pl.multiple_of(i * QB, QB)
            qbs = [q_ref[h, pl.ds(t0, QB), :].astype(BF16) for h in HEADS]
            fqs = [fq_ref[h, pl.ds(t0, QB), :] for h in HEADS]
            lses = [lse_ref[h, pl.ds(t0, QB), :] for h in HEADS]
            dobs = [do_ref[h, pl.ds(t0, QB), :].astype(BF16) for h in HEADS]

            def probs(h, j):
                s0 = pl.multiple_of(j * QB, QB)
                kb = k_ref[h, pl.ds(s0, QB), :].astype(BF16)
                vb = v_ref[h, pl.ds(s0, QB), :].astype(BF16)
                z = _fox_logits(qbs[h], kb, fqs[h], fk_ref[h, pl.ds(j, 1), :], t0, s0, row, col)
                return s0, kb, jnp.exp(z - lses[h]), _dot_nt(dobs[h], vb)

            def row_dot(j, deltas):
                out = []
                for h in HEADS:
                    _, _, p, dp = probs(h, j)
                    out.append(deltas[h] + jnp.sum(p * dp, axis=1, keepdims=True))
                return tuple(out)

            deltas = lax.fori_loop(0, i + 1, row_dot, tuple(jnp.zeros((QB, 1), F32) for _ in HEADS))

            def kblock(j, dqs):
                out = []
                for h in HEADS:
                    s0, kb, p, dp = probs(h, j)
                    dz = p * (dp - deltas[h])
                    dzb = dz.astype(BF16)
                    dk_ref[h, pl.ds(s0, QB), :] += _dot_tn(dzb, qbs[h]) * SCALE
                    dv_ref[h, pl.ds(s0, QB), :] += _dot_tn(p.astype(BF16), dobs[h])
                    dfk_ref[h, pl.ds(j, 1), :] += -jnp.sum(dz, axis=0, keepdims=True)
                    out.append(dqs[h] + jnp.dot(dzb, kb, preferred_element_type=F32))
                return tuple(out)

            dqs = lax.fori_loop(0, i + 1, kblock, tuple(jnp.zeros((QB, HEAD_DIM), F32) for _ in HEADS))
            for h in HEADS:
                dq_ref[h, pl.ds(t0, QB), :] = dqs[h] * SCALE
            return 0

        lax.fori_loop(0, nb, qblock, 0)

    spec = _head_spec(T)
    fq_spec, fk_spec = _fox_specs(T)
    return _pcall(body, name=name, grid=(H // HEADS_PER_STEP,),
                  in_specs=[spec] * 3 + [fq_spec, fk_spec, fq_spec, spec],
                  out_specs=[spec, spec, spec, fk_spec],
                  out_shape=[jax.ShapeDtypeStruct((H, T, HEAD_DIM), F32)] * 3
                  + [jax.ShapeDtypeStruct((H, nb, QB), F32)],
                  )(q, k, v, fq, fk, lse, do)


def _forget_cumsum(flog, name):
    R, T = flog.shape
    nb = T // QB

    def body(x_ref, o_ref):
        row, col = _iota2((QB, QB))
        m_le = (row <= col).astype(BF16)
        carry = jnp.zeros((R, 1), F32)
        for b in range(nb):
            lf = _log_sigmoid(x_ref[:, b * QB:(b + 1) * QB])
            o_ref[:, b * QB:(b + 1) * QB] = _dot3(lf, m_le) + carry
            carry = carry + jnp.sum(lf, axis=1, keepdims=True)

    spec = pl.BlockSpec((R, T), lambda: (0, 0))
    return _pcall(body, name=name, in_specs=[spec], out_specs=spec,
                  out_shape=jax.ShapeDtypeStruct((R, T), F32))(flog)


def _forget_cumsum_bwd(flog, df, name):
    R, T = flog.shape
    nb = T // QB

    def body(x_ref, df_ref, o_ref):
        row, col = _iota2((QB, QB))
        m_ge = (row >= col).astype(BF16)
        carry = jnp.zeros((R, 1), F32)
        for b in reversed(range(nb)):
            dfb = df_ref[:, b * QB:(b + 1) * QB]
            dlog = _dot3(dfb, m_ge) + carry
            o_ref[:, b * QB:(b + 1) * QB] = dlog * _sigmoid(-x_ref[:, b * QB:(b + 1) * QB])
            carry = carry + jnp.sum(dfb, axis=1, keepdims=True)

    spec = pl.BlockSpec((R, T), lambda: (0, 0))
    return _pcall(body, name=name, in_specs=[spec, spec], out_specs=spec,
                  out_shape=jax.ShapeDtypeStruct((R, T), F32))(flog, df)


def _chunk_probs(qc, kb, bias, c, col):
    z = _dot_nt(qc, kb) * SCALE + bias
    z = jnp.where((c - (LEFT_CHUNKS + 1)) * CHUNK + col >= jnp.maximum(0, (c - LEFT_CHUNKS) * CHUNK), z, NEG)
    p = jnp.exp(z - jnp.max(z, axis=1, keepdims=True))
    return p, jnp.sum(p, axis=1, keepdims=True)


def _chunk_specs(T):
    n = CHUNK_HEADS_PER_STEP
    return (_head_spec(T, heads=n), _head_spec(T, rows=T + BAND_PAD, heads=n),
            _head_spec(T, rows=CHUNK, width=BAND, heads=n))


def _chunk_fwd(q, kp, vp, bias, name):
    H, T, _ = q.shape
    nc = T // CHUNK

    def body(q_ref, kp_ref, vp_ref, bias_ref, o_ref):
        _, col = _iota2((CHUNK, BAND))

        def chunk(c, _):
            t0 = pl.multiple_of(c * CHUNK, CHUNK)
            for h in CHUNK_HEADS:
                qc = q_ref[h, pl.ds(t0, CHUNK), :].astype(BF16)
                kb = kp_ref[h, pl.ds(t0, BAND), :].astype(BF16)
                vb = vp_ref[h, pl.ds(t0, BAND), :].astype(BF16)
                p, l = _chunk_probs(qc, kb, bias_ref[h], c, col)
                o_ref[h, pl.ds(t0, CHUNK), :] = jnp.dot(p.astype(BF16), vb, preferred_element_type=F32) / l
            return 0

        lax.fori_loop(0, nc, chunk, 0)

    q_spec, kp_spec, b_spec = _chunk_specs(T)
    return _pcall(body, name=name, grid=(H // CHUNK_HEADS_PER_STEP,), in_specs=[q_spec, kp_spec, kp_spec, b_spec],
                  out_specs=q_spec,
                  out_shape=jax.ShapeDtypeStruct((H, T, HEAD_DIM), F32))(q, kp, vp, bias)


def _chunk_bwd(q, kp, vp, bias, do, name):
    H, T, _ = q.shape
    nc = T // CHUNK

    def body(q_ref, kp_ref, vp_ref, bias_ref, do_ref, dq_ref, dkp_ref, dvp_ref, db_ref):
        _, col = _iota2((CHUNK, BAND))
        dkp_ref[...] = jnp.zeros_like(dkp_ref)
        dvp_ref[...] = jnp.zeros_like(dvp_ref)
        db_ref[...] = jnp.zeros_like(db_ref)

        def chunk(c, _):
            t0 = pl.multiple_of(c * CHUNK, CHUNK)
            for h in CHUNK_HEADS:
                qc = q_ref[h, pl.ds(t0, CHUNK), :].astype(BF16)
                kb = kp_ref[h, pl.ds(t0, BAND), :].astype(BF16)
                vb = vp_ref[h, pl.ds(t0, BAND), :].astype(BF16)
                dob = do_ref[h, pl.ds(t0, CHUNK), :].astype(BF16)
                p, l = _chunk_probs(qc, kb, bias_ref[h], c, col)
                p = p / l
                dp = _dot_nt(dob, vb)
                dz = p * (dp - jnp.sum(p * dp, axis=1, keepdims=True))
                dzb = dz.astype(BF16)
                dq_ref[h, pl.ds(t0, CHUNK), :] = jnp.dot(dzb, kb, preferred_element_type=F32) * SCALE
                dkp_ref[h, pl.ds(t0, BAND), :] += _dot_tn(dzb, qc) * SCALE
                dvp_ref[h, pl.ds(t0, BAND), :] += _dot_tn(p.astype(BF16), dob)
                db_ref[h] += dz
            return 0

        lax.fori_loop(0, nc, chunk, 0)

    q_spec, kp_spec, b_spec = _chunk_specs(T)
    return _pcall(body, name=name, grid=(H // CHUNK_HEADS_PER_STEP,),
                  in_specs=[q_spec, kp_spec, kp_spec, b_spec, q_spec],
                  out_specs=[q_spec, kp_spec, kp_spec, b_spec],
                  out_shape=[jax.ShapeDtypeStruct((H, T, HEAD_DIM), F32),
                             jax.ShapeDtypeStruct((H, T + BAND_PAD, HEAD_DIM), F32),
                             jax.ShapeDtypeStruct((H, T + BAND_PAD, HEAD_DIM), F32),
                             jax.ShapeDtypeStruct((H, CHUNK, BAND), F32)])(q, kp, vp, bias, do)


HBM_SPEC = pl.BlockSpec(memory_space=pltpu.HBM)


def _position():
    return lax.axis_index("x"), lax.axis_index("y"), lax.axis_index("c")


def _other_chips(x, y):
    chips = [(1 - x, y), (x, 1 - y), (1 - x, 1 - y)]
    return [(chip, 2 * chip[0] + chip[1]) for chip in chips]


def _gather_weights(pack, name):
    R, C = pack.shape
    half = R // 2

    def body(pack_ref, out_ref, send_sems, recv_sems, local_sem):
        x, y, c = _position()
        me = 2 * x + y
        sibling = (x, y, 1 - c)
        others = _other_chips(x, y)

        def rows(chip, h):
            return out_ref.at[chip, pl.ds(pl.multiple_of(h * half, 16), half), :]

        def copy(k, src, dst, to):
            return pltpu.make_async_remote_copy(src_ref=src, dst_ref=dst, send_sem=send_sems.at[k],
                                                recv_sem=recv_sems.at[k], device_id=to, device_id_type=MESH)

        mine = pltpu.make_async_copy(pack_ref, out_ref.at[me], local_sem)
        mine.start()
        my_half = pack_ref.at[pl.ds(pl.multiple_of(c * half, 16), half), :]
        first = [copy(k, my_half, rows(me, c), (*chip, c)) for k, (chip, _) in enumerate(others)]
        for cp in first:
            cp.start()
        passed = [copy(3 + k, rows(idx, c), rows(idx, c), sibling) for k, (_, idx) in enumerate(others)]
        for k, (_, idx) in enumerate(others):
            copy(k, my_half, rows(idx, c), sibling).wait_recv()
            passed[k].start()
        for k, (_, idx) in enumerate(others):
            copy(3 + k, my_half, rows(idx, 1 - c), sibling).wait_recv()
        for cp in first + passed:
            cp.wait_send()
        mine.wait()

    return _pcall(
        body, name=name, in_specs=[HBM_SPEC], out_specs=HBM_SPEC,
        out_shape=jax.ShapeDtypeStruct((N_CHIPS, R, C), pack.dtype),
        scratch_shapes=[pltpu.SemaphoreType.DMA((6,)), pltpu.SemaphoreType.DMA((6,)), pltpu.SemaphoreType.DMA],
    )(pack)


def _swap_sibling(send, name):
    def body(send_ref, recv_ref, send_sem, recv_sem):
        x, y, c = _position()
        cp = pltpu.make_async_remote_copy(src_ref=send_ref, dst_ref=recv_ref, send_sem=send_sem,
                                          recv_sem=recv_sem, device_id=(x, y, 1 - c), device_id_type=MESH)
        cp.start()
        cp.wait()

    return _pcall(
        body, name=name, in_specs=[HBM_SPEC], out_specs=HBM_SPEC,
        out_shape=jax.ShapeDtypeStruct(send.shape, send.dtype),
        scratch_shapes=[pltpu.SemaphoreType.DMA, pltpu.SemaphoreType.DMA],
    )(send)


def _scatter_chips(parts, name):
    def body(parts_ref, out_ref, send_sems, recv_sems, local_sem):
        x, y, c = _position()
        me = 2 * x + y
        mine = pltpu.make_async_copy(parts_ref.at[me], out_ref.at[me], local_sem)
        mine.start()
        sends = []
        for k, (chip, idx) in enumerate(_other_chips(x, y)):
            cp = pltpu.make_async_remote_copy(src_ref=parts_ref.at[idx], dst_ref=out_ref.at[me],
                                              send_sem=send_sems.at[k], recv_sem=recv_sems.at[k],
                                              device_id=(*chip, c), device_id_type=MESH)
            cp.start()
            sends.append(cp)
        for k, (chip, idx) in enumerate(_other_chips(x, y)):
            pltpu.make_async_remote_copy(src_ref=parts_ref.at[idx], dst_ref=out_ref.at[idx],
                                         send_sem=send_sems.at[k], recv_sem=recv_sems.at[k],
                                         device_id=(*chip, c), device_id_type=MESH).wait_recv()
        for cp in sends:
            cp.wait_send()
        mine.wait()

    return _pcall(
        body, name=name, in_specs=[HBM_SPEC], out_specs=HBM_SPEC,
        out_shape=jax.ShapeDtypeStruct(parts.shape, parts.dtype),
        scratch_shapes=[pltpu.SemaphoreType.DMA((3,)), pltpu.SemaphoreType.DMA((3,)), pltpu.SemaphoreType.DMA],
    )(parts)


def _allreduce_small(v, name):
    R, C = v.shape

    def body(v_ref, o_ref, slots, send_sems, recv_sems):
        x, y, c = _position()
        me = 4 * x + 2 * y + c
        slots[0] = v_ref[...]
        sends = []
        for r in range(1, 8):
            fx, fy, fc = (r >> 2) & 1, (r >> 1) & 1, r & 1
            to = (x ^ fx, y ^ fy, c ^ fc)
            cp = pltpu.make_async_remote_copy(src_ref=v_ref, dst_ref=slots.at[r], send_sem=send_sems.at[r - 1],
                                              recv_sem=recv_sems.at[r - 1], device_id=to, device_id_type=MESH)
            cp.start()
            sends.append(cp)
        for cp in sends:
            cp.wait()
        acc = slots[me]
        for d in range(1, 8):
            acc = acc + slots[d ^ me]
        o_ref[...] = acc

    vmem = pl.BlockSpec(memory_space=pltpu.VMEM)
    return _pcall(
        body, name=name, in_specs=[vmem], out_specs=vmem, out_shape=jax.ShapeDtypeStruct((R, C), F32),
        scratch_shapes=[pltpu.VMEM((8, R, C), F32), pltpu.SemaphoreType.DMA((7,)), pltpu.SemaphoreType.DMA((7,))],
    )(v)


def _add_pair(a, b, name):
    N, R, C = a.shape
    tr = _pick(R, 512, 16)

    def body(a_ref, b_ref, o_ref):
        o_ref[...] = (a_ref[...].astype(F32) + b_ref[...].astype(F32)).astype(BF16)

    spec = pl.BlockSpec((None, tr, C), lambda n, i: (n, i, 0))
    return _pcall(body, name=name, grid=(N, R // tr), in_specs=[spec, spec], out_specs=spec,
                  out_shape=jax.ShapeDtypeStruct((N, R, C), BF16))(a, b)


def _sum_chips(parts, name):
    N, R, C = parts.shape
    tr = _pick(R, 512, 16)

    def body(p_ref, o_ref):
        acc = p_ref[0].astype(F32)
        for n in range(1, N):
            acc = acc + p_ref[n].astype(F32)
        o_ref[...] = acc

    return _pcall(body, name=name, grid=(R // tr,), in_specs=[pl.BlockSpec((N, tr, C), lambda i: (0, i, 0))],
                  out_specs=pl.BlockSpec((tr, C), lambda i: (i, 0)),
                  out_shape=jax.ShapeDtypeStruct((R, C), F32))(parts)


BIG = ("w_ffn1_in", "w_ffn1_out", "w_in", "w_br_sb", "w_br_ch", "w_br_fox", "w_out", "w_ffn2_in", "w_ffn2_out")
ROW_SHARDED = ("w_ffn1_out", "w_out", "w_ffn2_out")
SMALL = ("g_ffn1", "g_mix", "b_in", "rel_bias", "g_ffn2", "g_final")
SHARD_SHAPES = {
    "w_ffn1_in": (D_MODEL, 2 * D_FF // N_CHIPS), "w_ffn1_out": (D_FF // N_CHIPS, D_MODEL),
    "w_in": (D_MODEL, IN_WIDTH // N_CHIPS), "w_br_sb": (W_SB, D_MODEL // N_CHIPS),
    "w_br_ch": (W_CH, D_MODEL // N_CHIPS), "w_br_fox": (W_FOX, D_MODEL // N_CHIPS),
    "w_out": (D_MODEL // N_CHIPS, D_MODEL), "w_ffn2_in": (D_MODEL, 2 * D_FF // N_CHIPS),
    "w_ffn2_out": (D_FF // N_CHIPS, D_MODEL),
}
PACK_ROWS = sum(DEPTH * s[0] * s[1] // PACK_COLS for s in SHARD_SHAPES.values())
PACK_ROWS_PAD = -(-PACK_ROWS // 1024) * 1024


def _pack_shards(shards, dtype):
    flat = [shards[n].astype(dtype).reshape(-1, PACK_COLS) for n in BIG]
    flat.append(jnp.zeros((PACK_ROWS_PAD - PACK_ROWS, PACK_COLS), dtype))
    return jnp.concatenate(flat, axis=0)


def _unpack_shards(pack):
    out, off = {}, 0
    for n in BIG:
        r, c = SHARD_SHAPES[n]
        rows = DEPTH * r * c // PACK_COLS
        out[n] = pack[..., off:off + rows, :].reshape(pack.shape[:-2] + (DEPTH, r, c))
        off += rows
    return out


def _full_weight(gathered, name, layer):
    return jnp.concatenate([gathered[name][j, layer] for j in range(N_CHIPS)],
                           axis=0 if name in ROW_SHARDED else 1)


def _split_grad(name, g):
    r, c = SHARD_SHAPES[name]
    if name in ROW_SHARDED:
        return g.reshape(N_CHIPS, r, c)
    return jnp.transpose(g.reshape(g.shape[0], N_CHIPS, c), (1, 0, 2))


SMALL_SIZES = {"g_ffn1": DEPTH * D_MODEL, "g_mix": DEPTH * D_MODEL, "b_in": DEPTH * IN_WIDTH,
               "rel_bias": DEPTH * N_REL * H_CH, "g_ffn2": DEPTH * D_MODEL, "g_final": D_MODEL}
SMALL_TOTAL = sum(SMALL_SIZES.values()) + 1
SMALL_ROWS = -(-SMALL_TOTAL // (8 * 128)) * 8


def _pack_small(vals, extra):
    flat = [vals[n].reshape(-1) for n in SMALL] + [extra.reshape(-1)]
    flat.append(jnp.zeros((SMALL_ROWS * 128 - SMALL_TOTAL,), F32))
    return jnp.concatenate(flat).reshape(SMALL_ROWS, 128)


def _unpack_small(pack, shapes):
    flat, out, off = pack.reshape(-1), {}, 0
    for n in SMALL:
        out[n] = flat[off:off + SMALL_SIZES[n]].reshape(shapes[n])
        off += SMALL_SIZES[n]
    return out, flat[off]


def _to_heads(t, n_heads):
    return jnp.transpose(t.reshape(t.shape[0], n_heads, HEAD_DIM), (1, 0, 2)).astype(BF16)


def _from_heads(t):
    return jnp.transpose(t, (1, 0, 2)).reshape(t.shape[1], t.shape[0] * HEAD_DIM)


def _pad_keys(t):
    return jnp.pad(t, ((0, 0), (BAND_PAD, 0), (0, 0)))


DIAGONALS = CHUNK + BAND - 1


def _band_bias(table):
    n_far = (LEFT_CHUNKS + 1) * CHUNK + CHUNK - MAX_REL
    near = table[N_REL - 1 - (DIAGONALS - n_far):N_REL - 1][::-1]
    diag = jnp.concatenate([jnp.broadcast_to(table[N_REL - 1], (n_far, H_CH)), near], axis=0).T
    diag = jnp.pad(diag, ((0, 0), (0, 1)))
    skew = jnp.tile(diag, (1, CHUNK))[:, :CHUNK * DIAGONALS].reshape(H_CH, CHUNK, DIAGONALS)
    return skew[:, :, CHUNK - 1:]


def _pad_w_in(w):
    qkv, f, gates = w[:, :QKV_WIDTH], w[:, QKV_WIDTH:QKV_WIDTH + H_FOX], w[:, QKV_WIDTH + H_FOX:]
    return jnp.concatenate([qkv, gates, f, jnp.zeros((w.shape[0], F_PAD - H_FOX), w.dtype)], axis=1)


def _unpad_w_in(w):
    return jnp.concatenate([w[:, :QKV_WIDTH], w[:, QKV_WIDTH + GATE_WIDTH:QKV_WIDTH + GATE_WIDTH + H_FOX],
                            w[:, QKV_WIDTH:QKV_WIDTH + GATE_WIDTH]], axis=1)


QKV_SPLITS = np.cumsum([0, W_SB, W_SB, W_SB, W_CH, W_CH, W_CH, W_FOX, W_FOX, W_FOX])


def _ffn_fwd(x, g, w_in, w_out, tag):
    h = _rmsnorm_fwd(x, g, f"{tag}_norm")
    gu = _mm(h, w_in, mode="nn", name=f"{tag}_in")
    a = _swiglu_fwd(gu, f"{tag}_act")
    y = _mm(a, w_out, mode="nn", name=f"{tag}_out", alpha=0.5, res=x)
    return y, (h, gu, a)


def _ffn_bwd(x, g, w_in, w_out, saved, dy, tag):
    h, gu, a = saved
    da = _mm(dy, w_out, mode="nt", name=f"{tag}_da", alpha=0.5, tn_cap=1408)
    dw_out = _mm(a, dy, mode="tn", name=f"{tag}_dwout", alpha=0.5, tm_cap=512)
    dgu = _swiglu_bwd(gu, da, f"{tag}_dact")
    dw_in = _mm(h, dgu, mode="tn", name=f"{tag}_dwin", tm_cap=512)
    dh = _mm(dgu, w_in, mode="nt", name=f"{tag}_dh", tk_cap=1408)
    dx, dg = _rmsnorm_bwd(x, g, dh, dy, f"{tag}_dnorm")
    return dx, dg, dw_in, dw_out


def _mixer_fwd(x, lw, tag):
    T = x.shape[0]
    h = _rmsnorm_fwd(x, lw["g_mix"], f"{tag}_norm")
    p = _mm(h, lw["w_in"], mode="nn", name=f"{tag}_proj", bias=lw["b_in"], tn_cap=896)
    parts = [p[:, QKV_SPLITS[i]:QKV_SPLITS[i + 1]] for i in range(9)]
    qa, ka, va = (_to_heads(t, H_SB) for t in parts[0:3])
    qb, kb, vb = (_to_heads(t, H_CH) for t in parts[3:6])
    qc, kc, vc = (_to_heads(t, H_FOX) for t in parts[6:9])
    flog = jnp.pad(p[:, QKV_WIDTH + GATE_WIDTH:QKV_WIDTH + GATE_WIDTH + H_FOX].T, ((0, 8 - H_FOX), (0, 0)))
    fcum = _forget_cumsum(flog, f"{tag}_fcum")[:H_FOX]
    fq, fk = fcum.reshape(H_FOX, T, 1), fcum.reshape(H_FOX, T // QB, QB)
    kbp, vbp = _pad_keys(kb), _pad_keys(vb)
    oa = _sb_fwd(qa, ka, va, f"{tag}_sb")
    ob = _chunk_fwd(qb, kbp, vbp, lw["bias"], f"{tag}_ch")
    oc, lse = _fox_fwd(qc, kc, vc, fq, fk, f"{tag}_fox")
    oam, obm, ocm = _from_heads(oa), _from_heads(ob), _from_heads(oc)
    ya = _mm(oam, lw["w_br_sb"], mode="nn", name=f"{tag}_bra", tn_cap=1024)
    yb = _mm(obm, lw["w_br_ch"], mode="nn", name=f"{tag}_brb", tn_cap=1024)
    yc = _mm(ocm, lw["w_br_fox"], mode="nn", name=f"{tag}_brc", tn_cap=1024)
    merged = _gate_fwd(p, ya, yb, yc, f"{tag}_gate")
    y = _mm(merged, lw["w_out"], mode="nn", name=f"{tag}_out", res=x)
    saved = (h, p, (qa, ka, va), (qb, kbp, vbp), (qc, kc, vc), flog, fq, fk, lse, (oam, obm, ocm),
             (ya, yb, yc), merged)
    return y, saved


def _mixer_bwd(x, lw, saved, dy, tag):
    (h, p, (qa, ka, va), (qb, kbp, vbp), (qc, kc, vc), flog, fq, fk, lse, (oam, obm, ocm),
     (ya, yb, yc), merged) = saved
    T = x.shape[0]
    grads = {}
    dmerged = _mm(dy, lw["w_out"], mode="nt", name=f"{tag}_dmerged", tn_cap=1024)
    grads["w_out"] = _mm(merged, dy, mode="tn", name=f"{tag}_dwout", tm_cap=512)
    dya, dyb, dyc, dgate = _gate_bwd(p, ya, yb, yc, dmerged, f"{tag}_dgate")
    doa = _mm(dya, lw["w_br_sb"], mode="nt", name=f"{tag}_doa")
    dob = _mm(dyb, lw["w_br_ch"], mode="nt", name=f"{tag}_dob")
    doc = _mm(dyc, lw["w_br_fox"], mode="nt", name=f"{tag}_doc")
    grads["w_br_sb"] = _mm(oam, dya, mode="tn", name=f"{tag}_dwbra", tm_cap=512)
    grads["w_br_ch"] = _mm(obm, dyb, mode="tn", name=f"{tag}_dwbrb", tm_cap=512)
    grads["w_br_fox"] = _mm(ocm, dyc, mode="tn", name=f"{tag}_dwbrc", tm_cap=512)
    dqa, dka, dva = _sb_bwd(qa, ka, va, _to_heads(doa, H_SB), f"{tag}_dsb")
    dqb, dkbp, dvbp, dbias = _chunk_bwd(qb, kbp, vbp, lw["bias"], _to_heads(dob, H_CH), f"{tag}_dch")
    dqc, dkc, dvc, dfk = _fox_bwd(qc, kc, vc, fq, fk, lse, _to_heads(doc, H_FOX), f"{tag}_dfox")
    dflog = _forget_cumsum_bwd(flog, jnp.pad(dfk.reshape(H_FOX, T), ((0, 8 - H_FOX), (0, 0))), f"{tag}_dfcum")
    dqkv = [_from_heads(t) for t in (dqa, dka, dva, dqb, dkbp[:, BAND_PAD:], dvbp[:, BAND_PAD:], dqc, dkc, dvc)]
    dp = jnp.concatenate(dqkv + [dgate, jnp.pad(dflog[:H_FOX].T, ((0, 0), (0, F_PAD - H_FOX)))], axis=1)
    grads["b_in"] = _colsum(dp, f"{tag}_dbin")
    dpb = dp.astype(BF16)
    grads["w_in"] = _mm(h, dpb, mode="tn", name=f"{tag}_dwin", tm_cap=512, tn_cap=896)
    dh = _mm(dpb, lw["w_in"], mode="nt", name=f"{tag}_dh", tk_cap=896)
    dx, grads["g_mix"] = _rmsnorm_bwd(x, lw["g_mix"], dh, dy, f"{tag}_dnorm")
    grads["rel_bias"] = lw["bias_vjp"](dbias)[0]
    return dx, grads


def kernel(x, g_ffn1, w_ffn1_in, w_ffn1_out, g_mix, w_in, b_in, rel_bias, w_br_sb, w_br_ch, w_br_fox, w_out, g_ffn2, w_ffn2_in, w_ffn2_out, g_final, loss_target, m_g_ffn1, m_w_ffn1_in, m_w_ffn1_out, m_g_mix, m_w_in, m_b_in, m_rel_bias, m_w_br_sb, m_w_br_ch, m_w_br_fox, m_w_out, m_g_ffn2, m_w_ffn2_in, m_w_ffn2_out, m_g_final, v_g_ffn1, v_w_ffn1_in, v_w_ffn1_out, v_g_mix, v_w_in, v_b_in, v_rel_bias, v_w_br_sb, v_w_br_ch, v_w_br_fox, v_w_out, v_g_ffn2, v_w_ffn2_in, v_w_ffn2_out, v_g_final):
    names = ("g_ffn1", "w_ffn1_in", "w_ffn1_out", "g_mix", "w_in", "b_in", "rel_bias", "w_br_sb", "w_br_ch",
             "w_br_fox", "w_out", "g_ffn2", "w_ffn2_in", "w_ffn2_out", "g_final")
    w = dict(zip(names, (g_ffn1, w_ffn1_in, w_ffn1_out, g_mix, w_in, b_in, rel_bias, w_br_sb, w_br_ch,
                         w_br_fox, w_out, g_ffn2, w_ffn2_in, w_ffn2_out, g_final)))
    m = dict(zip(names, (m_g_ffn1, m_w_ffn1_in, m_w_ffn1_out, m_g_mix, m_w_in, m_b_in, m_rel_bias, m_w_br_sb,
                         m_w_br_ch, m_w_br_fox, m_w_out, m_g_ffn2, m_w_ffn2_in, m_w_ffn2_out, m_g_final)))
    v = dict(zip(names, (v_g_ffn1, v_w_ffn1_in, v_w_ffn1_out, v_g_mix, v_w_in, v_b_in, v_rel_bias, v_w_br_sb,
                         v_w_br_ch, v_w_br_fox, v_w_out, v_g_ffn2, v_w_ffn2_in, v_w_ffn2_out, v_g_final)))
    c = lax.axis_index("c")
    xs = x[0]
    tgt = loss_target[0]

    gathered = _unpack_shards(_gather_weights(_pack_shards(w, BF16), "gather_weights"))
    layers = []
    for l in range(DEPTH):
        lw = {n: _full_weight(gathered, n, l) for n in BIG}
        lw["w_in"] = _pad_w_in(lw["w_in"])
        lw["b_in"] = _pad_w_in(w["b_in"][l][None, :])
        lw["bias"], lw["bias_vjp"] = jax.vjp(_band_bias, w["rel_bias"][l])
        for n in ("g_ffn1", "g_mix", "g_ffn2"):
            lw[n] = w[n][l][None, :]
        layers.append(lw)

    saved = []
    act = xs
    for l, lw in enumerate(layers):
        x0 = act
        x1, s1 = _ffn_fwd(x0, lw["g_ffn1"], lw["w_ffn1_in"], lw["w_ffn1_out"], f"l{l}_ffn1")
        x2, s2 = _mixer_fwd(x1, lw, f"l{l}_mix")
        act, s3 = _ffn_fwd(x2, lw["g_ffn2"], lw["w_ffn2_in"], lw["w_ffn2_out"], f"l{l}_ffn2")
        saved.append((x0, s1, x1, s2, x2, s3))

    dact, dg_final, loss_row = _final_loss(act, w["g_final"][None, :], tgt, "final_loss")

    big_grads = {n: [None] * DEPTH for n in BIG}
    small_grads = {n: [None] * DEPTH for n in ("g_ffn1", "g_mix", "b_in", "rel_bias", "g_ffn2")}
    for l in reversed(range(DEPTH)):
        lw = layers[l]
        x0, s1, x1, s2, x2, s3 = saved[l]
        dx2, small_grads["g_ffn2"][l], big_grads["w_ffn2_in"][l], big_grads["w_ffn2_out"][l] = _ffn_bwd(
            x2, lw["g_ffn2"], lw["w_ffn2_in"], lw["w_ffn2_out"], s3, dact, f"l{l}_ffn2")
        dx1, mg = _mixer_bwd(x1, lw, s2, dx2, f"l{l}_mix")
        for n in ("w_br_sb", "w_br_ch", "w_br_fox", "w_out"):
            big_grads[n][l] = mg[n]
        big_grads["w_in"][l] = _unpad_w_in(mg["w_in"])
        small_grads["b_in"][l] = _unpad_w_in(mg["b_in"])[0]
        small_grads["g_mix"][l] = mg["g_mix"][0]
        small_grads["rel_bias"][l] = mg["rel_bias"]
        dact, dg1, big_grads["w_ffn1_in"][l], big_grads["w_ffn1_out"][l] = _ffn_bwd(
            x0, lw["g_ffn1"], lw["w_ffn1_in"], lw["w_ffn1_out"], s1, dx1, f"l{l}_ffn1")
        small_grads["g_ffn1"][l] = dg1[0]
        small_grads["g_ffn2"][l] = small_grads["g_ffn2"][l][0]
    grad_x = dact[None]

    small = {n: jnp.stack(small_grads[n]) for n in small_grads}
    small["g_final"] = dg_final[0]
    small_sum, loss = _unpack_small(_allreduce_small(_pack_small(small, loss_row[0, :1]), "allreduce_small"),
                                    {n: w[n].shape for n in SMALL})

    by_chip = {n: jnp.stack([_split_grad(n, big_grads[n][l]) for l in range(DEPTH)], axis=1) for n in BIG}
    gpack = jnp.stack([_pack_shards({n: by_chip[n][j] for n in BIG}, BF16) for j in range(N_CHIPS)])
    half = PACK_ROWS_PAD // 2
    keep = lax.dynamic_slice_in_dim(gpack, c * half, half, axis=1)
    give = lax.dynamic_slice_in_dim(gpack, (1 - c) * half, half, axis=1)
    pair = _add_pair(keep, _swap_sibling(give, "grad_swap"), "grad_pair_sum")
    mine = _sum_chips(_scatter_chips(pair, "grad_scatter"), "grad_chip_sum")
    theirs = _swap_sibling(mine, "grad_share")
    lo = jnp.where(c == 0, mine, theirs)
    hi = jnp.where(c == 0, theirs, mine)
    grads = _unpack_shards(jnp.concatenate([lo, hi], axis=0))
    grads.update(small_sum)

    delta, new_m, new_v = {}, {}, {}
    for n in BIG:
        shape = w[n].shape
        flat = lambda t: t.reshape(-1, shape[-1])
        d, nm, nv = _adamw(flat(w[n]), flat(grads[n]), flat(m[n]), flat(v[n]), f"adamw_{n}")
        delta[n], new_m[n], new_v[n] = d.reshape(shape), nm.reshape(shape), nv.reshape(shape)
    zero = jnp.zeros((1,), F32)
    sd, sm, sv = _adamw(_pack_small(w, zero), _pack_small(grads, zero), _pack_small(m, zero), _pack_small(v, zero),
                        "adamw_small")
    shapes = {n: w[n].shape for n in SMALL}
    for dst, src in ((delta, sd), (new_m, sm), (new_v, sv)):
        dst.update(_unpack_small(src, shapes)[0])

    return (loss, grad_x, *[grads[n] for n in names], *[delta[n] for n in names],
            *[new_m[n] for n in names], *[new_v[n] for n in names])
```

```python
import functools

import numpy as np
import jax
import jax.numpy as jnp
from jax import lax
from jax.experimental import pallas as pl
from jax.experimental.pallas import tpu as pltpu

F32 = jnp.float32
BF16 = jnp.bfloat16

D_MODEL = 1024
DEPTH = 2
CHUNK = 64
HEAD_DIM = 64
H_SB, H_CH, H_FOX = 4, 8, 4
W_SB, W_CH, W_FOX = H_SB * HEAD_DIM, H_CH * HEAD_DIM, H_FOX * HEAD_DIM
LEFT_CHUNKS = 8
MAX_REL = 128
N_REL = 2 * MAX_REL + 1
D_FF = 2816
QKV_WIDTH = 3 * (W_SB + W_CH + W_FOX)
GATE_WIDTH = 3 * D_MODEL
IN_WIDTH = QKV_WIDTH + H_FOX + GATE_WIDTH
F_PAD = 128
P_WIDTH = QKV_WIDTH + GATE_WIDTH + F_PAD
RMS_EPS = 1e-6
NEG = -1e30
SCALE = HEAD_DIM ** -0.5
QB = 256
BAND = (LEFT_CHUNKS + 2) * CHUNK
BAND_PAD = BAND - CHUNK

ADAM_LR, ADAM_B1, ADAM_B2, ADAM_EPS, ADAM_WD, ADAM_STEP = 0.001, 0.9, 0.999, 1e-08, 0.01, 10

N_CHIPS = 4
PACK_COLS = 1024
VMEM_BUDGET = 52 * 1024 * 1024

NT = (((1,), (1,)), ((), ()))
TN = (((0,), (0,)), ((), ()))
NN = (((1,), (0,)), ((), ()))
MESH = pl.DeviceIdType.MESH


def _pcall(body, **kw):
    return pl.pallas_call(body, **kw)


def _pick(n, cap, mult=128):
    best = None
    d = mult
    while d <= min(n, cap):
        if n % d == 0:
            best = d
        d += mult
    return n if best is None else best


def _nbytes(shape, dtype):
    return int(np.prod(shape)) * jnp.dtype(dtype).itemsize


def _mm(a, b, *, mode, name, out_dtype=F32, alpha=1.0, bias=None, res=None, tm_cap=1024, tn_cap=512,
        tk_cap=2816, gathered=None, out_by_chip=False):
    if mode == "tn":
        K, M = a.shape
    else:
        M, K = a.shape
    dn = {"nn": NN, "nt": NT, "tn": TN}[mode]
    b_rows = None
    if gathered is None:
        N = b.shape[0] if mode == "nt" else b.shape[1]
        tn = N // N_CHIPS if out_by_chip else _pick(N, tn_cap)
        tk = K if K <= tk_cap else _pick(K, tk_cap)
        b_spec = (pl.BlockSpec((tn, tk), lambda i, j, k: (j, k)) if mode == "nt"
                  else pl.BlockSpec((tk, tn), lambda i, j, k: (k, j)))
    else:
        layer, sharding = gathered
        r, c = b.shape[2:]
        rows, cols = (r, N_CHIPS * c) if sharding == "col" else (N_CHIPS * r, c)
        N = rows if mode == "nt" else cols
        assert K == (cols if mode == "nt" else rows), (name, K, rows, cols)
        if sharding == "col" and mode == "nn":
            tn, tk = c, (r if r <= tk_cap else _pick(r, tk_cap))
            b_spec = pl.BlockSpec((None, None, tk, c), lambda i, j, k: (layer, j, k, 0))
        elif sharding == "col":
            tn, tk = _pick(r, tn_cap), c
            b_spec = pl.BlockSpec((None, None, tn, c), lambda i, j, k: (layer, k, j, 0))
        elif mode == "nn":
            tn, tk, b_rows = _pick(c, tn_cap), K, N_CHIPS
            b_spec = pl.BlockSpec((None, N_CHIPS, r, tn), lambda i, j, k: (layer, 0, 0, j))
        else:
            tn, tk, b_rows = 2 * r, K, 2
            b_spec = pl.BlockSpec((None, 2, r, c), lambda i, j, k: (layer, j, 0, 0))
    tm = _pick(M, tm_cap)
    nk = K // tk

    a_spec = (pl.BlockSpec((tk, tm), lambda i, j, k: (k, i)) if mode == "tn"
              else pl.BlockSpec((tm, tk), lambda i, j, k: (i, k)))
    in_specs = [a_spec, b_spec]
    args = [a, b]
    if bias is not None:
        in_specs.append(pl.BlockSpec((1, tn), lambda i, j, k: (0, j)))
        args.append(bias)
    if res is not None:
        in_specs.append(pl.BlockSpec((tm, tn), lambda i, j, k: (i, j)))
        args.append(res)

    est = 2 * (_nbytes((tm, tk), a.dtype) + _nbytes((tk, tn), b.dtype) + _nbytes((tm, tn), out_dtype))
    est += _nbytes((tm, tn), F32) * (3 if res is not None else 1)
    assert est < VMEM_BUDGET, (name, est)

    def body(*refs):
        a_ref, b_ref = refs[0], refs[1]
        pos = 2
        bias_ref = res_ref = None
        if bias is not None:
            bias_ref = refs[pos]
            pos += 1
        if res is not None:
            res_ref = refs[pos]
            pos += 1
        o_ref = refs[pos]
        acc_ref = refs[pos + 1] if nk > 1 else None

        bv = b_ref[...]
        if b_rows is not None:
            bv = bv.reshape(b_rows * bv.shape[1], bv.shape[2])
        part = lax.dot_general(a_ref[...].astype(BF16), bv.astype(BF16), dn, preferred_element_type=F32)

        def finish(acc):
            if alpha != 1.0:
                acc = acc * alpha
            if bias_ref is not None:
                acc = acc + bias_ref[...]
            if res_ref is not None:
                acc = acc + res_ref[...]
            o_ref[...] = acc.astype(out_dtype)

        if nk == 1:
            finish(part)
        else:
            k = pl.program_id(2)

            @pl.when(k == 0)
            def _():
                acc_ref[...] = part

            @pl.when(k > 0)
            def _():
                acc_ref[...] += part

            @pl.when(k == nk - 1)
            def _():
                finish(acc_ref[...])

    if out_by_chip:
        out_spec = pl.BlockSpec((None, tm, tn), lambda i, j, k: (j, i, 0))
        out_shape = jax.ShapeDtypeStruct((N_CHIPS, M, tn), out_dtype)
    else:
        out_spec = pl.BlockSpec((tm, tn), lambda i, j, k: (i, j))
        out_shape = jax.ShapeDtypeStruct((M, N), out_dtype)
    return _pcall(
        body, name=name, grid=(M // tm, N // tn, nk), in_specs=in_specs, out_specs=out_spec, out_shape=out_shape,
        scratch_shapes=[pltpu.VMEM((tm, tn), F32)] if nk > 1 else [],
        compiler_params=pltpu.CompilerParams(dimension_semantics=("parallel", "parallel", "arbitrary")),
    )(*args)


def _rmsnorm_fwd(x, g, name):
    T, Dm = x.shape
    tm = _pick(T, 256, 8)

    def body(x_ref, g_ref, h_ref):
        xv = x_ref[...]
        rstd = lax.rsqrt(jnp.mean(xv * xv, axis=-1, keepdims=True) + RMS_EPS)
        h_ref[...] = (xv * rstd * g_ref[...]).astype(BF16)

    return _pcall(
        body, name=name, grid=(T // tm,),
        in_specs=[pl.BlockSpec((tm, Dm), lambda i: (i, 0)), pl.BlockSpec((1, Dm), lambda i: (0, 0))],
        out_specs=pl.BlockSpec((tm, Dm), lambda i: (i, 0)),
        out_shape=jax.ShapeDtypeStruct((T, Dm), BF16),
    )(x, g)


def _rmsnorm_bwd(x, g, dh, dres, name):
    T, Dm = x.shape
    tm = _pick(T, 256, 8)

    def body(x_ref, g_ref, dh_ref, dres_ref, dx_ref, dg_ref):
        xv = x_ref[...]
        rstd = lax.rsqrt(jnp.mean(xv * xv, axis=-1, keepdims=True) + RMS_EPS)
        xhat = xv * rstd
        dhv = dh_ref[...]
        dyg = dhv * g_ref[...]
        dx_ref[...] = dres_ref[...] + rstd * (dyg - xhat * jnp.mean(dyg * xhat, axis=-1, keepdims=True))
        part = jnp.sum(dhv * xhat, axis=0, keepdims=True)

        @pl.when(pl.program_id(0) == 0)
        def _():
            dg_ref[...] = part

        @pl.when(pl.program_id(0) > 0)
        def _():
            dg_ref[...] += part

    row = pl.BlockSpec((tm, Dm), lambda i: (i, 0))
    vec = pl.BlockSpec((1, Dm), lambda i: (0, 0))
    return _pcall(
        body, name=name, grid=(T // tm,), in_specs=[row, vec, row, row], out_specs=[row, vec],
        out_shape=[jax.ShapeDtypeStruct((T, Dm), F32), jax.ShapeDtypeStruct((1, Dm), F32)],
        compiler_params=pltpu.CompilerParams(dimension_semantics=("arbitrary",)),
    )(x, g, dh, dres)


def _final_loss(x, g, tgt, name):
    T, Dm = x.shape
    tm = _pick(T, 256, 8)

    def body(x_ref, g_ref, t_ref, dx_ref, dg_ref, loss_ref):
        xv = x_ref[...]
        gv = g_ref[...]
        rstd = lax.rsqrt(jnp.mean(xv * xv, axis=-1, keepdims=True) + RMS_EPS)
        xhat = xv * rstd
        err = xhat * gv - t_ref[...]
        part_loss = 0.5 * jnp.sum(jnp.mean(err * err, axis=-1, keepdims=True), axis=0, keepdims=True)
        dy = err * (1.0 / Dm)
        dyg = dy * gv
        dx_ref[...] = rstd * (dyg - xhat * jnp.mean(dyg * xhat, axis=-1, keepdims=True))
        part_g = jnp.sum(dy * xhat, axis=0, keepdims=True)
        part_l = jnp.broadcast_to(part_loss, (1, 128))

        @pl.when(pl.program_id(0) == 0)
        def _():
            dg_ref[...] = part_g
            loss_ref[...] = part_l

        @pl.when(pl.program_id(0) > 0)
        def _():
            dg_ref[...] += part_g
            loss_ref[...] += part_l

    row = pl.BlockSpec((tm, Dm), lambda i: (i, 0))
    vec = pl.BlockSpec((1, Dm), lambda i: (0, 0))
    return _pcall(
        body, name=name, grid=(T // tm,), in_specs=[row, vec, row],
        out_specs=[row, vec, pl.BlockSpec((1, 128), lambda i: (0, 0))],
        out_shape=[jax.ShapeDtypeStruct((T, Dm), F32), jax.ShapeDtypeStruct((1, Dm), F32),
                   jax.ShapeDtypeStruct((1, 128), F32)],
        compiler_params=pltpu.CompilerParams(dimension_semantics=("arbitrary",)),
    )(x, g, tgt)


def _sigmoid(z):
    return 1.0 / (1.0 + jnp.exp(-z))


def _swiglu_fwd(gu, name):
    T = gu.shape[0]
    tm = _pick(T, 128, 8)

    def body(gu_ref, a_ref):
        gv = gu_ref[:, :D_FF]
        uv = gu_ref[:, D_FF:]
        a_ref[...] = (gv * _sigmoid(gv) * uv).astype(BF16)

    return _pcall(
        body, name=name, grid=(T // tm,), in_specs=[pl.BlockSpec((tm, 2 * D_FF), lambda i: (i, 0))],
        out_specs=pl.BlockSpec((tm, D_FF), lambda i: (i, 0)),
        out_shape=jax.ShapeDtypeStruct((T, D_FF), BF16),
    )(gu)


def _swiglu_bwd(gu, da, name):
    T = gu.shape[0]
    tm = _pick(T, 128, 8)

    def body(gu_ref, da_ref, d_ref):
        gv = gu_ref[:, :D_FF]
        uv = gu_ref[:, D_FF:]
        dav = da_ref[...]
        sg = _sigmoid(gv)
        d_ref[:, :D_FF] = (dav * uv * (sg + gv * sg * (1.0 - sg))).astype(BF16)
        d_ref[:, D_FF:] = (dav * gv * sg).astype(BF16)

    return _pcall(
        body, name=name, grid=(T // tm,),
        in_specs=[pl.BlockSpec((tm, 2 * D_FF), lambda i: (i, 0)), pl.BlockSpec((tm, D_FF), lambda i: (i, 0))],
        out_specs=pl.BlockSpec((tm, 2 * D_FF), lambda i: (i, 0)),
        out_shape=jax.ShapeDtypeStruct((T, 2 * D_FF), BF16),
    )(gu, da)


GATE_BLOCK0 = QKV_WIDTH // D_MODEL


def _gate_fwd(p, ya, yb, yc, name):
    T = p.shape[0]
    tm = _pick(T, 256, 8)

    def body(ga_ref, gb_ref, gc_ref, ya_ref, yb_ref, yc_ref, o_ref):
        o_ref[...] = (_sigmoid(ga_ref[...]) * ya_ref[...] + _sigmoid(gb_ref[...]) * yb_ref[...]
                      + _sigmoid(gc_ref[...]) * yc_ref[...]).astype(BF16)

    gate = [pl.BlockSpec((tm, D_MODEL), functools.partial(lambda i, kk: (i, GATE_BLOCK0 + kk), kk=kk))
            for kk in range(3)]
    row = pl.BlockSpec((tm, D_MODEL), lambda i: (i, 0))
    return _pcall(
        body, name=name, grid=(T // tm,), in_specs=gate + [row, row, row], out_specs=row,
        out_shape=jax.ShapeDtypeStruct((T, D_MODEL), BF16),
    )(p, p, p, ya, yb, yc)


def _gate_bwd(p, ya, yb, yc, dm, name):
    T = p.shape[0]
    tm = _pick(T, 256, 8)

    def body(ga_ref, gb_ref, gc_ref, ya_ref, yb_ref, yc_ref, dm_ref, da_ref, db_ref, dc_ref, dg_ref):
        dmv = dm_ref[...]
        for kk, (g_ref, y_ref, d_ref) in enumerate(((ga_ref, ya_ref, da_ref), (gb_ref, yb_ref, db_ref),
                                                    (gc_ref, yc_ref, dc_ref))):
            s = _sigmoid(g_ref[...])
            d_ref[...] = (s * dmv).astype(BF16)
            dg_ref[:, kk * D_MODEL:(kk + 1) * D_MODEL] = dmv * y_ref[...] * s * (1.0 - s)

    gate = [pl.BlockSpec((tm, D_MODEL), functools.partial(lambda i, kk: (i, GATE_BLOCK0 + kk), kk=kk))
            for kk in range(3)]
    row = pl.BlockSpec((tm, D_MODEL), lambda i: (i, 0))
    return _pcall(
        body, name=name, grid=(T // tm,), in_specs=gate + [row, row, row, row],
        out_specs=[row, row, row, pl.BlockSpec((tm, GATE_WIDTH), lambda i: (i, 0))],
        out_shape=[jax.ShapeDtypeStruct((T, D_MODEL), BF16)] * 3 + [jax.ShapeDtypeStruct((T, GATE_WIDTH), F32)],
    )(p, p, p, ya, yb, yc, dm)


def _colsum(a, name):
    T, N = a.shape
    tm = _pick(T, 256, 8)

    def body(a_ref, o_ref):
        part = jnp.sum(a_ref[...], axis=0, keepdims=True)

        @pl.when(pl.program_id(0) == 0)
        def _():
            o_ref[...] = part

        @pl.when(pl.program_id(0) > 0)
        def _():
            o_ref[...] += part

    return _pcall(
        body, name=name, grid=(T // tm,), in_specs=[pl.BlockSpec((tm, N), lambda i: (i, 0))],
        out_specs=pl.BlockSpec((1, N), lambda i: (0, 0)), out_shape=jax.ShapeDtypeStruct((1, N), F32),
        compiler_params=pltpu.CompilerParams(dimension_semantics=("arbitrary",)),
    )(a)


def _adamw(w, g, m, v, name):
    R, C = w.shape
    tr = 256 if R % 256 == 0 else R
    c1 = 1.0 / (1.0 - ADAM_B1 ** ADAM_STEP)
    c2 = 1.0 / (1.0 - ADAM_B2 ** ADAM_STEP)

    def body(w_ref, g_ref, m_ref, v_ref, d_ref, nm_ref, nv_ref):
        gv = g_ref[...]
        mn = ADAM_B1 * m_ref[...] + (1.0 - ADAM_B1) * gv
        vn = ADAM_B2 * v_ref[...] + (1.0 - ADAM_B2) * (gv * gv)
        nm_ref[...] = mn
        nv_ref[...] = vn
        d_ref[...] = -ADAM_LR * ((mn * c1) / (jnp.sqrt(vn * c2) + ADAM_EPS) + ADAM_WD * w_ref[...])

    spec = pl.BlockSpec((tr, C), lambda i: (i, 0))
    return _pcall(
        body, name=name, grid=(R // tr,), in_specs=[spec] * 4, out_specs=[spec] * 3,
        out_shape=[jax.ShapeDtypeStruct((R, C), F32)] * 3,
    )(w, g, m, v)


def _log_sigmoid(z):
    return jnp.minimum(z, 0.0) - jnp.log(1.0 + jnp.exp(-jnp.abs(z)))


def _dot3(x, m01):
    x1 = x.astype(BF16)
    r1 = x - x1.astype(F32)
    x2 = r1.astype(BF16)
    x3 = (r1 - x2.astype(F32)).astype(BF16)
    n = x.shape[0]
    if n % 16:
        return (jnp.dot(x1, m01, preferred_element_type=F32) + jnp.dot(x2, m01, preferred_element_type=F32)
                + jnp.dot(x3, m01, preferred_element_type=F32))
    y = jnp.dot(jnp.concatenate([x1, x2, x3], axis=0), m01, preferred_element_type=F32)
    return y[:n] + y[n:2 * n] + y[2 * n:]


def _dot_nt(a, b):
    return lax.dot_general(a, b, NT, preferred_element_type=F32)


def _dot_tn(a, b):
    return lax.dot_general(a, b, TN, preferred_element_type=F32)


def _iota2(shape):
    return lax.broadcasted_iota(jnp.int32, shape, 0), lax.broadcasted_iota(jnp.int32, shape, 1)


HEADS_PER_STEP = 4
HEADS = range(HEADS_PER_STEP)


CHUNK_HEADS_PER_STEP = 2
CHUNK_HEADS = range(CHUNK_HEADS_PER_STEP)


def _head_spec(T, rows=None, width=HEAD_DIM, heads=HEADS_PER_STEP):
    return pl.BlockSpec((heads, T if rows is None else rows, width), lambda g: (g, 0, 0))


def _sb_weights(qb, kb, t0, s0, racc, m_gt, row, col):
    z = _dot_nt(qb, kb) * SCALE
    strict = (s0 + col) < (t0 + row)
    lb = _log_sigmoid(z)
    lf = jnp.where(strict, lb - z, 0.0)
    between = _dot3(lf, m_gt) + racc
    w = jnp.where(strict, jnp.exp(lb + between), 0.0)
    return strict, lb, lf, w


def _sb_fwd(q, k, v, name):
    H, T, _ = q.shape
    nb = T // QB

    def body(q_ref, k_ref, v_ref, o_ref):
        row, col = _iota2((QB, QB))
        m_gt = (row > col).astype(BF16)

        def qblock(i, _):
            t0 = pl.multiple_of(i * QB, QB)
            qbs = [q_ref[h, pl.ds(t0, QB), :].astype(BF16) for h in HEADS]

            def kblock(jj, carry):
                s0 = pl.multiple_of((i - jj) * QB, QB)
                out = []
                for h in HEADS:
                    acc, racc = carry[h]
                    kb = k_ref[h, pl.ds(s0, QB), :].astype(BF16)
                    vb = v_ref[h, pl.ds(s0, QB), :].astype(BF16)
                    _, _, lf, w = _sb_weights(qbs[h], kb, t0, s0, racc, m_gt, row, col)
                    acc = acc + jnp.dot(w.astype(BF16), vb, preferred_element_type=F32)
                    out.append((acc, racc + jnp.sum(lf, axis=1, keepdims=True)))
                return tuple(out)

            res = lax.fori_loop(0, i + 1, kblock,
                                tuple((jnp.zeros((QB, HEAD_DIM), F32), jnp.zeros((QB, 1), F32)) for _ in HEADS))
            for h in HEADS:
                o_ref[h, pl.ds(t0, QB), :] = res[h][0]
            return 0

        lax.fori_loop(0, nb, qblock, 0)

    spec = _head_spec(T)
    return _pcall(body, name=name, grid=(H // HEADS_PER_STEP,), in_specs=[spec] * 3, out_specs=spec,
                  out_shape=jax.ShapeDtypeStruct((H, T, HEAD_DIM), F32))(q, k, v)


def _sb_bwd(q, k, v, do, name):
    H, T, _ = q.shape
    nb = T // QB

    def body(q_ref, k_ref, v_ref, do_ref, dq_ref, dk_ref, dv_ref, racc_ref):
        row, col = _iota2((QB, QB))
        m_gt = (row > col).astype(BF16)
        m_lt = (row < col).astype(BF16)
        dk_ref[...] = jnp.zeros_like(dk_ref)
        dv_ref[...] = jnp.zeros_like(dv_ref)

        def qblock(i, _):
            t0 = pl.multiple_of(i * QB, QB)
            qbs = [q_ref[h, pl.ds(t0, QB), :].astype(BF16) for h in HEADS]
            dobs = [do_ref[h, pl.ds(t0, QB), :].astype(BF16) for h in HEADS]

            def suffix(jj, raccs):
                j = i - jj
                s0 = pl.multiple_of(j * QB, QB)
                out = []
                for h in HEADS:
                    kb = k_ref[h, pl.ds(s0, QB), :].astype(BF16)
                    z = _dot_nt(qbs[h], kb) * SCALE
                    lf = jnp.where((s0 + col) < (t0 + row), _log_sigmoid(z) - z, 0.0)
                    racc_ref[h, j] = raccs[h]
                    out.append(raccs[h] + jnp.sum(lf, axis=1, keepdims=True))
                return tuple(out)

            lax.fori_loop(0, i + 1, suffix, tuple(jnp.zeros((QB, 1), F32) for _ in HEADS))

            def kblock(j, carry):
                s0 = pl.multiple_of(j * QB, QB)
                out = []
                for h in HEADS:
                    dq, cacc = carry[h]
                    kb = k_ref[h, pl.ds(s0, QB), :].astype(BF16)
                    vb = v_ref[h, pl.ds(s0, QB), :].astype(BF16)
                    strict, lb, _, w = _sb_weights(qbs[h], kb, t0, s0, racc_ref[h, j], m_gt, row, col)
                    e = _dot_nt(dobs[h], vb) * w
                    c_left = _dot3(e, m_lt) + cacc
                    sig = jnp.exp(lb)
                    dz = jnp.where(strict, e * (1.0 - sig) - c_left * sig, 0.0).astype(BF16)
                    dq = dq + jnp.dot(dz, kb, preferred_element_type=F32)
                    dk_ref[h, pl.ds(s0, QB), :] += _dot_tn(dz, qbs[h]) * SCALE
                    dv_ref[h, pl.ds(s0, QB), :] += _dot_tn(w.astype(BF16), dobs[h])
                    out.append((dq, cacc + jnp.sum(e, axis=1, keepdims=True)))
                return tuple(out)

            res = lax.fori_loop(0, i + 1, kblock,
                                tuple((jnp.zeros((QB, HEAD_DIM), F32), jnp.zeros((QB, 1), F32)) for _ in HEADS))
            for h in HEADS:
                dq_ref[h, pl.ds(t0, QB), :] = res[h][0] * SCALE
            return 0

        lax.fori_loop(0, nb, qblock, 0)

    spec = _head_spec(T)
    return _pcall(body, name=name, grid=(H // HEADS_PER_STEP,), in_specs=[spec] * 4, out_specs=[spec] * 3,
                  out_shape=[jax.ShapeDtypeStruct((H, T, HEAD_DIM), F32)] * 3,
                  scratch_shapes=[pltpu.VMEM((HEADS_PER_STEP, nb, QB, 1), F32)])(q, k, v, do)


def _fox_logits(qb, kb, fq, fk, t0, s0, row, col):
    z = _dot_nt(qb, kb) * SCALE + fq - fk
    return jnp.where((s0 + col) <= (t0 + row), z, NEG)


def _fox_specs(T):
    return _head_spec(T, width=1), _head_spec(T, rows=T // QB, width=QB)


def _fox_fwd(q, k, v, fq, fk, name):
    H, T, _ = q.shape
    nb = T // QB

    def body(q_ref, k_ref, v_ref, fq_ref, fk_ref, o_ref, lse_ref):
        row, col = _iota2((QB, QB))

        def qblock(i, _):
            t0 = pl.multiple_of(i * QB, QB)
            qbs = [q_ref[h, pl.ds(t0, QB), :].astype(BF16) for h in HEADS]
            fqs = [fq_ref[h, pl.ds(t0, QB), :] for h in HEADS]

            def kblock(j, carry):
                s0 = pl.multiple_of(j * QB, QB)
                out = []
                for h in HEADS:
                    acc, m, l = carry[h]
                    kb = k_ref[h, pl.ds(s0, QB), :].astype(BF16)
                    vb = v_ref[h, pl.ds(s0, QB), :].astype(BF16)
                    z = _fox_logits(qbs[h], kb, fqs[h], fk_ref[h, pl.ds(j, 1), :], t0, s0, row, col)
                    m_new = jnp.maximum(m, jnp.max(z, axis=1, keepdims=True))
                    a = jnp.exp(m - m_new)
                    p = jnp.exp(z - m_new)
                    acc = a * acc + jnp.dot(p.astype(BF16), vb, preferred_element_type=F32)
                    out.append((acc, m_new, a * l + jnp.sum(p, axis=1, keepdims=True)))
                return tuple(out)

            res = lax.fori_loop(0, i + 1, kblock,
                                tuple((jnp.zeros((QB, HEAD_DIM), F32), jnp.full((QB, 1), NEG, F32),
                                       jnp.zeros((QB, 1), F32)) for _ in HEADS))
            for h in HEADS:
                acc, m, l = res[h]
                o_ref[h, pl.ds(t0, QB), :] = acc / l
                lse_ref[h, pl.ds(t0, QB), :] = m + jnp.log(l)
            return 0

        lax.fori_loop(0, nb, qblock, 0)

    spec = _head_spec(T)
    fq_spec, fk_spec = _fox_specs(T)
    return _pcall(body, name=name, grid=(H // HEADS_PER_STEP,), in_specs=[spec] * 3 + [fq_spec, fk_spec],
                  out_specs=[spec, fq_spec],
                  out_shape=[jax.ShapeDtypeStruct((H, T, HEAD_DIM), F32), jax.ShapeDtypeStruct((H, T, 1), F32)],
                  )(q, k, v, fq, fk)


def _fox_bwd(q, k, v, fq, fk, lse, do, name):
    H, T, _ = q.shape
    nb = T // QB

    def body(q_ref, k_ref, v_ref, fq_ref, fk_ref, lse_ref, do_ref, dq_ref, dk_ref, dv_ref, dfk_ref):
        row, col = _iota2((QB, QB))
        dk_ref[...] = jnp.zeros_like(dk_ref)
        dv_ref[...] = jnp.zeros_like(dv_ref)
        dfk_ref[...] = jnp.zeros_like(dfk_ref)

        def qblock(i, _):
            t0 = pl.multiple_of(i * QB, QB)
            qbs = [q_ref[h, pl.ds(t0, QB), :].astype(BF16) for h in HEADS]
            fqs = [fq_ref[h, pl.ds(t0, QB), :] for h in HEADS]
            lses = [lse_ref[h, pl.ds(t0, QB), :] for h in HEADS]
            dobs = [do_ref[h, pl.ds(t0, QB), :].astype(BF16) for h in HEADS]

            def probs(h, j):
                s0 = pl.multiple_of(j * QB, QB)
                kb = k_ref[h, pl.ds(s0, QB), :].astype(BF16)
                vb = v_ref[h, pl.ds(s0, QB), :].astype(BF16)
                z = _fox_logits(qbs[h], kb, fqs[h], fk_ref[h, pl.ds(j, 1), :], t0, s0, row, col)
                return s0, kb, jnp.exp(z - lses[h]), _dot_nt(dobs[h], vb)

            def row_dot(j, deltas):
                out = []
                for h in HEADS:
                    _, _, p, dp = probs(h, j)
                    out.append(deltas[h] + jnp.sum(p * dp, axis=1, keepdims=True))
                return tuple(out)

            deltas = lax.fori_loop(0, i + 1, row_dot, tuple(jnp.zeros((QB, 1), F32) for _ in HEADS))

            def kblock(j, dqs):
                out = []
                for h in HEADS:
                    s0, kb, p, dp = probs(h, j)
                    dz = p * (dp - deltas[h])
                    dzb = dz.astype(BF16)
                    dk_ref[h, pl.ds(s0, QB), :] += _dot_tn(dzb, qbs[h]) * SCALE
                    dv_ref[h, pl.ds(s0, QB), :] += _dot_tn(p.astype(BF16), dobs[h])
                    dfk_ref[h, pl.ds(j, 1), :] += -jnp.sum(dz, axis=0, keepdims=True)
                    out.append(dqs[h] + jnp.dot(dzb, kb, preferred_element_type=F32))
                return tuple(out)

            dqs = lax.fori_loop(0, i + 1, kblock, tuple(jnp.zeros((QB, HEAD_DIM), F32) for _ in HEADS))
            for h in HEADS:
                dq_ref[h, pl.ds(t0, QB), :] = dqs[h] * SCALE
            return 0

        lax.fori_loop(0, nb, qblock, 0)

    spec = _head_spec(T)
    fq_spec, fk_spec = _fox_specs(T)
    return _pcall(body, name=name, grid=(H // HEADS_PER_STEP,),
                  in_specs=[spec] * 3 + [fq_spec, fk_spec, fq_spec, spec],
                  out_specs=[spec, spec, spec, fk_spec],
                  out_shape=[jax.ShapeDtypeStruct((H, T, HEAD_DIM), F32)] * 3
                  + [jax.ShapeDtypeStruct((H, nb, QB), F32)],
                  )(q, k, v, fq, fk, lse, do)


def _forget_cumsum(flog, name):
    R, T = flog.shape
    nb = T // QB

    def body(x_ref, o_ref):
        row, col = _iota2((QB, QB))
        m_le = (row <= col).astype(BF16)
        carry = jnp.zeros((R, 1), F32)
        for b in range(nb):
            lf = _log_sigmoid(x_ref[:, b * QB:(b + 1) * QB])
            o_ref[:, b * QB:(b + 1) * QB] = _dot3(lf, m_le) + carry
            carry = carry + jnp.sum(lf, axis=1, keepdims=True)

    spec = pl.BlockSpec((R, T), lambda: (0, 0))
    return _pcall(body, name=name, in_specs=[spec], out_specs=spec,
                  out_shape=jax.ShapeDtypeStruct((R, T), F32))(flog)


def _forget_cumsum_bwd(flog, df, name):
    R, T = flog.shape
    nb = T // QB

    def body(x_ref, df_ref, o_ref):
        row, col = _iota2((QB, QB))
        m_ge = (row >= col).astype(BF16)
        carry = jnp.zeros((R, 1), F32)
        for b in reversed(range(nb)):
            dfb = df_ref[:, b * QB:(b + 1) * QB]
            dlog = _dot3(dfb, m_ge) + carry
            o_ref[:, b * QB:(b + 1) * QB] = dlog * _sigmoid(-x_ref[:, b * QB:(b + 1) * QB])
            carry = carry + jnp.sum(dfb, axis=1, keepdims=True)

    spec = pl.BlockSpec((R, T), lambda: (0, 0))
    return _pcall(body, name=name, in_specs=[spec, spec], out_specs=spec,
                  out_shape=jax.ShapeDtypeStruct((R, T), F32))(flog, df)


def _chunk_probs(qc, kb, bias, c, col):
    z = _dot_nt(qc, kb) * SCALE + bias
    z = jnp.where((c - (LEFT_CHUNKS + 1)) * CHUNK + col >= jnp.maximum(0, (c - LEFT_CHUNKS) * CHUNK), z, NEG)
    p = jnp.exp(z - jnp.max(z, axis=1, keepdims=True))
    return p, jnp.sum(p, axis=1, keepdims=True)


def _chunk_specs(T):
    n = CHUNK_HEADS_PER_STEP
    return (_head_spec(T, heads=n), _head_spec(T, rows=T + BAND_PAD, heads=n),
            _head_spec(T, rows=CHUNK, width=BAND, heads=n))


def _chunk_fwd(q, kp, vp, bias, name):
    H, T, _ = q.shape
    nc = T // CHUNK

    def body(q_ref, kp_ref, vp_ref, bias_ref, o_ref):
        _, col = _iota2((CHUNK, BAND))

        def chunk(c, _):
            t0 = pl.multiple_of(c * CHUNK, CHUNK)
            for h in CHUNK_HEADS:
                qc = q_ref[h, pl.ds(t0, CHUNK), :].astype(BF16)
                kb = kp_ref[h, pl.ds(t0, BAND), :].astype(BF16)
                vb = vp_ref[h, pl.ds(t0, BAND), :].astype(BF16)
                p, l = _chunk_probs(qc, kb, bias_ref[h], c, col)
                o_ref[h, pl.ds(t0, CHUNK), :] = jnp.dot(p.astype(BF16), vb, preferred_element_type=F32) / l
            return 0

        lax.fori_loop(0, nc, chunk, 0)

    q_spec, kp_spec, b_spec = _chunk_specs(T)
    return _pcall(body, name=name, grid=(H // CHUNK_HEADS_PER_STEP,), in_specs=[q_spec, kp_spec, kp_spec, b_spec],
                  out_specs=q_spec,
                  out_shape=jax.ShapeDtypeStruct((H, T, HEAD_DIM), F32))(q, kp, vp, bias)


def _chunk_bwd(q, kp, vp, bias, do, name):
    H, T, _ = q.shape
    nc = T // CHUNK

    def body(q_ref, kp_ref, vp_ref, bias_ref, do_ref, dq_ref, dkp_ref, dvp_ref, db_ref):
        _, col = _iota2((CHUNK, BAND))
        dkp_ref[...] = jnp.zeros_like(dkp_ref)
        dvp_ref[...] = jnp.zeros_like(dvp_ref)
        db_ref[...] = jnp.zeros_like(db_ref)

        def chunk(c, _):
            t0 = pl.multiple_of(c * CHUNK, CHUNK)
            for h in CHUNK_HEADS:
                qc = q_ref[h, pl.ds(t0, CHUNK), :].astype(BF16)
                kb = kp_ref[h, pl.ds(t0, BAND), :].astype(BF16)
                vb = vp_ref[h, pl.ds(t0, BAND), :].astype(BF16)
                dob = do_ref[h, pl.ds(t0, CHUNK), :].astype(BF16)
                p, l = _chunk_probs(qc, kb, bias_ref[h], c, col)
                p = p / l
                dp = _dot_nt(dob, vb)
                dz = p * (dp - jnp.sum(p * dp, axis=1, keepdims=True))
                dzb = dz.astype(BF16)
                dq_ref[h, pl.ds(t0, CHUNK), :] = jnp.dot(dzb, kb, preferred_element_type=F32) * SCALE
                dkp_ref[h, pl.ds(t0, BAND), :] += _dot_tn(dzb, qc) * SCALE
                dvp_ref[h, pl.ds(t0, BAND), :] += _dot_tn(p.astype(BF16), dob)
                db_ref[h] += dz
            return 0

        lax.fori_loop(0, nc, chunk, 0)

    q_spec, kp_spec, b_spec = _chunk_specs(T)
    return _pcall(body, name=name, grid=(H // CHUNK_HEADS_PER_STEP,),
                  in_specs=[q_spec, kp_spec, kp_spec, b_spec, q_spec],
                  out_specs=[q_spec, kp_spec, kp_spec, b_spec],
                  out_shape=[jax.ShapeDtypeStruct((H, T, HEAD_DIM), F32),
                             jax.ShapeDtypeStruct((H, T + BAND_PAD, HEAD_DIM), F32),
                             jax.ShapeDtypeStruct((H, T + BAND_PAD, HEAD_DIM), F32),
                             jax.ShapeDtypeStruct((H, CHUNK, BAND), F32)])(q, kp, vp, bias, do)


HBM_SPEC = pl.BlockSpec(memory_space=pltpu.HBM)


def _position():
    return lax.axis_index("x"), lax.axis_index("y"), lax.axis_index("c")


def _other_chips(x, y):
    chips = [(1 - x, y), (x, 1 - y), (1 - x, 1 - y)]
    return [(chip, 2 * chip[0] + chip[1]) for chip in chips]


def _remote_copy(src, dst, send_sems, recv_sems, k, to):
    return pltpu.make_async_remote_copy(src_ref=src, dst_ref=dst, send_sem=send_sems.at[k], recv_sem=recv_sems.at[k],
                                        device_id=to, device_id_type=MESH)


def _gather_weights(shards, name):
    n = len(shards)

    def body(*refs):
        src, dst = refs[:n], refs[n:2 * n]
        send_sems, recv_sems, local_sems = refs[2 * n:]
        x, y, c = _position()
        me = 2 * x + y
        sibling = (x, y, 1 - c)
        others = _other_chips(x, y)
        local = [pltpu.make_async_copy(src[i].at[l], dst[i].at[l, me], local_sems.at[DEPTH * i + l])
                 for i in range(n) for l in range(DEPTH)]
        for cp in local:
            cp.start()
        first = [_remote_copy(src[i].at[c], dst[i].at[c, me], send_sems, recv_sems, 3 * i + k, (*chip, c))
                 for i in range(n) for k, (chip, _) in enumerate(others)]
        for cp in first:
            cp.start()
        passed = []
        for i in range(n):
            for k, (_, idx) in enumerate(others):
                landed = dst[i].at[c, idx]
                _remote_copy(landed, landed, send_sems, recv_sems, 3 * i + k, sibling).wait_recv()
                passed.append(_remote_copy(landed, landed, send_sems, recv_sems, 3 * (n + i) + k, sibling))
                passed[-1].start()
        for i in range(n):
            for k, (_, idx) in enumerate(others):
                from_sibling = dst[i].at[1 - c, idx]
                _remote_copy(from_sibling, from_sibling, send_sems, recv_sems, 3 * (n + i) + k, sibling).wait_recv()
        for cp in first + passed:
            cp.wait_send()
        for cp in local:
            cp.wait()

    return _pcall(
        body, name=name, in_specs=[HBM_SPEC] * n, out_specs=[HBM_SPEC] * n,
        out_shape=[jax.ShapeDtypeStruct((DEPTH, N_CHIPS) + s.shape[1:], s.dtype) for s in shards],
        scratch_shapes=[pltpu.SemaphoreType.DMA((6 * n,)), pltpu.SemaphoreType.DMA((6 * n,)),
                        pltpu.SemaphoreType.DMA((DEPTH * n,))],
    )(*shards)


def _split_with_sibling(parts, name):
    n = len(parts)

    def body(*refs):
        src, keep, got = refs[:n], refs[n:2 * n], refs[2 * n:3 * n]
        send_sems, recv_sems, local_sems = refs[3 * n:]
        x, y, c = _position()
        copies = []
        for i in range(n):
            half = src[i].shape[1] // 2
            mine = src[i].at[:, pl.ds(pl.multiple_of(c * half, 16), half), :]
            theirs = src[i].at[:, pl.ds(pl.multiple_of((1 - c) * half, 16), half), :]
            copies.append(pltpu.make_async_copy(mine, keep[i], local_sems.at[i]))
            copies.append(_remote_copy(theirs, got[i], send_sems, recv_sems, i, (x, y, 1 - c)))
        for cp in copies:
            cp.start()
        for cp in copies:
            cp.wait()

    halves = [jax.ShapeDtypeStruct((p.shape[0], p.shape[1] // 2, p.shape[2]), p.dtype) for p in parts]
    out = _pcall(
        body, name=name, in_specs=[HBM_SPEC] * n, out_specs=[HBM_SPEC] * (2 * n), out_shape=halves + halves,
        scratch_shapes=[pltpu.SemaphoreType.DMA((n,)), pltpu.SemaphoreType.DMA((n,)), pltpu.SemaphoreType.DMA((n,))],
    )(*parts)
    return out[:n], out[n:]


def _scatter_chips(parts, name):
    n = len(parts)

    def body(*refs):
        src, dst = refs[:n], refs[n:2 * n]
        send_sems, recv_sems, local_sems = refs[2 * n:]
        x, y, c = _position()
        me = 2 * x + y
        others = _other_chips(x, y)
        local = [pltpu.make_async_copy(src[i].at[me], dst[i].at[me], local_sems.at[i]) for i in range(n)]
        for cp in local:
            cp.start()
        sends = [_remote_copy(src[i].at[idx], dst[i].at[me], send_sems, recv_sems, 3 * i + k, (*chip, c))
                 for i in range(n) for k, (chip, idx) in enumerate(others)]
        for cp in sends:
            cp.start()
        for i in range(n):
            for k, (_, idx) in enumerate(others):
                _remote_copy(src[i].at[idx], dst[i].at[idx], send_sems, recv_sems, 3 * i + k, (x, y, c)).wait_recv()
        for cp in sends:
            cp.wait_send()
        for cp in local:
            cp.wait()

    return _pcall(
        body, name=name, in_specs=[HBM_SPEC] * n, out_specs=[HBM_SPEC] * n,
        out_shape=[jax.ShapeDtypeStruct(p.shape, p.dtype) for p in parts],
        scratch_shapes=[pltpu.SemaphoreType.DMA((3 * n,)), pltpu.SemaphoreType.DMA((3 * n,)),
                        pltpu.SemaphoreType.DMA((n,))],
    )(*parts)


def _share_with_sibling(halves, shapes, name):
    n = len(halves)
    m = len(shapes)

    def body(*refs):
        src, dst = refs[:n], refs[n:n + m]
        send_sems, recv_sems, local_sems = refs[n + m:]
        x, y, c = _position()
        copies = []
        for i in range(n):
            h = src[i].shape[0]
            place = dst[i // DEPTH].at[i % DEPTH, pl.ds(pl.multiple_of(c * h, 8), h), :]
            copies.append(pltpu.make_async_copy(src[i], place, local_sems.at[i]))
            copies.append(_remote_copy(src[i], place, send_sems, recv_sems, i, (x, y, 1 - c)))
        for cp in copies:
            cp.start()
        for cp in copies:
            cp.wait()

    return _pcall(
        body, name=name, in_specs=[HBM_SPEC] * n, out_specs=[HBM_SPEC] * m,
        out_shape=[jax.ShapeDtypeStruct(s, F32) for s in shapes],
        scratch_shapes=[pltpu.SemaphoreType.DMA((n,)), pltpu.SemaphoreType.DMA((n,)), pltpu.SemaphoreType.DMA((n,))],
    )(*halves)


def _allreduce_small(v, name):
    R, C = v.shape

    def body(v_ref, o_ref, slots, send_sems, recv_sems):
        x, y, c = _position()
        me = 4 * x + 2 * y + c
        slots[0] = v_ref[...]
        sends = []
        for r in range(1, 8):
            fx, fy, fc = (r >> 2) & 1, (r >> 1) & 1, r & 1
            to = (x ^ fx, y ^ fy, c ^ fc)
            cp = pltpu.make_async_remote_copy(src_ref=v_ref, dst_ref=slots.at[r], send_sem=send_sems.at[r - 1],
                                              recv_sem=recv_sems.at[r - 1], device_id=to, device_id_type=MESH)
            cp.start()
            sends.append(cp)
        for cp in sends:
            cp.wait()
        acc = slots[me]
        for d in range(1, 8):
            acc = acc + slots[d ^ me]
        o_ref[...] = acc

    vmem = pl.BlockSpec(memory_space=pltpu.VMEM)
    return _pcall(
        body, name=name, in_specs=[vmem], out_specs=vmem, out_shape=jax.ShapeDtypeStruct((R, C), F32),
        scratch_shapes=[pltpu.VMEM((8, R, C), F32), pltpu.SemaphoreType.DMA((7,)), pltpu.SemaphoreType.DMA((7,))],
    )(v)


def _add_pair(a, b, name):
    N, R, C = a.shape
    tr = _pick(R, 512, 16)

    def body(a_ref, b_ref, o_ref):
        o_ref[...] = (a_ref[...].astype(F32) + b_ref[...].astype(F32)).astype(BF16)

    spec = pl.BlockSpec((None, tr, C), lambda n, i: (n, i, 0))
    return _pcall(body, name=name, grid=(N, R // tr), in_specs=[spec, spec], out_specs=spec,
                  out_shape=jax.ShapeDtypeStruct((N, R, C), BF16))(a, b)


def _sum_chips(parts, name):
    N, R, C = parts.shape
    tr = _pick(R, 512, 16)

    def body(p_ref, o_ref):
        acc = p_ref[0].astype(F32)
        for n in range(1, N):
            acc = acc + p_ref[n].astype(F32)
        o_ref[...] = acc

    return _pcall(body, name=name, grid=(R // tr,), in_specs=[pl.BlockSpec((N, tr, C), lambda i: (0, i, 0))],
                  out_specs=pl.BlockSpec((tr, C), lambda i: (i, 0)),
                  out_shape=jax.ShapeDtypeStruct((R, C), F32))(parts)


BIG = ("w_ffn1_in", "w_ffn1_out", "w_in", "w_br_sb", "w_br_ch", "w_br_fox", "w_out", "w_ffn2_in", "w_ffn2_out")
ROW_SHARDED = ("w_ffn1_out", "w_out", "w_ffn2_out")
SMALL = ("g_ffn1", "g_mix", "b_in", "rel_bias", "g_ffn2", "g_final")
SHARD_SHAPES = {
    "w_ffn1_in": (D_MODEL, 2 * D_FF // N_CHIPS), "w_ffn1_out": (D_FF // N_CHIPS, D_MODEL),
    "w_in": (D_MODEL, IN_WIDTH // N_CHIPS), "w_br_sb": (W_SB, D_MODEL // N_CHIPS),
    "w_br_ch": (W_CH, D_MODEL // N_CHIPS), "w_br_fox": (W_FOX, D_MODEL // N_CHIPS),
    "w_out": (D_MODEL // N_CHIPS, D_MODEL), "w_ffn2_in": (D_MODEL, 2 * D_FF // N_CHIPS),
    "w_ffn2_out": (D_FF // N_CHIPS, D_MODEL),
}
def _reduce_scatter_grads(by_chip):
    keep, got = _split_with_sibling(by_chip, "grad_split")
    pair = [_add_pair(a, b, f"grad_pair_sum_{i}") for i, (a, b) in enumerate(zip(keep, got))]
    landed = _scatter_chips(pair, "grad_scatter")
    halves = [_sum_chips(p, f"grad_chip_sum_{i}") for i, p in enumerate(landed)]
    shapes = [(DEPTH,) + by_chip[i].shape[1:] for i in range(0, len(by_chip), DEPTH)]
    return _share_with_sibling(halves, shapes, "grad_share")


SMALL_SIZES = {"g_ffn1": DEPTH * D_MODEL, "g_mix": DEPTH * D_MODEL, "b_in": DEPTH * IN_WIDTH,
               "rel_bias": DEPTH * N_REL * H_CH, "g_ffn2": DEPTH * D_MODEL, "g_final": D_MODEL}
SMALL_TOTAL = sum(SMALL_SIZES.values()) + 1
SMALL_ROWS = -(-SMALL_TOTAL // (8 * 128)) * 8


def _pack_small(vals, extra):
    flat = [vals[n].reshape(-1) for n in SMALL] + [extra.reshape(-1)]
    flat.append(jnp.zeros((SMALL_ROWS * 128 - SMALL_TOTAL,), F32))
    return jnp.concatenate(flat).reshape(SMALL_ROWS, 128)


def _unpack_small(pack, shapes):
    flat, out, off = pack.reshape(-1), {}, 0
    for n in SMALL:
        out[n] = flat[off:off + SMALL_SIZES[n]].reshape(shapes[n])
        off += SMALL_SIZES[n]
    return out, flat[off]


def _to_heads(t, n_heads):
    return jnp.transpose(t.reshape(t.shape[0], n_heads, HEAD_DIM), (1, 0, 2)).astype(BF16)


def _from_heads(t):
    return jnp.transpose(t, (1, 0, 2)).reshape(t.shape[1], t.shape[0] * HEAD_DIM)


def _pad_keys(t):
    return jnp.pad(t, ((0, 0), (BAND_PAD, 0), (0, 0)))


DIAGONALS = CHUNK + BAND - 1


def _band_bias(table):
    n_far = (LEFT_CHUNKS + 1) * CHUNK + CHUNK - MAX_REL
    near = table[N_REL - 1 - (DIAGONALS - n_far):N_REL - 1][::-1]
    diag = jnp.concatenate([jnp.broadcast_to(table[N_REL - 1], (n_far, H_CH)), near], axis=0).T
    diag = jnp.pad(diag, ((0, 0), (0, 1)))
    skew = jnp.tile(diag, (1, CHUNK))[:, :CHUNK * DIAGONALS].reshape(H_CH, CHUNK, DIAGONALS)
    return skew[:, :, CHUNK - 1:]


def _pad_w_in(w):
    qkv, f, gates = w[:, :QKV_WIDTH], w[:, QKV_WIDTH:QKV_WIDTH + H_FOX], w[:, QKV_WIDTH + H_FOX:]
    return jnp.concatenate([qkv, gates, f, jnp.zeros((w.shape[0], F_PAD - H_FOX), w.dtype)], axis=1)


def _unpad_w_in(w):
    return jnp.concatenate([w[:, :QKV_WIDTH], w[:, QKV_WIDTH + GATE_WIDTH:QKV_WIDTH + GATE_WIDTH + H_FOX],
                            w[:, QKV_WIDTH:QKV_WIDTH + GATE_WIDTH]], axis=1)


QKV_SPLITS = np.cumsum([0, W_SB, W_SB, W_SB, W_CH, W_CH, W_CH, W_FOX, W_FOX, W_FOX])


def _by_chip_rows(g):
    return g.reshape(N_CHIPS, g.shape[0] // N_CHIPS, g.shape[1])


def _ffn_fwd(x, g, w_in, w_out, layer, tag):
    h = _rmsnorm_fwd(x, g, f"{tag}_norm")
    gu = _mm(h, w_in, mode="nn", name=f"{tag}_in", gathered=(layer, "col"))
    a = _swiglu_fwd(gu, f"{tag}_act")
    y = _mm(a, w_out, mode="nn", name=f"{tag}_out", alpha=0.5, res=x, gathered=(layer, "row"))
    return y, (h, gu, a)


def _ffn_bwd(x, g, w_in, w_out, layer, saved, dy, tag):
    h, gu, a = saved
    da = _mm(dy, w_out, mode="nt", name=f"{tag}_da", alpha=0.5, gathered=(layer, "row"))
    dw_out = _mm(a, dy, mode="tn", name=f"{tag}_dwout", alpha=0.5, tm_cap=512, out_dtype=BF16)
    dgu = _swiglu_bwd(gu, da, f"{tag}_dact")
    dw_in = _mm(h, dgu, mode="tn", name=f"{tag}_dwin", tm_cap=512, out_dtype=BF16, out_by_chip=True)
    dh = _mm(dgu, w_in, mode="nt", name=f"{tag}_dh", gathered=(layer, "col"))
    dx, dg = _rmsnorm_bwd(x, g, dh, dy, f"{tag}_dnorm")
    return dx, dg, dw_in, _by_chip_rows(dw_out)


def _mixer_fwd(x, lw, tag):
    T = x.shape[0]
    h = _rmsnorm_fwd(x, lw["g_mix"], f"{tag}_norm")
    p = _mm(h, lw["w_in"], mode="nn", name=f"{tag}_proj", bias=lw["b_in"], tn_cap=896)
    parts = [p[:, QKV_SPLITS[i]:QKV_SPLITS[i + 1]] for i in range(9)]
    qa, ka, va = (_to_heads(t, H_SB) for t in parts[0:3])
    qb, kb, vb = (_to_heads(t, H_CH) for t in parts[3:6])
    qc, kc, vc = (_to_heads(t, H_FOX) for t in parts[6:9])
    flog = jnp.pad(p[:, QKV_WIDTH + GATE_WIDTH:QKV_WIDTH + GATE_WIDTH + H_FOX].T, ((0, 8 - H_FOX), (0, 0)))
    fcum = _forget_cumsum(flog, f"{tag}_fcum")[:H_FOX]
    fq, fk = fcum.reshape(H_FOX, T, 1), fcum.reshape(H_FOX, T // QB, QB)
    kbp, vbp = _pad_keys(kb), _pad_keys(vb)
    oa = _sb_fwd(qa, ka, va, f"{tag}_sb")
    ob = _chunk_fwd(qb, kbp, vbp, lw["bias"], f"{tag}_ch")
    oc, lse = _fox_fwd(qc, kc, vc, fq, fk, f"{tag}_fox")
    oam, obm, ocm = _from_heads(oa), _from_heads(ob), _from_heads(oc)
    col, row = (lw["layer"], "col"), (lw["layer"], "row")
    ya = _mm(oam, lw["w_br_sb"], mode="nn", name=f"{tag}_bra", gathered=col)
    yb = _mm(obm, lw["w_br_ch"], mode="nn", name=f"{tag}_brb", gathered=col)
    yc = _mm(ocm, lw["w_br_fox"], mode="nn", name=f"{tag}_brc", gathered=col)
    merged = _gate_fwd(p, ya, yb, yc, f"{tag}_gate")
    y = _mm(merged, lw["w_out"], mode="nn", name=f"{tag}_out", res=x, gathered=row)
    saved = (h, p, (qa, ka, va), (qb, kbp, vbp), (qc, kc, vc), flog, fq, fk, lse, (oam, obm, ocm),
             (ya, yb, yc), merged)
    return y, saved


def _mixer_bwd(x, lw, saved, dy, tag):
    (h, p, (qa, ka, va), (qb, kbp, vbp), (qc, kc, vc), flog, fq, fk, lse, (oam, obm, ocm),
     (ya, yb, yc), merged) = saved
    T = x.shape[0]
    grads = {}
    col, row = (lw["layer"], "col"), (lw["layer"], "row")
    dmerged = _mm(dy, lw["w_out"], mode="nt", name=f"{tag}_dmerged", gathered=row)
    grads["w_out"] = _by_chip_rows(_mm(merged, dy, mode="tn", name=f"{tag}_dwout", tm_cap=512, out_dtype=BF16))
    dya, dyb, dyc, dgate = _gate_bwd(p, ya, yb, yc, dmerged, f"{tag}_dgate")
    doa = _mm(dya, lw["w_br_sb"], mode="nt", name=f"{tag}_doa", gathered=col)
    dob = _mm(dyb, lw["w_br_ch"], mode="nt", name=f"{tag}_dob", gathered=col)
    doc = _mm(dyc, lw["w_br_fox"], mode="nt", name=f"{tag}_doc", gathered=col)
    by_chip = dict(mode="tn", tm_cap=512, out_dtype=BF16, out_by_chip=True)
    grads["w_br_sb"] = _mm(oam, dya, name=f"{tag}_dwbra", **by_chip)
    grads["w_br_ch"] = _mm(obm, dyb, name=f"{tag}_dwbrb", **by_chip)
    grads["w_br_fox"] = _mm(ocm, dyc, name=f"{tag}_dwbrc", **by_chip)
    dqa, dka, dva = _sb_bwd(qa, ka, va, _to_heads(doa, H_SB), f"{tag}_dsb")
    dqb, dkbp, dvbp, dbias = _chunk_bwd(qb, kbp, vbp, lw["bias"], _to_heads(dob, H_CH), f"{tag}_dch")
    dqc, dkc, dvc, dfk = _fox_bwd(qc, kc, vc, fq, fk, lse, _to_heads(doc, H_FOX), f"{tag}_dfox")
    dflog = _forget_cumsum_bwd(flog, jnp.pad(dfk.reshape(H_FOX, T), ((0, 8 - H_FOX), (0, 0))), f"{tag}_dfcum")
    dqkv = [_from_heads(t) for t in (dqa, dka, dva, dqb, dkbp[:, BAND_PAD:], dvbp[:, BAND_PAD:], dqc, dkc, dvc)]
    dp = jnp.concatenate(dqkv + [dgate, jnp.pad(dflog[:H_FOX].T, ((0, 0), (0, F_PAD - H_FOX)))], axis=1)
    grads["b_in"] = _colsum(dp, f"{tag}_dbin")
    dpb = dp.astype(BF16)
    dw_in = _unpad_w_in(_mm(h, dpb, mode="tn", name=f"{tag}_dwin", tm_cap=512, tn_cap=896, out_dtype=BF16))
    grads["w_in"] = jnp.transpose(dw_in.reshape(D_MODEL, N_CHIPS, IN_WIDTH // N_CHIPS), (1, 0, 2))
    dh = _mm(dpb, lw["w_in"], mode="nt", name=f"{tag}_dh", tk_cap=896)
    dx, grads["g_mix"] = _rmsnorm_bwd(x, lw["g_mix"], dh, dy, f"{tag}_dnorm")
    grads["rel_bias"] = lw["bias_vjp"](dbias)[0]
    return dx, grads


def kernel(x, g_ffn1, w_ffn1_in, w_ffn1_out, g_mix, w_in, b_in, rel_bias, w_br_sb, w_br_ch, w_br_fox, w_out, g_ffn2, w_ffn2_in, w_ffn2_out, g_final, loss_target, m_g_ffn1, m_w_ffn1_in, m_w_ffn1_out, m_g_mix, m_w_in, m_b_in, m_rel_bias, m_w_br_sb, m_w_br_ch, m_w_br_fox, m_w_out, m_g_ffn2, m_w_ffn2_in, m_w_ffn2_out, m_g_final, v_g_ffn1, v_w_ffn1_in, v_w_ffn1_out, v_g_mix, v_w_in, v_b_in, v_rel_bias, v_w_br_sb, v_w_br_ch, v_w_br_fox, v_w_out, v_g_ffn2, v_w_ffn2_in, v_w_ffn2_out, v_g_final):
    names = ("g_ffn1", "w_ffn1_in", "w_ffn1_out", "g_mix", "w_in", "b_in", "rel_bias", "w_br_sb", "w_br_ch",
             "w_br_fox", "w_out", "g_ffn2", "w_ffn2_in", "w_ffn2_out", "g_final")
    w = dict(zip(names, (g_ffn1, w_ffn1_in, w_ffn1_out, g_mix, w_in, b_in, rel_bias, w_br_sb, w_br_ch,
                         w_br_fox, w_out, g_ffn2, w_ffn2_in, w_ffn2_out, g_final)))
    m = dict(zip(names, (m_g_ffn1, m_w_ffn1_in, m_w_ffn1_out, m_g_mix, m_w_in, m_b_in, m_rel_bias, m_w_br_sb,
                         m_w_br_ch, m_w_br_fox, m_w_out, m_g_ffn2, m_w_ffn2_in, m_w_ffn2_out, m_g_final)))
    v = dict(zip(names, (v_g_ffn1, v_w_ffn1_in, v_w_ffn1_out, v_g_mix, v_w_in, v_b_in, v_rel_bias, v_w_br_sb,
                         v_w_br_ch, v_w_br_fox, v_w_out, v_g_ffn2, v_w_ffn2_in, v_w_ffn2_out, v_g_final)))
    xs = x[0]
    tgt = loss_target[0]

    gathered = dict(zip(BIG, _gather_weights([w[n].astype(BF16) for n in BIG], "gather_weights")))
    layers = []
    for l in range(DEPTH):
        lw = dict(gathered)
        lw["layer"] = l
        lw["w_in"] = _pad_w_in(jnp.concatenate([gathered["w_in"][l, j] for j in range(N_CHIPS)], axis=1))
        lw["b_in"] = _pad_w_in(w["b_in"][l][None, :])
        lw["bias"], lw["bias_vjp"] = jax.vjp(_band_bias, w["rel_bias"][l])
        for n in ("g_ffn1", "g_mix", "g_ffn2"):
            lw[n] = w[n][l][None, :]
        layers.append(lw)

    saved = []
    act = xs
    for l, lw in enumerate(layers):
        x0 = act
        x1, s1 = _ffn_fwd(x0, lw["g_ffn1"], lw["w_ffn1_in"], lw["w_ffn1_out"], l, f"l{l}_ffn1")
        x2, s2 = _mixer_fwd(x1, lw, f"l{l}_mix")
        act, s3 = _ffn_fwd(x2, lw["g_ffn2"], lw["w_ffn2_in"], lw["w_ffn2_out"], l, f"l{l}_ffn2")
        saved.append((x0, s1, x1, s2, x2, s3))

    dact, dg_final, loss_row = _final_loss(act, w["g_final"][None, :], tgt, "final_loss")

    big_grads = {n: [None] * DEPTH for n in BIG}
    small_grads = {n: [None] * DEPTH for n in ("g_ffn1", "g_mix", "b_in", "rel_bias", "g_ffn2")}
    for l in reversed(range(DEPTH)):
        lw = layers[l]
        x0, s1, x1, s2, x2, s3 = saved[l]
        dx2, small_grads["g_ffn2"][l], big_grads["w_ffn2_in"][l], big_grads["w_ffn2_out"][l] = _ffn_bwd(
            x2, lw["g_ffn2"], lw["w_ffn2_in"], lw["w_ffn2_out"], l, s3, dact, f"l{l}_ffn2")
        dx1, mg = _mixer_bwd(x1, lw, s2, dx2, f"l{l}_mix")
        for n in ("w_in", "w_br_sb", "w_br_ch", "w_br_fox", "w_out"):
            big_grads[n][l] = mg[n]
        small_grads["b_in"][l] = _unpad_w_in(mg["b_in"])[0]
        small_grads["g_mix"][l] = mg["g_mix"][0]
        small_grads["rel_bias"][l] = mg["rel_bias"]
        dact, dg1, big_grads["w_ffn1_in"][l], big_grads["w_ffn1_out"][l] = _ffn_bwd(
            x0, lw["g_ffn1"], lw["w_ffn1_in"], lw["w_ffn1_out"], l, s1, dx1, f"l{l}_ffn1")
        small_grads["g_ffn1"][l] = dg1[0]
        small_grads["g_ffn2"][l] = small_grads["g_ffn2"][l][0]
    grad_x = dact[None]

    small = {n: jnp.stack(small_grads[n]) for n in small_grads}
    small["g_final"] = dg_final[0]
    small_sum, loss = _unpack_small(_allreduce_small(_pack_small(small, loss_row[0, :1]), "allreduce_small"),
                                    {n: w[n].shape for n in SMALL})

    grads = dict(zip(BIG, _reduce_scatter_grads([big_grads[n][l] for n in BIG for l in range(DEPTH)])))
    grads.update(small_sum)

    delta, new_m, new_v = {}, {}, {}
    for n in BIG:
        shape = w[n].shape
        flat = lambda t: t.reshape(-1, shape[-1])
        d, nm, nv = _adamw(flat(w[n]), flat(grads[n]), flat(m[n]), flat(v[n]), f"adamw_{n}")
        delta[n], new_m[n], new_v[n] = d.reshape(shape), nm.reshape(shape), nv.reshape(shape)
    zero = jnp.zeros((1,), F32)
    sd, sm, sv = _adamw(_pack_small(w, zero), _pack_small(grads, zero), _pack_small(m, zero), _pack_small(v, zero),
                        "adamw_small")
    shapes = {n: w[n].shape for n in SMALL}
    for dst, src in ((delta, sd), (new_m, sm), (new_v, sv)):
        dst.update(_unpack_small(src, shapes)[0])

    return (loss, grad_x, *[grads[n] for n in names], *[delta[n] for n in names],
            *[new_m[n] for n in names], *[new_v[n] for n in names])
```

```python
import functools

import numpy as np
import jax
import jax.numpy as jnp
from jax import lax
from jax.experimental import pallas as pl
from jax.experimental.pallas import tpu as pltpu

F32 = jnp.float32
BF16 = jnp.bfloat16

D_MODEL = 1024
DEPTH = 2
CHUNK = 64
HEAD_DIM = 64
H_SB, H_CH, H_FOX = 4, 8, 4
W_SB, W_CH, W_FOX = H_SB * HEAD_DIM, H_CH * HEAD_DIM, H_FOX * HEAD_DIM
LEFT_CHUNKS = 8
MAX_REL = 128
N_REL = 2 * MAX_REL + 1
D_FF = 2816
QKV_WIDTH = 3 * (W_SB + W_CH + W_FOX)
GATE_WIDTH = 3 * D_MODEL
IN_WIDTH = QKV_WIDTH + H_FOX + GATE_WIDTH
F_PAD = 128
P_WIDTH = QKV_WIDTH + GATE_WIDTH + F_PAD
RMS_EPS = 1e-6
NEG = -1e30
SCALE = HEAD_DIM ** -0.5
QB = 256
BAND = (LEFT_CHUNKS + 2) * CHUNK
BAND_PAD = BAND - CHUNK

ADAM_LR, ADAM_B1, ADAM_B2, ADAM_EPS, ADAM_WD, ADAM_STEP = 0.001, 0.9, 0.999, 1e-08, 0.01, 10

N_CHIPS = 4
PACK_COLS = 1024
VMEM_BUDGET = 52 * 1024 * 1024

NT = (((1,), (1,)), ((), ()))
TN = (((0,), (0,)), ((), ()))
NN = (((1,), (0,)), ((), ()))
MESH = pl.DeviceIdType.MESH


def _pcall(body, **kw):
    return pl.pallas_call(body, **kw)


def _pick(n, cap, mult=128):
    best = None
    d = mult
    while d <= min(n, cap):
        if n % d == 0:
            best = d
        d += mult
    return n if best is None else best


def _nbytes(shape, dtype):
    return int(np.prod(shape)) * jnp.dtype(dtype).itemsize


def _mm(a, b, *, mode, name, out_dtype=F32, alpha=1.0, bias=None, res=None, tm_cap=1024, tn_cap=512,
        tk_cap=2816, gathered=None, out_split=None):
    if mode == "tn":
        K, M = a.shape
    else:
        M, K = a.shape
    dn = {"nn": NN, "nt": NT, "tn": TN}[mode]
    b_rows = None
    if gathered is None:
        N = b.shape[0] if mode == "nt" else b.shape[1]
        tn = {None: _pick(N, tn_cap), "col": N // N_CHIPS, "row": N // 2}[out_split]
        if out_split == "col":
            tm_cap = min(tm_cap, M // 2)
        tk = K if K <= tk_cap else _pick(K, tk_cap)
        b_spec = (pl.BlockSpec((tn, tk), lambda i, j, k: (j, k)) if mode == "nt"
                  else pl.BlockSpec((tk, tn), lambda i, j, k: (k, j)))
    else:
        layer, sharding = gathered
        r, c = b.shape[2:]
        rows, cols = (r, N_CHIPS * c) if sharding == "col" else (N_CHIPS * r, c)
        N = rows if mode == "nt" else cols
        assert K == (cols if mode == "nt" else rows), (name, K, rows, cols)
        if sharding == "col" and mode == "nn":
            tn, tk = c, (r if r <= tk_cap else _pick(r, tk_cap))
            b_spec = pl.BlockSpec((None, None, tk, c), lambda i, j, k: (layer, j, k, 0))
        elif sharding == "col":
            tn, tk = _pick(r, tn_cap), c
            b_spec = pl.BlockSpec((None, None, tn, c), lambda i, j, k: (layer, k, j, 0))
        elif mode == "nn":
            tn, tk, b_rows = _pick(c, tn_cap), K, N_CHIPS
            b_spec = pl.BlockSpec((None, N_CHIPS, r, tn), lambda i, j, k: (layer, 0, 0, j))
        else:
            tn, tk, b_rows = 2 * r, K, 2
            b_spec = pl.BlockSpec((None, 2, r, c), lambda i, j, k: (layer, j, 0, 0))
    tm = _pick(M, tm_cap)
    nk = K // tk

    a_spec = (pl.BlockSpec((tk, tm), lambda i, j, k: (k, i)) if mode == "tn"
              else pl.BlockSpec((tm, tk), lambda i, j, k: (i, k)))
    in_specs = [a_spec, b_spec]
    args = [a, b]
    if bias is not None:
        in_specs.append(pl.BlockSpec((1, tn), lambda i, j, k: (0, j)))
        args.append(bias)
    if res is not None:
        in_specs.append(pl.BlockSpec((tm, tn), lambda i, j, k: (i, j)))
        args.append(res)

    est = 2 * (_nbytes((tm, tk), a.dtype) + _nbytes((tk, tn), b.dtype) + _nbytes((tm, tn), out_dtype))
    est += _nbytes((tm, tn), F32) * (3 if res is not None else 1)
    assert est < VMEM_BUDGET, (name, est)

    def body(*refs):
        a_ref, b_ref = refs[0], refs[1]
        pos = 2
        bias_ref = res_ref = None
        if bias is not None:
            bias_ref = refs[pos]
            pos += 1
        if res is not None:
            res_ref = refs[pos]
            pos += 1
        o_ref = refs[pos]
        acc_ref = refs[pos + 1] if nk > 1 else None

        bv = b_ref[...]
        if b_rows is not None:
            bv = bv.reshape(b_rows * bv.shape[1], bv.shape[2])
        part = lax.dot_general(a_ref[...].astype(BF16), bv.astype(BF16), dn, preferred_element_type=F32)

        def finish(acc):
            if alpha != 1.0:
                acc = acc * alpha
            if bias_ref is not None:
                acc = acc + bias_ref[...]
            if res_ref is not None:
                acc = acc + res_ref[...]
            o_ref[...] = acc.astype(out_dtype)

        if nk == 1:
            finish(part)
        else:
            k = pl.program_id(2)

            @pl.when(k == 0)
            def _():
                acc_ref[...] = part

            @pl.when(k > 0)
            def _():
                acc_ref[...] += part

            @pl.when(k == nk - 1)
            def _():
                finish(acc_ref[...])

    if out_split == "col":
        per_half = M // 2 // tm
        out_spec = pl.BlockSpec((None, None, tm, tn), lambda i, j, k: (i // per_half, j, i % per_half, 0))
        out_shape = jax.ShapeDtypeStruct((2, N_CHIPS, M // 2, tn), out_dtype)
    elif out_split == "row":
        out_spec = pl.BlockSpec((None, tm, tn), lambda i, j, k: (j, i, 0))
        out_shape = jax.ShapeDtypeStruct((2, M, tn), out_dtype)
    else:
        out_spec = pl.BlockSpec((tm, tn), lambda i, j, k: (i, j))
        out_shape = jax.ShapeDtypeStruct((M, N), out_dtype)
    return _pcall(
        body, name=name, grid=(M // tm, N // tn, nk), in_specs=in_specs, out_specs=out_spec, out_shape=out_shape,
        scratch_shapes=[pltpu.VMEM((tm, tn), F32)] if nk > 1 else [],
        compiler_params=pltpu.CompilerParams(dimension_semantics=("parallel", "parallel", "arbitrary")),
    )(*args)


def _rmsnorm_fwd(x, g, name):
    T, Dm = x.shape
    tm = _pick(T, 256, 8)

    def body(x_ref, g_ref, h_ref):
        xv = x_ref[...]
        rstd = lax.rsqrt(jnp.mean(xv * xv, axis=-1, keepdims=True) + RMS_EPS)
        h_ref[...] = (xv * rstd * g_ref[...]).astype(BF16)

    return _pcall(
        body, name=name, grid=(T // tm,),
        in_specs=[pl.BlockSpec((tm, Dm), lambda i: (i, 0)), pl.BlockSpec((1, Dm), lambda i: (0, 0))],
        out_specs=pl.BlockSpec((tm, Dm), lambda i: (i, 0)),
        out_shape=jax.ShapeDtypeStruct((T, Dm), BF16),
    )(x, g)


def _rmsnorm_bwd(x, g, dh, dres, name):
    T, Dm = x.shape
    tm = _pick(T, 256, 8)

    def body(x_ref, g_ref, dh_ref, dres_ref, dx_ref, dg_ref):
        xv = x_ref[...]
        rstd = lax.rsqrt(jnp.mean(xv * xv, axis=-1, keepdims=True) + RMS_EPS)
        xhat = xv * rstd
        dhv = dh_ref[...]
        dyg = dhv * g_ref[...]
        dx_ref[...] = dres_ref[...] + rstd * (dyg - xhat * jnp.mean(dyg * xhat, axis=-1, keepdims=True))
        part = jnp.sum(dhv * xhat, axis=0, keepdims=True)

        @pl.when(pl.program_id(0) == 0)
        def _():
            dg_ref[...] = part

        @pl.when(pl.program_id(0) > 0)
        def _():
            dg_ref[...] += part

    row = pl.BlockSpec((tm, Dm), lambda i: (i, 0))
    vec = pl.BlockSpec((1, Dm), lambda i: (0, 0))
    return _pcall(
        body, name=name, grid=(T // tm,), in_specs=[row, vec, row, row], out_specs=[row, vec],
        out_shape=[jax.ShapeDtypeStruct((T, Dm), F32), jax.ShapeDtypeStruct((1, Dm), F32)],
        compiler_params=pltpu.CompilerParams(dimension_semantics=("arbitrary",)),
    )(x, g, dh, dres)


def _final_loss(x, g, tgt, name):
    T, Dm = x.shape
    tm = _pick(T, 256, 8)

    def body(x_ref, g_ref, t_ref, dx_ref, dg_ref, loss_ref):
        xv = x_ref[...]
        gv = g_ref[...]
        rstd = lax.rsqrt(jnp.mean(xv * xv, axis=-1, keepdims=True) + RMS_EPS)
        xhat = xv * rstd
        err = xhat * gv - t_ref[...]
        part_loss = 0.5 * jnp.sum(jnp.mean(err * err, axis=-1, keepdims=True), axis=0, keepdims=True)
        dy = err * (1.0 / Dm)
        dyg = dy * gv
        dx_ref[...] = rstd * (dyg - xhat * jnp.mean(dyg * xhat, axis=-1, keepdims=True))
        part_g = jnp.sum(dy * xhat, axis=0, keepdims=True)
        part_l = jnp.broadcast_to(part_loss, (1, 128))

        @pl.when(pl.program_id(0) == 0)
        def _():
            dg_ref[...] = part_g
            loss_ref[...] = part_l

        @pl.when(pl.program_id(0) > 0)
        def _():
            dg_ref[...] += part_g
            loss_ref[...] += part_l

    row = pl.BlockSpec((tm, Dm), lambda i: (i, 0))
    vec = pl.BlockSpec((1, Dm), lambda i: (0, 0))
    return _pcall(
        body, name=name, grid=(T // tm,), in_specs=[row, vec, row],
        out_specs=[row, vec, pl.BlockSpec((1, 128), lambda i: (0, 0))],
        out_shape=[jax.ShapeDtypeStruct((T, Dm), F32), jax.ShapeDtypeStruct((1, Dm), F32),
                   jax.ShapeDtypeStruct((1, 128), F32)],
        compiler_params=pltpu.CompilerParams(dimension_semantics=("arbitrary",)),
    )(x, g, tgt)


def _sigmoid(z):
    return 1.0 / (1.0 + jnp.exp(-z))


def _swiglu_fwd(gu, name):
    T = gu.shape[0]
    tm = _pick(T, 128, 8)

    def body(gu_ref, a_ref):
        gv = gu_ref[:, :D_FF]
        uv = gu_ref[:, D_FF:]
        a_ref[...] = (gv * _sigmoid(gv) * uv).astype(BF16)

    return _pcall(
        body, name=name, grid=(T // tm,), in_specs=[pl.BlockSpec((tm, 2 * D_FF), lambda i: (i, 0))],
        out_specs=pl.BlockSpec((tm, D_FF), lambda i: (i, 0)),
        out_shape=jax.ShapeDtypeStruct((T, D_FF), BF16),
    )(gu)


def _swiglu_bwd(gu, da, name):
    T = gu.shape[0]
    tm = _pick(T, 128, 8)

    def body(gu_ref, da_ref, d_ref):
        gv = gu_ref[:, :D_FF]
        uv = gu_ref[:, D_FF:]
        dav = da_ref[...]
        sg = _sigmoid(gv)
        d_ref[:, :D_FF] = (dav * uv * (sg + gv * sg * (1.0 - sg))).astype(BF16)
        d_ref[:, D_FF:] = (dav * gv * sg).astype(BF16)

    return _pcall(
        body, name=name, grid=(T // tm,),
        in_specs=[pl.BlockSpec((tm, 2 * D_FF), lambda i: (i, 0)), pl.BlockSpec((tm, D_FF), lambda i: (i, 0))],
        out_specs=pl.BlockSpec((tm, 2 * D_FF), lambda i: (i, 0)),
        out_shape=jax.ShapeDtypeStruct((T, 2 * D_FF), BF16),
    )(gu, da)


GATE_BLOCK0 = QKV_WIDTH // D_MODEL


def _gate_fwd(p, ya, yb, yc, name):
    T = p.shape[0]
    tm = _pick(T, 256, 8)

    def body(ga_ref, gb_ref, gc_ref, ya_ref, yb_ref, yc_ref, o_ref):
        o_ref[...] = (_sigmoid(ga_ref[...]) * ya_ref[...] + _sigmoid(gb_ref[...]) * yb_ref[...]
                      + _sigmoid(gc_ref[...]) * yc_ref[...]).astype(BF16)

    gate = [pl.BlockSpec((tm, D_MODEL), functools.partial(lambda i, kk: (i, GATE_BLOCK0 + kk), kk=kk))
            for kk in range(3)]
    row = pl.BlockSpec((tm, D_MODEL), lambda i: (i, 0))
    return _pcall(
        body, name=name, grid=(T // tm,), in_specs=gate + [row, row, row], out_specs=row,
        out_shape=jax.ShapeDtypeStruct((T, D_MODEL), BF16),
    )(p, p, p, ya, yb, yc)


def _gate_bwd(p, ya, yb, yc, dm, name):
    T = p.shape[0]
    tm = _pick(T, 256, 8)

    def body(ga_ref, gb_ref, gc_ref, ya_ref, yb_ref, yc_ref, dm_ref, da_ref, db_ref, dc_ref, dg_ref):
        dmv = dm_ref[...]
        for kk, (g_ref, y_ref, d_ref) in enumerate(((ga_ref, ya_ref, da_ref), (gb_ref, yb_ref, db_ref),
                                                    (gc_ref, yc_ref, dc_ref))):
            s = _sigmoid(g_ref[...])
            d_ref[...] = (s * dmv).astype(BF16)
            dg_ref[:, kk * D_MODEL:(kk + 1) * D_MODEL] = dmv * y_ref[...] * s * (1.0 - s)

    gate = [pl.BlockSpec((tm, D_MODEL), functools.partial(lambda i, kk: (i, GATE_BLOCK0 + kk), kk=kk))
            for kk in range(3)]
    row = pl.BlockSpec((tm, D_MODEL), lambda i: (i, 0))
    return _pcall(
        body, name=name, grid=(T // tm,), in_specs=gate + [row, row, row, row],
        out_specs=[row, row, row, pl.BlockSpec((tm, GATE_WIDTH), lambda i: (i, 0))],
        out_shape=[jax.ShapeDtypeStruct((T, D_MODEL), BF16)] * 3 + [jax.ShapeDtypeStruct((T, GATE_WIDTH), F32)],
    )(p, p, p, ya, yb, yc, dm)


def _colsum(a, name):
    T, N = a.shape
    tm = _pick(T, 256, 8)

    def body(a_ref, o_ref):
        part = jnp.sum(a_ref[...], axis=0, keepdims=True)

        @pl.when(pl.program_id(0) == 0)
        def _():
            o_ref[...] = part

        @pl.when(pl.program_id(0) > 0)
        def _():
            o_ref[...] += part

    return _pcall(
        body, name=name, grid=(T // tm,), in_specs=[pl.BlockSpec((tm, N), lambda i: (i, 0))],
        out_specs=pl.BlockSpec((1, N), lambda i: (0, 0)), out_shape=jax.ShapeDtypeStruct((1, N), F32),
        compiler_params=pltpu.CompilerParams(dimension_semantics=("arbitrary",)),
    )(a)


def _adamw(w, g, m, v, name):
    R, C = w.shape
    tr = 256 if R % 256 == 0 else R
    c1 = 1.0 / (1.0 - ADAM_B1 ** ADAM_STEP)
    c2 = 1.0 / (1.0 - ADAM_B2 ** ADAM_STEP)

    def body(w_ref, g_ref, m_ref, v_ref, d_ref, nm_ref, nv_ref):
        gv = g_ref[...]
        mn = ADAM_B1 * m_ref[...] + (1.0 - ADAM_B1) * gv
        vn = ADAM_B2 * v_ref[...] + (1.0 - ADAM_B2) * (gv * gv)
        nm_ref[...] = mn
        nv_ref[...] = vn
        d_ref[...] = -ADAM_LR * ((mn * c1) / (jnp.sqrt(vn * c2) + ADAM_EPS) + ADAM_WD * w_ref[...])

    spec = pl.BlockSpec((tr, C), lambda i: (i, 0))
    return _pcall(
        body, name=name, grid=(R // tr,), in_specs=[spec] * 4, out_specs=[spec] * 3,
        out_shape=[jax.ShapeDtypeStruct((R, C), F32)] * 3,
    )(w, g, m, v)


def _log_sigmoid(z):
    return jnp.minimum(z, 0.0) - jnp.log(1.0 + jnp.exp(-jnp.abs(z)))


def _dot3(x, m01):
    x1 = x.astype(BF16)
    r1 = x - x1.astype(F32)
    x2 = r1.astype(BF16)
    x3 = (r1 - x2.astype(F32)).astype(BF16)
    n = x.shape[0]
    if n % 16:
        return (jnp.dot(x1, m01, preferred_element_type=F32) + jnp.dot(x2, m01, preferred_element_type=F32)
                + jnp.dot(x3, m01, preferred_element_type=F32))
    y = jnp.dot(jnp.concatenate([x1, x2, x3], axis=0), m01, preferred_element_type=F32)
    return y[:n] + y[n:2 * n] + y[2 * n:]


def _dot_nt(a, b):
    return lax.dot_general(a, b, NT, preferred_element_type=F32)


def _dot_tn(a, b):
    return lax.dot_general(a, b, TN, preferred_element_type=F32)


def _iota2(shape):
    return lax.broadcasted_iota(jnp.int32, shape, 0), lax.broadcasted_iota(jnp.int32, shape, 1)


HEADS_PER_STEP = 4
HEADS = range(HEADS_PER_STEP)


CHUNK_HEADS_PER_STEP = 2
CHUNK_HEADS = range(CHUNK_HEADS_PER_STEP)


def _head_spec(T, rows=None, width=HEAD_DIM, heads=HEADS_PER_STEP):
    return pl.BlockSpec((heads, T if rows is None else rows, width), lambda g: (g, 0, 0))


def _sb_weights(qb, kb, t0, s0, racc, m_gt, row, col):
    z = _dot_nt(qb, kb) * SCALE
    strict = (s0 + col) < (t0 + row)
    lb = _log_sigmoid(z)
    lf = jnp.where(strict, lb - z, 0.0)
    between = _dot3(lf, m_gt) + racc
    w = jnp.where(strict, jnp.exp(lb + between), 0.0)
    return strict, lb, lf, w


def _sb_fwd(q, k, v, name):
    H, T, _ = q.shape
    nb = T // QB

    def body(q_ref, k_ref, v_ref, o_ref):
        row, col = _iota2((QB, QB))
        m_gt = (row > col).astype(BF16)

        def qblock(i, _):
            t0 = pl.multiple_of(i * QB, QB)
            qbs = [q_ref[h, pl.ds(t0, QB), :].astype(BF16) for h in HEADS]

            def kblock(jj, carry):
                s0 = pl.multiple_of((i - jj) * QB, QB)
                out = []
                for h in HEADS:
                    acc, racc = carry[h]
                    kb = k_ref[h, pl.ds(s0, QB), :].astype(BF16)
                    vb = v_ref[h, pl.ds(s0, QB), :].astype(BF16)
                    _, _, lf, w = _sb_weights(qbs[h], kb, t0, s0, racc, m_gt, row, col)
                    acc = acc + jnp.dot(w.astype(BF16), vb, preferred_element_type=F32)
                    out.append((acc, racc + jnp.sum(lf, axis=1, keepdims=True)))
                return tuple(out)

            res = lax.fori_loop(0, i + 1, kblock,
                                tuple((jnp.zeros((QB, HEAD_DIM), F32), jnp.zeros((QB, 1), F32)) for _ in HEADS))
            for h in HEADS:
                o_ref[h, pl.ds(t0, QB), :] = res[h][0]
            return 0

        lax.fori_loop(0, nb, qblock, 0)

    spec = _head_spec(T)
    return _pcall(body, name=name, grid=(H // HEADS_PER_STEP,), in_specs=[spec] * 3, out_specs=spec,
                  out_shape=jax.ShapeDtypeStruct((H, T, HEAD_DIM), F32))(q, k, v)


def _sb_bwd(q, k, v, do, name):
    H, T, _ = q.shape
    nb = T // QB

    def body(q_ref, k_ref, v_ref, do_ref, dq_ref, dk_ref, dv_ref, racc_ref):
        row, col = _iota2((QB, QB))
        m_gt = (row > col).astype(BF16)
        m_lt = (row < col).astype(BF16)
        dk_ref[...] = jnp.zeros_like(dk_ref)
        dv_ref[...] = jnp.zeros_like(dv_ref)

        def qblock(i, _):
            t0 = pl.multiple_of(i * QB, QB)
            qbs = [q_ref[h, pl.ds(t0, QB), :].astype(BF16) for h in HEADS]
            dobs = [do_ref[h, pl.ds(t0, QB), :].astype(BF16) for h in HEADS]

            def suffix(jj, raccs):
                j = i - jj
                s0 = pl.multiple_of(j * QB, QB)
                out = []
                for h in HEADS:
                    kb = k_ref[h, pl.ds(s0, QB), :].astype(BF16)
                    z = _dot_nt(qbs[h], kb) * SCALE
                    lf = jnp.where((s0 + col) < (t0 + row), _log_sigmoid(z) - z, 0.0)
                    racc_ref[h, j] = raccs[h]
                    out.append(raccs[h] + jnp.sum(lf, axis=1, keepdims=True))
                return tuple(out)

            lax.fori_loop(0, i + 1, suffix, tuple(jnp.zeros((QB, 1), F32) for _ in HEADS))

            def kblock(j, carry):
                s0 = pl.multiple_of(j * QB, QB)
                out = []
                for h in HEADS:
                    dq, cacc = carry[h]
                    kb = k_ref[h, pl.ds(s0, QB), :].astype(BF16)
                    vb = v_ref[h, pl.ds(s0, QB), :].astype(BF16)
                    strict, lb, _, w = _sb_weights(qbs[h], kb, t0, s0, racc_ref[h, j], m_gt, row, col)
                    e = _dot_nt(dobs[h], vb) * w
                    c_left = _dot3(e, m_lt) + cacc
                    sig = jnp.exp(lb)
                    dz = jnp.where(strict, e * (1.0 - sig) - c_left * sig, 0.0).astype(BF16)
                    dq = dq + jnp.dot(dz, kb, preferred_element_type=F32)
                    dk_ref[h, pl.ds(s0, QB), :] += _dot_tn(dz, qbs[h]) * SCALE
                    dv_ref[h, pl.ds(s0, QB), :] += _dot_tn(w.astype(BF16), dobs[h])
                    out.append((dq, cacc + jnp.sum(e, axis=1, keepdims=True)))
                return tuple(out)

            res = lax.fori_loop(0, i + 1, kblock,
                                tuple((jnp.zeros((QB, HEAD_DIM), F32), jnp.zeros((QB, 1), F32)) for _ in HEADS))
            for h in HEADS:
                dq_ref[h, pl.ds(t0, QB), :] = res[h][0] * SCALE
            return 0

        lax.fori_loop(0, nb, qblock, 0)

    spec = _head_spec(T)
    return _pcall(body, name=name, grid=(H // HEADS_PER_STEP,), in_specs=[spec] * 4, out_specs=[spec] * 3,
                  out_shape=[jax.ShapeDtypeStruct((H, T, HEAD_DIM), F32)] * 3,
                  scratch_shapes=[pltpu.VMEM((HEADS_PER_STEP, nb, QB, 1), F32)])(q, k, v, do)


def _fox_logits(qb, kb, fq, fk, t0, s0, row, col):
    z = _dot_nt(qb, kb) * SCALE + fq - fk
    return jnp.where((s0 + col) <= (t0 + row), z, NEG)


def _fox_specs(T):
    return _head_spec(T, width=1), _head_spec(T, rows=T // QB, width=QB)


def _fox_fwd(q, k, v, fq, fk, name):
    H, T, _ = q.shape
    nb = T // QB

    def body(q_ref, k_ref, v_ref, fq_ref, fk_ref, o_ref, lse_ref):
        row, col = _iota2((QB, QB))

        def qblock(i, _):
            t0 = pl.multiple_of(i * QB, QB)
            qbs = [q_ref[h, pl.ds(t0, QB), :].astype(BF16) for h in HEADS]
            fqs = [fq_ref[h, pl.ds(t0, QB), :] for h in HEADS]

            def kblock(j, carry):
                s0 = pl.multiple_of(j * QB, QB)
                out = []
                for h in HEADS:
                    acc, m, l = carry[h]
                    kb = k_ref[h, pl.ds(s0, QB), :].astype(BF16)
                    vb = v_ref[h, pl.ds(s0, QB), :].astype(BF16)
                    z = _fox_logits(qbs[h], kb, fqs[h], fk_ref[h, pl.ds(j, 1), :], t0, s0, row, col)
                    m_new = jnp.maximum(m, jnp.max(z, axis=1, keepdims=True))
                    a = jnp.exp(m - m_new)
                    p = jnp.exp(z - m_new)
                    acc = a * acc + jnp.dot(p.astype(BF16), vb, preferred_element_type=F32)
                    out.append((acc, m_new, a * l + jnp.sum(p, axis=1, keepdims=True)))
                return tuple(out)

            res = lax.fori_loop(0, i + 1, kblock,
                                tuple((jnp.zeros((QB, HEAD_DIM), F32), jnp.full((QB, 1), NEG, F32),
                                       jnp.zeros((QB, 1), F32)) for _ in HEADS))
            for h in HEADS:
                acc, m, l = res[h]
                o_ref[h, pl.ds(t0, QB), :] = acc / l
                lse_ref[h, pl.ds(t0, QB), :] = m + jnp.log(l)
            return 0

        lax.fori_loop(0, nb, qblock, 0)

    spec = _head_spec(T)
    fq_spec, fk_spec = _fox_specs(T)
    return _pcall(body, name=name, grid=(H // HEADS_PER_STEP,), in_specs=[spec] * 3 + [fq_spec, fk_spec],
                  out_specs=[spec, fq_spec],
                  out_shape=[jax.ShapeDtypeStruct((H, T, HEAD_DIM), F32), jax.ShapeDtypeStruct((H, T, 1), F32)],
                  )(q, k, v, fq, fk)


def _fox_bwd(q, k, v, fq, fk, lse, do, name):
    H, T, _ = q.shape
    nb = T // QB

    def body(q_ref, k_ref, v_ref, fq_ref, fk_ref, lse_ref, do_ref, dq_ref, dk_ref, dv_ref, dfk_ref):
        row, col = _iota2((QB, QB))
        dk_ref[...] = jnp.zeros_like(dk_ref)
        dv_ref[...] = jnp.zeros_like(dv_ref)
        dfk_ref[...] = jnp.zeros_like(dfk_ref)

        def qblock(i, _):
            t0 = pl.multiple_of(i * QB, QB)
            qbs = [q_ref[h, pl.ds(t0, QB), :].astype(BF16) for h in HEADS]
            fqs = [fq_ref[h, pl.ds(t0, QB), :] for h in HEADS]
            lses = [lse_ref[h, pl.ds(t0, QB), :] for h in HEADS]
            dobs = [do_ref[h, pl.ds(t0, QB), :].astype(BF16) for h in HEADS]

            def probs(h, j):
                s0 = pl.multiple_of(j * QB, QB)
                kb = k_ref[h, pl.ds(s0, QB), :].astype(BF16)
                vb = v_ref[h, pl.ds(s0, QB), :].astype(BF16)
                z = _fox_logits(qbs[h], kb, fqs[h], fk_ref[h, pl.ds(j, 1), :], t0, s0, row, col)
                return s0, kb, jnp.exp(z - lses[h]), _dot_nt(dobs[h], vb)

            def row_dot(j, deltas):
                out = []
                for h in HEADS:
                    _, _, p, dp = probs(h, j)
                    out.append(deltas[h] + jnp.sum(p * dp, axis=1, keepdims=True))
                return tuple(out)

            deltas = lax.fori_loop(0, i + 1, row_dot, tuple(jnp.zeros((QB, 1), F32) for _ in HEADS))

            def kblock(j, dqs):
                out = []
                for h in HEADS:
                    s0, kb, p, dp = probs(h, j)
                    dz = p * (dp - deltas[h])
                    dzb = dz.astype(BF16)
                    dk_ref[h, pl.ds(s0, QB), :] += _dot_tn(dzb, qbs[h]) * SCALE
                    dv_ref[h, pl.ds(s0, QB), :] += _dot_tn(p.astype(BF16), dobs[h])
                    dfk_ref[h, pl.ds(j, 1), :] += -jnp.sum(dz, axis=0, keepdims=True)
                    out.append(dqs[h] + jnp.dot(dzb, kb, preferred_element_type=F32))
                return tuple(out)

            dqs = lax.fori_loop(0, i + 1, kblock, tuple(jnp.zeros((QB, HEAD_DIM), F32) for _ in HEADS))
            for h in HEADS:
                dq_ref[h, pl.ds(t0, QB), :] = dqs[h] * SCALE
            return 0

        lax.fori_loop(0, nb, qblock, 0)

    spec = _head_spec(T)
    fq_spec, fk_spec = _fox_specs(T)
    return _pcall(body, name=name, grid=(H // HEADS_PER_STEP,),
                  in_specs=[spec] * 3 + [fq_spec, fk_spec, fq_spec, spec],
                  out_specs=[spec, spec, spec, fk_spec],
                  out_shape=[jax.ShapeDtypeStruct((H, T, HEAD_DIM), F32)] * 3
                  + [jax.ShapeDtypeStruct((H, nb, QB), F32)],
                  )(q, k, v, fq, fk, lse, do)


def _forget_cumsum(flog, name):
    R, T = flog.shape
    nb = T // QB

    def body(x_ref, o_ref):
        row, col = _iota2((QB, QB))
        m_le = (row <= col).astype(BF16)
        carry = jnp.zeros((R, 1), F32)
        for b in range(nb):
            lf = _log_sigmoid(x_ref[:, b * QB:(b + 1) * QB])
            o_ref[:, b * QB:(b + 1) * QB] = _dot3(lf, m_le) + carry
            carry = carry + jnp.sum(lf, axis=1, keepdims=True)

    spec = pl.BlockSpec((R, T), lambda: (0, 0))
    return _pcall(body, name=name, in_specs=[spec], out_specs=spec,
                  out_shape=jax.ShapeDtypeStruct((R, T), F32))(flog)


def _forget_cumsum_bwd(flog, df, name):
    R, T = flog.shape
    nb = T // QB

    def body(x_ref, df_ref, o_ref):
        row, col = _iota2((QB, QB))
        m_ge = (row >= col).astype(BF16)
        carry = jnp.zeros((R, 1), F32)
        for b in reversed(range(nb)):
            dfb = df_ref[:, b * QB:(b + 1) * QB]
            dlog = _dot3(dfb, m_ge) + carry
            o_ref[:, b * QB:(b + 1) * QB] = dlog * _sigmoid(-x_ref[:, b * QB:(b + 1) * QB])
            carry = carry + jnp.sum(dfb, axis=1, keepdims=True)

    spec = pl.BlockSpec((R, T), lambda: (0, 0))
    return _pcall(body, name=name, in_specs=[spec, spec], out_specs=spec,
                  out_shape=jax.ShapeDtypeStruct((R, T), F32))(flog, df)


def _chunk_probs(qc, kb, bias, c, col):
    z = _dot_nt(qc, kb) * SCALE + bias
    z = jnp.where((c - (LEFT_CHUNKS + 1)) * CHUNK + col >= jnp.maximum(0, (c - LEFT_CHUNKS) * CHUNK), z, NEG)
    p = jnp.exp(z - jnp.max(z, axis=1, keepdims=True))
    return p, jnp.sum(p, axis=1, keepdims=True)


def _chunk_specs(T):
    n = CHUNK_HEADS_PER_STEP
    return (_head_spec(T, heads=n), _head_spec(T, rows=T + BAND_PAD, heads=n),
            _head_spec(T, rows=CHUNK, width=BAND, heads=n))


def _chunk_fwd(q, kp, vp, bias, name):
    H, T, _ = q.shape
    nc = T // CHUNK

    def body(q_ref, kp_ref, vp_ref, bias_ref, o_ref):
        _, col = _iota2((CHUNK, BAND))

        def chunk(c, _):
            t0 = pl.multiple_of(c * CHUNK, CHUNK)
            for h in CHUNK_HEADS:
                qc = q_ref[h, pl.ds(t0, CHUNK), :].astype(BF16)
                kb = kp_ref[h, pl.ds(t0, BAND), :].astype(BF16)
                vb = vp_ref[h, pl.ds(t0, BAND), :].astype(BF16)
                p, l = _chunk_probs(qc, kb, bias_ref[h], c, col)
                o_ref[h, pl.ds(t0, CHUNK), :] = jnp.dot(p.astype(BF16), vb, preferred_element_type=F32) / l
            return 0

        lax.fori_loop(0, nc, chunk, 0)

    q_spec, kp_spec, b_spec = _chunk_specs(T)
    return _pcall(body, name=name, grid=(H // CHUNK_HEADS_PER_STEP,), in_specs=[q_spec, kp_spec, kp_spec, b_spec],
                  out_specs=q_spec,
                  out_shape=jax.ShapeDtypeStruct((H, T, HEAD_DIM), F32))(q, kp, vp, bias)


def _chunk_bwd(q, kp, vp, bias, do, name):
    H, T, _ = q.shape
    nc = T // CHUNK

    def body(q_ref, kp_ref, vp_ref, bias_ref, do_ref, dq_ref, dkp_ref, dvp_ref, db_ref):
        _, col = _iota2((CHUNK, BAND))
        dkp_ref[...] = jnp.zeros_like(dkp_ref)
        dvp_ref[...] = jnp.zeros_like(dvp_ref)
        db_ref[...] = jnp.zeros_like(db_ref)

        def chunk(c, _):
            t0 = pl.multiple_of(c * CHUNK, CHUNK)
            for h in CHUNK_HEADS:
                qc = q_ref[h, pl.ds(t0, CHUNK), :].astype(BF16)
                kb = kp_ref[h, pl.ds(t0, BAND), :].astype(BF16)
                vb = vp_ref[h, pl.ds(t0, BAND), :].astype(BF16)
                dob = do_ref[h, pl.ds(t0, CHUNK), :].astype(BF16)
                p, l = _chunk_probs(qc, kb, bias_ref[h], c, col)
                p = p / l
                dp = _dot_nt(dob, vb)
                dz = p * (dp - jnp.sum(p * dp, axis=1, keepdims=True))
                dzb = dz.astype(BF16)
                dq_ref[h, pl.ds(t0, CHUNK), :] = jnp.dot(dzb, kb, preferred_element_type=F32) * SCALE
                dkp_ref[h, pl.ds(t0, BAND), :] += _dot_tn(dzb, qc) * SCALE
                dvp_ref[h, pl.ds(t0, BAND), :] += _dot_tn(p.astype(BF16), dob)
                db_ref[h] += dz
            return 0

        lax.fori_loop(0, nc, chunk, 0)

    q_spec, kp_spec, b_spec = _chunk_specs(T)
    return _pcall(body, name=name, grid=(H // CHUNK_HEADS_PER_STEP,),
                  in_specs=[q_spec, kp_spec, kp_spec, b_spec, q_spec],
                  out_specs=[q_spec, kp_spec, kp_spec, b_spec],
                  out_shape=[jax.ShapeDtypeStruct((H, T, HEAD_DIM), F32),
                             jax.ShapeDtypeStruct((H, T + BAND_PAD, HEAD_DIM), F32),
                             jax.ShapeDtypeStruct((H, T + BAND_PAD, HEAD_DIM), F32),
                             jax.ShapeDtypeStruct((H, CHUNK, BAND), F32)])(q, kp, vp, bias, do)


HBM_SPEC = pl.BlockSpec(memory_space=pltpu.HBM)


def _position():
    return lax.axis_index("x"), lax.axis_index("y"), lax.axis_index("c")


def _other_chips(x, y):
    chips = [(1 - x, y), (x, 1 - y), (1 - x, 1 - y)]
    return [(chip, 2 * chip[0] + chip[1]) for chip in chips]


def _remote_copy(src, dst, send_sems, recv_sems, k, to):
    return pltpu.make_async_remote_copy(src_ref=src, dst_ref=dst, send_sem=send_sems.at[k], recv_sem=recv_sems.at[k],
                                        device_id=to, device_id_type=MESH)


def _gather_weights(shards, name):
    n = len(shards)

    def body(*refs):
        src, dst = refs[:n], refs[n:2 * n]
        send_sems, recv_sems, local_sems = refs[2 * n:]
        x, y, c = _position()
        me = 2 * x + y
        sibling = (x, y, 1 - c)
        others = _other_chips(x, y)
        local = [pltpu.make_async_copy(src[i].at[l], dst[i].at[l, me], local_sems.at[DEPTH * i + l])
                 for i in range(n) for l in range(DEPTH)]
        for cp in local:
            cp.start()
        first = [_remote_copy(src[i].at[c], dst[i].at[c, me], send_sems, recv_sems, 3 * i + k, (*chip, c))
                 for i in range(n) for k, (chip, _) in enumerate(others)]
        for cp in first:
            cp.start()
        passed = []
        for i in range(n):
            for k, (_, idx) in enumerate(others):
                landed = dst[i].at[c, idx]
                _remote_copy(landed, landed, send_sems, recv_sems, 3 * i + k, sibling).wait_recv()
                passed.append(_remote_copy(landed, landed, send_sems, recv_sems, 3 * (n + i) + k, sibling))
                passed[-1].start()
        for i in range(n):
            for k, (_, idx) in enumerate(others):
                from_sibling = dst[i].at[1 - c, idx]
                _remote_copy(from_sibling, from_sibling, send_sems, recv_sems, 3 * (n + i) + k, sibling).wait_recv()
        for cp in first + passed:
            cp.wait_send()
        for cp in local:
            cp.wait()

    return _pcall(
        body, name=name, in_specs=[HBM_SPEC] * n, out_specs=[HBM_SPEC] * n,
        out_shape=[jax.ShapeDtypeStruct((DEPTH, N_CHIPS) + s.shape[1:], s.dtype) for s in shards],
        scratch_shapes=[pltpu.SemaphoreType.DMA((6 * n,)), pltpu.SemaphoreType.DMA((6 * n,)),
                        pltpu.SemaphoreType.DMA((DEPTH * n,))],
    )(*shards)


def _split_with_sibling(parts, name):
    n = len(parts)

    def body(*refs):
        src, keep, got = refs[:n], refs[n:2 * n], refs[2 * n:3 * n]
        send_sems, recv_sems, local_sems = refs[3 * n:]
        x, y, c = _position()
        copies = []
        for i in range(n):
            copies.append(pltpu.make_async_copy(src[i].at[c], keep[i], local_sems.at[i]))
            copies.append(_remote_copy(src[i].at[1 - c], got[i], send_sems, recv_sems, i, (x, y, 1 - c)))
        for cp in copies:
            cp.start()
        for cp in copies:
            cp.wait()

    halves = [jax.ShapeDtypeStruct(p.shape[1:], p.dtype) for p in parts]
    out = _pcall(
        body, name=name, in_specs=[HBM_SPEC] * n, out_specs=[HBM_SPEC] * (2 * n), out_shape=halves + halves,
        scratch_shapes=[pltpu.SemaphoreType.DMA((n,)), pltpu.SemaphoreType.DMA((n,)), pltpu.SemaphoreType.DMA((n,))],
    )(*parts)
    return out[:n], out[n:]


def _scatter_chips(parts, name):
    n = len(parts)

    def body(*refs):
        src, dst = refs[:n], refs[n:2 * n]
        send_sems, recv_sems, local_sems = refs[2 * n:]
        x, y, c = _position()
        me = 2 * x + y
        others = _other_chips(x, y)
        local = [pltpu.make_async_copy(src[i].at[me], dst[i].at[me], local_sems.at[i]) for i in range(n)]
        for cp in local:
            cp.start()
        sends = [_remote_copy(src[i].at[idx], dst[i].at[me], send_sems, recv_sems, 3 * i + k, (*chip, c))
                 for i in range(n) for k, (chip, idx) in enumerate(others)]
        for cp in sends:
            cp.start()
        for i in range(n):
            for k, (_, idx) in enumerate(others):
                _remote_copy(src[i].at[idx], dst[i].at[idx], send_sems, recv_sems, 3 * i + k, (x, y, c)).wait_recv()
        for cp in sends:
            cp.wait_send()
        for cp in local:
            cp.wait()

    return _pcall(
        body, name=name, in_specs=[HBM_SPEC] * n, out_specs=[HBM_SPEC] * n,
        out_shape=[jax.ShapeDtypeStruct(p.shape, p.dtype) for p in parts],
        scratch_shapes=[pltpu.SemaphoreType.DMA((3 * n,)), pltpu.SemaphoreType.DMA((3 * n,)),
                        pltpu.SemaphoreType.DMA((n,))],
    )(*parts)


def _swap_with_sibling(arrs, name):
    n = len(arrs)

    def body(*refs):
        src, dst = refs[:n], refs[n:2 * n]
        send_sems, recv_sems = refs[2 * n:]
        x, y, c = _position()
        copies = [_remote_copy(src[i], dst[i], send_sems, recv_sems, i, (x, y, 1 - c)) for i in range(n)]
        for cp in copies:
            cp.start()
        for cp in copies:
            cp.wait()

    return _pcall(
        body, name=name, in_specs=[HBM_SPEC] * n, out_specs=[HBM_SPEC] * n,
        out_shape=[jax.ShapeDtypeStruct(a.shape, a.dtype) for a in arrs],
        scratch_shapes=[pltpu.SemaphoreType.DMA((n,)), pltpu.SemaphoreType.DMA((n,))],
    )(*arrs)


def _allreduce_small(v, name):
    R, C = v.shape

    def body(v_ref, o_ref, slots, send_sems, recv_sems):
        x, y, c = _position()
        me = 4 * x + 2 * y + c
        slots[0] = v_ref[...]
        sends = []
        for r in range(1, 8):
            fx, fy, fc = (r >> 2) & 1, (r >> 1) & 1, r & 1
            to = (x ^ fx, y ^ fy, c ^ fc)
            cp = pltpu.make_async_remote_copy(src_ref=v_ref, dst_ref=slots.at[r], send_sem=send_sems.at[r - 1],
                                              recv_sem=recv_sems.at[r - 1], device_id=to, device_id_type=MESH)
            cp.start()
            sends.append(cp)
        for cp in sends:
            cp.wait()
        acc = slots[me]
        for d in range(1, 8):
            acc = acc + slots[d ^ me]
        o_ref[...] = acc

    vmem = pl.BlockSpec(memory_space=pltpu.VMEM)
    return _pcall(
        body, name=name, in_specs=[vmem], out_specs=vmem, out_shape=jax.ShapeDtypeStruct((R, C), F32),
        scratch_shapes=[pltpu.VMEM((8, R, C), F32), pltpu.SemaphoreType.DMA((7,)), pltpu.SemaphoreType.DMA((7,))],
    )(v)


def _add_pair(a, b, name):
    N, R, C = a.shape
    tr = _pick(R, 512, 16)

    def body(a_ref, b_ref, o_ref):
        o_ref[...] = (a_ref[...].astype(F32) + b_ref[...].astype(F32)).astype(BF16)

    spec = pl.BlockSpec((None, tr, C), lambda n, i: (n, i, 0))
    return _pcall(body, name=name, grid=(N, R // tr), in_specs=[spec, spec], out_specs=spec,
                  out_shape=jax.ShapeDtypeStruct((N, R, C), BF16))(a, b)


def _sum_chips(parts, name):
    N, R, C = parts.shape
    tr = _pick(R, 512, 16)

    def body(p_ref, o_ref):
        acc = p_ref[0].astype(F32)
        for n in range(1, N):
            acc = acc + p_ref[n].astype(F32)
        o_ref[...] = acc

    return _pcall(body, name=name, grid=(R // tr,), in_specs=[pl.BlockSpec((N, tr, C), lambda i: (0, i, 0))],
                  out_specs=pl.BlockSpec((tr, C), lambda i: (i, 0)),
                  out_shape=jax.ShapeDtypeStruct((R, C), F32))(parts)


BIG = ("w_ffn1_in", "w_ffn1_out", "w_in", "w_br_sb", "w_br_ch", "w_br_fox", "w_out", "w_ffn2_in", "w_ffn2_out")
ROW_SHARDED = ("w_ffn1_out", "w_out", "w_ffn2_out")
SMALL = ("g_ffn1", "g_mix", "b_in", "rel_bias", "g_ffn2", "g_final")
SHARD_SHAPES = {
    "w_ffn1_in": (D_MODEL, 2 * D_FF // N_CHIPS), "w_ffn1_out": (D_FF // N_CHIPS, D_MODEL),
    "w_in": (D_MODEL, IN_WIDTH // N_CHIPS), "w_br_sb": (W_SB, D_MODEL // N_CHIPS),
    "w_br_ch": (W_CH, D_MODEL // N_CHIPS), "w_br_fox": (W_FOX, D_MODEL // N_CHIPS),
    "w_out": (D_MODEL // N_CHIPS, D_MODEL), "w_ffn2_in": (D_MODEL, 2 * D_FF // N_CHIPS),
    "w_ffn2_out": (D_FF // N_CHIPS, D_MODEL),
}
def _reduce_scatter_grads(split, axes):
    keep, got = _split_with_sibling(split, "grad_split")
    pair = [_add_pair(a, b, f"grad_pair_sum_{i}") for i, (a, b) in enumerate(zip(keep, got))]
    landed = _scatter_chips(pair, "grad_scatter")
    mine = [_sum_chips(p, f"grad_chip_sum_{i}") for i, p in enumerate(landed)]
    theirs = _swap_with_sibling(mine, "grad_share")
    first = lax.axis_index("c") == 0
    return [jnp.concatenate([jnp.where(first, a, b), jnp.where(first, b, a)], axis=ax)
            for a, b, ax in zip(mine, theirs, axes)]


SMALL_SIZES = {"g_ffn1": DEPTH * D_MODEL, "g_mix": DEPTH * D_MODEL, "b_in": DEPTH * IN_WIDTH,
               "rel_bias": DEPTH * N_REL * H_CH, "g_ffn2": DEPTH * D_MODEL, "g_final": D_MODEL}
SMALL_TOTAL = sum(SMALL_SIZES.values()) + 1
SMALL_ROWS = -(-SMALL_TOTAL // (8 * 128)) * 8


def _pack_small(vals, extra):
    flat = [vals[n].reshape(-1) for n in SMALL] + [extra.reshape(-1)]
    flat.append(jnp.zeros((SMALL_ROWS * 128 - SMALL_TOTAL,), F32))
    return jnp.concatenate(flat).reshape(SMALL_ROWS, 128)


def _unpack_small(pack, shapes):
    flat, out, off = pack.reshape(-1), {}, 0
    for n in SMALL:
        out[n] = flat[off:off + SMALL_SIZES[n]].reshape(shapes[n])
        off += SMALL_SIZES[n]
    return out, flat[off]


def _to_heads(t, n_heads):
    return jnp.transpose(t.reshape(t.shape[0], n_heads, HEAD_DIM), (1, 0, 2)).astype(BF16)


def _from_heads(t):
    return jnp.transpose(t, (1, 0, 2)).reshape(t.shape[1], t.shape[0] * HEAD_DIM)


def _pad_keys(t):
    return jnp.pad(t, ((0, 0), (BAND_PAD, 0), (0, 0)))


DIAGONALS = CHUNK + BAND - 1


def _band_bias(table):
    n_far = (LEFT_CHUNKS + 1) * CHUNK + CHUNK - MAX_REL
    near = table[N_REL - 1 - (DIAGONALS - n_far):N_REL - 1][::-1]
    diag = jnp.concatenate([jnp.broadcast_to(table[N_REL - 1], (n_far, H_CH)), near], axis=0).T
    diag = jnp.pad(diag, ((0, 0), (0, 1)))
    skew = jnp.tile(diag, (1, CHUNK))[:, :CHUNK * DIAGONALS].reshape(H_CH, CHUNK, DIAGONALS)
    return skew[:, :, CHUNK - 1:]


def _pad_w_in(w):
    qkv, f, gates = w[:, :QKV_WIDTH], w[:, QKV_WIDTH:QKV_WIDTH + H_FOX], w[:, QKV_WIDTH + H_FOX:]
    return jnp.concatenate([qkv, gates, f, jnp.zeros((w.shape[0], F_PAD - H_FOX), w.dtype)], axis=1)


def _unpad_w_in(w):
    return jnp.concatenate([w[:, :QKV_WIDTH], w[:, QKV_WIDTH + GATE_WIDTH:QKV_WIDTH + GATE_WIDTH + H_FOX],
                            w[:, QKV_WIDTH:QKV_WIDTH + GATE_WIDTH]], axis=1)


QKV_SPLITS = np.cumsum([0, W_SB, W_SB, W_SB, W_CH, W_CH, W_CH, W_FOX, W_FOX, W_FOX])


def _by_chip_rows(g):
    return g.reshape(2, N_CHIPS, g.shape[1] // N_CHIPS, g.shape[2])


def _ffn_fwd(x, g, w_in, w_out, layer, tag):
    h = _rmsnorm_fwd(x, g, f"{tag}_norm")
    gu = _mm(h, w_in, mode="nn", name=f"{tag}_in", gathered=(layer, "col"))
    a = _swiglu_fwd(gu, f"{tag}_act")
    y = _mm(a, w_out, mode="nn", name=f"{tag}_out", alpha=0.5, res=x, gathered=(layer, "row"))
    return y, (h, gu, a)


def _ffn_bwd(x, g, w_in, w_out, layer, saved, dy, tag):
    h, gu, a = saved
    da = _mm(dy, w_out, mode="nt", name=f"{tag}_da", alpha=0.5, gathered=(layer, "row"))
    dw_out = _mm(a, dy, mode="tn", name=f"{tag}_dwout", alpha=0.5, tm_cap=512, out_dtype=BF16, out_split="row")
    dgu = _swiglu_bwd(gu, da, f"{tag}_dact")
    dw_in = _mm(h, dgu, mode="tn", name=f"{tag}_dwin", tm_cap=512, out_dtype=BF16, out_split="col")
    dh = _mm(dgu, w_in, mode="nt", name=f"{tag}_dh", gathered=(layer, "col"))
    dx, dg = _rmsnorm_bwd(x, g, dh, dy, f"{tag}_dnorm")
    return dx, dg, dw_in, _by_chip_rows(dw_out)


def _mixer_fwd(x, lw, tag):
    T = x.shape[0]
    h = _rmsnorm_fwd(x, lw["g_mix"], f"{tag}_norm")
    p = _mm(h, lw["w_in"], mode="nn", name=f"{tag}_proj", bias=lw["b_in"], tn_cap=896)
    parts = [p[:, QKV_SPLITS[i]:QKV_SPLITS[i + 1]] for i in range(9)]
    qa, ka, va = (_to_heads(t, H_SB) for t in parts[0:3])
    qb, kb, vb = (_to_heads(t, H_CH) for t in parts[3:6])
    qc, kc, vc = (_to_heads(t, H_FOX) for t in parts[6:9])
    flog = jnp.pad(p[:, QKV_WIDTH + GATE_WIDTH:QKV_WIDTH + GATE_WIDTH + H_FOX].T, ((0, 8 - H_FOX), (0, 0)))
    fcum = _forget_cumsum(flog, f"{tag}_fcum")[:H_FOX]
    fq, fk = fcum.reshape(H_FOX, T, 1), fcum.reshape(H_FOX, T // QB, QB)
    kbp, vbp = _pad_keys(kb), _pad_keys(vb)
    oa = _sb_fwd(qa, ka, va, f"{tag}_sb")
    ob = _chunk_fwd(qb, kbp, vbp, lw["bias"], f"{tag}_ch")
    oc, lse = _fox_fwd(qc, kc, vc, fq, fk, f"{tag}_fox")
    oam, obm, ocm = _from_heads(oa), _from_heads(ob), _from_heads(oc)
    col, row = (lw["layer"], "col"), (lw["layer"], "row")
    ya = _mm(oam, lw["w_br_sb"], mode="nn", name=f"{tag}_bra", gathered=col)
    yb = _mm(obm, lw["w_br_ch"], mode="nn", name=f"{tag}_brb", gathered=col)
    yc = _mm(ocm, lw["w_br_fox"], mode="nn", name=f"{tag}_brc", gathered=col)
    merged = _gate_fwd(p, ya, yb, yc, f"{tag}_gate")
    y = _mm(merged, lw["w_out"], mode="nn", name=f"{tag}_out", res=x, gathered=row)
    saved = (h, p, (qa, ka, va), (qb, kbp, vbp), (qc, kc, vc), flog, fq, fk, lse, (oam, obm, ocm),
             (ya, yb, yc), merged)
    return y, saved


def _mixer_bwd(x, lw, saved, dy, tag):
    (h, p, (qa, ka, va), (qb, kbp, vbp), (qc, kc, vc), flog, fq, fk, lse, (oam, obm, ocm),
     (ya, yb, yc), merged) = saved
    T = x.shape[0]
    grads = {}
    col, row = (lw["layer"], "col"), (lw["layer"], "row")
    dmerged = _mm(dy, lw["w_out"], mode="nt", name=f"{tag}_dmerged", gathered=row)
    grads["w_out"] = _by_chip_rows(_mm(merged, dy, mode="tn", name=f"{tag}_dwout", tm_cap=512, out_dtype=BF16,
                                       out_split="row"))
    dya, dyb, dyc, dgate = _gate_bwd(p, ya, yb, yc, dmerged, f"{tag}_dgate")
    doa = _mm(dya, lw["w_br_sb"], mode="nt", name=f"{tag}_doa", gathered=col)
    dob = _mm(dyb, lw["w_br_ch"], mode="nt", name=f"{tag}_dob", gathered=col)
    doc = _mm(dyc, lw["w_br_fox"], mode="nt", name=f"{tag}_doc", gathered=col)
    by_chip = dict(mode="tn", tm_cap=512, out_dtype=BF16, out_split="col")
    grads["w_br_sb"] = _mm(oam, dya, name=f"{tag}_dwbra", **by_chip)
    grads["w_br_ch"] = _mm(obm, dyb, name=f"{tag}_dwbrb", **by_chip)
    grads["w_br_fox"] = _mm(ocm, dyc, name=f"{tag}_dwbrc", **by_chip)
    dqa, dka, dva = _sb_bwd(qa, ka, va, _to_heads(doa, H_SB), f"{tag}_dsb")
    dqb, dkbp, dvbp, dbias = _chunk_bwd(qb, kbp, vbp, lw["bias"], _to_heads(dob, H_CH), f"{tag}_dch")
    dqc, dkc, dvc, dfk = _fox_bwd(qc, kc, vc, fq, fk, lse, _to_heads(doc, H_FOX), f"{tag}_dfox")
    dflog = _forget_cumsum_bwd(flog, jnp.pad(dfk.reshape(H_FOX, T), ((0, 8 - H_FOX), (0, 0))), f"{tag}_dfcum")
    dqkv = [_from_heads(t) for t in (dqa, dka, dva, dqb, dkbp[:, BAND_PAD:], dvbp[:, BAND_PAD:], dqc, dkc, dvc)]
    dp = jnp.concatenate(dqkv + [dgate, jnp.pad(dflog[:H_FOX].T, ((0, 0), (0, F_PAD - H_FOX)))], axis=1)
    grads["b_in"] = _colsum(dp, f"{tag}_dbin")
    dpb = dp.astype(BF16)
    dw_in = _unpad_w_in(_mm(h, dpb, mode="tn", name=f"{tag}_dwin", tm_cap=512, tn_cap=896, out_dtype=BF16))
    grads["w_in"] = jnp.transpose(dw_in.reshape(2, D_MODEL // 2, N_CHIPS, IN_WIDTH // N_CHIPS), (0, 2, 1, 3))
    dh = _mm(dpb, lw["w_in"], mode="nt", name=f"{tag}_dh", tk_cap=896)
    dx, grads["g_mix"] = _rmsnorm_bwd(x, lw["g_mix"], dh, dy, f"{tag}_dnorm")
    grads["rel_bias"] = lw["bias_vjp"](dbias)[0]
    return dx, grads


def kernel(x, g_ffn1, w_ffn1_in, w_ffn1_out, g_mix, w_in, b_in, rel_bias, w_br_sb, w_br_ch, w_br_fox, w_out, g_ffn2, w_ffn2_in, w_ffn2_out, g_final, loss_target, m_g_ffn1, m_w_ffn1_in, m_w_ffn1_out, m_g_mix, m_w_in, m_b_in, m_rel_bias, m_w_br_sb, m_w_br_ch, m_w_br_fox, m_w_out, m_g_ffn2, m_w_ffn2_in, m_w_ffn2_out, m_g_final, v_g_ffn1, v_w_ffn1_in, v_w_ffn1_out, v_g_mix, v_w_in, v_b_in, v_rel_bias, v_w_br_sb, v_w_br_ch, v_w_br_fox, v_w_out, v_g_ffn2, v_w_ffn2_in, v_w_ffn2_out, v_g_final):
    names = ("g_ffn1", "w_ffn1_in", "w_ffn1_out", "g_mix", "w_in", "b_in", "rel_bias", "w_br_sb", "w_br_ch",
             "w_br_fox", "w_out", "g_ffn2", "w_ffn2_in", "w_ffn2_out", "g_final")
    w = dict(zip(names, (g_ffn1, w_ffn1_in, w_ffn1_out, g_mix, w_in, b_in, rel_bias, w_br_sb, w_br_ch,
                         w_br_fox, w_out, g_ffn2, w_ffn2_in, w_ffn2_out, g_final)))
    m = dict(zip(names, (m_g_ffn1, m_w_ffn1_in, m_w_ffn1_out, m_g_mix, m_w_in, m_b_in, m_rel_bias, m_w_br_sb,
                         m_w_br_ch, m_w_br_fox, m_w_out, m_g_ffn2, m_w_ffn2_in, m_w_ffn2_out, m_g_final)))
    v = dict(zip(names, (v_g_ffn1, v_w_ffn1_in, v_w_ffn1_out, v_g_mix, v_w_in, v_b_in, v_rel_bias, v_w_br_sb,
                         v_w_br_ch, v_w_br_fox, v_w_out, v_g_ffn2, v_w_ffn2_in, v_w_ffn2_out, v_g_final)))
    xs = x[0]
    tgt = loss_target[0]

    gathered = dict(zip(BIG, _gather_weights([w[n].astype(BF16) for n in BIG], "gather_weights")))
    layers = []
    for l in range(DEPTH):
        lw = dict(gathered)
        lw["layer"] = l
        lw["w_in"] = _pad_w_in(jnp.concatenate([gathered["w_in"][l, j] for j in range(N_CHIPS)], axis=1))
        lw["b_in"] = _pad_w_in(w["b_in"][l][None, :])
        lw["bias"], lw["bias_vjp"] = jax.vjp(_band_bias, w["rel_bias"][l])
        for n in ("g_ffn1", "g_mix", "g_ffn2"):
            lw[n] = w[n][l][None, :]
        layers.append(lw)

    saved = []
    act = xs
    for l, lw in enumerate(layers):
        x0 = act
        x1, s1 = _ffn_fwd(x0, lw["g_ffn1"], lw["w_ffn1_in"], lw["w_ffn1_out"], l, f"l{l}_ffn1")
        x2, s2 = _mixer_fwd(x1, lw, f"l{l}_mix")
        act, s3 = _ffn_fwd(x2, lw["g_ffn2"], lw["w_ffn2_in"], lw["w_ffn2_out"], l, f"l{l}_ffn2")
        saved.append((x0, s1, x1, s2, x2, s3))

    dact, dg_final, loss_row = _final_loss(act, w["g_final"][None, :], tgt, "final_loss")

    big_grads = {n: [None] * DEPTH for n in BIG}
    small_grads = {n: [None] * DEPTH for n in ("g_ffn1", "g_mix", "b_in", "rel_bias", "g_ffn2")}
    for l in reversed(range(DEPTH)):
        lw = layers[l]
        x0, s1, x1, s2, x2, s3 = saved[l]
        dx2, small_grads["g_ffn2"][l], big_grads["w_ffn2_in"][l], big_grads["w_ffn2_out"][l] = _ffn_bwd(
            x2, lw["g_ffn2"], lw["w_ffn2_in"], lw["w_ffn2_out"], l, s3, dact, f"l{l}_ffn2")
        dx1, mg = _mixer_bwd(x1, lw, s2, dx2, f"l{l}_mix")
        for n in ("w_in", "w_br_sb", "w_br_ch", "w_br_fox", "w_out"):
            big_grads[n][l] = mg[n]
        small_grads["b_in"][l] = _unpad_w_in(mg["b_in"])[0]
        small_grads["g_mix"][l] = mg["g_mix"][0]
        small_grads["rel_bias"][l] = mg["rel_bias"]
        dact, dg1, big_grads["w_ffn1_in"][l], big_grads["w_ffn1_out"][l] = _ffn_bwd(
            x0, lw["g_ffn1"], lw["w_ffn1_in"], lw["w_ffn1_out"], l, s1, dx1, f"l{l}_ffn1")
        small_grads["g_ffn1"][l] = dg1[0]
        small_grads["g_ffn2"][l] = small_grads["g_ffn2"][l][0]
    grad_x = dact[None]

    small = {n: jnp.stack(small_grads[n]) for n in small_grads}
    small["g_final"] = dg_final[0]
    small_sum, loss = _unpack_small(_allreduce_small(_pack_small(small, loss_row[0, :1]), "allreduce_small"),
                                    {n: w[n].shape for n in SMALL})

    order = [(n, l) for n in BIG for l in range(DEPTH)]
    summed = _reduce_scatter_grads([big_grads[n][l] for n, l in order],
                                   [1 if n in ROW_SHARDED else 0 for n, l in order])
    grads = {n: jnp.stack(summed[DEPTH * i:DEPTH * (i + 1)]) for i, n in enumerate(BIG)}
    grads.update(small_sum)

    delta, new_m, new_v = {}, {}, {}
    for n in BIG:
        shape = w[n].shape
        flat = lambda t: t.reshape(-1, shape[-1])
        d, nm, nv = _adamw(flat(w[n]), flat(grads[n]), flat(m[n]), flat(v[n]), f"adamw_{n}")
        delta[n], new_m[n], new_v[n] = d.reshape(shape), nm.reshape(shape), nv.reshape(shape)
    zero = jnp.zeros((1,), F32)
    sd, sm, sv = _adamw(_pack_small(w, zero), _pack_small(grads, zero), _pack_small(m, zero), _pack_small(v, zero),
                        "adamw_small")
    shapes = {n: w[n].shape for n in SMALL}
    for dst, src in ((delta, sd), (new_m, sm), (new_v, sv)):
        dst.update(_unpack_small(src, shapes)[0])

    return (loss, grad_x, *[grads[n] for n in names], *[delta[n] for n in names],
            *[new_m[n] for n in names], *[new_v[n] for n in names])
```

```python
import functools

import numpy as np
import jax
import jax.numpy as jnp
from jax import lax
from jax.experimental import pallas as pl
from jax.experimental.pallas import tpu as pltpu

F32 = jnp.float32
BF16 = jnp.bfloat16

D_MODEL = 1024
DEPTH = 2
CHUNK = 64
HEAD_DIM = 64
H_SB, H_CH, H_FOX = 4, 8, 4
W_SB, W_CH, W_FOX = H_SB * HEAD_DIM, H_CH * HEAD_DIM, H_FOX * HEAD_DIM
LEFT_CHUNKS = 8
MAX_REL = 128
N_REL = 2 * MAX_REL + 1
D_FF = 2816
QKV_WIDTH = 3 * (W_SB + W_CH + W_FOX)
GATE_WIDTH = 3 * D_MODEL
IN_WIDTH = QKV_WIDTH + H_FOX + GATE_WIDTH
F_PAD = 128
P_WIDTH = QKV_WIDTH + GATE_WIDTH + F_PAD
RMS_EPS = 1e-6
NEG = -1e30
SCALE = HEAD_DIM ** -0.5
QB = 256
BAND = (LEFT_CHUNKS + 2) * CHUNK
BAND_PAD = BAND - CHUNK

ADAM_LR, ADAM_B1, ADAM_B2, ADAM_EPS, ADAM_WD, ADAM_STEP = 0.001, 0.9, 0.999, 1e-08, 0.01, 10

N_CHIPS = 4
PACK_COLS = 1024
VMEM_BUDGET = 52 * 1024 * 1024

NT = (((1,), (1,)), ((), ()))
TN = (((0,), (0,)), ((), ()))
NN = (((1,), (0,)), ((), ()))
MESH = pl.DeviceIdType.MESH


def _pcall(body, **kw):
    return pl.pallas_call(body, **kw)


def _pick(n, cap, mult=128):
    best = None
    d = mult
    while d <= min(n, cap):
        if n % d == 0:
            best = d
        d += mult
    return n if best is None else best


def _nbytes(shape, dtype):
    return int(np.prod(shape)) * jnp.dtype(dtype).itemsize


def _mm(a, b, *, mode, name, out_dtype=F32, alpha=1.0, bias=None, res=None, tm_cap=1024, tn_cap=512,
        tk_cap=2816, gathered=None, out_split=None):
    if mode == "tn":
        K, M = a.shape
    else:
        M, K = a.shape
    dn = {"nn": NN, "nt": NT, "tn": TN}[mode]
    b_rows = None
    if gathered is None:
        N = b.shape[0] if mode == "nt" else b.shape[1]
        tn = {None: _pick(N, tn_cap), "col": N // N_CHIPS, "row": N // 2}[out_split]
        if out_split == "col":
            tm_cap = min(tm_cap, M // 2)
        tk = K if K <= tk_cap else _pick(K, tk_cap)
        b_spec = (pl.BlockSpec((tn, tk), lambda i, j, k: (j, k)) if mode == "nt"
                  else pl.BlockSpec((tk, tn), lambda i, j, k: (k, j)))
    else:
        layer, sharding = gathered
        r, c = b.shape[2:]
        rows, cols = (r, N_CHIPS * c) if sharding == "col" else (N_CHIPS * r, c)
        N = rows if mode == "nt" else cols
        assert K == (cols if mode == "nt" else rows), (name, K, rows, cols)
        if sharding == "col" and mode == "nn":
            tn, tk = c, (r if r <= tk_cap else _pick(r, tk_cap))
            b_spec = pl.BlockSpec((None, None, tk, c), lambda i, j, k: (layer, j, k, 0))
        elif sharding == "col":
            tn, tk = _pick(r, tn_cap), c
            b_spec = pl.BlockSpec((None, None, tn, c), lambda i, j, k: (layer, k, j, 0))
        elif mode == "nn":
            tn, tk, b_rows = _pick(c, tn_cap), K, N_CHIPS
            b_spec = pl.BlockSpec((None, N_CHIPS, r, tn), lambda i, j, k: (layer, 0, 0, j))
        else:
            tn, tk, b_rows = 2 * r, K, 2
            b_spec = pl.BlockSpec((None, 2, r, c), lambda i, j, k: (layer, j, 0, 0))
    tm = _pick(M, tm_cap)
    nk = K // tk

    a_spec = (pl.BlockSpec((tk, tm), lambda i, j, k: (k, i)) if mode == "tn"
              else pl.BlockSpec((tm, tk), lambda i, j, k: (i, k)))
    in_specs = [a_spec, b_spec]
    args = [a, b]
    if bias is not None:
        in_specs.append(pl.BlockSpec((1, tn), lambda i, j, k: (0, j)))
        args.append(bias)
    if res is not None:
        in_specs.append(pl.BlockSpec((tm, tn), lambda i, j, k: (i, j)))
        args.append(res)

    est = 2 * (_nbytes((tm, tk), a.dtype) + _nbytes((tk, tn), b.dtype) + _nbytes((tm, tn), out_dtype))
    est += _nbytes((tm, tn), F32) * (3 if res is not None else 1)
    assert est < VMEM_BUDGET, (name, est)

    def body(*refs):
        a_ref, b_ref = refs[0], refs[1]
        pos = 2
        bias_ref = res_ref = None
        if bias is not None:
            bias_ref = refs[pos]
            pos += 1
        if res is not None:
            res_ref = refs[pos]
            pos += 1
        o_ref = refs[pos]
        acc_ref = refs[pos + 1] if nk > 1 else None

        bv = b_ref[...]
        if b_rows is not None:
            bv = bv.reshape(b_rows * bv.shape[1], bv.shape[2])
        part = lax.dot_general(a_ref[...].astype(BF16), bv.astype(BF16), dn, preferred_element_type=F32)

        def finish(acc):
            if alpha != 1.0:
                acc = acc * alpha
            if bias_ref is not None:
                acc = acc + bias_ref[...]
            if res_ref is not None:
                acc = acc + res_ref[...]
            o_ref[...] = acc.astype(out_dtype)

        if nk == 1:
            finish(part)
        else:
            k = pl.program_id(2)

            @pl.when(k == 0)
            def _():
                acc_ref[...] = part

            @pl.when(k > 0)
            def _():
                acc_ref[...] += part

            @pl.when(k == nk - 1)
            def _():
                finish(acc_ref[...])

    if out_split == "col":
        per_half = M // 2 // tm
        out_spec = pl.BlockSpec((None, None, tm, tn), lambda i, j, k: (i // per_half, j, i % per_half, 0))
        out_shape = jax.ShapeDtypeStruct((2, N_CHIPS, M // 2, tn), out_dtype)
    elif out_split == "row":
        out_spec = pl.BlockSpec((None, tm, tn), lambda i, j, k: (j, i, 0))
        out_shape = jax.ShapeDtypeStruct((2, M, tn), out_dtype)
    else:
        out_spec = pl.BlockSpec((tm, tn), lambda i, j, k: (i, j))
        out_shape = jax.ShapeDtypeStruct((M, N), out_dtype)
    return _pcall(
        body, name=name, grid=(M // tm, N // tn, nk), in_specs=in_specs, out_specs=out_spec, out_shape=out_shape,
        scratch_shapes=[pltpu.VMEM((tm, tn), F32)] if nk > 1 else [],
        compiler_params=pltpu.CompilerParams(dimension_semantics=("parallel", "parallel", "arbitrary")),
    )(*args)


def _rmsnorm_fwd(x, g, name):
    T, Dm = x.shape
    tm = _pick(T, 256, 8)

    def body(x_ref, g_ref, h_ref):
        xv = x_ref[...]
        rstd = lax.rsqrt(jnp.mean(xv * xv, axis=-1, keepdims=True) + RMS_EPS)
        h_ref[...] = (xv * rstd * g_ref[...]).astype(BF16)

    return _pcall(
        body, name=name, grid=(T // tm,),
        in_specs=[pl.BlockSpec((tm, Dm), lambda i: (i, 0)), pl.BlockSpec((1, Dm), lambda i: (0, 0))],
        out_specs=pl.BlockSpec((tm, Dm), lambda i: (i, 0)),
        out_shape=jax.ShapeDtypeStruct((T, Dm), BF16),
    )(x, g)


def _rmsnorm_bwd(x, g, dh, dres, name):
    T, Dm = x.shape
    tm = _pick(T, 256, 8)

    def body(x_ref, g_ref, dh_ref, dres_ref, dx_ref, dg_ref):
        xv = x_ref[...]
        rstd = lax.rsqrt(jnp.mean(xv * xv, axis=-1, keepdims=True) + RMS_EPS)
        xhat = xv * rstd
        dhv = dh_ref[...]
        dyg = dhv * g_ref[...]
        dx_ref[...] = dres_ref[...] + rstd * (dyg - xhat * jnp.mean(dyg * xhat, axis=-1, keepdims=True))
        part = jnp.sum(dhv * xhat, axis=0, keepdims=True)

        @pl.when(pl.program_id(0) == 0)
        def _():
            dg_ref[...] = part

        @pl.when(pl.program_id(0) > 0)
        def _():
            dg_ref[...] += part

    row = pl.BlockSpec((tm, Dm), lambda i: (i, 0))
    vec = pl.BlockSpec((1, Dm), lambda i: (0, 0))
    return _pcall(
        body, name=name, grid=(T // tm,), in_specs=[row, vec, row, row], out_specs=[row, vec],
        out_shape=[jax.ShapeDtypeStruct((T, Dm), F32), jax.ShapeDtypeStruct((1, Dm), F32)],
        compiler_params=pltpu.CompilerParams(dimension_semantics=("arbitrary",)),
    )(x, g, dh, dres)


def _final_loss(x, g, tgt, name):
    T, Dm = x.shape
    tm = _pick(T, 256, 8)

    def body(x_ref, g_ref, t_ref, dx_ref, dg_ref, loss_ref):
        xv = x_ref[...]
        gv = g_ref[...]
        rstd = lax.rsqrt(jnp.mean(xv * xv, axis=-1, keepdims=True) + RMS_EPS)
        xhat = xv * rstd
        err = xhat * gv - t_ref[...]
        part_loss = 0.5 * jnp.sum(jnp.mean(err * err, axis=-1, keepdims=True), axis=0, keepdims=True)
        dy = err * (1.0 / Dm)
        dyg = dy * gv
        dx_ref[...] = rstd * (dyg - xhat * jnp.mean(dyg * xhat, axis=-1, keepdims=True))
        part_g = jnp.sum(dy * xhat, axis=0, keepdims=True)
        part_l = jnp.broadcast_to(part_loss, (1, 128))

        @pl.when(pl.program_id(0) == 0)
        def _():
            dg_ref[...] = part_g
            loss_ref[...] = part_l

        @pl.when(pl.program_id(0) > 0)
        def _():
            dg_ref[...] += part_g
            loss_ref[...] += part_l

    row = pl.BlockSpec((tm, Dm), lambda i: (i, 0))
    vec = pl.BlockSpec((1, Dm), lambda i: (0, 0))
    return _pcall(
        body, name=name, grid=(T // tm,), in_specs=[row, vec, row],
        out_specs=[row, vec, pl.BlockSpec((1, 128), lambda i: (0, 0))],
        out_shape=[jax.ShapeDtypeStruct((T, Dm), F32), jax.ShapeDtypeStruct((1, Dm), F32),
                   jax.ShapeDtypeStruct((1, 128), F32)],
        compiler_params=pltpu.CompilerParams(dimension_semantics=("arbitrary",)),
    )(x, g, tgt)


def _sigmoid(z):
    return 1.0 / (1.0 + jnp.exp(-z))


def _swiglu_fwd(gu, name):
    T = gu.shape[0]
    tm = _pick(T, 128, 8)

    def body(gu_ref, a_ref):
        gv = gu_ref[:, :D_FF]
        uv = gu_ref[:, D_FF:]
        a_ref[...] = (gv * _sigmoid(gv) * uv).astype(BF16)

    return _pcall(
        body, name=name, grid=(T // tm,), in_specs=[pl.BlockSpec((tm, 2 * D_FF), lambda i: (i, 0))],
        out_specs=pl.BlockSpec((tm, D_FF), lambda i: (i, 0)),
        out_shape=jax.ShapeDtypeStruct((T, D_FF), BF16),
    )(gu)


def _swiglu_bwd(gu, da, name):
    T = gu.shape[0]
    tm = _pick(T, 128, 8)

    def body(gu_ref, da_ref, d_ref):
        gv = gu_ref[:, :D_FF]
        uv = gu_ref[:, D_FF:]
        dav = da_ref[...]
        sg = _sigmoid(gv)
        d_ref[:, :D_FF] = (dav * uv * (sg + gv * sg * (1.0 - sg))).astype(BF16)
        d_ref[:, D_FF:] = (dav * gv * sg).astype(BF16)

    return _pcall(
        body, name=name, grid=(T // tm,),
        in_specs=[pl.BlockSpec((tm, 2 * D_FF), lambda i: (i, 0)), pl.BlockSpec((tm, D_FF), lambda i: (i, 0))],
        out_specs=pl.BlockSpec((tm, 2 * D_FF), lambda i: (i, 0)),
        out_shape=jax.ShapeDtypeStruct((T, 2 * D_FF), BF16),
    )(gu, da)


GATE_BLOCK0 = QKV_WIDTH // D_MODEL


def _gate_fwd(p, ya, yb, yc, name):
    T = p.shape[0]
    tm = _pick(T, 256, 8)

    def body(ga_ref, gb_ref, gc_ref, ya_ref, yb_ref, yc_ref, o_ref):
        o_ref[...] = (_sigmoid(ga_ref[...]) * ya_ref[...] + _sigmoid(gb_ref[...]) * yb_ref[...]
                      + _sigmoid(gc_ref[...]) * yc_ref[...]).astype(BF16)

    gate = [pl.BlockSpec((tm, D_MODEL), functools.partial(lambda i, kk: (i, GATE_BLOCK0 + kk), kk=kk))
            for kk in range(3)]
    row = pl.BlockSpec((tm, D_MODEL), lambda i: (i, 0))
    return _pcall(
        body, name=name, grid=(T // tm,), in_specs=gate + [row, row, row], out_specs=row,
        out_shape=jax.ShapeDtypeStruct((T, D_MODEL), BF16),
    )(p, p, p, ya, yb, yc)


def _gate_bwd(p, ya, yb, yc, dm, name):
    T = p.shape[0]
    tm = _pick(T, 256, 8)

    def body(ga_ref, gb_ref, gc_ref, ya_ref, yb_ref, yc_ref, dm_ref, da_ref, db_ref, dc_ref, dg_ref):
        dmv = dm_ref[...]
        for kk, (g_ref, y_ref, d_ref) in enumerate(((ga_ref, ya_ref, da_ref), (gb_ref, yb_ref, db_ref),
                                                    (gc_ref, yc_ref, dc_ref))):
            s = _sigmoid(g_ref[...])
            d_ref[...] = (s * dmv).astype(BF16)
            dg_ref[:, kk * D_MODEL:(kk + 1) * D_MODEL] = dmv * y_ref[...] * s * (1.0 - s)

    gate = [pl.BlockSpec((tm, D_MODEL), functools.partial(lambda i, kk: (i, GATE_BLOCK0 + kk), kk=kk))
            for kk in range(3)]
    row = pl.BlockSpec((tm, D_MODEL), lambda i: (i, 0))
    return _pcall(
        body, name=name, grid=(T // tm,), in_specs=gate + [row, row, row, row],
        out_specs=[row, row, row, pl.BlockSpec((tm, GATE_WIDTH), lambda i: (i, 0))],
        out_shape=[jax.ShapeDtypeStruct((T, D_MODEL), BF16)] * 3 + [jax.ShapeDtypeStruct((T, GATE_WIDTH), F32)],
    )(p, p, p, ya, yb, yc, dm)


def _colsum(a, name):
    T, N = a.shape
    tm = _pick(T, 256, 8)

    def body(a_ref, o_ref):
        part = jnp.sum(a_ref[...], axis=0, keepdims=True)

        @pl.when(pl.program_id(0) == 0)
        def _():
            o_ref[...] = part

        @pl.when(pl.program_id(0) > 0)
        def _():
            o_ref[...] += part

    return _pcall(
        body, name=name, grid=(T // tm,), in_specs=[pl.BlockSpec((tm, N), lambda i: (i, 0))],
        out_specs=pl.BlockSpec((1, N), lambda i: (0, 0)), out_shape=jax.ShapeDtypeStruct((1, N), F32),
        compiler_params=pltpu.CompilerParams(dimension_semantics=("arbitrary",)),
    )(a)


def _adamw(w, g, m, v, name):
    R, C = w.shape
    tr = 256 if R % 256 == 0 else R
    c1 = 1.0 / (1.0 - ADAM_B1 ** ADAM_STEP)
    c2 = 1.0 / (1.0 - ADAM_B2 ** ADAM_STEP)

    def body(w_ref, g_ref, m_ref, v_ref, d_ref, nm_ref, nv_ref):
        gv = g_ref[...]
        mn = ADAM_B1 * m_ref[...] + (1.0 - ADAM_B1) * gv
        vn = ADAM_B2 * v_ref[...] + (1.0 - ADAM_B2) * (gv * gv)
        nm_ref[...] = mn
        nv_ref[...] = vn
        d_ref[...] = -ADAM_LR * ((mn * c1) / (jnp.sqrt(vn * c2) + ADAM_EPS) + ADAM_WD * w_ref[...])

    spec = pl.BlockSpec((tr, C), lambda i: (i, 0))
    return _pcall(
        body, name=name, grid=(R // tr,), in_specs=[spec] * 4, out_specs=[spec] * 3,
        out_shape=[jax.ShapeDtypeStruct((R, C), F32)] * 3,
    )(w, g, m, v)


def _log_sigmoid(z):
    return jnp.minimum(z, 0.0) - jnp.log(1.0 + jnp.exp(-jnp.abs(z)))


def _dot3(x, m01):
    x1 = x.astype(BF16)
    r1 = x - x1.astype(F32)
    x2 = r1.astype(BF16)
    x3 = (r1 - x2.astype(F32)).astype(BF16)
    n = x.shape[0]
    if n % 16:
        return (jnp.dot(x1, m01, preferred_element_type=F32) + jnp.dot(x2, m01, preferred_element_type=F32)
                + jnp.dot(x3, m01, preferred_element_type=F32))
    y = jnp.dot(jnp.concatenate([x1, x2, x3], axis=0), m01, preferred_element_type=F32)
    return y[:n] + y[n:2 * n] + y[2 * n:]


def _dot_nt(a, b):
    return lax.dot_general(a, b, NT, preferred_element_type=F32)


def _dot_tn(a, b):
    return lax.dot_general(a, b, TN, preferred_element_type=F32)


def _iota2(shape):
    return lax.broadcasted_iota(jnp.int32, shape, 0), lax.broadcasted_iota(jnp.int32, shape, 1)


HEADS_PER_STEP = 4
HEADS = range(HEADS_PER_STEP)


CHUNK_HEADS_PER_STEP = 2
CHUNK_HEADS = range(CHUNK_HEADS_PER_STEP)


def _head_spec(T, rows=None, width=HEAD_DIM, heads=HEADS_PER_STEP):
    return pl.BlockSpec((heads, T if rows is None else rows, width), lambda g: (g, 0, 0))


def _sb_weights(qb, kb, t0, s0, racc, m_gt, row, col):
    z = _dot_nt(qb, kb) * SCALE
    strict = (s0 + col) < (t0 + row)
    lb = _log_sigmoid(z)
    lf = jnp.where(strict, lb - z, 0.0)
    between = _dot3(lf, m_gt) + racc
    w = jnp.where(strict, jnp.exp(lb + between), 0.0)
    return strict, lb, lf, w


def _sb_fwd(q, k, v, name):
    H, T, _ = q.shape
    nb = T // QB

    def body(q_ref, k_ref, v_ref, o_ref):
        row, col = _iota2((QB, QB))
        m_gt = (row > col).astype(BF16)

        def qblock(i, _):
            t0 = pl.multiple_of(i * QB, QB)
            qbs = [q_ref[h, pl.ds(t0, QB), :].astype(BF16) for h in HEADS]

            def kblock(jj, carry):
                s0 = pl.multiple_of((i - jj) * QB, QB)
                out = []
                for h in HEADS:
                    acc, racc = carry[h]
                    kb = k_ref[h, pl.ds(s0, QB), :].astype(BF16)
                    vb = v_ref[h, pl.ds(s0, QB), :].astype(BF16)
                    _, _, lf, w = _sb_weights(qbs[h], kb, t0, s0, racc, m_gt, row, col)
                    acc = acc + jnp.dot(w.astype(BF16), vb, preferred_element_type=F32)
                    out.append((acc, racc + jnp.sum(lf, axis=1, keepdims=True)))
                return tuple(out)

            res = lax.fori_loop(0, i + 1, kblock,
                                tuple((jnp.zeros((QB, HEAD_DIM), F32), jnp.zeros((QB, 1), F32)) for _ in HEADS))
            for h in HEADS:
                o_ref[h, pl.ds(t0, QB), :] = res[h][0]
            return 0

        lax.fori_loop(0, nb, qblock, 0)

    spec = _head_spec(T)
    return _pcall(body, name=name, grid=(H // HEADS_PER_STEP,), in_specs=[spec] * 3, out_specs=spec,
                  out_shape=jax.ShapeDtypeStruct((H, T, HEAD_DIM), F32))(q, k, v)


def _sb_bwd(q, k, v, do, name):
    H, T, _ = q.shape
    nb = T // QB

    def body(q_ref, k_ref, v_ref, do_ref, dq_ref, dk_ref, dv_ref, racc_ref):
        row, col = _iota2((QB, QB))
        m_gt = (row > col).astype(BF16)
        m_lt = (row < col).astype(BF16)
        dk_ref[...] = jnp.zeros_like(dk_ref)
        dv_ref[...] = jnp.zeros_like(dv_ref)

        def qblock(i, _):
            t0 = pl.multiple_of(i * QB, QB)
            qbs = [q_ref[h, pl.ds(t0, QB), :].astype(BF16) for h in HEADS]
            dobs = [do_ref[h, pl.ds(t0, QB), :].astype(BF16) for h in HEADS]

            def suffix(jj, raccs):
                j = i - jj
                s0 = pl.multiple_of(j * QB, QB)
                out = []
                for h in HEADS:
                    kb = k_ref[h, pl.ds(s0, QB), :].astype(BF16)
                    z = _dot_nt(qbs[h], kb) * SCALE
                    lf = jnp.where((s0 + col) < (t0 + row), _log_sigmoid(z) - z, 0.0)
                    racc_ref[h, j] = raccs[h]
                    out.append(raccs[h] + jnp.sum(lf, axis=1, keepdims=True))
                return tuple(out)

            lax.fori_loop(0, i + 1, suffix, tuple(jnp.zeros((QB, 1), F32) for _ in HEADS))

            def kblock(j, carry):
                s0 = pl.multiple_of(j * QB, QB)
                out = []
                for h in HEADS:
                    dq, cacc = carry[h]
                    kb = k_ref[h, pl.ds(s0, QB), :].astype(BF16)
                    vb = v_ref[h, pl.ds(s0, QB), :].astype(BF16)
                    strict, lb, _, w = _sb_weights(qbs[h], kb, t0, s0, racc_ref[h, j], m_gt, row, col)
                    e = _dot_nt(dobs[h], vb) * w
                    c_left = _dot3(e, m_lt) + cacc
                    sig = jnp.exp(lb)
                    dz = jnp.where(strict, e * (1.0 - sig) - c_left * sig, 0.0).astype(BF16)
                    dq = dq + jnp.dot(dz, kb, preferred_element_type=F32)
                    dk_ref[h, pl.ds(s0, QB), :] += _dot_tn(dz, qbs[h]) * SCALE
                    dv_ref[h, pl.ds(s0, QB), :] += _dot_tn(w.astype(BF16), dobs[h])
                    out.append((dq, cacc + jnp.sum(e, axis=1, keepdims=True)))
                return tuple(out)

            res = lax.fori_loop(0, i + 1, kblock,
                                tuple((jnp.zeros((QB, HEAD_DIM), F32), jnp.zeros((QB, 1), F32)) for _ in HEADS))
            for h in HEADS:
                dq_ref[h, pl.ds(t0, QB), :] = res[h][0] * SCALE
            return 0

        lax.fori_loop(0, nb, qblock, 0)

    spec = _head_spec(T)
    return _pcall(body, name=name, grid=(H // HEADS_PER_STEP,), in_specs=[spec] * 4, out_specs=[spec] * 3,
                  out_shape=[jax.ShapeDtypeStruct((H, T, HEAD_DIM), F32)] * 3,
                  scratch_shapes=[pltpu.VMEM((HEADS_PER_STEP, nb, QB, 1), F32)])(q, k, v, do)


def _fox_logits(qb, kb, fq, fk, t0, s0, row, col):
    z = _dot_nt(qb, kb) * SCALE + fq - fk
    return jnp.where((s0 + col) <= (t0 + row), z, NEG)


def _fox_specs(T):
    return _head_spec(T, width=1), _head_spec(T, rows=T // QB, width=QB)


def _fox_fwd(q, k, v, fq, fk, name):
    H, T, _ = q.shape
    nb = T // QB

    def body(q_ref, k_ref, v_ref, fq_ref, fk_ref, o_ref, lse_ref):
        row, col = _iota2((QB, QB))

        def qblock(i, _):
            t0 = pl.multiple_of(i * QB, QB)
            qbs = [q_ref[h, pl.ds(t0, QB), :].astype(BF16) for h in HEADS]
            fqs = [fq_ref[h, pl.ds(t0, QB), :] for h in HEADS]

            def kblock(j, carry):
                s0 = pl.multiple_of(j * QB, QB)
                out = []
                for h in HEADS:
                    acc, m, l = carry[h]
                    kb = k_ref[h, pl.ds(s0, QB), :].astype(BF16)
                    vb = v_ref[h, pl.ds(s0, QB), :].astype(BF16)
                    z = _fox_logits(qbs[h], kb, fqs[h], fk_ref[h, pl.ds(j, 1), :], t0, s0, row, col)
                    m_new = jnp.maximum(m, jnp.max(z, axis=1, keepdims=True))
                    a = jnp.exp(m - m_new)
                    p = jnp.exp(z - m_new)
                    acc = a * acc + jnp.dot(p.astype(BF16), vb, preferred_element_type=F32)
                    out.append((acc, m_new, a * l + jnp.sum(p, axis=1, keepdims=True)))
                return tuple(out)

            res = lax.fori_loop(0, i + 1, kblock,
                                tuple((jnp.zeros((QB, HEAD_DIM), F32), jnp.full((QB, 1), NEG, F32),
                                       jnp.zeros((QB, 1), F32)) for _ in HEADS))
            for h in HEADS:
                acc, m, l = res[h]
                o_ref[h, pl.ds(t0, QB), :] = acc / l
                lse_ref[h, pl.ds(t0, QB), :] = m + jnp.log(l)
            return 0

        lax.fori_loop(0, nb, qblock, 0)

    spec = _head_spec(T)
    fq_spec, fk_spec = _fox_specs(T)
    return _pcall(body, name=name, grid=(H // HEADS_PER_STEP,), in_specs=[spec] * 3 + [fq_spec, fk_spec],
                  out_specs=[spec, fq_spec],
                  out_shape=[jax.ShapeDtypeStruct((H, T, HEAD_DIM), F32), jax.ShapeDtypeStruct((H, T, 1), F32)],
                  )(q, k, v, fq, fk)


def _fox_bwd(q, k, v, fq, fk, lse, do, name):
    H, T, _ = q.shape
    nb = T // QB

    def body(q_ref, k_ref, v_ref, fq_ref, fk_ref, lse_ref, do_ref, dq_ref, dk_ref, dv_ref, dfk_ref):
        row, col = _iota2((QB, QB))
        dk_ref[...] = jnp.zeros_like(dk_ref)
        dv_ref[...] = jnp.zeros_like(dv_ref)
        dfk_ref[...] = jnp.zeros_like(dfk_ref)

        def qblock(i, _):
            t0 = pl.multiple_of(i * QB, QB)
            qbs = [q_ref[h, pl.ds(t0, QB), :].astype(BF16) for h in HEADS]
            fqs = [fq_ref[h, pl.ds(t0, QB), :] for h in HEADS]
            lses = [lse_ref[h, pl.ds(t0, QB), :] for h in HEADS]
            dobs = [do_ref[h, pl.ds(t0, QB), :].astype(BF16) for h in HEADS]

            def probs(h, j):
                s0 = pl.multiple_of(j * QB, QB)
                kb = k_ref[h, pl.ds(s0, QB), :].astype(BF16)
                vb = v_ref[h, pl.ds(s0, QB), :].astype(BF16)
                z = _fox_logits(qbs[h], kb, fqs[h], fk_ref[h, pl.ds(j, 1), :], t0, s0, row, col)
                return s0, kb, jnp.exp(z - lses[h]), _dot_nt(dobs[h], vb)

            def row_dot(j, deltas):
                out = []
                for h in HEADS:
                    _, _, p, dp = probs(h, j)
                    out.append(deltas[h] + jnp.sum(p * dp, axis=1, keepdims=True))
                return tuple(out)

            deltas = lax.fori_loop(0, i + 1, row_dot, tuple(jnp.zeros((QB, 1), F32) for _ in HEADS))

            def kblock(j, dqs):
                out = []
                for h in HEADS:
                    s0, kb, p, dp = probs(h, j)
                    dz = p * (dp - deltas[h])
                    dzb = dz.astype(BF16)
                    dk_ref[h, pl.ds(s0, QB), :] += _dot_tn(dzb, qbs[h]) * SCALE
                    dv_ref[h, pl.ds(s0, QB), :] += _dot_tn(p.astype(BF16), dobs[h])
                    dfk_ref[h, pl.ds(j, 1), :] += -jnp.sum(dz, axis=0, keepdims=True)
                    out.append(dqs[h] + jnp.dot(dzb, kb, preferred_element_type=F32))
                return tuple(out)

            dqs = lax.fori_loop(0, i + 1, kblock, tuple(jnp.zeros((QB, HEAD_DIM), F32) for _ in HEADS))
            for h in HEADS:
                dq_ref[h, pl.ds(t0, QB), :] = dqs[h] * SCALE
            return 0

        lax.fori_loop(0, nb, qblock, 0)

    spec = _head_spec(T)
    fq_spec, fk_spec = _fox_specs(T)
    return _pcall(body, name=name, grid=(H // HEADS_PER_STEP,),
                  in_specs=[spec] * 3 + [fq_spec, fk_spec, fq_spec, spec],
                  out_specs=[spec, spec, spec, fk_spec],
                  out_shape=[jax.ShapeDtypeStruct((H, T, HEAD_DIM), F32)] * 3
                  + [jax.ShapeDtypeStruct((H, nb, QB), F32)],
                  )(q, k, v, fq, fk, lse, do)


def _forget_cumsum(flog, name):
    R, T = flog.shape
    nb = T // QB

    def body(x_ref, o_ref):
        row, col = _iota2((QB, QB))
        m_le = (row <= col).astype(BF16)
        carry = jnp.zeros((R, 1), F32)
        for b in range(nb):
            lf = _log_sigmoid(x_ref[:, b * QB:(b + 1) * QB])
            o_ref[:, b * QB:(b + 1) * QB] = _dot3(lf, m_le) + carry
            carry = carry + jnp.sum(lf, axis=1, keepdims=True)

    spec = pl.BlockSpec((R, T), lambda: (0, 0))
    return _pcall(body, name=name, in_specs=[spec], out_specs=spec,
                  out_shape=jax.ShapeDtypeStruct((R, T), F32))(flog)


def _forget_cumsum_bwd(flog, df, name):
    R, T = flog.shape
    nb = T // QB

    def body(x_ref, df_ref, o_ref):
        row, col = _iota2((QB, QB))
        m_ge = (row >= col).astype(BF16)
        carry = jnp.zeros((R, 1), F32)
        for b in reversed(range(nb)):
            dfb = df_ref[:, b * QB:(b + 1) * QB]
            dlog = _dot3(dfb, m_ge) + carry
            o_ref[:, b * QB:(b + 1) * QB] = dlog * _sigmoid(-x_ref[:, b * QB:(b + 1) * QB])
            carry = carry + jnp.sum(dfb, axis=1, keepdims=True)

    spec = pl.BlockSpec((R, T), lambda: (0, 0))
    return _pcall(body, name=name, in_specs=[spec, spec], out_specs=spec,
                  out_shape=jax.ShapeDtypeStruct((R, T), F32))(flog, df)


def _chunk_probs(qc, kb, bias, c, col):
    z = _dot_nt(qc, kb) * SCALE + bias
    z = jnp.where((c - (LEFT_CHUNKS + 1)) * CHUNK + col >= jnp.maximum(0, (c - LEFT_CHUNKS) * CHUNK), z, NEG)
    p = jnp.exp(z - jnp.max(z, axis=1, keepdims=True))
    return p, jnp.sum(p, axis=1, keepdims=True)


def _chunk_specs(T):
    n = CHUNK_HEADS_PER_STEP
    return (_head_spec(T, heads=n), _head_spec(T, rows=T + BAND_PAD, heads=n),
            _head_spec(T, rows=CHUNK, width=BAND, heads=n))


def _chunk_fwd(q, kp, vp, bias, name):
    H, T, _ = q.shape
    nc = T // CHUNK

    def body(q_ref, kp_ref, vp_ref, bias_ref, o_ref):
        _, col = _iota2((CHUNK, BAND))

        def chunk(c, _):
            t0 = pl.multiple_of(c * CHUNK, CHUNK)
            for h in CHUNK_HEADS:
                qc = q_ref[h, pl.ds(t0, CHUNK), :].astype(BF16)
                kb = kp_ref[h, pl.ds(t0, BAND), :].astype(BF16)
                vb = vp_ref[h, pl.ds(t0, BAND), :].astype(BF16)
                p, l = _chunk_probs(qc, kb, bias_ref[h], c, col)
                o_ref[h, pl.ds(t0, CHUNK), :] = jnp.dot(p.astype(BF16), vb, preferred_element_type=F32) / l
            return 0

        lax.fori_loop(0, nc, chunk, 0)

    q_spec, kp_spec, b_spec = _chunk_specs(T)
    return _pcall(body, name=name, grid=(H // CHUNK_HEADS_PER_STEP,), in_specs=[q_spec, kp_spec, kp_spec, b_spec],
                  out_specs=q_spec,
                  out_shape=jax.ShapeDtypeStruct((H, T, HEAD_DIM), F32))(q, kp, vp, bias)


def _chunk_bwd(q, kp, vp, bias, do, name):
    H, T, _ = q.shape
    nc = T // CHUNK

    def body(q_ref, kp_ref, vp_ref, bias_ref, do_ref, dq_ref, dkp_ref, dvp_ref, db_ref):
        _, col = _iota2((CHUNK, BAND))
        dkp_ref[...] = jnp.zeros_like(dkp_ref)
        dvp_ref[...] = jnp.zeros_like(dvp_ref)
        db_ref[...] = jnp.zeros_like(db_ref)

        def chunk(c, _):
            t0 = pl.multiple_of(c * CHUNK, CHUNK)
            for h in CHUNK_HEADS:
                qc = q_ref[h, pl.ds(t0, CHUNK), :].astype(BF16)
                kb = kp_ref[h, pl.ds(t0, BAND), :].astype(BF16)
                vb = vp_ref[h, pl.ds(t0, BAND), :].astype(BF16)
                dob = do_ref[h, pl.ds(t0, CHUNK), :].astype(BF16)
                p, l = _chunk_probs(qc, kb, bias_ref[h], c, col)
                p = p / l
                dp = _dot_nt(dob, vb)
                dz = p * (dp - jnp.sum(p * dp, axis=1, keepdims=True))
                dzb = dz.astype(BF16)
                dq_ref[h, pl.ds(t0, CHUNK), :] = jnp.dot(dzb, kb, preferred_element_type=F32) * SCALE
                dkp_ref[h, pl.ds(t0, BAND), :] += _dot_tn(dzb, qc) * SCALE
                dvp_ref[h, pl.ds(t0, BAND), :] += _dot_tn(p.astype(BF16), dob)
                db_ref[h] += dz
            return 0

        lax.fori_loop(0, nc, chunk, 0)

    q_spec, kp_spec, b_spec = _chunk_specs(T)
    return _pcall(body, name=name, grid=(H // CHUNK_HEADS_PER_STEP,),
                  in_specs=[q_spec, kp_spec, kp_spec, b_spec, q_spec],
                  out_specs=[q_spec, kp_spec, kp_spec, b_spec],
                  out_shape=[jax.ShapeDtypeStruct((H, T, HEAD_DIM), F32),
                             jax.ShapeDtypeStruct((H, T + BAND_PAD, HEAD_DIM), F32),
                             jax.ShapeDtypeStruct((H, T + BAND_PAD, HEAD_DIM), F32),
                             jax.ShapeDtypeStruct((H, CHUNK, BAND), F32)])(q, kp, vp, bias, do)


HBM_SPEC = pl.BlockSpec(memory_space=pltpu.HBM)


def _position():
    return lax.axis_index("x"), lax.axis_index("y"), lax.axis_index("c")


def _other_chips(x, y):
    chips = [(1 - x, y), (x, 1 - y), (1 - x, 1 - y)]
    return [(chip, 2 * chip[0] + chip[1]) for chip in chips]


def _remote_copy(src, dst, send_sems, recv_sems, k, to):
    return pltpu.make_async_remote_copy(src_ref=src, dst_ref=dst, send_sem=send_sems.at[k], recv_sem=recv_sems.at[k],
                                        device_id=to, device_id_type=MESH)


def _gather_weights(shards, name):
    n = len(shards)

    def body(*refs):
        src, dst = refs[:n], refs[n:2 * n]
        send_sems, recv_sems = refs[2 * n:]
        x, y, c = _position()
        me = 2 * x + y
        sibling = (x, y, 1 - c)
        others = _other_chips(x, y)
        first = [_remote_copy(src[i].at[c], dst[i].at[c, me], send_sems, recv_sems, 3 * i + k, (*chip, c))
                 for i in range(n) for k, (chip, _) in enumerate(others)]
        for cp in first:
            cp.start()
        passed = []
        for i in range(n):
            for k, (_, idx) in enumerate(others):
                landed = dst[i].at[c, idx]
                _remote_copy(landed, landed, send_sems, recv_sems, 3 * i + k, sibling).wait_recv()
                passed.append(_remote_copy(landed, landed, send_sems, recv_sems, 3 * (n + i) + k, sibling))
                passed[-1].start()
        for i in range(n):
            for k, (_, idx) in enumerate(others):
                from_sibling = dst[i].at[1 - c, idx]
                _remote_copy(from_sibling, from_sibling, send_sems, recv_sems, 3 * (n + i) + k, sibling).wait_recv()
        for cp in first + passed:
            cp.wait_send()

    return _pcall(
        body, name=name, in_specs=[HBM_SPEC] * n, out_specs=[HBM_SPEC] * n,
        out_shape=[jax.ShapeDtypeStruct((DEPTH, N_CHIPS) + s.shape[1:], s.dtype) for s in shards],
        scratch_shapes=[pltpu.SemaphoreType.DMA((6 * n,)), pltpu.SemaphoreType.DMA((6 * n,))],
    )(*shards)


def _send_other_half(parts, name):
    n = len(parts)

    def body(*refs):
        src, got = refs[:n], refs[n:2 * n]
        send_sems, recv_sems = refs[2 * n:]
        x, y, c = _position()
        copies = [_remote_copy(src[i].at[1 - c], got[i], send_sems, recv_sems, i, (x, y, 1 - c)) for i in range(n)]
        for cp in copies:
            cp.start()
        for cp in copies:
            cp.wait()

    return _pcall(
        body, name=name, in_specs=[HBM_SPEC] * n, out_specs=[HBM_SPEC] * n,
        out_shape=[jax.ShapeDtypeStruct(p.shape[1:], p.dtype) for p in parts],
        scratch_shapes=[pltpu.SemaphoreType.DMA((n,)), pltpu.SemaphoreType.DMA((n,))],
    )(*parts)


def _scatter_chips(parts, name):
    n = len(parts)

    def body(*refs):
        src, dst = refs[:n], refs[n:2 * n]
        send_sems, recv_sems = refs[2 * n:]
        x, y, c = _position()
        others = _other_chips(x, y)
        sends = [_remote_copy(src[i].at[idx], dst[i].at[k], send_sems, recv_sems, 3 * i + k, (*chip, c))
                 for i in range(n) for k, (chip, idx) in enumerate(others)]
        for cp in sends:
            cp.start()
        for cp in sends:
            cp.wait()

    return _pcall(
        body, name=name, in_specs=[HBM_SPEC] * n, out_specs=[HBM_SPEC] * n,
        out_shape=[jax.ShapeDtypeStruct((3,) + p.shape[1:], p.dtype) for p in parts],
        scratch_shapes=[pltpu.SemaphoreType.DMA((3 * n,)), pltpu.SemaphoreType.DMA((3 * n,))],
    )(*parts)


def _swap_with_sibling(arrs, name):
    n = len(arrs)

    def body(*refs):
        src, dst = refs[:n], refs[n:2 * n]
        send_sems, recv_sems = refs[2 * n:]
        x, y, c = _position()
        copies = [_remote_copy(src[i], dst[i], send_sems, recv_sems, i, (x, y, 1 - c)) for i in range(n)]
        for cp in copies:
            cp.start()
        for cp in copies:
            cp.wait()

    return _pcall(
        body, name=name, in_specs=[HBM_SPEC] * n, out_specs=[HBM_SPEC] * n,
        out_shape=[jax.ShapeDtypeStruct(a.shape, a.dtype) for a in arrs],
        scratch_shapes=[pltpu.SemaphoreType.DMA((n,)), pltpu.SemaphoreType.DMA((n,))],
    )(*arrs)


def _allreduce_small(v, name):
    R, C = v.shape

    def body(v_ref, o_ref, slots, send_sems, recv_sems):
        x, y, c = _position()
        me = 4 * x + 2 * y + c
        slots[0] = v_ref[...]
        sends = []
        for r in range(1, 8):
            fx, fy, fc = (r >> 2) & 1, (r >> 1) & 1, r & 1
            to = (x ^ fx, y ^ fy, c ^ fc)
            cp = pltpu.make_async_remote_copy(src_ref=v_ref, dst_ref=slots.at[r], send_sem=send_sems.at[r - 1],
                                              recv_sem=recv_sems.at[r - 1], device_id=to, device_id_type=MESH)
            cp.start()
            sends.append(cp)
        for cp in sends:
            cp.wait()
        acc = slots[me]
        for d in range(1, 8):
            acc = acc + slots[d ^ me]
        o_ref[...] = acc

    vmem = pl.BlockSpec(memory_space=pltpu.VMEM)
    return _pcall(
        body, name=name, in_specs=[vmem], out_specs=vmem, out_shape=jax.ShapeDtypeStruct((R, C), F32),
        scratch_shapes=[pltpu.VMEM((8, R, C), F32), pltpu.SemaphoreType.DMA((7,)), pltpu.SemaphoreType.DMA((7,))],
    )(v)


def _add_pair(which, both, got, name):
    _, N, R, C = both.shape
    tr = _pick(R, 512, 16)

    def body(which_ref, a_ref, b_ref, o_ref):
        o_ref[...] = (a_ref[...].astype(F32) + b_ref[...].astype(F32)).astype(BF16)

    spec = pl.BlockSpec((None, tr, C), lambda n, i, w: (n, i, 0))
    grid_spec = pltpu.PrefetchScalarGridSpec(
        num_scalar_prefetch=1, grid=(N, R // tr),
        in_specs=[pl.BlockSpec((None, None, tr, C), lambda n, i, w: (w[0], n, i, 0)), spec], out_specs=spec)
    return _pcall(body, name=name, grid_spec=grid_spec,
                  out_shape=jax.ShapeDtypeStruct((N, R, C), BF16))(which, both, got)


def _sum_chips(which, own, others, name):
    N, R, C = others.shape
    tr = _pick(R, 512, 16)

    def body(which_ref, own_ref, p_ref, o_ref):
        acc = own_ref[...].astype(F32)
        for n in range(N):
            acc = acc + p_ref[n].astype(F32)
        o_ref[...] = acc

    grid_spec = pltpu.PrefetchScalarGridSpec(
        num_scalar_prefetch=1, grid=(R // tr,),
        in_specs=[pl.BlockSpec((None, tr, C), lambda i, w: (w[0], i, 0)), pl.BlockSpec((N, tr, C), lambda i, w: (0, i, 0))],
        out_specs=pl.BlockSpec((tr, C), lambda i, w: (i, 0)))
    return _pcall(body, name=name, grid_spec=grid_spec,
                  out_shape=jax.ShapeDtypeStruct((R, C), F32))(which, own, others)


BIG = ("w_ffn1_in", "w_ffn1_out", "w_in", "w_br_sb", "w_br_ch", "w_br_fox", "w_out", "w_ffn2_in", "w_ffn2_out")
ROW_SHARDED = ("w_ffn1_out", "w_out", "w_ffn2_out")
SMALL = ("g_ffn1", "g_mix", "b_in", "rel_bias", "g_ffn2", "g_final")
SHARD_SHAPES = {
    "w_ffn1_in": (D_MODEL, 2 * D_FF // N_CHIPS), "w_ffn1_out": (D_FF // N_CHIPS, D_MODEL),
    "w_in": (D_MODEL, IN_WIDTH // N_CHIPS), "w_br_sb": (W_SB, D_MODEL // N_CHIPS),
    "w_br_ch": (W_CH, D_MODEL // N_CHIPS), "w_br_fox": (W_FOX, D_MODEL // N_CHIPS),
    "w_out": (D_MODEL // N_CHIPS, D_MODEL), "w_ffn2_in": (D_MODEL, 2 * D_FF // N_CHIPS),
    "w_ffn2_out": (D_FF // N_CHIPS, D_MODEL),
}
def _reduce_scatter_grads(split, axes):
    core = lax.axis_index("c")
    chip = 2 * lax.axis_index("x") + lax.axis_index("y")
    core_arr, chip_arr = core.astype(jnp.int32)[None], chip.astype(jnp.int32)[None]
    got = _send_other_half(split, "grad_split")
    pair = [_add_pair(core_arr, a, b, f"grad_pair_sum_{i}") for i, (a, b) in enumerate(zip(split, got))]
    landed = _scatter_chips(pair, "grad_scatter")
    mine = [_sum_chips(chip_arr, p, q, f"grad_chip_sum_{i}") for i, (p, q) in enumerate(zip(pair, landed))]
    theirs = _swap_with_sibling(mine, "grad_share")
    first = core == 0
    return [jnp.concatenate([jnp.where(first, a, b), jnp.where(first, b, a)], axis=ax)
            for a, b, ax in zip(mine, theirs, axes)]


SMALL_SIZES = {"g_ffn1": DEPTH * D_MODEL, "g_mix": DEPTH * D_MODEL, "b_in": DEPTH * IN_WIDTH,
               "rel_bias": DEPTH * N_REL * H_CH, "g_ffn2": DEPTH * D_MODEL, "g_final": D_MODEL}
SMALL_TOTAL = sum(SMALL_SIZES.values()) + 1
SMALL_ROWS = -(-SMALL_TOTAL // (8 * 128)) * 8


def _pack_small(vals, extra):
    flat = [vals[n].reshape(-1) for n in SMALL] + [extra.reshape(-1)]
    flat.append(jnp.zeros((SMALL_ROWS * 128 - SMALL_TOTAL,), F32))
    return jnp.concatenate(flat).reshape(SMALL_ROWS, 128)


def _unpack_small(pack, shapes):
    flat, out, off = pack.reshape(-1), {}, 0
    for n in SMALL:
        out[n] = flat[off:off + SMALL_SIZES[n]].reshape(shapes[n])
        off += SMALL_SIZES[n]
    return out, flat[off]


def _to_heads(t, n_heads):
    return jnp.transpose(t.reshape(t.shape[0], n_heads, HEAD_DIM), (1, 0, 2)).astype(BF16)


def _from_heads(t):
    return jnp.transpose(t, (1, 0, 2)).reshape(t.shape[1], t.shape[0] * HEAD_DIM)


def _pad_keys(t):
    return jnp.pad(t, ((0, 0), (BAND_PAD, 0), (0, 0)))


DIAGONALS = CHUNK + BAND - 1


def _band_bias(table):
    n_far = (LEFT_CHUNKS + 1) * CHUNK + CHUNK - MAX_REL
    near = table[N_REL - 1 - (DIAGONALS - n_far):N_REL - 1][::-1]
    diag = jnp.concatenate([jnp.broadcast_to(table[N_REL - 1], (n_far, H_CH)), near], axis=0).T
    diag = jnp.pad(diag, ((0, 0), (0, 1)))
    skew = jnp.tile(diag, (1, CHUNK))[:, :CHUNK * DIAGONALS].reshape(H_CH, CHUNK, DIAGONALS)
    return skew[:, :, CHUNK - 1:]


def _pad_w_in(w):
    qkv, f, gates = w[:, :QKV_WIDTH], w[:, QKV_WIDTH:QKV_WIDTH + H_FOX], w[:, QKV_WIDTH + H_FOX:]
    return jnp.concatenate([qkv, gates, f, jnp.zeros((w.shape[0], F_PAD - H_FOX), w.dtype)], axis=1)


def _unpad_w_in(w):
    return jnp.concatenate([w[:, :QKV_WIDTH], w[:, QKV_WIDTH + GATE_WIDTH:QKV_WIDTH + GATE_WIDTH + H_FOX],
                            w[:, QKV_WIDTH:QKV_WIDTH + GATE_WIDTH]], axis=1)


QKV_SPLITS = np.cumsum([0, W_SB, W_SB, W_SB, W_CH, W_CH, W_CH, W_FOX, W_FOX, W_FOX])


def _by_chip_rows(g):
    return g.reshape(2, N_CHIPS, g.shape[1] // N_CHIPS, g.shape[2])


def _ffn_fwd(x, g, w_in, w_out, layer, tag):
    h = _rmsnorm_fwd(x, g, f"{tag}_norm")
    gu = _mm(h, w_in, mode="nn", name=f"{tag}_in", gathered=(layer, "col"))
    a = _swiglu_fwd(gu, f"{tag}_act")
    y = _mm(a, w_out, mode="nn", name=f"{tag}_out", alpha=0.5, res=x, gathered=(layer, "row"))
    return y, (h, gu, a)


def _ffn_bwd(x, g, w_in, w_out, layer, saved, dy, tag):
    h, gu, a = saved
    da = _mm(dy, w_out, mode="nt", name=f"{tag}_da", alpha=0.5, gathered=(layer, "row"))
    dw_out = _mm(a, dy, mode="tn", name=f"{tag}_dwout", alpha=0.5, tm_cap=512, out_dtype=BF16, out_split="row")
    dgu = _swiglu_bwd(gu, da, f"{tag}_dact")
    dw_in = _mm(h, dgu, mode="tn", name=f"{tag}_dwin", tm_cap=512, out_dtype=BF16, out_split="col")
    dh = _mm(dgu, w_in, mode="nt", name=f"{tag}_dh", gathered=(layer, "col"))
    dx, dg = _rmsnorm_bwd(x, g, dh, dy, f"{tag}_dnorm")
    return dx, dg, dw_in, _by_chip_rows(dw_out)


def _mixer_fwd(x, lw, tag):
    T = x.shape[0]
    h = _rmsnorm_fwd(x, lw["g_mix"], f"{tag}_norm")
    p = _mm(h, lw["w_in"], mode="nn", name=f"{tag}_proj", bias=lw["b_in"], tn_cap=896)
    parts = [p[:, QKV_SPLITS[i]:QKV_SPLITS[i + 1]] for i in range(9)]
    qa, ka, va = (_to_heads(t, H_SB) for t in parts[0:3])
    qb, kb, vb = (_to_heads(t, H_CH) for t in parts[3:6])
    qc, kc, vc = (_to_heads(t, H_FOX) for t in parts[6:9])
    flog = jnp.pad(p[:, QKV_WIDTH + GATE_WIDTH:QKV_WIDTH + GATE_WIDTH + H_FOX].T, ((0, 8 - H_FOX), (0, 0)))
    fcum = _forget_cumsum(flog, f"{tag}_fcum")[:H_FOX]
    fq, fk = fcum.reshape(H_FOX, T, 1), fcum.reshape(H_FOX, T // QB, QB)
    kbp, vbp = _pad_keys(kb), _pad_keys(vb)
    oa = _sb_fwd(qa, ka, va, f"{tag}_sb")
    ob = _chunk_fwd(qb, kbp, vbp, lw["bias"], f"{tag}_ch")
    oc, lse = _fox_fwd(qc, kc, vc, fq, fk, f"{tag}_fox")
    oam, obm, ocm = _from_heads(oa), _from_heads(ob), _from_heads(oc)
    col, row = (lw["layer"], "col"), (lw["layer"], "row")
    ya = _mm(oam, lw["w_br_sb"], mode="nn", name=f"{tag}_bra", gathered=col)
    yb = _mm(obm, lw["w_br_ch"], mode="nn", name=f"{tag}_brb", gathered=col)
    yc = _mm(ocm, lw["w_br_fox"], mode="nn", name=f"{tag}_brc", gathered=col)
    merged = _gate_fwd(p, ya, yb, yc, f"{tag}_gate")
    y = _mm(merged, lw["w_out"], mode="nn", name=f"{tag}_out", res=x, gathered=row)
    saved = (h, p, (qa, ka, va), (qb, kbp, vbp), (qc, kc, vc), flog, fq, fk, lse, (oam, obm, ocm),
             (ya, yb, yc), merged)
    return y, saved


def _mixer_bwd(x, lw, saved, dy, tag):
    (h, p, (qa, ka, va), (qb, kbp, vbp), (qc, kc, vc), flog, fq, fk, lse, (oam, obm, ocm),
     (ya, yb, yc), merged) = saved
    T = x.shape[0]
    grads = {}
    col, row = (lw["layer"], "col"), (lw["layer"], "row")
    dmerged = _mm(dy, lw["w_out"], mode="nt", name=f"{tag}_dmerged", gathered=row)
    grads["w_out"] = _by_chip_rows(_mm(merged, dy, mode="tn", name=f"{tag}_dwout", tm_cap=512, out_dtype=BF16,
                                       out_split="row"))
    dya, dyb, dyc, dgate = _gate_bwd(p, ya, yb, yc, dmerged, f"{tag}_dgate")
    doa = _mm(dya, lw["w_br_sb"], mode="nt", name=f"{tag}_doa", gathered=col)
    dob = _mm(dyb, lw["w_br_ch"], mode="nt", name=f"{tag}_dob", gathered=col)
    doc = _mm(dyc, lw["w_br_fox"], mode="nt", name=f"{tag}_doc", gathered=col)
    by_chip = dict(mode="tn", tm_cap=512, out_dtype=BF16, out_split="col")
    grads["w_br_sb"] = _mm(oam, dya, name=f"{tag}_dwbra", **by_chip)
    grads["w_br_ch"] = _mm(obm, dyb, name=f"{tag}_dwbrb", **by_chip)
    grads["w_br_fox"] = _mm(ocm, dyc, name=f"{tag}_dwbrc", **by_chip)
    dqa, dka, dva = _sb_bwd(qa, ka, va, _to_heads(doa, H_SB), f"{tag}_dsb")
    dqb, dkbp, dvbp, dbias = _chunk_bwd(qb, kbp, vbp, lw["bias"], _to_heads(dob, H_CH), f"{tag}_dch")
    dqc, dkc, dvc, dfk = _fox_bwd(qc, kc, vc, fq, fk, lse, _to_heads(doc, H_FOX), f"{tag}_dfox")
    dflog = _forget_cumsum_bwd(flog, jnp.pad(dfk.reshape(H_FOX, T), ((0, 8 - H_FOX), (0, 0))), f"{tag}_dfcum")
    dqkv = [_from_heads(t) for t in (dqa, dka, dva, dqb, dkbp[:, BAND_PAD:], dvbp[:, BAND_PAD:], dqc, dkc, dvc)]
    dp = jnp.concatenate(dqkv + [dgate, jnp.pad(dflog[:H_FOX].T, ((0, 0), (0, F_PAD - H_FOX)))], axis=1)
    grads["b_in"] = _colsum(dp, f"{tag}_dbin")
    dpb = dp.astype(BF16)
    dw_in = _unpad_w_in(_mm(h, dpb, mode="tn", name=f"{tag}_dwin", tm_cap=512, tn_cap=896, out_dtype=BF16))
    grads["w_in"] = jnp.transpose(dw_in.reshape(2, D_MODEL // 2, N_CHIPS, IN_WIDTH // N_CHIPS), (0, 2, 1, 3))
    dh = _mm(dpb, lw["w_in"], mode="nt", name=f"{tag}_dh", tk_cap=896)
    dx, grads["g_mix"] = _rmsnorm_bwd(x, lw["g_mix"], dh, dy, f"{tag}_dnorm")
    grads["rel_bias"] = lw["bias_vjp"](dbias)[0]
    return dx, grads


def kernel(x, g_ffn1, w_ffn1_in, w_ffn1_out, g_mix, w_in, b_in, rel_bias, w_br_sb, w_br_ch, w_br_fox, w_out, g_ffn2, w_ffn2_in, w_ffn2_out, g_final, loss_target, m_g_ffn1, m_w_ffn1_in, m_w_ffn1_out, m_g_mix, m_w_in, m_b_in, m_rel_bias, m_w_br_sb, m_w_br_ch, m_w_br_fox, m_w_out, m_g_ffn2, m_w_ffn2_in, m_w_ffn2_out, m_g_final, v_g_ffn1, v_w_ffn1_in, v_w_ffn1_out, v_g_mix, v_w_in, v_b_in, v_rel_bias, v_w_br_sb, v_w_br_ch, v_w_br_fox, v_w_out, v_g_ffn2, v_w_ffn2_in, v_w_ffn2_out, v_g_final):
    names = ("g_ffn1", "w_ffn1_in", "w_ffn1_out", "g_mix", "w_in", "b_in", "rel_bias", "w_br_sb", "w_br_ch",
             "w_br_fox", "w_out", "g_ffn2", "w_ffn2_in", "w_ffn2_out", "g_final")
    w = dict(zip(names, (g_ffn1, w_ffn1_in, w_ffn1_out, g_mix, w_in, b_in, rel_bias, w_br_sb, w_br_ch,
                         w_br_fox, w_out, g_ffn2, w_ffn2_in, w_ffn2_out, g_final)))
    m = dict(zip(names, (m_g_ffn1, m_w_ffn1_in, m_w_ffn1_out, m_g_mix, m_w_in, m_b_in, m_rel_bias, m_w_br_sb,
                         m_w_br_ch, m_w_br_fox, m_w_out, m_g_ffn2, m_w_ffn2_in, m_w_ffn2_out, m_g_final)))
    v = dict(zip(names, (v_g_ffn1, v_w_ffn1_in, v_w_ffn1_out, v_g_mix, v_w_in, v_b_in, v_rel_bias, v_w_br_sb,
                         v_w_br_ch, v_w_br_fox, v_w_out, v_g_ffn2, v_w_ffn2_in, v_w_ffn2_out, v_g_final)))
    xs = x[0]
    tgt = loss_target[0]

    own = [w[n].astype(BF16) for n in BIG]
    chip = 2 * lax.axis_index("x") + lax.axis_index("y")
    gathered = {n: lax.dynamic_update_slice(g, o[:, None], (0, chip, 0, 0))
                for n, g, o in zip(BIG, _gather_weights(own, "gather_weights"), own)}
    layers = []
    for l in range(DEPTH):
        lw = dict(gathered)
        lw["layer"] = l
        lw["w_in"] = _pad_w_in(jnp.concatenate([gathered["w_in"][l, j] for j in range(N_CHIPS)], axis=1))
        lw["b_in"] = _pad_w_in(w["b_in"][l][None, :])
        lw["bias"], lw["bias_vjp"] = jax.vjp(_band_bias, w["rel_bias"][l])
        for n in ("g_ffn1", "g_mix", "g_ffn2"):
            lw[n] = w[n][l][None, :]
        layers.append(lw)

    saved = []
    act = xs
    for l, lw in enumerate(layers):
        x0 = act
        x1, s1 = _ffn_fwd(x0, lw["g_ffn1"], lw["w_ffn1_in"], lw["w_ffn1_out"], l, f"l{l}_ffn1")
        x2, s2 = _mixer_fwd(x1, lw, f"l{l}_mix")
        act, s3 = _ffn_fwd(x2, lw["g_ffn2"], lw["w_ffn2_in"], lw["w_ffn2_out"], l, f"l{l}_ffn2")
        saved.append((x0, s1, x1, s2, x2, s3))

    dact, dg_final, loss_row = _final_loss(act, w["g_final"][None, :], tgt, "final_loss")

    big_grads = {n: [None] * DEPTH for n in BIG}
    small_grads = {n: [None] * DEPTH for n in ("g_ffn1", "g_mix", "b_in", "rel_bias", "g_ffn2")}
    for l in reversed(range(DEPTH)):
        lw = layers[l]
        x0, s1, x1, s2, x2, s3 = saved[l]
        dx2, small_grads["g_ffn2"][l], big_grads["w_ffn2_in"][l], big_grads["w_ffn2_out"][l] = _ffn_bwd(
            x2, lw["g_ffn2"], lw["w_ffn2_in"], lw["w_ffn2_out"], l, s3, dact, f"l{l}_ffn2")
        dx1, mg = _mixer_bwd(x1, lw, s2, dx2, f"l{l}_mix")
        for n in ("w_in", "w_br_sb", "w_br_ch", "w_br_fox", "w_out"):
            big_grads[n][l] = mg[n]
        small_grads["b_in"][l] = _unpad_w_in(mg["b_in"])[0]
        small_grads["g_mix"][l] = mg["g_mix"][0]
        small_grads["rel_bias"][l] = mg["rel_bias"]
        dact, dg1, big_grads["w_ffn1_in"][l], big_grads["w_ffn1_out"][l] = _ffn_bwd(
            x0, lw["g_ffn1"], lw["w_ffn1_in"], lw["w_ffn1_out"], l, s1, dx1, f"l{l}_ffn1")
        small_grads["g_ffn1"][l] = dg1[0]
        small_grads["g_ffn2"][l] = small_grads["g_ffn2"][l][0]
    grad_x = dact[None]

    small = {n: jnp.stack(small_grads[n]) for n in small_grads}
    small["g_final"] = dg_final[0]
    small_sum, loss = _unpack_small(_allreduce_small(_pack_small(small, loss_row[0, :1]), "allreduce_small"),
                                    {n: w[n].shape for n in SMALL})

    order = [(n, l) for n in BIG for l in range(DEPTH)]
    summed = _reduce_scatter_grads([big_grads[n][l] for n, l in order],
                                   [1 if n in ROW_SHARDED else 0 for n, l in order])
    grads = {n: jnp.stack(summed[DEPTH * i:DEPTH * (i + 1)]) for i, n in enumerate(BIG)}
    grads.update(small_sum)

    delta, new_m, new_v = {}, {}, {}
    for n in BIG:
        shape = w[n].shape
        flat = lambda t: t.reshape(-1, shape[-1])
        d, nm, nv = _adamw(flat(w[n]), flat(grads[n]), flat(m[n]), flat(v[n]), f"adamw_{n}")
        delta[n], new_m[n], new_v[n] = d.reshape(shape), nm.reshape(shape), nv.reshape(shape)
    zero = jnp.zeros((1,), F32)
    sd, sm, sv = _adamw(_pack_small(w, zero), _pack_small(grads, zero), _pack_small(m, zero), _pack_small(v, zero),
                        "adamw_small")
    shapes = {n: w[n].shape for n in SMALL}
    for dst, src in ((delta, sd), (new_m, sm), (new_v, sv)):
        dst.update(_unpack_small(src, shapes)[0])

    return (loss, grad_x, *[grads[n] for n in names], *[delta[n] for n in names],
            *[new_m[n] for n in names], *[new_v[n] for n in names])
```

```python
import functools

import numpy as np
import jax
import jax.numpy as jnp
from jax import lax
from jax.experimental import pallas as pl
from jax.experimental.pallas import tpu as pltpu

F32 = jnp.float32
BF16 = jnp.bfloat16

D_MODEL = 1024
DEPTH = 2
CHUNK = 64
HEAD_DIM = 64
H_SB, H_CH, H_FOX = 4, 8, 4
W_SB, W_CH, W_FOX = H_SB * HEAD_DIM, H_CH * HEAD_DIM, H_FOX * HEAD_DIM
LEFT_CHUNKS = 8
MAX_REL = 128
N_REL = 2 * MAX_REL + 1
D_FF = 2816
QKV_WIDTH = 3 * (W_SB + W_CH + W_FOX)
GATE_WIDTH = 3 * D_MODEL
IN_WIDTH = QKV_WIDTH + H_FOX + GATE_WIDTH
F_PAD = 128
P_WIDTH = QKV_WIDTH + GATE_WIDTH + F_PAD
RMS_EPS = 1e-6
NEG = -1e30
SCALE = HEAD_DIM ** -0.5
QB = 256
BAND = (LEFT_CHUNKS + 2) * CHUNK
BAND_PAD = BAND - CHUNK

ADAM_LR, ADAM_B1, ADAM_B2, ADAM_EPS, ADAM_WD, ADAM_STEP = 0.001, 0.9, 0.999, 1e-08, 0.01, 10

N_CHIPS = 4
PACK_COLS = 1024
VMEM_BUDGET = 52 * 1024 * 1024

NT = (((1,), (1,)), ((), ()))
TN = (((0,), (0,)), ((), ()))
NN = (((1,), (0,)), ((), ()))
MESH = pl.DeviceIdType.MESH


def _pcall(body, **kw):
    return pl.pallas_call(body, **kw)


def _pick(n, cap, mult=128):
    best = None
    d = mult
    while d <= min(n, cap):
        if n % d == 0:
            best = d
        d += mult
    return n if best is None else best


def _nbytes(shape, dtype):
    return int(np.prod(shape)) * jnp.dtype(dtype).itemsize


def _mm(a, b, *, mode, name, out_dtype=F32, alpha=1.0, bias=None, res=None, tm_cap=1024, tn_cap=512,
        tk_cap=2816, gathered=None, out_split=None):
    if mode == "tn":
        K, M = a.shape
    else:
        M, K = a.shape
    dn = {"nn": NN, "nt": NT, "tn": TN}[mode]
    b_rows = None
    if gathered is None:
        N = b.shape[0] if mode == "nt" else b.shape[1]
        tn = {None: _pick(N, tn_cap), "col": N // N_CHIPS, "row": N // 2}[out_split]
        if out_split == "col":
            tm_cap = min(tm_cap, M // 2)
        tk = K if K <= tk_cap else _pick(K, tk_cap)
        b_spec = (pl.BlockSpec((tn, tk), lambda i, j, k: (j, k)) if mode == "nt"
                  else pl.BlockSpec((tk, tn), lambda i, j, k: (k, j)))
    else:
        layer, sharding = gathered
        r, c = b.shape[2:]
        rows, cols = (r, N_CHIPS * c) if sharding == "col" else (N_CHIPS * r, c)
        N = rows if mode == "nt" else cols
        assert K == (cols if mode == "nt" else rows), (name, K, rows, cols)
        if sharding == "col" and mode == "nn":
            tn, tk = c, (r if r <= tk_cap else _pick(r, tk_cap))
            b_spec = pl.BlockSpec((None, None, tk, c), lambda i, j, k: (layer, j, k, 0))
        elif sharding == "col":
            tn, tk = _pick(r, tn_cap), c
            b_spec = pl.BlockSpec((None, None, tn, c), lambda i, j, k: (layer, k, j, 0))
        elif mode == "nn":
            tn, tk, b_rows = _pick(c, tn_cap), K, N_CHIPS
            b_spec = pl.BlockSpec((None, N_CHIPS, r, tn), lambda i, j, k: (layer, 0, 0, j))
        else:
            tn, tk, b_rows = 2 * r, K, 2
            b_spec = pl.BlockSpec((None, 2, r, c), lambda i, j, k: (layer, j, 0, 0))
    tm = _pick(M, tm_cap)
    nk = K // tk

    a_spec = (pl.BlockSpec((tk, tm), lambda i, j, k: (k, i)) if mode == "tn"
              else pl.BlockSpec((tm, tk), lambda i, j, k: (i, k)))
    in_specs = [a_spec, b_spec]
    args = [a, b]
    if bias is not None:
        in_specs.append(pl.BlockSpec((1, tn), lambda i, j, k: (0, j)))
        args.append(bias)
    if res is not None:
        in_specs.append(pl.BlockSpec((tm, tn), lambda i, j, k: (i, j)))
        args.append(res)

    est = 2 * (_nbytes((tm, tk), a.dtype) + _nbytes((tk, tn), b.dtype) + _nbytes((tm, tn), out_dtype))
    est += _nbytes((tm, tn), F32) * (3 if res is not None else 1)
    assert est < VMEM_BUDGET, (name, est)

    def body(*refs):
        a_ref, b_ref = refs[0], refs[1]
        pos = 2
        bias_ref = res_ref = None
        if bias is not None:
            bias_ref = refs[pos]
            pos += 1
        if res is not None:
            res_ref = refs[pos]
            pos += 1
        o_ref = refs[pos]
        acc_ref = refs[pos + 1] if nk > 1 else None

        bv = b_ref[...]
        if b_rows is not None:
            bv = bv.reshape(b_rows * bv.shape[1], bv.shape[2])
        part = lax.dot_general(a_ref[...].astype(BF16), bv.astype(BF16), dn, preferred_element_type=F32)

        def finish(acc):
            if alpha != 1.0:
                acc = acc * alpha
            if bias_ref is not None:
                acc = acc + bias_ref[...]
            if res_ref is not None:
                acc = acc + res_ref[...]
            o_ref[...] = acc.astype(out_dtype)

        if nk == 1:
            finish(part)
        else:
            k = pl.program_id(2)

            @pl.when(k == 0)
            def _():
                acc_ref[...] = part

            @pl.when(k > 0)
            def _():
                acc_ref[...] += part

            @pl.when(k == nk - 1)
            def _():
                finish(acc_ref[...])

    if out_split == "col":
        per_half = M // 2 // tm
        out_spec = pl.BlockSpec((None, None, tm, tn), lambda i, j, k: (i // per_half, j, i % per_half, 0))
        out_shape = jax.ShapeDtypeStruct((2, N_CHIPS, M // 2, tn), out_dtype)
    elif out_split == "row":
        out_spec = pl.BlockSpec((None, tm, tn), lambda i, j, k: (j, i, 0))
        out_shape = jax.ShapeDtypeStruct((2, M, tn), out_dtype)
    else:
        out_spec = pl.BlockSpec((tm, tn), lambda i, j, k: (i, j))
        out_shape = jax.ShapeDtypeStruct((M, N), out_dtype)
    return _pcall(
        body, name=name, grid=(M // tm, N // tn, nk), in_specs=in_specs, out_specs=out_spec, out_shape=out_shape,
        scratch_shapes=[pltpu.VMEM((tm, tn), F32)] if nk > 1 else [],
        compiler_params=pltpu.CompilerParams(dimension_semantics=("parallel", "parallel", "arbitrary")),
    )(*args)


def _rmsnorm_fwd(x, g, name):
    T, Dm = x.shape
    tm = _pick(T, 256, 8)

    def body(x_ref, g_ref, h_ref):
        xv = x_ref[...]
        rstd = lax.rsqrt(jnp.mean(xv * xv, axis=-1, keepdims=True) + RMS_EPS)
        h_ref[...] = (xv * rstd * g_ref[...]).astype(BF16)

    return _pcall(
        body, name=name, grid=(T // tm,),
        in_specs=[pl.BlockSpec((tm, Dm), lambda i: (i, 0)), pl.BlockSpec((1, Dm), lambda i: (0, 0))],
        out_specs=pl.BlockSpec((tm, Dm), lambda i: (i, 0)),
        out_shape=jax.ShapeDtypeStruct((T, Dm), BF16),
    )(x, g)


def _rmsnorm_bwd(x, g, dh, dres, name):
    T, Dm = x.shape
    tm = _pick(T, 256, 8)

    def body(x_ref, g_ref, dh_ref, dres_ref, dx_ref, dg_ref):
        xv = x_ref[...]
        rstd = lax.rsqrt(jnp.mean(xv * xv, axis=-1, keepdims=True) + RMS_EPS)
        xhat = xv * rstd
        dhv = dh_ref[...]
        dyg = dhv * g_ref[...]
        dx_ref[...] = dres_ref[...] + rstd * (dyg - xhat * jnp.mean(dyg * xhat, axis=-1, keepdims=True))
        part = jnp.sum(dhv * xhat, axis=0, keepdims=True)

        @pl.when(pl.program_id(0) == 0)
        def _():
            dg_ref[...] = part

        @pl.when(pl.program_id(0) > 0)
        def _():
            dg_ref[...] += part

    row = pl.BlockSpec((tm, Dm), lambda i: (i, 0))
    vec = pl.BlockSpec((1, Dm), lambda i: (0, 0))
    return _pcall(
        body, name=name, grid=(T // tm,), in_specs=[row, vec, row, row], out_specs=[row, vec],
        out_shape=[jax.ShapeDtypeStruct((T, Dm), F32), jax.ShapeDtypeStruct((1, Dm), F32)],
        compiler_params=pltpu.CompilerParams(dimension_semantics=("arbitrary",)),
    )(x, g, dh, dres)


def _final_loss(x, g, tgt, name):
    T, Dm = x.shape
    tm = _pick(T, 256, 8)

    def body(x_ref, g_ref, t_ref, dx_ref, dg_ref, loss_ref):
        xv = x_ref[...]
        gv = g_ref[...]
        rstd = lax.rsqrt(jnp.mean(xv * xv, axis=-1, keepdims=True) + RMS_EPS)
        xhat = xv * rstd
        err = xhat * gv - t_ref[...]
        part_loss = 0.5 * jnp.sum(jnp.mean(err * err, axis=-1, keepdims=True), axis=0, keepdims=True)
        dy = err * (1.0 / Dm)
        dyg = dy * gv
        dx_ref[...] = rstd * (dyg - xhat * jnp.mean(dyg * xhat, axis=-1, keepdims=True))
        part_g = jnp.sum(dy * xhat, axis=0, keepdims=True)
        part_l = jnp.broadcast_to(part_loss, (1, 128))

        @pl.when(pl.program_id(0) == 0)
        def _():
            dg_ref[...] = part_g
            loss_ref[...] = part_l

        @pl.when(pl.program_id(0) > 0)
        def _():
            dg_ref[...] += part_g
            loss_ref[...] += part_l

    row = pl.BlockSpec((tm, Dm), lambda i: (i, 0))
    vec = pl.BlockSpec((1, Dm), lambda i: (0, 0))
    return _pcall(
        body, name=name, grid=(T // tm,), in_specs=[row, vec, row],
        out_specs=[row, vec, pl.BlockSpec((1, 128), lambda i: (0, 0))],
        out_shape=[jax.ShapeDtypeStruct((T, Dm), F32), jax.ShapeDtypeStruct((1, Dm), F32),
                   jax.ShapeDtypeStruct((1, 128), F32)],
        compiler_params=pltpu.CompilerParams(dimension_semantics=("arbitrary",)),
    )(x, g, tgt)


def _sigmoid(z):
    return 1.0 / (1.0 + jnp.exp(-z))


def _swiglu_fwd(gu, name):
    T = gu.shape[0]
    tm = _pick(T, 256, 16)

    def body(gu_ref, a_ref):
        gv = gu_ref[:, :D_FF].astype(F32)
        uv = gu_ref[:, D_FF:].astype(F32)
        a_ref[...] = (gv * _sigmoid(gv) * uv).astype(BF16)

    return _pcall(
        body, name=name, grid=(T // tm,), in_specs=[pl.BlockSpec((tm, 2 * D_FF), lambda i: (i, 0))],
        out_specs=pl.BlockSpec((tm, D_FF), lambda i: (i, 0)),
        out_shape=jax.ShapeDtypeStruct((T, D_FF), BF16),
    )(gu)


def _swiglu_bwd(gu, da, name):
    T = gu.shape[0]
    tm = _pick(T, 256, 16)

    def body(gu_ref, da_ref, d_ref):
        gv = gu_ref[:, :D_FF].astype(F32)
        uv = gu_ref[:, D_FF:].astype(F32)
        dav = da_ref[...].astype(F32)
        sg = _sigmoid(gv)
        d_ref[:, :D_FF] = (dav * uv * (sg + gv * sg * (1.0 - sg))).astype(BF16)
        d_ref[:, D_FF:] = (dav * gv * sg).astype(BF16)

    return _pcall(
        body, name=name, grid=(T // tm,),
        in_specs=[pl.BlockSpec((tm, 2 * D_FF), lambda i: (i, 0)), pl.BlockSpec((tm, D_FF), lambda i: (i, 0))],
        out_specs=pl.BlockSpec((tm, 2 * D_FF), lambda i: (i, 0)),
        out_shape=jax.ShapeDtypeStruct((T, 2 * D_FF), BF16),
    )(gu, da)


GATE_BLOCK0 = QKV_WIDTH // D_MODEL


def _gate_fwd(p, ya, yb, yc, name):
    T = p.shape[0]
    tm = _pick(T, 256, 8)

    def body(ga_ref, gb_ref, gc_ref, ya_ref, yb_ref, yc_ref, o_ref):
        o_ref[...] = (_sigmoid(ga_ref[...]) * ya_ref[...] + _sigmoid(gb_ref[...]) * yb_ref[...]
                      + _sigmoid(gc_ref[...]) * yc_ref[...]).astype(BF16)

    gate = [pl.BlockSpec((tm, D_MODEL), functools.partial(lambda i, kk: (i, GATE_BLOCK0 + kk), kk=kk))
            for kk in range(3)]
    row = pl.BlockSpec((tm, D_MODEL), lambda i: (i, 0))
    return _pcall(
        body, name=name, grid=(T // tm,), in_specs=gate + [row, row, row], out_specs=row,
        out_shape=jax.ShapeDtypeStruct((T, D_MODEL), BF16),
    )(p, p, p, ya, yb, yc)


def _gate_bwd(p, ya, yb, yc, dm, name):
    T = p.shape[0]
    tm = _pick(T, 256, 8)

    def body(ga_ref, gb_ref, gc_ref, ya_ref, yb_ref, yc_ref, dm_ref, da_ref, db_ref, dc_ref, dg_ref):
        dmv = dm_ref[...]
        for kk, (g_ref, y_ref, d_ref) in enumerate(((ga_ref, ya_ref, da_ref), (gb_ref, yb_ref, db_ref),
                                                    (gc_ref, yc_ref, dc_ref))):
            s = _sigmoid(g_ref[...])
            d_ref[...] = (s * dmv).astype(BF16)
            dg_ref[:, kk * D_MODEL:(kk + 1) * D_MODEL] = dmv * y_ref[...] * s * (1.0 - s)

    gate = [pl.BlockSpec((tm, D_MODEL), functools.partial(lambda i, kk: (i, GATE_BLOCK0 + kk), kk=kk))
            for kk in range(3)]
    row = pl.BlockSpec((tm, D_MODEL), lambda i: (i, 0))
    return _pcall(
        body, name=name, grid=(T // tm,), in_specs=gate + [row, row, row, row],
        out_specs=[row, row, row, pl.BlockSpec((tm, GATE_WIDTH), lambda i: (i, 0))],
        out_shape=[jax.ShapeDtypeStruct((T, D_MODEL), BF16)] * 3 + [jax.ShapeDtypeStruct((T, GATE_WIDTH), F32)],
    )(p, p, p, ya, yb, yc, dm)


def _colsum(a, name):
    T, N = a.shape
    tm = _pick(T, 256, 8)

    def body(a_ref, o_ref):
        part = jnp.sum(a_ref[...], axis=0, keepdims=True)

        @pl.when(pl.program_id(0) == 0)
        def _():
            o_ref[...] = part

        @pl.when(pl.program_id(0) > 0)
        def _():
            o_ref[...] += part

    return _pcall(
        body, name=name, grid=(T // tm,), in_specs=[pl.BlockSpec((tm, N), lambda i: (i, 0))],
        out_specs=pl.BlockSpec((1, N), lambda i: (0, 0)), out_shape=jax.ShapeDtypeStruct((1, N), F32),
        compiler_params=pltpu.CompilerParams(dimension_semantics=("arbitrary",)),
    )(a)


def _adamw(w, g, m, v, name):
    R, C = w.shape
    tr = 256 if R % 256 == 0 else R
    c1 = 1.0 / (1.0 - ADAM_B1 ** ADAM_STEP)
    c2 = 1.0 / (1.0 - ADAM_B2 ** ADAM_STEP)

    def body(w_ref, g_ref, m_ref, v_ref, d_ref, nm_ref, nv_ref):
        gv = g_ref[...]
        mn = ADAM_B1 * m_ref[...] + (1.0 - ADAM_B1) * gv
        vn = ADAM_B2 * v_ref[...] + (1.0 - ADAM_B2) * (gv * gv)
        nm_ref[...] = mn
        nv_ref[...] = vn
        d_ref[...] = -ADAM_LR * ((mn * c1) / (jnp.sqrt(vn * c2) + ADAM_EPS) + ADAM_WD * w_ref[...])

    spec = pl.BlockSpec((tr, C), lambda i: (i, 0))
    return _pcall(
        body, name=name, grid=(R // tr,), in_specs=[spec] * 4, out_specs=[spec] * 3,
        out_shape=[jax.ShapeDtypeStruct((R, C), F32)] * 3,
    )(w, g, m, v)


def _log_sigmoid(z):
    return jnp.minimum(z, 0.0) - jnp.log(1.0 + jnp.exp(-jnp.abs(z)))


def _dot3(x, m01):
    x1 = x.astype(BF16)
    r1 = x - x1.astype(F32)
    x2 = r1.astype(BF16)
    x3 = (r1 - x2.astype(F32)).astype(BF16)
    n = x.shape[0]
    if n % 16:
        return (jnp.dot(x1, m01, preferred_element_type=F32) + jnp.dot(x2, m01, preferred_element_type=F32)
                + jnp.dot(x3, m01, preferred_element_type=F32))
    y = jnp.dot(jnp.concatenate([x1, x2, x3], axis=0), m01, preferred_element_type=F32)
    return y[:n] + y[n:2 * n] + y[2 * n:]


def _dot_nt(a, b):
    return lax.dot_general(a, b, NT, preferred_element_type=F32)


def _dot_tn(a, b):
    return lax.dot_general(a, b, TN, preferred_element_type=F32)


def _iota2(shape):
    return lax.broadcasted_iota(jnp.int32, shape, 0), lax.broadcasted_iota(jnp.int32, shape, 1)


HEADS_PER_STEP = 4
HEADS = range(HEADS_PER_STEP)


CHUNK_HEADS_PER_STEP = 2
CHUNK_HEADS = range(CHUNK_HEADS_PER_STEP)


def _head_spec(T, rows=None, width=HEAD_DIM, heads=HEADS_PER_STEP):
    return pl.BlockSpec((heads, T if rows is None else rows, width), lambda g: (g, 0, 0))


def _sb_weights(qb, kb, t0, s0, racc, m_gt, row, col):
    z = _dot_nt(qb, kb) * SCALE
    strict = (s0 + col) < (t0 + row)
    lb = _log_sigmoid(z)
    lf = jnp.where(strict, lb - z, 0.0)
    between = _dot3(lf, m_gt) + racc
    w = jnp.where(strict, jnp.exp(lb + between), 0.0)
    return strict, lb, lf, w


def _sb_fwd(q, k, v, name):
    H, T, _ = q.shape
    nb = T // QB

    def body(q_ref, k_ref, v_ref, o_ref):
        row, col = _iota2((QB, QB))
        m_gt = (row > col).astype(BF16)

        def qblock(i, _):
            t0 = pl.multiple_of(i * QB, QB)
            qbs = [q_ref[h, pl.ds(t0, QB), :].astype(BF16) for h in HEADS]

            def kblock(jj, carry):
                s0 = pl.multiple_of((i - jj) * QB, QB)
                out = []
                for h in HEADS:
                    acc, racc = carry[h]
                    kb = k_ref[h, pl.ds(s0, QB), :].astype(BF16)
                    vb = v_ref[h, pl.ds(s0, QB), :].astype(BF16)
                    _, _, lf, w = _sb_weights(qbs[h], kb, t0, s0, racc, m_gt, row, col)
                    acc = acc + jnp.dot(w.astype(BF16), vb, preferred_element_type=F32)
                    out.append((acc, racc + jnp.sum(lf, axis=1, keepdims=True)))
                return tuple(out)

            res = lax.fori_loop(0, i + 1, kblock,
                                tuple((jnp.zeros((QB, HEAD_DIM), F32), jnp.zeros((QB, 1), F32)) for _ in HEADS))
            for h in HEADS:
                o_ref[h, pl.ds(t0, QB), :] = res[h][0]
            return 0

        lax.fori_loop(0, nb, qblock, 0)

    spec = _head_spec(T)
    return _pcall(body, name=name, grid=(H // HEADS_PER_STEP,), in_specs=[spec] * 3, out_specs=spec,
                  out_shape=jax.ShapeDtypeStruct((H, T, HEAD_DIM), F32))(q, k, v)


def _sb_bwd(q, k, v, do, name):
    H, T, _ = q.shape
    nb = T // QB

    def body(q_ref, k_ref, v_ref, do_ref, dq_ref, dk_ref, dv_ref, racc_ref):
        row, col = _iota2((QB, QB))
        m_gt = (row > col).astype(BF16)
        m_lt = (row < col).astype(BF16)
        dk_ref[...] = jnp.zeros_like(dk_ref)
        dv_ref[...] = jnp.zeros_like(dv_ref)

        def qblock(i, _):
            t0 = pl.multiple_of(i * QB, QB)
            qbs = [q_ref[h, pl.ds(t0, QB), :].astype(BF16) for h in HEADS]
            dobs = [do_ref[h, pl.ds(t0, QB), :].astype(BF16) for h in HEADS]

            def suffix(jj, raccs):
                j = i - jj
                s0 = pl.multiple_of(j * QB, QB)
                out = []
                for h in HEADS:
                    kb = k_ref[h, pl.ds(s0, QB), :].astype(BF16)
                    z = _dot_nt(qbs[h], kb) * SCALE
                    lf = jnp.where((s0 + col) < (t0 + row), _log_sigmoid(z) - z, 0.0)
                    racc_ref[h, j] = raccs[h]
                    out.append(raccs[h] + jnp.sum(lf, axis=1, keepdims=True))
                return tuple(out)

            lax.fori_loop(0, i + 1, suffix, tuple(jnp.zeros((QB, 1), F32) for _ in HEADS))

            def kblock(j, carry):
                s0 = pl.multiple_of(j * QB, QB)
                out = []
                for h in HEADS:
                    dq, cacc = carry[h]
                    kb = k_ref[h, pl.ds(s0, QB), :].astype(BF16)
                    vb = v_ref[h, pl.ds(s0, QB), :].astype(BF16)
                    strict, lb, _, w = _sb_weights(qbs[h], kb, t0, s0, racc_ref[h, j], m_gt, row, col)
                    e = _dot_nt(dobs[h], vb) * w
                    c_left = _dot3(e, m_lt) + cacc
                    sig = jnp.exp(lb)
                    dz = jnp.where(strict, e * (1.0 - sig) - c_left * sig, 0.0).astype(BF16)
                    dq = dq + jnp.dot(dz, kb, preferred_element_type=F32)
                    dk_ref[h, pl.ds(s0, QB), :] += _dot_tn(dz, qbs[h]) * SCALE
                    dv_ref[h, pl.ds(s0, QB), :] += _dot_tn(w.astype(BF16), dobs[h])
                    out.append((dq, cacc + jnp.sum(e, axis=1, keepdims=True)))
                return tuple(out)

            res = lax.fori_loop(0, i + 1, kblock,
                                tuple((jnp.zeros((QB, HEAD_DIM), F32), jnp.zeros((QB, 1), F32)) for _ in HEADS))
            for h in HEADS:
                dq_ref[h, pl.ds(t0, QB), :] = res[h][0] * SCALE
            return 0

        lax.fori_loop(0, nb, qblock, 0)

    spec = _head_spec(T)
    return _pcall(body, name=name, grid=(H // HEADS_PER_STEP,), in_specs=[spec] * 4, out_specs=[spec] * 3,
                  out_shape=[jax.ShapeDtypeStruct((H, T, HEAD_DIM), F32)] * 3,
                  scratch_shapes=[pltpu.VMEM((HEADS_PER_STEP, nb, QB, 1), F32)])(q, k, v, do)


def _fox_logits(qb, kb, fq, fk, t0, s0, row, col):
    z = _dot_nt(qb, kb) * SCALE + fq - fk
    return jnp.where((s0 + col) <= (t0 + row), z, NEG)


def _fox_specs(T):
    return _head_spec(T, width=1), _head_spec(T, rows=T // QB, width=QB)


def _fox_fwd(q, k, v, fq, fk, name):
    H, T, _ = q.shape
    nb = T // QB

    def body(q_ref, k_ref, v_ref, fq_ref, fk_ref, o_ref, lse_ref):
        row, col = _iota2((QB, QB))

        def qblock(i, _):
            t0 = pl.multiple_of(i * QB, QB)
            qbs = [q_ref[h, pl.ds(t0, QB), :].astype(BF16) for h in HEADS]
            fqs = [fq_ref[h, pl.ds(t0, QB), :] for h in HEADS]

            def kblock(j, carry):
                s0 = pl.multiple_of(j * QB, QB)
                out = []
                for h in HEADS:
                    acc, m, l = carry[h]
                    kb = k_ref[h, pl.ds(s0, QB), :].astype(BF16)
                    vb = v_ref[h, pl.ds(s0, QB), :].astype(BF16)
                    z = _fox_logits(qbs[h], kb, fqs[h], fk_ref[h, pl.ds(j, 1), :], t0, s0, row, col)
                    m_new = jnp.maximum(m, jnp.max(z, axis=1, keepdims=True))
                    a = jnp.exp(m - m_new)
                    p = jnp.exp(z - m_new)
                    acc = a * acc + jnp.dot(p.astype(BF16), vb, preferred_element_type=F32)
                    out.append((acc, m_new, a * l + jnp.sum(p, axis=1, keepdims=True)))
                return tuple(out)

            res = lax.fori_loop(0, i + 1, kblock,
                                tuple((jnp.zeros((QB, HEAD_DIM), F32), jnp.full((QB, 1), NEG, F32),
                                       jnp.zeros((QB, 1), F32)) for _ in HEADS))
            for h in HEADS:
                acc, m, l = res[h]
                o_ref[h, pl.ds(t0, QB), :] = acc / l
                lse_ref[h, pl.ds(t0, QB), :] = m + jnp.log(l)
            return 0

        lax.fori_loop(0, nb, qblock, 0)

    spec = _head_spec(T)
    fq_spec, fk_spec = _fox_specs(T)
    return _pcall(body, name=name, grid=(H // HEADS_PER_STEP,), in_specs=[spec] * 3 + [fq_spec, fk_spec],
                  out_specs=[spec, fq_spec],
                  out_shape=[jax.ShapeDtypeStruct((H, T, HEAD_DIM), F32), jax.ShapeDtypeStruct((H, T, 1), F32)],
                  )(q, k, v, fq, fk)


def _fox_bwd(q, k, v, fq, fk, lse, do, name):
    H, T, _ = q.shape
    nb = T // QB

    def body(q_ref, k_ref, v_ref, fq_ref, fk_ref, lse_ref, do_ref, dq_ref, dk_ref, dv_ref, dfk_ref):
        row, col = _iota2((QB, QB))
        dk_ref[...] = jnp.zeros_like(dk_ref)
        dv_ref[...] = jnp.zeros_like(dv_ref)
        dfk_ref[...] = jnp.zeros_like(dfk_ref)

        def qblock(i, _):
            t0 = pl.multiple_of(i * QB, QB)
            qbs = [q_ref[h, pl.ds(t0, QB), :].astype(BF16) for h in HEADS]
            fqs = [fq_ref[h, pl.ds(t0, QB), :] for h in HEADS]
            lses = [lse_ref[h, pl.ds(t0, QB), :] for h in HEADS]
            dobs = [do_ref[h, pl.ds(t0, QB), :].astype(BF16) for h in HEADS]

            def probs(h, j):
                s0 = pl.multiple_of(j * QB, QB)
                kb = k_ref[h, pl.ds(s0, QB), :].astype(BF16)
                vb = v_ref[h, pl.ds(s0, QB), :].astype(BF16)
                z = _fox_logits(qbs[h], kb, fqs[h], fk_ref[h, pl.ds(j, 1), :], t0, s0, row, col)
                return s0, kb, jnp.exp(z - lses[h]), _dot_nt(dobs[h], vb)

            def row_dot(j, deltas):
                out = []
                for h in HEADS:
                    _, _, p, dp = probs(h, j)
                    out.append(deltas[h] + jnp.sum(p * dp, axis=1, keepdims=True))
                return tuple(out)

            deltas = lax.fori_loop(0, i + 1, row_dot, tuple(jnp.zeros((QB, 1), F32) for _ in HEADS))

            def kblock(j, dqs):
                out = []
                for h in HEADS:
                    s0, kb, p, dp = probs(h, j)
                    dz = p * (dp - deltas[h])
                    dzb = dz.astype(BF16)
                    dk_ref[h, pl.ds(s0, QB), :] += _dot_tn(dzb, qbs[h]) * SCALE
                    dv_ref[h, pl.ds(s0, QB), :] += _dot_tn(p.astype(BF16), dobs[h])
                    dfk_ref[h, pl.ds(j, 1), :] += -jnp.sum(dz, axis=0, keepdims=True)
                    out.append(dqs[h] + jnp.dot(dzb, kb, preferred_element_type=F32))
                return tuple(out)

            dqs = lax.fori_loop(0, i + 1, kblock, tuple(jnp.zeros((QB, HEAD_DIM), F32) for _ in HEADS))
            for h in HEADS:
                dq_ref[h, pl.ds(t0, QB), :] = dqs[h] * SCALE
            return 0

        lax.fori_loop(0, nb, qblock, 0)

    spec = _head_spec(T)
    fq_spec, fk_spec = _fox_specs(T)
    return _pcall(body, name=name, grid=(H // HEADS_PER_STEP,),
                  in_specs=[spec] * 3 + [fq_spec, fk_spec, fq_spec, spec],
                  out_specs=[spec, spec, spec, fk_spec],
                  out_shape=[jax.ShapeDtypeStruct((H, T, HEAD_DIM), F32)] * 3
                  + [jax.ShapeDtypeStruct((H, nb, QB), F32)],
                  )(q, k, v, fq, fk, lse, do)


def _forget_cumsum(flog, name):
    R, T = flog.shape
    nb = T // QB

    def body(x_ref, o_ref):
        row, col = _iota2((QB, QB))
        m_le = (row <= col).astype(BF16)
        carry = jnp.zeros((R, 1), F32)
        for b in range(nb):
            lf = _log_sigmoid(x_ref[:, b * QB:(b + 1) * QB])
            o_ref[:, b * QB:(b + 1) * QB] = _dot3(lf, m_le) + carry
            carry = carry + jnp.sum(lf, axis=1, keepdims=True)

    spec = pl.BlockSpec((R, T), lambda: (0, 0))
    return _pcall(body, name=name, in_specs=[spec], out_specs=spec,
                  out_shape=jax.ShapeDtypeStruct((R, T), F32))(flog)


def _forget_cumsum_bwd(flog, df, name):
    R, T = flog.shape
    nb = T // QB

    def body(x_ref, df_ref, o_ref):
        row, col = _iota2((QB, QB))
        m_ge = (row >= col).astype(BF16)
        carry = jnp.zeros((R, 1), F32)
        for b in reversed(range(nb)):
            dfb = df_ref[:, b * QB:(b + 1) * QB]
            dlog = _dot3(dfb, m_ge) + carry
            o_ref[:, b * QB:(b + 1) * QB] = dlog * _sigmoid(-x_ref[:, b * QB:(b + 1) * QB])
            carry = carry + jnp.sum(dfb, axis=1, keepdims=True)

    spec = pl.BlockSpec((R, T), lambda: (0, 0))
    return _pcall(body, name=name, in_specs=[spec, spec], out_specs=spec,
                  out_shape=jax.ShapeDtypeStruct((R, T), F32))(flog, df)


def _chunk_probs(qc, kb, bias, c, col):
    z = _dot_nt(qc, kb) * SCALE + bias
    z = jnp.where((c - (LEFT_CHUNKS + 1)) * CHUNK + col >= jnp.maximum(0, (c - LEFT_CHUNKS) * CHUNK), z, NEG)
    p = jnp.exp(z - jnp.max(z, axis=1, keepdims=True))
    return p, jnp.sum(p, axis=1, keepdims=True)


def _chunk_specs(T):
    n = CHUNK_HEADS_PER_STEP
    return (_head_spec(T, heads=n), _head_spec(T, rows=T + BAND_PAD, heads=n),
            _head_spec(T, rows=CHUNK, width=BAND, heads=n))


def _chunk_fwd(q, kp, vp, bias, name):
    H, T, _ = q.shape
    nc = T // CHUNK

    def body(q_ref, kp_ref, vp_ref, bias_ref, o_ref):
        _, col = _iota2((CHUNK, BAND))

        def chunk(c, _):
            t0 = pl.multiple_of(c * CHUNK, CHUNK)
            for h in CHUNK_HEADS:
                qc = q_ref[h, pl.ds(t0, CHUNK), :].astype(BF16)
                kb = kp_ref[h, pl.ds(t0, BAND), :].astype(BF16)
                vb = vp_ref[h, pl.ds(t0, BAND), :].astype(BF16)
                p, l = _chunk_probs(qc, kb, bias_ref[h], c, col)
                o_ref[h, pl.ds(t0, CHUNK), :] = jnp.dot(p.astype(BF16), vb, preferred_element_type=F32) / l
            return 0

        lax.fori_loop(0, nc, chunk, 0)

    q_spec, kp_spec, b_spec = _chunk_specs(T)
    return _pcall(body, name=name, grid=(H // CHUNK_HEADS_PER_STEP,), in_specs=[q_spec, kp_spec, kp_spec, b_spec],
                  out_specs=q_spec,
                  out_shape=jax.ShapeDtypeStruct((H, T, HEAD_DIM), F32))(q, kp, vp, bias)


def _chunk_bwd(q, kp, vp, bias, do, name):
    H, T, _ = q.shape
    nc = T // CHUNK

    def body(q_ref, kp_ref, vp_ref, bias_ref, do_ref, dq_ref, dkp_ref, dvp_ref, db_ref):
        _, col = _iota2((CHUNK, BAND))
        dkp_ref[...] = jnp.zeros_like(dkp_ref)
        dvp_ref[...] = jnp.zeros_like(dvp_ref)
        db_ref[...] = jnp.zeros_like(db_ref)

        def chunk(c, _):
            t0 = pl.multiple_of(c * CHUNK, CHUNK)
            for h in CHUNK_HEADS:
                qc = q_ref[h, pl.ds(t0, CHUNK), :].astype(BF16)
                kb = kp_ref[h, pl.ds(t0, BAND), :].astype(BF16)
                vb = vp_ref[h, pl.ds(t0, BAND), :].astype(BF16)
                dob = do_ref[h, pl.ds(t0, CHUNK), :].astype(BF16)
                p, l = _chunk_probs(qc, kb, bias_ref[h], c, col)
                p = p / l
                dp = _dot_nt(dob, vb)
                dz = p * (dp - jnp.sum(p * dp, axis=1, keepdims=True))
                dzb = dz.astype(BF16)
                dq_ref[h, pl.ds(t0, CHUNK), :] = jnp.dot(dzb, kb, preferred_element_type=F32) * SCALE
                dkp_ref[h, pl.ds(t0, BAND), :] += _dot_tn(dzb, qc) * SCALE
                dvp_ref[h, pl.ds(t0, BAND), :] += _dot_tn(p.astype(BF16), dob)
                db_ref[h] += dz
            return 0

        lax.fori_loop(0, nc, chunk, 0)

    q_spec, kp_spec, b_spec = _chunk_specs(T)
    return _pcall(body, name=name, grid=(H // CHUNK_HEADS_PER_STEP,),
                  in_specs=[q_spec, kp_spec, kp_spec, b_spec, q_spec],
                  out_specs=[q_spec, kp_spec, kp_spec, b_spec],
                  out_shape=[jax.ShapeDtypeStruct((H, T, HEAD_DIM), F32),
                             jax.ShapeDtypeStruct((H, T + BAND_PAD, HEAD_DIM), F32),
                             jax.ShapeDtypeStruct((H, T + BAND_PAD, HEAD_DIM), F32),
                             jax.ShapeDtypeStruct((H, CHUNK, BAND), F32)])(q, kp, vp, bias, do)


HBM_SPEC = pl.BlockSpec(memory_space=pltpu.HBM)


def _position():
    return lax.axis_index("x"), lax.axis_index("y"), lax.axis_index("c")


def _other_chips(x, y):
    chips = [(1 - x, y), (x, 1 - y), (1 - x, 1 - y)]
    return [(chip, 2 * chip[0] + chip[1]) for chip in chips]


def _remote_copy(src, dst, send_sems, recv_sems, k, to):
    return pltpu.make_async_remote_copy(src_ref=src, dst_ref=dst, send_sem=send_sems.at[k], recv_sem=recv_sems.at[k],
                                        device_id=to, device_id_type=MESH)


def _gather_weights(shards, name):
    n = len(shards)

    def body(*refs):
        src, dst = refs[:n], refs[n:2 * n]
        send_sems, recv_sems = refs[2 * n:]
        x, y, c = _position()
        me = 2 * x + y
        sibling = (x, y, 1 - c)
        others = _other_chips(x, y)
        first = [_remote_copy(src[i].at[c], dst[i].at[c, me], send_sems, recv_sems, 3 * i + k, (*chip, c))
                 for i in range(n) for k, (chip, _) in enumerate(others)]
        for cp in first:
            cp.start()
        passed = []
        for i in range(n):
            for k, (_, idx) in enumerate(others):
                landed = dst[i].at[c, idx]
                _remote_copy(landed, landed, send_sems, recv_sems, 3 * i + k, sibling).wait_recv()
                passed.append(_remote_copy(landed, landed, send_sems, recv_sems, 3 * (n + i) + k, sibling))
                passed[-1].start()
        for i in range(n):
            for k, (_, idx) in enumerate(others):
                from_sibling = dst[i].at[1 - c, idx]
                _remote_copy(from_sibling, from_sibling, send_sems, recv_sems, 3 * (n + i) + k, sibling).wait_recv()
        for cp in first + passed:
            cp.wait_send()

    return _pcall(
        body, name=name, in_specs=[HBM_SPEC] * n, out_specs=[HBM_SPEC] * n,
        out_shape=[jax.ShapeDtypeStruct((DEPTH, N_CHIPS) + s.shape[1:], s.dtype) for s in shards],
        scratch_shapes=[pltpu.SemaphoreType.DMA((6 * n,)), pltpu.SemaphoreType.DMA((6 * n,))],
    )(*shards)


def _send_other_half(parts, name):
    n = len(parts)

    def body(*refs):
        src, got = refs[:n], refs[n:2 * n]
        send_sems, recv_sems = refs[2 * n:]
        x, y, c = _position()
        copies = [_remote_copy(src[i].at[1 - c], got[i], send_sems, recv_sems, i, (x, y, 1 - c)) for i in range(n)]
        for cp in copies:
            cp.start()
        for cp in copies:
            cp.wait()

    return _pcall(
        body, name=name, in_specs=[HBM_SPEC] * n, out_specs=[HBM_SPEC] * n,
        out_shape=[jax.ShapeDtypeStruct(p.shape[1:], p.dtype) for p in parts],
        scratch_shapes=[pltpu.SemaphoreType.DMA((n,)), pltpu.SemaphoreType.DMA((n,))],
    )(*parts)


def _scatter_chips(parts, name):
    n = len(parts)

    def body(*refs):
        src, dst = refs[:n], refs[n:2 * n]
        send_sems, recv_sems = refs[2 * n:]
        x, y, c = _position()
        others = _other_chips(x, y)
        sends = [_remote_copy(src[i].at[idx], dst[i].at[k], send_sems, recv_sems, 3 * i + k, (*chip, c))
                 for i in range(n) for k, (chip, idx) in enumerate(others)]
        for cp in sends:
            cp.start()
        for cp in sends:
            cp.wait()

    return _pcall(
        body, name=name, in_specs=[HBM_SPEC] * n, out_specs=[HBM_SPEC] * n,
        out_shape=[jax.ShapeDtypeStruct((3,) + p.shape[1:], p.dtype) for p in parts],
        scratch_shapes=[pltpu.SemaphoreType.DMA((3 * n,)), pltpu.SemaphoreType.DMA((3 * n,))],
    )(*parts)


def _swap_with_sibling(arrs, name):
    n = len(arrs)

    def body(*refs):
        src, dst = refs[:n], refs[n:2 * n]
        send_sems, recv_sems = refs[2 * n:]
        x, y, c = _position()
        copies = [_remote_copy(src[i], dst[i], send_sems, recv_sems, i, (x, y, 1 - c)) for i in range(n)]
        for cp in copies:
            cp.start()
        for cp in copies:
            cp.wait()

    return _pcall(
        body, name=name, in_specs=[HBM_SPEC] * n, out_specs=[HBM_SPEC] * n,
        out_shape=[jax.ShapeDtypeStruct(a.shape, a.dtype) for a in arrs],
        scratch_shapes=[pltpu.SemaphoreType.DMA((n,)), pltpu.SemaphoreType.DMA((n,))],
    )(*arrs)


def _allreduce_small(v, name):
    R, C = v.shape

    def body(v_ref, o_ref, slots, send_sems, recv_sems):
        x, y, c = _position()
        me = 4 * x + 2 * y + c
        slots[0] = v_ref[...]
        sends = []
        for r in range(1, 8):
            fx, fy, fc = (r >> 2) & 1, (r >> 1) & 1, r & 1
            to = (x ^ fx, y ^ fy, c ^ fc)
            cp = pltpu.make_async_remote_copy(src_ref=v_ref, dst_ref=slots.at[r], send_sem=send_sems.at[r - 1],
                                              recv_sem=recv_sems.at[r - 1], device_id=to, device_id_type=MESH)
            cp.start()
            sends.append(cp)
        for cp in sends:
            cp.wait()
        acc = slots[me]
        for d in range(1, 8):
            acc = acc + slots[d ^ me]
        o_ref[...] = acc

    vmem = pl.BlockSpec(memory_space=pltpu.VMEM)
    return _pcall(
        body, name=name, in_specs=[vmem], out_specs=vmem, out_shape=jax.ShapeDtypeStruct((R, C), F32),
        scratch_shapes=[pltpu.VMEM((8, R, C), F32), pltpu.SemaphoreType.DMA((7,)), pltpu.SemaphoreType.DMA((7,))],
    )(v)


def _add_pair(which, both, got, name):
    _, N, R, C = both.shape
    tr = _pick(R, 512, 16)

    def body(which_ref, a_ref, b_ref, o_ref):
        o_ref[...] = (a_ref[...].astype(F32) + b_ref[...].astype(F32)).astype(BF16)

    spec = pl.BlockSpec((None, tr, C), lambda n, i, w: (n, i, 0))
    grid_spec = pltpu.PrefetchScalarGridSpec(
        num_scalar_prefetch=1, grid=(N, R // tr),
        in_specs=[pl.BlockSpec((None, None, tr, C), lambda n, i, w: (w[0], n, i, 0)), spec], out_specs=spec)
    return _pcall(body, name=name, grid_spec=grid_spec,
                  out_shape=jax.ShapeDtypeStruct((N, R, C), BF16))(which, both, got)


def _sum_chips(which, own, others, name):
    N, R, C = others.shape
    tr = _pick(R, 512, 16)

    def body(which_ref, own_ref, p_ref, o_ref):
        acc = own_ref[...].astype(F32)
        for n in range(N):
            acc = acc + p_ref[n].astype(F32)
        o_ref[...] = acc

    grid_spec = pltpu.PrefetchScalarGridSpec(
        num_scalar_prefetch=1, grid=(R // tr,),
        in_specs=[pl.BlockSpec((None, tr, C), lambda i, w: (w[0], i, 0)), pl.BlockSpec((N, tr, C), lambda i, w: (0, i, 0))],
        out_specs=pl.BlockSpec((tr, C), lambda i, w: (i, 0)))
    return _pcall(body, name=name, grid_spec=grid_spec,
                  out_shape=jax.ShapeDtypeStruct((R, C), F32))(which, own, others)


BIG = ("w_ffn1_in", "w_ffn1_out", "w_in", "w_br_sb", "w_br_ch", "w_br_fox", "w_out", "w_ffn2_in", "w_ffn2_out")
ROW_SHARDED = ("w_ffn1_out", "w_out", "w_ffn2_out")
SMALL = ("g_ffn1", "g_mix", "b_in", "rel_bias", "g_ffn2", "g_final")
SHARD_SHAPES = {
    "w_ffn1_in": (D_MODEL, 2 * D_FF // N_CHIPS), "w_ffn1_out": (D_FF // N_CHIPS, D_MODEL),
    "w_in": (D_MODEL, IN_WIDTH // N_CHIPS), "w_br_sb": (W_SB, D_MODEL // N_CHIPS),
    "w_br_ch": (W_CH, D_MODEL // N_CHIPS), "w_br_fox": (W_FOX, D_MODEL // N_CHIPS),
    "w_out": (D_MODEL // N_CHIPS, D_MODEL), "w_ffn2_in": (D_MODEL, 2 * D_FF // N_CHIPS),
    "w_ffn2_out": (D_FF // N_CHIPS, D_MODEL),
}
def _reduce_scatter_grads(split, axes):
    core = lax.axis_index("c")
    chip = 2 * lax.axis_index("x") + lax.axis_index("y")
    core_arr, chip_arr = core.astype(jnp.int32)[None], chip.astype(jnp.int32)[None]
    got = _send_other_half(split, "grad_split")
    pair = [_add_pair(core_arr, a, b, f"grad_pair_sum_{i}") for i, (a, b) in enumerate(zip(split, got))]
    landed = _scatter_chips(pair, "grad_scatter")
    mine = [_sum_chips(chip_arr, p, q, f"grad_chip_sum_{i}") for i, (p, q) in enumerate(zip(pair, landed))]
    theirs = _swap_with_sibling(mine, "grad_share")
    first = core == 0
    return [jnp.concatenate([jnp.where(first, a, b), jnp.where(first, b, a)], axis=ax)
            for a, b, ax in zip(mine, theirs, axes)]


SMALL_SIZES = {"g_ffn1": DEPTH * D_MODEL, "g_mix": DEPTH * D_MODEL, "b_in": DEPTH * IN_WIDTH,
               "rel_bias": DEPTH * N_REL * H_CH, "g_ffn2": DEPTH * D_MODEL, "g_final": D_MODEL}
SMALL_TOTAL = sum(SMALL_SIZES.values()) + 1
SMALL_ROWS = -(-SMALL_TOTAL // (8 * 128)) * 8


def _pack_small(vals, extra):
    flat = [vals[n].reshape(-1) for n in SMALL] + [extra.reshape(-1)]
    flat.append(jnp.zeros((SMALL_ROWS * 128 - SMALL_TOTAL,), F32))
    return jnp.concatenate(flat).reshape(SMALL_ROWS, 128)


def _unpack_small(pack, shapes):
    flat, out, off = pack.reshape(-1), {}, 0
    for n in SMALL:
        out[n] = flat[off:off + SMALL_SIZES[n]].reshape(shapes[n])
        off += SMALL_SIZES[n]
    return out, flat[off]


def _to_heads(t, n_heads):
    return jnp.transpose(t.reshape(t.shape[0], n_heads, HEAD_DIM), (1, 0, 2)).astype(BF16)


def _from_heads(t):
    return jnp.transpose(t, (1, 0, 2)).reshape(t.shape[1], t.shape[0] * HEAD_DIM)


def _pad_keys(t):
    return jnp.pad(t, ((0, 0), (BAND_PAD, 0), (0, 0)))


DIAGONALS = CHUNK + BAND - 1


def _band_bias(table):
    n_far = (LEFT_CHUNKS + 1) * CHUNK + CHUNK - MAX_REL
    near = table[N_REL - 1 - (DIAGONALS - n_far):N_REL - 1][::-1]
    diag = jnp.concatenate([jnp.broadcast_to(table[N_REL - 1], (n_far, H_CH)), near], axis=0).T
    diag = jnp.pad(diag, ((0, 0), (0, 1)))
    skew = jnp.tile(diag, (1, CHUNK))[:, :CHUNK * DIAGONALS].reshape(H_CH, CHUNK, DIAGONALS)
    return skew[:, :, CHUNK - 1:]


def _pad_w_in(w):
    qkv, f, gates = w[:, :QKV_WIDTH], w[:, QKV_WIDTH:QKV_WIDTH + H_FOX], w[:, QKV_WIDTH + H_FOX:]
    return jnp.concatenate([qkv, gates, f, jnp.zeros((w.shape[0], F_PAD - H_FOX), w.dtype)], axis=1)


def _unpad_w_in(w):
    return jnp.concatenate([w[:, :QKV_WIDTH], w[:, QKV_WIDTH + GATE_WIDTH:QKV_WIDTH + GATE_WIDTH + H_FOX],
                            w[:, QKV_WIDTH:QKV_WIDTH + GATE_WIDTH]], axis=1)


QKV_SPLITS = np.cumsum([0, W_SB, W_SB, W_SB, W_CH, W_CH, W_CH, W_FOX, W_FOX, W_FOX])


def _by_chip_rows(g):
    return g.reshape(2, N_CHIPS, g.shape[1] // N_CHIPS, g.shape[2])


def _ffn_fwd(x, g, w_in, w_out, layer, tag):
    h = _rmsnorm_fwd(x, g, f"{tag}_norm")
    gu = _mm(h, w_in, mode="nn", name=f"{tag}_in", gathered=(layer, "col"), out_dtype=BF16)
    a = _swiglu_fwd(gu, f"{tag}_act")
    y = _mm(a, w_out, mode="nn", name=f"{tag}_out", alpha=0.5, res=x, gathered=(layer, "row"))
    return y, (h, gu, a)


def _ffn_bwd(x, g, w_in, w_out, layer, saved, dy, tag):
    h, gu, a = saved
    da = _mm(dy, w_out, mode="nt", name=f"{tag}_da", alpha=0.5, gathered=(layer, "row"), out_dtype=BF16)
    dw_out = _mm(a, dy, mode="tn", name=f"{tag}_dwout", alpha=0.5, tm_cap=1408, out_dtype=BF16, out_split="row")
    dgu = _swiglu_bwd(gu, da, f"{tag}_dact")
    dw_in = _mm(h, dgu, mode="tn", name=f"{tag}_dwin", tm_cap=512, out_dtype=BF16, out_split="col")
    dh = _mm(dgu, w_in, mode="nt", name=f"{tag}_dh", gathered=(layer, "col"), tn_cap=1024)
    dx, dg = _rmsnorm_bwd(x, g, dh, dy, f"{tag}_dnorm")
    return dx, dg, dw_in, _by_chip_rows(dw_out)


def _mixer_fwd(x, lw, tag):
    T = x.shape[0]
    h = _rmsnorm_fwd(x, lw["g_mix"], f"{tag}_norm")
    p = _mm(h, lw["w_in"], mode="nn", name=f"{tag}_proj", bias=lw["b_in"], tn_cap=896)
    parts = [p[:, QKV_SPLITS[i]:QKV_SPLITS[i + 1]] for i in range(9)]
    qa, ka, va = (_to_heads(t, H_SB) for t in parts[0:3])
    qb, kb, vb = (_to_heads(t, H_CH) for t in parts[3:6])
    qc, kc, vc = (_to_heads(t, H_FOX) for t in parts[6:9])
    flog = jnp.pad(p[:, QKV_WIDTH + GATE_WIDTH:QKV_WIDTH + GATE_WIDTH + H_FOX].T, ((0, 8 - H_FOX), (0, 0)))
    fcum = _forget_cumsum(flog, f"{tag}_fcum")[:H_FOX]
    fq, fk = fcum.reshape(H_FOX, T, 1), fcum.reshape(H_FOX, T // QB, QB)
    kbp, vbp = _pad_keys(kb), _pad_keys(vb)
    oa = _sb_fwd(qa, ka, va, f"{tag}_sb")
    ob = _chunk_fwd(qb, kbp, vbp, lw["bias"], f"{tag}_ch")
    oc, lse = _fox_fwd(qc, kc, vc, fq, fk, f"{tag}_fox")
    oam, obm, ocm = _from_heads(oa), _from_heads(ob), _from_heads(oc)
    col, row = (lw["layer"], "col"), (lw["layer"], "row")
    ya = _mm(oam, lw["w_br_sb"], mode="nn", name=f"{tag}_bra", gathered=col)
    yb = _mm(obm, lw["w_br_ch"], mode="nn", name=f"{tag}_brb", gathered=col)
    yc = _mm(ocm, lw["w_br_fox"], mode="nn", name=f"{tag}_brc", gathered=col)
    merged = _gate_fwd(p, ya, yb, yc, f"{tag}_gate")
    y = _mm(merged, lw["w_out"], mode="nn", name=f"{tag}_out", res=x, gathered=row)
    saved = (h, p, (qa, ka, va), (qb, kbp, vbp), (qc, kc, vc), flog, fq, fk, lse, (oam, obm, ocm),
             (ya, yb, yc), merged)
    return y, saved


def _mixer_bwd(x, lw, saved, dy, tag):
    (h, p, (qa, ka, va), (qb, kbp, vbp), (qc, kc, vc), flog, fq, fk, lse, (oam, obm, ocm),
     (ya, yb, yc), merged) = saved
    T = x.shape[0]
    grads = {}
    col, row = (lw["layer"], "col"), (lw["layer"], "row")
    dmerged = _mm(dy, lw["w_out"], mode="nt", name=f"{tag}_dmerged", gathered=row)
    grads["w_out"] = _by_chip_rows(_mm(merged, dy, mode="tn", name=f"{tag}_dwout", tm_cap=1024, out_dtype=BF16,
                                       out_split="row"))
    dya, dyb, dyc, dgate = _gate_bwd(p, ya, yb, yc, dmerged, f"{tag}_dgate")
    doa = _mm(dya, lw["w_br_sb"], mode="nt", name=f"{tag}_doa", gathered=col)
    dob = _mm(dyb, lw["w_br_ch"], mode="nt", name=f"{tag}_dob", gathered=col)
    doc = _mm(dyc, lw["w_br_fox"], mode="nt", name=f"{tag}_doc", gathered=col)
    by_chip = dict(mode="tn", tm_cap=512, out_dtype=BF16, out_split="col")
    grads["w_br_sb"] = _mm(oam, dya, name=f"{tag}_dwbra", **by_chip)
    grads["w_br_ch"] = _mm(obm, dyb, name=f"{tag}_dwbrb", **by_chip)
    grads["w_br_fox"] = _mm(ocm, dyc, name=f"{tag}_dwbrc", **by_chip)
    dqa, dka, dva = _sb_bwd(qa, ka, va, _to_heads(doa, H_SB), f"{tag}_dsb")
    dqb, dkbp, dvbp, dbias = _chunk_bwd(qb, kbp, vbp, lw["bias"], _to_heads(dob, H_CH), f"{tag}_dch")
    dqc, dkc, dvc, dfk = _fox_bwd(qc, kc, vc, fq, fk, lse, _to_heads(doc, H_FOX), f"{tag}_dfox")
    dflog = _forget_cumsum_bwd(flog, jnp.pad(dfk.reshape(H_FOX, T), ((0, 8 - H_FOX), (0, 0))), f"{tag}_dfcum")
    dqkv = [_from_heads(t) for t in (dqa, dka, dva, dqb, dkbp[:, BAND_PAD:], dvbp[:, BAND_PAD:], dqc, dkc, dvc)]
    dp = jnp.concatenate(dqkv + [dgate, jnp.pad(dflog[:H_FOX].T, ((0, 0), (0, F_PAD - H_FOX)))], axis=1)
    grads["b_in"] = _colsum(dp, f"{tag}_dbin")
    dpb = dp.astype(BF16)
    dw_in = _unpad_w_in(_mm(h, dpb, mode="tn", name=f"{tag}_dwin", tm_cap=512, tn_cap=896, out_dtype=BF16))
    grads["w_in"] = jnp.transpose(dw_in.reshape(2, D_MODEL // 2, N_CHIPS, IN_WIDTH // N_CHIPS), (0, 2, 1, 3))
    dh = _mm(dpb, lw["w_in"], mode="nt", name=f"{tag}_dh", tk_cap=896, tn_cap=1024)
    dx, grads["g_mix"] = _rmsnorm_bwd(x, lw["g_mix"], dh, dy, f"{tag}_dnorm")
    grads["rel_bias"] = lw["bias_vjp"](dbias)[0]
    return dx, grads


def kernel(x, g_ffn1, w_ffn1_in, w_ffn1_out, g_mix, w_in, b_in, rel_bias, w_br_sb, w_br_ch, w_br_fox, w_out, g_ffn2, w_ffn2_in, w_ffn2_out, g_final, loss_target, m_g_ffn1, m_w_ffn1_in, m_w_ffn1_out, m_g_mix, m_w_in, m_b_in, m_rel_bias, m_w_br_sb, m_w_br_ch, m_w_br_fox, m_w_out, m_g_ffn2, m_w_ffn2_in, m_w_ffn2_out, m_g_final, v_g_ffn1, v_w_ffn1_in, v_w_ffn1_out, v_g_mix, v_w_in, v_b_in, v_rel_bias, v_w_br_sb, v_w_br_ch, v_w_br_fox, v_w_out, v_g_ffn2, v_w_ffn2_in, v_w_ffn2_out, v_g_final):
    names = ("g_ffn1", "w_ffn1_in", "w_ffn1_out", "g_mix", "w_in", "b_in", "rel_bias", "w_br_sb", "w_br_ch",
             "w_br_fox", "w_out", "g_ffn2", "w_ffn2_in", "w_ffn2_out", "g_final")
    w = dict(zip(names, (g_ffn1, w_ffn1_in, w_ffn1_out, g_mix, w_in, b_in, rel_bias, w_br_sb, w_br_ch,
                         w_br_fox, w_out, g_ffn2, w_ffn2_in, w_ffn2_out, g_final)))
    m = dict(zip(names, (m_g_ffn1, m_w_ffn1_in, m_w_ffn1_out, m_g_mix, m_w_in, m_b_in, m_rel_bias, m_w_br_sb,
                         m_w_br_ch, m_w_br_fox, m_w_out, m_g_ffn2, m_w_ffn2_in, m_w_ffn2_out, m_g_final)))
    v = dict(zip(names, (v_g_ffn1, v_w_ffn1_in, v_w_ffn1_out, v_g_mix, v_w_in, v_b_in, v_rel_bias, v_w_br_sb,
                         v_w_br_ch, v_w_br_fox, v_w_out, v_g_ffn2, v_w_ffn2_in, v_w_ffn2_out, v_g_final)))
    xs = x[0]
    tgt = loss_target[0]

    own = [w[n].astype(BF16) for n in BIG]
    chip = 2 * lax.axis_index("x") + lax.axis_index("y")
    gathered = {n: lax.dynamic_update_slice(g, o[:, None], (0, chip, 0, 0))
                for n, g, o in zip(BIG, _gather_weights(own, "gather_weights"), own)}
    layers = []
    for l in range(DEPTH):
        lw = dict(gathered)
        lw["layer"] = l
        lw["w_in"] = _pad_w_in(jnp.concatenate([gathered["w_in"][l, j] for j in range(N_CHIPS)], axis=1))
        lw["b_in"] = _pad_w_in(w["b_in"][l][None, :])
        lw["bias"], lw["bias_vjp"] = jax.vjp(_band_bias, w["rel_bias"][l])
        for n in ("g_ffn1", "g_mix", "g_ffn2"):
            lw[n] = w[n][l][None, :]
        layers.append(lw)

    saved = []
    act = xs
    for l, lw in enumerate(layers):
        x0 = act
        x1, s1 = _ffn_fwd(x0, lw["g_ffn1"], lw["w_ffn1_in"], lw["w_ffn1_out"], l, f"l{l}_ffn1")
        x2, s2 = _mixer_fwd(x1, lw, f"l{l}_mix")
        act, s3 = _ffn_fwd(x2, lw["g_ffn2"], lw["w_ffn2_in"], lw["w_ffn2_out"], l, f"l{l}_ffn2")
        saved.append((x0, s1, x1, s2, x2, s3))

    dact, dg_final, loss_row = _final_loss(act, w["g_final"][None, :], tgt, "final_loss")

    big_grads = {n: [None] * DEPTH for n in BIG}
    small_grads = {n: [None] * DEPTH for n in ("g_ffn1", "g_mix", "b_in", "rel_bias", "g_ffn2")}
    for l in reversed(range(DEPTH)):
        lw = layers[l]
        x0, s1, x1, s2, x2, s3 = saved[l]
        dx2, small_grads["g_ffn2"][l], big_grads["w_ffn2_in"][l], big_grads["w_ffn2_out"][l] = _ffn_bwd(
            x2, lw["g_ffn2"], lw["w_ffn2_in"], lw["w_ffn2_out"], l, s3, dact, f"l{l}_ffn2")
        dx1, mg = _mixer_bwd(x1, lw, s2, dx2, f"l{l}_mix")
        for n in ("w_in", "w_br_sb", "w_br_ch", "w_br_fox", "w_out"):
            big_grads[n][l] = mg[n]
        small_grads["b_in"][l] = _unpad_w_in(mg["b_in"])[0]
        small_grads["g_mix"][l] = mg["g_mix"][0]
        small_grads["rel_bias"][l] = mg["rel_bias"]
        dact, dg1, big_grads["w_ffn1_in"][l], big_grads["w_ffn1_out"][l] = _ffn_bwd(
            x0, lw["g_ffn1"], lw["w_ffn1_in"], lw["w_ffn1_out"], l, s1, dx1, f"l{l}_ffn1")
        small_grads["g_ffn1"][l] = dg1[0]
        small_grads["g_ffn2"][l] = small_grads["g_ffn2"][l][0]
    grad_x = dact[None]

    small = {n: jnp.stack(small_grads[n]) for n in small_grads}
    small["g_final"] = dg_final[0]
    small_sum, loss = _unpack_small(_allreduce_small(_pack_small(small, loss_row[0, :1]), "allreduce_small"),
                                    {n: w[n].shape for n in SMALL})

    order = [(n, l) for n in BIG for l in range(DEPTH)]
    summed = _reduce_scatter_grads([big_grads[n][l] for n, l in order],
                                   [1 if n in ROW_SHARDED else 0 for n, l in order])
    grads = {n: jnp.stack(summed[DEPTH * i:DEPTH * (i + 1)]) for i, n in enumerate(BIG)}
    grads.update(small_sum)

    delta, new_m, new_v = {}, {}, {}
    for n in BIG:
        shape = w[n].shape
        flat = lambda t: t.reshape(-1, shape[-1])
        d, nm, nv = _adamw(flat(w[n]), flat(grads[n]), flat(m[n]), flat(v[n]), f"adamw_{n}")
        delta[n], new_m[n], new_v[n] = d.reshape(shape), nm.reshape(shape), nv.reshape(shape)
    zero = jnp.zeros((1,), F32)
    sd, sm, sv = _adamw(_pack_small(w, zero), _pack_small(grads, zero), _pack_small(m, zero), _pack_small(v, zero),
                        "adamw_small")
    shapes = {n: w[n].shape for n in SMALL}
    for dst, src in ((delta, sd), (new_m, sm), (new_v, sv)):
        dst.update(_unpack_small(src, shapes)[0])

    return (loss, grad_x, *[grads[n] for n in names], *[delta[n] for n in names],
            *[new_m[n] for n in names], *[new_v[n] for n in names])
```

```python
import functools

import numpy as np
import jax
import jax.numpy as jnp
from jax import lax
from jax.experimental import pallas as pl
from jax.experimental.pallas import tpu as pltpu

F32 = jnp.float32
BF16 = jnp.bfloat16

D_MODEL = 1024
DEPTH = 2
CHUNK = 64
HEAD_DIM = 64
H_SB, H_CH, H_FOX = 4, 8, 4
W_SB, W_CH, W_FOX = H_SB * HEAD_DIM, H_CH * HEAD_DIM, H_FOX * HEAD_DIM
LEFT_CHUNKS = 8
MAX_REL = 128
N_REL = 2 * MAX_REL + 1
D_FF = 2816
QKV_WIDTH = 3 * (W_SB + W_CH + W_FOX)
GATE_WIDTH = 3 * D_MODEL
IN_WIDTH = QKV_WIDTH + H_FOX + GATE_WIDTH
F_PAD = 128
P_WIDTH = QKV_WIDTH + GATE_WIDTH + F_PAD
RMS_EPS = 1e-6
NEG = -1e30
SCALE = HEAD_DIM ** -0.5
QB = 256
BAND = (LEFT_CHUNKS + 2) * CHUNK
BAND_PAD = BAND - CHUNK

ADAM_LR, ADAM_B1, ADAM_B2, ADAM_EPS, ADAM_WD, ADAM_STEP = 0.001, 0.9, 0.999, 1e-08, 0.01, 10

N_CHIPS = 4
PACK_COLS = 1024
VMEM_BUDGET = 52 * 1024 * 1024

NT = (((1,), (1,)), ((), ()))
TN = (((0,), (0,)), ((), ()))
NN = (((1,), (0,)), ((), ()))
MESH = pl.DeviceIdType.MESH


def _pcall(body, **kw):
    return pl.pallas_call(body, **kw)


class _Carry:
    def __init__(self, srcs, bufs, new, count, plan):
        self.srcs, self.bufs, self.new, self.count, self.plan = list(srcs), list(bufs), list(new), count, plan
        self.out = None


def _pcall_carrying(body, carry, **kw):
    if carry is None:
        return _pcall(body, **kw)
    in_specs = list(kw.pop("in_specs"))
    out_specs, out_shape = kw.pop("out_specs"), kw.pop("out_shape")
    single = not isinstance(out_shape, (list, tuple))
    out_specs = [out_specs] if single else list(out_specs)
    out_shape = [out_shape] if single else list(out_shape)
    scratch = list(kw.pop("scratch_shapes", []))
    grid = tuple(kw.get("grid", ()))
    n_in, n_out, n_scr = len(in_specs), len(out_specs), len(scratch)
    ns, nb, nn = len(carry.srcs), len(carry.bufs), len(carry.new)

    def body2(*refs):
        ins, srcs = refs[:n_in], refs[n_in:n_in + ns]
        at = n_in + ns + nb
        outs, bufs, new = refs[at:at + n_out], refs[at + n_out:at + n_out + nb], refs[at + n_out + nb:at + n_out + nb + nn]
        at += n_out + nb + nn
        scr, (send_sems, recv_sems) = refs[at:at + n_scr], refs[at + n_scr:]

        def copies():
            return [_remote_copy(s, d, send_sems, recv_sems, k, to)
                    for k, (s, d, to) in enumerate(carry.plan(srcs, bufs, new))]

        def start():
            for cp in copies():
                cp.start()

        def wait():
            for cp in copies():
                cp.wait()

        if grid:
            ids = [pl.program_id(a) for a in range(len(grid))]
            pl.when(functools.reduce(jnp.logical_and, [i == 0 for i in ids]))(start)
        else:
            start()
        body(*ins, *outs, *scr)
        if grid:
            pl.when(functools.reduce(jnp.logical_and, [i == g - 1 for i, g in zip(ids, grid)]))(wait)
        else:
            wait()

    call = _pcall(
        body2, in_specs=in_specs + [HBM_SPEC] * (ns + nb), out_specs=out_specs + [HBM_SPEC] * (nb + nn),
        out_shape=out_shape + [jax.ShapeDtypeStruct(b.shape, b.dtype) for b in carry.bufs] + carry.new,
        scratch_shapes=scratch + [pltpu.SemaphoreType.DMA((carry.count,)), pltpu.SemaphoreType.DMA((carry.count,))],
        input_output_aliases={n_in + ns + j: n_out + j for j in range(nb)}, **kw)

    def apply(*args):
        res = call(*args, *carry.srcs, *carry.bufs)
        carry.out = (list(res[n_out:n_out + nb]), list(res[n_out + nb:]))
        return res[0] if single else list(res[:n_out])

    return apply


def _pick(n, cap, mult=128):
    best = None
    d = mult
    while d <= min(n, cap):
        if n % d == 0:
            best = d
        d += mult
    return n if best is None else best


def _nbytes(shape, dtype):
    return int(np.prod(shape)) * jnp.dtype(dtype).itemsize


def _mm(a, b, *, mode, name, out_dtype=F32, alpha=1.0, bias=None, res=None, tm_cap=1024, tn_cap=512,
        tk_cap=2816, gathered=None, out_split=None, carry=None):
    if mode == "tn":
        K, M = a.shape
    else:
        M, K = a.shape
    dn = {"nn": NN, "nt": NT, "tn": TN}[mode]
    b_rows = None
    if gathered is None:
        N = b.shape[0] if mode == "nt" else b.shape[1]
        tn = {None: _pick(N, tn_cap), "col": N // N_CHIPS, "row": N // 2}[out_split]
        if out_split == "col":
            tm_cap = min(tm_cap, M // 2)
        tk = K if K <= tk_cap else _pick(K, tk_cap)
        b_spec = (pl.BlockSpec((tn, tk), lambda i, j, k: (j, k)) if mode == "nt"
                  else pl.BlockSpec((tk, tn), lambda i, j, k: (k, j)))
    else:
        r, c = b.shape[1:]
        rows, cols = (r, N_CHIPS * c) if gathered == "col" else (N_CHIPS * r, c)
        N = rows if mode == "nt" else cols
        assert K == (cols if mode == "nt" else rows), (name, K, rows, cols)
        if gathered == "col" and mode == "nn":
            tn, tk = c, (r if r <= tk_cap else _pick(r, tk_cap))
            b_spec = pl.BlockSpec((None, tk, c), lambda i, j, k: (j, k, 0))
        elif gathered == "col":
            tn, tk = _pick(r, tn_cap), c
            b_spec = pl.BlockSpec((None, tn, c), lambda i, j, k: (k, j, 0))
        elif mode == "nn":
            tn, tk, b_rows = _pick(c, tn_cap), K, N_CHIPS
            b_spec = pl.BlockSpec((N_CHIPS, r, tn), lambda i, j, k: (0, 0, j))
        else:
            tn, tk, b_rows = 2 * r, K, 2
            b_spec = pl.BlockSpec((2, r, c), lambda i, j, k: (j, 0, 0))
    tm = _pick(M, tm_cap)
    nk = K // tk

    a_spec = (pl.BlockSpec((tk, tm), lambda i, j, k: (k, i)) if mode == "tn"
              else pl.BlockSpec((tm, tk), lambda i, j, k: (i, k)))
    in_specs = [a_spec, b_spec]
    args = [a, b]
    if bias is not None:
        in_specs.append(pl.BlockSpec((1, tn), lambda i, j, k: (0, j)))
        args.append(bias)
    if res is not None:
        in_specs.append(pl.BlockSpec((tm, tn), lambda i, j, k: (i, j)))
        args.append(res)

    est = 2 * (_nbytes((tm, tk), a.dtype) + _nbytes((tk, tn), b.dtype) + _nbytes((tm, tn), out_dtype))
    est += _nbytes((tm, tn), F32) * (3 if res is not None else 1)
    assert est < VMEM_BUDGET, (name, est)

    def body(*refs):
        a_ref, b_ref = refs[0], refs[1]
        pos = 2
        bias_ref = res_ref = None
        if bias is not None:
            bias_ref = refs[pos]
            pos += 1
        if res is not None:
            res_ref = refs[pos]
            pos += 1
        o_ref = refs[pos]
        acc_ref = refs[pos + 1] if nk > 1 else None

        bv = b_ref[...]
        if b_rows is not None:
            bv = bv.reshape(b_rows * bv.shape[1], bv.shape[2])
        part = lax.dot_general(a_ref[...].astype(BF16), bv.astype(BF16), dn, preferred_element_type=F32)

        def finish(acc):
            if alpha != 1.0:
                acc = acc * alpha
            if bias_ref is not None:
                acc = acc + bias_ref[...]
            if res_ref is not None:
                acc = acc + res_ref[...]
            o_ref[...] = acc.astype(out_dtype)

        if nk == 1:
            finish(part)
        else:
            k = pl.program_id(2)

            @pl.when(k == 0)
            def _():
                acc_ref[...] = part

            @pl.when(k > 0)
            def _():
                acc_ref[...] += part

            @pl.when(k == nk - 1)
            def _():
                finish(acc_ref[...])

    if out_split == "col":
        per_half = M // 2 // tm
        out_spec = pl.BlockSpec((None, None, tm, tn), lambda i, j, k: (i // per_half, j, i % per_half, 0))
        out_shape = jax.ShapeDtypeStruct((2, N_CHIPS, M // 2, tn), out_dtype)
    elif out_split == "row":
        out_spec = pl.BlockSpec((None, tm, tn), lambda i, j, k: (j, i, 0))
        out_shape = jax.ShapeDtypeStruct((2, M, tn), out_dtype)
    else:
        out_spec = pl.BlockSpec((tm, tn), lambda i, j, k: (i, j))
        out_shape = jax.ShapeDtypeStruct((M, N), out_dtype)
    return _pcall_carrying(
        body, carry, name=name, grid=(M // tm, N // tn, nk), in_specs=in_specs, out_specs=out_spec,
        out_shape=out_shape,
        scratch_shapes=[pltpu.VMEM((tm, tn), F32)] if nk > 1 else [],
        compiler_params=pltpu.CompilerParams(dimension_semantics=("parallel", "parallel", "arbitrary")),
    )(*args)


def _rmsnorm_fwd(x, g, name):
    T, Dm = x.shape
    tm = _pick(T, 256, 8)

    def body(x_ref, g_ref, h_ref):
        xv = x_ref[...]
        rstd = lax.rsqrt(jnp.mean(xv * xv, axis=-1, keepdims=True) + RMS_EPS)
        h_ref[...] = (xv * rstd * g_ref[...]).astype(BF16)

    return _pcall(
        body, name=name, grid=(T // tm,),
        in_specs=[pl.BlockSpec((tm, Dm), lambda i: (i, 0)), pl.BlockSpec((1, Dm), lambda i: (0, 0))],
        out_specs=pl.BlockSpec((tm, Dm), lambda i: (i, 0)),
        out_shape=jax.ShapeDtypeStruct((T, Dm), BF16),
    )(x, g)


def _rmsnorm_bwd(x, g, dh, dres, name):
    T, Dm = x.shape
    tm = _pick(T, 256, 8)

    def body(x_ref, g_ref, dh_ref, dres_ref, dx_ref, dg_ref):
        xv = x_ref[...]
        rstd = lax.rsqrt(jnp.mean(xv * xv, axis=-1, keepdims=True) + RMS_EPS)
        xhat = xv * rstd
        dhv = dh_ref[...]
        dyg = dhv * g_ref[...]
        dx_ref[...] = dres_ref[...] + rstd * (dyg - xhat * jnp.mean(dyg * xhat, axis=-1, keepdims=True))
        part = jnp.sum(dhv * xhat, axis=0, keepdims=True)

        @pl.when(pl.program_id(0) == 0)
        def _():
            dg_ref[...] = part

        @pl.when(pl.program_id(0) > 0)
        def _():
            dg_ref[...] += part

    row = pl.BlockSpec((tm, Dm), lambda i: (i, 0))
    vec = pl.BlockSpec((1, Dm), lambda i: (0, 0))
    return _pcall(
        body, name=name, grid=(T // tm,), in_specs=[row, vec, row, row], out_specs=[row, vec],
        out_shape=[jax.ShapeDtypeStruct((T, Dm), F32), jax.ShapeDtypeStruct((1, Dm), F32)],
        compiler_params=pltpu.CompilerParams(dimension_semantics=("arbitrary",)),
    )(x, g, dh, dres)


def _final_loss(x, g, tgt, name):
    T, Dm = x.shape
    tm = _pick(T, 256, 8)

    def body(x_ref, g_ref, t_ref, dx_ref, dg_ref, loss_ref):
        xv = x_ref[...]
        gv = g_ref[...]
        rstd = lax.rsqrt(jnp.mean(xv * xv, axis=-1, keepdims=True) + RMS_EPS)
        xhat = xv * rstd
        err = xhat * gv - t_ref[...]
        part_loss = 0.5 * jnp.sum(jnp.mean(err * err, axis=-1, keepdims=True), axis=0, keepdims=True)
        dy = err * (1.0 / Dm)
        dyg = dy * gv
        dx_ref[...] = rstd * (dyg - xhat * jnp.mean(dyg * xhat, axis=-1, keepdims=True))
        part_g = jnp.sum(dy * xhat, axis=0, keepdims=True)
        part_l = jnp.broadcast_to(part_loss, (1, 128))

        @pl.when(pl.program_id(0) == 0)
        def _():
            dg_ref[...] = part_g
            loss_ref[...] = part_l

        @pl.when(pl.program_id(0) > 0)
        def _():
            dg_ref[...] += part_g
            loss_ref[...] += part_l

    row = pl.BlockSpec((tm, Dm), lambda i: (i, 0))
    vec = pl.BlockSpec((1, Dm), lambda i: (0, 0))
    return _pcall(
        body, name=name, grid=(T // tm,), in_specs=[row, vec, row],
        out_specs=[row, vec, pl.BlockSpec((1, 128), lambda i: (0, 0))],
        out_shape=[jax.ShapeDtypeStruct((T, Dm), F32), jax.ShapeDtypeStruct((1, Dm), F32),
                   jax.ShapeDtypeStruct((1, 128), F32)],
        compiler_params=pltpu.CompilerParams(dimension_semantics=("arbitrary",)),
    )(x, g, tgt)


def _sigmoid(z):
    return 1.0 / (1.0 + jnp.exp(-z))


def _swiglu_fwd(gu, name):
    T = gu.shape[0]
    tm = _pick(T, 256, 16)

    def body(gu_ref, a_ref):
        gv = gu_ref[:, :D_FF].astype(F32)
        uv = gu_ref[:, D_FF:].astype(F32)
        a_ref[...] = (gv * _sigmoid(gv) * uv).astype(BF16)

    return _pcall(
        body, name=name, grid=(T // tm,), in_specs=[pl.BlockSpec((tm, 2 * D_FF), lambda i: (i, 0))],
        out_specs=pl.BlockSpec((tm, D_FF), lambda i: (i, 0)),
        out_shape=jax.ShapeDtypeStruct((T, D_FF), BF16),
    )(gu)


def _swiglu_bwd(gu, da, name):
    T = gu.shape[0]
    tm = _pick(T, 256, 16)

    def body(gu_ref, da_ref, d_ref):
        gv = gu_ref[:, :D_FF].astype(F32)
        uv = gu_ref[:, D_FF:].astype(F32)
        dav = da_ref[...].astype(F32)
        sg = _sigmoid(gv)
        d_ref[:, :D_FF] = (dav * uv * (sg + gv * sg * (1.0 - sg))).astype(BF16)
        d_ref[:, D_FF:] = (dav * gv * sg).astype(BF16)

    return _pcall(
        body, name=name, grid=(T // tm,),
        in_specs=[pl.BlockSpec((tm, 2 * D_FF), lambda i: (i, 0)), pl.BlockSpec((tm, D_FF), lambda i: (i, 0))],
        out_specs=pl.BlockSpec((tm, 2 * D_FF), lambda i: (i, 0)),
        out_shape=jax.ShapeDtypeStruct((T, 2 * D_FF), BF16),
    )(gu, da)


GATE_BLOCK0 = QKV_WIDTH // D_MODEL


def _gate_fwd(p, ya, yb, yc, name):
    T = p.shape[0]
    tm = _pick(T, 256, 8)

    def body(ga_ref, gb_ref, gc_ref, ya_ref, yb_ref, yc_ref, o_ref):
        o_ref[...] = (_sigmoid(ga_ref[...]) * ya_ref[...] + _sigmoid(gb_ref[...]) * yb_ref[...]
                      + _sigmoid(gc_ref[...]) * yc_ref[...]).astype(BF16)

    gate = [pl.BlockSpec((tm, D_MODEL), functools.partial(lambda i, kk: (i, GATE_BLOCK0 + kk), kk=kk))
            for kk in range(3)]
    row = pl.BlockSpec((tm, D_MODEL), lambda i: (i, 0))
    return _pcall(
        body, name=name, grid=(T // tm,), in_specs=gate + [row, row, row], out_specs=row,
        out_shape=jax.ShapeDtypeStruct((T, D_MODEL), BF16),
    )(p, p, p, ya, yb, yc)


def _gate_bwd(p, ya, yb, yc, dm, name):
    T = p.shape[0]
    tm = _pick(T, 256, 8)

    def body(ga_ref, gb_ref, gc_ref, ya_ref, yb_ref, yc_ref, dm_ref, da_ref, db_ref, dc_ref, dg_ref):
        dmv = dm_ref[...]
        for kk, (g_ref, y_ref, d_ref) in enumerate(((ga_ref, ya_ref, da_ref), (gb_ref, yb_ref, db_ref),
                                                    (gc_ref, yc_ref, dc_ref))):
            s = _sigmoid(g_ref[...])
            d_ref[...] = (s * dmv).astype(BF16)
            dg_ref[:, kk * D_MODEL:(kk + 1) * D_MODEL] = dmv * y_ref[...] * s * (1.0 - s)

    gate = [pl.BlockSpec((tm, D_MODEL), functools.partial(lambda i, kk: (i, GATE_BLOCK0 + kk), kk=kk))
            for kk in range(3)]
    row = pl.BlockSpec((tm, D_MODEL), lambda i: (i, 0))
    return _pcall(
        body, name=name, grid=(T // tm,), in_specs=gate + [row, row, row, row],
        out_specs=[row, row, row, pl.BlockSpec((tm, GATE_WIDTH), lambda i: (i, 0))],
        out_shape=[jax.ShapeDtypeStruct((T, D_MODEL), BF16)] * 3 + [jax.ShapeDtypeStruct((T, GATE_WIDTH), F32)],
    )(p, p, p, ya, yb, yc, dm)


def _colsum(a, name):
    T, N = a.shape
    tm = _pick(T, 256, 8)

    def body(a_ref, o_ref):
        part = jnp.sum(a_ref[...], axis=0, keepdims=True)

        @pl.when(pl.program_id(0) == 0)
        def _():
            o_ref[...] = part

        @pl.when(pl.program_id(0) > 0)
        def _():
            o_ref[...] += part

    return _pcall(
        body, name=name, grid=(T // tm,), in_specs=[pl.BlockSpec((tm, N), lambda i: (i, 0))],
        out_specs=pl.BlockSpec((1, N), lambda i: (0, 0)), out_shape=jax.ShapeDtypeStruct((1, N), F32),
        compiler_params=pltpu.CompilerParams(dimension_semantics=("arbitrary",)),
    )(a)


def _adamw(w, g, m, v, name):
    R, C = w.shape
    tr = 256 if R % 256 == 0 else R
    c1 = 1.0 / (1.0 - ADAM_B1 ** ADAM_STEP)
    c2 = 1.0 / (1.0 - ADAM_B2 ** ADAM_STEP)

    def body(w_ref, g_ref, m_ref, v_ref, d_ref, nm_ref, nv_ref):
        gv = g_ref[...]
        mn = ADAM_B1 * m_ref[...] + (1.0 - ADAM_B1) * gv
        vn = ADAM_B2 * v_ref[...] + (1.0 - ADAM_B2) * (gv * gv)
        nm_ref[...] = mn
        nv_ref[...] = vn
        d_ref[...] = -ADAM_LR * ((mn * c1) / (jnp.sqrt(vn * c2) + ADAM_EPS) + ADAM_WD * w_ref[...])

    spec = pl.BlockSpec((tr, C), lambda i: (i, 0))
    return _pcall(
        body, name=name, grid=(R // tr,), in_specs=[spec] * 4, out_specs=[spec] * 3,
        out_shape=[jax.ShapeDtypeStruct((R, C), F32)] * 3,
    )(w, g, m, v)


def _log_sigmoid(z):
    return jnp.minimum(z, 0.0) - jnp.log(1.0 + jnp.exp(-jnp.abs(z)))


def _dot3(x, m01):
    x1 = x.astype(BF16)
    r1 = x - x1.astype(F32)
    x2 = r1.astype(BF16)
    x3 = (r1 - x2.astype(F32)).astype(BF16)
    n = x.shape[0]
    if n % 16:
        return (jnp.dot(x1, m01, preferred_element_type=F32) + jnp.dot(x2, m01, preferred_element_type=F32)
                + jnp.dot(x3, m01, preferred_element_type=F32))
    y = jnp.dot(jnp.concatenate([x1, x2, x3], axis=0), m01, preferred_element_type=F32)
    return y[:n] + y[n:2 * n] + y[2 * n:]


def _dot_nt(a, b):
    return lax.dot_general(a, b, NT, preferred_element_type=F32)


def _dot_tn(a, b):
    return lax.dot_general(a, b, TN, preferred_element_type=F32)


def _iota2(shape):
    return lax.broadcasted_iota(jnp.int32, shape, 0), lax.broadcasted_iota(jnp.int32, shape, 1)


HEADS_PER_STEP = 4
HEADS = range(HEADS_PER_STEP)


CHUNK_HEADS_PER_STEP = 2
CHUNK_HEADS = range(CHUNK_HEADS_PER_STEP)


def _head_spec(T, rows=None, width=HEAD_DIM, heads=HEADS_PER_STEP):
    return pl.BlockSpec((heads, T if rows is None else rows, width), lambda g: (g, 0, 0))


def _sb_weights(qb, kb, t0, s0, racc, m_gt, row, col):
    z = _dot_nt(qb, kb) * SCALE
    strict = (s0 + col) < (t0 + row)
    lb = _log_sigmoid(z)
    lf = jnp.where(strict, lb - z, 0.0)
    between = _dot3(lf, m_gt) + racc
    w = jnp.where(strict, jnp.exp(lb + between), 0.0)
    return strict, lb, lf, w


def _sb_fwd(q, k, v, name, carry=None):
    H, T, _ = q.shape
    nb = T // QB

    def body(q_ref, k_ref, v_ref, o_ref):
        row, col = _iota2((QB, QB))
        m_gt = (row > col).astype(BF16)

        def qblock(i, _):
            t0 = pl.multiple_of(i * QB, QB)
            qbs = [q_ref[h, pl.ds(t0, QB), :].astype(BF16) for h in HEADS]

            def kblock(jj, carry):
                s0 = pl.multiple_of((i - jj) * QB, QB)
                out = []
                for h in HEADS:
                    acc, racc = carry[h]
                    kb = k_ref[h, pl.ds(s0, QB), :].astype(BF16)
                    vb = v_ref[h, pl.ds(s0, QB), :].astype(BF16)
                    _, _, lf, w = _sb_weights(qbs[h], kb, t0, s0, racc, m_gt, row, col)
                    acc = acc + jnp.dot(w.astype(BF16), vb, preferred_element_type=F32)
                    out.append((acc, racc + jnp.sum(lf, axis=1, keepdims=True)))
                return tuple(out)

            res = lax.fori_loop(0, i + 1, kblock,
                                tuple((jnp.zeros((QB, HEAD_DIM), F32), jnp.zeros((QB, 1), F32)) for _ in HEADS))
            for h in HEADS:
                o_ref[h, pl.ds(t0, QB), :] = res[h][0]
            return 0

        lax.fori_loop(0, nb, qblock, 0)

    spec = _head_spec(T)
    return _pcall_carrying(body, carry, name=name, grid=(H // HEADS_PER_STEP,), in_specs=[spec] * 3, out_specs=spec,
                           out_shape=jax.ShapeDtypeStruct((H, T, HEAD_DIM), F32))(q, k, v)


def _sb_bwd(q, k, v, do, name, carry=None):
    H, T, _ = q.shape
    nb = T // QB

    def body(q_ref, k_ref, v_ref, do_ref, dq_ref, dk_ref, dv_ref, racc_ref):
        row, col = _iota2((QB, QB))
        m_gt = (row > col).astype(BF16)
        m_lt = (row < col).astype(BF16)
        dk_ref[...] = jnp.zeros_like(dk_ref)
        dv_ref[...] = jnp.zeros_like(dv_ref)

        def qblock(i, _):
            t0 = pl.multiple_of(i * QB, QB)
            qbs = [q_ref[h, pl.ds(t0, QB), :].astype(BF16) for h in HEADS]
            dobs = [do_ref[h, pl.ds(t0, QB), :].astype(BF16) for h in HEADS]

            def suffix(jj, raccs):
                j = i - jj
                s0 = pl.multiple_of(j * QB, QB)
                out = []
                for h in HEADS:
                    kb = k_ref[h, pl.ds(s0, QB), :].astype(BF16)
                    z = _dot_nt(qbs[h], kb) * SCALE
                    lf = jnp.where((s0 + col) < (t0 + row), _log_sigmoid(z) - z, 0.0)
                    racc_ref[h, j] = raccs[h]
                    out.append(raccs[h] + jnp.sum(lf, axis=1, keepdims=True))
                return tuple(out)

            lax.fori_loop(0, i + 1, suffix, tuple(jnp.zeros((QB, 1), F32) for _ in HEADS))

            def kblock(j, carry):
                s0 = pl.multiple_of(j * QB, QB)
                out = []
                for h in HEADS:
                    dq, cacc = carry[h]
                    kb = k_ref[h, pl.ds(s0, QB), :].astype(BF16)
                    vb = v_ref[h, pl.ds(s0, QB), :].astype(BF16)
                    strict, lb, _, w = _sb_weights(qbs[h], kb, t0, s0, racc_ref[h, j], m_gt, row, col)
                    e = _dot_nt(dobs[h], vb) * w
                    c_left = _dot3(e, m_lt) + cacc
                    sig = jnp.exp(lb)
                    dz = jnp.where(strict, e * (1.0 - sig) - c_left * sig, 0.0).astype(BF16)
                    dq = dq + jnp.dot(dz, kb, preferred_element_type=F32)
                    dk_ref[h, pl.ds(s0, QB), :] += _dot_tn(dz, qbs[h]) * SCALE
                    dv_ref[h, pl.ds(s0, QB), :] += _dot_tn(w.astype(BF16), dobs[h])
                    out.append((dq, cacc + jnp.sum(e, axis=1, keepdims=True)))
                return tuple(out)

            res = lax.fori_loop(0, i + 1, kblock,
                                tuple((jnp.zeros((QB, HEAD_DIM), F32), jnp.zeros((QB, 1), F32)) for _ in HEADS))
            for h in HEADS:
                dq_ref[h, pl.ds(t0, QB), :] = res[h][0] * SCALE
            return 0

        lax.fori_loop(0, nb, qblock, 0)

    spec = _head_spec(T)
    return _pcall_carrying(body, carry, name=name, grid=(H // HEADS_PER_STEP,), in_specs=[spec] * 4,
                           out_specs=[spec] * 3, out_shape=[jax.ShapeDtypeStruct((H, T, HEAD_DIM), F32)] * 3,
                           scratch_shapes=[pltpu.VMEM((HEADS_PER_STEP, nb, QB, 1), F32)])(q, k, v, do)


def _fox_logits(qb, kb, fq, fk, t0, s0, row, col):
    z = _dot_nt(qb, kb) * SCALE + fq - fk
    return jnp.where((s0 + col) <= (t0 + row), z, NEG)


def _fox_specs(T):
    return _head_spec(T, width=1), _head_spec(T, rows=T // QB, width=QB)


def _fox_fwd(q, k, v, fq, fk, name, carry=None):
    H, T, _ = q.shape
    nb = T // QB

    def body(q_ref, k_ref, v_ref, fq_ref, fk_ref, o_ref, lse_ref):
        row, col = _iota2((QB, QB))

        def qblock(i, _):
            t0 = pl.multiple_of(i * QB, QB)
            qbs = [q_ref[h, pl.ds(t0, QB), :].astype(BF16) for h in HEADS]
            fqs = [fq_ref[h, pl.ds(t0, QB), :] for h in HEADS]

            def kblock(j, carry):
                s0 = pl.multiple_of(j * QB, QB)
                out = []
                for h in HEADS:
                    acc, m, l = carry[h]
                    kb = k_ref[h, pl.ds(s0, QB), :].astype(BF16)
                    vb = v_ref[h, pl.ds(s0, QB), :].astype(BF16)
                    z = _fox_logits(qbs[h], kb, fqs[h], fk_ref[h, pl.ds(j, 1), :], t0, s0, row, col)
                    m_new = jnp.maximum(m, jnp.max(z, axis=1, keepdims=True))
                    a = jnp.exp(m - m_new)
                    p = jnp.exp(z - m_new)
                    acc = a * acc + jnp.dot(p.astype(BF16), vb, preferred_element_type=F32)
                    out.append((acc, m_new, a * l + jnp.sum(p, axis=1, keepdims=True)))
                return tuple(out)

            res = lax.fori_loop(0, i + 1, kblock,
                                tuple((jnp.zeros((QB, HEAD_DIM), F32), jnp.full((QB, 1), NEG, F32),
                                       jnp.zeros((QB, 1), F32)) for _ in HEADS))
            for h in HEADS:
                acc, m, l = res[h]
                o_ref[h, pl.ds(t0, QB), :] = acc / l
                lse_ref[h, pl.ds(t0, QB), :] = m + jnp.log(l)
            return 0

        lax.fori_loop(0, nb, qblock, 0)

    spec = _head_spec(T)
    fq_spec, fk_spec = _fox_specs(T)
    return _pcall_carrying(
        body, carry, name=name, grid=(H // HEADS_PER_STEP,), in_specs=[spec] * 3 + [fq_spec, fk_spec],
        out_specs=[spec, fq_spec],
        out_shape=[jax.ShapeDtypeStruct((H, T, HEAD_DIM), F32), jax.ShapeDtypeStruct((H, T, 1), F32)],
    )(q, k, v, fq, fk)


def _fox_bwd(q, k, v, fq, fk, lse, do, name):
    H, T, _ = q.shape
    nb = T // QB

    def body(q_ref, k_ref, v_ref, fq_ref, fk_ref, lse_ref, do_ref, dq_ref, dk_ref, dv_ref, dfk_ref):
        row, col = _iota2((QB, QB))
        dk_ref[...] = jnp.zeros_like(dk_ref)
        dv_ref[...] = jnp.zeros_like(dv_ref)
        dfk_ref[...] = jnp.zeros_like(dfk_ref)

        def qblock(i, _):
            t0 = pl.multiple_of(i * QB, QB)
            qbs = [q_ref[h, pl.ds(t0, QB), :].astype(BF16) for h in HEADS]
            fqs = [fq_ref[h, pl.ds(t0, QB), :] for h in HEADS]
            lses = [lse_ref[h, pl.ds(t0, QB), :] for h in HEADS]
            dobs = [do_ref[h, pl.ds(t0, QB), :].astype(BF16) for h in HEADS]

            def probs(h, j):
                s0 = pl.multiple_of(j * QB, QB)
                kb = k_ref[h, pl.ds(s0, QB), :].astype(BF16)
                vb = v_ref[h, pl.ds(s0, QB), :].astype(BF16)
                z = _fox_logits(qbs[h], kb, fqs[h], fk_ref[h, pl.ds(j, 1), :], t0, s0, row, col)
                return s0, kb, jnp.exp(z - lses[h]), _dot_nt(dobs[h], vb)

            def row_dot(j, deltas):
                out = []
                for h in HEADS:
                    _, _, p, dp = probs(h, j)
                    out.append(deltas[h] + jnp.sum(p * dp, axis=1, keepdims=True))
                return tuple(out)

            deltas = lax.fori_loop(0, i + 1, row_dot, tuple(jnp.zeros((QB, 1), F32) for _ in HEADS))

            def kblock(j, dqs):
                out = []
                for h in HEADS:
                    s0, kb, p, dp = probs(h, j)
                    dz = p * (dp - deltas[h])
                    dzb = dz.astype(BF16)
                    dk_ref[h, pl.ds(s0, QB), :] += _dot_tn(dzb, qbs[h]) * SCALE
                    dv_ref[h, pl.ds(s0, QB), :] += _dot_tn(p.astype(BF16), dobs[h])
                    dfk_ref[h, pl.ds(j, 1), :] += -jnp.sum(dz, axis=0, keepdims=True)
                    out.append(dqs[h] + jnp.dot(dzb, kb, preferred_element_type=F32))
                return tuple(out)

            dqs = lax.fori_loop(0, i + 1, kblock, tuple(jnp.zeros((QB, HEAD_DIM), F32) for _ in HEADS))
            for h in HEADS:
                dq_ref[h, pl.ds(t0, QB), :] = dqs[h] * SCALE
            return 0

        lax.fori_loop(0, nb, qblock, 0)

    spec = _head_spec(T)
    fq_spec, fk_spec = _fox_specs(T)
    return _pcall(body, name=name, grid=(H // HEADS_PER_STEP,),
                  in_specs=[spec] * 3 + [fq_spec, fk_spec, fq_spec, spec],
                  out_specs=[spec, spec, spec, fk_spec],
                  out_shape=[jax.ShapeDtypeStruct((H, T, HEAD_DIM), F32)] * 3
                  + [jax.ShapeDtypeStruct((H, nb, QB), F32)],
                  )(q, k, v, fq, fk, lse, do)


def _forget_cumsum(flog, name):
    R, T = flog.shape
    nb = T // QB

    def body(x_ref, o_ref):
        row, col = _iota2((QB, QB))
        m_le = (row <= col).astype(BF16)
        carry = jnp.zeros((R, 1), F32)
        for b in range(nb):
            lf = _log_sigmoid(x_ref[:, b * QB:(b + 1) * QB])
            o_ref[:, b * QB:(b + 1) * QB] = _dot3(lf, m_le) + carry
            carry = carry + jnp.sum(lf, axis=1, keepdims=True)

    spec = pl.BlockSpec((R, T), lambda: (0, 0))
    return _pcall(body, name=name, in_specs=[spec], out_specs=spec,
                  out_shape=jax.ShapeDtypeStruct((R, T), F32))(flog)


def _forget_cumsum_bwd(flog, df, name):
    R, T = flog.shape
    nb = T // QB

    def body(x_ref, df_ref, o_ref):
        row, col = _iota2((QB, QB))
        m_ge = (row >= col).astype(BF16)
        carry = jnp.zeros((R, 1), F32)
        for b in reversed(range(nb)):
            dfb = df_ref[:, b * QB:(b + 1) * QB]
            dlog = _dot3(dfb, m_ge) + carry
            o_ref[:, b * QB:(b + 1) * QB] = dlog * _sigmoid(-x_ref[:, b * QB:(b + 1) * QB])
            carry = carry + jnp.sum(dfb, axis=1, keepdims=True)

    spec = pl.BlockSpec((R, T), lambda: (0, 0))
    return _pcall(body, name=name, in_specs=[spec, spec], out_specs=spec,
                  out_shape=jax.ShapeDtypeStruct((R, T), F32))(flog, df)


def _chunk_probs(qc, kb, bias, c, col):
    z = _dot_nt(qc, kb) * SCALE + bias
    z = jnp.where((c - (LEFT_CHUNKS + 1)) * CHUNK + col >= jnp.maximum(0, (c - LEFT_CHUNKS) * CHUNK), z, NEG)
    p = jnp.exp(z - jnp.max(z, axis=1, keepdims=True))
    return p, jnp.sum(p, axis=1, keepdims=True)


def _chunk_specs(T):
    n = CHUNK_HEADS_PER_STEP
    return (_head_spec(T, heads=n), _head_spec(T, rows=T + BAND_PAD, heads=n),
            _head_spec(T, rows=CHUNK, width=BAND, heads=n))


def _chunk_fwd(q, kp, vp, bias, name, carry=None):
    H, T, _ = q.shape
    nc = T // CHUNK

    def body(q_ref, kp_ref, vp_ref, bias_ref, o_ref):
        _, col = _iota2((CHUNK, BAND))

        def chunk(c, _):
            t0 = pl.multiple_of(c * CHUNK, CHUNK)
            for h in CHUNK_HEADS:
                qc = q_ref[h, pl.ds(t0, CHUNK), :].astype(BF16)
                kb = kp_ref[h, pl.ds(t0, BAND), :].astype(BF16)
                vb = vp_ref[h, pl.ds(t0, BAND), :].astype(BF16)
                p, l = _chunk_probs(qc, kb, bias_ref[h], c, col)
                o_ref[h, pl.ds(t0, CHUNK), :] = jnp.dot(p.astype(BF16), vb, preferred_element_type=F32) / l
            return 0

        lax.fori_loop(0, nc, chunk, 0)

    q_spec, kp_spec, b_spec = _chunk_specs(T)
    return _pcall_carrying(body, carry, name=name, grid=(H // CHUNK_HEADS_PER_STEP,),
                           in_specs=[q_spec, kp_spec, kp_spec, b_spec], out_specs=q_spec,
                           out_shape=jax.ShapeDtypeStruct((H, T, HEAD_DIM), F32))(q, kp, vp, bias)


def _chunk_bwd(q, kp, vp, bias, do, name, carry=None):
    H, T, _ = q.shape
    nc = T // CHUNK

    def body(q_ref, kp_ref, vp_ref, bias_ref, do_ref, dq_ref, dkp_ref, dvp_ref, db_ref):
        _, col = _iota2((CHUNK, BAND))
        dkp_ref[...] = jnp.zeros_like(dkp_ref)
        dvp_ref[...] = jnp.zeros_like(dvp_ref)
        db_ref[...] = jnp.zeros_like(db_ref)

        def chunk(c, _):
            t0 = pl.multiple_of(c * CHUNK, CHUNK)
            for h in CHUNK_HEADS:
                qc = q_ref[h, pl.ds(t0, CHUNK), :].astype(BF16)
                kb = kp_ref[h, pl.ds(t0, BAND), :].astype(BF16)
                vb = vp_ref[h, pl.ds(t0, BAND), :].astype(BF16)
                dob = do_ref[h, pl.ds(t0, CHUNK), :].astype(BF16)
                p, l = _chunk_probs(qc, kb, bias_ref[h], c, col)
                p = p / l
                dp = _dot_nt(dob, vb)
                dz = p * (dp - jnp.sum(p * dp, axis=1, keepdims=True))
                dzb = dz.astype(BF16)
                dq_ref[h, pl.ds(t0, CHUNK), :] = jnp.dot(dzb, kb, preferred_element_type=F32) * SCALE
                dkp_ref[h, pl.ds(t0, BAND), :] += _dot_tn(dzb, qc) * SCALE
                dvp_ref[h, pl.ds(t0, BAND), :] += _dot_tn(p.astype(BF16), dob)
                db_ref[h] += dz
            return 0

        lax.fori_loop(0, nc, chunk, 0)

    q_spec, kp_spec, b_spec = _chunk_specs(T)
    return _pcall_carrying(body, carry, name=name, grid=(H // CHUNK_HEADS_PER_STEP,),
                  in_specs=[q_spec, kp_spec, kp_spec, b_spec, q_spec],
                  out_specs=[q_spec, kp_spec, kp_spec, b_spec],
                  out_shape=[jax.ShapeDtypeStruct((H, T, HEAD_DIM), F32),
                             jax.ShapeDtypeStruct((H, T + BAND_PAD, HEAD_DIM), F32),
                             jax.ShapeDtypeStruct((H, T + BAND_PAD, HEAD_DIM), F32),
                             jax.ShapeDtypeStruct((H, CHUNK, BAND), F32)])(q, kp, vp, bias, do)


HBM_SPEC = pl.BlockSpec(memory_space=pltpu.HBM)


def _position():
    return lax.axis_index("x"), lax.axis_index("y"), lax.axis_index("c")


def _other_chips(x, y):
    chips = [(1 - x, y), (x, 1 - y), (1 - x, 1 - y)]
    return [(chip, 2 * chip[0] + chip[1]) for chip in chips]


def _remote_copy(src, dst, send_sems, recv_sems, k, to):
    return pltpu.make_async_remote_copy(src_ref=src, dst_ref=dst, send_sem=send_sems.at[k], recv_sem=recv_sems.at[k],
                                        device_id=to, device_id_type=MESH)


def _gather_over_ici(layer):
    def plan(srcs, bufs, new):
        x, y, c = _position()
        return [(s.at[layer, c], b.at[2 * x + y, c], (*chip, c))
                for s, b in zip(srcs, bufs) for chip, _ in _other_chips(x, y)]
    return plan


def _gather_to_sibling(srcs, bufs, new):
    x, y, c = _position()
    return [(b.at[idx, c], b.at[idx, c], (x, y, 1 - c)) for b in bufs for _, idx in _other_chips(x, y)]


def _gather_layer(shards, bufs, layer, name):
    n = len(shards)

    def body(*refs):
        src, dst = refs[:n], refs[2 * n:3 * n]
        send_sems, recv_sems = refs[3 * n:]
        x, y, c = _position()
        me = 2 * x + y
        sibling = (x, y, 1 - c)
        others = _other_chips(x, y)
        first = [_remote_copy(src[i].at[layer, c], dst[i].at[me, c], send_sems, recv_sems, 3 * i + k, (*chip, c))
                 for i in range(n) for k, (chip, _) in enumerate(others)]
        for cp in first:
            cp.start()
        passed = []
        for i in range(n):
            for k, (_, idx) in enumerate(others):
                landed = dst[i].at[idx, c]
                _remote_copy(landed, landed, send_sems, recv_sems, 3 * i + k, sibling).wait_recv()
                passed.append(_remote_copy(landed, landed, send_sems, recv_sems, 3 * (n + i) + k, sibling))
                passed[-1].start()
        for i in range(n):
            for k, (_, idx) in enumerate(others):
                from_sibling = dst[i].at[idx, 1 - c]
                _remote_copy(from_sibling, from_sibling, send_sems, recv_sems, 3 * (n + i) + k, sibling).wait_recv()
        for cp in first + passed:
            cp.wait_send()

    return _pcall(
        body, name=name, in_specs=[HBM_SPEC] * (2 * n), out_specs=[HBM_SPEC] * n,
        out_shape=[jax.ShapeDtypeStruct(b.shape, b.dtype) for b in bufs],
        scratch_shapes=[pltpu.SemaphoreType.DMA((6 * n,)), pltpu.SemaphoreType.DMA((6 * n,))],
        input_output_aliases={n + i: i for i in range(n)},
    )(*shards, *bufs)


def _send_other_half(parts, name):
    n = len(parts)

    def body(*refs):
        src, got = refs[:n], refs[n:2 * n]
        send_sems, recv_sems = refs[2 * n:]
        x, y, c = _position()
        copies = [_remote_copy(src[i].at[1 - c], got[i], send_sems, recv_sems, i, (x, y, 1 - c)) for i in range(n)]
        for cp in copies:
            cp.start()
        for cp in copies:
            cp.wait()

    return _pcall(
        body, name=name, in_specs=[HBM_SPEC] * n, out_specs=[HBM_SPEC] * n,
        out_shape=[jax.ShapeDtypeStruct(p.shape[1:], p.dtype) for p in parts],
        scratch_shapes=[pltpu.SemaphoreType.DMA((n,)), pltpu.SemaphoreType.DMA((n,))],
    )(*parts)


def _scatter_chips(parts, name):
    n = len(parts)

    def body(*refs):
        src, dst = refs[:n], refs[n:2 * n]
        send_sems, recv_sems = refs[2 * n:]
        x, y, c = _position()
        others = _other_chips(x, y)
        sends = [_remote_copy(src[i].at[idx], dst[i].at[k], send_sems, recv_sems, 3 * i + k, (*chip, c))
                 for i in range(n) for k, (chip, idx) in enumerate(others)]
        for cp in sends:
            cp.start()
        for cp in sends:
            cp.wait()

    return _pcall(
        body, name=name, in_specs=[HBM_SPEC] * n, out_specs=[HBM_SPEC] * n,
        out_shape=[jax.ShapeDtypeStruct((3,) + p.shape[1:], p.dtype) for p in parts],
        scratch_shapes=[pltpu.SemaphoreType.DMA((3 * n,)), pltpu.SemaphoreType.DMA((3 * n,))],
    )(*parts)


def _swap_with_sibling(arrs, name):
    n = len(arrs)

    def body(*refs):
        src, dst = refs[:n], refs[n:2 * n]
        send_sems, recv_sems = refs[2 * n:]
        x, y, c = _position()
        copies = [_remote_copy(src[i], dst[i], send_sems, recv_sems, i, (x, y, 1 - c)) for i in range(n)]
        for cp in copies:
            cp.start()
        for cp in copies:
            cp.wait()

    return _pcall(
        body, name=name, in_specs=[HBM_SPEC] * n, out_specs=[HBM_SPEC] * n,
        out_shape=[jax.ShapeDtypeStruct(a.shape, a.dtype) for a in arrs],
        scratch_shapes=[pltpu.SemaphoreType.DMA((n,)), pltpu.SemaphoreType.DMA((n,))],
    )(*arrs)


def _allreduce_small(v, name):
    R, C = v.shape

    def body(v_ref, o_ref, slots, send_sems, recv_sems):
        x, y, c = _position()
        me = 4 * x + 2 * y + c
        slots[0] = v_ref[...]
        sends = []
        for r in range(1, 8):
            fx, fy, fc = (r >> 2) & 1, (r >> 1) & 1, r & 1
            to = (x ^ fx, y ^ fy, c ^ fc)
            cp = pltpu.make_async_remote_copy(src_ref=v_ref, dst_ref=slots.at[r], send_sem=send_sems.at[r - 1],
                                              recv_sem=recv_sems.at[r - 1], device_id=to, device_id_type=MESH)
            cp.start()
            sends.append(cp)
        for cp in sends:
            cp.wait()
        acc = slots[me]
        for d in range(1, 8):
            acc = acc + slots[d ^ me]
        o_ref[...] = acc

    vmem = pl.BlockSpec(memory_space=pltpu.VMEM)
    return _pcall(
        body, name=name, in_specs=[vmem], out_specs=vmem, out_shape=jax.ShapeDtypeStruct((R, C), F32),
        scratch_shapes=[pltpu.VMEM((8, R, C), F32), pltpu.SemaphoreType.DMA((7,)), pltpu.SemaphoreType.DMA((7,))],
    )(v)


def _add_pair(which, both, got, name):
    _, N, R, C = both.shape
    tr = _pick(R, 512, 16)

    def body(which_ref, a_ref, b_ref, o_ref):
        o_ref[...] = (a_ref[...].astype(F32) + b_ref[...].astype(F32)).astype(BF16)

    spec = pl.BlockSpec((None, tr, C), lambda n, i, w: (n, i, 0))
    grid_spec = pltpu.PrefetchScalarGridSpec(
        num_scalar_prefetch=1, grid=(N, R // tr),
        in_specs=[pl.BlockSpec((None, None, tr, C), lambda n, i, w: (w[0], n, i, 0)), spec], out_specs=spec)
    return _pcall(body, name=name, grid_spec=grid_spec,
                  out_shape=jax.ShapeDtypeStruct((N, R, C), BF16))(which, both, got)


def _sum_chips(which, own, others, name):
    N, R, C = others.shape
    tr = _pick(R, 512, 16)

    def body(which_ref, own_ref, p_ref, o_ref):
        acc = own_ref[...].astype(F32)
        for n in range(N):
            acc = acc + p_ref[n].astype(F32)
        o_ref[...] = acc

    grid_spec = pltpu.PrefetchScalarGridSpec(
        num_scalar_prefetch=1, grid=(R // tr,),
        in_specs=[pl.BlockSpec((None, tr, C), lambda i, w: (w[0], i, 0)), pl.BlockSpec((N, tr, C), lambda i, w: (0, i, 0))],
        out_specs=pl.BlockSpec((tr, C), lambda i, w: (i, 0)))
    return _pcall(body, name=name, grid_spec=grid_spec,
                  out_shape=jax.ShapeDtypeStruct((R, C), F32))(which, own, others)


BIG = ("w_ffn1_in", "w_ffn1_out", "w_in", "w_br_sb", "w_br_ch", "w_br_fox", "w_out", "w_ffn2_in", "w_ffn2_out")
ROW_SHARDED = ("w_ffn1_out", "w_out", "w_ffn2_out")
SMALL = ("g_ffn1", "g_mix", "b_in", "rel_bias", "g_ffn2", "g_final")
SHARD_SHAPES = {
    "w_ffn1_in": (D_MODEL, 2 * D_FF // N_CHIPS), "w_ffn1_out": (D_FF // N_CHIPS, D_MODEL),
    "w_in": (D_MODEL, IN_WIDTH // N_CHIPS), "w_br_sb": (W_SB, D_MODEL // N_CHIPS),
    "w_br_ch": (W_CH, D_MODEL // N_CHIPS), "w_br_fox": (W_FOX, D_MODEL // N_CHIPS),
    "w_out": (D_MODEL // N_CHIPS, D_MODEL), "w_ffn2_in": (D_MODEL, 2 * D_FF // N_CHIPS),
    "w_ffn2_out": (D_FF // N_CHIPS, D_MODEL),
}


def _pair_sums(split, tag):
    core = lax.axis_index("c").astype(jnp.int32)[None]
    got = _send_other_half(split, f"grad_split_{tag}")
    return [_add_pair(core, a, b, f"grad_pair_sum_{tag}_{i}") for i, (a, b) in enumerate(zip(split, got))]


def _scatter_plan(srcs, bufs, new):
    x, y, c = _position()
    return [(s.at[idx], d.at[k], (*chip, c)) for s, d in zip(srcs, new) for k, (chip, idx) in enumerate(_other_chips(x, y))]


def _scatter_carry(pair):
    landing = [jax.ShapeDtypeStruct((3,) + p.shape[1:], p.dtype) for p in pair]
    return _Carry(pair, [], landing, 3 * len(pair), _scatter_plan)


def _finish_grads(pair, landed, axes, tag):
    chip = (2 * lax.axis_index("x") + lax.axis_index("y")).astype(jnp.int32)[None]
    mine = [_sum_chips(chip, p, q, f"grad_chip_sum_{tag}_{i}") for i, (p, q) in enumerate(zip(pair, landed))]
    theirs = _swap_with_sibling(mine, f"grad_share_{tag}")
    first = lax.axis_index("c") == 0
    return [jnp.concatenate([jnp.where(first, a, b), jnp.where(first, b, a)], axis=ax)
            for a, b, ax in zip(mine, theirs, axes)]


SMALL_SIZES = {"g_ffn1": DEPTH * D_MODEL, "g_mix": DEPTH * D_MODEL, "b_in": DEPTH * IN_WIDTH,
               "rel_bias": DEPTH * N_REL * H_CH, "g_ffn2": DEPTH * D_MODEL, "g_final": D_MODEL}
SMALL_TOTAL = sum(SMALL_SIZES.values()) + 1
SMALL_ROWS = -(-SMALL_TOTAL // (8 * 128)) * 8


def _pack_small(vals, extra):
    flat = [vals[n].reshape(-1) for n in SMALL] + [extra.reshape(-1)]
    flat.append(jnp.zeros((SMALL_ROWS * 128 - SMALL_TOTAL,), F32))
    return jnp.concatenate(flat).reshape(SMALL_ROWS, 128)


def _unpack_small(pack, shapes):
    flat, out, off = pack.reshape(-1), {}, 0
    for n in SMALL:
        out[n] = flat[off:off + SMALL_SIZES[n]].reshape(shapes[n])
        off += SMALL_SIZES[n]
    return out, flat[off]


def _to_heads(t, n_heads):
    return jnp.transpose(t.reshape(t.shape[0], n_heads, HEAD_DIM), (1, 0, 2)).astype(BF16)


def _from_heads(t):
    return jnp.transpose(t, (1, 0, 2)).reshape(t.shape[1], t.shape[0] * HEAD_DIM)


def _pad_keys(t):
    return jnp.pad(t, ((0, 0), (BAND_PAD, 0), (0, 0)))


DIAGONALS = CHUNK + BAND - 1


def _band_bias(table):
    n_far = (LEFT_CHUNKS + 1) * CHUNK + CHUNK - MAX_REL
    near = table[N_REL - 1 - (DIAGONALS - n_far):N_REL - 1][::-1]
    diag = jnp.concatenate([jnp.broadcast_to(table[N_REL - 1], (n_far, H_CH)), near], axis=0).T
    diag = jnp.pad(diag, ((0, 0), (0, 1)))
    skew = jnp.tile(diag, (1, CHUNK))[:, :CHUNK * DIAGONALS].reshape(H_CH, CHUNK, DIAGONALS)
    return skew[:, :, CHUNK - 1:]


def _pad_w_in(w):
    qkv, f, gates = w[:, :QKV_WIDTH], w[:, QKV_WIDTH:QKV_WIDTH + H_FOX], w[:, QKV_WIDTH + H_FOX:]
    return jnp.concatenate([qkv, gates, f, jnp.zeros((w.shape[0], F_PAD - H_FOX), w.dtype)], axis=1)


def _unpad_w_in(w):
    return jnp.concatenate([w[:, :QKV_WIDTH], w[:, QKV_WIDTH + GATE_WIDTH:QKV_WIDTH + GATE_WIDTH + H_FOX],
                            w[:, QKV_WIDTH:QKV_WIDTH + GATE_WIDTH]], axis=1)


QKV_SPLITS = np.cumsum([0, W_SB, W_SB, W_SB, W_CH, W_CH, W_CH, W_FOX, W_FOX, W_FOX])


def _by_chip_rows(g):
    return g.reshape(2, N_CHIPS, g.shape[1] // N_CHIPS, g.shape[2])


def _ffn_fwd(x, g, w_in, w_out, tag, carry=None):
    h = _rmsnorm_fwd(x, g, f"{tag}_norm")
    gu = _mm(h, w_in, mode="nn", name=f"{tag}_in", gathered="col", out_dtype=BF16, carry=carry)
    a = _swiglu_fwd(gu, f"{tag}_act")
    y = _mm(a, w_out, mode="nn", name=f"{tag}_out", alpha=0.5, res=x, gathered="row")
    return y, (h, gu, a)


def _ffn_bwd(x, g, w_in, w_out, saved, dy, tag):
    h, gu, a = saved
    da = _mm(dy, w_out, mode="nt", name=f"{tag}_da", alpha=0.5, gathered="row", out_dtype=BF16)
    dw_out = _mm(a, dy, mode="tn", name=f"{tag}_dwout", alpha=0.5, tm_cap=1408, out_dtype=BF16, out_split="row")
    dgu = _swiglu_bwd(gu, da, f"{tag}_dact")
    dw_in = _mm(h, dgu, mode="tn", name=f"{tag}_dwin", tm_cap=512, out_dtype=BF16, out_split="col")
    dh = _mm(dgu, w_in, mode="nt", name=f"{tag}_dh", gathered="col", tn_cap=1024)
    dx, dg = _rmsnorm_bwd(x, g, dh, dy, f"{tag}_dnorm")
    return dx, dg, dw_in, _by_chip_rows(dw_out)


def _mixer_fwd(x, lw, tag, carries):
    T = x.shape[0]
    h = _rmsnorm_fwd(x, lw["g_mix"], f"{tag}_norm")
    p = _mm(h, lw["w_in"], mode="nn", name=f"{tag}_proj", bias=lw["b_in"], tn_cap=896, carry=carries.get("proj"))
    parts = [p[:, QKV_SPLITS[i]:QKV_SPLITS[i + 1]] for i in range(9)]
    qa, ka, va = (_to_heads(t, H_SB) for t in parts[0:3])
    qb, kb, vb = (_to_heads(t, H_CH) for t in parts[3:6])
    qc, kc, vc = (_to_heads(t, H_FOX) for t in parts[6:9])
    flog = jnp.pad(p[:, QKV_WIDTH + GATE_WIDTH:QKV_WIDTH + GATE_WIDTH + H_FOX].T, ((0, 8 - H_FOX), (0, 0)))
    fcum = _forget_cumsum(flog, f"{tag}_fcum")[:H_FOX]
    fq, fk = fcum.reshape(H_FOX, T, 1), fcum.reshape(H_FOX, T // QB, QB)
    kbp, vbp = _pad_keys(kb), _pad_keys(vb)
    oa = _sb_fwd(qa, ka, va, f"{tag}_sb", carries.get("sb"))
    ob = _chunk_fwd(qb, kbp, vbp, lw["bias"], f"{tag}_ch", carries.get("ch"))
    oc, lse = _fox_fwd(qc, kc, vc, fq, fk, f"{tag}_fox", carries.get("fox"))
    oam, obm, ocm = _from_heads(oa), _from_heads(ob), _from_heads(oc)
    ya = _mm(oam, lw["w_br_sb"], mode="nn", name=f"{tag}_bra", gathered="col")
    yb = _mm(obm, lw["w_br_ch"], mode="nn", name=f"{tag}_brb", gathered="col")
    yc = _mm(ocm, lw["w_br_fox"], mode="nn", name=f"{tag}_brc", gathered="col")
    merged = _gate_fwd(p, ya, yb, yc, f"{tag}_gate")
    y = _mm(merged, lw["w_out"], mode="nn", name=f"{tag}_out", res=x, gathered="row")
    saved = (h, p, (qa, ka, va), (qb, kbp, vbp), (qc, kc, vc), flog, fq, fk, lse, (oam, obm, ocm),
             (ya, yb, yc), merged)
    return y, saved


def _mixer_bwd(x, lw, saved, dy, tag, carries):
    (h, p, (qa, ka, va), (qb, kbp, vbp), (qc, kc, vc), flog, fq, fk, lse, (oam, obm, ocm),
     (ya, yb, yc), merged) = saved
    T = x.shape[0]
    grads = {}
    col, row = "col", "row"
    dmerged = _mm(dy, lw["w_out"], mode="nt", name=f"{tag}_dmerged", gathered=row)
    grads["w_out"] = _by_chip_rows(_mm(merged, dy, mode="tn", name=f"{tag}_dwout", tm_cap=1024, out_dtype=BF16,
                                       out_split="row"))
    dya, dyb, dyc, dgate = _gate_bwd(p, ya, yb, yc, dmerged, f"{tag}_dgate")
    doa = _mm(dya, lw["w_br_sb"], mode="nt", name=f"{tag}_doa", gathered=col)
    dob = _mm(dyb, lw["w_br_ch"], mode="nt", name=f"{tag}_dob", gathered=col)
    doc = _mm(dyc, lw["w_br_fox"], mode="nt", name=f"{tag}_doc", gathered=col)
    by_chip = dict(mode="tn", tm_cap=512, out_dtype=BF16, out_split="col")
    grads["w_br_sb"] = _mm(oam, dya, name=f"{tag}_dwbra", **by_chip)
    grads["w_br_ch"] = _mm(obm, dyb, name=f"{tag}_dwbrb", **by_chip)
    grads["w_br_fox"] = _mm(ocm, dyc, name=f"{tag}_dwbrc", **by_chip)
    dqa, dka, dva = _sb_bwd(qa, ka, va, _to_heads(doa, H_SB), f"{tag}_dsb", carries.get("dsb"))
    dqb, dkbp, dvbp, dbias = _chunk_bwd(qb, kbp, vbp, lw["bias"], _to_heads(dob, H_CH), f"{tag}_dch",
                                        carries.get("dch"))
    dqc, dkc, dvc, dfk = _fox_bwd(qc, kc, vc, fq, fk, lse, _to_heads(doc, H_FOX), f"{tag}_dfox")
    dflog = _forget_cumsum_bwd(flog, jnp.pad(dfk.reshape(H_FOX, T), ((0, 8 - H_FOX), (0, 0))), f"{tag}_dfcum")
    dqkv = [_from_heads(t) for t in (dqa, dka, dva, dqb, dkbp[:, BAND_PAD:], dvbp[:, BAND_PAD:], dqc, dkc, dvc)]
    dp = jnp.concatenate(dqkv + [dgate, jnp.pad(dflog[:H_FOX].T, ((0, 0), (0, F_PAD - H_FOX)))], axis=1)
    grads["b_in"] = _colsum(dp, f"{tag}_dbin")
    dpb = dp.astype(BF16)
    dw_in = _unpad_w_in(_mm(h, dpb, mode="tn", name=f"{tag}_dwin", tm_cap=512, tn_cap=896, out_dtype=BF16))
    grads["w_in"] = jnp.transpose(dw_in.reshape(2, D_MODEL // 2, N_CHIPS, IN_WIDTH // N_CHIPS), (0, 2, 1, 3))
    dh = _mm(dpb, lw["w_in"], mode="nt", name=f"{tag}_dh", tk_cap=896, tn_cap=1024)
    dx, grads["g_mix"] = _rmsnorm_bwd(x, lw["g_mix"], dh, dy, f"{tag}_dnorm")
    grads["rel_bias"] = lw["bias_vjp"](dbias)[0]
    return dx, grads


def kernel(x, g_ffn1, w_ffn1_in, w_ffn1_out, g_mix, w_in, b_in, rel_bias, w_br_sb, w_br_ch, w_br_fox, w_out, g_ffn2, w_ffn2_in, w_ffn2_out, g_final, loss_target, m_g_ffn1, m_w_ffn1_in, m_w_ffn1_out, m_g_mix, m_w_in, m_b_in, m_rel_bias, m_w_br_sb, m_w_br_ch, m_w_br_fox, m_w_out, m_g_ffn2, m_w_ffn2_in, m_w_ffn2_out, m_g_final, v_g_ffn1, v_w_ffn1_in, v_w_ffn1_out, v_g_mix, v_w_in, v_b_in, v_rel_bias, v_w_br_sb, v_w_br_ch, v_w_br_fox, v_w_out, v_g_ffn2, v_w_ffn2_in, v_w_ffn2_out, v_g_final):
    names = ("g_ffn1", "w_ffn1_in", "w_ffn1_out", "g_mix", "w_in", "b_in", "rel_bias", "w_br_sb", "w_br_ch",
             "w_br_fox", "w_out", "g_ffn2", "w_ffn2_in", "w_ffn2_out", "g_final")
    w = dict(zip(names, (g_ffn1, w_ffn1_in, w_ffn1_out, g_mix, w_in, b_in, rel_bias, w_br_sb, w_br_ch,
                         w_br_fox, w_out, g_ffn2, w_ffn2_in, w_ffn2_out, g_final)))
    m = dict(zip(names, (m_g_ffn1, m_w_ffn1_in, m_w_ffn1_out, m_g_mix, m_w_in, m_b_in, m_rel_bias, m_w_br_sb,
                         m_w_br_ch, m_w_br_fox, m_w_out, m_g_ffn2, m_w_ffn2_in, m_w_ffn2_out, m_g_final)))
    v = dict(zip(names, (v_g_ffn1, v_w_ffn1_in, v_w_ffn1_out, v_g_mix, v_w_in, v_b_in, v_rel_bias, v_w_br_sb,
                         v_w_br_ch, v_w_br_fox, v_w_out, v_g_ffn2, v_w_ffn2_in, v_w_ffn2_out, v_g_final)))
    xs = x[0]
    tgt = loss_target[0]

    own = [w[n].astype(BF16) for n in BIG]
    halves = [o.reshape(DEPTH, 2, o.shape[1] // 2, o.shape[2]) for o in own]
    chip = 2 * lax.axis_index("x") + lax.axis_index("y")

    def own_in_place(l):
        return [lax.dynamic_update_slice(lax.empty((N_CHIPS,) + o.shape[1:], BF16), o[l][None], (chip, 0, 0))
                .reshape((N_CHIPS,) + h.shape[1:]) for o, h in zip(own, halves)]

    def layer_weights(l, bufs):
        lw = {n: b.reshape((N_CHIPS,) + o.shape[1:]) for n, b, o in zip(BIG, bufs, own)}
        lw["w_in"] = _pad_w_in(jnp.concatenate([lw["w_in"][j] for j in range(N_CHIPS)], axis=1))
        lw["b_in"] = _pad_w_in(w["b_in"][l][None, :])
        lw["bias"], lw["bias_vjp"] = jax.vjp(_band_bias, w["rel_bias"][l])
        for n in ("g_ffn1", "g_mix", "g_ffn2"):
            lw[n] = w[n][l][None, :]
        return lw

    def ffn_fwd(x_in, lw, which, tag, carry=None):
        return _ffn_fwd(x_in, lw[f"g_{which}"], lw[f"w_{which}_in"], lw[f"w_{which}_out"], tag, carry)

    def ffn_bwd(x_in, lw, which, saved_acts, dy, tag):
        return _ffn_bwd(x_in, lw[f"g_{which}"], lw[f"w_{which}_in"], lw[f"w_{which}_out"], saved_acts, dy, tag)

    layers = [layer_weights(0, _gather_layer(halves, own_in_place(0), 0, "gather_l0")), None]
    next_bufs = own_in_place(1)
    over_ici = {"proj": (8,), "sb": (0, 1), "ch": (2, 3, 4, 5, 6), "fox": (7,)}
    fwd_carries = {k: _Carry([halves[i] for i in idx], [next_bufs[i] for i in idx], [], 3 * len(idx),
                             _gather_over_ici(1)) for k, idx in over_ici.items()}

    saved = []
    act = xs
    for l in range(DEPTH):
        lw = layers[l]
        x0 = act
        x1, s1 = ffn_fwd(x0, lw, "ffn1", f"l{l}_ffn1")
        x2, s2 = _mixer_fwd(x1, lw, f"l{l}_mix", fwd_carries if l == 0 else {})
        to_sibling = None
        if l == 0:
            for k, idx in over_ici.items():
                for i, b in zip(idx, fwd_carries[k].out[0]):
                    next_bufs[i] = b
            to_sibling = _Carry([], next_bufs, [], 3 * len(BIG), _gather_to_sibling)
        act, s3 = ffn_fwd(x2, lw, "ffn2", f"l{l}_ffn2", to_sibling)
        if l == 0:
            layers[1] = layer_weights(1, to_sibling.out[0])
        saved.append((x0, s1, x1, s2, x2, s3))

    dact, dg_final, loss_row = _final_loss(act, w["g_final"][None, :], tgt, "final_loss")

    big_grads = {n: [None] * DEPTH for n in BIG}
    small_grads = {n: [None] * DEPTH for n in ("g_ffn1", "g_mix", "b_in", "rel_bias", "g_ffn2")}
    bwd_carries, pair = {}, [None] * DEPTH
    for l in reversed(range(DEPTH)):
        lw = layers[l]
        x0, s1, x1, s2, x2, s3 = saved[l]
        dx2, small_grads["g_ffn2"][l], big_grads["w_ffn2_in"][l], big_grads["w_ffn2_out"][l] = ffn_bwd(
            x2, lw, "ffn2", s3, dact, f"l{l}_ffn2")
        dx1, mg = _mixer_bwd(x1, lw, s2, dx2, f"l{l}_mix", bwd_carries)
        for n in ("w_in", "w_br_sb", "w_br_ch", "w_br_fox", "w_out"):
            big_grads[n][l] = mg[n]
        small_grads["b_in"][l] = _unpad_w_in(mg["b_in"])[0]
        small_grads["g_mix"][l] = mg["g_mix"][0]
        small_grads["rel_bias"][l] = mg["rel_bias"]
        dact, dg1, big_grads["w_ffn1_in"][l], big_grads["w_ffn1_out"][l] = ffn_bwd(
            x0, lw, "ffn1", s1, dx1, f"l{l}_ffn1")
        small_grads["g_ffn1"][l] = dg1[0]
        small_grads["g_ffn2"][l] = small_grads["g_ffn2"][l][0]
        pair[l] = _pair_sums([big_grads[n][l] for n in BIG], f"l{l}")
        if l == 1:
            bwd_carries = {"dsb": _scatter_carry(pair[1][:3]), "dch": _scatter_carry(pair[1][3:])}
    grad_x = dact[None]
    landed = [list(_scatter_chips(pair[0], "grad_scatter_l0")), bwd_carries["dsb"].out[1] + bwd_carries["dch"].out[1]]

    small = {n: jnp.stack(small_grads[n]) for n in small_grads}
    small["g_final"] = dg_final[0]
    small_sum, loss = _unpack_small(_allreduce_small(_pack_small(small, loss_row[0, :1]), "allreduce_small"),
                                    {n: w[n].shape for n in SMALL})

    axes = [1 if n in ROW_SHARDED else 0 for n in BIG] * DEPTH
    summed = _finish_grads(pair[0] + pair[1], landed[0] + landed[1], axes, "all")
    grads = {n: jnp.stack([summed[i], summed[len(BIG) + i]]) for i, n in enumerate(BIG)}
    grads.update(small_sum)

    delta, new_m, new_v = {}, {}, {}
    for n in BIG:
        shape = w[n].shape
        flat = lambda t: t.reshape(-1, shape[-1])
        d, nm, nv = _adamw(flat(w[n]), flat(grads[n]), flat(m[n]), flat(v[n]), f"adamw_{n}")
        delta[n], new_m[n], new_v[n] = d.reshape(shape), nm.reshape(shape), nv.reshape(shape)
    zero = jnp.zeros((1,), F32)
    sd, sm, sv = _adamw(_pack_small(w, zero), _pack_small(grads, zero), _pack_small(m, zero), _pack_small(v, zero),
                        "adamw_small")
    shapes = {n: w[n].shape for n in SMALL}
    for dst, src in ((delta, sd), (new_m, sm), (new_v, sv)):
        dst.update(_unpack_small(src, shapes)[0])

    return (loss, grad_x, *[grads[n] for n in names], *[delta[n] for n in names],
            *[new_m[n] for n in names], *[new_v[n] for n in names])
```

```python
import functools

import numpy as np
import jax
import jax.numpy as jnp
from jax import lax
from jax.experimental import pallas as pl
from jax.experimental.pallas import tpu as pltpu

F32 = jnp.float32
BF16 = jnp.bfloat16

D_MODEL = 1024
DEPTH = 2
CHUNK = 64
HEAD_DIM = 64
H_SB, H_CH, H_FOX = 4, 8, 4
W_SB, W_CH, W_FOX = H_SB * HEAD_DIM, H_CH * HEAD_DIM, H_FOX * HEAD_DIM
LEFT_CHUNKS = 8
MAX_REL = 128
N_REL = 2 * MAX_REL + 1
D_FF = 2816
QKV_WIDTH = 3 * (W_SB + W_CH + W_FOX)
GATE_WIDTH = 3 * D_MODEL
IN_WIDTH = QKV_WIDTH + H_FOX + GATE_WIDTH
F_PAD = 128
P_WIDTH = QKV_WIDTH + GATE_WIDTH + F_PAD
RMS_EPS = 1e-6
NEG = -1e30
SCALE = HEAD_DIM ** -0.5
QB = 256
BAND = (LEFT_CHUNKS + 2) * CHUNK
BAND_PAD = BAND - CHUNK

ADAM_LR, ADAM_B1, ADAM_B2, ADAM_EPS, ADAM_WD, ADAM_STEP = 0.001, 0.9, 0.999, 1e-08, 0.01, 10

N_CHIPS = 4
PACK_COLS = 1024
VMEM_BUDGET = 52 * 1024 * 1024

NT = (((1,), (1,)), ((), ()))
TN = (((0,), (0,)), ((), ()))
NN = (((1,), (0,)), ((), ()))
MESH = pl.DeviceIdType.MESH


def _pcall(body, **kw):
    return pl.pallas_call(body, **kw)


class _Carry:
    def __init__(self, srcs, bufs, new, count, plan, done=None):
        self.srcs, self.bufs, self.new, self.count, self.plan = list(srcs), list(bufs), list(new), count, plan
        self.out, self.done = None, done

    @staticmethod
    def join(parts):
        parts = [p for p in parts if p is not None]
        if not parts:
            return None

        def plan(srcs, bufs, new):
            copies, s, b, n = [], 0, 0, 0
            for p in parts:
                copies += p.plan(srcs[s:s + len(p.srcs)], bufs[b:b + len(p.bufs)], new[n:n + len(p.new)])
                s, b, n = s + len(p.srcs), b + len(p.bufs), n + len(p.new)
            return copies

        whole = _Carry(sum((p.srcs for p in parts), []), sum((p.bufs for p in parts), []),
                       sum((p.new for p in parts), []), sum(p.count for p in parts), plan)
        whole.parts = parts
        return whole

    def share_out(self):
        b, n = 0, 0
        for p in getattr(self, "parts", []):
            p.out = (self.out[0][b:b + len(p.bufs)], self.out[1][n:n + len(p.new)])
            b, n = b + len(p.bufs), n + len(p.new)
            p.share_out()
        if self.done is not None:
            self.done(self)


def _carry_of(carries, key):
    return _Carry.join([make() for make in carries.get(key, [])])


def _pcall_carrying(body, carry, **kw):
    if carry is None:
        return _pcall(body, **kw)
    in_specs = list(kw.pop("in_specs"))
    out_specs, out_shape = kw.pop("out_specs"), kw.pop("out_shape")
    single = not isinstance(out_shape, (list, tuple))
    out_specs = [out_specs] if single else list(out_specs)
    out_shape = [out_shape] if single else list(out_shape)
    scratch = list(kw.pop("scratch_shapes", []))
    grid = tuple(kw.get("grid", ()))
    n_in, n_out, n_scr = len(in_specs), len(out_specs), len(scratch)
    ns, nb, nn = len(carry.srcs), len(carry.bufs), len(carry.new)

    def body2(*refs):
        ins, srcs = refs[:n_in], refs[n_in:n_in + ns]
        at = n_in + ns + nb
        outs, bufs, new = refs[at:at + n_out], refs[at + n_out:at + n_out + nb], refs[at + n_out + nb:at + n_out + nb + nn]
        at += n_out + nb + nn
        scr, (send_sems, recv_sems) = refs[at:at + n_scr], refs[at + n_scr:]

        def copies():
            return [_remote_copy(s, d, send_sems, recv_sems, k, to)
                    for k, (s, d, to) in enumerate(carry.plan(srcs, bufs, new))]

        def start():
            for cp in copies():
                cp.start()

        def wait():
            for cp in copies():
                cp.wait()

        if grid:
            ids = [pl.program_id(a) for a in range(len(grid))]
            pl.when(functools.reduce(jnp.logical_and, [i == 0 for i in ids]))(start)
        else:
            start()
        body(*ins, *outs, *scr)
        if grid:
            pl.when(functools.reduce(jnp.logical_and, [i == g - 1 for i, g in zip(ids, grid)]))(wait)
        else:
            wait()

    call = _pcall(
        body2, in_specs=in_specs + [HBM_SPEC] * (ns + nb), out_specs=out_specs + [HBM_SPEC] * (nb + nn),
        out_shape=out_shape + [jax.ShapeDtypeStruct(b.shape, b.dtype) for b in carry.bufs] + carry.new,
        scratch_shapes=scratch + [pltpu.SemaphoreType.DMA((carry.count,)), pltpu.SemaphoreType.DMA((carry.count,))],
        input_output_aliases={n_in + ns + j: n_out + j for j in range(nb)}, **kw)

    def apply(*args):
        res = call(*args, *carry.srcs, *carry.bufs)
        carry.out = (list(res[n_out:n_out + nb]), list(res[n_out + nb:]))
        carry.share_out()
        return res[0] if single else list(res[:n_out])

    return apply


def _pick(n, cap, mult=128):
    best = None
    d = mult
    while d <= min(n, cap):
        if n % d == 0:
            best = d
        d += mult
    return n if best is None else best


def _nbytes(shape, dtype):
    return int(np.prod(shape)) * jnp.dtype(dtype).itemsize


def _mm(a, b, *, mode, name, out_dtype=F32, alpha=1.0, bias=None, res=None, tm_cap=1024, tn_cap=512,
        tk_cap=2816, gathered=None, out_split=None, carry=None):
    if mode == "tn":
        K, M = a.shape
    else:
        M, K = a.shape
    dn = {"nn": NN, "nt": NT, "tn": TN}[mode]
    b_rows = None
    if gathered is None:
        N = b.shape[0] if mode == "nt" else b.shape[1]
        tn = {None: _pick(N, tn_cap), "col": N // N_CHIPS, "row": N // 2}[out_split]
        if out_split == "col":
            tm_cap = min(tm_cap, M // 2)
        tk = K if K <= tk_cap else _pick(K, tk_cap)
        b_spec = (pl.BlockSpec((tn, tk), lambda i, j, k: (j, k)) if mode == "nt"
                  else pl.BlockSpec((tk, tn), lambda i, j, k: (k, j)))
    else:
        r, c = b.shape[1:]
        rows, cols = (r, N_CHIPS * c) if gathered == "col" else (N_CHIPS * r, c)
        N = rows if mode == "nt" else cols
        assert K == (cols if mode == "nt" else rows), (name, K, rows, cols)
        if gathered == "col" and mode == "nn":
            tn, tk = c, (r if r <= tk_cap else _pick(r, tk_cap))
            b_spec = pl.BlockSpec((None, tk, c), lambda i, j, k: (j, k, 0))
        elif gathered == "col":
            tn, tk = _pick(r, tn_cap), c
            b_spec = pl.BlockSpec((None, tn, c), lambda i, j, k: (k, j, 0))
        elif mode == "nn":
            tn, tk, b_rows = _pick(c, tn_cap), K, N_CHIPS
            b_spec = pl.BlockSpec((N_CHIPS, r, tn), lambda i, j, k: (0, 0, j))
        else:
            tn, tk, b_rows = 2 * r, K, 2
            b_spec = pl.BlockSpec((2, r, c), lambda i, j, k: (j, 0, 0))
    tm = _pick(M, tm_cap)
    nk = K // tk

    a_spec = (pl.BlockSpec((tk, tm), lambda i, j, k: (k, i)) if mode == "tn"
              else pl.BlockSpec((tm, tk), lambda i, j, k: (i, k)))
    in_specs = [a_spec, b_spec]
    args = [a, b]
    if bias is not None:
        in_specs.append(pl.BlockSpec((1, tn), lambda i, j, k: (0, j)))
        args.append(bias)
    if res is not None:
        in_specs.append(pl.BlockSpec((tm, tn), lambda i, j, k: (i, j)))
        args.append(res)

    est = 2 * (_nbytes((tm, tk), a.dtype) + _nbytes((tk, tn), b.dtype) + _nbytes((tm, tn), out_dtype))
    est += _nbytes((tm, tn), F32) * (3 if res is not None else 1)
    assert est < VMEM_BUDGET, (name, est)

    def body(*refs):
        a_ref, b_ref = refs[0], refs[1]
        pos = 2
        bias_ref = res_ref = None
        if bias is not None:
            bias_ref = refs[pos]
            pos += 1
        if res is not None:
            res_ref = refs[pos]
            pos += 1
        o_ref = refs[pos]
        acc_ref = refs[pos + 1] if nk > 1 else None

        bv = b_ref[...]
        if b_rows is not None:
            bv = bv.reshape(b_rows * bv.shape[1], bv.shape[2])
        part = lax.dot_general(a_ref[...].astype(BF16), bv.astype(BF16), dn, preferred_element_type=F32)

        def finish(acc):
            if alpha != 1.0:
                acc = acc * alpha
            if bias_ref is not None:
                acc = acc + bias_ref[...]
            if res_ref is not None:
                acc = acc + res_ref[...]
            o_ref[...] = acc.astype(out_dtype)

        if nk == 1:
            finish(part)
        else:
            k = pl.program_id(2)

            @pl.when(k == 0)
            def _():
                acc_ref[...] = part

            @pl.when(k > 0)
            def _():
                acc_ref[...] += part

            @pl.when(k == nk - 1)
            def _():
                finish(acc_ref[...])

    if out_split == "col":
        per_half = M // 2 // tm
        out_spec = pl.BlockSpec((None, None, tm, tn), lambda i, j, k: (i // per_half, j, i % per_half, 0))
        out_shape = jax.ShapeDtypeStruct((2, N_CHIPS, M // 2, tn), out_dtype)
    elif out_split == "row":
        out_spec = pl.BlockSpec((None, tm, tn), lambda i, j, k: (j, i, 0))
        out_shape = jax.ShapeDtypeStruct((2, M, tn), out_dtype)
    else:
        out_spec = pl.BlockSpec((tm, tn), lambda i, j, k: (i, j))
        out_shape = jax.ShapeDtypeStruct((M, N), out_dtype)
    return _pcall_carrying(
        body, carry, name=name, grid=(M // tm, N // tn, nk), in_specs=in_specs, out_specs=out_spec,
        out_shape=out_shape,
        scratch_shapes=[pltpu.VMEM((tm, tn), F32)] if nk > 1 else [],
        compiler_params=pltpu.CompilerParams(dimension_semantics=("parallel", "parallel", "arbitrary")),
    )(*args)


def _rmsnorm_fwd(x, g, name):
    T, Dm = x.shape
    tm = _pick(T, 256, 8)

    def body(x_ref, g_ref, h_ref):
        xv = x_ref[...]
        rstd = lax.rsqrt(jnp.mean(xv * xv, axis=-1, keepdims=True) + RMS_EPS)
        h_ref[...] = (xv * rstd * g_ref[...]).astype(BF16)

    return _pcall(
        body, name=name, grid=(T // tm,),
        in_specs=[pl.BlockSpec((tm, Dm), lambda i: (i, 0)), pl.BlockSpec((1, Dm), lambda i: (0, 0))],
        out_specs=pl.BlockSpec((tm, Dm), lambda i: (i, 0)),
        out_shape=jax.ShapeDtypeStruct((T, Dm), BF16),
    )(x, g)


def _rmsnorm_bwd(x, g, dh, dres, name):
    T, Dm = x.shape
    tm = _pick(T, 256, 8)

    def body(x_ref, g_ref, dh_ref, dres_ref, dx_ref, dg_ref):
        xv = x_ref[...]
        rstd = lax.rsqrt(jnp.mean(xv * xv, axis=-1, keepdims=True) + RMS_EPS)
        xhat = xv * rstd
        dhv = dh_ref[...]
        dyg = dhv * g_ref[...]
        dx_ref[...] = dres_ref[...] + rstd * (dyg - xhat * jnp.mean(dyg * xhat, axis=-1, keepdims=True))
        part = jnp.sum(dhv * xhat, axis=0, keepdims=True)

        @pl.when(pl.program_id(0) == 0)
        def _():
            dg_ref[...] = part

        @pl.when(pl.program_id(0) > 0)
        def _():
            dg_ref[...] += part

    row = pl.BlockSpec((tm, Dm), lambda i: (i, 0))
    vec = pl.BlockSpec((1, Dm), lambda i: (0, 0))
    return _pcall(
        body, name=name, grid=(T // tm,), in_specs=[row, vec, row, row], out_specs=[row, vec],
        out_shape=[jax.ShapeDtypeStruct((T, Dm), F32), jax.ShapeDtypeStruct((1, Dm), F32)],
        compiler_params=pltpu.CompilerParams(dimension_semantics=("arbitrary",)),
    )(x, g, dh, dres)


def _final_loss(x, g, tgt, name):
    T, Dm = x.shape
    tm = _pick(T, 256, 8)

    def body(x_ref, g_ref, t_ref, dx_ref, dg_ref, loss_ref):
        xv = x_ref[...]
        gv = g_ref[...]
        rstd = lax.rsqrt(jnp.mean(xv * xv, axis=-1, keepdims=True) + RMS_EPS)
        xhat = xv * rstd
        err = xhat * gv - t_ref[...]
        part_loss = 0.5 * jnp.sum(jnp.mean(err * err, axis=-1, keepdims=True), axis=0, keepdims=True)
        dy = err * (1.0 / Dm)
        dyg = dy * gv
        dx_ref[...] = rstd * (dyg - xhat * jnp.mean(dyg * xhat, axis=-1, keepdims=True))
        part_g = jnp.sum(dy * xhat, axis=0, keepdims=True)
        part_l = jnp.broadcast_to(part_loss, (1, 128))

        @pl.when(pl.program_id(0) == 0)
        def _():
            dg_ref[...] = part_g
            loss_ref[...] = part_l

        @pl.when(pl.program_id(0) > 0)
        def _():
            dg_ref[...] += part_g
            loss_ref[...] += part_l

    row = pl.BlockSpec((tm, Dm), lambda i: (i, 0))
    vec = pl.BlockSpec((1, Dm), lambda i: (0, 0))
    return _pcall(
        body, name=name, grid=(T // tm,), in_specs=[row, vec, row],
        out_specs=[row, vec, pl.BlockSpec((1, 128), lambda i: (0, 0))],
        out_shape=[jax.ShapeDtypeStruct((T, Dm), F32), jax.ShapeDtypeStruct((1, Dm), F32),
                   jax.ShapeDtypeStruct((1, 128), F32)],
        compiler_params=pltpu.CompilerParams(dimension_semantics=("arbitrary",)),
    )(x, g, tgt)


def _sigmoid(z):
    return 1.0 / (1.0 + jnp.exp(-z))


def _swiglu_fwd(gu, name):
    T = gu.shape[0]
    tm = _pick(T, 256, 16)

    def body(gu_ref, a_ref):
        gv = gu_ref[:, :D_FF].astype(F32)
        uv = gu_ref[:, D_FF:].astype(F32)
        a_ref[...] = (gv * _sigmoid(gv) * uv).astype(BF16)

    return _pcall(
        body, name=name, grid=(T // tm,), in_specs=[pl.BlockSpec((tm, 2 * D_FF), lambda i: (i, 0))],
        out_specs=pl.BlockSpec((tm, D_FF), lambda i: (i, 0)),
        out_shape=jax.ShapeDtypeStruct((T, D_FF), BF16),
    )(gu)


def _swiglu_bwd(gu, da, name):
    T = gu.shape[0]
    tm = _pick(T, 256, 16)

    def body(gu_ref, da_ref, d_ref):
        gv = gu_ref[:, :D_FF].astype(F32)
        uv = gu_ref[:, D_FF:].astype(F32)
        dav = da_ref[...].astype(F32)
        sg = _sigmoid(gv)
        d_ref[:, :D_FF] = (dav * uv * (sg + gv * sg * (1.0 - sg))).astype(BF16)
        d_ref[:, D_FF:] = (dav * gv * sg).astype(BF16)

    return _pcall(
        body, name=name, grid=(T // tm,),
        in_specs=[pl.BlockSpec((tm, 2 * D_FF), lambda i: (i, 0)), pl.BlockSpec((tm, D_FF), lambda i: (i, 0))],
        out_specs=pl.BlockSpec((tm, 2 * D_FF), lambda i: (i, 0)),
        out_shape=jax.ShapeDtypeStruct((T, 2 * D_FF), BF16),
    )(gu, da)


GATE_BLOCK0 = QKV_WIDTH // D_MODEL


def _gate_fwd(p, ya, yb, yc, name):
    T = p.shape[0]
    tm = _pick(T, 256, 8)

    def body(ga_ref, gb_ref, gc_ref, ya_ref, yb_ref, yc_ref, o_ref):
        o_ref[...] = (_sigmoid(ga_ref[...]) * ya_ref[...] + _sigmoid(gb_ref[...]) * yb_ref[...]
                      + _sigmoid(gc_ref[...]) * yc_ref[...]).astype(BF16)

    gate = [pl.BlockSpec((tm, D_MODEL), functools.partial(lambda i, kk: (i, GATE_BLOCK0 + kk), kk=kk))
            for kk in range(3)]
    row = pl.BlockSpec((tm, D_MODEL), lambda i: (i, 0))
    return _pcall(
        body, name=name, grid=(T // tm,), in_specs=gate + [row, row, row], out_specs=row,
        out_shape=jax.ShapeDtypeStruct((T, D_MODEL), BF16),
    )(p, p, p, ya, yb, yc)


def _gate_bwd(p, ya, yb, yc, dm, name):
    T = p.shape[0]
    tm = _pick(T, 256, 8)

    def body(ga_ref, gb_ref, gc_ref, ya_ref, yb_ref, yc_ref, dm_ref, da_ref, db_ref, dc_ref, dg_ref):
        dmv = dm_ref[...]
        for kk, (g_ref, y_ref, d_ref) in enumerate(((ga_ref, ya_ref, da_ref), (gb_ref, yb_ref, db_ref),
                                                    (gc_ref, yc_ref, dc_ref))):
            s = _sigmoid(g_ref[...])
            d_ref[...] = (s * dmv).astype(BF16)
            dg_ref[:, kk * D_MODEL:(kk + 1) * D_MODEL] = dmv * y_ref[...] * s * (1.0 - s)

    gate = [pl.BlockSpec((tm, D_MODEL), functools.partial(lambda i, kk: (i, GATE_BLOCK0 + kk), kk=kk))
            for kk in range(3)]
    row = pl.BlockSpec((tm, D_MODEL), lambda i: (i, 0))
    return _pcall(
        body, name=name, grid=(T // tm,), in_specs=gate + [row, row, row, row],
        out_specs=[row, row, row, pl.BlockSpec((tm, GATE_WIDTH), lambda i: (i, 0))],
        out_shape=[jax.ShapeDtypeStruct((T, D_MODEL), BF16)] * 3 + [jax.ShapeDtypeStruct((T, GATE_WIDTH), F32)],
    )(p, p, p, ya, yb, yc, dm)


def _colsum(a, name):
    T, N = a.shape
    tm = _pick(T, 256, 8)

    def body(a_ref, o_ref):
        part = jnp.sum(a_ref[...], axis=0, keepdims=True)

        @pl.when(pl.program_id(0) == 0)
        def _():
            o_ref[...] = part

        @pl.when(pl.program_id(0) > 0)
        def _():
            o_ref[...] += part

    return _pcall(
        body, name=name, grid=(T // tm,), in_specs=[pl.BlockSpec((tm, N), lambda i: (i, 0))],
        out_specs=pl.BlockSpec((1, N), lambda i: (0, 0)), out_shape=jax.ShapeDtypeStruct((1, N), F32),
        compiler_params=pltpu.CompilerParams(dimension_semantics=("arbitrary",)),
    )(a)


def _adamw(w, g, m, v, name):
    R, C = w.shape
    tr = 256 if R % 256 == 0 else R
    c1 = 1.0 / (1.0 - ADAM_B1 ** ADAM_STEP)
    c2 = 1.0 / (1.0 - ADAM_B2 ** ADAM_STEP)

    def body(w_ref, g_ref, m_ref, v_ref, d_ref, nm_ref, nv_ref):
        gv = g_ref[...]
        mn = ADAM_B1 * m_ref[...] + (1.0 - ADAM_B1) * gv
        vn = ADAM_B2 * v_ref[...] + (1.0 - ADAM_B2) * (gv * gv)
        nm_ref[...] = mn
        nv_ref[...] = vn
        d_ref[...] = -ADAM_LR * ((mn * c1) / (jnp.sqrt(vn * c2) + ADAM_EPS) + ADAM_WD * w_ref[...])

    spec = pl.BlockSpec((tr, C), lambda i: (i, 0))
    return _pcall(
        body, name=name, grid=(R // tr,), in_specs=[spec] * 4, out_specs=[spec] * 3,
        out_shape=[jax.ShapeDtypeStruct((R, C), F32)] * 3,
    )(w, g, m, v)


def _log_sigmoid(z):
    return jnp.minimum(z, 0.0) - jnp.log(1.0 + jnp.exp(-jnp.abs(z)))


def _dot3(x, m01):
    x1 = x.astype(BF16)
    r1 = x - x1.astype(F32)
    x2 = r1.astype(BF16)
    x3 = (r1 - x2.astype(F32)).astype(BF16)
    n = x.shape[0]
    if n % 16:
        return (jnp.dot(x1, m01, preferred_element_type=F32) + jnp.dot(x2, m01, preferred_element_type=F32)
                + jnp.dot(x3, m01, preferred_element_type=F32))
    y = jnp.dot(jnp.concatenate([x1, x2, x3], axis=0), m01, preferred_element_type=F32)
    return y[:n] + y[n:2 * n] + y[2 * n:]


def _dot_nt(a, b):
    return lax.dot_general(a, b, NT, preferred_element_type=F32)


def _dot_tn(a, b):
    return lax.dot_general(a, b, TN, preferred_element_type=F32)


def _iota2(shape):
    return lax.broadcasted_iota(jnp.int32, shape, 0), lax.broadcasted_iota(jnp.int32, shape, 1)


HEADS_PER_STEP = 4
HEADS = range(HEADS_PER_STEP)


CHUNK_HEADS_PER_STEP = 2
CHUNK_HEADS = range(CHUNK_HEADS_PER_STEP)


def _head_spec(T, rows=None, width=HEAD_DIM, heads=HEADS_PER_STEP):
    return pl.BlockSpec((heads, T if rows is None else rows, width), lambda g: (g, 0, 0))


def _sb_weights(qb, kb, t0, s0, racc, m_gt, row, col):
    z = _dot_nt(qb, kb) * SCALE
    strict = (s0 + col) < (t0 + row)
    lb = _log_sigmoid(z)
    lf = jnp.where(strict, lb - z, 0.0)
    between = _dot3(lf, m_gt) + racc
    w = jnp.where(strict, jnp.exp(lb + between), 0.0)
    return strict, lb, lf, w


def _sb_fwd(q, k, v, name, carry=None):
    H, T, _ = q.shape
    nb = T // QB

    def body(q_ref, k_ref, v_ref, o_ref):
        row, col = _iota2((QB, QB))
        m_gt = (row > col).astype(BF16)

        def qblock(i, _):
            t0 = pl.multiple_of(i * QB, QB)
            qbs = [q_ref[h, pl.ds(t0, QB), :].astype(BF16) for h in HEADS]

            def kblock(jj, carry):
                s0 = pl.multiple_of((i - jj) * QB, QB)
                out = []
                for h in HEADS:
                    acc, racc = carry[h]
                    kb = k_ref[h, pl.ds(s0, QB), :].astype(BF16)
                    vb = v_ref[h, pl.ds(s0, QB), :].astype(BF16)
                    _, _, lf, w = _sb_weights(qbs[h], kb, t0, s0, racc, m_gt, row, col)
                    acc = acc + jnp.dot(w.astype(BF16), vb, preferred_element_type=F32)
                    out.append((acc, racc + jnp.sum(lf, axis=1, keepdims=True)))
                return tuple(out)

            res = lax.fori_loop(0, i + 1, kblock,
                                tuple((jnp.zeros((QB, HEAD_DIM), F32), jnp.zeros((QB, 1), F32)) for _ in HEADS))
            for h in HEADS:
                o_ref[h, pl.ds(t0, QB), :] = res[h][0]
            return 0

        lax.fori_loop(0, nb, qblock, 0)

    spec = _head_spec(T)
    return _pcall_carrying(body, carry, name=name, grid=(H // HEADS_PER_STEP,), in_specs=[spec] * 3, out_specs=spec,
                           out_shape=jax.ShapeDtypeStruct((H, T, HEAD_DIM), F32))(q, k, v)


def _sb_bwd(q, k, v, do, name, carry=None):
    H, T, _ = q.shape
    nb = T // QB

    def body(q_ref, k_ref, v_ref, do_ref, dq_ref, dk_ref, dv_ref, racc_ref):
        row, col = _iota2((QB, QB))
        m_gt = (row > col).astype(BF16)
        m_lt = (row < col).astype(BF16)
        dk_ref[...] = jnp.zeros_like(dk_ref)
        dv_ref[...] = jnp.zeros_like(dv_ref)

        def qblock(i, _):
            t0 = pl.multiple_of(i * QB, QB)
            qbs = [q_ref[h, pl.ds(t0, QB), :].astype(BF16) for h in HEADS]
            dobs = [do_ref[h, pl.ds(t0, QB), :].astype(BF16) for h in HEADS]

            def suffix(jj, raccs):
                j = i - jj
                s0 = pl.multiple_of(j * QB, QB)
                out = []
                for h in HEADS:
                    kb = k_ref[h, pl.ds(s0, QB), :].astype(BF16)
                    z = _dot_nt(qbs[h], kb) * SCALE
                    lf = jnp.where((s0 + col) < (t0 + row), _log_sigmoid(z) - z, 0.0)
                    racc_ref[h, j] = raccs[h]
                    out.append(raccs[h] + jnp.sum(lf, axis=1, keepdims=True))
                return tuple(out)

            lax.fori_loop(0, i + 1, suffix, tuple(jnp.zeros((QB, 1), F32) for _ in HEADS))

            def kblock(j, carry):
                s0 = pl.multiple_of(j * QB, QB)
                out = []
                for h in HEADS:
                    dq, cacc = carry[h]
                    kb = k_ref[h, pl.ds(s0, QB), :].astype(BF16)
                    vb = v_ref[h, pl.ds(s0, QB), :].astype(BF16)
                    strict, lb, _, w = _sb_weights(qbs[h], kb, t0, s0, racc_ref[h, j], m_gt, row, col)
                    e = _dot_nt(dobs[h], vb) * w
                    c_left = _dot3(e, m_lt) + cacc
                    sig = jnp.exp(lb)
                    dz = jnp.where(strict, e * (1.0 - sig) - c_left * sig, 0.0).astype(BF16)
                    dq = dq + jnp.dot(dz, kb, preferred_element_type=F32)
                    dk_ref[h, pl.ds(s0, QB), :] += _dot_tn(dz, qbs[h]) * SCALE
                    dv_ref[h, pl.ds(s0, QB), :] += _dot_tn(w.astype(BF16), dobs[h])
                    out.append((dq, cacc + jnp.sum(e, axis=1, keepdims=True)))
                return tuple(out)

            res = lax.fori_loop(0, i + 1, kblock,
                                tuple((jnp.zeros((QB, HEAD_DIM), F32), jnp.zeros((QB, 1), F32)) for _ in HEADS))
            for h in HEADS:
                dq_ref[h, pl.ds(t0, QB), :] = res[h][0] * SCALE
            return 0

        lax.fori_loop(0, nb, qblock, 0)

    spec = _head_spec(T)
    return _pcall_carrying(body, carry, name=name, grid=(H // HEADS_PER_STEP,), in_specs=[spec] * 4,
                           out_specs=[spec] * 3, out_shape=[jax.ShapeDtypeStruct((H, T, HEAD_DIM), F32)] * 3,
                           scratch_shapes=[pltpu.VMEM((HEADS_PER_STEP, nb, QB, 1), F32)])(q, k, v, do)


def _fox_logits(qb, kb, fq, fk, t0, s0, row, col):
    z = _dot_nt(qb, kb) * SCALE + fq - fk
    return jnp.where((s0 + col) <= (t0 + row), z, NEG)


def _fox_specs(T):
    return _head_spec(T, width=1), _head_spec(T, rows=T // QB, width=QB)


def _fox_fwd(q, k, v, fq, fk, name, carry=None):
    H, T, _ = q.shape
    nb = T // QB

    def body(q_ref, k_ref, v_ref, fq_ref, fk_ref, o_ref, lse_ref):
        row, col = _iota2((QB, QB))

        def qblock(i, _):
            t0 = pl.multiple_of(i * QB, QB)
            qbs = [q_ref[h, pl.ds(t0, QB), :].astype(BF16) for h in HEADS]
            fqs = [fq_ref[h, pl.ds(t0, QB), :] for h in HEADS]

            def kblock(j, carry):
                s0 = pl.multiple_of(j * QB, QB)
                out = []
                for h in HEADS:
                    acc, m, l = carry[h]
                    kb = k_ref[h, pl.ds(s0, QB), :].astype(BF16)
                    vb = v_ref[h, pl.ds(s0, QB), :].astype(BF16)
                    z = _fox_logits(qbs[h], kb, fqs[h], fk_ref[h, pl.ds(j, 1), :], t0, s0, row, col)
                    m_new = jnp.maximum(m, jnp.max(z, axis=1, keepdims=True))
                    a = jnp.exp(m - m_new)
                    p = jnp.exp(z - m_new)
                    acc = a * acc + jnp.dot(p.astype(BF16), vb, preferred_element_type=F32)
                    out.append((acc, m_new, a * l + jnp.sum(p, axis=1, keepdims=True)))
                return tuple(out)

            res = lax.fori_loop(0, i + 1, kblock,
                                tuple((jnp.zeros((QB, HEAD_DIM), F32), jnp.full((QB, 1), NEG, F32),
                                       jnp.zeros((QB, 1), F32)) for _ in HEADS))
            for h in HEADS:
                acc, m, l = res[h]
                o_ref[h, pl.ds(t0, QB), :] = acc / l
                lse_ref[h, pl.ds(t0, QB), :] = m + jnp.log(l)
            return 0

        lax.fori_loop(0, nb, qblock, 0)

    spec = _head_spec(T)
    fq_spec, fk_spec = _fox_specs(T)
    return _pcall_carrying(
        body, carry, name=name, grid=(H // HEADS_PER_STEP,), in_specs=[spec] * 3 + [fq_spec, fk_spec],
        out_specs=[spec, fq_spec],
        out_shape=[jax.ShapeDtypeStruct((H, T, HEAD_DIM), F32), jax.ShapeDtypeStruct((H, T, 1), F32)],
    )(q, k, v, fq, fk)


def _fox_bwd(q, k, v, fq, fk, lse, do, name, carry=None):
    H, T, _ = q.shape
    nb = T // QB

    def body(q_ref, k_ref, v_ref, fq_ref, fk_ref, lse_ref, do_ref, dq_ref, dk_ref, dv_ref, dfk_ref):
        row, col = _iota2((QB, QB))
        dk_ref[...] = jnp.zeros_like(dk_ref)
        dv_ref[...] = jnp.zeros_like(dv_ref)
        dfk_ref[...] = jnp.zeros_like(dfk_ref)

        def qblock(i, _):
            t0 = pl.multiple_of(i * QB, QB)
            qbs = [q_ref[h, pl.ds(t0, QB), :].astype(BF16) for h in HEADS]
            fqs = [fq_ref[h, pl.ds(t0, QB), :] for h in HEADS]
            lses = [lse_ref[h, pl.ds(t0, QB), :] for h in HEADS]
            dobs = [do_ref[h, pl.ds(t0, QB), :].astype(BF16) for h in HEADS]

            def probs(h, j):
                s0 = pl.multiple_of(j * QB, QB)
                kb = k_ref[h, pl.ds(s0, QB), :].astype(BF16)
                vb = v_ref[h, pl.ds(s0, QB), :].astype(BF16)
                z = _fox_logits(qbs[h], kb, fqs[h], fk_ref[h, pl.ds(j, 1), :], t0, s0, row, col)
                return s0, kb, jnp.exp(z - lses[h]), _dot_nt(dobs[h], vb)

            def row_dot(j, deltas):
                out = []
                for h in HEADS:
                    _, _, p, dp = probs(h, j)
                    out.append(deltas[h] + jnp.sum(p * dp, axis=1, keepdims=True))
                return tuple(out)

            deltas = lax.fori_loop(0, i + 1, row_dot, tuple(jnp.zeros((QB, 1), F32) for _ in HEADS))

            def kblock(j, dqs):
                out = []
                for h in HEADS:
                    s0, kb, p, dp = probs(h, j)
                    dz = p * (dp - deltas[h])
                    dzb = dz.astype(BF16)
                    dk_ref[h, pl.ds(s0, QB), :] += _dot_tn(dzb, qbs[h]) * SCALE
                    dv_ref[h, pl.ds(s0, QB), :] += _dot_tn(p.astype(BF16), dobs[h])
                    dfk_ref[h, pl.ds(j, 1), :] += -jnp.sum(dz, axis=0, keepdims=True)
                    out.append(dqs[h] + jnp.dot(dzb, kb, preferred_element_type=F32))
                return tuple(out)

            dqs = lax.fori_loop(0, i + 1, kblock, tuple(jnp.zeros((QB, HEAD_DIM), F32) for _ in HEADS))
            for h in HEADS:
                dq_ref[h, pl.ds(t0, QB), :] = dqs[h] * SCALE
            return 0

        lax.fori_loop(0, nb, qblock, 0)

    spec = _head_spec(T)
    fq_spec, fk_spec = _fox_specs(T)
    return _pcall_carrying(body, carry, name=name, grid=(H // HEADS_PER_STEP,),
                  in_specs=[spec] * 3 + [fq_spec, fk_spec, fq_spec, spec],
                  out_specs=[spec, spec, spec, fk_spec],
                  out_shape=[jax.ShapeDtypeStruct((H, T, HEAD_DIM), F32)] * 3
                  + [jax.ShapeDtypeStruct((H, nb, QB), F32)],
                  )(q, k, v, fq, fk, lse, do)


def _forget_cumsum(flog, name):
    R, T = flog.shape
    nb = T // QB

    def body(x_ref, o_ref):
        row, col = _iota2((QB, QB))
        m_le = (row <= col).astype(BF16)
        carry = jnp.zeros((R, 1), F32)
        for b in range(nb):
            lf = _log_sigmoid(x_ref[:, b * QB:(b + 1) * QB])
            o_ref[:, b * QB:(b + 1) * QB] = _dot3(lf, m_le) + carry
            carry = carry + jnp.sum(lf, axis=1, keepdims=True)

    spec = pl.BlockSpec((R, T), lambda: (0, 0))
    return _pcall(body, name=name, in_specs=[spec], out_specs=spec,
                  out_shape=jax.ShapeDtypeStruct((R, T), F32))(flog)


def _forget_cumsum_bwd(flog, df, name):
    R, T = flog.shape
    nb = T // QB

    def body(x_ref, df_ref, o_ref):
        row, col = _iota2((QB, QB))
        m_ge = (row >= col).astype(BF16)
        carry = jnp.zeros((R, 1), F32)
        for b in reversed(range(nb)):
            dfb = df_ref[:, b * QB:(b + 1) * QB]
            dlog = _dot3(dfb, m_ge) + carry
            o_ref[:, b * QB:(b + 1) * QB] = dlog * _sigmoid(-x_ref[:, b * QB:(b + 1) * QB])
            carry = carry + jnp.sum(dfb, axis=1, keepdims=True)

    spec = pl.BlockSpec((R, T), lambda: (0, 0))
    return _pcall(body, name=name, in_specs=[spec, spec], out_specs=spec,
                  out_shape=jax.ShapeDtypeStruct((R, T), F32))(flog, df)


def _chunk_probs(qc, kb, bias, c, col):
    z = _dot_nt(qc, kb) * SCALE + bias
    z = jnp.where((c - (LEFT_CHUNKS + 1)) * CHUNK + col >= jnp.maximum(0, (c - LEFT_CHUNKS) * CHUNK), z, NEG)
    p = jnp.exp(z - jnp.max(z, axis=1, keepdims=True))
    return p, jnp.sum(p, axis=1, keepdims=True)


def _chunk_specs(T, n=CHUNK_HEADS_PER_STEP):
    return (_head_spec(T, heads=n), _head_spec(T, rows=T + BAND_PAD, heads=n),
            _head_spec(T, rows=CHUNK, width=BAND, heads=n))


def _chunk_fwd(q, kp, vp, bias, name, carry=None):
    H, T, _ = q.shape
    nc = T // CHUNK

    def body(q_ref, kp_ref, vp_ref, bias_ref, o_ref):
        _, col = _iota2((CHUNK, BAND))

        def chunk(c, _):
            t0 = pl.multiple_of(c * CHUNK, CHUNK)
            for h in HEADS:
                qc = q_ref[h, pl.ds(t0, CHUNK), :].astype(BF16)
                kb = kp_ref[h, pl.ds(t0, BAND), :].astype(BF16)
                vb = vp_ref[h, pl.ds(t0, BAND), :].astype(BF16)
                p, l = _chunk_probs(qc, kb, bias_ref[h], c, col)
                o_ref[h, pl.ds(t0, CHUNK), :] = jnp.dot(p.astype(BF16), vb, preferred_element_type=F32) / l
            return 0

        lax.fori_loop(0, nc, chunk, 0)

    q_spec, kp_spec, b_spec = _chunk_specs(T, HEADS_PER_STEP)
    return _pcall_carrying(body, carry, name=name, grid=(H // HEADS_PER_STEP,),
                           in_specs=[q_spec, kp_spec, kp_spec, b_spec], out_specs=q_spec,
                           out_shape=jax.ShapeDtypeStruct((H, T, HEAD_DIM), F32))(q, kp, vp, bias)


def _chunk_bwd(q, kp, vp, bias, do, name, carry=None):
    H, T, _ = q.shape
    nc = T // CHUNK

    def body(q_ref, kp_ref, vp_ref, bias_ref, do_ref, dq_ref, dkp_ref, dvp_ref, db_ref):
        _, col = _iota2((CHUNK, BAND))
        dkp_ref[...] = jnp.zeros_like(dkp_ref)
        dvp_ref[...] = jnp.zeros_like(dvp_ref)
        db_ref[...] = jnp.zeros_like(db_ref)

        def chunk(c, _):
            t0 = pl.multiple_of(c * CHUNK, CHUNK)
            for h in CHUNK_HEADS:
                qc = q_ref[h, pl.ds(t0, CHUNK), :].astype(BF16)
                kb = kp_ref[h, pl.ds(t0, BAND), :].astype(BF16)
                vb = vp_ref[h, pl.ds(t0, BAND), :].astype(BF16)
                dob = do_ref[h, pl.ds(t0, CHUNK), :].astype(BF16)
                p, l = _chunk_probs(qc, kb, bias_ref[h], c, col)
                p = p / l
                dp = _dot_nt(dob, vb)
                dz = p * (dp - jnp.sum(p * dp, axis=1, keepdims=True))
                dzb = dz.astype(BF16)
                dq_ref[h, pl.ds(t0, CHUNK), :] = jnp.dot(dzb, kb, preferred_element_type=F32) * SCALE
                dkp_ref[h, pl.ds(t0, BAND), :] += _dot_tn(dzb, qc) * SCALE
                dvp_ref[h, pl.ds(t0, BAND), :] += _dot_tn(p.astype(BF16), dob)
                db_ref[h] += dz
            return 0

        lax.fori_loop(0, nc, chunk, 0)

    q_spec, kp_spec, b_spec = _chunk_specs(T)
    return _pcall_carrying(body, carry, name=name, grid=(H // CHUNK_HEADS_PER_STEP,),
                  in_specs=[q_spec, kp_spec, kp_spec, b_spec, q_spec],
                  out_specs=[q_spec, kp_spec, kp_spec, b_spec],
                  out_shape=[jax.ShapeDtypeStruct((H, T, HEAD_DIM), F32),
                             jax.ShapeDtypeStruct((H, T + BAND_PAD, HEAD_DIM), F32),
                             jax.ShapeDtypeStruct((H, T + BAND_PAD, HEAD_DIM), F32),
                             jax.ShapeDtypeStruct((H, CHUNK, BAND), F32)])(q, kp, vp, bias, do)


HBM_SPEC = pl.BlockSpec(memory_space=pltpu.HBM)


def _position():
    return lax.axis_index("x"), lax.axis_index("y"), lax.axis_index("c")


def _other_chips(x, y):
    chips = [(1 - x, y), (x, 1 - y), (1 - x, 1 - y)]
    return [(chip, 2 * chip[0] + chip[1]) for chip in chips]


def _remote_copy(src, dst, send_sems, recv_sems, k, to):
    return pltpu.make_async_remote_copy(src_ref=src, dst_ref=dst, send_sem=send_sems.at[k], recv_sem=recv_sems.at[k],
                                        device_id=to, device_id_type=MESH)


def _gather_over_ici(layer):
    def plan(srcs, bufs, new):
        x, y, c = _position()
        return [(s.at[layer, c], b.at[2 * x + y, c], (*chip, c))
                for s, b in zip(srcs, bufs) for chip, _ in _other_chips(x, y)]
    return plan


def _gather_to_sibling(srcs, bufs, new):
    x, y, c = _position()
    return [(b.at[idx, c], b.at[idx, c], (x, y, 1 - c)) for b in bufs for _, idx in _other_chips(x, y)]


def _gather_layer(shards, bufs, layer, name):
    n = len(shards)

    def body(*refs):
        src, dst = refs[:n], refs[2 * n:3 * n]
        send_sems, recv_sems = refs[3 * n:]
        x, y, c = _position()
        me = 2 * x + y
        sibling = (x, y, 1 - c)
        others = _other_chips(x, y)
        first = [_remote_copy(src[i].at[layer, c], dst[i].at[me, c], send_sems, recv_sems, 3 * i + k, (*chip, c))
                 for i in range(n) for k, (chip, _) in enumerate(others)]
        for cp in first:
            cp.start()
        passed = []
        for i in range(n):
            for k, (_, idx) in enumerate(others):
                landed = dst[i].at[idx, c]
                _remote_copy(landed, landed, send_sems, recv_sems, 3 * i + k, sibling).wait_recv()
                passed.append(_remote_copy(landed, landed, send_sems, recv_sems, 3 * (n + i) + k, sibling))
                passed[-1].start()
        for i in range(n):
            for k, (_, idx) in enumerate(others):
                from_sibling = dst[i].at[idx, 1 - c]
                _remote_copy(from_sibling, from_sibling, send_sems, recv_sems, 3 * (n + i) + k, sibling).wait_recv()
        for cp in first + passed:
            cp.wait_send()

    return _pcall(
        body, name=name, in_specs=[HBM_SPEC] * (2 * n), out_specs=[HBM_SPEC] * n,
        out_shape=[jax.ShapeDtypeStruct(b.shape, b.dtype) for b in bufs],
        scratch_shapes=[pltpu.SemaphoreType.DMA((6 * n,)), pltpu.SemaphoreType.DMA((6 * n,))],
        input_output_aliases={n + i: i for i in range(n)},
    )(*shards, *bufs)


def _send_other_half(parts, name):
    n = len(parts)

    def body(*refs):
        src, got = refs[:n], refs[n:2 * n]
        send_sems, recv_sems = refs[2 * n:]
        x, y, c = _position()
        copies = [_remote_copy(src[i].at[1 - c], got[i], send_sems, recv_sems, i, (x, y, 1 - c)) for i in range(n)]
        for cp in copies:
            cp.start()
        for cp in copies:
            cp.wait()

    return _pcall(
        body, name=name, in_specs=[HBM_SPEC] * n, out_specs=[HBM_SPEC] * n,
        out_shape=[jax.ShapeDtypeStruct(p.shape[1:], p.dtype) for p in parts],
        scratch_shapes=[pltpu.SemaphoreType.DMA((n,)), pltpu.SemaphoreType.DMA((n,))],
    )(*parts)


def _scatter_chips(parts, name):
    n = len(parts)

    def body(*refs):
        src, dst = refs[:n], refs[n:2 * n]
        send_sems, recv_sems = refs[2 * n:]
        x, y, c = _position()
        others = _other_chips(x, y)
        sends = [_remote_copy(src[i].at[idx], dst[i].at[k], send_sems, recv_sems, 3 * i + k, (*chip, c))
                 for i in range(n) for k, (chip, idx) in enumerate(others)]
        for cp in sends:
            cp.start()
        for cp in sends:
            cp.wait()

    return _pcall(
        body, name=name, in_specs=[HBM_SPEC] * n, out_specs=[HBM_SPEC] * n,
        out_shape=[jax.ShapeDtypeStruct((3,) + p.shape[1:], p.dtype) for p in parts],
        scratch_shapes=[pltpu.SemaphoreType.DMA((3 * n,)), pltpu.SemaphoreType.DMA((3 * n,))],
    )(*parts)


def _swap_with_sibling(arrs, name):
    n = len(arrs)

    def body(*refs):
        src, dst = refs[:n], refs[n:2 * n]
        send_sems, recv_sems = refs[2 * n:]
        x, y, c = _position()
        copies = [_remote_copy(src[i], dst[i], send_sems, recv_sems, i, (x, y, 1 - c)) for i in range(n)]
        for cp in copies:
            cp.start()
        for cp in copies:
            cp.wait()

    return _pcall(
        body, name=name, in_specs=[HBM_SPEC] * n, out_specs=[HBM_SPEC] * n,
        out_shape=[jax.ShapeDtypeStruct(a.shape, a.dtype) for a in arrs],
        scratch_shapes=[pltpu.SemaphoreType.DMA((n,)), pltpu.SemaphoreType.DMA((n,))],
    )(*arrs)


def _allreduce_small(v, name):
    R, C = v.shape

    def body(v_ref, o_ref, slots, send_sems, recv_sems):
        x, y, c = _position()
        me = 4 * x + 2 * y + c
        slots[0] = v_ref[...]
        sends = []
        for r in range(1, 8):
            fx, fy, fc = (r >> 2) & 1, (r >> 1) & 1, r & 1
            to = (x ^ fx, y ^ fy, c ^ fc)
            cp = pltpu.make_async_remote_copy(src_ref=v_ref, dst_ref=slots.at[r], send_sem=send_sems.at[r - 1],
                                              recv_sem=recv_sems.at[r - 1], device_id=to, device_id_type=MESH)
            cp.start()
            sends.append(cp)
        for cp in sends:
            cp.wait()
        acc = slots[me]
        for d in range(1, 8):
            acc = acc + slots[d ^ me]
        o_ref[...] = acc

    vmem = pl.BlockSpec(memory_space=pltpu.VMEM)
    return _pcall(
        body, name=name, in_specs=[vmem], out_specs=vmem, out_shape=jax.ShapeDtypeStruct((R, C), F32),
        scratch_shapes=[pltpu.VMEM((8, R, C), F32), pltpu.SemaphoreType.DMA((7,)), pltpu.SemaphoreType.DMA((7,))],
    )(v)


def _add_pair(which, both, got, name):
    _, N, R, C = both.shape
    tr = _pick(R, 512, 16)

    def body(which_ref, a_ref, b_ref, o_ref):
        o_ref[...] = (a_ref[...].astype(F32) + b_ref[...].astype(F32)).astype(BF16)

    spec = pl.BlockSpec((None, tr, C), lambda n, i, w: (n, i, 0))
    grid_spec = pltpu.PrefetchScalarGridSpec(
        num_scalar_prefetch=1, grid=(N, R // tr),
        in_specs=[pl.BlockSpec((None, None, tr, C), lambda n, i, w: (w[0], n, i, 0)), spec], out_specs=spec)
    return _pcall(body, name=name, grid_spec=grid_spec,
                  out_shape=jax.ShapeDtypeStruct((N, R, C), BF16))(which, both, got)


def _sum_chips(which, own, others, name):
    N, R, C = others.shape
    tr = _pick(R, 512, 16)

    def body(which_ref, own_ref, p_ref, o_ref):
        acc = own_ref[...].astype(F32)
        for n in range(N):
            acc = acc + p_ref[n].astype(F32)
        o_ref[...] = acc

    grid_spec = pltpu.PrefetchScalarGridSpec(
        num_scalar_prefetch=1, grid=(R // tr,),
        in_specs=[pl.BlockSpec((None, tr, C), lambda i, w: (w[0], i, 0)), pl.BlockSpec((N, tr, C), lambda i, w: (0, i, 0))],
        out_specs=pl.BlockSpec((tr, C), lambda i, w: (i, 0)))
    return _pcall(body, name=name, grid_spec=grid_spec,
                  out_shape=jax.ShapeDtypeStruct((R, C), F32))(which, own, others)


BIG = ("w_ffn1_in", "w_ffn1_out", "w_in", "w_br_sb", "w_br_ch", "w_br_fox", "w_out", "w_ffn2_in", "w_ffn2_out")
ROW_SHARDED = ("w_ffn1_out", "w_out", "w_ffn2_out")
SMALL = ("g_ffn1", "g_mix", "b_in", "rel_bias", "g_ffn2", "g_final")
SHARD_SHAPES = {
    "w_ffn1_in": (D_MODEL, 2 * D_FF // N_CHIPS), "w_ffn1_out": (D_FF // N_CHIPS, D_MODEL),
    "w_in": (D_MODEL, IN_WIDTH // N_CHIPS), "w_br_sb": (W_SB, D_MODEL // N_CHIPS),
    "w_br_ch": (W_CH, D_MODEL // N_CHIPS), "w_br_fox": (W_FOX, D_MODEL // N_CHIPS),
    "w_out": (D_MODEL // N_CHIPS, D_MODEL), "w_ffn2_in": (D_MODEL, 2 * D_FF // N_CHIPS),
    "w_ffn2_out": (D_FF // N_CHIPS, D_MODEL),
}


def _pair_sums(split, tag):
    core = lax.axis_index("c").astype(jnp.int32)[None]
    got = _send_other_half(split, f"grad_split_{tag}")
    return [_add_pair(core, a, b, f"grad_pair_sum_{tag}_{i}") for i, (a, b) in enumerate(zip(split, got))]


def _scatter_plan(srcs, bufs, new):
    x, y, c = _position()
    return [(s.at[idx], d.at[k], (*chip, c)) for s, d in zip(srcs, new) for k, (chip, idx) in enumerate(_other_chips(x, y))]


def _scatter_carry(pair):
    landing = [jax.ShapeDtypeStruct((3,) + p.shape[1:], p.dtype) for p in pair]
    return _Carry(pair, [], landing, 3 * len(pair), _scatter_plan)


def _finish_grads(pair, landed, axes, tag):
    chip = (2 * lax.axis_index("x") + lax.axis_index("y")).astype(jnp.int32)[None]
    mine = [_sum_chips(chip, p, q, f"grad_chip_sum_{tag}_{i}") for i, (p, q) in enumerate(zip(pair, landed))]
    theirs = _swap_with_sibling(mine, f"grad_share_{tag}")
    first = lax.axis_index("c") == 0
    return [jnp.concatenate([jnp.where(first, a, b), jnp.where(first, b, a)], axis=ax)
            for a, b, ax in zip(mine, theirs, axes)]


SMALL_SIZES = {"g_ffn1": DEPTH * D_MODEL, "g_mix": DEPTH * D_MODEL, "b_in": DEPTH * IN_WIDTH,
               "rel_bias": DEPTH * N_REL * H_CH, "g_ffn2": DEPTH * D_MODEL, "g_final": D_MODEL}
SMALL_TOTAL = sum(SMALL_SIZES.values()) + 1
SMALL_ROWS = -(-SMALL_TOTAL // (8 * 128)) * 8


def _pack_small(vals, extra):
    flat = [vals[n].reshape(-1) for n in SMALL] + [extra.reshape(-1)]
    flat.append(jnp.zeros((SMALL_ROWS * 128 - SMALL_TOTAL,), F32))
    return jnp.concatenate(flat).reshape(SMALL_ROWS, 128)


def _unpack_small(pack, shapes):
    flat, out, off = pack.reshape(-1), {}, 0
    for n in SMALL:
        out[n] = flat[off:off + SMALL_SIZES[n]].reshape(shapes[n])
        off += SMALL_SIZES[n]
    return out, flat[off]


def _to_heads(t, n_heads):
    return jnp.transpose(t.reshape(t.shape[0], n_heads, HEAD_DIM), (1, 0, 2)).astype(BF16)


def _from_heads(t):
    return jnp.transpose(t, (1, 0, 2)).reshape(t.shape[1], t.shape[0] * HEAD_DIM)


def _pad_keys(t):
    return jnp.pad(t, ((0, 0), (BAND_PAD, 0), (0, 0)))


DIAGONALS = CHUNK + BAND - 1


def _band_bias(table):
    n_far = (LEFT_CHUNKS + 1) * CHUNK + CHUNK - MAX_REL
    near = table[N_REL - 1 - (DIAGONALS - n_far):N_REL - 1][::-1]
    diag = jnp.concatenate([jnp.broadcast_to(table[N_REL - 1], (n_far, H_CH)), near], axis=0).T
    diag = jnp.pad(diag, ((0, 0), (0, 1)))
    skew = jnp.tile(diag, (1, CHUNK))[:, :CHUNK * DIAGONALS].reshape(H_CH, CHUNK, DIAGONALS)
    return skew[:, :, CHUNK - 1:]


def _pad_w_in(w):
    qkv, f, gates = w[:, :QKV_WIDTH], w[:, QKV_WIDTH:QKV_WIDTH + H_FOX], w[:, QKV_WIDTH + H_FOX:]
    return jnp.concatenate([qkv, gates, f, jnp.zeros((w.shape[0], F_PAD - H_FOX), w.dtype)], axis=1)


def _unpad_w_in(w):
    return jnp.concatenate([w[:, :QKV_WIDTH], w[:, QKV_WIDTH + GATE_WIDTH:QKV_WIDTH + GATE_WIDTH + H_FOX],
                            w[:, QKV_WIDTH:QKV_WIDTH + GATE_WIDTH]], axis=1)


QKV_SPLITS = np.cumsum([0, W_SB, W_SB, W_SB, W_CH, W_CH, W_CH, W_FOX, W_FOX, W_FOX])


def _by_chip_rows(g):
    return g.reshape(2, N_CHIPS, g.shape[1] // N_CHIPS, g.shape[2])


def _ffn_fwd(x, g, w_in, w_out, tag, carries):
    h = _rmsnorm_fwd(x, g, f"{tag}_norm")
    gu = _mm(h, w_in, mode="nn", name=f"{tag}_in", gathered="col", out_dtype=BF16, carry=_carry_of(carries, "in"))
    a = _swiglu_fwd(gu, f"{tag}_act")
    y = _mm(a, w_out, mode="nn", name=f"{tag}_out", alpha=0.5, res=x, gathered="row",
            carry=_carry_of(carries, "out"))
    return y, (h, gu, a)


def _ffn_bwd(x, g, w_in, w_out, saved, dy, tag, carries):
    h, gu, a = saved
    da = _mm(dy, w_out, mode="nt", name=f"{tag}_da", alpha=0.5, gathered="row", out_dtype=BF16,
             carry=_carry_of(carries, "da"))
    dw_out = _mm(a, dy, mode="tn", name=f"{tag}_dwout", alpha=0.5, tm_cap=1408, out_dtype=BF16, out_split="row",
                 carry=_carry_of(carries, "dwout"))
    dgu = _swiglu_bwd(gu, da, f"{tag}_dact")
    dw_in = _mm(h, dgu, mode="tn", name=f"{tag}_dwin", tm_cap=512, out_dtype=BF16, out_split="col",
                carry=_carry_of(carries, "dwin"))
    dh = _mm(dgu, w_in, mode="nt", name=f"{tag}_dh", gathered="col", tn_cap=1024, carry=_carry_of(carries, "dh"))
    dx, dg = _rmsnorm_bwd(x, g, dh, dy, f"{tag}_dnorm")
    return dx, dg, dw_in, _by_chip_rows(dw_out)


def _mixer_fwd(x, lw, tag, carries):
    T = x.shape[0]
    h = _rmsnorm_fwd(x, lw["g_mix"], f"{tag}_norm")
    p = _mm(h, lw["w_in"], mode="nn", name=f"{tag}_proj", bias=lw["b_in"], tn_cap=896,
            carry=_carry_of(carries, "proj"))
    parts = [p[:, QKV_SPLITS[i]:QKV_SPLITS[i + 1]] for i in range(9)]
    qa, ka, va = (_to_heads(t, H_SB) for t in parts[0:3])
    qb, kb, vb = (_to_heads(t, H_CH) for t in parts[3:6])
    qc, kc, vc = (_to_heads(t, H_FOX) for t in parts[6:9])
    fcols = lax.optimization_barrier(p[:, QKV_WIDTH + GATE_WIDTH:])
    flog = jnp.pad(fcols[:, :H_FOX].T, ((0, 8 - H_FOX), (0, 0)))
    fcum = _forget_cumsum(flog, f"{tag}_fcum")[:H_FOX]
    fq, fk = fcum.reshape(H_FOX, T, 1), fcum.reshape(H_FOX, T // QB, QB)
    kbp, vbp = _pad_keys(kb), _pad_keys(vb)
    oa = _sb_fwd(qa, ka, va, f"{tag}_sb", _carry_of(carries, "sb"))
    ob = _chunk_fwd(qb, kbp, vbp, lw["bias"], f"{tag}_ch", _carry_of(carries, "ch"))
    oc, lse = _fox_fwd(qc, kc, vc, fq, fk, f"{tag}_fox", _carry_of(carries, "fox"))
    oam, obm, ocm = _from_heads(oa), _from_heads(ob), _from_heads(oc)
    ya = _mm(oam, lw["w_br_sb"], mode="nn", name=f"{tag}_bra", gathered="col")
    yb = _mm(obm, lw["w_br_ch"], mode="nn", name=f"{tag}_brb", gathered="col")
    yc = _mm(ocm, lw["w_br_fox"], mode="nn", name=f"{tag}_brc", gathered="col")
    merged = _gate_fwd(p, ya, yb, yc, f"{tag}_gate")
    y = _mm(merged, lw["w_out"], mode="nn", name=f"{tag}_out", res=x, gathered="row")
    saved = (h, p, (qa, ka, va), (qb, kbp, vbp), (qc, kc, vc), flog, fq, fk, lse, (oam, obm, ocm),
             (ya, yb, yc), merged)
    return y, saved


def _mixer_bwd(x, lw, saved, dy, tag, carries):
    (h, p, (qa, ka, va), (qb, kbp, vbp), (qc, kc, vc), flog, fq, fk, lse, (oam, obm, ocm),
     (ya, yb, yc), merged) = saved
    T = x.shape[0]
    grads = {}
    col, row = "col", "row"
    dmerged = _mm(dy, lw["w_out"], mode="nt", name=f"{tag}_dmerged", gathered=row)
    grads["w_out"] = _by_chip_rows(_mm(merged, dy, mode="tn", name=f"{tag}_dwout", tm_cap=1024, out_dtype=BF16,
                                       out_split="row"))
    dya, dyb, dyc, dgate = _gate_bwd(p, ya, yb, yc, dmerged, f"{tag}_dgate")
    doa = _mm(dya, lw["w_br_sb"], mode="nt", name=f"{tag}_doa", gathered=col)
    dob = _mm(dyb, lw["w_br_ch"], mode="nt", name=f"{tag}_dob", gathered=col)
    doc = _mm(dyc, lw["w_br_fox"], mode="nt", name=f"{tag}_doc", gathered=col)
    by_chip = dict(mode="tn", tm_cap=512, out_dtype=BF16, out_split="col")
    grads["w_br_sb"] = _mm(oam, dya, name=f"{tag}_dwbra", **by_chip)
    grads["w_br_ch"] = _mm(obm, dyb, name=f"{tag}_dwbrb", **by_chip)
    grads["w_br_fox"] = _mm(ocm, dyc, name=f"{tag}_dwbrc", **by_chip)
    dqa, dka, dva = _sb_bwd(qa, ka, va, _to_heads(doa, H_SB), f"{tag}_dsb", _carry_of(carries, "dsb"))
    dqb, dkbp, dvbp, dbias = _chunk_bwd(qb, kbp, vbp, lw["bias"], _to_heads(dob, H_CH), f"{tag}_dch",
                                        _carry_of(carries, "dch"))
    dqc, dkc, dvc, dfk = _fox_bwd(qc, kc, vc, fq, fk, lse, _to_heads(doc, H_FOX), f"{tag}_dfox",
                                  _carry_of(carries, "dfox"))
    dflog = _forget_cumsum_bwd(flog, jnp.pad(dfk.reshape(H_FOX, T), ((0, 8 - H_FOX), (0, 0))), f"{tag}_dfcum")
    dqkv = [_from_heads(t) for t in (dqa, dka, dva, dqb, dkbp[:, BAND_PAD:], dvbp[:, BAND_PAD:], dqc, dkc, dvc)]
    dp = jnp.concatenate(dqkv + [dgate, jnp.pad(dflog[:H_FOX].T, ((0, 0), (0, F_PAD - H_FOX)))], axis=1)
    grads["b_in"] = _colsum(dp, f"{tag}_dbin")
    dpb = dp.astype(BF16)
    dw_in = _unpad_w_in(_mm(h, dpb, mode="tn", name=f"{tag}_dwin", tm_cap=512, tn_cap=896, out_dtype=BF16))
    grads["w_in"] = jnp.transpose(dw_in.reshape(2, D_MODEL // 2, N_CHIPS, IN_WIDTH // N_CHIPS), (0, 2, 1, 3))
    dh = _mm(dpb, lw["w_in"], mode="nt", name=f"{tag}_dh", tk_cap=896, tn_cap=1024)
    dx, grads["g_mix"] = _rmsnorm_bwd(x, lw["g_mix"], dh, dy, f"{tag}_dnorm")
    grads["rel_bias"] = lw["bias_vjp"](dbias)[0]
    return dx, grads


def kernel(x, g_ffn1, w_ffn1_in, w_ffn1_out, g_mix, w_in, b_in, rel_bias, w_br_sb, w_br_ch, w_br_fox, w_out, g_ffn2, w_ffn2_in, w_ffn2_out, g_final, loss_target, m_g_ffn1, m_w_ffn1_in, m_w_ffn1_out, m_g_mix, m_w_in, m_b_in, m_rel_bias, m_w_br_sb, m_w_br_ch, m_w_br_fox, m_w_out, m_g_ffn2, m_w_ffn2_in, m_w_ffn2_out, m_g_final, v_g_ffn1, v_w_ffn1_in, v_w_ffn1_out, v_g_mix, v_w_in, v_b_in, v_rel_bias, v_w_br_sb, v_w_br_ch, v_w_br_fox, v_w_out, v_g_ffn2, v_w_ffn2_in, v_w_ffn2_out, v_g_final):
    names = ("g_ffn1", "w_ffn1_in", "w_ffn1_out", "g_mix", "w_in", "b_in", "rel_bias", "w_br_sb", "w_br_ch",
             "w_br_fox", "w_out", "g_ffn2", "w_ffn2_in", "w_ffn2_out", "g_final")
    w = dict(zip(names, (g_ffn1, w_ffn1_in, w_ffn1_out, g_mix, w_in, b_in, rel_bias, w_br_sb, w_br_ch,
                         w_br_fox, w_out, g_ffn2, w_ffn2_in, w_ffn2_out, g_final)))
    m = dict(zip(names, (m_g_ffn1, m_w_ffn1_in, m_w_ffn1_out, m_g_mix, m_w_in, m_b_in, m_rel_bias, m_w_br_sb,
                         m_w_br_ch, m_w_br_fox, m_w_out, m_g_ffn2, m_w_ffn2_in, m_w_ffn2_out, m_g_final)))
    v = dict(zip(names, (v_g_ffn1, v_w_ffn1_in, v_w_ffn1_out, v_g_mix, v_w_in, v_b_in, v_rel_bias, v_w_br_sb,
                         v_w_br_ch, v_w_br_fox, v_w_out, v_g_ffn2, v_w_ffn2_in, v_w_ffn2_out, v_g_final)))
    xs = x[0]
    tgt = loss_target[0]

    own = [w[n].astype(BF16) for n in BIG]
    halves = [o.reshape(DEPTH, 2, o.shape[1] // 2, o.shape[2]) for o in own]
    chip = 2 * lax.axis_index("x") + lax.axis_index("y")

    def own_in_place(l):
        return [lax.dynamic_update_slice(lax.empty((N_CHIPS,) + o.shape[1:], BF16), o[l][None], (chip, 0, 0))
                .reshape((N_CHIPS,) + h.shape[1:]) for o, h in zip(own, halves)]

    bufs = [own_in_place(l) for l in range(DEPTH)]
    padded_w_in = {}

    def layer_weights(l):
        lw = {n: b.reshape((N_CHIPS,) + o.shape[1:]) for n, b, o in zip(BIG, bufs[l], own)}
        if l not in padded_w_in:
            padded_w_in[l] = _pad_w_in(jnp.concatenate([lw["w_in"][j] for j in range(N_CHIPS)], axis=1))
        lw["w_in"] = padded_w_in[l]
        lw["b_in"] = _pad_w_in(w["b_in"][l][None, :])
        lw["bias"], lw["bias_vjp"] = jax.vjp(_band_bias, w["rel_bias"][l])
        for n in ("g_ffn1", "g_mix", "g_ffn2"):
            lw[n] = w[n][l][None, :]
        return lw

    def landed_in(l, idx):
        def done(carry):
            for i, b in zip(idx, carry.out[0]):
                bufs[l][i] = b
        return done

    def over_ici(l, idx):
        return lambda: _Carry([halves[i] for i in idx], [bufs[l][i] for i in idx], [], 3 * len(idx),
                              _gather_over_ici(l), landed_in(l, idx))

    def to_sibling(l, idx):
        return lambda: _Carry([], [bufs[l][i] for i in idx], [], 3 * len(idx), _gather_to_sibling, landed_in(l, idx))

    def ffn_fwd(x_in, lw, which, tag, carries):
        return _ffn_fwd(x_in, lw[f"g_{which}"], lw[f"w_{which}_in"], lw[f"w_{which}_out"], tag, carries)

    def ffn_bwd(x_in, lw, which, saved_acts, dy, tag, carries):
        return _ffn_bwd(x_in, lw[f"g_{which}"], lw[f"w_{which}_in"], lw[f"w_{which}_out"], saved_acts, dy, tag,
                        carries)

    FFN1, W_IN, MIXER_REST, FFN2_IN, FFN2_OUT = (0, 1), (2,), (3, 4, 5, 6), (7,), (8,)
    first = FFN1 + W_IN
    for i, b in zip(first, _gather_layer([halves[i] for i in first], [bufs[0][i] for i in first], 0, "gather_l0")):
        bufs[0][i] = b
    fwd_carries = [
        {"ffn1": {"in": [over_ici(0, MIXER_REST)], "out": [to_sibling(0, MIXER_REST), over_ici(0, FFN2_OUT)]},
         "mix": {"proj": [to_sibling(0, FFN2_OUT), over_ici(1, MIXER_REST)],
                 "sb": [over_ici(0, FFN2_IN), over_ici(1, FFN2_OUT)],
                 "ch": [to_sibling(0, FFN2_IN), over_ici(1, FFN1)],
                 "fox": [over_ici(1, W_IN)]},
         "ffn2": {"in": [to_sibling(1, MIXER_REST + FFN2_OUT + FFN1 + W_IN)]}},
        {"ffn1": {}, "mix": {"sb": [over_ici(1, FFN2_IN)], "ch": [to_sibling(1, FFN2_IN)]}, "ffn2": {}},
    ]

    saved = []
    act = xs
    for l in range(DEPTH):
        x0 = act
        x1, s1 = ffn_fwd(x0, layer_weights(l), "ffn1", f"l{l}_ffn1", fwd_carries[l]["ffn1"])
        x2, s2 = _mixer_fwd(x1, layer_weights(l), f"l{l}_mix", fwd_carries[l]["mix"])
        act, s3 = ffn_fwd(x2, layer_weights(l), "ffn2", f"l{l}_ffn2", fwd_carries[l]["ffn2"])
        saved.append((x0, s1, x1, s2, x2, s3))
    layers = [layer_weights(l) for l in range(DEPTH)]

    dact, dg_final, loss_row = _final_loss(act, w["g_final"][None, :], tgt, "final_loss")

    big_grads = {n: [None] * DEPTH for n in BIG}
    small_grads = {n: [None] * DEPTH for n in ("g_ffn1", "g_mix", "b_in", "rel_bias", "g_ffn2")}
    pair = [[None] * len(BIG) for _ in range(DEPTH)]
    landed = [[None] * len(BIG) for _ in range(DEPTH)]

    def pair_up(l, idx, tag):
        for i, p in zip(idx, _pair_sums([big_grads[BIG[i]][l] for i in idx], tag)):
            pair[l][i] = p

    def exchange(l, idx):
        def done(carry):
            for i, a in zip(idx, carry.out[1]):
                landed[l][i] = a
        def make():
            carry = _scatter_carry([pair[l][i] for i in idx])
            carry.done = done
            return carry
        return make

    def exchange_one_way(l, i, k):
        def plan(srcs, bufs, new):
            x, y, c = _position()
            chip_k, idx_k = _other_chips(x, y)[k]
            return [(srcs[0].at[idx_k], bufs[0].at[k], (*chip_k, c))]
        def done(carry):
            landed[l][i] = carry.out[0][0]
        def make():
            if landed[l][i] is None:
                landed[l][i] = lax.empty((3,) + pair[l][i].shape[1:], BF16)
            return _Carry([pair[l][i]], [landed[l][i]], [], 1, plan, done)
        return make

    bwd_carries = [
        {"ffn2": {},
         "mix": {"dsb": [exchange(1, FFN1 + W_IN)], "dch": [exchange(1, MIXER_REST + FFN2_IN + FFN2_OUT)],
                 "dfox": [exchange(0, FFN2_IN + FFN2_OUT)]},
         "ffn1": {"da": [exchange(0, MIXER_REST)], "dwout": [exchange_one_way(0, 2, 0)],
                  "dwin": [exchange_one_way(0, 2, 1)], "dh": [exchange_one_way(0, 2, 2)]}},
        {"ffn2": {}, "mix": {}, "ffn1": {}},
    ]
    for l in reversed(range(DEPTH)):
        lw = layers[l]
        x0, s1, x1, s2, x2, s3 = saved[l]
        dx2, small_grads["g_ffn2"][l], big_grads["w_ffn2_in"][l], big_grads["w_ffn2_out"][l] = ffn_bwd(
            x2, lw, "ffn2", s3, dact, f"l{l}_ffn2", bwd_carries[l]["ffn2"])
        if l == 0:
            pair_up(0, FFN2_IN + FFN2_OUT, "l0_ffn2")
        dx1, mg = _mixer_bwd(x1, lw, s2, dx2, f"l{l}_mix", bwd_carries[l]["mix"])
        for n in ("w_in", "w_br_sb", "w_br_ch", "w_br_fox", "w_out"):
            big_grads[n][l] = mg[n]
        small_grads["b_in"][l] = _unpad_w_in(mg["b_in"])[0]
        small_grads["g_mix"][l] = mg["g_mix"][0]
        small_grads["rel_bias"][l] = mg["rel_bias"]
        if l == 0:
            pair_up(0, W_IN + MIXER_REST, "l0_mix")
        dact, dg1, big_grads["w_ffn1_in"][l], big_grads["w_ffn1_out"][l] = ffn_bwd(
            x0, lw, "ffn1", s1, dx1, f"l{l}_ffn1", bwd_carries[l]["ffn1"])
        small_grads["g_ffn1"][l] = dg1[0]
        small_grads["g_ffn2"][l] = small_grads["g_ffn2"][l][0]
        if l == 1:
            pair_up(1, tuple(range(len(BIG))), "l1")
    grad_x = dact[None]
    pair_up(0, FFN1, "l0_ffn1")
    for i, a in zip(FFN1, _scatter_chips([pair[0][i] for i in FFN1], "grad_scatter_l0")):
        landed[0][i] = a

    small = {n: jnp.stack(small_grads[n]) for n in small_grads}
    small["g_final"] = dg_final[0]
    small_sum, loss = _unpack_small(_allreduce_small(_pack_small(small, loss_row[0, :1]), "allreduce_small"),
                                    {n: w[n].shape for n in SMALL})

    axes = [1 if n in ROW_SHARDED else 0 for n in BIG] * DEPTH
    summed = _finish_grads(pair[0] + pair[1], landed[0] + landed[1], axes, "all")
    grads = {n: jnp.stack([summed[i], summed[len(BIG) + i]]) for i, n in enumerate(BIG)}
    grads.update(small_sum)

    delta, new_m, new_v = {}, {}, {}
    for n in BIG:
        shape = w[n].shape
        flat = lambda t: t.reshape(-1, shape[-1])
        d, nm, nv = _adamw(flat(w[n]), flat(grads[n]), flat(m[n]), flat(v[n]), f"adamw_{n}")
        delta[n], new_m[n], new_v[n] = d.reshape(shape), nm.reshape(shape), nv.reshape(shape)
    zero = jnp.zeros((1,), F32)
    sd, sm, sv = _adamw(_pack_small(w, zero), _pack_small(grads, zero), _pack_small(m, zero), _pack_small(v, zero),
                        "adamw_small")
    shapes = {n: w[n].shape for n in SMALL}
    for dst, src in ((delta, sd), (new_m, sm), (new_v, sv)):
        dst.update(_unpack_small(src, shapes)[0])

    return (loss, grad_x, *[grads[n] for n in names], *[delta[n] for n in names],
            *[new_m[n] for n in names], *[new_v[n] for n in names])
```

```python
import functools

import numpy as np
import jax
import jax.numpy as jnp
from jax import lax
from jax.experimental import pallas as pl
from jax.experimental.pallas import tpu as pltpu

F32 = jnp.float32
BF16 = jnp.bfloat16

D_MODEL = 1024
DEPTH = 2
CHUNK = 64
HEAD_DIM = 64
H_SB, H_CH, H_FOX = 4, 8, 4
W_SB, W_CH, W_FOX = H_SB * HEAD_DIM, H_CH * HEAD_DIM, H_FOX * HEAD_DIM
LEFT_CHUNKS = 8
MAX_REL = 128
N_REL = 2 * MAX_REL + 1
D_FF = 2816
QKV_WIDTH = 3 * (W_SB + W_CH + W_FOX)
GATE_WIDTH = 3 * D_MODEL
IN_WIDTH = QKV_WIDTH + H_FOX + GATE_WIDTH
F_PAD = 128
P_WIDTH = QKV_WIDTH + GATE_WIDTH + F_PAD
RMS_EPS = 1e-6
NEG = -1e30
SCALE = HEAD_DIM ** -0.5
QB = 256
BAND = (LEFT_CHUNKS + 2) * CHUNK
BAND_PAD = BAND - CHUNK

ADAM_LR, ADAM_B1, ADAM_B2, ADAM_EPS, ADAM_WD, ADAM_STEP = 0.001, 0.9, 0.999, 1e-08, 0.01, 10

N_CHIPS = 4
PACK_COLS = 1024
VMEM_BUDGET = 52 * 1024 * 1024

NT = (((1,), (1,)), ((), ()))
TN = (((0,), (0,)), ((), ()))
NN = (((1,), (0,)), ((), ()))
MESH = pl.DeviceIdType.MESH


def _pcall(body, **kw):
    return pl.pallas_call(body, **kw)


class _Carry:
    def __init__(self, srcs, bufs, new, count, plan, done=None):
        self.srcs, self.bufs, self.new, self.count, self.plan = list(srcs), list(bufs), list(new), count, plan
        self.out, self.done = None, done

    @staticmethod
    def join(parts):
        parts = [p for p in parts if p is not None]
        if not parts:
            return None

        def plan(srcs, bufs, new):
            copies, s, b, n = [], 0, 0, 0
            for p in parts:
                copies += p.plan(srcs[s:s + len(p.srcs)], bufs[b:b + len(p.bufs)], new[n:n + len(p.new)])
                s, b, n = s + len(p.srcs), b + len(p.bufs), n + len(p.new)
            return copies

        whole = _Carry(sum((p.srcs for p in parts), []), sum((p.bufs for p in parts), []),
                       sum((p.new for p in parts), []), sum(p.count for p in parts), plan)
        whole.parts = parts
        return whole

    def share_out(self):
        b, n = 0, 0
        for p in getattr(self, "parts", []):
            p.out = (self.out[0][b:b + len(p.bufs)], self.out[1][n:n + len(p.new)])
            b, n = b + len(p.bufs), n + len(p.new)
            p.share_out()
        if self.done is not None:
            self.done(self)


def _carry_of(carries, key):
    return _Carry.join([make() for make in carries.get(key, [])])


def _pcall_carrying(body, carry, **kw):
    if carry is None:
        return _pcall(body, **kw)
    in_specs = list(kw.pop("in_specs"))
    out_specs, out_shape = kw.pop("out_specs"), kw.pop("out_shape")
    single = not isinstance(out_shape, (list, tuple))
    out_specs = [out_specs] if single else list(out_specs)
    out_shape = [out_shape] if single else list(out_shape)
    scratch = list(kw.pop("scratch_shapes", []))
    grid = tuple(kw.get("grid", ()))
    n_in, n_out, n_scr = len(in_specs), len(out_specs), len(scratch)
    ns, nb, nn = len(carry.srcs), len(carry.bufs), len(carry.new)

    def body2(*refs):
        ins, srcs = refs[:n_in], refs[n_in:n_in + ns]
        at = n_in + ns + nb
        outs, bufs, new = refs[at:at + n_out], refs[at + n_out:at + n_out + nb], refs[at + n_out + nb:at + n_out + nb + nn]
        at += n_out + nb + nn
        scr, (send_sems, recv_sems) = refs[at:at + n_scr], refs[at + n_scr:]

        def copies():
            return [_remote_copy(s, d, send_sems, recv_sems, k, to)
                    for k, (s, d, to) in enumerate(carry.plan(srcs, bufs, new))]

        def start():
            for cp in copies():
                cp.start()

        def wait():
            for cp in copies():
                cp.wait()

        if grid:
            ids = [pl.program_id(a) for a in range(len(grid))]
            pl.when(functools.reduce(jnp.logical_and, [i == 0 for i in ids]))(start)
        else:
            start()
        body(*ins, *outs, *scr)
        if grid:
            pl.when(functools.reduce(jnp.logical_and, [i == g - 1 for i, g in zip(ids, grid)]))(wait)
        else:
            wait()

    call = _pcall(
        body2, in_specs=in_specs + [HBM_SPEC] * (ns + nb), out_specs=out_specs + [HBM_SPEC] * (nb + nn),
        out_shape=out_shape + [jax.ShapeDtypeStruct(b.shape, b.dtype) for b in carry.bufs] + carry.new,
        scratch_shapes=scratch + [pltpu.SemaphoreType.DMA((carry.count,)), pltpu.SemaphoreType.DMA((carry.count,))],
        input_output_aliases={n_in + ns + j: n_out + j for j in range(nb)}, **kw)

    def apply(*args):
        res = call(*args, *carry.srcs, *carry.bufs)
        carry.out = (list(res[n_out:n_out + nb]), list(res[n_out + nb:]))
        carry.share_out()
        return res[0] if single else list(res[:n_out])

    return apply


def _pick(n, cap, mult=128):
    best = None
    d = mult
    while d <= min(n, cap):
        if n % d == 0:
            best = d
        d += mult
    return n if best is None else best


def _nbytes(shape, dtype):
    return int(np.prod(shape)) * jnp.dtype(dtype).itemsize


def _mm(a, b, *, mode, name, out_dtype=F32, alpha=1.0, bias=None, res=None, tm_cap=1024, tn_cap=512,
        tk_cap=2816, gathered=None, out_split=None, carry=None):
    if mode == "tn":
        K, M = a.shape
    else:
        M, K = a.shape
    dn = {"nn": NN, "nt": NT, "tn": TN}[mode]
    b_rows = None
    if gathered is None:
        N = b.shape[0] if mode == "nt" else b.shape[1]
        tn = {None: _pick(N, tn_cap), "col": N // N_CHIPS, "row": N // 2}[out_split]
        if out_split == "col":
            tm_cap = min(tm_cap, M // 2)
        tk = K if K <= tk_cap else _pick(K, tk_cap)
        b_spec = (pl.BlockSpec((tn, tk), lambda i, j, k: (j, k)) if mode == "nt"
                  else pl.BlockSpec((tk, tn), lambda i, j, k: (k, j)))
    else:
        r, c = b.shape[1:]
        rows, cols = (r, N_CHIPS * c) if gathered == "col" else (N_CHIPS * r, c)
        N = rows if mode == "nt" else cols
        assert K == (cols if mode == "nt" else rows), (name, K, rows, cols)
        if gathered == "col" and mode == "nn":
            tn, tk = c, (r if r <= tk_cap else _pick(r, tk_cap))
            b_spec = pl.BlockSpec((None, tk, c), lambda i, j, k: (j, k, 0))
        elif gathered == "col":
            tn, tk = _pick(r, tn_cap), c
            b_spec = pl.BlockSpec((None, tn, c), lambda i, j, k: (k, j, 0))
        elif mode == "nn":
            tn, tk, b_rows = _pick(c, tn_cap), K, N_CHIPS
            b_spec = pl.BlockSpec((N_CHIPS, r, tn), lambda i, j, k: (0, 0, j))
        else:
            tn, tk, b_rows = 2 * r, K, 2
            b_spec = pl.BlockSpec((2, r, c), lambda i, j, k: (j, 0, 0))
    tm = _pick(M, tm_cap)
    nk = K // tk

    a_spec = (pl.BlockSpec((tk, tm), lambda i, j, k: (k, i)) if mode == "tn"
              else pl.BlockSpec((tm, tk), lambda i, j, k: (i, k)))
    in_specs = [a_spec, b_spec]
    args = [a, b]
    if bias is not None:
        in_specs.append(pl.BlockSpec((1, tn), lambda i, j, k: (0, j)))
        args.append(bias)
    if res is not None:
        in_specs.append(pl.BlockSpec((tm, tn), lambda i, j, k: (i, j)))
        args.append(res)

    est = 2 * (_nbytes((tm, tk), a.dtype) + _nbytes((tk, tn), b.dtype) + _nbytes((tm, tn), out_dtype))
    est += _nbytes((tm, tn), F32) * (3 if res is not None else 1)
    assert est < VMEM_BUDGET, (name, est)

    def body(*refs):
        a_ref, b_ref = refs[0], refs[1]
        pos = 2
        bias_ref = res_ref = None
        if bias is not None:
            bias_ref = refs[pos]
            pos += 1
        if res is not None:
            res_ref = refs[pos]
            pos += 1
        o_ref = refs[pos]
        acc_ref = refs[pos + 1] if nk > 1 else None

        bv = b_ref[...]
        if b_rows is not None:
            bv = bv.reshape(b_rows * bv.shape[1], bv.shape[2])
        part = lax.dot_general(a_ref[...].astype(BF16), bv.astype(BF16), dn, preferred_element_type=F32)

        def finish(acc):
            if alpha != 1.0:
                acc = acc * alpha
            if bias_ref is not None:
                acc = acc + bias_ref[...]
            if res_ref is not None:
                acc = acc + res_ref[...]
            o_ref[...] = acc.astype(out_dtype)

        if nk == 1:
            finish(part)
        else:
            k = pl.program_id(2)

            @pl.when(k == 0)
            def _():
                acc_ref[...] = part

            @pl.when(k > 0)
            def _():
                acc_ref[...] += part

            @pl.when(k == nk - 1)
            def _():
                finish(acc_ref[...])

    if out_split == "col":
        per_half = M // 2 // tm
        out_spec = pl.BlockSpec((None, None, tm, tn), lambda i, j, k: (i // per_half, j, i % per_half, 0))
        out_shape = jax.ShapeDtypeStruct((2, N_CHIPS, M // 2, tn), out_dtype)
    elif out_split == "row":
        out_spec = pl.BlockSpec((None, tm, tn), lambda i, j, k: (j, i, 0))
        out_shape = jax.ShapeDtypeStruct((2, M, tn), out_dtype)
    else:
        out_spec = pl.BlockSpec((tm, tn), lambda i, j, k: (i, j))
        out_shape = jax.ShapeDtypeStruct((M, N), out_dtype)
    return _pcall_carrying(
        body, carry, name=name, grid=(M // tm, N // tn, nk), in_specs=in_specs, out_specs=out_spec,
        out_shape=out_shape,
        scratch_shapes=[pltpu.VMEM((tm, tn), F32)] if nk > 1 else [],
        compiler_params=pltpu.CompilerParams(dimension_semantics=("parallel", "parallel", "arbitrary")),
    )(*args)


def _rmsnorm_fwd(x, g, name):
    T, Dm = x.shape
    tm = _pick(T, 256, 8)

    def body(x_ref, g_ref, h_ref):
        xv = x_ref[...]
        rstd = lax.rsqrt(jnp.mean(xv * xv, axis=-1, keepdims=True) + RMS_EPS)
        h_ref[...] = (xv * rstd * g_ref[...]).astype(BF16)

    return _pcall(
        body, name=name, grid=(T // tm,),
        in_specs=[pl.BlockSpec((tm, Dm), lambda i: (i, 0)), pl.BlockSpec((1, Dm), lambda i: (0, 0))],
        out_specs=pl.BlockSpec((tm, Dm), lambda i: (i, 0)),
        out_shape=jax.ShapeDtypeStruct((T, Dm), BF16),
    )(x, g)


def _rmsnorm_bwd(x, g, dh, dres, name):
    T, Dm = x.shape
    tm = _pick(T, 256, 8)

    def body(x_ref, g_ref, dh_ref, dres_ref, dx_ref, dg_ref):
        xv = x_ref[...]
        rstd = lax.rsqrt(jnp.mean(xv * xv, axis=-1, keepdims=True) + RMS_EPS)
        xhat = xv * rstd
        dhv = dh_ref[...]
        dyg = dhv * g_ref[...]
        dx_ref[...] = dres_ref[...] + rstd * (dyg - xhat * jnp.mean(dyg * xhat, axis=-1, keepdims=True))
        part = jnp.sum(dhv * xhat, axis=0, keepdims=True)

        @pl.when(pl.program_id(0) == 0)
        def _():
            dg_ref[...] = part

        @pl.when(pl.program_id(0) > 0)
        def _():
            dg_ref[...] += part

    row = pl.BlockSpec((tm, Dm), lambda i: (i, 0))
    vec = pl.BlockSpec((1, Dm), lambda i: (0, 0))
    return _pcall(
        body, name=name, grid=(T // tm,), in_specs=[row, vec, row, row], out_specs=[row, vec],
        out_shape=[jax.ShapeDtypeStruct((T, Dm), F32), jax.ShapeDtypeStruct((1, Dm), F32)],
        compiler_params=pltpu.CompilerParams(dimension_semantics=("arbitrary",)),
    )(x, g, dh, dres)


def _final_loss(x, g, tgt, name):
    T, Dm = x.shape
    tm = _pick(T, 256, 8)

    def body(x_ref, g_ref, t_ref, dx_ref, dg_ref, loss_ref):
        xv = x_ref[...]
        gv = g_ref[...]
        rstd = lax.rsqrt(jnp.mean(xv * xv, axis=-1, keepdims=True) + RMS_EPS)
        xhat = xv * rstd
        err = xhat * gv - t_ref[...]
        part_loss = 0.5 * jnp.sum(jnp.mean(err * err, axis=-1, keepdims=True), axis=0, keepdims=True)
        dy = err * (1.0 / Dm)
        dyg = dy * gv
        dx_ref[...] = rstd * (dyg - xhat * jnp.mean(dyg * xhat, axis=-1, keepdims=True))
        part_g = jnp.sum(dy * xhat, axis=0, keepdims=True)
        part_l = jnp.broadcast_to(part_loss, (1, 128))

        @pl.when(pl.program_id(0) == 0)
        def _():
            dg_ref[...] = part_g
            loss_ref[...] = part_l

        @pl.when(pl.program_id(0) > 0)
        def _():
            dg_ref[...] += part_g
            loss_ref[...] += part_l

    row = pl.BlockSpec((tm, Dm), lambda i: (i, 0))
    vec = pl.BlockSpec((1, Dm), lambda i: (0, 0))
    return _pcall(
        body, name=name, grid=(T // tm,), in_specs=[row, vec, row],
        out_specs=[row, vec, pl.BlockSpec((1, 128), lambda i: (0, 0))],
        out_shape=[jax.ShapeDtypeStruct((T, Dm), F32), jax.ShapeDtypeStruct((1, Dm), F32),
                   jax.ShapeDtypeStruct((1, 128), F32)],
        compiler_params=pltpu.CompilerParams(dimension_semantics=("arbitrary",)),
    )(x, g, tgt)


def _sigmoid(z):
    return 1.0 / (1.0 + jnp.exp(-z))


def _swiglu_fwd(gu, name):
    T = gu.shape[0]
    tm = _pick(T, 256, 16)

    def body(gu_ref, a_ref):
        gv = gu_ref[:, :D_FF].astype(F32)
        uv = gu_ref[:, D_FF:].astype(F32)
        a_ref[...] = (gv * _sigmoid(gv) * uv).astype(BF16)

    return _pcall(
        body, name=name, grid=(T // tm,), in_specs=[pl.BlockSpec((tm, 2 * D_FF), lambda i: (i, 0))],
        out_specs=pl.BlockSpec((tm, D_FF), lambda i: (i, 0)),
        out_shape=jax.ShapeDtypeStruct((T, D_FF), BF16),
    )(gu)


def _swiglu_bwd(gu, da, name):
    T = gu.shape[0]
    tm = _pick(T, 256, 16)

    def body(gu_ref, da_ref, d_ref):
        gv = gu_ref[:, :D_FF].astype(F32)
        uv = gu_ref[:, D_FF:].astype(F32)
        dav = da_ref[...].astype(F32)
        sg = _sigmoid(gv)
        d_ref[:, :D_FF] = (dav * uv * (sg + gv * sg * (1.0 - sg))).astype(BF16)
        d_ref[:, D_FF:] = (dav * gv * sg).astype(BF16)

    return _pcall(
        body, name=name, grid=(T // tm,),
        in_specs=[pl.BlockSpec((tm, 2 * D_FF), lambda i: (i, 0)), pl.BlockSpec((tm, D_FF), lambda i: (i, 0))],
        out_specs=pl.BlockSpec((tm, 2 * D_FF), lambda i: (i, 0)),
        out_shape=jax.ShapeDtypeStruct((T, 2 * D_FF), BF16),
    )(gu, da)


GATE_BLOCK0 = QKV_WIDTH // D_MODEL


def _gate_fwd(p, ya, yb, yc, name):
    T = p.shape[0]
    tm = _pick(T, 256, 8)

    def body(ga_ref, gb_ref, gc_ref, ya_ref, yb_ref, yc_ref, o_ref):
        o_ref[...] = (_sigmoid(ga_ref[...]) * ya_ref[...] + _sigmoid(gb_ref[...]) * yb_ref[...]
                      + _sigmoid(gc_ref[...]) * yc_ref[...]).astype(BF16)

    gate = [pl.BlockSpec((tm, D_MODEL), functools.partial(lambda i, kk: (i, GATE_BLOCK0 + kk), kk=kk))
            for kk in range(3)]
    row = pl.BlockSpec((tm, D_MODEL), lambda i: (i, 0))
    return _pcall(
        body, name=name, grid=(T // tm,), in_specs=gate + [row, row, row], out_specs=row,
        out_shape=jax.ShapeDtypeStruct((T, D_MODEL), BF16),
    )(p, p, p, ya, yb, yc)


def _gate_bwd(p, ya, yb, yc, dm, name):
    T = p.shape[0]
    tm = _pick(T, 256, 16)

    def body(ga_ref, gb_ref, gc_ref, ya_ref, yb_ref, yc_ref, dm_ref, da_ref, db_ref, dc_ref, dg_ref):
        dmv = dm_ref[...]
        for kk, (g_ref, y_ref, d_ref) in enumerate(((ga_ref, ya_ref, da_ref), (gb_ref, yb_ref, db_ref),
                                                    (gc_ref, yc_ref, dc_ref))):
            s = _sigmoid(g_ref[...])
            d_ref[...] = (s * dmv).astype(BF16)
            dg_ref[:, kk * D_MODEL:(kk + 1) * D_MODEL] = (dmv * y_ref[...] * s * (1.0 - s)).astype(BF16)

    gate = [pl.BlockSpec((tm, D_MODEL), functools.partial(lambda i, kk: (i, GATE_BLOCK0 + kk), kk=kk))
            for kk in range(3)]
    row = pl.BlockSpec((tm, D_MODEL), lambda i: (i, 0))
    return _pcall(
        body, name=name, grid=(T // tm,), in_specs=gate + [row, row, row, row],
        out_specs=[row, row, row, pl.BlockSpec((tm, GATE_WIDTH), lambda i: (i, 0))],
        out_shape=[jax.ShapeDtypeStruct((T, D_MODEL), BF16)] * 3 + [jax.ShapeDtypeStruct((T, GATE_WIDTH), BF16)],
    )(p, p, p, ya, yb, yc, dm)


def _colsum(a, name):
    T, N = a.shape
    tm = _pick(T, 256, 16)

    def body(a_ref, o_ref):
        part = jnp.sum(a_ref[...].astype(F32), axis=0, keepdims=True)

        @pl.when(pl.program_id(0) == 0)
        def _():
            o_ref[...] = part

        @pl.when(pl.program_id(0) > 0)
        def _():
            o_ref[...] += part

    return _pcall(
        body, name=name, grid=(T // tm,), in_specs=[pl.BlockSpec((tm, N), lambda i: (i, 0))],
        out_specs=pl.BlockSpec((1, N), lambda i: (0, 0)), out_shape=jax.ShapeDtypeStruct((1, N), F32),
        compiler_params=pltpu.CompilerParams(dimension_semantics=("arbitrary",)),
    )(a)


def _adamw(w, g, m, v, name):
    R, C = w.shape
    tr = 256 if R % 256 == 0 else R
    c1 = 1.0 / (1.0 - ADAM_B1 ** ADAM_STEP)
    c2 = 1.0 / (1.0 - ADAM_B2 ** ADAM_STEP)

    def body(w_ref, g_ref, m_ref, v_ref, d_ref, nm_ref, nv_ref):
        gv = g_ref[...]
        mn = ADAM_B1 * m_ref[...] + (1.0 - ADAM_B1) * gv
        vn = ADAM_B2 * v_ref[...] + (1.0 - ADAM_B2) * (gv * gv)
        nm_ref[...] = mn
        nv_ref[...] = vn
        d_ref[...] = -ADAM_LR * ((mn * c1) / (jnp.sqrt(vn * c2) + ADAM_EPS) + ADAM_WD * w_ref[...])

    spec = pl.BlockSpec((tr, C), lambda i: (i, 0))
    return _pcall(
        body, name=name, grid=(R // tr,), in_specs=[spec] * 4, out_specs=[spec] * 3,
        out_shape=[jax.ShapeDtypeStruct((R, C), F32)] * 3,
    )(w, g, m, v)


def _log_sigmoid(z):
    return jnp.minimum(z, 0.0) - jnp.log(1.0 + jnp.exp(-jnp.abs(z)))


def _dot3(x, m01):
    x1 = x.astype(BF16)
    r1 = x - x1.astype(F32)
    x2 = r1.astype(BF16)
    x3 = (r1 - x2.astype(F32)).astype(BF16)
    n = x.shape[0]
    if n % 16:
        return (jnp.dot(x1, m01, preferred_element_type=F32) + jnp.dot(x2, m01, preferred_element_type=F32)
                + jnp.dot(x3, m01, preferred_element_type=F32))
    y = jnp.dot(jnp.concatenate([x1, x2, x3], axis=0), m01, preferred_element_type=F32)
    return y[:n] + y[n:2 * n] + y[2 * n:]


def _dot_nt(a, b):
    return lax.dot_general(a, b, NT, preferred_element_type=F32)


def _dot_tn(a, b):
    return lax.dot_general(a, b, TN, preferred_element_type=F32)


def _iota2(shape):
    return lax.broadcasted_iota(jnp.int32, shape, 0), lax.broadcasted_iota(jnp.int32, shape, 1)


HEADS_PER_STEP = 4
HEADS = range(HEADS_PER_STEP)


CHUNK_HEADS_PER_STEP = 2
CHUNK_HEADS = range(CHUNK_HEADS_PER_STEP)


def _head_spec(T, rows=None, width=HEAD_DIM, heads=HEADS_PER_STEP):
    return pl.BlockSpec((heads, T if rows is None else rows, width), lambda g: (g, 0, 0))


def _sb_weights(qb, kb, t0, s0, racc, m_gt, row, col):
    z = _dot_nt(qb, kb) * SCALE
    strict = (s0 + col) < (t0 + row)
    lb = _log_sigmoid(z)
    lf = jnp.where(strict, lb - z, 0.0)
    between = _dot3(lf, m_gt) + racc
    w = jnp.where(strict, jnp.exp(lb + between), 0.0)
    return strict, lb, lf, w


def _sb_fwd(q, k, v, name, carry=None):
    H, T, _ = q.shape
    nb = T // QB

    def body(q_ref, k_ref, v_ref, o_ref):
        row, col = _iota2((QB, QB))
        m_gt = (row > col).astype(BF16)

        def qblock(i, _):
            t0 = pl.multiple_of(i * QB, QB)
            qbs = [q_ref[h, pl.ds(t0, QB), :].astype(BF16) for h in HEADS]

            def kblock(jj, carry):
                s0 = pl.multiple_of((i - jj) * QB, QB)
                out = []
                for h in HEADS:
                    acc, racc = carry[h]
                    kb = k_ref[h, pl.ds(s0, QB), :].astype(BF16)
                    vb = v_ref[h, pl.ds(s0, QB), :].astype(BF16)
                    _, _, lf, w = _sb_weights(qbs[h], kb, t0, s0, racc, m_gt, row, col)
                    acc = acc + jnp.dot(w.astype(BF16), vb, preferred_element_type=F32)
                    out.append((acc, racc + jnp.sum(lf, axis=1, keepdims=True)))
                return tuple(out)

            res = lax.fori_loop(0, i + 1, kblock,
                                tuple((jnp.zeros((QB, HEAD_DIM), F32), jnp.zeros((QB, 1), F32)) for _ in HEADS))
            for h in HEADS:
                o_ref[h, pl.ds(t0, QB), :] = res[h][0].astype(BF16)
            return 0

        lax.fori_loop(0, nb, qblock, 0)

    spec = _head_spec(T)
    return _pcall_carrying(body, carry, name=name, grid=(H // HEADS_PER_STEP,), in_specs=[spec] * 3, out_specs=spec,
                           out_shape=jax.ShapeDtypeStruct((H, T, HEAD_DIM), BF16))(q, k, v)


def _sb_bwd(q, k, v, do, name, carry=None):
    H, T, _ = q.shape
    nb = T // QB

    def body(q_ref, k_ref, v_ref, do_ref, dq_ref, dk_out, dv_out, racc_ref, dk_ref, dv_ref):
        row, col = _iota2((QB, QB))
        m_gt = (row > col).astype(BF16)
        m_lt = (row < col).astype(BF16)
        dk_ref[...] = jnp.zeros_like(dk_ref)
        dv_ref[...] = jnp.zeros_like(dv_ref)

        def qblock(i, _):
            t0 = pl.multiple_of(i * QB, QB)
            qbs = [q_ref[h, pl.ds(t0, QB), :].astype(BF16) for h in HEADS]
            dobs = [do_ref[h, pl.ds(t0, QB), :].astype(BF16) for h in HEADS]

            def suffix(jj, raccs):
                j = i - jj
                s0 = pl.multiple_of(j * QB, QB)
                out = []
                for h in HEADS:
                    kb = k_ref[h, pl.ds(s0, QB), :].astype(BF16)
                    z = _dot_nt(qbs[h], kb) * SCALE
                    lf = jnp.where((s0 + col) < (t0 + row), _log_sigmoid(z) - z, 0.0)
                    racc_ref[h, j] = raccs[h]
                    out.append(raccs[h] + jnp.sum(lf, axis=1, keepdims=True))
                return tuple(out)

            lax.fori_loop(0, i + 1, suffix, tuple(jnp.zeros((QB, 1), F32) for _ in HEADS))

            def kblock(j, carry):
                s0 = pl.multiple_of(j * QB, QB)
                out = []
                for h in HEADS:
                    dq, cacc = carry[h]
                    kb = k_ref[h, pl.ds(s0, QB), :].astype(BF16)
                    vb = v_ref[h, pl.ds(s0, QB), :].astype(BF16)
                    strict, lb, _, w = _sb_weights(qbs[h], kb, t0, s0, racc_ref[h, j], m_gt, row, col)
                    e = _dot_nt(dobs[h], vb) * w
                    c_left = _dot3(e, m_lt) + cacc
                    sig = jnp.exp(lb)
                    dz = jnp.where(strict, e * (1.0 - sig) - c_left * sig, 0.0).astype(BF16)
                    dq = dq + jnp.dot(dz, kb, preferred_element_type=F32)
                    dk_ref[h, pl.ds(s0, QB), :] += _dot_tn(dz, qbs[h]) * SCALE
                    dv_ref[h, pl.ds(s0, QB), :] += _dot_tn(w.astype(BF16), dobs[h])
                    out.append((dq, cacc + jnp.sum(e, axis=1, keepdims=True)))
                return tuple(out)

            res = lax.fori_loop(0, i + 1, kblock,
                                tuple((jnp.zeros((QB, HEAD_DIM), F32), jnp.zeros((QB, 1), F32)) for _ in HEADS))
            for h in HEADS:
                dq_ref[h, pl.ds(t0, QB), :] = (res[h][0] * SCALE).astype(BF16)
            return 0

        lax.fori_loop(0, nb, qblock, 0)
        dk_out[...] = dk_ref[...].astype(BF16)
        dv_out[...] = dv_ref[...].astype(BF16)

    spec = _head_spec(T)
    acc = pltpu.VMEM((HEADS_PER_STEP, T, HEAD_DIM), F32)
    return _pcall_carrying(body, carry, name=name, grid=(H // HEADS_PER_STEP,), in_specs=[spec] * 4,
                           out_specs=[spec] * 3, out_shape=[jax.ShapeDtypeStruct((H, T, HEAD_DIM), BF16)] * 3,
                           scratch_shapes=[pltpu.VMEM((HEADS_PER_STEP, nb, QB, 1), F32), acc, acc])(q, k, v, do)


def _fox_logits(qb, kb, fq, fk, t0, s0, row, col):
    z = _dot_nt(qb, kb) * SCALE + fq - fk
    return jnp.where((s0 + col) <= (t0 + row), z, NEG)


def _fox_specs(T):
    return _head_spec(T, width=1), _head_spec(T, rows=T // QB, width=QB)


def _fox_fwd(q, k, v, fq, fk, name, carry=None):
    H, T, _ = q.shape
    nb = T // QB

    def body(q_ref, k_ref, v_ref, fq_ref, fk_ref, o_ref, lse_ref):
        row, col = _iota2((QB, QB))

        def qblock(i, _):
            t0 = pl.multiple_of(i * QB, QB)
            qbs = [q_ref[h, pl.ds(t0, QB), :].astype(BF16) for h in HEADS]
            fqs = [fq_ref[h, pl.ds(t0, QB), :] for h in HEADS]

            def kblock(j, carry):
                s0 = pl.multiple_of(j * QB, QB)
                out = []
                for h in HEADS:
                    acc, m, l = carry[h]
                    kb = k_ref[h, pl.ds(s0, QB), :].astype(BF16)
                    vb = v_ref[h, pl.ds(s0, QB), :].astype(BF16)
                    z = _fox_logits(qbs[h], kb, fqs[h], fk_ref[h, pl.ds(j, 1), :], t0, s0, row, col)
                    m_new = jnp.maximum(m, jnp.max(z, axis=1, keepdims=True))
                    a = jnp.exp(m - m_new)
                    p = jnp.exp(z - m_new)
                    acc = a * acc + jnp.dot(p.astype(BF16), vb, preferred_element_type=F32)
                    out.append((acc, m_new, a * l + jnp.sum(p, axis=1, keepdims=True)))
                return tuple(out)

            res = lax.fori_loop(0, i + 1, kblock,
                                tuple((jnp.zeros((QB, HEAD_DIM), F32), jnp.full((QB, 1), NEG, F32),
                                       jnp.zeros((QB, 1), F32)) for _ in HEADS))
            for h in HEADS:
                acc, m, l = res[h]
                o_ref[h, pl.ds(t0, QB), :] = (acc / l).astype(BF16)
                lse_ref[h, pl.ds(t0, QB), :] = m + jnp.log(l)
            return 0

        lax.fori_loop(0, nb, qblock, 0)

    spec = _head_spec(T)
    fq_spec, fk_spec = _fox_specs(T)
    return _pcall_carrying(
        body, carry, name=name, grid=(H // HEADS_PER_STEP,), in_specs=[spec] * 3 + [fq_spec, fk_spec],
        out_specs=[spec, fq_spec],
        out_shape=[jax.ShapeDtypeStruct((H, T, HEAD_DIM), BF16), jax.ShapeDtypeStruct((H, T, 1), F32)],
    )(q, k, v, fq, fk)


def _fox_bwd(q, k, v, fq, fk, lse, do, name, carry=None):
    H, T, _ = q.shape
    nb = T // QB

    def body(q_ref, k_ref, v_ref, fq_ref, fk_ref, lse_ref, do_ref, dq_ref, dk_out, dv_out, dfk_ref, dk_ref, dv_ref):
        row, col = _iota2((QB, QB))
        dk_ref[...] = jnp.zeros_like(dk_ref)
        dv_ref[...] = jnp.zeros_like(dv_ref)
        dfk_ref[...] = jnp.zeros_like(dfk_ref)

        def qblock(i, _):
            t0 = pl.multiple_of(i * QB, QB)
            qbs = [q_ref[h, pl.ds(t0, QB), :].astype(BF16) for h in HEADS]
            fqs = [fq_ref[h, pl.ds(t0, QB), :] for h in HEADS]
            lses = [lse_ref[h, pl.ds(t0, QB), :] for h in HEADS]
            dobs = [do_ref[h, pl.ds(t0, QB), :].astype(BF16) for h in HEADS]

            def probs(h, j):
                s0 = pl.multiple_of(j * QB, QB)
                kb = k_ref[h, pl.ds(s0, QB), :].astype(BF16)
                vb = v_ref[h, pl.ds(s0, QB), :].astype(BF16)
                z = _fox_logits(qbs[h], kb, fqs[h], fk_ref[h, pl.ds(j, 1), :], t0, s0, row, col)
                return s0, kb, jnp.exp(z - lses[h]), _dot_nt(dobs[h], vb)

            def row_dot(j, deltas):
                out = []
                for h in HEADS:
                    _, _, p, dp = probs(h, j)
                    out.append(deltas[h] + jnp.sum(p * dp, axis=1, keepdims=True))
                return tuple(out)

            deltas = lax.fori_loop(0, i + 1, row_dot, tuple(jnp.zeros((QB, 1), F32) for _ in HEADS))

            def kblock(j, dqs):
                out = []
                for h in HEADS:
                    s0, kb, p, dp = probs(h, j)
                    dz = p * (dp - deltas[h])
                    dzb = dz.astype(BF16)
                    dk_ref[h, pl.ds(s0, QB), :] += _dot_tn(dzb, qbs[h]) * SCALE
                    dv_ref[h, pl.ds(s0, QB), :] += _dot_tn(p.astype(BF16), dobs[h])
                    dfk_ref[h, pl.ds(j, 1), :] += -jnp.sum(dz, axis=0, keepdims=True)
                    out.append(dqs[h] + jnp.dot(dzb, kb, preferred_element_type=F32))
                return tuple(out)

            dqs = lax.fori_loop(0, i + 1, kblock, tuple(jnp.zeros((QB, HEAD_DIM), F32) for _ in HEADS))
            for h in HEADS:
                dq_ref[h, pl.ds(t0, QB), :] = (dqs[h] * SCALE).astype(BF16)
            return 0

        lax.fori_loop(0, nb, qblock, 0)
        dk_out[...] = dk_ref[...].astype(BF16)
        dv_out[...] = dv_ref[...].astype(BF16)

    spec = _head_spec(T)
    fq_spec, fk_spec = _fox_specs(T)
    acc = pltpu.VMEM((HEADS_PER_STEP, T, HEAD_DIM), F32)
    return _pcall_carrying(body, carry, name=name, grid=(H // HEADS_PER_STEP,),
                           in_specs=[spec] * 3 + [fq_spec, fk_spec, fq_spec, spec],
                           out_specs=[spec, spec, spec, fk_spec],
                           out_shape=[jax.ShapeDtypeStruct((H, T, HEAD_DIM), BF16)] * 3
                           + [jax.ShapeDtypeStruct((H, nb, QB), F32)],
                           scratch_shapes=[acc, acc])(q, k, v, fq, fk, lse, do)


def _forget_cumsum(flog, name):
    R, T = flog.shape
    nb = T // QB

    def body(x_ref, o_ref):
        row, col = _iota2((QB, QB))
        m_le = (row <= col).astype(BF16)
        carry = jnp.zeros((R, 1), F32)
        for b in range(nb):
            lf = _log_sigmoid(x_ref[:, b * QB:(b + 1) * QB])
            o_ref[:, b * QB:(b + 1) * QB] = _dot3(lf, m_le) + carry
            carry = carry + jnp.sum(lf, axis=1, keepdims=True)

    spec = pl.BlockSpec((R, T), lambda: (0, 0))
    return _pcall(body, name=name, in_specs=[spec], out_specs=spec,
                  out_shape=jax.ShapeDtypeStruct((R, T), F32))(flog)


def _forget_cumsum_bwd(flog, df, name):
    R, T = flog.shape
    nb = T // QB

    def body(x_ref, df_ref, o_ref):
        row, col = _iota2((QB, QB))
        m_ge = (row >= col).astype(BF16)
        carry = jnp.zeros((R, 1), F32)
        for b in reversed(range(nb)):
            dfb = df_ref[:, b * QB:(b + 1) * QB]
            dlog = _dot3(dfb, m_ge) + carry
            o_ref[:, b * QB:(b + 1) * QB] = dlog * _sigmoid(-x_ref[:, b * QB:(b + 1) * QB])
            carry = carry + jnp.sum(dfb, axis=1, keepdims=True)

    spec = pl.BlockSpec((R, T), lambda: (0, 0))
    return _pcall(body, name=name, in_specs=[spec, spec], out_specs=spec,
                  out_shape=jax.ShapeDtypeStruct((R, T), F32))(flog, df)


def _chunk_probs(qc, kb, bias, c, col):
    z = _dot_nt(qc, kb) * SCALE + bias
    z = jnp.where((c - (LEFT_CHUNKS + 1)) * CHUNK + col >= jnp.maximum(0, (c - LEFT_CHUNKS) * CHUNK), z, NEG)
    p = jnp.exp(z - jnp.max(z, axis=1, keepdims=True))
    return p, jnp.sum(p, axis=1, keepdims=True)


def _chunk_specs(T, n=CHUNK_HEADS_PER_STEP):
    return (_head_spec(T, heads=n), _head_spec(T, rows=T + BAND_PAD, heads=n),
            _head_spec(T, rows=CHUNK, width=BAND, heads=n))


def _chunk_fwd(q, kp, vp, bias, name, carry=None):
    H, T, _ = q.shape
    nc = T // CHUNK

    def body(q_ref, kp_ref, vp_ref, bias_ref, o_ref):
        _, col = _iota2((CHUNK, BAND))

        def chunk(c, _):
            t0 = pl.multiple_of(c * CHUNK, CHUNK)
            for h in HEADS:
                qc = q_ref[h, pl.ds(t0, CHUNK), :].astype(BF16)
                kb = kp_ref[h, pl.ds(t0, BAND), :].astype(BF16)
                vb = vp_ref[h, pl.ds(t0, BAND), :].astype(BF16)
                p, l = _chunk_probs(qc, kb, bias_ref[h], c, col)
                o = jnp.dot(p.astype(BF16), vb, preferred_element_type=F32) / l
                o_ref[h, pl.ds(t0, CHUNK), :] = o.astype(BF16)
            return 0

        lax.fori_loop(0, nc, chunk, 0)

    q_spec, kp_spec, b_spec = _chunk_specs(T, HEADS_PER_STEP)
    return _pcall_carrying(body, carry, name=name, grid=(H // HEADS_PER_STEP,),
                           in_specs=[q_spec, kp_spec, kp_spec, b_spec], out_specs=q_spec,
                           out_shape=jax.ShapeDtypeStruct((H, T, HEAD_DIM), BF16))(q, kp, vp, bias)


def _chunk_bwd(q, kp, vp, bias, do, name, carry=None):
    H, T, _ = q.shape
    nc = T // CHUNK

    def body(q_ref, kp_ref, vp_ref, bias_ref, do_ref, dq_ref, dk_out, dv_out, db_ref, dkp_ref, dvp_ref):
        _, col = _iota2((CHUNK, BAND))
        dkp_ref[...] = jnp.zeros_like(dkp_ref)
        dvp_ref[...] = jnp.zeros_like(dvp_ref)
        db_ref[...] = jnp.zeros_like(db_ref)

        def chunk(c, _):
            t0 = pl.multiple_of(c * CHUNK, CHUNK)
            for h in CHUNK_HEADS:
                qc = q_ref[h, pl.ds(t0, CHUNK), :].astype(BF16)
                kb = kp_ref[h, pl.ds(t0, BAND), :].astype(BF16)
                vb = vp_ref[h, pl.ds(t0, BAND), :].astype(BF16)
                dob = do_ref[h, pl.ds(t0, CHUNK), :].astype(BF16)
                p, l = _chunk_probs(qc, kb, bias_ref[h], c, col)
                p = p / l
                dp = _dot_nt(dob, vb)
                dz = p * (dp - jnp.sum(p * dp, axis=1, keepdims=True))
                dzb = dz.astype(BF16)
                dq = jnp.dot(dzb, kb, preferred_element_type=F32) * SCALE
                dq_ref[h, pl.ds(t0, CHUNK), :] = dq.astype(BF16)
                dkp_ref[h, pl.ds(t0, BAND), :] += _dot_tn(dzb, qc) * SCALE
                dvp_ref[h, pl.ds(t0, BAND), :] += _dot_tn(p.astype(BF16), dob)
                db_ref[h] += dz
            return 0

        lax.fori_loop(0, nc, chunk, 0)
        dk_out[...] = dkp_ref[:, BAND_PAD:, :].astype(BF16)
        dv_out[...] = dvp_ref[:, BAND_PAD:, :].astype(BF16)

    q_spec, kp_spec, b_spec = _chunk_specs(T)
    acc = pltpu.VMEM((CHUNK_HEADS_PER_STEP, T + BAND_PAD, HEAD_DIM), F32)
    return _pcall_carrying(body, carry, name=name, grid=(H // CHUNK_HEADS_PER_STEP,),
                           in_specs=[q_spec, kp_spec, kp_spec, b_spec, q_spec],
                           out_specs=[q_spec, q_spec, q_spec, b_spec],
                           out_shape=[jax.ShapeDtypeStruct((H, T, HEAD_DIM), BF16)] * 3
                           + [jax.ShapeDtypeStruct((H, CHUNK, BAND), F32)],
                           scratch_shapes=[acc, acc])(q, kp, vp, bias, do)


HBM_SPEC = pl.BlockSpec(memory_space=pltpu.HBM)


def _position():
    return lax.axis_index("x"), lax.axis_index("y"), lax.axis_index("c")


def _other_chips(x, y):
    chips = [(1 - x, y), (x, 1 - y), (1 - x, 1 - y)]
    return [(chip, 2 * chip[0] + chip[1]) for chip in chips]


def _remote_copy(src, dst, send_sems, recv_sems, k, to):
    return pltpu.make_async_remote_copy(src_ref=src, dst_ref=dst, send_sem=send_sems.at[k], recv_sem=recv_sems.at[k],
                                        device_id=to, device_id_type=MESH)


def _place_own(w, chip, name):
    _, r, c = w.shape
    tr = _pick(r, 256, 16)

    def body(chip_ref, w_ref, *out_refs):
        for l, o_ref in enumerate(out_refs):
            o_ref[...] = w_ref[l].astype(BF16)

    grid_spec = pltpu.PrefetchScalarGridSpec(
        num_scalar_prefetch=1, grid=(r // tr,), in_specs=[pl.BlockSpec((DEPTH, tr, c), lambda i, ch: (0, i, 0))],
        out_specs=[pl.BlockSpec((None, tr, c), lambda i, ch: (ch[0], i, 0))] * DEPTH)
    return _pcall(body, name=name, grid_spec=grid_spec,
                  out_shape=[jax.ShapeDtypeStruct((N_CHIPS, r, c), BF16)] * DEPTH)(chip, w)


def _gather_over_ici(srcs, bufs, new):
    x, y, c = _position()
    me = 2 * x + y
    return [(b.at[me, c], b.at[me, c], (*chip, c)) for b in bufs for chip, _ in _other_chips(x, y)]


def _gather_to_sibling(srcs, bufs, new):
    x, y, c = _position()
    return [(b.at[idx, c], b.at[idx, c], (x, y, 1 - c)) for b in bufs for _, idx in _other_chips(x, y)]


def _gather_layer(bufs, name):
    n = len(bufs)

    def body(*refs):
        dst = refs[n:2 * n]
        send_sems, recv_sems = refs[2 * n:]
        x, y, c = _position()
        me = 2 * x + y
        sibling = (x, y, 1 - c)
        others = _other_chips(x, y)
        first = [_remote_copy(dst[i].at[me, c], dst[i].at[me, c], send_sems, recv_sems, 3 * i + k, (*chip, c))
                 for i in range(n) for k, (chip, _) in enumerate(others)]
        for cp in first:
            cp.start()
        passed = []
        for i in range(n):
            for k, (_, idx) in enumerate(others):
                landed = dst[i].at[idx, c]
                _remote_copy(landed, landed, send_sems, recv_sems, 3 * i + k, sibling).wait_recv()
                passed.append(_remote_copy(landed, landed, send_sems, recv_sems, 3 * (n + i) + k, sibling))
                passed[-1].start()
        for i in range(n):
            for k, (_, idx) in enumerate(others):
                from_sibling = dst[i].at[idx, 1 - c]
                _remote_copy(from_sibling, from_sibling, send_sems, recv_sems, 3 * (n + i) + k, sibling).wait_recv()
        for cp in first + passed:
            cp.wait_send()

    return _pcall(
        body, name=name, in_specs=[HBM_SPEC] * n, out_specs=[HBM_SPEC] * n,
        out_shape=[jax.ShapeDtypeStruct(b.shape, b.dtype) for b in bufs],
        scratch_shapes=[pltpu.SemaphoreType.DMA((6 * n,)), pltpu.SemaphoreType.DMA((6 * n,))],
        input_output_aliases={i: i for i in range(n)},
    )(*bufs)


def _send_other_half(parts, name):
    n = len(parts)

    def body(*refs):
        src, got = refs[:n], refs[n:2 * n]
        send_sems, recv_sems = refs[2 * n:]
        x, y, c = _position()
        copies = [_remote_copy(src[i].at[1 - c], got[i], send_sems, recv_sems, i, (x, y, 1 - c)) for i in range(n)]
        for cp in copies:
            cp.start()
        for cp in copies:
            cp.wait()

    return _pcall(
        body, name=name, in_specs=[HBM_SPEC] * n, out_specs=[HBM_SPEC] * n,
        out_shape=[jax.ShapeDtypeStruct(p.shape[1:], p.dtype) for p in parts],
        scratch_shapes=[pltpu.SemaphoreType.DMA((n,)), pltpu.SemaphoreType.DMA((n,))],
    )(*parts)


def _scatter_chips(parts, name):
    n = len(parts)

    def body(*refs):
        src, dst = refs[:n], refs[n:2 * n]
        send_sems, recv_sems = refs[2 * n:]
        x, y, c = _position()
        others = _other_chips(x, y)
        sends = [_remote_copy(src[i].at[idx], dst[i].at[k], send_sems, recv_sems, 3 * i + k, (*chip, c))
                 for i in range(n) for k, (chip, idx) in enumerate(others)]
        for cp in sends:
            cp.start()
        for cp in sends:
            cp.wait()

    return _pcall(
        body, name=name, in_specs=[HBM_SPEC] * n, out_specs=[HBM_SPEC] * n,
        out_shape=[jax.ShapeDtypeStruct((3,) + p.shape[1:], p.dtype) for p in parts],
        scratch_shapes=[pltpu.SemaphoreType.DMA((3 * n,)), pltpu.SemaphoreType.DMA((3 * n,))],
    )(*parts)


def _swap_with_sibling(arrs, name):
    n = len(arrs)

    def body(*refs):
        src, dst = refs[:n], refs[n:2 * n]
        send_sems, recv_sems = refs[2 * n:]
        x, y, c = _position()
        copies = [_remote_copy(src[i], dst[i], send_sems, recv_sems, i, (x, y, 1 - c)) for i in range(n)]
        for cp in copies:
            cp.start()
        for cp in copies:
            cp.wait()

    return _pcall(
        body, name=name, in_specs=[HBM_SPEC] * n, out_specs=[HBM_SPEC] * n,
        out_shape=[jax.ShapeDtypeStruct(a.shape, a.dtype) for a in arrs],
        scratch_shapes=[pltpu.SemaphoreType.DMA((n,)), pltpu.SemaphoreType.DMA((n,))],
    )(*arrs)


def _allreduce_small(v, name):
    R, C = v.shape

    def body(v_ref, o_ref, slots, send_sems, recv_sems):
        x, y, c = _position()
        me = 4 * x + 2 * y + c
        slots[0] = v_ref[...]
        sends = []
        for r in range(1, 8):
            fx, fy, fc = (r >> 2) & 1, (r >> 1) & 1, r & 1
            to = (x ^ fx, y ^ fy, c ^ fc)
            cp = pltpu.make_async_remote_copy(src_ref=v_ref, dst_ref=slots.at[r], send_sem=send_sems.at[r - 1],
                                              recv_sem=recv_sems.at[r - 1], device_id=to, device_id_type=MESH)
            cp.start()
            sends.append(cp)
        for cp in sends:
            cp.wait()
        acc = slots[me]
        for d in range(1, 8):
            acc = acc + slots[d ^ me]
        o_ref[...] = acc

    vmem = pl.BlockSpec(memory_space=pltpu.VMEM)
    return _pcall(
        body, name=name, in_specs=[vmem], out_specs=vmem, out_shape=jax.ShapeDtypeStruct((R, C), F32),
        scratch_shapes=[pltpu.VMEM((8, R, C), F32), pltpu.SemaphoreType.DMA((7,)), pltpu.SemaphoreType.DMA((7,))],
    )(v)


def _add_pair(which, both, got, name):
    _, N, R, C = both.shape
    tr = _pick(R, 512, 16)

    def body(which_ref, a_ref, b_ref, o_ref):
        o_ref[...] = (a_ref[...].astype(F32) + b_ref[...].astype(F32)).astype(BF16)

    spec = pl.BlockSpec((None, tr, C), lambda n, i, w: (n, i, 0))
    grid_spec = pltpu.PrefetchScalarGridSpec(
        num_scalar_prefetch=1, grid=(N, R // tr),
        in_specs=[pl.BlockSpec((None, None, tr, C), lambda n, i, w: (w[0], n, i, 0)), spec], out_specs=spec)
    return _pcall(body, name=name, grid_spec=grid_spec,
                  out_shape=jax.ShapeDtypeStruct((N, R, C), BF16))(which, both, got)


def _sum_chips(which, own, others, name):
    N, R, C = others.shape
    tr = _pick(R, 512, 16)

    def body(which_ref, own_ref, p_ref, o_ref):
        acc = own_ref[...].astype(F32)
        for n in range(N):
            acc = acc + p_ref[n].astype(F32)
        o_ref[...] = acc

    grid_spec = pltpu.PrefetchScalarGridSpec(
        num_scalar_prefetch=1, grid=(R // tr,),
        in_specs=[pl.BlockSpec((None, tr, C), lambda i, w: (w[0], i, 0)), pl.BlockSpec((N, tr, C), lambda i, w: (0, i, 0))],
        out_specs=pl.BlockSpec((tr, C), lambda i, w: (i, 0)))
    return _pcall(body, name=name, grid_spec=grid_spec,
                  out_shape=jax.ShapeDtypeStruct((R, C), F32))(which, own, others)


BIG = ("w_ffn1_in", "w_ffn1_out", "w_in", "w_br_sb", "w_br_ch", "w_br_fox", "w_out", "w_ffn2_in", "w_ffn2_out")
ROW_SHARDED = ("w_ffn1_out", "w_out", "w_ffn2_out")
SMALL = ("g_ffn1", "g_mix", "b_in", "rel_bias", "g_ffn2", "g_final")
SHARD_SHAPES = {
    "w_ffn1_in": (D_MODEL, 2 * D_FF // N_CHIPS), "w_ffn1_out": (D_FF // N_CHIPS, D_MODEL),
    "w_in": (D_MODEL, IN_WIDTH // N_CHIPS), "w_br_sb": (W_SB, D_MODEL // N_CHIPS),
    "w_br_ch": (W_CH, D_MODEL // N_CHIPS), "w_br_fox": (W_FOX, D_MODEL // N_CHIPS),
    "w_out": (D_MODEL // N_CHIPS, D_MODEL), "w_ffn2_in": (D_MODEL, 2 * D_FF // N_CHIPS),
    "w_ffn2_out": (D_FF // N_CHIPS, D_MODEL),
}


def _pair_sums(split, tag):
    core = lax.axis_index("c").astype(jnp.int32)[None]
    got = _send_other_half(split, f"grad_split_{tag}")
    return [_add_pair(core, a, b, f"grad_pair_sum_{tag}_{i}") for i, (a, b) in enumerate(zip(split, got))]


def _scatter_plan(srcs, bufs, new):
    x, y, c = _position()
    return [(s.at[idx], d.at[k], (*chip, c)) for s, d in zip(srcs, new) for k, (chip, idx) in enumerate(_other_chips(x, y))]


def _scatter_carry(pair):
    landing = [jax.ShapeDtypeStruct((3,) + p.shape[1:], p.dtype) for p in pair]
    return _Carry(pair, [], landing, 3 * len(pair), _scatter_plan)


def _finish_grads(pair, landed, axes, tag):
    chip = (2 * lax.axis_index("x") + lax.axis_index("y")).astype(jnp.int32)[None]
    mine = [_sum_chips(chip, p, q, f"grad_chip_sum_{tag}_{i}") for i, (p, q) in enumerate(zip(pair, landed))]
    theirs = _swap_with_sibling(mine, f"grad_share_{tag}")
    first = lax.axis_index("c") == 0
    return [jnp.concatenate([jnp.where(first, a, b), jnp.where(first, b, a)], axis=ax)
            for a, b, ax in zip(mine, theirs, axes)]


SMALL_SIZES = {"g_ffn1": DEPTH * D_MODEL, "g_mix": DEPTH * D_MODEL, "b_in": DEPTH * IN_WIDTH,
               "rel_bias": DEPTH * N_REL * H_CH, "g_ffn2": DEPTH * D_MODEL, "g_final": D_MODEL}
SMALL_TOTAL = sum(SMALL_SIZES.values()) + 1
SMALL_ROWS = -(-SMALL_TOTAL // (8 * 128)) * 8


def _pack_small(vals, extra):
    flat = [vals[n].reshape(-1) for n in SMALL] + [extra.reshape(-1)]
    flat.append(jnp.zeros((SMALL_ROWS * 128 - SMALL_TOTAL,), F32))
    return jnp.concatenate(flat).reshape(SMALL_ROWS, 128)


def _unpack_small(pack, shapes):
    flat, out, off = pack.reshape(-1), {}, 0
    for n in SMALL:
        out[n] = flat[off:off + SMALL_SIZES[n]].reshape(shapes[n])
        off += SMALL_SIZES[n]
    return out, flat[off]


def _to_heads(t, n_heads):
    return jnp.transpose(t.reshape(t.shape[0], n_heads, HEAD_DIM), (1, 0, 2)).astype(BF16)


def _from_heads(t):
    return jnp.transpose(t, (1, 0, 2)).reshape(t.shape[1], t.shape[0] * HEAD_DIM)


def _pad_keys(t):
    return jnp.pad(t, ((0, 0), (BAND_PAD, 0), (0, 0)))


DIAGONALS = CHUNK + BAND - 1


def _band_bias(table):
    n_far = (LEFT_CHUNKS + 1) * CHUNK + CHUNK - MAX_REL
    near = table[N_REL - 1 - (DIAGONALS - n_far):N_REL - 1][::-1]
    diag = jnp.concatenate([jnp.broadcast_to(table[N_REL - 1], (n_far, H_CH)), near], axis=0).T
    diag = jnp.pad(diag, ((0, 0), (0, 1)))
    skew = jnp.tile(diag, (1, CHUNK))[:, :CHUNK * DIAGONALS].reshape(H_CH, CHUNK, DIAGONALS)
    return skew[:, :, CHUNK - 1:]


def _pad_w_in(w):
    qkv, f, gates = w[:, :QKV_WIDTH], w[:, QKV_WIDTH:QKV_WIDTH + H_FOX], w[:, QKV_WIDTH + H_FOX:]
    return jnp.concatenate([qkv, gates, f, jnp.zeros((w.shape[0], F_PAD - H_FOX), w.dtype)], axis=1)


def _unpad_w_in(w):
    return jnp.concatenate([w[:, :QKV_WIDTH], w[:, QKV_WIDTH + GATE_WIDTH:QKV_WIDTH + GATE_WIDTH + H_FOX],
                            w[:, QKV_WIDTH:QKV_WIDTH + GATE_WIDTH]], axis=1)


QKV_SPLITS = np.cumsum([0, W_SB, W_SB, W_SB, W_CH, W_CH, W_CH, W_FOX, W_FOX, W_FOX])


def _by_chip_rows(g):
    return g.reshape(2, N_CHIPS, g.shape[1] // N_CHIPS, g.shape[2])


def _ffn_fwd(x, g, w_in, w_out, tag, carries):
    h = _rmsnorm_fwd(x, g, f"{tag}_norm")
    gu = _mm(h, w_in, mode="nn", name=f"{tag}_in", gathered="col", out_dtype=BF16, carry=_carry_of(carries, "in"))
    a = _swiglu_fwd(gu, f"{tag}_act")
    y = _mm(a, w_out, mode="nn", name=f"{tag}_out", alpha=0.5, res=x, gathered="row",
            carry=_carry_of(carries, "out"))
    return y, (h, gu, a)


def _ffn_bwd(x, g, w_in, w_out, saved, dy, tag, carries):
    h, gu, a = saved
    da = _mm(dy, w_out, mode="nt", name=f"{tag}_da", alpha=0.5, gathered="row", out_dtype=BF16,
             carry=_carry_of(carries, "da"))
    dw_out = _mm(a, dy, mode="tn", name=f"{tag}_dwout", alpha=0.5, tm_cap=1408, out_dtype=BF16, out_split="row",
                 carry=_carry_of(carries, "dwout"))
    dgu = _swiglu_bwd(gu, da, f"{tag}_dact")
    dw_in = _mm(h, dgu, mode="tn", name=f"{tag}_dwin", tm_cap=512, out_dtype=BF16, out_split="col",
                carry=_carry_of(carries, "dwin"))
    dh = _mm(dgu, w_in, mode="nt", name=f"{tag}_dh", gathered="col", tn_cap=1024, carry=_carry_of(carries, "dh"))
    dx, dg = _rmsnorm_bwd(x, g, dh, dy, f"{tag}_dnorm")
    return dx, dg, dw_in, _by_chip_rows(dw_out)


def _mixer_fwd(x, lw, tag, carries):
    T = x.shape[0]
    h = _rmsnorm_fwd(x, lw["g_mix"], f"{tag}_norm")
    p = _mm(h, lw["w_in"], mode="nn", name=f"{tag}_proj", bias=lw["b_in"], tn_cap=896,
            carry=_carry_of(carries, "proj"))
    parts = [p[:, QKV_SPLITS[i]:QKV_SPLITS[i + 1]] for i in range(9)]
    qa, ka, va = (_to_heads(t, H_SB) for t in parts[0:3])
    qb, kb, vb = (_to_heads(t, H_CH) for t in parts[3:6])
    qc, kc, vc = (_to_heads(t, H_FOX) for t in parts[6:9])
    fcols = lax.optimization_barrier(p[:, QKV_WIDTH + GATE_WIDTH:])
    flog = jnp.pad(fcols[:, :H_FOX].T, ((0, 8 - H_FOX), (0, 0)))
    fcum = _forget_cumsum(flog, f"{tag}_fcum")[:H_FOX]
    fq, fk = fcum.reshape(H_FOX, T, 1), fcum.reshape(H_FOX, T // QB, QB)
    kbp, vbp = _pad_keys(kb), _pad_keys(vb)
    oa = _sb_fwd(qa, ka, va, f"{tag}_sb", _carry_of(carries, "sb"))
    ob = _chunk_fwd(qb, kbp, vbp, lw["bias"], f"{tag}_ch", _carry_of(carries, "ch"))
    oc, lse = _fox_fwd(qc, kc, vc, fq, fk, f"{tag}_fox", _carry_of(carries, "fox"))
    oam, obm, ocm = _from_heads(oa), _from_heads(ob), _from_heads(oc)
    ya = _mm(oam, lw["w_br_sb"], mode="nn", name=f"{tag}_bra", gathered="col")
    yb = _mm(obm, lw["w_br_ch"], mode="nn", name=f"{tag}_brb", gathered="col")
    yc = _mm(ocm, lw["w_br_fox"], mode="nn", name=f"{tag}_brc", gathered="col")
    merged = _gate_fwd(p, ya, yb, yc, f"{tag}_gate")
    y = _mm(merged, lw["w_out"], mode="nn", name=f"{tag}_out", res=x, gathered="row")
    saved = (h, p, (qa, ka, va), (qb, kbp, vbp), (qc, kc, vc), flog, fq, fk, lse, (oam, obm, ocm),
             (ya, yb, yc), merged)
    return y, saved


def _mixer_bwd(x, lw, saved, dy, tag, carries):
    (h, p, (qa, ka, va), (qb, kbp, vbp), (qc, kc, vc), flog, fq, fk, lse, (oam, obm, ocm),
     (ya, yb, yc), merged) = saved
    T = x.shape[0]
    grads = {}
    col, row = "col", "row"
    dmerged = _mm(dy, lw["w_out"], mode="nt", name=f"{tag}_dmerged", gathered=row)
    grads["w_out"] = _by_chip_rows(_mm(merged, dy, mode="tn", name=f"{tag}_dwout", tm_cap=1024, out_dtype=BF16,
                                       out_split="row"))
    dya, dyb, dyc, dgate = _gate_bwd(p, ya, yb, yc, dmerged, f"{tag}_dgate")
    doa = _mm(dya, lw["w_br_sb"], mode="nt", name=f"{tag}_doa", gathered=col)
    dob = _mm(dyb, lw["w_br_ch"], mode="nt", name=f"{tag}_dob", gathered=col)
    doc = _mm(dyc, lw["w_br_fox"], mode="nt", name=f"{tag}_doc", gathered=col)
    by_chip = dict(mode="tn", tm_cap=512, out_dtype=BF16, out_split="col")
    grads["w_br_sb"] = _mm(oam, dya, name=f"{tag}_dwbra", **by_chip)
    grads["w_br_ch"] = _mm(obm, dyb, name=f"{tag}_dwbrb", **by_chip)
    grads["w_br_fox"] = _mm(ocm, dyc, name=f"{tag}_dwbrc", **by_chip)
    dqa, dka, dva = _sb_bwd(qa, ka, va, _to_heads(doa, H_SB), f"{tag}_dsb", _carry_of(carries, "dsb"))
    dqb, dkb, dvb, dbias = _chunk_bwd(qb, kbp, vbp, lw["bias"], _to_heads(dob, H_CH), f"{tag}_dch",
                                      _carry_of(carries, "dch"))
    dqc, dkc, dvc, dfk = _fox_bwd(qc, kc, vc, fq, fk, lse, _to_heads(doc, H_FOX), f"{tag}_dfox",
                                  _carry_of(carries, "dfox"))
    dflog = _forget_cumsum_bwd(flog, jnp.pad(dfk.reshape(H_FOX, T), ((0, 8 - H_FOX), (0, 0))), f"{tag}_dfcum")
    dqkv = [_from_heads(t) for t in (dqa, dka, dva, dqb, dkb, dvb, dqc, dkc, dvc)]
    dforget = jnp.pad(dflog[:H_FOX].T, ((0, 0), (0, F_PAD - H_FOX))).astype(BF16)
    dpb = jnp.concatenate(dqkv + [dgate, dforget], axis=1)
    grads["b_in"] = _colsum(dpb, f"{tag}_dbin")
    dw_in =_unpad_w_in(_mm(h, dpb, mode="tn", name=f"{tag}_dwin", tm_cap=512, tn_cap=896, out_dtype=BF16))
    grads["w_in"] = jnp.transpose(dw_in.reshape(2, D_MODEL // 2, N_CHIPS, IN_WIDTH // N_CHIPS), (0, 2, 1, 3))
    dh = _mm(dpb, lw["w_in"], mode="nt", name=f"{tag}_dh", tk_cap=896, tn_cap=1024)
    dx, grads["g_mix"] = _rmsnorm_bwd(x, lw["g_mix"], dh, dy, f"{tag}_dnorm")
    grads["rel_bias"] = lw["bias_vjp"](dbias)[0]
    return dx, grads


def kernel(x, g_ffn1, w_ffn1_in, w_ffn1_out, g_mix, w_in, b_in, rel_bias, w_br_sb, w_br_ch, w_br_fox, w_out, g_ffn2, w_ffn2_in, w_ffn2_out, g_final, loss_target, m_g_ffn1, m_w_ffn1_in, m_w_ffn1_out, m_g_mix, m_w_in, m_b_in, m_rel_bias, m_w_br_sb, m_w_br_ch, m_w_br_fox, m_w_out, m_g_ffn2, m_w_ffn2_in, m_w_ffn2_out, m_g_final, v_g_ffn1, v_w_ffn1_in, v_w_ffn1_out, v_g_mix, v_w_in, v_b_in, v_rel_bias, v_w_br_sb, v_w_br_ch, v_w_br_fox, v_w_out, v_g_ffn2, v_w_ffn2_in, v_w_ffn2_out, v_g_final):
    names = ("g_ffn1", "w_ffn1_in", "w_ffn1_out", "g_mix", "w_in", "b_in", "rel_bias", "w_br_sb", "w_br_ch",
             "w_br_fox", "w_out", "g_ffn2", "w_ffn2_in", "w_ffn2_out", "g_final")
    w = dict(zip(names, (g_ffn1, w_ffn1_in, w_ffn1_out, g_mix, w_in, b_in, rel_bias, w_br_sb, w_br_ch,
                         w_br_fox, w_out, g_ffn2, w_ffn2_in, w_ffn2_out, g_final)))
    m = dict(zip(names, (m_g_ffn1, m_w_ffn1_in, m_w_ffn1_out, m_g_mix, m_w_in, m_b_in, m_rel_bias, m_w_br_sb,
                         m_w_br_ch, m_w_br_fox, m_w_out, m_g_ffn2, m_w_ffn2_in, m_w_ffn2_out, m_g_final)))
    v = dict(zip(names, (v_g_ffn1, v_w_ffn1_in, v_w_ffn1_out, v_g_mix, v_w_in, v_b_in, v_rel_bias, v_w_br_sb,
                         v_w_br_ch, v_w_br_fox, v_w_out, v_g_ffn2, v_w_ffn2_in, v_w_ffn2_out, v_g_final)))
    xs = x[0]
    tgt = loss_target[0]

    chip = (2 * lax.axis_index("x") + lax.axis_index("y")).astype(jnp.int32)[None]
    placed = [_place_own(w[n], chip, f"place_{n}") for n in BIG]
    bufs = [[p[l].reshape(N_CHIPS, 2, p[l].shape[1] // 2, p[l].shape[2]) for p in placed] for l in range(DEPTH)]
    padded_w_in = {}

    def layer_weights(l):
        lw = {n: b.reshape((N_CHIPS,) + w[n].shape[1:]) for n, b in zip(BIG, bufs[l])}
        if l not in padded_w_in:
            padded_w_in[l] = _pad_w_in(jnp.concatenate([lw["w_in"][j] for j in range(N_CHIPS)], axis=1))
        lw["w_in"] = padded_w_in[l]
        lw["b_in"] = _pad_w_in(w["b_in"][l][None, :])
        lw["bias"], lw["bias_vjp"] = jax.vjp(_band_bias, w["rel_bias"][l])
        for n in ("g_ffn1", "g_mix", "g_ffn2"):
            lw[n] = w[n][l][None, :]
        return lw

    def landed_in(l, idx):
        def done(carry):
            for i, b in zip(idx, carry.out[0]):
                bufs[l][i] = b
        return done

    def over_ici(l, idx):
        return lambda: _Carry([], [bufs[l][i] for i in idx], [], 3 * len(idx), _gather_over_ici, landed_in(l, idx))

    def to_sibling(l, idx):
        return lambda: _Carry([], [bufs[l][i] for i in idx], [], 3 * len(idx), _gather_to_sibling, landed_in(l, idx))

    def ffn_fwd(x_in, lw, which, tag, carries):
        return _ffn_fwd(x_in, lw[f"g_{which}"], lw[f"w_{which}_in"], lw[f"w_{which}_out"], tag, carries)

    def ffn_bwd(x_in, lw, which, saved_acts, dy, tag, carries):
        return _ffn_bwd(x_in, lw[f"g_{which}"], lw[f"w_{which}_in"], lw[f"w_{which}_out"], saved_acts, dy, tag,
                        carries)

    FFN1, W_IN, MIXER_REST, FFN2_IN, FFN2_OUT = (0, 1), (2,), (3, 4, 5, 6), (7,), (8,)
    first = FFN1 + W_IN
    for i, b in zip(first, _gather_layer([bufs[0][i] for i in first], "gather_l0")):
        bufs[0][i] = b
    fwd_carries = [
        {"ffn1": {"in": [over_ici(0, MIXER_REST)], "out": [to_sibling(0, MIXER_REST), over_ici(0, FFN2_OUT)]},
         "mix": {"proj": [to_sibling(0, FFN2_OUT), over_ici(1, MIXER_REST)],
                 "sb": [over_ici(0, FFN2_IN), over_ici(1, FFN2_OUT)],
                 "ch": [to_sibling(0, FFN2_IN), over_ici(1, FFN1)],
                 "fox": [over_ici(1, W_IN)]},
         "ffn2": {"in": [to_sibling(1, MIXER_REST + FFN2_OUT + FFN1 + W_IN)]}},
        {"ffn1": {}, "mix": {"sb": [over_ici(1, FFN2_IN)], "ch": [to_sibling(1, FFN2_IN)]}, "ffn2": {}},
    ]

    saved = []
    act = xs
    for l in range(DEPTH):
        x0 = act
        x1, s1 = ffn_fwd(x0, layer_weights(l), "ffn1", f"l{l}_ffn1", fwd_carries[l]["ffn1"])
        x2, s2 = _mixer_fwd(x1, layer_weights(l), f"l{l}_mix", fwd_carries[l]["mix"])
        act, s3 = ffn_fwd(x2, layer_weights(l), "ffn2", f"l{l}_ffn2", fwd_carries[l]["ffn2"])
        saved.append((x0, s1, x1, s2, x2, s3))
    layers = [layer_weights(l) for l in range(DEPTH)]

    dact, dg_final, loss_row = _final_loss(act, w["g_final"][None, :], tgt, "final_loss")

    big_grads = {n: [None] * DEPTH for n in BIG}
    small_grads = {n: [None] * DEPTH for n in ("g_ffn1", "g_mix", "b_in", "rel_bias", "g_ffn2")}
    pair = [[None] * len(BIG) for _ in range(DEPTH)]
    landed = [[None] * len(BIG) for _ in range(DEPTH)]

    def pair_up(l, idx, tag):
        for i, p in zip(idx, _pair_sums([big_grads[BIG[i]][l] for i in idx], tag)):
            pair[l][i] = p

    def exchange(l, idx):
        def done(carry):
            for i, a in zip(idx, carry.out[1]):
                landed[l][i] = a
        def make():
            carry = _scatter_carry([pair[l][i] for i in idx])
            carry.done = done
            return carry
        return make

    def exchange_one_way(l, i, k):
        def plan(srcs, bufs, new):
            x, y, c = _position()
            chip_k, idx_k = _other_chips(x, y)[k]
            return [(srcs[0].at[idx_k], bufs[0].at[k], (*chip_k, c))]
        def done(carry):
            landed[l][i] = carry.out[0][0]
        def make():
            if landed[l][i] is None:
                landed[l][i] = lax.empty((3,) + pair[l][i].shape[1:], BF16)
            return _Carry([pair[l][i]], [landed[l][i]], [], 1, plan, done)
        return make

    bwd_carries = [
        {"ffn2": {},
         "mix": {"dsb": [exchange(1, FFN1 + W_IN)], "dch": [exchange(1, MIXER_REST + FFN2_IN + FFN2_OUT)],
                 "dfox": [exchange(0, FFN2_IN + FFN2_OUT)]},
         "ffn1": {"da": [exchange(0, MIXER_REST)], "dwout": [exchange_one_way(0, 2, 0)],
                  "dwin": [exchange_one_way(0, 2, 1)], "dh": [exchange_one_way(0, 2, 2)]}},
        {"ffn2": {}, "mix": {}, "ffn1": {}},
    ]
    for l in reversed(range(DEPTH)):
        lw = layers[l]
        x0, s1, x1, s2, x2, s3 = saved[l]
        dx2, small_grads["g_ffn2"][l], big_grads["w_ffn2_in"][l], big_grads["w_ffn2_out"][l] = ffn_bwd(
            x2, lw, "ffn2", s3, dact, f"l{l}_ffn2", bwd_carries[l]["ffn2"])
        if l == 0:
            pair_up(0, FFN2_IN + FFN2_OUT, "l0_ffn2")
        dx1, mg = _mixer_bwd(x1, lw, s2, dx2, f"l{l}_mix", bwd_carries[l]["mix"])
        for n in ("w_in", "w_br_sb", "w_br_ch", "w_br_fox", "w_out"):
            big_grads[n][l] = mg[n]
        small_grads["b_in"][l] = _unpad_w_in(mg["b_in"])[0]
        small_grads["g_mix"][l] = mg["g_mix"][0]
        small_grads["rel_bias"][l] = mg["rel_bias"]
        if l == 0:
            pair_up(0, W_IN + MIXER_REST, "l0_mix")
        dact, dg1, big_grads["w_ffn1_in"][l], big_grads["w_ffn1_out"][l] = ffn_bwd(
            x0, lw, "ffn1", s1, dx1, f"l{l}_ffn1", bwd_carries[l]["ffn1"])
        small_grads["g_ffn1"][l] = dg1[0]
        small_grads["g_ffn2"][l] = small_grads["g_ffn2"][l][0]
        if l == 1:
            pair_up(1, tuple(range(len(BIG))), "l1")
    grad_x = dact[None]
    pair_up(0, FFN1, "l0_ffn1")
    for i, a in zip(FFN1, _scatter_chips([pair[0][i] for i in FFN1], "grad_scatter_l0")):
        landed[0][i] = a

    small = {n: jnp.stack(small_grads[n]) for n in small_grads}
    small["g_final"] = dg_final[0]
    small_sum, loss = _unpack_small(_allreduce_small(_pack_small(small, loss_row[0, :1]), "allreduce_small"),
                                    {n: w[n].shape for n in SMALL})

    axes = [1 if n in ROW_SHARDED else 0 for n in BIG] * DEPTH
    summed = _finish_grads(pair[0] + pair[1], landed[0] + landed[1], axes, "all")
    grads = {n: jnp.stack([summed[i], summed[len(BIG) + i]]) for i, n in enumerate(BIG)}
    grads.update(small_sum)

    delta, new_m, new_v = {}, {}, {}
    for n in BIG:
        shape = w[n].shape
        flat = lambda t: t.reshape(-1, shape[-1])
        d, nm, nv = _adamw(flat(w[n]), flat(grads[n]), flat(m[n]), flat(v[n]), f"adamw_{n}")
        delta[n], new_m[n], new_v[n] = d.reshape(shape), nm.reshape(shape), nv.reshape(shape)
    zero = jnp.zeros((1,), F32)
    sd, sm, sv = _adamw(_pack_small(w, zero), _pack_small(grads, zero), _pack_small(m, zero), _pack_small(v, zero),
                        "adamw_small")
    shapes = {n: w[n].shape for n in SMALL}
    for dst, src in ((delta, sd), (new_m, sm), (new_v, sv)):
        dst.update(_unpack_small(src, shapes)[0])

    return (loss, grad_x, *[grads[n] for n in names], *[delta[n] for n in names],
            *[new_m[n] for n in names], *[new_v[n] for n in names])
```

```python
import functools

import numpy as np
import jax
import jax.numpy as jnp
from jax import lax
from jax.experimental import pallas as pl
from jax.experimental.pallas import tpu as pltpu

F32 = jnp.float32
BF16 = jnp.bfloat16

D_MODEL = 1024
DEPTH = 2
CHUNK = 64
HEAD_DIM = 64
H_SB, H_CH, H_FOX = 4, 8, 4
W_SB, W_CH, W_FOX = H_SB * HEAD_DIM, H_CH * HEAD_DIM, H_FOX * HEAD_DIM
LEFT_CHUNKS = 8
MAX_REL = 128
N_REL = 2 * MAX_REL + 1
D_FF = 2816
QKV_WIDTH = 3 * (W_SB + W_CH + W_FOX)
GATE_WIDTH = 3 * D_MODEL
IN_WIDTH = QKV_WIDTH + H_FOX + GATE_WIDTH
F_PAD = 128
P_WIDTH = QKV_WIDTH + GATE_WIDTH + F_PAD
RMS_EPS = 1e-6
NEG = -1e30
SCALE = HEAD_DIM ** -0.5
QB = 256
BAND = (LEFT_CHUNKS + 2) * CHUNK
BAND_PAD = BAND - CHUNK

ADAM_LR, ADAM_B1, ADAM_B2, ADAM_EPS, ADAM_WD, ADAM_STEP = 0.001, 0.9, 0.999, 1e-08, 0.01, 10

N_CHIPS = 4
PACK_COLS = 1024
VMEM_BUDGET = 52 * 1024 * 1024

NT = (((1,), (1,)), ((), ()))
TN = (((0,), (0,)), ((), ()))
NN = (((1,), (0,)), ((), ()))
MESH = pl.DeviceIdType.MESH


def _pcall(body, **kw):
    return pl.pallas_call(body, **kw)


class _Carry:
    def __init__(self, srcs, bufs, new, count, plan, done=None):
        self.srcs, self.bufs, self.new, self.count, self.plan = list(srcs), list(bufs), list(new), count, plan
        self.out, self.done = None, done

    @staticmethod
    def join(parts):
        parts = [p for p in parts if p is not None]
        if not parts:
            return None

        def plan(srcs, bufs, new):
            copies, s, b, n = [], 0, 0, 0
            for p in parts:
                copies += p.plan(srcs[s:s + len(p.srcs)], bufs[b:b + len(p.bufs)], new[n:n + len(p.new)])
                s, b, n = s + len(p.srcs), b + len(p.bufs), n + len(p.new)
            return copies

        whole = _Carry(sum((p.srcs for p in parts), []), sum((p.bufs for p in parts), []),
                       sum((p.new for p in parts), []), sum(p.count for p in parts), plan)
        whole.parts = parts
        return whole

    def share_out(self):
        b, n = 0, 0
        for p in getattr(self, "parts", []):
            p.out = (self.out[0][b:b + len(p.bufs)], self.out[1][n:n + len(p.new)])
            b, n = b + len(p.bufs), n + len(p.new)
            p.share_out()
        if self.done is not None:
            self.done(self)


def _carry_of(carries, key):
    return _Carry.join([make() for make in carries.get(key, [])])


def _pcall_carrying(body, carry, **kw):
    if carry is None:
        return _pcall(body, **kw)
    in_specs = list(kw.pop("in_specs"))
    out_specs, out_shape = kw.pop("out_specs"), kw.pop("out_shape")
    single = not isinstance(out_shape, (list, tuple))
    out_specs = [out_specs] if single else list(out_specs)
    out_shape = [out_shape] if single else list(out_shape)
    scratch = list(kw.pop("scratch_shapes", []))
    grid = tuple(kw.get("grid", ()))
    n_in, n_out, n_scr = len(in_specs), len(out_specs), len(scratch)
    ns, nb, nn = len(carry.srcs), len(carry.bufs), len(carry.new)

    def body2(*refs):
        ins, srcs = refs[:n_in], refs[n_in:n_in + ns]
        at = n_in + ns + nb
        outs, bufs, new = refs[at:at + n_out], refs[at + n_out:at + n_out + nb], refs[at + n_out + nb:at + n_out + nb + nn]
        at += n_out + nb + nn
        scr, (send_sems, recv_sems) = refs[at:at + n_scr], refs[at + n_scr:]

        def copies():
            return [_remote_copy(s, d, send_sems, recv_sems, k, to)
                    for k, (s, d, to) in enumerate(carry.plan(srcs, bufs, new))]

        def start():
            for cp in copies():
                cp.start()

        def wait():
            for cp in copies():
                cp.wait()

        if grid:
            ids = [pl.program_id(a) for a in range(len(grid))]
            pl.when(functools.reduce(jnp.logical_and, [i == 0 for i in ids]))(start)
        else:
            start()
        body(*ins, *outs, *scr)
        if grid:
            pl.when(functools.reduce(jnp.logical_and, [i == g - 1 for i, g in zip(ids, grid)]))(wait)
        else:
            wait()

    call = _pcall(
        body2, in_specs=in_specs + [HBM_SPEC] * (ns + nb), out_specs=out_specs + [HBM_SPEC] * (nb + nn),
        out_shape=out_shape + [jax.ShapeDtypeStruct(b.shape, b.dtype) for b in carry.bufs] + carry.new,
        scratch_shapes=scratch + [pltpu.SemaphoreType.DMA((carry.count,)), pltpu.SemaphoreType.DMA((carry.count,))],
        input_output_aliases={n_in + ns + j: n_out + j for j in range(nb)}, **kw)

    def apply(*args):
        res = call(*args, *carry.srcs, *carry.bufs)
        carry.out = (list(res[n_out:n_out + nb]), list(res[n_out + nb:]))
        carry.share_out()
        return res[0] if single else list(res[:n_out])

    return apply


def _pick(n, cap, mult=128):
    best = None
    d = mult
    while d <= min(n, cap):
        if n % d == 0:
            best = d
        d += mult
    return n if best is None else best


def _nbytes(shape, dtype):
    return int(np.prod(shape)) * jnp.dtype(dtype).itemsize


def _mm(a, b, *, mode, name, out_dtype=F32, alpha=1.0, bias=None, res=None, tm_cap=1024, tn_cap=512,
        tk_cap=2816, gathered=None, out_split=None, carry=None):
    if mode == "tn":
        K, M = a.shape
    else:
        M, K = a.shape
    dn = {"nn": NN, "nt": NT, "tn": TN}[mode]
    b_rows = None
    if gathered is None:
        N = b.shape[0] if mode == "nt" else b.shape[1]
        tn = {None: _pick(N, tn_cap), "col": N // N_CHIPS, "row": N // 2}[out_split]
        if out_split == "col":
            tm_cap = min(tm_cap, M // 2)
        tk = K if K <= tk_cap else _pick(K, tk_cap)
        b_spec = (pl.BlockSpec((tn, tk), lambda i, j, k: (j, k)) if mode == "nt"
                  else pl.BlockSpec((tk, tn), lambda i, j, k: (k, j)))
    else:
        r, c = b.shape[1:]
        rows, cols = (r, N_CHIPS * c) if gathered == "col" else (N_CHIPS * r, c)
        N = rows if mode == "nt" else cols
        assert K == (cols if mode == "nt" else rows), (name, K, rows, cols)
        if gathered == "col" and mode == "nn":
            tn, tk = c, (r if r <= tk_cap else _pick(r, tk_cap))
            b_spec = pl.BlockSpec((None, tk, c), lambda i, j, k: (j, k, 0))
        elif gathered == "col":
            tn, tk = _pick(r, tn_cap), c
            b_spec = pl.BlockSpec((None, tn, c), lambda i, j, k: (k, j, 0))
        elif mode == "nn":
            tn, tk, b_rows = _pick(c, tn_cap), K, N_CHIPS
            b_spec = pl.BlockSpec((N_CHIPS, r, tn), lambda i, j, k: (0, 0, j))
        else:
            tn, tk, b_rows = 2 * r, K, 2
            b_spec = pl.BlockSpec((2, r, c), lambda i, j, k: (j, 0, 0))
    tm = _pick(M, tm_cap)
    nk = K // tk

    a_spec = (pl.BlockSpec((tk, tm), lambda i, j, k: (k, i)) if mode == "tn"
              else pl.BlockSpec((tm, tk), lambda i, j, k: (i, k)))
    in_specs = [a_spec, b_spec]
    args = [a, b]
    if bias is not None:
        in_specs.append(pl.BlockSpec((1, tn), lambda i, j, k: (0, j)))
        args.append(bias)
    if res is not None:
        in_specs.append(pl.BlockSpec((tm, tn), lambda i, j, k: (i, j)))
        args.append(res)

    est = 2 * (_nbytes((tm, tk), a.dtype) + _nbytes((tk, tn), b.dtype) + _nbytes((tm, tn), out_dtype))
    est += _nbytes((tm, tn), F32) * (3 if res is not None else 1)
    assert est < VMEM_BUDGET, (name, est)

    def body(*refs):
        a_ref, b_ref = refs[0], refs[1]
        pos = 2
        bias_ref = res_ref = None
        if bias is not None:
            bias_ref = refs[pos]
            pos += 1
        if res is not None:
            res_ref = refs[pos]
            pos += 1
        o_ref = refs[pos]
        acc_ref = refs[pos + 1] if nk > 1 else None

        bv = b_ref[...]
        if b_rows is not None:
            bv = bv.reshape(b_rows * bv.shape[1], bv.shape[2])
        part = lax.dot_general(a_ref[...].astype(BF16), bv.astype(BF16), dn, preferred_element_type=F32)

        def finish(acc):
            if alpha != 1.0:
                acc = acc * alpha
            if bias_ref is not None:
                acc = acc + bias_ref[...]
            if res_ref is not None:
                acc = acc + res_ref[...]
            o_ref[...] = acc.astype(out_dtype)

        if nk == 1:
            finish(part)
        else:
            k = pl.program_id(2)

            @pl.when(k == 0)
            def _():
                acc_ref[...] = part

            @pl.when(k > 0)
            def _():
                acc_ref[...] += part

            @pl.when(k == nk - 1)
            def _():
                finish(acc_ref[...])

    if out_split == "col":
        per_half = M // 2 // tm
        out_spec = pl.BlockSpec((None, None, tm, tn), lambda i, j, k: (i // per_half, j, i % per_half, 0))
        out_shape = jax.ShapeDtypeStruct((2, N_CHIPS, M // 2, tn), out_dtype)
    elif out_split == "row":
        out_spec = pl.BlockSpec((None, tm, tn), lambda i, j, k: (j, i, 0))
        out_shape = jax.ShapeDtypeStruct((2, M, tn), out_dtype)
    else:
        out_spec = pl.BlockSpec((tm, tn), lambda i, j, k: (i, j))
        out_shape = jax.ShapeDtypeStruct((M, N), out_dtype)
    return _pcall_carrying(
        body, carry, name=name, grid=(M // tm, N // tn, nk), in_specs=in_specs, out_specs=out_spec,
        out_shape=out_shape,
        scratch_shapes=[pltpu.VMEM((tm, tn), F32)] if nk > 1 else [],
        compiler_params=pltpu.CompilerParams(dimension_semantics=("parallel", "parallel", "arbitrary")),
    )(*args)


def _rmsnorm_fwd(x, g, name):
    T, Dm = x.shape
    tm = _pick(T, 256, 8)

    def body(x_ref, g_ref, h_ref):
        xv = x_ref[...]
        rstd = lax.rsqrt(jnp.mean(xv * xv, axis=-1, keepdims=True) + RMS_EPS)
        h_ref[...] = (xv * rstd * g_ref[...]).astype(BF16)

    return _pcall(
        body, name=name, grid=(T // tm,),
        in_specs=[pl.BlockSpec((tm, Dm), lambda i: (i, 0)), pl.BlockSpec((1, Dm), lambda i: (0, 0))],
        out_specs=pl.BlockSpec((tm, Dm), lambda i: (i, 0)),
        out_shape=jax.ShapeDtypeStruct((T, Dm), BF16),
    )(x, g)


def _rmsnorm_bwd(x, g, dh, dres, name):
    T, Dm = x.shape
    tm = _pick(T, 256, 8)

    def body(x_ref, g_ref, dh_ref, dres_ref, dx_ref, dg_ref):
        xv = x_ref[...]
        rstd = lax.rsqrt(jnp.mean(xv * xv, axis=-1, keepdims=True) + RMS_EPS)
        xhat = xv * rstd
        dhv = dh_ref[...]
        dyg = dhv * g_ref[...]
        dx_ref[...] = dres_ref[...] + rstd * (dyg - xhat * jnp.mean(dyg * xhat, axis=-1, keepdims=True))
        part = jnp.sum(dhv * xhat, axis=0, keepdims=True)

        @pl.when(pl.program_id(0) == 0)
        def _():
            dg_ref[...] = part

        @pl.when(pl.program_id(0) > 0)
        def _():
            dg_ref[...] += part

    row = pl.BlockSpec((tm, Dm), lambda i: (i, 0))
    vec = pl.BlockSpec((1, Dm), lambda i: (0, 0))
    return _pcall(
        body, name=name, grid=(T // tm,), in_specs=[row, vec, row, row], out_specs=[row, vec],
        out_shape=[jax.ShapeDtypeStruct((T, Dm), F32), jax.ShapeDtypeStruct((1, Dm), F32)],
        compiler_params=pltpu.CompilerParams(dimension_semantics=("arbitrary",)),
    )(x, g, dh, dres)


def _final_loss(x, g, tgt, name):
    T, Dm = x.shape
    tm = _pick(T, 256, 8)

    def body(x_ref, g_ref, t_ref, dx_ref, dg_ref, loss_ref):
        xv = x_ref[...]
        gv = g_ref[...]
        rstd = lax.rsqrt(jnp.mean(xv * xv, axis=-1, keepdims=True) + RMS_EPS)
        xhat = xv * rstd
        err = xhat * gv - t_ref[...]
        part_loss = 0.5 * jnp.sum(jnp.mean(err * err, axis=-1, keepdims=True), axis=0, keepdims=True)
        dy = err * (1.0 / Dm)
        dyg = dy * gv
        dx_ref[...] = rstd * (dyg - xhat * jnp.mean(dyg * xhat, axis=-1, keepdims=True))
        part_g = jnp.sum(dy * xhat, axis=0, keepdims=True)
        part_l = jnp.broadcast_to(part_loss, (1, 128))

        @pl.when(pl.program_id(0) == 0)
        def _():
            dg_ref[...] = part_g
            loss_ref[...] = part_l

        @pl.when(pl.program_id(0) > 0)
        def _():
            dg_ref[...] += part_g
            loss_ref[...] += part_l

    row = pl.BlockSpec((tm, Dm), lambda i: (i, 0))
    vec = pl.BlockSpec((1, Dm), lambda i: (0, 0))
    return _pcall(
        body, name=name, grid=(T // tm,), in_specs=[row, vec, row],
        out_specs=[row, vec, pl.BlockSpec((1, 128), lambda i: (0, 0))],
        out_shape=[jax.ShapeDtypeStruct((T, Dm), F32), jax.ShapeDtypeStruct((1, Dm), F32),
                   jax.ShapeDtypeStruct((1, 128), F32)],
        compiler_params=pltpu.CompilerParams(dimension_semantics=("arbitrary",)),
    )(x, g, tgt)


def _sigmoid(z):
    return 1.0 / (1.0 + jnp.exp(-z))


def _ffn_in_swiglu(h, w_in, name, carry=None):
    T, Dm = h.shape
    cw = w_in.shape[2]
    tm = _pick(T, 512, 16)

    def body(h_ref, wg_ref, wu_ref, g_ref, u_ref, a_ref):
        hv = h_ref[...]
        gv = jnp.dot(hv, wg_ref[...], preferred_element_type=F32)
        uv = jnp.dot(hv, wu_ref[...], preferred_element_type=F32)
        g_ref[...] = gv.astype(BF16)
        u_ref[...] = uv.astype(BF16)
        a_ref[...] = (gv * _sigmoid(gv) * uv).astype(BF16)

    out = pl.BlockSpec((tm, cw), lambda i, j: (i, j))
    return _pcall_carrying(
        body, carry, name=name, grid=(T // tm, 2),
        in_specs=[pl.BlockSpec((tm, Dm), lambda i, j: (i, 0)), pl.BlockSpec((None, Dm, cw), lambda i, j: (j, 0, 0)),
                  pl.BlockSpec((None, Dm, cw), lambda i, j: (j + 2, 0, 0))],
        out_specs=[out, out, out], out_shape=[jax.ShapeDtypeStruct((T, D_FF), BF16)] * 3,
    )(h, w_in, w_in)


def _swiglu_bwd(g, u, da, name):
    T = g.shape[0]
    tm = _pick(T, 256, 16)

    def body(g_ref, u_ref, da_ref, d_ref):
        gv = g_ref[...].astype(F32)
        uv = u_ref[...].astype(F32)
        dav = da_ref[...].astype(F32)
        sg = _sigmoid(gv)
        d_ref[:, :D_FF] = (dav * uv * (sg + gv * sg * (1.0 - sg))).astype(BF16)
        d_ref[:, D_FF:] = (dav * gv * sg).astype(BF16)

    row = pl.BlockSpec((tm, D_FF), lambda i: (i, 0))
    return _pcall(
        body, name=name, grid=(T // tm,), in_specs=[row, row, row],
        out_specs=pl.BlockSpec((tm, 2 * D_FF), lambda i: (i, 0)),
        out_shape=jax.ShapeDtypeStruct((T, 2 * D_FF), BF16),
    )(g, u, da)


GATE_BLOCK0 = QKV_WIDTH // D_MODEL


def _gate_fwd(p, ya, yb, yc, name):
    T = p.shape[0]
    tm = _pick(T, 256, 8)

    def body(ga_ref, gb_ref, gc_ref, ya_ref, yb_ref, yc_ref, o_ref):
        o_ref[...] = (_sigmoid(ga_ref[...]) * ya_ref[...] + _sigmoid(gb_ref[...]) * yb_ref[...]
                      + _sigmoid(gc_ref[...]) * yc_ref[...]).astype(BF16)

    gate = [pl.BlockSpec((tm, D_MODEL), functools.partial(lambda i, kk: (i, GATE_BLOCK0 + kk), kk=kk))
            for kk in range(3)]
    row = pl.BlockSpec((tm, D_MODEL), lambda i: (i, 0))
    return _pcall(
        body, name=name, grid=(T // tm,), in_specs=gate + [row, row, row], out_specs=row,
        out_shape=jax.ShapeDtypeStruct((T, D_MODEL), BF16),
    )(p, p, p, ya, yb, yc)


def _gate_bwd(p, ya, yb, yc, dm, name):
    T = p.shape[0]
    tm = _pick(T, 256, 16)

    def body(ga_ref, gb_ref, gc_ref, ya_ref, yb_ref, yc_ref, dm_ref, da_ref, db_ref, dc_ref, dg_ref):
        dmv = dm_ref[...]
        for kk, (g_ref, y_ref, d_ref) in enumerate(((ga_ref, ya_ref, da_ref), (gb_ref, yb_ref, db_ref),
                                                    (gc_ref, yc_ref, dc_ref))):
            s = _sigmoid(g_ref[...])
            d_ref[...] = (s * dmv).astype(BF16)
            dg_ref[:, kk * D_MODEL:(kk + 1) * D_MODEL] = (dmv * y_ref[...] * s * (1.0 - s)).astype(BF16)

    gate = [pl.BlockSpec((tm, D_MODEL), functools.partial(lambda i, kk: (i, GATE_BLOCK0 + kk), kk=kk))
            for kk in range(3)]
    row = pl.BlockSpec((tm, D_MODEL), lambda i: (i, 0))
    return _pcall(
        body, name=name, grid=(T // tm,), in_specs=gate + [row, row, row, row],
        out_specs=[row, row, row, pl.BlockSpec((tm, GATE_WIDTH), lambda i: (i, 0))],
        out_shape=[jax.ShapeDtypeStruct((T, D_MODEL), BF16)] * 3 + [jax.ShapeDtypeStruct((T, GATE_WIDTH), BF16)],
    )(p, p, p, ya, yb, yc, dm)


def _colsum(a, name):
    T, N = a.shape
    tm = _pick(T, 256, 16)

    def body(a_ref, o_ref):
        part = jnp.sum(a_ref[...].astype(F32), axis=0, keepdims=True)

        @pl.when(pl.program_id(0) == 0)
        def _():
            o_ref[...] = part

        @pl.when(pl.program_id(0) > 0)
        def _():
            o_ref[...] += part

    return _pcall(
        body, name=name, grid=(T // tm,), in_specs=[pl.BlockSpec((tm, N), lambda i: (i, 0))],
        out_specs=pl.BlockSpec((1, N), lambda i: (0, 0)), out_shape=jax.ShapeDtypeStruct((1, N), F32),
        compiler_params=pltpu.CompilerParams(dimension_semantics=("arbitrary",)),
    )(a)


def _adamw(w, g, m, v, name):
    R, C = w.shape
    tr = 256 if R % 256 == 0 else R
    c1 = 1.0 / (1.0 - ADAM_B1 ** ADAM_STEP)
    c2 = 1.0 / (1.0 - ADAM_B2 ** ADAM_STEP)

    def body(w_ref, g_ref, m_ref, v_ref, d_ref, nm_ref, nv_ref):
        gv = g_ref[...]
        mn = ADAM_B1 * m_ref[...] + (1.0 - ADAM_B1) * gv
        vn = ADAM_B2 * v_ref[...] + (1.0 - ADAM_B2) * (gv * gv)
        nm_ref[...] = mn
        nv_ref[...] = vn
        d_ref[...] = -ADAM_LR * ((mn * c1) / (jnp.sqrt(vn * c2) + ADAM_EPS) + ADAM_WD * w_ref[...])

    spec = pl.BlockSpec((tr, C), lambda i: (i, 0))
    return _pcall(
        body, name=name, grid=(R // tr,), in_specs=[spec] * 4, out_specs=[spec] * 3,
        out_shape=[jax.ShapeDtypeStruct((R, C), F32)] * 3,
    )(w, g, m, v)


def _log_sigmoid(z):
    return jnp.minimum(z, 0.0) - jnp.log(1.0 + jnp.exp(-jnp.abs(z)))


def _dot3(x, m01):
    x1 = x.astype(BF16)
    r1 = x - x1.astype(F32)
    x2 = r1.astype(BF16)
    x3 = (r1 - x2.astype(F32)).astype(BF16)
    n = x.shape[0]
    if n % 16:
        return (jnp.dot(x1, m01, preferred_element_type=F32) + jnp.dot(x2, m01, preferred_element_type=F32)
                + jnp.dot(x3, m01, preferred_element_type=F32))
    y = jnp.dot(jnp.concatenate([x1, x2, x3], axis=0), m01, preferred_element_type=F32)
    return y[:n] + y[n:2 * n] + y[2 * n:]


def _dot_nt(a, b):
    return lax.dot_general(a, b, NT, preferred_element_type=F32)


def _dot_tn(a, b):
    return lax.dot_general(a, b, TN, preferred_element_type=F32)


def _iota2(shape):
    return lax.broadcasted_iota(jnp.int32, shape, 0), lax.broadcasted_iota(jnp.int32, shape, 1)


HEADS_PER_STEP = 4
HEADS = range(HEADS_PER_STEP)


CHUNK_HEADS_PER_STEP = 2
CHUNK_HEADS = range(CHUNK_HEADS_PER_STEP)


def _head_spec(T, rows=None, width=HEAD_DIM, heads=HEADS_PER_STEP):
    return pl.BlockSpec((heads, T if rows is None else rows, width), lambda g: (g, 0, 0))


def _sb_weights(qb, kb, t0, s0, racc, m_gt, row, col):
    z = _dot_nt(qb, kb) * SCALE
    strict = (s0 + col) < (t0 + row)
    lb = _log_sigmoid(z)
    lf = jnp.where(strict, lb - z, 0.0)
    between = _dot3(lf, m_gt) + racc
    w = jnp.where(strict, jnp.exp(lb + between), 0.0)
    return strict, lb, lf, w


def _sb_fwd(q, k, v, name, carry=None):
    H, T, _ = q.shape
    nb = T // QB

    def body(q_ref, k_ref, v_ref, o_ref):
        row, col = _iota2((QB, QB))
        m_gt = (row > col).astype(BF16)

        def qblock(i, _):
            t0 = pl.multiple_of(i * QB, QB)
            qbs = [q_ref[h, pl.ds(t0, QB), :].astype(BF16) for h in HEADS]

            def kblock(jj, carry):
                s0 = pl.multiple_of((i - jj) * QB, QB)
                out = []
                for h in HEADS:
                    acc, racc = carry[h]
                    kb = k_ref[h, pl.ds(s0, QB), :].astype(BF16)
                    vb = v_ref[h, pl.ds(s0, QB), :].astype(BF16)
                    _, _, lf, w = _sb_weights(qbs[h], kb, t0, s0, racc, m_gt, row, col)
                    acc = acc + jnp.dot(w.astype(BF16), vb, preferred_element_type=F32)
                    out.append((acc, racc + jnp.sum(lf, axis=1, keepdims=True)))
                return tuple(out)

            res = lax.fori_loop(0, i + 1, kblock,
                                tuple((jnp.zeros((QB, HEAD_DIM), F32), jnp.zeros((QB, 1), F32)) for _ in HEADS))
            for h in HEADS:
                o_ref[h, pl.ds(t0, QB), :] = res[h][0].astype(BF16)
            return 0

        lax.fori_loop(0, nb, qblock, 0)

    spec = _head_spec(T)
    return _pcall_carrying(body, carry, name=name, grid=(H // HEADS_PER_STEP,), in_specs=[spec] * 3, out_specs=spec,
                           out_shape=jax.ShapeDtypeStruct((H, T, HEAD_DIM), BF16))(q, k, v)


def _sb_bwd(q, k, v, do, name, carry=None):
    H, T, _ = q.shape
    nb = T // QB

    def body(q_ref, k_ref, v_ref, do_ref, dq_ref, dk_out, dv_out, racc_ref, dk_ref, dv_ref):
        row, col = _iota2((QB, QB))
        m_gt = (row > col).astype(BF16)
        m_lt = (row < col).astype(BF16)
        dk_ref[...] = jnp.zeros_like(dk_ref)
        dv_ref[...] = jnp.zeros_like(dv_ref)

        def qblock(i, _):
            t0 = pl.multiple_of(i * QB, QB)
            qbs = [q_ref[h, pl.ds(t0, QB), :].astype(BF16) for h in HEADS]
            dobs = [do_ref[h, pl.ds(t0, QB), :].astype(BF16) for h in HEADS]

            def suffix(jj, raccs):
                j = i - jj
                s0 = pl.multiple_of(j * QB, QB)
                out = []
                for h in HEADS:
                    kb = k_ref[h, pl.ds(s0, QB), :].astype(BF16)
                    z = _dot_nt(qbs[h], kb) * SCALE
                    lf = jnp.where((s0 + col) < (t0 + row), _log_sigmoid(z) - z, 0.0)
                    racc_ref[h, j] = raccs[h]
                    out.append(raccs[h] + jnp.sum(lf, axis=1, keepdims=True))
                return tuple(out)

            lax.fori_loop(0, i + 1, suffix, tuple(jnp.zeros((QB, 1), F32) for _ in HEADS))

            def kblock(j, carry):
                s0 = pl.multiple_of(j * QB, QB)
                out = []
                for h in HEADS:
                    dq, cacc = carry[h]
                    kb = k_ref[h, pl.ds(s0, QB), :].astype(BF16)
                    vb = v_ref[h, pl.ds(s0, QB), :].astype(BF16)
                    strict, lb, _, w = _sb_weights(qbs[h], kb, t0, s0, racc_ref[h, j], m_gt, row, col)
                    e = _dot_nt(dobs[h], vb) * w
                    c_left = _dot3(e, m_lt) + cacc
                    sig = jnp.exp(lb)
                    dz = jnp.where(strict, e * (1.0 - sig) - c_left * sig, 0.0).astype(BF16)
                    dq = dq + jnp.dot(dz, kb, preferred_element_type=F32)
                    dk_ref[h, pl.ds(s0, QB), :] += _dot_tn(dz, qbs[h]) * SCALE
                    dv_ref[h, pl.ds(s0, QB), :] += _dot_tn(w.astype(BF16), dobs[h])
                    out.append((dq, cacc + jnp.sum(e, axis=1, keepdims=True)))
                return tuple(out)

            res = lax.fori_loop(0, i + 1, kblock,
                                tuple((jnp.zeros((QB, HEAD_DIM), F32), jnp.zeros((QB, 1), F32)) for _ in HEADS))
            for h in HEADS:
                dq_ref[h, pl.ds(t0, QB), :] = (res[h][0] * SCALE).astype(BF16)
            return 0

        lax.fori_loop(0, nb, qblock, 0)
        dk_out[...] = dk_ref[...].astype(BF16)
        dv_out[...] = dv_ref[...].astype(BF16)

    spec = _head_spec(T)
    acc = pltpu.VMEM((HEADS_PER_STEP, T, HEAD_DIM), F32)
    return _pcall_carrying(body, carry, name=name, grid=(H // HEADS_PER_STEP,), in_specs=[spec] * 4,
                           out_specs=[spec] * 3, out_shape=[jax.ShapeDtypeStruct((H, T, HEAD_DIM), BF16)] * 3,
                           scratch_shapes=[pltpu.VMEM((HEADS_PER_STEP, nb, QB, 1), F32), acc, acc])(q, k, v, do)


def _fox_logits(qb, kb, fq, fk, t0, s0, row, col):
    z = _dot_nt(qb, kb) * SCALE + fq - fk
    return jnp.where((s0 + col) <= (t0 + row), z, NEG)


def _fox_specs(T):
    return _head_spec(T, width=1), _head_spec(T, rows=T // QB, width=QB)


def _fox_fwd(q, k, v, fq, fk, name, carry=None):
    H, T, _ = q.shape
    nb = T // QB

    def body(q_ref, k_ref, v_ref, fq_ref, fk_ref, o_ref, lse_ref):
        row, col = _iota2((QB, QB))

        def qblock(i, _):
            t0 = pl.multiple_of(i * QB, QB)
            qbs = [q_ref[h, pl.ds(t0, QB), :].astype(BF16) for h in HEADS]
            fqs = [fq_ref[h, pl.ds(t0, QB), :] for h in HEADS]

            def kblock(j, carry):
                s0 = pl.multiple_of(j * QB, QB)
                out = []
                for h in HEADS:
                    acc, m, l = carry[h]
                    kb = k_ref[h, pl.ds(s0, QB), :].astype(BF16)
                    vb = v_ref[h, pl.ds(s0, QB), :].astype(BF16)
                    z = _fox_logits(qbs[h], kb, fqs[h], fk_ref[h, pl.ds(j, 1), :], t0, s0, row, col)
                    m_new = jnp.maximum(m, jnp.max(z, axis=1, keepdims=True))
                    a = jnp.exp(m - m_new)
                    p = jnp.exp(z - m_new)
                    acc = a * acc + jnp.dot(p.astype(BF16), vb, preferred_element_type=F32)
                    out.append((acc, m_new, a * l + jnp.sum(p, axis=1, keepdims=True)))
                return tuple(out)

            res = lax.fori_loop(0, i + 1, kblock,
                                tuple((jnp.zeros((QB, HEAD_DIM), F32), jnp.full((QB, 1), NEG, F32),
                                       jnp.zeros((QB, 1), F32)) for _ in HEADS))
            for h in HEADS:
                acc, m, l = res[h]
                o_ref[h, pl.ds(t0, QB), :] = (acc / l).astype(BF16)
                lse_ref[h, pl.ds(t0, QB), :] = m + jnp.log(l)
            return 0

        lax.fori_loop(0, nb, qblock, 0)

    spec = _head_spec(T)
    fq_spec, fk_spec = _fox_specs(T)
    return _pcall_carrying(
        body, carry, name=name, grid=(H // HEADS_PER_STEP,), in_specs=[spec] * 3 + [fq_spec, fk_spec],
        out_specs=[spec, fq_spec],
        out_shape=[jax.ShapeDtypeStruct((H, T, HEAD_DIM), BF16), jax.ShapeDtypeStruct((H, T, 1), F32)],
    )(q, k, v, fq, fk)


def _fox_bwd(q, k, v, fq, fk, lse, do, name, carry=None):
    H, T, _ = q.shape
    nb = T // QB

    def body(q_ref, k_ref, v_ref, fq_ref, fk_ref, lse_ref, do_ref, dq_ref, dk_out, dv_out, dfk_ref, dk_ref, dv_ref):
        row, col = _iota2((QB, QB))
        dk_ref[...] = jnp.zeros_like(dk_ref)
        dv_ref[...] = jnp.zeros_like(dv_ref)
        dfk_ref[...] = jnp.zeros_like(dfk_ref)

        def qblock(i, _):
            t0 = pl.multiple_of(i * QB, QB)
            qbs = [q_ref[h, pl.ds(t0, QB), :].astype(BF16) for h in HEADS]
            fqs = [fq_ref[h, pl.ds(t0, QB), :] for h in HEADS]
            lses = [lse_ref[h, pl.ds(t0, QB), :] for h in HEADS]
            dobs = [do_ref[h, pl.ds(t0, QB), :].astype(BF16) for h in HEADS]

            def probs(h, j):
                s0 = pl.multiple_of(j * QB, QB)
                kb = k_ref[h, pl.ds(s0, QB), :].astype(BF16)
                vb = v_ref[h, pl.ds(s0, QB), :].astype(BF16)
                z = _fox_logits(qbs[h], kb, fqs[h], fk_ref[h, pl.ds(j, 1), :], t0, s0, row, col)
                return s0, kb, jnp.exp(z - lses[h]), _dot_nt(dobs[h], vb)

            def row_dot(j, deltas):
                out = []
                for h in HEADS:
                    _, _, p, dp = probs(h, j)
                    out.append(deltas[h] + jnp.sum(p * dp, axis=1, keepdims=True))
                return tuple(out)

            deltas = lax.fori_loop(0, i + 1, row_dot, tuple(jnp.zeros((QB, 1), F32) for _ in HEADS))

            def kblock(j, dqs):
                out = []
                for h in HEADS:
                    s0, kb, p, dp = probs(h, j)
                    dz = p * (dp - deltas[h])
                    dzb = dz.astype(BF16)
                    dk_ref[h, pl.ds(s0, QB), :] += _dot_tn(dzb, qbs[h]) * SCALE
                    dv_ref[h, pl.ds(s0, QB), :] += _dot_tn(p.astype(BF16), dobs[h])
                    dfk_ref[h, pl.ds(j, 1), :] += -jnp.sum(dz, axis=0, keepdims=True)
                    out.append(dqs[h] + jnp.dot(dzb, kb, preferred_element_type=F32))
                return tuple(out)

            dqs = lax.fori_loop(0, i + 1, kblock, tuple(jnp.zeros((QB, HEAD_DIM), F32) for _ in HEADS))
            for h in HEADS:
                dq_ref[h, pl.ds(t0, QB), :] = (dqs[h] * SCALE).astype(BF16)
            return 0

        lax.fori_loop(0, nb, qblock, 0)
        dk_out[...] = dk_ref[...].astype(BF16)
        dv_out[...] = dv_ref[...].astype(BF16)

    spec = _head_spec(T)
    fq_spec, fk_spec = _fox_specs(T)
    acc = pltpu.VMEM((HEADS_PER_STEP, T, HEAD_DIM), F32)
    return _pcall_carrying(body, carry, name=name, grid=(H // HEADS_PER_STEP,),
                           in_specs=[spec] * 3 + [fq_spec, fk_spec, fq_spec, spec],
                           out_specs=[spec, spec, spec, fk_spec],
                           out_shape=[jax.ShapeDtypeStruct((H, T, HEAD_DIM), BF16)] * 3
                           + [jax.ShapeDtypeStruct((H, nb, QB), F32)],
                           scratch_shapes=[acc, acc])(q, k, v, fq, fk, lse, do)


def _forget_cumsum(flog, name):
    R, T = flog.shape
    nb = T // QB

    def body(x_ref, o_ref):
        row, col = _iota2((QB, QB))
        m_le = (row <= col).astype(BF16)
        carry = jnp.zeros((R, 1), F32)
        for b in range(nb):
            lf = _log_sigmoid(x_ref[:, b * QB:(b + 1) * QB])
            o_ref[:, b * QB:(b + 1) * QB] = _dot3(lf, m_le) + carry
            carry = carry + jnp.sum(lf, axis=1, keepdims=True)

    spec = pl.BlockSpec((R, T), lambda: (0, 0))
    return _pcall(body, name=name, in_specs=[spec], out_specs=spec,
                  out_shape=jax.ShapeDtypeStruct((R, T), F32))(flog)


def _forget_cumsum_bwd(flog, df, name):
    R, T = flog.shape
    nb = T // QB

    def body(x_ref, df_ref, o_ref):
        row, col = _iota2((QB, QB))
        m_ge = (row >= col).astype(BF16)
        carry = jnp.zeros((R, 1), F32)
        for b in reversed(range(nb)):
            dfb = df_ref[:, b * QB:(b + 1) * QB]
            dlog = _dot3(dfb, m_ge) + carry
            o_ref[:, b * QB:(b + 1) * QB] = dlog * _sigmoid(-x_ref[:, b * QB:(b + 1) * QB])
            carry = carry + jnp.sum(dfb, axis=1, keepdims=True)

    spec = pl.BlockSpec((R, T), lambda: (0, 0))
    return _pcall(body, name=name, in_specs=[spec, spec], out_specs=spec,
                  out_shape=jax.ShapeDtypeStruct((R, T), F32))(flog, df)


def _chunk_probs(qc, kb, bias, c, col):
    z = _dot_nt(qc, kb) * SCALE + bias
    z = jnp.where((c - (LEFT_CHUNKS + 1)) * CHUNK + col >= jnp.maximum(0, (c - LEFT_CHUNKS) * CHUNK), z, NEG)
    p = jnp.exp(z - jnp.max(z, axis=1, keepdims=True))
    return p, jnp.sum(p, axis=1, keepdims=True)


def _chunk_specs(T, n=CHUNK_HEADS_PER_STEP):
    return (_head_spec(T, heads=n), _head_spec(T, rows=T + BAND_PAD, heads=n),
            _head_spec(T, rows=CHUNK, width=BAND, heads=n))


def _chunk_fwd(q, kp, vp, bias, name, carry=None):
    H, T, _ = q.shape
    nc = T // CHUNK

    def body(q_ref, kp_ref, vp_ref, bias_ref, o_ref):
        _, col = _iota2((CHUNK, BAND))

        def chunk(c, _):
            t0 = pl.multiple_of(c * CHUNK, CHUNK)
            for h in HEADS:
                qc = q_ref[h, pl.ds(t0, CHUNK), :].astype(BF16)
                kb = kp_ref[h, pl.ds(t0, BAND), :].astype(BF16)
                vb = vp_ref[h, pl.ds(t0, BAND), :].astype(BF16)
                p, l = _chunk_probs(qc, kb, bias_ref[h], c, col)
                o = jnp.dot(p.astype(BF16), vb, preferred_element_type=F32) / l
                o_ref[h, pl.ds(t0, CHUNK), :] = o.astype(BF16)
            return 0

        lax.fori_loop(0, nc, chunk, 0)

    q_spec, kp_spec, b_spec = _chunk_specs(T, HEADS_PER_STEP)
    return _pcall_carrying(body, carry, name=name, grid=(H // HEADS_PER_STEP,),
                           in_specs=[q_spec, kp_spec, kp_spec, b_spec], out_specs=q_spec,
                           out_shape=jax.ShapeDtypeStruct((H, T, HEAD_DIM), BF16))(q, kp, vp, bias)


def _chunk_bwd(q, kp, vp, bias, do, name, carry=None):
    H, T, _ = q.shape
    nc = T // CHUNK

    def body(q_ref, kp_ref, vp_ref, bias_ref, do_ref, dq_ref, dk_out, dv_out, db_ref, dkp_ref, dvp_ref):
        _, col = _iota2((CHUNK, BAND))
        dkp_ref[...] = jnp.zeros_like(dkp_ref)
        dvp_ref[...] = jnp.zeros_like(dvp_ref)
        db_ref[...] = jnp.zeros_like(db_ref)

        def chunk(c, _):
            t0 = pl.multiple_of(c * CHUNK, CHUNK)
            for h in CHUNK_HEADS:
                qc = q_ref[h, pl.ds(t0, CHUNK), :].astype(BF16)
                kb = kp_ref[h, pl.ds(t0, BAND), :].astype(BF16)
                vb = vp_ref[h, pl.ds(t0, BAND), :].astype(BF16)
                dob = do_ref[h, pl.ds(t0, CHUNK), :].astype(BF16)
                p, l = _chunk_probs(qc, kb, bias_ref[h], c, col)
                p = p / l
                dp = _dot_nt(dob, vb)
                dz = p * (dp - jnp.sum(p * dp, axis=1, keepdims=True))
                dzb = dz.astype(BF16)
                dq = jnp.dot(dzb, kb, preferred_element_type=F32) * SCALE
                dq_ref[h, pl.ds(t0, CHUNK), :] = dq.astype(BF16)
                dkp_ref[h, pl.ds(t0, BAND), :] += _dot_tn(dzb, qc) * SCALE
                dvp_ref[h, pl.ds(t0, BAND), :] += _dot_tn(p.astype(BF16), dob)
                db_ref[h] += dz
            return 0

        lax.fori_loop(0, nc, chunk, 0)
        dk_out[...] = dkp_ref[:, BAND_PAD:, :].astype(BF16)
        dv_out[...] = dvp_ref[:, BAND_PAD:, :].astype(BF16)

    q_spec, kp_spec, b_spec = _chunk_specs(T)
    acc = pltpu.VMEM((CHUNK_HEADS_PER_STEP, T + BAND_PAD, HEAD_DIM), F32)
    return _pcall_carrying(body, carry, name=name, grid=(H // CHUNK_HEADS_PER_STEP,),
                           in_specs=[q_spec, kp_spec, kp_spec, b_spec, q_spec],
                           out_specs=[q_spec, q_spec, q_spec, b_spec],
                           out_shape=[jax.ShapeDtypeStruct((H, T, HEAD_DIM), BF16)] * 3
                           + [jax.ShapeDtypeStruct((H, CHUNK, BAND), F32)],
                           scratch_shapes=[acc, acc])(q, kp, vp, bias, do)


HBM_SPEC = pl.BlockSpec(memory_space=pltpu.HBM)


def _position():
    return lax.axis_index("x"), lax.axis_index("y"), lax.axis_index("c")


def _other_chips(x, y):
    chips = [(1 - x, y), (x, 1 - y), (1 - x, 1 - y)]
    return [(chip, 2 * chip[0] + chip[1]) for chip in chips]


def _remote_copy(src, dst, send_sems, recv_sems, k, to):
    return pltpu.make_async_remote_copy(src_ref=src, dst_ref=dst, send_sem=send_sems.at[k], recv_sem=recv_sems.at[k],
                                        device_id=to, device_id_type=MESH)


def _place_own(w, chip, name):
    _, r, c = w.shape
    tr = _pick(r, 256, 16)

    def body(chip_ref, w_ref, *out_refs):
        for l, o_ref in enumerate(out_refs):
            o_ref[...] = w_ref[l].astype(BF16)

    grid_spec = pltpu.PrefetchScalarGridSpec(
        num_scalar_prefetch=1, grid=(r // tr,), in_specs=[pl.BlockSpec((DEPTH, tr, c), lambda i, ch: (0, i, 0))],
        out_specs=[pl.BlockSpec((None, tr, c), lambda i, ch: (ch[0], i, 0))] * DEPTH)
    return _pcall(body, name=name, grid_spec=grid_spec,
                  out_shape=[jax.ShapeDtypeStruct((N_CHIPS, r, c), BF16)] * DEPTH)(chip, w)


def _gather_over_ici(srcs, bufs, new):
    x, y, c = _position()
    me = 2 * x + y
    return [(b.at[me, c], b.at[me, c], (*chip, c)) for b in bufs for chip, _ in _other_chips(x, y)]


def _gather_to_sibling(srcs, bufs, new):
    x, y, c = _position()
    return [(b.at[idx, c], b.at[idx, c], (x, y, 1 - c)) for b in bufs for _, idx in _other_chips(x, y)]


def _gather_layer(bufs, name):
    n = len(bufs)

    def body(*refs):
        dst = refs[n:2 * n]
        send_sems, recv_sems = refs[2 * n:]
        x, y, c = _position()
        me = 2 * x + y
        sibling = (x, y, 1 - c)
        others = _other_chips(x, y)
        first = [_remote_copy(dst[i].at[me, c], dst[i].at[me, c], send_sems, recv_sems, 3 * i + k, (*chip, c))
                 for i in range(n) for k, (chip, _) in enumerate(others)]
        for cp in first:
            cp.start()
        passed = []
        for i in range(n):
            for k, (_, idx) in enumerate(others):
                landed = dst[i].at[idx, c]
                _remote_copy(landed, landed, send_sems, recv_sems, 3 * i + k, sibling).wait_recv()
                passed.append(_remote_copy(landed, landed, send_sems, recv_sems, 3 * (n + i) + k, sibling))
                passed[-1].start()
        for i in range(n):
            for k, (_, idx) in enumerate(others):
                from_sibling = dst[i].at[idx, 1 - c]
                _remote_copy(from_sibling, from_sibling, send_sems, recv_sems, 3 * (n + i) + k, sibling).wait_recv()
        for cp in first + passed:
            cp.wait_send()

    return _pcall(
        body, name=name, in_specs=[HBM_SPEC] * n, out_specs=[HBM_SPEC] * n,
        out_shape=[jax.ShapeDtypeStruct(b.shape, b.dtype) for b in bufs],
        scratch_shapes=[pltpu.SemaphoreType.DMA((6 * n,)), pltpu.SemaphoreType.DMA((6 * n,))],
        input_output_aliases={i: i for i in range(n)},
    )(*bufs)


def _send_other_half(parts, name):
    n = len(parts)

    def body(*refs):
        src, got = refs[:n], refs[n:2 * n]
        send_sems, recv_sems = refs[2 * n:]
        x, y, c = _position()
        copies = [_remote_copy(src[i].at[1 - c], got[i], send_sems, recv_sems, i, (x, y, 1 - c)) for i in range(n)]
        for cp in copies:
            cp.start()
        for cp in copies:
            cp.wait()

    return _pcall(
        body, name=name, in_specs=[HBM_SPEC] * n, out_specs=[HBM_SPEC] * n,
        out_shape=[jax.ShapeDtypeStruct(p.shape[1:], p.dtype) for p in parts],
        scratch_shapes=[pltpu.SemaphoreType.DMA((n,)), pltpu.SemaphoreType.DMA((n,))],
    )(*parts)


def _scatter_chips(parts, name):
    n = len(parts)

    def body(*refs):
        src, dst = refs[:n], refs[n:2 * n]
        send_sems, recv_sems = refs[2 * n:]
        x, y, c = _position()
        others = _other_chips(x, y)
        sends = [_remote_copy(src[i].at[idx], dst[i].at[k], send_sems, recv_sems, 3 * i + k, (*chip, c))
                 for i in range(n) for k, (chip, idx) in enumerate(others)]
        for cp in sends:
            cp.start()
        for cp in sends:
            cp.wait()

    return _pcall(
        body, name=name, in_specs=[HBM_SPEC] * n, out_specs=[HBM_SPEC] * n,
        out_shape=[jax.ShapeDtypeStruct((3,) + p.shape[1:], p.dtype) for p in parts],
        scratch_shapes=[pltpu.SemaphoreType.DMA((3 * n,)), pltpu.SemaphoreType.DMA((3 * n,))],
    )(*parts)


def _swap_with_sibling(arrs, name):
    n = len(arrs)

    def body(*refs):
        src, dst = refs[:n], refs[n:2 * n]
        send_sems, recv_sems = refs[2 * n:]
        x, y, c = _position()
        copies = [_remote_copy(src[i], dst[i], send_sems, recv_sems, i, (x, y, 1 - c)) for i in range(n)]
        for cp in copies:
            cp.start()
        for cp in copies:
            cp.wait()

    return _pcall(
        body, name=name, in_specs=[HBM_SPEC] * n, out_specs=[HBM_SPEC] * n,
        out_shape=[jax.ShapeDtypeStruct(a.shape, a.dtype) for a in arrs],
        scratch_shapes=[pltpu.SemaphoreType.DMA((n,)), pltpu.SemaphoreType.DMA((n,))],
    )(*arrs)


def _allreduce_small(v, name):
    R, C = v.shape

    def body(v_ref, o_ref, slots, send_sems, recv_sems):
        x, y, c = _position()
        me = 4 * x + 2 * y + c
        slots[0] = v_ref[...]
        sends = []
        for r in range(1, 8):
            fx, fy, fc = (r >> 2) & 1, (r >> 1) & 1, r & 1
            to = (x ^ fx, y ^ fy, c ^ fc)
            cp = pltpu.make_async_remote_copy(src_ref=v_ref, dst_ref=slots.at[r], send_sem=send_sems.at[r - 1],
                                              recv_sem=recv_sems.at[r - 1], device_id=to, device_id_type=MESH)
            cp.start()
            sends.append(cp)
        for cp in sends:
            cp.wait()
        acc = slots[me]
        for d in range(1, 8):
            acc = acc + slots[d ^ me]
        o_ref[...] = acc

    vmem = pl.BlockSpec(memory_space=pltpu.VMEM)
    return _pcall(
        body, name=name, in_specs=[vmem], out_specs=vmem, out_shape=jax.ShapeDtypeStruct((R, C), F32),
        scratch_shapes=[pltpu.VMEM((8, R, C), F32), pltpu.SemaphoreType.DMA((7,)), pltpu.SemaphoreType.DMA((7,))],
    )(v)


def _add_pair(which, both, got, name):
    _, N, R, C = both.shape
    tr = _pick(R, 512, 16)

    def body(which_ref, a_ref, b_ref, o_ref):
        o_ref[...] = (a_ref[...].astype(F32) + b_ref[...].astype(F32)).astype(BF16)

    spec = pl.BlockSpec((None, tr, C), lambda n, i, w: (n, i, 0))
    grid_spec = pltpu.PrefetchScalarGridSpec(
        num_scalar_prefetch=1, grid=(N, R // tr),
        in_specs=[pl.BlockSpec((None, None, tr, C), lambda n, i, w: (w[0], n, i, 0)), spec], out_specs=spec)
    return _pcall(body, name=name, grid_spec=grid_spec,
                  out_shape=jax.ShapeDtypeStruct((N, R, C), BF16))(which, both, got)


def _sum_chips(which, own, others, name):
    N, R, C = others.shape
    tr = _pick(R, 512, 16)

    def body(which_ref, own_ref, p_ref, o_ref):
        acc = own_ref[...].astype(F32)
        for n in range(N):
            acc = acc + p_ref[n].astype(F32)
        o_ref[...] = acc

    grid_spec = pltpu.PrefetchScalarGridSpec(
        num_scalar_prefetch=1, grid=(R // tr,),
        in_specs=[pl.BlockSpec((None, tr, C), lambda i, w: (w[0], i, 0)), pl.BlockSpec((N, tr, C), lambda i, w: (0, i, 0))],
        out_specs=pl.BlockSpec((tr, C), lambda i, w: (i, 0)))
    return _pcall(body, name=name, grid_spec=grid_spec,
                  out_shape=jax.ShapeDtypeStruct((R, C), F32))(which, own, others)


BIG = ("w_ffn1_in", "w_ffn1_out", "w_in", "w_br_sb", "w_br_ch", "w_br_fox", "w_out", "w_ffn2_in", "w_ffn2_out")
ROW_SHARDED = ("w_ffn1_out", "w_out", "w_ffn2_out")
SMALL = ("g_ffn1", "g_mix", "b_in", "rel_bias", "g_ffn2", "g_final")
SHARD_SHAPES = {
    "w_ffn1_in": (D_MODEL, 2 * D_FF // N_CHIPS), "w_ffn1_out": (D_FF // N_CHIPS, D_MODEL),
    "w_in": (D_MODEL, IN_WIDTH // N_CHIPS), "w_br_sb": (W_SB, D_MODEL // N_CHIPS),
    "w_br_ch": (W_CH, D_MODEL // N_CHIPS), "w_br_fox": (W_FOX, D_MODEL // N_CHIPS),
    "w_out": (D_MODEL // N_CHIPS, D_MODEL), "w_ffn2_in": (D_MODEL, 2 * D_FF // N_CHIPS),
    "w_ffn2_out": (D_FF // N_CHIPS, D_MODEL),
}


def _pair_sums(split, tag):
    core = lax.axis_index("c").astype(jnp.int32)[None]
    got = _send_other_half(split, f"grad_split_{tag}")
    return [_add_pair(core, a, b, f"grad_pair_sum_{tag}_{i}") for i, (a, b) in enumerate(zip(split, got))]


def _scatter_plan(srcs, bufs, new):
    x, y, c = _position()
    return [(s.at[idx], d.at[k], (*chip, c)) for s, d in zip(srcs, new) for k, (chip, idx) in enumerate(_other_chips(x, y))]


def _scatter_carry(pair):
    landing = [jax.ShapeDtypeStruct((3,) + p.shape[1:], p.dtype) for p in pair]
    return _Carry(pair, [], landing, 3 * len(pair), _scatter_plan)


def _finish_grads(pair, landed, axes, tag):
    chip = (2 * lax.axis_index("x") + lax.axis_index("y")).astype(jnp.int32)[None]
    mine = [_sum_chips(chip, p, q, f"grad_chip_sum_{tag}_{i}") for i, (p, q) in enumerate(zip(pair, landed))]
    theirs = _swap_with_sibling(mine, f"grad_share_{tag}")
    first = lax.axis_index("c") == 0
    return [jnp.concatenate([jnp.where(first, a, b), jnp.where(first, b, a)], axis=ax)
            for a, b, ax in zip(mine, theirs, axes)]


SMALL_SIZES = {"g_ffn1": DEPTH * D_MODEL, "g_mix": DEPTH * D_MODEL, "b_in": DEPTH * IN_WIDTH,
               "rel_bias": DEPTH * N_REL * H_CH, "g_ffn2": DEPTH * D_MODEL, "g_final": D_MODEL}
SMALL_TOTAL = sum(SMALL_SIZES.values()) + 1
SMALL_ROWS = -(-SMALL_TOTAL // (8 * 128)) * 8


def _pack_small(vals, extra):
    flat = [vals[n].reshape(-1) for n in SMALL] + [extra.reshape(-1)]
    flat.append(jnp.zeros((SMALL_ROWS * 128 - SMALL_TOTAL,), F32))
    return jnp.concatenate(flat).reshape(SMALL_ROWS, 128)


def _unpack_small(pack, shapes):
    flat, out, off = pack.reshape(-1), {}, 0
    for n in SMALL:
        out[n] = flat[off:off + SMALL_SIZES[n]].reshape(shapes[n])
        off += SMALL_SIZES[n]
    return out, flat[off]


def _to_heads(t, n_heads):
    return jnp.transpose(t.reshape(t.shape[0], n_heads, HEAD_DIM), (1, 0, 2)).astype(BF16)


def _from_heads(t):
    return jnp.transpose(t, (1, 0, 2)).reshape(t.shape[1], t.shape[0] * HEAD_DIM)


def _pad_keys(t):
    return jnp.pad(t, ((0, 0), (BAND_PAD, 0), (0, 0)))


DIAGONALS = CHUNK + BAND - 1


def _band_bias(table):
    n_far = (LEFT_CHUNKS + 1) * CHUNK + CHUNK - MAX_REL
    near = table[N_REL - 1 - (DIAGONALS - n_far):N_REL - 1][::-1]
    diag = jnp.concatenate([jnp.broadcast_to(table[N_REL - 1], (n_far, H_CH)), near], axis=0).T
    diag = jnp.pad(diag, ((0, 0), (0, 1)))
    skew = jnp.tile(diag, (1, CHUNK))[:, :CHUNK * DIAGONALS].reshape(H_CH, CHUNK, DIAGONALS)
    return skew[:, :, CHUNK - 1:]


def _pad_w_in(w):
    qkv, f, gates = w[:, :QKV_WIDTH], w[:, QKV_WIDTH:QKV_WIDTH + H_FOX], w[:, QKV_WIDTH + H_FOX:]
    return jnp.concatenate([qkv, gates, f, jnp.zeros((w.shape[0], F_PAD - H_FOX), w.dtype)], axis=1)


def _unpad_w_in(w):
    return jnp.concatenate([w[:, :QKV_WIDTH], w[:, QKV_WIDTH + GATE_WIDTH:QKV_WIDTH + GATE_WIDTH + H_FOX],
                            w[:, QKV_WIDTH:QKV_WIDTH + GATE_WIDTH]], axis=1)


QKV_SPLITS = np.cumsum([0, W_SB, W_SB, W_SB, W_CH, W_CH, W_CH, W_FOX, W_FOX, W_FOX])


def _by_chip_rows(g):
    return g.reshape(2, N_CHIPS, g.shape[1] // N_CHIPS, g.shape[2])


def _ffn_fwd(x, g, w_in, w_out, tag, carries):
    h = _rmsnorm_fwd(x, g, f"{tag}_norm")
    gate, up, a = _ffn_in_swiglu(h, w_in, f"{tag}_in", _carry_of(carries, "in"))
    y = _mm(a, w_out, mode="nn", name=f"{tag}_out", alpha=0.5, res=x, gathered="row",
            carry=_carry_of(carries, "out"))
    return y, (h, gate, up, a)


def _ffn_bwd(x, g, w_in, w_out, saved, dy, tag, carries):
    h, gate, up, a = saved
    da = _mm(dy, w_out, mode="nt", name=f"{tag}_da", alpha=0.5, gathered="row", out_dtype=BF16,
             carry=_carry_of(carries, "da"))
    dw_out = _mm(a, dy, mode="tn", name=f"{tag}_dwout", alpha=0.5, tm_cap=1408, out_dtype=BF16, out_split="row",
                 carry=_carry_of(carries, "dwout"))
    dgu = _swiglu_bwd(gate, up, da, f"{tag}_dact")
    dw_in = _mm(h, dgu, mode="tn", name=f"{tag}_dwin", tm_cap=512, out_dtype=BF16, out_split="col",
                carry=_carry_of(carries, "dwin"))
    dh = _mm(dgu, w_in, mode="nt", name=f"{tag}_dh", gathered="col", tn_cap=1024, carry=_carry_of(carries, "dh"))
    dx, dg = _rmsnorm_bwd(x, g, dh, dy, f"{tag}_dnorm")
    return dx, dg, dw_in, _by_chip_rows(dw_out)


def _mixer_fwd(x, lw, tag, carries):
    T = x.shape[0]
    h = _rmsnorm_fwd(x, lw["g_mix"], f"{tag}_norm")
    p = _mm(h, lw["w_in"], mode="nn", name=f"{tag}_proj", bias=lw["b_in"], tn_cap=896,
            carry=_carry_of(carries, "proj"))
    parts = [p[:, QKV_SPLITS[i]:QKV_SPLITS[i + 1]] for i in range(9)]
    qa, ka, va = (_to_heads(t, H_SB) for t in parts[0:3])
    qb, kb, vb = (_to_heads(t, H_CH) for t in parts[3:6])
    qc, kc, vc = (_to_heads(t, H_FOX) for t in parts[6:9])
    flog = _mm(lw["w_forget_t"], h, mode="nt", name=f"{tag}_flog")[:8] + lw["b_forget"]
    fcum = _forget_cumsum(flog, f"{tag}_fcum")[:H_FOX]
    fq, fk = fcum.reshape(H_FOX, T, 1), fcum.reshape(H_FOX, T // QB, QB)
    kbp, vbp = _pad_keys(kb), _pad_keys(vb)
    oa = _sb_fwd(qa, ka, va, f"{tag}_sb", _carry_of(carries, "sb"))
    ob = _chunk_fwd(qb, kbp, vbp, lw["bias"], f"{tag}_ch", _carry_of(carries, "ch"))
    oc, lse = _fox_fwd(qc, kc, vc, fq, fk, f"{tag}_fox", _carry_of(carries, "fox"))
    oam, obm, ocm = _from_heads(oa), _from_heads(ob), _from_heads(oc)
    ya = _mm(oam, lw["w_br_sb"], mode="nn", name=f"{tag}_bra", gathered="col")
    yb = _mm(obm, lw["w_br_ch"], mode="nn", name=f"{tag}_brb", gathered="col")
    yc = _mm(ocm, lw["w_br_fox"], mode="nn", name=f"{tag}_brc", gathered="col")
    merged = _gate_fwd(p, ya, yb, yc, f"{tag}_gate")
    y = _mm(merged, lw["w_out"], mode="nn", name=f"{tag}_out", res=x, gathered="row")
    saved = (h, p, (qa, ka, va), (qb, kbp, vbp), (qc, kc, vc), flog, fq, fk, lse, (oam, obm, ocm),
             (ya, yb, yc), merged)
    return y, saved


def _mixer_bwd(x, lw, saved, dy, tag, carries):
    (h, p, (qa, ka, va), (qb, kbp, vbp), (qc, kc, vc), flog, fq, fk, lse, (oam, obm, ocm),
     (ya, yb, yc), merged) = saved
    T = x.shape[0]
    grads = {}
    col, row = "col", "row"
    dmerged = _mm(dy, lw["w_out"], mode="nt", name=f"{tag}_dmerged", gathered=row)
    grads["w_out"] = _by_chip_rows(_mm(merged, dy, mode="tn", name=f"{tag}_dwout", tm_cap=1024, out_dtype=BF16,
                                       out_split="row"))
    dya, dyb, dyc, dgate = _gate_bwd(p, ya, yb, yc, dmerged, f"{tag}_dgate")
    doa = _mm(dya, lw["w_br_sb"], mode="nt", name=f"{tag}_doa", gathered=col)
    dob = _mm(dyb, lw["w_br_ch"], mode="nt", name=f"{tag}_dob", gathered=col)
    doc = _mm(dyc, lw["w_br_fox"], mode="nt", name=f"{tag}_doc", gathered=col)
    by_chip = dict(mode="tn", tm_cap=512, out_dtype=BF16, out_split="col")
    grads["w_br_sb"] = _mm(oam, dya, name=f"{tag}_dwbra", **by_chip)
    grads["w_br_ch"] = _mm(obm, dyb, name=f"{tag}_dwbrb", **by_chip)
    grads["w_br_fox"] = _mm(ocm, dyc, name=f"{tag}_dwbrc", **by_chip)
    dqa, dka, dva = _sb_bwd(qa, ka, va, _to_heads(doa, H_SB), f"{tag}_dsb", _carry_of(carries, "dsb"))
    dqb, dkb, dvb, dbias = _chunk_bwd(qb, kbp, vbp, lw["bias"], _to_heads(dob, H_CH), f"{tag}_dch",
                                      _carry_of(carries, "dch"))
    dqc, dkc, dvc, dfk = _fox_bwd(qc, kc, vc, fq, fk, lse, _to_heads(doc, H_FOX), f"{tag}_dfox",
                                  _carry_of(carries, "dfox"))
    dflog = _forget_cumsum_bwd(flog, jnp.pad(dfk.reshape(H_FOX, T), ((0, 8 - H_FOX), (0, 0))), f"{tag}_dfcum")
    dqkv = [_from_heads(t) for t in (dqa, dka, dva, dqb, dkb, dvb, dqc, dkc, dvc)]
    dforget = jnp.pad(dflog[:H_FOX].T, ((0, 0), (0, F_PAD - H_FOX))).astype(BF16)
    dpb = jnp.concatenate(dqkv + [dgate, dforget], axis=1)
    grads["b_in"] = _colsum(dpb, f"{tag}_dbin")
    dw_in =_unpad_w_in(_mm(h, dpb, mode="tn", name=f"{tag}_dwin", tm_cap=512, tn_cap=896, out_dtype=BF16))
    grads["w_in"] = jnp.transpose(dw_in.reshape(2, D_MODEL // 2, N_CHIPS, IN_WIDTH // N_CHIPS), (0, 2, 1, 3))
    dh = _mm(dpb, lw["w_in"], mode="nt", name=f"{tag}_dh", tk_cap=896, tn_cap=1024)
    dx, grads["g_mix"] = _rmsnorm_bwd(x, lw["g_mix"], dh, dy, f"{tag}_dnorm")
    grads["rel_bias"] = lw["bias_vjp"](dbias)[0]
    return dx, grads


def kernel(x, g_ffn1, w_ffn1_in, w_ffn1_out, g_mix, w_in, b_in, rel_bias, w_br_sb, w_br_ch, w_br_fox, w_out, g_ffn2, w_ffn2_in, w_ffn2_out, g_final, loss_target, m_g_ffn1, m_w_ffn1_in, m_w_ffn1_out, m_g_mix, m_w_in, m_b_in, m_rel_bias, m_w_br_sb, m_w_br_ch, m_w_br_fox, m_w_out, m_g_ffn2, m_w_ffn2_in, m_w_ffn2_out, m_g_final, v_g_ffn1, v_w_ffn1_in, v_w_ffn1_out, v_g_mix, v_w_in, v_b_in, v_rel_bias, v_w_br_sb, v_w_br_ch, v_w_br_fox, v_w_out, v_g_ffn2, v_w_ffn2_in, v_w_ffn2_out, v_g_final):
    names = ("g_ffn1", "w_ffn1_in", "w_ffn1_out", "g_mix", "w_in", "b_in", "rel_bias", "w_br_sb", "w_br_ch",
             "w_br_fox", "w_out", "g_ffn2", "w_ffn2_in", "w_ffn2_out", "g_final")
    w = dict(zip(names, (g_ffn1, w_ffn1_in, w_ffn1_out, g_mix, w_in, b_in, rel_bias, w_br_sb, w_br_ch,
                         w_br_fox, w_out, g_ffn2, w_ffn2_in, w_ffn2_out, g_final)))
    m = dict(zip(names, (m_g_ffn1, m_w_ffn1_in, m_w_ffn1_out, m_g_mix, m_w_in, m_b_in, m_rel_bias, m_w_br_sb,
                         m_w_br_ch, m_w_br_fox, m_w_out, m_g_ffn2, m_w_ffn2_in, m_w_ffn2_out, m_g_final)))
    v = dict(zip(names, (v_g_ffn1, v_w_ffn1_in, v_w_ffn1_out, v_g_mix, v_w_in, v_b_in, v_rel_bias, v_w_br_sb,
                         v_w_br_ch, v_w_br_fox, v_w_out, v_g_ffn2, v_w_ffn2_in, v_w_ffn2_out, v_g_final)))
    xs = x[0]
    tgt = loss_target[0]

    chip = (2 * lax.axis_index("x") + lax.axis_index("y")).astype(jnp.int32)[None]
    placed = [_place_own(w[n], chip, f"place_{n}") for n in BIG]
    bufs = [[p[l].reshape(N_CHIPS, 2, p[l].shape[1] // 2, p[l].shape[2]) for p in placed] for l in range(DEPTH)]
    padded_w_in = {}

    def layer_weights(l):
        lw = {n: b.reshape((N_CHIPS,) + w[n].shape[1:]) for n, b in zip(BIG, bufs[l])}
        if l not in padded_w_in:
            whole = jnp.concatenate([lw["w_in"][j] for j in range(N_CHIPS)], axis=1)
            forget_t = jnp.pad(whole[:, QKV_WIDTH:QKV_WIDTH + H_FOX].T, ((0, F_PAD - H_FOX), (0, 0)))
            padded_w_in[l] = (_pad_w_in(whole), forget_t)
        lw["w_in"], lw["w_forget_t"] = padded_w_in[l]
        lw["b_forget"] = jnp.pad(w["b_in"][l][QKV_WIDTH:QKV_WIDTH + H_FOX], (0, 8 - H_FOX))[:, None]
        lw["b_in"] = _pad_w_in(w["b_in"][l][None, :])
        lw["bias"], lw["bias_vjp"] = jax.vjp(_band_bias, w["rel_bias"][l])
        for n in ("g_ffn1", "g_mix", "g_ffn2"):
            lw[n] = w[n][l][None, :]
        return lw

    def landed_in(l, idx):
        def done(carry):
            for i, b in zip(idx, carry.out[0]):
                bufs[l][i] = b
        return done

    def over_ici(l, idx):
        return lambda: _Carry([], [bufs[l][i] for i in idx], [], 3 * len(idx), _gather_over_ici, landed_in(l, idx))

    def to_sibling(l, idx):
        return lambda: _Carry([], [bufs[l][i] for i in idx], [], 3 * len(idx), _gather_to_sibling, landed_in(l, idx))

    def ffn_fwd(x_in, lw, which, tag, carries):
        return _ffn_fwd(x_in, lw[f"g_{which}"], lw[f"w_{which}_in"], lw[f"w_{which}_out"], tag, carries)

    def ffn_bwd(x_in, lw, which, saved_acts, dy, tag, carries):
        return _ffn_bwd(x_in, lw[f"g_{which}"], lw[f"w_{which}_in"], lw[f"w_{which}_out"], saved_acts, dy, tag,
                        carries)

    FFN1, W_IN, MIXER_REST, FFN2_IN, FFN2_OUT = (0, 1), (2,), (3, 4, 5, 6), (7,), (8,)
    first = FFN1 + W_IN
    for i, b in zip(first, _gather_layer([bufs[0][i] for i in first], "gather_l0")):
        bufs[0][i] = b
    fwd_carries = [
        {"ffn1": {"in": [over_ici(0, MIXER_REST)], "out": [to_sibling(0, MIXER_REST), over_ici(0, FFN2_OUT)]},
         "mix": {"proj": [to_sibling(0, FFN2_OUT), over_ici(1, MIXER_REST)],
                 "sb": [over_ici(0, FFN2_IN), over_ici(1, FFN2_OUT)],
                 "ch": [to_sibling(0, FFN2_IN), over_ici(1, FFN1)],
                 "fox": [over_ici(1, W_IN)]},
         "ffn2": {"in": [to_sibling(1, MIXER_REST + FFN2_OUT + FFN1 + W_IN)]}},
        {"ffn1": {}, "mix": {"sb": [over_ici(1, FFN2_IN)], "ch": [to_sibling(1, FFN2_IN)]}, "ffn2": {}},
    ]

    saved = []
    act = xs
    for l in range(DEPTH):
        x0 = act
        x1, s1 = ffn_fwd(x0, layer_weights(l), "ffn1", f"l{l}_ffn1", fwd_carries[l]["ffn1"])
        x2, s2 = _mixer_fwd(x1, layer_weights(l), f"l{l}_mix", fwd_carries[l]["mix"])
        act, s3 = ffn_fwd(x2, layer_weights(l), "ffn2", f"l{l}_ffn2", fwd_carries[l]["ffn2"])
        saved.append((x0, s1, x1, s2, x2, s3))
    layers = [layer_weights(l) for l in range(DEPTH)]

    dact, dg_final, loss_row = _final_loss(act, w["g_final"][None, :], tgt, "final_loss")

    big_grads = {n: [None] * DEPTH for n in BIG}
    small_grads = {n: [None] * DEPTH for n in ("g_ffn1", "g_mix", "b_in", "rel_bias", "g_ffn2")}
    pair = [[None] * len(BIG) for _ in range(DEPTH)]
    landed = [[None] * len(BIG) for _ in range(DEPTH)]

    def pair_up(l, idx, tag):
        for i, p in zip(idx, _pair_sums([big_grads[BIG[i]][l] for i in idx], tag)):
            pair[l][i] = p

    def exchange(l, idx):
        def done(carry):
            for i, a in zip(idx, carry.out[1]):
                landed[l][i] = a
        def make():
            carry = _scatter_carry([pair[l][i] for i in idx])
            carry.done = done
            return carry
        return make

    def exchange_one_way(l, i, k):
        def plan(srcs, bufs, new):
            x, y, c = _position()
            chip_k, idx_k = _other_chips(x, y)[k]
            return [(srcs[0].at[idx_k], bufs[0].at[k], (*chip_k, c))]
        def done(carry):
            landed[l][i] = carry.out[0][0]
        def make():
            if landed[l][i] is None:
                landed[l][i] = lax.empty((3,) + pair[l][i].shape[1:], BF16)
            return _Carry([pair[l][i]], [landed[l][i]], [], 1, plan, done)
        return make

    bwd_carries = [
        {"ffn2": {},
         "mix": {"dsb": [exchange(1, FFN1 + W_IN)], "dch": [exchange(1, MIXER_REST + FFN2_IN + FFN2_OUT)],
                 "dfox": [exchange(0, FFN2_IN + FFN2_OUT)]},
         "ffn1": {"da": [exchange(0, MIXER_REST)], "dwout": [exchange_one_way(0, 2, 0)],
                  "dwin": [exchange_one_way(0, 2, 1)], "dh": [exchange_one_way(0, 2, 2)]}},
        {"ffn2": {}, "mix": {}, "ffn1": {}},
    ]
    for l in reversed(range(DEPTH)):
        lw = layers[l]
        x0, s1, x1, s2, x2, s3 = saved[l]
        dx2, small_grads["g_ffn2"][l], big_grads["w_ffn2_in"][l], big_grads["w_ffn2_out"][l] = ffn_bwd(
            x2, lw, "ffn2", s3, dact, f"l{l}_ffn2", bwd_carries[l]["ffn2"])
        if l == 0:
            pair_up(0, FFN2_IN + FFN2_OUT, "l0_ffn2")
        dx1, mg = _mixer_bwd(x1, lw, s2, dx2, f"l{l}_mix", bwd_carries[l]["mix"])
        for n in ("w_in", "w_br_sb", "w_br_ch", "w_br_fox", "w_out"):
            big_grads[n][l] = mg[n]
        small_grads["b_in"][l] = _unpad_w_in(mg["b_in"])[0]
        small_grads["g_mix"][l] = mg["g_mix"][0]
        small_grads["rel_bias"][l] = mg["rel_bias"]
        if l == 0:
            pair_up(0, W_IN + MIXER_REST, "l0_mix")
        dact, dg1, big_grads["w_ffn1_in"][l], big_grads["w_ffn1_out"][l] = ffn_bwd(
            x0, lw, "ffn1", s1, dx1, f"l{l}_ffn1", bwd_carries[l]["ffn1"])
        small_grads["g_ffn1"][l] = dg1[0]
        small_grads["g_ffn2"][l] = small_grads["g_ffn2"][l][0]
        if l == 1:
            pair_up(1, tuple(range(len(BIG))), "l1")
    grad_x = dact[None]
    pair_up(0, FFN1, "l0_ffn1")
    for i, a in zip(FFN1, _scatter_chips([pair[0][i] for i in FFN1], "grad_scatter_l0")):
        landed[0][i] = a

    small = {n: jnp.stack(small_grads[n]) for n in small_grads}
    small["g_final"] = dg_final[0]
    small_sum, loss = _unpack_small(_allreduce_small(_pack_small(small, loss_row[0, :1]), "allreduce_small"),
                                    {n: w[n].shape for n in SMALL})

    axes = [1 if n in ROW_SHARDED else 0 for n in BIG] * DEPTH
    summed = _finish_grads(pair[0] + pair[1], landed[0] + landed[1], axes, "all")
    grads = {n: jnp.stack([summed[i], summed[len(BIG) + i]]) for i, n in enumerate(BIG)}
    grads.update(small_sum)

    delta, new_m, new_v = {}, {}, {}
    for n in BIG:
        shape = w[n].shape
        flat = lambda t: t.reshape(-1, shape[-1])
        d, nm, nv = _adamw(flat(w[n]), flat(grads[n]), flat(m[n]), flat(v[n]), f"adamw_{n}")
        delta[n], new_m[n], new_v[n] = d.reshape(shape), nm.reshape(shape), nv.reshape(shape)
    zero = jnp.zeros((1,), F32)
    sd, sm, sv = _adamw(_pack_small(w, zero), _pack_small(grads, zero), _pack_small(m, zero), _pack_small(v, zero),
                        "adamw_small")
    shapes = {n: w[n].shape for n in SMALL}
    for dst, src in ((delta, sd), (new_m, sm), (new_v, sv)):
        dst.update(_unpack_small(src, shapes)[0])

    return (loss, grad_x, *[grads[n] for n in names], *[delta[n] for n in names],
            *[new_m[n] for n in names], *[new_v[n] for n in names])
```

```python
import functools

import numpy as np
import jax
import jax.numpy as jnp
from jax import lax
from jax.experimental import pallas as pl
from jax.experimental.pallas import tpu as pltpu

F32 = jnp.float32
BF16 = jnp.bfloat16

D_MODEL = 1024
DEPTH = 2
CHUNK = 64
HEAD_DIM = 64
H_SB, H_CH, H_FOX = 4, 8, 4
W_SB, W_CH, W_FOX = H_SB * HEAD_DIM, H_CH * HEAD_DIM, H_FOX * HEAD_DIM
LEFT_CHUNKS = 8
MAX_REL = 128
N_REL = 2 * MAX_REL + 1
D_FF = 2816
QKV_WIDTH = 3 * (W_SB + W_CH + W_FOX)
GATE_WIDTH = 3 * D_MODEL
IN_WIDTH = QKV_WIDTH + H_FOX + GATE_WIDTH
F_PAD = 128
P_WIDTH = QKV_WIDTH + GATE_WIDTH + F_PAD
RMS_EPS = 1e-6
NEG = -1e30
SCALE = HEAD_DIM ** -0.5
QB = 256
BAND = (LEFT_CHUNKS + 2) * CHUNK
BAND_PAD = BAND - CHUNK

ADAM_LR, ADAM_B1, ADAM_B2, ADAM_EPS, ADAM_WD, ADAM_STEP = 0.001, 0.9, 0.999, 1e-08, 0.01, 10

N_CHIPS = 4
PACK_COLS = 1024
VMEM_BUDGET = 52 * 1024 * 1024

NT = (((1,), (1,)), ((), ()))
TN = (((0,), (0,)), ((), ()))
NN = (((1,), (0,)), ((), ()))
MESH = pl.DeviceIdType.MESH


def _pcall(body, **kw):
    return pl.pallas_call(body, **kw)


class _Carry:
    def __init__(self, srcs, bufs, new, count, plan, done=None):
        self.srcs, self.bufs, self.new, self.count, self.plan = list(srcs), list(bufs), list(new), count, plan
        self.out, self.done = None, done

    @staticmethod
    def join(parts):
        parts = [p for p in parts if p is not None]
        if not parts:
            return None

        def plan(srcs, bufs, new):
            copies, s, b, n = [], 0, 0, 0
            for p in parts:
                copies += p.plan(srcs[s:s + len(p.srcs)], bufs[b:b + len(p.bufs)], new[n:n + len(p.new)])
                s, b, n = s + len(p.srcs), b + len(p.bufs), n + len(p.new)
            return copies

        whole = _Carry(sum((p.srcs for p in parts), []), sum((p.bufs for p in parts), []),
                       sum((p.new for p in parts), []), sum(p.count for p in parts), plan)
        whole.parts = parts
        return whole

    def share_out(self):
        b, n = 0, 0
        for p in getattr(self, "parts", []):
            p.out = (self.out[0][b:b + len(p.bufs)], self.out[1][n:n + len(p.new)])
            b, n = b + len(p.bufs), n + len(p.new)
            p.share_out()
        if self.done is not None:
            self.done(self)


def _carry_of(carries, key):
    return _Carry.join([make() for make in carries.get(key, [])])


def _pcall_carrying(body, carry, **kw):
    if carry is None:
        return _pcall(body, **kw)
    in_specs = list(kw.pop("in_specs"))
    out_specs, out_shape = kw.pop("out_specs"), kw.pop("out_shape")
    single = not isinstance(out_shape, (list, tuple))
    out_specs = [out_specs] if single else list(out_specs)
    out_shape = [out_shape] if single else list(out_shape)
    scratch = list(kw.pop("scratch_shapes", []))
    grid = tuple(kw.get("grid", ()))
    n_in, n_out, n_scr = len(in_specs), len(out_specs), len(scratch)
    ns, nb, nn = len(carry.srcs), len(carry.bufs), len(carry.new)

    def body2(*refs):
        ins, srcs = refs[:n_in], refs[n_in:n_in + ns]
        at = n_in + ns + nb
        outs, bufs, new = refs[at:at + n_out], refs[at + n_out:at + n_out + nb], refs[at + n_out + nb:at + n_out + nb + nn]
        at += n_out + nb + nn
        scr, (send_sems, recv_sems) = refs[at:at + n_scr], refs[at + n_scr:]

        def copies():
            return [_remote_copy(s, d, send_sems, recv_sems, k, to)
                    for k, (s, d, to) in enumerate(carry.plan(srcs, bufs, new))]

        def start():
            for cp in copies():
                cp.start()

        def wait():
            for cp in copies():
                cp.wait()

        if grid:
            ids = [pl.program_id(a) for a in range(len(grid))]
            pl.when(functools.reduce(jnp.logical_and, [i == 0 for i in ids]))(start)
        else:
            start()
        body(*ins, *outs, *scr)
        if grid:
            pl.when(functools.reduce(jnp.logical_and, [i == g - 1 for i, g in zip(ids, grid)]))(wait)
        else:
            wait()

    call = _pcall(
        body2, in_specs=in_specs + [HBM_SPEC] * (ns + nb), out_specs=out_specs + [HBM_SPEC] * (nb + nn),
        out_shape=out_shape + [jax.ShapeDtypeStruct(b.shape, b.dtype) for b in carry.bufs] + carry.new,
        scratch_shapes=scratch + [pltpu.SemaphoreType.DMA((carry.count,)), pltpu.SemaphoreType.DMA((carry.count,))],
        input_output_aliases={n_in + ns + j: n_out + j for j in range(nb)}, **kw)

    def apply(*args):
        res = call(*args, *carry.srcs, *carry.bufs)
        carry.out = (list(res[n_out:n_out + nb]), list(res[n_out + nb:]))
        carry.share_out()
        return res[0] if single else list(res[:n_out])

    return apply


def _pick(n, cap, mult=128):
    best = None
    d = mult
    while d <= min(n, cap):
        if n % d == 0:
            best = d
        d += mult
    return n if best is None else best


def _nbytes(shape, dtype):
    return int(np.prod(shape)) * jnp.dtype(dtype).itemsize


def _mm(a, b, *, mode, name, out_dtype=F32, alpha=1.0, bias=None, res=None, tm_cap=1024, tn_cap=512,
        tk_cap=2816, gathered=None, out_split=None, carry=None):
    if mode == "tn":
        K, M = a.shape
    else:
        M, K = a.shape
    dn = {"nn": NN, "nt": NT, "tn": TN}[mode]
    b_rows = None
    if gathered is None:
        N = b.shape[0] if mode == "nt" else b.shape[1]
        tn = {None: _pick(N, tn_cap), "col": N // N_CHIPS, "row": N // 2}[out_split]
        if out_split == "col":
            tm_cap = min(tm_cap, M // 2)
        tk = K if K <= tk_cap else _pick(K, tk_cap)
        b_spec = (pl.BlockSpec((tn, tk), lambda i, j, k: (j, k)) if mode == "nt"
                  else pl.BlockSpec((tk, tn), lambda i, j, k: (k, j)))
    else:
        r, c = b.shape[1:]
        rows, cols = (r, N_CHIPS * c) if gathered == "col" else (N_CHIPS * r, c)
        N = rows if mode == "nt" else cols
        assert K == (cols if mode == "nt" else rows), (name, K, rows, cols)
        if gathered == "col" and mode == "nn":
            tn, tk = c, (r if r <= tk_cap else _pick(r, tk_cap))
            b_spec = pl.BlockSpec((None, tk, c), lambda i, j, k: (j, k, 0))
        elif gathered == "col":
            tn, tk = _pick(r, tn_cap), c
            b_spec = pl.BlockSpec((None, tn, c), lambda i, j, k: (k, j, 0))
        elif mode == "nn":
            tn, tk, b_rows = _pick(c, tn_cap), K, N_CHIPS
            b_spec = pl.BlockSpec((N_CHIPS, r, tn), lambda i, j, k: (0, 0, j))
        else:
            tn, tk, b_rows = 2 * r, K, 2
            b_spec = pl.BlockSpec((2, r, c), lambda i, j, k: (j, 0, 0))
    tm = _pick(M, tm_cap)
    nk = K // tk

    a_spec = (pl.BlockSpec((tk, tm), lambda i, j, k: (k, i)) if mode == "tn"
              else pl.BlockSpec((tm, tk), lambda i, j, k: (i, k)))
    in_specs = [a_spec, b_spec]
    args = [a, b]
    if bias is not None:
        in_specs.append(pl.BlockSpec((1, tn), lambda i, j, k: (0, j)))
        args.append(bias)
    if res is not None:
        in_specs.append(pl.BlockSpec((tm, tn), lambda i, j, k: (i, j)))
        args.append(res)

    est = 2 * (_nbytes((tm, tk), a.dtype) + _nbytes((tk, tn), b.dtype) + _nbytes((tm, tn), out_dtype))
    est += _nbytes((tm, tn), F32) * (3 if res is not None else 1)
    assert est < VMEM_BUDGET, (name, est)

    def body(*refs):
        a_ref, b_ref = refs[0], refs[1]
        pos = 2
        bias_ref = res_ref = None
        if bias is not None:
            bias_ref = refs[pos]
            pos += 1
        if res is not None:
            res_ref = refs[pos]
            pos += 1
        o_ref = refs[pos]
        acc_ref = refs[pos + 1] if nk > 1 else None

        bv = b_ref[...]
        if b_rows is not None:
            bv = bv.reshape(b_rows * bv.shape[1], bv.shape[2])
        part = lax.dot_general(a_ref[...].astype(BF16), bv.astype(BF16), dn, preferred_element_type=F32)

        def finish(acc):
            if alpha != 1.0:
                acc = acc * alpha
            if bias_ref is not None:
                acc = acc + bias_ref[...]
            if res_ref is not None:
                acc = acc + res_ref[...]
            o_ref[...] = acc.astype(out_dtype)

        if nk == 1:
            finish(part)
        else:
            k = pl.program_id(2)

            @pl.when(k == 0)
            def _():
                acc_ref[...] = part

            @pl.when(k > 0)
            def _():
                acc_ref[...] += part

            @pl.when(k == nk - 1)
            def _():
                finish(acc_ref[...])

    if out_split == "col":
        per_half = M // 2 // tm
        out_spec = pl.BlockSpec((None, None, tm, tn), lambda i, j, k: (i // per_half, j, i % per_half, 0))
        out_shape = jax.ShapeDtypeStruct((2, N_CHIPS, M // 2, tn), out_dtype)
    elif out_split == "row":
        out_spec = pl.BlockSpec((None, tm, tn), lambda i, j, k: (j, i, 0))
        out_shape = jax.ShapeDtypeStruct((2, M, tn), out_dtype)
    else:
        out_spec = pl.BlockSpec((tm, tn), lambda i, j, k: (i, j))
        out_shape = jax.ShapeDtypeStruct((M, N), out_dtype)
    return _pcall_carrying(
        body, carry, name=name, grid=(M // tm, N // tn, nk), in_specs=in_specs, out_specs=out_spec,
        out_shape=out_shape,
        scratch_shapes=[pltpu.VMEM((tm, tn), F32)] if nk > 1 else [],
        compiler_params=pltpu.CompilerParams(dimension_semantics=("parallel", "parallel", "arbitrary")),
    )(*args)


def _rmsnorm_fwd(x, g, name):
    T, Dm = x.shape
    tm = _pick(T, 256, 8)

    def body(x_ref, g_ref, h_ref):
        xv = x_ref[...]
        rstd = lax.rsqrt(jnp.mean(xv * xv, axis=-1, keepdims=True) + RMS_EPS)
        h_ref[...] = (xv * rstd * g_ref[...]).astype(BF16)

    return _pcall(
        body, name=name, grid=(T // tm,),
        in_specs=[pl.BlockSpec((tm, Dm), lambda i: (i, 0)), pl.BlockSpec((1, Dm), lambda i: (0, 0))],
        out_specs=pl.BlockSpec((tm, Dm), lambda i: (i, 0)),
        out_shape=jax.ShapeDtypeStruct((T, Dm), BF16),
    )(x, g)


def _rmsnorm_bwd(x, g, dh, dres, name):
    T, Dm = x.shape
    tm = _pick(T, 256, 8)

    def body(x_ref, g_ref, dh_ref, dres_ref, dx_ref, dg_ref):
        xv = x_ref[...]
        rstd = lax.rsqrt(jnp.mean(xv * xv, axis=-1, keepdims=True) + RMS_EPS)
        xhat = xv * rstd
        dhv = dh_ref[...]
        dyg = dhv * g_ref[...]
        dx_ref[...] = dres_ref[...] + rstd * (dyg - xhat * jnp.mean(dyg * xhat, axis=-1, keepdims=True))
        part = jnp.sum(dhv * xhat, axis=0, keepdims=True)

        @pl.when(pl.program_id(0) == 0)
        def _():
            dg_ref[...] = part

        @pl.when(pl.program_id(0) > 0)
        def _():
            dg_ref[...] += part

    row = pl.BlockSpec((tm, Dm), lambda i: (i, 0))
    vec = pl.BlockSpec((1, Dm), lambda i: (0, 0))
    return _pcall(
        body, name=name, grid=(T // tm,), in_specs=[row, vec, row, row], out_specs=[row, vec],
        out_shape=[jax.ShapeDtypeStruct((T, Dm), F32), jax.ShapeDtypeStruct((1, Dm), F32)],
        compiler_params=pltpu.CompilerParams(dimension_semantics=("arbitrary",)),
    )(x, g, dh, dres)


def _final_loss(x, g, tgt, name):
    T, Dm = x.shape
    tm = _pick(T, 256, 8)

    def body(x_ref, g_ref, t_ref, dx_ref, dg_ref, loss_ref):
        xv = x_ref[...]
        gv = g_ref[...]
        rstd = lax.rsqrt(jnp.mean(xv * xv, axis=-1, keepdims=True) + RMS_EPS)
        xhat = xv * rstd
        err = xhat * gv - t_ref[...]
        part_loss = 0.5 * jnp.sum(jnp.mean(err * err, axis=-1, keepdims=True), axis=0, keepdims=True)
        dy = err * (1.0 / Dm)
        dyg = dy * gv
        dx_ref[...] = rstd * (dyg - xhat * jnp.mean(dyg * xhat, axis=-1, keepdims=True))
        part_g = jnp.sum(dy * xhat, axis=0, keepdims=True)
        part_l = jnp.broadcast_to(part_loss, (1, 128))

        @pl.when(pl.program_id(0) == 0)
        def _():
            dg_ref[...] = part_g
            loss_ref[...] = part_l

        @pl.when(pl.program_id(0) > 0)
        def _():
            dg_ref[...] += part_g
            loss_ref[...] += part_l

    row = pl.BlockSpec((tm, Dm), lambda i: (i, 0))
    vec = pl.BlockSpec((1, Dm), lambda i: (0, 0))
    return _pcall(
        body, name=name, grid=(T // tm,), in_specs=[row, vec, row],
        out_specs=[row, vec, pl.BlockSpec((1, 128), lambda i: (0, 0))],
        out_shape=[jax.ShapeDtypeStruct((T, Dm), F32), jax.ShapeDtypeStruct((1, Dm), F32),
                   jax.ShapeDtypeStruct((1, 128), F32)],
        compiler_params=pltpu.CompilerParams(dimension_semantics=("arbitrary",)),
    )(x, g, tgt)


def _sigmoid(z):
    return 1.0 / (1.0 + jnp.exp(-z))


def _ffn_in_swiglu(h, w_in, name, carry=None):
    T, Dm = h.shape
    cw = w_in.shape[2]
    tm = _pick(T, 512, 16)

    def body(h_ref, wg_ref, wu_ref, g_ref, u_ref, a_ref):
        hv = h_ref[...]
        gv = jnp.dot(hv, wg_ref[...], preferred_element_type=F32)
        uv = jnp.dot(hv, wu_ref[...], preferred_element_type=F32)
        g_ref[...] = gv.astype(BF16)
        u_ref[...] = uv.astype(BF16)
        a_ref[...] = (gv * _sigmoid(gv) * uv).astype(BF16)

    out = pl.BlockSpec((tm, cw), lambda i, j: (i, j))
    return _pcall_carrying(
        body, carry, name=name, grid=(T // tm, 2),
        in_specs=[pl.BlockSpec((tm, Dm), lambda i, j: (i, 0)), pl.BlockSpec((None, Dm, cw), lambda i, j: (j, 0, 0)),
                  pl.BlockSpec((None, Dm, cw), lambda i, j: (j + 2, 0, 0))],
        out_specs=[out, out, out], out_shape=[jax.ShapeDtypeStruct((T, D_FF), BF16)] * 3,
    )(h, w_in, w_in)


def _swiglu_bwd(g, u, da, name):
    T = g.shape[0]
    tm = _pick(T, 256, 16)

    def body(g_ref, u_ref, da_ref, d_ref):
        gv = g_ref[...].astype(F32)
        uv = u_ref[...].astype(F32)
        dav = da_ref[...].astype(F32)
        sg = _sigmoid(gv)
        d_ref[:, :D_FF] = (dav * uv * (sg + gv * sg * (1.0 - sg))).astype(BF16)
        d_ref[:, D_FF:] = (dav * gv * sg).astype(BF16)

    row = pl.BlockSpec((tm, D_FF), lambda i: (i, 0))
    return _pcall(
        body, name=name, grid=(T // tm,), in_specs=[row, row, row],
        out_specs=pl.BlockSpec((tm, 2 * D_FF), lambda i: (i, 0)),
        out_shape=jax.ShapeDtypeStruct((T, 2 * D_FF), BF16),
    )(g, u, da)


GATE_BLOCK0 = QKV_WIDTH // D_MODEL


def _gate_fwd(p, ya, yb, yc, name):
    T = p.shape[0]
    tm = _pick(T, 256, 8)

    def body(ga_ref, gb_ref, gc_ref, ya_ref, yb_ref, yc_ref, o_ref):
        o_ref[...] = (_sigmoid(ga_ref[...]) * ya_ref[...] + _sigmoid(gb_ref[...]) * yb_ref[...]
                      + _sigmoid(gc_ref[...]) * yc_ref[...]).astype(BF16)

    gate = [pl.BlockSpec((tm, D_MODEL), functools.partial(lambda i, kk: (i, GATE_BLOCK0 + kk), kk=kk))
            for kk in range(3)]
    row = pl.BlockSpec((tm, D_MODEL), lambda i: (i, 0))
    return _pcall(
        body, name=name, grid=(T // tm,), in_specs=gate + [row, row, row], out_specs=row,
        out_shape=jax.ShapeDtypeStruct((T, D_MODEL), BF16),
    )(p, p, p, ya, yb, yc)


def _gate_bwd(p, ya, yb, yc, dm, name):
    T = p.shape[0]
    tm = _pick(T, 256, 16)

    def body(ga_ref, gb_ref, gc_ref, ya_ref, yb_ref, yc_ref, dm_ref, da_ref, db_ref, dc_ref, dg_ref):
        dmv = dm_ref[...]
        for kk, (g_ref, y_ref, d_ref) in enumerate(((ga_ref, ya_ref, da_ref), (gb_ref, yb_ref, db_ref),
                                                    (gc_ref, yc_ref, dc_ref))):
            s = _sigmoid(g_ref[...])
            d_ref[...] = (s * dmv).astype(BF16)
            dg_ref[:, kk * D_MODEL:(kk + 1) * D_MODEL] = (dmv * y_ref[...] * s * (1.0 - s)).astype(BF16)

    gate = [pl.BlockSpec((tm, D_MODEL), functools.partial(lambda i, kk: (i, GATE_BLOCK0 + kk), kk=kk))
            for kk in range(3)]
    row = pl.BlockSpec((tm, D_MODEL), lambda i: (i, 0))
    return _pcall(
        body, name=name, grid=(T // tm,), in_specs=gate + [row, row, row, row],
        out_specs=[row, row, row, pl.BlockSpec((tm, GATE_WIDTH), lambda i: (i, 0))],
        out_shape=[jax.ShapeDtypeStruct((T, D_MODEL), BF16)] * 3 + [jax.ShapeDtypeStruct((T, GATE_WIDTH), BF16)],
    )(p, p, p, ya, yb, yc, dm)


def _colsum(a, name):
    T, N = a.shape
    tm = _pick(T, 256, 16)

    def body(a_ref, o_ref):
        part = jnp.sum(a_ref[...].astype(F32), axis=0, keepdims=True)

        @pl.when(pl.program_id(0) == 0)
        def _():
            o_ref[...] = part

        @pl.when(pl.program_id(0) > 0)
        def _():
            o_ref[...] += part

    return _pcall(
        body, name=name, grid=(T // tm,), in_specs=[pl.BlockSpec((tm, N), lambda i: (i, 0))],
        out_specs=pl.BlockSpec((1, N), lambda i: (0, 0)), out_shape=jax.ShapeDtypeStruct((1, N), F32),
        compiler_params=pltpu.CompilerParams(dimension_semantics=("arbitrary",)),
    )(a)


def _adamw(w, g, m, v, name):
    R, C = w.shape
    tr = 256 if R % 256 == 0 else R
    c1 = 1.0 / (1.0 - ADAM_B1 ** ADAM_STEP)
    c2 = 1.0 / (1.0 - ADAM_B2 ** ADAM_STEP)

    def body(w_ref, g_ref, m_ref, v_ref, d_ref, nm_ref, nv_ref):
        gv = g_ref[...]
        mn = ADAM_B1 * m_ref[...] + (1.0 - ADAM_B1) * gv
        vn = ADAM_B2 * v_ref[...] + (1.0 - ADAM_B2) * (gv * gv)
        nm_ref[...] = mn
        nv_ref[...] = vn
        d_ref[...] = -ADAM_LR * ((mn * c1) / (jnp.sqrt(vn * c2) + ADAM_EPS) + ADAM_WD * w_ref[...])

    spec = pl.BlockSpec((tr, C), lambda i: (i, 0))
    return _pcall(
        body, name=name, grid=(R // tr,), in_specs=[spec] * 4, out_specs=[spec] * 3,
        out_shape=[jax.ShapeDtypeStruct((R, C), F32)] * 3,
    )(w, g, m, v)


def _log_sigmoid(z):
    return jnp.minimum(z, 0.0) - jnp.log(1.0 + jnp.exp(-jnp.abs(z)))


def _dot3(x, m01):
    x1 = x.astype(BF16)
    r1 = x - x1.astype(F32)
    x2 = r1.astype(BF16)
    x3 = (r1 - x2.astype(F32)).astype(BF16)
    n = x.shape[0]
    if n % 16:
        return (jnp.dot(x1, m01, preferred_element_type=F32) + jnp.dot(x2, m01, preferred_element_type=F32)
                + jnp.dot(x3, m01, preferred_element_type=F32))
    y = jnp.dot(jnp.concatenate([x1, x2, x3], axis=0), m01, preferred_element_type=F32)
    return y[:n] + y[n:2 * n] + y[2 * n:]


def _dot_nt(a, b):
    return lax.dot_general(a, b, NT, preferred_element_type=F32)


def _dot_tn(a, b):
    return lax.dot_general(a, b, TN, preferred_element_type=F32)


def _iota2(shape):
    return lax.broadcasted_iota(jnp.int32, shape, 0), lax.broadcasted_iota(jnp.int32, shape, 1)


HEADS_PER_STEP = 4
HEADS = range(HEADS_PER_STEP)


CHUNK_HEADS_PER_STEP = 4
CHUNK_HEADS = range(CHUNK_HEADS_PER_STEP)


def _head_spec(T, rows=None, width=HEAD_DIM, heads=HEADS_PER_STEP):
    return pl.BlockSpec((heads, T if rows is None else rows, width), lambda g: (g, 0, 0))


def _sb_weights(qb, kb, t0, s0, racc, m_gt, row, col):
    z = _dot_nt(qb, kb) * SCALE
    strict = (s0 + col) < (t0 + row)
    lb = _log_sigmoid(z)
    lf = jnp.where(strict, lb - z, 0.0)
    between = _dot3(lf, m_gt) + racc
    w = jnp.where(strict, jnp.exp(lb + between), 0.0)
    return strict, lb, lf, w


def _sb_fwd(q, k, v, name, carry=None):
    H, T, _ = q.shape
    nb = T // QB

    def body(q_ref, k_ref, v_ref, o_ref):
        row, col = _iota2((QB, QB))
        m_gt = (row > col).astype(BF16)

        def qblock(i, _):
            t0 = pl.multiple_of(i * QB, QB)
            qbs = [q_ref[h, pl.ds(t0, QB), :].astype(BF16) for h in HEADS]

            def kblock(jj, carry):
                s0 = pl.multiple_of((i - jj) * QB, QB)
                out = []
                for h in HEADS:
                    acc, racc = carry[h]
                    kb = k_ref[h, pl.ds(s0, QB), :].astype(BF16)
                    vb = v_ref[h, pl.ds(s0, QB), :].astype(BF16)
                    _, _, lf, w = _sb_weights(qbs[h], kb, t0, s0, racc, m_gt, row, col)
                    acc = acc + jnp.dot(w.astype(BF16), vb, preferred_element_type=F32)
                    out.append((acc, racc + jnp.sum(lf, axis=1, keepdims=True)))
                return tuple(out)

            res = lax.fori_loop(0, i + 1, kblock,
                                tuple((jnp.zeros((QB, HEAD_DIM), F32), jnp.zeros((QB, 1), F32)) for _ in HEADS))
            for h in HEADS:
                o_ref[h, pl.ds(t0, QB), :] = res[h][0].astype(BF16)
            return 0

        lax.fori_loop(0, nb, qblock, 0)

    spec = _head_spec(T)
    return _pcall_carrying(body, carry, name=name, grid=(H // HEADS_PER_STEP,), in_specs=[spec] * 3, out_specs=spec,
                           out_shape=jax.ShapeDtypeStruct((H, T, HEAD_DIM), BF16))(q, k, v)


def _sb_bwd(q, k, v, do, name, carry=None):
    H, T, _ = q.shape
    nb = T // QB

    def body(q_ref, k_ref, v_ref, do_ref, dq_ref, dk_out, dv_out, racc_ref, dk_ref, dv_ref):
        row, col = _iota2((QB, QB))
        m_gt = (row > col).astype(BF16)
        m_lt = (row < col).astype(BF16)
        dk_ref[...] = jnp.zeros_like(dk_ref)
        dv_ref[...] = jnp.zeros_like(dv_ref)

        def qblock(i, _):
            t0 = pl.multiple_of(i * QB, QB)
            qbs = [q_ref[h, pl.ds(t0, QB), :].astype(BF16) for h in HEADS]
            dobs = [do_ref[h, pl.ds(t0, QB), :].astype(BF16) for h in HEADS]

            def suffix(jj, raccs):
                j = i - jj
                s0 = pl.multiple_of(j * QB, QB)
                out = []
                for h in HEADS:
                    kb = k_ref[h, pl.ds(s0, QB), :].astype(BF16)
                    z = _dot_nt(qbs[h], kb) * SCALE
                    lf = jnp.where((s0 + col) < (t0 + row), _log_sigmoid(z) - z, 0.0)
                    racc_ref[h, j] = raccs[h]
                    out.append(raccs[h] + jnp.sum(lf, axis=1, keepdims=True))
                return tuple(out)

            lax.fori_loop(0, i + 1, suffix, tuple(jnp.zeros((QB, 1), F32) for _ in HEADS))

            def kblock(j, carry):
                s0 = pl.multiple_of(j * QB, QB)
                out = []
                for h in HEADS:
                    dq, cacc = carry[h]
                    kb = k_ref[h, pl.ds(s0, QB), :].astype(BF16)
                    vb = v_ref[h, pl.ds(s0, QB), :].astype(BF16)
                    strict, lb, _, w = _sb_weights(qbs[h], kb, t0, s0, racc_ref[h, j], m_gt, row, col)
                    e = _dot_nt(dobs[h], vb) * w
                    c_left = _dot3(e, m_lt) + cacc
                    sig = jnp.exp(lb)
                    dz = jnp.where(strict, e * (1.0 - sig) - c_left * sig, 0.0).astype(BF16)
                    dq = dq + jnp.dot(dz, kb, preferred_element_type=F32)
                    dk_ref[h, pl.ds(s0, QB), :] += _dot_tn(dz, qbs[h]) * SCALE
                    dv_ref[h, pl.ds(s0, QB), :] += _dot_tn(w.astype(BF16), dobs[h])
                    out.append((dq, cacc + jnp.sum(e, axis=1, keepdims=True)))
                return tuple(out)

            res = lax.fori_loop(0, i + 1, kblock,
                                tuple((jnp.zeros((QB, HEAD_DIM), F32), jnp.zeros((QB, 1), F32)) for _ in HEADS))
            for h in HEADS:
                dq_ref[h, pl.ds(t0, QB), :] = (res[h][0] * SCALE).astype(BF16)
            return 0

        lax.fori_loop(0, nb, qblock, 0)
        dk_out[...] = dk_ref[...].astype(BF16)
        dv_out[...] = dv_ref[...].astype(BF16)

    spec = _head_spec(T)
    acc = pltpu.VMEM((HEADS_PER_STEP, T, HEAD_DIM), F32)
    return _pcall_carrying(body, carry, name=name, grid=(H // HEADS_PER_STEP,), in_specs=[spec] * 4,
                           out_specs=[spec] * 3, out_shape=[jax.ShapeDtypeStruct((H, T, HEAD_DIM), BF16)] * 3,
                           scratch_shapes=[pltpu.VMEM((HEADS_PER_STEP, nb, QB, 1), F32), acc, acc])(q, k, v, do)


def _fox_logits(qb, kb, fq, fk, t0, s0, row, col):
    z = _dot_nt(qb, kb) * SCALE + fq - fk
    return jnp.where((s0 + col) <= (t0 + row), z, NEG)


def _fox_specs(T):
    return _head_spec(T, width=1), _head_spec(T, rows=T // QB, width=QB)


def _fox_fwd(q, k, v, fq, fk, name, carry=None):
    H, T, _ = q.shape
    nb = T // QB

    def body(q_ref, k_ref, v_ref, fq_ref, fk_ref, o_ref, lse_ref):
        row, col = _iota2((QB, QB))

        def qblock(i, _):
            t0 = pl.multiple_of(i * QB, QB)
            qbs = [q_ref[h, pl.ds(t0, QB), :].astype(BF16) for h in HEADS]
            fqs = [fq_ref[h, pl.ds(t0, QB), :] for h in HEADS]

            def kblock(j, carry):
                s0 = pl.multiple_of(j * QB, QB)
                out = []
                for h in HEADS:
                    acc, m, l = carry[h]
                    kb = k_ref[h, pl.ds(s0, QB), :].astype(BF16)
                    vb = v_ref[h, pl.ds(s0, QB), :].astype(BF16)
                    z = _fox_logits(qbs[h], kb, fqs[h], fk_ref[h, pl.ds(j, 1), :], t0, s0, row, col)
                    m_new = jnp.maximum(m, jnp.max(z, axis=1, keepdims=True))
                    a = jnp.exp(m - m_new)
                    p = jnp.exp(z - m_new)
                    acc = a * acc + jnp.dot(p.astype(BF16), vb, preferred_element_type=F32)
                    out.append((acc, m_new, a * l + jnp.sum(p, axis=1, keepdims=True)))
                return tuple(out)

            res = lax.fori_loop(0, i + 1, kblock,
                                tuple((jnp.zeros((QB, HEAD_DIM), F32), jnp.full((QB, 1), NEG, F32),
                                       jnp.zeros((QB, 1), F32)) for _ in HEADS))
            for h in HEADS:
                acc, m, l = res[h]
                o_ref[h, pl.ds(t0, QB), :] = (acc / l).astype(BF16)
                lse_ref[h, pl.ds(t0, QB), :] = m + jnp.log(l)
            return 0

        lax.fori_loop(0, nb, qblock, 0)

    spec = _head_spec(T)
    fq_spec, fk_spec = _fox_specs(T)
    return _pcall_carrying(
        body, carry, name=name, grid=(H // HEADS_PER_STEP,), in_specs=[spec] * 3 + [fq_spec, fk_spec],
        out_specs=[spec, fq_spec],
        out_shape=[jax.ShapeDtypeStruct((H, T, HEAD_DIM), BF16), jax.ShapeDtypeStruct((H, T, 1), F32)],
    )(q, k, v, fq, fk)


def _fox_bwd(q, k, v, fq, fk, lse, do, name, carry=None):
    H, T, _ = q.shape
    nb = T // QB

    def body(q_ref, k_ref, v_ref, fq_ref, fk_ref, lse_ref, do_ref, dq_ref, dk_out, dv_out, dfk_ref, dk_ref, dv_ref):
        row, col = _iota2((QB, QB))
        dk_ref[...] = jnp.zeros_like(dk_ref)
        dv_ref[...] = jnp.zeros_like(dv_ref)
        dfk_ref[...] = jnp.zeros_like(dfk_ref)

        def qblock(i, _):
            t0 = pl.multiple_of(i * QB, QB)
            qbs = [q_ref[h, pl.ds(t0, QB), :].astype(BF16) for h in HEADS]
            fqs = [fq_ref[h, pl.ds(t0, QB), :] for h in HEADS]
            lses = [lse_ref[h, pl.ds(t0, QB), :] for h in HEADS]
            dobs = [do_ref[h, pl.ds(t0, QB), :].astype(BF16) for h in HEADS]

            def probs(h, j):
                s0 = pl.multiple_of(j * QB, QB)
                kb = k_ref[h, pl.ds(s0, QB), :].astype(BF16)
                vb = v_ref[h, pl.ds(s0, QB), :].astype(BF16)
                z = _fox_logits(qbs[h], kb, fqs[h], fk_ref[h, pl.ds(j, 1), :], t0, s0, row, col)
                return s0, kb, jnp.exp(z - lses[h]), _dot_nt(dobs[h], vb)

            def row_dot(j, deltas):
                out = []
                for h in HEADS:
                    _, _, p, dp = probs(h, j)
                    out.append(deltas[h] + jnp.sum(p * dp, axis=1, keepdims=True))
                return tuple(out)

            deltas = lax.fori_loop(0, i + 1, row_dot, tuple(jnp.zeros((QB, 1), F32) for _ in HEADS))

            def kblock(j, dqs):
                out = []
                for h in HEADS:
                    s0, kb, p, dp = probs(h, j)
                    dz = p * (dp - deltas[h])
                    dzb = dz.astype(BF16)
                    dk_ref[h, pl.ds(s0, QB), :] += _dot_tn(dzb, qbs[h]) * SCALE
                    dv_ref[h, pl.ds(s0, QB), :] += _dot_tn(p.astype(BF16), dobs[h])
                    dfk_ref[h, pl.ds(j, 1), :] += -jnp.sum(dz, axis=0, keepdims=True)
                    out.append(dqs[h] + jnp.dot(dzb, kb, preferred_element_type=F32))
                return tuple(out)

            dqs = lax.fori_loop(0, i + 1, kblock, tuple(jnp.zeros((QB, HEAD_DIM), F32) for _ in HEADS))
            for h in HEADS:
                dq_ref[h, pl.ds(t0, QB), :] = (dqs[h] * SCALE).astype(BF16)
            return 0

        lax.fori_loop(0, nb, qblock, 0)
        dk_out[...] = dk_ref[...].astype(BF16)
        dv_out[...] = dv_ref[...].astype(BF16)

    spec = _head_spec(T)
    fq_spec, fk_spec = _fox_specs(T)
    acc = pltpu.VMEM((HEADS_PER_STEP, T, HEAD_DIM), F32)
    return _pcall_carrying(body, carry, name=name, grid=(H // HEADS_PER_STEP,),
                           in_specs=[spec] * 3 + [fq_spec, fk_spec, fq_spec, spec],
                           out_specs=[spec, spec, spec, fk_spec],
                           out_shape=[jax.ShapeDtypeStruct((H, T, HEAD_DIM), BF16)] * 3
                           + [jax.ShapeDtypeStruct((H, nb, QB), F32)],
                           scratch_shapes=[acc, acc])(q, k, v, fq, fk, lse, do)


def _forget_cumsum(flog, name):
    R, T = flog.shape
    nb = T // QB

    def body(x_ref, o_ref):
        row, col = _iota2((QB, QB))
        m_le = (row <= col).astype(BF16)
        carry = jnp.zeros((R, 1), F32)
        for b in range(nb):
            lf = _log_sigmoid(x_ref[:, b * QB:(b + 1) * QB])
            o_ref[:, b * QB:(b + 1) * QB] = _dot3(lf, m_le) + carry
            carry = carry + jnp.sum(lf, axis=1, keepdims=True)

    spec = pl.BlockSpec((R, T), lambda: (0, 0))
    return _pcall(body, name=name, in_specs=[spec], out_specs=spec,
                  out_shape=jax.ShapeDtypeStruct((R, T), F32))(flog)


def _forget_cumsum_bwd(flog, df, name):
    R, T = flog.shape
    nb = T // QB

    def body(x_ref, df_ref, o_ref):
        row, col = _iota2((QB, QB))
        m_ge = (row >= col).astype(BF16)
        carry = jnp.zeros((R, 1), F32)
        for b in reversed(range(nb)):
            dfb = df_ref[:, b * QB:(b + 1) * QB]
            dlog = _dot3(dfb, m_ge) + carry
            o_ref[:, b * QB:(b + 1) * QB] = dlog * _sigmoid(-x_ref[:, b * QB:(b + 1) * QB])
            carry = carry + jnp.sum(dfb, axis=1, keepdims=True)

    spec = pl.BlockSpec((R, T), lambda: (0, 0))
    return _pcall(body, name=name, in_specs=[spec, spec], out_specs=spec,
                  out_shape=jax.ShapeDtypeStruct((R, T), F32))(flog, df)


def _chunk_probs(qc, kb, bias, c, col):
    z = _dot_nt(qc, kb) * SCALE + bias
    z = jnp.where((c - (LEFT_CHUNKS + 1)) * CHUNK + col >= jnp.maximum(0, (c - LEFT_CHUNKS) * CHUNK), z, NEG)
    p = jnp.exp(z - jnp.max(z, axis=1, keepdims=True))
    return p, jnp.sum(p, axis=1, keepdims=True)


def _chunk_specs(T, n=CHUNK_HEADS_PER_STEP):
    return (_head_spec(T, heads=n), _head_spec(T, rows=T + BAND_PAD, heads=n),
            _head_spec(T, rows=CHUNK, width=BAND, heads=n))


def _chunk_fwd(q, kp, vp, bias, name, carry=None):
    H, T, _ = q.shape
    nc = T // CHUNK

    def body(q_ref, kp_ref, vp_ref, bias_ref, o_ref):
        _, col = _iota2((CHUNK, BAND))

        def chunk(c, _):
            t0 = pl.multiple_of(c * CHUNK, CHUNK)
            for h in HEADS:
                qc = q_ref[h, pl.ds(t0, CHUNK), :].astype(BF16)
                kb = kp_ref[h, pl.ds(t0, BAND), :].astype(BF16)
                vb = vp_ref[h, pl.ds(t0, BAND), :].astype(BF16)
                p, l = _chunk_probs(qc, kb, bias_ref[h], c, col)
                o = jnp.dot(p.astype(BF16), vb, preferred_element_type=F32) / l
                o_ref[h, pl.ds(t0, CHUNK), :] = o.astype(BF16)
            return 0

        lax.fori_loop(0, nc, chunk, 0)

    q_spec, kp_spec, b_spec = _chunk_specs(T, HEADS_PER_STEP)
    return _pcall_carrying(body, carry, name=name, grid=(H // HEADS_PER_STEP,),
                           in_specs=[q_spec, kp_spec, kp_spec, b_spec], out_specs=q_spec,
                           out_shape=jax.ShapeDtypeStruct((H, T, HEAD_DIM), BF16))(q, kp, vp, bias)


def _chunk_bwd(q, kp, vp, bias, do, name, carry=None):
    H, T, _ = q.shape
    nc = T // CHUNK

    def body(q_ref, kp_ref, vp_ref, bias_ref, do_ref, dq_ref, dk_out, dv_out, db_ref, dkp_ref, dvp_ref):
        _, col = _iota2((CHUNK, BAND))
        dkp_ref[...] = jnp.zeros_like(dkp_ref)
        dvp_ref[...] = jnp.zeros_like(dvp_ref)
        db_ref[...] = jnp.zeros_like(db_ref)

        def chunk(c, _):
            t0 = pl.multiple_of(c * CHUNK, CHUNK)
            for h in CHUNK_HEADS:
                qc = q_ref[h, pl.ds(t0, CHUNK), :].astype(BF16)
                kb = kp_ref[h, pl.ds(t0, BAND), :].astype(BF16)
                vb = vp_ref[h, pl.ds(t0, BAND), :].astype(BF16)
                dob = do_ref[h, pl.ds(t0, CHUNK), :].astype(BF16)
                p, l = _chunk_probs(qc, kb, bias_ref[h], c, col)
                p = p / l
                dp = _dot_nt(dob, vb)
                dz = p * (dp - jnp.sum(p * dp, axis=1, keepdims=True))
                dzb = dz.astype(BF16)
                dq = jnp.dot(dzb, kb, preferred_element_type=F32) * SCALE
                dq_ref[h, pl.ds(t0, CHUNK), :] = dq.astype(BF16)
                dkp_ref[h, pl.ds(t0, BAND), :] += _dot_tn(dzb, qc) * SCALE
                dvp_ref[h, pl.ds(t0, BAND), :] += _dot_tn(p.astype(BF16), dob)
                db_ref[h] += dz
            return 0

        lax.fori_loop(0, nc, chunk, 0)
        dk_out[...] = dkp_ref[:, BAND_PAD:, :].astype(BF16)
        dv_out[...] = dvp_ref[:, BAND_PAD:, :].astype(BF16)

    q_spec, kp_spec, b_spec = _chunk_specs(T)
    acc = pltpu.VMEM((CHUNK_HEADS_PER_STEP, T + BAND_PAD, HEAD_DIM), F32)
    return _pcall_carrying(body, carry, name=name, grid=(H // CHUNK_HEADS_PER_STEP,),
                           in_specs=[q_spec, kp_spec, kp_spec, b_spec, q_spec],
                           out_specs=[q_spec, q_spec, q_spec, b_spec],
                           out_shape=[jax.ShapeDtypeStruct((H, T, HEAD_DIM), BF16)] * 3
                           + [jax.ShapeDtypeStruct((H, CHUNK, BAND), F32)],
                           scratch_shapes=[acc, acc])(q, kp, vp, bias, do)


HBM_SPEC = pl.BlockSpec(memory_space=pltpu.HBM)


def _position():
    return lax.axis_index("x"), lax.axis_index("y"), lax.axis_index("c")


def _other_chips(x, y):
    chips = [(1 - x, y), (x, 1 - y), (1 - x, 1 - y)]
    return [(chip, 2 * chip[0] + chip[1]) for chip in chips]


def _remote_copy(src, dst, send_sems, recv_sems, k, to):
    return pltpu.make_async_remote_copy(src_ref=src, dst_ref=dst, send_sem=send_sems.at[k], recv_sem=recv_sems.at[k],
                                        device_id=to, device_id_type=MESH)


def _place_own(w, chip, name):
    _, r, c = w.shape
    tr = _pick(r, 256, 16)

    def body(chip_ref, w_ref, *out_refs):
        for l, o_ref in enumerate(out_refs):
            o_ref[...] = w_ref[l].astype(BF16)

    grid_spec = pltpu.PrefetchScalarGridSpec(
        num_scalar_prefetch=1, grid=(r // tr,), in_specs=[pl.BlockSpec((DEPTH, tr, c), lambda i, ch: (0, i, 0))],
        out_specs=[pl.BlockSpec((None, tr, c), lambda i, ch: (ch[0], i, 0))] * DEPTH)
    return _pcall(body, name=name, grid_spec=grid_spec,
                  out_shape=[jax.ShapeDtypeStruct((N_CHIPS, r, c), BF16)] * DEPTH)(chip, w)


def _gather_over_ici(srcs, bufs, new):
    x, y, c = _position()
    me = 2 * x + y
    return [(b.at[me, c], b.at[me, c], (*chip, c)) for b in bufs for chip, _ in _other_chips(x, y)]


def _gather_to_sibling(srcs, bufs, new):
    x, y, c = _position()
    return [(b.at[idx, c], b.at[idx, c], (x, y, 1 - c)) for b in bufs for _, idx in _other_chips(x, y)]


def _gather_layer(bufs, name):
    n = len(bufs)

    def body(*refs):
        dst = refs[n:2 * n]
        send_sems, recv_sems = refs[2 * n:]
        x, y, c = _position()
        me = 2 * x + y
        sibling = (x, y, 1 - c)
        others = _other_chips(x, y)
        first = [_remote_copy(dst[i].at[me, c], dst[i].at[me, c], send_sems, recv_sems, 3 * i + k, (*chip, c))
                 for i in range(n) for k, (chip, _) in enumerate(others)]
        for cp in first:
            cp.start()
        passed = []
        for i in range(n):
            for k, (_, idx) in enumerate(others):
                landed = dst[i].at[idx, c]
                _remote_copy(landed, landed, send_sems, recv_sems, 3 * i + k, sibling).wait_recv()
                passed.append(_remote_copy(landed, landed, send_sems, recv_sems, 3 * (n + i) + k, sibling))
                passed[-1].start()
        for i in range(n):
            for k, (_, idx) in enumerate(others):
                from_sibling = dst[i].at[idx, 1 - c]
                _remote_copy(from_sibling, from_sibling, send_sems, recv_sems, 3 * (n + i) + k, sibling).wait_recv()
        for cp in first + passed:
            cp.wait_send()

    return _pcall(
        body, name=name, in_specs=[HBM_SPEC] * n, out_specs=[HBM_SPEC] * n,
        out_shape=[jax.ShapeDtypeStruct(b.shape, b.dtype) for b in bufs],
        scratch_shapes=[pltpu.SemaphoreType.DMA((6 * n,)), pltpu.SemaphoreType.DMA((6 * n,))],
        input_output_aliases={i: i for i in range(n)},
    )(*bufs)


def _send_other_half(parts, name):
    n = len(parts)

    def body(*refs):
        src, got = refs[:n], refs[n:2 * n]
        send_sems, recv_sems = refs[2 * n:]
        x, y, c = _position()
        copies = [_remote_copy(src[i].at[1 - c], got[i], send_sems, recv_sems, i, (x, y, 1 - c)) for i in range(n)]
        for cp in copies:
            cp.start()
        for cp in copies:
            cp.wait()

    return _pcall(
        body, name=name, in_specs=[HBM_SPEC] * n, out_specs=[HBM_SPEC] * n,
        out_shape=[jax.ShapeDtypeStruct(p.shape[1:], p.dtype) for p in parts],
        scratch_shapes=[pltpu.SemaphoreType.DMA((n,)), pltpu.SemaphoreType.DMA((n,))],
    )(*parts)


def _scatter_chips(parts, name):
    n = len(parts)

    def body(*refs):
        src, dst = refs[:n], refs[n:2 * n]
        send_sems, recv_sems = refs[2 * n:]
        x, y, c = _position()
        others = _other_chips(x, y)
        sends = [_remote_copy(src[i].at[idx], dst[i].at[k], send_sems, recv_sems, 3 * i + k, (*chip, c))
                 for i in range(n) for k, (chip, idx) in enumerate(others)]
        for cp in sends:
            cp.start()
        for cp in sends:
            cp.wait()

    return _pcall(
        body, name=name, in_specs=[HBM_SPEC] * n, out_specs=[HBM_SPEC] * n,
        out_shape=[jax.ShapeDtypeStruct((3,) + p.shape[1:], p.dtype) for p in parts],
        scratch_shapes=[pltpu.SemaphoreType.DMA((3 * n,)), pltpu.SemaphoreType.DMA((3 * n,))],
    )(*parts)


def _swap_with_sibling(arrs, name):
    n = len(arrs)

    def body(*refs):
        src, dst = refs[:n], refs[n:2 * n]
        send_sems, recv_sems = refs[2 * n:]
        x, y, c = _position()
        copies = [_remote_copy(src[i], dst[i], send_sems, recv_sems, i, (x, y, 1 - c)) for i in range(n)]
        for cp in copies:
            cp.start()
        for cp in copies:
            cp.wait()

    return _pcall(
        body, name=name, in_specs=[HBM_SPEC] * n, out_specs=[HBM_SPEC] * n,
        out_shape=[jax.ShapeDtypeStruct(a.shape, a.dtype) for a in arrs],
        scratch_shapes=[pltpu.SemaphoreType.DMA((n,)), pltpu.SemaphoreType.DMA((n,))],
    )(*arrs)


def _allreduce_small(v, name):
    R, C = v.shape

    def body(v_ref, o_ref, slots, send_sems, recv_sems):
        x, y, c = _position()
        me = 4 * x + 2 * y + c
        slots[0] = v_ref[...]
        sends = []
        for r in range(1, 8):
            fx, fy, fc = (r >> 2) & 1, (r >> 1) & 1, r & 1
            to = (x ^ fx, y ^ fy, c ^ fc)
            cp = pltpu.make_async_remote_copy(src_ref=v_ref, dst_ref=slots.at[r], send_sem=send_sems.at[r - 1],
                                              recv_sem=recv_sems.at[r - 1], device_id=to, device_id_type=MESH)
            cp.start()
            sends.append(cp)
        for cp in sends:
            cp.wait()
        acc = slots[me]
        for d in range(1, 8):
            acc = acc + slots[d ^ me]
        o_ref[...] = acc

    vmem = pl.BlockSpec(memory_space=pltpu.VMEM)
    return _pcall(
        body, name=name, in_specs=[vmem], out_specs=vmem, out_shape=jax.ShapeDtypeStruct((R, C), F32),
        scratch_shapes=[pltpu.VMEM((8, R, C), F32), pltpu.SemaphoreType.DMA((7,)), pltpu.SemaphoreType.DMA((7,))],
    )(v)


def _add_pair(which, both, got, name):
    _, N, R, C = both.shape
    tr = _pick(R, 512, 16)

    def body(which_ref, a_ref, b_ref, o_ref):
        o_ref[...] = (a_ref[...].astype(F32) + b_ref[...].astype(F32)).astype(BF16)

    spec = pl.BlockSpec((None, tr, C), lambda n, i, w: (n, i, 0))
    grid_spec = pltpu.PrefetchScalarGridSpec(
        num_scalar_prefetch=1, grid=(N, R // tr),
        in_specs=[pl.BlockSpec((None, None, tr, C), lambda n, i, w: (w[0], n, i, 0)), spec], out_specs=spec)
    return _pcall(body, name=name, grid_spec=grid_spec,
                  out_shape=jax.ShapeDtypeStruct((N, R, C), BF16))(which, both, got)


def _sum_chips(which, own, others, name):
    N, R, C = others.shape
    tr = _pick(R, 512, 16)

    def body(which_ref, own_ref, p_ref, o_ref):
        acc = own_ref[...].astype(F32)
        for n in range(N):
            acc = acc + p_ref[n].astype(F32)
        o_ref[...] = acc

    grid_spec = pltpu.PrefetchScalarGridSpec(
        num_scalar_prefetch=1, grid=(R // tr,),
        in_specs=[pl.BlockSpec((None, tr, C), lambda i, w: (w[0], i, 0)), pl.BlockSpec((N, tr, C), lambda i, w: (0, i, 0))],
        out_specs=pl.BlockSpec((tr, C), lambda i, w: (i, 0)))
    return _pcall(body, name=name, grid_spec=grid_spec,
                  out_shape=jax.ShapeDtypeStruct((R, C), F32))(which, own, others)


BIG = ("w_ffn1_in", "w_ffn1_out", "w_in", "w_br_sb", "w_br_ch", "w_br_fox", "w_out", "w_ffn2_in", "w_ffn2_out")
ROW_SHARDED = ("w_ffn1_out", "w_out", "w_ffn2_out")
SMALL = ("g_ffn1", "g_mix", "b_in", "rel_bias", "g_ffn2", "g_final")
SHARD_SHAPES = {
    "w_ffn1_in": (D_MODEL, 2 * D_FF // N_CHIPS), "w_ffn1_out": (D_FF // N_CHIPS, D_MODEL),
    "w_in": (D_MODEL, IN_WIDTH // N_CHIPS), "w_br_sb": (W_SB, D_MODEL // N_CHIPS),
    "w_br_ch": (W_CH, D_MODEL // N_CHIPS), "w_br_fox": (W_FOX, D_MODEL // N_CHIPS),
    "w_out": (D_MODEL // N_CHIPS, D_MODEL), "w_ffn2_in": (D_MODEL, 2 * D_FF // N_CHIPS),
    "w_ffn2_out": (D_FF // N_CHIPS, D_MODEL),
}


def _pair_sums(split, tag):
    core = lax.axis_index("c").astype(jnp.int32)[None]
    got = _send_other_half(split, f"grad_split_{tag}")
    return [_add_pair(core, a, b, f"grad_pair_sum_{tag}_{i}") for i, (a, b) in enumerate(zip(split, got))]


def _scatter_plan(srcs, bufs, new):
    x, y, c = _position()
    return [(s.at[idx], d.at[k], (*chip, c)) for s, d in zip(srcs, new) for k, (chip, idx) in enumerate(_other_chips(x, y))]


def _scatter_carry(pair):
    landing = [jax.ShapeDtypeStruct((3,) + p.shape[1:], p.dtype) for p in pair]
    return _Carry(pair, [], landing, 3 * len(pair), _scatter_plan)


def _finish_grads(pair, landed, axes, tag):
    chip = (2 * lax.axis_index("x") + lax.axis_index("y")).astype(jnp.int32)[None]
    mine = [_sum_chips(chip, p, q, f"grad_chip_sum_{tag}_{i}") for i, (p, q) in enumerate(zip(pair, landed))]
    theirs = _swap_with_sibling(mine, f"grad_share_{tag}")
    first = lax.axis_index("c") == 0
    return [jnp.concatenate([jnp.where(first, a, b), jnp.where(first, b, a)], axis=ax)
            for a, b, ax in zip(mine, theirs, axes)]


SMALL_SIZES = {"g_ffn1": DEPTH * D_MODEL, "g_mix": DEPTH * D_MODEL, "b_in": DEPTH * IN_WIDTH,
               "rel_bias": DEPTH * N_REL * H_CH, "g_ffn2": DEPTH * D_MODEL, "g_final": D_MODEL}
SMALL_TOTAL = sum(SMALL_SIZES.values()) + 1
SMALL_ROWS = -(-SMALL_TOTAL // (8 * 128)) * 8


def _pack_small(vals, extra):
    flat = [vals[n].reshape(-1) for n in SMALL] + [extra.reshape(-1)]
    flat.append(jnp.zeros((SMALL_ROWS * 128 - SMALL_TOTAL,), F32))
    return jnp.concatenate(flat).reshape(SMALL_ROWS, 128)


def _unpack_small(pack, shapes):
    flat, out, off = pack.reshape(-1), {}, 0
    for n in SMALL:
        out[n] = flat[off:off + SMALL_SIZES[n]].reshape(shapes[n])
        off += SMALL_SIZES[n]
    return out, flat[off]


def _to_heads(t, n_heads):
    return jnp.transpose(t.reshape(t.shape[0], n_heads, HEAD_DIM), (1, 0, 2)).astype(BF16)


def _from_heads(t):
    return jnp.transpose(t, (1, 0, 2)).reshape(t.shape[1], t.shape[0] * HEAD_DIM)


def _pad_keys(t):
    return jnp.pad(t, ((0, 0), (BAND_PAD, 0), (0, 0)))


DIAGONALS = CHUNK + BAND - 1


def _band_bias(table):
    n_far = (LEFT_CHUNKS + 1) * CHUNK + CHUNK - MAX_REL
    near = table[N_REL - 1 - (DIAGONALS - n_far):N_REL - 1][::-1]
    diag = jnp.concatenate([jnp.broadcast_to(table[N_REL - 1], (n_far, H_CH)), near], axis=0).T
    diag = jnp.pad(diag, ((0, 0), (0, 1)))
    skew = jnp.tile(diag, (1, CHUNK))[:, :CHUNK * DIAGONALS].reshape(H_CH, CHUNK, DIAGONALS)
    return skew[:, :, CHUNK - 1:]


def _pad_w_in(w):
    qkv, f, gates = w[:, :QKV_WIDTH], w[:, QKV_WIDTH:QKV_WIDTH + H_FOX], w[:, QKV_WIDTH + H_FOX:]
    return jnp.concatenate([qkv, gates, f, jnp.zeros((w.shape[0], F_PAD - H_FOX), w.dtype)], axis=1)


def _unpad_w_in(w):
    return jnp.concatenate([w[:, :QKV_WIDTH], w[:, QKV_WIDTH + GATE_WIDTH:QKV_WIDTH + GATE_WIDTH + H_FOX],
                            w[:, QKV_WIDTH:QKV_WIDTH + GATE_WIDTH]], axis=1)


QKV_SPLITS = np.cumsum([0, W_SB, W_SB, W_SB, W_CH, W_CH, W_CH, W_FOX, W_FOX, W_FOX])


def _by_chip_rows(g):
    return g.reshape(2, N_CHIPS, g.shape[1] // N_CHIPS, g.shape[2])


def _ffn_fwd(x, g, w_in, w_out, tag, carries):
    h = _rmsnorm_fwd(x, g, f"{tag}_norm")
    gate, up, a = _ffn_in_swiglu(h, w_in, f"{tag}_in", _carry_of(carries, "in"))
    y = _mm(a, w_out, mode="nn", name=f"{tag}_out", alpha=0.5, res=x, gathered="row",
            carry=_carry_of(carries, "out"))
    return y, (h, gate, up, a)


def _ffn_bwd(x, g, w_in, w_out, saved, dy, tag, carries):
    h, gate, up, a = saved
    da = _mm(dy, w_out, mode="nt", name=f"{tag}_da", alpha=0.5, gathered="row", out_dtype=BF16,
             carry=_carry_of(carries, "da"))
    dw_out = _mm(a, dy, mode="tn", name=f"{tag}_dwout", alpha=0.5, tm_cap=1408, out_dtype=BF16, out_split="row",
                 carry=_carry_of(carries, "dwout"))
    dgu = _swiglu_bwd(gate, up, da, f"{tag}_dact")
    dw_in = _mm(h, dgu, mode="tn", name=f"{tag}_dwin", tm_cap=512, out_dtype=BF16, out_split="col",
                carry=_carry_of(carries, "dwin"))
    dh = _mm(dgu, w_in, mode="nt", name=f"{tag}_dh", gathered="col", tn_cap=1024, carry=_carry_of(carries, "dh"))
    dx, dg = _rmsnorm_bwd(x, g, dh, dy, f"{tag}_dnorm")
    return dx, dg, dw_in, _by_chip_rows(dw_out)


def _mixer_fwd(x, lw, tag, carries):
    T = x.shape[0]
    h = _rmsnorm_fwd(x, lw["g_mix"], f"{tag}_norm")
    p = _mm(h, lw["w_in"], mode="nn", name=f"{tag}_proj", bias=lw["b_in"], tn_cap=896,
            carry=_carry_of(carries, "proj"))
    parts = [p[:, QKV_SPLITS[i]:QKV_SPLITS[i + 1]] for i in range(9)]
    qa, ka, va = (_to_heads(t, H_SB) for t in parts[0:3])
    qb, kb, vb = (_to_heads(t, H_CH) for t in parts[3:6])
    qc, kc, vc = (_to_heads(t, H_FOX) for t in parts[6:9])
    flog = _mm(lw["w_forget_t"], h, mode="nt", name=f"{tag}_flog")[:8] + lw["b_forget"]
    fcum = _forget_cumsum(flog, f"{tag}_fcum")[:H_FOX]
    fq, fk = fcum.reshape(H_FOX, T, 1), fcum.reshape(H_FOX, T // QB, QB)
    kbp, vbp = _pad_keys(kb), _pad_keys(vb)
    oa = _sb_fwd(qa, ka, va, f"{tag}_sb", _carry_of(carries, "sb"))
    ob = _chunk_fwd(qb, kbp, vbp, lw["bias"], f"{tag}_ch", _carry_of(carries, "ch"))
    oc, lse = _fox_fwd(qc, kc, vc, fq, fk, f"{tag}_fox", _carry_of(carries, "fox"))
    oam, obm, ocm = _from_heads(oa), _from_heads(ob), _from_heads(oc)
    ya = _mm(oam, lw["w_br_sb"], mode="nn", name=f"{tag}_bra", gathered="col")
    yb = _mm(obm, lw["w_br_ch"], mode="nn", name=f"{tag}_brb", gathered="col")
    yc = _mm(ocm, lw["w_br_fox"], mode="nn", name=f"{tag}_brc", gathered="col")
    merged = _gate_fwd(p, ya, yb, yc, f"{tag}_gate")
    y = _mm(merged, lw["w_out"], mode="nn", name=f"{tag}_out", res=x, gathered="row")
    saved = (h, p, (qa, ka, va), (qb, kbp, vbp), (qc, kc, vc), flog, fq, fk, lse, (oam, obm, ocm),
             (ya, yb, yc), merged)
    return y, saved


def _mixer_bwd(x, lw, saved, dy, tag, carries):
    (h, p, (qa, ka, va), (qb, kbp, vbp), (qc, kc, vc), flog, fq, fk, lse, (oam, obm, ocm),
     (ya, yb, yc), merged) = saved
    T = x.shape[0]
    grads = {}
    col, row = "col", "row"
    dmerged = _mm(dy, lw["w_out"], mode="nt", name=f"{tag}_dmerged", gathered=row,
                  carry=_carry_of(carries, "dmerged"))
    grads["w_out"] = _by_chip_rows(_mm(merged, dy, mode="tn", name=f"{tag}_dwout", tm_cap=1024, out_dtype=BF16,
                                       out_split="row"))
    dya, dyb, dyc, dgate = _gate_bwd(p, ya, yb, yc, dmerged, f"{tag}_dgate")
    doa = _mm(dya, lw["w_br_sb"], mode="nt", name=f"{tag}_doa", gathered=col)
    dob = _mm(dyb, lw["w_br_ch"], mode="nt", name=f"{tag}_dob", gathered=col)
    doc = _mm(dyc, lw["w_br_fox"], mode="nt", name=f"{tag}_doc", gathered=col)
    by_chip = dict(mode="tn", tm_cap=512, out_dtype=BF16, out_split="col")
    grads["w_br_sb"] = _mm(oam, dya, name=f"{tag}_dwbra", **by_chip)
    grads["w_br_ch"] = _mm(obm, dyb, name=f"{tag}_dwbrb", **by_chip)
    grads["w_br_fox"] = _mm(ocm, dyc, name=f"{tag}_dwbrc", **by_chip)
    dqa, dka, dva = _sb_bwd(qa, ka, va, _to_heads(doa, H_SB), f"{tag}_dsb", _carry_of(carries, "dsb"))
    dqb, dkb, dvb, dbias = _chunk_bwd(qb, kbp, vbp, lw["bias"], _to_heads(dob, H_CH), f"{tag}_dch",
                                      _carry_of(carries, "dch"))
    dqc, dkc, dvc, dfk = _fox_bwd(qc, kc, vc, fq, fk, lse, _to_heads(doc, H_FOX), f"{tag}_dfox",
                                  _carry_of(carries, "dfox"))
    dflog = _forget_cumsum_bwd(flog, jnp.pad(dfk.reshape(H_FOX, T), ((0, 8 - H_FOX), (0, 0))), f"{tag}_dfcum")
    dqkv = [_from_heads(t) for t in (dqa, dka, dva, dqb, dkb, dvb, dqc, dkc, dvc)]
    dforget = jnp.pad(dflog[:H_FOX].T, ((0, 0), (0, F_PAD - H_FOX))).astype(BF16)
    dpb = jnp.concatenate(dqkv + [dgate, dforget], axis=1)
    grads["b_in"] = _colsum(dpb, f"{tag}_dbin")
    dw_in =_unpad_w_in(_mm(h, dpb, mode="tn", name=f"{tag}_dwin", tm_cap=512, tn_cap=896, out_dtype=BF16))
    grads["w_in"] = jnp.transpose(dw_in.reshape(2, D_MODEL // 2, N_CHIPS, IN_WIDTH // N_CHIPS), (0, 2, 1, 3))
    dh = _mm(dpb, lw["w_in"], mode="nt", name=f"{tag}_dh", tk_cap=896, tn_cap=1024)
    dx, grads["g_mix"] = _rmsnorm_bwd(x, lw["g_mix"], dh, dy, f"{tag}_dnorm")
    grads["rel_bias"] = lw["bias_vjp"](dbias)[0]
    return dx, grads


def kernel(x, g_ffn1, w_ffn1_in, w_ffn1_out, g_mix, w_in, b_in, rel_bias, w_br_sb, w_br_ch, w_br_fox, w_out, g_ffn2, w_ffn2_in, w_ffn2_out, g_final, loss_target, m_g_ffn1, m_w_ffn1_in, m_w_ffn1_out, m_g_mix, m_w_in, m_b_in, m_rel_bias, m_w_br_sb, m_w_br_ch, m_w_br_fox, m_w_out, m_g_ffn2, m_w_ffn2_in, m_w_ffn2_out, m_g_final, v_g_ffn1, v_w_ffn1_in, v_w_ffn1_out, v_g_mix, v_w_in, v_b_in, v_rel_bias, v_w_br_sb, v_w_br_ch, v_w_br_fox, v_w_out, v_g_ffn2, v_w_ffn2_in, v_w_ffn2_out, v_g_final):
    names = ("g_ffn1", "w_ffn1_in", "w_ffn1_out", "g_mix", "w_in", "b_in", "rel_bias", "w_br_sb", "w_br_ch",
             "w_br_fox", "w_out", "g_ffn2", "w_ffn2_in", "w_ffn2_out", "g_final")
    w = dict(zip(names, (g_ffn1, w_ffn1_in, w_ffn1_out, g_mix, w_in, b_in, rel_bias, w_br_sb, w_br_ch,
                         w_br_fox, w_out, g_ffn2, w_ffn2_in, w_ffn2_out, g_final)))
    m = dict(zip(names, (m_g_ffn1, m_w_ffn1_in, m_w_ffn1_out, m_g_mix, m_w_in, m_b_in, m_rel_bias, m_w_br_sb,
                         m_w_br_ch, m_w_br_fox, m_w_out, m_g_ffn2, m_w_ffn2_in, m_w_ffn2_out, m_g_final)))
    v = dict(zip(names, (v_g_ffn1, v_w_ffn1_in, v_w_ffn1_out, v_g_mix, v_w_in, v_b_in, v_rel_bias, v_w_br_sb,
                         v_w_br_ch, v_w_br_fox, v_w_out, v_g_ffn2, v_w_ffn2_in, v_w_ffn2_out, v_g_final)))
    xs = x[0]
    tgt = loss_target[0]

    chip = (2 * lax.axis_index("x") + lax.axis_index("y")).astype(jnp.int32)[None]
    placed = [_place_own(w[n], chip, f"place_{n}") for n in BIG]
    bufs = [[p[l].reshape(N_CHIPS, 2, p[l].shape[1] // 2, p[l].shape[2]) for p in placed] for l in range(DEPTH)]
    padded_w_in = {}

    def layer_weights(l):
        lw = {n: b.reshape((N_CHIPS,) + w[n].shape[1:]) for n, b in zip(BIG, bufs[l])}
        if l not in padded_w_in:
            whole = jnp.concatenate([lw["w_in"][j] for j in range(N_CHIPS)], axis=1)
            forget_t = jnp.pad(whole[:, QKV_WIDTH:QKV_WIDTH + H_FOX].T, ((0, F_PAD - H_FOX), (0, 0)))
            padded_w_in[l] = (_pad_w_in(whole), forget_t)
        lw["w_in"], lw["w_forget_t"] = padded_w_in[l]
        lw["b_forget"] = jnp.pad(w["b_in"][l][QKV_WIDTH:QKV_WIDTH + H_FOX], (0, 8 - H_FOX))[:, None]
        lw["b_in"] = _pad_w_in(w["b_in"][l][None, :])
        lw["bias"], lw["bias_vjp"] = jax.vjp(_band_bias, w["rel_bias"][l])
        for n in ("g_ffn1", "g_mix", "g_ffn2"):
            lw[n] = w[n][l][None, :]
        return lw

    def landed_in(l, idx):
        def done(carry):
            for i, b in zip(idx, carry.out[0]):
                bufs[l][i] = b
        return done

    def over_ici(l, idx):
        return lambda: _Carry([], [bufs[l][i] for i in idx], [], 3 * len(idx), _gather_over_ici, landed_in(l, idx))

    def to_sibling(l, idx):
        return lambda: _Carry([], [bufs[l][i] for i in idx], [], 3 * len(idx), _gather_to_sibling, landed_in(l, idx))

    def ffn_fwd(x_in, lw, which, tag, carries):
        return _ffn_fwd(x_in, lw[f"g_{which}"], lw[f"w_{which}_in"], lw[f"w_{which}_out"], tag, carries)

    def ffn_bwd(x_in, lw, which, saved_acts, dy, tag, carries):
        return _ffn_bwd(x_in, lw[f"g_{which}"], lw[f"w_{which}_in"], lw[f"w_{which}_out"], saved_acts, dy, tag,
                        carries)

    FFN1, W_IN, MIXER_REST, FFN2_IN, FFN2_OUT = (0, 1), (2,), (3, 4, 5, 6), (7,), (8,)
    first = FFN1 + W_IN
    for i, b in zip(first, _gather_layer([bufs[0][i] for i in first], "gather_l0")):
        bufs[0][i] = b
    fwd_carries = [
        {"ffn1": {"in": [over_ici(0, MIXER_REST)], "out": [to_sibling(0, MIXER_REST), over_ici(0, FFN2_OUT)]},
         "mix": {"proj": [to_sibling(0, FFN2_OUT), over_ici(1, MIXER_REST)],
                 "sb": [over_ici(0, FFN2_IN), over_ici(1, FFN2_OUT)],
                 "ch": [to_sibling(0, FFN2_IN), over_ici(1, FFN1)],
                 "fox": [over_ici(1, W_IN)]},
         "ffn2": {"in": [to_sibling(1, MIXER_REST + FFN2_OUT + FFN1 + W_IN)]}},
        {"ffn1": {}, "mix": {"sb": [over_ici(1, FFN2_IN)], "ch": [to_sibling(1, FFN2_IN)]}, "ffn2": {}},
    ]

    saved = []
    act = xs
    for l in range(DEPTH):
        x0 = act
        x1, s1 = ffn_fwd(x0, layer_weights(l), "ffn1", f"l{l}_ffn1", fwd_carries[l]["ffn1"])
        x2, s2 = _mixer_fwd(x1, layer_weights(l), f"l{l}_mix", fwd_carries[l]["mix"])
        act, s3 = ffn_fwd(x2, layer_weights(l), "ffn2", f"l{l}_ffn2", fwd_carries[l]["ffn2"])
        saved.append((x0, s1, x1, s2, x2, s3))
    layers = [layer_weights(l) for l in range(DEPTH)]

    dact, dg_final, loss_row = _final_loss(act, w["g_final"][None, :], tgt, "final_loss")

    big_grads = {n: [None] * DEPTH for n in BIG}
    small_grads = {n: [None] * DEPTH for n in ("g_ffn1", "g_mix", "b_in", "rel_bias", "g_ffn2")}
    pair = [[None] * len(BIG) for _ in range(DEPTH)]
    landed = [[None] * len(BIG) for _ in range(DEPTH)]

    def pair_up(l, idx, tag):
        for i, p in zip(idx, _pair_sums([big_grads[BIG[i]][l] for i in idx], tag)):
            pair[l][i] = p

    core = lax.axis_index("c").astype(jnp.int32)[None]

    def to_sibling_half(l, idx, tag):
        def plan(srcs, bufs, new):
            x, y, c = _position()
            return [(s.at[1 - c], d, (x, y, 1 - c)) for s, d in zip(srcs, new)]
        def done(carry):
            for j, (i, got) in enumerate(zip(idx, carry.out[1])):
                pair[l][i] = _add_pair(core, big_grads[BIG[i]][l], got, f"grad_pair_sum_{tag}_{j}")
        def make():
            split = [big_grads[BIG[i]][l] for i in idx]
            return _Carry(split, [], [jax.ShapeDtypeStruct(s.shape[1:], s.dtype) for s in split], len(idx), plan, done)
        return make

    def exchange(l, idx):
        def done(carry):
            for i, a in zip(idx, carry.out[1]):
                landed[l][i] = a
        def make():
            carry = _scatter_carry([pair[l][i] for i in idx])
            carry.done = done
            return carry
        return make

    def exchange_one_way(l, i, k):
        def plan(srcs, bufs, new):
            x, y, c = _position()
            chip_k, idx_k = _other_chips(x, y)[k]
            return [(srcs[0].at[idx_k], bufs[0].at[k], (*chip_k, c))]
        def done(carry):
            landed[l][i] = carry.out[0][0]
        def make():
            if landed[l][i] is None:
                landed[l][i] = lax.empty((3,) + pair[l][i].shape[1:], BF16)
            return _Carry([pair[l][i]], [landed[l][i]], [], 1, plan, done)
        return make

    bwd_carries = [
        {"ffn2": {},
         "mix": {"dmerged": [to_sibling_half(0, FFN2_IN + FFN2_OUT, "l0_ffn2")],
                 "dsb": [exchange(1, FFN1 + W_IN)], "dch": [exchange(1, MIXER_REST + FFN2_IN + FFN2_OUT)],
                 "dfox": [exchange(0, FFN2_IN + FFN2_OUT)]},
         "ffn1": {"da": [exchange(0, MIXER_REST)], "dwout": [exchange_one_way(0, 2, 0)],
                  "dwin": [exchange_one_way(0, 2, 1)], "dh": [exchange_one_way(0, 2, 2)]}},
        {"ffn2": {}, "mix": {},
         "ffn1": {"da": [to_sibling_half(1, W_IN + MIXER_REST + FFN2_IN + FFN2_OUT, "l1_rest")]}},
    ]
    for l in reversed(range(DEPTH)):
        lw = layers[l]
        x0, s1, x1, s2, x2, s3 = saved[l]
        dx2, small_grads["g_ffn2"][l], big_grads["w_ffn2_in"][l], big_grads["w_ffn2_out"][l] = ffn_bwd(
            x2, lw, "ffn2", s3, dact, f"l{l}_ffn2", bwd_carries[l]["ffn2"])
        dx1, mg = _mixer_bwd(x1, lw, s2, dx2, f"l{l}_mix", bwd_carries[l]["mix"])
        for n in ("w_in", "w_br_sb", "w_br_ch", "w_br_fox", "w_out"):
            big_grads[n][l] = mg[n]
        small_grads["b_in"][l] = _unpad_w_in(mg["b_in"])[0]
        small_grads["g_mix"][l] = mg["g_mix"][0]
        small_grads["rel_bias"][l] = mg["rel_bias"]
        if l == 0:
            pair_up(0, W_IN + MIXER_REST, "l0_mix")
        dact, dg1, big_grads["w_ffn1_in"][l], big_grads["w_ffn1_out"][l] = ffn_bwd(
            x0, lw, "ffn1", s1, dx1, f"l{l}_ffn1", bwd_carries[l]["ffn1"])
        small_grads["g_ffn1"][l] = dg1[0]
        small_grads["g_ffn2"][l] = small_grads["g_ffn2"][l][0]
        if l == 1:
            pair_up(1, FFN1, "l1_ffn1")
    grad_x = dact[None]
    pair_up(0, FFN1, "l0_ffn1")
    for i, a in zip(FFN1, _scatter_chips([pair[0][i] for i in FFN1], "grad_scatter_l0")):
        landed[0][i] = a

    small = {n: jnp.stack(small_grads[n]) for n in small_grads}
    small["g_final"] = dg_final[0]
    small_sum, loss = _unpack_small(_allreduce_small(_pack_small(small, loss_row[0, :1]), "allreduce_small"),
                                    {n: w[n].shape for n in SMALL})

    axes = [1 if n in ROW_SHARDED else 0 for n in BIG] * DEPTH
    summed = _finish_grads(pair[0] + pair[1], landed[0] + landed[1], axes, "all")
    grads = {n: jnp.stack([summed[i], summed[len(BIG) + i]]) for i, n in enumerate(BIG)}
    grads.update(small_sum)

    delta, new_m, new_v = {}, {}, {}
    for n in BIG:
        shape = w[n].shape
        flat = lambda t: t.reshape(-1, shape[-1])
        d, nm, nv = _adamw(flat(w[n]), flat(grads[n]), flat(m[n]), flat(v[n]), f"adamw_{n}")
        delta[n], new_m[n], new_v[n] = d.reshape(shape), nm.reshape(shape), nv.reshape(shape)
    zero = jnp.zeros((1,), F32)
    sd, sm, sv = _adamw(_pack_small(w, zero), _pack_small(grads, zero), _pack_small(m, zero), _pack_small(v, zero),
                        "adamw_small")
    shapes = {n: w[n].shape for n in SMALL}
    for dst, src in ((delta, sd), (new_m, sm), (new_v, sv)):
        dst.update(_unpack_small(src, shapes)[0])

    return (loss, grad_x, *[grads[n] for n in names], *[delta[n] for n in names],
            *[new_m[n] for n in names], *[new_v[n] for n in names])
```

```python
import functools

import numpy as np
import jax
import jax.numpy as jnp
from jax import lax
from jax.experimental import pallas as pl
from jax.experimental.pallas import tpu as pltpu

F32 = jnp.float32
BF16 = jnp.bfloat16

D_MODEL = 1024
DEPTH = 2
CHUNK = 64
HEAD_DIM = 64
H_SB, H_CH, H_FOX = 4, 8, 4
W_SB, W_CH, W_FOX = H_SB * HEAD_DIM, H_CH * HEAD_DIM, H_FOX * HEAD_DIM
LEFT_CHUNKS = 8
MAX_REL = 128
N_REL = 2 * MAX_REL + 1
D_FF = 2816
QKV_WIDTH = 3 * (W_SB + W_CH + W_FOX)
GATE_WIDTH = 3 * D_MODEL
IN_WIDTH = QKV_WIDTH + H_FOX + GATE_WIDTH
F_PAD = 128
P_WIDTH = QKV_WIDTH + GATE_WIDTH + F_PAD
RMS_EPS = 1e-6
NEG = -1e30
SCALE = HEAD_DIM ** -0.5
QB = 256
BAND = (LEFT_CHUNKS + 2) * CHUNK
BAND_PAD = BAND - CHUNK

ADAM_LR, ADAM_B1, ADAM_B2, ADAM_EPS, ADAM_WD, ADAM_STEP = 0.001, 0.9, 0.999, 1e-08, 0.01, 10

N_CHIPS = 4
PACK_COLS = 1024
VMEM_BUDGET = 52 * 1024 * 1024

NT = (((1,), (1,)), ((), ()))
TN = (((0,), (0,)), ((), ()))
NN = (((1,), (0,)), ((), ()))
MESH = pl.DeviceIdType.MESH


def _pcall(body, **kw):
    return pl.pallas_call(body, **kw)


class _Carry:
    def __init__(self, srcs, bufs, new, count, plan, done=None):
        self.srcs, self.bufs, self.new, self.count, self.plan = list(srcs), list(bufs), list(new), count, plan
        self.out, self.done = None, done

    @staticmethod
    def join(parts):
        parts = [p for p in parts if p is not None]
        if not parts:
            return None

        def plan(srcs, bufs, new):
            copies, s, b, n = [], 0, 0, 0
            for p in parts:
                copies += p.plan(srcs[s:s + len(p.srcs)], bufs[b:b + len(p.bufs)], new[n:n + len(p.new)])
                s, b, n = s + len(p.srcs), b + len(p.bufs), n + len(p.new)
            return copies

        whole = _Carry(sum((p.srcs for p in parts), []), sum((p.bufs for p in parts), []),
                       sum((p.new for p in parts), []), sum(p.count for p in parts), plan)
        whole.parts = parts
        return whole

    def share_out(self):
        b, n = 0, 0
        for p in getattr(self, "parts", []):
            p.out = (self.out[0][b:b + len(p.bufs)], self.out[1][n:n + len(p.new)])
            b, n = b + len(p.bufs), n + len(p.new)
            p.share_out()
        if self.done is not None:
            self.done(self)


def _carry_of(carries, key):
    return _Carry.join([make() for make in carries.get(key, [])])


def _pcall_carrying(body, carry, **kw):
    if carry is None:
        return _pcall(body, **kw)
    in_specs = list(kw.pop("in_specs"))
    out_specs, out_shape = kw.pop("out_specs"), kw.pop("out_shape")
    single = not isinstance(out_shape, (list, tuple))
    out_specs = [out_specs] if single else list(out_specs)
    out_shape = [out_shape] if single else list(out_shape)
    scratch = list(kw.pop("scratch_shapes", []))
    grid = tuple(kw.get("grid", ()))
    n_in, n_out, n_scr = len(in_specs), len(out_specs), len(scratch)
    ns, nb, nn = len(carry.srcs), len(carry.bufs), len(carry.new)

    def body2(*refs):
        ins, srcs = refs[:n_in], refs[n_in:n_in + ns]
        at = n_in + ns + nb
        outs, bufs, new = refs[at:at + n_out], refs[at + n_out:at + n_out + nb], refs[at + n_out + nb:at + n_out + nb + nn]
        at += n_out + nb + nn
        scr, (send_sems, recv_sems) = refs[at:at + n_scr], refs[at + n_scr:]

        def copies():
            return [_remote_copy(s, d, send_sems, recv_sems, k, to)
                    for k, (s, d, to) in enumerate(carry.plan(srcs, bufs, new))]

        def start():
            for cp in copies():
                cp.start()

        def wait():
            for cp in copies():
                cp.wait()

        if grid:
            ids = [pl.program_id(a) for a in range(len(grid))]
            pl.when(functools.reduce(jnp.logical_and, [i == 0 for i in ids]))(start)
        else:
            start()
        body(*ins, *outs, *scr)
        if grid:
            pl.when(functools.reduce(jnp.logical_and, [i == g - 1 for i, g in zip(ids, grid)]))(wait)
        else:
            wait()

    call = _pcall(
        body2, in_specs=in_specs + [HBM_SPEC] * (ns + nb), out_specs=out_specs + [HBM_SPEC] * (nb + nn),
        out_shape=out_shape + [jax.ShapeDtypeStruct(b.shape, b.dtype) for b in carry.bufs] + carry.new,
        scratch_shapes=scratch + [pltpu.SemaphoreType.DMA((carry.count,)), pltpu.SemaphoreType.DMA((carry.count,))],
        input_output_aliases={n_in + ns + j: n_out + j for j in range(nb)}, **kw)

    def apply(*args):
        res = call(*args, *carry.srcs, *carry.bufs)
        carry.out = (list(res[n_out:n_out + nb]), list(res[n_out + nb:]))
        carry.share_out()
        return res[0] if single else list(res[:n_out])

    return apply


def _pick(n, cap, mult=128):
    best = None
    d = mult
    while d <= min(n, cap):
        if n % d == 0:
            best = d
        d += mult
    return n if best is None else best


def _nbytes(shape, dtype):
    return int(np.prod(shape)) * jnp.dtype(dtype).itemsize


def _mm(a, b, *, mode, name, out_dtype=F32, alpha=1.0, bias=None, res=None, tm_cap=1024, tn_cap=512,
        tk_cap=2816, gathered=None, out_split=None, carry=None, norm_bwd=None):
    if mode == "tn":
        K, M = a.shape
    else:
        M, K = a.shape
    dn = {"nn": NN, "nt": NT, "tn": TN}[mode]
    b_rows = None
    if gathered is None:
        N = b.shape[0] if mode == "nt" else b.shape[1]
        tn = {None: _pick(N, tn_cap), "col": N // N_CHIPS, "row": N // 2}[out_split]
        if out_split == "col":
            tm_cap = min(tm_cap, M // 2)
        tk = K if K <= tk_cap else _pick(K, tk_cap)
        b_spec = (pl.BlockSpec((tn, tk), lambda i, j, k: (j, k)) if mode == "nt"
                  else pl.BlockSpec((tk, tn), lambda i, j, k: (k, j)))
    else:
        r, c = b.shape[1:]
        rows, cols = (r, N_CHIPS * c) if gathered == "col" else (N_CHIPS * r, c)
        N = rows if mode == "nt" else cols
        assert K == (cols if mode == "nt" else rows), (name, K, rows, cols)
        if gathered == "col" and mode == "nn":
            tn, tk = c, (r if r <= tk_cap else _pick(r, tk_cap))
            b_spec = pl.BlockSpec((None, tk, c), lambda i, j, k: (j, k, 0))
        elif gathered == "col":
            tn, tk = _pick(r, tn_cap), c
            b_spec = pl.BlockSpec((None, tn, c), lambda i, j, k: (k, j, 0))
        elif mode == "nn":
            tn, tk, b_rows = _pick(c, tn_cap), K, N_CHIPS
            b_spec = pl.BlockSpec((N_CHIPS, r, tn), lambda i, j, k: (0, 0, j))
        else:
            tn, tk, b_rows = 2 * r, K, 2
            b_spec = pl.BlockSpec((2, r, c), lambda i, j, k: (j, 0, 0))
    tm = _pick(M, tm_cap)
    nk = K // tk

    a_spec = (pl.BlockSpec((tk, tm), lambda i, j, k: (k, i)) if mode == "tn"
              else pl.BlockSpec((tm, tk), lambda i, j, k: (i, k)))
    in_specs = [a_spec, b_spec]
    args = [a, b]
    if bias is not None:
        in_specs.append(pl.BlockSpec((1, tn), lambda i, j, k: (0, j)))
        args.append(bias)
    if res is not None:
        in_specs.append(pl.BlockSpec((tm, tn), lambda i, j, k: (i, j)))
        args.append(res)
    if norm_bwd is not None:
        assert tn == N and alpha == 1.0 and out_split is None and bias is None and res is None, name
        x_in, g_in, dres_in = norm_bwd
        in_specs += [pl.BlockSpec((tm, tn), lambda i, j, k: (i, 0)), pl.BlockSpec((1, tn), lambda i, j, k: (0, 0)),
                     pl.BlockSpec((tm, tn), lambda i, j, k: (i, 0))]
        args += [x_in, g_in, dres_in]

    est = 2 * (_nbytes((tm, tk), a.dtype) + _nbytes((tk, tn), b.dtype) + _nbytes((tm, tn), out_dtype))
    est += _nbytes((tm, tn), F32) * (3 if res is not None else 1)
    est += _nbytes((tm, tn), F32) * (4 if norm_bwd is not None else 0)
    assert est < VMEM_BUDGET, (name, est)

    def body(*refs):
        a_ref, b_ref = refs[0], refs[1]
        pos = 2
        bias_ref = res_ref = None
        if bias is not None:
            bias_ref = refs[pos]
            pos += 1
        if res is not None:
            res_ref = refs[pos]
            pos += 1
        if norm_bwd is not None:
            x_ref, g_ref, dres_ref = refs[pos:pos + 3]
            pos += 3
        o_ref = refs[pos]
        if norm_bwd is not None:
            dg_ref = refs[pos + 1]
            pos += 1
        acc_ref = refs[pos + 1] if nk > 1 else None

        bv = b_ref[...]
        if b_rows is not None:
            bv = bv.reshape(b_rows * bv.shape[1], bv.shape[2])
        part = lax.dot_general(a_ref[...].astype(BF16), bv.astype(BF16), dn, preferred_element_type=F32)

        def finish(acc):
            if alpha != 1.0:
                acc = acc * alpha
            if bias_ref is not None:
                acc = acc + bias_ref[...]
            if res_ref is not None:
                acc = acc + res_ref[...]
            if norm_bwd is not None:
                xv = x_ref[...]
                rstd = lax.rsqrt(jnp.mean(xv * xv, axis=-1, keepdims=True) + RMS_EPS)
                xhat = xv * rstd
                dyg = acc * g_ref[...]
                part_g = jnp.sum(acc * xhat, axis=0, keepdims=True)
                acc = dres_ref[...] + rstd * (dyg - xhat * jnp.mean(dyg * xhat, axis=-1, keepdims=True))
                first = pl.program_id(0) == 0

                @pl.when(first)
                def _():
                    dg_ref[...] = part_g

                @pl.when(jnp.logical_not(first))
                def _():
                    dg_ref[...] += part_g
            o_ref[...] = acc.astype(out_dtype)

        if nk == 1:
            finish(part)
        else:
            k = pl.program_id(2)

            @pl.when(k == 0)
            def _():
                acc_ref[...] = part

            @pl.when(k > 0)
            def _():
                acc_ref[...] += part

            @pl.when(k == nk - 1)
            def _():
                finish(acc_ref[...])

    if out_split == "col":
        per_half = M // 2 // tm
        out_spec = pl.BlockSpec((None, None, tm, tn), lambda i, j, k: (i // per_half, j, i % per_half, 0))
        out_shape = jax.ShapeDtypeStruct((2, N_CHIPS, M // 2, tn), out_dtype)
    elif out_split == "row":
        out_spec = pl.BlockSpec((None, tm, tn), lambda i, j, k: (j, i, 0))
        out_shape = jax.ShapeDtypeStruct((2, M, tn), out_dtype)
    else:
        out_spec = pl.BlockSpec((tm, tn), lambda i, j, k: (i, j))
        out_shape = jax.ShapeDtypeStruct((M, N), out_dtype)
    semantics = ("parallel", "parallel", "arbitrary")
    if norm_bwd is not None:
        out_spec = [out_spec, pl.BlockSpec((1, tn), lambda i, j, k: (0, 0))]
        out_shape = [out_shape, jax.ShapeDtypeStruct((1, N), F32)]
        semantics = ("arbitrary", "arbitrary", "arbitrary")
    return _pcall_carrying(
        body, carry, name=name, grid=(M // tm, N // tn, nk), in_specs=in_specs, out_specs=out_spec,
        out_shape=out_shape,
        scratch_shapes=[pltpu.VMEM((tm, tn), F32)] if nk > 1 else [],
        compiler_params=pltpu.CompilerParams(dimension_semantics=semantics),
    )(*args)


def _rmsnorm_fwd(x, g, name):
    T, Dm = x.shape
    tm = _pick(T, 256, 8)

    def body(x_ref, g_ref, h_ref):
        xv = x_ref[...]
        rstd = lax.rsqrt(jnp.mean(xv * xv, axis=-1, keepdims=True) + RMS_EPS)
        h_ref[...] = (xv * rstd * g_ref[...]).astype(BF16)

    return _pcall(
        body, name=name, grid=(T // tm,),
        in_specs=[pl.BlockSpec((tm, Dm), lambda i: (i, 0)), pl.BlockSpec((1, Dm), lambda i: (0, 0))],
        out_specs=pl.BlockSpec((tm, Dm), lambda i: (i, 0)),
        out_shape=jax.ShapeDtypeStruct((T, Dm), BF16),
    )(x, g)


def _rmsnorm_bwd(x, g, dh, dres, name):
    T, Dm = x.shape
    tm = _pick(T, 256, 8)

    def body(x_ref, g_ref, dh_ref, dres_ref, dx_ref, dg_ref):
        xv = x_ref[...]
        rstd = lax.rsqrt(jnp.mean(xv * xv, axis=-1, keepdims=True) + RMS_EPS)
        xhat = xv * rstd
        dhv = dh_ref[...]
        dyg = dhv * g_ref[...]
        dx_ref[...] = dres_ref[...] + rstd * (dyg - xhat * jnp.mean(dyg * xhat, axis=-1, keepdims=True))
        part = jnp.sum(dhv * xhat, axis=0, keepdims=True)

        @pl.when(pl.program_id(0) == 0)
        def _():
            dg_ref[...] = part

        @pl.when(pl.program_id(0) > 0)
        def _():
            dg_ref[...] += part

    row = pl.BlockSpec((tm, Dm), lambda i: (i, 0))
    vec = pl.BlockSpec((1, Dm), lambda i: (0, 0))
    return _pcall(
        body, name=name, grid=(T // tm,), in_specs=[row, vec, row, row], out_specs=[row, vec],
        out_shape=[jax.ShapeDtypeStruct((T, Dm), F32), jax.ShapeDtypeStruct((1, Dm), F32)],
        compiler_params=pltpu.CompilerParams(dimension_semantics=("arbitrary",)),
    )(x, g, dh, dres)


def _final_loss(x, g, tgt, name):
    T, Dm = x.shape
    tm = _pick(T, 256, 8)

    def body(x_ref, g_ref, t_ref, dx_ref, dg_ref, loss_ref):
        xv = x_ref[...]
        gv = g_ref[...]
        rstd = lax.rsqrt(jnp.mean(xv * xv, axis=-1, keepdims=True) + RMS_EPS)
        xhat = xv * rstd
        err = xhat * gv - t_ref[...]
        part_loss = 0.5 * jnp.sum(jnp.mean(err * err, axis=-1, keepdims=True), axis=0, keepdims=True)
        dy = err * (1.0 / Dm)
        dyg = dy * gv
        dx_ref[...] = rstd * (dyg - xhat * jnp.mean(dyg * xhat, axis=-1, keepdims=True))
        part_g = jnp.sum(dy * xhat, axis=0, keepdims=True)
        part_l = jnp.broadcast_to(part_loss, (1, 128))

        @pl.when(pl.program_id(0) == 0)
        def _():
            dg_ref[...] = part_g
            loss_ref[...] = part_l

        @pl.when(pl.program_id(0) > 0)
        def _():
            dg_ref[...] += part_g
            loss_ref[...] += part_l

    row = pl.BlockSpec((tm, Dm), lambda i: (i, 0))
    vec = pl.BlockSpec((1, Dm), lambda i: (0, 0))
    return _pcall(
        body, name=name, grid=(T // tm,), in_specs=[row, vec, row],
        out_specs=[row, vec, pl.BlockSpec((1, 128), lambda i: (0, 0))],
        out_shape=[jax.ShapeDtypeStruct((T, Dm), F32), jax.ShapeDtypeStruct((1, Dm), F32),
                   jax.ShapeDtypeStruct((1, 128), F32)],
        compiler_params=pltpu.CompilerParams(dimension_semantics=("arbitrary",)),
    )(x, g, tgt)


def _sigmoid(z):
    return 1.0 / (1.0 + jnp.exp(-z))


def _ffn_in_swiglu(h, w_in, name, carry=None):
    T, Dm = h.shape
    cw = w_in.shape[2]
    tm = _pick(T, 512, 16)

    def body(h_ref, wg_ref, wu_ref, g_ref, u_ref, a_ref):
        hv = h_ref[...]
        gv = jnp.dot(hv, wg_ref[...], preferred_element_type=F32)
        uv = jnp.dot(hv, wu_ref[...], preferred_element_type=F32)
        g_ref[...] = gv.astype(BF16)
        u_ref[...] = uv.astype(BF16)
        a_ref[...] = (gv * _sigmoid(gv) * uv).astype(BF16)

    out = pl.BlockSpec((tm, cw), lambda i, j: (i, j))
    return _pcall_carrying(
        body, carry, name=name, grid=(T // tm, 2),
        in_specs=[pl.BlockSpec((tm, Dm), lambda i, j: (i, 0)), pl.BlockSpec((None, Dm, cw), lambda i, j: (j, 0, 0)),
                  pl.BlockSpec((None, Dm, cw), lambda i, j: (j + 2, 0, 0))],
        out_specs=[out, out, out], out_shape=[jax.ShapeDtypeStruct((T, D_FF), BF16)] * 3,
    )(h, w_in, w_in)


def _swiglu_bwd(g, u, da, name):
    T = g.shape[0]
    tm = _pick(T, 256, 16)

    def body(g_ref, u_ref, da_ref, d_ref):
        gv = g_ref[...].astype(F32)
        uv = u_ref[...].astype(F32)
        dav = da_ref[...].astype(F32)
        sg = _sigmoid(gv)
        d_ref[:, :D_FF] = (dav * uv * (sg + gv * sg * (1.0 - sg))).astype(BF16)
        d_ref[:, D_FF:] = (dav * gv * sg).astype(BF16)

    row = pl.BlockSpec((tm, D_FF), lambda i: (i, 0))
    return _pcall(
        body, name=name, grid=(T // tm,), in_specs=[row, row, row],
        out_specs=pl.BlockSpec((tm, 2 * D_FF), lambda i: (i, 0)),
        out_shape=jax.ShapeDtypeStruct((T, 2 * D_FF), BF16),
    )(g, u, da)


GATE_BLOCK0 = QKV_WIDTH // D_MODEL


def _gate_fwd(p, ya, yb, yc, name):
    T = p.shape[0]
    tm = _pick(T, 256, 8)

    def body(ga_ref, gb_ref, gc_ref, ya_ref, yb_ref, yc_ref, o_ref):
        o_ref[...] = (_sigmoid(ga_ref[...]) * ya_ref[...] + _sigmoid(gb_ref[...]) * yb_ref[...]
                      + _sigmoid(gc_ref[...]) * yc_ref[...]).astype(BF16)

    gate = [pl.BlockSpec((tm, D_MODEL), functools.partial(lambda i, kk: (i, GATE_BLOCK0 + kk), kk=kk))
            for kk in range(3)]
    row = pl.BlockSpec((tm, D_MODEL), lambda i: (i, 0))
    return _pcall(
        body, name=name, grid=(T // tm,), in_specs=gate + [row, row, row], out_specs=row,
        out_shape=jax.ShapeDtypeStruct((T, D_MODEL), BF16),
    )(p, p, p, ya, yb, yc)


def _gate_bwd(p, ya, yb, yc, dm, name):
    T = p.shape[0]
    tm = _pick(T, 256, 16)

    def body(ga_ref, gb_ref, gc_ref, ya_ref, yb_ref, yc_ref, dm_ref, da_ref, db_ref, dc_ref, dg_ref):
        dmv = dm_ref[...]
        for kk, (g_ref, y_ref, d_ref) in enumerate(((ga_ref, ya_ref, da_ref), (gb_ref, yb_ref, db_ref),
                                                    (gc_ref, yc_ref, dc_ref))):
            s = _sigmoid(g_ref[...])
            d_ref[...] = (s * dmv).astype(BF16)
            dg_ref[:, kk * D_MODEL:(kk + 1) * D_MODEL] = (dmv * y_ref[...] * s * (1.0 - s)).astype(BF16)

    gate = [pl.BlockSpec((tm, D_MODEL), functools.partial(lambda i, kk: (i, GATE_BLOCK0 + kk), kk=kk))
            for kk in range(3)]
    row = pl.BlockSpec((tm, D_MODEL), lambda i: (i, 0))
    return _pcall(
        body, name=name, grid=(T // tm,), in_specs=gate + [row, row, row, row],
        out_specs=[row, row, row, pl.BlockSpec((tm, GATE_WIDTH), lambda i: (i, 0))],
        out_shape=[jax.ShapeDtypeStruct((T, D_MODEL), BF16)] * 3 + [jax.ShapeDtypeStruct((T, GATE_WIDTH), BF16)],
    )(p, p, p, ya, yb, yc, dm)


def _colsum(a, name):
    T, N = a.shape
    tm = _pick(T, 256, 16)

    def body(a_ref, o_ref):
        part = jnp.sum(a_ref[...].astype(F32), axis=0, keepdims=True)

        @pl.when(pl.program_id(0) == 0)
        def _():
            o_ref[...] = part

        @pl.when(pl.program_id(0) > 0)
        def _():
            o_ref[...] += part

    return _pcall(
        body, name=name, grid=(T // tm,), in_specs=[pl.BlockSpec((tm, N), lambda i: (i, 0))],
        out_specs=pl.BlockSpec((1, N), lambda i: (0, 0)), out_shape=jax.ShapeDtypeStruct((1, N), F32),
        compiler_params=pltpu.CompilerParams(dimension_semantics=("arbitrary",)),
    )(a)


def _adamw(w, g, m, v, name):
    R, C = w.shape
    tr = 256 if R % 256 == 0 else R
    c1 = 1.0 / (1.0 - ADAM_B1 ** ADAM_STEP)
    c2 = 1.0 / (1.0 - ADAM_B2 ** ADAM_STEP)

    def body(w_ref, g_ref, m_ref, v_ref, d_ref, nm_ref, nv_ref):
        gv = g_ref[...]
        mn = ADAM_B1 * m_ref[...] + (1.0 - ADAM_B1) * gv
        vn = ADAM_B2 * v_ref[...] + (1.0 - ADAM_B2) * (gv * gv)
        nm_ref[...] = mn
        nv_ref[...] = vn
        d_ref[...] = -ADAM_LR * ((mn * c1) / (jnp.sqrt(vn * c2) + ADAM_EPS) + ADAM_WD * w_ref[...])

    spec = pl.BlockSpec((tr, C), lambda i: (i, 0))
    return _pcall(
        body, name=name, grid=(R // tr,), in_specs=[spec] * 4, out_specs=[spec] * 3,
        out_shape=[jax.ShapeDtypeStruct((R, C), F32)] * 3,
    )(w, g, m, v)


def _log_sigmoid(z):
    return jnp.minimum(z, 0.0) - jnp.log(1.0 + jnp.exp(-jnp.abs(z)))


def _dot3(x, m01):
    x1 = x.astype(BF16)
    r1 = x - x1.astype(F32)
    x2 = r1.astype(BF16)
    x3 = (r1 - x2.astype(F32)).astype(BF16)
    n = x.shape[0]
    if n % 16:
        return (jnp.dot(x1, m01, preferred_element_type=F32) + jnp.dot(x2, m01, preferred_element_type=F32)
                + jnp.dot(x3, m01, preferred_element_type=F32))
    y = jnp.dot(jnp.concatenate([x1, x2, x3], axis=0), m01, preferred_element_type=F32)
    return y[:n] + y[n:2 * n] + y[2 * n:]


def _dot_nt(a, b):
    return lax.dot_general(a, b, NT, preferred_element_type=F32)


def _dot_tn(a, b):
    return lax.dot_general(a, b, TN, preferred_element_type=F32)


def _iota2(shape):
    return lax.broadcasted_iota(jnp.int32, shape, 0), lax.broadcasted_iota(jnp.int32, shape, 1)


HEADS_PER_STEP = 4
HEADS = range(HEADS_PER_STEP)


CHUNK_HEADS_PER_STEP = 4
CHUNK_HEADS = range(CHUNK_HEADS_PER_STEP)


def _head_spec(T, rows=None, width=HEAD_DIM, heads=HEADS_PER_STEP):
    return pl.BlockSpec((heads, T if rows is None else rows, width), lambda g: (g, 0, 0))


def _sb_weights(qb, kb, t0, s0, racc, m_gt, row, col):
    z = _dot_nt(qb, kb) * SCALE
    strict = (s0 + col) < (t0 + row)
    lb = _log_sigmoid(z)
    lf = jnp.where(strict, lb - z, 0.0)
    between = _dot3(lf, m_gt) + racc
    w = jnp.where(strict, jnp.exp(lb + between), 0.0)
    return strict, lb, lf, w


def _sb_fwd(q, k, v, name, carry=None):
    H, T, _ = q.shape
    nb = T // QB

    def body(q_ref, k_ref, v_ref, o_ref):
        row, col = _iota2((QB, QB))
        m_gt = (row > col).astype(BF16)

        def qblock(i, _):
            t0 = pl.multiple_of(i * QB, QB)
            qbs = [q_ref[h, pl.ds(t0, QB), :].astype(BF16) for h in HEADS]

            def kblock(jj, carry):
                s0 = pl.multiple_of((i - jj) * QB, QB)
                out = []
                for h in HEADS:
                    acc, racc = carry[h]
                    kb = k_ref[h, pl.ds(s0, QB), :].astype(BF16)
                    vb = v_ref[h, pl.ds(s0, QB), :].astype(BF16)
                    _, _, lf, w = _sb_weights(qbs[h], kb, t0, s0, racc, m_gt, row, col)
                    acc = acc + jnp.dot(w.astype(BF16), vb, preferred_element_type=F32)
                    out.append((acc, racc + jnp.sum(lf, axis=1, keepdims=True)))
                return tuple(out)

            res = lax.fori_loop(0, i + 1, kblock,
                                tuple((jnp.zeros((QB, HEAD_DIM), F32), jnp.zeros((QB, 1), F32)) for _ in HEADS))
            for h in HEADS:
                o_ref[h, pl.ds(t0, QB), :] = res[h][0].astype(BF16)
            return 0

        lax.fori_loop(0, nb, qblock, 0)

    spec = _head_spec(T)
    return _pcall_carrying(body, carry, name=name, grid=(H // HEADS_PER_STEP,), in_specs=[spec] * 3, out_specs=spec,
                           out_shape=jax.ShapeDtypeStruct((H, T, HEAD_DIM), BF16))(q, k, v)


def _sb_bwd(q, k, v, do, name, carry=None):
    H, T, _ = q.shape
    nb = T // QB

    def body(q_ref, k_ref, v_ref, do_ref, dq_ref, dk_out, dv_out, racc_ref, dk_ref, dv_ref):
        row, col = _iota2((QB, QB))
        m_gt = (row > col).astype(BF16)
        m_lt = (row < col).astype(BF16)
        dk_ref[...] = jnp.zeros_like(dk_ref)
        dv_ref[...] = jnp.zeros_like(dv_ref)

        def qblock(i, _):
            t0 = pl.multiple_of(i * QB, QB)
            qbs = [q_ref[h, pl.ds(t0, QB), :].astype(BF16) for h in HEADS]
            dobs = [do_ref[h, pl.ds(t0, QB), :].astype(BF16) for h in HEADS]

            def suffix(jj, raccs):
                j = i - jj
                s0 = pl.multiple_of(j * QB, QB)
                out = []
                for h in HEADS:
                    kb = k_ref[h, pl.ds(s0, QB), :].astype(BF16)
                    z = _dot_nt(qbs[h], kb) * SCALE
                    lf = jnp.where((s0 + col) < (t0 + row), _log_sigmoid(z) - z, 0.0)
                    racc_ref[h, j] = raccs[h]
                    out.append(raccs[h] + jnp.sum(lf, axis=1, keepdims=True))
                return tuple(out)

            lax.fori_loop(0, i + 1, suffix, tuple(jnp.zeros((QB, 1), F32) for _ in HEADS))

            def kblock(j, carry):
                s0 = pl.multiple_of(j * QB, QB)
                out = []
                for h in HEADS:
                    dq, cacc = carry[h]
                    kb = k_ref[h, pl.ds(s0, QB), :].astype(BF16)
                    vb = v_ref[h, pl.ds(s0, QB), :].astype(BF16)
                    strict, lb, _, w = _sb_weights(qbs[h], kb, t0, s0, racc_ref[h, j], m_gt, row, col)
                    e = _dot_nt(dobs[h], vb) * w
                    c_left = _dot3(e, m_lt) + cacc
                    sig = jnp.exp(lb)
                    dz = jnp.where(strict, e * (1.0 - sig) - c_left * sig, 0.0).astype(BF16)
                    dq = dq + jnp.dot(dz, kb, preferred_element_type=F32)
                    dk_ref[h, pl.ds(s0, QB), :] += _dot_tn(dz, qbs[h]) * SCALE
                    dv_ref[h, pl.ds(s0, QB), :] += _dot_tn(w.astype(BF16), dobs[h])
                    out.append((dq, cacc + jnp.sum(e, axis=1, keepdims=True)))
                return tuple(out)

            res = lax.fori_loop(0, i + 1, kblock,
                                tuple((jnp.zeros((QB, HEAD_DIM), F32), jnp.zeros((QB, 1), F32)) for _ in HEADS))
            for h in HEADS:
                dq_ref[h, pl.ds(t0, QB), :] = (res[h][0] * SCALE).astype(BF16)
            return 0

        lax.fori_loop(0, nb, qblock, 0)
        dk_out[...] = dk_ref[...].astype(BF16)
        dv_out[...] = dv_ref[...].astype(BF16)

    spec = _head_spec(T)
    acc = pltpu.VMEM((HEADS_PER_STEP, T, HEAD_DIM), F32)
    return _pcall_carrying(body, carry, name=name, grid=(H // HEADS_PER_STEP,), in_specs=[spec] * 4,
                           out_specs=[spec] * 3, out_shape=[jax.ShapeDtypeStruct((H, T, HEAD_DIM), BF16)] * 3,
                           scratch_shapes=[pltpu.VMEM((HEADS_PER_STEP, nb, QB, 1), F32), acc, acc])(q, k, v, do)


def _fox_logits(qb, kb, fq, fk, t0, s0, row, col):
    z = _dot_nt(qb, kb) * SCALE + fq - fk
    return jnp.where((s0 + col) <= (t0 + row), z, NEG)


def _fox_specs(T):
    return _head_spec(T, width=1), _head_spec(T, rows=T // QB, width=QB)


def _fox_fwd(q, k, v, fq, fk, name, carry=None):
    H, T, _ = q.shape
    nb = T // QB

    def body(q_ref, k_ref, v_ref, fq_ref, fk_ref, o_ref, lse_ref):
        row, col = _iota2((QB, QB))

        def qblock(i, _):
            t0 = pl.multiple_of(i * QB, QB)
            qbs = [q_ref[h, pl.ds(t0, QB), :].astype(BF16) for h in HEADS]
            fqs = [fq_ref[h, pl.ds(t0, QB), :] for h in HEADS]

            def kblock(j, carry):
                s0 = pl.multiple_of(j * QB, QB)
                out = []
                for h in HEADS:
                    acc, m, l = carry[h]
                    kb = k_ref[h, pl.ds(s0, QB), :].astype(BF16)
                    vb = v_ref[h, pl.ds(s0, QB), :].astype(BF16)
                    z = _fox_logits(qbs[h], kb, fqs[h], fk_ref[h, pl.ds(j, 1), :], t0, s0, row, col)
                    m_new = jnp.maximum(m, jnp.max(z, axis=1, keepdims=True))
                    a = jnp.exp(m - m_new)
                    p = jnp.exp(z - m_new)
                    acc = a * acc + jnp.dot(p.astype(BF16), vb, preferred_element_type=F32)
                    out.append((acc, m_new, a * l + jnp.sum(p, axis=1, keepdims=True)))
                return tuple(out)

            res = lax.fori_loop(0, i + 1, kblock,
                                tuple((jnp.zeros((QB, HEAD_DIM), F32), jnp.full((QB, 1), NEG, F32),
                                       jnp.zeros((QB, 1), F32)) for _ in HEADS))
            for h in HEADS:
                acc, m, l = res[h]
                o_ref[h, pl.ds(t0, QB), :] = (acc / l).astype(BF16)
                lse_ref[h, pl.ds(t0, QB), :] = m + jnp.log(l)
            return 0

        lax.fori_loop(0, nb, qblock, 0)

    spec = _head_spec(T)
    fq_spec, fk_spec = _fox_specs(T)
    return _pcall_carrying(
        body, carry, name=name, grid=(H // HEADS_PER_STEP,), in_specs=[spec] * 3 + [fq_spec, fk_spec],
        out_specs=[spec, fq_spec],
        out_shape=[jax.ShapeDtypeStruct((H, T, HEAD_DIM), BF16), jax.ShapeDtypeStruct((H, T, 1), F32)],
    )(q, k, v, fq, fk)


def _fox_bwd(q, k, v, fq, fk, lse, do, name, carry=None):
    H, T, _ = q.shape
    nb = T // QB

    def body(q_ref, k_ref, v_ref, fq_ref, fk_ref, lse_ref, do_ref, dq_ref, dk_out, dv_out, dfk_ref, dk_ref, dv_ref):
        row, col = _iota2((QB, QB))
        dk_ref[...] = jnp.zeros_like(dk_ref)
        dv_ref[...] = jnp.zeros_like(dv_ref)
        dfk_ref[...] = jnp.zeros_like(dfk_ref)

        def qblock(i, _):
            t0 = pl.multiple_of(i * QB, QB)
            qbs = [q_ref[h, pl.ds(t0, QB), :].astype(BF16) for h in HEADS]
            fqs = [fq_ref[h, pl.ds(t0, QB), :] for h in HEADS]
            lses = [lse_ref[h, pl.ds(t0, QB), :] for h in HEADS]
            dobs = [do_ref[h, pl.ds(t0, QB), :].astype(BF16) for h in HEADS]

            def probs(h, j):
                s0 = pl.multiple_of(j * QB, QB)
                kb = k_ref[h, pl.ds(s0, QB), :].astype(BF16)
                vb = v_ref[h, pl.ds(s0, QB), :].astype(BF16)
                z = _fox_logits(qbs[h], kb, fqs[h], fk_ref[h, pl.ds(j, 1), :], t0, s0, row, col)
                return s0, kb, jnp.exp(z - lses[h]), _dot_nt(dobs[h], vb)

            def row_dot(j, deltas):
                out = []
                for h in HEADS:
                    _, _, p, dp = probs(h, j)
                    out.append(deltas[h] + jnp.sum(p * dp, axis=1, keepdims=True))
                return tuple(out)

            deltas = lax.fori_loop(0, i + 1, row_dot, tuple(jnp.zeros((QB, 1), F32) for _ in HEADS))

            def kblock(j, dqs):
                out = []
                for h in HEADS:
                    s0, kb, p, dp = probs(h, j)
                    dz = p * (dp - deltas[h])
                    dzb = dz.astype(BF16)
                    dk_ref[h, pl.ds(s0, QB), :] += _dot_tn(dzb, qbs[h]) * SCALE
                    dv_ref[h, pl.ds(s0, QB), :] += _dot_tn(p.astype(BF16), dobs[h])
                    dfk_ref[h, pl.ds(j, 1), :] += -jnp.sum(dz, axis=0, keepdims=True)
                    out.append(dqs[h] + jnp.dot(dzb, kb, preferred_element_type=F32))
                return tuple(out)

            dqs = lax.fori_loop(0, i + 1, kblock, tuple(jnp.zeros((QB, HEAD_DIM), F32) for _ in HEADS))
            for h in HEADS:
                dq_ref[h, pl.ds(t0, QB), :] = (dqs[h] * SCALE).astype(BF16)
            return 0

        lax.fori_loop(0, nb, qblock, 0)
        dk_out[...] = dk_ref[...].astype(BF16)
        dv_out[...] = dv_ref[...].astype(BF16)

    spec = _head_spec(T)
    fq_spec, fk_spec = _fox_specs(T)
    acc = pltpu.VMEM((HEADS_PER_STEP, T, HEAD_DIM), F32)
    return _pcall_carrying(body, carry, name=name, grid=(H // HEADS_PER_STEP,),
                           in_specs=[spec] * 3 + [fq_spec, fk_spec, fq_spec, spec],
                           out_specs=[spec, spec, spec, fk_spec],
                           out_shape=[jax.ShapeDtypeStruct((H, T, HEAD_DIM), BF16)] * 3
                           + [jax.ShapeDtypeStruct((H, nb, QB), F32)],
                           scratch_shapes=[acc, acc])(q, k, v, fq, fk, lse, do)


def _forget_cumsum(flog, name):
    R, T = flog.shape
    nb = T // QB

    def body(x_ref, o_ref):
        row, col = _iota2((QB, QB))
        m_le = (row <= col).astype(BF16)
        carry = jnp.zeros((R, 1), F32)
        for b in range(nb):
            lf = _log_sigmoid(x_ref[:, b * QB:(b + 1) * QB])
            o_ref[:, b * QB:(b + 1) * QB] = _dot3(lf, m_le) + carry
            carry = carry + jnp.sum(lf, axis=1, keepdims=True)

    spec = pl.BlockSpec((R, T), lambda: (0, 0))
    return _pcall(body, name=name, in_specs=[spec], out_specs=spec,
                  out_shape=jax.ShapeDtypeStruct((R, T), F32))(flog)


def _forget_cumsum_bwd(flog, df, name):
    R, T = flog.shape
    nb = T // QB

    def body(x_ref, df_ref, o_ref):
        row, col = _iota2((QB, QB))
        m_ge = (row >= col).astype(BF16)
        carry = jnp.zeros((R, 1), F32)
        for b in reversed(range(nb)):
            dfb = df_ref[:, b * QB:(b + 1) * QB]
            dlog = _dot3(dfb, m_ge) + carry
            o_ref[:, b * QB:(b + 1) * QB] = dlog * _sigmoid(-x_ref[:, b * QB:(b + 1) * QB])
            carry = carry + jnp.sum(dfb, axis=1, keepdims=True)

    spec = pl.BlockSpec((R, T), lambda: (0, 0))
    return _pcall(body, name=name, in_specs=[spec, spec], out_specs=spec,
                  out_shape=jax.ShapeDtypeStruct((R, T), F32))(flog, df)


def _chunk_probs(qc, kb, bias, c, col):
    z = _dot_nt(qc, kb) * SCALE + bias
    z = jnp.where((c - (LEFT_CHUNKS + 1)) * CHUNK + col >= jnp.maximum(0, (c - LEFT_CHUNKS) * CHUNK), z, NEG)
    p = jnp.exp(z - jnp.max(z, axis=1, keepdims=True))
    return p, jnp.sum(p, axis=1, keepdims=True)


def _chunk_specs(T, n=CHUNK_HEADS_PER_STEP):
    return (_head_spec(T, heads=n), _head_spec(T, rows=T + BAND_PAD, heads=n),
            _head_spec(T, rows=CHUNK, width=BAND, heads=n))


def _chunk_fwd(q, kp, vp, bias, name, carry=None):
    H, T, _ = q.shape
    nc = T // CHUNK

    def body(q_ref, kp_ref, vp_ref, bias_ref, o_ref):
        _, col = _iota2((CHUNK, BAND))

        def chunk(c, _):
            t0 = pl.multiple_of(c * CHUNK, CHUNK)
            for h in HEADS:
                qc = q_ref[h, pl.ds(t0, CHUNK), :].astype(BF16)
                kb = kp_ref[h, pl.ds(t0, BAND), :].astype(BF16)
                vb = vp_ref[h, pl.ds(t0, BAND), :].astype(BF16)
                p, l = _chunk_probs(qc, kb, bias_ref[h], c, col)
                o = jnp.dot(p.astype(BF16), vb, preferred_element_type=F32) / l
                o_ref[h, pl.ds(t0, CHUNK), :] = o.astype(BF16)
            return 0

        lax.fori_loop(0, nc, chunk, 0)

    q_spec, kp_spec, b_spec = _chunk_specs(T, HEADS_PER_STEP)
    return _pcall_carrying(body, carry, name=name, grid=(H // HEADS_PER_STEP,),
                           in_specs=[q_spec, kp_spec, kp_spec, b_spec], out_specs=q_spec,
                           out_shape=jax.ShapeDtypeStruct((H, T, HEAD_DIM), BF16))(q, kp, vp, bias)


def _chunk_bwd(q, kp, vp, bias, do, name, carry=None):
    H, T, _ = q.shape
    nc = T // CHUNK

    def body(q_ref, kp_ref, vp_ref, bias_ref, do_ref, dq_ref, dk_out, dv_out, db_ref, dkp_ref, dvp_ref):
        _, col = _iota2((CHUNK, BAND))
        dkp_ref[...] = jnp.zeros_like(dkp_ref)
        dvp_ref[...] = jnp.zeros_like(dvp_ref)
        db_ref[...] = jnp.zeros_like(db_ref)

        def chunk(c, _):
            t0 = pl.multiple_of(c * CHUNK, CHUNK)
            for h in CHUNK_HEADS:
                qc = q_ref[h, pl.ds(t0, CHUNK), :].astype(BF16)
                kb = kp_ref[h, pl.ds(t0, BAND), :].astype(BF16)
                vb = vp_ref[h, pl.ds(t0, BAND), :].astype(BF16)
                dob = do_ref[h, pl.ds(t0, CHUNK), :].astype(BF16)
                p, l = _chunk_probs(qc, kb, bias_ref[h], c, col)
                p = p / l
                dp = _dot_nt(dob, vb)
                dz = p * (dp - jnp.sum(p * dp, axis=1, keepdims=True))
                dzb = dz.astype(BF16)
                dq = jnp.dot(dzb, kb, preferred_element_type=F32) * SCALE
                dq_ref[h, pl.ds(t0, CHUNK), :] = dq.astype(BF16)
                dkp_ref[h, pl.ds(t0, BAND), :] += _dot_tn(dzb, qc) * SCALE
                dvp_ref[h, pl.ds(t0, BAND), :] += _dot_tn(p.astype(BF16), dob)
                db_ref[h] += dz
            return 0

        lax.fori_loop(0, nc, chunk, 0)
        dk_out[...] = dkp_ref[:, BAND_PAD:, :].astype(BF16)
        dv_out[...] = dvp_ref[:, BAND_PAD:, :].astype(BF16)

    q_spec, kp_spec, b_spec = _chunk_specs(T)
    acc = pltpu.VMEM((CHUNK_HEADS_PER_STEP, T + BAND_PAD, HEAD_DIM), F32)
    return _pcall_carrying(body, carry, name=name, grid=(H // CHUNK_HEADS_PER_STEP,),
                           in_specs=[q_spec, kp_spec, kp_spec, b_spec, q_spec],
                           out_specs=[q_spec, q_spec, q_spec, b_spec],
                           out_shape=[jax.ShapeDtypeStruct((H, T, HEAD_DIM), BF16)] * 3
                           + [jax.ShapeDtypeStruct((H, CHUNK, BAND), F32)],
                           scratch_shapes=[acc, acc])(q, kp, vp, bias, do)


HBM_SPEC = pl.BlockSpec(memory_space=pltpu.HBM)


def _position():
    return lax.axis_index("x"), lax.axis_index("y"), lax.axis_index("c")


def _other_chips(x, y):
    chips = [(1 - x, y), (x, 1 - y), (1 - x, 1 - y)]
    return [(chip, 2 * chip[0] + chip[1]) for chip in chips]


def _remote_copy(src, dst, send_sems, recv_sems, k, to):
    return pltpu.make_async_remote_copy(src_ref=src, dst_ref=dst, send_sem=send_sems.at[k], recv_sem=recv_sems.at[k],
                                        device_id=to, device_id_type=MESH)


def _place_own(w, chip, name):
    _, r, c = w.shape
    tr = _pick(r, 256, 16)

    def body(chip_ref, w_ref, *out_refs):
        for l, o_ref in enumerate(out_refs):
            o_ref[...] = w_ref[l].astype(BF16)

    grid_spec = pltpu.PrefetchScalarGridSpec(
        num_scalar_prefetch=1, grid=(r // tr,), in_specs=[pl.BlockSpec((DEPTH, tr, c), lambda i, ch: (0, i, 0))],
        out_specs=[pl.BlockSpec((None, tr, c), lambda i, ch: (ch[0], i, 0))] * DEPTH)
    return _pcall(body, name=name, grid_spec=grid_spec,
                  out_shape=[jax.ShapeDtypeStruct((N_CHIPS, r, c), BF16)] * DEPTH)(chip, w)


def _gather_over_ici(srcs, bufs, new):
    x, y, c = _position()
    me = 2 * x + y
    return [(b.at[me, c], b.at[me, c], (*chip, c)) for b in bufs for chip, _ in _other_chips(x, y)]


def _gather_to_sibling(srcs, bufs, new):
    x, y, c = _position()
    return [(b.at[idx, c], b.at[idx, c], (x, y, 1 - c)) for b in bufs for _, idx in _other_chips(x, y)]


def _gather_layer(bufs, name):
    n = len(bufs)

    def body(*refs):
        dst = refs[n:2 * n]
        send_sems, recv_sems = refs[2 * n:]
        x, y, c = _position()
        me = 2 * x + y
        sibling = (x, y, 1 - c)
        others = _other_chips(x, y)
        first = [_remote_copy(dst[i].at[me, c], dst[i].at[me, c], send_sems, recv_sems, 3 * i + k, (*chip, c))
                 for i in range(n) for k, (chip, _) in enumerate(others)]
        for cp in first:
            cp.start()
        passed = []
        for i in range(n):
            for k, (_, idx) in enumerate(others):
                landed = dst[i].at[idx, c]
                _remote_copy(landed, landed, send_sems, recv_sems, 3 * i + k, sibling).wait_recv()
                passed.append(_remote_copy(landed, landed, send_sems, recv_sems, 3 * (n + i) + k, sibling))
                passed[-1].start()
        for i in range(n):
            for k, (_, idx) in enumerate(others):
                from_sibling = dst[i].at[idx, 1 - c]
                _remote_copy(from_sibling, from_sibling, send_sems, recv_sems, 3 * (n + i) + k, sibling).wait_recv()
        for cp in first + passed:
            cp.wait_send()

    return _pcall(
        body, name=name, in_specs=[HBM_SPEC] * n, out_specs=[HBM_SPEC] * n,
        out_shape=[jax.ShapeDtypeStruct(b.shape, b.dtype) for b in bufs],
        scratch_shapes=[pltpu.SemaphoreType.DMA((6 * n,)), pltpu.SemaphoreType.DMA((6 * n,))],
        input_output_aliases={i: i for i in range(n)},
    )(*bufs)


def _send_other_half(parts, name):
    n = len(parts)

    def body(*refs):
        src, got = refs[:n], refs[n:2 * n]
        send_sems, recv_sems = refs[2 * n:]
        x, y, c = _position()
        copies = [_remote_copy(src[i].at[1 - c], got[i], send_sems, recv_sems, i, (x, y, 1 - c)) for i in range(n)]
        for cp in copies:
            cp.start()
        for cp in copies:
            cp.wait()

    return _pcall(
        body, name=name, in_specs=[HBM_SPEC] * n, out_specs=[HBM_SPEC] * n,
        out_shape=[jax.ShapeDtypeStruct(p.shape[1:], p.dtype) for p in parts],
        scratch_shapes=[pltpu.SemaphoreType.DMA((n,)), pltpu.SemaphoreType.DMA((n,))],
    )(*parts)


def _scatter_chips(parts, name):
    n = len(parts)

    def body(*refs):
        src, dst = refs[:n], refs[n:2 * n]
        send_sems, recv_sems = refs[2 * n:]
        x, y, c = _position()
        others = _other_chips(x, y)
        sends = [_remote_copy(src[i].at[idx], dst[i].at[k], send_sems, recv_sems, 3 * i + k, (*chip, c))
                 for i in range(n) for k, (chip, idx) in enumerate(others)]
        for cp in sends:
            cp.start()
        for cp in sends:
            cp.wait()

    return _pcall(
        body, name=name, in_specs=[HBM_SPEC] * n, out_specs=[HBM_SPEC] * n,
        out_shape=[jax.ShapeDtypeStruct((3,) + p.shape[1:], p.dtype) for p in parts],
        scratch_shapes=[pltpu.SemaphoreType.DMA((3 * n,)), pltpu.SemaphoreType.DMA((3 * n,))],
    )(*parts)


def _swap_with_sibling(arrs, name):
    n = len(arrs)

    def body(*refs):
        src, dst = refs[:n], refs[n:2 * n]
        send_sems, recv_sems = refs[2 * n:]
        x, y, c = _position()
        copies = [_remote_copy(src[i], dst[i], send_sems, recv_sems, i, (x, y, 1 - c)) for i in range(n)]
        for cp in copies:
            cp.start()
        for cp in copies:
            cp.wait()

    return _pcall(
        body, name=name, in_specs=[HBM_SPEC] * n, out_specs=[HBM_SPEC] * n,
        out_shape=[jax.ShapeDtypeStruct(a.shape, a.dtype) for a in arrs],
        scratch_shapes=[pltpu.SemaphoreType.DMA((n,)), pltpu.SemaphoreType.DMA((n,))],
    )(*arrs)


def _allreduce_small(v, name):
    R, C = v.shape

    def body(v_ref, o_ref, slots, send_sems, recv_sems):
        x, y, c = _position()
        me = 4 * x + 2 * y + c
        slots[0] = v_ref[...]
        sends = []
        for r in range(1, 8):
            fx, fy, fc = (r >> 2) & 1, (r >> 1) & 1, r & 1
            to = (x ^ fx, y ^ fy, c ^ fc)
            cp = pltpu.make_async_remote_copy(src_ref=v_ref, dst_ref=slots.at[r], send_sem=send_sems.at[r - 1],
                                              recv_sem=recv_sems.at[r - 1], device_id=to, device_id_type=MESH)
            cp.start()
            sends.append(cp)
        for cp in sends:
            cp.wait()
        acc = slots[me]
        for d in range(1, 8):
            acc = acc + slots[d ^ me]
        o_ref[...] = acc

    vmem = pl.BlockSpec(memory_space=pltpu.VMEM)
    return _pcall(
        body, name=name, in_specs=[vmem], out_specs=vmem, out_shape=jax.ShapeDtypeStruct((R, C), F32),
        scratch_shapes=[pltpu.VMEM((8, R, C), F32), pltpu.SemaphoreType.DMA((7,)), pltpu.SemaphoreType.DMA((7,))],
    )(v)


def _add_pair(which, both, got, name):
    _, N, R, C = both.shape
    tr = _pick(R, 512, 16)

    def body(which_ref, a_ref, b_ref, o_ref):
        o_ref[...] = (a_ref[...].astype(F32) + b_ref[...].astype(F32)).astype(BF16)

    spec = pl.BlockSpec((None, tr, C), lambda n, i, w: (n, i, 0))
    grid_spec = pltpu.PrefetchScalarGridSpec(
        num_scalar_prefetch=1, grid=(N, R // tr),
        in_specs=[pl.BlockSpec((None, None, tr, C), lambda n, i, w: (w[0], n, i, 0)), spec], out_specs=spec)
    return _pcall(body, name=name, grid_spec=grid_spec,
                  out_shape=jax.ShapeDtypeStruct((N, R, C), BF16))(which, both, got)


def _sum_chips(which, own, others, name):
    N, R, C = others.shape
    tr = _pick(R, 512, 16)

    def body(which_ref, own_ref, p_ref, o_ref):
        acc = own_ref[...].astype(F32)
        for n in range(N):
            acc = acc + p_ref[n].astype(F32)
        o_ref[...] = acc

    grid_spec = pltpu.PrefetchScalarGridSpec(
        num_scalar_prefetch=1, grid=(R // tr,),
        in_specs=[pl.BlockSpec((None, tr, C), lambda i, w: (w[0], i, 0)), pl.BlockSpec((N, tr, C), lambda i, w: (0, i, 0))],
        out_specs=pl.BlockSpec((tr, C), lambda i, w: (i, 0)))
    return _pcall(body, name=name, grid_spec=grid_spec,
                  out_shape=jax.ShapeDtypeStruct((R, C), F32))(which, own, others)


BIG = ("w_ffn1_in", "w_ffn1_out", "w_in", "w_br_sb", "w_br_ch", "w_br_fox", "w_out", "w_ffn2_in", "w_ffn2_out")
ROW_SHARDED = ("w_ffn1_out", "w_out", "w_ffn2_out")
SMALL = ("g_ffn1", "g_mix", "b_in", "rel_bias", "g_ffn2", "g_final")
SHARD_SHAPES = {
    "w_ffn1_in": (D_MODEL, 2 * D_FF // N_CHIPS), "w_ffn1_out": (D_FF // N_CHIPS, D_MODEL),
    "w_in": (D_MODEL, IN_WIDTH // N_CHIPS), "w_br_sb": (W_SB, D_MODEL // N_CHIPS),
    "w_br_ch": (W_CH, D_MODEL // N_CHIPS), "w_br_fox": (W_FOX, D_MODEL // N_CHIPS),
    "w_out": (D_MODEL // N_CHIPS, D_MODEL), "w_ffn2_in": (D_MODEL, 2 * D_FF // N_CHIPS),
    "w_ffn2_out": (D_FF // N_CHIPS, D_MODEL),
}


def _pair_sums(split, tag):
    core = lax.axis_index("c").astype(jnp.int32)[None]
    got = _send_other_half(split, f"grad_split_{tag}")
    return [_add_pair(core, a, b, f"grad_pair_sum_{tag}_{i}") for i, (a, b) in enumerate(zip(split, got))]


def _scatter_plan(srcs, bufs, new):
    x, y, c = _position()
    return [(s.at[idx], d.at[k], (*chip, c)) for s, d in zip(srcs, new) for k, (chip, idx) in enumerate(_other_chips(x, y))]


def _scatter_carry(pair):
    landing = [jax.ShapeDtypeStruct((3,) + p.shape[1:], p.dtype) for p in pair]
    return _Carry(pair, [], landing, 3 * len(pair), _scatter_plan)


def _finish_grads(pair, landed, axes, tag):
    chip = (2 * lax.axis_index("x") + lax.axis_index("y")).astype(jnp.int32)[None]
    mine = [_sum_chips(chip, p, q, f"grad_chip_sum_{tag}_{i}") for i, (p, q) in enumerate(zip(pair, landed))]
    theirs = _swap_with_sibling(mine, f"grad_share_{tag}")
    first = lax.axis_index("c") == 0
    return [jnp.concatenate([jnp.where(first, a, b), jnp.where(first, b, a)], axis=ax)
            for a, b, ax in zip(mine, theirs, axes)]


SMALL_SIZES = {"g_ffn1": DEPTH * D_MODEL, "g_mix": DEPTH * D_MODEL, "b_in": DEPTH * IN_WIDTH,
               "rel_bias": DEPTH * N_REL * H_CH, "g_ffn2": DEPTH * D_MODEL, "g_final": D_MODEL}
SMALL_TOTAL = sum(SMALL_SIZES.values()) + 1
SMALL_ROWS = -(-SMALL_TOTAL // (8 * 128)) * 8


def _pack_small(vals, extra):
    flat = [vals[n].reshape(-1) for n in SMALL] + [extra.reshape(-1)]
    flat.append(jnp.zeros((SMALL_ROWS * 128 - SMALL_TOTAL,), F32))
    return jnp.concatenate(flat).reshape(SMALL_ROWS, 128)


def _unpack_small(pack, shapes):
    flat, out, off = pack.reshape(-1), {}, 0
    for n in SMALL:
        out[n] = flat[off:off + SMALL_SIZES[n]].reshape(shapes[n])
        off += SMALL_SIZES[n]
    return out, flat[off]


def _to_heads(t, n_heads):
    return jnp.transpose(t.reshape(t.shape[0], n_heads, HEAD_DIM), (1, 0, 2)).astype(BF16)


def _from_heads(t):
    return jnp.transpose(t, (1, 0, 2)).reshape(t.shape[1], t.shape[0] * HEAD_DIM)


def _pad_keys(t):
    return jnp.pad(t, ((0, 0), (BAND_PAD, 0), (0, 0)))


DIAGONALS = CHUNK + BAND - 1


def _band_bias(table):
    n_far = (LEFT_CHUNKS + 1) * CHUNK + CHUNK - MAX_REL
    near = table[N_REL - 1 - (DIAGONALS - n_far):N_REL - 1][::-1]
    diag = jnp.concatenate([jnp.broadcast_to(table[N_REL - 1], (n_far, H_CH)), near], axis=0).T
    diag = jnp.pad(diag, ((0, 0), (0, 1)))
    skew = jnp.tile(diag, (1, CHUNK))[:, :CHUNK * DIAGONALS].reshape(H_CH, CHUNK, DIAGONALS)
    return skew[:, :, CHUNK - 1:]


def _pad_w_in(w):
    qkv, f, gates = w[:, :QKV_WIDTH], w[:, QKV_WIDTH:QKV_WIDTH + H_FOX], w[:, QKV_WIDTH + H_FOX:]
    return jnp.concatenate([qkv, gates, f, jnp.zeros((w.shape[0], F_PAD - H_FOX), w.dtype)], axis=1)


def _unpad_w_in(w):
    return jnp.concatenate([w[:, :QKV_WIDTH], w[:, QKV_WIDTH + GATE_WIDTH:QKV_WIDTH + GATE_WIDTH + H_FOX],
                            w[:, QKV_WIDTH:QKV_WIDTH + GATE_WIDTH]], axis=1)


QKV_SPLITS = np.cumsum([0, W_SB, W_SB, W_SB, W_CH, W_CH, W_CH, W_FOX, W_FOX, W_FOX])


def _by_chip_rows(g):
    return g.reshape(2, N_CHIPS, g.shape[1] // N_CHIPS, g.shape[2])


def _ffn_fwd(x, g, w_in, w_out, tag, carries):
    h = _rmsnorm_fwd(x, g, f"{tag}_norm")
    gate, up, a = _ffn_in_swiglu(h, w_in, f"{tag}_in", _carry_of(carries, "in"))
    y = _mm(a, w_out, mode="nn", name=f"{tag}_out", alpha=0.5, res=x, gathered="row",
            carry=_carry_of(carries, "out"))
    return y, (h, gate, up, a)


def _ffn_bwd(x, g, w_in, w_out, saved, dy, tag, carries):
    h, gate, up, a = saved
    da = _mm(dy, w_out, mode="nt", name=f"{tag}_da", alpha=0.5, gathered="row", out_dtype=BF16,
             carry=_carry_of(carries, "da"))
    dw_out = _mm(a, dy, mode="tn", name=f"{tag}_dwout", alpha=0.5, tm_cap=1408, out_dtype=BF16, out_split="row",
                 carry=_carry_of(carries, "dwout"))
    dgu = _swiglu_bwd(gate, up, da, f"{tag}_dact")
    dw_in = _mm(h, dgu, mode="tn", name=f"{tag}_dwin", tm_cap=512, out_dtype=BF16, out_split="col",
                carry=_carry_of(carries, "dwin"))
    dx, dg = _mm(dgu, w_in, mode="nt", name=f"{tag}_dh", gathered="col", tn_cap=1024, carry=_carry_of(carries, "dh"),
                 norm_bwd=(x, g, dy))
    return dx, dg, dw_in, _by_chip_rows(dw_out)


def _mixer_fwd(x, lw, tag, carries):
    T = x.shape[0]
    h = _rmsnorm_fwd(x, lw["g_mix"], f"{tag}_norm")
    p = _mm(h, lw["w_in"], mode="nn", name=f"{tag}_proj", bias=lw["b_in"], tn_cap=896,
            carry=_carry_of(carries, "proj"))
    parts = [p[:, QKV_SPLITS[i]:QKV_SPLITS[i + 1]] for i in range(9)]
    qa, ka, va = (_to_heads(t, H_SB) for t in parts[0:3])
    qb, kb, vb = (_to_heads(t, H_CH) for t in parts[3:6])
    qc, kc, vc = (_to_heads(t, H_FOX) for t in parts[6:9])
    flog = _mm(lw["w_forget_t"], h, mode="nt", name=f"{tag}_flog")[:8] + lw["b_forget"]
    fcum = _forget_cumsum(flog, f"{tag}_fcum")[:H_FOX]
    fq, fk = fcum.reshape(H_FOX, T, 1), fcum.reshape(H_FOX, T // QB, QB)
    kbp, vbp = _pad_keys(kb), _pad_keys(vb)
    oa = _sb_fwd(qa, ka, va, f"{tag}_sb", _carry_of(carries, "sb"))
    ob = _chunk_fwd(qb, kbp, vbp, lw["bias"], f"{tag}_ch", _carry_of(carries, "ch"))
    oc, lse = _fox_fwd(qc, kc, vc, fq, fk, f"{tag}_fox", _carry_of(carries, "fox"))
    oam, obm, ocm = _from_heads(oa), _from_heads(ob), _from_heads(oc)
    ya = _mm(oam, lw["w_br_sb"], mode="nn", name=f"{tag}_bra", gathered="col")
    yb = _mm(obm, lw["w_br_ch"], mode="nn", name=f"{tag}_brb", gathered="col")
    yc = _mm(ocm, lw["w_br_fox"], mode="nn", name=f"{tag}_brc", gathered="col")
    merged = _gate_fwd(p, ya, yb, yc, f"{tag}_gate")
    y = _mm(merged, lw["w_out"], mode="nn", name=f"{tag}_out", res=x, gathered="row")
    saved = (h, p, (qa, ka, va), (qb, kbp, vbp), (qc, kc, vc), flog, fq, fk, lse, (oam, obm, ocm),
             (ya, yb, yc), merged)
    return y, saved


def _mixer_bwd(x, lw, saved, dy, tag, carries):
    (h, p, (qa, ka, va), (qb, kbp, vbp), (qc, kc, vc), flog, fq, fk, lse, (oam, obm, ocm),
     (ya, yb, yc), merged) = saved
    T = x.shape[0]
    grads = {}
    col, row = "col", "row"
    dmerged = _mm(dy, lw["w_out"], mode="nt", name=f"{tag}_dmerged", gathered=row,
                  carry=_carry_of(carries, "dmerged"))
    grads["w_out"] = _by_chip_rows(_mm(merged, dy, mode="tn", name=f"{tag}_dwout", tm_cap=1024, out_dtype=BF16,
                                       out_split="row"))
    dya, dyb, dyc, dgate = _gate_bwd(p, ya, yb, yc, dmerged, f"{tag}_dgate")
    doa = _mm(dya, lw["w_br_sb"], mode="nt", name=f"{tag}_doa", gathered=col)
    dob = _mm(dyb, lw["w_br_ch"], mode="nt", name=f"{tag}_dob", gathered=col)
    doc = _mm(dyc, lw["w_br_fox"], mode="nt", name=f"{tag}_doc", gathered=col)
    by_chip = dict(mode="tn", tm_cap=512, out_dtype=BF16, out_split="col")
    grads["w_br_sb"] = _mm(oam, dya, name=f"{tag}_dwbra", **by_chip)
    grads["w_br_ch"] = _mm(obm, dyb, name=f"{tag}_dwbrb", **by_chip)
    grads["w_br_fox"] = _mm(ocm, dyc, name=f"{tag}_dwbrc", **by_chip)
    dqa, dka, dva = _sb_bwd(qa, ka, va, _to_heads(doa, H_SB), f"{tag}_dsb", _carry_of(carries, "dsb"))
    dqb, dkb, dvb, dbias = _chunk_bwd(qb, kbp, vbp, lw["bias"], _to_heads(dob, H_CH), f"{tag}_dch",
                                      _carry_of(carries, "dch"))
    dqc, dkc, dvc, dfk = _fox_bwd(qc, kc, vc, fq, fk, lse, _to_heads(doc, H_FOX), f"{tag}_dfox",
                                  _carry_of(carries, "dfox"))
    dflog = _forget_cumsum_bwd(flog, jnp.pad(dfk.reshape(H_FOX, T), ((0, 8 - H_FOX), (0, 0))), f"{tag}_dfcum")
    dqkv = [_from_heads(t) for t in (dqa, dka, dva, dqb, dkb, dvb, dqc, dkc, dvc)]
    dforget = jnp.pad(dflog[:H_FOX].T, ((0, 0), (0, F_PAD - H_FOX))).astype(BF16)
    dpb = jnp.concatenate(dqkv + [dgate, dforget], axis=1)
    grads["b_in"] = _colsum(dpb, f"{tag}_dbin")
    dw_in =_unpad_w_in(_mm(h, dpb, mode="tn", name=f"{tag}_dwin", tm_cap=512, tn_cap=896, out_dtype=BF16))
    grads["w_in"] = jnp.transpose(dw_in.reshape(2, D_MODEL // 2, N_CHIPS, IN_WIDTH // N_CHIPS), (0, 2, 1, 3))
    dx, grads["g_mix"] = _mm(dpb, lw["w_in"], mode="nt", name=f"{tag}_dh", tk_cap=896, tn_cap=1024,
                             norm_bwd=(x, lw["g_mix"], dy))
    grads["rel_bias"] = lw["bias_vjp"](dbias)[0]
    return dx, grads


def kernel(x, g_ffn1, w_ffn1_in, w_ffn1_out, g_mix, w_in, b_in, rel_bias, w_br_sb, w_br_ch, w_br_fox, w_out, g_ffn2, w_ffn2_in, w_ffn2_out, g_final, loss_target, m_g_ffn1, m_w_ffn1_in, m_w_ffn1_out, m_g_mix, m_w_in, m_b_in, m_rel_bias, m_w_br_sb, m_w_br_ch, m_w_br_fox, m_w_out, m_g_ffn2, m_w_ffn2_in, m_w_ffn2_out, m_g_final, v_g_ffn1, v_w_ffn1_in, v_w_ffn1_out, v_g_mix, v_w_in, v_b_in, v_rel_bias, v_w_br_sb, v_w_br_ch, v_w_br_fox, v_w_out, v_g_ffn2, v_w_ffn2_in, v_w_ffn2_out, v_g_final):
    names = ("g_ffn1", "w_ffn1_in", "w_ffn1_out", "g_mix", "w_in", "b_in", "rel_bias", "w_br_sb", "w_br_ch",
             "w_br_fox", "w_out", "g_ffn2", "w_ffn2_in", "w_ffn2_out", "g_final")
    w = dict(zip(names, (g_ffn1, w_ffn1_in, w_ffn1_out, g_mix, w_in, b_in, rel_bias, w_br_sb, w_br_ch,
                         w_br_fox, w_out, g_ffn2, w_ffn2_in, w_ffn2_out, g_final)))
    m = dict(zip(names, (m_g_ffn1, m_w_ffn1_in, m_w_ffn1_out, m_g_mix, m_w_in, m_b_in, m_rel_bias, m_w_br_sb,
                         m_w_br_ch, m_w_br_fox, m_w_out, m_g_ffn2, m_w_ffn2_in, m_w_ffn2_out, m_g_final)))
    v = dict(zip(names, (v_g_ffn1, v_w_ffn1_in, v_w_ffn1_out, v_g_mix, v_w_in, v_b_in, v_rel_bias, v_w_br_sb,
                         v_w_br_ch, v_w_br_fox, v_w_out, v_g_ffn2, v_w_ffn2_in, v_w_ffn2_out, v_g_final)))
    xs = x[0]
    tgt = loss_target[0]

    chip = (2 * lax.axis_index("x") + lax.axis_index("y")).astype(jnp.int32)[None]
    placed = [_place_own(w[n], chip, f"place_{n}") for n in BIG]
    bufs = [[p[l].reshape(N_CHIPS, 2, p[l].shape[1] // 2, p[l].shape[2]) for p in placed] for l in range(DEPTH)]
    padded_w_in = {}

    def layer_weights(l):
        lw = {n: b.reshape((N_CHIPS,) + w[n].shape[1:]) for n, b in zip(BIG, bufs[l])}
        if l not in padded_w_in:
            whole = jnp.concatenate([lw["w_in"][j] for j in range(N_CHIPS)], axis=1)
            forget_t = jnp.pad(whole[:, QKV_WIDTH:QKV_WIDTH + H_FOX].T, ((0, F_PAD - H_FOX), (0, 0)))
            padded_w_in[l] = (_pad_w_in(whole), forget_t)
        lw["w_in"], lw["w_forget_t"] = padded_w_in[l]
        lw["b_forget"] = jnp.pad(w["b_in"][l][QKV_WIDTH:QKV_WIDTH + H_FOX], (0, 8 - H_FOX))[:, None]
        lw["b_in"] = _pad_w_in(w["b_in"][l][None, :])
        lw["bias"], lw["bias_vjp"] = jax.vjp(_band_bias, w["rel_bias"][l])
        for n in ("g_ffn1", "g_mix", "g_ffn2"):
            lw[n] = w[n][l][None, :]
        return lw

    def landed_in(l, idx):
        def done(carry):
            for i, b in zip(idx, carry.out[0]):
                bufs[l][i] = b
        return done

    def over_ici(l, idx):
        return lambda: _Carry([], [bufs[l][i] for i in idx], [], 3 * len(idx), _gather_over_ici, landed_in(l, idx))

    def to_sibling(l, idx):
        return lambda: _Carry([], [bufs[l][i] for i in idx], [], 3 * len(idx), _gather_to_sibling, landed_in(l, idx))

    def ffn_fwd(x_in, lw, which, tag, carries):
        return _ffn_fwd(x_in, lw[f"g_{which}"], lw[f"w_{which}_in"], lw[f"w_{which}_out"], tag, carries)

    def ffn_bwd(x_in, lw, which, saved_acts, dy, tag, carries):
        return _ffn_bwd(x_in, lw[f"g_{which}"], lw[f"w_{which}_in"], lw[f"w_{which}_out"], saved_acts, dy, tag,
                        carries)

    FFN1, W_IN, MIXER_REST, FFN2_IN, FFN2_OUT = (0, 1), (2,), (3, 4, 5, 6), (7,), (8,)
    first = FFN1 + W_IN
    for i, b in zip(first, _gather_layer([bufs[0][i] for i in first], "gather_l0")):
        bufs[0][i] = b
    fwd_carries = [
        {"ffn1": {"in": [over_ici(0, MIXER_REST)], "out": [to_sibling(0, MIXER_REST), over_ici(0, FFN2_OUT)]},
         "mix": {"proj": [to_sibling(0, FFN2_OUT), over_ici(1, MIXER_REST)],
                 "sb": [over_ici(0, FFN2_IN), over_ici(1, FFN2_OUT)],
                 "ch": [to_sibling(0, FFN2_IN), over_ici(1, FFN1)],
                 "fox": [over_ici(1, W_IN)]},
         "ffn2": {"in": [to_sibling(1, MIXER_REST + FFN2_OUT + FFN1 + W_IN)]}},
        {"ffn1": {}, "mix": {"sb": [over_ici(1, FFN2_IN)], "ch": [to_sibling(1, FFN2_IN)]}, "ffn2": {}},
    ]

    saved = []
    act = xs
    for l in range(DEPTH):
        x0 = act
        x1, s1 = ffn_fwd(x0, layer_weights(l), "ffn1", f"l{l}_ffn1", fwd_carries[l]["ffn1"])
        x2, s2 = _mixer_fwd(x1, layer_weights(l), f"l{l}_mix", fwd_carries[l]["mix"])
        act, s3 = ffn_fwd(x2, layer_weights(l), "ffn2", f"l{l}_ffn2", fwd_carries[l]["ffn2"])
        saved.append((x0, s1, x1, s2, x2, s3))
    layers = [layer_weights(l) for l in range(DEPTH)]

    dact, dg_final, loss_row = _final_loss(act, w["g_final"][None, :], tgt, "final_loss")

    big_grads = {n: [None] * DEPTH for n in BIG}
    small_grads = {n: [None] * DEPTH for n in ("g_ffn1", "g_mix", "b_in", "rel_bias", "g_ffn2")}
    pair = [[None] * len(BIG) for _ in range(DEPTH)]
    landed = [[None] * len(BIG) for _ in range(DEPTH)]

    def pair_up(l, idx, tag):
        for i, p in zip(idx, _pair_sums([big_grads[BIG[i]][l] for i in idx], tag)):
            pair[l][i] = p

    core = lax.axis_index("c").astype(jnp.int32)[None]

    def to_sibling_half(l, idx, tag):
        def plan(srcs, bufs, new):
            x, y, c = _position()
            return [(s.at[1 - c], d, (x, y, 1 - c)) for s, d in zip(srcs, new)]
        def done(carry):
            for j, (i, got) in enumerate(zip(idx, carry.out[1])):
                pair[l][i] = _add_pair(core, big_grads[BIG[i]][l], got, f"grad_pair_sum_{tag}_{j}")
        def make():
            split = [big_grads[BIG[i]][l] for i in idx]
            return _Carry(split, [], [jax.ShapeDtypeStruct(s.shape[1:], s.dtype) for s in split], len(idx), plan, done)
        return make

    def exchange(l, idx):
        def done(carry):
            for i, a in zip(idx, carry.out[1]):
                landed[l][i] = a
        def make():
            carry = _scatter_carry([pair[l][i] for i in idx])
            carry.done = done
            return carry
        return make

    def exchange_one_way(l, i, k):
        def plan(srcs, bufs, new):
            x, y, c = _position()
            chip_k, idx_k = _other_chips(x, y)[k]
            return [(srcs[0].at[idx_k], bufs[0].at[k], (*chip_k, c))]
        def done(carry):
            landed[l][i] = carry.out[0][0]
        def make():
            if landed[l][i] is None:
                landed[l][i] = lax.empty((3,) + pair[l][i].shape[1:], BF16)
            return _Carry([pair[l][i]], [landed[l][i]], [], 1, plan, done)
        return make

    bwd_carries = [
        {"ffn2": {},
         "mix": {"dmerged": [to_sibling_half(0, FFN2_IN + FFN2_OUT, "l0_ffn2")],
                 "dsb": [exchange(1, FFN1 + W_IN)], "dch": [exchange(1, MIXER_REST + FFN2_IN + FFN2_OUT)],
                 "dfox": [exchange(0, FFN2_IN + FFN2_OUT)]},
         "ffn1": {"da": [exchange(0, MIXER_REST)], "dwout": [exchange_one_way(0, 2, 0)],
                  "dwin": [exchange_one_way(0, 2, 1)], "dh": [exchange_one_way(0, 2, 2)]}},
        {"ffn2": {}, "mix": {},
         "ffn1": {"da": [to_sibling_half(1, W_IN + MIXER_REST + FFN2_IN + FFN2_OUT, "l1_rest")]}},
    ]
    for l in reversed(range(DEPTH)):
        lw = layers[l]
        x0, s1, x1, s2, x2, s3 = saved[l]
        dx2, small_grads["g_ffn2"][l], big_grads["w_ffn2_in"][l], big_grads["w_ffn2_out"][l] = ffn_bwd(
            x2, lw, "ffn2", s3, dact, f"l{l}_ffn2", bwd_carries[l]["ffn2"])
        dx1, mg = _mixer_bwd(x1, lw, s2, dx2, f"l{l}_mix", bwd_carries[l]["mix"])
        for n in ("w_in", "w_br_sb", "w_br_ch", "w_br_fox", "w_out"):
            big_grads[n][l] = mg[n]
        small_grads["b_in"][l] = _unpad_w_in(mg["b_in"])[0]
        small_grads["g_mix"][l] = mg["g_mix"][0]
        small_grads["rel_bias"][l] = mg["rel_bias"]
        if l == 0:
            pair_up(0, W_IN + MIXER_REST, "l0_mix")
        dact, dg1, big_grads["w_ffn1_in"][l], big_grads["w_ffn1_out"][l] = ffn_bwd(
            x0, lw, "ffn1", s1, dx1, f"l{l}_ffn1", bwd_carries[l]["ffn1"])
        small_grads["g_ffn1"][l] = dg1[0]
        small_grads["g_ffn2"][l] = small_grads["g_ffn2"][l][0]
        if l == 1:
            pair_up(1, FFN1, "l1_ffn1")
    grad_x = dact[None]
    pair_up(0, FFN1, "l0_ffn1")
    for i, a in zip(FFN1, _scatter_chips([pair[0][i] for i in FFN1], "grad_scatter_l0")):
        landed[0][i] = a

    small = {n: jnp.stack(small_grads[n]) for n in small_grads}
    small["g_final"] = dg_final[0]
    small_sum, loss = _unpack_small(_allreduce_small(_pack_small(small, loss_row[0, :1]), "allreduce_small"),
                                    {n: w[n].shape for n in SMALL})

    axes = [1 if n in ROW_SHARDED else 0 for n in BIG] * DEPTH
    summed = _finish_grads(pair[0] + pair[1], landed[0] + landed[1], axes, "all")
    grads = {n: jnp.stack([summed[i], summed[len(BIG) + i]]) for i, n in enumerate(BIG)}
    grads.update(small_sum)

    delta, new_m, new_v = {}, {}, {}
    for n in BIG:
        shape = w[n].shape
        flat = lambda t: t.reshape(-1, shape[-1])
        d, nm, nv = _adamw(flat(w[n]), flat(grads[n]), flat(m[n]), flat(v[n]), f"adamw_{n}")
        delta[n], new_m[n], new_v[n] = d.reshape(shape), nm.reshape(shape), nv.reshape(shape)
    zero = jnp.zeros((1,), F32)
    sd, sm, sv = _adamw(_pack_small(w, zero), _pack_small(grads, zero), _pack_small(m, zero), _pack_small(v, zero),
                        "adamw_small")
    shapes = {n: w[n].shape for n in SMALL}
    for dst, src in ((delta, sd), (new_m, sm), (new_v, sv)):
        dst.update(_unpack_small(src, shapes)[0])

    return (loss, grad_x, *[grads[n] for n in names], *[delta[n] for n in names],
            *[new_m[n] for n in names], *[new_v[n] for n in names])
```

```python
import functools

import numpy as np
import jax
import jax.numpy as jnp
from jax import lax
from jax.experimental import pallas as pl
from jax.experimental.pallas import tpu as pltpu

F32 = jnp.float32
BF16 = jnp.bfloat16

D_MODEL = 1024
DEPTH = 2
CHUNK = 64
HEAD_DIM = 64
H_SB, H_CH, H_FOX = 4, 8, 4
W_SB, W_CH, W_FOX = H_SB * HEAD_DIM, H_CH * HEAD_DIM, H_FOX * HEAD_DIM
LEFT_CHUNKS = 8
MAX_REL = 128
N_REL = 2 * MAX_REL + 1
D_FF = 2816
QKV_WIDTH = 3 * (W_SB + W_CH + W_FOX)
GATE_WIDTH = 3 * D_MODEL
IN_WIDTH = QKV_WIDTH + H_FOX + GATE_WIDTH
F_PAD = 128
P_WIDTH = QKV_WIDTH + GATE_WIDTH + F_PAD
RMS_EPS = 1e-6
NEG = -1e30
SCALE = HEAD_DIM ** -0.5
QB = 256
BAND = (LEFT_CHUNKS + 2) * CHUNK
BAND_PAD = BAND - CHUNK

ADAM_LR, ADAM_B1, ADAM_B2, ADAM_EPS, ADAM_WD, ADAM_STEP = 0.001, 0.9, 0.999, 1e-08, 0.01, 10

N_CHIPS = 4
PACK_COLS = 1024
VMEM_BUDGET = 52 * 1024 * 1024

NT = (((1,), (1,)), ((), ()))
TN = (((0,), (0,)), ((), ()))
NN = (((1,), (0,)), ((), ()))
MESH = pl.DeviceIdType.MESH


def _pcall(body, **kw):
    return pl.pallas_call(body, **kw)


class _Carry:
    def __init__(self, srcs, bufs, new, count, plan, done=None):
        self.srcs, self.bufs, self.new, self.count, self.plan = list(srcs), list(bufs), list(new), count, plan
        self.out, self.done = None, done

    @staticmethod
    def join(parts):
        parts = [p for p in parts if p is not None]
        if not parts:
            return None

        def plan(srcs, bufs, new):
            copies, s, b, n = [], 0, 0, 0
            for p in parts:
                copies += p.plan(srcs[s:s + len(p.srcs)], bufs[b:b + len(p.bufs)], new[n:n + len(p.new)])
                s, b, n = s + len(p.srcs), b + len(p.bufs), n + len(p.new)
            return copies

        whole = _Carry(sum((p.srcs for p in parts), []), sum((p.bufs for p in parts), []),
                       sum((p.new for p in parts), []), sum(p.count for p in parts), plan)
        whole.parts = parts
        return whole

    def share_out(self):
        b, n = 0, 0
        for p in getattr(self, "parts", []):
            p.out = (self.out[0][b:b + len(p.bufs)], self.out[1][n:n + len(p.new)])
            b, n = b + len(p.bufs), n + len(p.new)
            p.share_out()
        if self.done is not None:
            self.done(self)


def _carry_of(carries, key):
    return _Carry.join([make() for make in carries.get(key, [])])


def _pcall_carrying(body, carry, **kw):
    if carry is None:
        return _pcall(body, **kw)
    in_specs = list(kw.pop("in_specs"))
    out_specs, out_shape = kw.pop("out_specs"), kw.pop("out_shape")
    single = not isinstance(out_shape, (list, tuple))
    out_specs = [out_specs] if single else list(out_specs)
    out_shape = [out_shape] if single else list(out_shape)
    scratch = list(kw.pop("scratch_shapes", []))
    grid = tuple(kw.get("grid", ()))
    n_in, n_out, n_scr = len(in_specs), len(out_specs), len(scratch)
    ns, nb, nn = len(carry.srcs), len(carry.bufs), len(carry.new)

    def body2(*refs):
        ins, srcs = refs[:n_in], refs[n_in:n_in + ns]
        at = n_in + ns + nb
        outs, bufs, new = refs[at:at + n_out], refs[at + n_out:at + n_out + nb], refs[at + n_out + nb:at + n_out + nb + nn]
        at += n_out + nb + nn
        scr, (send_sems, recv_sems) = refs[at:at + n_scr], refs[at + n_scr:]

        def copies():
            return [_remote_copy(s, d, send_sems, recv_sems, k, to)
                    for k, (s, d, to) in enumerate(carry.plan(srcs, bufs, new))]

        def start():
            for cp in copies():
                cp.start()

        def wait():
            for cp in copies():
                cp.wait()

        if grid:
            ids = [pl.program_id(a) for a in range(len(grid))]
            pl.when(functools.reduce(jnp.logical_and, [i == 0 for i in ids]))(start)
        else:
            start()
        body(*ins, *outs, *scr)
        if grid:
            pl.when(functools.reduce(jnp.logical_and, [i == g - 1 for i, g in zip(ids, grid)]))(wait)
        else:
            wait()

    call = _pcall(
        body2, in_specs=in_specs + [HBM_SPEC] * (ns + nb), out_specs=out_specs + [HBM_SPEC] * (nb + nn),
        out_shape=out_shape + [jax.ShapeDtypeStruct(b.shape, b.dtype) for b in carry.bufs] + carry.new,
        scratch_shapes=scratch + [pltpu.SemaphoreType.DMA((carry.count,)), pltpu.SemaphoreType.DMA((carry.count,))],
        input_output_aliases={n_in + ns + j: n_out + j for j in range(nb)}, **kw)

    def apply(*args):
        res = call(*args, *carry.srcs, *carry.bufs)
        carry.out = (list(res[n_out:n_out + nb]), list(res[n_out + nb:]))
        carry.share_out()
        return res[0] if single else list(res[:n_out])

    return apply


def _pick(n, cap, mult=128):
    best = None
    d = mult
    while d <= min(n, cap):
        if n % d == 0:
            best = d
        d += mult
    return n if best is None else best


def _nbytes(shape, dtype):
    return int(np.prod(shape)) * jnp.dtype(dtype).itemsize


def _mm(a, b, *, mode, name, out_dtype=F32, alpha=1.0, bias=None, res=None, tm_cap=1024, tn_cap=512,
        tk_cap=2816, gathered=None, out_split=None, carry=None, norm_bwd=None):
    if mode == "tn":
        K, M = a.shape
    else:
        M, K = a.shape
    dn = {"nn": NN, "nt": NT, "tn": TN}[mode]
    b_rows = None
    if gathered is None:
        N = b.shape[0] if mode == "nt" else b.shape[1]
        tn = {None: _pick(N, tn_cap), "col": N // N_CHIPS, "row": N // 2}[out_split]
        if out_split == "col":
            tm_cap = min(tm_cap, M // 2)
        tk = K if K <= tk_cap else _pick(K, tk_cap)
        b_spec = (pl.BlockSpec((tn, tk), lambda i, j, k: (j, k)) if mode == "nt"
                  else pl.BlockSpec((tk, tn), lambda i, j, k: (k, j)))
    else:
        r, c = b.shape[1:]
        rows, cols = (r, N_CHIPS * c) if gathered == "col" else (N_CHIPS * r, c)
        N = rows if mode == "nt" else cols
        assert K == (cols if mode == "nt" else rows), (name, K, rows, cols)
        if gathered == "col" and mode == "nn":
            tn, tk = c, (r if r <= tk_cap else _pick(r, tk_cap))
            b_spec = pl.BlockSpec((None, tk, c), lambda i, j, k: (j, k, 0))
        elif gathered == "col":
            tn, tk = _pick(r, tn_cap), c
            b_spec = pl.BlockSpec((None, tn, c), lambda i, j, k: (k, j, 0))
        elif mode == "nn":
            tn, tk, b_rows = _pick(c, tn_cap), K, N_CHIPS
            b_spec = pl.BlockSpec((N_CHIPS, r, tn), lambda i, j, k: (0, 0, j))
        else:
            tn, tk, b_rows = 2 * r, K, 2
            b_spec = pl.BlockSpec((2, r, c), lambda i, j, k: (j, 0, 0))
    tm = _pick(M, tm_cap)
    nk = K // tk

    a_spec = (pl.BlockSpec((tk, tm), lambda i, j, k: (k, i)) if mode == "tn"
              else pl.BlockSpec((tm, tk), lambda i, j, k: (i, k)))
    in_specs = [a_spec, b_spec]
    args = [a, b]
    if bias is not None:
        in_specs.append(pl.BlockSpec((1, tn), lambda i, j, k: (0, j)))
        args.append(bias)
    if res is not None:
        in_specs.append(pl.BlockSpec((tm, tn), lambda i, j, k: (i, j)))
        args.append(res)
    if norm_bwd is not None:
        assert tn == N and alpha == 1.0 and out_split is None and bias is None and res is None, name
        x_in, g_in, dres_in = norm_bwd
        in_specs += [pl.BlockSpec((tm, tn), lambda i, j, k: (i, 0)), pl.BlockSpec((1, tn), lambda i, j, k: (0, 0)),
                     pl.BlockSpec((tm, tn), lambda i, j, k: (i, 0))]
        args += [x_in, g_in, dres_in]

    est = 2 * (_nbytes((tm, tk), a.dtype) + _nbytes((tk, tn), b.dtype) + _nbytes((tm, tn), out_dtype))
    est += _nbytes((tm, tn), F32) * (3 if res is not None else 1)
    est += _nbytes((tm, tn), F32) * (4 if norm_bwd is not None else 0)
    assert est < VMEM_BUDGET, (name, est)

    def body(*refs):
        a_ref, b_ref = refs[0], refs[1]
        pos = 2
        bias_ref = res_ref = None
        if bias is not None:
            bias_ref = refs[pos]
            pos += 1
        if res is not None:
            res_ref = refs[pos]
            pos += 1
        if norm_bwd is not None:
            x_ref, g_ref, dres_ref = refs[pos:pos + 3]
            pos += 3
        o_ref = refs[pos]
        if norm_bwd is not None:
            dg_ref = refs[pos + 1]
            pos += 1
        acc_ref = refs[pos + 1] if nk > 1 else None

        bv = b_ref[...]
        if b_rows is not None:
            bv = bv.reshape(b_rows * bv.shape[1], bv.shape[2])
        part = lax.dot_general(a_ref[...].astype(BF16), bv.astype(BF16), dn, preferred_element_type=F32)

        def finish(acc):
            if alpha != 1.0:
                acc = acc * alpha
            if bias_ref is not None:
                acc = acc + bias_ref[...]
            if res_ref is not None:
                acc = acc + res_ref[...]
            if norm_bwd is not None:
                xv = x_ref[...]
                rstd = lax.rsqrt(jnp.mean(xv * xv, axis=-1, keepdims=True) + RMS_EPS)
                xhat = xv * rstd
                dyg = acc * g_ref[...]
                part_g = jnp.sum(acc * xhat, axis=0, keepdims=True)
                acc = dres_ref[...] + rstd * (dyg - xhat * jnp.mean(dyg * xhat, axis=-1, keepdims=True))
                first = pl.program_id(0) == 0

                @pl.when(first)
                def _():
                    dg_ref[...] = part_g

                @pl.when(jnp.logical_not(first))
                def _():
                    dg_ref[...] += part_g
            o_ref[...] = acc.astype(out_dtype)

        if nk == 1:
            finish(part)
        else:
            k = pl.program_id(2)

            @pl.when(k == 0)
            def _():
                acc_ref[...] = part

            @pl.when(k > 0)
            def _():
                acc_ref[...] += part

            @pl.when(k == nk - 1)
            def _():
                finish(acc_ref[...])

    if out_split == "col":
        per_half = M // 2 // tm
        out_spec = pl.BlockSpec((None, None, tm, tn), lambda i, j, k: (i // per_half, j, i % per_half, 0))
        out_shape = jax.ShapeDtypeStruct((2, N_CHIPS, M // 2, tn), out_dtype)
    elif out_split == "row":
        out_spec = pl.BlockSpec((None, tm, tn), lambda i, j, k: (j, i, 0))
        out_shape = jax.ShapeDtypeStruct((2, M, tn), out_dtype)
    else:
        out_spec = pl.BlockSpec((tm, tn), lambda i, j, k: (i, j))
        out_shape = jax.ShapeDtypeStruct((M, N), out_dtype)
    semantics = ("parallel", "parallel", "arbitrary")
    if norm_bwd is not None:
        out_spec = [out_spec, pl.BlockSpec((1, tn), lambda i, j, k: (0, 0))]
        out_shape = [out_shape, jax.ShapeDtypeStruct((1, N), F32)]
        semantics = ("arbitrary", "arbitrary", "arbitrary")
    return _pcall_carrying(
        body, carry, name=name, grid=(M // tm, N // tn, nk), in_specs=in_specs, out_specs=out_spec,
        out_shape=out_shape,
        scratch_shapes=[pltpu.VMEM((tm, tn), F32)] if nk > 1 else [],
        compiler_params=pltpu.CompilerParams(dimension_semantics=semantics),
    )(*args)


def _rmsnorm_fwd(x, g, name):
    T, Dm = x.shape
    tm = _pick(T, 256, 8)

    def body(x_ref, g_ref, h_ref):
        xv = x_ref[...]
        rstd = lax.rsqrt(jnp.mean(xv * xv, axis=-1, keepdims=True) + RMS_EPS)
        h_ref[...] = (xv * rstd * g_ref[...]).astype(BF16)

    return _pcall(
        body, name=name, grid=(T // tm,),
        in_specs=[pl.BlockSpec((tm, Dm), lambda i: (i, 0)), pl.BlockSpec((1, Dm), lambda i: (0, 0))],
        out_specs=pl.BlockSpec((tm, Dm), lambda i: (i, 0)),
        out_shape=jax.ShapeDtypeStruct((T, Dm), BF16),
    )(x, g)


def _final_loss(x, g, tgt, name):
    T, Dm = x.shape
    tm = _pick(T, 256, 8)

    def body(x_ref, g_ref, t_ref, dx_ref, dg_ref, loss_ref):
        xv = x_ref[...]
        gv = g_ref[...]
        rstd = lax.rsqrt(jnp.mean(xv * xv, axis=-1, keepdims=True) + RMS_EPS)
        xhat = xv * rstd
        err = xhat * gv - t_ref[...]
        part_loss = 0.5 * jnp.sum(jnp.mean(err * err, axis=-1, keepdims=True), axis=0, keepdims=True)
        dy = err * (1.0 / Dm)
        dyg = dy * gv
        dx_ref[...] = rstd * (dyg - xhat * jnp.mean(dyg * xhat, axis=-1, keepdims=True))
        part_g = jnp.sum(dy * xhat, axis=0, keepdims=True)
        part_l = jnp.broadcast_to(part_loss, (1, 128))

        @pl.when(pl.program_id(0) == 0)
        def _():
            dg_ref[...] = part_g
            loss_ref[...] = part_l

        @pl.when(pl.program_id(0) > 0)
        def _():
            dg_ref[...] += part_g
            loss_ref[...] += part_l

    row = pl.BlockSpec((tm, Dm), lambda i: (i, 0))
    vec = pl.BlockSpec((1, Dm), lambda i: (0, 0))
    return _pcall(
        body, name=name, grid=(T // tm,), in_specs=[row, vec, row],
        out_specs=[row, vec, pl.BlockSpec((1, 128), lambda i: (0, 0))],
        out_shape=[jax.ShapeDtypeStruct((T, Dm), F32), jax.ShapeDtypeStruct((1, Dm), F32),
                   jax.ShapeDtypeStruct((1, 128), F32)],
        compiler_params=pltpu.CompilerParams(dimension_semantics=("arbitrary",)),
    )(x, g, tgt)


def _sigmoid(z):
    return 1.0 / (1.0 + jnp.exp(-z))


def _ffn_in_swiglu(x, g, w_in, name, carry=None):
    T, Dm = x.shape
    cw = w_in.shape[2]
    tm = _pick(T, 512, 16)

    def body(x_ref, gain_ref, wg_ref, wu_ref, h_ref, g_ref, u_ref, a_ref):
        @pl.when(pl.program_id(1) == 0)
        def _():
            xv = x_ref[...]
            rstd = lax.rsqrt(jnp.mean(xv * xv, axis=-1, keepdims=True) + RMS_EPS)
            h_ref[...] = (xv * rstd * gain_ref[...]).astype(BF16)

        hv = h_ref[...]
        gv = jnp.dot(hv, wg_ref[...], preferred_element_type=F32)
        uv = jnp.dot(hv, wu_ref[...], preferred_element_type=F32)
        g_ref[...] = gv.astype(BF16)
        u_ref[...] = uv.astype(BF16)
        a_ref[...] = (gv * _sigmoid(gv) * uv).astype(BF16)

    row = pl.BlockSpec((tm, Dm), lambda i, j: (i, 0))
    out = pl.BlockSpec((tm, cw), lambda i, j: (i, j))
    return _pcall_carrying(
        body, carry, name=name, grid=(T // tm, 2),
        in_specs=[row, pl.BlockSpec((1, Dm), lambda i, j: (0, 0)), pl.BlockSpec((None, Dm, cw), lambda i, j: (j, 0, 0)),
                  pl.BlockSpec((None, Dm, cw), lambda i, j: (j + 2, 0, 0))],
        out_specs=[row, out, out, out],
        out_shape=[jax.ShapeDtypeStruct((T, Dm), BF16)] + [jax.ShapeDtypeStruct((T, D_FF), BF16)] * 3,
        compiler_params=pltpu.CompilerParams(dimension_semantics=("parallel", "arbitrary")),
    )(x, g, w_in, w_in)


def _swiglu_bwd(g, u, da, name):
    T = g.shape[0]
    tm = _pick(T, 256, 16)

    def body(g_ref, u_ref, da_ref, d_ref):
        gv = g_ref[...].astype(F32)
        uv = u_ref[...].astype(F32)
        dav = da_ref[...].astype(F32)
        sg = _sigmoid(gv)
        d_ref[:, :D_FF] = (dav * uv * (sg + gv * sg * (1.0 - sg))).astype(BF16)
        d_ref[:, D_FF:] = (dav * gv * sg).astype(BF16)

    row = pl.BlockSpec((tm, D_FF), lambda i: (i, 0))
    return _pcall(
        body, name=name, grid=(T // tm,), in_specs=[row, row, row],
        out_specs=pl.BlockSpec((tm, 2 * D_FF), lambda i: (i, 0)),
        out_shape=jax.ShapeDtypeStruct((T, 2 * D_FF), BF16),
    )(g, u, da)


GATE_BLOCK0 = QKV_WIDTH // D_MODEL


def _branches_fwd(p, outs, weights, name):
    T = p.shape[0]
    cw = weights[0].shape[2]
    tm = _pick(T, 1024, 16)
    first_gate, per_branch = QKV_WIDTH // cw, D_MODEL // cw

    def body(*refs):
        gates, o_refs, w_refs, y_refs, m_ref = refs[0:3], refs[3:6], refs[6:9], refs[9:12], refs[12]
        merged = None
        for g_ref, o_ref, w_ref, y_ref in zip(gates, o_refs, w_refs, y_refs):
            y = jnp.dot(o_ref[...], w_ref[...], preferred_element_type=F32)
            y_ref[...] = y
            term = _sigmoid(g_ref[...]) * y
            merged = term if merged is None else merged + term
        m_ref[...] = merged.astype(BF16)

    gate_specs = [pl.BlockSpec((tm, cw), functools.partial(
        lambda i, j, kk: (i, first_gate + per_branch * kk + j), kk=kk)) for kk in range(3)]
    o_specs = [pl.BlockSpec((tm, o.shape[1]), lambda i, j: (i, 0)) for o in outs]
    w_specs = [pl.BlockSpec((None, wk.shape[1], cw), lambda i, j: (j, 0, 0)) for wk in weights]
    tile = pl.BlockSpec((tm, cw), lambda i, j: (i, j))
    return _pcall(
        body, name=name, grid=(T // tm, N_CHIPS), in_specs=gate_specs + o_specs + w_specs, out_specs=[tile] * 4,
        out_shape=[jax.ShapeDtypeStruct((T, D_MODEL), F32)] * 3 + [jax.ShapeDtypeStruct((T, D_MODEL), BF16)],
    )(p, p, p, *outs, *weights)


def _gate_bwd(p, ya, yb, yc, dm, name):
    T = p.shape[0]
    tm = _pick(T, 256, 16)

    def body(ga_ref, gb_ref, gc_ref, ya_ref, yb_ref, yc_ref, dm_ref, da_ref, db_ref, dc_ref, dg_ref):
        dmv = dm_ref[...]
        for kk, (g_ref, y_ref, d_ref) in enumerate(((ga_ref, ya_ref, da_ref), (gb_ref, yb_ref, db_ref),
                                                    (gc_ref, yc_ref, dc_ref))):
            s = _sigmoid(g_ref[...])
            d_ref[...] = (s * dmv).astype(BF16)
            dg_ref[:, kk * D_MODEL:(kk + 1) * D_MODEL] = (dmv * y_ref[...] * s * (1.0 - s)).astype(BF16)

    gate = [pl.BlockSpec((tm, D_MODEL), functools.partial(lambda i, kk: (i, GATE_BLOCK0 + kk), kk=kk))
            for kk in range(3)]
    row = pl.BlockSpec((tm, D_MODEL), lambda i: (i, 0))
    return _pcall(
        body, name=name, grid=(T // tm,), in_specs=gate + [row, row, row, row],
        out_specs=[row, row, row, pl.BlockSpec((tm, GATE_WIDTH), lambda i: (i, 0))],
        out_shape=[jax.ShapeDtypeStruct((T, D_MODEL), BF16)] * 3 + [jax.ShapeDtypeStruct((T, GATE_WIDTH), BF16)],
    )(p, p, p, ya, yb, yc, dm)


def _colsum(a, name):
    T, N = a.shape
    tm = _pick(T, 256, 16)

    def body(a_ref, o_ref):
        part = jnp.sum(a_ref[...].astype(F32), axis=0, keepdims=True)

        @pl.when(pl.program_id(0) == 0)
        def _():
            o_ref[...] = part

        @pl.when(pl.program_id(0) > 0)
        def _():
            o_ref[...] += part

    return _pcall(
        body, name=name, grid=(T // tm,), in_specs=[pl.BlockSpec((tm, N), lambda i: (i, 0))],
        out_specs=pl.BlockSpec((1, N), lambda i: (0, 0)), out_shape=jax.ShapeDtypeStruct((1, N), F32),
        compiler_params=pltpu.CompilerParams(dimension_semantics=("arbitrary",)),
    )(a)


def _adamw(w, g, m, v, name):
    R, C = w.shape
    tr = 256 if R % 256 == 0 else R
    c1 = 1.0 / (1.0 - ADAM_B1 ** ADAM_STEP)
    c2 = 1.0 / (1.0 - ADAM_B2 ** ADAM_STEP)

    def body(w_ref, g_ref, m_ref, v_ref, d_ref, nm_ref, nv_ref):
        gv = g_ref[...]
        mn = ADAM_B1 * m_ref[...] + (1.0 - ADAM_B1) * gv
        vn = ADAM_B2 * v_ref[...] + (1.0 - ADAM_B2) * (gv * gv)
        nm_ref[...] = mn
        nv_ref[...] = vn
        d_ref[...] = -ADAM_LR * ((mn * c1) / (jnp.sqrt(vn * c2) + ADAM_EPS) + ADAM_WD * w_ref[...])

    spec = pl.BlockSpec((tr, C), lambda i: (i, 0))
    return _pcall(
        body, name=name, grid=(R // tr,), in_specs=[spec] * 4, out_specs=[spec] * 3,
        out_shape=[jax.ShapeDtypeStruct((R, C), F32)] * 3,
    )(w, g, m, v)


def _log_sigmoid(z):
    return jnp.minimum(z, 0.0) - jnp.log(1.0 + jnp.exp(-jnp.abs(z)))


def _dot3(x, m01):
    x1 = x.astype(BF16)
    r1 = x - x1.astype(F32)
    x2 = r1.astype(BF16)
    x3 = (r1 - x2.astype(F32)).astype(BF16)
    n = x.shape[0]
    if n % 16:
        return (jnp.dot(x1, m01, preferred_element_type=F32) + jnp.dot(x2, m01, preferred_element_type=F32)
                + jnp.dot(x3, m01, preferred_element_type=F32))
    y = jnp.dot(jnp.concatenate([x1, x2, x3], axis=0), m01, preferred_element_type=F32)
    return y[:n] + y[n:2 * n] + y[2 * n:]


def _dot_nt(a, b):
    return lax.dot_general(a, b, NT, preferred_element_type=F32)


def _dot_tn(a, b):
    return lax.dot_general(a, b, TN, preferred_element_type=F32)


def _iota2(shape):
    return lax.broadcasted_iota(jnp.int32, shape, 0), lax.broadcasted_iota(jnp.int32, shape, 1)


HEADS_PER_STEP = 4
HEADS = range(HEADS_PER_STEP)


CHUNK_HEADS_PER_STEP = 4
CHUNK_HEADS = range(CHUNK_HEADS_PER_STEP)


def _head_spec(T, rows=None, width=HEAD_DIM, heads=HEADS_PER_STEP):
    return pl.BlockSpec((heads, T if rows is None else rows, width), lambda g: (g, 0, 0))


def _sb_weights(qb, kb, t0, s0, racc, m_gt, row, col):
    z = _dot_nt(qb, kb) * SCALE
    strict = (s0 + col) < (t0 + row)
    lb = _log_sigmoid(z)
    lf = jnp.where(strict, lb - z, 0.0)
    between = _dot3(lf, m_gt) + racc
    w = jnp.where(strict, jnp.exp(lb + between), 0.0)
    return strict, lb, lf, w


def _sb_fwd(q, k, v, name, carry=None):
    H, T, _ = q.shape
    nb = T // QB

    def body(q_ref, k_ref, v_ref, o_ref):
        row, col = _iota2((QB, QB))
        m_gt = (row > col).astype(BF16)

        def qblock(i, _):
            t0 = pl.multiple_of(i * QB, QB)
            qbs = [q_ref[h, pl.ds(t0, QB), :].astype(BF16) for h in HEADS]

            def kblock(jj, carry):
                s0 = pl.multiple_of((i - jj) * QB, QB)
                out = []
                for h in HEADS:
                    acc, racc = carry[h]
                    kb = k_ref[h, pl.ds(s0, QB), :].astype(BF16)
                    vb = v_ref[h, pl.ds(s0, QB), :].astype(BF16)
                    _, _, lf, w = _sb_weights(qbs[h], kb, t0, s0, racc, m_gt, row, col)
                    acc = acc + jnp.dot(w.astype(BF16), vb, preferred_element_type=F32)
                    out.append((acc, racc + jnp.sum(lf, axis=1, keepdims=True)))
                return tuple(out)

            res = lax.fori_loop(0, i + 1, kblock,
                                tuple((jnp.zeros((QB, HEAD_DIM), F32), jnp.zeros((QB, 1), F32)) for _ in HEADS))
            for h in HEADS:
                o_ref[h, pl.ds(t0, QB), :] = res[h][0].astype(BF16)
            return 0

        lax.fori_loop(0, nb, qblock, 0)

    spec = _head_spec(T)
    return _pcall_carrying(body, carry, name=name, grid=(H // HEADS_PER_STEP,), in_specs=[spec] * 3, out_specs=spec,
                           out_shape=jax.ShapeDtypeStruct((H, T, HEAD_DIM), BF16))(q, k, v)


def _sb_bwd(q, k, v, do, name, carry=None):
    H, T, _ = q.shape
    nb = T // QB

    def body(q_ref, k_ref, v_ref, do_ref, dq_ref, dk_out, dv_out, racc_ref, dk_ref, dv_ref):
        row, col = _iota2((QB, QB))
        m_gt = (row > col).astype(BF16)
        m_lt = (row < col).astype(BF16)
        dk_ref[...] = jnp.zeros_like(dk_ref)
        dv_ref[...] = jnp.zeros_like(dv_ref)

        def qblock(i, _):
            t0 = pl.multiple_of(i * QB, QB)
            qbs = [q_ref[h, pl.ds(t0, QB), :].astype(BF16) for h in HEADS]
            dobs = [do_ref[h, pl.ds(t0, QB), :].astype(BF16) for h in HEADS]

            def suffix(jj, raccs):
                j = i - jj
                s0 = pl.multiple_of(j * QB, QB)
                out = []
                for h in HEADS:
                    kb = k_ref[h, pl.ds(s0, QB), :].astype(BF16)
                    z = _dot_nt(qbs[h], kb) * SCALE
                    lf = jnp.where((s0 + col) < (t0 + row), _log_sigmoid(z) - z, 0.0)
                    racc_ref[h, j] = raccs[h]
                    out.append(raccs[h] + jnp.sum(lf, axis=1, keepdims=True))
                return tuple(out)

            lax.fori_loop(0, i + 1, suffix, tuple(jnp.zeros((QB, 1), F32) for _ in HEADS))

            def kblock(j, carry):
                s0 = pl.multiple_of(j * QB, QB)
                out = []
                for h in HEADS:
                    dq, cacc = carry[h]
                    kb = k_ref[h, pl.ds(s0, QB), :].astype(BF16)
                    vb = v_ref[h, pl.ds(s0, QB), :].astype(BF16)
                    strict, lb, _, w = _sb_weights(qbs[h], kb, t0, s0, racc_ref[h, j], m_gt, row, col)
                    e = _dot_nt(dobs[h], vb) * w
                    c_left = _dot3(e, m_lt) + cacc
                    sig = jnp.exp(lb)
                    dz = jnp.where(strict, e * (1.0 - sig) - c_left * sig, 0.0).astype(BF16)
                    dq = dq + jnp.dot(dz, kb, preferred_element_type=F32)
                    dk_ref[h, pl.ds(s0, QB), :] += _dot_tn(dz, qbs[h]) * SCALE
                    dv_ref[h, pl.ds(s0, QB), :] += _dot_tn(w.astype(BF16), dobs[h])
                    out.append((dq, cacc + jnp.sum(e, axis=1, keepdims=True)))
                return tuple(out)

            res = lax.fori_loop(0, i + 1, kblock,
                                tuple((jnp.zeros((QB, HEAD_DIM), F32), jnp.zeros((QB, 1), F32)) for _ in HEADS))
            for h in HEADS:
                dq_ref[h, pl.ds(t0, QB), :] = (res[h][0] * SCALE).astype(BF16)
            return 0

        lax.fori_loop(0, nb, qblock, 0)
        dk_out[...] = dk_ref[...].astype(BF16)
        dv_out[...] = dv_ref[...].astype(BF16)

    spec = _head_spec(T)
    acc = pltpu.VMEM((HEADS_PER_STEP, T, HEAD_DIM), F32)
    return _pcall_carrying(body, carry, name=name, grid=(H // HEADS_PER_STEP,), in_specs=[spec] * 4,
                           out_specs=[spec] * 3, out_shape=[jax.ShapeDtypeStruct((H, T, HEAD_DIM), BF16)] * 3,
                           scratch_shapes=[pltpu.VMEM((HEADS_PER_STEP, nb, QB, 1), F32), acc, acc])(q, k, v, do)


def _fox_logits(qb, kb, fq, fk, t0, s0, row, col):
    z = _dot_nt(qb, kb) * SCALE + fq - fk
    return jnp.where((s0 + col) <= (t0 + row), z, NEG)


def _fox_specs(T):
    return _head_spec(T, width=1), _head_spec(T, rows=T // QB, width=QB)


def _fox_fwd(q, k, v, fq, fk, name, carry=None):
    H, T, _ = q.shape
    nb = T // QB

    def body(q_ref, k_ref, v_ref, fq_ref, fk_ref, o_ref, lse_ref):
        row, col = _iota2((QB, QB))

        def qblock(i, _):
            t0 = pl.multiple_of(i * QB, QB)
            qbs = [q_ref[h, pl.ds(t0, QB), :].astype(BF16) for h in HEADS]
            fqs = [fq_ref[h, pl.ds(t0, QB), :] for h in HEADS]

            def kblock(j, carry):
                s0 = pl.multiple_of(j * QB, QB)
                out = []
                for h in HEADS:
                    acc, m, l = carry[h]
                    kb = k_ref[h, pl.ds(s0, QB), :].astype(BF16)
                    vb = v_ref[h, pl.ds(s0, QB), :].astype(BF16)
                    z = _fox_logits(qbs[h], kb, fqs[h], fk_ref[h, pl.ds(j, 1), :], t0, s0, row, col)
                    m_new = jnp.maximum(m, jnp.max(z, axis=1, keepdims=True))
                    a = jnp.exp(m - m_new)
                    p = jnp.exp(z - m_new)
                    acc = a * acc + jnp.dot(p.astype(BF16), vb, preferred_element_type=F32)
                    out.append((acc, m_new, a * l + jnp.sum(p, axis=1, keepdims=True)))
                return tuple(out)

            res = lax.fori_loop(0, i + 1, kblock,
                                tuple((jnp.zeros((QB, HEAD_DIM), F32), jnp.full((QB, 1), NEG, F32),
                                       jnp.zeros((QB, 1), F32)) for _ in HEADS))
            for h in HEADS:
                acc, m, l = res[h]
                o_ref[h, pl.ds(t0, QB), :] = (acc / l).astype(BF16)
                lse_ref[h, pl.ds(t0, QB), :] = m + jnp.log(l)
            return 0

        lax.fori_loop(0, nb, qblock, 0)

    spec = _head_spec(T)
    fq_spec, fk_spec = _fox_specs(T)
    return _pcall_carrying(
        body, carry, name=name, grid=(H // HEADS_PER_STEP,), in_specs=[spec] * 3 + [fq_spec, fk_spec],
        out_specs=[spec, fq_spec],
        out_shape=[jax.ShapeDtypeStruct((H, T, HEAD_DIM), BF16), jax.ShapeDtypeStruct((H, T, 1), F32)],
    )(q, k, v, fq, fk)


def _fox_bwd(q, k, v, fq, fk, lse, do, name, carry=None):
    H, T, _ = q.shape
    nb = T // QB

    def body(q_ref, k_ref, v_ref, fq_ref, fk_ref, lse_ref, do_ref, dq_ref, dk_out, dv_out, dfk_ref, dk_ref, dv_ref):
        row, col = _iota2((QB, QB))
        dk_ref[...] = jnp.zeros_like(dk_ref)
        dv_ref[...] = jnp.zeros_like(dv_ref)
        dfk_ref[...] = jnp.zeros_like(dfk_ref)

        def qblock(i, _):
            t0 = pl.multiple_of(i * QB, QB)
            qbs = [q_ref[h, pl.ds(t0, QB), :].astype(BF16) for h in HEADS]
            fqs = [fq_ref[h, pl.ds(t0, QB), :] for h in HEADS]
            lses = [lse_ref[h, pl.ds(t0, QB), :] for h in HEADS]
            dobs = [do_ref[h, pl.ds(t0, QB), :].astype(BF16) for h in HEADS]

            def probs(h, j):
                s0 = pl.multiple_of(j * QB, QB)
                kb = k_ref[h, pl.ds(s0, QB), :].astype(BF16)
                vb = v_ref[h, pl.ds(s0, QB), :].astype(BF16)
                z = _fox_logits(qbs[h], kb, fqs[h], fk_ref[h, pl.ds(j, 1), :], t0, s0, row, col)
                return s0, kb, jnp.exp(z - lses[h]), _dot_nt(dobs[h], vb)

            def row_dot(j, deltas):
                out = []
                for h in HEADS:
                    _, _, p, dp = probs(h, j)
                    out.append(deltas[h] + jnp.sum(p * dp, axis=1, keepdims=True))
                return tuple(out)

            deltas = lax.fori_loop(0, i + 1, row_dot, tuple(jnp.zeros((QB, 1), F32) for _ in HEADS))

            def kblock(j, dqs):
                out = []
                for h in HEADS:
                    s0, kb, p, dp = probs(h, j)
                    dz = p * (dp - deltas[h])
                    dzb = dz.astype(BF16)
                    dk_ref[h, pl.ds(s0, QB), :] += _dot_tn(dzb, qbs[h]) * SCALE
                    dv_ref[h, pl.ds(s0, QB), :] += _dot_tn(p.astype(BF16), dobs[h])
                    dfk_ref[h, pl.ds(j, 1), :] += -jnp.sum(dz, axis=0, keepdims=True)
                    out.append(dqs[h] + jnp.dot(dzb, kb, preferred_element_type=F32))
                return tuple(out)

            dqs = lax.fori_loop(0, i + 1, kblock, tuple(jnp.zeros((QB, HEAD_DIM), F32) for _ in HEADS))
            for h in HEADS:
                dq_ref[h, pl.ds(t0, QB), :] = (dqs[h] * SCALE).astype(BF16)
            return 0

        lax.fori_loop(0, nb, qblock, 0)
        dk_out[...] = dk_ref[...].astype(BF16)
        dv_out[...] = dv_ref[...].astype(BF16)

    spec = _head_spec(T)
    fq_spec, fk_spec = _fox_specs(T)
    acc = pltpu.VMEM((HEADS_PER_STEP, T, HEAD_DIM), F32)
    return _pcall_carrying(body, carry, name=name, grid=(H // HEADS_PER_STEP,),
                           in_specs=[spec] * 3 + [fq_spec, fk_spec, fq_spec, spec],
                           out_specs=[spec, spec, spec, fk_spec],
                           out_shape=[jax.ShapeDtypeStruct((H, T, HEAD_DIM), BF16)] * 3
                           + [jax.ShapeDtypeStruct((H, nb, QB), F32)],
                           scratch_shapes=[acc, acc])(q, k, v, fq, fk, lse, do)


def _forget_cumsum(flog, name):
    R, T = flog.shape
    nb = T // QB

    def body(x_ref, o_ref):
        row, col = _iota2((QB, QB))
        m_le = (row <= col).astype(BF16)
        carry = jnp.zeros((R, 1), F32)
        for b in range(nb):
            lf = _log_sigmoid(x_ref[:, b * QB:(b + 1) * QB])
            o_ref[:, b * QB:(b + 1) * QB] = _dot3(lf, m_le) + carry
            carry = carry + jnp.sum(lf, axis=1, keepdims=True)

    spec = pl.BlockSpec((R, T), lambda: (0, 0))
    return _pcall(body, name=name, in_specs=[spec], out_specs=spec,
                  out_shape=jax.ShapeDtypeStruct((R, T), F32))(flog)


def _forget_cumsum_bwd(flog, df, name):
    R, T = flog.shape
    nb = T // QB

    def body(x_ref, df_ref, o_ref):
        row, col = _iota2((QB, QB))
        m_ge = (row >= col).astype(BF16)
        carry = jnp.zeros((R, 1), F32)
        for b in reversed(range(nb)):
            dfb = df_ref[:, b * QB:(b + 1) * QB]
            dlog = _dot3(dfb, m_ge) + carry
            o_ref[:, b * QB:(b + 1) * QB] = dlog * _sigmoid(-x_ref[:, b * QB:(b + 1) * QB])
            carry = carry + jnp.sum(dfb, axis=1, keepdims=True)

    spec = pl.BlockSpec((R, T), lambda: (0, 0))
    return _pcall(body, name=name, in_specs=[spec, spec], out_specs=spec,
                  out_shape=jax.ShapeDtypeStruct((R, T), F32))(flog, df)


def _chunk_probs(qc, kb, bias, c, col):
    z = _dot_nt(qc, kb) * SCALE + bias
    z = jnp.where((c - (LEFT_CHUNKS + 1)) * CHUNK + col >= jnp.maximum(0, (c - LEFT_CHUNKS) * CHUNK), z, NEG)
    p = jnp.exp(z - jnp.max(z, axis=1, keepdims=True))
    return p, jnp.sum(p, axis=1, keepdims=True)


def _chunk_specs(T, n=CHUNK_HEADS_PER_STEP):
    return (_head_spec(T, heads=n), _head_spec(T, rows=T + BAND_PAD, heads=n),
            _head_spec(T, rows=CHUNK, width=BAND, heads=n))


def _chunk_fwd(q, kp, vp, bias, name, carry=None):
    H, T, _ = q.shape
    nc = T // CHUNK

    def body(q_ref, kp_ref, vp_ref, bias_ref, o_ref):
        _, col = _iota2((CHUNK, BAND))

        def chunk(c, _):
            t0 = pl.multiple_of(c * CHUNK, CHUNK)
            for h in HEADS:
                qc = q_ref[h, pl.ds(t0, CHUNK), :].astype(BF16)
                kb = kp_ref[h, pl.ds(t0, BAND), :].astype(BF16)
                vb = vp_ref[h, pl.ds(t0, BAND), :].astype(BF16)
                p, l = _chunk_probs(qc, kb, bias_ref[h], c, col)
                o = jnp.dot(p.astype(BF16), vb, preferred_element_type=F32) / l
                o_ref[h, pl.ds(t0, CHUNK), :] = o.astype(BF16)
            return 0

        lax.fori_loop(0, nc, chunk, 0)

    q_spec, kp_spec, b_spec = _chunk_specs(T, HEADS_PER_STEP)
    return _pcall_carrying(body, carry, name=name, grid=(H // HEADS_PER_STEP,),
                           in_specs=[q_spec, kp_spec, kp_spec, b_spec], out_specs=q_spec,
                           out_shape=jax.ShapeDtypeStruct((H, T, HEAD_DIM), BF16))(q, kp, vp, bias)


def _chunk_bwd(q, kp, vp, bias, do, name, carry=None):
    H, T, _ = q.shape
    nc = T // CHUNK

    def body(q_ref, kp_ref, vp_ref, bias_ref, do_ref, dq_ref, dk_out, dv_out, db_ref, dkp_ref, dvp_ref):
        _, col = _iota2((CHUNK, BAND))
        dkp_ref[...] = jnp.zeros_like(dkp_ref)
        dvp_ref[...] = jnp.zeros_like(dvp_ref)
        db_ref[...] = jnp.zeros_like(db_ref)

        def chunk(c, _):
            t0 = pl.multiple_of(c * CHUNK, CHUNK)
            for h in CHUNK_HEADS:
                qc = q_ref[h, pl.ds(t0, CHUNK), :].astype(BF16)
                kb = kp_ref[h, pl.ds(t0, BAND), :].astype(BF16)
                vb = vp_ref[h, pl.ds(t0, BAND), :].astype(BF16)
                dob = do_ref[h, pl.ds(t0, CHUNK), :].astype(BF16)
                p, l = _chunk_probs(qc, kb, bias_ref[h], c, col)
                p = p / l
                dp = _dot_nt(dob, vb)
                dz = p * (dp - jnp.sum(p * dp, axis=1, keepdims=True))
                dzb = dz.astype(BF16)
                dq = jnp.dot(dzb, kb, preferred_element_type=F32) * SCALE
                dq_ref[h, pl.ds(t0, CHUNK), :] = dq.astype(BF16)
                dkp_ref[h, pl.ds(t0, BAND), :] += _dot_tn(dzb, qc) * SCALE
                dvp_ref[h, pl.ds(t0, BAND), :] += _dot_tn(p.astype(BF16), dob)
                db_ref[h] += dz
            return 0

        lax.fori_loop(0, nc, chunk, 0)
        dk_out[...] = dkp_ref[:, BAND_PAD:, :].astype(BF16)
        dv_out[...] = dvp_ref[:, BAND_PAD:, :].astype(BF16)

    q_spec, kp_spec, b_spec = _chunk_specs(T)
    acc = pltpu.VMEM((CHUNK_HEADS_PER_STEP, T + BAND_PAD, HEAD_DIM), F32)
    return _pcall_carrying(body, carry, name=name, grid=(H // CHUNK_HEADS_PER_STEP,),
                           in_specs=[q_spec, kp_spec, kp_spec, b_spec, q_spec],
                           out_specs=[q_spec, q_spec, q_spec, b_spec],
                           out_shape=[jax.ShapeDtypeStruct((H, T, HEAD_DIM), BF16)] * 3
                           + [jax.ShapeDtypeStruct((H, CHUNK, BAND), F32)],
                           scratch_shapes=[acc, acc])(q, kp, vp, bias, do)


HBM_SPEC = pl.BlockSpec(memory_space=pltpu.HBM)


def _position():
    return lax.axis_index("x"), lax.axis_index("y"), lax.axis_index("c")


def _other_chips(x, y):
    chips = [(1 - x, y), (x, 1 - y), (1 - x, 1 - y)]
    return [(chip, 2 * chip[0] + chip[1]) for chip in chips]


def _remote_copy(src, dst, send_sems, recv_sems, k, to):
    return pltpu.make_async_remote_copy(src_ref=src, dst_ref=dst, send_sem=send_sems.at[k], recv_sem=recv_sems.at[k],
                                        device_id=to, device_id_type=MESH)


def _place_own(w, chip, name):
    _, r, c = w.shape
    tr = _pick(r, 256, 16)

    def body(chip_ref, w_ref, *out_refs):
        for l, o_ref in enumerate(out_refs):
            o_ref[...] = w_ref[l].astype(BF16)

    grid_spec = pltpu.PrefetchScalarGridSpec(
        num_scalar_prefetch=1, grid=(r // tr,), in_specs=[pl.BlockSpec((DEPTH, tr, c), lambda i, ch: (0, i, 0))],
        out_specs=[pl.BlockSpec((None, tr, c), lambda i, ch: (ch[0], i, 0))] * DEPTH)
    return _pcall(body, name=name, grid_spec=grid_spec,
                  out_shape=[jax.ShapeDtypeStruct((N_CHIPS, r, c), BF16)] * DEPTH)(chip, w)


def _gather_over_ici(srcs, bufs, new):
    x, y, c = _position()
    me = 2 * x + y
    return [(b.at[me, c], b.at[me, c], (*chip, c)) for b in bufs for chip, _ in _other_chips(x, y)]


def _gather_to_sibling(srcs, bufs, new):
    x, y, c = _position()
    return [(b.at[idx, c], b.at[idx, c], (x, y, 1 - c)) for b in bufs for _, idx in _other_chips(x, y)]


def _gather_layer(bufs, name):
    n = len(bufs)

    def body(*refs):
        dst = refs[n:2 * n]
        send_sems, recv_sems = refs[2 * n:]
        x, y, c = _position()
        me = 2 * x + y
        sibling = (x, y, 1 - c)
        others = _other_chips(x, y)
        first = [_remote_copy(dst[i].at[me, c], dst[i].at[me, c], send_sems, recv_sems, 3 * i + k, (*chip, c))
                 for i in range(n) for k, (chip, _) in enumerate(others)]
        for cp in first:
            cp.start()
        passed = []
        for i in range(n):
            for k, (_, idx) in enumerate(others):
                landed = dst[i].at[idx, c]
                _remote_copy(landed, landed, send_sems, recv_sems, 3 * i + k, sibling).wait_recv()
                passed.append(_remote_copy(landed, landed, send_sems, recv_sems, 3 * (n + i) + k, sibling))
                passed[-1].start()
        for i in range(n):
            for k, (_, idx) in enumerate(others):
                from_sibling = dst[i].at[idx, 1 - c]
                _remote_copy(from_sibling, from_sibling, send_sems, recv_sems, 3 * (n + i) + k, sibling).wait_recv()
        for cp in first + passed:
            cp.wait_send()

    return _pcall(
        body, name=name, in_specs=[HBM_SPEC] * n, out_specs=[HBM_SPEC] * n,
        out_shape=[jax.ShapeDtypeStruct(b.shape, b.dtype) for b in bufs],
        scratch_shapes=[pltpu.SemaphoreType.DMA((6 * n,)), pltpu.SemaphoreType.DMA((6 * n,))],
        input_output_aliases={i: i for i in range(n)},
    )(*bufs)


def _send_other_half(parts, name):
    n = len(parts)

    def body(*refs):
        src, got = refs[:n], refs[n:2 * n]
        send_sems, recv_sems = refs[2 * n:]
        x, y, c = _position()
        copies = [_remote_copy(src[i].at[1 - c], got[i], send_sems, recv_sems, i, (x, y, 1 - c)) for i in range(n)]
        for cp in copies:
            cp.start()
        for cp in copies:
            cp.wait()

    return _pcall(
        body, name=name, in_specs=[HBM_SPEC] * n, out_specs=[HBM_SPEC] * n,
        out_shape=[jax.ShapeDtypeStruct(p.shape[1:], p.dtype) for p in parts],
        scratch_shapes=[pltpu.SemaphoreType.DMA((n,)), pltpu.SemaphoreType.DMA((n,))],
    )(*parts)


def _scatter_chips(parts, name):
    n = len(parts)

    def body(*refs):
        src, dst = refs[:n], refs[n:2 * n]
        send_sems, recv_sems = refs[2 * n:]
        x, y, c = _position()
        others = _other_chips(x, y)
        sends = [_remote_copy(src[i].at[idx], dst[i].at[k], send_sems, recv_sems, 3 * i + k, (*chip, c))
                 for i in range(n) for k, (chip, idx) in enumerate(others)]
        for cp in sends:
            cp.start()
        for cp in sends:
            cp.wait()

    return _pcall(
        body, name=name, in_specs=[HBM_SPEC] * n, out_specs=[HBM_SPEC] * n,
        out_shape=[jax.ShapeDtypeStruct((3,) + p.shape[1:], p.dtype) for p in parts],
        scratch_shapes=[pltpu.SemaphoreType.DMA((3 * n,)), pltpu.SemaphoreType.DMA((3 * n,))],
    )(*parts)


def _swap_with_sibling(arrs, name):
    n = len(arrs)

    def body(*refs):
        src, dst = refs[:n], refs[n:2 * n]
        send_sems, recv_sems = refs[2 * n:]
        x, y, c = _position()
        copies = [_remote_copy(src[i], dst[i], send_sems, recv_sems, i, (x, y, 1 - c)) for i in range(n)]
        for cp in copies:
            cp.start()
        for cp in copies:
            cp.wait()

    return _pcall(
        body, name=name, in_specs=[HBM_SPEC] * n, out_specs=[HBM_SPEC] * n,
        out_shape=[jax.ShapeDtypeStruct(a.shape, a.dtype) for a in arrs],
        scratch_shapes=[pltpu.SemaphoreType.DMA((n,)), pltpu.SemaphoreType.DMA((n,))],
    )(*arrs)


def _allreduce_small(v, name):
    R, C = v.shape

    def body(v_ref, o_ref, slots, send_sems, recv_sems):
        x, y, c = _position()
        me = 4 * x + 2 * y + c
        slots[0] = v_ref[...]
        sends = []
        for r in range(1, 8):
            fx, fy, fc = (r >> 2) & 1, (r >> 1) & 1, r & 1
            to = (x ^ fx, y ^ fy, c ^ fc)
            cp = pltpu.make_async_remote_copy(src_ref=v_ref, dst_ref=slots.at[r], send_sem=send_sems.at[r - 1],
                                              recv_sem=recv_sems.at[r - 1], device_id=to, device_id_type=MESH)
            cp.start()
            sends.append(cp)
        for cp in sends:
            cp.wait()
        acc = slots[me]
        for d in range(1, 8):
            acc = acc + slots[d ^ me]
        o_ref[...] = acc

    vmem = pl.BlockSpec(memory_space=pltpu.VMEM)
    return _pcall(
        body, name=name, in_specs=[vmem], out_specs=vmem, out_shape=jax.ShapeDtypeStruct((R, C), F32),
        scratch_shapes=[pltpu.VMEM((8, R, C), F32), pltpu.SemaphoreType.DMA((7,)), pltpu.SemaphoreType.DMA((7,))],
    )(v)


def _add_pair(which, both, got, name):
    _, N, R, C = both.shape
    tr = _pick(R, 512, 16)

    def body(which_ref, a_ref, b_ref, o_ref):
        o_ref[...] = (a_ref[...].astype(F32) + b_ref[...].astype(F32)).astype(BF16)

    spec = pl.BlockSpec((None, tr, C), lambda n, i, w: (n, i, 0))
    grid_spec = pltpu.PrefetchScalarGridSpec(
        num_scalar_prefetch=1, grid=(N, R // tr),
        in_specs=[pl.BlockSpec((None, None, tr, C), lambda n, i, w: (w[0], n, i, 0)), spec], out_specs=spec)
    return _pcall(body, name=name, grid_spec=grid_spec,
                  out_shape=jax.ShapeDtypeStruct((N, R, C), BF16))(which, both, got)


def _sum_chips(which, own, others, name):
    N, R, C = others.shape
    tr = _pick(R, 512, 16)

    def body(which_ref, own_ref, p_ref, o_ref):
        acc = own_ref[...].astype(F32)
        for n in range(N):
            acc = acc + p_ref[n].astype(F32)
        o_ref[...] = acc

    grid_spec = pltpu.PrefetchScalarGridSpec(
        num_scalar_prefetch=1, grid=(R // tr,),
        in_specs=[pl.BlockSpec((None, tr, C), lambda i, w: (w[0], i, 0)), pl.BlockSpec((N, tr, C), lambda i, w: (0, i, 0))],
        out_specs=pl.BlockSpec((tr, C), lambda i, w: (i, 0)))
    return _pcall(body, name=name, grid_spec=grid_spec,
                  out_shape=jax.ShapeDtypeStruct((R, C), F32))(which, own, others)


BIG = ("w_ffn1_in", "w_ffn1_out", "w_in", "w_br_sb", "w_br_ch", "w_br_fox", "w_out", "w_ffn2_in", "w_ffn2_out")
ROW_SHARDED = ("w_ffn1_out", "w_out", "w_ffn2_out")
SMALL = ("g_ffn1", "g_mix", "b_in", "rel_bias", "g_ffn2", "g_final")
SHARD_SHAPES = {
    "w_ffn1_in": (D_MODEL, 2 * D_FF // N_CHIPS), "w_ffn1_out": (D_FF // N_CHIPS, D_MODEL),
    "w_in": (D_MODEL, IN_WIDTH // N_CHIPS), "w_br_sb": (W_SB, D_MODEL // N_CHIPS),
    "w_br_ch": (W_CH, D_MODEL // N_CHIPS), "w_br_fox": (W_FOX, D_MODEL // N_CHIPS),
    "w_out": (D_MODEL // N_CHIPS, D_MODEL), "w_ffn2_in": (D_MODEL, 2 * D_FF // N_CHIPS),
    "w_ffn2_out": (D_FF // N_CHIPS, D_MODEL),
}


def _pair_sums(split, tag):
    core = lax.axis_index("c").astype(jnp.int32)[None]
    got = _send_other_half(split, f"grad_split_{tag}")
    return [_add_pair(core, a, b, f"grad_pair_sum_{tag}_{i}") for i, (a, b) in enumerate(zip(split, got))]


def _scatter_plan(srcs, bufs, new):
    x, y, c = _position()
    return [(s.at[idx], d.at[k], (*chip, c)) for s, d in zip(srcs, new) for k, (chip, idx) in enumerate(_other_chips(x, y))]


def _scatter_carry(pair):
    landing = [jax.ShapeDtypeStruct((3,) + p.shape[1:], p.dtype) for p in pair]
    return _Carry(pair, [], landing, 3 * len(pair), _scatter_plan)


def _finish_grads(pair, landed, axes, tag):
    chip = (2 * lax.axis_index("x") + lax.axis_index("y")).astype(jnp.int32)[None]
    mine = [_sum_chips(chip, p, q, f"grad_chip_sum_{tag}_{i}") for i, (p, q) in enumerate(zip(pair, landed))]
    theirs = _swap_with_sibling(mine, f"grad_share_{tag}")
    first = lax.axis_index("c") == 0
    return [jnp.concatenate([jnp.where(first, a, b), jnp.where(first, b, a)], axis=ax)
            for a, b, ax in zip(mine, theirs, axes)]


SMALL_SIZES = {"g_ffn1": DEPTH * D_MODEL, "g_mix": DEPTH * D_MODEL, "b_in": DEPTH * IN_WIDTH,
               "rel_bias": DEPTH * N_REL * H_CH, "g_ffn2": DEPTH * D_MODEL, "g_final": D_MODEL}
SMALL_TOTAL = sum(SMALL_SIZES.values()) + 1
SMALL_ROWS = -(-SMALL_TOTAL // (8 * 128)) * 8


def _pack_small(vals, extra):
    flat = [vals[n].reshape(-1) for n in SMALL] + [extra.reshape(-1)]
    flat.append(jnp.zeros((SMALL_ROWS * 128 - SMALL_TOTAL,), F32))
    return jnp.concatenate(flat).reshape(SMALL_ROWS, 128)


def _unpack_small(pack, shapes):
    flat, out, off = pack.reshape(-1), {}, 0
    for n in SMALL:
        out[n] = flat[off:off + SMALL_SIZES[n]].reshape(shapes[n])
        off += SMALL_SIZES[n]
    return out, flat[off]


def _to_heads(t, n_heads):
    return jnp.transpose(t.reshape(t.shape[0], n_heads, HEAD_DIM), (1, 0, 2)).astype(BF16)


def _from_heads(t):
    return jnp.transpose(t, (1, 0, 2)).reshape(t.shape[1], t.shape[0] * HEAD_DIM)


def _pad_keys(t):
    return jnp.pad(t, ((0, 0), (BAND_PAD, 0), (0, 0)))


DIAGONALS = CHUNK + BAND - 1


def _band_bias(table):
    n_far = (LEFT_CHUNKS + 1) * CHUNK + CHUNK - MAX_REL
    near = table[N_REL - 1 - (DIAGONALS - n_far):N_REL - 1][::-1]
    diag = jnp.concatenate([jnp.broadcast_to(table[N_REL - 1], (n_far, H_CH)), near], axis=0).T
    diag = jnp.pad(diag, ((0, 0), (0, 1)))
    skew = jnp.tile(diag, (1, CHUNK))[:, :CHUNK * DIAGONALS].reshape(H_CH, CHUNK, DIAGONALS)
    return skew[:, :, CHUNK - 1:]


def _pad_w_in(w):
    qkv, f, gates = w[:, :QKV_WIDTH], w[:, QKV_WIDTH:QKV_WIDTH + H_FOX], w[:, QKV_WIDTH + H_FOX:]
    return jnp.concatenate([qkv, gates, f, jnp.zeros((w.shape[0], F_PAD - H_FOX), w.dtype)], axis=1)


def _unpad_w_in(w):
    return jnp.concatenate([w[:, :QKV_WIDTH], w[:, QKV_WIDTH + GATE_WIDTH:QKV_WIDTH + GATE_WIDTH + H_FOX],
                            w[:, QKV_WIDTH:QKV_WIDTH + GATE_WIDTH]], axis=1)


QKV_SPLITS = np.cumsum([0, W_SB, W_SB, W_SB, W_CH, W_CH, W_CH, W_FOX, W_FOX, W_FOX])


def _by_chip_rows(g):
    return g.reshape(2, N_CHIPS, g.shape[1] // N_CHIPS, g.shape[2])


def _ffn_fwd(x, g, w_in, w_out, tag, carries):
    h, gate, up, a = _ffn_in_swiglu(x, g, w_in, f"{tag}_in", _carry_of(carries, "in"))
    y = _mm(a, w_out, mode="nn", name=f"{tag}_out", alpha=0.5, res=x, gathered="row",
            carry=_carry_of(carries, "out"))
    return y, (h, gate, up, a)


def _ffn_bwd(x, g, w_in, w_out, saved, dy, tag, carries):
    h, gate, up, a = saved
    da = _mm(dy, w_out, mode="nt", name=f"{tag}_da", alpha=0.5, gathered="row", out_dtype=BF16,
             carry=_carry_of(carries, "da"))
    dw_out = _mm(a, dy, mode="tn", name=f"{tag}_dwout", alpha=0.5, tm_cap=1408, out_dtype=BF16, out_split="row",
                 carry=_carry_of(carries, "dwout"))
    dgu = _swiglu_bwd(gate, up, da, f"{tag}_dact")
    dw_in = _mm(h, dgu, mode="tn", name=f"{tag}_dwin", tm_cap=512, out_dtype=BF16, out_split="col",
                carry=_carry_of(carries, "dwin"))
    dx, dg = _mm(dgu, w_in, mode="nt", name=f"{tag}_dh", gathered="col", tn_cap=1024, carry=_carry_of(carries, "dh"),
                 norm_bwd=(x, g, dy))
    return dx, dg, dw_in, _by_chip_rows(dw_out)


def _mixer_fwd(x, lw, tag, carries):
    T = x.shape[0]
    h = _rmsnorm_fwd(x, lw["g_mix"], f"{tag}_norm")
    p = _mm(h, lw["w_in"], mode="nn", name=f"{tag}_proj", bias=lw["b_in"], tn_cap=896,
            carry=_carry_of(carries, "proj"))
    parts = [p[:, QKV_SPLITS[i]:QKV_SPLITS[i + 1]] for i in range(9)]
    qa, ka, va = (_to_heads(t, H_SB) for t in parts[0:3])
    qb, kb, vb = (_to_heads(t, H_CH) for t in parts[3:6])
    qc, kc, vc = (_to_heads(t, H_FOX) for t in parts[6:9])
    flog = _mm(lw["w_forget_t"], h, mode="nt", name=f"{tag}_flog")[:8] + lw["b_forget"]
    fcum = _forget_cumsum(flog, f"{tag}_fcum")[:H_FOX]
    fq, fk = fcum.reshape(H_FOX, T, 1), fcum.reshape(H_FOX, T // QB, QB)
    kbp, vbp = _pad_keys(kb), _pad_keys(vb)
    oa = _sb_fwd(qa, ka, va, f"{tag}_sb", _carry_of(carries, "sb"))
    ob = _chunk_fwd(qb, kbp, vbp, lw["bias"], f"{tag}_ch", _carry_of(carries, "ch"))
    oc, lse = _fox_fwd(qc, kc, vc, fq, fk, f"{tag}_fox", _carry_of(carries, "fox"))
    oam, obm, ocm = _from_heads(oa), _from_heads(ob), _from_heads(oc)
    ya, yb, yc, merged = _branches_fwd(p, (oam, obm, ocm), (lw["w_br_sb"], lw["w_br_ch"], lw["w_br_fox"]),
                                       f"{tag}_branches")
    y = _mm(merged, lw["w_out"], mode="nn", name=f"{tag}_out", res=x, gathered="row")
    saved = (h, p, (qa, ka, va), (qb, kbp, vbp), (qc, kc, vc), flog, fq, fk, lse, (oam, obm, ocm),
             (ya, yb, yc), merged)
    return y, saved


def _mixer_bwd(x, lw, saved, dy, tag, carries):
    (h, p, (qa, ka, va), (qb, kbp, vbp), (qc, kc, vc), flog, fq, fk, lse, (oam, obm, ocm),
     (ya, yb, yc), merged) = saved
    T = x.shape[0]
    grads = {}
    col, row = "col", "row"
    dmerged = _mm(dy, lw["w_out"], mode="nt", name=f"{tag}_dmerged", gathered=row,
                  carry=_carry_of(carries, "dmerged"))
    grads["w_out"] = _by_chip_rows(_mm(merged, dy, mode="tn", name=f"{tag}_dwout", tm_cap=1024, out_dtype=BF16,
                                       out_split="row"))
    dya, dyb, dyc, dgate = _gate_bwd(p, ya, yb, yc, dmerged, f"{tag}_dgate")
    doa = _mm(dya, lw["w_br_sb"], mode="nt", name=f"{tag}_doa", gathered=col)
    dob = _mm(dyb, lw["w_br_ch"], mode="nt", name=f"{tag}_dob", gathered=col)
    doc = _mm(dyc, lw["w_br_fox"], mode="nt", name=f"{tag}_doc", gathered=col)
    by_chip = dict(mode="tn", tm_cap=512, out_dtype=BF16, out_split="col")
    grads["w_br_sb"] = _mm(oam, dya, name=f"{tag}_dwbra", **by_chip)
    grads["w_br_ch"] = _mm(obm, dyb, name=f"{tag}_dwbrb", **by_chip)
    grads["w_br_fox"] = _mm(ocm, dyc, name=f"{tag}_dwbrc", **by_chip)
    dqa, dka, dva = _sb_bwd(qa, ka, va, _to_heads(doa, H_SB), f"{tag}_dsb", _carry_of(carries, "dsb"))
    dqb, dkb, dvb, dbias = _chunk_bwd(qb, kbp, vbp, lw["bias"], _to_heads(dob, H_CH), f"{tag}_dch",
                                      _carry_of(carries, "dch"))
    dqc, dkc, dvc, dfk = _fox_bwd(qc, kc, vc, fq, fk, lse, _to_heads(doc, H_FOX), f"{tag}_dfox",
                                  _carry_of(carries, "dfox"))
    dflog = _forget_cumsum_bwd(flog, jnp.pad(dfk.reshape(H_FOX, T), ((0, 8 - H_FOX), (0, 0))), f"{tag}_dfcum")
    dqkv = [_from_heads(t) for t in (dqa, dka, dva, dqb, dkb, dvb, dqc, dkc, dvc)]
    dforget = jnp.pad(dflog[:H_FOX].T, ((0, 0), (0, F_PAD - H_FOX))).astype(BF16)
    dpb = jnp.concatenate(dqkv + [dgate, dforget], axis=1)
    grads["b_in"] = _colsum(dpb, f"{tag}_dbin")
    dw_in =_unpad_w_in(_mm(h, dpb, mode="tn", name=f"{tag}_dwin", tm_cap=512, tn_cap=896, out_dtype=BF16))
    grads["w_in"] = jnp.transpose(dw_in.reshape(2, D_MODEL // 2, N_CHIPS, IN_WIDTH // N_CHIPS), (0, 2, 1, 3))
    dx, grads["g_mix"] = _mm(dpb, lw["w_in"], mode="nt", name=f"{tag}_dh", tk_cap=896, tn_cap=1024,
                             norm_bwd=(x, lw["g_mix"], dy))
    grads["rel_bias"] = lw["bias_vjp"](dbias)[0]
    return dx, grads


def kernel(x, g_ffn1, w_ffn1_in, w_ffn1_out, g_mix, w_in, b_in, rel_bias, w_br_sb, w_br_ch, w_br_fox, w_out, g_ffn2, w_ffn2_in, w_ffn2_out, g_final, loss_target, m_g_ffn1, m_w_ffn1_in, m_w_ffn1_out, m_g_mix, m_w_in, m_b_in, m_rel_bias, m_w_br_sb, m_w_br_ch, m_w_br_fox, m_w_out, m_g_ffn2, m_w_ffn2_in, m_w_ffn2_out, m_g_final, v_g_ffn1, v_w_ffn1_in, v_w_ffn1_out, v_g_mix, v_w_in, v_b_in, v_rel_bias, v_w_br_sb, v_w_br_ch, v_w_br_fox, v_w_out, v_g_ffn2, v_w_ffn2_in, v_w_ffn2_out, v_g_final):
    names = ("g_ffn1", "w_ffn1_in", "w_ffn1_out", "g_mix", "w_in", "b_in", "rel_bias", "w_br_sb", "w_br_ch",
             "w_br_fox", "w_out", "g_ffn2", "w_ffn2_in", "w_ffn2_out", "g_final")
    w = dict(zip(names, (g_ffn1, w_ffn1_in, w_ffn1_out, g_mix, w_in, b_in, rel_bias, w_br_sb, w_br_ch,
                         w_br_fox, w_out, g_ffn2, w_ffn2_in, w_ffn2_out, g_final)))
    m = dict(zip(names, (m_g_ffn1, m_w_ffn1_in, m_w_ffn1_out, m_g_mix, m_w_in, m_b_in, m_rel_bias, m_w_br_sb,
                         m_w_br_ch, m_w_br_fox, m_w_out, m_g_ffn2, m_w_ffn2_in, m_w_ffn2_out, m_g_final)))
    v = dict(zip(names, (v_g_ffn1, v_w_ffn1_in, v_w_ffn1_out, v_g_mix, v_w_in, v_b_in, v_rel_bias, v_w_br_sb,
                         v_w_br_ch, v_w_br_fox, v_w_out, v_g_ffn2, v_w_ffn2_in, v_w_ffn2_out, v_g_final)))
    xs = x[0]
    tgt = loss_target[0]

    chip = (2 * lax.axis_index("x") + lax.axis_index("y")).astype(jnp.int32)[None]
    placed = [_place_own(w[n], chip, f"place_{n}") for n in BIG]
    bufs = [[p[l].reshape(N_CHIPS, 2, p[l].shape[1] // 2, p[l].shape[2]) for p in placed] for l in range(DEPTH)]
    padded_w_in = {}

    def layer_weights(l):
        lw = {n: b.reshape((N_CHIPS,) + w[n].shape[1:]) for n, b in zip(BIG, bufs[l])}
        if l not in padded_w_in:
            whole = jnp.concatenate([lw["w_in"][j] for j in range(N_CHIPS)], axis=1)
            forget_t = jnp.pad(whole[:, QKV_WIDTH:QKV_WIDTH + H_FOX].T, ((0, F_PAD - H_FOX), (0, 0)))
            padded_w_in[l] = (_pad_w_in(whole), forget_t)
        lw["w_in"], lw["w_forget_t"] = padded_w_in[l]
        lw["b_forget"] = jnp.pad(w["b_in"][l][QKV_WIDTH:QKV_WIDTH + H_FOX], (0, 8 - H_FOX))[:, None]
        lw["b_in"] = _pad_w_in(w["b_in"][l][None, :])
        lw["bias"], lw["bias_vjp"] = jax.vjp(_band_bias, w["rel_bias"][l])
        for n in ("g_ffn1", "g_mix", "g_ffn2"):
            lw[n] = w[n][l][None, :]
        return lw

    def landed_in(l, idx):
        def done(carry):
            for i, b in zip(idx, carry.out[0]):
                bufs[l][i] = b
        return done

    def over_ici(l, idx):
        return lambda: _Carry([], [bufs[l][i] for i in idx], [], 3 * len(idx), _gather_over_ici, landed_in(l, idx))

    def to_sibling(l, idx):
        return lambda: _Carry([], [bufs[l][i] for i in idx], [], 3 * len(idx), _gather_to_sibling, landed_in(l, idx))

    def ffn_fwd(x_in, lw, which, tag, carries):
        return _ffn_fwd(x_in, lw[f"g_{which}"], lw[f"w_{which}_in"], lw[f"w_{which}_out"], tag, carries)

    def ffn_bwd(x_in, lw, which, saved_acts, dy, tag, carries):
        return _ffn_bwd(x_in, lw[f"g_{which}"], lw[f"w_{which}_in"], lw[f"w_{which}_out"], saved_acts, dy, tag,
                        carries)

    FFN1, W_IN, MIXER_REST, FFN2_IN, FFN2_OUT = (0, 1), (2,), (3, 4, 5, 6), (7,), (8,)
    first = FFN1 + W_IN
    for i, b in zip(first, _gather_layer([bufs[0][i] for i in first], "gather_l0")):
        bufs[0][i] = b
    fwd_carries = [
        {"ffn1": {"in": [over_ici(0, MIXER_REST)], "out": [to_sibling(0, MIXER_REST), over_ici(0, FFN2_OUT)]},
         "mix": {"proj": [to_sibling(0, FFN2_OUT), over_ici(1, MIXER_REST)],
                 "sb": [over_ici(0, FFN2_IN), over_ici(1, FFN2_OUT)],
                 "ch": [to_sibling(0, FFN2_IN), over_ici(1, FFN1)],
                 "fox": [over_ici(1, W_IN)]},
         "ffn2": {"in": [to_sibling(1, MIXER_REST + FFN2_OUT + FFN1 + W_IN)]}},
        {"ffn1": {}, "mix": {"sb": [over_ici(1, FFN2_IN)], "ch": [to_sibling(1, FFN2_IN)]}, "ffn2": {}},
    ]

    saved = []
    act = xs
    for l in range(DEPTH):
        x0 = act
        x1, s1 = ffn_fwd(x0, layer_weights(l), "ffn1", f"l{l}_ffn1", fwd_carries[l]["ffn1"])
        x2, s2 = _mixer_fwd(x1, layer_weights(l), f"l{l}_mix", fwd_carries[l]["mix"])
        act, s3 = ffn_fwd(x2, layer_weights(l), "ffn2", f"l{l}_ffn2", fwd_carries[l]["ffn2"])
        saved.append((x0, s1, x1, s2, x2, s3))
    layers = [layer_weights(l) for l in range(DEPTH)]

    dact, dg_final, loss_row = _final_loss(act, w["g_final"][None, :], tgt, "final_loss")

    big_grads = {n: [None] * DEPTH for n in BIG}
    small_grads = {n: [None] * DEPTH for n in ("g_ffn1", "g_mix", "b_in", "rel_bias", "g_ffn2")}
    pair = [[None] * len(BIG) for _ in range(DEPTH)]
    landed = [[None] * len(BIG) for _ in range(DEPTH)]

    def pair_up(l, idx, tag):
        for i, p in zip(idx, _pair_sums([big_grads[BIG[i]][l] for i in idx], tag)):
            pair[l][i] = p

    core = lax.axis_index("c").astype(jnp.int32)[None]

    def to_sibling_half(l, idx, tag):
        def plan(srcs, bufs, new):
            x, y, c = _position()
            return [(s.at[1 - c], d, (x, y, 1 - c)) for s, d in zip(srcs, new)]
        def done(carry):
            for j, (i, got) in enumerate(zip(idx, carry.out[1])):
                pair[l][i] = _add_pair(core, big_grads[BIG[i]][l], got, f"grad_pair_sum_{tag}_{j}")
        def make():
            split = [big_grads[BIG[i]][l] for i in idx]
            return _Carry(split, [], [jax.ShapeDtypeStruct(s.shape[1:], s.dtype) for s in split], len(idx), plan, done)
        return make

    def exchange(l, idx):
        def done(carry):
            for i, a in zip(idx, carry.out[1]):
                landed[l][i] = a
        def make():
            carry = _scatter_carry([pair[l][i] for i in idx])
            carry.done = done
            return carry
        return make

    def exchange_one_way(l, i, k):
        def plan(srcs, bufs, new):
            x, y, c = _position()
            chip_k, idx_k = _other_chips(x, y)[k]
            return [(srcs[0].at[idx_k], bufs[0].at[k], (*chip_k, c))]
        def done(carry):
            landed[l][i] = carry.out[0][0]
        def make():
            if landed[l][i] is None:
                landed[l][i] = lax.empty((3,) + pair[l][i].shape[1:], BF16)
            return _Carry([pair[l][i]], [landed[l][i]], [], 1, plan, done)
        return make

    bwd_carries = [
        {"ffn2": {},
         "mix": {"dmerged": [to_sibling_half(0, FFN2_IN + FFN2_OUT, "l0_ffn2")],
                 "dsb": [exchange(1, FFN1 + W_IN)], "dch": [exchange(1, MIXER_REST + FFN2_IN + FFN2_OUT)],
                 "dfox": [exchange(0, FFN2_IN + FFN2_OUT)]},
         "ffn1": {"da": [exchange(0, MIXER_REST)], "dwout": [exchange_one_way(0, 2, 0)],
                  "dwin": [exchange_one_way(0, 2, 1)], "dh": [exchange_one_way(0, 2, 2)]}},
        {"ffn2": {}, "mix": {},
         "ffn1": {"da": [to_sibling_half(1, W_IN + MIXER_REST + FFN2_IN + FFN2_OUT, "l1_rest")]}},
    ]
    for l in reversed(range(DEPTH)):
        lw = layers[l]
        x0, s1, x1, s2, x2, s3 = saved[l]
        dx2, small_grads["g_ffn2"][l], big_grads["w_ffn2_in"][l], big_grads["w_ffn2_out"][l] = ffn_bwd(
            x2, lw, "ffn2", s3, dact, f"l{l}_ffn2", bwd_carries[l]["ffn2"])
        dx1, mg = _mixer_bwd(x1, lw, s2, dx2, f"l{l}_mix", bwd_carries[l]["mix"])
        for n in ("w_in", "w_br_sb", "w_br_ch", "w_br_fox", "w_out"):
            big_grads[n][l] = mg[n]
        small_grads["b_in"][l] = _unpad_w_in(mg["b_in"])[0]
        small_grads["g_mix"][l] = mg["g_mix"][0]
        small_grads["rel_bias"][l] = mg["rel_bias"]
        if l == 0:
            pair_up(0, W_IN + MIXER_REST, "l0_mix")
        dact, dg1, big_grads["w_ffn1_in"][l], big_grads["w_ffn1_out"][l] = ffn_bwd(
            x0, lw, "ffn1", s1, dx1, f"l{l}_ffn1", bwd_carries[l]["ffn1"])
        small_grads["g_ffn1"][l] = dg1[0]
        small_grads["g_ffn2"][l] = small_grads["g_ffn2"][l][0]
        if l == 1:
            pair_up(1, FFN1, "l1_ffn1")
    grad_x = dact[None]
    pair_up(0, FFN1, "l0_ffn1")
    for i, a in zip(FFN1, _scatter_chips([pair[0][i] for i in FFN1], "grad_scatter_l0")):
        landed[0][i] = a

    small = {n: jnp.stack(small_grads[n]) for n in small_grads}
    small["g_final"] = dg_final[0]
    small_sum, loss = _unpack_small(_allreduce_small(_pack_small(small, loss_row[0, :1]), "allreduce_small"),
                                    {n: w[n].shape for n in SMALL})

    axes = [1 if n in ROW_SHARDED else 0 for n in BIG] * DEPTH
    summed = _finish_grads(pair[0] + pair[1], landed[0] + landed[1], axes, "all")
    grads = {n: jnp.stack([summed[i], summed[len(BIG) + i]]) for i, n in enumerate(BIG)}
    grads.update(small_sum)

    delta, new_m, new_v = {}, {}, {}
    for n in BIG:
        shape = w[n].shape
        flat = lambda t: t.reshape(-1, shape[-1])
        d, nm, nv = _adamw(flat(w[n]), flat(grads[n]), flat(m[n]), flat(v[n]), f"adamw_{n}")
        delta[n], new_m[n], new_v[n] = d.reshape(shape), nm.reshape(shape), nv.reshape(shape)
    zero = jnp.zeros((1,), F32)
    sd, sm, sv = _adamw(_pack_small(w, zero), _pack_small(grads, zero), _pack_small(m, zero), _pack_small(v, zero),
                        "adamw_small")
    shapes = {n: w[n].shape for n in SMALL}
    for dst, src in ((delta, sd), (new_m, sm), (new_v, sv)):
        dst.update(_unpack_small(src, shapes)[0])

    return (loss, grad_x, *[grads[n] for n in names], *[delta[n] for n in names],
            *[new_m[n] for n in names], *[new_v[n] for n in names])
```

```python
import functools

import numpy as np
import jax
import jax.numpy as jnp
from jax import lax
from jax.experimental import pallas as pl
from jax.experimental.pallas import tpu as pltpu

F32 = jnp.float32
BF16 = jnp.bfloat16

D_MODEL = 1024
DEPTH = 2
CHUNK = 64
HEAD_DIM = 64
H_SB, H_CH, H_FOX = 4, 8, 4
W_SB, W_CH, W_FOX = H_SB * HEAD_DIM, H_CH * HEAD_DIM, H_FOX * HEAD_DIM
LEFT_CHUNKS = 8
MAX_REL = 128
N_REL = 2 * MAX_REL + 1
D_FF = 2816
QKV_WIDTH = 3 * (W_SB + W_CH + W_FOX)
GATE_WIDTH = 3 * D_MODEL
IN_WIDTH = QKV_WIDTH + H_FOX + GATE_WIDTH
F_PAD = 128
P_WIDTH = QKV_WIDTH + GATE_WIDTH + F_PAD
RMS_EPS = 1e-6
NEG = -1e30
SCALE = HEAD_DIM ** -0.5
QB = 256
BAND = (LEFT_CHUNKS + 2) * CHUNK
BAND_PAD = BAND - CHUNK

ADAM_LR, ADAM_B1, ADAM_B2, ADAM_EPS, ADAM_WD, ADAM_STEP = 0.001, 0.9, 0.999, 1e-08, 0.01, 10

N_CHIPS = 4
VMEM_BUDGET = 52 * 1024 * 1024

NT = (((1,), (1,)), ((), ()))
TN = (((0,), (0,)), ((), ()))
NN = (((1,), (0,)), ((), ()))
MESH = pl.DeviceIdType.MESH


def _pcall(body, **kw):
    return pl.pallas_call(body, **kw)


class _Carry:
    def __init__(self, srcs, bufs, new, count, plan, done=None):
        self.srcs, self.bufs, self.new, self.count, self.plan = list(srcs), list(bufs), list(new), count, plan
        self.out, self.done = None, done

    @staticmethod
    def join(parts):
        parts = [p for p in parts if p is not None]
        if not parts:
            return None

        def plan(srcs, bufs, new):
            copies, s, b, n = [], 0, 0, 0
            for p in parts:
                copies += p.plan(srcs[s:s + len(p.srcs)], bufs[b:b + len(p.bufs)], new[n:n + len(p.new)])
                s, b, n = s + len(p.srcs), b + len(p.bufs), n + len(p.new)
            return copies

        whole = _Carry(sum((p.srcs for p in parts), []), sum((p.bufs for p in parts), []),
                       sum((p.new for p in parts), []), sum(p.count for p in parts), plan)
        whole.parts = parts
        return whole

    def share_out(self):
        b, n = 0, 0
        for p in getattr(self, "parts", []):
            p.out = (self.out[0][b:b + len(p.bufs)], self.out[1][n:n + len(p.new)])
            b, n = b + len(p.bufs), n + len(p.new)
            p.share_out()
        if self.done is not None:
            self.done(self)


def _carry_of(carries, key):
    return _Carry.join([make() for make in carries.get(key, [])])


def _pcall_carrying(body, carry, **kw):
    if carry is None:
        return _pcall(body, **kw)
    in_specs = list(kw.pop("in_specs"))
    out_specs, out_shape = kw.pop("out_specs"), kw.pop("out_shape")
    single = not isinstance(out_shape, (list, tuple))
    out_specs = [out_specs] if single else list(out_specs)
    out_shape = [out_shape] if single else list(out_shape)
    scratch = list(kw.pop("scratch_shapes", []))
    grid = tuple(kw.get("grid", ()))
    n_in, n_out, n_scr = len(in_specs), len(out_specs), len(scratch)
    ns, nb, nn = len(carry.srcs), len(carry.bufs), len(carry.new)

    def body2(*refs):
        ins, srcs = refs[:n_in], refs[n_in:n_in + ns]
        at = n_in + ns + nb
        outs, bufs, new = refs[at:at + n_out], refs[at + n_out:at + n_out + nb], refs[at + n_out + nb:at + n_out + nb + nn]
        at += n_out + nb + nn
        scr, (send_sems, recv_sems) = refs[at:at + n_scr], refs[at + n_scr:]

        def copies():
            return [_remote_copy(s, d, send_sems, recv_sems, k, to)
                    for k, (s, d, to) in enumerate(carry.plan(srcs, bufs, new))]

        def start():
            for cp in copies():
                cp.start()

        def wait():
            for cp in copies():
                cp.wait()

        if grid:
            ids = [pl.program_id(a) for a in range(len(grid))]
            pl.when(functools.reduce(jnp.logical_and, [i == 0 for i in ids]))(start)
        else:
            start()
        body(*ins, *outs, *scr)
        if grid:
            pl.when(functools.reduce(jnp.logical_and, [i == g - 1 for i, g in zip(ids, grid)]))(wait)
        else:
            wait()

    call = _pcall(
        body2, in_specs=in_specs + [HBM_SPEC] * (ns + nb), out_specs=out_specs + [HBM_SPEC] * (nb + nn),
        out_shape=out_shape + [jax.ShapeDtypeStruct(b.shape, b.dtype) for b in carry.bufs] + carry.new,
        scratch_shapes=scratch + [pltpu.SemaphoreType.DMA((carry.count,)), pltpu.SemaphoreType.DMA((carry.count,))],
        input_output_aliases={n_in + ns + j: n_out + j for j in range(nb)}, **kw)

    def apply(*args):
        res = call(*args, *carry.srcs, *carry.bufs)
        carry.out = (list(res[n_out:n_out + nb]), list(res[n_out + nb:]))
        carry.share_out()
        return res[0] if single else list(res[:n_out])

    return apply


def _pick(n, cap, mult=128):
    best = None
    d = mult
    while d <= min(n, cap):
        if n % d == 0:
            best = d
        d += mult
    return n if best is None else best


def _nbytes(shape, dtype):
    return int(np.prod(shape)) * jnp.dtype(dtype).itemsize


def _mm(a, b, *, mode, name, out_dtype=F32, alpha=1.0, bias=None, res=None, tm_cap=1024, tn_cap=512,
        tk_cap=2816, gathered=None, out_split=None, carry=None, norm_bwd=None):
    if mode == "tn":
        K, M = a.shape
    else:
        M, K = a.shape
    dn = {"nn": NN, "nt": NT, "tn": TN}[mode]
    b_rows = None
    if gathered is None:
        N = b.shape[0] if mode == "nt" else b.shape[1]
        tn = {None: _pick(N, tn_cap), "col": N // N_CHIPS, "row": N // 2}[out_split]
        if out_split == "col":
            tm_cap = min(tm_cap, M // 2)
        tk = K if K <= tk_cap else _pick(K, tk_cap)
        b_spec = (pl.BlockSpec((tn, tk), lambda i, j, k: (j, k)) if mode == "nt"
                  else pl.BlockSpec((tk, tn), lambda i, j, k: (k, j)))
    else:
        r, c = b.shape[1:]
        rows, cols = (r, N_CHIPS * c) if gathered == "col" else (N_CHIPS * r, c)
        N = rows if mode == "nt" else cols
        assert K == (cols if mode == "nt" else rows), (name, K, rows, cols)
        if gathered == "col" and mode == "nn":
            tn, tk = c, (r if r <= tk_cap else _pick(r, tk_cap))
            b_spec = pl.BlockSpec((None, tk, c), lambda i, j, k: (j, k, 0))
        elif gathered == "col":
            tn, tk = _pick(r, tn_cap), c
            b_spec = pl.BlockSpec((None, tn, c), lambda i, j, k: (k, j, 0))
        elif mode == "nn":
            tn, tk, b_rows = _pick(c, tn_cap), K, N_CHIPS
            b_spec = pl.BlockSpec((N_CHIPS, r, tn), lambda i, j, k: (0, 0, j))
        else:
            tn, tk, b_rows = 2 * r, K, 2
            b_spec = pl.BlockSpec((2, r, c), lambda i, j, k: (j, 0, 0))
    tm = _pick(M, tm_cap)
    nk = K // tk

    a_spec = (pl.BlockSpec((tk, tm), lambda i, j, k: (k, i)) if mode == "tn"
              else pl.BlockSpec((tm, tk), lambda i, j, k: (i, k)))
    in_specs = [a_spec, b_spec]
    args = [a, b]
    if bias is not None:
        in_specs.append(pl.BlockSpec((1, tn), lambda i, j, k: (0, j)))
        args.append(bias)
    if res is not None:
        in_specs.append(pl.BlockSpec((tm, tn), lambda i, j, k: (i, j)))
        args.append(res)
    if norm_bwd is not None:
        assert tn == N and alpha == 1.0 and out_split is None and bias is None and res is None, name
        x_in, g_in, dres_in = norm_bwd
        in_specs += [pl.BlockSpec((tm, tn), lambda i, j, k: (i, 0)), pl.BlockSpec((1, tn), lambda i, j, k: (0, 0)),
                     pl.BlockSpec((tm, tn), lambda i, j, k: (i, 0))]
        args += [x_in, g_in, dres_in]

    est = 2 * (_nbytes((tm, tk), a.dtype) + _nbytes((tk, tn), b.dtype) + _nbytes((tm, tn), out_dtype))
    est += _nbytes((tm, tn), F32) * (3 if res is not None else 1)
    est += _nbytes((tm, tn), F32) * (4 if norm_bwd is not None else 0)
    assert est < VMEM_BUDGET, (name, est)

    def body(*refs):
        a_ref, b_ref = refs[0], refs[1]
        pos = 2
        bias_ref = res_ref = None
        if bias is not None:
            bias_ref = refs[pos]
            pos += 1
        if res is not None:
            res_ref = refs[pos]
            pos += 1
        if norm_bwd is not None:
            x_ref, g_ref, dres_ref = refs[pos:pos + 3]
            pos += 3
        o_ref = refs[pos]
        if norm_bwd is not None:
            dg_ref = refs[pos + 1]
            pos += 1
        acc_ref = refs[pos + 1] if nk > 1 else None

        bv = b_ref[...]
        if b_rows is not None:
            bv = bv.reshape(b_rows * bv.shape[1], bv.shape[2])
        part = lax.dot_general(a_ref[...].astype(BF16), bv.astype(BF16), dn, preferred_element_type=F32)

        def finish(acc):
            if alpha != 1.0:
                acc = acc * alpha
            if bias_ref is not None:
                acc = acc + bias_ref[...]
            if res_ref is not None:
                acc = acc + res_ref[...]
            if norm_bwd is not None:
                xv = x_ref[...]
                rstd = lax.rsqrt(jnp.mean(xv * xv, axis=-1, keepdims=True) + RMS_EPS)
                xhat = xv * rstd
                dyg = acc * g_ref[...]
                part_g = jnp.sum(acc * xhat, axis=0, keepdims=True)
                acc = dres_ref[...] + rstd * (dyg - xhat * jnp.mean(dyg * xhat, axis=-1, keepdims=True))
                first = pl.program_id(0) == 0

                @pl.when(first)
                def _():
                    dg_ref[...] = part_g

                @pl.when(jnp.logical_not(first))
                def _():
                    dg_ref[...] += part_g
            o_ref[...] = acc.astype(out_dtype)

        if nk == 1:
            finish(part)
        else:
            k = pl.program_id(2)

            @pl.when(k == 0)
            def _():
                acc_ref[...] = part

            @pl.when(k > 0)
            def _():
                acc_ref[...] += part

            @pl.when(k == nk - 1)
            def _():
                finish(acc_ref[...])

    if out_split == "col":
        per_half = M // 2 // tm
        out_spec = pl.BlockSpec((None, None, tm, tn), lambda i, j, k: (i // per_half, j, i % per_half, 0))
        out_shape = jax.ShapeDtypeStruct((2, N_CHIPS, M // 2, tn), out_dtype)
    elif out_split == "row":
        out_spec = pl.BlockSpec((None, tm, tn), lambda i, j, k: (j, i, 0))
        out_shape = jax.ShapeDtypeStruct((2, M, tn), out_dtype)
    else:
        out_spec = pl.BlockSpec((tm, tn), lambda i, j, k: (i, j))
        out_shape = jax.ShapeDtypeStruct((M, N), out_dtype)
    semantics = ("parallel", "parallel", "arbitrary")
    if norm_bwd is not None:
        out_spec = [out_spec, pl.BlockSpec((1, tn), lambda i, j, k: (0, 0))]
        out_shape = [out_shape, jax.ShapeDtypeStruct((1, N), F32)]
        semantics = ("arbitrary", "arbitrary", "arbitrary")
    return _pcall_carrying(
        body, carry, name=name, grid=(M // tm, N // tn, nk), in_specs=in_specs, out_specs=out_spec,
        out_shape=out_shape,
        scratch_shapes=[pltpu.VMEM((tm, tn), F32)] if nk > 1 else [],
        compiler_params=pltpu.CompilerParams(dimension_semantics=semantics),
    )(*args)


def _rmsnorm_fwd(x, g, name):
    T, Dm = x.shape
    tm = _pick(T, 256, 8)

    def body(x_ref, g_ref, h_ref):
        xv = x_ref[...]
        rstd = lax.rsqrt(jnp.mean(xv * xv, axis=-1, keepdims=True) + RMS_EPS)
        h_ref[...] = (xv * rstd * g_ref[...]).astype(BF16)

    return _pcall(
        body, name=name, grid=(T // tm,),
        in_specs=[pl.BlockSpec((tm, Dm), lambda i: (i, 0)), pl.BlockSpec((1, Dm), lambda i: (0, 0))],
        out_specs=pl.BlockSpec((tm, Dm), lambda i: (i, 0)),
        out_shape=jax.ShapeDtypeStruct((T, Dm), BF16),
    )(x, g)


def _final_loss(x, g, tgt, name):
    T, Dm = x.shape
    tm = _pick(T, 256, 8)

    def body(x_ref, g_ref, t_ref, dx_ref, dg_ref, loss_ref):
        xv = x_ref[...]
        gv = g_ref[...]
        rstd = lax.rsqrt(jnp.mean(xv * xv, axis=-1, keepdims=True) + RMS_EPS)
        xhat = xv * rstd
        err = xhat * gv - t_ref[...]
        part_loss = 0.5 * jnp.sum(jnp.mean(err * err, axis=-1, keepdims=True), axis=0, keepdims=True)
        dy = err * (1.0 / Dm)
        dyg = dy * gv
        dx_ref[...] = rstd * (dyg - xhat * jnp.mean(dyg * xhat, axis=-1, keepdims=True))
        part_g = jnp.sum(dy * xhat, axis=0, keepdims=True)
        part_l = jnp.broadcast_to(part_loss, (1, 128))

        @pl.when(pl.program_id(0) == 0)
        def _():
            dg_ref[...] = part_g
            loss_ref[...] = part_l

        @pl.when(pl.program_id(0) > 0)
        def _():
            dg_ref[...] += part_g
            loss_ref[...] += part_l

    row = pl.BlockSpec((tm, Dm), lambda i: (i, 0))
    vec = pl.BlockSpec((1, Dm), lambda i: (0, 0))
    return _pcall(
        body, name=name, grid=(T // tm,), in_specs=[row, vec, row],
        out_specs=[row, vec, pl.BlockSpec((1, 128), lambda i: (0, 0))],
        out_shape=[jax.ShapeDtypeStruct((T, Dm), F32), jax.ShapeDtypeStruct((1, Dm), F32),
                   jax.ShapeDtypeStruct((1, 128), F32)],
        compiler_params=pltpu.CompilerParams(dimension_semantics=("arbitrary",)),
    )(x, g, tgt)


def _sigmoid(z):
    return 1.0 / (1.0 + jnp.exp(-z))


def _ffn_in_swiglu(x, g, w_in, name, carry=None):
    T, Dm = x.shape
    cw = w_in.shape[2]
    tm = _pick(T, 512, 16)

    def body(x_ref, gain_ref, wg_ref, wu_ref, h_ref, g_ref, u_ref, a_ref):
        @pl.when(pl.program_id(1) == 0)
        def _():
            xv = x_ref[...]
            rstd = lax.rsqrt(jnp.mean(xv * xv, axis=-1, keepdims=True) + RMS_EPS)
            h_ref[...] = (xv * rstd * gain_ref[...]).astype(BF16)

        hv = h_ref[...]
        gv = jnp.dot(hv, wg_ref[...], preferred_element_type=F32)
        uv = jnp.dot(hv, wu_ref[...], preferred_element_type=F32)
        g_ref[...] = gv.astype(BF16)
        u_ref[...] = uv.astype(BF16)
        a_ref[...] = (gv * _sigmoid(gv) * uv).astype(BF16)

    row = pl.BlockSpec((tm, Dm), lambda i, j: (i, 0))
    out = pl.BlockSpec((tm, cw), lambda i, j: (i, j))
    return _pcall_carrying(
        body, carry, name=name, grid=(T // tm, 2),
        in_specs=[row, pl.BlockSpec((1, Dm), lambda i, j: (0, 0)), pl.BlockSpec((None, Dm, cw), lambda i, j: (j, 0, 0)),
                  pl.BlockSpec((None, Dm, cw), lambda i, j: (j + 2, 0, 0))],
        out_specs=[row, out, out, out],
        out_shape=[jax.ShapeDtypeStruct((T, Dm), BF16)] + [jax.ShapeDtypeStruct((T, D_FF), BF16)] * 3,
        compiler_params=pltpu.CompilerParams(dimension_semantics=("parallel", "arbitrary")),
    )(x, g, w_in, w_in)


def _swiglu_bwd(g, u, da, name):
    T = g.shape[0]
    tm = _pick(T, 256, 16)

    def body(g_ref, u_ref, da_ref, d_ref):
        gv = g_ref[...].astype(F32)
        uv = u_ref[...].astype(F32)
        dav = da_ref[...].astype(F32)
        sg = _sigmoid(gv)
        d_ref[:, :D_FF] = (dav * uv * (sg + gv * sg * (1.0 - sg))).astype(BF16)
        d_ref[:, D_FF:] = (dav * gv * sg).astype(BF16)

    row = pl.BlockSpec((tm, D_FF), lambda i: (i, 0))
    return _pcall(
        body, name=name, grid=(T // tm,), in_specs=[row, row, row],
        out_specs=pl.BlockSpec((tm, 2 * D_FF), lambda i: (i, 0)),
        out_shape=jax.ShapeDtypeStruct((T, 2 * D_FF), BF16),
    )(g, u, da)


def _branches_fwd(p, outs, weights, name):
    T = p.shape[0]
    cw = weights[0].shape[2]
    tm = _pick(T, 1024, 16)
    first_gate, per_branch = QKV_WIDTH // cw, D_MODEL // cw

    def body(*refs):
        gates, o_refs, w_refs, y_refs, m_ref = refs[0:3], refs[3:6], refs[6:9], refs[9:12], refs[12]
        merged = None
        for g_ref, o_ref, w_ref, y_ref in zip(gates, o_refs, w_refs, y_refs):
            y = jnp.dot(o_ref[...], w_ref[...], preferred_element_type=F32)
            y_ref[...] = y
            term = _sigmoid(g_ref[...]) * y
            merged = term if merged is None else merged + term
        m_ref[...] = merged.astype(BF16)

    gate_specs = [pl.BlockSpec((tm, cw), functools.partial(
        lambda i, j, kk: (i, first_gate + per_branch * kk + j), kk=kk)) for kk in range(3)]
    o_specs = [pl.BlockSpec((tm, o.shape[1]), lambda i, j: (i, 0)) for o in outs]
    w_specs = [pl.BlockSpec((None, wk.shape[1], cw), lambda i, j: (j, 0, 0)) for wk in weights]
    tile = pl.BlockSpec((tm, cw), lambda i, j: (i, j))
    return _pcall(
        body, name=name, grid=(T // tm, N_CHIPS), in_specs=gate_specs + o_specs + w_specs, out_specs=[tile] * 4,
        out_shape=[jax.ShapeDtypeStruct((T, D_MODEL), F32)] * 3 + [jax.ShapeDtypeStruct((T, D_MODEL), BF16)],
    )(p, p, p, *outs, *weights)


def _branches_bwd(p, ys, dm, weights, name):
    T = p.shape[0]
    cw = weights[0].shape[2]
    tm = _pick(T, 1024, 16)
    first_gate, per_branch = QKV_WIDTH // cw, D_MODEL // cw
    widths = [wk.shape[1] for wk in weights]

    def body(*refs):
        gates, y_refs, dm_ref, w_refs = refs[0:3], refs[3:6], refs[6], refs[7:10]
        dy_refs, dg_refs, do_refs, acc_refs = refs[10:13], refs[13:16], refs[16:19], refs[19:22]
        j = pl.program_id(1)
        dmv = dm_ref[...]
        for g_ref, y_ref, w_ref, dy_ref, dg_ref, do_ref, acc_ref in zip(gates, y_refs, w_refs, dy_refs, dg_refs,
                                                                       do_refs, acc_refs):
            s = _sigmoid(g_ref[...])
            dy = (s * dmv).astype(BF16)
            dy_ref[...] = dy
            dg_ref[...] = (dmv * y_ref[...] * s * (1.0 - s)).astype(BF16)
            part = _dot_nt(dy, w_ref[...])

            @pl.when(j == 0)
            def _():
                acc_ref[...] = part

            @pl.when(j > 0)
            def _():
                acc_ref[...] += part

            @pl.when(j == N_CHIPS - 1)
            def _():
                do_ref[...] = acc_ref[...].astype(BF16)

    gate_specs = [pl.BlockSpec((tm, cw), functools.partial(
        lambda i, j, kk: (i, first_gate + per_branch * kk + j), kk=kk)) for kk in range(3)]
    tile = pl.BlockSpec((tm, cw), lambda i, j: (i, j))
    w_specs = [pl.BlockSpec((None, r, cw), lambda i, j: (j, 0, 0)) for r in widths]
    do_specs = [pl.BlockSpec((tm, r), lambda i, j: (i, 0)) for r in widths]
    wide = jax.ShapeDtypeStruct((T, D_MODEL), BF16)
    out = _pcall(
        body, name=name, grid=(T // tm, N_CHIPS), in_specs=gate_specs + [tile] * 4 + w_specs,
        out_specs=[tile] * 6 + do_specs,
        out_shape=[wide] * 6 + [jax.ShapeDtypeStruct((T, r), BF16) for r in widths],
        scratch_shapes=[pltpu.VMEM((tm, r), F32) for r in widths],
        compiler_params=pltpu.CompilerParams(dimension_semantics=("parallel", "arbitrary")),
    )(p, p, p, *ys, dm, *weights)
    return out[0:3], out[3:6], out[6:9]


def _colsum(a, name):
    T, N = a.shape
    tm = _pick(T, 256, 16)

    def body(a_ref, o_ref):
        part = jnp.sum(a_ref[...].astype(F32), axis=0, keepdims=True)

        @pl.when(pl.program_id(0) == 0)
        def _():
            o_ref[...] = part

        @pl.when(pl.program_id(0) > 0)
        def _():
            o_ref[...] += part

    return _pcall(
        body, name=name, grid=(T // tm,), in_specs=[pl.BlockSpec((tm, N), lambda i: (i, 0))],
        out_specs=pl.BlockSpec((1, N), lambda i: (0, 0)), out_shape=jax.ShapeDtypeStruct((1, N), F32),
        compiler_params=pltpu.CompilerParams(dimension_semantics=("arbitrary",)),
    )(a)


def _adamw(w, g, m, v, name):
    R, C = w.shape
    tr = 256 if R % 256 == 0 else R
    c1 = 1.0 / (1.0 - ADAM_B1 ** ADAM_STEP)
    c2 = 1.0 / (1.0 - ADAM_B2 ** ADAM_STEP)

    def body(w_ref, g_ref, m_ref, v_ref, d_ref, nm_ref, nv_ref):
        gv = g_ref[...]
        mn = ADAM_B1 * m_ref[...] + (1.0 - ADAM_B1) * gv
        vn = ADAM_B2 * v_ref[...] + (1.0 - ADAM_B2) * (gv * gv)
        nm_ref[...] = mn
        nv_ref[...] = vn
        d_ref[...] = -ADAM_LR * ((mn * c1) / (jnp.sqrt(vn * c2) + ADAM_EPS) + ADAM_WD * w_ref[...])

    spec = pl.BlockSpec((tr, C), lambda i: (i, 0))
    return _pcall(
        body, name=name, grid=(R // tr,), in_specs=[spec] * 4, out_specs=[spec] * 3,
        out_shape=[jax.ShapeDtypeStruct((R, C), F32)] * 3,
    )(w, g, m, v)


def _log_sigmoid(z):
    return jnp.minimum(z, 0.0) - jnp.log(1.0 + jnp.exp(-jnp.abs(z)))


def _dot3(x, m01):
    x1 = x.astype(BF16)
    r1 = x - x1.astype(F32)
    x2 = r1.astype(BF16)
    x3 = (r1 - x2.astype(F32)).astype(BF16)
    n = x.shape[0]
    if n % 16:
        return (jnp.dot(x1, m01, preferred_element_type=F32) + jnp.dot(x2, m01, preferred_element_type=F32)
                + jnp.dot(x3, m01, preferred_element_type=F32))
    y = jnp.dot(jnp.concatenate([x1, x2, x3], axis=0), m01, preferred_element_type=F32)
    return y[:n] + y[n:2 * n] + y[2 * n:]


def _dot_nt(a, b):
    return lax.dot_general(a, b, NT, preferred_element_type=F32)


def _dot_tn(a, b):
    return lax.dot_general(a, b, TN, preferred_element_type=F32)


def _iota2(shape):
    return lax.broadcasted_iota(jnp.int32, shape, 0), lax.broadcasted_iota(jnp.int32, shape, 1)


HEADS_PER_STEP = 4
HEADS = range(HEADS_PER_STEP)


CHUNK_HEADS_PER_STEP = 4
CHUNK_HEADS = range(CHUNK_HEADS_PER_STEP)


def _head_spec(T, rows=None, width=HEAD_DIM, heads=HEADS_PER_STEP):
    return pl.BlockSpec((heads, T if rows is None else rows, width), lambda g: (g, 0, 0))


def _sb_weights(qb, kb, t0, s0, racc, m_gt, row, col):
    z = _dot_nt(qb, kb) * SCALE
    strict = (s0 + col) < (t0 + row)
    lb = _log_sigmoid(z)
    lf = jnp.where(strict, lb - z, 0.0)
    between = _dot3(lf, m_gt) + racc
    w = jnp.where(strict, jnp.exp(lb + between), 0.0)
    return strict, lb, lf, w


def _sb_fwd(q, k, v, name, carry=None):
    H, T, _ = q.shape
    nb = T // QB

    def body(q_ref, k_ref, v_ref, o_ref):
        row, col = _iota2((QB, QB))
        m_gt = (row > col).astype(BF16)

        def qblock(i, _):
            t0 = pl.multiple_of(i * QB, QB)
            qbs = [q_ref[h, pl.ds(t0, QB), :].astype(BF16) for h in HEADS]

            def kblock(jj, carry):
                s0 = pl.multiple_of((i - jj) * QB, QB)
                out = []
                for h in HEADS:
                    acc, racc = carry[h]
                    kb = k_ref[h, pl.ds(s0, QB), :].astype(BF16)
                    vb = v_ref[h, pl.ds(s0, QB), :].astype(BF16)
                    _, _, lf, w = _sb_weights(qbs[h], kb, t0, s0, racc, m_gt, row, col)
                    acc = acc + jnp.dot(w.astype(BF16), vb, preferred_element_type=F32)
                    out.append((acc, racc + jnp.sum(lf, axis=1, keepdims=True)))
                return tuple(out)

            res = lax.fori_loop(0, i + 1, kblock,
                                tuple((jnp.zeros((QB, HEAD_DIM), F32), jnp.zeros((QB, 1), F32)) for _ in HEADS))
            for h in HEADS:
                o_ref[h, pl.ds(t0, QB), :] = res[h][0].astype(BF16)
            return 0

        lax.fori_loop(0, nb, qblock, 0)

    spec = _head_spec(T)
    return _pcall_carrying(body, carry, name=name, grid=(H // HEADS_PER_STEP,), in_specs=[spec] * 3, out_specs=spec,
                           out_shape=jax.ShapeDtypeStruct((H, T, HEAD_DIM), BF16))(q, k, v)


def _sb_bwd(q, k, v, do, name, carry=None):
    H, T, _ = q.shape
    nb = T // QB

    def body(q_ref, k_ref, v_ref, do_ref, dq_ref, dk_out, dv_out, racc_ref, dk_ref, dv_ref):
        row, col = _iota2((QB, QB))
        m_gt = (row > col).astype(BF16)
        m_lt = (row < col).astype(BF16)
        dk_ref[...] = jnp.zeros_like(dk_ref)
        dv_ref[...] = jnp.zeros_like(dv_ref)

        def qblock(i, _):
            t0 = pl.multiple_of(i * QB, QB)
            qbs = [q_ref[h, pl.ds(t0, QB), :].astype(BF16) for h in HEADS]
            dobs = [do_ref[h, pl.ds(t0, QB), :].astype(BF16) for h in HEADS]

            def suffix(jj, raccs):
                j = i - jj
                s0 = pl.multiple_of(j * QB, QB)
                out = []
                for h in HEADS:
                    kb = k_ref[h, pl.ds(s0, QB), :].astype(BF16)
                    z = _dot_nt(qbs[h], kb) * SCALE
                    lf = jnp.where((s0 + col) < (t0 + row), _log_sigmoid(z) - z, 0.0)
                    racc_ref[h, j] = raccs[h]
                    out.append(raccs[h] + jnp.sum(lf, axis=1, keepdims=True))
                return tuple(out)

            lax.fori_loop(0, i + 1, suffix, tuple(jnp.zeros((QB, 1), F32) for _ in HEADS))

            def kblock(j, carry):
                s0 = pl.multiple_of(j * QB, QB)
                out = []
                for h in HEADS:
                    dq, cacc = carry[h]
                    kb = k_ref[h, pl.ds(s0, QB), :].astype(BF16)
                    vb = v_ref[h, pl.ds(s0, QB), :].astype(BF16)
                    strict, lb, _, w = _sb_weights(qbs[h], kb, t0, s0, racc_ref[h, j], m_gt, row, col)
                    e = _dot_nt(dobs[h], vb) * w
                    c_left = _dot3(e, m_lt) + cacc
                    sig = jnp.exp(lb)
                    dz = jnp.where(strict, e * (1.0 - sig) - c_left * sig, 0.0).astype(BF16)
                    dq = dq + jnp.dot(dz, kb, preferred_element_type=F32)
                    dk_ref[h, pl.ds(s0, QB), :] += _dot_tn(dz, qbs[h]) * SCALE
                    dv_ref[h, pl.ds(s0, QB), :] += _dot_tn(w.astype(BF16), dobs[h])
                    out.append((dq, cacc + jnp.sum(e, axis=1, keepdims=True)))
                return tuple(out)

            res = lax.fori_loop(0, i + 1, kblock,
                                tuple((jnp.zeros((QB, HEAD_DIM), F32), jnp.zeros((QB, 1), F32)) for _ in HEADS))
            for h in HEADS:
                dq_ref[h, pl.ds(t0, QB), :] = (res[h][0] * SCALE).astype(BF16)
            return 0

        lax.fori_loop(0, nb, qblock, 0)
        dk_out[...] = dk_ref[...].astype(BF16)
        dv_out[...] = dv_ref[...].astype(BF16)

    spec = _head_spec(T)
    acc = pltpu.VMEM((HEADS_PER_STEP, T, HEAD_DIM), F32)
    return _pcall_carrying(body, carry, name=name, grid=(H // HEADS_PER_STEP,), in_specs=[spec] * 4,
                           out_specs=[spec] * 3, out_shape=[jax.ShapeDtypeStruct((H, T, HEAD_DIM), BF16)] * 3,
                           scratch_shapes=[pltpu.VMEM((HEADS_PER_STEP, nb, QB, 1), F32), acc, acc])(q, k, v, do)


def _fox_logits(qb, kb, fq, fk, t0, s0, row, col):
    z = _dot_nt(qb, kb) * SCALE + fq - fk
    return jnp.where((s0 + col) <= (t0 + row), z, NEG)


def _fox_specs(T):
    return _head_spec(T, width=1), _head_spec(T, rows=T // QB, width=QB)


def _fox_fwd(q, k, v, fq, fk, name, carry=None):
    H, T, _ = q.shape
    nb = T // QB

    def body(q_ref, k_ref, v_ref, fq_ref, fk_ref, o_ref, lse_ref):
        row, col = _iota2((QB, QB))

        def qblock(i, _):
            t0 = pl.multiple_of(i * QB, QB)
            qbs = [q_ref[h, pl.ds(t0, QB), :].astype(BF16) for h in HEADS]
            fqs = [fq_ref[h, pl.ds(t0, QB), :] for h in HEADS]

            def kblock(j, carry):
                s0 = pl.multiple_of(j * QB, QB)
                out = []
                for h in HEADS:
                    acc, m, l = carry[h]
                    kb = k_ref[h, pl.ds(s0, QB), :].astype(BF16)
                    vb = v_ref[h, pl.ds(s0, QB), :].astype(BF16)
                    z = _fox_logits(qbs[h], kb, fqs[h], fk_ref[h, pl.ds(j, 1), :], t0, s0, row, col)
                    m_new = jnp.maximum(m, jnp.max(z, axis=1, keepdims=True))
                    a = jnp.exp(m - m_new)
                    p = jnp.exp(z - m_new)
                    acc = a * acc + jnp.dot(p.astype(BF16), vb, preferred_element_type=F32)
                    out.append((acc, m_new, a * l + jnp.sum(p, axis=1, keepdims=True)))
                return tuple(out)

            res = lax.fori_loop(0, i + 1, kblock,
                                tuple((jnp.zeros((QB, HEAD_DIM), F32), jnp.full((QB, 1), NEG, F32),
                                       jnp.zeros((QB, 1), F32)) for _ in HEADS))
            for h in HEADS:
                acc, m, l = res[h]
                o_ref[h, pl.ds(t0, QB), :] = (acc / l).astype(BF16)
                lse_ref[h, pl.ds(t0, QB), :] = m + jnp.log(l)
            return 0

        lax.fori_loop(0, nb, qblock, 0)

    spec = _head_spec(T)
    fq_spec, fk_spec = _fox_specs(T)
    return _pcall_carrying(
        body, carry, name=name, grid=(H // HEADS_PER_STEP,), in_specs=[spec] * 3 + [fq_spec, fk_spec],
        out_specs=[spec, fq_spec],
        out_shape=[jax.ShapeDtypeStruct((H, T, HEAD_DIM), BF16), jax.ShapeDtypeStruct((H, T, 1), F32)],
    )(q, k, v, fq, fk)


def _fox_bwd(q, k, v, fq, fk, lse, do, name, carry=None):
    H, T, _ = q.shape
    nb = T // QB

    def body(q_ref, k_ref, v_ref, fq_ref, fk_ref, lse_ref, do_ref, dq_ref, dk_out, dv_out, dfk_ref, dk_ref, dv_ref):
        row, col = _iota2((QB, QB))
        dk_ref[...] = jnp.zeros_like(dk_ref)
        dv_ref[...] = jnp.zeros_like(dv_ref)
        dfk_ref[...] = jnp.zeros_like(dfk_ref)

        def qblock(i, _):
            t0 = pl.multiple_of(i * QB, QB)
            qbs = [q_ref[h, pl.ds(t0, QB), :].astype(BF16) for h in HEADS]
            fqs = [fq_ref[h, pl.ds(t0, QB), :] for h in HEADS]
            lses = [lse_ref[h, pl.ds(t0, QB), :] for h in HEADS]
            dobs = [do_ref[h, pl.ds(t0, QB), :].astype(BF16) for h in HEADS]

            def probs(h, j):
                s0 = pl.multiple_of(j * QB, QB)
                kb = k_ref[h, pl.ds(s0, QB), :].astype(BF16)
                vb = v_ref[h, pl.ds(s0, QB), :].astype(BF16)
                z = _fox_logits(qbs[h], kb, fqs[h], fk_ref[h, pl.ds(j, 1), :], t0, s0, row, col)
                return s0, kb, jnp.exp(z - lses[h]), _dot_nt(dobs[h], vb)

            def row_dot(j, deltas):
                out = []
                for h in HEADS:
                    _, _, p, dp = probs(h, j)
                    out.append(deltas[h] + jnp.sum(p * dp, axis=1, keepdims=True))
                return tuple(out)

            deltas = lax.fori_loop(0, i + 1, row_dot, tuple(jnp.zeros((QB, 1), F32) for _ in HEADS))

            def kblock(j, dqs):
                out = []
                for h in HEADS:
                    s0, kb, p, dp = probs(h, j)
                    dz = p * (dp - deltas[h])
                    dzb = dz.astype(BF16)
                    dk_ref[h, pl.ds(s0, QB), :] += _dot_tn(dzb, qbs[h]) * SCALE
                    dv_ref[h, pl.ds(s0, QB), :] += _dot_tn(p.astype(BF16), dobs[h])
                    dfk_ref[h, pl.ds(j, 1), :] += -jnp.sum(dz, axis=0, keepdims=True)
                    out.append(dqs[h] + jnp.dot(dzb, kb, preferred_element_type=F32))
                return tuple(out)

            dqs = lax.fori_loop(0, i + 1, kblock, tuple(jnp.zeros((QB, HEAD_DIM), F32) for _ in HEADS))
            for h in HEADS:
                dq_ref[h, pl.ds(t0, QB), :] = (dqs[h] * SCALE).astype(BF16)
            return 0

        lax.fori_loop(0, nb, qblock, 0)
        dk_out[...] = dk_ref[...].astype(BF16)
        dv_out[...] = dv_ref[...].astype(BF16)

    spec = _head_spec(T)
    fq_spec, fk_spec = _fox_specs(T)
    acc = pltpu.VMEM((HEADS_PER_STEP, T, HEAD_DIM), F32)
    return _pcall_carrying(body, carry, name=name, grid=(H // HEADS_PER_STEP,),
                           in_specs=[spec] * 3 + [fq_spec, fk_spec, fq_spec, spec],
                           out_specs=[spec, spec, spec, fk_spec],
                           out_shape=[jax.ShapeDtypeStruct((H, T, HEAD_DIM), BF16)] * 3
                           + [jax.ShapeDtypeStruct((H, nb, QB), F32)],
                           scratch_shapes=[acc, acc])(q, k, v, fq, fk, lse, do)


def _forget_cumsum(flog, name):
    R, T = flog.shape
    nb = T // QB

    def body(x_ref, o_ref):
        row, col = _iota2((QB, QB))
        m_le = (row <= col).astype(BF16)
        carry = jnp.zeros((R, 1), F32)
        for b in range(nb):
            lf = _log_sigmoid(x_ref[:, b * QB:(b + 1) * QB])
            o_ref[:, b * QB:(b + 1) * QB] = _dot3(lf, m_le) + carry
            carry = carry + jnp.sum(lf, axis=1, keepdims=True)

    spec = pl.BlockSpec((R, T), lambda: (0, 0))
    return _pcall(body, name=name, in_specs=[spec], out_specs=spec,
                  out_shape=jax.ShapeDtypeStruct((R, T), F32))(flog)


def _forget_cumsum_bwd(flog, df, name):
    R, T = flog.shape
    nb = T // QB

    def body(x_ref, df_ref, o_ref):
        row, col = _iota2((QB, QB))
        m_ge = (row >= col).astype(BF16)
        carry = jnp.zeros((R, 1), F32)
        for b in reversed(range(nb)):
            dfb = df_ref[:, b * QB:(b + 1) * QB]
            dlog = _dot3(dfb, m_ge) + carry
            o_ref[:, b * QB:(b + 1) * QB] = dlog * _sigmoid(-x_ref[:, b * QB:(b + 1) * QB])
            carry = carry + jnp.sum(dfb, axis=1, keepdims=True)

    spec = pl.BlockSpec((R, T), lambda: (0, 0))
    return _pcall(body, name=name, in_specs=[spec, spec], out_specs=spec,
                  out_shape=jax.ShapeDtypeStruct((R, T), F32))(flog, df)


def _chunk_probs(qc, kb, bias, c, col):
    z = _dot_nt(qc, kb) * SCALE + bias
    z = jnp.where((c - (LEFT_CHUNKS + 1)) * CHUNK + col >= jnp.maximum(0, (c - LEFT_CHUNKS) * CHUNK), z, NEG)
    p = jnp.exp(z - jnp.max(z, axis=1, keepdims=True))
    return p, jnp.sum(p, axis=1, keepdims=True)


def _chunk_specs(T, n=CHUNK_HEADS_PER_STEP):
    return (_head_spec(T, heads=n), _head_spec(T, rows=T + BAND_PAD, heads=n),
            _head_spec(T, rows=CHUNK, width=BAND, heads=n))


def _chunk_fwd(q, kp, vp, bias, name, carry=None):
    H, T, _ = q.shape
    nc = T // CHUNK

    def body(q_ref, kp_ref, vp_ref, bias_ref, o_ref):
        _, col = _iota2((CHUNK, BAND))

        def chunk(c, _):
            t0 = pl.multiple_of(c * CHUNK, CHUNK)
            for h in HEADS:
                qc = q_ref[h, pl.ds(t0, CHUNK), :].astype(BF16)
                kb = kp_ref[h, pl.ds(t0, BAND), :].astype(BF16)
                vb = vp_ref[h, pl.ds(t0, BAND), :].astype(BF16)
                p, l = _chunk_probs(qc, kb, bias_ref[h], c, col)
                o = jnp.dot(p.astype(BF16), vb, preferred_element_type=F32) / l
                o_ref[h, pl.ds(t0, CHUNK), :] = o.astype(BF16)
            return 0

        lax.fori_loop(0, nc, chunk, 0)

    q_spec, kp_spec, b_spec = _chunk_specs(T, HEADS_PER_STEP)
    return _pcall_carrying(body, carry, name=name, grid=(H // HEADS_PER_STEP,),
                           in_specs=[q_spec, kp_spec, kp_spec, b_spec], out_specs=q_spec,
                           out_shape=jax.ShapeDtypeStruct((H, T, HEAD_DIM), BF16))(q, kp, vp, bias)


def _chunk_bwd(q, kp, vp, bias, do, name, carry=None):
    H, T, _ = q.shape
    nc = T // CHUNK

    def body(q_ref, kp_ref, vp_ref, bias_ref, do_ref, dq_ref, dk_out, dv_out, db_ref, dkp_ref, dvp_ref):
        _, col = _iota2((CHUNK, BAND))
        dkp_ref[...] = jnp.zeros_like(dkp_ref)
        dvp_ref[...] = jnp.zeros_like(dvp_ref)
        db_ref[...] = jnp.zeros_like(db_ref)

        def chunk(c, _):
            t0 = pl.multiple_of(c * CHUNK, CHUNK)
            for h in CHUNK_HEADS:
                qc = q_ref[h, pl.ds(t0, CHUNK), :].astype(BF16)
                kb = kp_ref[h, pl.ds(t0, BAND), :].astype(BF16)
                vb = vp_ref[h, pl.ds(t0, BAND), :].astype(BF16)
                dob = do_ref[h, pl.ds(t0, CHUNK), :].astype(BF16)
                p, l = _chunk_probs(qc, kb, bias_ref[h], c, col)
                p = p / l
                dp = _dot_nt(dob, vb)
                dz = p * (dp - jnp.sum(p * dp, axis=1, keepdims=True))
                dzb = dz.astype(BF16)
                dq = jnp.dot(dzb, kb, preferred_element_type=F32) * SCALE
                dq_ref[h, pl.ds(t0, CHUNK), :] = dq.astype(BF16)
                dkp_ref[h, pl.ds(t0, BAND), :] += _dot_tn(dzb, qc) * SCALE
                dvp_ref[h, pl.ds(t0, BAND), :] += _dot_tn(p.astype(BF16), dob)
                db_ref[h] += dz
            return 0

        lax.fori_loop(0, nc, chunk, 0)
        dk_out[...] = dkp_ref[:, BAND_PAD:, :].astype(BF16)
        dv_out[...] = dvp_ref[:, BAND_PAD:, :].astype(BF16)

    q_spec, kp_spec, b_spec = _chunk_specs(T)
    acc = pltpu.VMEM((CHUNK_HEADS_PER_STEP, T + BAND_PAD, HEAD_DIM), F32)
    return _pcall_carrying(body, carry, name=name, grid=(H // CHUNK_HEADS_PER_STEP,),
                           in_specs=[q_spec, kp_spec, kp_spec, b_spec, q_spec],
                           out_specs=[q_spec, q_spec, q_spec, b_spec],
                           out_shape=[jax.ShapeDtypeStruct((H, T, HEAD_DIM), BF16)] * 3
                           + [jax.ShapeDtypeStruct((H, CHUNK, BAND), F32)],
                           scratch_shapes=[acc, acc])(q, kp, vp, bias, do)


HBM_SPEC = pl.BlockSpec(memory_space=pltpu.HBM)


def _position():
    return lax.axis_index("x"), lax.axis_index("y"), lax.axis_index("c")


def _other_chips(x, y):
    chips = [(1 - x, y), (x, 1 - y), (1 - x, 1 - y)]
    return [(chip, 2 * chip[0] + chip[1]) for chip in chips]


def _remote_copy(src, dst, send_sems, recv_sems, k, to):
    return pltpu.make_async_remote_copy(src_ref=src, dst_ref=dst, send_sem=send_sems.at[k], recv_sem=recv_sems.at[k],
                                        device_id=to, device_id_type=MESH)


def _place_own(w, chip, name):
    _, r, c = w.shape
    tr = _pick(r, 256, 16)

    def body(chip_ref, w_ref, *out_refs):
        for l, o_ref in enumerate(out_refs):
            o_ref[...] = w_ref[l].astype(BF16)

    grid_spec = pltpu.PrefetchScalarGridSpec(
        num_scalar_prefetch=1, grid=(r // tr,), in_specs=[pl.BlockSpec((DEPTH, tr, c), lambda i, ch: (0, i, 0))],
        out_specs=[pl.BlockSpec((None, tr, c), lambda i, ch: (ch[0], i, 0))] * DEPTH)
    return _pcall(body, name=name, grid_spec=grid_spec,
                  out_shape=[jax.ShapeDtypeStruct((N_CHIPS, r, c), BF16)] * DEPTH)(chip, w)


def _gather_over_ici(srcs, bufs, new):
    x, y, c = _position()
    me = 2 * x + y
    return [(b.at[me, c], b.at[me, c], (*chip, c)) for b in bufs for chip, _ in _other_chips(x, y)]


def _gather_to_sibling(srcs, bufs, new):
    x, y, c = _position()
    return [(b.at[idx, c], b.at[idx, c], (x, y, 1 - c)) for b in bufs for _, idx in _other_chips(x, y)]


def _gather_layer(bufs, name):
    n = len(bufs)

    def body(*refs):
        dst = refs[n:2 * n]
        send_sems, recv_sems = refs[2 * n:]
        x, y, c = _position()
        me = 2 * x + y
        sibling = (x, y, 1 - c)
        others = _other_chips(x, y)
        first = [_remote_copy(dst[i].at[me, c], dst[i].at[me, c], send_sems, recv_sems, 3 * i + k, (*chip, c))
                 for i in range(n) for k, (chip, _) in enumerate(others)]
        for cp in first:
            cp.start()
        passed = []
        for i in range(n):
            for k, (_, idx) in enumerate(others):
                landed = dst[i].at[idx, c]
                _remote_copy(landed, landed, send_sems, recv_sems, 3 * i + k, sibling).wait_recv()
                passed.append(_remote_copy(landed, landed, send_sems, recv_sems, 3 * (n + i) + k, sibling))
                passed[-1].start()
        for i in range(n):
            for k, (_, idx) in enumerate(others):
                from_sibling = dst[i].at[idx, 1 - c]
                _remote_copy(from_sibling, from_sibling, send_sems, recv_sems, 3 * (n + i) + k, sibling).wait_recv()
        for cp in first + passed:
            cp.wait_send()

    return _pcall(
        body, name=name, in_specs=[HBM_SPEC] * n, out_specs=[HBM_SPEC] * n,
        out_shape=[jax.ShapeDtypeStruct(b.shape, b.dtype) for b in bufs],
        scratch_shapes=[pltpu.SemaphoreType.DMA((6 * n,)), pltpu.SemaphoreType.DMA((6 * n,))],
        input_output_aliases={i: i for i in range(n)},
    )(*bufs)


def _send_other_half(parts, name):
    n = len(parts)

    def body(*refs):
        src, got = refs[:n], refs[n:2 * n]
        send_sems, recv_sems = refs[2 * n:]
        x, y, c = _position()
        copies = [_remote_copy(src[i].at[1 - c], got[i], send_sems, recv_sems, i, (x, y, 1 - c)) for i in range(n)]
        for cp in copies:
            cp.start()
        for cp in copies:
            cp.wait()

    return _pcall(
        body, name=name, in_specs=[HBM_SPEC] * n, out_specs=[HBM_SPEC] * n,
        out_shape=[jax.ShapeDtypeStruct(p.shape[1:], p.dtype) for p in parts],
        scratch_shapes=[pltpu.SemaphoreType.DMA((n,)), pltpu.SemaphoreType.DMA((n,))],
    )(*parts)


def _scatter_chips(parts, name):
    n = len(parts)

    def body(*refs):
        src, dst = refs[:n], refs[n:2 * n]
        send_sems, recv_sems = refs[2 * n:]
        x, y, c = _position()
        others = _other_chips(x, y)
        sends = [_remote_copy(src[i].at[idx], dst[i].at[k], send_sems, recv_sems, 3 * i + k, (*chip, c))
                 for i in range(n) for k, (chip, idx) in enumerate(others)]
        for cp in sends:
            cp.start()
        for cp in sends:
            cp.wait()

    return _pcall(
        body, name=name, in_specs=[HBM_SPEC] * n, out_specs=[HBM_SPEC] * n,
        out_shape=[jax.ShapeDtypeStruct((3,) + p.shape[1:], p.dtype) for p in parts],
        scratch_shapes=[pltpu.SemaphoreType.DMA((3 * n,)), pltpu.SemaphoreType.DMA((3 * n,))],
    )(*parts)


def _swap_with_sibling(arrs, name):
    n = len(arrs)

    def body(*refs):
        src, dst = refs[:n], refs[n:2 * n]
        send_sems, recv_sems = refs[2 * n:]
        x, y, c = _position()
        copies = [_remote_copy(src[i], dst[i], send_sems, recv_sems, i, (x, y, 1 - c)) for i in range(n)]
        for cp in copies:
            cp.start()
        for cp in copies:
            cp.wait()

    return _pcall(
        body, name=name, in_specs=[HBM_SPEC] * n, out_specs=[HBM_SPEC] * n,
        out_shape=[jax.ShapeDtypeStruct(a.shape, a.dtype) for a in arrs],
        scratch_shapes=[pltpu.SemaphoreType.DMA((n,)), pltpu.SemaphoreType.DMA((n,))],
    )(*arrs)


def _allreduce_small(v, name):
    R, C = v.shape

    def body(v_ref, o_ref, slots, send_sems, recv_sems):
        x, y, c = _position()
        me = 4 * x + 2 * y + c
        slots[0] = v_ref[...]
        sends = []
        for r in range(1, 8):
            fx, fy, fc = (r >> 2) & 1, (r >> 1) & 1, r & 1
            to = (x ^ fx, y ^ fy, c ^ fc)
            cp = pltpu.make_async_remote_copy(src_ref=v_ref, dst_ref=slots.at[r], send_sem=send_sems.at[r - 1],
                                              recv_sem=recv_sems.at[r - 1], device_id=to, device_id_type=MESH)
            cp.start()
            sends.append(cp)
        for cp in sends:
            cp.wait()
        acc = slots[me]
        for d in range(1, 8):
            acc = acc + slots[d ^ me]
        o_ref[...] = acc

    vmem = pl.BlockSpec(memory_space=pltpu.VMEM)
    return _pcall(
        body, name=name, in_specs=[vmem], out_specs=vmem, out_shape=jax.ShapeDtypeStruct((R, C), F32),
        scratch_shapes=[pltpu.VMEM((8, R, C), F32), pltpu.SemaphoreType.DMA((7,)), pltpu.SemaphoreType.DMA((7,))],
    )(v)


def _add_pair(which, both, got, name):
    _, N, R, C = both.shape
    tr = _pick(R, 512, 16)

    def body(which_ref, a_ref, b_ref, o_ref):
        o_ref[...] = (a_ref[...].astype(F32) + b_ref[...].astype(F32)).astype(BF16)

    spec = pl.BlockSpec((None, tr, C), lambda n, i, w: (n, i, 0))
    grid_spec = pltpu.PrefetchScalarGridSpec(
        num_scalar_prefetch=1, grid=(N, R // tr),
        in_specs=[pl.BlockSpec((None, None, tr, C), lambda n, i, w: (w[0], n, i, 0)), spec], out_specs=spec)
    return _pcall(body, name=name, grid_spec=grid_spec,
                  out_shape=jax.ShapeDtypeStruct((N, R, C), BF16))(which, both, got)


def _sum_chips(which, own, others, name):
    N, R, C = others.shape
    tr = _pick(R, 512, 16)

    def body(which_ref, own_ref, p_ref, o_ref):
        acc = own_ref[...].astype(F32)
        for n in range(N):
            acc = acc + p_ref[n].astype(F32)
        o_ref[...] = acc

    grid_spec = pltpu.PrefetchScalarGridSpec(
        num_scalar_prefetch=1, grid=(R // tr,),
        in_specs=[pl.BlockSpec((None, tr, C), lambda i, w: (w[0], i, 0)), pl.BlockSpec((N, tr, C), lambda i, w: (0, i, 0))],
        out_specs=pl.BlockSpec((tr, C), lambda i, w: (i, 0)))
    return _pcall(body, name=name, grid_spec=grid_spec,
                  out_shape=jax.ShapeDtypeStruct((R, C), F32))(which, own, others)


BIG = ("w_ffn1_in", "w_ffn1_out", "w_in", "w_br_sb", "w_br_ch", "w_br_fox", "w_out", "w_ffn2_in", "w_ffn2_out")
ROW_SHARDED = ("w_ffn1_out", "w_out", "w_ffn2_out")
SMALL = ("g_ffn1", "g_mix", "b_in", "rel_bias", "g_ffn2", "g_final")


def _pair_sums(split, tag):
    core = lax.axis_index("c").astype(jnp.int32)[None]
    got = _send_other_half(split, f"grad_split_{tag}")
    return [_add_pair(core, a, b, f"grad_pair_sum_{tag}_{i}") for i, (a, b) in enumerate(zip(split, got))]


def _scatter_plan(srcs, bufs, new):
    x, y, c = _position()
    return [(s.at[idx], d.at[k], (*chip, c)) for s, d in zip(srcs, new) for k, (chip, idx) in enumerate(_other_chips(x, y))]


def _scatter_carry(pair):
    landing = [jax.ShapeDtypeStruct((3,) + p.shape[1:], p.dtype) for p in pair]
    return _Carry(pair, [], landing, 3 * len(pair), _scatter_plan)


def _finish_grads(pair, landed, axes, tag):
    chip = (2 * lax.axis_index("x") + lax.axis_index("y")).astype(jnp.int32)[None]
    mine = [_sum_chips(chip, p, q, f"grad_chip_sum_{tag}_{i}") for i, (p, q) in enumerate(zip(pair, landed))]
    theirs = _swap_with_sibling(mine, f"grad_share_{tag}")
    first = lax.axis_index("c") == 0
    return [jnp.concatenate([jnp.where(first, a, b), jnp.where(first, b, a)], axis=ax)
            for a, b, ax in zip(mine, theirs, axes)]


SMALL_SIZES = {"g_ffn1": DEPTH * D_MODEL, "g_mix": DEPTH * D_MODEL, "b_in": DEPTH * IN_WIDTH,
               "rel_bias": DEPTH * N_REL * H_CH, "g_ffn2": DEPTH * D_MODEL, "g_final": D_MODEL}
SMALL_TOTAL = sum(SMALL_SIZES.values()) + 1
SMALL_ROWS = -(-SMALL_TOTAL // (8 * 128)) * 8


def _pack_small(vals, extra):
    flat = [vals[n].reshape(-1) for n in SMALL] + [extra.reshape(-1)]
    flat.append(jnp.zeros((SMALL_ROWS * 128 - SMALL_TOTAL,), F32))
    return jnp.concatenate(flat).reshape(SMALL_ROWS, 128)


def _unpack_small(pack, shapes):
    flat, out, off = pack.reshape(-1), {}, 0
    for n in SMALL:
        out[n] = flat[off:off + SMALL_SIZES[n]].reshape(shapes[n])
        off += SMALL_SIZES[n]
    return out, flat[off]


def _to_heads(t, n_heads):
    return jnp.transpose(t.reshape(t.shape[0], n_heads, HEAD_DIM), (1, 0, 2)).astype(BF16)


def _from_heads(t):
    return jnp.transpose(t, (1, 0, 2)).reshape(t.shape[1], t.shape[0] * HEAD_DIM)


def _pad_keys(t):
    return jnp.pad(t, ((0, 0), (BAND_PAD, 0), (0, 0)))


DIAGONALS = CHUNK + BAND - 1


def _band_bias(table):
    n_far = (LEFT_CHUNKS + 1) * CHUNK + CHUNK - MAX_REL
    near = table[N_REL - 1 - (DIAGONALS - n_far):N_REL - 1][::-1]
    diag = jnp.concatenate([jnp.broadcast_to(table[N_REL - 1], (n_far, H_CH)), near], axis=0).T
    diag = jnp.pad(diag, ((0, 0), (0, 1)))
    skew = jnp.tile(diag, (1, CHUNK))[:, :CHUNK * DIAGONALS].reshape(H_CH, CHUNK, DIAGONALS)
    return skew[:, :, CHUNK - 1:]


def _pad_w_in(w):
    qkv, f, gates = w[:, :QKV_WIDTH], w[:, QKV_WIDTH:QKV_WIDTH + H_FOX], w[:, QKV_WIDTH + H_FOX:]
    return jnp.concatenate([qkv, gates, f, jnp.zeros((w.shape[0], F_PAD - H_FOX), w.dtype)], axis=1)


def _unpad_w_in(w):
    return jnp.concatenate([w[:, :QKV_WIDTH], w[:, QKV_WIDTH + GATE_WIDTH:QKV_WIDTH + GATE_WIDTH + H_FOX],
                            w[:, QKV_WIDTH:QKV_WIDTH + GATE_WIDTH]], axis=1)


QKV_SPLITS = np.cumsum([0, W_SB, W_SB, W_SB, W_CH, W_CH, W_CH, W_FOX, W_FOX, W_FOX])


def _by_chip_rows(g):
    return g.reshape(2, N_CHIPS, g.shape[1] // N_CHIPS, g.shape[2])


def _ffn_fwd(x, g, w_in, w_out, tag, carries):
    h, gate, up, a = _ffn_in_swiglu(x, g, w_in, f"{tag}_in", _carry_of(carries, "in"))
    y = _mm(a, w_out, mode="nn", name=f"{tag}_out", alpha=0.5, res=x, gathered="row",
            carry=_carry_of(carries, "out"))
    return y, (h, gate, up, a)


def _ffn_bwd(x, g, w_in, w_out, saved, dy, tag, carries):
    h, gate, up, a = saved
    da = _mm(dy, w_out, mode="nt", name=f"{tag}_da", alpha=0.5, gathered="row", out_dtype=BF16,
             carry=_carry_of(carries, "da"))
    dw_out = _mm(a, dy, mode="tn", name=f"{tag}_dwout", alpha=0.5, tm_cap=1408, out_dtype=BF16, out_split="row",
                 carry=_carry_of(carries, "dwout"))
    dgu = _swiglu_bwd(gate, up, da, f"{tag}_dact")
    dw_in = _mm(h, dgu, mode="tn", name=f"{tag}_dwin", tm_cap=512, out_dtype=BF16, out_split="col",
                carry=_carry_of(carries, "dwin"))
    dx, dg = _mm(dgu, w_in, mode="nt", name=f"{tag}_dh", gathered="col", tn_cap=1024, carry=_carry_of(carries, "dh"),
                 norm_bwd=(x, g, dy))
    return dx, dg, dw_in, _by_chip_rows(dw_out)


def _mixer_fwd(x, lw, tag, carries):
    T = x.shape[0]
    h = _rmsnorm_fwd(x, lw["g_mix"], f"{tag}_norm")
    p = _mm(h, lw["w_in"], mode="nn", name=f"{tag}_proj", bias=lw["b_in"], tn_cap=896,
            carry=_carry_of(carries, "proj"))
    parts = [p[:, QKV_SPLITS[i]:QKV_SPLITS[i + 1]] for i in range(9)]
    qa, ka, va = (_to_heads(t, H_SB) for t in parts[0:3])
    qb, kb, vb = (_to_heads(t, H_CH) for t in parts[3:6])
    qc, kc, vc = (_to_heads(t, H_FOX) for t in parts[6:9])
    flog = _mm(lw["w_forget_t"], h, mode="nt", name=f"{tag}_flog")[:8] + lw["b_forget"]
    fcum = _forget_cumsum(flog, f"{tag}_fcum")[:H_FOX]
    fq, fk = fcum.reshape(H_FOX, T, 1), fcum.reshape(H_FOX, T // QB, QB)
    kbp, vbp = _pad_keys(kb), _pad_keys(vb)
    oa = _sb_fwd(qa, ka, va, f"{tag}_sb", _carry_of(carries, "sb"))
    ob = _chunk_fwd(qb, kbp, vbp, lw["bias"], f"{tag}_ch", _carry_of(carries, "ch"))
    oc, lse = _fox_fwd(qc, kc, vc, fq, fk, f"{tag}_fox", _carry_of(carries, "fox"))
    oam, obm, ocm = _from_heads(oa), _from_heads(ob), _from_heads(oc)
    ya, yb, yc, merged = _branches_fwd(p, (oam, obm, ocm), (lw["w_br_sb"], lw["w_br_ch"], lw["w_br_fox"]),
                                       f"{tag}_branches")
    y = _mm(merged, lw["w_out"], mode="nn", name=f"{tag}_out", res=x, gathered="row")
    saved = (h, p, (qa, ka, va), (qb, kbp, vbp), (qc, kc, vc), flog, fq, fk, lse, (oam, obm, ocm),
             (ya, yb, yc), merged)
    return y, saved


def _mixer_bwd(x, lw, saved, dy, tag, carries):
    (h, p, (qa, ka, va), (qb, kbp, vbp), (qc, kc, vc), flog, fq, fk, lse, (oam, obm, ocm),
     (ya, yb, yc), merged) = saved
    T = x.shape[0]
    grads = {}
    dmerged = _mm(dy, lw["w_out"], mode="nt", name=f"{tag}_dmerged", gathered="row",
                  carry=_carry_of(carries, "dmerged"))
    grads["w_out"] = _by_chip_rows(_mm(merged, dy, mode="tn", name=f"{tag}_dwout", tm_cap=1024, out_dtype=BF16,
                                       out_split="row"))
    (dya, dyb, dyc), dgates, (doa, dob, doc) = _branches_bwd(
        p, (ya, yb, yc), dmerged, (lw["w_br_sb"], lw["w_br_ch"], lw["w_br_fox"]), f"{tag}_dbranches")
    by_chip = dict(mode="tn", tm_cap=512, out_dtype=BF16, out_split="col")
    grads["w_br_sb"] = _mm(oam, dya, name=f"{tag}_dwbra", **by_chip)
    grads["w_br_ch"] = _mm(obm, dyb, name=f"{tag}_dwbrb", **by_chip)
    grads["w_br_fox"] = _mm(ocm, dyc, name=f"{tag}_dwbrc", **by_chip)
    dqa, dka, dva = _sb_bwd(qa, ka, va, _to_heads(doa, H_SB), f"{tag}_dsb", _carry_of(carries, "dsb"))
    dqb, dkb, dvb, dbias = _chunk_bwd(qb, kbp, vbp, lw["bias"], _to_heads(dob, H_CH), f"{tag}_dch",
                                      _carry_of(carries, "dch"))
    dqc, dkc, dvc, dfk = _fox_bwd(qc, kc, vc, fq, fk, lse, _to_heads(doc, H_FOX), f"{tag}_dfox",
                                  _carry_of(carries, "dfox"))
    dflog = _forget_cumsum_bwd(flog, jnp.pad(dfk.reshape(H_FOX, T), ((0, 8 - H_FOX), (0, 0))), f"{tag}_dfcum")
    dqkv = [_from_heads(t) for t in (dqa, dka, dva, dqb, dkb, dvb, dqc, dkc, dvc)]
    dforget = jnp.pad(dflog[:H_FOX].T, ((0, 0), (0, F_PAD - H_FOX))).astype(BF16)
    dpb = jnp.concatenate(dqkv + list(dgates) + [dforget], axis=1)
    grads["b_in"] = _colsum(dpb, f"{tag}_dbin")
    dw_in =_unpad_w_in(_mm(h, dpb, mode="tn", name=f"{tag}_dwin", tm_cap=512, tn_cap=896, out_dtype=BF16))
    grads["w_in"] = jnp.transpose(dw_in.reshape(2, D_MODEL // 2, N_CHIPS, IN_WIDTH // N_CHIPS), (0, 2, 1, 3))
    dx, grads["g_mix"] = _mm(dpb, lw["w_in"], mode="nt", name=f"{tag}_dh", tk_cap=896, tn_cap=1024,
                             norm_bwd=(x, lw["g_mix"], dy))
    grads["rel_bias"] = lw["bias_vjp"](dbias)[0]
    return dx, grads


def kernel(x, g_ffn1, w_ffn1_in, w_ffn1_out, g_mix, w_in, b_in, rel_bias, w_br_sb, w_br_ch, w_br_fox, w_out, g_ffn2, w_ffn2_in, w_ffn2_out, g_final, loss_target, m_g_ffn1, m_w_ffn1_in, m_w_ffn1_out, m_g_mix, m_w_in, m_b_in, m_rel_bias, m_w_br_sb, m_w_br_ch, m_w_br_fox, m_w_out, m_g_ffn2, m_w_ffn2_in, m_w_ffn2_out, m_g_final, v_g_ffn1, v_w_ffn1_in, v_w_ffn1_out, v_g_mix, v_w_in, v_b_in, v_rel_bias, v_w_br_sb, v_w_br_ch, v_w_br_fox, v_w_out, v_g_ffn2, v_w_ffn2_in, v_w_ffn2_out, v_g_final):
    names = ("g_ffn1", "w_ffn1_in", "w_ffn1_out", "g_mix", "w_in", "b_in", "rel_bias", "w_br_sb", "w_br_ch",
             "w_br_fox", "w_out", "g_ffn2", "w_ffn2_in", "w_ffn2_out", "g_final")
    w = dict(zip(names, (g_ffn1, w_ffn1_in, w_ffn1_out, g_mix, w_in, b_in, rel_bias, w_br_sb, w_br_ch,
                         w_br_fox, w_out, g_ffn2, w_ffn2_in, w_ffn2_out, g_final)))
    m = dict(zip(names, (m_g_ffn1, m_w_ffn1_in, m_w_ffn1_out, m_g_mix, m_w_in, m_b_in, m_rel_bias, m_w_br_sb,
                         m_w_br_ch, m_w_br_fox, m_w_out, m_g_ffn2, m_w_ffn2_in, m_w_ffn2_out, m_g_final)))
    v = dict(zip(names, (v_g_ffn1, v_w_ffn1_in, v_w_ffn1_out, v_g_mix, v_w_in, v_b_in, v_rel_bias, v_w_br_sb,
                         v_w_br_ch, v_w_br_fox, v_w_out, v_g_ffn2, v_w_ffn2_in, v_w_ffn2_out, v_g_final)))
    xs = x[0]
    tgt = loss_target[0]

    chip = (2 * lax.axis_index("x") + lax.axis_index("y")).astype(jnp.int32)[None]
    placed = [_place_own(w[n], chip, f"place_{n}") for n in BIG]
    bufs = [[p[l].reshape(N_CHIPS, 2, p[l].shape[1] // 2, p[l].shape[2]) for p in placed] for l in range(DEPTH)]
    padded_w_in = {}

    def layer_weights(l):
        lw = {n: b.reshape((N_CHIPS,) + w[n].shape[1:]) for n, b in zip(BIG, bufs[l])}
        if l not in padded_w_in:
            whole = jnp.concatenate([lw["w_in"][j] for j in range(N_CHIPS)], axis=1)
            forget_t = jnp.pad(whole[:, QKV_WIDTH:QKV_WIDTH + H_FOX].T, ((0, F_PAD - H_FOX), (0, 0)))
            padded_w_in[l] = (_pad_w_in(whole), forget_t)
        lw["w_in"], lw["w_forget_t"] = padded_w_in[l]
        lw["b_forget"] = jnp.pad(w["b_in"][l][QKV_WIDTH:QKV_WIDTH + H_FOX], (0, 8 - H_FOX))[:, None]
        lw["b_in"] = _pad_w_in(w["b_in"][l][None, :])
        lw["bias"], lw["bias_vjp"] = jax.vjp(_band_bias, w["rel_bias"][l])
        for n in ("g_ffn1", "g_mix", "g_ffn2"):
            lw[n] = w[n][l][None, :]
        return lw

    def landed_in(l, idx):
        def done(carry):
            for i, b in zip(idx, carry.out[0]):
                bufs[l][i] = b
        return done

    def over_ici(l, idx):
        return lambda: _Carry([], [bufs[l][i] for i in idx], [], 3 * len(idx), _gather_over_ici, landed_in(l, idx))

    def to_sibling(l, idx):
        return lambda: _Carry([], [bufs[l][i] for i in idx], [], 3 * len(idx), _gather_to_sibling, landed_in(l, idx))

    def ffn_fwd(x_in, lw, which, tag, carries):
        return _ffn_fwd(x_in, lw[f"g_{which}"], lw[f"w_{which}_in"], lw[f"w_{which}_out"], tag, carries)

    def ffn_bwd(x_in, lw, which, saved_acts, dy, tag, carries):
        return _ffn_bwd(x_in, lw[f"g_{which}"], lw[f"w_{which}_in"], lw[f"w_{which}_out"], saved_acts, dy, tag,
                        carries)

    FFN1, W_IN, MIXER_REST, FFN2_IN, FFN2_OUT = (0, 1), (2,), (3, 4, 5, 6), (7,), (8,)
    first = FFN1 + W_IN
    for i, b in zip(first, _gather_layer([bufs[0][i] for i in first], "gather_l0")):
        bufs[0][i] = b
    fwd_carries = [
        {"ffn1": {"in": [over_ici(0, MIXER_REST)], "out": [to_sibling(0, MIXER_REST), over_ici(0, FFN2_OUT)]},
         "mix": {"proj": [to_sibling(0, FFN2_OUT), over_ici(1, MIXER_REST)],
                 "sb": [over_ici(0, FFN2_IN), over_ici(1, FFN2_OUT)],
                 "ch": [to_sibling(0, FFN2_IN), over_ici(1, FFN1)],
                 "fox": [over_ici(1, W_IN)]},
         "ffn2": {"in": [to_sibling(1, MIXER_REST + FFN2_OUT + FFN1 + W_IN)]}},
        {"ffn1": {}, "mix": {"sb": [over_ici(1, FFN2_IN)], "ch": [to_sibling(1, FFN2_IN)]}, "ffn2": {}},
    ]

    saved = []
    act = xs
    for l in range(DEPTH):
        x0 = act
        x1, s1 = ffn_fwd(x0, layer_weights(l), "ffn1", f"l{l}_ffn1", fwd_carries[l]["ffn1"])
        x2, s2 = _mixer_fwd(x1, layer_weights(l), f"l{l}_mix", fwd_carries[l]["mix"])
        act, s3 = ffn_fwd(x2, layer_weights(l), "ffn2", f"l{l}_ffn2", fwd_carries[l]["ffn2"])
        saved.append((x0, s1, x1, s2, x2, s3))
    layers = [layer_weights(l) for l in range(DEPTH)]

    dact, dg_final, loss_row = _final_loss(act, w["g_final"][None, :], tgt, "final_loss")

    big_grads = {n: [None] * DEPTH for n in BIG}
    small_grads = {n: [None] * DEPTH for n in ("g_ffn1", "g_mix", "b_in", "rel_bias", "g_ffn2")}
    pair = [[None] * len(BIG) for _ in range(DEPTH)]
    landed = [[None] * len(BIG) for _ in range(DEPTH)]

    def pair_up(l, idx, tag):
        for i, p in zip(idx, _pair_sums([big_grads[BIG[i]][l] for i in idx], tag)):
            pair[l][i] = p

    core = lax.axis_index("c").astype(jnp.int32)[None]

    def to_sibling_half(l, idx, tag):
        def plan(srcs, bufs, new):
            x, y, c = _position()
            return [(s.at[1 - c], d, (x, y, 1 - c)) for s, d in zip(srcs, new)]
        def done(carry):
            for j, (i, got) in enumerate(zip(idx, carry.out[1])):
                pair[l][i] = _add_pair(core, big_grads[BIG[i]][l], got, f"grad_pair_sum_{tag}_{j}")
        def make():
            split = [big_grads[BIG[i]][l] for i in idx]
            return _Carry(split, [], [jax.ShapeDtypeStruct(s.shape[1:], s.dtype) for s in split], len(idx), plan, done)
        return make

    def exchange(l, idx):
        def done(carry):
            for i, a in zip(idx, carry.out[1]):
                landed[l][i] = a
        def make():
            carry = _scatter_carry([pair[l][i] for i in idx])
            carry.done = done
            return carry
        return make

    def exchange_one_way(l, i, k):
        def plan(srcs, bufs, new):
            x, y, c = _position()
            chip_k, idx_k = _other_chips(x, y)[k]
            return [(srcs[0].at[idx_k], bufs[0].at[k], (*chip_k, c))]
        def done(carry):
            landed[l][i] = carry.out[0][0]
        def make():
            if landed[l][i] is None:
                landed[l][i] = lax.empty((3,) + pair[l][i].shape[1:], BF16)
            return _Carry([pair[l][i]], [landed[l][i]], [], 1, plan, done)
        return make

    bwd_carries = [
        {"ffn2": {},
         "mix": {"dmerged": [to_sibling_half(0, FFN2_IN + FFN2_OUT, "l0_ffn2")],
                 "dsb": [exchange(1, FFN1 + W_IN)], "dch": [exchange(1, MIXER_REST + FFN2_IN + FFN2_OUT)],
                 "dfox": [exchange(0, FFN2_IN + FFN2_OUT)]},
         "ffn1": {"da": [exchange(0, MIXER_REST)], "dwout": [exchange_one_way(0, 2, 0)],
                  "dwin": [exchange_one_way(0, 2, 1)], "dh": [exchange_one_way(0, 2, 2)]}},
        {"ffn2": {}, "mix": {},
         "ffn1": {"da": [to_sibling_half(1, W_IN + MIXER_REST + FFN2_IN + FFN2_OUT, "l1_rest")]}},
    ]
    for l in reversed(range(DEPTH)):
        lw = layers[l]
        x0, s1, x1, s2, x2, s3 = saved[l]
        dx2, small_grads["g_ffn2"][l], big_grads["w_ffn2_in"][l], big_grads["w_ffn2_out"][l] = ffn_bwd(
            x2, lw, "ffn2", s3, dact, f"l{l}_ffn2", bwd_carries[l]["ffn2"])
        dx1, mg = _mixer_bwd(x1, lw, s2, dx2, f"l{l}_mix", bwd_carries[l]["mix"])
        for n in ("w_in", "w_br_sb", "w_br_ch", "w_br_fox", "w_out"):
            big_grads[n][l] = mg[n]
        small_grads["b_in"][l] = _unpad_w_in(mg["b_in"])[0]
        small_grads["g_mix"][l] = mg["g_mix"][0]
        small_grads["rel_bias"][l] = mg["rel_bias"]
        if l == 0:
            pair_up(0, W_IN + MIXER_REST, "l0_mix")
        dact, dg1, big_grads["w_ffn1_in"][l], big_grads["w_ffn1_out"][l] = ffn_bwd(
            x0, lw, "ffn1", s1, dx1, f"l{l}_ffn1", bwd_carries[l]["ffn1"])
        small_grads["g_ffn1"][l] = dg1[0]
        small_grads["g_ffn2"][l] = small_grads["g_ffn2"][l][0]
        if l == 1:
            pair_up(1, FFN1, "l1_ffn1")
    grad_x = dact[None]
    pair_up(0, FFN1, "l0_ffn1")
    for i, a in zip(FFN1, _scatter_chips([pair[0][i] for i in FFN1], "grad_scatter_l0")):
        landed[0][i] = a

    small = {n: jnp.stack(small_grads[n]) for n in small_grads}
    small["g_final"] = dg_final[0]
    small_sum, loss = _unpack_small(_allreduce_small(_pack_small(small, loss_row[0, :1]), "allreduce_small"),
                                    {n: w[n].shape for n in SMALL})

    axes = [1 if n in ROW_SHARDED else 0 for n in BIG] * DEPTH
    summed = _finish_grads(pair[0] + pair[1], landed[0] + landed[1], axes, "all")
    grads = {n: jnp.stack([summed[i], summed[len(BIG) + i]]) for i, n in enumerate(BIG)}
    grads.update(small_sum)

    delta, new_m, new_v = {}, {}, {}
    for n in BIG:
        shape = w[n].shape
        flat = lambda t: t.reshape(-1, shape[-1])
        d, nm, nv = _adamw(flat(w[n]), flat(grads[n]), flat(m[n]), flat(v[n]), f"adamw_{n}")
        delta[n], new_m[n], new_v[n] = d.reshape(shape), nm.reshape(shape), nv.reshape(shape)
    zero = jnp.zeros((1,), F32)
    sd, sm, sv = _adamw(_pack_small(w, zero), _pack_small(grads, zero), _pack_small(m, zero), _pack_small(v, zero),
                        "adamw_small")
    shapes = {n: w[n].shape for n in SMALL}
    for dst, src in ((delta, sd), (new_m, sm), (new_v, sv)):
        dst.update(_unpack_small(src, shapes)[0])

    return (loss, grad_x, *[grads[n] for n in names], *[delta[n] for n in names],
            *[new_m[n] for n in names], *[new_v[n] for n in names])
```

```python
import functools

import numpy as np
import jax
import jax.numpy as jnp
from jax import lax
from jax.experimental import pallas as pl
from jax.experimental.pallas import tpu as pltpu

F32 = jnp.float32
BF16 = jnp.bfloat16

D_MODEL = 1024
DEPTH = 2
CHUNK = 64
HEAD_DIM = 64
H_SB, H_CH, H_FOX = 4, 8, 4
W_SB, W_CH, W_FOX = H_SB * HEAD_DIM, H_CH * HEAD_DIM, H_FOX * HEAD_DIM
LEFT_CHUNKS = 8
MAX_REL = 128
N_REL = 2 * MAX_REL + 1
D_FF = 2816
QKV_WIDTH = 3 * (W_SB + W_CH + W_FOX)
GATE_WIDTH = 3 * D_MODEL
IN_WIDTH = QKV_WIDTH + H_FOX + GATE_WIDTH
F_PAD = 128
P_WIDTH = QKV_WIDTH + GATE_WIDTH + F_PAD
RMS_EPS = 1e-6
NEG = -1e30
SCALE = HEAD_DIM ** -0.5
QB = 256
BAND = (LEFT_CHUNKS + 2) * CHUNK
BAND_PAD = BAND - CHUNK

ADAM_LR, ADAM_B1, ADAM_B2, ADAM_EPS, ADAM_WD, ADAM_STEP = 0.001, 0.9, 0.999, 1e-08, 0.01, 10

N_CHIPS = 4
VMEM_BUDGET = 52 * 1024 * 1024

NT = (((1,), (1,)), ((), ()))
TN = (((0,), (0,)), ((), ()))
NN = (((1,), (0,)), ((), ()))
MESH = pl.DeviceIdType.MESH


def _pcall(body, **kw):
    return pl.pallas_call(body, **kw)


class _Carry:
    def __init__(self, srcs, bufs, new, count, plan, done=None):
        self.srcs, self.bufs, self.new, self.count, self.plan = list(srcs), list(bufs), list(new), count, plan
        self.out, self.done = None, done

    @staticmethod
    def join(parts):
        parts = [p for p in parts if p is not None]
        if not parts:
            return None

        def plan(srcs, bufs, new):
            copies, s, b, n = [], 0, 0, 0
            for p in parts:
                copies += p.plan(srcs[s:s + len(p.srcs)], bufs[b:b + len(p.bufs)], new[n:n + len(p.new)])
                s, b, n = s + len(p.srcs), b + len(p.bufs), n + len(p.new)
            return copies

        whole = _Carry(sum((p.srcs for p in parts), []), sum((p.bufs for p in parts), []),
                       sum((p.new for p in parts), []), sum(p.count for p in parts), plan)
        whole.parts = parts
        return whole

    def share_out(self):
        b, n = 0, 0
        for p in getattr(self, "parts", []):
            p.out = (self.out[0][b:b + len(p.bufs)], self.out[1][n:n + len(p.new)])
            b, n = b + len(p.bufs), n + len(p.new)
            p.share_out()
        if self.done is not None:
            self.done(self)


def _carry_of(carries, key):
    return _Carry.join([make() for make in carries.get(key, [])])


def _pcall_carrying(body, carry, **kw):
    if carry is None:
        return _pcall(body, **kw)
    in_specs = list(kw.pop("in_specs"))
    out_specs, out_shape = kw.pop("out_specs"), kw.pop("out_shape")
    single = not isinstance(out_shape, (list, tuple))
    out_specs = [out_specs] if single else list(out_specs)
    out_shape = [out_shape] if single else list(out_shape)
    scratch = list(kw.pop("scratch_shapes", []))
    grid = tuple(kw.get("grid", ()))
    n_in, n_out, n_scr = len(in_specs), len(out_specs), len(scratch)
    ns, nb, nn = len(carry.srcs), len(carry.bufs), len(carry.new)

    def body2(*refs):
        ins, srcs = refs[:n_in], refs[n_in:n_in + ns]
        at = n_in + ns + nb
        outs, bufs, new = refs[at:at + n_out], refs[at + n_out:at + n_out + nb], refs[at + n_out + nb:at + n_out + nb + nn]
        at += n_out + nb + nn
        scr, (send_sems, recv_sems) = refs[at:at + n_scr], refs[at + n_scr:]

        def copies():
            return [_remote_copy(s, d, send_sems, recv_sems, k, to)
                    for k, (s, d, to) in enumerate(carry.plan(srcs, bufs, new))]

        def start():
            for cp in copies():
                cp.start()

        def wait():
            for cp in copies():
                cp.wait()

        if grid:
            ids = [pl.program_id(a) for a in range(len(grid))]
            pl.when(functools.reduce(jnp.logical_and, [i == 0 for i in ids]))(start)
        else:
            start()
        body(*ins, *outs, *scr)
        if grid:
            pl.when(functools.reduce(jnp.logical_and, [i == g - 1 for i, g in zip(ids, grid)]))(wait)
        else:
            wait()

    call = _pcall(
        body2, in_specs=in_specs + [HBM_SPEC] * (ns + nb), out_specs=out_specs + [HBM_SPEC] * (nb + nn),
        out_shape=out_shape + [jax.ShapeDtypeStruct(b.shape, b.dtype) for b in carry.bufs] + carry.new,
        scratch_shapes=scratch + [pltpu.SemaphoreType.DMA((carry.count,)), pltpu.SemaphoreType.DMA((carry.count,))],
        input_output_aliases={n_in + ns + j: n_out + j for j in range(nb)}, **kw)

    def apply(*args):
        res = call(*args, *carry.srcs, *carry.bufs)
        carry.out = (list(res[n_out:n_out + nb]), list(res[n_out + nb:]))
        carry.share_out()
        return res[0] if single else list(res[:n_out])

    return apply


def _pick(n, cap, mult=128):
    best = None
    d = mult
    while d <= min(n, cap):
        if n % d == 0:
            best = d
        d += mult
    return n if best is None else best


def _nbytes(shape, dtype):
    return int(np.prod(shape)) * jnp.dtype(dtype).itemsize


def _mm(a, b, *, mode, name, out_dtype=F32, alpha=1.0, bias=None, res=None, tm_cap=1024, tn_cap=512,
        tk_cap=2816, gathered=None, out_split=None, carry=None, norm_bwd=None):
    if mode == "tn":
        K, M = a.shape
    else:
        M, K = a.shape
    dn = {"nn": NN, "nt": NT, "tn": TN}[mode]
    b_rows = None
    if gathered is None:
        N = b.shape[0] if mode == "nt" else b.shape[1]
        tn = {None: _pick(N, tn_cap), "col": N // N_CHIPS, "row": N // 2}[out_split]
        if out_split == "col":
            tm_cap = min(tm_cap, M // 2)
        tk = K if K <= tk_cap else _pick(K, tk_cap)
        b_spec = (pl.BlockSpec((tn, tk), lambda i, j, k: (j, k)) if mode == "nt"
                  else pl.BlockSpec((tk, tn), lambda i, j, k: (k, j)))
    else:
        r, c = b.shape[1:]
        rows, cols = (r, N_CHIPS * c) if gathered == "col" else (N_CHIPS * r, c)
        N = rows if mode == "nt" else cols
        assert K == (cols if mode == "nt" else rows), (name, K, rows, cols)
        if gathered == "col" and mode == "nn":
            tn, tk = c, (r if r <= tk_cap else _pick(r, tk_cap))
            b_spec = pl.BlockSpec((None, tk, c), lambda i, j, k: (j, k, 0))
        elif gathered == "col":
            tn, tk = _pick(r, tn_cap), c
            b_spec = pl.BlockSpec((None, tn, c), lambda i, j, k: (k, j, 0))
        elif mode == "nn":
            tn, tk, b_rows = _pick(c, tn_cap), K, N_CHIPS
            b_spec = pl.BlockSpec((N_CHIPS, r, tn), lambda i, j, k: (0, 0, j))
        else:
            tn, tk, b_rows = 2 * r, K, 2
            b_spec = pl.BlockSpec((2, r, c), lambda i, j, k: (j, 0, 0))
    tm = _pick(M, tm_cap)
    nk = K // tk

    a_spec = (pl.BlockSpec((tk, tm), lambda i, j, k: (k, i)) if mode == "tn"
              else pl.BlockSpec((tm, tk), lambda i, j, k: (i, k)))
    in_specs = [a_spec, b_spec]
    args = [a, b]
    if bias is not None:
        in_specs.append(pl.BlockSpec((1, tn), lambda i, j, k: (0, j)))
        args.append(bias)
    if res is not None:
        in_specs.append(pl.BlockSpec((tm, tn), lambda i, j, k: (i, j)))
        args.append(res)
    if norm_bwd is not None:
        assert tn == N and alpha == 1.0 and out_split is None and bias is None and res is None, name
        x_in, g_in, dres_in = norm_bwd
        in_specs += [pl.BlockSpec((tm, tn), lambda i, j, k: (i, 0)), pl.BlockSpec((1, tn), lambda i, j, k: (0, 0)),
                     pl.BlockSpec((tm, tn), lambda i, j, k: (i, 0))]
        args += [x_in, g_in, dres_in]

    est = 2 * (_nbytes((tm, tk), a.dtype) + _nbytes((tk, tn), b.dtype) + _nbytes((tm, tn), out_dtype))
    est += _nbytes((tm, tn), F32) * (3 if res is not None else 1)
    est += _nbytes((tm, tn), F32) * (4 if norm_bwd is not None else 0)
    assert est < VMEM_BUDGET, (name, est)

    def body(*refs):
        a_ref, b_ref = refs[0], refs[1]
        pos = 2
        bias_ref = res_ref = None
        if bias is not None:
            bias_ref = refs[pos]
            pos += 1
        if res is not None:
            res_ref = refs[pos]
            pos += 1
        if norm_bwd is not None:
            x_ref, g_ref, dres_ref = refs[pos:pos + 3]
            pos += 3
        o_ref = refs[pos]
        if norm_bwd is not None:
            dg_ref = refs[pos + 1]
            pos += 1
        acc_ref = refs[pos + 1] if nk > 1 else None

        bv = b_ref[...]
        if b_rows is not None:
            bv = bv.reshape(b_rows * bv.shape[1], bv.shape[2])
        part = lax.dot_general(a_ref[...].astype(BF16), bv.astype(BF16), dn, preferred_element_type=F32)

        def finish(acc):
            if alpha != 1.0:
                acc = acc * alpha
            if bias_ref is not None:
                acc = acc + bias_ref[...]
            if res_ref is not None:
                acc = acc + res_ref[...]
            if norm_bwd is not None:
                xv = x_ref[...]
                rstd = lax.rsqrt(jnp.mean(xv * xv, axis=-1, keepdims=True) + RMS_EPS)
                xhat = xv * rstd
                dyg = acc * g_ref[...]
                part_g = jnp.sum(acc * xhat, axis=0, keepdims=True)
                acc = dres_ref[...] + rstd * (dyg - xhat * jnp.mean(dyg * xhat, axis=-1, keepdims=True))
                first = pl.program_id(0) == 0

                @pl.when(first)
                def _():
                    dg_ref[...] = part_g

                @pl.when(jnp.logical_not(first))
                def _():
                    dg_ref[...] += part_g
            o_ref[...] = acc.astype(out_dtype)

        if nk == 1:
            finish(part)
        else:
            k = pl.program_id(2)

            @pl.when(k == 0)
            def _():
                acc_ref[...] = part

            @pl.when(k > 0)
            def _():
                acc_ref[...] += part

            @pl.when(k == nk - 1)
            def _():
                finish(acc_ref[...])

    if out_split == "col":
        per_half = M // 2 // tm
        out_spec = pl.BlockSpec((None, None, tm, tn), lambda i, j, k: (i // per_half, j, i % per_half, 0))
        out_shape = jax.ShapeDtypeStruct((2, N_CHIPS, M // 2, tn), out_dtype)
    elif out_split == "row":
        out_spec = pl.BlockSpec((None, tm, tn), lambda i, j, k: (j, i, 0))
        out_shape = jax.ShapeDtypeStruct((2, M, tn), out_dtype)
    else:
        out_spec = pl.BlockSpec((tm, tn), lambda i, j, k: (i, j))
        out_shape = jax.ShapeDtypeStruct((M, N), out_dtype)
    semantics = ("parallel", "parallel", "arbitrary")
    if norm_bwd is not None:
        out_spec = [out_spec, pl.BlockSpec((1, tn), lambda i, j, k: (0, 0))]
        out_shape = [out_shape, jax.ShapeDtypeStruct((1, N), F32)]
        semantics = ("arbitrary", "arbitrary", "arbitrary")
    return _pcall_carrying(
        body, carry, name=name, grid=(M // tm, N // tn, nk), in_specs=in_specs, out_specs=out_spec,
        out_shape=out_shape,
        scratch_shapes=[pltpu.VMEM((tm, tn), F32)] if nk > 1 else [],
        compiler_params=pltpu.CompilerParams(dimension_semantics=semantics),
    )(*args)


def _rmsnorm_fwd(x, g, name):
    T, Dm = x.shape
    tm = _pick(T, 256, 8)

    def body(x_ref, g_ref, h_ref):
        xv = x_ref[...]
        rstd = lax.rsqrt(jnp.mean(xv * xv, axis=-1, keepdims=True) + RMS_EPS)
        h_ref[...] = (xv * rstd * g_ref[...]).astype(BF16)

    return _pcall(
        body, name=name, grid=(T // tm,),
        in_specs=[pl.BlockSpec((tm, Dm), lambda i: (i, 0)), pl.BlockSpec((1, Dm), lambda i: (0, 0))],
        out_specs=pl.BlockSpec((tm, Dm), lambda i: (i, 0)),
        out_shape=jax.ShapeDtypeStruct((T, Dm), BF16),
    )(x, g)


def _final_loss(x, g, tgt, name):
    T, Dm = x.shape
    tm = _pick(T, 256, 8)

    def body(x_ref, g_ref, t_ref, dx_ref, dg_ref, loss_ref):
        xv = x_ref[...]
        gv = g_ref[...]
        rstd = lax.rsqrt(jnp.mean(xv * xv, axis=-1, keepdims=True) + RMS_EPS)
        xhat = xv * rstd
        err = xhat * gv - t_ref[...]
        part_loss = 0.5 * jnp.sum(jnp.mean(err * err, axis=-1, keepdims=True), axis=0, keepdims=True)
        dy = err * (1.0 / Dm)
        dyg = dy * gv
        dx_ref[...] = rstd * (dyg - xhat * jnp.mean(dyg * xhat, axis=-1, keepdims=True))
        part_g = jnp.sum(dy * xhat, axis=0, keepdims=True)
        part_l = jnp.broadcast_to(part_loss, (1, 128))

        @pl.when(pl.program_id(0) == 0)
        def _():
            dg_ref[...] = part_g
            loss_ref[...] = part_l

        @pl.when(pl.program_id(0) > 0)
        def _():
            dg_ref[...] += part_g
            loss_ref[...] += part_l

    row = pl.BlockSpec((tm, Dm), lambda i: (i, 0))
    vec = pl.BlockSpec((1, Dm), lambda i: (0, 0))
    return _pcall(
        body, name=name, grid=(T // tm,), in_specs=[row, vec, row],
        out_specs=[row, vec, pl.BlockSpec((1, 128), lambda i: (0, 0))],
        out_shape=[jax.ShapeDtypeStruct((T, Dm), F32), jax.ShapeDtypeStruct((1, Dm), F32),
                   jax.ShapeDtypeStruct((1, 128), F32)],
        compiler_params=pltpu.CompilerParams(dimension_semantics=("arbitrary",)),
    )(x, g, tgt)


def _sigmoid(z):
    return 1.0 / (1.0 + jnp.exp(-z))


def _ffn_in_swiglu(x, g, w_in, name, carry=None):
    T, Dm = x.shape
    cw = w_in.shape[2]
    tm = _pick(T, 512, 16)

    def body(x_ref, gain_ref, wg_ref, wu_ref, h_ref, g_ref, u_ref, a_ref):
        @pl.when(pl.program_id(1) == 0)
        def _():
            xv = x_ref[...]
            rstd = lax.rsqrt(jnp.mean(xv * xv, axis=-1, keepdims=True) + RMS_EPS)
            h_ref[...] = (xv * rstd * gain_ref[...]).astype(BF16)

        hv = h_ref[...]
        gv = jnp.dot(hv, wg_ref[...], preferred_element_type=F32)
        uv = jnp.dot(hv, wu_ref[...], preferred_element_type=F32)
        g_ref[...] = gv.astype(BF16)
        u_ref[...] = uv.astype(BF16)
        a_ref[...] = (gv * _sigmoid(gv) * uv).astype(BF16)

    row = pl.BlockSpec((tm, Dm), lambda i, j: (i, 0))
    out = pl.BlockSpec((tm, cw), lambda i, j: (i, j))
    return _pcall_carrying(
        body, carry, name=name, grid=(T // tm, 2),
        in_specs=[row, pl.BlockSpec((1, Dm), lambda i, j: (0, 0)), pl.BlockSpec((None, Dm, cw), lambda i, j: (j, 0, 0)),
                  pl.BlockSpec((None, Dm, cw), lambda i, j: (j + 2, 0, 0))],
        out_specs=[row, out, out, out],
        out_shape=[jax.ShapeDtypeStruct((T, Dm), BF16)] + [jax.ShapeDtypeStruct((T, D_FF), BF16)] * 3,
        compiler_params=pltpu.CompilerParams(dimension_semantics=("parallel", "arbitrary")),
    )(x, g, w_in, w_in)


def _swiglu_bwd(g, u, da, name):
    T = g.shape[0]
    tm = _pick(T, 256, 16)

    def body(g_ref, u_ref, da_ref, d_ref):
        gv = g_ref[...].astype(F32)
        uv = u_ref[...].astype(F32)
        dav = da_ref[...].astype(F32)
        sg = _sigmoid(gv)
        d_ref[:, :D_FF] = (dav * uv * (sg + gv * sg * (1.0 - sg))).astype(BF16)
        d_ref[:, D_FF:] = (dav * gv * sg).astype(BF16)

    row = pl.BlockSpec((tm, D_FF), lambda i: (i, 0))
    return _pcall(
        body, name=name, grid=(T // tm,), in_specs=[row, row, row],
        out_specs=pl.BlockSpec((tm, 2 * D_FF), lambda i: (i, 0)),
        out_shape=jax.ShapeDtypeStruct((T, 2 * D_FF), BF16),
    )(g, u, da)


def _branches_fwd(p, outs, weights, name):
    T = p.shape[0]
    cw = weights[0].shape[2]
    tm = _pick(T, 1024, 16)
    first_gate, per_branch = QKV_WIDTH // cw, D_MODEL // cw

    def body(*refs):
        gates, o_refs, w_refs, y_refs, m_ref = refs[0:3], refs[3:6], refs[6:9], refs[9:12], refs[12]
        merged = None
        for g_ref, o_ref, w_ref, y_ref in zip(gates, o_refs, w_refs, y_refs):
            y = jnp.dot(o_ref[...], w_ref[...], preferred_element_type=F32)
            y_ref[...] = y
            term = _sigmoid(g_ref[...].astype(F32)) * y
            merged = term if merged is None else merged + term
        m_ref[...] = merged.astype(BF16)

    gate_specs = [pl.BlockSpec((tm, cw), functools.partial(
        lambda i, j, kk: (i, first_gate + per_branch * kk + j), kk=kk)) for kk in range(3)]
    o_specs = [pl.BlockSpec((tm, o.shape[1]), lambda i, j: (i, 0)) for o in outs]
    w_specs = [pl.BlockSpec((None, wk.shape[1], cw), lambda i, j: (j, 0, 0)) for wk in weights]
    tile = pl.BlockSpec((tm, cw), lambda i, j: (i, j))
    return _pcall(
        body, name=name, grid=(T // tm, N_CHIPS), in_specs=gate_specs + o_specs + w_specs, out_specs=[tile] * 4,
        out_shape=[jax.ShapeDtypeStruct((T, D_MODEL), F32)] * 3 + [jax.ShapeDtypeStruct((T, D_MODEL), BF16)],
    )(p, p, p, *outs, *weights)


def _branches_bwd(p, ys, dm, weights, name):
    T = p.shape[0]
    cw = weights[0].shape[2]
    tm = _pick(T, 1024, 16)
    first_gate, per_branch = QKV_WIDTH // cw, D_MODEL // cw
    widths = [wk.shape[1] for wk in weights]

    def body(*refs):
        gates, y_refs, dm_ref, w_refs = refs[0:3], refs[3:6], refs[6], refs[7:10]
        dy_refs, dg_refs, do_refs, acc_refs = refs[10:13], refs[13:16], refs[16:19], refs[19:22]
        j = pl.program_id(1)
        dmv = dm_ref[...]
        for g_ref, y_ref, w_ref, dy_ref, dg_ref, do_ref, acc_ref in zip(gates, y_refs, w_refs, dy_refs, dg_refs,
                                                                       do_refs, acc_refs):
            s = _sigmoid(g_ref[...].astype(F32))
            dy = (s * dmv).astype(BF16)
            dy_ref[...] = dy
            dg_ref[...] = (dmv * y_ref[...] * s * (1.0 - s)).astype(BF16)
            part = _dot_nt(dy, w_ref[...])

            @pl.when(j == 0)
            def _():
                acc_ref[...] = part

            @pl.when(j > 0)
            def _():
                acc_ref[...] += part

            @pl.when(j == N_CHIPS - 1)
            def _():
                do_ref[...] = acc_ref[...].astype(BF16)

    gate_specs = [pl.BlockSpec((tm, cw), functools.partial(
        lambda i, j, kk: (i, first_gate + per_branch * kk + j), kk=kk)) for kk in range(3)]
    tile = pl.BlockSpec((tm, cw), lambda i, j: (i, j))
    w_specs = [pl.BlockSpec((None, r, cw), lambda i, j: (j, 0, 0)) for r in widths]
    do_specs = [pl.BlockSpec((tm, r), lambda i, j: (i, 0)) for r in widths]
    wide = jax.ShapeDtypeStruct((T, D_MODEL), BF16)
    out = _pcall(
        body, name=name, grid=(T // tm, N_CHIPS), in_specs=gate_specs + [tile] * 4 + w_specs,
        out_specs=[tile] * 6 + do_specs,
        out_shape=[wide] * 6 + [jax.ShapeDtypeStruct((T, r), BF16) for r in widths],
        scratch_shapes=[pltpu.VMEM((tm, r), F32) for r in widths],
        compiler_params=pltpu.CompilerParams(dimension_semantics=("parallel", "arbitrary")),
    )(p, p, p, *ys, dm, *weights)
    return out[0:3], out[3:6], out[6:9]


def _colsum(a, name):
    T, N = a.shape
    tm = _pick(T, 256, 16)

    def body(a_ref, o_ref):
        part = jnp.sum(a_ref[...].astype(F32), axis=0, keepdims=True)

        @pl.when(pl.program_id(0) == 0)
        def _():
            o_ref[...] = part

        @pl.when(pl.program_id(0) > 0)
        def _():
            o_ref[...] += part

    return _pcall(
        body, name=name, grid=(T // tm,), in_specs=[pl.BlockSpec((tm, N), lambda i: (i, 0))],
        out_specs=pl.BlockSpec((1, N), lambda i: (0, 0)), out_shape=jax.ShapeDtypeStruct((1, N), F32),
        compiler_params=pltpu.CompilerParams(dimension_semantics=("arbitrary",)),
    )(a)


def _adamw(w, g, m, v, name):
    R, C = w.shape
    tr = 256 if R % 256 == 0 else R
    c1 = 1.0 / (1.0 - ADAM_B1 ** ADAM_STEP)
    c2 = 1.0 / (1.0 - ADAM_B2 ** ADAM_STEP)

    def body(w_ref, g_ref, m_ref, v_ref, d_ref, nm_ref, nv_ref):
        gv = g_ref[...]
        mn = ADAM_B1 * m_ref[...] + (1.0 - ADAM_B1) * gv
        vn = ADAM_B2 * v_ref[...] + (1.0 - ADAM_B2) * (gv * gv)
        nm_ref[...] = mn
        nv_ref[...] = vn
        d_ref[...] = -ADAM_LR * ((mn * c1) / (jnp.sqrt(vn * c2) + ADAM_EPS) + ADAM_WD * w_ref[...])

    spec = pl.BlockSpec((tr, C), lambda i: (i, 0))
    return _pcall(
        body, name=name, grid=(R // tr,), in_specs=[spec] * 4, out_specs=[spec] * 3,
        out_shape=[jax.ShapeDtypeStruct((R, C), F32)] * 3,
    )(w, g, m, v)


def _log_sigmoid(z):
    return jnp.minimum(z, 0.0) - jnp.log(1.0 + jnp.exp(-jnp.abs(z)))


def _dot3(x, m01):
    x1 = x.astype(BF16)
    r1 = x - x1.astype(F32)
    x2 = r1.astype(BF16)
    x3 = (r1 - x2.astype(F32)).astype(BF16)
    n = x.shape[0]
    if n % 16:
        return (jnp.dot(x1, m01, preferred_element_type=F32) + jnp.dot(x2, m01, preferred_element_type=F32)
                + jnp.dot(x3, m01, preferred_element_type=F32))
    y = jnp.dot(jnp.concatenate([x1, x2, x3], axis=0), m01, preferred_element_type=F32)
    return y[:n] + y[n:2 * n] + y[2 * n:]


def _dot_nt(a, b):
    return lax.dot_general(a, b, NT, preferred_element_type=F32)


def _dot_tn(a, b):
    return lax.dot_general(a, b, TN, preferred_element_type=F32)


def _iota2(shape):
    return lax.broadcasted_iota(jnp.int32, shape, 0), lax.broadcasted_iota(jnp.int32, shape, 1)


HEADS_PER_STEP = 4
HEADS = range(HEADS_PER_STEP)


CHUNK_HEADS_PER_STEP = 4
CHUNK_HEADS = range(CHUNK_HEADS_PER_STEP)


def _head_spec(T, rows=None, width=HEAD_DIM, heads=HEADS_PER_STEP):
    return pl.BlockSpec((heads, T if rows is None else rows, width), lambda g: (g, 0, 0))


def _sb_weights(qb, kb, t0, s0, racc, m_gt, row, col):
    z = _dot_nt(qb, kb) * SCALE
    strict = (s0 + col) < (t0 + row)
    lb = _log_sigmoid(z)
    lf = jnp.where(strict, lb - z, 0.0)
    between = _dot3(lf, m_gt) + racc
    w = jnp.where(strict, jnp.exp(lb + between), 0.0)
    return strict, lb, lf, w


def _sb_fwd(q, k, v, name, carry=None):
    H, T, _ = q.shape
    nb = T // QB

    def body(q_ref, k_ref, v_ref, o_ref):
        row, col = _iota2((QB, QB))
        m_gt = (row > col).astype(BF16)

        def qblock(i, _):
            t0 = pl.multiple_of(i * QB, QB)
            qbs = [q_ref[h, pl.ds(t0, QB), :].astype(BF16) for h in HEADS]

            def kblock(jj, carry):
                s0 = pl.multiple_of((i - jj) * QB, QB)
                out = []
                for h in HEADS:
                    acc, racc = carry[h]
                    kb = k_ref[h, pl.ds(s0, QB), :].astype(BF16)
                    vb = v_ref[h, pl.ds(s0, QB), :].astype(BF16)
                    _, _, lf, w = _sb_weights(qbs[h], kb, t0, s0, racc, m_gt, row, col)
                    acc = acc + jnp.dot(w.astype(BF16), vb, preferred_element_type=F32)
                    out.append((acc, racc + jnp.sum(lf, axis=1, keepdims=True)))
                return tuple(out)

            res = lax.fori_loop(0, i + 1, kblock,
                                tuple((jnp.zeros((QB, HEAD_DIM), F32), jnp.zeros((QB, 1), F32)) for _ in HEADS))
            for h in HEADS:
                o_ref[h, pl.ds(t0, QB), :] = res[h][0].astype(BF16)
            return 0

        lax.fori_loop(0, nb, qblock, 0)

    spec = _head_spec(T)
    return _pcall_carrying(body, carry, name=name, grid=(H // HEADS_PER_STEP,), in_specs=[spec] * 3, out_specs=spec,
                           out_shape=jax.ShapeDtypeStruct((H, T, HEAD_DIM), BF16))(q, k, v)


def _sb_bwd(q, k, v, do, name, carry=None):
    H, T, _ = q.shape
    nb = T // QB

    def body(q_ref, k_ref, v_ref, do_ref, dq_ref, dk_out, dv_out, racc_ref, dk_ref, dv_ref):
        row, col = _iota2((QB, QB))
        m_gt = (row > col).astype(BF16)
        m_lt = (row < col).astype(BF16)
        dk_ref[...] = jnp.zeros_like(dk_ref)
        dv_ref[...] = jnp.zeros_like(dv_ref)

        def qblock(i, _):
            t0 = pl.multiple_of(i * QB, QB)
            qbs = [q_ref[h, pl.ds(t0, QB), :].astype(BF16) for h in HEADS]
            dobs = [do_ref[h, pl.ds(t0, QB), :].astype(BF16) for h in HEADS]

            def suffix(jj, raccs):
                j = i - jj
                s0 = pl.multiple_of(j * QB, QB)
                out = []
                for h in HEADS:
                    kb = k_ref[h, pl.ds(s0, QB), :].astype(BF16)
                    z = _dot_nt(qbs[h], kb) * SCALE
                    lf = jnp.where((s0 + col) < (t0 + row), _log_sigmoid(z) - z, 0.0)
                    racc_ref[h, j] = raccs[h]
                    out.append(raccs[h] + jnp.sum(lf, axis=1, keepdims=True))
                return tuple(out)

            lax.fori_loop(0, i + 1, suffix, tuple(jnp.zeros((QB, 1), F32) for _ in HEADS))

            def kblock(j, carry):
                s0 = pl.multiple_of(j * QB, QB)
                out = []
                for h in HEADS:
                    dq, cacc = carry[h]
                    kb = k_ref[h, pl.ds(s0, QB), :].astype(BF16)
                    vb = v_ref[h, pl.ds(s0, QB), :].astype(BF16)
                    strict, lb, _, w = _sb_weights(qbs[h], kb, t0, s0, racc_ref[h, j], m_gt, row, col)
                    e = _dot_nt(dobs[h], vb) * w
                    c_left = _dot3(e, m_lt) + cacc
                    sig = jnp.exp(lb)
                    dz = jnp.where(strict, e * (1.0 - sig) - c_left * sig, 0.0).astype(BF16)
                    dq = dq + jnp.dot(dz, kb, preferred_element_type=F32)
                    dk_ref[h, pl.ds(s0, QB), :] += _dot_tn(dz, qbs[h]) * SCALE
                    dv_ref[h, pl.ds(s0, QB), :] += _dot_tn(w.astype(BF16), dobs[h])
                    out.append((dq, cacc + jnp.sum(e, axis=1, keepdims=True)))
                return tuple(out)

            res = lax.fori_loop(0, i + 1, kblock,
                                tuple((jnp.zeros((QB, HEAD_DIM), F32), jnp.zeros((QB, 1), F32)) for _ in HEADS))
            for h in HEADS:
                dq_ref[h, pl.ds(t0, QB), :] = (res[h][0] * SCALE).astype(BF16)
            return 0

        lax.fori_loop(0, nb, qblock, 0)
        dk_out[...] = dk_ref[...].astype(BF16)
        dv_out[...] = dv_ref[...].astype(BF16)

    spec = _head_spec(T)
    acc = pltpu.VMEM((HEADS_PER_STEP, T, HEAD_DIM), F32)
    return _pcall_carrying(body, carry, name=name, grid=(H // HEADS_PER_STEP,), in_specs=[spec] * 4,
                           out_specs=[spec] * 3, out_shape=[jax.ShapeDtypeStruct((H, T, HEAD_DIM), BF16)] * 3,
                           scratch_shapes=[pltpu.VMEM((HEADS_PER_STEP, nb, QB, 1), F32), acc, acc])(q, k, v, do)


def _fox_logits(qb, kb, fq, fk, t0, s0, row, col):
    z = _dot_nt(qb, kb) * SCALE + fq - fk
    return jnp.where((s0 + col) <= (t0 + row), z, NEG)


def _fox_specs(T):
    return _head_spec(T, width=1), _head_spec(T, rows=T // QB, width=QB)


def _fox_fwd(q, k, v, fq, fk, name, carry=None):
    H, T, _ = q.shape
    nb = T // QB

    def body(q_ref, k_ref, v_ref, fq_ref, fk_ref, o_ref, lse_ref):
        row, col = _iota2((QB, QB))

        def qblock(i, _):
            t0 = pl.multiple_of(i * QB, QB)
            qbs = [q_ref[h, pl.ds(t0, QB), :].astype(BF16) for h in HEADS]
            fqs = [fq_ref[h, pl.ds(t0, QB), :] for h in HEADS]

            def kblock(j, carry):
                s0 = pl.multiple_of(j * QB, QB)
                out = []
                for h in HEADS:
                    acc, m, l = carry[h]
                    kb = k_ref[h, pl.ds(s0, QB), :].astype(BF16)
                    vb = v_ref[h, pl.ds(s0, QB), :].astype(BF16)
                    z = _fox_logits(qbs[h], kb, fqs[h], fk_ref[h, pl.ds(j, 1), :], t0, s0, row, col)
                    m_new = jnp.maximum(m, jnp.max(z, axis=1, keepdims=True))
                    a = jnp.exp(m - m_new)
                    p = jnp.exp(z - m_new)
                    acc = a * acc + jnp.dot(p.astype(BF16), vb, preferred_element_type=F32)
                    out.append((acc, m_new, a * l + jnp.sum(p, axis=1, keepdims=True)))
                return tuple(out)

            res = lax.fori_loop(0, i + 1, kblock,
                                tuple((jnp.zeros((QB, HEAD_DIM), F32), jnp.full((QB, 1), NEG, F32),
                                       jnp.zeros((QB, 1), F32)) for _ in HEADS))
            for h in HEADS:
                acc, m, l = res[h]
                o_ref[h, pl.ds(t0, QB), :] = (acc / l).astype(BF16)
                lse_ref[h, pl.ds(t0, QB), :] = m + jnp.log(l)
            return 0

        lax.fori_loop(0, nb, qblock, 0)

    spec = _head_spec(T)
    fq_spec, fk_spec = _fox_specs(T)
    return _pcall_carrying(
        body, carry, name=name, grid=(H // HEADS_PER_STEP,), in_specs=[spec] * 3 + [fq_spec, fk_spec],
        out_specs=[spec, fq_spec],
        out_shape=[jax.ShapeDtypeStruct((H, T, HEAD_DIM), BF16), jax.ShapeDtypeStruct((H, T, 1), F32)],
    )(q, k, v, fq, fk)


def _fox_bwd(q, k, v, fq, fk, lse, do, name, carry=None):
    H, T, _ = q.shape
    nb = T // QB

    def body(q_ref, k_ref, v_ref, fq_ref, fk_ref, lse_ref, do_ref, dq_ref, dk_out, dv_out, dfk_ref, dk_ref, dv_ref):
        row, col = _iota2((QB, QB))
        dk_ref[...] = jnp.zeros_like(dk_ref)
        dv_ref[...] = jnp.zeros_like(dv_ref)
        dfk_ref[...] = jnp.zeros_like(dfk_ref)

        def qblock(i, _):
            t0 = pl.multiple_of(i * QB, QB)
            qbs = [q_ref[h, pl.ds(t0, QB), :].astype(BF16) for h in HEADS]
            fqs = [fq_ref[h, pl.ds(t0, QB), :] for h in HEADS]
            lses = [lse_ref[h, pl.ds(t0, QB), :] for h in HEADS]
            dobs = [do_ref[h, pl.ds(t0, QB), :].astype(BF16) for h in HEADS]

            def probs(h, j):
                s0 = pl.multiple_of(j * QB, QB)
                kb = k_ref[h, pl.ds(s0, QB), :].astype(BF16)
                vb = v_ref[h, pl.ds(s0, QB), :].astype(BF16)
                z = _fox_logits(qbs[h], kb, fqs[h], fk_ref[h, pl.ds(j, 1), :], t0, s0, row, col)
                return s0, kb, jnp.exp(z - lses[h]), _dot_nt(dobs[h], vb)

            def row_dot(j, deltas):
                out = []
                for h in HEADS:
                    _, _, p, dp = probs(h, j)
                    out.append(deltas[h] + jnp.sum(p * dp, axis=1, keepdims=True))
                return tuple(out)

            deltas = lax.fori_loop(0, i + 1, row_dot, tuple(jnp.zeros((QB, 1), F32) for _ in HEADS))

            def kblock(j, dqs):
                out = []
                for h in HEADS:
                    s0, kb, p, dp = probs(h, j)
                    dz = p * (dp - deltas[h])
                    dzb = dz.astype(BF16)
                    dk_ref[h, pl.ds(s0, QB), :] += _dot_tn(dzb, qbs[h]) * SCALE
                    dv_ref[h, pl.ds(s0, QB), :] += _dot_tn(p.astype(BF16), dobs[h])
                    dfk_ref[h, pl.ds(j, 1), :] += -jnp.sum(dz, axis=0, keepdims=True)
                    out.append(dqs[h] + jnp.dot(dzb, kb, preferred_element_type=F32))
                return tuple(out)

            dqs = lax.fori_loop(0, i + 1, kblock, tuple(jnp.zeros((QB, HEAD_DIM), F32) for _ in HEADS))
            for h in HEADS:
                dq_ref[h, pl.ds(t0, QB), :] = (dqs[h] * SCALE).astype(BF16)
            return 0

        lax.fori_loop(0, nb, qblock, 0)
        dk_out[...] = dk_ref[...].astype(BF16)
        dv_out[...] = dv_ref[...].astype(BF16)

    spec = _head_spec(T)
    fq_spec, fk_spec = _fox_specs(T)
    acc = pltpu.VMEM((HEADS_PER_STEP, T, HEAD_DIM), F32)
    return _pcall_carrying(body, carry, name=name, grid=(H // HEADS_PER_STEP,),
                           in_specs=[spec] * 3 + [fq_spec, fk_spec, fq_spec, spec],
                           out_specs=[spec, spec, spec, fk_spec],
                           out_shape=[jax.ShapeDtypeStruct((H, T, HEAD_DIM), BF16)] * 3
                           + [jax.ShapeDtypeStruct((H, nb, QB), F32)],
                           scratch_shapes=[acc, acc])(q, k, v, fq, fk, lse, do)


def _forget_cumsum(flog, name):
    R, T = flog.shape
    nb = T // QB

    def body(x_ref, o_ref):
        row, col = _iota2((QB, QB))
        m_le = (row <= col).astype(BF16)
        carry = jnp.zeros((R, 1), F32)
        for b in range(nb):
            lf = _log_sigmoid(x_ref[:, b * QB:(b + 1) * QB])
            o_ref[:, b * QB:(b + 1) * QB] = _dot3(lf, m_le) + carry
            carry = carry + jnp.sum(lf, axis=1, keepdims=True)

    spec = pl.BlockSpec((R, T), lambda: (0, 0))
    return _pcall(body, name=name, in_specs=[spec], out_specs=spec,
                  out_shape=jax.ShapeDtypeStruct((R, T), F32))(flog)


def _forget_cumsum_bwd(flog, df, name):
    R, T = flog.shape
    nb = T // QB

    def body(x_ref, df_ref, o_ref):
        row, col = _iota2((QB, QB))
        m_ge = (row >= col).astype(BF16)
        carry = jnp.zeros((R, 1), F32)
        for b in reversed(range(nb)):
            dfb = df_ref[:, b * QB:(b + 1) * QB]
            dlog = _dot3(dfb, m_ge) + carry
            o_ref[:, b * QB:(b + 1) * QB] = dlog * _sigmoid(-x_ref[:, b * QB:(b + 1) * QB])
            carry = carry + jnp.sum(dfb, axis=1, keepdims=True)

    spec = pl.BlockSpec((R, T), lambda: (0, 0))
    return _pcall(body, name=name, in_specs=[spec, spec], out_specs=spec,
                  out_shape=jax.ShapeDtypeStruct((R, T), F32))(flog, df)


def _chunk_probs(qc, kb, bias, c, col):
    z = _dot_nt(qc, kb) * SCALE + bias
    z = jnp.where((c - (LEFT_CHUNKS + 1)) * CHUNK + col >= jnp.maximum(0, (c - LEFT_CHUNKS) * CHUNK), z, NEG)
    p = jnp.exp(z - jnp.max(z, axis=1, keepdims=True))
    return p, jnp.sum(p, axis=1, keepdims=True)


def _chunk_specs(T, n=CHUNK_HEADS_PER_STEP):
    return (_head_spec(T, heads=n), _head_spec(T, rows=T + BAND_PAD, heads=n),
            _head_spec(T, rows=CHUNK, width=BAND, heads=n))


def _chunk_fwd(q, kp, vp, bias, name, carry=None):
    H, T, _ = q.shape
    nc = T // CHUNK

    def body(q_ref, kp_ref, vp_ref, bias_ref, o_ref):
        _, col = _iota2((CHUNK, BAND))

        def chunk(c, _):
            t0 = pl.multiple_of(c * CHUNK, CHUNK)
            for h in HEADS:
                qc = q_ref[h, pl.ds(t0, CHUNK), :].astype(BF16)
                kb = kp_ref[h, pl.ds(t0, BAND), :].astype(BF16)
                vb = vp_ref[h, pl.ds(t0, BAND), :].astype(BF16)
                p, l = _chunk_probs(qc, kb, bias_ref[h], c, col)
                o = jnp.dot(p.astype(BF16), vb, preferred_element_type=F32) / l
                o_ref[h, pl.ds(t0, CHUNK), :] = o.astype(BF16)
            return 0

        lax.fori_loop(0, nc, chunk, 0)

    q_spec, kp_spec, b_spec = _chunk_specs(T, HEADS_PER_STEP)
    return _pcall_carrying(body, carry, name=name, grid=(H // HEADS_PER_STEP,),
                           in_specs=[q_spec, kp_spec, kp_spec, b_spec], out_specs=q_spec,
                           out_shape=jax.ShapeDtypeStruct((H, T, HEAD_DIM), BF16))(q, kp, vp, bias)


def _chunk_bwd(q, kp, vp, bias, do, name, carry=None):
    H, T, _ = q.shape
    nc = T // CHUNK

    def body(q_ref, kp_ref, vp_ref, bias_ref, do_ref, dq_ref, dk_out, dv_out, db_ref, dkp_ref, dvp_ref):
        _, col = _iota2((CHUNK, BAND))
        dkp_ref[...] = jnp.zeros_like(dkp_ref)
        dvp_ref[...] = jnp.zeros_like(dvp_ref)
        db_ref[...] = jnp.zeros_like(db_ref)

        def chunk(c, _):
            t0 = pl.multiple_of(c * CHUNK, CHUNK)
            for h in CHUNK_HEADS:
                qc = q_ref[h, pl.ds(t0, CHUNK), :].astype(BF16)
                kb = kp_ref[h, pl.ds(t0, BAND), :].astype(BF16)
                vb = vp_ref[h, pl.ds(t0, BAND), :].astype(BF16)
                dob = do_ref[h, pl.ds(t0, CHUNK), :].astype(BF16)
                p, l = _chunk_probs(qc, kb, bias_ref[h], c, col)
                p = p / l
                dp = _dot_nt(dob, vb)
                dz = p * (dp - jnp.sum(p * dp, axis=1, keepdims=True))
                dzb = dz.astype(BF16)
                dq = jnp.dot(dzb, kb, preferred_element_type=F32) * SCALE
                dq_ref[h, pl.ds(t0, CHUNK), :] = dq.astype(BF16)
                dkp_ref[h, pl.ds(t0, BAND), :] += _dot_tn(dzb, qc) * SCALE
                dvp_ref[h, pl.ds(t0, BAND), :] += _dot_tn(p.astype(BF16), dob)
                db_ref[h] += dz
            return 0

        lax.fori_loop(0, nc, chunk, 0)
        dk_out[...] = dkp_ref[:, BAND_PAD:, :].astype(BF16)
        dv_out[...] = dvp_ref[:, BAND_PAD:, :].astype(BF16)

    q_spec, kp_spec, b_spec = _chunk_specs(T)
    acc = pltpu.VMEM((CHUNK_HEADS_PER_STEP, T + BAND_PAD, HEAD_DIM), F32)
    return _pcall_carrying(body, carry, name=name, grid=(H // CHUNK_HEADS_PER_STEP,),
                           in_specs=[q_spec, kp_spec, kp_spec, b_spec, q_spec],
                           out_specs=[q_spec, q_spec, q_spec, b_spec],
                           out_shape=[jax.ShapeDtypeStruct((H, T, HEAD_DIM), BF16)] * 3
                           + [jax.ShapeDtypeStruct((H, CHUNK, BAND), F32)],
                           scratch_shapes=[acc, acc])(q, kp, vp, bias, do)


HBM_SPEC = pl.BlockSpec(memory_space=pltpu.HBM)


def _position():
    return lax.axis_index("x"), lax.axis_index("y"), lax.axis_index("c")


def _other_chips(x, y):
    chips = [(1 - x, y), (x, 1 - y), (1 - x, 1 - y)]
    return [(chip, 2 * chip[0] + chip[1]) for chip in chips]


def _remote_copy(src, dst, send_sems, recv_sems, k, to):
    return pltpu.make_async_remote_copy(src_ref=src, dst_ref=dst, send_sem=send_sems.at[k], recv_sem=recv_sems.at[k],
                                        device_id=to, device_id_type=MESH)


def _place_own(w, chip, name):
    _, r, c = w.shape
    tr = _pick(r, 256, 16)

    def body(chip_ref, w_ref, *out_refs):
        for l, o_ref in enumerate(out_refs):
            o_ref[...] = w_ref[l].astype(BF16)

    grid_spec = pltpu.PrefetchScalarGridSpec(
        num_scalar_prefetch=1, grid=(r // tr,), in_specs=[pl.BlockSpec((DEPTH, tr, c), lambda i, ch: (0, i, 0))],
        out_specs=[pl.BlockSpec((None, tr, c), lambda i, ch: (ch[0], i, 0))] * DEPTH)
    return _pcall(body, name=name, grid_spec=grid_spec,
                  out_shape=[jax.ShapeDtypeStruct((N_CHIPS, r, c), BF16)] * DEPTH)(chip, w)


def _gather_over_ici(srcs, bufs, new):
    x, y, c = _position()
    me = 2 * x + y
    return [(b.at[me, c], b.at[me, c], (*chip, c)) for b in bufs for chip, _ in _other_chips(x, y)]


def _gather_to_sibling(srcs, bufs, new):
    x, y, c = _position()
    return [(b.at[idx, c], b.at[idx, c], (x, y, 1 - c)) for b in bufs for _, idx in _other_chips(x, y)]


def _gather_layer(bufs, name):
    n = len(bufs)

    def body(*refs):
        dst = refs[n:2 * n]
        send_sems, recv_sems = refs[2 * n:]
        x, y, c = _position()
        me = 2 * x + y
        sibling = (x, y, 1 - c)
        others = _other_chips(x, y)
        first = [_remote_copy(dst[i].at[me, c], dst[i].at[me, c], send_sems, recv_sems, 3 * i + k, (*chip, c))
                 for i in range(n) for k, (chip, _) in enumerate(others)]
        for cp in first:
            cp.start()
        passed = []
        for i in range(n):
            for k, (_, idx) in enumerate(others):
                landed = dst[i].at[idx, c]
                _remote_copy(landed, landed, send_sems, recv_sems, 3 * i + k, sibling).wait_recv()
                passed.append(_remote_copy(landed, landed, send_sems, recv_sems, 3 * (n + i) + k, sibling))
                passed[-1].start()
        for i in range(n):
            for k, (_, idx) in enumerate(others):
                from_sibling = dst[i].at[idx, 1 - c]
                _remote_copy(from_sibling, from_sibling, send_sems, recv_sems, 3 * (n + i) + k, sibling).wait_recv()
        for cp in first + passed:
            cp.wait_send()

    return _pcall(
        body, name=name, in_specs=[HBM_SPEC] * n, out_specs=[HBM_SPEC] * n,
        out_shape=[jax.ShapeDtypeStruct(b.shape, b.dtype) for b in bufs],
        scratch_shapes=[pltpu.SemaphoreType.DMA((6 * n,)), pltpu.SemaphoreType.DMA((6 * n,))],
        input_output_aliases={i: i for i in range(n)},
    )(*bufs)


def _send_other_half(parts, name):
    n = len(parts)

    def body(*refs):
        src, got = refs[:n], refs[n:2 * n]
        send_sems, recv_sems = refs[2 * n:]
        x, y, c = _position()
        copies = [_remote_copy(src[i].at[1 - c], got[i], send_sems, recv_sems, i, (x, y, 1 - c)) for i in range(n)]
        for cp in copies:
            cp.start()
        for cp in copies:
            cp.wait()

    return _pcall(
        body, name=name, in_specs=[HBM_SPEC] * n, out_specs=[HBM_SPEC] * n,
        out_shape=[jax.ShapeDtypeStruct(p.shape[1:], p.dtype) for p in parts],
        scratch_shapes=[pltpu.SemaphoreType.DMA((n,)), pltpu.SemaphoreType.DMA((n,))],
    )(*parts)


def _scatter_chips(parts, name):
    n = len(parts)

    def body(*refs):
        src, dst = refs[:n], refs[n:2 * n]
        send_sems, recv_sems = refs[2 * n:]
        x, y, c = _position()
        others = _other_chips(x, y)
        sends = [_remote_copy(src[i].at[idx], dst[i].at[k], send_sems, recv_sems, 3 * i + k, (*chip, c))
                 for i in range(n) for k, (chip, idx) in enumerate(others)]
        for cp in sends:
            cp.start()
        for cp in sends:
            cp.wait()

    return _pcall(
        body, name=name, in_specs=[HBM_SPEC] * n, out_specs=[HBM_SPEC] * n,
        out_shape=[jax.ShapeDtypeStruct((3,) + p.shape[1:], p.dtype) for p in parts],
        scratch_shapes=[pltpu.SemaphoreType.DMA((3 * n,)), pltpu.SemaphoreType.DMA((3 * n,))],
    )(*parts)


def _swap_with_sibling(arrs, name):
    n = len(arrs)

    def body(*refs):
        src, dst = refs[:n], refs[n:2 * n]
        send_sems, recv_sems = refs[2 * n:]
        x, y, c = _position()
        copies = [_remote_copy(src[i], dst[i], send_sems, recv_sems, i, (x, y, 1 - c)) for i in range(n)]
        for cp in copies:
            cp.start()
        for cp in copies:
            cp.wait()

    return _pcall(
        body, name=name, in_specs=[HBM_SPEC] * n, out_specs=[HBM_SPEC] * n,
        out_shape=[jax.ShapeDtypeStruct(a.shape, a.dtype) for a in arrs],
        scratch_shapes=[pltpu.SemaphoreType.DMA((n,)), pltpu.SemaphoreType.DMA((n,))],
    )(*arrs)


def _allreduce_small(v, name):
    R, C = v.shape

    def body(v_ref, o_ref, slots, send_sems, recv_sems):
        x, y, c = _position()
        me = 4 * x + 2 * y + c
        slots[0] = v_ref[...]
        sends = []
        for r in range(1, 8):
            fx, fy, fc = (r >> 2) & 1, (r >> 1) & 1, r & 1
            to = (x ^ fx, y ^ fy, c ^ fc)
            cp = pltpu.make_async_remote_copy(src_ref=v_ref, dst_ref=slots.at[r], send_sem=send_sems.at[r - 1],
                                              recv_sem=recv_sems.at[r - 1], device_id=to, device_id_type=MESH)
            cp.start()
            sends.append(cp)
        for cp in sends:
            cp.wait()
        acc = slots[me]
        for d in range(1, 8):
            acc = acc + slots[d ^ me]
        o_ref[...] = acc

    vmem = pl.BlockSpec(memory_space=pltpu.VMEM)
    return _pcall(
        body, name=name, in_specs=[vmem], out_specs=vmem, out_shape=jax.ShapeDtypeStruct((R, C), F32),
        scratch_shapes=[pltpu.VMEM((8, R, C), F32), pltpu.SemaphoreType.DMA((7,)), pltpu.SemaphoreType.DMA((7,))],
    )(v)


def _add_pair(which, both, got, name):
    _, N, R, C = both.shape
    tr = _pick(R, 512, 16)

    def body(which_ref, a_ref, b_ref, o_ref):
        o_ref[...] = (a_ref[...].astype(F32) + b_ref[...].astype(F32)).astype(BF16)

    spec = pl.BlockSpec((None, tr, C), lambda n, i, w: (n, i, 0))
    grid_spec = pltpu.PrefetchScalarGridSpec(
        num_scalar_prefetch=1, grid=(N, R // tr),
        in_specs=[pl.BlockSpec((None, None, tr, C), lambda n, i, w: (w[0], n, i, 0)), spec], out_specs=spec)
    return _pcall(body, name=name, grid_spec=grid_spec,
                  out_shape=jax.ShapeDtypeStruct((N, R, C), BF16))(which, both, got)


def _sum_chips(which, own, others, name):
    N, R, C = others.shape
    tr = _pick(R, 512, 16)

    def body(which_ref, own_ref, p_ref, o_ref):
        acc = own_ref[...].astype(F32)
        for n in range(N):
            acc = acc + p_ref[n].astype(F32)
        o_ref[...] = acc

    grid_spec = pltpu.PrefetchScalarGridSpec(
        num_scalar_prefetch=1, grid=(R // tr,),
        in_specs=[pl.BlockSpec((None, tr, C), lambda i, w: (w[0], i, 0)), pl.BlockSpec((N, tr, C), lambda i, w: (0, i, 0))],
        out_specs=pl.BlockSpec((tr, C), lambda i, w: (i, 0)))
    return _pcall(body, name=name, grid_spec=grid_spec,
                  out_shape=jax.ShapeDtypeStruct((R, C), F32))(which, own, others)


BIG = ("w_ffn1_in", "w_ffn1_out", "w_in", "w_br_sb", "w_br_ch", "w_br_fox", "w_out", "w_ffn2_in", "w_ffn2_out")
ROW_SHARDED = ("w_ffn1_out", "w_out", "w_ffn2_out")
SMALL = ("g_ffn1", "g_mix", "b_in", "rel_bias", "g_ffn2", "g_final")


def _pair_sums(split, tag):
    core = lax.axis_index("c").astype(jnp.int32)[None]
    got = _send_other_half(split, f"grad_split_{tag}")
    return [_add_pair(core, a, b, f"grad_pair_sum_{tag}_{i}") for i, (a, b) in enumerate(zip(split, got))]


def _scatter_plan(srcs, bufs, new):
    x, y, c = _position()
    return [(s.at[idx], d.at[k], (*chip, c)) for s, d in zip(srcs, new) for k, (chip, idx) in enumerate(_other_chips(x, y))]


def _scatter_carry(pair):
    landing = [jax.ShapeDtypeStruct((3,) + p.shape[1:], p.dtype) for p in pair]
    return _Carry(pair, [], landing, 3 * len(pair), _scatter_plan)


def _finish_grads(pair, landed, axes, tag):
    chip = (2 * lax.axis_index("x") + lax.axis_index("y")).astype(jnp.int32)[None]
    mine = [_sum_chips(chip, p, q, f"grad_chip_sum_{tag}_{i}") for i, (p, q) in enumerate(zip(pair, landed))]
    theirs = _swap_with_sibling(mine, f"grad_share_{tag}")
    first = lax.axis_index("c") == 0
    return [jnp.concatenate([jnp.where(first, a, b), jnp.where(first, b, a)], axis=ax)
            for a, b, ax in zip(mine, theirs, axes)]


SMALL_SIZES = {"g_ffn1": DEPTH * D_MODEL, "g_mix": DEPTH * D_MODEL, "b_in": DEPTH * IN_WIDTH,
               "rel_bias": DEPTH * N_REL * H_CH, "g_ffn2": DEPTH * D_MODEL, "g_final": D_MODEL}
SMALL_TOTAL = sum(SMALL_SIZES.values()) + 1
SMALL_ROWS = -(-SMALL_TOTAL // (8 * 128)) * 8


def _pack_small(vals, extra):
    flat = [vals[n].reshape(-1) for n in SMALL] + [extra.reshape(-1)]
    flat.append(jnp.zeros((SMALL_ROWS * 128 - SMALL_TOTAL,), F32))
    return jnp.concatenate(flat).reshape(SMALL_ROWS, 128)


def _unpack_small(pack, shapes):
    flat, out, off = pack.reshape(-1), {}, 0
    for n in SMALL:
        out[n] = flat[off:off + SMALL_SIZES[n]].reshape(shapes[n])
        off += SMALL_SIZES[n]
    return out, flat[off]


def _to_heads(t, n_heads):
    return jnp.transpose(t.reshape(t.shape[0], n_heads, HEAD_DIM), (1, 0, 2)).astype(BF16)


def _from_heads(t):
    return jnp.transpose(t, (1, 0, 2)).reshape(t.shape[1], t.shape[0] * HEAD_DIM)


def _pad_keys(t):
    return jnp.pad(t, ((0, 0), (BAND_PAD, 0), (0, 0)))


DIAGONALS = CHUNK + BAND - 1


def _band_bias(table):
    n_far = (LEFT_CHUNKS + 1) * CHUNK + CHUNK - MAX_REL
    near = table[N_REL - 1 - (DIAGONALS - n_far):N_REL - 1][::-1]
    diag = jnp.concatenate([jnp.broadcast_to(table[N_REL - 1], (n_far, H_CH)), near], axis=0).T
    diag = jnp.pad(diag, ((0, 0), (0, 1)))
    skew = jnp.tile(diag, (1, CHUNK))[:, :CHUNK * DIAGONALS].reshape(H_CH, CHUNK, DIAGONALS)
    return skew[:, :, CHUNK - 1:]


def _pad_w_in(w):
    qkv, f, gates = w[:, :QKV_WIDTH], w[:, QKV_WIDTH:QKV_WIDTH + H_FOX], w[:, QKV_WIDTH + H_FOX:]
    return jnp.concatenate([qkv, gates, f, jnp.zeros((w.shape[0], F_PAD - H_FOX), w.dtype)], axis=1)


def _unpad_w_in(w):
    return jnp.concatenate([w[:, :QKV_WIDTH], w[:, QKV_WIDTH + GATE_WIDTH:QKV_WIDTH + GATE_WIDTH + H_FOX],
                            w[:, QKV_WIDTH:QKV_WIDTH + GATE_WIDTH]], axis=1)


QKV_SPLITS = np.cumsum([0, W_SB, W_SB, W_SB, W_CH, W_CH, W_CH, W_FOX, W_FOX, W_FOX])


def _by_chip_rows(g):
    return g.reshape(2, N_CHIPS, g.shape[1] // N_CHIPS, g.shape[2])


def _ffn_fwd(x, g, w_in, w_out, tag, carries):
    h, gate, up, a = _ffn_in_swiglu(x, g, w_in, f"{tag}_in", _carry_of(carries, "in"))
    y = _mm(a, w_out, mode="nn", name=f"{tag}_out", alpha=0.5, res=x, gathered="row",
            carry=_carry_of(carries, "out"))
    return y, (h, gate, up, a)


def _ffn_bwd(x, g, w_in, w_out, saved, dy, tag, carries):
    h, gate, up, a = saved
    da = _mm(dy, w_out, mode="nt", name=f"{tag}_da", alpha=0.5, gathered="row", out_dtype=BF16,
             carry=_carry_of(carries, "da"))
    dw_out = _mm(a, dy, mode="tn", name=f"{tag}_dwout", alpha=0.5, tm_cap=1408, out_dtype=BF16, out_split="row",
                 carry=_carry_of(carries, "dwout"))
    dgu = _swiglu_bwd(gate, up, da, f"{tag}_dact")
    dw_in = _mm(h, dgu, mode="tn", name=f"{tag}_dwin", tm_cap=512, out_dtype=BF16, out_split="col",
                carry=_carry_of(carries, "dwin"))
    dx, dg = _mm(dgu, w_in, mode="nt", name=f"{tag}_dh", gathered="col", tn_cap=1024, carry=_carry_of(carries, "dh"),
                 norm_bwd=(x, g, dy))
    return dx, dg, dw_in, _by_chip_rows(dw_out)


def _mixer_fwd(x, lw, tag, carries):
    T = x.shape[0]
    h = _rmsnorm_fwd(x, lw["g_mix"], f"{tag}_norm")
    p = _mm(h, lw["w_in"], mode="nn", name=f"{tag}_proj", bias=lw["b_in"], tn_cap=896, out_dtype=BF16,
            carry=_carry_of(carries, "proj"))
    parts = [p[:, QKV_SPLITS[i]:QKV_SPLITS[i + 1]] for i in range(9)]
    qa, ka, va = (_to_heads(t, H_SB) for t in parts[0:3])
    qb, kb, vb = (_to_heads(t, H_CH) for t in parts[3:6])
    qc, kc, vc = (_to_heads(t, H_FOX) for t in parts[6:9])
    flog = _mm(lw["w_forget_t"], h, mode="nt", name=f"{tag}_flog")[:8] + lw["b_forget"]
    fcum = _forget_cumsum(flog, f"{tag}_fcum")[:H_FOX]
    fq, fk = fcum.reshape(H_FOX, T, 1), fcum.reshape(H_FOX, T // QB, QB)
    kbp, vbp = _pad_keys(kb), _pad_keys(vb)
    oa = _sb_fwd(qa, ka, va, f"{tag}_sb", _carry_of(carries, "sb"))
    ob = _chunk_fwd(qb, kbp, vbp, lw["bias"], f"{tag}_ch", _carry_of(carries, "ch"))
    oc, lse = _fox_fwd(qc, kc, vc, fq, fk, f"{tag}_fox", _carry_of(carries, "fox"))
    oam, obm, ocm = _from_heads(oa), _from_heads(ob), _from_heads(oc)
    ya, yb, yc, merged = _branches_fwd(p, (oam, obm, ocm), (lw["w_br_sb"], lw["w_br_ch"], lw["w_br_fox"]),
                                       f"{tag}_branches")
    y = _mm(merged, lw["w_out"], mode="nn", name=f"{tag}_out", res=x, gathered="row")
    saved = (h, p, (qa, ka, va), (qb, kbp, vbp), (qc, kc, vc), flog, fq, fk, lse, (oam, obm, ocm),
             (ya, yb, yc), merged)
    return y, saved


def _mixer_bwd(x, lw, saved, dy, tag, carries):
    (h, p, (qa, ka, va), (qb, kbp, vbp), (qc, kc, vc), flog, fq, fk, lse, (oam, obm, ocm),
     (ya, yb, yc), merged) = saved
    T = x.shape[0]
    grads = {}
    dmerged = _mm(dy, lw["w_out"], mode="nt", name=f"{tag}_dmerged", gathered="row",
                  carry=_carry_of(carries, "dmerged"))
    grads["w_out"] = _by_chip_rows(_mm(merged, dy, mode="tn", name=f"{tag}_dwout", tm_cap=1024, out_dtype=BF16,
                                       out_split="row"))
    (dya, dyb, dyc), dgates, (doa, dob, doc) = _branches_bwd(
        p, (ya, yb, yc), dmerged, (lw["w_br_sb"], lw["w_br_ch"], lw["w_br_fox"]), f"{tag}_dbranches")
    by_chip = dict(mode="tn", tm_cap=512, out_dtype=BF16, out_split="col")
    grads["w_br_sb"] = _mm(oam, dya, name=f"{tag}_dwbra", **by_chip)
    grads["w_br_ch"] = _mm(obm, dyb, name=f"{tag}_dwbrb", **by_chip)
    grads["w_br_fox"] = _mm(ocm, dyc, name=f"{tag}_dwbrc", **by_chip)
    dqa, dka, dva = _sb_bwd(qa, ka, va, _to_heads(doa, H_SB), f"{tag}_dsb", _carry_of(carries, "dsb"))
    dqb, dkb, dvb, dbias = _chunk_bwd(qb, kbp, vbp, lw["bias"], _to_heads(dob, H_CH), f"{tag}_dch",
                                      _carry_of(carries, "dch"))
    dqc, dkc, dvc, dfk = _fox_bwd(qc, kc, vc, fq, fk, lse, _to_heads(doc, H_FOX), f"{tag}_dfox",
                                  _carry_of(carries, "dfox"))
    dflog = _forget_cumsum_bwd(flog, jnp.pad(dfk.reshape(H_FOX, T), ((0, 8 - H_FOX), (0, 0))), f"{tag}_dfcum")
    dqkv = [_from_heads(t) for t in (dqa, dka, dva, dqb, dkb, dvb, dqc, dkc, dvc)]
    dforget = jnp.pad(dflog[:H_FOX].T, ((0, 0), (0, F_PAD - H_FOX))).astype(BF16)
    dpb = jnp.concatenate(dqkv + list(dgates) + [dforget], axis=1)
    grads["b_in"] = _colsum(dpb, f"{tag}_dbin")
    dw_in =_unpad_w_in(_mm(h, dpb, mode="tn", name=f"{tag}_dwin", tm_cap=512, tn_cap=896, out_dtype=BF16))
    grads["w_in"] = jnp.transpose(dw_in.reshape(2, D_MODEL // 2, N_CHIPS, IN_WIDTH // N_CHIPS), (0, 2, 1, 3))
    dx, grads["g_mix"] = _mm(dpb, lw["w_in"], mode="nt", name=f"{tag}_dh", tk_cap=896, tn_cap=1024,
                             norm_bwd=(x, lw["g_mix"], dy))
    grads["rel_bias"] = lw["bias_vjp"](dbias)[0]
    return dx, grads


def kernel(x, g_ffn1, w_ffn1_in, w_ffn1_out, g_mix, w_in, b_in, rel_bias, w_br_sb, w_br_ch, w_br_fox, w_out, g_ffn2, w_ffn2_in, w_ffn2_out, g_final, loss_target, m_g_ffn1, m_w_ffn1_in, m_w_ffn1_out, m_g_mix, m_w_in, m_b_in, m_rel_bias, m_w_br_sb, m_w_br_ch, m_w_br_fox, m_w_out, m_g_ffn2, m_w_ffn2_in, m_w_ffn2_out, m_g_final, v_g_ffn1, v_w_ffn1_in, v_w_ffn1_out, v_g_mix, v_w_in, v_b_in, v_rel_bias, v_w_br_sb, v_w_br_ch, v_w_br_fox, v_w_out, v_g_ffn2, v_w_ffn2_in, v_w_ffn2_out, v_g_final):
    names = ("g_ffn1", "w_ffn1_in", "w_ffn1_out", "g_mix", "w_in", "b_in", "rel_bias", "w_br_sb", "w_br_ch",
             "w_br_fox", "w_out", "g_ffn2", "w_ffn2_in", "w_ffn2_out", "g_final")
    w = dict(zip(names, (g_ffn1, w_ffn1_in, w_ffn1_out, g_mix, w_in, b_in, rel_bias, w_br_sb, w_br_ch,
                         w_br_fox, w_out, g_ffn2, w_ffn2_in, w_ffn2_out, g_final)))
    m = dict(zip(names, (m_g_ffn1, m_w_ffn1_in, m_w_ffn1_out, m_g_mix, m_w_in, m_b_in, m_rel_bias, m_w_br_sb,
                         m_w_br_ch, m_w_br_fox, m_w_out, m_g_ffn2, m_w_ffn2_in, m_w_ffn2_out, m_g_final)))
    v = dict(zip(names, (v_g_ffn1, v_w_ffn1_in, v_w_ffn1_out, v_g_mix, v_w_in, v_b_in, v_rel_bias, v_w_br_sb,
                         v_w_br_ch, v_w_br_fox, v_w_out, v_g_ffn2, v_w_ffn2_in, v_w_ffn2_out, v_g_final)))
    xs = x[0]
    tgt = loss_target[0]

    chip = (2 * lax.axis_index("x") + lax.axis_index("y")).astype(jnp.int32)[None]
    placed = [_place_own(w[n], chip, f"place_{n}") for n in BIG]
    bufs = [[p[l].reshape(N_CHIPS, 2, p[l].shape[1] // 2, p[l].shape[2]) for p in placed] for l in range(DEPTH)]
    padded_w_in = {}

    def layer_weights(l):
        lw = {n: b.reshape((N_CHIPS,) + w[n].shape[1:]) for n, b in zip(BIG, bufs[l])}
        if l not in padded_w_in:
            whole = jnp.concatenate([lw["w_in"][j] for j in range(N_CHIPS)], axis=1)
            forget_t = jnp.pad(whole[:, QKV_WIDTH:QKV_WIDTH + H_FOX].T, ((0, F_PAD - H_FOX), (0, 0)))
            padded_w_in[l] = (_pad_w_in(whole), forget_t)
        lw["w_in"], lw["w_forget_t"] = padded_w_in[l]
        lw["b_forget"] = jnp.pad(w["b_in"][l][QKV_WIDTH:QKV_WIDTH + H_FOX], (0, 8 - H_FOX))[:, None]
        lw["b_in"] = _pad_w_in(w["b_in"][l][None, :])
        lw["bias"], lw["bias_vjp"] = jax.vjp(_band_bias, w["rel_bias"][l])
        for n in ("g_ffn1", "g_mix", "g_ffn2"):
            lw[n] = w[n][l][None, :]
        return lw

    def landed_in(l, idx):
        def done(carry):
            for i, b in zip(idx, carry.out[0]):
                bufs[l][i] = b
        return done

    def over_ici(l, idx):
        return lambda: _Carry([], [bufs[l][i] for i in idx], [], 3 * len(idx), _gather_over_ici, landed_in(l, idx))

    def to_sibling(l, idx):
        return lambda: _Carry([], [bufs[l][i] for i in idx], [], 3 * len(idx), _gather_to_sibling, landed_in(l, idx))

    def ffn_fwd(x_in, lw, which, tag, carries):
        return _ffn_fwd(x_in, lw[f"g_{which}"], lw[f"w_{which}_in"], lw[f"w_{which}_out"], tag, carries)

    def ffn_bwd(x_in, lw, which, saved_acts, dy, tag, carries):
        return _ffn_bwd(x_in, lw[f"g_{which}"], lw[f"w_{which}_in"], lw[f"w_{which}_out"], saved_acts, dy, tag,
                        carries)

    FFN1, W_IN, MIXER_REST, FFN2_IN, FFN2_OUT = (0, 1), (2,), (3, 4, 5, 6), (7,), (8,)
    first = FFN1 + W_IN
    for i, b in zip(first, _gather_layer([bufs[0][i] for i in first], "gather_l0")):
        bufs[0][i] = b
    fwd_carries = [
        {"ffn1": {"in": [over_ici(0, MIXER_REST)], "out": [to_sibling(0, MIXER_REST), over_ici(0, FFN2_OUT)]},
         "mix": {"proj": [to_sibling(0, FFN2_OUT), over_ici(1, MIXER_REST)],
                 "sb": [over_ici(0, FFN2_IN), over_ici(1, FFN2_OUT)],
                 "ch": [to_sibling(0, FFN2_IN), over_ici(1, FFN1)],
                 "fox": [over_ici(1, W_IN)]},
         "ffn2": {"in": [to_sibling(1, MIXER_REST + FFN2_OUT + FFN1 + W_IN)]}},
        {"ffn1": {}, "mix": {"sb": [over_ici(1, FFN2_IN)], "ch": [to_sibling(1, FFN2_IN)]}, "ffn2": {}},
    ]

    saved = []
    act = xs
    for l in range(DEPTH):
        x0 = act
        x1, s1 = ffn_fwd(x0, layer_weights(l), "ffn1", f"l{l}_ffn1", fwd_carries[l]["ffn1"])
        x2, s2 = _mixer_fwd(x1, layer_weights(l), f"l{l}_mix", fwd_carries[l]["mix"])
        act, s3 = ffn_fwd(x2, layer_weights(l), "ffn2", f"l{l}_ffn2", fwd_carries[l]["ffn2"])
        saved.append((x0, s1, x1, s2, x2, s3))
    layers = [layer_weights(l) for l in range(DEPTH)]

    dact, dg_final, loss_row = _final_loss(act, w["g_final"][None, :], tgt, "final_loss")

    big_grads = {n: [None] * DEPTH for n in BIG}
    small_grads = {n: [None] * DEPTH for n in ("g_ffn1", "g_mix", "b_in", "rel_bias", "g_ffn2")}
    pair = [[None] * len(BIG) for _ in range(DEPTH)]
    landed = [[None] * len(BIG) for _ in range(DEPTH)]

    def pair_up(l, idx, tag):
        for i, p in zip(idx, _pair_sums([big_grads[BIG[i]][l] for i in idx], tag)):
            pair[l][i] = p

    core = lax.axis_index("c").astype(jnp.int32)[None]

    def to_sibling_half(l, idx, tag):
        def plan(srcs, bufs, new):
            x, y, c = _position()
            return [(s.at[1 - c], d, (x, y, 1 - c)) for s, d in zip(srcs, new)]
        def done(carry):
            for j, (i, got) in enumerate(zip(idx, carry.out[1])):
                pair[l][i] = _add_pair(core, big_grads[BIG[i]][l], got, f"grad_pair_sum_{tag}_{j}")
        def make():
            split = [big_grads[BIG[i]][l] for i in idx]
            return _Carry(split, [], [jax.ShapeDtypeStruct(s.shape[1:], s.dtype) for s in split], len(idx), plan, done)
        return make

    def exchange(l, idx):
        def done(carry):
            for i, a in zip(idx, carry.out[1]):
                landed[l][i] = a
        def make():
            carry = _scatter_carry([pair[l][i] for i in idx])
            carry.done = done
            return carry
        return make

    def exchange_one_way(l, i, k):
        def plan(srcs, bufs, new):
            x, y, c = _position()
            chip_k, idx_k = _other_chips(x, y)[k]
            return [(srcs[0].at[idx_k], bufs[0].at[k], (*chip_k, c))]
        def done(carry):
            landed[l][i] = carry.out[0][0]
        def make():
            if landed[l][i] is None:
                landed[l][i] = lax.empty((3,) + pair[l][i].shape[1:], BF16)
            return _Carry([pair[l][i]], [landed[l][i]], [], 1, plan, done)
        return make

    bwd_carries = [
        {"ffn2": {},
         "mix": {"dmerged": [to_sibling_half(0, FFN2_IN + FFN2_OUT, "l0_ffn2")],
                 "dsb": [exchange(1, FFN1 + W_IN)], "dch": [exchange(1, MIXER_REST + FFN2_IN + FFN2_OUT)],
                 "dfox": [exchange(0, FFN2_IN + FFN2_OUT)]},
         "ffn1": {"da": [exchange(0, MIXER_REST)], "dwout": [exchange_one_way(0, 2, 0)],
                  "dwin": [exchange_one_way(0, 2, 1)], "dh": [exchange_one_way(0, 2, 2)]}},
        {"ffn2": {}, "mix": {},
         "ffn1": {"da": [to_sibling_half(1, W_IN + MIXER_REST + FFN2_IN + FFN2_OUT, "l1_rest")]}},
    ]
    for l in reversed(range(DEPTH)):
        lw = layers[l]
        x0, s1, x1, s2, x2, s3 = saved[l]
        dx2, small_grads["g_ffn2"][l], big_grads["w_ffn2_in"][l], big_grads["w_ffn2_out"][l] = ffn_bwd(
            x2, lw, "ffn2", s3, dact, f"l{l}_ffn2", bwd_carries[l]["ffn2"])
        dx1, mg = _mixer_bwd(x1, lw, s2, dx2, f"l{l}_mix", bwd_carries[l]["mix"])
        for n in ("w_in", "w_br_sb", "w_br_ch", "w_br_fox", "w_out"):
            big_grads[n][l] = mg[n]
        small_grads["b_in"][l] = _unpad_w_in(mg["b_in"])[0]
        small_grads["g_mix"][l] = mg["g_mix"][0]
        small_grads["rel_bias"][l] = mg["rel_bias"]
        if l == 0:
            pair_up(0, W_IN + MIXER_REST, "l0_mix")
        dact, dg1, big_grads["w_ffn1_in"][l], big_grads["w_ffn1_out"][l] = ffn_bwd(
            x0, lw, "ffn1", s1, dx1, f"l{l}_ffn1", bwd_carries[l]["ffn1"])
        small_grads["g_ffn1"][l] = dg1[0]
        small_grads["g_ffn2"][l] = small_grads["g_ffn2"][l][0]
        if l == 1:
            pair_up(1, FFN1, "l1_ffn1")
    grad_x = dact[None]
    pair_up(0, FFN1, "l0_ffn1")
    for i, a in zip(FFN1, _scatter_chips([pair[0][i] for i in FFN1], "grad_scatter_l0")):
        landed[0][i] = a

    small = {n: jnp.stack(small_grads[n]) for n in small_grads}
    small["g_final"] = dg_final[0]
    small_sum, loss = _unpack_small(_allreduce_small(_pack_small(small, loss_row[0, :1]), "allreduce_small"),
                                    {n: w[n].shape for n in SMALL})

    axes = [1 if n in ROW_SHARDED else 0 for n in BIG] * DEPTH
    summed = _finish_grads(pair[0] + pair[1], landed[0] + landed[1], axes, "all")
    grads = {n: jnp.stack([summed[i], summed[len(BIG) + i]]) for i, n in enumerate(BIG)}
    grads.update(small_sum)

    delta, new_m, new_v = {}, {}, {}
    for n in BIG:
        shape = w[n].shape
        flat = lambda t: t.reshape(-1, shape[-1])
        d, nm, nv = _adamw(flat(w[n]), flat(grads[n]), flat(m[n]), flat(v[n]), f"adamw_{n}")
        delta[n], new_m[n], new_v[n] = d.reshape(shape), nm.reshape(shape), nv.reshape(shape)
    zero = jnp.zeros((1,), F32)
    sd, sm, sv = _adamw(_pack_small(w, zero), _pack_small(grads, zero), _pack_small(m, zero), _pack_small(v, zero),
                        "adamw_small")
    shapes = {n: w[n].shape for n in SMALL}
    for dst, src in ((delta, sd), (new_m, sm), (new_v, sv)):
        dst.update(_unpack_small(src, shapes)[0])

    return (loss, grad_x, *[grads[n] for n in names], *[delta[n] for n in names],
            *[new_m[n] for n in names], *[new_v[n] for n in names])
```

```python
import functools

import numpy as np
import jax
import jax.numpy as jnp
from jax import lax
from jax.experimental import pallas as pl
from jax.experimental.pallas import tpu as pltpu

F32 = jnp.float32
BF16 = jnp.bfloat16

D_MODEL = 1024
DEPTH = 2
CHUNK = 64
HEAD_DIM = 64
H_SB, H_CH, H_FOX = 4, 8, 4
W_SB, W_CH, W_FOX = H_SB * HEAD_DIM, H_CH * HEAD_DIM, H_FOX * HEAD_DIM
LEFT_CHUNKS = 8
MAX_REL = 128
N_REL = 2 * MAX_REL + 1
D_FF = 2816
QKV_WIDTH = 3 * (W_SB + W_CH + W_FOX)
GATE_WIDTH = 3 * D_MODEL
IN_WIDTH = QKV_WIDTH + H_FOX + GATE_WIDTH
F_PAD = 128
P_WIDTH = QKV_WIDTH + GATE_WIDTH + F_PAD
RMS_EPS = 1e-6
NEG = -1e30
SCALE = HEAD_DIM ** -0.5
QB = 256
BAND = (LEFT_CHUNKS + 2) * CHUNK
BAND_PAD = BAND - CHUNK

ADAM_LR, ADAM_B1, ADAM_B2, ADAM_EPS, ADAM_WD, ADAM_STEP = 0.001, 0.9, 0.999, 1e-08, 0.01, 10

N_CHIPS = 4
VMEM_BUDGET = 52 * 1024 * 1024

NT = (((1,), (1,)), ((), ()))
TN = (((0,), (0,)), ((), ()))
NN = (((1,), (0,)), ((), ()))
MESH = pl.DeviceIdType.MESH


def _pcall(body, **kw):
    return pl.pallas_call(body, **kw)


class _Carry:
    def __init__(self, srcs, bufs, new, count, plan, done=None):
        self.srcs, self.bufs, self.new, self.count, self.plan = list(srcs), list(bufs), list(new), count, plan
        self.out, self.done = None, done

    @staticmethod
    def join(parts):
        parts = [p for p in parts if p is not None]
        if not parts:
            return None

        def plan(srcs, bufs, new):
            copies, s, b, n = [], 0, 0, 0
            for p in parts:
                copies += p.plan(srcs[s:s + len(p.srcs)], bufs[b:b + len(p.bufs)], new[n:n + len(p.new)])
                s, b, n = s + len(p.srcs), b + len(p.bufs), n + len(p.new)
            return copies

        whole = _Carry(sum((p.srcs for p in parts), []), sum((p.bufs for p in parts), []),
                       sum((p.new for p in parts), []), sum(p.count for p in parts), plan)
        whole.parts = parts
        return whole

    def share_out(self):
        b, n = 0, 0
        for p in getattr(self, "parts", []):
            p.out = (self.out[0][b:b + len(p.bufs)], self.out[1][n:n + len(p.new)])
            b, n = b + len(p.bufs), n + len(p.new)
            p.share_out()
        if self.done is not None:
            self.done(self)


def _carry_of(carries, key):
    return _Carry.join([make() for make in carries.get(key, [])])


def _pcall_carrying(body, carry, **kw):
    if carry is None:
        return _pcall(body, **kw)
    in_specs = list(kw.pop("in_specs"))
    out_specs, out_shape = kw.pop("out_specs"), kw.pop("out_shape")
    single = not isinstance(out_shape, (list, tuple))
    out_specs = [out_specs] if single else list(out_specs)
    out_shape = [out_shape] if single else list(out_shape)
    scratch = list(kw.pop("scratch_shapes", []))
    grid = tuple(kw.get("grid", ()))
    n_in, n_out, n_scr = len(in_specs), len(out_specs), len(scratch)
    ns, nb, nn = len(carry.srcs), len(carry.bufs), len(carry.new)

    def body2(*refs):
        ins, srcs = refs[:n_in], refs[n_in:n_in + ns]
        at = n_in + ns + nb
        outs, bufs, new = refs[at:at + n_out], refs[at + n_out:at + n_out + nb], refs[at + n_out + nb:at + n_out + nb + nn]
        at += n_out + nb + nn
        scr, (send_sems, recv_sems) = refs[at:at + n_scr], refs[at + n_scr:]

        def copies():
            return [_remote_copy(s, d, send_sems, recv_sems, k, to)
                    for k, (s, d, to) in enumerate(carry.plan(srcs, bufs, new))]

        def start():
            for cp in copies():
                cp.start()

        def wait():
            for cp in copies():
                cp.wait()

        if grid:
            ids = [pl.program_id(a) for a in range(len(grid))]
            pl.when(functools.reduce(jnp.logical_and, [i == 0 for i in ids]))(start)
        else:
            start()
        body(*ins, *outs, *scr)
        if grid:
            pl.when(functools.reduce(jnp.logical_and, [i == g - 1 for i, g in zip(ids, grid)]))(wait)
        else:
            wait()

    call = _pcall(
        body2, in_specs=in_specs + [HBM_SPEC] * (ns + nb), out_specs=out_specs + [HBM_SPEC] * (nb + nn),
        out_shape=out_shape + [jax.ShapeDtypeStruct(b.shape, b.dtype) for b in carry.bufs] + carry.new,
        scratch_shapes=scratch + [pltpu.SemaphoreType.DMA((carry.count,)), pltpu.SemaphoreType.DMA((carry.count,))],
        input_output_aliases={n_in + ns + j: n_out + j for j in range(nb)}, **kw)

    def apply(*args):
        res = call(*args, *carry.srcs, *carry.bufs)
        carry.out = (list(res[n_out:n_out + nb]), list(res[n_out + nb:]))
        carry.share_out()
        return res[0] if single else list(res[:n_out])

    return apply


def _pick(n, cap, mult=128):
    best = None
    d = mult
    while d <= min(n, cap):
        if n % d == 0:
            best = d
        d += mult
    return n if best is None else best


def _nbytes(shape, dtype):
    return int(np.prod(shape)) * jnp.dtype(dtype).itemsize


def _mm(a, b, *, mode, name, out_dtype=F32, alpha=1.0, bias=None, res=None, tm_cap=1024, tn_cap=512,
        tk_cap=2816, gathered=None, out_split=None, carry=None, norm_bwd=None):
    if mode == "tn":
        K, M = a.shape
    else:
        M, K = a.shape
    dn = {"nn": NN, "nt": NT, "tn": TN}[mode]
    b_rows = None
    if gathered is None:
        N = b.shape[0] if mode == "nt" else b.shape[1]
        tn = {None: _pick(N, tn_cap), "col": N // N_CHIPS, "row": N // 2}[out_split]
        if out_split == "col":
            tm_cap = min(tm_cap, M // 2)
        tk = K if K <= tk_cap else _pick(K, tk_cap)
        b_spec = (pl.BlockSpec((tn, tk), lambda i, j, k: (j, k)) if mode == "nt"
                  else pl.BlockSpec((tk, tn), lambda i, j, k: (k, j)))
    else:
        r, c = b.shape[1:]
        rows, cols = (r, N_CHIPS * c) if gathered == "col" else (N_CHIPS * r, c)
        N = rows if mode == "nt" else cols
        assert K == (cols if mode == "nt" else rows), (name, K, rows, cols)
        if gathered == "col" and mode == "nn":
            tn, tk = c, (r if r <= tk_cap else _pick(r, tk_cap))
            b_spec = pl.BlockSpec((None, tk, c), lambda i, j, k: (j, k, 0))
        elif gathered == "col":
            tn, tk = _pick(r, tn_cap), c
            b_spec = pl.BlockSpec((None, tn, c), lambda i, j, k: (k, j, 0))
        elif mode == "nn":
            tn, tk, b_rows = _pick(c, tn_cap), K, N_CHIPS
            b_spec = pl.BlockSpec((N_CHIPS, r, tn), lambda i, j, k: (0, 0, j))
        else:
            tn, tk, b_rows = 2 * r, K, 2
            b_spec = pl.BlockSpec((2, r, c), lambda i, j, k: (j, 0, 0))
    tm = _pick(M, tm_cap)
    nk = K // tk

    a_spec = (pl.BlockSpec((tk, tm), lambda i, j, k: (k, i)) if mode == "tn"
              else pl.BlockSpec((tm, tk), lambda i, j, k: (i, k)))
    in_specs = [a_spec, b_spec]
    args = [a, b]
    if bias is not None:
        in_specs.append(pl.BlockSpec((1, tn), lambda i, j, k: (0, j)))
        args.append(bias)
    if res is not None:
        in_specs.append(pl.BlockSpec((tm, tn), lambda i, j, k: (i, j)))
        args.append(res)
    if norm_bwd is not None:
        assert tn == N and alpha == 1.0 and out_split is None and bias is None and res is None, name
        x_in, g_in, dres_in = norm_bwd
        in_specs += [pl.BlockSpec((tm, tn), lambda i, j, k: (i, 0)), pl.BlockSpec((1, tn), lambda i, j, k: (0, 0)),
                     pl.BlockSpec((tm, tn), lambda i, j, k: (i, 0))]
        args += [x_in, g_in, dres_in]

    est = 2 * (_nbytes((tm, tk), a.dtype) + _nbytes((tk, tn), b.dtype) + _nbytes((tm, tn), out_dtype))
    est += _nbytes((tm, tn), F32) * (3 if res is not None else 1)
    est += _nbytes((tm, tn), F32) * (4 if norm_bwd is not None else 0)
    assert est < VMEM_BUDGET, (name, est)

    def body(*refs):
        a_ref, b_ref = refs[0], refs[1]
        pos = 2
        bias_ref = res_ref = None
        if bias is not None:
            bias_ref = refs[pos]
            pos += 1
        if res is not None:
            res_ref = refs[pos]
            pos += 1
        if norm_bwd is not None:
            x_ref, g_ref, dres_ref = refs[pos:pos + 3]
            pos += 3
        o_ref = refs[pos]
        if norm_bwd is not None:
            dg_ref = refs[pos + 1]
            pos += 1
        acc_ref = refs[pos + 1] if nk > 1 else None

        bv = b_ref[...]
        if b_rows is not None:
            bv = bv.reshape(b_rows * bv.shape[1], bv.shape[2])
        part = lax.dot_general(a_ref[...].astype(BF16), bv.astype(BF16), dn, preferred_element_type=F32)

        def finish(acc):
            if alpha != 1.0:
                acc = acc * alpha
            if bias_ref is not None:
                acc = acc + bias_ref[...]
            if res_ref is not None:
                acc = acc + res_ref[...]
            if norm_bwd is not None:
                xv = x_ref[...]
                rstd = lax.rsqrt(jnp.mean(xv * xv, axis=-1, keepdims=True) + RMS_EPS)
                xhat = xv * rstd
                dyg = acc * g_ref[...]
                part_g = jnp.sum(acc * xhat, axis=0, keepdims=True)
                acc = dres_ref[...] + rstd * (dyg - xhat * jnp.mean(dyg * xhat, axis=-1, keepdims=True))
                first = pl.program_id(0) == 0

                @pl.when(first)
                def _():
                    dg_ref[...] = part_g

                @pl.when(jnp.logical_not(first))
                def _():
                    dg_ref[...] += part_g
            o_ref[...] = acc.astype(out_dtype)

        if nk == 1:
            finish(part)
        else:
            k = pl.program_id(2)

            @pl.when(k == 0)
            def _():
                acc_ref[...] = part

            @pl.when(k > 0)
            def _():
                acc_ref[...] += part

            @pl.when(k == nk - 1)
            def _():
                finish(acc_ref[...])

    if out_split == "col":
        per_half = M // 2 // tm
        out_spec = pl.BlockSpec((None, None, tm, tn), lambda i, j, k: (i // per_half, j, i % per_half, 0))
        out_shape = jax.ShapeDtypeStruct((2, N_CHIPS, M // 2, tn), out_dtype)
    elif out_split == "row":
        out_spec = pl.BlockSpec((None, tm, tn), lambda i, j, k: (j, i, 0))
        out_shape = jax.ShapeDtypeStruct((2, M, tn), out_dtype)
    else:
        out_spec = pl.BlockSpec((tm, tn), lambda i, j, k: (i, j))
        out_shape = jax.ShapeDtypeStruct((M, N), out_dtype)
    semantics = ("parallel", "parallel", "arbitrary")
    if norm_bwd is not None:
        out_spec = [out_spec, pl.BlockSpec((1, tn), lambda i, j, k: (0, 0))]
        out_shape = [out_shape, jax.ShapeDtypeStruct((1, N), F32)]
        semantics = ("arbitrary", "arbitrary", "arbitrary")
    return _pcall_carrying(
        body, carry, name=name, grid=(M // tm, N // tn, nk), in_specs=in_specs, out_specs=out_spec,
        out_shape=out_shape,
        scratch_shapes=[pltpu.VMEM((tm, tn), F32)] if nk > 1 else [],
        compiler_params=pltpu.CompilerParams(dimension_semantics=semantics),
    )(*args)


def _rmsnorm_fwd(x, g, name):
    T, Dm = x.shape
    tm = _pick(T, 256, 8)

    def body(x_ref, g_ref, h_ref):
        xv = x_ref[...]
        rstd = lax.rsqrt(jnp.mean(xv * xv, axis=-1, keepdims=True) + RMS_EPS)
        h_ref[...] = (xv * rstd * g_ref[...]).astype(BF16)

    return _pcall(
        body, name=name, grid=(T // tm,),
        in_specs=[pl.BlockSpec((tm, Dm), lambda i: (i, 0)), pl.BlockSpec((1, Dm), lambda i: (0, 0))],
        out_specs=pl.BlockSpec((tm, Dm), lambda i: (i, 0)),
        out_shape=jax.ShapeDtypeStruct((T, Dm), BF16),
    )(x, g)


def _final_loss(x, g, tgt, name):
    T, Dm = x.shape
    tm = _pick(T, 256, 8)

    def body(x_ref, g_ref, t_ref, dx_ref, dg_ref, loss_ref):
        xv = x_ref[...]
        gv = g_ref[...]
        rstd = lax.rsqrt(jnp.mean(xv * xv, axis=-1, keepdims=True) + RMS_EPS)
        xhat = xv * rstd
        err = xhat * gv - t_ref[...]
        part_loss = 0.5 * jnp.sum(jnp.mean(err * err, axis=-1, keepdims=True), axis=0, keepdims=True)
        dy = err * (1.0 / Dm)
        dyg = dy * gv
        dx_ref[...] = rstd * (dyg - xhat * jnp.mean(dyg * xhat, axis=-1, keepdims=True))
        part_g = jnp.sum(dy * xhat, axis=0, keepdims=True)
        part_l = jnp.broadcast_to(part_loss, (1, 128))

        @pl.when(pl.program_id(0) == 0)
        def _():
            dg_ref[...] = part_g
            loss_ref[...] = part_l

        @pl.when(pl.program_id(0) > 0)
        def _():
            dg_ref[...] += part_g
            loss_ref[...] += part_l

    row = pl.BlockSpec((tm, Dm), lambda i: (i, 0))
    vec = pl.BlockSpec((1, Dm), lambda i: (0, 0))
    return _pcall(
        body, name=name, grid=(T // tm,), in_specs=[row, vec, row],
        out_specs=[row, vec, pl.BlockSpec((1, 128), lambda i: (0, 0))],
        out_shape=[jax.ShapeDtypeStruct((T, Dm), F32), jax.ShapeDtypeStruct((1, Dm), F32),
                   jax.ShapeDtypeStruct((1, 128), F32)],
        compiler_params=pltpu.CompilerParams(dimension_semantics=("arbitrary",)),
    )(x, g, tgt)


def _sigmoid(z):
    return 1.0 / (1.0 + jnp.exp(-z))


def _ffn_in_swiglu(x, g, w_in, name, carry=None):
    T, Dm = x.shape
    cw = w_in.shape[2]
    tm = _pick(T, 512, 16)

    def body(x_ref, gain_ref, wg_ref, wu_ref, h_ref, g_ref, u_ref, a_ref):
        @pl.when(pl.program_id(1) == 0)
        def _():
            xv = x_ref[...]
            rstd = lax.rsqrt(jnp.mean(xv * xv, axis=-1, keepdims=True) + RMS_EPS)
            h_ref[...] = (xv * rstd * gain_ref[...]).astype(BF16)

        hv = h_ref[...]
        gv = jnp.dot(hv, wg_ref[...], preferred_element_type=F32)
        uv = jnp.dot(hv, wu_ref[...], preferred_element_type=F32)
        g_ref[...] = gv.astype(BF16)
        u_ref[...] = uv.astype(BF16)
        a_ref[...] = (gv * _sigmoid(gv) * uv).astype(BF16)

    row = pl.BlockSpec((tm, Dm), lambda i, j: (i, 0))
    out = pl.BlockSpec((tm, cw), lambda i, j: (i, j))
    return _pcall_carrying(
        body, carry, name=name, grid=(T // tm, 2),
        in_specs=[row, pl.BlockSpec((1, Dm), lambda i, j: (0, 0)), pl.BlockSpec((None, Dm, cw), lambda i, j: (j, 0, 0)),
                  pl.BlockSpec((None, Dm, cw), lambda i, j: (j + 2, 0, 0))],
        out_specs=[row, out, out, out],
        out_shape=[jax.ShapeDtypeStruct((T, Dm), BF16)] + [jax.ShapeDtypeStruct((T, D_FF), BF16)] * 3,
        compiler_params=pltpu.CompilerParams(dimension_semantics=("parallel", "arbitrary")),
    )(x, g, w_in, w_in)


def _swiglu_bwd(g, u, da, name):
    T = g.shape[0]
    tm = _pick(T, 256, 16)

    def body(g_ref, u_ref, da_ref, d_ref):
        gv = g_ref[...].astype(F32)
        uv = u_ref[...].astype(F32)
        dav = da_ref[...].astype(F32)
        sg = _sigmoid(gv)
        d_ref[:, :D_FF] = (dav * uv * (sg + gv * sg * (1.0 - sg))).astype(BF16)
        d_ref[:, D_FF:] = (dav * gv * sg).astype(BF16)

    row = pl.BlockSpec((tm, D_FF), lambda i: (i, 0))
    return _pcall(
        body, name=name, grid=(T // tm,), in_specs=[row, row, row],
        out_specs=pl.BlockSpec((tm, 2 * D_FF), lambda i: (i, 0)),
        out_shape=jax.ShapeDtypeStruct((T, 2 * D_FF), BF16),
    )(g, u, da)


def _branches_fwd(p, outs, weights, name):
    T = p.shape[0]
    cw = weights[0].shape[2]
    tm = _pick(T, 1024, 16)
    first_gate, per_branch = QKV_WIDTH // cw, D_MODEL // cw

    def body(*refs):
        gates, o_refs, w_refs, y_refs, m_ref = refs[0:3], refs[3:6], refs[6:9], refs[9:12], refs[12]
        merged = None
        for g_ref, o_ref, w_ref, y_ref in zip(gates, o_refs, w_refs, y_refs):
            y = jnp.dot(o_ref[...], w_ref[...], preferred_element_type=F32)
            y_ref[...] = y
            term = _sigmoid(g_ref[...].astype(F32)) * y
            merged = term if merged is None else merged + term
        m_ref[...] = merged.astype(BF16)

    gate_specs = [pl.BlockSpec((tm, cw), functools.partial(
        lambda i, j, kk: (i, first_gate + per_branch * kk + j), kk=kk)) for kk in range(3)]
    o_specs = [pl.BlockSpec((tm, o.shape[1]), lambda i, j: (i, 0)) for o in outs]
    w_specs = [pl.BlockSpec((None, wk.shape[1], cw), lambda i, j: (j, 0, 0)) for wk in weights]
    tile = pl.BlockSpec((tm, cw), lambda i, j: (i, j))
    return _pcall(
        body, name=name, grid=(T // tm, N_CHIPS), in_specs=gate_specs + o_specs + w_specs, out_specs=[tile] * 4,
        out_shape=[jax.ShapeDtypeStruct((T, D_MODEL), F32)] * 3 + [jax.ShapeDtypeStruct((T, D_MODEL), BF16)],
    )(p, p, p, *outs, *weights)


def _branches_bwd(p, ys, dm, weights, name):
    T = p.shape[0]
    cw = weights[0].shape[2]
    tm = _pick(T, 1024, 16)
    first_gate, per_branch = QKV_WIDTH // cw, D_MODEL // cw
    widths = [wk.shape[1] for wk in weights]

    def body(*refs):
        gates, y_refs, dm_ref, w_refs = refs[0:3], refs[3:6], refs[6], refs[7:10]
        dy_refs, dg_refs, do_refs, acc_refs = refs[10:13], refs[13:16], refs[16:19], refs[19:22]
        j = pl.program_id(1)
        dmv = dm_ref[...]
        for g_ref, y_ref, w_ref, dy_ref, dg_ref, do_ref, acc_ref in zip(gates, y_refs, w_refs, dy_refs, dg_refs,
                                                                       do_refs, acc_refs):
            s = _sigmoid(g_ref[...].astype(F32))
            dy = (s * dmv).astype(BF16)
            dy_ref[...] = dy
            dg_ref[...] = (dmv * y_ref[...] * s * (1.0 - s)).astype(BF16)
            part = _dot_nt(dy, w_ref[...])

            @pl.when(j == 0)
            def _():
                acc_ref[...] = part

            @pl.when(j > 0)
            def _():
                acc_ref[...] += part

            @pl.when(j == N_CHIPS - 1)
            def _():
                do_ref[...] = acc_ref[...].astype(BF16)

    gate_specs = [pl.BlockSpec((tm, cw), functools.partial(
        lambda i, j, kk: (i, first_gate + per_branch * kk + j), kk=kk)) for kk in range(3)]
    tile = pl.BlockSpec((tm, cw), lambda i, j: (i, j))
    w_specs = [pl.BlockSpec((None, r, cw), lambda i, j: (j, 0, 0)) for r in widths]
    do_specs = [pl.BlockSpec((tm, r), lambda i, j: (i, 0)) for r in widths]
    wide = jax.ShapeDtypeStruct((T, D_MODEL), BF16)
    out = _pcall(
        body, name=name, grid=(T // tm, N_CHIPS), in_specs=gate_specs + [tile] * 4 + w_specs,
        out_specs=[tile] * 6 + do_specs,
        out_shape=[wide] * 6 + [jax.ShapeDtypeStruct((T, r), BF16) for r in widths],
        scratch_shapes=[pltpu.VMEM((tm, r), F32) for r in widths],
        compiler_params=pltpu.CompilerParams(dimension_semantics=("parallel", "arbitrary")),
    )(p, p, p, *ys, dm, *weights)
    return out[0:3], out[3:6], out[6:9]


def _branches_dw(outs, dys, name):
    T = dys[0].shape[0]
    cw = D_MODEL // N_CHIPS
    widths = [o.shape[1] for o in outs]

    def body(*refs):
        o_refs, dy_refs, dw_refs = refs[0:3], refs[3:6], refs[6:9]
        for o_ref, dy_ref, dw_ref, r in zip(o_refs, dy_refs, dw_refs, widths):
            dw = _dot_tn(o_ref[...], dy_ref[...]).astype(BF16)
            dw_ref[0] = dw[:r // 2]
            dw_ref[1] = dw[r // 2:]

    return _pcall(
        body, name=name, grid=(N_CHIPS,),
        in_specs=[pl.BlockSpec((T, r), lambda j: (0, 0)) for r in widths] + [pl.BlockSpec((T, cw), lambda j: (0, j))] * 3,
        out_specs=[pl.BlockSpec((2, None, r // 2, cw), lambda j: (0, j, 0, 0)) for r in widths],
        out_shape=[jax.ShapeDtypeStruct((2, N_CHIPS, r // 2, cw), BF16) for r in widths],
    )(*outs, *dys)


def _colsum(a, name):
    T, N = a.shape
    tm = _pick(T, 256, 16)

    def body(a_ref, o_ref):
        part = jnp.sum(a_ref[...].astype(F32), axis=0, keepdims=True)

        @pl.when(pl.program_id(0) == 0)
        def _():
            o_ref[...] = part

        @pl.when(pl.program_id(0) > 0)
        def _():
            o_ref[...] += part

    return _pcall(
        body, name=name, grid=(T // tm,), in_specs=[pl.BlockSpec((tm, N), lambda i: (i, 0))],
        out_specs=pl.BlockSpec((1, N), lambda i: (0, 0)), out_shape=jax.ShapeDtypeStruct((1, N), F32),
        compiler_params=pltpu.CompilerParams(dimension_semantics=("arbitrary",)),
    )(a)


def _adamw(w, g, m, v, name):
    R, C = w.shape
    tr = 256 if R % 256 == 0 else R
    c1 = 1.0 / (1.0 - ADAM_B1 ** ADAM_STEP)
    c2 = 1.0 / (1.0 - ADAM_B2 ** ADAM_STEP)

    def body(w_ref, g_ref, m_ref, v_ref, d_ref, nm_ref, nv_ref):
        gv = g_ref[...]
        mn = ADAM_B1 * m_ref[...] + (1.0 - ADAM_B1) * gv
        vn = ADAM_B2 * v_ref[...] + (1.0 - ADAM_B2) * (gv * gv)
        nm_ref[...] = mn
        nv_ref[...] = vn
        d_ref[...] = -ADAM_LR * ((mn * c1) / (jnp.sqrt(vn * c2) + ADAM_EPS) + ADAM_WD * w_ref[...])

    spec = pl.BlockSpec((tr, C), lambda i: (i, 0))
    return _pcall(
        body, name=name, grid=(R // tr,), in_specs=[spec] * 4, out_specs=[spec] * 3,
        out_shape=[jax.ShapeDtypeStruct((R, C), F32)] * 3,
    )(w, g, m, v)


def _log_sigmoid(z):
    return jnp.minimum(z, 0.0) - jnp.log(1.0 + jnp.exp(-jnp.abs(z)))


def _dot3(x, m01):
    x1 = x.astype(BF16)
    r1 = x - x1.astype(F32)
    x2 = r1.astype(BF16)
    x3 = (r1 - x2.astype(F32)).astype(BF16)
    n = x.shape[0]
    if n % 16:
        return (jnp.dot(x1, m01, preferred_element_type=F32) + jnp.dot(x2, m01, preferred_element_type=F32)
                + jnp.dot(x3, m01, preferred_element_type=F32))
    y = jnp.dot(jnp.concatenate([x1, x2, x3], axis=0), m01, preferred_element_type=F32)
    return y[:n] + y[n:2 * n] + y[2 * n:]


def _dot_nt(a, b):
    return lax.dot_general(a, b, NT, preferred_element_type=F32)


def _dot_tn(a, b):
    return lax.dot_general(a, b, TN, preferred_element_type=F32)


def _iota2(shape):
    return lax.broadcasted_iota(jnp.int32, shape, 0), lax.broadcasted_iota(jnp.int32, shape, 1)


HEADS_PER_STEP = 4
HEADS = range(HEADS_PER_STEP)


CHUNK_HEADS_PER_STEP = 4
CHUNK_HEADS = range(CHUNK_HEADS_PER_STEP)


def _head_spec(T, rows=None, width=HEAD_DIM, heads=HEADS_PER_STEP):
    return pl.BlockSpec((heads, T if rows is None else rows, width), lambda g: (g, 0, 0))


def _sb_weights(qb, kb, t0, s0, racc, m_gt, row, col):
    z = _dot_nt(qb, kb) * SCALE
    strict = (s0 + col) < (t0 + row)
    lb = _log_sigmoid(z)
    lf = jnp.where(strict, lb - z, 0.0)
    between = _dot3(lf, m_gt) + racc
    w = jnp.where(strict, jnp.exp(lb + between), 0.0)
    return strict, lb, lf, w


def _sb_fwd(q, k, v, name, carry=None):
    H, T, _ = q.shape
    nb = T // QB

    def body(q_ref, k_ref, v_ref, o_ref):
        row, col = _iota2((QB, QB))
        m_gt = (row > col).astype(BF16)

        def qblock(i, _):
            t0 = pl.multiple_of(i * QB, QB)
            qbs = [q_ref[h, pl.ds(t0, QB), :].astype(BF16) for h in HEADS]

            def kblock(jj, carry):
                s0 = pl.multiple_of((i - jj) * QB, QB)
                out = []
                for h in HEADS:
                    acc, racc = carry[h]
                    kb = k_ref[h, pl.ds(s0, QB), :].astype(BF16)
                    vb = v_ref[h, pl.ds(s0, QB), :].astype(BF16)
                    _, _, lf, w = _sb_weights(qbs[h], kb, t0, s0, racc, m_gt, row, col)
                    acc = acc + jnp.dot(w.astype(BF16), vb, preferred_element_type=F32)
                    out.append((acc, racc + jnp.sum(lf, axis=1, keepdims=True)))
                return tuple(out)

            res = lax.fori_loop(0, i + 1, kblock,
                                tuple((jnp.zeros((QB, HEAD_DIM), F32), jnp.zeros((QB, 1), F32)) for _ in HEADS))
            for h in HEADS:
                o_ref[h, pl.ds(t0, QB), :] = res[h][0].astype(BF16)
            return 0

        lax.fori_loop(0, nb, qblock, 0)

    spec = _head_spec(T)
    return _pcall_carrying(body, carry, name=name, grid=(H // HEADS_PER_STEP,), in_specs=[spec] * 3, out_specs=spec,
                           out_shape=jax.ShapeDtypeStruct((H, T, HEAD_DIM), BF16))(q, k, v)


def _sb_bwd(q, k, v, do, name, carry=None):
    H, T, _ = q.shape
    nb = T // QB

    def body(q_ref, k_ref, v_ref, do_ref, dq_ref, dk_out, dv_out, racc_ref, dk_ref, dv_ref):
        row, col = _iota2((QB, QB))
        m_gt = (row > col).astype(BF16)
        m_lt = (row < col).astype(BF16)
        dk_ref[...] = jnp.zeros_like(dk_ref)
        dv_ref[...] = jnp.zeros_like(dv_ref)

        def qblock(i, _):
            t0 = pl.multiple_of(i * QB, QB)
            qbs = [q_ref[h, pl.ds(t0, QB), :].astype(BF16) for h in HEADS]
            dobs = [do_ref[h, pl.ds(t0, QB), :].astype(BF16) for h in HEADS]

            def suffix(jj, raccs):
                j = i - jj
                s0 = pl.multiple_of(j * QB, QB)
                out = []
                for h in HEADS:
                    kb = k_ref[h, pl.ds(s0, QB), :].astype(BF16)
                    z = _dot_nt(qbs[h], kb) * SCALE
                    lf = jnp.where((s0 + col) < (t0 + row), _log_sigmoid(z) - z, 0.0)
                    racc_ref[h, j] = raccs[h]
                    out.append(raccs[h] + jnp.sum(lf, axis=1, keepdims=True))
                return tuple(out)

            lax.fori_loop(0, i + 1, suffix, tuple(jnp.zeros((QB, 1), F32) for _ in HEADS))

            def kblock(j, carry):
                s0 = pl.multiple_of(j * QB, QB)
                out = []
                for h in HEADS:
                    dq, cacc = carry[h]
                    kb = k_ref[h, pl.ds(s0, QB), :].astype(BF16)
                    vb = v_ref[h, pl.ds(s0, QB), :].astype(BF16)
                    strict, lb, _, w = _sb_weights(qbs[h], kb, t0, s0, racc_ref[h, j], m_gt, row, col)
                    e = _dot_nt(dobs[h], vb) * w
                    c_left = _dot3(e, m_lt) + cacc
                    sig = jnp.exp(lb)
                    dz = jnp.where(strict, e * (1.0 - sig) - c_left * sig, 0.0).astype(BF16)
                    dq = dq + jnp.dot(dz, kb, preferred_element_type=F32)
                    dk_ref[h, pl.ds(s0, QB), :] += _dot_tn(dz, qbs[h]) * SCALE
                    dv_ref[h, pl.ds(s0, QB), :] += _dot_tn(w.astype(BF16), dobs[h])
                    out.append((dq, cacc + jnp.sum(e, axis=1, keepdims=True)))
                return tuple(out)

            res = lax.fori_loop(0, i + 1, kblock,
                                tuple((jnp.zeros((QB, HEAD_DIM), F32), jnp.zeros((QB, 1), F32)) for _ in HEADS))
            for h in HEADS:
                dq_ref[h, pl.ds(t0, QB), :] = (res[h][0] * SCALE).astype(BF16)
            return 0

        lax.fori_loop(0, nb, qblock, 0)
        dk_out[...] = dk_ref[...].astype(BF16)
        dv_out[...] = dv_ref[...].astype(BF16)

    spec = _head_spec(T)
    acc = pltpu.VMEM((HEADS_PER_STEP, T, HEAD_DIM), F32)
    return _pcall_carrying(body, carry, name=name, grid=(H // HEADS_PER_STEP,), in_specs=[spec] * 4,
                           out_specs=[spec] * 3, out_shape=[jax.ShapeDtypeStruct((H, T, HEAD_DIM), BF16)] * 3,
                           scratch_shapes=[pltpu.VMEM((HEADS_PER_STEP, nb, QB, 1), F32), acc, acc])(q, k, v, do)


def _fox_logits(qb, kb, fq, fk, t0, s0, row, col):
    z = _dot_nt(qb, kb) * SCALE + fq - fk
    return jnp.where((s0 + col) <= (t0 + row), z, NEG)


def _fox_specs(T):
    return _head_spec(T, width=1), _head_spec(T, rows=T // QB, width=QB)


def _fox_fwd(q, k, v, fq, fk, name, carry=None):
    H, T, _ = q.shape
    nb = T // QB

    def body(q_ref, k_ref, v_ref, fq_ref, fk_ref, o_ref, lse_ref):
        row, col = _iota2((QB, QB))

        def qblock(i, _):
            t0 = pl.multiple_of(i * QB, QB)
            qbs = [q_ref[h, pl.ds(t0, QB), :].astype(BF16) for h in HEADS]
            fqs = [fq_ref[h, pl.ds(t0, QB), :] for h in HEADS]

            def kblock(j, carry):
                s0 = pl.multiple_of(j * QB, QB)
                out = []
                for h in HEADS:
                    acc, m, l = carry[h]
                    kb = k_ref[h, pl.ds(s0, QB), :].astype(BF16)
                    vb = v_ref[h, pl.ds(s0, QB), :].astype(BF16)
                    z = _fox_logits(qbs[h], kb, fqs[h], fk_ref[h, pl.ds(j, 1), :], t0, s0, row, col)
                    m_new = jnp.maximum(m, jnp.max(z, axis=1, keepdims=True))
                    a = jnp.exp(m - m_new)
                    p = jnp.exp(z - m_new)
                    acc = a * acc + jnp.dot(p.astype(BF16), vb, preferred_element_type=F32)
                    out.append((acc, m_new, a * l + jnp.sum(p, axis=1, keepdims=True)))
                return tuple(out)

            res = lax.fori_loop(0, i + 1, kblock,
                                tuple((jnp.zeros((QB, HEAD_DIM), F32), jnp.full((QB, 1), NEG, F32),
                                       jnp.zeros((QB, 1), F32)) for _ in HEADS))
            for h in HEADS:
                acc, m, l = res[h]
                o_ref[h, pl.ds(t0, QB), :] = (acc / l).astype(BF16)
                lse_ref[h, pl.ds(t0, QB), :] = m + jnp.log(l)
            return 0

        lax.fori_loop(0, nb, qblock, 0)

    spec = _head_spec(T)
    fq_spec, fk_spec = _fox_specs(T)
    return _pcall_carrying(
        body, carry, name=name, grid=(H // HEADS_PER_STEP,), in_specs=[spec] * 3 + [fq_spec, fk_spec],
        out_specs=[spec, fq_spec],
        out_shape=[jax.ShapeDtypeStruct((H, T, HEAD_DIM), BF16), jax.ShapeDtypeStruct((H, T, 1), F32)],
    )(q, k, v, fq, fk)


def _fox_bwd(q, k, v, fq, fk, lse, do, name, carry=None):
    H, T, _ = q.shape
    nb = T // QB

    def body(q_ref, k_ref, v_ref, fq_ref, fk_ref, lse_ref, do_ref, dq_ref, dk_out, dv_out, dfk_ref, dk_ref, dv_ref):
        row, col = _iota2((QB, QB))
        dk_ref[...] = jnp.zeros_like(dk_ref)
        dv_ref[...] = jnp.zeros_like(dv_ref)
        dfk_ref[...] = jnp.zeros_like(dfk_ref)

        def qblock(i, _):
            t0 = pl.multiple_of(i * QB, QB)
            qbs = [q_ref[h, pl.ds(t0, QB), :].astype(BF16) for h in HEADS]
            fqs = [fq_ref[h, pl.ds(t0, QB), :] for h in HEADS]
            lses = [lse_ref[h, pl.ds(t0, QB), :] for h in HEADS]
            dobs = [do_ref[h, pl.ds(t0, QB), :].astype(BF16) for h in HEADS]

            def probs(h, j):
                s0 = pl.multiple_of(j * QB, QB)
                kb = k_ref[h, pl.ds(s0, QB), :].astype(BF16)
                vb = v_ref[h, pl.ds(s0, QB), :].astype(BF16)
                z = _fox_logits(qbs[h], kb, fqs[h], fk_ref[h, pl.ds(j, 1), :], t0, s0, row, col)
                return s0, kb, jnp.exp(z - lses[h]), _dot_nt(dobs[h], vb)

            def row_dot(j, deltas):
                out = []
                for h in HEADS:
                    _, _, p, dp = probs(h, j)
                    out.append(deltas[h] + jnp.sum(p * dp, axis=1, keepdims=True))
                return tuple(out)

            deltas = lax.fori_loop(0, i + 1, row_dot, tuple(jnp.zeros((QB, 1), F32) for _ in HEADS))

            def kblock(j, dqs):
                out = []
                for h in HEADS:
                    s0, kb, p, dp = probs(h, j)
                    dz = p * (dp - deltas[h])
                    dzb = dz.astype(BF16)
                    dk_ref[h, pl.ds(s0, QB), :] += _dot_tn(dzb, qbs[h]) * SCALE
                    dv_ref[h, pl.ds(s0, QB), :] += _dot_tn(p.astype(BF16), dobs[h])
                    dfk_ref[h, pl.ds(j, 1), :] += -jnp.sum(dz, axis=0, keepdims=True)
                    out.append(dqs[h] + jnp.dot(dzb, kb, preferred_element_type=F32))
                return tuple(out)

            dqs = lax.fori_loop(0, i + 1, kblock, tuple(jnp.zeros((QB, HEAD_DIM), F32) for _ in HEADS))
            for h in HEADS:
                dq_ref[h, pl.ds(t0, QB), :] = (dqs[h] * SCALE).astype(BF16)
            return 0

        lax.fori_loop(0, nb, qblock, 0)
        dk_out[...] = dk_ref[...].astype(BF16)
        dv_out[...] = dv_ref[...].astype(BF16)

    spec = _head_spec(T)
    fq_spec, fk_spec = _fox_specs(T)
    acc = pltpu.VMEM((HEADS_PER_STEP, T, HEAD_DIM), F32)
    return _pcall_carrying(body, carry, name=name, grid=(H // HEADS_PER_STEP,),
                           in_specs=[spec] * 3 + [fq_spec, fk_spec, fq_spec, spec],
                           out_specs=[spec, spec, spec, fk_spec],
                           out_shape=[jax.ShapeDtypeStruct((H, T, HEAD_DIM), BF16)] * 3
                           + [jax.ShapeDtypeStruct((H, nb, QB), F32)],
                           scratch_shapes=[acc, acc])(q, k, v, fq, fk, lse, do)


def _forget_cumsum(flog, name):
    R, T = flog.shape
    nb = T // QB

    def body(x_ref, o_ref):
        row, col = _iota2((QB, QB))
        m_le = (row <= col).astype(BF16)
        carry = jnp.zeros((R, 1), F32)
        for b in range(nb):
            lf = _log_sigmoid(x_ref[:, b * QB:(b + 1) * QB])
            o_ref[:, b * QB:(b + 1) * QB] = _dot3(lf, m_le) + carry
            carry = carry + jnp.sum(lf, axis=1, keepdims=True)

    spec = pl.BlockSpec((R, T), lambda: (0, 0))
    return _pcall(body, name=name, in_specs=[spec], out_specs=spec,
                  out_shape=jax.ShapeDtypeStruct((R, T), F32))(flog)


def _forget_cumsum_bwd(flog, df, name):
    R, T = flog.shape
    nb = T // QB

    def body(x_ref, df_ref, o_ref):
        row, col = _iota2((QB, QB))
        m_ge = (row >= col).astype(BF16)
        carry = jnp.zeros((R, 1), F32)
        for b in reversed(range(nb)):
            dfb = df_ref[:, b * QB:(b + 1) * QB]
            dlog = _dot3(dfb, m_ge) + carry
            o_ref[:, b * QB:(b + 1) * QB] = dlog * _sigmoid(-x_ref[:, b * QB:(b + 1) * QB])
            carry = carry + jnp.sum(dfb, axis=1, keepdims=True)

    spec = pl.BlockSpec((R, T), lambda: (0, 0))
    return _pcall(body, name=name, in_specs=[spec, spec], out_specs=spec,
                  out_shape=jax.ShapeDtypeStruct((R, T), F32))(flog, df)


def _chunk_probs(qc, kb, bias, c, col):
    z = _dot_nt(qc, kb) * SCALE + bias
    z = jnp.where((c - (LEFT_CHUNKS + 1)) * CHUNK + col >= jnp.maximum(0, (c - LEFT_CHUNKS) * CHUNK), z, NEG)
    p = jnp.exp(z - jnp.max(z, axis=1, keepdims=True))
    return p, jnp.sum(p, axis=1, keepdims=True)


def _chunk_specs(T, n=CHUNK_HEADS_PER_STEP):
    return (_head_spec(T, heads=n), _head_spec(T, rows=T + BAND_PAD, heads=n),
            _head_spec(T, rows=CHUNK, width=BAND, heads=n))


def _chunk_fwd(q, kp, vp, bias, name, carry=None):
    H, T, _ = q.shape
    nc = T // CHUNK

    def body(q_ref, kp_ref, vp_ref, bias_ref, o_ref):
        _, col = _iota2((CHUNK, BAND))

        def chunk(c, _):
            t0 = pl.multiple_of(c * CHUNK, CHUNK)
            for h in HEADS:
                qc = q_ref[h, pl.ds(t0, CHUNK), :].astype(BF16)
                kb = kp_ref[h, pl.ds(t0, BAND), :].astype(BF16)
                vb = vp_ref[h, pl.ds(t0, BAND), :].astype(BF16)
                p, l = _chunk_probs(qc, kb, bias_ref[h], c, col)
                o = jnp.dot(p.astype(BF16), vb, preferred_element_type=F32) / l
                o_ref[h, pl.ds(t0, CHUNK), :] = o.astype(BF16)
            return 0

        lax.fori_loop(0, nc, chunk, 0)

    q_spec, kp_spec, b_spec = _chunk_specs(T, HEADS_PER_STEP)
    return _pcall_carrying(body, carry, name=name, grid=(H // HEADS_PER_STEP,),
                           in_specs=[q_spec, kp_spec, kp_spec, b_spec], out_specs=q_spec,
                           out_shape=jax.ShapeDtypeStruct((H, T, HEAD_DIM), BF16))(q, kp, vp, bias)


def _chunk_bwd(q, kp, vp, bias, do, name, carry=None):
    H, T, _ = q.shape
    nc = T // CHUNK

    def body(q_ref, kp_ref, vp_ref, bias_ref, do_ref, dq_ref, dk_out, dv_out, db_ref, dkp_ref, dvp_ref):
        _, col = _iota2((CHUNK, BAND))
        dkp_ref[...] = jnp.zeros_like(dkp_ref)
        dvp_ref[...] = jnp.zeros_like(dvp_ref)
        db_ref[...] = jnp.zeros_like(db_ref)

        def chunk(c, _):
            t0 = pl.multiple_of(c * CHUNK, CHUNK)
            for h in CHUNK_HEADS:
                qc = q_ref[h, pl.ds(t0, CHUNK), :].astype(BF16)
                kb = kp_ref[h, pl.ds(t0, BAND), :].astype(BF16)
                vb = vp_ref[h, pl.ds(t0, BAND), :].astype(BF16)
                dob = do_ref[h, pl.ds(t0, CHUNK), :].astype(BF16)
                p, l = _chunk_probs(qc, kb, bias_ref[h], c, col)
                p = p / l
                dp = _dot_nt(dob, vb)
                dz = p * (dp - jnp.sum(p * dp, axis=1, keepdims=True))
                dzb = dz.astype(BF16)
                dq = jnp.dot(dzb, kb, preferred_element_type=F32) * SCALE
                dq_ref[h, pl.ds(t0, CHUNK), :] = dq.astype(BF16)
                dkp_ref[h, pl.ds(t0, BAND), :] += _dot_tn(dzb, qc) * SCALE
                dvp_ref[h, pl.ds(t0, BAND), :] += _dot_tn(p.astype(BF16), dob)
                db_ref[h] += dz
            return 0

        lax.fori_loop(0, nc, chunk, 0)
        dk_out[...] = dkp_ref[:, BAND_PAD:, :].astype(BF16)
        dv_out[...] = dvp_ref[:, BAND_PAD:, :].astype(BF16)

    q_spec, kp_spec, b_spec = _chunk_specs(T)
    acc = pltpu.VMEM((CHUNK_HEADS_PER_STEP, T + BAND_PAD, HEAD_DIM), F32)
    return _pcall_carrying(body, carry, name=name, grid=(H // CHUNK_HEADS_PER_STEP,),
                           in_specs=[q_spec, kp_spec, kp_spec, b_spec, q_spec],
                           out_specs=[q_spec, q_spec, q_spec, b_spec],
                           out_shape=[jax.ShapeDtypeStruct((H, T, HEAD_DIM), BF16)] * 3
                           + [jax.ShapeDtypeStruct((H, CHUNK, BAND), F32)],
                           scratch_shapes=[acc, acc])(q, kp, vp, bias, do)


HBM_SPEC = pl.BlockSpec(memory_space=pltpu.HBM)


def _position():
    return lax.axis_index("x"), lax.axis_index("y"), lax.axis_index("c")


def _other_chips(x, y):
    chips = [(1 - x, y), (x, 1 - y), (1 - x, 1 - y)]
    return [(chip, 2 * chip[0] + chip[1]) for chip in chips]


def _remote_copy(src, dst, send_sems, recv_sems, k, to):
    return pltpu.make_async_remote_copy(src_ref=src, dst_ref=dst, send_sem=send_sems.at[k], recv_sem=recv_sems.at[k],
                                        device_id=to, device_id_type=MESH)


def _place_own(w, chip, name):
    _, r, c = w.shape
    tr = _pick(r, 256, 16)

    def body(chip_ref, w_ref, *out_refs):
        for l, o_ref in enumerate(out_refs):
            o_ref[...] = w_ref[l].astype(BF16)

    grid_spec = pltpu.PrefetchScalarGridSpec(
        num_scalar_prefetch=1, grid=(r // tr,), in_specs=[pl.BlockSpec((DEPTH, tr, c), lambda i, ch: (0, i, 0))],
        out_specs=[pl.BlockSpec((None, tr, c), lambda i, ch: (ch[0], i, 0))] * DEPTH)
    return _pcall(body, name=name, grid_spec=grid_spec,
                  out_shape=[jax.ShapeDtypeStruct((N_CHIPS, r, c), BF16)] * DEPTH)(chip, w)


def _gather_over_ici(srcs, bufs, new):
    x, y, c = _position()
    me = 2 * x + y
    return [(b.at[me, c], b.at[me, c], (*chip, c)) for b in bufs for chip, _ in _other_chips(x, y)]


def _gather_to_sibling(srcs, bufs, new):
    x, y, c = _position()
    return [(b.at[idx, c], b.at[idx, c], (x, y, 1 - c)) for b in bufs for _, idx in _other_chips(x, y)]


def _gather_layer(bufs, name):
    n = len(bufs)

    def body(*refs):
        dst = refs[n:2 * n]
        send_sems, recv_sems = refs[2 * n:]
        x, y, c = _position()
        me = 2 * x + y
        sibling = (x, y, 1 - c)
        others = _other_chips(x, y)
        first = [_remote_copy(dst[i].at[me, c], dst[i].at[me, c], send_sems, recv_sems, 3 * i + k, (*chip, c))
                 for i in range(n) for k, (chip, _) in enumerate(others)]
        for cp in first:
            cp.start()
        passed = []
        for i in range(n):
            for k, (_, idx) in enumerate(others):
                landed = dst[i].at[idx, c]
                _remote_copy(landed, landed, send_sems, recv_sems, 3 * i + k, sibling).wait_recv()
                passed.append(_remote_copy(landed, landed, send_sems, recv_sems, 3 * (n + i) + k, sibling))
                passed[-1].start()
        for i in range(n):
            for k, (_, idx) in enumerate(others):
                from_sibling = dst[i].at[idx, 1 - c]
                _remote_copy(from_sibling, from_sibling, send_sems, recv_sems, 3 * (n + i) + k, sibling).wait_recv()
        for cp in first + passed:
            cp.wait_send()

    return _pcall(
        body, name=name, in_specs=[HBM_SPEC] * n, out_specs=[HBM_SPEC] * n,
        out_shape=[jax.ShapeDtypeStruct(b.shape, b.dtype) for b in bufs],
        scratch_shapes=[pltpu.SemaphoreType.DMA((6 * n,)), pltpu.SemaphoreType.DMA((6 * n,))],
        input_output_aliases={i: i for i in range(n)},
    )(*bufs)


def _send_other_half(parts, name):
    n = len(parts)

    def body(*refs):
        src, got = refs[:n], refs[n:2 * n]
        send_sems, recv_sems = refs[2 * n:]
        x, y, c = _position()
        copies = [_remote_copy(src[i].at[1 - c], got[i], send_sems, recv_sems, i, (x, y, 1 - c)) for i in range(n)]
        for cp in copies:
            cp.start()
        for cp in copies:
            cp.wait()

    return _pcall(
        body, name=name, in_specs=[HBM_SPEC] * n, out_specs=[HBM_SPEC] * n,
        out_shape=[jax.ShapeDtypeStruct(p.shape[1:], p.dtype) for p in parts],
        scratch_shapes=[pltpu.SemaphoreType.DMA((n,)), pltpu.SemaphoreType.DMA((n,))],
    )(*parts)


def _scatter_chips(parts, name):
    n = len(parts)

    def body(*refs):
        src, dst = refs[:n], refs[n:2 * n]
        send_sems, recv_sems = refs[2 * n:]
        x, y, c = _position()
        others = _other_chips(x, y)
        sends = [_remote_copy(src[i].at[idx], dst[i].at[k], send_sems, recv_sems, 3 * i + k, (*chip, c))
                 for i in range(n) for k, (chip, idx) in enumerate(others)]
        for cp in sends:
            cp.start()
        for cp in sends:
            cp.wait()

    return _pcall(
        body, name=name, in_specs=[HBM_SPEC] * n, out_specs=[HBM_SPEC] * n,
        out_shape=[jax.ShapeDtypeStruct((3,) + p.shape[1:], p.dtype) for p in parts],
        scratch_shapes=[pltpu.SemaphoreType.DMA((3 * n,)), pltpu.SemaphoreType.DMA((3 * n,))],
    )(*parts)


def _swap_with_sibling(arrs, name):
    n = len(arrs)

    def body(*refs):
        src, dst = refs[:n], refs[n:2 * n]
        send_sems, recv_sems = refs[2 * n:]
        x, y, c = _position()
        copies = [_remote_copy(src[i], dst[i], send_sems, recv_sems, i, (x, y, 1 - c)) for i in range(n)]
        for cp in copies:
            cp.start()
        for cp in copies:
            cp.wait()

    return _pcall(
        body, name=name, in_specs=[HBM_SPEC] * n, out_specs=[HBM_SPEC] * n,
        out_shape=[jax.ShapeDtypeStruct(a.shape, a.dtype) for a in arrs],
        scratch_shapes=[pltpu.SemaphoreType.DMA((n,)), pltpu.SemaphoreType.DMA((n,))],
    )(*arrs)


def _allreduce_small(v, name):
    R, C = v.shape

    def body(v_ref, o_ref, slots, send_sems, recv_sems):
        x, y, c = _position()
        me = 4 * x + 2 * y + c
        slots[0] = v_ref[...]
        sends = []
        for r in range(1, 8):
            fx, fy, fc = (r >> 2) & 1, (r >> 1) & 1, r & 1
            to = (x ^ fx, y ^ fy, c ^ fc)
            cp = pltpu.make_async_remote_copy(src_ref=v_ref, dst_ref=slots.at[r], send_sem=send_sems.at[r - 1],
                                              recv_sem=recv_sems.at[r - 1], device_id=to, device_id_type=MESH)
            cp.start()
            sends.append(cp)
        for cp in sends:
            cp.wait()
        acc = slots[me]
        for d in range(1, 8):
            acc = acc + slots[d ^ me]
        o_ref[...] = acc

    vmem = pl.BlockSpec(memory_space=pltpu.VMEM)
    return _pcall(
        body, name=name, in_specs=[vmem], out_specs=vmem, out_shape=jax.ShapeDtypeStruct((R, C), F32),
        scratch_shapes=[pltpu.VMEM((8, R, C), F32), pltpu.SemaphoreType.DMA((7,)), pltpu.SemaphoreType.DMA((7,))],
    )(v)


def _add_pair(which, both, got, name):
    _, N, R, C = both.shape
    tr = _pick(R, 512, 16)

    def body(which_ref, a_ref, b_ref, o_ref):
        o_ref[...] = (a_ref[...].astype(F32) + b_ref[...].astype(F32)).astype(BF16)

    spec = pl.BlockSpec((None, tr, C), lambda n, i, w: (n, i, 0))
    grid_spec = pltpu.PrefetchScalarGridSpec(
        num_scalar_prefetch=1, grid=(N, R // tr),
        in_specs=[pl.BlockSpec((None, None, tr, C), lambda n, i, w: (w[0], n, i, 0)), spec], out_specs=spec)
    return _pcall(body, name=name, grid_spec=grid_spec,
                  out_shape=jax.ShapeDtypeStruct((N, R, C), BF16))(which, both, got)


def _sum_chips(which, own, others, name):
    N, R, C = others.shape
    tr = _pick(R, 512, 16)

    def body(which_ref, own_ref, p_ref, o_ref):
        acc = own_ref[...].astype(F32)
        for n in range(N):
            acc = acc + p_ref[n].astype(F32)
        o_ref[...] = acc

    grid_spec = pltpu.PrefetchScalarGridSpec(
        num_scalar_prefetch=1, grid=(R // tr,),
        in_specs=[pl.BlockSpec((None, tr, C), lambda i, w: (w[0], i, 0)), pl.BlockSpec((N, tr, C), lambda i, w: (0, i, 0))],
        out_specs=pl.BlockSpec((tr, C), lambda i, w: (i, 0)))
    return _pcall(body, name=name, grid_spec=grid_spec,
                  out_shape=jax.ShapeDtypeStruct((R, C), F32))(which, own, others)


BIG = ("w_ffn1_in", "w_ffn1_out", "w_in", "w_br_sb", "w_br_ch", "w_br_fox", "w_out", "w_ffn2_in", "w_ffn2_out")
ROW_SHARDED = ("w_ffn1_out", "w_out", "w_ffn2_out")
SMALL = ("g_ffn1", "g_mix", "b_in", "rel_bias", "g_ffn2", "g_final")


def _pair_sums(split, tag):
    core = lax.axis_index("c").astype(jnp.int32)[None]
    got = _send_other_half(split, f"grad_split_{tag}")
    return [_add_pair(core, a, b, f"grad_pair_sum_{tag}_{i}") for i, (a, b) in enumerate(zip(split, got))]


def _scatter_plan(srcs, bufs, new):
    x, y, c = _position()
    return [(s.at[idx], d.at[k], (*chip, c)) for s, d in zip(srcs, new) for k, (chip, idx) in enumerate(_other_chips(x, y))]


def _scatter_carry(pair):
    landing = [jax.ShapeDtypeStruct((3,) + p.shape[1:], p.dtype) for p in pair]
    return _Carry(pair, [], landing, 3 * len(pair), _scatter_plan)


def _finish_grads(pair, landed, axes, tag):
    chip = (2 * lax.axis_index("x") + lax.axis_index("y")).astype(jnp.int32)[None]
    mine = [_sum_chips(chip, p, q, f"grad_chip_sum_{tag}_{i}") for i, (p, q) in enumerate(zip(pair, landed))]
    theirs = _swap_with_sibling(mine, f"grad_share_{tag}")
    first = lax.axis_index("c") == 0
    return [jnp.concatenate([jnp.where(first, a, b), jnp.where(first, b, a)], axis=ax)
            for a, b, ax in zip(mine, theirs, axes)]


SMALL_SIZES = {"g_ffn1": DEPTH * D_MODEL, "g_mix": DEPTH * D_MODEL, "b_in": DEPTH * IN_WIDTH,
               "rel_bias": DEPTH * N_REL * H_CH, "g_ffn2": DEPTH * D_MODEL, "g_final": D_MODEL}
SMALL_TOTAL = sum(SMALL_SIZES.values()) + 1
SMALL_ROWS = -(-SMALL_TOTAL // (8 * 128)) * 8


def _pack_small(vals, extra):
    flat = [vals[n].reshape(-1) for n in SMALL] + [extra.reshape(-1)]
    flat.append(jnp.zeros((SMALL_ROWS * 128 - SMALL_TOTAL,), F32))
    return jnp.concatenate(flat).reshape(SMALL_ROWS, 128)


def _unpack_small(pack, shapes):
    flat, out, off = pack.reshape(-1), {}, 0
    for n in SMALL:
        out[n] = flat[off:off + SMALL_SIZES[n]].reshape(shapes[n])
        off += SMALL_SIZES[n]
    return out, flat[off]


def _to_heads(t, n_heads):
    return jnp.transpose(t.reshape(t.shape[0], n_heads, HEAD_DIM), (1, 0, 2)).astype(BF16)


def _from_heads(t):
    return jnp.transpose(t, (1, 0, 2)).reshape(t.shape[1], t.shape[0] * HEAD_DIM)


def _pad_keys(t):
    return jnp.pad(t, ((0, 0), (BAND_PAD, 0), (0, 0)))


DIAGONALS = CHUNK + BAND - 1


def _band_bias(table):
    n_far = (LEFT_CHUNKS + 1) * CHUNK + CHUNK - MAX_REL
    near = table[N_REL - 1 - (DIAGONALS - n_far):N_REL - 1][::-1]
    diag = jnp.concatenate([jnp.broadcast_to(table[N_REL - 1], (n_far, H_CH)), near], axis=0).T
    diag = jnp.pad(diag, ((0, 0), (0, 1)))
    skew = jnp.tile(diag, (1, CHUNK))[:, :CHUNK * DIAGONALS].reshape(H_CH, CHUNK, DIAGONALS)
    return skew[:, :, CHUNK - 1:]


def _pad_w_in(w):
    qkv, f, gates = w[:, :QKV_WIDTH], w[:, QKV_WIDTH:QKV_WIDTH + H_FOX], w[:, QKV_WIDTH + H_FOX:]
    return jnp.concatenate([qkv, gates, f, jnp.zeros((w.shape[0], F_PAD - H_FOX), w.dtype)], axis=1)


def _unpad_w_in(w):
    return jnp.concatenate([w[:, :QKV_WIDTH], w[:, QKV_WIDTH + GATE_WIDTH:QKV_WIDTH + GATE_WIDTH + H_FOX],
                            w[:, QKV_WIDTH:QKV_WIDTH + GATE_WIDTH]], axis=1)


QKV_SPLITS = np.cumsum([0, W_SB, W_SB, W_SB, W_CH, W_CH, W_CH, W_FOX, W_FOX, W_FOX])


def _by_chip_rows(g):
    return g.reshape(2, N_CHIPS, g.shape[1] // N_CHIPS, g.shape[2])


def _ffn_fwd(x, g, w_in, w_out, tag, carries):
    h, gate, up, a = _ffn_in_swiglu(x, g, w_in, f"{tag}_in", _carry_of(carries, "in"))
    y = _mm(a, w_out, mode="nn", name=f"{tag}_out", alpha=0.5, res=x, gathered="row",
            carry=_carry_of(carries, "out"))
    return y, (h, gate, up, a)


def _ffn_bwd(x, g, w_in, w_out, saved, dy, tag, carries):
    h, gate, up, a = saved
    da = _mm(dy, w_out, mode="nt", name=f"{tag}_da", alpha=0.5, gathered="row", out_dtype=BF16,
             carry=_carry_of(carries, "da"))
    dw_out = _mm(a, dy, mode="tn", name=f"{tag}_dwout", alpha=0.5, tm_cap=1408, out_dtype=BF16, out_split="row",
                 carry=_carry_of(carries, "dwout"))
    dgu = _swiglu_bwd(gate, up, da, f"{tag}_dact")
    dw_in = _mm(h, dgu, mode="tn", name=f"{tag}_dwin", tm_cap=512, out_dtype=BF16, out_split="col",
                carry=_carry_of(carries, "dwin"))
    dx, dg = _mm(dgu, w_in, mode="nt", name=f"{tag}_dh", gathered="col", tn_cap=1024, carry=_carry_of(carries, "dh"),
                 norm_bwd=(x, g, dy))
    return dx, dg, dw_in, _by_chip_rows(dw_out)


def _mixer_fwd(x, lw, tag, carries):
    T = x.shape[0]
    h = _rmsnorm_fwd(x, lw["g_mix"], f"{tag}_norm")
    p = _mm(h, lw["w_in"], mode="nn", name=f"{tag}_proj", bias=lw["b_in"], tn_cap=896, out_dtype=BF16,
            carry=_carry_of(carries, "proj"))
    parts = [p[:, QKV_SPLITS[i]:QKV_SPLITS[i + 1]] for i in range(9)]
    qa, ka, va = (_to_heads(t, H_SB) for t in parts[0:3])
    qb, kb, vb = (_to_heads(t, H_CH) for t in parts[3:6])
    qc, kc, vc = (_to_heads(t, H_FOX) for t in parts[6:9])
    flog = _mm(lw["w_forget_t"], h, mode="nt", name=f"{tag}_flog")[:8] + lw["b_forget"]
    fcum = _forget_cumsum(flog, f"{tag}_fcum")[:H_FOX]
    fq, fk = fcum.reshape(H_FOX, T, 1), fcum.reshape(H_FOX, T // QB, QB)
    kbp, vbp = _pad_keys(kb), _pad_keys(vb)
    oa = _sb_fwd(qa, ka, va, f"{tag}_sb", _carry_of(carries, "sb"))
    ob = _chunk_fwd(qb, kbp, vbp, lw["bias"], f"{tag}_ch", _carry_of(carries, "ch"))
    oc, lse = _fox_fwd(qc, kc, vc, fq, fk, f"{tag}_fox", _carry_of(carries, "fox"))
    oam, obm, ocm = _from_heads(oa), _from_heads(ob), _from_heads(oc)
    ya, yb, yc, merged = _branches_fwd(p, (oam, obm, ocm), (lw["w_br_sb"], lw["w_br_ch"], lw["w_br_fox"]),
                                       f"{tag}_branches")
    y = _mm(merged, lw["w_out"], mode="nn", name=f"{tag}_out", res=x, gathered="row")
    saved = (h, p, (qa, ka, va), (qb, kbp, vbp), (qc, kc, vc), flog, fq, fk, lse, (oam, obm, ocm),
             (ya, yb, yc), merged)
    return y, saved


def _mixer_bwd(x, lw, saved, dy, tag, carries):
    (h, p, (qa, ka, va), (qb, kbp, vbp), (qc, kc, vc), flog, fq, fk, lse, (oam, obm, ocm),
     (ya, yb, yc), merged) = saved
    T = x.shape[0]
    grads = {}
    dmerged = _mm(dy, lw["w_out"], mode="nt", name=f"{tag}_dmerged", gathered="row",
                  carry=_carry_of(carries, "dmerged"))
    grads["w_out"] = _by_chip_rows(_mm(merged, dy, mode="tn", name=f"{tag}_dwout", tm_cap=1024, out_dtype=BF16,
                                       out_split="row"))
    (dya, dyb, dyc), dgates, (doa, dob, doc) = _branches_bwd(
        p, (ya, yb, yc), dmerged, (lw["w_br_sb"], lw["w_br_ch"], lw["w_br_fox"]), f"{tag}_dbranches")
    grads["w_br_sb"], grads["w_br_ch"], grads["w_br_fox"] = _branches_dw(
        (oam, obm, ocm), (dya, dyb, dyc), f"{tag}_dwbranches")
    dqa, dka, dva = _sb_bwd(qa, ka, va, _to_heads(doa, H_SB), f"{tag}_dsb", _carry_of(carries, "dsb"))
    dqb, dkb, dvb, dbias = _chunk_bwd(qb, kbp, vbp, lw["bias"], _to_heads(dob, H_CH), f"{tag}_dch",
                                      _carry_of(carries, "dch"))
    dqc, dkc, dvc, dfk = _fox_bwd(qc, kc, vc, fq, fk, lse, _to_heads(doc, H_FOX), f"{tag}_dfox",
                                  _carry_of(carries, "dfox"))
    dflog = _forget_cumsum_bwd(flog, jnp.pad(dfk.reshape(H_FOX, T), ((0, 8 - H_FOX), (0, 0))), f"{tag}_dfcum")
    dqkv = [_from_heads(t) for t in (dqa, dka, dva, dqb, dkb, dvb, dqc, dkc, dvc)]
    dforget = jnp.pad(dflog[:H_FOX].T, ((0, 0), (0, F_PAD - H_FOX))).astype(BF16)
    dpb = jnp.concatenate(dqkv + list(dgates) + [dforget], axis=1)
    grads["b_in"] = _colsum(dpb, f"{tag}_dbin")
    dw_in =_unpad_w_in(_mm(h, dpb, mode="tn", name=f"{tag}_dwin", tm_cap=512, tn_cap=896, out_dtype=BF16))
    grads["w_in"] = jnp.transpose(dw_in.reshape(2, D_MODEL // 2, N_CHIPS, IN_WIDTH // N_CHIPS), (0, 2, 1, 3))
    dx, grads["g_mix"] = _mm(dpb, lw["w_in"], mode="nt", name=f"{tag}_dh", tk_cap=896, tn_cap=1024,
                             norm_bwd=(x, lw["g_mix"], dy))
    grads["rel_bias"] = lw["bias_vjp"](dbias)[0]
    return dx, grads


def kernel(x, g_ffn1, w_ffn1_in, w_ffn1_out, g_mix, w_in, b_in, rel_bias, w_br_sb, w_br_ch, w_br_fox, w_out, g_ffn2, w_ffn2_in, w_ffn2_out, g_final, loss_target, m_g_ffn1, m_w_ffn1_in, m_w_ffn1_out, m_g_mix, m_w_in, m_b_in, m_rel_bias, m_w_br_sb, m_w_br_ch, m_w_br_fox, m_w_out, m_g_ffn2, m_w_ffn2_in, m_w_ffn2_out, m_g_final, v_g_ffn1, v_w_ffn1_in, v_w_ffn1_out, v_g_mix, v_w_in, v_b_in, v_rel_bias, v_w_br_sb, v_w_br_ch, v_w_br_fox, v_w_out, v_g_ffn2, v_w_ffn2_in, v_w_ffn2_out, v_g_final):
    names = ("g_ffn1", "w_ffn1_in", "w_ffn1_out", "g_mix", "w_in", "b_in", "rel_bias", "w_br_sb", "w_br_ch",
             "w_br_fox", "w_out", "g_ffn2", "w_ffn2_in", "w_ffn2_out", "g_final")
    w = dict(zip(names, (g_ffn1, w_ffn1_in, w_ffn1_out, g_mix, w_in, b_in, rel_bias, w_br_sb, w_br_ch,
                         w_br_fox, w_out, g_ffn2, w_ffn2_in, w_ffn2_out, g_final)))
    m = dict(zip(names, (m_g_ffn1, m_w_ffn1_in, m_w_ffn1_out, m_g_mix, m_w_in, m_b_in, m_rel_bias, m_w_br_sb,
                         m_w_br_ch, m_w_br_fox, m_w_out, m_g_ffn2, m_w_ffn2_in, m_w_ffn2_out, m_g_final)))
    v = dict(zip(names, (v_g_ffn1, v_w_ffn1_in, v_w_ffn1_out, v_g_mix, v_w_in, v_b_in, v_rel_bias, v_w_br_sb,
                         v_w_br_ch, v_w_br_fox, v_w_out, v_g_ffn2, v_w_ffn2_in, v_w_ffn2_out, v_g_final)))
    xs = x[0]
    tgt = loss_target[0]

    chip = (2 * lax.axis_index("x") + lax.axis_index("y")).astype(jnp.int32)[None]
    placed = [_place_own(w[n], chip, f"place_{n}") for n in BIG]
    bufs = [[p[l].reshape(N_CHIPS, 2, p[l].shape[1] // 2, p[l].shape[2]) for p in placed] for l in range(DEPTH)]
    padded_w_in = {}

    def layer_weights(l):
        lw = {n: b.reshape((N_CHIPS,) + w[n].shape[1:]) for n, b in zip(BIG, bufs[l])}
        if l not in padded_w_in:
            whole = jnp.concatenate([lw["w_in"][j] for j in range(N_CHIPS)], axis=1)
            forget_t = jnp.pad(whole[:, QKV_WIDTH:QKV_WIDTH + H_FOX].T, ((0, F_PAD - H_FOX), (0, 0)))
            padded_w_in[l] = (_pad_w_in(whole), forget_t)
        lw["w_in"], lw["w_forget_t"] = padded_w_in[l]
        lw["b_forget"] = jnp.pad(w["b_in"][l][QKV_WIDTH:QKV_WIDTH + H_FOX], (0, 8 - H_FOX))[:, None]
        lw["b_in"] = _pad_w_in(w["b_in"][l][None, :])
        lw["bias"], lw["bias_vjp"] = jax.vjp(_band_bias, w["rel_bias"][l])
        for n in ("g_ffn1", "g_mix", "g_ffn2"):
            lw[n] = w[n][l][None, :]
        return lw

    def landed_in(l, idx):
        def done(carry):
            for i, b in zip(idx, carry.out[0]):
                bufs[l][i] = b
        return done

    def over_ici(l, idx):
        return lambda: _Carry([], [bufs[l][i] for i in idx], [], 3 * len(idx), _gather_over_ici, landed_in(l, idx))

    def to_sibling(l, idx):
        return lambda: _Carry([], [bufs[l][i] for i in idx], [], 3 * len(idx), _gather_to_sibling, landed_in(l, idx))

    def ffn_fwd(x_in, lw, which, tag, carries):
        return _ffn_fwd(x_in, lw[f"g_{which}"], lw[f"w_{which}_in"], lw[f"w_{which}_out"], tag, carries)

    def ffn_bwd(x_in, lw, which, saved_acts, dy, tag, carries):
        return _ffn_bwd(x_in, lw[f"g_{which}"], lw[f"w_{which}_in"], lw[f"w_{which}_out"], saved_acts, dy, tag,
                        carries)

    FFN1, W_IN, MIXER_REST, FFN2_IN, FFN2_OUT = (0, 1), (2,), (3, 4, 5, 6), (7,), (8,)
    first = FFN1 + W_IN
    for i, b in zip(first, _gather_layer([bufs[0][i] for i in first], "gather_l0")):
        bufs[0][i] = b
    fwd_carries = [
        {"ffn1": {"in": [over_ici(0, MIXER_REST)], "out": [to_sibling(0, MIXER_REST), over_ici(0, FFN2_OUT)]},
         "mix": {"proj": [to_sibling(0, FFN2_OUT), over_ici(1, MIXER_REST)],
                 "sb": [over_ici(0, FFN2_IN), over_ici(1, FFN2_OUT)],
                 "ch": [to_sibling(0, FFN2_IN), over_ici(1, FFN1)],
                 "fox": [over_ici(1, W_IN)]},
         "ffn2": {"in": [to_sibling(1, MIXER_REST + FFN2_OUT + FFN1 + W_IN)]}},
        {"ffn1": {}, "mix": {"sb": [over_ici(1, FFN2_IN)], "ch": [to_sibling(1, FFN2_IN)]}, "ffn2": {}},
    ]

    saved = []
    act = xs
    for l in range(DEPTH):
        x0 = act
        x1, s1 = ffn_fwd(x0, layer_weights(l), "ffn1", f"l{l}_ffn1", fwd_carries[l]["ffn1"])
        x2, s2 = _mixer_fwd(x1, layer_weights(l), f"l{l}_mix", fwd_carries[l]["mix"])
        act, s3 = ffn_fwd(x2, layer_weights(l), "ffn2", f"l{l}_ffn2", fwd_carries[l]["ffn2"])
        saved.append((x0, s1, x1, s2, x2, s3))
    layers = [layer_weights(l) for l in range(DEPTH)]

    dact, dg_final, loss_row = _final_loss(act, w["g_final"][None, :], tgt, "final_loss")

    big_grads = {n: [None] * DEPTH for n in BIG}
    small_grads = {n: [None] * DEPTH for n in ("g_ffn1", "g_mix", "b_in", "rel_bias", "g_ffn2")}
    pair = [[None] * len(BIG) for _ in range(DEPTH)]
    landed = [[None] * len(BIG) for _ in range(DEPTH)]

    def pair_up(l, idx, tag):
        for i, p in zip(idx, _pair_sums([big_grads[BIG[i]][l] for i in idx], tag)):
            pair[l][i] = p

    core = lax.axis_index("c").astype(jnp.int32)[None]

    def to_sibling_half(l, idx, tag):
        def plan(srcs, bufs, new):
            x, y, c = _position()
            return [(s.at[1 - c], d, (x, y, 1 - c)) for s, d in zip(srcs, new)]
        def done(carry):
            for j, (i, got) in enumerate(zip(idx, carry.out[1])):
                pair[l][i] = _add_pair(core, big_grads[BIG[i]][l], got, f"grad_pair_sum_{tag}_{j}")
        def make():
            split = [big_grads[BIG[i]][l] for i in idx]
            return _Carry(split, [], [jax.ShapeDtypeStruct(s.shape[1:], s.dtype) for s in split], len(idx), plan, done)
        return make

    def exchange(l, idx):
        def done(carry):
            for i, a in zip(idx, carry.out[1]):
                landed[l][i] = a
        def make():
            carry = _scatter_carry([pair[l][i] for i in idx])
            carry.done = done
            return carry
        return make

    def exchange_one_way(l, i, k):
        def plan(srcs, bufs, new):
            x, y, c = _position()
            chip_k, idx_k = _other_chips(x, y)[k]
            return [(srcs[0].at[idx_k], bufs[0].at[k], (*chip_k, c))]
        def done(carry):
            landed[l][i] = carry.out[0][0]
        def make():
            if landed[l][i] is None:
                landed[l][i] = lax.empty((3,) + pair[l][i].shape[1:], BF16)
            return _Carry([pair[l][i]], [landed[l][i]], [], 1, plan, done)
        return make

    bwd_carries = [
        {"ffn2": {},
         "mix": {"dmerged": [to_sibling_half(0, FFN2_IN + FFN2_OUT, "l0_ffn2")],
                 "dsb": [exchange(1, FFN1 + W_IN)], "dch": [exchange(1, MIXER_REST + FFN2_IN + FFN2_OUT)],
                 "dfox": [exchange(0, FFN2_IN + FFN2_OUT)]},
         "ffn1": {"da": [exchange(0, MIXER_REST)], "dwout": [exchange_one_way(0, 2, 0)],
                  "dwin": [exchange_one_way(0, 2, 1)], "dh": [exchange_one_way(0, 2, 2)]}},
        {"ffn2": {}, "mix": {},
         "ffn1": {"da": [to_sibling_half(1, W_IN + MIXER_REST + FFN2_IN + FFN2_OUT, "l1_rest")]}},
    ]
    for l in reversed(range(DEPTH)):
        lw = layers[l]
        x0, s1, x1, s2, x2, s3 = saved[l]
        dx2, small_grads["g_ffn2"][l], big_grads["w_ffn2_in"][l], big_grads["w_ffn2_out"][l] = ffn_bwd(
            x2, lw, "ffn2", s3, dact, f"l{l}_ffn2", bwd_carries[l]["ffn2"])
        dx1, mg = _mixer_bwd(x1, lw, s2, dx2, f"l{l}_mix", bwd_carries[l]["mix"])
        for n in ("w_in", "w_br_sb", "w_br_ch", "w_br_fox", "w_out"):
            big_grads[n][l] = mg[n]
        small_grads["b_in"][l] = _unpad_w_in(mg["b_in"])[0]
        small_grads["g_mix"][l] = mg["g_mix"][0]
        small_grads["rel_bias"][l] = mg["rel_bias"]
        if l == 0:
            pair_up(0, W_IN + MIXER_REST, "l0_mix")
        dact, dg1, big_grads["w_ffn1_in"][l], big_grads["w_ffn1_out"][l] = ffn_bwd(
            x0, lw, "ffn1", s1, dx1, f"l{l}_ffn1", bwd_carries[l]["ffn1"])
        small_grads["g_ffn1"][l] = dg1[0]
        small_grads["g_ffn2"][l] = small_grads["g_ffn2"][l][0]
        if l == 1:
            pair_up(1, FFN1, "l1_ffn1")
    grad_x = dact[None]
    pair_up(0, FFN1, "l0_ffn1")
    for i, a in zip(FFN1, _scatter_chips([pair[0][i] for i in FFN1], "grad_scatter_l0")):
        landed[0][i] = a

    small = {n: jnp.stack(small_grads[n]) for n in small_grads}
    small["g_final"] = dg_final[0]
    small_sum, loss = _unpack_small(_allreduce_small(_pack_small(small, loss_row[0, :1]), "allreduce_small"),
                                    {n: w[n].shape for n in SMALL})

    axes = [1 if n in ROW_SHARDED else 0 for n in BIG] * DEPTH
    summed = _finish_grads(pair[0] + pair[1], landed[0] + landed[1], axes, "all")
    grads = {n: jnp.stack([summed[i], summed[len(BIG) + i]]) for i, n in enumerate(BIG)}
    grads.update(small_sum)

    delta, new_m, new_v = {}, {}, {}
    for n in BIG:
        shape = w[n].shape
        flat = lambda t: t.reshape(-1, shape[-1])
        d, nm, nv = _adamw(flat(w[n]), flat(grads[n]), flat(m[n]), flat(v[n]), f"adamw_{n}")
        delta[n], new_m[n], new_v[n] = d.reshape(shape), nm.reshape(shape), nv.reshape(shape)
    zero = jnp.zeros((1,), F32)
    sd, sm, sv = _adamw(_pack_small(w, zero), _pack_small(grads, zero), _pack_small(m, zero), _pack_small(v, zero),
                        "adamw_small")
    shapes = {n: w[n].shape for n in SMALL}
    for dst, src in ((delta, sd), (new_m, sm), (new_v, sv)):
        dst.update(_unpack_small(src, shapes)[0])

    return (loss, grad_x, *[grads[n] for n in names], *[delta[n] for n in names],
            *[new_m[n] for n in names], *[new_v[n] for n in names])
```

```python
import functools

import numpy as np
import jax
import jax.numpy as jnp
from jax import lax
from jax.experimental import pallas as pl
from jax.experimental.pallas import tpu as pltpu

F32 = jnp.float32
BF16 = jnp.bfloat16

D_MODEL = 1024
DEPTH = 2
CHUNK = 64
HEAD_DIM = 64
H_SB, H_CH, H_FOX = 4, 8, 4
W_SB, W_CH, W_FOX = H_SB * HEAD_DIM, H_CH * HEAD_DIM, H_FOX * HEAD_DIM
LEFT_CHUNKS = 8
MAX_REL = 128
N_REL = 2 * MAX_REL + 1
D_FF = 2816
QKV_WIDTH = 3 * (W_SB + W_CH + W_FOX)
GATE_WIDTH = 3 * D_MODEL
IN_WIDTH = QKV_WIDTH + H_FOX + GATE_WIDTH
F_PAD = 128
P_WIDTH = QKV_WIDTH + GATE_WIDTH + F_PAD
RMS_EPS = 1e-6
NEG = -1e30
SCALE = HEAD_DIM ** -0.5
QB = 256
BAND = (LEFT_CHUNKS + 2) * CHUNK
BAND_PAD = BAND - CHUNK

ADAM_LR, ADAM_B1, ADAM_B2, ADAM_EPS, ADAM_WD, ADAM_STEP = 0.001, 0.9, 0.999, 1e-08, 0.01, 10

N_CHIPS = 4
VMEM_BUDGET = 52 * 1024 * 1024

NT = (((1,), (1,)), ((), ()))
TN = (((0,), (0,)), ((), ()))
NN = (((1,), (0,)), ((), ()))
MESH = pl.DeviceIdType.MESH


def _pcall(body, **kw):
    return pl.pallas_call(body, **kw)


class _Carry:
    def __init__(self, srcs, bufs, new, count, plan, done=None):
        self.srcs, self.bufs, self.new, self.count, self.plan = list(srcs), list(bufs), list(new), count, plan
        self.out, self.done = None, done

    @staticmethod
    def join(parts):
        parts = [p for p in parts if p is not None]
        if not parts:
            return None

        def plan(srcs, bufs, new):
            copies, s, b, n = [], 0, 0, 0
            for p in parts:
                copies += p.plan(srcs[s:s + len(p.srcs)], bufs[b:b + len(p.bufs)], new[n:n + len(p.new)])
                s, b, n = s + len(p.srcs), b + len(p.bufs), n + len(p.new)
            return copies

        whole = _Carry(sum((p.srcs for p in parts), []), sum((p.bufs for p in parts), []),
                       sum((p.new for p in parts), []), sum(p.count for p in parts), plan)
        whole.parts = parts
        return whole

    def share_out(self):
        b, n = 0, 0
        for p in getattr(self, "parts", []):
            p.out = (self.out[0][b:b + len(p.bufs)], self.out[1][n:n + len(p.new)])
            b, n = b + len(p.bufs), n + len(p.new)
            p.share_out()
        if self.done is not None:
            self.done(self)


def _carry_of(carries, key):
    return _Carry.join([make() for make in carries.get(key, [])])


def _pcall_carrying(body, carry, **kw):
    if carry is None:
        return _pcall(body, **kw)
    in_specs = list(kw.pop("in_specs"))
    out_specs, out_shape = kw.pop("out_specs"), kw.pop("out_shape")
    single = not isinstance(out_shape, (list, tuple))
    out_specs = [out_specs] if single else list(out_specs)
    out_shape = [out_shape] if single else list(out_shape)
    scratch = list(kw.pop("scratch_shapes", []))
    grid = tuple(kw.get("grid", ()))
    n_in, n_out, n_scr = len(in_specs), len(out_specs), len(scratch)
    ns, nb, nn = len(carry.srcs), len(carry.bufs), len(carry.new)

    def body2(*refs):
        ins, srcs = refs[:n_in], refs[n_in:n_in + ns]
        at = n_in + ns + nb
        outs, bufs, new = refs[at:at + n_out], refs[at + n_out:at + n_out + nb], refs[at + n_out + nb:at + n_out + nb + nn]
        at += n_out + nb + nn
        scr, (send_sems, recv_sems) = refs[at:at + n_scr], refs[at + n_scr:]

        def copies():
            return [_remote_copy(s, d, send_sems, recv_sems, k, to)
                    for k, (s, d, to) in enumerate(carry.plan(srcs, bufs, new))]

        def start():
            for cp in copies():
                cp.start()

        def wait():
            for cp in copies():
                cp.wait()

        if grid:
            ids = [pl.program_id(a) for a in range(len(grid))]
            pl.when(functools.reduce(jnp.logical_and, [i == 0 for i in ids]))(start)
        else:
            start()
        body(*ins, *outs, *scr)
        if grid:
            pl.when(functools.reduce(jnp.logical_and, [i == g - 1 for i, g in zip(ids, grid)]))(wait)
        else:
            wait()

    call = _pcall(
        body2, in_specs=in_specs + [HBM_SPEC] * (ns + nb), out_specs=out_specs + [HBM_SPEC] * (nb + nn),
        out_shape=out_shape + [jax.ShapeDtypeStruct(b.shape, b.dtype) for b in carry.bufs] + carry.new,
        scratch_shapes=scratch + [pltpu.SemaphoreType.DMA((carry.count,)), pltpu.SemaphoreType.DMA((carry.count,))],
        input_output_aliases={n_in + ns + j: n_out + j for j in range(nb)}, **kw)

    def apply(*args):
        res = call(*args, *carry.srcs, *carry.bufs)
        carry.out = (list(res[n_out:n_out + nb]), list(res[n_out + nb:]))
        carry.share_out()
        return res[0] if single else list(res[:n_out])

    return apply


def _pick(n, cap, mult=128):
    best = None
    d = mult
    while d <= min(n, cap):
        if n % d == 0:
            best = d
        d += mult
    return n if best is None else best


def _nbytes(shape, dtype):
    return int(np.prod(shape)) * jnp.dtype(dtype).itemsize


def _mm(a, b, *, mode, name, out_dtype=F32, alpha=1.0, bias=None, res=None, tm_cap=1024, tn_cap=512,
        tk_cap=2816, gathered=None, out_split=None, carry=None, norm_bwd=None):
    if mode == "tn":
        K, M = a.shape
    else:
        M, K = a.shape
    dn = {"nn": NN, "nt": NT, "tn": TN}[mode]
    b_rows = None
    if gathered is None:
        N = b.shape[0] if mode == "nt" else b.shape[1]
        tn = {None: _pick(N, tn_cap), "col": N // N_CHIPS, "row": N // 2}[out_split]
        if out_split == "col":
            tm_cap = min(tm_cap, M // 2)
        tk = K if K <= tk_cap else _pick(K, tk_cap)
        b_spec = (pl.BlockSpec((tn, tk), lambda i, j, k: (j, k)) if mode == "nt"
                  else pl.BlockSpec((tk, tn), lambda i, j, k: (k, j)))
    else:
        r, c = b.shape[1:]
        rows, cols = (r, N_CHIPS * c) if gathered == "col" else (N_CHIPS * r, c)
        N = rows if mode == "nt" else cols
        assert K == (cols if mode == "nt" else rows), (name, K, rows, cols)
        if gathered == "col" and mode == "nn":
            tn, tk = c, (r if r <= tk_cap else _pick(r, tk_cap))
            b_spec = pl.BlockSpec((None, tk, c), lambda i, j, k: (j, k, 0))
        elif gathered == "col":
            tn, tk = _pick(r, tn_cap), c
            b_spec = pl.BlockSpec((None, tn, c), lambda i, j, k: (k, j, 0))
        elif mode == "nn":
            tn, tk, b_rows = _pick(c, tn_cap), K, N_CHIPS
            b_spec = pl.BlockSpec((N_CHIPS, r, tn), lambda i, j, k: (0, 0, j))
        else:
            tn, tk, b_rows = 2 * r, K, 2
            b_spec = pl.BlockSpec((2, r, c), lambda i, j, k: (j, 0, 0))
    tm = _pick(M, tm_cap)
    nk = K // tk

    a_spec = (pl.BlockSpec((tk, tm), lambda i, j, k: (k, i)) if mode == "tn"
              else pl.BlockSpec((tm, tk), lambda i, j, k: (i, k)))
    in_specs = [a_spec, b_spec]
    args = [a, b]
    if bias is not None:
        in_specs.append(pl.BlockSpec((1, tn), lambda i, j, k: (0, j)))
        args.append(bias)
    if res is not None:
        in_specs.append(pl.BlockSpec((tm, tn), lambda i, j, k: (i, j)))
        args.append(res)
    if norm_bwd is not None:
        assert tn == N and alpha == 1.0 and out_split is None and bias is None and res is None, name
        x_in, g_in, dres_in = norm_bwd
        in_specs += [pl.BlockSpec((tm, tn), lambda i, j, k: (i, 0)), pl.BlockSpec((1, tn), lambda i, j, k: (0, 0)),
                     pl.BlockSpec((tm, tn), lambda i, j, k: (i, 0))]
        args += [x_in, g_in, dres_in]

    est = 2 * (_nbytes((tm, tk), a.dtype) + _nbytes((tk, tn), b.dtype) + _nbytes((tm, tn), out_dtype))
    est += _nbytes((tm, tn), F32) * (3 if res is not None else 1)
    est += _nbytes((tm, tn), F32) * (4 if norm_bwd is not None else 0)
    assert est < VMEM_BUDGET, (name, est)

    def body(*refs):
        a_ref, b_ref = refs[0], refs[1]
        pos = 2
        bias_ref = res_ref = None
        if bias is not None:
            bias_ref = refs[pos]
            pos += 1
        if res is not None:
            res_ref = refs[pos]
            pos += 1
        if norm_bwd is not None:
            x_ref, g_ref, dres_ref = refs[pos:pos + 3]
            pos += 3
        o_ref = refs[pos]
        if norm_bwd is not None:
            dg_ref = refs[pos + 1]
            pos += 1
        acc_ref = refs[pos + 1] if nk > 1 else None

        bv = b_ref[...]
        if b_rows is not None:
            bv = bv.reshape(b_rows * bv.shape[1], bv.shape[2])
        part = lax.dot_general(a_ref[...].astype(BF16), bv.astype(BF16), dn, preferred_element_type=F32)

        def finish(acc):
            if alpha != 1.0:
                acc = acc * alpha
            if bias_ref is not None:
                acc = acc + bias_ref[...]
            if res_ref is not None:
                acc = acc + res_ref[...]
            if norm_bwd is not None:
                xv = x_ref[...]
                rstd = lax.rsqrt(jnp.mean(xv * xv, axis=-1, keepdims=True) + RMS_EPS)
                xhat = xv * rstd
                dyg = acc * g_ref[...]
                part_g = jnp.sum(acc * xhat, axis=0, keepdims=True)
                acc = dres_ref[...] + rstd * (dyg - xhat * jnp.mean(dyg * xhat, axis=-1, keepdims=True))
                first = pl.program_id(0) == 0

                @pl.when(first)
                def _():
                    dg_ref[...] = part_g

                @pl.when(jnp.logical_not(first))
                def _():
                    dg_ref[...] += part_g
            o_ref[...] = acc.astype(out_dtype)

        if nk == 1:
            finish(part)
        else:
            k = pl.program_id(2)

            @pl.when(k == 0)
            def _():
                acc_ref[...] = part

            @pl.when(k > 0)
            def _():
                acc_ref[...] += part

            @pl.when(k == nk - 1)
            def _():
                finish(acc_ref[...])

    if out_split == "col":
        per_half = M // 2 // tm
        out_spec = pl.BlockSpec((None, None, tm, tn), lambda i, j, k: (i // per_half, j, i % per_half, 0))
        out_shape = jax.ShapeDtypeStruct((2, N_CHIPS, M // 2, tn), out_dtype)
    elif out_split == "row":
        out_spec = pl.BlockSpec((None, tm, tn), lambda i, j, k: (j, i, 0))
        out_shape = jax.ShapeDtypeStruct((2, M, tn), out_dtype)
    else:
        out_spec = pl.BlockSpec((tm, tn), lambda i, j, k: (i, j))
        out_shape = jax.ShapeDtypeStruct((M, N), out_dtype)
    semantics = ("parallel", "parallel", "arbitrary")
    if norm_bwd is not None:
        out_spec = [out_spec, pl.BlockSpec((1, tn), lambda i, j, k: (0, 0))]
        out_shape = [out_shape, jax.ShapeDtypeStruct((1, N), F32)]
        semantics = ("arbitrary", "arbitrary", "arbitrary")
    return _pcall_carrying(
        body, carry, name=name, grid=(M // tm, N // tn, nk), in_specs=in_specs, out_specs=out_spec,
        out_shape=out_shape,
        scratch_shapes=[pltpu.VMEM((tm, tn), F32)] if nk > 1 else [],
        compiler_params=pltpu.CompilerParams(dimension_semantics=semantics),
    )(*args)


def _rmsnorm_fwd(x, g, name):
    T, Dm = x.shape
    tm = _pick(T, 256, 8)

    def body(x_ref, g_ref, h_ref):
        xv = x_ref[...]
        rstd = lax.rsqrt(jnp.mean(xv * xv, axis=-1, keepdims=True) + RMS_EPS)
        h_ref[...] = (xv * rstd * g_ref[...]).astype(BF16)

    return _pcall(
        body, name=name, grid=(T // tm,),
        in_specs=[pl.BlockSpec((tm, Dm), lambda i: (i, 0)), pl.BlockSpec((1, Dm), lambda i: (0, 0))],
        out_specs=pl.BlockSpec((tm, Dm), lambda i: (i, 0)),
        out_shape=jax.ShapeDtypeStruct((T, Dm), BF16),
    )(x, g)


def _final_loss(x, g, tgt, name):
    T, Dm = x.shape
    tm = _pick(T, 256, 8)

    def body(x_ref, g_ref, t_ref, dx_ref, dg_ref, loss_ref):
        xv = x_ref[...]
        gv = g_ref[...]
        rstd = lax.rsqrt(jnp.mean(xv * xv, axis=-1, keepdims=True) + RMS_EPS)
        xhat = xv * rstd
        err = xhat * gv - t_ref[...]
        part_loss = 0.5 * jnp.sum(jnp.mean(err * err, axis=-1, keepdims=True), axis=0, keepdims=True)
        dy = err * (1.0 / Dm)
        dyg = dy * gv
        dx_ref[...] = rstd * (dyg - xhat * jnp.mean(dyg * xhat, axis=-1, keepdims=True))
        part_g = jnp.sum(dy * xhat, axis=0, keepdims=True)
        part_l = jnp.broadcast_to(part_loss, (1, 128))

        @pl.when(pl.program_id(0) == 0)
        def _():
            dg_ref[...] = part_g
            loss_ref[...] = part_l

        @pl.when(pl.program_id(0) > 0)
        def _():
            dg_ref[...] += part_g
            loss_ref[...] += part_l

    row = pl.BlockSpec((tm, Dm), lambda i: (i, 0))
    vec = pl.BlockSpec((1, Dm), lambda i: (0, 0))
    return _pcall(
        body, name=name, grid=(T // tm,), in_specs=[row, vec, row],
        out_specs=[row, vec, pl.BlockSpec((1, 128), lambda i: (0, 0))],
        out_shape=[jax.ShapeDtypeStruct((T, Dm), F32), jax.ShapeDtypeStruct((1, Dm), F32),
                   jax.ShapeDtypeStruct((1, 128), F32)],
        compiler_params=pltpu.CompilerParams(dimension_semantics=("arbitrary",)),
    )(x, g, tgt)


def _sigmoid(z):
    return 1.0 / (1.0 + jnp.exp(-z))


def _ffn_in_swiglu(x, g, w_in, name, carry=None):
    T, Dm = x.shape
    cw = w_in.shape[2]
    tm = _pick(T, 512, 16)

    def body(x_ref, gain_ref, wg_ref, wu_ref, h_ref, g_ref, u_ref, a_ref):
        @pl.when(pl.program_id(1) == 0)
        def _():
            xv = x_ref[...]
            rstd = lax.rsqrt(jnp.mean(xv * xv, axis=-1, keepdims=True) + RMS_EPS)
            h_ref[...] = (xv * rstd * gain_ref[...]).astype(BF16)

        hv = h_ref[...]
        gv = jnp.dot(hv, wg_ref[...], preferred_element_type=F32)
        uv = jnp.dot(hv, wu_ref[...], preferred_element_type=F32)
        g_ref[...] = gv.astype(BF16)
        u_ref[...] = uv.astype(BF16)
        a_ref[...] = (gv * _sigmoid(gv) * uv).astype(BF16)

    row = pl.BlockSpec((tm, Dm), lambda i, j: (i, 0))
    out = pl.BlockSpec((tm, cw), lambda i, j: (i, j))
    return _pcall_carrying(
        body, carry, name=name, grid=(T // tm, 2),
        in_specs=[row, pl.BlockSpec((1, Dm), lambda i, j: (0, 0)), pl.BlockSpec((None, Dm, cw), lambda i, j: (j, 0, 0)),
                  pl.BlockSpec((None, Dm, cw), lambda i, j: (j + 2, 0, 0))],
        out_specs=[row, out, out, out],
        out_shape=[jax.ShapeDtypeStruct((T, Dm), BF16)] + [jax.ShapeDtypeStruct((T, D_FF), BF16)] * 3,
        compiler_params=pltpu.CompilerParams(dimension_semantics=("parallel", "arbitrary")),
    )(x, g, w_in, w_in)


def _swiglu_bwd(g, u, da, name):
    T = g.shape[0]
    tm = _pick(T, 256, 16)

    def body(g_ref, u_ref, da_ref, d_ref):
        gv = g_ref[...].astype(F32)
        uv = u_ref[...].astype(F32)
        dav = da_ref[...].astype(F32)
        sg = _sigmoid(gv)
        d_ref[:, :D_FF] = (dav * uv * (sg + gv * sg * (1.0 - sg))).astype(BF16)
        d_ref[:, D_FF:] = (dav * gv * sg).astype(BF16)

    row = pl.BlockSpec((tm, D_FF), lambda i: (i, 0))
    return _pcall(
        body, name=name, grid=(T // tm,), in_specs=[row, row, row],
        out_specs=pl.BlockSpec((tm, 2 * D_FF), lambda i: (i, 0)),
        out_shape=jax.ShapeDtypeStruct((T, 2 * D_FF), BF16),
    )(g, u, da)


def _branches_fwd(p, outs, weights, name):
    T = p.shape[0]
    cw = weights[0].shape[2]
    tm = _pick(T, 1024, 16)
    first_gate, per_branch = QKV_WIDTH // cw, D_MODEL // cw

    def body(*refs):
        gates, o_refs, w_refs, y_refs, m_ref = refs[0:3], refs[3:6], refs[6:9], refs[9:12], refs[12]
        merged = None
        for g_ref, o_ref, w_ref, y_ref in zip(gates, o_refs, w_refs, y_refs):
            y = jnp.dot(o_ref[...], w_ref[...], preferred_element_type=F32)
            y_ref[...] = y
            term = _sigmoid(g_ref[...].astype(F32)) * y
            merged = term if merged is None else merged + term
        m_ref[...] = merged.astype(BF16)

    gate_specs = [pl.BlockSpec((tm, cw), functools.partial(
        lambda i, j, kk: (i, first_gate + per_branch * kk + j), kk=kk)) for kk in range(3)]
    o_specs = [pl.BlockSpec((tm, o.shape[1]), lambda i, j: (i, 0)) for o in outs]
    w_specs = [pl.BlockSpec((None, wk.shape[1], cw), lambda i, j: (j, 0, 0)) for wk in weights]
    tile = pl.BlockSpec((tm, cw), lambda i, j: (i, j))
    return _pcall(
        body, name=name, grid=(T // tm, N_CHIPS), in_specs=gate_specs + o_specs + w_specs, out_specs=[tile] * 4,
        out_shape=[jax.ShapeDtypeStruct((T, D_MODEL), F32)] * 3 + [jax.ShapeDtypeStruct((T, D_MODEL), BF16)],
    )(p, p, p, *outs, *weights)


def _branches_bwd(p, ys, dm, weights, name):
    T = p.shape[0]
    cw = weights[0].shape[2]
    tm = _pick(T, 1024, 16)
    first_gate, per_branch = QKV_WIDTH // cw, D_MODEL // cw
    widths = [wk.shape[1] for wk in weights]

    def body(*refs):
        gates, y_refs, dm_ref, w_refs = refs[0:3], refs[3:6], refs[6], refs[7:10]
        dy_refs, dg_refs, do_refs, acc_refs = refs[10:13], refs[13:16], refs[16:19], refs[19:22]
        j = pl.program_id(1)
        dmv = dm_ref[...]
        for g_ref, y_ref, w_ref, dy_ref, dg_ref, do_ref, acc_ref in zip(gates, y_refs, w_refs, dy_refs, dg_refs,
                                                                       do_refs, acc_refs):
            s = _sigmoid(g_ref[...].astype(F32))
            dy = (s * dmv).astype(BF16)
            dy_ref[...] = dy
            dg_ref[...] = (dmv * y_ref[...] * s * (1.0 - s)).astype(BF16)
            part = _dot_nt(dy, w_ref[...])

            @pl.when(j == 0)
            def _():
                acc_ref[...] = part

            @pl.when(j > 0)
            def _():
                acc_ref[...] += part

            @pl.when(j == N_CHIPS - 1)
            def _():
                do_ref[...] = acc_ref[...].astype(BF16)

    gate_specs = [pl.BlockSpec((tm, cw), functools.partial(
        lambda i, j, kk: (i, first_gate + per_branch * kk + j), kk=kk)) for kk in range(3)]
    tile = pl.BlockSpec((tm, cw), lambda i, j: (i, j))
    w_specs = [pl.BlockSpec((None, r, cw), lambda i, j: (j, 0, 0)) for r in widths]
    do_specs = [pl.BlockSpec((tm, r), lambda i, j: (i, 0)) for r in widths]
    wide = jax.ShapeDtypeStruct((T, D_MODEL), BF16)
    out = _pcall(
        body, name=name, grid=(T // tm, N_CHIPS), in_specs=gate_specs + [tile] * 4 + w_specs,
        out_specs=[tile] * 6 + do_specs,
        out_shape=[wide] * 6 + [jax.ShapeDtypeStruct((T, r), BF16) for r in widths],
        scratch_shapes=[pltpu.VMEM((tm, r), F32) for r in widths],
        compiler_params=pltpu.CompilerParams(dimension_semantics=("parallel", "arbitrary")),
    )(p, p, p, *ys, dm, *weights)
    return out[0:3], out[3:6], out[6:9]


def _branches_dw(outs, dys, name):
    T = dys[0].shape[0]
    cw = D_MODEL // N_CHIPS
    widths = [o.shape[1] for o in outs]

    def body(*refs):
        o_refs, dy_refs, dw_refs = refs[0:3], refs[3:6], refs[6:9]
        for o_ref, dy_ref, dw_ref, r in zip(o_refs, dy_refs, dw_refs, widths):
            dw = _dot_tn(o_ref[...], dy_ref[...]).astype(BF16)
            dw_ref[0] = dw[:r // 2]
            dw_ref[1] = dw[r // 2:]

    return _pcall(
        body, name=name, grid=(N_CHIPS,),
        in_specs=[pl.BlockSpec((T, r), lambda j: (0, 0)) for r in widths] + [pl.BlockSpec((T, cw), lambda j: (0, j))] * 3,
        out_specs=[pl.BlockSpec((2, None, r // 2, cw), lambda j: (0, j, 0, 0)) for r in widths],
        out_shape=[jax.ShapeDtypeStruct((2, N_CHIPS, r // 2, cw), BF16) for r in widths],
    )(*outs, *dys)


def _colsum(a, name):
    T, N = a.shape
    tm = _pick(T, 256, 16)

    def body(a_ref, o_ref):
        part = jnp.sum(a_ref[...].astype(F32), axis=0, keepdims=True)

        @pl.when(pl.program_id(0) == 0)
        def _():
            o_ref[...] = part

        @pl.when(pl.program_id(0) > 0)
        def _():
            o_ref[...] += part

    return _pcall(
        body, name=name, grid=(T // tm,), in_specs=[pl.BlockSpec((tm, N), lambda i: (i, 0))],
        out_specs=pl.BlockSpec((1, N), lambda i: (0, 0)), out_shape=jax.ShapeDtypeStruct((1, N), F32),
        compiler_params=pltpu.CompilerParams(dimension_semantics=("arbitrary",)),
    )(a)


def _adamw(w, g, m, v, name):
    R, C = w.shape
    tr = 256 if R % 256 == 0 else R
    c1 = 1.0 / (1.0 - ADAM_B1 ** ADAM_STEP)
    c2 = 1.0 / (1.0 - ADAM_B2 ** ADAM_STEP)

    def body(w_ref, g_ref, m_ref, v_ref, d_ref, nm_ref, nv_ref):
        gv = g_ref[...]
        mn = ADAM_B1 * m_ref[...] + (1.0 - ADAM_B1) * gv
        vn = ADAM_B2 * v_ref[...] + (1.0 - ADAM_B2) * (gv * gv)
        nm_ref[...] = mn
        nv_ref[...] = vn
        d_ref[...] = -ADAM_LR * ((mn * c1) / (jnp.sqrt(vn * c2) + ADAM_EPS) + ADAM_WD * w_ref[...])

    spec = pl.BlockSpec((tr, C), lambda i: (i, 0))
    return _pcall(
        body, name=name, grid=(R // tr,), in_specs=[spec] * 4, out_specs=[spec] * 3,
        out_shape=[jax.ShapeDtypeStruct((R, C), F32)] * 3,
    )(w, g, m, v)


def _adamw_from_halves(w, m, v, halves, axis, core, name):
    L, r, c = w.shape
    assert L == len(halves) == 2
    c1 = 1.0 / (1.0 - ADAM_B1 ** ADAM_STEP)
    c2 = 1.0 / (1.0 - ADAM_B2 ** ADAM_STEP)
    if axis == 0:
        tr = _pick(r // 2, 256, 8)
        per_half = r // 2 // tr
        steps = 2 * per_half
        half_spec = pl.BlockSpec((tr, c), lambda l, i, core_ref: (i % per_half, 0))
    else:
        tr = _pick(r, 256, 8)
        steps = r // tr
        half_spec = pl.BlockSpec((tr, c // 2), lambda l, i, core_ref: (i, 0))
    whole = pl.BlockSpec((None, tr, c), lambda l, i, core_ref: (l, i, 0))

    def body(core_ref, w_ref, m_ref, v_ref, mine0, theirs0, mine1, theirs1, g_ref, d_ref, nm_ref, nv_ref):
        shape = mine0.shape
        layer0 = jnp.full(shape, pl.program_id(0), jnp.int32) == 0
        mine = jnp.where(layer0, mine0[...], mine1[...])
        theirs = jnp.where(layer0, theirs0[...], theirs1[...])
        core_v = jnp.full(shape, core_ref[0], jnp.int32)

        def update(gv, cols):
            mn = ADAM_B1 * m_ref[:, cols] + (1.0 - ADAM_B1) * gv
            vn = ADAM_B2 * v_ref[:, cols] + (1.0 - ADAM_B2) * (gv * gv)
            g_ref[:, cols] = gv
            nm_ref[:, cols] = mn
            nv_ref[:, cols] = vn
            d_ref[:, cols] = -ADAM_LR * ((mn * c1) / (jnp.sqrt(vn * c2) + ADAM_EPS) + ADAM_WD * w_ref[:, cols])

        if axis == 0:
            here = jnp.full(shape, pl.program_id(1) // per_half, jnp.int32)
            update(jnp.where(here == core_v, mine, theirs), slice(None))
        else:
            update(jnp.where(core_v == 0, mine, theirs), slice(0, c // 2))
            update(jnp.where(core_v == 0, theirs, mine), slice(c // 2, c))

    grid_spec = pltpu.PrefetchScalarGridSpec(
        num_scalar_prefetch=1, grid=(L, steps), in_specs=[whole] * 3 + [half_spec] * 4, out_specs=[whole] * 4)
    return _pcall(body, name=name, grid_spec=grid_spec, out_shape=[jax.ShapeDtypeStruct((L, r, c), F32)] * 4,
                  )(core, w, m, v, *halves[0], *halves[1])


def _log_sigmoid(z):
    return jnp.minimum(z, 0.0) - jnp.log(1.0 + jnp.exp(-jnp.abs(z)))


def _dot3(x, m01):
    x1 = x.astype(BF16)
    r1 = x - x1.astype(F32)
    x2 = r1.astype(BF16)
    x3 = (r1 - x2.astype(F32)).astype(BF16)
    n = x.shape[0]
    if n % 16:
        return (jnp.dot(x1, m01, preferred_element_type=F32) + jnp.dot(x2, m01, preferred_element_type=F32)
                + jnp.dot(x3, m01, preferred_element_type=F32))
    y = jnp.dot(jnp.concatenate([x1, x2, x3], axis=0), m01, preferred_element_type=F32)
    return y[:n] + y[n:2 * n] + y[2 * n:]


def _dot_nt(a, b):
    return lax.dot_general(a, b, NT, preferred_element_type=F32)


def _dot_tn(a, b):
    return lax.dot_general(a, b, TN, preferred_element_type=F32)


def _iota2(shape):
    return lax.broadcasted_iota(jnp.int32, shape, 0), lax.broadcasted_iota(jnp.int32, shape, 1)


HEADS_PER_STEP = 4
HEADS = range(HEADS_PER_STEP)


CHUNK_HEADS_PER_STEP = 4
CHUNK_HEADS = range(CHUNK_HEADS_PER_STEP)


def _head_spec(T, rows=None, width=HEAD_DIM, heads=HEADS_PER_STEP):
    return pl.BlockSpec((heads, T if rows is None else rows, width), lambda g: (g, 0, 0))


def _sb_weights(qb, kb, t0, s0, racc, m_gt, row, col):
    z = _dot_nt(qb, kb) * SCALE
    strict = (s0 + col) < (t0 + row)
    lb = _log_sigmoid(z)
    lf = jnp.where(strict, lb - z, 0.0)
    between = _dot3(lf, m_gt) + racc
    w = jnp.where(strict, jnp.exp(lb + between), 0.0)
    return strict, lb, lf, w


def _sb_fwd(q, k, v, name, carry=None):
    H, T, _ = q.shape
    nb = T // QB

    def body(q_ref, k_ref, v_ref, o_ref):
        row, col = _iota2((QB, QB))
        m_gt = (row > col).astype(BF16)

        def qblock(i, _):
            t0 = pl.multiple_of(i * QB, QB)
            qbs = [q_ref[h, pl.ds(t0, QB), :].astype(BF16) for h in HEADS]

            def kblock(jj, carry):
                s0 = pl.multiple_of((i - jj) * QB, QB)
                out = []
                for h in HEADS:
                    acc, racc = carry[h]
                    kb = k_ref[h, pl.ds(s0, QB), :].astype(BF16)
                    vb = v_ref[h, pl.ds(s0, QB), :].astype(BF16)
                    _, _, lf, w = _sb_weights(qbs[h], kb, t0, s0, racc, m_gt, row, col)
                    acc = acc + jnp.dot(w.astype(BF16), vb, preferred_element_type=F32)
                    out.append((acc, racc + jnp.sum(lf, axis=1, keepdims=True)))
                return tuple(out)

            res = lax.fori_loop(0, i + 1, kblock,
                                tuple((jnp.zeros((QB, HEAD_DIM), F32), jnp.zeros((QB, 1), F32)) for _ in HEADS))
            for h in HEADS:
                o_ref[h, pl.ds(t0, QB), :] = res[h][0].astype(BF16)
            return 0

        lax.fori_loop(0, nb, qblock, 0)

    spec = _head_spec(T)
    return _pcall_carrying(body, carry, name=name, grid=(H // HEADS_PER_STEP,), in_specs=[spec] * 3, out_specs=spec,
                           out_shape=jax.ShapeDtypeStruct((H, T, HEAD_DIM), BF16))(q, k, v)


def _sb_bwd(q, k, v, do, name, carry=None):
    H, T, _ = q.shape
    nb = T // QB

    def body(q_ref, k_ref, v_ref, do_ref, dq_ref, dk_out, dv_out, racc_ref, dk_ref, dv_ref):
        row, col = _iota2((QB, QB))
        m_gt = (row > col).astype(BF16)
        m_lt = (row < col).astype(BF16)
        dk_ref[...] = jnp.zeros_like(dk_ref)
        dv_ref[...] = jnp.zeros_like(dv_ref)

        def qblock(i, _):
            t0 = pl.multiple_of(i * QB, QB)
            qbs = [q_ref[h, pl.ds(t0, QB), :].astype(BF16) for h in HEADS]
            dobs = [do_ref[h, pl.ds(t0, QB), :].astype(BF16) for h in HEADS]

            def suffix(jj, raccs):
                j = i - jj
                s0 = pl.multiple_of(j * QB, QB)
                out = []
                for h in HEADS:
                    kb = k_ref[h, pl.ds(s0, QB), :].astype(BF16)
                    z = _dot_nt(qbs[h], kb) * SCALE
                    lf = jnp.where((s0 + col) < (t0 + row), _log_sigmoid(z) - z, 0.0)
                    racc_ref[h, j] = raccs[h]
                    out.append(raccs[h] + jnp.sum(lf, axis=1, keepdims=True))
                return tuple(out)

            lax.fori_loop(0, i + 1, suffix, tuple(jnp.zeros((QB, 1), F32) for _ in HEADS))

            def kblock(j, carry):
                s0 = pl.multiple_of(j * QB, QB)
                out = []
                for h in HEADS:
                    dq, cacc = carry[h]
                    kb = k_ref[h, pl.ds(s0, QB), :].astype(BF16)
                    vb = v_ref[h, pl.ds(s0, QB), :].astype(BF16)
                    strict, lb, _, w = _sb_weights(qbs[h], kb, t0, s0, racc_ref[h, j], m_gt, row, col)
                    e = _dot_nt(dobs[h], vb) * w
                    c_left = _dot3(e, m_lt) + cacc
                    sig = jnp.exp(lb)
                    dz = jnp.where(strict, e * (1.0 - sig) - c_left * sig, 0.0).astype(BF16)
                    dq = dq + jnp.dot(dz, kb, preferred_element_type=F32)
                    dk_ref[h, pl.ds(s0, QB), :] += _dot_tn(dz, qbs[h]) * SCALE
                    dv_ref[h, pl.ds(s0, QB), :] += _dot_tn(w.astype(BF16), dobs[h])
                    out.append((dq, cacc + jnp.sum(e, axis=1, keepdims=True)))
                return tuple(out)

            res = lax.fori_loop(0, i + 1, kblock,
                                tuple((jnp.zeros((QB, HEAD_DIM), F32), jnp.zeros((QB, 1), F32)) for _ in HEADS))
            for h in HEADS:
                dq_ref[h, pl.ds(t0, QB), :] = (res[h][0] * SCALE).astype(BF16)
            return 0

        lax.fori_loop(0, nb, qblock, 0)
        dk_out[...] = dk_ref[...].astype(BF16)
        dv_out[...] = dv_ref[...].astype(BF16)

    spec = _head_spec(T)
    acc = pltpu.VMEM((HEADS_PER_STEP, T, HEAD_DIM), F32)
    return _pcall_carrying(body, carry, name=name, grid=(H // HEADS_PER_STEP,), in_specs=[spec] * 4,
                           out_specs=[spec] * 3, out_shape=[jax.ShapeDtypeStruct((H, T, HEAD_DIM), BF16)] * 3,
                           scratch_shapes=[pltpu.VMEM((HEADS_PER_STEP, nb, QB, 1), F32), acc, acc])(q, k, v, do)


def _fox_logits(qb, kb, fq, fk, t0, s0, row, col):
    z = _dot_nt(qb, kb) * SCALE + fq - fk
    return jnp.where((s0 + col) <= (t0 + row), z, NEG)


def _fox_specs(T):
    return _head_spec(T, width=1), _head_spec(T, rows=T // QB, width=QB)


def _fox_fwd(q, k, v, fq, fk, name, carry=None):
    H, T, _ = q.shape
    nb = T // QB

    def body(q_ref, k_ref, v_ref, fq_ref, fk_ref, o_ref, lse_ref):
        row, col = _iota2((QB, QB))

        def qblock(i, _):
            t0 = pl.multiple_of(i * QB, QB)
            qbs = [q_ref[h, pl.ds(t0, QB), :].astype(BF16) for h in HEADS]
            fqs = [fq_ref[h, pl.ds(t0, QB), :] for h in HEADS]

            def kblock(j, carry):
                s0 = pl.multiple_of(j * QB, QB)
                out = []
                for h in HEADS:
                    acc, m, l = carry[h]
                    kb = k_ref[h, pl.ds(s0, QB), :].astype(BF16)
                    vb = v_ref[h, pl.ds(s0, QB), :].astype(BF16)
                    z = _fox_logits(qbs[h], kb, fqs[h], fk_ref[h, pl.ds(j, 1), :], t0, s0, row, col)
                    m_new = jnp.maximum(m, jnp.max(z, axis=1, keepdims=True))
                    a = jnp.exp(m - m_new)
                    p = jnp.exp(z - m_new)
                    acc = a * acc + jnp.dot(p.astype(BF16), vb, preferred_element_type=F32)
                    out.append((acc, m_new, a * l + jnp.sum(p, axis=1, keepdims=True)))
                return tuple(out)

            res = lax.fori_loop(0, i + 1, kblock,
                                tuple((jnp.zeros((QB, HEAD_DIM), F32), jnp.full((QB, 1), NEG, F32),
                                       jnp.zeros((QB, 1), F32)) for _ in HEADS))
            for h in HEADS:
                acc, m, l = res[h]
                o_ref[h, pl.ds(t0, QB), :] = (acc / l).astype(BF16)
                lse_ref[h, pl.ds(t0, QB), :] = m + jnp.log(l)
            return 0

        lax.fori_loop(0, nb, qblock, 0)

    spec = _head_spec(T)
    fq_spec, fk_spec = _fox_specs(T)
    return _pcall_carrying(
        body, carry, name=name, grid=(H // HEADS_PER_STEP,), in_specs=[spec] * 3 + [fq_spec, fk_spec],
        out_specs=[spec, fq_spec],
        out_shape=[jax.ShapeDtypeStruct((H, T, HEAD_DIM), BF16), jax.ShapeDtypeStruct((H, T, 1), F32)],
    )(q, k, v, fq, fk)


def _fox_bwd(q, k, v, fq, fk, lse, do, name, carry=None):
    H, T, _ = q.shape
    nb = T // QB

    def body(q_ref, k_ref, v_ref, fq_ref, fk_ref, lse_ref, do_ref, dq_ref, dk_out, dv_out, dfk_ref, dk_ref, dv_ref):
        row, col = _iota2((QB, QB))
        dk_ref[...] = jnp.zeros_like(dk_ref)
        dv_ref[...] = jnp.zeros_like(dv_ref)
        dfk_ref[...] = jnp.zeros_like(dfk_ref)

        def qblock(i, _):
            t0 = pl.multiple_of(i * QB, QB)
            qbs = [q_ref[h, pl.ds(t0, QB), :].astype(BF16) for h in HEADS]
            fqs = [fq_ref[h, pl.ds(t0, QB), :] for h in HEADS]
            lses = [lse_ref[h, pl.ds(t0, QB), :] for h in HEADS]
            dobs = [do_ref[h, pl.ds(t0, QB), :].astype(BF16) for h in HEADS]

            def probs(h, j):
                s0 = pl.multiple_of(j * QB, QB)
                kb = k_ref[h, pl.ds(s0, QB), :].astype(BF16)
                vb = v_ref[h, pl.ds(s0, QB), :].astype(BF16)
                z = _fox_logits(qbs[h], kb, fqs[h], fk_ref[h, pl.ds(j, 1), :], t0, s0, row, col)
                return s0, kb, jnp.exp(z - lses[h]), _dot_nt(dobs[h], vb)

            def row_dot(j, deltas):
                out = []
                for h in HEADS:
                    _, _, p, dp = probs(h, j)
                    out.append(deltas[h] + jnp.sum(p * dp, axis=1, keepdims=True))
                return tuple(out)

            deltas = lax.fori_loop(0, i + 1, row_dot, tuple(jnp.zeros((QB, 1), F32) for _ in HEADS))

            def kblock(j, dqs):
                out = []
                for h in HEADS:
                    s0, kb, p, dp = probs(h, j)
                    dz = p * (dp - deltas[h])
                    dzb = dz.astype(BF16)
                    dk_ref[h, pl.ds(s0, QB), :] += _dot_tn(dzb, qbs[h]) * SCALE
                    dv_ref[h, pl.ds(s0, QB), :] += _dot_tn(p.astype(BF16), dobs[h])
                    dfk_ref[h, pl.ds(j, 1), :] += -jnp.sum(dz, axis=0, keepdims=True)
                    out.append(dqs[h] + jnp.dot(dzb, kb, preferred_element_type=F32))
                return tuple(out)

            dqs = lax.fori_loop(0, i + 1, kblock, tuple(jnp.zeros((QB, HEAD_DIM), F32) for _ in HEADS))
            for h in HEADS:
                dq_ref[h, pl.ds(t0, QB), :] = (dqs[h] * SCALE).astype(BF16)
            return 0

        lax.fori_loop(0, nb, qblock, 0)
        dk_out[...] = dk_ref[...].astype(BF16)
        dv_out[...] = dv_ref[...].astype(BF16)

    spec = _head_spec(T)
    fq_spec, fk_spec = _fox_specs(T)
    acc = pltpu.VMEM((HEADS_PER_STEP, T, HEAD_DIM), F32)
    return _pcall_carrying(body, carry, name=name, grid=(H // HEADS_PER_STEP,),
                           in_specs=[spec] * 3 + [fq_spec, fk_spec, fq_spec, spec],
                           out_specs=[spec, spec, spec, fk_spec],
                           out_shape=[jax.ShapeDtypeStruct((H, T, HEAD_DIM), BF16)] * 3
                           + [jax.ShapeDtypeStruct((H, nb, QB), F32)],
                           scratch_shapes=[acc, acc])(q, k, v, fq, fk, lse, do)


def _forget_cumsum(flog, name):
    R, T = flog.shape
    nb = T // QB

    def body(x_ref, o_ref):
        row, col = _iota2((QB, QB))
        m_le = (row <= col).astype(BF16)
        carry = jnp.zeros((R, 1), F32)
        for b in range(nb):
            lf = _log_sigmoid(x_ref[:, b * QB:(b + 1) * QB])
            o_ref[:, b * QB:(b + 1) * QB] = _dot3(lf, m_le) + carry
            carry = carry + jnp.sum(lf, axis=1, keepdims=True)

    spec = pl.BlockSpec((R, T), lambda: (0, 0))
    return _pcall(body, name=name, in_specs=[spec], out_specs=spec,
                  out_shape=jax.ShapeDtypeStruct((R, T), F32))(flog)


def _forget_cumsum_bwd(flog, df, name):
    R, T = flog.shape
    nb = T // QB

    def body(x_ref, df_ref, o_ref):
        row, col = _iota2((QB, QB))
        m_ge = (row >= col).astype(BF16)
        carry = jnp.zeros((R, 1), F32)
        for b in reversed(range(nb)):
            dfb = df_ref[:, b * QB:(b + 1) * QB]
            dlog = _dot3(dfb, m_ge) + carry
            o_ref[:, b * QB:(b + 1) * QB] = dlog * _sigmoid(-x_ref[:, b * QB:(b + 1) * QB])
            carry = carry + jnp.sum(dfb, axis=1, keepdims=True)

    spec = pl.BlockSpec((R, T), lambda: (0, 0))
    return _pcall(body, name=name, in_specs=[spec, spec], out_specs=spec,
                  out_shape=jax.ShapeDtypeStruct((R, T), F32))(flog, df)


def _chunk_probs(qc, kb, bias, c, col):
    z = _dot_nt(qc, kb) * SCALE + bias
    z = jnp.where((c - (LEFT_CHUNKS + 1)) * CHUNK + col >= jnp.maximum(0, (c - LEFT_CHUNKS) * CHUNK), z, NEG)
    p = jnp.exp(z - jnp.max(z, axis=1, keepdims=True))
    return p, jnp.sum(p, axis=1, keepdims=True)


def _chunk_specs(T, n=CHUNK_HEADS_PER_STEP):
    return (_head_spec(T, heads=n), _head_spec(T, rows=T + BAND_PAD, heads=n),
            _head_spec(T, rows=CHUNK, width=BAND, heads=n))


def _chunk_fwd(q, kp, vp, bias, name, carry=None):
    H, T, _ = q.shape
    nc = T // CHUNK

    def body(q_ref, kp_ref, vp_ref, bias_ref, o_ref):
        _, col = _iota2((CHUNK, BAND))

        def chunk(c, _):
            t0 = pl.multiple_of(c * CHUNK, CHUNK)
            for h in HEADS:
                qc = q_ref[h, pl.ds(t0, CHUNK), :].astype(BF16)
                kb = kp_ref[h, pl.ds(t0, BAND), :].astype(BF16)
                vb = vp_ref[h, pl.ds(t0, BAND), :].astype(BF16)
                p, l = _chunk_probs(qc, kb, bias_ref[h], c, col)
                o = jnp.dot(p.astype(BF16), vb, preferred_element_type=F32) / l
                o_ref[h, pl.ds(t0, CHUNK), :] = o.astype(BF16)
            return 0

        lax.fori_loop(0, nc, chunk, 0)

    q_spec, kp_spec, b_spec = _chunk_specs(T, HEADS_PER_STEP)
    return _pcall_carrying(body, carry, name=name, grid=(H // HEADS_PER_STEP,),
                           in_specs=[q_spec, kp_spec, kp_spec, b_spec], out_specs=q_spec,
                           out_shape=jax.ShapeDtypeStruct((H, T, HEAD_DIM), BF16))(q, kp, vp, bias)


def _chunk_bwd(q, kp, vp, bias, do, name, carry=None):
    H, T, _ = q.shape
    nc = T // CHUNK

    def body(q_ref, kp_ref, vp_ref, bias_ref, do_ref, dq_ref, dk_out, dv_out, db_ref, dkp_ref, dvp_ref):
        _, col = _iota2((CHUNK, BAND))
        dkp_ref[...] = jnp.zeros_like(dkp_ref)
        dvp_ref[...] = jnp.zeros_like(dvp_ref)
        db_ref[...] = jnp.zeros_like(db_ref)

        def chunk(c, _):
            t0 = pl.multiple_of(c * CHUNK, CHUNK)
            for h in CHUNK_HEADS:
                qc = q_ref[h, pl.ds(t0, CHUNK), :].astype(BF16)
                kb = kp_ref[h, pl.ds(t0, BAND), :].astype(BF16)
                vb = vp_ref[h, pl.ds(t0, BAND), :].astype(BF16)
                dob = do_ref[h, pl.ds(t0, CHUNK), :].astype(BF16)
                p, l = _chunk_probs(qc, kb, bias_ref[h], c, col)
                p = p / l
                dp = _dot_nt(dob, vb)
                dz = p * (dp - jnp.sum(p * dp, axis=1, keepdims=True))
                dzb = dz.astype(BF16)
                dq = jnp.dot(dzb, kb, preferred_element_type=F32) * SCALE
                dq_ref[h, pl.ds(t0, CHUNK), :] = dq.astype(BF16)
                dkp_ref[h, pl.ds(t0, BAND), :] += _dot_tn(dzb, qc) * SCALE
                dvp_ref[h, pl.ds(t0, BAND), :] += _dot_tn(p.astype(BF16), dob)
                db_ref[h] += dz
            return 0

        lax.fori_loop(0, nc, chunk, 0)
        dk_out[...] = dkp_ref[:, BAND_PAD:, :].astype(BF16)
        dv_out[...] = dvp_ref[:, BAND_PAD:, :].astype(BF16)

    q_spec, kp_spec, b_spec = _chunk_specs(T)
    acc = pltpu.VMEM((CHUNK_HEADS_PER_STEP, T + BAND_PAD, HEAD_DIM), F32)
    return _pcall_carrying(body, carry, name=name, grid=(H // CHUNK_HEADS_PER_STEP,),
                           in_specs=[q_spec, kp_spec, kp_spec, b_spec, q_spec],
                           out_specs=[q_spec, q_spec, q_spec, b_spec],
                           out_shape=[jax.ShapeDtypeStruct((H, T, HEAD_DIM), BF16)] * 3
                           + [jax.ShapeDtypeStruct((H, CHUNK, BAND), F32)],
                           scratch_shapes=[acc, acc])(q, kp, vp, bias, do)


HBM_SPEC = pl.BlockSpec(memory_space=pltpu.HBM)


def _position():
    return lax.axis_index("x"), lax.axis_index("y"), lax.axis_index("c")


def _other_chips(x, y):
    chips = [(1 - x, y), (x, 1 - y), (1 - x, 1 - y)]
    return [(chip, 2 * chip[0] + chip[1]) for chip in chips]


def _remote_copy(src, dst, send_sems, recv_sems, k, to):
    return pltpu.make_async_remote_copy(src_ref=src, dst_ref=dst, send_sem=send_sems.at[k], recv_sem=recv_sems.at[k],
                                        device_id=to, device_id_type=MESH)


def _place_own(w, chip, name):
    _, r, c = w.shape
    tr = _pick(r, 256, 16)

    def body(chip_ref, w_ref, *out_refs):
        for l, o_ref in enumerate(out_refs):
            o_ref[...] = w_ref[l].astype(BF16)

    grid_spec = pltpu.PrefetchScalarGridSpec(
        num_scalar_prefetch=1, grid=(r // tr,), in_specs=[pl.BlockSpec((DEPTH, tr, c), lambda i, ch: (0, i, 0))],
        out_specs=[pl.BlockSpec((None, tr, c), lambda i, ch: (ch[0], i, 0))] * DEPTH)
    return _pcall(body, name=name, grid_spec=grid_spec,
                  out_shape=[jax.ShapeDtypeStruct((N_CHIPS, r, c), BF16)] * DEPTH)(chip, w)


def _gather_over_ici(srcs, bufs, new):
    x, y, c = _position()
    me = 2 * x + y
    return [(b.at[me, c], b.at[me, c], (*chip, c)) for b in bufs for chip, _ in _other_chips(x, y)]


def _gather_to_sibling(srcs, bufs, new):
    x, y, c = _position()
    return [(b.at[idx, c], b.at[idx, c], (x, y, 1 - c)) for b in bufs for _, idx in _other_chips(x, y)]


def _gather_layer(bufs, name):
    n = len(bufs)

    def body(*refs):
        dst = refs[n:2 * n]
        send_sems, recv_sems = refs[2 * n:]
        x, y, c = _position()
        me = 2 * x + y
        sibling = (x, y, 1 - c)
        others = _other_chips(x, y)
        first = [_remote_copy(dst[i].at[me, c], dst[i].at[me, c], send_sems, recv_sems, 3 * i + k, (*chip, c))
                 for i in range(n) for k, (chip, _) in enumerate(others)]
        for cp in first:
            cp.start()
        passed = []
        for i in range(n):
            for k, (_, idx) in enumerate(others):
                landed = dst[i].at[idx, c]
                _remote_copy(landed, landed, send_sems, recv_sems, 3 * i + k, sibling).wait_recv()
                passed.append(_remote_copy(landed, landed, send_sems, recv_sems, 3 * (n + i) + k, sibling))
                passed[-1].start()
        for i in range(n):
            for k, (_, idx) in enumerate(others):
                from_sibling = dst[i].at[idx, 1 - c]
                _remote_copy(from_sibling, from_sibling, send_sems, recv_sems, 3 * (n + i) + k, sibling).wait_recv()
        for cp in first + passed:
            cp.wait_send()

    return _pcall(
        body, name=name, in_specs=[HBM_SPEC] * n, out_specs=[HBM_SPEC] * n,
        out_shape=[jax.ShapeDtypeStruct(b.shape, b.dtype) for b in bufs],
        scratch_shapes=[pltpu.SemaphoreType.DMA((6 * n,)), pltpu.SemaphoreType.DMA((6 * n,))],
        input_output_aliases={i: i for i in range(n)},
    )(*bufs)


def _send_other_half(parts, name):
    n = len(parts)

    def body(*refs):
        src, got = refs[:n], refs[n:2 * n]
        send_sems, recv_sems = refs[2 * n:]
        x, y, c = _position()
        copies = [_remote_copy(src[i].at[1 - c], got[i], send_sems, recv_sems, i, (x, y, 1 - c)) for i in range(n)]
        for cp in copies:
            cp.start()
        for cp in copies:
            cp.wait()

    return _pcall(
        body, name=name, in_specs=[HBM_SPEC] * n, out_specs=[HBM_SPEC] * n,
        out_shape=[jax.ShapeDtypeStruct(p.shape[1:], p.dtype) for p in parts],
        scratch_shapes=[pltpu.SemaphoreType.DMA((n,)), pltpu.SemaphoreType.DMA((n,))],
    )(*parts)


def _scatter_chips(parts, name):
    n = len(parts)

    def body(*refs):
        src, dst = refs[:n], refs[n:2 * n]
        send_sems, recv_sems = refs[2 * n:]
        x, y, c = _position()
        others = _other_chips(x, y)
        sends = [_remote_copy(src[i].at[idx], dst[i].at[k], send_sems, recv_sems, 3 * i + k, (*chip, c))
                 for i in range(n) for k, (chip, idx) in enumerate(others)]
        for cp in sends:
            cp.start()
        for cp in sends:
            cp.wait()

    return _pcall(
        body, name=name, in_specs=[HBM_SPEC] * n, out_specs=[HBM_SPEC] * n,
        out_shape=[jax.ShapeDtypeStruct((3,) + p.shape[1:], p.dtype) for p in parts],
        scratch_shapes=[pltpu.SemaphoreType.DMA((3 * n,)), pltpu.SemaphoreType.DMA((3 * n,))],
    )(*parts)


def _swap_with_sibling(arrs, name):
    n = len(arrs)

    def body(*refs):
        src, dst = refs[:n], refs[n:2 * n]
        send_sems, recv_sems = refs[2 * n:]
        x, y, c = _position()
        copies = [_remote_copy(src[i], dst[i], send_sems, recv_sems, i, (x, y, 1 - c)) for i in range(n)]
        for cp in copies:
            cp.start()
        for cp in copies:
            cp.wait()

    return _pcall(
        body, name=name, in_specs=[HBM_SPEC] * n, out_specs=[HBM_SPEC] * n,
        out_shape=[jax.ShapeDtypeStruct(a.shape, a.dtype) for a in arrs],
        scratch_shapes=[pltpu.SemaphoreType.DMA((n,)), pltpu.SemaphoreType.DMA((n,))],
    )(*arrs)


def _allreduce_small(v, name):
    R, C = v.shape

    def body(v_ref, o_ref, slots, send_sems, recv_sems):
        x, y, c = _position()
        me = 4 * x + 2 * y + c
        slots[0] = v_ref[...]
        sends = []
        for r in range(1, 8):
            fx, fy, fc = (r >> 2) & 1, (r >> 1) & 1, r & 1
            to = (x ^ fx, y ^ fy, c ^ fc)
            cp = pltpu.make_async_remote_copy(src_ref=v_ref, dst_ref=slots.at[r], send_sem=send_sems.at[r - 1],
                                              recv_sem=recv_sems.at[r - 1], device_id=to, device_id_type=MESH)
            cp.start()
            sends.append(cp)
        for cp in sends:
            cp.wait()
        acc = slots[me]
        for d in range(1, 8):
            acc = acc + slots[d ^ me]
        o_ref[...] = acc

    vmem = pl.BlockSpec(memory_space=pltpu.VMEM)
    return _pcall(
        body, name=name, in_specs=[vmem], out_specs=vmem, out_shape=jax.ShapeDtypeStruct((R, C), F32),
        scratch_shapes=[pltpu.VMEM((8, R, C), F32), pltpu.SemaphoreType.DMA((7,)), pltpu.SemaphoreType.DMA((7,))],
    )(v)


def _add_pair(which, both, got, name):
    _, N, R, C = both.shape
    tr = _pick(R, 512, 16)

    def body(which_ref, a_ref, b_ref, o_ref):
        o_ref[...] = (a_ref[...].astype(F32) + b_ref[...].astype(F32)).astype(BF16)

    spec = pl.BlockSpec((None, tr, C), lambda n, i, w: (n, i, 0))
    grid_spec = pltpu.PrefetchScalarGridSpec(
        num_scalar_prefetch=1, grid=(N, R // tr),
        in_specs=[pl.BlockSpec((None, None, tr, C), lambda n, i, w: (w[0], n, i, 0)), spec], out_specs=spec)
    return _pcall(body, name=name, grid_spec=grid_spec,
                  out_shape=jax.ShapeDtypeStruct((N, R, C), BF16))(which, both, got)


def _sum_chips(which, own, others, name):
    N, R, C = others.shape
    tr = _pick(R, 512, 16)

    def body(which_ref, own_ref, p_ref, o_ref):
        acc = own_ref[...].astype(F32)
        for n in range(N):
            acc = acc + p_ref[n].astype(F32)
        o_ref[...] = acc

    grid_spec = pltpu.PrefetchScalarGridSpec(
        num_scalar_prefetch=1, grid=(R // tr,),
        in_specs=[pl.BlockSpec((None, tr, C), lambda i, w: (w[0], i, 0)), pl.BlockSpec((N, tr, C), lambda i, w: (0, i, 0))],
        out_specs=pl.BlockSpec((tr, C), lambda i, w: (i, 0)))
    return _pcall(body, name=name, grid_spec=grid_spec,
                  out_shape=jax.ShapeDtypeStruct((R, C), F32))(which, own, others)


BIG = ("w_ffn1_in", "w_ffn1_out", "w_in", "w_br_sb", "w_br_ch", "w_br_fox", "w_out", "w_ffn2_in", "w_ffn2_out")
ROW_SHARDED = ("w_ffn1_out", "w_out", "w_ffn2_out")
SMALL = ("g_ffn1", "g_mix", "b_in", "rel_bias", "g_ffn2", "g_final")


def _pair_sums(split, tag):
    core = lax.axis_index("c").astype(jnp.int32)[None]
    got = _send_other_half(split, f"grad_split_{tag}")
    return [_add_pair(core, a, b, f"grad_pair_sum_{tag}_{i}") for i, (a, b) in enumerate(zip(split, got))]


def _scatter_plan(srcs, bufs, new):
    x, y, c = _position()
    return [(s.at[idx], d.at[k], (*chip, c)) for s, d in zip(srcs, new) for k, (chip, idx) in enumerate(_other_chips(x, y))]


def _scatter_carry(pair):
    landing = [jax.ShapeDtypeStruct((3,) + p.shape[1:], p.dtype) for p in pair]
    return _Carry(pair, [], landing, 3 * len(pair), _scatter_plan)


def _finish_grads(pair, landed, tag):
    chip = (2 * lax.axis_index("x") + lax.axis_index("y")).astype(jnp.int32)[None]
    mine = [_sum_chips(chip, p, q, f"grad_chip_sum_{tag}_{i}") for i, (p, q) in enumerate(zip(pair, landed))]
    return mine, _swap_with_sibling(mine, f"grad_share_{tag}")


SMALL_SIZES = {"g_ffn1": DEPTH * D_MODEL, "g_mix": DEPTH * D_MODEL, "b_in": DEPTH * IN_WIDTH,
               "rel_bias": DEPTH * N_REL * H_CH, "g_ffn2": DEPTH * D_MODEL, "g_final": D_MODEL}
SMALL_TOTAL = sum(SMALL_SIZES.values()) + 1
SMALL_ROWS = -(-SMALL_TOTAL // (8 * 128)) * 8


def _pack_small(vals, extra):
    flat = [vals[n].reshape(-1) for n in SMALL] + [extra.reshape(-1)]
    flat.append(jnp.zeros((SMALL_ROWS * 128 - SMALL_TOTAL,), F32))
    return jnp.concatenate(flat).reshape(SMALL_ROWS, 128)


def _unpack_small(pack, shapes):
    flat, out, off = pack.reshape(-1), {}, 0
    for n in SMALL:
        out[n] = flat[off:off + SMALL_SIZES[n]].reshape(shapes[n])
        off += SMALL_SIZES[n]
    return out, flat[off]


def _to_heads(t, n_heads):
    return jnp.transpose(t.reshape(t.shape[0], n_heads, HEAD_DIM), (1, 0, 2)).astype(BF16)


def _from_heads(t):
    return jnp.transpose(t, (1, 0, 2)).reshape(t.shape[1], t.shape[0] * HEAD_DIM)


def _pad_keys(t):
    return jnp.pad(t, ((0, 0), (BAND_PAD, 0), (0, 0)))


DIAGONALS = CHUNK + BAND - 1


def _band_bias(table):
    n_far = (LEFT_CHUNKS + 1) * CHUNK + CHUNK - MAX_REL
    near = table[N_REL - 1 - (DIAGONALS - n_far):N_REL - 1][::-1]
    diag = jnp.concatenate([jnp.broadcast_to(table[N_REL - 1], (n_far, H_CH)), near], axis=0).T
    diag = jnp.pad(diag, ((0, 0), (0, 1)))
    skew = jnp.tile(diag, (1, CHUNK))[:, :CHUNK * DIAGONALS].reshape(H_CH, CHUNK, DIAGONALS)
    return skew[:, :, CHUNK - 1:]


def _pad_w_in(w):
    qkv, f, gates = w[:, :QKV_WIDTH], w[:, QKV_WIDTH:QKV_WIDTH + H_FOX], w[:, QKV_WIDTH + H_FOX:]
    return jnp.concatenate([qkv, gates, f, jnp.zeros((w.shape[0], F_PAD - H_FOX), w.dtype)], axis=1)


def _unpad_w_in(w):
    return jnp.concatenate([w[:, :QKV_WIDTH], w[:, QKV_WIDTH + GATE_WIDTH:QKV_WIDTH + GATE_WIDTH + H_FOX],
                            w[:, QKV_WIDTH:QKV_WIDTH + GATE_WIDTH]], axis=1)


QKV_SPLITS = np.cumsum([0, W_SB, W_SB, W_SB, W_CH, W_CH, W_CH, W_FOX, W_FOX, W_FOX])


def _by_chip_rows(g):
    return g.reshape(2, N_CHIPS, g.shape[1] // N_CHIPS, g.shape[2])


def _ffn_fwd(x, g, w_in, w_out, tag, carries):
    h, gate, up, a = _ffn_in_swiglu(x, g, w_in, f"{tag}_in", _carry_of(carries, "in"))
    y = _mm(a, w_out, mode="nn", name=f"{tag}_out", alpha=0.5, res=x, gathered="row",
            carry=_carry_of(carries, "out"))
    return y, (h, gate, up, a)


def _ffn_bwd(x, g, w_in, w_out, saved, dy, tag, carries):
    h, gate, up, a = saved
    da = _mm(dy, w_out, mode="nt", name=f"{tag}_da", alpha=0.5, gathered="row", out_dtype=BF16,
             carry=_carry_of(carries, "da"))
    dw_out = _mm(a, dy, mode="tn", name=f"{tag}_dwout", alpha=0.5, tm_cap=1408, out_dtype=BF16, out_split="row",
                 carry=_carry_of(carries, "dwout"))
    dgu = _swiglu_bwd(gate, up, da, f"{tag}_dact")
    dw_in = _mm(h, dgu, mode="tn", name=f"{tag}_dwin", tm_cap=512, out_dtype=BF16, out_split="col",
                carry=_carry_of(carries, "dwin"))
    dx, dg = _mm(dgu, w_in, mode="nt", name=f"{tag}_dh", gathered="col", tn_cap=1024, carry=_carry_of(carries, "dh"),
                 norm_bwd=(x, g, dy))
    return dx, dg, dw_in, _by_chip_rows(dw_out)


def _mixer_fwd(x, lw, tag, carries):
    T = x.shape[0]
    h = _rmsnorm_fwd(x, lw["g_mix"], f"{tag}_norm")
    p = _mm(h, lw["w_in"], mode="nn", name=f"{tag}_proj", bias=lw["b_in"], tn_cap=896, out_dtype=BF16,
            carry=_carry_of(carries, "proj"))
    parts = [p[:, QKV_SPLITS[i]:QKV_SPLITS[i + 1]] for i in range(9)]
    qa, ka, va = (_to_heads(t, H_SB) for t in parts[0:3])
    qb, kb, vb = (_to_heads(t, H_CH) for t in parts[3:6])
    qc, kc, vc = (_to_heads(t, H_FOX) for t in parts[6:9])
    flog = _mm(lw["w_forget_t"], h, mode="nt", name=f"{tag}_flog")[:8] + lw["b_forget"]
    fcum = _forget_cumsum(flog, f"{tag}_fcum")[:H_FOX]
    fq, fk = fcum.reshape(H_FOX, T, 1), fcum.reshape(H_FOX, T // QB, QB)
    kbp, vbp = _pad_keys(kb), _pad_keys(vb)
    oa = _sb_fwd(qa, ka, va, f"{tag}_sb", _carry_of(carries, "sb"))
    ob = _chunk_fwd(qb, kbp, vbp, lw["bias"], f"{tag}_ch", _carry_of(carries, "ch"))
    oc, lse = _fox_fwd(qc, kc, vc, fq, fk, f"{tag}_fox", _carry_of(carries, "fox"))
    oam, obm, ocm = _from_heads(oa), _from_heads(ob), _from_heads(oc)
    ya, yb, yc, merged = _branches_fwd(p, (oam, obm, ocm), (lw["w_br_sb"], lw["w_br_ch"], lw["w_br_fox"]),
                                       f"{tag}_branches")
    y = _mm(merged, lw["w_out"], mode="nn", name=f"{tag}_out", res=x, gathered="row")
    saved = (h, p, (qa, ka, va), (qb, kbp, vbp), (qc, kc, vc), flog, fq, fk, lse, (oam, obm, ocm),
             (ya, yb, yc), merged)
    return y, saved


def _mixer_bwd(x, lw, saved, dy, tag, carries):
    (h, p, (qa, ka, va), (qb, kbp, vbp), (qc, kc, vc), flog, fq, fk, lse, (oam, obm, ocm),
     (ya, yb, yc), merged) = saved
    T = x.shape[0]
    grads = {}
    dmerged = _mm(dy, lw["w_out"], mode="nt", name=f"{tag}_dmerged", gathered="row",
                  carry=_carry_of(carries, "dmerged"))
    grads["w_out"] = _by_chip_rows(_mm(merged, dy, mode="tn", name=f"{tag}_dwout", tm_cap=1024, out_dtype=BF16,
                                       out_split="row"))
    (dya, dyb, dyc), dgates, (doa, dob, doc) = _branches_bwd(
        p, (ya, yb, yc), dmerged, (lw["w_br_sb"], lw["w_br_ch"], lw["w_br_fox"]), f"{tag}_dbranches")
    grads["w_br_sb"], grads["w_br_ch"], grads["w_br_fox"] = _branches_dw(
        (oam, obm, ocm), (dya, dyb, dyc), f"{tag}_dwbranches")
    dqa, dka, dva = _sb_bwd(qa, ka, va, _to_heads(doa, H_SB), f"{tag}_dsb", _carry_of(carries, "dsb"))
    dqb, dkb, dvb, dbias = _chunk_bwd(qb, kbp, vbp, lw["bias"], _to_heads(dob, H_CH), f"{tag}_dch",
                                      _carry_of(carries, "dch"))
    dqc, dkc, dvc, dfk = _fox_bwd(qc, kc, vc, fq, fk, lse, _to_heads(doc, H_FOX), f"{tag}_dfox",
                                  _carry_of(carries, "dfox"))
    dflog = _forget_cumsum_bwd(flog, jnp.pad(dfk.reshape(H_FOX, T), ((0, 8 - H_FOX), (0, 0))), f"{tag}_dfcum")
    dqkv = [_from_heads(t) for t in (dqa, dka, dva, dqb, dkb, dvb, dqc, dkc, dvc)]
    dforget = jnp.pad(dflog[:H_FOX].T, ((0, 0), (0, F_PAD - H_FOX))).astype(BF16)
    dpb = jnp.concatenate(dqkv + list(dgates) + [dforget], axis=1)
    grads["b_in"] = _colsum(dpb, f"{tag}_dbin")
    dw_in =_unpad_w_in(_mm(h, dpb, mode="tn", name=f"{tag}_dwin", tm_cap=512, tn_cap=896, out_dtype=BF16))
    grads["w_in"] = jnp.transpose(dw_in.reshape(2, D_MODEL // 2, N_CHIPS, IN_WIDTH // N_CHIPS), (0, 2, 1, 3))
    dx, grads["g_mix"] = _mm(dpb, lw["w_in"], mode="nt", name=f"{tag}_dh", tk_cap=896, tn_cap=1024,
                             norm_bwd=(x, lw["g_mix"], dy))
    grads["rel_bias"] = lw["bias_vjp"](dbias)[0]
    return dx, grads


def kernel(x, g_ffn1, w_ffn1_in, w_ffn1_out, g_mix, w_in, b_in, rel_bias, w_br_sb, w_br_ch, w_br_fox, w_out, g_ffn2, w_ffn2_in, w_ffn2_out, g_final, loss_target, m_g_ffn1, m_w_ffn1_in, m_w_ffn1_out, m_g_mix, m_w_in, m_b_in, m_rel_bias, m_w_br_sb, m_w_br_ch, m_w_br_fox, m_w_out, m_g_ffn2, m_w_ffn2_in, m_w_ffn2_out, m_g_final, v_g_ffn1, v_w_ffn1_in, v_w_ffn1_out, v_g_mix, v_w_in, v_b_in, v_rel_bias, v_w_br_sb, v_w_br_ch, v_w_br_fox, v_w_out, v_g_ffn2, v_w_ffn2_in, v_w_ffn2_out, v_g_final):
    names = ("g_ffn1", "w_ffn1_in", "w_ffn1_out", "g_mix", "w_in", "b_in", "rel_bias", "w_br_sb", "w_br_ch",
             "w_br_fox", "w_out", "g_ffn2", "w_ffn2_in", "w_ffn2_out", "g_final")
    w = dict(zip(names, (g_ffn1, w_ffn1_in, w_ffn1_out, g_mix, w_in, b_in, rel_bias, w_br_sb, w_br_ch,
                         w_br_fox, w_out, g_ffn2, w_ffn2_in, w_ffn2_out, g_final)))
    m = dict(zip(names, (m_g_ffn1, m_w_ffn1_in, m_w_ffn1_out, m_g_mix, m_w_in, m_b_in, m_rel_bias, m_w_br_sb,
                         m_w_br_ch, m_w_br_fox, m_w_out, m_g_ffn2, m_w_ffn2_in, m_w_ffn2_out, m_g_final)))
    v = dict(zip(names, (v_g_ffn1, v_w_ffn1_in, v_w_ffn1_out, v_g_mix, v_w_in, v_b_in, v_rel_bias, v_w_br_sb,
                         v_w_br_ch, v_w_br_fox, v_w_out, v_g_ffn2, v_w_ffn2_in, v_w_ffn2_out, v_g_final)))
    xs = x[0]
    tgt = loss_target[0]

    chip = (2 * lax.axis_index("x") + lax.axis_index("y")).astype(jnp.int32)[None]
    placed = [_place_own(w[n], chip, f"place_{n}") for n in BIG]
    bufs = [[p[l].reshape(N_CHIPS, 2, p[l].shape[1] // 2, p[l].shape[2]) for p in placed] for l in range(DEPTH)]
    padded_w_in = {}

    def layer_weights(l):
        lw = {n: b.reshape((N_CHIPS,) + w[n].shape[1:]) for n, b in zip(BIG, bufs[l])}
        if l not in padded_w_in:
            whole = jnp.concatenate([lw["w_in"][j] for j in range(N_CHIPS)], axis=1)
            forget_t = jnp.pad(whole[:, QKV_WIDTH:QKV_WIDTH + H_FOX].T, ((0, F_PAD - H_FOX), (0, 0)))
            padded_w_in[l] = (_pad_w_in(whole), forget_t)
        lw["w_in"], lw["w_forget_t"] = padded_w_in[l]
        lw["b_forget"] = jnp.pad(w["b_in"][l][QKV_WIDTH:QKV_WIDTH + H_FOX], (0, 8 - H_FOX))[:, None]
        lw["b_in"] = _pad_w_in(w["b_in"][l][None, :])
        lw["bias"], lw["bias_vjp"] = jax.vjp(_band_bias, w["rel_bias"][l])
        for n in ("g_ffn1", "g_mix", "g_ffn2"):
            lw[n] = w[n][l][None, :]
        return lw

    def landed_in(l, idx):
        def done(carry):
            for i, b in zip(idx, carry.out[0]):
                bufs[l][i] = b
        return done

    def over_ici(l, idx):
        return lambda: _Carry([], [bufs[l][i] for i in idx], [], 3 * len(idx), _gather_over_ici, landed_in(l, idx))

    def to_sibling(l, idx):
        return lambda: _Carry([], [bufs[l][i] for i in idx], [], 3 * len(idx), _gather_to_sibling, landed_in(l, idx))

    def ffn_fwd(x_in, lw, which, tag, carries):
        return _ffn_fwd(x_in, lw[f"g_{which}"], lw[f"w_{which}_in"], lw[f"w_{which}_out"], tag, carries)

    def ffn_bwd(x_in, lw, which, saved_acts, dy, tag, carries):
        return _ffn_bwd(x_in, lw[f"g_{which}"], lw[f"w_{which}_in"], lw[f"w_{which}_out"], saved_acts, dy, tag,
                        carries)

    FFN1, W_IN, MIXER_REST, FFN2_IN, FFN2_OUT = (0, 1), (2,), (3, 4, 5, 6), (7,), (8,)
    first = FFN1 + W_IN
    for i, b in zip(first, _gather_layer([bufs[0][i] for i in first], "gather_l0")):
        bufs[0][i] = b
    fwd_carries = [
        {"ffn1": {"in": [over_ici(0, MIXER_REST)], "out": [to_sibling(0, MIXER_REST), over_ici(0, FFN2_OUT)]},
         "mix": {"proj": [to_sibling(0, FFN2_OUT), over_ici(1, MIXER_REST)],
                 "sb": [over_ici(0, FFN2_IN), over_ici(1, FFN2_OUT)],
                 "ch": [to_sibling(0, FFN2_IN), over_ici(1, FFN1)],
                 "fox": [over_ici(1, W_IN)]},
         "ffn2": {"in": [to_sibling(1, MIXER_REST + FFN2_OUT + FFN1 + W_IN)]}},
        {"ffn1": {}, "mix": {"sb": [over_ici(1, FFN2_IN)], "ch": [to_sibling(1, FFN2_IN)]}, "ffn2": {}},
    ]

    saved = []
    act = xs
    for l in range(DEPTH):
        x0 = act
        x1, s1 = ffn_fwd(x0, layer_weights(l), "ffn1", f"l{l}_ffn1", fwd_carries[l]["ffn1"])
        x2, s2 = _mixer_fwd(x1, layer_weights(l), f"l{l}_mix", fwd_carries[l]["mix"])
        act, s3 = ffn_fwd(x2, layer_weights(l), "ffn2", f"l{l}_ffn2", fwd_carries[l]["ffn2"])
        saved.append((x0, s1, x1, s2, x2, s3))
    layers = [layer_weights(l) for l in range(DEPTH)]

    dact, dg_final, loss_row = _final_loss(act, w["g_final"][None, :], tgt, "final_loss")

    big_grads = {n: [None] * DEPTH for n in BIG}
    small_grads = {n: [None] * DEPTH for n in ("g_ffn1", "g_mix", "b_in", "rel_bias", "g_ffn2")}
    pair = [[None] * len(BIG) for _ in range(DEPTH)]
    landed = [[None] * len(BIG) for _ in range(DEPTH)]

    def pair_up(l, idx, tag):
        for i, p in zip(idx, _pair_sums([big_grads[BIG[i]][l] for i in idx], tag)):
            pair[l][i] = p

    core = lax.axis_index("c").astype(jnp.int32)[None]

    def to_sibling_half(l, idx, tag):
        def plan(srcs, bufs, new):
            x, y, c = _position()
            return [(s.at[1 - c], d, (x, y, 1 - c)) for s, d in zip(srcs, new)]
        def done(carry):
            for j, (i, got) in enumerate(zip(idx, carry.out[1])):
                pair[l][i] = _add_pair(core, big_grads[BIG[i]][l], got, f"grad_pair_sum_{tag}_{j}")
        def make():
            split = [big_grads[BIG[i]][l] for i in idx]
            return _Carry(split, [], [jax.ShapeDtypeStruct(s.shape[1:], s.dtype) for s in split], len(idx), plan, done)
        return make

    def exchange(l, idx):
        def done(carry):
            for i, a in zip(idx, carry.out[1]):
                landed[l][i] = a
        def make():
            carry = _scatter_carry([pair[l][i] for i in idx])
            carry.done = done
            return carry
        return make

    def exchange_one_way(l, i, k):
        def plan(srcs, bufs, new):
            x, y, c = _position()
            chip_k, idx_k = _other_chips(x, y)[k]
            return [(srcs[0].at[idx_k], bufs[0].at[k], (*chip_k, c))]
        def done(carry):
            landed[l][i] = carry.out[0][0]
        def make():
            if landed[l][i] is None:
                landed[l][i] = lax.empty((3,) + pair[l][i].shape[1:], BF16)
            return _Carry([pair[l][i]], [landed[l][i]], [], 1, plan, done)
        return make

    bwd_carries = [
        {"ffn2": {},
         "mix": {"dmerged": [to_sibling_half(0, FFN2_IN + FFN2_OUT, "l0_ffn2")],
                 "dsb": [exchange(1, FFN1 + W_IN)], "dch": [exchange(1, MIXER_REST + FFN2_IN + FFN2_OUT)],
                 "dfox": [exchange(0, FFN2_IN + FFN2_OUT)]},
         "ffn1": {"da": [exchange(0, MIXER_REST)], "dwout": [exchange_one_way(0, 2, 0)],
                  "dwin": [exchange_one_way(0, 2, 1)], "dh": [exchange_one_way(0, 2, 2)]}},
        {"ffn2": {}, "mix": {},
         "ffn1": {"da": [to_sibling_half(1, W_IN + MIXER_REST + FFN2_IN + FFN2_OUT, "l1_rest")]}},
    ]
    for l in reversed(range(DEPTH)):
        lw = layers[l]
        x0, s1, x1, s2, x2, s3 = saved[l]
        dx2, small_grads["g_ffn2"][l], big_grads["w_ffn2_in"][l], big_grads["w_ffn2_out"][l] = ffn_bwd(
            x2, lw, "ffn2", s3, dact, f"l{l}_ffn2", bwd_carries[l]["ffn2"])
        dx1, mg = _mixer_bwd(x1, lw, s2, dx2, f"l{l}_mix", bwd_carries[l]["mix"])
        for n in ("w_in", "w_br_sb", "w_br_ch", "w_br_fox", "w_out"):
            big_grads[n][l] = mg[n]
        small_grads["b_in"][l] = _unpad_w_in(mg["b_in"])[0]
        small_grads["g_mix"][l] = mg["g_mix"][0]
        small_grads["rel_bias"][l] = mg["rel_bias"]
        if l == 0:
            pair_up(0, W_IN + MIXER_REST, "l0_mix")
        dact, dg1, big_grads["w_ffn1_in"][l], big_grads["w_ffn1_out"][l] = ffn_bwd(
            x0, lw, "ffn1", s1, dx1, f"l{l}_ffn1", bwd_carries[l]["ffn1"])
        small_grads["g_ffn1"][l] = dg1[0]
        small_grads["g_ffn2"][l] = small_grads["g_ffn2"][l][0]
        if l == 1:
            pair_up(1, FFN1, "l1_ffn1")
    grad_x = dact[None]
    pair_up(0, FFN1, "l0_ffn1")
    for i, a in zip(FFN1, _scatter_chips([pair[0][i] for i in FFN1], "grad_scatter_l0")):
        landed[0][i] = a

    small = {n: jnp.stack(small_grads[n]) for n in small_grads}
    small["g_final"] = dg_final[0]
    small_sum, loss = _unpack_small(_allreduce_small(_pack_small(small, loss_row[0, :1]), "allreduce_small"),
                                    {n: w[n].shape for n in SMALL})

    mine, theirs = _finish_grads(pair[0] + pair[1], landed[0] + landed[1], "all")

    grads, delta, new_m, new_v = dict(small_sum), {}, {}, {}
    for i, n in enumerate(BIG):
        halves = [(mine[l * len(BIG) + i], theirs[l * len(BIG) + i]) for l in range(DEPTH)]
        grads[n], delta[n], new_m[n], new_v[n] = _adamw_from_halves(
            w[n], m[n], v[n], halves, 1 if n in ROW_SHARDED else 0, core, f"adamw_{n}")
    zero = jnp.zeros((1,), F32)
    sd, sm, sv = _adamw(_pack_small(w, zero), _pack_small(grads, zero), _pack_small(m, zero), _pack_small(v, zero),
                        "adamw_small")
    shapes = {n: w[n].shape for n in SMALL}
    for dst, src in ((delta, sd), (new_m, sm), (new_v, sv)):
        dst.update(_unpack_small(src, shapes)[0])

    return (loss, grad_x, *[grads[n] for n in names], *[delta[n] for n in names],
            *[new_m[n] for n in names], *[new_v[n] for n in names])
```

```python
import functools

import numpy as np
import jax
import jax.numpy as jnp
from jax import lax
from jax.experimental import pallas as pl
from jax.experimental.pallas import tpu as pltpu

F32 = jnp.float32
BF16 = jnp.bfloat16

D_MODEL = 1024
DEPTH = 2
CHUNK = 64
HEAD_DIM = 64
H_SB, H_CH, H_FOX = 4, 8, 4
W_SB, W_CH, W_FOX = H_SB * HEAD_DIM, H_CH * HEAD_DIM, H_FOX * HEAD_DIM
LEFT_CHUNKS = 8
MAX_REL = 128
N_REL = 2 * MAX_REL + 1
D_FF = 2816
QKV_WIDTH = 3 * (W_SB + W_CH + W_FOX)
GATE_WIDTH = 3 * D_MODEL
IN_WIDTH = QKV_WIDTH + H_FOX + GATE_WIDTH
F_PAD = 128
P_WIDTH = QKV_WIDTH + GATE_WIDTH + F_PAD
RMS_EPS = 1e-6
NEG = -1e30
SCALE = HEAD_DIM ** -0.5
QB = 256
BAND = (LEFT_CHUNKS + 2) * CHUNK
BAND_PAD = BAND - CHUNK

ADAM_LR, ADAM_B1, ADAM_B2, ADAM_EPS, ADAM_WD, ADAM_STEP = 0.001, 0.9, 0.999, 1e-08, 0.01, 10

N_CHIPS = 4
VMEM_BUDGET = 52 * 1024 * 1024

NT = (((1,), (1,)), ((), ()))
TN = (((0,), (0,)), ((), ()))
NN = (((1,), (0,)), ((), ()))
MESH = pl.DeviceIdType.MESH


def _pcall(body, **kw):
    return pl.pallas_call(body, **kw)


class _Carry:
    def __init__(self, srcs, bufs, new, count, plan, done=None):
        self.srcs, self.bufs, self.new, self.count, self.plan = list(srcs), list(bufs), list(new), count, plan
        self.out, self.done = None, done

    @staticmethod
    def join(parts):
        parts = [p for p in parts if p is not None]
        if not parts:
            return None

        def plan(srcs, bufs, new):
            copies, s, b, n = [], 0, 0, 0
            for p in parts:
                copies += p.plan(srcs[s:s + len(p.srcs)], bufs[b:b + len(p.bufs)], new[n:n + len(p.new)])
                s, b, n = s + len(p.srcs), b + len(p.bufs), n + len(p.new)
            return copies

        whole = _Carry(sum((p.srcs for p in parts), []), sum((p.bufs for p in parts), []),
                       sum((p.new for p in parts), []), sum(p.count for p in parts), plan)
        whole.parts = parts
        return whole

    def share_out(self):
        b, n = 0, 0
        for p in getattr(self, "parts", []):
            p.out = (self.out[0][b:b + len(p.bufs)], self.out[1][n:n + len(p.new)])
            b, n = b + len(p.bufs), n + len(p.new)
            p.share_out()
        if self.done is not None:
            self.done(self)


def _carry_of(carries, key):
    return _Carry.join([make() for make in carries.get(key, [])])


def _pcall_carrying(body, carry, **kw):
    if carry is None:
        return _pcall(body, **kw)
    in_specs = list(kw.pop("in_specs"))
    out_specs, out_shape = kw.pop("out_specs"), kw.pop("out_shape")
    single = not isinstance(out_shape, (list, tuple))
    out_specs = [out_specs] if single else list(out_specs)
    out_shape = [out_shape] if single else list(out_shape)
    scratch = list(kw.pop("scratch_shapes", []))
    grid = tuple(kw.get("grid", ()))
    n_in, n_out, n_scr = len(in_specs), len(out_specs), len(scratch)
    ns, nb, nn = len(carry.srcs), len(carry.bufs), len(carry.new)

    def body2(*refs):
        ins, srcs = refs[:n_in], refs[n_in:n_in + ns]
        at = n_in + ns + nb
        outs, bufs, new = refs[at:at + n_out], refs[at + n_out:at + n_out + nb], refs[at + n_out + nb:at + n_out + nb + nn]
        at += n_out + nb + nn
        scr, (send_sems, recv_sems) = refs[at:at + n_scr], refs[at + n_scr:]

        def copies():
            return [_remote_copy(s, d, send_sems, recv_sems, k, to)
                    for k, (s, d, to) in enumerate(carry.plan(srcs, bufs, new))]

        def start():
            for cp in copies():
                cp.start()

        def wait():
            for cp in copies():
                cp.wait()

        if grid:
            ids = [pl.program_id(a) for a in range(len(grid))]
            pl.when(functools.reduce(jnp.logical_and, [i == 0 for i in ids]))(start)
        else:
            start()
        body(*ins, *outs, *scr)
        if grid:
            pl.when(functools.reduce(jnp.logical_and, [i == g - 1 for i, g in zip(ids, grid)]))(wait)
        else:
            wait()

    call = _pcall(
        body2, in_specs=in_specs + [HBM_SPEC] * (ns + nb), out_specs=out_specs + [HBM_SPEC] * (nb + nn),
        out_shape=out_shape + [jax.ShapeDtypeStruct(b.shape, b.dtype) for b in carry.bufs] + carry.new,
        scratch_shapes=scratch + [pltpu.SemaphoreType.DMA((carry.count,)), pltpu.SemaphoreType.DMA((carry.count,))],
        input_output_aliases={n_in + ns + j: n_out + j for j in range(nb)}, **kw)

    def apply(*args):
        res = call(*args, *carry.srcs, *carry.bufs)
        carry.out = (list(res[n_out:n_out + nb]), list(res[n_out + nb:]))
        carry.share_out()
        return res[0] if single else list(res[:n_out])

    return apply


def _pick(n, cap, mult=128):
    best = None
    d = mult
    while d <= min(n, cap):
        if n % d == 0:
            best = d
        d += mult
    return n if best is None else best


def _nbytes(shape, dtype):
    return int(np.prod(shape)) * jnp.dtype(dtype).itemsize


def _mm(a, b, *, mode, name, out_dtype=F32, alpha=1.0, bias=None, res=None, tm_cap=1024, tn_cap=512,
        tk_cap=2816, gathered=None, out_split=None, carry=None, norm_bwd=None):
    if mode == "tn":
        K, M = a.shape
    else:
        M, K = a.shape
    dn = {"nn": NN, "nt": NT, "tn": TN}[mode]
    b_rows = None
    if gathered is None:
        N = b.shape[0] if mode == "nt" else b.shape[1]
        tn = {None: _pick(N, tn_cap), "col": N // N_CHIPS, "row": N // 2}[out_split]
        if out_split == "col":
            tm_cap = min(tm_cap, M // 2)
        tk = K if K <= tk_cap else _pick(K, tk_cap)
        b_spec = (pl.BlockSpec((tn, tk), lambda i, j, k: (j, k)) if mode == "nt"
                  else pl.BlockSpec((tk, tn), lambda i, j, k: (k, j)))
    else:
        r, c = b.shape[1:]
        rows, cols = (r, N_CHIPS * c) if gathered == "col" else (N_CHIPS * r, c)
        N = rows if mode == "nt" else cols
        assert K == (cols if mode == "nt" else rows), (name, K, rows, cols)
        if gathered == "col" and mode == "nn":
            tn, tk = c, (r if r <= tk_cap else _pick(r, tk_cap))
            b_spec = pl.BlockSpec((None, tk, c), lambda i, j, k: (j, k, 0))
        elif gathered == "col":
            tn, tk = _pick(r, tn_cap), c
            b_spec = pl.BlockSpec((None, tn, c), lambda i, j, k: (k, j, 0))
        elif mode == "nn":
            tn, tk, b_rows = _pick(c, tn_cap), K, N_CHIPS
            b_spec = pl.BlockSpec((N_CHIPS, r, tn), lambda i, j, k: (0, 0, j))
        else:
            tn, tk, b_rows = 2 * r, K, 2
            b_spec = pl.BlockSpec((2, r, c), lambda i, j, k: (j, 0, 0))
    tm = _pick(M, tm_cap)
    nk = K // tk

    a_spec = (pl.BlockSpec((tk, tm), lambda i, j, k: (k, i)) if mode == "tn"
              else pl.BlockSpec((tm, tk), lambda i, j, k: (i, k)))
    in_specs = [a_spec, b_spec]
    args = [a, b]
    if bias is not None:
        in_specs.append(pl.BlockSpec((1, tn), lambda i, j, k: (0, j)))
        args.append(bias)
    if res is not None:
        in_specs.append(pl.BlockSpec((tm, tn), lambda i, j, k: (i, j)))
        args.append(res)
    if norm_bwd is not None:
        assert tn == N and alpha == 1.0 and out_split is None and bias is None and res is None, name
        x_in, g_in, dres_in = norm_bwd
        in_specs += [pl.BlockSpec((tm, tn), lambda i, j, k: (i, 0)), pl.BlockSpec((1, tn), lambda i, j, k: (0, 0)),
                     pl.BlockSpec((tm, tn), lambda i, j, k: (i, 0))]
        args += [x_in, g_in, dres_in]

    est = 2 * (_nbytes((tm, tk), a.dtype) + _nbytes((tk, tn), b.dtype) + _nbytes((tm, tn), out_dtype))
    est += _nbytes((tm, tn), F32) * (3 if res is not None else 1)
    est += _nbytes((tm, tn), F32) * (4 if norm_bwd is not None else 0)
    assert est < VMEM_BUDGET, (name, est)

    def body(*refs):
        a_ref, b_ref = refs[0], refs[1]
        pos = 2
        bias_ref = res_ref = None
        if bias is not None:
            bias_ref = refs[pos]
            pos += 1
        if res is not None:
            res_ref = refs[pos]
            pos += 1
        if norm_bwd is not None:
            x_ref, g_ref, dres_ref = refs[pos:pos + 3]
            pos += 3
        o_ref = refs[pos]
        if norm_bwd is not None:
            dg_ref = refs[pos + 1]
            pos += 1
        acc_ref = refs[pos + 1] if nk > 1 else None

        bv = b_ref[...]
        if b_rows is not None:
            bv = bv.reshape(b_rows * bv.shape[1], bv.shape[2])
        part = lax.dot_general(a_ref[...].astype(BF16), bv.astype(BF16), dn, preferred_element_type=F32)

        def finish(acc):
            if alpha != 1.0:
                acc = acc * alpha
            if bias_ref is not None:
                acc = acc + bias_ref[...]
            if res_ref is not None:
                acc = acc + res_ref[...]
            if norm_bwd is not None:
                xv = x_ref[...]
                rstd = lax.rsqrt(jnp.mean(xv * xv, axis=-1, keepdims=True) + RMS_EPS)
                xhat = xv * rstd
                dyg = acc * g_ref[...]
                part_g = jnp.sum(acc * xhat, axis=0, keepdims=True)
                acc = dres_ref[...] + rstd * (dyg - xhat * jnp.mean(dyg * xhat, axis=-1, keepdims=True))
                first = pl.program_id(0) == 0

                @pl.when(first)
                def _():
                    dg_ref[...] = part_g

                @pl.when(jnp.logical_not(first))
                def _():
                    dg_ref[...] += part_g
            o_ref[...] = acc.astype(out_dtype)

        if nk == 1:
            finish(part)
        else:
            k = pl.program_id(2)

            @pl.when(k == 0)
            def _():
                acc_ref[...] = part

            @pl.when(k > 0)
            def _():
                acc_ref[...] += part

            @pl.when(k == nk - 1)
            def _():
                finish(acc_ref[...])

    if out_split == "col":
        per_half = M // 2 // tm
        out_spec = pl.BlockSpec((None, None, tm, tn), lambda i, j, k: (i // per_half, j, i % per_half, 0))
        out_shape = jax.ShapeDtypeStruct((2, N_CHIPS, M // 2, tn), out_dtype)
    elif out_split == "row":
        out_spec = pl.BlockSpec((None, tm, tn), lambda i, j, k: (j, i, 0))
        out_shape = jax.ShapeDtypeStruct((2, M, tn), out_dtype)
    else:
        out_spec = pl.BlockSpec((tm, tn), lambda i, j, k: (i, j))
        out_shape = jax.ShapeDtypeStruct((M, N), out_dtype)
    semantics = ("parallel", "parallel", "arbitrary")
    if norm_bwd is not None:
        out_spec = [out_spec, pl.BlockSpec((1, tn), lambda i, j, k: (0, 0))]
        out_shape = [out_shape, jax.ShapeDtypeStruct((1, N), F32)]
        semantics = ("arbitrary", "arbitrary", "arbitrary")
    return _pcall_carrying(
        body, carry, name=name, grid=(M // tm, N // tn, nk), in_specs=in_specs, out_specs=out_spec,
        out_shape=out_shape,
        scratch_shapes=[pltpu.VMEM((tm, tn), F32)] if nk > 1 else [],
        compiler_params=pltpu.CompilerParams(dimension_semantics=semantics),
    )(*args)


def _rmsnorm_fwd(x, g, name):
    T, Dm = x.shape
    tm = _pick(T, 256, 8)

    def body(x_ref, g_ref, h_ref):
        xv = x_ref[...]
        rstd = lax.rsqrt(jnp.mean(xv * xv, axis=-1, keepdims=True) + RMS_EPS)
        h_ref[...] = (xv * rstd * g_ref[...]).astype(BF16)

    return _pcall(
        body, name=name, grid=(T // tm,),
        in_specs=[pl.BlockSpec((tm, Dm), lambda i: (i, 0)), pl.BlockSpec((1, Dm), lambda i: (0, 0))],
        out_specs=pl.BlockSpec((tm, Dm), lambda i: (i, 0)),
        out_shape=jax.ShapeDtypeStruct((T, Dm), BF16),
    )(x, g)


def _final_loss(x, g, tgt, name):
    T, Dm = x.shape
    tm = _pick(T, 256, 8)

    def body(x_ref, g_ref, t_ref, dx_ref, dg_ref, loss_ref):
        xv = x_ref[...]
        gv = g_ref[...]
        rstd = lax.rsqrt(jnp.mean(xv * xv, axis=-1, keepdims=True) + RMS_EPS)
        xhat = xv * rstd
        err = xhat * gv - t_ref[...]
        part_loss = 0.5 * jnp.sum(jnp.mean(err * err, axis=-1, keepdims=True), axis=0, keepdims=True)
        dy = err * (1.0 / Dm)
        dyg = dy * gv
        dx_ref[...] = rstd * (dyg - xhat * jnp.mean(dyg * xhat, axis=-1, keepdims=True))
        part_g = jnp.sum(dy * xhat, axis=0, keepdims=True)
        part_l = jnp.broadcast_to(part_loss, (1, 128))

        @pl.when(pl.program_id(0) == 0)
        def _():
            dg_ref[...] = part_g
            loss_ref[...] = part_l

        @pl.when(pl.program_id(0) > 0)
        def _():
            dg_ref[...] += part_g
            loss_ref[...] += part_l

    row = pl.BlockSpec((tm, Dm), lambda i: (i, 0))
    vec = pl.BlockSpec((1, Dm), lambda i: (0, 0))
    return _pcall(
        body, name=name, grid=(T // tm,), in_specs=[row, vec, row],
        out_specs=[row, vec, pl.BlockSpec((1, 128), lambda i: (0, 0))],
        out_shape=[jax.ShapeDtypeStruct((T, Dm), F32), jax.ShapeDtypeStruct((1, Dm), F32),
                   jax.ShapeDtypeStruct((1, 128), F32)],
        compiler_params=pltpu.CompilerParams(dimension_semantics=("arbitrary",)),
    )(x, g, tgt)


def _sigmoid(z):
    return 1.0 / (1.0 + jnp.exp(-z))


def _ffn_in_swiglu(x, g, w_in, name, carry=None):
    T, Dm = x.shape
    cw = w_in.shape[2]
    tm = _pick(T, 512, 16)

    def body(x_ref, gain_ref, wg_ref, wu_ref, h_ref, g_ref, u_ref, a_ref):
        @pl.when(pl.program_id(1) == 0)
        def _():
            xv = x_ref[...]
            rstd = lax.rsqrt(jnp.mean(xv * xv, axis=-1, keepdims=True) + RMS_EPS)
            h_ref[...] = (xv * rstd * gain_ref[...]).astype(BF16)

        hv = h_ref[...]
        gv = jnp.dot(hv, wg_ref[...], preferred_element_type=F32)
        uv = jnp.dot(hv, wu_ref[...], preferred_element_type=F32)
        g_ref[...] = gv.astype(BF16)
        u_ref[...] = uv.astype(BF16)
        a_ref[...] = (gv * _sigmoid(gv) * uv).astype(BF16)

    row = pl.BlockSpec((tm, Dm), lambda i, j: (i, 0))
    out = pl.BlockSpec((tm, cw), lambda i, j: (i, j))
    return _pcall_carrying(
        body, carry, name=name, grid=(T // tm, 2),
        in_specs=[row, pl.BlockSpec((1, Dm), lambda i, j: (0, 0)), pl.BlockSpec((None, Dm, cw), lambda i, j: (j, 0, 0)),
                  pl.BlockSpec((None, Dm, cw), lambda i, j: (j + 2, 0, 0))],
        out_specs=[row, out, out, out],
        out_shape=[jax.ShapeDtypeStruct((T, Dm), BF16)] + [jax.ShapeDtypeStruct((T, D_FF), BF16)] * 3,
        compiler_params=pltpu.CompilerParams(dimension_semantics=("parallel", "arbitrary")),
    )(x, g, w_in, w_in)


def _swiglu_bwd(g, u, da, name):
    T = g.shape[0]
    tm = _pick(T, 256, 16)

    def body(g_ref, u_ref, da_ref, d_ref):
        gv = g_ref[...].astype(F32)
        uv = u_ref[...].astype(F32)
        dav = da_ref[...].astype(F32)
        sg = _sigmoid(gv)
        d_ref[:, :D_FF] = (dav * uv * (sg + gv * sg * (1.0 - sg))).astype(BF16)
        d_ref[:, D_FF:] = (dav * gv * sg).astype(BF16)

    row = pl.BlockSpec((tm, D_FF), lambda i: (i, 0))
    return _pcall(
        body, name=name, grid=(T // tm,), in_specs=[row, row, row],
        out_specs=pl.BlockSpec((tm, 2 * D_FF), lambda i: (i, 0)),
        out_shape=jax.ShapeDtypeStruct((T, 2 * D_FF), BF16),
    )(g, u, da)


def _branches_fwd(p, outs, weights, name):
    T = p.shape[0]
    cw = weights[0].shape[2]
    tm = _pick(T, 1024, 16)
    first_gate, per_branch = QKV_WIDTH // cw, D_MODEL // cw

    def body(*refs):
        gates, o_refs, w_refs, y_refs, m_ref = refs[0:3], refs[3:6], refs[6:9], refs[9:12], refs[12]
        merged = None
        for g_ref, o_ref, w_ref, y_ref in zip(gates, o_refs, w_refs, y_refs):
            y = jnp.dot(o_ref[...], w_ref[...], preferred_element_type=F32)
            y_ref[...] = y
            term = _sigmoid(g_ref[...].astype(F32)) * y
            merged = term if merged is None else merged + term
        m_ref[...] = merged.astype(BF16)

    gate_specs = [pl.BlockSpec((tm, cw), functools.partial(
        lambda i, j, kk: (i, first_gate + per_branch * kk + j), kk=kk)) for kk in range(3)]
    o_specs = [pl.BlockSpec((tm, o.shape[1]), lambda i, j: (i, 0)) for o in outs]
    w_specs = [pl.BlockSpec((None, wk.shape[1], cw), lambda i, j: (j, 0, 0)) for wk in weights]
    tile = pl.BlockSpec((tm, cw), lambda i, j: (i, j))
    return _pcall(
        body, name=name, grid=(T // tm, N_CHIPS), in_specs=gate_specs + o_specs + w_specs, out_specs=[tile] * 4,
        out_shape=[jax.ShapeDtypeStruct((T, D_MODEL), F32)] * 3 + [jax.ShapeDtypeStruct((T, D_MODEL), BF16)],
    )(p, p, p, *outs, *weights)


def _branches_bwd(p, ys, dm, weights, name):
    T = p.shape[0]
    cw = weights[0].shape[2]
    tm = _pick(T, 1024, 16)
    first_gate, per_branch = QKV_WIDTH // cw, D_MODEL // cw
    widths = [wk.shape[1] for wk in weights]

    def body(*refs):
        gates, y_refs, dm_ref, w_refs = refs[0:3], refs[3:6], refs[6], refs[7:10]
        dy_refs, dg_refs, do_refs, acc_refs = refs[10:13], refs[13:16], refs[16:19], refs[19:22]
        j = pl.program_id(1)
        dmv = dm_ref[...]
        for g_ref, y_ref, w_ref, dy_ref, dg_ref, do_ref, acc_ref in zip(gates, y_refs, w_refs, dy_refs, dg_refs,
                                                                       do_refs, acc_refs):
            s = _sigmoid(g_ref[...].astype(F32))
            dy = (s * dmv).astype(BF16)
            dy_ref[...] = dy
            dg_ref[...] = (dmv * y_ref[...] * s * (1.0 - s)).astype(BF16)
            part = _dot_nt(dy, w_ref[...])

            @pl.when(j == 0)
            def _():
                acc_ref[...] = part

            @pl.when(j > 0)
            def _():
                acc_ref[...] += part

            @pl.when(j == N_CHIPS - 1)
            def _():
                do_ref[...] = acc_ref[...].astype(BF16)

    gate_specs = [pl.BlockSpec((tm, cw), functools.partial(
        lambda i, j, kk: (i, first_gate + per_branch * kk + j), kk=kk)) for kk in range(3)]
    tile = pl.BlockSpec((tm, cw), lambda i, j: (i, j))
    w_specs = [pl.BlockSpec((None, r, cw), lambda i, j: (j, 0, 0)) for r in widths]
    do_specs = [pl.BlockSpec((tm, r), lambda i, j: (i, 0)) for r in widths]
    wide = jax.ShapeDtypeStruct((T, D_MODEL), BF16)
    out = _pcall(
        body, name=name, grid=(T // tm, N_CHIPS), in_specs=gate_specs + [tile] * 4 + w_specs,
        out_specs=[tile] * 6 + do_specs,
        out_shape=[wide] * 6 + [jax.ShapeDtypeStruct((T, r), BF16) for r in widths],
        scratch_shapes=[pltpu.VMEM((tm, r), F32) for r in widths],
        compiler_params=pltpu.CompilerParams(dimension_semantics=("parallel", "arbitrary")),
    )(p, p, p, *ys, dm, *weights)
    return out[0:3], out[3:6], out[6:9]


def _branches_dw(outs, dys, name):
    T = dys[0].shape[0]
    cw = D_MODEL // N_CHIPS
    widths = [o.shape[1] for o in outs]

    def body(*refs):
        o_refs, dy_refs, dw_refs = refs[0:3], refs[3:6], refs[6:9]
        for o_ref, dy_ref, dw_ref, r in zip(o_refs, dy_refs, dw_refs, widths):
            dw = _dot_tn(o_ref[...], dy_ref[...]).astype(BF16)
            dw_ref[0] = dw[:r // 2]
            dw_ref[1] = dw[r // 2:]

    return _pcall(
        body, name=name, grid=(N_CHIPS,),
        in_specs=[pl.BlockSpec((T, r), lambda j: (0, 0)) for r in widths] + [pl.BlockSpec((T, cw), lambda j: (0, j))] * 3,
        out_specs=[pl.BlockSpec((2, None, r // 2, cw), lambda j: (0, j, 0, 0)) for r in widths],
        out_shape=[jax.ShapeDtypeStruct((2, N_CHIPS, r // 2, cw), BF16) for r in widths],
    )(*outs, *dys)


def _colsum(a, name):
    T, N = a.shape
    tm = _pick(T, 256, 16)

    def body(a_ref, o_ref):
        part = jnp.sum(a_ref[...].astype(F32), axis=0, keepdims=True)

        @pl.when(pl.program_id(0) == 0)
        def _():
            o_ref[...] = part

        @pl.when(pl.program_id(0) > 0)
        def _():
            o_ref[...] += part

    return _pcall(
        body, name=name, grid=(T // tm,), in_specs=[pl.BlockSpec((tm, N), lambda i: (i, 0))],
        out_specs=pl.BlockSpec((1, N), lambda i: (0, 0)), out_shape=jax.ShapeDtypeStruct((1, N), F32),
        compiler_params=pltpu.CompilerParams(dimension_semantics=("arbitrary",)),
    )(a)


def _adamw(w, g, m, v, name):
    R, C = w.shape
    tr = 256 if R % 256 == 0 else R
    c1 = 1.0 / (1.0 - ADAM_B1 ** ADAM_STEP)
    c2 = 1.0 / (1.0 - ADAM_B2 ** ADAM_STEP)

    def body(w_ref, g_ref, m_ref, v_ref, d_ref, nm_ref, nv_ref):
        gv = g_ref[...]
        mn = ADAM_B1 * m_ref[...] + (1.0 - ADAM_B1) * gv
        vn = ADAM_B2 * v_ref[...] + (1.0 - ADAM_B2) * (gv * gv)
        nm_ref[...] = mn
        nv_ref[...] = vn
        d_ref[...] = -ADAM_LR * ((mn * c1) / (jnp.sqrt(vn * c2) + ADAM_EPS) + ADAM_WD * w_ref[...])

    spec = pl.BlockSpec((tr, C), lambda i: (i, 0))
    return _pcall(
        body, name=name, grid=(R // tr,), in_specs=[spec] * 4, out_specs=[spec] * 3,
        out_shape=[jax.ShapeDtypeStruct((R, C), F32)] * 3,
    )(w, g, m, v)


def _adamw_from_halves(w, m, v, halves, axis, core, name):
    L, r, c = w.shape
    assert L == len(halves) == 2
    c1 = 1.0 / (1.0 - ADAM_B1 ** ADAM_STEP)
    c2 = 1.0 / (1.0 - ADAM_B2 ** ADAM_STEP)
    if axis == 0:
        tr = _pick(r // 2, 256, 8)
        per_half = r // 2 // tr
        steps = 2 * per_half

        def half_spec(layer, is_mine):
            def index(l, i, core_ref):
                read = jnp.logical_and(l == layer, (i // per_half == core_ref[0]) == is_mine)
                return jnp.where(read, i % per_half, 0), 0
            return pl.BlockSpec((tr, c), index)
    else:
        tr = _pick(r, 256, 8)
        steps = r // tr

        def half_spec(layer, is_mine):
            return pl.BlockSpec((tr, c // 2), lambda l, i, core_ref: (jnp.where(l == layer, i, 0), 0))
    whole = pl.BlockSpec((None, tr, c), lambda l, i, core_ref: (l, i, 0))
    half_specs = [half_spec(layer, is_mine) for layer in range(L) for is_mine in (True, False)]

    def body(core_ref, w_ref, m_ref, v_ref, mine0, theirs0, mine1, theirs1, g_ref, d_ref, nm_ref, nv_ref):
        shape = mine0.shape
        layer0 = jnp.full(shape, pl.program_id(0), jnp.int32) == 0
        mine = jnp.where(layer0, mine0[...], mine1[...])
        theirs = jnp.where(layer0, theirs0[...], theirs1[...])
        core_v = jnp.full(shape, core_ref[0], jnp.int32)

        def update(gv, cols):
            mn = ADAM_B1 * m_ref[:, cols] + (1.0 - ADAM_B1) * gv
            vn = ADAM_B2 * v_ref[:, cols] + (1.0 - ADAM_B2) * (gv * gv)
            g_ref[:, cols] = gv
            nm_ref[:, cols] = mn
            nv_ref[:, cols] = vn
            d_ref[:, cols] = -ADAM_LR * ((mn * c1) / (jnp.sqrt(vn * c2) + ADAM_EPS) + ADAM_WD * w_ref[:, cols])

        if axis == 0:
            here = jnp.full(shape, pl.program_id(1) // per_half, jnp.int32)
            update(jnp.where(here == core_v, mine, theirs), slice(None))
        else:
            update(jnp.where(core_v == 0, mine, theirs), slice(0, c // 2))
            update(jnp.where(core_v == 0, theirs, mine), slice(c // 2, c))

    grid_spec = pltpu.PrefetchScalarGridSpec(
        num_scalar_prefetch=1, grid=(L, steps), in_specs=[whole] * 3 + half_specs, out_specs=[whole] * 4)
    return _pcall(body, name=name, grid_spec=grid_spec, out_shape=[jax.ShapeDtypeStruct((L, r, c), F32)] * 4,
                  )(core, w, m, v, *halves[0], *halves[1])


def _log_sigmoid(z):
    return jnp.minimum(z, 0.0) - jnp.log(1.0 + jnp.exp(-jnp.abs(z)))


def _dot3(x, m01):
    x1 = x.astype(BF16)
    r1 = x - x1.astype(F32)
    x2 = r1.astype(BF16)
    x3 = (r1 - x2.astype(F32)).astype(BF16)
    n = x.shape[0]
    if n % 16:
        return (jnp.dot(x1, m01, preferred_element_type=F32) + jnp.dot(x2, m01, preferred_element_type=F32)
                + jnp.dot(x3, m01, preferred_element_type=F32))
    y = jnp.dot(jnp.concatenate([x1, x2, x3], axis=0), m01, preferred_element_type=F32)
    return y[:n] + y[n:2 * n] + y[2 * n:]


def _dot_nt(a, b):
    return lax.dot_general(a, b, NT, preferred_element_type=F32)


def _dot_tn(a, b):
    return lax.dot_general(a, b, TN, preferred_element_type=F32)


def _iota2(shape):
    return lax.broadcasted_iota(jnp.int32, shape, 0), lax.broadcasted_iota(jnp.int32, shape, 1)


HEADS_PER_STEP = 4
HEADS = range(HEADS_PER_STEP)


CHUNK_HEADS_PER_STEP = 4
CHUNK_HEADS = range(CHUNK_HEADS_PER_STEP)


def _head_spec(T, rows=None, width=HEAD_DIM, heads=HEADS_PER_STEP):
    return pl.BlockSpec((heads, T if rows is None else rows, width), lambda g: (g, 0, 0))


def _sb_weights(qb, kb, t0, s0, racc, m_gt, row, col):
    z = _dot_nt(qb, kb) * SCALE
    strict = (s0 + col) < (t0 + row)
    lb = _log_sigmoid(z)
    lf = jnp.where(strict, lb - z, 0.0)
    between = _dot3(lf, m_gt) + racc
    w = jnp.where(strict, jnp.exp(lb + between), 0.0)
    return strict, lb, lf, w


def _sb_fwd(q, k, v, name, carry=None):
    H, T, _ = q.shape
    nb = T // QB

    def body(q_ref, k_ref, v_ref, o_ref):
        row, col = _iota2((QB, QB))
        m_gt = (row > col).astype(BF16)

        def qblock(i, _):
            t0 = pl.multiple_of(i * QB, QB)
            qbs = [q_ref[h, pl.ds(t0, QB), :].astype(BF16) for h in HEADS]

            def kblock(jj, carry):
                s0 = pl.multiple_of((i - jj) * QB, QB)
                out = []
                for h in HEADS:
                    acc, racc = carry[h]
                    kb = k_ref[h, pl.ds(s0, QB), :].astype(BF16)
                    vb = v_ref[h, pl.ds(s0, QB), :].astype(BF16)
                    _, _, lf, w = _sb_weights(qbs[h], kb, t0, s0, racc, m_gt, row, col)
                    acc = acc + jnp.dot(w.astype(BF16), vb, preferred_element_type=F32)
                    out.append((acc, racc + jnp.sum(lf, axis=1, keepdims=True)))
                return tuple(out)

            res = lax.fori_loop(0, i + 1, kblock,
                                tuple((jnp.zeros((QB, HEAD_DIM), F32), jnp.zeros((QB, 1), F32)) for _ in HEADS))
            for h in HEADS:
                o_ref[h, pl.ds(t0, QB), :] = res[h][0].astype(BF16)
            return 0

        lax.fori_loop(0, nb, qblock, 0)

    spec = _head_spec(T)
    return _pcall_carrying(body, carry, name=name, grid=(H // HEADS_PER_STEP,), in_specs=[spec] * 3, out_specs=spec,
                           out_shape=jax.ShapeDtypeStruct((H, T, HEAD_DIM), BF16))(q, k, v)


def _sb_bwd(q, k, v, do, name, carry=None):
    H, T, _ = q.shape
    nb = T // QB

    def body(q_ref, k_ref, v_ref, do_ref, dq_ref, dk_out, dv_out, racc_ref, dk_ref, dv_ref):
        row, col = _iota2((QB, QB))
        m_gt = (row > col).astype(BF16)
        m_lt = (row < col).astype(BF16)
        dk_ref[...] = jnp.zeros_like(dk_ref)
        dv_ref[...] = jnp.zeros_like(dv_ref)

        def qblock(i, _):
            t0 = pl.multiple_of(i * QB, QB)
            qbs = [q_ref[h, pl.ds(t0, QB), :].astype(BF16) for h in HEADS]
            dobs = [do_ref[h, pl.ds(t0, QB), :].astype(BF16) for h in HEADS]

            def suffix(jj, raccs):
                j = i - jj
                s0 = pl.multiple_of(j * QB, QB)
                out = []
                for h in HEADS:
                    kb = k_ref[h, pl.ds(s0, QB), :].astype(BF16)
                    z = _dot_nt(qbs[h], kb) * SCALE
                    lf = jnp.where((s0 + col) < (t0 + row), _log_sigmoid(z) - z, 0.0)
                    racc_ref[h, j] = raccs[h]
                    out.append(raccs[h] + jnp.sum(lf, axis=1, keepdims=True))
                return tuple(out)

            lax.fori_loop(0, i + 1, suffix, tuple(jnp.zeros((QB, 1), F32) for _ in HEADS))

            def kblock(j, carry):
                s0 = pl.multiple_of(j * QB, QB)
                out = []
                for h in HEADS:
                    dq, cacc = carry[h]
                    kb = k_ref[h, pl.ds(s0, QB), :].astype(BF16)
                    vb = v_ref[h, pl.ds(s0, QB), :].astype(BF16)
                    strict, lb, _, w = _sb_weights(qbs[h], kb, t0, s0, racc_ref[h, j], m_gt, row, col)
                    e = _dot_nt(dobs[h], vb) * w
                    c_left = _dot3(e, m_lt) + cacc
                    sig = jnp.exp(lb)
                    dz = jnp.where(strict, e * (1.0 - sig) - c_left * sig, 0.0).astype(BF16)
                    dq = dq + jnp.dot(dz, kb, preferred_element_type=F32)
                    dk_ref[h, pl.ds(s0, QB), :] += _dot_tn(dz, qbs[h]) * SCALE
                    dv_ref[h, pl.ds(s0, QB), :] += _dot_tn(w.astype(BF16), dobs[h])
                    out.append((dq, cacc + jnp.sum(e, axis=1, keepdims=True)))
                return tuple(out)

            res = lax.fori_loop(0, i + 1, kblock,
                                tuple((jnp.zeros((QB, HEAD_DIM), F32), jnp.zeros((QB, 1), F32)) for _ in HEADS))
            for h in HEADS:
                dq_ref[h, pl.ds(t0, QB), :] = (res[h][0] * SCALE).astype(BF16)
            return 0

        lax.fori_loop(0, nb, qblock, 0)
        dk_out[...] = dk_ref[...].astype(BF16)
        dv_out[...] = dv_ref[...].astype(BF16)

    spec = _head_spec(T)
    acc = pltpu.VMEM((HEADS_PER_STEP, T, HEAD_DIM), F32)
    return _pcall_carrying(body, carry, name=name, grid=(H // HEADS_PER_STEP,), in_specs=[spec] * 4,
                           out_specs=[spec] * 3, out_shape=[jax.ShapeDtypeStruct((H, T, HEAD_DIM), BF16)] * 3,
                           scratch_shapes=[pltpu.VMEM((HEADS_PER_STEP, nb, QB, 1), F32), acc, acc])(q, k, v, do)


def _fox_logits(qb, kb, fq, fk, t0, s0, row, col):
    z = _dot_nt(qb, kb) * SCALE + fq - fk
    return jnp.where((s0 + col) <= (t0 + row), z, NEG)


def _fox_specs(T):
    return _head_spec(T, width=1), _head_spec(T, rows=T // QB, width=QB)


def _fox_fwd(q, k, v, fq, fk, name, carry=None):
    H, T, _ = q.shape
    nb = T // QB

    def body(q_ref, k_ref, v_ref, fq_ref, fk_ref, o_ref, lse_ref):
        row, col = _iota2((QB, QB))

        def qblock(i, _):
            t0 = pl.multiple_of(i * QB, QB)
            qbs = [q_ref[h, pl.ds(t0, QB), :].astype(BF16) for h in HEADS]
            fqs = [fq_ref[h, pl.ds(t0, QB), :] for h in HEADS]

            def kblock(j, carry):
                s0 = pl.multiple_of(j * QB, QB)
                out = []
                for h in HEADS:
                    acc, m, l = carry[h]
                    kb = k_ref[h, pl.ds(s0, QB), :].astype(BF16)
                    vb = v_ref[h, pl.ds(s0, QB), :].astype(BF16)
                    z = _fox_logits(qbs[h], kb, fqs[h], fk_ref[h, pl.ds(j, 1), :], t0, s0, row, col)
                    m_new = jnp.maximum(m, jnp.max(z, axis=1, keepdims=True))
                    a = jnp.exp(m - m_new)
                    p = jnp.exp(z - m_new)
                    acc = a * acc + jnp.dot(p.astype(BF16), vb, preferred_element_type=F32)
                    out.append((acc, m_new, a * l + jnp.sum(p, axis=1, keepdims=True)))
                return tuple(out)

            res = lax.fori_loop(0, i + 1, kblock,
                                tuple((jnp.zeros((QB, HEAD_DIM), F32), jnp.full((QB, 1), NEG, F32),
                                       jnp.zeros((QB, 1), F32)) for _ in HEADS))
            for h in HEADS:
                acc, m, l = res[h]
                o_ref[h, pl.ds(t0, QB), :] = (acc / l).astype(BF16)
                lse_ref[h, pl.ds(t0, QB), :] = m + jnp.log(l)
            return 0

        lax.fori_loop(0, nb, qblock, 0)

    spec = _head_spec(T)
    fq_spec, fk_spec = _fox_specs(T)
    return _pcall_carrying(
        body, carry, name=name, grid=(H // HEADS_PER_STEP,), in_specs=[spec] * 3 + [fq_spec, fk_spec],
        out_specs=[spec, fq_spec],
        out_shape=[jax.ShapeDtypeStruct((H, T, HEAD_DIM), BF16), jax.ShapeDtypeStruct((H, T, 1), F32)],
    )(q, k, v, fq, fk)


def _fox_bwd(q, k, v, fq, fk, lse, do, name, carry=None):
    H, T, _ = q.shape
    nb = T // QB

    def body(q_ref, k_ref, v_ref, fq_ref, fk_ref, lse_ref, do_ref, dq_ref, dk_out, dv_out, dfk_ref, dk_ref, dv_ref):
        row, col = _iota2((QB, QB))
        dk_ref[...] = jnp.zeros_like(dk_ref)
        dv_ref[...] = jnp.zeros_like(dv_ref)
        dfk_ref[...] = jnp.zeros_like(dfk_ref)

        def qblock(i, _):
            t0 = pl.multiple_of(i * QB, QB)
            qbs = [q_ref[h, pl.ds(t0, QB), :].astype(BF16) for h in HEADS]
            fqs = [fq_ref[h, pl.ds(t0, QB), :] for h in HEADS]
            lses = [lse_ref[h, pl.ds(t0, QB), :] for h in HEADS]
            dobs = [do_ref[h, pl.ds(t0, QB), :].astype(BF16) for h in HEADS]

            def probs(h, j):
                s0 = pl.multiple_of(j * QB, QB)
                kb = k_ref[h, pl.ds(s0, QB), :].astype(BF16)
                vb = v_ref[h, pl.ds(s0, QB), :].astype(BF16)
                z = _fox_logits(qbs[h], kb, fqs[h], fk_ref[h, pl.ds(j, 1), :], t0, s0, row, col)
                return s0, kb, jnp.exp(z - lses[h]), _dot_nt(dobs[h], vb)

            def row_dot(j, deltas):
                out = []
                for h in HEADS:
                    _, _, p, dp = probs(h, j)
                    out.append(deltas[h] + jnp.sum(p * dp, axis=1, keepdims=True))
                return tuple(out)

            deltas = lax.fori_loop(0, i + 1, row_dot, tuple(jnp.zeros((QB, 1), F32) for _ in HEADS))

            def kblock(j, dqs):
                out = []
                for h in HEADS:
                    s0, kb, p, dp = probs(h, j)
                    dz = p * (dp - deltas[h])
                    dzb = dz.astype(BF16)
                    dk_ref[h, pl.ds(s0, QB), :] += _dot_tn(dzb, qbs[h]) * SCALE
                    dv_ref[h, pl.ds(s0, QB), :] += _dot_tn(p.astype(BF16), dobs[h])
                    dfk_ref[h, pl.ds(j, 1), :] += -jnp.sum(dz, axis=0, keepdims=True)
                    out.append(dqs[h] + jnp.dot(dzb, kb, preferred_element_type=F32))
                return tuple(out)

            dqs = lax.fori_loop(0, i + 1, kblock, tuple(jnp.zeros((QB, HEAD_DIM), F32) for _ in HEADS))
            for h in HEADS:
                dq_ref[h, pl.ds(t0, QB), :] = (dqs[h] * SCALE).astype(BF16)
            return 0

        lax.fori_loop(0, nb, qblock, 0)
        dk_out[...] = dk_ref[...].astype(BF16)
        dv_out[...] = dv_ref[...].astype(BF16)

    spec = _head_spec(T)
    fq_spec, fk_spec = _fox_specs(T)
    acc = pltpu.VMEM((HEADS_PER_STEP, T, HEAD_DIM), F32)
    return _pcall_carrying(body, carry, name=name, grid=(H // HEADS_PER_STEP,),
                           in_specs=[spec] * 3 + [fq_spec, fk_spec, fq_spec, spec],
                           out_specs=[spec, spec, spec, fk_spec],
                           out_shape=[jax.ShapeDtypeStruct((H, T, HEAD_DIM), BF16)] * 3
                           + [jax.ShapeDtypeStruct((H, nb, QB), F32)],
                           scratch_shapes=[acc, acc])(q, k, v, fq, fk, lse, do)


def _forget_cumsum(flog, name):
    R, T = flog.shape
    nb = T // QB

    def body(x_ref, o_ref):
        row, col = _iota2((QB, QB))
        m_le = (row <= col).astype(BF16)
        carry = jnp.zeros((R, 1), F32)
        for b in range(nb):
            lf = _log_sigmoid(x_ref[:, b * QB:(b + 1) * QB])
            o_ref[:, b * QB:(b + 1) * QB] = _dot3(lf, m_le) + carry
            carry = carry + jnp.sum(lf, axis=1, keepdims=True)

    spec = pl.BlockSpec((R, T), lambda: (0, 0))
    return _pcall(body, name=name, in_specs=[spec], out_specs=spec,
                  out_shape=jax.ShapeDtypeStruct((R, T), F32))(flog)


def _forget_cumsum_bwd(flog, df, name):
    R, T = flog.shape
    nb = T // QB

    def body(x_ref, df_ref, o_ref):
        row, col = _iota2((QB, QB))
        m_ge = (row >= col).astype(BF16)
        carry = jnp.zeros((R, 1), F32)
        for b in reversed(range(nb)):
            dfb = df_ref[:, b * QB:(b + 1) * QB]
            dlog = _dot3(dfb, m_ge) + carry
            o_ref[:, b * QB:(b + 1) * QB] = dlog * _sigmoid(-x_ref[:, b * QB:(b + 1) * QB])
            carry = carry + jnp.sum(dfb, axis=1, keepdims=True)

    spec = pl.BlockSpec((R, T), lambda: (0, 0))
    return _pcall(body, name=name, in_specs=[spec, spec], out_specs=spec,
                  out_shape=jax.ShapeDtypeStruct((R, T), F32))(flog, df)


def _chunk_probs(qc, kb, bias, c, col):
    z = _dot_nt(qc, kb) * SCALE + bias
    z = jnp.where((c - (LEFT_CHUNKS + 1)) * CHUNK + col >= jnp.maximum(0, (c - LEFT_CHUNKS) * CHUNK), z, NEG)
    p = jnp.exp(z - jnp.max(z, axis=1, keepdims=True))
    return p, jnp.sum(p, axis=1, keepdims=True)


def _chunk_specs(T, n=CHUNK_HEADS_PER_STEP):
    return (_head_spec(T, heads=n), _head_spec(T, rows=T + BAND_PAD, heads=n),
            _head_spec(T, rows=CHUNK, width=BAND, heads=n))


def _chunk_fwd(q, kp, vp, bias, name, carry=None):
    H, T, _ = q.shape
    nc = T // CHUNK

    def body(q_ref, kp_ref, vp_ref, bias_ref, o_ref):
        _, col = _iota2((CHUNK, BAND))

        def chunk(c, _):
            t0 = pl.multiple_of(c * CHUNK, CHUNK)
            for h in HEADS:
                qc = q_ref[h, pl.ds(t0, CHUNK), :].astype(BF16)
                kb = kp_ref[h, pl.ds(t0, BAND), :].astype(BF16)
                vb = vp_ref[h, pl.ds(t0, BAND), :].astype(BF16)
                p, l = _chunk_probs(qc, kb, bias_ref[h], c, col)
                o = jnp.dot(p.astype(BF16), vb, preferred_element_type=F32) / l
                o_ref[h, pl.ds(t0, CHUNK), :] = o.astype(BF16)
            return 0

        lax.fori_loop(0, nc, chunk, 0)

    q_spec, kp_spec, b_spec = _chunk_specs(T, HEADS_PER_STEP)
    return _pcall_carrying(body, carry, name=name, grid=(H // HEADS_PER_STEP,),
                           in_specs=[q_spec, kp_spec, kp_spec, b_spec], out_specs=q_spec,
                           out_shape=jax.ShapeDtypeStruct((H, T, HEAD_DIM), BF16))(q, kp, vp, bias)


def _chunk_bwd(q, kp, vp, bias, do, name, carry=None):
    H, T, _ = q.shape
    nc = T // CHUNK

    def body(q_ref, kp_ref, vp_ref, bias_ref, do_ref, dq_ref, dk_out, dv_out, db_ref, dkp_ref, dvp_ref):
        _, col = _iota2((CHUNK, BAND))
        dkp_ref[...] = jnp.zeros_like(dkp_ref)
        dvp_ref[...] = jnp.zeros_like(dvp_ref)
        db_ref[...] = jnp.zeros_like(db_ref)

        def chunk(c, _):
            t0 = pl.multiple_of(c * CHUNK, CHUNK)
            for h in CHUNK_HEADS:
                qc = q_ref[h, pl.ds(t0, CHUNK), :].astype(BF16)
                kb = kp_ref[h, pl.ds(t0, BAND), :].astype(BF16)
                vb = vp_ref[h, pl.ds(t0, BAND), :].astype(BF16)
                dob = do_ref[h, pl.ds(t0, CHUNK), :].astype(BF16)
                p, l = _chunk_probs(qc, kb, bias_ref[h], c, col)
                p = p / l
                dp = _dot_nt(dob, vb)
                dz = p * (dp - jnp.sum(p * dp, axis=1, keepdims=True))
                dzb = dz.astype(BF16)
                dq = jnp.dot(dzb, kb, preferred_element_type=F32) * SCALE
                dq_ref[h, pl.ds(t0, CHUNK), :] = dq.astype(BF16)
                dkp_ref[h, pl.ds(t0, BAND), :] += _dot_tn(dzb, qc) * SCALE
                dvp_ref[h, pl.ds(t0, BAND), :] += _dot_tn(p.astype(BF16), dob)
                db_ref[h] += dz
            return 0

        lax.fori_loop(0, nc, chunk, 0)
        dk_out[...] = dkp_ref[:, BAND_PAD:, :].astype(BF16)
        dv_out[...] = dvp_ref[:, BAND_PAD:, :].astype(BF16)

    q_spec, kp_spec, b_spec = _chunk_specs(T)
    acc = pltpu.VMEM((CHUNK_HEADS_PER_STEP, T + BAND_PAD, HEAD_DIM), F32)
    return _pcall_carrying(body, carry, name=name, grid=(H // CHUNK_HEADS_PER_STEP,),
                           in_specs=[q_spec, kp_spec, kp_spec, b_spec, q_spec],
                           out_specs=[q_spec, q_spec, q_spec, b_spec],
                           out_shape=[jax.ShapeDtypeStruct((H, T, HEAD_DIM), BF16)] * 3
                           + [jax.ShapeDtypeStruct((H, CHUNK, BAND), F32)],
                           scratch_shapes=[acc, acc])(q, kp, vp, bias, do)


HBM_SPEC = pl.BlockSpec(memory_space=pltpu.HBM)


def _position():
    return lax.axis_index("x"), lax.axis_index("y"), lax.axis_index("c")


def _other_chips(x, y):
    chips = [(1 - x, y), (x, 1 - y), (1 - x, 1 - y)]
    return [(chip, 2 * chip[0] + chip[1]) for chip in chips]


def _remote_copy(src, dst, send_sems, recv_sems, k, to):
    return pltpu.make_async_remote_copy(src_ref=src, dst_ref=dst, send_sem=send_sems.at[k], recv_sem=recv_sems.at[k],
                                        device_id=to, device_id_type=MESH)


def _place_own(w, chip, name):
    _, r, c = w.shape
    tr = _pick(r, 256, 16)

    def body(chip_ref, w_ref, *out_refs):
        for l, o_ref in enumerate(out_refs):
            o_ref[...] = w_ref[l].astype(BF16)

    grid_spec = pltpu.PrefetchScalarGridSpec(
        num_scalar_prefetch=1, grid=(r // tr,), in_specs=[pl.BlockSpec((DEPTH, tr, c), lambda i, ch: (0, i, 0))],
        out_specs=[pl.BlockSpec((None, tr, c), lambda i, ch: (ch[0], i, 0))] * DEPTH)
    return _pcall(body, name=name, grid_spec=grid_spec,
                  out_shape=[jax.ShapeDtypeStruct((N_CHIPS, r, c), BF16)] * DEPTH)(chip, w)


def _gather_over_ici(srcs, bufs, new):
    x, y, c = _position()
    me = 2 * x + y
    return [(b.at[me, c], b.at[me, c], (*chip, c)) for b in bufs for chip, _ in _other_chips(x, y)]


def _gather_to_sibling(srcs, bufs, new):
    x, y, c = _position()
    return [(b.at[idx, c], b.at[idx, c], (x, y, 1 - c)) for b in bufs for _, idx in _other_chips(x, y)]


def _gather_layer(bufs, name):
    n = len(bufs)

    def body(*refs):
        dst = refs[n:2 * n]
        send_sems, recv_sems = refs[2 * n:]
        x, y, c = _position()
        me = 2 * x + y
        sibling = (x, y, 1 - c)
        others = _other_chips(x, y)
        first = [_remote_copy(dst[i].at[me, c], dst[i].at[me, c], send_sems, recv_sems, 3 * i + k, (*chip, c))
                 for i in range(n) for k, (chip, _) in enumerate(others)]
        for cp in first:
            cp.start()
        passed = []
        for i in range(n):
            for k, (_, idx) in enumerate(others):
                landed = dst[i].at[idx, c]
                _remote_copy(landed, landed, send_sems, recv_sems, 3 * i + k, sibling).wait_recv()
                passed.append(_remote_copy(landed, landed, send_sems, recv_sems, 3 * (n + i) + k, sibling))
                passed[-1].start()
        for i in range(n):
            for k, (_, idx) in enumerate(others):
                from_sibling = dst[i].at[idx, 1 - c]
                _remote_copy(from_sibling, from_sibling, send_sems, recv_sems, 3 * (n + i) + k, sibling).wait_recv()
        for cp in first + passed:
            cp.wait_send()

    return _pcall(
        body, name=name, in_specs=[HBM_SPEC] * n, out_specs=[HBM_SPEC] * n,
        out_shape=[jax.ShapeDtypeStruct(b.shape, b.dtype) for b in bufs],
        scratch_shapes=[pltpu.SemaphoreType.DMA((6 * n,)), pltpu.SemaphoreType.DMA((6 * n,))],
        input_output_aliases={i: i for i in range(n)},
    )(*bufs)


def _send_other_half(parts, name):
    n = len(parts)

    def body(*refs):
        src, got = refs[:n], refs[n:2 * n]
        send_sems, recv_sems = refs[2 * n:]
        x, y, c = _position()
        copies = [_remote_copy(src[i].at[1 - c], got[i], send_sems, recv_sems, i, (x, y, 1 - c)) for i in range(n)]
        for cp in copies:
            cp.start()
        for cp in copies:
            cp.wait()

    return _pcall(
        body, name=name, in_specs=[HBM_SPEC] * n, out_specs=[HBM_SPEC] * n,
        out_shape=[jax.ShapeDtypeStruct(p.shape[1:], p.dtype) for p in parts],
        scratch_shapes=[pltpu.SemaphoreType.DMA((n,)), pltpu.SemaphoreType.DMA((n,))],
    )(*parts)


def _scatter_chips(parts, name):
    n = len(parts)

    def body(*refs):
        src, dst = refs[:n], refs[n:2 * n]
        send_sems, recv_sems = refs[2 * n:]
        x, y, c = _position()
        others = _other_chips(x, y)
        sends = [_remote_copy(src[i].at[idx], dst[i].at[k], send_sems, recv_sems, 3 * i + k, (*chip, c))
                 for i in range(n) for k, (chip, idx) in enumerate(others)]
        for cp in sends:
            cp.start()
        for cp in sends:
            cp.wait()

    return _pcall(
        body, name=name, in_specs=[HBM_SPEC] * n, out_specs=[HBM_SPEC] * n,
        out_shape=[jax.ShapeDtypeStruct((3,) + p.shape[1:], p.dtype) for p in parts],
        scratch_shapes=[pltpu.SemaphoreType.DMA((3 * n,)), pltpu.SemaphoreType.DMA((3 * n,))],
    )(*parts)


def _swap_with_sibling(arrs, name):
    n = len(arrs)

    def body(*refs):
        src, dst = refs[:n], refs[n:2 * n]
        send_sems, recv_sems = refs[2 * n:]
        x, y, c = _position()
        copies = [_remote_copy(src[i], dst[i], send_sems, recv_sems, i, (x, y, 1 - c)) for i in range(n)]
        for cp in copies:
            cp.start()
        for cp in copies:
            cp.wait()

    return _pcall(
        body, name=name, in_specs=[HBM_SPEC] * n, out_specs=[HBM_SPEC] * n,
        out_shape=[jax.ShapeDtypeStruct(a.shape, a.dtype) for a in arrs],
        scratch_shapes=[pltpu.SemaphoreType.DMA((n,)), pltpu.SemaphoreType.DMA((n,))],
    )(*arrs)


def _allreduce_small(v, name):
    R, C = v.shape

    def body(v_ref, o_ref, slots, send_sems, recv_sems):
        x, y, c = _position()
        me = 4 * x + 2 * y + c
        slots[0] = v_ref[...]
        sends = []
        for r in range(1, 8):
            fx, fy, fc = (r >> 2) & 1, (r >> 1) & 1, r & 1
            to = (x ^ fx, y ^ fy, c ^ fc)
            cp = pltpu.make_async_remote_copy(src_ref=v_ref, dst_ref=slots.at[r], send_sem=send_sems.at[r - 1],
                                              recv_sem=recv_sems.at[r - 1], device_id=to, device_id_type=MESH)
            cp.start()
            sends.append(cp)
        for cp in sends:
            cp.wait()
        acc = slots[me]
        for d in range(1, 8):
            acc = acc + slots[d ^ me]
        o_ref[...] = acc

    vmem = pl.BlockSpec(memory_space=pltpu.VMEM)
    return _pcall(
        body, name=name, in_specs=[vmem], out_specs=vmem, out_shape=jax.ShapeDtypeStruct((R, C), F32),
        scratch_shapes=[pltpu.VMEM((8, R, C), F32), pltpu.SemaphoreType.DMA((7,)), pltpu.SemaphoreType.DMA((7,))],
    )(v)


def _add_pair(which, both, got, name):
    _, N, R, C = both.shape
    tr = _pick(R, 512, 16)

    def body(which_ref, a_ref, b_ref, o_ref):
        o_ref[...] = (a_ref[...].astype(F32) + b_ref[...].astype(F32)).astype(BF16)

    spec = pl.BlockSpec((None, tr, C), lambda n, i, w: (n, i, 0))
    grid_spec = pltpu.PrefetchScalarGridSpec(
        num_scalar_prefetch=1, grid=(N, R // tr),
        in_specs=[pl.BlockSpec((None, None, tr, C), lambda n, i, w: (w[0], n, i, 0)), spec], out_specs=spec)
    return _pcall(body, name=name, grid_spec=grid_spec,
                  out_shape=jax.ShapeDtypeStruct((N, R, C), BF16))(which, both, got)


def _sum_chips(which, own, others, name):
    N, R, C = others.shape
    tr = _pick(R, 512, 16)

    def body(which_ref, own_ref, p_ref, o_ref):
        acc = own_ref[...].astype(F32)
        for n in range(N):
            acc = acc + p_ref[n].astype(F32)
        o_ref[...] = acc

    grid_spec = pltpu.PrefetchScalarGridSpec(
        num_scalar_prefetch=1, grid=(R // tr,),
        in_specs=[pl.BlockSpec((None, tr, C), lambda i, w: (w[0], i, 0)), pl.BlockSpec((N, tr, C), lambda i, w: (0, i, 0))],
        out_specs=pl.BlockSpec((tr, C), lambda i, w: (i, 0)))
    return _pcall(body, name=name, grid_spec=grid_spec,
                  out_shape=jax.ShapeDtypeStruct((R, C), F32))(which, own, others)


BIG = ("w_ffn1_in", "w_ffn1_out", "w_in", "w_br_sb", "w_br_ch", "w_br_fox", "w_out", "w_ffn2_in", "w_ffn2_out")
ROW_SHARDED = ("w_ffn1_out", "w_out", "w_ffn2_out")
SMALL = ("g_ffn1", "g_mix", "b_in", "rel_bias", "g_ffn2", "g_final")


def _pair_sums(split, tag):
    core = lax.axis_index("c").astype(jnp.int32)[None]
    got = _send_other_half(split, f"grad_split_{tag}")
    return [_add_pair(core, a, b, f"grad_pair_sum_{tag}_{i}") for i, (a, b) in enumerate(zip(split, got))]


def _scatter_plan(srcs, bufs, new):
    x, y, c = _position()
    return [(s.at[idx], d.at[k], (*chip, c)) for s, d in zip(srcs, new) for k, (chip, idx) in enumerate(_other_chips(x, y))]


def _scatter_carry(pair):
    landing = [jax.ShapeDtypeStruct((3,) + p.shape[1:], p.dtype) for p in pair]
    return _Carry(pair, [], landing, 3 * len(pair), _scatter_plan)


def _finish_grads(pair, landed, tag):
    chip = (2 * lax.axis_index("x") + lax.axis_index("y")).astype(jnp.int32)[None]
    mine = [_sum_chips(chip, p, q, f"grad_chip_sum_{tag}_{i}") for i, (p, q) in enumerate(zip(pair, landed))]
    return mine, _swap_with_sibling(mine, f"grad_share_{tag}")


SMALL_SIZES = {"g_ffn1": DEPTH * D_MODEL, "g_mix": DEPTH * D_MODEL, "b_in": DEPTH * IN_WIDTH,
               "rel_bias": DEPTH * N_REL * H_CH, "g_ffn2": DEPTH * D_MODEL, "g_final": D_MODEL}
SMALL_TOTAL = sum(SMALL_SIZES.values()) + 1
SMALL_ROWS = -(-SMALL_TOTAL // (8 * 128)) * 8


def _pack_small(vals, extra):
    flat = [vals[n].reshape(-1) for n in SMALL] + [extra.reshape(-1)]
    flat.append(jnp.zeros((SMALL_ROWS * 128 - SMALL_TOTAL,), F32))
    return jnp.concatenate(flat).reshape(SMALL_ROWS, 128)


def _unpack_small(pack, shapes):
    flat, out, off = pack.reshape(-1), {}, 0
    for n in SMALL:
        out[n] = flat[off:off + SMALL_SIZES[n]].reshape(shapes[n])
        off += SMALL_SIZES[n]
    return out, flat[off]


def _to_heads(t, n_heads):
    return jnp.transpose(t.reshape(t.shape[0], n_heads, HEAD_DIM), (1, 0, 2)).astype(BF16)


def _from_heads(t):
    return jnp.transpose(t, (1, 0, 2)).reshape(t.shape[1], t.shape[0] * HEAD_DIM)


def _pad_keys(t):
    return jnp.pad(t, ((0, 0), (BAND_PAD, 0), (0, 0)))


DIAGONALS = CHUNK + BAND - 1


def _band_bias(table):
    n_far = (LEFT_CHUNKS + 1) * CHUNK + CHUNK - MAX_REL
    near = table[N_REL - 1 - (DIAGONALS - n_far):N_REL - 1][::-1]
    diag = jnp.concatenate([jnp.broadcast_to(table[N_REL - 1], (n_far, H_CH)), near], axis=0).T
    diag = jnp.pad(diag, ((0, 0), (0, 1)))
    skew = jnp.tile(diag, (1, CHUNK))[:, :CHUNK * DIAGONALS].reshape(H_CH, CHUNK, DIAGONALS)
    return skew[:, :, CHUNK - 1:]


def _pad_w_in(w):
    qkv, f, gates = w[:, :QKV_WIDTH], w[:, QKV_WIDTH:QKV_WIDTH + H_FOX], w[:, QKV_WIDTH + H_FOX:]
    return jnp.concatenate([qkv, gates, f, jnp.zeros((w.shape[0], F_PAD - H_FOX), w.dtype)], axis=1)


def _unpad_w_in(w):
    return jnp.concatenate([w[:, :QKV_WIDTH], w[:, QKV_WIDTH + GATE_WIDTH:QKV_WIDTH + GATE_WIDTH + H_FOX],
                            w[:, QKV_WIDTH:QKV_WIDTH + GATE_WIDTH]], axis=1)


QKV_SPLITS = np.cumsum([0, W_SB, W_SB, W_SB, W_CH, W_CH, W_CH, W_FOX, W_FOX, W_FOX])


def _by_chip_rows(g):
    return g.reshape(2, N_CHIPS, g.shape[1] // N_CHIPS, g.shape[2])


def _ffn_fwd(x, g, w_in, w_out, tag, carries):
    h, gate, up, a = _ffn_in_swiglu(x, g, w_in, f"{tag}_in", _carry_of(carries, "in"))
    y = _mm(a, w_out, mode="nn", name=f"{tag}_out", alpha=0.5, res=x, gathered="row",
            carry=_carry_of(carries, "out"))
    return y, (h, gate, up, a)


def _ffn_bwd(x, g, w_in, w_out, saved, dy, tag, carries):
    h, gate, up, a = saved
    da = _mm(dy, w_out, mode="nt", name=f"{tag}_da", alpha=0.5, gathered="row", out_dtype=BF16,
             carry=_carry_of(carries, "da"))
    dw_out = _mm(a, dy, mode="tn", name=f"{tag}_dwout", alpha=0.5, tm_cap=1408, out_dtype=BF16, out_split="row",
                 carry=_carry_of(carries, "dwout"))
    dgu = _swiglu_bwd(gate, up, da, f"{tag}_dact")
    dw_in = _mm(h, dgu, mode="tn", name=f"{tag}_dwin", tm_cap=512, out_dtype=BF16, out_split="col",
                carry=_carry_of(carries, "dwin"))
    dx, dg = _mm(dgu, w_in, mode="nt", name=f"{tag}_dh", gathered="col", tn_cap=1024, carry=_carry_of(carries, "dh"),
                 norm_bwd=(x, g, dy))
    return dx, dg, dw_in, _by_chip_rows(dw_out)


def _mixer_fwd(x, lw, tag, carries):
    T = x.shape[0]
    h = _rmsnorm_fwd(x, lw["g_mix"], f"{tag}_norm")
    p = _mm(h, lw["w_in"], mode="nn", name=f"{tag}_proj", bias=lw["b_in"], tn_cap=896, out_dtype=BF16,
            carry=_carry_of(carries, "proj"))
    parts = [p[:, QKV_SPLITS[i]:QKV_SPLITS[i + 1]] for i in range(9)]
    qa, ka, va = (_to_heads(t, H_SB) for t in parts[0:3])
    qb, kb, vb = (_to_heads(t, H_CH) for t in parts[3:6])
    qc, kc, vc = (_to_heads(t, H_FOX) for t in parts[6:9])
    flog = _mm(lw["w_forget_t"], h, mode="nt", name=f"{tag}_flog")[:8] + lw["b_forget"]
    fcum = _forget_cumsum(flog, f"{tag}_fcum")[:H_FOX]
    fq, fk = fcum.reshape(H_FOX, T, 1), fcum.reshape(H_FOX, T // QB, QB)
    kbp, vbp = _pad_keys(kb), _pad_keys(vb)
    oa = _sb_fwd(qa, ka, va, f"{tag}_sb", _carry_of(carries, "sb"))
    ob = _chunk_fwd(qb, kbp, vbp, lw["bias"], f"{tag}_ch", _carry_of(carries, "ch"))
    oc, lse = _fox_fwd(qc, kc, vc, fq, fk, f"{tag}_fox", _carry_of(carries, "fox"))
    oam, obm, ocm = _from_heads(oa), _from_heads(ob), _from_heads(oc)
    ya, yb, yc, merged = _branches_fwd(p, (oam, obm, ocm), (lw["w_br_sb"], lw["w_br_ch"], lw["w_br_fox"]),
                                       f"{tag}_branches")
    y = _mm(merged, lw["w_out"], mode="nn", name=f"{tag}_out", res=x, gathered="row")
    saved = (h, p, (qa, ka, va), (qb, kbp, vbp), (qc, kc, vc), flog, fq, fk, lse, (oam, obm, ocm),
             (ya, yb, yc), merged)
    return y, saved


def _mixer_bwd(x, lw, saved, dy, tag, carries):
    (h, p, (qa, ka, va), (qb, kbp, vbp), (qc, kc, vc), flog, fq, fk, lse, (oam, obm, ocm),
     (ya, yb, yc), merged) = saved
    T = x.shape[0]
    grads = {}
    dmerged = _mm(dy, lw["w_out"], mode="nt", name=f"{tag}_dmerged", gathered="row",
                  carry=_carry_of(carries, "dmerged"))
    grads["w_out"] = _by_chip_rows(_mm(merged, dy, mode="tn", name=f"{tag}_dwout", tm_cap=1024, out_dtype=BF16,
                                       out_split="row"))
    (dya, dyb, dyc), dgates, (doa, dob, doc) = _branches_bwd(
        p, (ya, yb, yc), dmerged, (lw["w_br_sb"], lw["w_br_ch"], lw["w_br_fox"]), f"{tag}_dbranches")
    grads["w_br_sb"], grads["w_br_ch"], grads["w_br_fox"] = _branches_dw(
        (oam, obm, ocm), (dya, dyb, dyc), f"{tag}_dwbranches")
    dqa, dka, dva = _sb_bwd(qa, ka, va, _to_heads(doa, H_SB), f"{tag}_dsb", _carry_of(carries, "dsb"))
    dqb, dkb, dvb, dbias = _chunk_bwd(qb, kbp, vbp, lw["bias"], _to_heads(dob, H_CH), f"{tag}_dch",
                                      _carry_of(carries, "dch"))
    dqc, dkc, dvc, dfk = _fox_bwd(qc, kc, vc, fq, fk, lse, _to_heads(doc, H_FOX), f"{tag}_dfox",
                                  _carry_of(carries, "dfox"))
    dflog = _forget_cumsum_bwd(flog, jnp.pad(dfk.reshape(H_FOX, T), ((0, 8 - H_FOX), (0, 0))), f"{tag}_dfcum")
    dqkv = [_from_heads(t) for t in (dqa, dka, dva, dqb, dkb, dvb, dqc, dkc, dvc)]
    dforget = jnp.pad(dflog[:H_FOX].T, ((0, 0), (0, F_PAD - H_FOX))).astype(BF16)
    dpb = jnp.concatenate(dqkv + list(dgates) + [dforget], axis=1)
    grads["b_in"] = _colsum(dpb, f"{tag}_dbin")
    dw_in =_unpad_w_in(_mm(h, dpb, mode="tn", name=f"{tag}_dwin", tm_cap=512, tn_cap=896, out_dtype=BF16))
    grads["w_in"] = jnp.transpose(dw_in.reshape(2, D_MODEL // 2, N_CHIPS, IN_WIDTH // N_CHIPS), (0, 2, 1, 3))
    dx, grads["g_mix"] = _mm(dpb, lw["w_in"], mode="nt", name=f"{tag}_dh", tk_cap=896, tn_cap=1024,
                             norm_bwd=(x, lw["g_mix"], dy))
    grads["rel_bias"] = lw["bias_vjp"](dbias)[0]
    return dx, grads


def kernel(x, g_ffn1, w_ffn1_in, w_ffn1_out, g_mix, w_in, b_in, rel_bias, w_br_sb, w_br_ch, w_br_fox, w_out, g_ffn2, w_ffn2_in, w_ffn2_out, g_final, loss_target, m_g_ffn1, m_w_ffn1_in, m_w_ffn1_out, m_g_mix, m_w_in, m_b_in, m_rel_bias, m_w_br_sb, m_w_br_ch, m_w_br_fox, m_w_out, m_g_ffn2, m_w_ffn2_in, m_w_ffn2_out, m_g_final, v_g_ffn1, v_w_ffn1_in, v_w_ffn1_out, v_g_mix, v_w_in, v_b_in, v_rel_bias, v_w_br_sb, v_w_br_ch, v_w_br_fox, v_w_out, v_g_ffn2, v_w_ffn2_in, v_w_ffn2_out, v_g_final):
    names = ("g_ffn1", "w_ffn1_in", "w_ffn1_out", "g_mix", "w_in", "b_in", "rel_bias", "w_br_sb", "w_br_ch",
             "w_br_fox", "w_out", "g_ffn2", "w_ffn2_in", "w_ffn2_out", "g_final")
    w = dict(zip(names, (g_ffn1, w_ffn1_in, w_ffn1_out, g_mix, w_in, b_in, rel_bias, w_br_sb, w_br_ch,
                         w_br_fox, w_out, g_ffn2, w_ffn2_in, w_ffn2_out, g_final)))
    m = dict(zip(names, (m_g_ffn1, m_w_ffn1_in, m_w_ffn1_out, m_g_mix, m_w_in, m_b_in, m_rel_bias, m_w_br_sb,
                         m_w_br_ch, m_w_br_fox, m_w_out, m_g_ffn2, m_w_ffn2_in, m_w_ffn2_out, m_g_final)))
    v = dict(zip(names, (v_g_ffn1, v_w_ffn1_in, v_w_ffn1_out, v_g_mix, v_w_in, v_b_in, v_rel_bias, v_w_br_sb,
                         v_w_br_ch, v_w_br_fox, v_w_out, v_g_ffn2, v_w_ffn2_in, v_w_ffn2_out, v_g_final)))
    xs = x[0]
    tgt = loss_target[0]

    chip = (2 * lax.axis_index("x") + lax.axis_index("y")).astype(jnp.int32)[None]
    placed = [_place_own(w[n], chip, f"place_{n}") for n in BIG]
    bufs = [[p[l].reshape(N_CHIPS, 2, p[l].shape[1] // 2, p[l].shape[2]) for p in placed] for l in range(DEPTH)]
    padded_w_in = {}

    def layer_weights(l):
        lw = {n: b.reshape((N_CHIPS,) + w[n].shape[1:]) for n, b in zip(BIG, bufs[l])}
        if l not in padded_w_in:
            whole = jnp.concatenate([lw["w_in"][j] for j in range(N_CHIPS)], axis=1)
            forget_t = jnp.pad(whole[:, QKV_WIDTH:QKV_WIDTH + H_FOX].T, ((0, F_PAD - H_FOX), (0, 0)))
            padded_w_in[l] = (_pad_w_in(whole), forget_t)
        lw["w_in"], lw["w_forget_t"] = padded_w_in[l]
        lw["b_forget"] = jnp.pad(w["b_in"][l][QKV_WIDTH:QKV_WIDTH + H_FOX], (0, 8 - H_FOX))[:, None]
        lw["b_in"] = _pad_w_in(w["b_in"][l][None, :])
        lw["bias"], lw["bias_vjp"] = jax.vjp(_band_bias, w["rel_bias"][l])
        for n in ("g_ffn1", "g_mix", "g_ffn2"):
            lw[n] = w[n][l][None, :]
        return lw

    def landed_in(l, idx):
        def done(carry):
            for i, b in zip(idx, carry.out[0]):
                bufs[l][i] = b
        return done

    def over_ici(l, idx):
        return lambda: _Carry([], [bufs[l][i] for i in idx], [], 3 * len(idx), _gather_over_ici, landed_in(l, idx))

    def to_sibling(l, idx):
        return lambda: _Carry([], [bufs[l][i] for i in idx], [], 3 * len(idx), _gather_to_sibling, landed_in(l, idx))

    def ffn_fwd(x_in, lw, which, tag, carries):
        return _ffn_fwd(x_in, lw[f"g_{which}"], lw[f"w_{which}_in"], lw[f"w_{which}_out"], tag, carries)

    def ffn_bwd(x_in, lw, which, saved_acts, dy, tag, carries):
        return _ffn_bwd(x_in, lw[f"g_{which}"], lw[f"w_{which}_in"], lw[f"w_{which}_out"], saved_acts, dy, tag,
                        carries)

    FFN1, W_IN, MIXER_REST, FFN2_IN, FFN2_OUT = (0, 1), (2,), (3, 4, 5, 6), (7,), (8,)
    first = FFN1 + W_IN
    for i, b in zip(first, _gather_layer([bufs[0][i] for i in first], "gather_l0")):
        bufs[0][i] = b
    fwd_carries = [
        {"ffn1": {"in": [over_ici(0, MIXER_REST)], "out": [to_sibling(0, MIXER_REST), over_ici(0, FFN2_OUT)]},
         "mix": {"proj": [to_sibling(0, FFN2_OUT), over_ici(1, MIXER_REST)],
                 "sb": [over_ici(0, FFN2_IN), over_ici(1, FFN2_OUT)],
                 "ch": [to_sibling(0, FFN2_IN), over_ici(1, FFN1)],
                 "fox": [over_ici(1, W_IN)]},
         "ffn2": {"in": [to_sibling(1, MIXER_REST + FFN2_OUT + FFN1 + W_IN)]}},
        {"ffn1": {}, "mix": {"sb": [over_ici(1, FFN2_IN)], "ch": [to_sibling(1, FFN2_IN)]}, "ffn2": {}},
    ]

    saved = []
    act = xs
    for l in range(DEPTH):
        x0 = act
        x1, s1 = ffn_fwd(x0, layer_weights(l), "ffn1", f"l{l}_ffn1", fwd_carries[l]["ffn1"])
        x2, s2 = _mixer_fwd(x1, layer_weights(l), f"l{l}_mix", fwd_carries[l]["mix"])
        act, s3 = ffn_fwd(x2, layer_weights(l), "ffn2", f"l{l}_ffn2", fwd_carries[l]["ffn2"])
        saved.append((x0, s1, x1, s2, x2, s3))
    layers = [layer_weights(l) for l in range(DEPTH)]

    dact, dg_final, loss_row = _final_loss(act, w["g_final"][None, :], tgt, "final_loss")

    big_grads = {n: [None] * DEPTH for n in BIG}
    small_grads = {n: [None] * DEPTH for n in ("g_ffn1", "g_mix", "b_in", "rel_bias", "g_ffn2")}
    pair = [[None] * len(BIG) for _ in range(DEPTH)]
    landed = [[None] * len(BIG) for _ in range(DEPTH)]

    def pair_up(l, idx, tag):
        for i, p in zip(idx, _pair_sums([big_grads[BIG[i]][l] for i in idx], tag)):
            pair[l][i] = p

    core = lax.axis_index("c").astype(jnp.int32)[None]

    def to_sibling_half(l, idx, tag):
        def plan(srcs, bufs, new):
            x, y, c = _position()
            return [(s.at[1 - c], d, (x, y, 1 - c)) for s, d in zip(srcs, new)]
        def done(carry):
            for j, (i, got) in enumerate(zip(idx, carry.out[1])):
                pair[l][i] = _add_pair(core, big_grads[BIG[i]][l], got, f"grad_pair_sum_{tag}_{j}")
        def make():
            split = [big_grads[BIG[i]][l] for i in idx]
            return _Carry(split, [], [jax.ShapeDtypeStruct(s.shape[1:], s.dtype) for s in split], len(idx), plan, done)
        return make

    def exchange(l, idx):
        def done(carry):
            for i, a in zip(idx, carry.out[1]):
                landed[l][i] = a
        def make():
            carry = _scatter_carry([pair[l][i] for i in idx])
            carry.done = done
            return carry
        return make

    def exchange_one_way(l, i, k):
        def plan(srcs, bufs, new):
            x, y, c = _position()
            chip_k, idx_k = _other_chips(x, y)[k]
            return [(srcs[0].at[idx_k], bufs[0].at[k], (*chip_k, c))]
        def done(carry):
            landed[l][i] = carry.out[0][0]
        def make():
            if landed[l][i] is None:
                landed[l][i] = lax.empty((3,) + pair[l][i].shape[1:], BF16)
            return _Carry([pair[l][i]], [landed[l][i]], [], 1, plan, done)
        return make

    bwd_carries = [
        {"ffn2": {},
         "mix": {"dmerged": [to_sibling_half(0, FFN2_IN + FFN2_OUT, "l0_ffn2")],
                 "dsb": [exchange(1, FFN1 + W_IN)], "dch": [exchange(1, MIXER_REST + FFN2_IN + FFN2_OUT)],
                 "dfox": [exchange(0, FFN2_IN + FFN2_OUT)]},
         "ffn1": {"da": [exchange(0, MIXER_REST)], "dwout": [exchange_one_way(0, 2, 0)],
                  "dwin": [exchange_one_way(0, 2, 1)], "dh": [exchange_one_way(0, 2, 2)]}},
        {"ffn2": {}, "mix": {},
         "ffn1": {"da": [to_sibling_half(1, W_IN + MIXER_REST + FFN2_IN + FFN2_OUT, "l1_rest")]}},
    ]
    for l in reversed(range(DEPTH)):
        lw = layers[l]
        x0, s1, x1, s2, x2, s3 = saved[l]
        dx2, small_grads["g_ffn2"][l], big_grads["w_ffn2_in"][l], big_grads["w_ffn2_out"][l] = ffn_bwd(
            x2, lw, "ffn2", s3, dact, f"l{l}_ffn2", bwd_carries[l]["ffn2"])
        dx1, mg = _mixer_bwd(x1, lw, s2, dx2, f"l{l}_mix", bwd_carries[l]["mix"])
        for n in ("w_in", "w_br_sb", "w_br_ch", "w_br_fox", "w_out"):
            big_grads[n][l] = mg[n]
        small_grads["b_in"][l] = _unpad_w_in(mg["b_in"])[0]
        small_grads["g_mix"][l] = mg["g_mix"][0]
        small_grads["rel_bias"][l] = mg["rel_bias"]
        if l == 0:
            pair_up(0, W_IN + MIXER_REST, "l0_mix")
        dact, dg1, big_grads["w_ffn1_in"][l], big_grads["w_ffn1_out"][l] = ffn_bwd(
            x0, lw, "ffn1", s1, dx1, f"l{l}_ffn1", bwd_carries[l]["ffn1"])
        small_grads["g_ffn1"][l] = dg1[0]
        small_grads["g_ffn2"][l] = small_grads["g_ffn2"][l][0]
        if l == 1:
            pair_up(1, FFN1, "l1_ffn1")
    grad_x = dact[None]
    pair_up(0, FFN1, "l0_ffn1")
    for i, a in zip(FFN1, _scatter_chips([pair[0][i] for i in FFN1], "grad_scatter_l0")):
        landed[0][i] = a

    small = {n: jnp.stack(small_grads[n]) for n in small_grads}
    small["g_final"] = dg_final[0]
    small_sum, loss = _unpack_small(_allreduce_small(_pack_small(small, loss_row[0, :1]), "allreduce_small"),
                                    {n: w[n].shape for n in SMALL})

    mine, theirs = _finish_grads(pair[0] + pair[1], landed[0] + landed[1], "all")

    grads, delta, new_m, new_v = dict(small_sum), {}, {}, {}
    for i, n in enumerate(BIG):
        halves = [(mine[l * len(BIG) + i], theirs[l * len(BIG) + i]) for l in range(DEPTH)]
        grads[n], delta[n], new_m[n], new_v[n] = _adamw_from_halves(
            w[n], m[n], v[n], halves, 1 if n in ROW_SHARDED else 0, core, f"adamw_{n}")
    zero = jnp.zeros((1,), F32)
    sd, sm, sv = _adamw(_pack_small(w, zero), _pack_small(grads, zero), _pack_small(m, zero), _pack_small(v, zero),
                        "adamw_small")
    shapes = {n: w[n].shape for n in SMALL}
    for dst, src in ((delta, sd), (new_m, sm), (new_v, sv)):
        dst.update(_unpack_small(src, shapes)[0])

    return (loss, grad_x, *[grads[n] for n in names], *[delta[n] for n in names],
            *[new_m[n] for n in names], *[new_v[n] for n in names])
```

```python
import functools

import numpy as np
import jax
import jax.numpy as jnp
from jax import lax
from jax.experimental import pallas as pl
from jax.experimental.pallas import tpu as pltpu

F32 = jnp.float32
BF16 = jnp.bfloat16

D_MODEL = 1024
DEPTH = 2
CHUNK = 64
HEAD_DIM = 64
H_SB, H_CH, H_FOX = 4, 8, 4
W_SB, W_CH, W_FOX = H_SB * HEAD_DIM, H_CH * HEAD_DIM, H_FOX * HEAD_DIM
LEFT_CHUNKS = 8
MAX_REL = 128
N_REL = 2 * MAX_REL + 1
D_FF = 2816
QKV_WIDTH = 3 * (W_SB + W_CH + W_FOX)
GATE_WIDTH = 3 * D_MODEL
IN_WIDTH = QKV_WIDTH + H_FOX + GATE_WIDTH
F_PAD = 128
P_WIDTH = QKV_WIDTH + GATE_WIDTH + F_PAD
RMS_EPS = 1e-6
NEG = -1e30
SCALE = HEAD_DIM ** -0.5
QB = 256
BAND = (LEFT_CHUNKS + 2) * CHUNK
BAND_PAD = BAND - CHUNK

ADAM_LR, ADAM_B1, ADAM_B2, ADAM_EPS, ADAM_WD, ADAM_STEP = 0.001, 0.9, 0.999, 1e-08, 0.01, 10

N_CHIPS = 4
VMEM_BUDGET = 52 * 1024 * 1024

NT = (((1,), (1,)), ((), ()))
TN = (((0,), (0,)), ((), ()))
NN = (((1,), (0,)), ((), ()))
MESH = pl.DeviceIdType.MESH


def _pcall(body, **kw):
    return pl.pallas_call(body, **kw)


class _Carry:
    def __init__(self, srcs, bufs, new, count, plan, done=None):
        self.srcs, self.bufs, self.new, self.count, self.plan = list(srcs), list(bufs), list(new), count, plan
        self.out, self.done = None, done

    @staticmethod
    def join(parts):
        parts = [p for p in parts if p is not None]
        if not parts:
            return None

        def plan(srcs, bufs, new):
            copies, s, b, n = [], 0, 0, 0
            for p in parts:
                copies += p.plan(srcs[s:s + len(p.srcs)], bufs[b:b + len(p.bufs)], new[n:n + len(p.new)])
                s, b, n = s + len(p.srcs), b + len(p.bufs), n + len(p.new)
            return copies

        whole = _Carry(sum((p.srcs for p in parts), []), sum((p.bufs for p in parts), []),
                       sum((p.new for p in parts), []), sum(p.count for p in parts), plan)
        whole.parts = parts
        return whole

    def share_out(self):
        b, n = 0, 0
        for p in getattr(self, "parts", []):
            p.out = (self.out[0][b:b + len(p.bufs)], self.out[1][n:n + len(p.new)])
            b, n = b + len(p.bufs), n + len(p.new)
            p.share_out()
        if self.done is not None:
            self.done(self)


def _carry_of(carries, key):
    return _Carry.join([make() for make in carries.get(key, [])])


def _pcall_carrying(body, carry, **kw):
    if carry is None:
        return _pcall(body, **kw)
    in_specs = list(kw.pop("in_specs"))
    out_specs, out_shape = kw.pop("out_specs"), kw.pop("out_shape")
    single = not isinstance(out_shape, (list, tuple))
    out_specs = [out_specs] if single else list(out_specs)
    out_shape = [out_shape] if single else list(out_shape)
    scratch = list(kw.pop("scratch_shapes", []))
    grid = tuple(kw.get("grid", ()))
    n_in, n_out, n_scr = len(in_specs), len(out_specs), len(scratch)
    ns, nb, nn = len(carry.srcs), len(carry.bufs), len(carry.new)

    def body2(*refs):
        ins, srcs = refs[:n_in], refs[n_in:n_in + ns]
        at = n_in + ns + nb
        outs, bufs, new = refs[at:at + n_out], refs[at + n_out:at + n_out + nb], refs[at + n_out + nb:at + n_out + nb + nn]
        at += n_out + nb + nn
        scr, (send_sems, recv_sems) = refs[at:at + n_scr], refs[at + n_scr:]

        def copies():
            return [_remote_copy(s, d, send_sems, recv_sems, k, to)
                    for k, (s, d, to) in enumerate(carry.plan(srcs, bufs, new))]

        def start():
            for cp in copies():
                cp.start()

        def wait():
            for cp in copies():
                cp.wait()

        if grid:
            ids = [pl.program_id(a) for a in range(len(grid))]
            pl.when(functools.reduce(jnp.logical_and, [i == 0 for i in ids]))(start)
        else:
            start()
        body(*ins, *outs, *scr)
        if grid:
            pl.when(functools.reduce(jnp.logical_and, [i == g - 1 for i, g in zip(ids, grid)]))(wait)
        else:
            wait()

    call = _pcall(
        body2, in_specs=in_specs + [HBM_SPEC] * (ns + nb), out_specs=out_specs + [HBM_SPEC] * (nb + nn),
        out_shape=out_shape + [jax.ShapeDtypeStruct(b.shape, b.dtype) for b in carry.bufs] + carry.new,
        scratch_shapes=scratch + [pltpu.SemaphoreType.DMA((carry.count,)), pltpu.SemaphoreType.DMA((carry.count,))],
        input_output_aliases={n_in + ns + j: n_out + j for j in range(nb)}, **kw)

    def apply(*args):
        res = call(*args, *carry.srcs, *carry.bufs)
        carry.out = (list(res[n_out:n_out + nb]), list(res[n_out + nb:]))
        carry.share_out()
        return res[0] if single else list(res[:n_out])

    return apply


def _pick(n, cap, mult=128):
    best = None
    d = mult
    while d <= min(n, cap):
        if n % d == 0:
            best = d
        d += mult
    return n if best is None else best


def _nbytes(shape, dtype):
    return int(np.prod(shape)) * jnp.dtype(dtype).itemsize


def _mm(a, b, *, mode, name, out_dtype=F32, alpha=1.0, bias=None, res=None, tm_cap=1024, tn_cap=512,
        tk_cap=2816, gathered=None, out_split=None, carry=None, norm_bwd=None):
    if mode == "tn":
        K, M = a.shape
    else:
        M, K = a.shape
    dn = {"nn": NN, "nt": NT, "tn": TN}[mode]
    b_rows = None
    if gathered is None:
        N = b.shape[0] if mode == "nt" else b.shape[1]
        tn = {None: _pick(N, tn_cap), "col": N // N_CHIPS, "row": N // 2}[out_split]
        if out_split == "col":
            tm_cap = min(tm_cap, M // 2)
        tk = K if K <= tk_cap else _pick(K, tk_cap)
        b_spec = (pl.BlockSpec((tn, tk), lambda i, j, k: (j, k)) if mode == "nt"
                  else pl.BlockSpec((tk, tn), lambda i, j, k: (k, j)))
    else:
        r, c = b.shape[1:]
        rows, cols = (r, N_CHIPS * c) if gathered == "col" else (N_CHIPS * r, c)
        N = rows if mode == "nt" else cols
        assert K == (cols if mode == "nt" else rows), (name, K, rows, cols)
        if gathered == "col" and mode == "nn":
            tn, tk = c, (r if r <= tk_cap else _pick(r, tk_cap))
            b_spec = pl.BlockSpec((None, tk, c), lambda i, j, k: (j, k, 0))
        elif gathered == "col":
            tn, tk = _pick(r, tn_cap), c
            b_spec = pl.BlockSpec((None, tn, c), lambda i, j, k: (k, j, 0))
        elif mode == "nn":
            tn, tk, b_rows = _pick(c, tn_cap), K, N_CHIPS
            b_spec = pl.BlockSpec((N_CHIPS, r, tn), lambda i, j, k: (0, 0, j))
        else:
            tn, tk, b_rows = 2 * r, K, 2
            b_spec = pl.BlockSpec((2, r, c), lambda i, j, k: (j, 0, 0))
    tm = _pick(M, tm_cap)
    nk = K // tk

    a_spec = (pl.BlockSpec((tk, tm), lambda i, j, k: (k, i)) if mode == "tn"
              else pl.BlockSpec((tm, tk), lambda i, j, k: (i, k)))
    in_specs = [a_spec, b_spec]
    args = [a, b]
    if bias is not None:
        in_specs.append(pl.BlockSpec((1, tn), lambda i, j, k: (0, j)))
        args.append(bias)
    if res is not None:
        in_specs.append(pl.BlockSpec((tm, tn), lambda i, j, k: (i, j)))
        args.append(res)
    if norm_bwd is not None:
        assert tn == N and alpha == 1.0 and out_split is None and bias is None and res is None, name
        x_in, g_in, dres_in = norm_bwd
        in_specs += [pl.BlockSpec((tm, tn), lambda i, j, k: (i, 0)), pl.BlockSpec((1, tn), lambda i, j, k: (0, 0)),
                     pl.BlockSpec((tm, tn), lambda i, j, k: (i, 0))]
        args += [x_in, g_in, dres_in]

    est = 2 * (_nbytes((tm, tk), a.dtype) + _nbytes((tk, tn), b.dtype) + _nbytes((tm, tn), out_dtype))
    est += _nbytes((tm, tn), F32) * (3 if res is not None else 1)
    est += _nbytes((tm, tn), F32) * (4 if norm_bwd is not None else 0)
    assert est < VMEM_BUDGET, (name, est)

    def body(*refs):
        a_ref, b_ref = refs[0], refs[1]
        pos = 2
        bias_ref = res_ref = None
        if bias is not None:
            bias_ref = refs[pos]
            pos += 1
        if res is not None:
            res_ref = refs[pos]
            pos += 1
        if norm_bwd is not None:
            x_ref, g_ref, dres_ref = refs[pos:pos + 3]
            pos += 3
        o_ref = refs[pos]
        if norm_bwd is not None:
            dg_ref = refs[pos + 1]
            pos += 1
        acc_ref = refs[pos + 1] if nk > 1 else None

        bv = b_ref[...]
        if b_rows is not None:
            bv = bv.reshape(b_rows * bv.shape[1], bv.shape[2])
        part = lax.dot_general(a_ref[...].astype(BF16), bv.astype(BF16), dn, preferred_element_type=F32)

        def finish(acc):
            if alpha != 1.0:
                acc = acc * alpha
            if bias_ref is not None:
                acc = acc + bias_ref[...]
            if res_ref is not None:
                acc = acc + res_ref[...]
            if norm_bwd is not None:
                xv = x_ref[...]
                rstd = lax.rsqrt(jnp.mean(xv * xv, axis=-1, keepdims=True) + RMS_EPS)
                xhat = xv * rstd
                dyg = acc * g_ref[...]
                part_g = jnp.sum(acc * xhat, axis=0, keepdims=True)
                acc = dres_ref[...] + rstd * (dyg - xhat * jnp.mean(dyg * xhat, axis=-1, keepdims=True))
                first = pl.program_id(0) == 0

                @pl.when(first)
                def _():
                    dg_ref[...] = part_g

                @pl.when(jnp.logical_not(first))
                def _():
                    dg_ref[...] += part_g
            o_ref[...] = acc.astype(out_dtype)

        if nk == 1:
            finish(part)
        else:
            k = pl.program_id(2)

            @pl.when(k == 0)
            def _():
                acc_ref[...] = part

            @pl.when(k > 0)
            def _():
                acc_ref[...] += part

            @pl.when(k == nk - 1)
            def _():
                finish(acc_ref[...])

    if out_split == "col":
        per_half = M // 2 // tm
        out_spec = pl.BlockSpec((None, None, tm, tn), lambda i, j, k: (i // per_half, j, i % per_half, 0))
        out_shape = jax.ShapeDtypeStruct((2, N_CHIPS, M // 2, tn), out_dtype)
    elif out_split == "row":
        out_spec = pl.BlockSpec((None, tm, tn), lambda i, j, k: (j, i, 0))
        out_shape = jax.ShapeDtypeStruct((2, M, tn), out_dtype)
    else:
        out_spec = pl.BlockSpec((tm, tn), lambda i, j, k: (i, j))
        out_shape = jax.ShapeDtypeStruct((M, N), out_dtype)
    semantics = ("parallel", "parallel", "arbitrary")
    if norm_bwd is not None:
        out_spec = [out_spec, pl.BlockSpec((1, tn), lambda i, j, k: (0, 0))]
        out_shape = [out_shape, jax.ShapeDtypeStruct((1, N), F32)]
        semantics = ("arbitrary", "arbitrary", "arbitrary")
    return _pcall_carrying(
        body, carry, name=name, grid=(M // tm, N // tn, nk), in_specs=in_specs, out_specs=out_spec,
        out_shape=out_shape,
        scratch_shapes=[pltpu.VMEM((tm, tn), F32)] if nk > 1 else [],
        compiler_params=pltpu.CompilerParams(dimension_semantics=semantics),
    )(*args)


def _rmsnorm_fwd(x, g, name):
    T, Dm = x.shape
    tm = _pick(T, 256, 8)

    def body(x_ref, g_ref, h_ref):
        xv = x_ref[...]
        rstd = lax.rsqrt(jnp.mean(xv * xv, axis=-1, keepdims=True) + RMS_EPS)
        h_ref[...] = (xv * rstd * g_ref[...]).astype(BF16)

    return _pcall(
        body, name=name, grid=(T // tm,),
        in_specs=[pl.BlockSpec((tm, Dm), lambda i: (i, 0)), pl.BlockSpec((1, Dm), lambda i: (0, 0))],
        out_specs=pl.BlockSpec((tm, Dm), lambda i: (i, 0)),
        out_shape=jax.ShapeDtypeStruct((T, Dm), BF16),
    )(x, g)


def _final_loss(x, g, tgt, name):
    T, Dm = x.shape
    tm = _pick(T, 256, 8)

    def body(x_ref, g_ref, t_ref, dx_ref, dg_ref, loss_ref):
        xv = x_ref[...]
        gv = g_ref[...]
        rstd = lax.rsqrt(jnp.mean(xv * xv, axis=-1, keepdims=True) + RMS_EPS)
        xhat = xv * rstd
        err = xhat * gv - t_ref[...]
        part_loss = 0.5 * jnp.sum(jnp.mean(err * err, axis=-1, keepdims=True), axis=0, keepdims=True)
        dy = err * (1.0 / Dm)
        dyg = dy * gv
        dx_ref[...] = rstd * (dyg - xhat * jnp.mean(dyg * xhat, axis=-1, keepdims=True))
        part_g = jnp.sum(dy * xhat, axis=0, keepdims=True)
        part_l = jnp.broadcast_to(part_loss, (1, 128))

        @pl.when(pl.program_id(0) == 0)
        def _():
            dg_ref[...] = part_g
            loss_ref[...] = part_l

        @pl.when(pl.program_id(0) > 0)
        def _():
            dg_ref[...] += part_g
            loss_ref[...] += part_l

    row = pl.BlockSpec((tm, Dm), lambda i: (i, 0))
    vec = pl.BlockSpec((1, Dm), lambda i: (0, 0))
    return _pcall(
        body, name=name, grid=(T // tm,), in_specs=[row, vec, row],
        out_specs=[row, vec, pl.BlockSpec((1, 128), lambda i: (0, 0))],
        out_shape=[jax.ShapeDtypeStruct((T, Dm), F32), jax.ShapeDtypeStruct((1, Dm), F32),
                   jax.ShapeDtypeStruct((1, 128), F32)],
        compiler_params=pltpu.CompilerParams(dimension_semantics=("arbitrary",)),
    )(x, g, tgt)


def _sigmoid(z):
    return 1.0 / (1.0 + jnp.exp(-z))


def _ffn_in_swiglu(x, g, w_in, name, carry=None):
    T, Dm = x.shape
    cw = w_in.shape[2]
    tm = _pick(T, 512, 16)

    def body(x_ref, gain_ref, wg_ref, wu_ref, h_ref, g_ref, u_ref, a_ref):
        @pl.when(pl.program_id(1) == 0)
        def _():
            xv = x_ref[...]
            rstd = lax.rsqrt(jnp.mean(xv * xv, axis=-1, keepdims=True) + RMS_EPS)
            h_ref[...] = (xv * rstd * gain_ref[...]).astype(BF16)

        hv = h_ref[...]
        gv = jnp.dot(hv, wg_ref[...], preferred_element_type=F32)
        uv = jnp.dot(hv, wu_ref[...], preferred_element_type=F32)
        g_ref[...] = gv.astype(BF16)
        u_ref[...] = uv.astype(BF16)
        a_ref[...] = (gv * _sigmoid(gv) * uv).astype(BF16)

    row = pl.BlockSpec((tm, Dm), lambda i, j: (i, 0))
    out = pl.BlockSpec((tm, cw), lambda i, j: (i, j))
    return _pcall_carrying(
        body, carry, name=name, grid=(T // tm, 2),
        in_specs=[row, pl.BlockSpec((1, Dm), lambda i, j: (0, 0)), pl.BlockSpec((None, Dm, cw), lambda i, j: (j, 0, 0)),
                  pl.BlockSpec((None, Dm, cw), lambda i, j: (j + 2, 0, 0))],
        out_specs=[row, out, out, out],
        out_shape=[jax.ShapeDtypeStruct((T, Dm), BF16)] + [jax.ShapeDtypeStruct((T, D_FF), BF16)] * 3,
        compiler_params=pltpu.CompilerParams(dimension_semantics=("parallel", "arbitrary")),
    )(x, g, w_in, w_in)


def _swiglu_bwd(g, u, da, name):
    T = g.shape[0]
    tm = _pick(T, 256, 16)

    def body(g_ref, u_ref, da_ref, d_ref):
        gv = g_ref[...].astype(F32)
        uv = u_ref[...].astype(F32)
        dav = da_ref[...].astype(F32)
        sg = _sigmoid(gv)
        d_ref[:, :D_FF] = (dav * uv * (sg + gv * sg * (1.0 - sg))).astype(BF16)
        d_ref[:, D_FF:] = (dav * gv * sg).astype(BF16)

    row = pl.BlockSpec((tm, D_FF), lambda i: (i, 0))
    return _pcall(
        body, name=name, grid=(T // tm,), in_specs=[row, row, row],
        out_specs=pl.BlockSpec((tm, 2 * D_FF), lambda i: (i, 0)),
        out_shape=jax.ShapeDtypeStruct((T, 2 * D_FF), BF16),
    )(g, u, da)


def _branches_fwd(p, outs, weights, name):
    T = p.shape[0]
    cw = weights[0].shape[2]
    tm = _pick(T, 1024, 16)
    first_gate, per_branch = QKV_WIDTH // cw, D_MODEL // cw

    def body(*refs):
        gates, o_refs, w_refs, y_refs, m_ref = refs[0:3], refs[3:6], refs[6:9], refs[9:12], refs[12]
        merged = None
        for g_ref, o_ref, w_ref, y_ref in zip(gates, o_refs, w_refs, y_refs):
            y = jnp.dot(o_ref[...], w_ref[...], preferred_element_type=F32)
            y_ref[...] = y
            term = _sigmoid(g_ref[...].astype(F32)) * y
            merged = term if merged is None else merged + term
        m_ref[...] = merged.astype(BF16)

    gate_specs = [pl.BlockSpec((tm, cw), functools.partial(
        lambda i, j, kk: (i, first_gate + per_branch * kk + j), kk=kk)) for kk in range(3)]
    o_specs = [pl.BlockSpec((tm, o.shape[1]), lambda i, j: (i, 0)) for o in outs]
    w_specs = [pl.BlockSpec((None, wk.shape[1], cw), lambda i, j: (j, 0, 0)) for wk in weights]
    tile = pl.BlockSpec((tm, cw), lambda i, j: (i, j))
    return _pcall(
        body, name=name, grid=(T // tm, N_CHIPS), in_specs=gate_specs + o_specs + w_specs, out_specs=[tile] * 4,
        out_shape=[jax.ShapeDtypeStruct((T, D_MODEL), F32)] * 3 + [jax.ShapeDtypeStruct((T, D_MODEL), BF16)],
    )(p, p, p, *outs, *weights)


def _branches_bwd(p, ys, dm, weights, name):
    T = p.shape[0]
    cw = weights[0].shape[2]
    tm = _pick(T, 1024, 16)
    first_gate, per_branch = QKV_WIDTH // cw, D_MODEL // cw
    widths = [wk.shape[1] for wk in weights]

    def body(*refs):
        gates, y_refs, dm_ref, w_refs = refs[0:3], refs[3:6], refs[6], refs[7:10]
        dy_refs, dg_refs, do_refs, acc_refs = refs[10:13], refs[13:16], refs[16:19], refs[19:22]
        j = pl.program_id(1)
        dmv = dm_ref[...]
        for g_ref, y_ref, w_ref, dy_ref, dg_ref, do_ref, acc_ref in zip(gates, y_refs, w_refs, dy_refs, dg_refs,
                                                                       do_refs, acc_refs):
            s = _sigmoid(g_ref[...].astype(F32))
            dy = (s * dmv).astype(BF16)
            dy_ref[...] = dy
            dg_ref[...] = (dmv * y_ref[...] * s * (1.0 - s)).astype(BF16)
            part = _dot_nt(dy, w_ref[...])

            @pl.when(j == 0)
            def _():
                acc_ref[...] = part

            @pl.when(j > 0)
            def _():
                acc_ref[...] += part

            @pl.when(j == N_CHIPS - 1)
            def _():
                do_ref[...] = acc_ref[...].astype(BF16)

    gate_specs = [pl.BlockSpec((tm, cw), functools.partial(
        lambda i, j, kk: (i, first_gate + per_branch * kk + j), kk=kk)) for kk in range(3)]
    tile = pl.BlockSpec((tm, cw), lambda i, j: (i, j))
    w_specs = [pl.BlockSpec((None, r, cw), lambda i, j: (j, 0, 0)) for r in widths]
    do_specs = [pl.BlockSpec((tm, r), lambda i, j: (i, 0)) for r in widths]
    wide = jax.ShapeDtypeStruct((T, D_MODEL), BF16)
    out = _pcall(
        body, name=name, grid=(T // tm, N_CHIPS), in_specs=gate_specs + [tile] * 4 + w_specs,
        out_specs=[tile] * 6 + do_specs,
        out_shape=[wide] * 6 + [jax.ShapeDtypeStruct((T, r), BF16) for r in widths],
        scratch_shapes=[pltpu.VMEM((tm, r), F32) for r in widths],
        compiler_params=pltpu.CompilerParams(dimension_semantics=("parallel", "arbitrary")),
    )(p, p, p, *ys, dm, *weights)
    return out[0:3], out[3:6], out[6:9]


def _branches_dw(outs, dys, name):
    T = dys[0].shape[0]
    cw = D_MODEL // N_CHIPS
    widths = [o.shape[1] for o in outs]

    def body(*refs):
        o_refs, dy_refs, dw_refs = refs[0:3], refs[3:6], refs[6:9]
        for o_ref, dy_ref, dw_ref, r in zip(o_refs, dy_refs, dw_refs, widths):
            dw = _dot_tn(o_ref[...], dy_ref[...]).astype(BF16)
            dw_ref[0] = dw[:r // 2]
            dw_ref[1] = dw[r // 2:]

    return _pcall(
        body, name=name, grid=(N_CHIPS,),
        in_specs=[pl.BlockSpec((T, r), lambda j: (0, 0)) for r in widths] + [pl.BlockSpec((T, cw), lambda j: (0, j))] * 3,
        out_specs=[pl.BlockSpec((2, None, r // 2, cw), lambda j: (0, j, 0, 0)) for r in widths],
        out_shape=[jax.ShapeDtypeStruct((2, N_CHIPS, r // 2, cw), BF16) for r in widths],
    )(*outs, *dys)


def _colsum(a, name):
    T, N = a.shape
    tm = _pick(T, 256, 16)

    def body(a_ref, o_ref):
        part = jnp.sum(a_ref[...].astype(F32), axis=0, keepdims=True)

        @pl.when(pl.program_id(0) == 0)
        def _():
            o_ref[...] = part

        @pl.when(pl.program_id(0) > 0)
        def _():
            o_ref[...] += part

    return _pcall(
        body, name=name, grid=(T // tm,), in_specs=[pl.BlockSpec((tm, N), lambda i: (i, 0))],
        out_specs=pl.BlockSpec((1, N), lambda i: (0, 0)), out_shape=jax.ShapeDtypeStruct((1, N), F32),
        compiler_params=pltpu.CompilerParams(dimension_semantics=("arbitrary",)),
    )(a)


def _adamw(w, g, m, v, name):
    R, C = w.shape
    tr = 256 if R % 256 == 0 else R
    c1 = 1.0 / (1.0 - ADAM_B1 ** ADAM_STEP)
    c2 = 1.0 / (1.0 - ADAM_B2 ** ADAM_STEP)

    def body(w_ref, g_ref, m_ref, v_ref, d_ref, nm_ref, nv_ref):
        gv = g_ref[...]
        mn = ADAM_B1 * m_ref[...] + (1.0 - ADAM_B1) * gv
        vn = ADAM_B2 * v_ref[...] + (1.0 - ADAM_B2) * (gv * gv)
        nm_ref[...] = mn
        nv_ref[...] = vn
        d_ref[...] = -ADAM_LR * ((mn * c1) / (jnp.sqrt(vn * c2) + ADAM_EPS) + ADAM_WD * w_ref[...])

    spec = pl.BlockSpec((tr, C), lambda i: (i, 0))
    return _pcall(
        body, name=name, grid=(R // tr,), in_specs=[spec] * 4, out_specs=[spec] * 3,
        out_shape=[jax.ShapeDtypeStruct((R, C), F32)] * 3,
    )(w, g, m, v)


def _adamw_from_halves(w, m, v, halves, axis, core, name):
    L, r, c = w.shape
    assert L == len(halves) == 2
    c1 = 1.0 / (1.0 - ADAM_B1 ** ADAM_STEP)
    c2 = 1.0 / (1.0 - ADAM_B2 ** ADAM_STEP)
    if axis == 0:
        tr = _pick(r // 2, 256, 8)
        per_half = r // 2 // tr
        steps = 2 * per_half

        def half_spec(layer, is_mine):
            def index(l, i, core_ref):
                read = jnp.logical_and(l == layer, (i // per_half == core_ref[0]) == is_mine)
                return jnp.where(read, i % per_half, 0), 0
            return pl.BlockSpec((tr, c), index)
    else:
        tr = _pick(r, 256, 8)
        steps = r // tr

        def half_spec(layer, is_mine):
            return pl.BlockSpec((tr, c // 2), lambda l, i, core_ref: (jnp.where(l == layer, i, 0), 0))
    whole = pl.BlockSpec((None, tr, c), lambda l, i, core_ref: (l, i, 0))
    half_specs = [half_spec(layer, is_mine) for layer in range(L) for is_mine in (True, False)]

    def body(core_ref, w_ref, m_ref, v_ref, mine0, theirs0, mine1, theirs1, g_ref, d_ref, nm_ref, nv_ref):
        shape = mine0.shape
        layer0 = jnp.full(shape, pl.program_id(0), jnp.int32) == 0
        mine = jnp.where(layer0, mine0[...], mine1[...])
        theirs = jnp.where(layer0, theirs0[...], theirs1[...])
        core_v = jnp.full(shape, core_ref[0], jnp.int32)

        def update(gv, cols):
            mn = ADAM_B1 * m_ref[:, cols] + (1.0 - ADAM_B1) * gv
            vn = ADAM_B2 * v_ref[:, cols] + (1.0 - ADAM_B2) * (gv * gv)
            g_ref[:, cols] = gv
            nm_ref[:, cols] = mn
            nv_ref[:, cols] = vn
            d_ref[:, cols] = -ADAM_LR * ((mn * c1) / (jnp.sqrt(vn * c2) + ADAM_EPS) + ADAM_WD * w_ref[:, cols])

        if axis == 0:
            here = jnp.full(shape, pl.program_id(1) // per_half, jnp.int32)
            update(jnp.where(here == core_v, mine, theirs), slice(None))
        else:
            update(jnp.where(core_v == 0, mine, theirs), slice(0, c // 2))
            update(jnp.where(core_v == 0, theirs, mine), slice(c // 2, c))

    grid_spec = pltpu.PrefetchScalarGridSpec(
        num_scalar_prefetch=1, grid=(L, steps), in_specs=[whole] * 3 + half_specs, out_specs=[whole] * 4)
    return _pcall(body, name=name, grid_spec=grid_spec, out_shape=[jax.ShapeDtypeStruct((L, r, c), F32)] * 4,
                  )(core, w, m, v, *halves[0], *halves[1])


def _log_sigmoid(z):
    return jnp.minimum(z, 0.0) - jnp.log(1.0 + jnp.exp(-jnp.abs(z)))


def _dot3(x, m01):
    x1 = x.astype(BF16)
    r1 = x - x1.astype(F32)
    x2 = r1.astype(BF16)
    x3 = (r1 - x2.astype(F32)).astype(BF16)
    n = x.shape[0]
    if n % 16:
        return (jnp.dot(x1, m01, preferred_element_type=F32) + jnp.dot(x2, m01, preferred_element_type=F32)
                + jnp.dot(x3, m01, preferred_element_type=F32))
    y = jnp.dot(jnp.concatenate([x1, x2], axis=0), m01, preferred_element_type=F32)
    return y[:n] + y[n:]


def _dot_nt(a, b):
    return lax.dot_general(a, b, NT, preferred_element_type=F32)


def _dot_tn(a, b):
    return lax.dot_general(a, b, TN, preferred_element_type=F32)


def _iota2(shape):
    return lax.broadcasted_iota(jnp.int32, shape, 0), lax.broadcasted_iota(jnp.int32, shape, 1)


HEADS_PER_STEP = 4
HEADS = range(HEADS_PER_STEP)


CHUNK_HEADS_PER_STEP = 4
CHUNK_HEADS = range(CHUNK_HEADS_PER_STEP)


def _head_spec(T, rows=None, width=HEAD_DIM, heads=HEADS_PER_STEP):
    return pl.BlockSpec((heads, T if rows is None else rows, width), lambda g: (g, 0, 0))


def _sb_weights(qb, kb, t0, s0, racc, m_gt, row, col):
    z = _dot_nt(qb, kb) * SCALE
    strict = (s0 + col) < (t0 + row)
    lb = _log_sigmoid(z)
    lf = jnp.where(strict, lb - z, 0.0)
    between = _dot3(lf, m_gt) + racc
    w = jnp.where(strict, jnp.exp(lb + between), 0.0)
    return strict, lb, lf, w


def _sb_fwd(q, k, v, name, carry=None):
    H, T, _ = q.shape
    nb = T // QB

    def body(q_ref, k_ref, v_ref, o_ref):
        row, col = _iota2((QB, QB))
        m_gt = (row > col).astype(BF16)

        def qblock(i, _):
            t0 = pl.multiple_of(i * QB, QB)
            qbs = [q_ref[h, pl.ds(t0, QB), :].astype(BF16) for h in HEADS]

            def kblock(jj, carry):
                s0 = pl.multiple_of((i - jj) * QB, QB)
                out = []
                for h in HEADS:
                    acc, racc = carry[h]
                    kb = k_ref[h, pl.ds(s0, QB), :].astype(BF16)
                    vb = v_ref[h, pl.ds(s0, QB), :].astype(BF16)
                    _, _, lf, w = _sb_weights(qbs[h], kb, t0, s0, racc, m_gt, row, col)
                    acc = acc + jnp.dot(w.astype(BF16), vb, preferred_element_type=F32)
                    out.append((acc, racc + jnp.sum(lf, axis=1, keepdims=True)))
                return tuple(out)

            res = lax.fori_loop(0, i + 1, kblock,
                                tuple((jnp.zeros((QB, HEAD_DIM), F32), jnp.zeros((QB, 1), F32)) for _ in HEADS))
            for h in HEADS:
                o_ref[h, pl.ds(t0, QB), :] = res[h][0].astype(BF16)
            return 0

        lax.fori_loop(0, nb, qblock, 0)

    spec = _head_spec(T)
    return _pcall_carrying(body, carry, name=name, grid=(H // HEADS_PER_STEP,), in_specs=[spec] * 3, out_specs=spec,
                           out_shape=jax.ShapeDtypeStruct((H, T, HEAD_DIM), BF16))(q, k, v)


def _sb_bwd(q, k, v, do, name, carry=None):
    H, T, _ = q.shape
    nb = T // QB

    def body(q_ref, k_ref, v_ref, do_ref, dq_ref, dk_out, dv_out, racc_ref, dk_ref, dv_ref):
        row, col = _iota2((QB, QB))
        m_gt = (row > col).astype(BF16)
        m_lt = (row < col).astype(BF16)
        dk_ref[...] = jnp.zeros_like(dk_ref)
        dv_ref[...] = jnp.zeros_like(dv_ref)

        def qblock(i, _):
            t0 = pl.multiple_of(i * QB, QB)
            qbs = [q_ref[h, pl.ds(t0, QB), :].astype(BF16) for h in HEADS]
            dobs = [do_ref[h, pl.ds(t0, QB), :].astype(BF16) for h in HEADS]

            def suffix(jj, raccs):
                j = i - jj
                s0 = pl.multiple_of(j * QB, QB)
                out = []
                for h in HEADS:
                    kb = k_ref[h, pl.ds(s0, QB), :].astype(BF16)
                    z = _dot_nt(qbs[h], kb) * SCALE
                    lf = jnp.where((s0 + col) < (t0 + row), _log_sigmoid(z) - z, 0.0)
                    racc_ref[h, j] = raccs[h]
                    out.append(raccs[h] + jnp.sum(lf, axis=1, keepdims=True))
                return tuple(out)

            lax.fori_loop(0, i + 1, suffix, tuple(jnp.zeros((QB, 1), F32) for _ in HEADS))

            def kblock(j, carry):
                s0 = pl.multiple_of(j * QB, QB)
                out = []
                for h in HEADS:
                    dq, cacc = carry[h]
                    kb = k_ref[h, pl.ds(s0, QB), :].astype(BF16)
                    vb = v_ref[h, pl.ds(s0, QB), :].astype(BF16)
                    strict, lb, _, w = _sb_weights(qbs[h], kb, t0, s0, racc_ref[h, j], m_gt, row, col)
                    e = _dot_nt(dobs[h], vb) * w
                    c_left = _dot3(e, m_lt) + cacc
                    sig = jnp.exp(lb)
                    dz = jnp.where(strict, e * (1.0 - sig) - c_left * sig, 0.0).astype(BF16)
                    dq = dq + jnp.dot(dz, kb, preferred_element_type=F32)
                    dk_ref[h, pl.ds(s0, QB), :] += _dot_tn(dz, qbs[h]) * SCALE
                    dv_ref[h, pl.ds(s0, QB), :] += _dot_tn(w.astype(BF16), dobs[h])
                    out.append((dq, cacc + jnp.sum(e, axis=1, keepdims=True)))
                return tuple(out)

            res = lax.fori_loop(0, i + 1, kblock,
                                tuple((jnp.zeros((QB, HEAD_DIM), F32), jnp.zeros((QB, 1), F32)) for _ in HEADS))
            for h in HEADS:
                dq_ref[h, pl.ds(t0, QB), :] = (res[h][0] * SCALE).astype(BF16)
            return 0

        lax.fori_loop(0, nb, qblock, 0)
        dk_out[...] = dk_ref[...].astype(BF16)
        dv_out[...] = dv_ref[...].astype(BF16)

    spec = _head_spec(T)
    acc = pltpu.VMEM((HEADS_PER_STEP, T, HEAD_DIM), F32)
    return _pcall_carrying(body, carry, name=name, grid=(H // HEADS_PER_STEP,), in_specs=[spec] * 4,
                           out_specs=[spec] * 3, out_shape=[jax.ShapeDtypeStruct((H, T, HEAD_DIM), BF16)] * 3,
                           scratch_shapes=[pltpu.VMEM((HEADS_PER_STEP, nb, QB, 1), F32), acc, acc])(q, k, v, do)


def _fox_logits(qb, kb, fq, fk, t0, s0, row, col):
    z = _dot_nt(qb, kb) * SCALE + fq - fk
    return jnp.where((s0 + col) <= (t0 + row), z, NEG)


def _fox_specs(T):
    return _head_spec(T, width=1), _head_spec(T, rows=T // QB, width=QB)


def _fox_fwd(q, k, v, fq, fk, name, carry=None):
    H, T, _ = q.shape
    nb = T // QB

    def body(q_ref, k_ref, v_ref, fq_ref, fk_ref, o_ref, lse_ref):
        row, col = _iota2((QB, QB))

        def qblock(i, _):
            t0 = pl.multiple_of(i * QB, QB)
            qbs = [q_ref[h, pl.ds(t0, QB), :].astype(BF16) for h in HEADS]
            fqs = [fq_ref[h, pl.ds(t0, QB), :] for h in HEADS]

            def kblock(j, carry):
                s0 = pl.multiple_of(j * QB, QB)
                out = []
                for h in HEADS:
                    acc, m, l = carry[h]
                    kb = k_ref[h, pl.ds(s0, QB), :].astype(BF16)
                    vb = v_ref[h, pl.ds(s0, QB), :].astype(BF16)
                    z = _fox_logits(qbs[h], kb, fqs[h], fk_ref[h, pl.ds(j, 1), :], t0, s0, row, col)
                    m_new = jnp.maximum(m, jnp.max(z, axis=1, keepdims=True))
                    a = jnp.exp(m - m_new)
                    p = jnp.exp(z - m_new)
                    acc = a * acc + jnp.dot(p.astype(BF16), vb, preferred_element_type=F32)
                    out.append((acc, m_new, a * l + jnp.sum(p, axis=1, keepdims=True)))
                return tuple(out)

            res = lax.fori_loop(0, i + 1, kblock,
                                tuple((jnp.zeros((QB, HEAD_DIM), F32), jnp.full((QB, 1), NEG, F32),
                                       jnp.zeros((QB, 1), F32)) for _ in HEADS))
            for h in HEADS:
                acc, m, l = res[h]
                o_ref[h, pl.ds(t0, QB), :] = (acc / l).astype(BF16)
                lse_ref[h, pl.ds(t0, QB), :] = m + jnp.log(l)
            return 0

        lax.fori_loop(0, nb, qblock, 0)

    spec = _head_spec(T)
    fq_spec, fk_spec = _fox_specs(T)
    return _pcall_carrying(
        body, carry, name=name, grid=(H // HEADS_PER_STEP,), in_specs=[spec] * 3 + [fq_spec, fk_spec],
        out_specs=[spec, fq_spec],
        out_shape=[jax.ShapeDtypeStruct((H, T, HEAD_DIM), BF16), jax.ShapeDtypeStruct((H, T, 1), F32)],
    )(q, k, v, fq, fk)


def _fox_bwd(q, k, v, fq, fk, lse, do, name, carry=None):
    H, T, _ = q.shape
    nb = T // QB

    def body(q_ref, k_ref, v_ref, fq_ref, fk_ref, lse_ref, do_ref, dq_ref, dk_out, dv_out, dfk_ref, dk_ref, dv_ref):
        row, col = _iota2((QB, QB))
        dk_ref[...] = jnp.zeros_like(dk_ref)
        dv_ref[...] = jnp.zeros_like(dv_ref)
        dfk_ref[...] = jnp.zeros_like(dfk_ref)

        def qblock(i, _):
            t0 = pl.multiple_of(i * QB, QB)
            qbs = [q_ref[h, pl.ds(t0, QB), :].astype(BF16) for h in HEADS]
            fqs = [fq_ref[h, pl.ds(t0, QB), :] for h in HEADS]
            lses = [lse_ref[h, pl.ds(t0, QB), :] for h in HEADS]
            dobs = [do_ref[h, pl.ds(t0, QB), :].astype(BF16) for h in HEADS]

            def probs(h, j):
                s0 = pl.multiple_of(j * QB, QB)
                kb = k_ref[h, pl.ds(s0, QB), :].astype(BF16)
                vb = v_ref[h, pl.ds(s0, QB), :].astype(BF16)
                z = _fox_logits(qbs[h], kb, fqs[h], fk_ref[h, pl.ds(j, 1), :], t0, s0, row, col)
                return s0, kb, jnp.exp(z - lses[h]), _dot_nt(dobs[h], vb)

            def row_dot(j, deltas):
                out = []
                for h in HEADS:
                    _, _, p, dp = probs(h, j)
                    out.append(deltas[h] + jnp.sum(p * dp, axis=1, keepdims=True))
                return tuple(out)

            deltas = lax.fori_loop(0, i + 1, row_dot, tuple(jnp.zeros((QB, 1), F32) for _ in HEADS))

            def kblock(j, dqs):
                out = []
                for h in HEADS:
                    s0, kb, p, dp = probs(h, j)
                    dz = p * (dp - deltas[h])
                    dzb = dz.astype(BF16)
                    dk_ref[h, pl.ds(s0, QB), :] += _dot_tn(dzb, qbs[h]) * SCALE
                    dv_ref[h, pl.ds(s0, QB), :] += _dot_tn(p.astype(BF16), dobs[h])
                    dfk_ref[h, pl.ds(j, 1), :] += -jnp.sum(dz, axis=0, keepdims=True)
                    out.append(dqs[h] + jnp.dot(dzb, kb, preferred_element_type=F32))
                return tuple(out)

            dqs = lax.fori_loop(0, i + 1, kblock, tuple(jnp.zeros((QB, HEAD_DIM), F32) for _ in HEADS))
            for h in HEADS:
                dq_ref[h, pl.ds(t0, QB), :] = (dqs[h] * SCALE).astype(BF16)
            return 0

        lax.fori_loop(0, nb, qblock, 0)
        dk_out[...] = dk_ref[...].astype(BF16)
        dv_out[...] = dv_ref[...].astype(BF16)

    spec = _head_spec(T)
    fq_spec, fk_spec = _fox_specs(T)
    acc = pltpu.VMEM((HEADS_PER_STEP, T, HEAD_DIM), F32)
    return _pcall_carrying(body, carry, name=name, grid=(H // HEADS_PER_STEP,),
                           in_specs=[spec] * 3 + [fq_spec, fk_spec, fq_spec, spec],
                           out_specs=[spec, spec, spec, fk_spec],
                           out_shape=[jax.ShapeDtypeStruct((H, T, HEAD_DIM), BF16)] * 3
                           + [jax.ShapeDtypeStruct((H, nb, QB), F32)],
                           scratch_shapes=[acc, acc])(q, k, v, fq, fk, lse, do)


def _forget_cumsum(flog, name):
    R, T = flog.shape
    nb = T // QB

    def body(x_ref, o_ref):
        row, col = _iota2((QB, QB))
        m_le = (row <= col).astype(BF16)
        carry = jnp.zeros((R, 1), F32)
        for b in range(nb):
            lf = _log_sigmoid(x_ref[:, b * QB:(b + 1) * QB])
            o_ref[:, b * QB:(b + 1) * QB] = _dot3(lf, m_le) + carry
            carry = carry + jnp.sum(lf, axis=1, keepdims=True)

    spec = pl.BlockSpec((R, T), lambda: (0, 0))
    return _pcall(body, name=name, in_specs=[spec], out_specs=spec,
                  out_shape=jax.ShapeDtypeStruct((R, T), F32))(flog)


def _forget_cumsum_bwd(flog, df, name):
    R, T = flog.shape
    nb = T // QB

    def body(x_ref, df_ref, o_ref):
        row, col = _iota2((QB, QB))
        m_ge = (row >= col).astype(BF16)
        carry = jnp.zeros((R, 1), F32)
        for b in reversed(range(nb)):
            dfb = df_ref[:, b * QB:(b + 1) * QB]
            dlog = _dot3(dfb, m_ge) + carry
            o_ref[:, b * QB:(b + 1) * QB] = dlog * _sigmoid(-x_ref[:, b * QB:(b + 1) * QB])
            carry = carry + jnp.sum(dfb, axis=1, keepdims=True)

    spec = pl.BlockSpec((R, T), lambda: (0, 0))
    return _pcall(body, name=name, in_specs=[spec, spec], out_specs=spec,
                  out_shape=jax.ShapeDtypeStruct((R, T), F32))(flog, df)


def _chunk_probs(qc, kb, bias, c, col):
    z = _dot_nt(qc, kb) * SCALE + bias
    z = jnp.where((c - (LEFT_CHUNKS + 1)) * CHUNK + col >= jnp.maximum(0, (c - LEFT_CHUNKS) * CHUNK), z, NEG)
    p = jnp.exp(z - jnp.max(z, axis=1, keepdims=True))
    return p, jnp.sum(p, axis=1, keepdims=True)


def _chunk_specs(T, n=CHUNK_HEADS_PER_STEP):
    return (_head_spec(T, heads=n), _head_spec(T, rows=T + BAND_PAD, heads=n),
            _head_spec(T, rows=CHUNK, width=BAND, heads=n))


def _chunk_fwd(q, kp, vp, bias, name, carry=None):
    H, T, _ = q.shape
    nc = T // CHUNK

    def body(q_ref, kp_ref, vp_ref, bias_ref, o_ref):
        _, col = _iota2((CHUNK, BAND))

        def chunk(c, _):
            t0 = pl.multiple_of(c * CHUNK, CHUNK)
            for h in HEADS:
                qc = q_ref[h, pl.ds(t0, CHUNK), :].astype(BF16)
                kb = kp_ref[h, pl.ds(t0, BAND), :].astype(BF16)
                vb = vp_ref[h, pl.ds(t0, BAND), :].astype(BF16)
                p, l = _chunk_probs(qc, kb, bias_ref[h], c, col)
                o = jnp.dot(p.astype(BF16), vb, preferred_element_type=F32) / l
                o_ref[h, pl.ds(t0, CHUNK), :] = o.astype(BF16)
            return 0

        lax.fori_loop(0, nc, chunk, 0)

    q_spec, kp_spec, b_spec = _chunk_specs(T, HEADS_PER_STEP)
    return _pcall_carrying(body, carry, name=name, grid=(H // HEADS_PER_STEP,),
                           in_specs=[q_spec, kp_spec, kp_spec, b_spec], out_specs=q_spec,
                           out_shape=jax.ShapeDtypeStruct((H, T, HEAD_DIM), BF16))(q, kp, vp, bias)


def _chunk_bwd(q, kp, vp, bias, do, name, carry=None):
    H, T, _ = q.shape
    nc = T // CHUNK

    def body(q_ref, kp_ref, vp_ref, bias_ref, do_ref, dq_ref, dk_out, dv_out, db_ref, dkp_ref, dvp_ref):
        _, col = _iota2((CHUNK, BAND))
        dkp_ref[...] = jnp.zeros_like(dkp_ref)
        dvp_ref[...] = jnp.zeros_like(dvp_ref)
        db_ref[...] = jnp.zeros_like(db_ref)

        def chunk(c, _):
            t0 = pl.multiple_of(c * CHUNK, CHUNK)
            for h in CHUNK_HEADS:
                qc = q_ref[h, pl.ds(t0, CHUNK), :].astype(BF16)
                kb = kp_ref[h, pl.ds(t0, BAND), :].astype(BF16)
                vb = vp_ref[h, pl.ds(t0, BAND), :].astype(BF16)
                dob = do_ref[h, pl.ds(t0, CHUNK), :].astype(BF16)
                p, l = _chunk_probs(qc, kb, bias_ref[h], c, col)
                p = p / l
                dp = _dot_nt(dob, vb)
                dz = p * (dp - jnp.sum(p * dp, axis=1, keepdims=True))
                dzb = dz.astype(BF16)
                dq = jnp.dot(dzb, kb, preferred_element_type=F32) * SCALE
                dq_ref[h, pl.ds(t0, CHUNK), :] = dq.astype(BF16)
                dkp_ref[h, pl.ds(t0, BAND), :] += _dot_tn(dzb, qc) * SCALE
                dvp_ref[h, pl.ds(t0, BAND), :] += _dot_tn(p.astype(BF16), dob)
                db_ref[h] += dz
            return 0

        lax.fori_loop(0, nc, chunk, 0)
        dk_out[...] = dkp_ref[:, BAND_PAD:, :].astype(BF16)
        dv_out[...] = dvp_ref[:, BAND_PAD:, :].astype(BF16)

    q_spec, kp_spec, b_spec = _chunk_specs(T)
    acc = pltpu.VMEM((CHUNK_HEADS_PER_STEP, T + BAND_PAD, HEAD_DIM), F32)
    return _pcall_carrying(body, carry, name=name, grid=(H // CHUNK_HEADS_PER_STEP,),
                           in_specs=[q_spec, kp_spec, kp_spec, b_spec, q_spec],
                           out_specs=[q_spec, q_spec, q_spec, b_spec],
                           out_shape=[jax.ShapeDtypeStruct((H, T, HEAD_DIM), BF16)] * 3
                           + [jax.ShapeDtypeStruct((H, CHUNK, BAND), F32)],
                           scratch_shapes=[acc, acc])(q, kp, vp, bias, do)


HBM_SPEC = pl.BlockSpec(memory_space=pltpu.HBM)


def _position():
    return lax.axis_index("x"), lax.axis_index("y"), lax.axis_index("c")


def _other_chips(x, y):
    chips = [(1 - x, y), (x, 1 - y), (1 - x, 1 - y)]
    return [(chip, 2 * chip[0] + chip[1]) for chip in chips]


def _remote_copy(src, dst, send_sems, recv_sems, k, to):
    return pltpu.make_async_remote_copy(src_ref=src, dst_ref=dst, send_sem=send_sems.at[k], recv_sem=recv_sems.at[k],
                                        device_id=to, device_id_type=MESH)


def _place_own(w, chip, name):
    _, r, c = w.shape
    tr = _pick(r, 256, 16)

    def body(chip_ref, w_ref, *out_refs):
        for l, o_ref in enumerate(out_refs):
            o_ref[...] = w_ref[l].astype(BF16)

    grid_spec = pltpu.PrefetchScalarGridSpec(
        num_scalar_prefetch=1, grid=(r // tr,), in_specs=[pl.BlockSpec((DEPTH, tr, c), lambda i, ch: (0, i, 0))],
        out_specs=[pl.BlockSpec((None, tr, c), lambda i, ch: (ch[0], i, 0))] * DEPTH)
    return _pcall(body, name=name, grid_spec=grid_spec,
                  out_shape=[jax.ShapeDtypeStruct((N_CHIPS, r, c), BF16)] * DEPTH)(chip, w)


def _gather_over_ici(srcs, bufs, new):
    x, y, c = _position()
    me = 2 * x + y
    return [(b.at[me, c], b.at[me, c], (*chip, c)) for b in bufs for chip, _ in _other_chips(x, y)]


def _gather_to_sibling(srcs, bufs, new):
    x, y, c = _position()
    return [(b.at[idx, c], b.at[idx, c], (x, y, 1 - c)) for b in bufs for _, idx in _other_chips(x, y)]


def _gather_layer(bufs, name):
    n = len(bufs)

    def body(*refs):
        dst = refs[n:2 * n]
        send_sems, recv_sems = refs[2 * n:]
        x, y, c = _position()
        me = 2 * x + y
        sibling = (x, y, 1 - c)
        others = _other_chips(x, y)
        first = [_remote_copy(dst[i].at[me, c], dst[i].at[me, c], send_sems, recv_sems, 3 * i + k, (*chip, c))
                 for i in range(n) for k, (chip, _) in enumerate(others)]
        for cp in first:
            cp.start()
        passed = []
        for i in range(n):
            for k, (_, idx) in enumerate(others):
                landed = dst[i].at[idx, c]
                _remote_copy(landed, landed, send_sems, recv_sems, 3 * i + k, sibling).wait_recv()
                passed.append(_remote_copy(landed, landed, send_sems, recv_sems, 3 * (n + i) + k, sibling))
                passed[-1].start()
        for i in range(n):
            for k, (_, idx) in enumerate(others):
                from_sibling = dst[i].at[idx, 1 - c]
                _remote_copy(from_sibling, from_sibling, send_sems, recv_sems, 3 * (n + i) + k, sibling).wait_recv()
        for cp in first + passed:
            cp.wait_send()

    return _pcall(
        body, name=name, in_specs=[HBM_SPEC] * n, out_specs=[HBM_SPEC] * n,
        out_shape=[jax.ShapeDtypeStruct(b.shape, b.dtype) for b in bufs],
        scratch_shapes=[pltpu.SemaphoreType.DMA((6 * n,)), pltpu.SemaphoreType.DMA((6 * n,))],
        input_output_aliases={i: i for i in range(n)},
    )(*bufs)


def _send_other_half(parts, name):
    n = len(parts)

    def body(*refs):
        src, got = refs[:n], refs[n:2 * n]
        send_sems, recv_sems = refs[2 * n:]
        x, y, c = _position()
        copies = [_remote_copy(src[i].at[1 - c], got[i], send_sems, recv_sems, i, (x, y, 1 - c)) for i in range(n)]
        for cp in copies:
            cp.start()
        for cp in copies:
            cp.wait()

    return _pcall(
        body, name=name, in_specs=[HBM_SPEC] * n, out_specs=[HBM_SPEC] * n,
        out_shape=[jax.ShapeDtypeStruct(p.shape[1:], p.dtype) for p in parts],
        scratch_shapes=[pltpu.SemaphoreType.DMA((n,)), pltpu.SemaphoreType.DMA((n,))],
    )(*parts)


def _scatter_chips(parts, name):
    n = len(parts)

    def body(*refs):
        src, dst = refs[:n], refs[n:2 * n]
        send_sems, recv_sems = refs[2 * n:]
        x, y, c = _position()
        others = _other_chips(x, y)
        sends = [_remote_copy(src[i].at[idx], dst[i].at[k], send_sems, recv_sems, 3 * i + k, (*chip, c))
                 for i in range(n) for k, (chip, idx) in enumerate(others)]
        for cp in sends:
            cp.start()
        for cp in sends:
            cp.wait()

    return _pcall(
        body, name=name, in_specs=[HBM_SPEC] * n, out_specs=[HBM_SPEC] * n,
        out_shape=[jax.ShapeDtypeStruct((3,) + p.shape[1:], p.dtype) for p in parts],
        scratch_shapes=[pltpu.SemaphoreType.DMA((3 * n,)), pltpu.SemaphoreType.DMA((3 * n,))],
    )(*parts)


def _swap_with_sibling(arrs, name):
    n = len(arrs)

    def body(*refs):
        src, dst = refs[:n], refs[n:2 * n]
        send_sems, recv_sems = refs[2 * n:]
        x, y, c = _position()
        copies = [_remote_copy(src[i], dst[i], send_sems, recv_sems, i, (x, y, 1 - c)) for i in range(n)]
        for cp in copies:
            cp.start()
        for cp in copies:
            cp.wait()

    return _pcall(
        body, name=name, in_specs=[HBM_SPEC] * n, out_specs=[HBM_SPEC] * n,
        out_shape=[jax.ShapeDtypeStruct(a.shape, a.dtype) for a in arrs],
        scratch_shapes=[pltpu.SemaphoreType.DMA((n,)), pltpu.SemaphoreType.DMA((n,))],
    )(*arrs)


def _allreduce_small(v, name):
    R, C = v.shape

    def body(v_ref, o_ref, slots, send_sems, recv_sems):
        x, y, c = _position()
        me = 4 * x + 2 * y + c
        slots[0] = v_ref[...]
        sends = []
        for r in range(1, 8):
            fx, fy, fc = (r >> 2) & 1, (r >> 1) & 1, r & 1
            to = (x ^ fx, y ^ fy, c ^ fc)
            cp = pltpu.make_async_remote_copy(src_ref=v_ref, dst_ref=slots.at[r], send_sem=send_sems.at[r - 1],
                                              recv_sem=recv_sems.at[r - 1], device_id=to, device_id_type=MESH)
            cp.start()
            sends.append(cp)
        for cp in sends:
            cp.wait()
        acc = slots[me]
        for d in range(1, 8):
            acc = acc + slots[d ^ me]
        o_ref[...] = acc

    vmem = pl.BlockSpec(memory_space=pltpu.VMEM)
    return _pcall(
        body, name=name, in_specs=[vmem], out_specs=vmem, out_shape=jax.ShapeDtypeStruct((R, C), F32),
        scratch_shapes=[pltpu.VMEM((8, R, C), F32), pltpu.SemaphoreType.DMA((7,)), pltpu.SemaphoreType.DMA((7,))],
    )(v)


def _add_pair(which, both, got, name):
    _, N, R, C = both.shape
    tr = _pick(R, 512, 16)

    def body(which_ref, a_ref, b_ref, o_ref):
        o_ref[...] = (a_ref[...].astype(F32) + b_ref[...].astype(F32)).astype(BF16)

    spec = pl.BlockSpec((None, tr, C), lambda n, i, w: (n, i, 0))
    grid_spec = pltpu.PrefetchScalarGridSpec(
        num_scalar_prefetch=1, grid=(N, R // tr),
        in_specs=[pl.BlockSpec((None, None, tr, C), lambda n, i, w: (w[0], n, i, 0)), spec], out_specs=spec)
    return _pcall(body, name=name, grid_spec=grid_spec,
                  out_shape=jax.ShapeDtypeStruct((N, R, C), BF16))(which, both, got)


def _sum_chips(which, own, others, name):
    N, R, C = others.shape
    tr = _pick(R, 512, 16)

    def body(which_ref, own_ref, p_ref, o_ref):
        acc = own_ref[...].astype(F32)
        for n in range(N):
            acc = acc + p_ref[n].astype(F32)
        o_ref[...] = acc

    grid_spec = pltpu.PrefetchScalarGridSpec(
        num_scalar_prefetch=1, grid=(R // tr,),
        in_specs=[pl.BlockSpec((None, tr, C), lambda i, w: (w[0], i, 0)), pl.BlockSpec((N, tr, C), lambda i, w: (0, i, 0))],
        out_specs=pl.BlockSpec((tr, C), lambda i, w: (i, 0)))
    return _pcall(body, name=name, grid_spec=grid_spec,
                  out_shape=jax.ShapeDtypeStruct((R, C), F32))(which, own, others)


BIG = ("w_ffn1_in", "w_ffn1_out", "w_in", "w_br_sb", "w_br_ch", "w_br_fox", "w_out", "w_ffn2_in", "w_ffn2_out")
ROW_SHARDED = ("w_ffn1_out", "w_out", "w_ffn2_out")
SMALL = ("g_ffn1", "g_mix", "b_in", "rel_bias", "g_ffn2", "g_final")


def _pair_sums(split, tag):
    core = lax.axis_index("c").astype(jnp.int32)[None]
    got = _send_other_half(split, f"grad_split_{tag}")
    return [_add_pair(core, a, b, f"grad_pair_sum_{tag}_{i}") for i, (a, b) in enumerate(zip(split, got))]


def _scatter_plan(srcs, bufs, new):
    x, y, c = _position()
    return [(s.at[idx], d.at[k], (*chip, c)) for s, d in zip(srcs, new) for k, (chip, idx) in enumerate(_other_chips(x, y))]


def _scatter_carry(pair):
    landing = [jax.ShapeDtypeStruct((3,) + p.shape[1:], p.dtype) for p in pair]
    return _Carry(pair, [], landing, 3 * len(pair), _scatter_plan)


def _finish_grads(pair, landed, tag):
    chip = (2 * lax.axis_index("x") + lax.axis_index("y")).astype(jnp.int32)[None]
    mine = [_sum_chips(chip, p, q, f"grad_chip_sum_{tag}_{i}") for i, (p, q) in enumerate(zip(pair, landed))]
    return mine, _swap_with_sibling(mine, f"grad_share_{tag}")


SMALL_SIZES = {"g_ffn1": DEPTH * D_MODEL, "g_mix": DEPTH * D_MODEL, "b_in": DEPTH * IN_WIDTH,
               "rel_bias": DEPTH * N_REL * H_CH, "g_ffn2": DEPTH * D_MODEL, "g_final": D_MODEL}
SMALL_TOTAL = sum(SMALL_SIZES.values()) + 1
SMALL_ROWS = -(-SMALL_TOTAL // (8 * 128)) * 8


def _pack_small(vals, extra):
    flat = [vals[n].reshape(-1) for n in SMALL] + [extra.reshape(-1)]
    flat.append(jnp.zeros((SMALL_ROWS * 128 - SMALL_TOTAL,), F32))
    return jnp.concatenate(flat).reshape(SMALL_ROWS, 128)


def _unpack_small(pack, shapes):
    flat, out, off = pack.reshape(-1), {}, 0
    for n in SMALL:
        out[n] = flat[off:off + SMALL_SIZES[n]].reshape(shapes[n])
        off += SMALL_SIZES[n]
    return out, flat[off]


def _to_heads(t, n_heads):
    return jnp.transpose(t.reshape(t.shape[0], n_heads, HEAD_DIM), (1, 0, 2)).astype(BF16)


def _from_heads(t):
    return jnp.transpose(t, (1, 0, 2)).reshape(t.shape[1], t.shape[0] * HEAD_DIM)


def _pad_keys(t):
    return jnp.pad(t, ((0, 0), (BAND_PAD, 0), (0, 0)))


DIAGONALS = CHUNK + BAND - 1


def _band_bias(table):
    n_far = (LEFT_CHUNKS + 1) * CHUNK + CHUNK - MAX_REL
    near = table[N_REL - 1 - (DIAGONALS - n_far):N_REL - 1][::-1]
    diag = jnp.concatenate([jnp.broadcast_to(table[N_REL - 1], (n_far, H_CH)), near], axis=0).T
    diag = jnp.pad(diag, ((0, 0), (0, 1)))
    skew = jnp.tile(diag, (1, CHUNK))[:, :CHUNK * DIAGONALS].reshape(H_CH, CHUNK, DIAGONALS)
    return skew[:, :, CHUNK - 1:]


def _pad_w_in(w):
    qkv, f, gates = w[:, :QKV_WIDTH], w[:, QKV_WIDTH:QKV_WIDTH + H_FOX], w[:, QKV_WIDTH + H_FOX:]
    return jnp.concatenate([qkv, gates, f, jnp.zeros((w.shape[0], F_PAD - H_FOX), w.dtype)], axis=1)


def _unpad_w_in(w):
    return jnp.concatenate([w[:, :QKV_WIDTH], w[:, QKV_WIDTH + GATE_WIDTH:QKV_WIDTH + GATE_WIDTH + H_FOX],
                            w[:, QKV_WIDTH:QKV_WIDTH + GATE_WIDTH]], axis=1)


QKV_SPLITS = np.cumsum([0, W_SB, W_SB, W_SB, W_CH, W_CH, W_CH, W_FOX, W_FOX, W_FOX])


def _by_chip_rows(g):
    return g.reshape(2, N_CHIPS, g.shape[1] // N_CHIPS, g.shape[2])


def _ffn_fwd(x, g, w_in, w_out, tag, carries):
    h, gate, up, a = _ffn_in_swiglu(x, g, w_in, f"{tag}_in", _carry_of(carries, "in"))
    y = _mm(a, w_out, mode="nn", name=f"{tag}_out", alpha=0.5, res=x, gathered="row",
            carry=_carry_of(carries, "out"))
    return y, (h, gate, up, a)


def _ffn_bwd(x, g, w_in, w_out, saved, dy, tag, carries):
    h, gate, up, a = saved
    da = _mm(dy, w_out, mode="nt", name=f"{tag}_da", alpha=0.5, gathered="row", out_dtype=BF16,
             carry=_carry_of(carries, "da"))
    dw_out = _mm(a, dy, mode="tn", name=f"{tag}_dwout", alpha=0.5, tm_cap=1408, out_dtype=BF16, out_split="row",
                 carry=_carry_of(carries, "dwout"))
    dgu = _swiglu_bwd(gate, up, da, f"{tag}_dact")
    dw_in = _mm(h, dgu, mode="tn", name=f"{tag}_dwin", tm_cap=512, out_dtype=BF16, out_split="col",
                carry=_carry_of(carries, "dwin"))
    dx, dg = _mm(dgu, w_in, mode="nt", name=f"{tag}_dh", gathered="col", tn_cap=1024, carry=_carry_of(carries, "dh"),
                 norm_bwd=(x, g, dy))
    return dx, dg, dw_in, _by_chip_rows(dw_out)


def _mixer_fwd(x, lw, tag, carries):
    T = x.shape[0]
    h = _rmsnorm_fwd(x, lw["g_mix"], f"{tag}_norm")
    p = _mm(h, lw["w_in"], mode="nn", name=f"{tag}_proj", bias=lw["b_in"], tn_cap=896, out_dtype=BF16,
            carry=_carry_of(carries, "proj"))
    parts = [p[:, QKV_SPLITS[i]:QKV_SPLITS[i + 1]] for i in range(9)]
    qa, ka, va = (_to_heads(t, H_SB) for t in parts[0:3])
    qb, kb, vb = (_to_heads(t, H_CH) for t in parts[3:6])
    qc, kc, vc = (_to_heads(t, H_FOX) for t in parts[6:9])
    flog = _mm(lw["w_forget_t"], h, mode="nt", name=f"{tag}_flog")[:8] + lw["b_forget"]
    fcum = _forget_cumsum(flog, f"{tag}_fcum")[:H_FOX]
    fq, fk = fcum.reshape(H_FOX, T, 1), fcum.reshape(H_FOX, T // QB, QB)
    kbp, vbp = _pad_keys(kb), _pad_keys(vb)
    oa = _sb_fwd(qa, ka, va, f"{tag}_sb", _carry_of(carries, "sb"))
    ob = _chunk_fwd(qb, kbp, vbp, lw["bias"], f"{tag}_ch", _carry_of(carries, "ch"))
    oc, lse = _fox_fwd(qc, kc, vc, fq, fk, f"{tag}_fox", _carry_of(carries, "fox"))
    oam, obm, ocm = _from_heads(oa), _from_heads(ob), _from_heads(oc)
    ya, yb, yc, merged = _branches_fwd(p, (oam, obm, ocm), (lw["w_br_sb"], lw["w_br_ch"], lw["w_br_fox"]),
                                       f"{tag}_branches")
    y = _mm(merged, lw["w_out"], mode="nn", name=f"{tag}_out", res=x, gathered="row")
    saved = (h, p, (qa, ka, va), (qb, kbp, vbp), (qc, kc, vc), flog, fq, fk, lse, (oam, obm, ocm),
             (ya, yb, yc), merged)
    return y, saved


def _mixer_bwd(x, lw, saved, dy, tag, carries):
    (h, p, (qa, ka, va), (qb, kbp, vbp), (qc, kc, vc), flog, fq, fk, lse, (oam, obm, ocm),
     (ya, yb, yc), merged) = saved
    T = x.shape[0]
    grads = {}
    dmerged = _mm(dy, lw["w_out"], mode="nt", name=f"{tag}_dmerged", gathered="row",
                  carry=_carry_of(carries, "dmerged"))
    grads["w_out"] = _by_chip_rows(_mm(merged, dy, mode="tn", name=f"{tag}_dwout", tm_cap=1024, out_dtype=BF16,
                                       out_split="row"))
    (dya, dyb, dyc), dgates, (doa, dob, doc) = _branches_bwd(
        p, (ya, yb, yc), dmerged, (lw["w_br_sb"], lw["w_br_ch"], lw["w_br_fox"]), f"{tag}_dbranches")
    grads["w_br_sb"], grads["w_br_ch"], grads["w_br_fox"] = _branches_dw(
        (oam, obm, ocm), (dya, dyb, dyc), f"{tag}_dwbranches")
    dqa, dka, dva = _sb_bwd(qa, ka, va, _to_heads(doa, H_SB), f"{tag}_dsb", _carry_of(carries, "dsb"))
    dqb, dkb, dvb, dbias = _chunk_bwd(qb, kbp, vbp, lw["bias"], _to_heads(dob, H_CH), f"{tag}_dch",
                                      _carry_of(carries, "dch"))
    dqc, dkc, dvc, dfk = _fox_bwd(qc, kc, vc, fq, fk, lse, _to_heads(doc, H_FOX), f"{tag}_dfox",
                                  _carry_of(carries, "dfox"))
    dflog = _forget_cumsum_bwd(flog, jnp.pad(dfk.reshape(H_FOX, T), ((0, 8 - H_FOX), (0, 0))), f"{tag}_dfcum")
    dqkv = [_from_heads(t) for t in (dqa, dka, dva, dqb, dkb, dvb, dqc, dkc, dvc)]
    dforget = jnp.pad(dflog[:H_FOX].T, ((0, 0), (0, F_PAD - H_FOX))).astype(BF16)
    dpb = jnp.concatenate(dqkv + list(dgates) + [dforget], axis=1)
    grads["b_in"] = _colsum(dpb, f"{tag}_dbin")
    dw_in =_unpad_w_in(_mm(h, dpb, mode="tn", name=f"{tag}_dwin", tm_cap=512, tn_cap=896, out_dtype=BF16))
    grads["w_in"] = jnp.transpose(dw_in.reshape(2, D_MODEL // 2, N_CHIPS, IN_WIDTH // N_CHIPS), (0, 2, 1, 3))
    dx, grads["g_mix"] = _mm(dpb, lw["w_in"], mode="nt", name=f"{tag}_dh", tk_cap=896, tn_cap=1024,
                             norm_bwd=(x, lw["g_mix"], dy))
    grads["rel_bias"] = lw["bias_vjp"](dbias)[0]
    return dx, grads


def kernel(x, g_ffn1, w_ffn1_in, w_ffn1_out, g_mix, w_in, b_in, rel_bias, w_br_sb, w_br_ch, w_br_fox, w_out, g_ffn2, w_ffn2_in, w_ffn2_out, g_final, loss_target, m_g_ffn1, m_w_ffn1_in, m_w_ffn1_out, m_g_mix, m_w_in, m_b_in, m_rel_bias, m_w_br_sb, m_w_br_ch, m_w_br_fox, m_w_out, m_g_ffn2, m_w_ffn2_in, m_w_ffn2_out, m_g_final, v_g_ffn1, v_w_ffn1_in, v_w_ffn1_out, v_g_mix, v_w_in, v_b_in, v_rel_bias, v_w_br_sb, v_w_br_ch, v_w_br_fox, v_w_out, v_g_ffn2, v_w_ffn2_in, v_w_ffn2_out, v_g_final):
    names = ("g_ffn1", "w_ffn1_in", "w_ffn1_out", "g_mix", "w_in", "b_in", "rel_bias", "w_br_sb", "w_br_ch",
             "w_br_fox", "w_out", "g_ffn2", "w_ffn2_in", "w_ffn2_out", "g_final")
    w = dict(zip(names, (g_ffn1, w_ffn1_in, w_ffn1_out, g_mix, w_in, b_in, rel_bias, w_br_sb, w_br_ch,
                         w_br_fox, w_out, g_ffn2, w_ffn2_in, w_ffn2_out, g_final)))
    m = dict(zip(names, (m_g_ffn1, m_w_ffn1_in, m_w_ffn1_out, m_g_mix, m_w_in, m_b_in, m_rel_bias, m_w_br_sb,
                         m_w_br_ch, m_w_br_fox, m_w_out, m_g_ffn2, m_w_ffn2_in, m_w_ffn2_out, m_g_final)))
    v = dict(zip(names, (v_g_ffn1, v_w_ffn1_in, v_w_ffn1_out, v_g_mix, v_w_in, v_b_in, v_rel_bias, v_w_br_sb,
                         v_w_br_ch, v_w_br_fox, v_w_out, v_g_ffn2, v_w_ffn2_in, v_w_ffn2_out, v_g_final)))
    xs = x[0]
    tgt = loss_target[0]

    chip = (2 * lax.axis_index("x") + lax.axis_index("y")).astype(jnp.int32)[None]
    placed = [_place_own(w[n], chip, f"place_{n}") for n in BIG]
    bufs = [[p[l].reshape(N_CHIPS, 2, p[l].shape[1] // 2, p[l].shape[2]) for p in placed] for l in range(DEPTH)]
    padded_w_in = {}

    def layer_weights(l):
        lw = {n: b.reshape((N_CHIPS,) + w[n].shape[1:]) for n, b in zip(BIG, bufs[l])}
        if l not in padded_w_in:
            whole = jnp.concatenate([lw["w_in"][j] for j in range(N_CHIPS)], axis=1)
            forget_t = jnp.pad(whole[:, QKV_WIDTH:QKV_WIDTH + H_FOX].T, ((0, F_PAD - H_FOX), (0, 0)))
            padded_w_in[l] = (_pad_w_in(whole), forget_t)
        lw["w_in"], lw["w_forget_t"] = padded_w_in[l]
        lw["b_forget"] = jnp.pad(w["b_in"][l][QKV_WIDTH:QKV_WIDTH + H_FOX], (0, 8 - H_FOX))[:, None]
        lw["b_in"] = _pad_w_in(w["b_in"][l][None, :])
        lw["bias"], lw["bias_vjp"] = jax.vjp(_band_bias, w["rel_bias"][l])
        for n in ("g_ffn1", "g_mix", "g_ffn2"):
            lw[n] = w[n][l][None, :]
        return lw

    def landed_in(l, idx):
        def done(carry):
            for i, b in zip(idx, carry.out[0]):
                bufs[l][i] = b
        return done

    def over_ici(l, idx):
        return lambda: _Carry([], [bufs[l][i] for i in idx], [], 3 * len(idx), _gather_over_ici, landed_in(l, idx))

    def to_sibling(l, idx):
        return lambda: _Carry([], [bufs[l][i] for i in idx], [], 3 * len(idx), _gather_to_sibling, landed_in(l, idx))

    def ffn_fwd(x_in, lw, which, tag, carries):
        return _ffn_fwd(x_in, lw[f"g_{which}"], lw[f"w_{which}_in"], lw[f"w_{which}_out"], tag, carries)

    def ffn_bwd(x_in, lw, which, saved_acts, dy, tag, carries):
        return _ffn_bwd(x_in, lw[f"g_{which}"], lw[f"w_{which}_in"], lw[f"w_{which}_out"], saved_acts, dy, tag,
                        carries)

    FFN1, W_IN, MIXER_REST, FFN2_IN, FFN2_OUT = (0, 1), (2,), (3, 4, 5, 6), (7,), (8,)
    first = FFN1 + W_IN
    for i, b in zip(first, _gather_layer([bufs[0][i] for i in first], "gather_l0")):
        bufs[0][i] = b
    fwd_carries = [
        {"ffn1": {"in": [over_ici(0, MIXER_REST)], "out": [to_sibling(0, MIXER_REST), over_ici(0, FFN2_OUT)]},
         "mix": {"proj": [to_sibling(0, FFN2_OUT), over_ici(1, MIXER_REST)],
                 "sb": [over_ici(0, FFN2_IN), over_ici(1, FFN2_OUT)],
                 "ch": [to_sibling(0, FFN2_IN), over_ici(1, FFN1)],
                 "fox": [over_ici(1, W_IN)]},
         "ffn2": {"in": [to_sibling(1, MIXER_REST + FFN2_OUT + FFN1 + W_IN)]}},
        {"ffn1": {}, "mix": {"sb": [over_ici(1, FFN2_IN)], "ch": [to_sibling(1, FFN2_IN)]}, "ffn2": {}},
    ]

    saved = []
    act = xs
    for l in range(DEPTH):
        x0 = act
        x1, s1 = ffn_fwd(x0, layer_weights(l), "ffn1", f"l{l}_ffn1", fwd_carries[l]["ffn1"])
        x2, s2 = _mixer_fwd(x1, layer_weights(l), f"l{l}_mix", fwd_carries[l]["mix"])
        act, s3 = ffn_fwd(x2, layer_weights(l), "ffn2", f"l{l}_ffn2", fwd_carries[l]["ffn2"])
        saved.append((x0, s1, x1, s2, x2, s3))
    layers = [layer_weights(l) for l in range(DEPTH)]

    dact, dg_final, loss_row = _final_loss(act, w["g_final"][None, :], tgt, "final_loss")

    big_grads = {n: [None] * DEPTH for n in BIG}
    small_grads = {n: [None] * DEPTH for n in ("g_ffn1", "g_mix", "b_in", "rel_bias", "g_ffn2")}
    pair = [[None] * len(BIG) for _ in range(DEPTH)]
    landed = [[None] * len(BIG) for _ in range(DEPTH)]

    def pair_up(l, idx, tag):
        for i, p in zip(idx, _pair_sums([big_grads[BIG[i]][l] for i in idx], tag)):
            pair[l][i] = p

    core = lax.axis_index("c").astype(jnp.int32)[None]

    def to_sibling_half(l, idx, tag):
        def plan(srcs, bufs, new):
            x, y, c = _position()
            return [(s.at[1 - c], d, (x, y, 1 - c)) for s, d in zip(srcs, new)]
        def done(carry):
            for j, (i, got) in enumerate(zip(idx, carry.out[1])):
                pair[l][i] = _add_pair(core, big_grads[BIG[i]][l], got, f"grad_pair_sum_{tag}_{j}")
        def make():
            split = [big_grads[BIG[i]][l] for i in idx]
            return _Carry(split, [], [jax.ShapeDtypeStruct(s.shape[1:], s.dtype) for s in split], len(idx), plan, done)
        return make

    def exchange(l, idx):
        def done(carry):
            for i, a in zip(idx, carry.out[1]):
                landed[l][i] = a
        def make():
            carry = _scatter_carry([pair[l][i] for i in idx])
            carry.done = done
            return carry
        return make

    def exchange_one_way(l, i, k):
        def plan(srcs, bufs, new):
            x, y, c = _position()
            chip_k, idx_k = _other_chips(x, y)[k]
            return [(srcs[0].at[idx_k], bufs[0].at[k], (*chip_k, c))]
        def done(carry):
            landed[l][i] = carry.out[0][0]
        def make():
            if landed[l][i] is None:
                landed[l][i] = lax.empty((3,) + pair[l][i].shape[1:], BF16)
            return _Carry([pair[l][i]], [landed[l][i]], [], 1, plan, done)
        return make

    bwd_carries = [
        {"ffn2": {},
         "mix": {"dmerged": [to_sibling_half(0, FFN2_IN + FFN2_OUT, "l0_ffn2")],
                 "dsb": [exchange(1, FFN1 + W_IN)], "dch": [exchange(1, MIXER_REST + FFN2_IN + FFN2_OUT)],
                 "dfox": [exchange(0, FFN2_IN + FFN2_OUT)]},
         "ffn1": {"da": [exchange(0, MIXER_REST)], "dwout": [exchange_one_way(0, 2, 0)],
                  "dwin": [exchange_one_way(0, 2, 1)], "dh": [exchange_one_way(0, 2, 2)]}},
        {"ffn2": {}, "mix": {},
         "ffn1": {"da": [to_sibling_half(1, W_IN + MIXER_REST + FFN2_IN + FFN2_OUT, "l1_rest")]}},
    ]
    for l in reversed(range(DEPTH)):
        lw = layers[l]
        x0, s1, x1, s2, x2, s3 = saved[l]
        dx2, small_grads["g_ffn2"][l], big_grads["w_ffn2_in"][l], big_grads["w_ffn2_out"][l] = ffn_bwd(
            x2, lw, "ffn2", s3, dact, f"l{l}_ffn2", bwd_carries[l]["ffn2"])
        dx1, mg = _mixer_bwd(x1, lw, s2, dx2, f"l{l}_mix", bwd_carries[l]["mix"])
        for n in ("w_in", "w_br_sb", "w_br_ch", "w_br_fox", "w_out"):
            big_grads[n][l] = mg[n]
        small_grads["b_in"][l] = _unpad_w_in(mg["b_in"])[0]
        small_grads["g_mix"][l] = mg["g_mix"][0]
        small_grads["rel_bias"][l] = mg["rel_bias"]
        if l == 0:
            pair_up(0, W_IN + MIXER_REST, "l0_mix")
        dact, dg1, big_grads["w_ffn1_in"][l], big_grads["w_ffn1_out"][l] = ffn_bwd(
            x0, lw, "ffn1", s1, dx1, f"l{l}_ffn1", bwd_carries[l]["ffn1"])
        small_grads["g_ffn1"][l] = dg1[0]
        small_grads["g_ffn2"][l] = small_grads["g_ffn2"][l][0]
        if l == 1:
            pair_up(1, FFN1, "l1_ffn1")
    grad_x = dact[None]
    pair_up(0, FFN1, "l0_ffn1")
    for i, a in zip(FFN1, _scatter_chips([pair[0][i] for i in FFN1], "grad_scatter_l0")):
        landed[0][i] = a

    small = {n: jnp.stack(small_grads[n]) for n in small_grads}
    small["g_final"] = dg_final[0]
    small_sum, loss = _unpack_small(_allreduce_small(_pack_small(small, loss_row[0, :1]), "allreduce_small"),
                                    {n: w[n].shape for n in SMALL})

    mine, theirs = _finish_grads(pair[0] + pair[1], landed[0] + landed[1], "all")

    grads, delta, new_m, new_v = dict(small_sum), {}, {}, {}
    for i, n in enumerate(BIG):
        halves = [(mine[l * len(BIG) + i], theirs[l * len(BIG) + i]) for l in range(DEPTH)]
        grads[n], delta[n], new_m[n], new_v[n] = _adamw_from_halves(
            w[n], m[n], v[n], halves, 1 if n in ROW_SHARDED else 0, core, f"adamw_{n}")
    zero = jnp.zeros((1,), F32)
    sd, sm, sv = _adamw(_pack_small(w, zero), _pack_small(grads, zero), _pack_small(m, zero), _pack_small(v, zero),
                        "adamw_small")
    shapes = {n: w[n].shape for n in SMALL}
    for dst, src in ((delta, sd), (new_m, sm), (new_v, sv)):
        dst.update(_unpack_small(src, shapes)[0])

    return (loss, grad_x, *[grads[n] for n in names], *[delta[n] for n in names],
            *[new_m[n] for n in names], *[new_v[n] for n in names])
```

```python
import functools

import numpy as np
import jax
import jax.numpy as jnp
from jax import lax
from jax.experimental import pallas as pl
from jax.experimental.pallas import tpu as pltpu

F32 = jnp.float32
BF16 = jnp.bfloat16

D_MODEL = 1024
DEPTH = 2
CHUNK = 64
HEAD_DIM = 64
H_SB, H_CH, H_FOX = 4, 8, 4
W_SB, W_CH, W_FOX = H_SB * HEAD_DIM, H_CH * HEAD_DIM, H_FOX * HEAD_DIM
LEFT_CHUNKS = 8
MAX_REL = 128
N_REL = 2 * MAX_REL + 1
D_FF = 2816
QKV_WIDTH = 3 * (W_SB + W_CH + W_FOX)
GATE_WIDTH = 3 * D_MODEL
IN_WIDTH = QKV_WIDTH + H_FOX + GATE_WIDTH
F_PAD = 128
P_WIDTH = QKV_WIDTH + GATE_WIDTH + F_PAD
RMS_EPS = 1e-6
NEG = -1e30
SCALE = HEAD_DIM ** -0.5
QB = 256
BAND = (LEFT_CHUNKS + 2) * CHUNK
BAND_PAD = BAND - CHUNK

ADAM_LR, ADAM_B1, ADAM_B2, ADAM_EPS, ADAM_WD, ADAM_STEP = 0.001, 0.9, 0.999, 1e-08, 0.01, 10

N_CHIPS = 4
VMEM_BUDGET = 52 * 1024 * 1024

NT = (((1,), (1,)), ((), ()))
TN = (((0,), (0,)), ((), ()))
NN = (((1,), (0,)), ((), ()))
MESH = pl.DeviceIdType.MESH


def _pcall(body, **kw):
    return pl.pallas_call(body, **kw)


class _Carry:
    def __init__(self, srcs, bufs, new, count, plan, done=None):
        self.srcs, self.bufs, self.new, self.count, self.plan = list(srcs), list(bufs), list(new), count, plan
        self.out, self.done = None, done

    @staticmethod
    def join(parts):
        parts = [p for p in parts if p is not None]
        if not parts:
            return None

        def plan(srcs, bufs, new):
            copies, s, b, n = [], 0, 0, 0
            for p in parts:
                copies += p.plan(srcs[s:s + len(p.srcs)], bufs[b:b + len(p.bufs)], new[n:n + len(p.new)])
                s, b, n = s + len(p.srcs), b + len(p.bufs), n + len(p.new)
            return copies

        whole = _Carry(sum((p.srcs for p in parts), []), sum((p.bufs for p in parts), []),
                       sum((p.new for p in parts), []), sum(p.count for p in parts), plan)
        whole.parts = parts
        return whole

    def share_out(self):
        b, n = 0, 0
        for p in getattr(self, "parts", []):
            p.out = (self.out[0][b:b + len(p.bufs)], self.out[1][n:n + len(p.new)])
            b, n = b + len(p.bufs), n + len(p.new)
            p.share_out()
        if self.done is not None:
            self.done(self)


def _carry_of(carries, key):
    return _Carry.join([make() for make in carries.get(key, [])])


def _pcall_carrying(body, carry, **kw):
    if carry is None:
        return _pcall(body, **kw)
    in_specs = list(kw.pop("in_specs"))
    out_specs, out_shape = kw.pop("out_specs"), kw.pop("out_shape")
    single = not isinstance(out_shape, (list, tuple))
    out_specs = [out_specs] if single else list(out_specs)
    out_shape = [out_shape] if single else list(out_shape)
    scratch = list(kw.pop("scratch_shapes", []))
    grid = tuple(kw.get("grid", ()))
    n_in, n_out, n_scr = len(in_specs), len(out_specs), len(scratch)
    ns, nb, nn = len(carry.srcs), len(carry.bufs), len(carry.new)

    def body2(*refs):
        ins, srcs = refs[:n_in], refs[n_in:n_in + ns]
        at = n_in + ns + nb
        outs, bufs, new = refs[at:at + n_out], refs[at + n_out:at + n_out + nb], refs[at + n_out + nb:at + n_out + nb + nn]
        at += n_out + nb + nn
        scr, (send_sems, recv_sems) = refs[at:at + n_scr], refs[at + n_scr:]

        def copies():
            return [_remote_copy(s, d, send_sems, recv_sems, k, to)
                    for k, (s, d, to) in enumerate(carry.plan(srcs, bufs, new))]

        def start():
            for cp in copies():
                cp.start()

        def wait():
            for cp in copies():
                cp.wait()

        if grid:
            ids = [pl.program_id(a) for a in range(len(grid))]
            pl.when(functools.reduce(jnp.logical_and, [i == 0 for i in ids]))(start)
        else:
            start()
        body(*ins, *outs, *scr)
        if grid:
            pl.when(functools.reduce(jnp.logical_and, [i == g - 1 for i, g in zip(ids, grid)]))(wait)
        else:
            wait()

    call = _pcall(
        body2, in_specs=in_specs + [HBM_SPEC] * (ns + nb), out_specs=out_specs + [HBM_SPEC] * (nb + nn),
        out_shape=out_shape + [jax.ShapeDtypeStruct(b.shape, b.dtype) for b in carry.bufs] + carry.new,
        scratch_shapes=scratch + [pltpu.SemaphoreType.DMA((carry.count,)), pltpu.SemaphoreType.DMA((carry.count,))],
        input_output_aliases={n_in + ns + j: n_out + j for j in range(nb)}, **kw)

    def apply(*args):
        res = call(*args, *carry.srcs, *carry.bufs)
        carry.out = (list(res[n_out:n_out + nb]), list(res[n_out + nb:]))
        carry.share_out()
        return res[0] if single else list(res[:n_out])

    return apply


def _pick(n, cap, mult=128):
    best = None
    d = mult
    while d <= min(n, cap):
        if n % d == 0:
            best = d
        d += mult
    return n if best is None else best


def _nbytes(shape, dtype):
    return int(np.prod(shape)) * jnp.dtype(dtype).itemsize


def _mm(a, b, *, mode, name, out_dtype=F32, alpha=1.0, bias=None, res=None, tm_cap=1024, tn_cap=512,
        tk_cap=2816, gathered=None, out_split=None, carry=None, norm_bwd=None):
    if mode == "tn":
        K, M = a.shape
    else:
        M, K = a.shape
    dn = {"nn": NN, "nt": NT, "tn": TN}[mode]
    b_rows = None
    if gathered is None:
        N = b.shape[0] if mode == "nt" else b.shape[1]
        tn = {None: _pick(N, tn_cap), "col": N // N_CHIPS, "row": N // 2}[out_split]
        if out_split == "col":
            tm_cap = min(tm_cap, M // 2)
        tk = K if K <= tk_cap else _pick(K, tk_cap)
        b_spec = (pl.BlockSpec((tn, tk), lambda i, j, k: (j, k)) if mode == "nt"
                  else pl.BlockSpec((tk, tn), lambda i, j, k: (k, j)))
    else:
        r, c = b.shape[1:]
        rows, cols = (r, N_CHIPS * c) if gathered == "col" else (N_CHIPS * r, c)
        N = rows if mode == "nt" else cols
        assert K == (cols if mode == "nt" else rows), (name, K, rows, cols)
        if gathered == "col" and mode == "nn":
            tn, tk = c, (r if r <= tk_cap else _pick(r, tk_cap))
            b_spec = pl.BlockSpec((None, tk, c), lambda i, j, k: (j, k, 0))
        elif gathered == "col":
            tn, tk = _pick(r, tn_cap), c
            b_spec = pl.BlockSpec((None, tn, c), lambda i, j, k: (k, j, 0))
        elif mode == "nn":
            tn, tk, b_rows = _pick(c, tn_cap), K, N_CHIPS
            b_spec = pl.BlockSpec((N_CHIPS, r, tn), lambda i, j, k: (0, 0, j))
        else:
            tn, tk, b_rows = 2 * r, K, 2
            b_spec = pl.BlockSpec((2, r, c), lambda i, j, k: (j, 0, 0))
    tm = _pick(M, tm_cap)
    nk = K // tk

    a_spec = (pl.BlockSpec((tk, tm), lambda i, j, k: (k, i)) if mode == "tn"
              else pl.BlockSpec((tm, tk), lambda i, j, k: (i, k)))
    in_specs = [a_spec, b_spec]
    args = [a, b]
    if bias is not None:
        in_specs.append(pl.BlockSpec((1, tn), lambda i, j, k: (0, j)))
        args.append(bias)
    if res is not None:
        in_specs.append(pl.BlockSpec((tm, tn), lambda i, j, k: (i, j)))
        args.append(res)
    if norm_bwd is not None:
        assert tn == N and alpha == 1.0 and out_split is None and bias is None and res is None, name
        x_in, g_in, dres_in = norm_bwd
        in_specs += [pl.BlockSpec((tm, tn), lambda i, j, k: (i, 0)), pl.BlockSpec((1, tn), lambda i, j, k: (0, 0)),
                     pl.BlockSpec((tm, tn), lambda i, j, k: (i, 0))]
        args += [x_in, g_in, dres_in]

    est = 2 * (_nbytes((tm, tk), a.dtype) + _nbytes((tk, tn), b.dtype) + _nbytes((tm, tn), out_dtype))
    est += _nbytes((tm, tn), F32) * (3 if res is not None else 1)
    est += _nbytes((tm, tn), F32) * (4 if norm_bwd is not None else 0)
    assert est < VMEM_BUDGET, (name, est)

    def body(*refs):
        a_ref, b_ref = refs[0], refs[1]
        pos = 2
        bias_ref = res_ref = None
        if bias is not None:
            bias_ref = refs[pos]
            pos += 1
        if res is not None:
            res_ref = refs[pos]
            pos += 1
        if norm_bwd is not None:
            x_ref, g_ref, dres_ref = refs[pos:pos + 3]
            pos += 3
        o_ref = refs[pos]
        if norm_bwd is not None:
            dg_ref = refs[pos + 1]
            pos += 1
        acc_ref = refs[pos + 1] if nk > 1 else None

        bv = b_ref[...]
        if b_rows is not None:
            bv = bv.reshape(b_rows * bv.shape[1], bv.shape[2])
        part = lax.dot_general(a_ref[...].astype(BF16), bv.astype(BF16), dn, preferred_element_type=F32)

        def finish(acc):
            if alpha != 1.0:
                acc = acc * alpha
            if bias_ref is not None:
                acc = acc + bias_ref[...]
            if res_ref is not None:
                acc = acc + res_ref[...]
            if norm_bwd is not None:
                xv = x_ref[...]
                rstd = lax.rsqrt(jnp.mean(xv * xv, axis=-1, keepdims=True) + RMS_EPS)
                xhat = xv * rstd
                dyg = acc * g_ref[...]
                part_g = jnp.sum(acc * xhat, axis=0, keepdims=True)
                acc = dres_ref[...] + rstd * (dyg - xhat * jnp.mean(dyg * xhat, axis=-1, keepdims=True))
                first = pl.program_id(0) == 0

                @pl.when(first)
                def _():
                    dg_ref[...] = part_g

                @pl.when(jnp.logical_not(first))
                def _():
                    dg_ref[...] += part_g
            o_ref[...] = acc.astype(out_dtype)

        if nk == 1:
            finish(part)
        else:
            k = pl.program_id(2)

            @pl.when(k == 0)
            def _():
                acc_ref[...] = part

            @pl.when(k > 0)
            def _():
                acc_ref[...] += part

            @pl.when(k == nk - 1)
            def _():
                finish(acc_ref[...])

    if out_split == "col":
        per_half = M // 2 // tm
        out_spec = pl.BlockSpec((None, None, tm, tn), lambda i, j, k: (i // per_half, j, i % per_half, 0))
        out_shape = jax.ShapeDtypeStruct((2, N_CHIPS, M // 2, tn), out_dtype)
    elif out_split == "row":
        out_spec = pl.BlockSpec((None, tm, tn), lambda i, j, k: (j, i, 0))
        out_shape = jax.ShapeDtypeStruct((2, M, tn), out_dtype)
    else:
        out_spec = pl.BlockSpec((tm, tn), lambda i, j, k: (i, j))
        out_shape = jax.ShapeDtypeStruct((M, N), out_dtype)
    semantics = ("parallel", "parallel", "arbitrary")
    if norm_bwd is not None:
        out_spec = [out_spec, pl.BlockSpec((1, tn), lambda i, j, k: (0, 0))]
        out_shape = [out_shape, jax.ShapeDtypeStruct((1, N), F32)]
        semantics = ("arbitrary", "arbitrary", "arbitrary")
    return _pcall_carrying(
        body, carry, name=name, grid=(M // tm, N // tn, nk), in_specs=in_specs, out_specs=out_spec,
        out_shape=out_shape,
        scratch_shapes=[pltpu.VMEM((tm, tn), F32)] if nk > 1 else [],
        compiler_params=pltpu.CompilerParams(dimension_semantics=semantics),
    )(*args)


def _rmsnorm_fwd(x, g, name):
    T, Dm = x.shape
    tm = _pick(T, 256, 8)

    def body(x_ref, g_ref, h_ref):
        xv = x_ref[...]
        rstd = lax.rsqrt(jnp.mean(xv * xv, axis=-1, keepdims=True) + RMS_EPS)
        h_ref[...] = (xv * rstd * g_ref[...]).astype(BF16)

    return _pcall(
        body, name=name, grid=(T // tm,),
        in_specs=[pl.BlockSpec((tm, Dm), lambda i: (i, 0)), pl.BlockSpec((1, Dm), lambda i: (0, 0))],
        out_specs=pl.BlockSpec((tm, Dm), lambda i: (i, 0)),
        out_shape=jax.ShapeDtypeStruct((T, Dm), BF16),
    )(x, g)


def _final_loss(x, g, tgt, name):
    T, Dm = x.shape
    tm = _pick(T, 256, 8)

    def body(x_ref, g_ref, t_ref, dx_ref, dg_ref, loss_ref):
        xv = x_ref[...]
        gv = g_ref[...]
        rstd = lax.rsqrt(jnp.mean(xv * xv, axis=-1, keepdims=True) + RMS_EPS)
        xhat = xv * rstd
        err = xhat * gv - t_ref[...]
        part_loss = 0.5 * jnp.sum(jnp.mean(err * err, axis=-1, keepdims=True), axis=0, keepdims=True)
        dy = err * (1.0 / Dm)
        dyg = dy * gv
        dx_ref[...] = rstd * (dyg - xhat * jnp.mean(dyg * xhat, axis=-1, keepdims=True))
        part_g = jnp.sum(dy * xhat, axis=0, keepdims=True)
        part_l = jnp.broadcast_to(part_loss, (1, 128))

        @pl.when(pl.program_id(0) == 0)
        def _():
            dg_ref[...] = part_g
            loss_ref[...] = part_l

        @pl.when(pl.program_id(0) > 0)
        def _():
            dg_ref[...] += part_g
            loss_ref[...] += part_l

    row = pl.BlockSpec((tm, Dm), lambda i: (i, 0))
    vec = pl.BlockSpec((1, Dm), lambda i: (0, 0))
    return _pcall(
        body, name=name, grid=(T // tm,), in_specs=[row, vec, row],
        out_specs=[row, vec, pl.BlockSpec((1, 128), lambda i: (0, 0))],
        out_shape=[jax.ShapeDtypeStruct((T, Dm), F32), jax.ShapeDtypeStruct((1, Dm), F32),
                   jax.ShapeDtypeStruct((1, 128), F32)],
        compiler_params=pltpu.CompilerParams(dimension_semantics=("arbitrary",)),
    )(x, g, tgt)


def _sigmoid(z):
    return 1.0 / (1.0 + jnp.exp(-z))


def _ffn_in_swiglu(x, g, w_in, name, carry=None):
    T, Dm = x.shape
    cw = w_in.shape[2]
    tm = _pick(T, 512, 16)

    def body(x_ref, gain_ref, wg_ref, wu_ref, h_ref, g_ref, u_ref, a_ref):
        @pl.when(pl.program_id(1) == 0)
        def _():
            xv = x_ref[...]
            rstd = lax.rsqrt(jnp.mean(xv * xv, axis=-1, keepdims=True) + RMS_EPS)
            h_ref[...] = (xv * rstd * gain_ref[...]).astype(BF16)

        hv = h_ref[...]
        gv = jnp.dot(hv, wg_ref[...], preferred_element_type=F32)
        uv = jnp.dot(hv, wu_ref[...], preferred_element_type=F32)
        g_ref[...] = gv.astype(BF16)
        u_ref[...] = uv.astype(BF16)
        a_ref[...] = (gv * _sigmoid(gv) * uv).astype(BF16)

    row = pl.BlockSpec((tm, Dm), lambda i, j: (i, 0))
    out = pl.BlockSpec((tm, cw), lambda i, j: (i, j))
    return _pcall_carrying(
        body, carry, name=name, grid=(T // tm, 2),
        in_specs=[row, pl.BlockSpec((1, Dm), lambda i, j: (0, 0)), pl.BlockSpec((None, Dm, cw), lambda i, j: (j, 0, 0)),
                  pl.BlockSpec((None, Dm, cw), lambda i, j: (j + 2, 0, 0))],
        out_specs=[row, out, out, out],
        out_shape=[jax.ShapeDtypeStruct((T, Dm), BF16)] + [jax.ShapeDtypeStruct((T, D_FF), BF16)] * 3,
        compiler_params=pltpu.CompilerParams(dimension_semantics=("parallel", "arbitrary")),
    )(x, g, w_in, w_in)


def _swiglu_bwd(g, u, da, name):
    T = g.shape[0]
    tm = _pick(T, 256, 16)

    def body(g_ref, u_ref, da_ref, d_ref):
        gv = g_ref[...].astype(F32)
        uv = u_ref[...].astype(F32)
        dav = da_ref[...].astype(F32)
        sg = _sigmoid(gv)
        d_ref[:, :D_FF] = (dav * uv * (sg + gv * sg * (1.0 - sg))).astype(BF16)
        d_ref[:, D_FF:] = (dav * gv * sg).astype(BF16)

    row = pl.BlockSpec((tm, D_FF), lambda i: (i, 0))
    return _pcall(
        body, name=name, grid=(T // tm,), in_specs=[row, row, row],
        out_specs=pl.BlockSpec((tm, 2 * D_FF), lambda i: (i, 0)),
        out_shape=jax.ShapeDtypeStruct((T, 2 * D_FF), BF16),
    )(g, u, da)


def _branches_fwd(p, outs, weights, name):
    T = p.shape[0]
    cw = weights[0].shape[2]
    tm = _pick(T, 1024, 16)
    first_gate, per_branch = QKV_WIDTH // cw, D_MODEL // cw

    def body(*refs):
        gates, o_refs, w_refs, y_refs, m_ref = refs[0:3], refs[3:6], refs[6:9], refs[9:12], refs[12]
        merged = None
        for g_ref, o_ref, w_ref, y_ref in zip(gates, o_refs, w_refs, y_refs):
            y = jnp.dot(o_ref[...], w_ref[...], preferred_element_type=F32)
            y_ref[...] = y
            term = _sigmoid(g_ref[...].astype(F32)) * y
            merged = term if merged is None else merged + term
        m_ref[...] = merged.astype(BF16)

    gate_specs = [pl.BlockSpec((tm, cw), functools.partial(
        lambda i, j, kk: (i, first_gate + per_branch * kk + j), kk=kk)) for kk in range(3)]
    o_specs = [pl.BlockSpec((tm, o.shape[1]), lambda i, j: (i, 0)) for o in outs]
    w_specs = [pl.BlockSpec((None, wk.shape[1], cw), lambda i, j: (j, 0, 0)) for wk in weights]
    tile = pl.BlockSpec((tm, cw), lambda i, j: (i, j))
    return _pcall(
        body, name=name, grid=(T // tm, N_CHIPS), in_specs=gate_specs + o_specs + w_specs, out_specs=[tile] * 4,
        out_shape=[jax.ShapeDtypeStruct((T, D_MODEL), F32)] * 3 + [jax.ShapeDtypeStruct((T, D_MODEL), BF16)],
    )(p, p, p, *outs, *weights)


def _branches_bwd(p, ys, dm, weights, name):
    T = p.shape[0]
    cw = weights[0].shape[2]
    tm = _pick(T, 1024, 16)
    first_gate, per_branch = QKV_WIDTH // cw, D_MODEL // cw
    widths = [wk.shape[1] for wk in weights]

    def body(*refs):
        gates, y_refs, dm_ref, w_refs = refs[0:3], refs[3:6], refs[6], refs[7:10]
        dy_refs, dg_refs, do_refs, acc_refs = refs[10:13], refs[13:16], refs[16:19], refs[19:22]
        j = pl.program_id(1)
        dmv = dm_ref[...]
        for g_ref, y_ref, w_ref, dy_ref, dg_ref, do_ref, acc_ref in zip(gates, y_refs, w_refs, dy_refs, dg_refs,
                                                                       do_refs, acc_refs):
            s = _sigmoid(g_ref[...].astype(F32))
            dy = (s * dmv).astype(BF16)
            dy_ref[...] = dy
            dg_ref[...] = (dmv * y_ref[...] * s * (1.0 - s)).astype(BF16)
            part = _dot_nt(dy, w_ref[...])

            @pl.when(j == 0)
            def _():
                acc_ref[...] = part

            @pl.when(j > 0)
            def _():
                acc_ref[...] += part

            @pl.when(j == N_CHIPS - 1)
            def _():
                do_ref[...] = acc_ref[...].astype(BF16)

    gate_specs = [pl.BlockSpec((tm, cw), functools.partial(
        lambda i, j, kk: (i, first_gate + per_branch * kk + j), kk=kk)) for kk in range(3)]
    tile = pl.BlockSpec((tm, cw), lambda i, j: (i, j))
    w_specs = [pl.BlockSpec((None, r, cw), lambda i, j: (j, 0, 0)) for r in widths]
    do_specs = [pl.BlockSpec((tm, r), lambda i, j: (i, 0)) for r in widths]
    wide = jax.ShapeDtypeStruct((T, D_MODEL), BF16)
    out = _pcall(
        body, name=name, grid=(T // tm, N_CHIPS), in_specs=gate_specs + [tile] * 4 + w_specs,
        out_specs=[tile] * 6 + do_specs,
        out_shape=[wide] * 6 + [jax.ShapeDtypeStruct((T, r), BF16) for r in widths],
        scratch_shapes=[pltpu.VMEM((tm, r), F32) for r in widths],
        compiler_params=pltpu.CompilerParams(dimension_semantics=("parallel", "arbitrary")),
    )(p, p, p, *ys, dm, *weights)
    return out[0:3], out[3:6], out[6:9]


def _branches_dw(outs, dys, name):
    T = dys[0].shape[0]
    cw = D_MODEL // N_CHIPS
    widths = [o.shape[1] for o in outs]

    def body(*refs):
        o_refs, dy_refs, dw_refs = refs[0:3], refs[3:6], refs[6:9]
        for o_ref, dy_ref, dw_ref, r in zip(o_refs, dy_refs, dw_refs, widths):
            dw = _dot_tn(o_ref[...], dy_ref[...]).astype(BF16)
            dw_ref[0] = dw[:r // 2]
            dw_ref[1] = dw[r // 2:]

    return _pcall(
        body, name=name, grid=(N_CHIPS,),
        in_specs=[pl.BlockSpec((T, r), lambda j: (0, 0)) for r in widths] + [pl.BlockSpec((T, cw), lambda j: (0, j))] * 3,
        out_specs=[pl.BlockSpec((2, None, r // 2, cw), lambda j: (0, j, 0, 0)) for r in widths],
        out_shape=[jax.ShapeDtypeStruct((2, N_CHIPS, r // 2, cw), BF16) for r in widths],
    )(*outs, *dys)


def _colsum(a, name):
    T, N = a.shape
    tm = _pick(T, 256, 16)

    def body(a_ref, o_ref):
        part = jnp.sum(a_ref[...].astype(F32), axis=0, keepdims=True)

        @pl.when(pl.program_id(0) == 0)
        def _():
            o_ref[...] = part

        @pl.when(pl.program_id(0) > 0)
        def _():
            o_ref[...] += part

    return _pcall(
        body, name=name, grid=(T // tm,), in_specs=[pl.BlockSpec((tm, N), lambda i: (i, 0))],
        out_specs=pl.BlockSpec((1, N), lambda i: (0, 0)), out_shape=jax.ShapeDtypeStruct((1, N), F32),
        compiler_params=pltpu.CompilerParams(dimension_semantics=("arbitrary",)),
    )(a)


def _adamw(w, g, m, v, name):
    R, C = w.shape
    tr = 256 if R % 256 == 0 else R
    c1 = 1.0 / (1.0 - ADAM_B1 ** ADAM_STEP)
    c2 = 1.0 / (1.0 - ADAM_B2 ** ADAM_STEP)

    def body(w_ref, g_ref, m_ref, v_ref, d_ref, nm_ref, nv_ref):
        gv = g_ref[...]
        mn = ADAM_B1 * m_ref[...] + (1.0 - ADAM_B1) * gv
        vn = ADAM_B2 * v_ref[...] + (1.0 - ADAM_B2) * (gv * gv)
        nm_ref[...] = mn
        nv_ref[...] = vn
        d_ref[...] = -ADAM_LR * ((mn * c1) / (jnp.sqrt(vn * c2) + ADAM_EPS) + ADAM_WD * w_ref[...])

    spec = pl.BlockSpec((tr, C), lambda i: (i, 0))
    return _pcall(
        body, name=name, grid=(R // tr,), in_specs=[spec] * 4, out_specs=[spec] * 3,
        out_shape=[jax.ShapeDtypeStruct((R, C), F32)] * 3,
    )(w, g, m, v)


def _adamw_from_halves(w, m, v, halves, axis, core, name):
    L, r, c = w.shape
    assert L == len(halves) == 2
    c1 = 1.0 / (1.0 - ADAM_B1 ** ADAM_STEP)
    c2 = 1.0 / (1.0 - ADAM_B2 ** ADAM_STEP)
    if axis == 0:
        tr = _pick(r // 2, 256, 8)
        per_half = r // 2 // tr
        steps = 2 * per_half

        def half_spec(layer, is_mine):
            def index(l, i, core_ref):
                read = jnp.logical_and(l == layer, (i // per_half == core_ref[0]) == is_mine)
                return jnp.where(read, i % per_half, 0), 0
            return pl.BlockSpec((tr, c), index)
    else:
        tr = _pick(r, 256, 8)
        steps = r // tr

        def half_spec(layer, is_mine):
            return pl.BlockSpec((tr, c // 2), lambda l, i, core_ref: (jnp.where(l == layer, i, 0), 0))
    whole = pl.BlockSpec((None, tr, c), lambda l, i, core_ref: (l, i, 0))
    half_specs = [half_spec(layer, is_mine) for layer in range(L) for is_mine in (True, False)]

    def body(core_ref, w_ref, m_ref, v_ref, mine0, theirs0, mine1, theirs1, g_ref, d_ref, nm_ref, nv_ref):
        shape = mine0.shape
        layer0 = jnp.full(shape, pl.program_id(0), jnp.int32) == 0
        mine = jnp.where(layer0, mine0[...], mine1[...])
        theirs = jnp.where(layer0, theirs0[...], theirs1[...])
        core_v = jnp.full(shape, core_ref[0], jnp.int32)

        def update(gv, cols):
            mn = ADAM_B1 * m_ref[:, cols] + (1.0 - ADAM_B1) * gv
            vn = ADAM_B2 * v_ref[:, cols] + (1.0 - ADAM_B2) * (gv * gv)
            g_ref[:, cols] = gv
            nm_ref[:, cols] = mn
            nv_ref[:, cols] = vn
            d_ref[:, cols] = -ADAM_LR * ((mn * c1) / (jnp.sqrt(vn * c2) + ADAM_EPS) + ADAM_WD * w_ref[:, cols])

        if axis == 0:
            here = jnp.full(shape, pl.program_id(1) // per_half, jnp.int32)
            update(jnp.where(here == core_v, mine, theirs), slice(None))
        else:
            update(jnp.where(core_v == 0, mine, theirs), slice(0, c // 2))
            update(jnp.where(core_v == 0, theirs, mine), slice(c // 2, c))

    grid_spec = pltpu.PrefetchScalarGridSpec(
        num_scalar_prefetch=1, grid=(L, steps), in_specs=[whole] * 3 + half_specs, out_specs=[whole] * 4)
    return _pcall(body, name=name, grid_spec=grid_spec, out_shape=[jax.ShapeDtypeStruct((L, r, c), F32)] * 4,
                  )(core, w, m, v, *halves[0], *halves[1])


def _log_sigmoid(z):
    return jnp.minimum(z, 0.0) - jnp.log(1.0 + jnp.exp(-jnp.abs(z)))


def _dot3(x, m01):
    x1 = x.astype(BF16)
    r1 = x - x1.astype(F32)
    x2 = r1.astype(BF16)
    x3 = (r1 - x2.astype(F32)).astype(BF16)
    n = x.shape[0]
    if n % 16:
        return (jnp.dot(x1, m01, preferred_element_type=F32) + jnp.dot(x2, m01, preferred_element_type=F32)
                + jnp.dot(x3, m01, preferred_element_type=F32))
    y = jnp.dot(jnp.concatenate([x1, x2], axis=0), m01, preferred_element_type=F32)
    return y[:n] + y[n:]


def _dot_nt(a, b):
    return lax.dot_general(a, b, NT, preferred_element_type=F32)


def _dot_tn(a, b):
    return lax.dot_general(a, b, TN, preferred_element_type=F32)


def _iota2(shape):
    return lax.broadcasted_iota(jnp.int32, shape, 0), lax.broadcasted_iota(jnp.int32, shape, 1)


HEADS_PER_STEP = 4
HEADS = range(HEADS_PER_STEP)


CHUNK_HEADS_PER_STEP = 4
CHUNK_HEADS = range(CHUNK_HEADS_PER_STEP)


def _head_spec(T, rows=None, width=HEAD_DIM, heads=HEADS_PER_STEP):
    return pl.BlockSpec((heads, T if rows is None else rows, width), lambda g: (g, 0, 0))


def _sb_weights(qb, kb, t0, s0, racc, m_gt, row, col, diagonal):
    z = _dot_nt(qb, kb) * SCALE
    lb = _log_sigmoid(z)
    strict = (s0 + col) < (t0 + row) if diagonal else None
    lf = jnp.where(strict, lb - z, 0.0) if diagonal else lb - z
    w = jnp.exp(lb + _dot3(lf, m_gt) + racc)
    return strict, lb, lf, (jnp.where(strict, w, 0.0) if diagonal else w)


def _sb_fwd(q, k, v, name, carry=None):
    H, T, _ = q.shape
    nb = T // QB

    def body(q_ref, k_ref, v_ref, o_ref):
        row, col = _iota2((QB, QB))
        m_gt = (row > col).astype(BF16)

        def qblock(i, _):
            t0 = pl.multiple_of(i * QB, QB)
            qbs = [q_ref[h, pl.ds(t0, QB), :].astype(BF16) for h in HEADS]

            def kblock(j, carry, diagonal):
                s0 = pl.multiple_of(j * QB, QB)
                out = []
                for h in HEADS:
                    acc, racc = carry[h]
                    kb = k_ref[h, pl.ds(s0, QB), :].astype(BF16)
                    vb = v_ref[h, pl.ds(s0, QB), :].astype(BF16)
                    _, _, lf, w = _sb_weights(qbs[h], kb, t0, s0, racc, m_gt, row, col, diagonal)
                    acc = acc + jnp.dot(w.astype(BF16), vb, preferred_element_type=F32)
                    out.append((acc, racc + jnp.sum(lf, axis=1, keepdims=True)))
                return tuple(out)

            res = kblock(i, tuple((jnp.zeros((QB, HEAD_DIM), F32), jnp.zeros((QB, 1), F32)) for _ in HEADS), True)
            res = lax.fori_loop(1, i + 1, lambda jj, carry: kblock(i - jj, carry, False), res)
            for h in HEADS:
                o_ref[h, pl.ds(t0, QB), :] = res[h][0].astype(BF16)
            return 0

        lax.fori_loop(0, nb, qblock, 0)

    spec = _head_spec(T)
    return _pcall_carrying(body, carry, name=name, grid=(H // HEADS_PER_STEP,), in_specs=[spec] * 3, out_specs=spec,
                           out_shape=jax.ShapeDtypeStruct((H, T, HEAD_DIM), BF16))(q, k, v)


def _sb_bwd(q, k, v, do, name, carry=None):
    H, T, _ = q.shape
    nb = T // QB

    def body(q_ref, k_ref, v_ref, do_ref, dq_ref, dk_out, dv_out, racc_ref, dk_ref, dv_ref):
        row, col = _iota2((QB, QB))
        m_gt = (row > col).astype(BF16)
        m_lt = (row < col).astype(BF16)
        dk_ref[...] = jnp.zeros_like(dk_ref)
        dv_ref[...] = jnp.zeros_like(dv_ref)

        def qblock(i, _):
            t0 = pl.multiple_of(i * QB, QB)
            qbs = [q_ref[h, pl.ds(t0, QB), :].astype(BF16) for h in HEADS]
            dobs = [do_ref[h, pl.ds(t0, QB), :].astype(BF16) for h in HEADS]

            def suffix(j, raccs, diagonal):
                s0 = pl.multiple_of(j * QB, QB)
                out = []
                for h in HEADS:
                    kb = k_ref[h, pl.ds(s0, QB), :].astype(BF16)
                    z = _dot_nt(qbs[h], kb) * SCALE
                    lf = _log_sigmoid(z) - z
                    if diagonal:
                        lf = jnp.where((s0 + col) < (t0 + row), lf, 0.0)
                    racc_ref[h, j] = raccs[h]
                    out.append(raccs[h] + jnp.sum(lf, axis=1, keepdims=True))
                return tuple(out)

            raccs = suffix(i, tuple(jnp.zeros((QB, 1), F32) for _ in HEADS), True)
            lax.fori_loop(1, i + 1, lambda jj, r: suffix(i - jj, r, False), raccs)

            def kblock(j, carry, diagonal):
                s0 = pl.multiple_of(j * QB, QB)
                out = []
                for h in HEADS:
                    dq, cacc = carry[h]
                    kb = k_ref[h, pl.ds(s0, QB), :].astype(BF16)
                    vb = v_ref[h, pl.ds(s0, QB), :].astype(BF16)
                    strict, lb, _, w = _sb_weights(qbs[h], kb, t0, s0, racc_ref[h, j], m_gt, row, col, diagonal)
                    e = _dot_nt(dobs[h], vb) * w
                    c_left = _dot3(e, m_lt) + cacc
                    sig = jnp.exp(lb)
                    dz = e * (1.0 - sig) - c_left * sig
                    dz = (jnp.where(strict, dz, 0.0) if diagonal else dz).astype(BF16)
                    dq = dq + jnp.dot(dz, kb, preferred_element_type=F32)
                    dk_ref[h, pl.ds(s0, QB), :] += _dot_tn(dz, qbs[h]) * SCALE
                    dv_ref[h, pl.ds(s0, QB), :] += _dot_tn(w.astype(BF16), dobs[h])
                    out.append((dq, cacc + jnp.sum(e, axis=1, keepdims=True)))
                return tuple(out)

            res = lax.fori_loop(0, i, lambda j, carry: kblock(j, carry, False),
                                tuple((jnp.zeros((QB, HEAD_DIM), F32), jnp.zeros((QB, 1), F32)) for _ in HEADS))
            res = kblock(i, res, True)
            for h in HEADS:
                dq_ref[h, pl.ds(t0, QB), :] = (res[h][0] * SCALE).astype(BF16)
            return 0

        lax.fori_loop(0, nb, qblock, 0)
        dk_out[...] = dk_ref[...].astype(BF16)
        dv_out[...] = dv_ref[...].astype(BF16)

    spec = _head_spec(T)
    acc = pltpu.VMEM((HEADS_PER_STEP, T, HEAD_DIM), F32)
    return _pcall_carrying(body, carry, name=name, grid=(H // HEADS_PER_STEP,), in_specs=[spec] * 4,
                           out_specs=[spec] * 3, out_shape=[jax.ShapeDtypeStruct((H, T, HEAD_DIM), BF16)] * 3,
                           scratch_shapes=[pltpu.VMEM((HEADS_PER_STEP, nb, QB, 1), F32), acc, acc])(q, k, v, do)


def _fox_logits(qb, kb, fq, fk, t0, s0, row, col):
    z = _dot_nt(qb, kb) * SCALE + fq - fk
    return jnp.where((s0 + col) <= (t0 + row), z, NEG)


def _fox_specs(T):
    return _head_spec(T, width=1), _head_spec(T, rows=T // QB, width=QB)


def _fox_fwd(q, k, v, fq, fk, name, carry=None):
    H, T, _ = q.shape
    nb = T // QB

    def body(q_ref, k_ref, v_ref, fq_ref, fk_ref, o_ref, lse_ref):
        row, col = _iota2((QB, QB))

        def qblock(i, _):
            t0 = pl.multiple_of(i * QB, QB)
            qbs = [q_ref[h, pl.ds(t0, QB), :].astype(BF16) for h in HEADS]
            fqs = [fq_ref[h, pl.ds(t0, QB), :] for h in HEADS]

            def kblock(j, carry):
                s0 = pl.multiple_of(j * QB, QB)
                out = []
                for h in HEADS:
                    acc, m, l = carry[h]
                    kb = k_ref[h, pl.ds(s0, QB), :].astype(BF16)
                    vb = v_ref[h, pl.ds(s0, QB), :].astype(BF16)
                    z = _fox_logits(qbs[h], kb, fqs[h], fk_ref[h, pl.ds(j, 1), :], t0, s0, row, col)
                    m_new = jnp.maximum(m, jnp.max(z, axis=1, keepdims=True))
                    a = jnp.exp(m - m_new)
                    p = jnp.exp(z - m_new)
                    acc = a * acc + jnp.dot(p.astype(BF16), vb, preferred_element_type=F32)
                    out.append((acc, m_new, a * l + jnp.sum(p, axis=1, keepdims=True)))
                return tuple(out)

            res = lax.fori_loop(0, i + 1, kblock,
                                tuple((jnp.zeros((QB, HEAD_DIM), F32), jnp.full((QB, 1), NEG, F32),
                                       jnp.zeros((QB, 1), F32)) for _ in HEADS))
            for h in HEADS:
                acc, m, l = res[h]
                o_ref[h, pl.ds(t0, QB), :] = (acc / l).astype(BF16)
                lse_ref[h, pl.ds(t0, QB), :] = m + jnp.log(l)
            return 0

        lax.fori_loop(0, nb, qblock, 0)

    spec = _head_spec(T)
    fq_spec, fk_spec = _fox_specs(T)
    return _pcall_carrying(
        body, carry, name=name, grid=(H // HEADS_PER_STEP,), in_specs=[spec] * 3 + [fq_spec, fk_spec],
        out_specs=[spec, fq_spec],
        out_shape=[jax.ShapeDtypeStruct((H, T, HEAD_DIM), BF16), jax.ShapeDtypeStruct((H, T, 1), F32)],
    )(q, k, v, fq, fk)


def _fox_bwd(q, k, v, fq, fk, lse, do, name, carry=None):
    H, T, _ = q.shape
    nb = T // QB

    def body(q_ref, k_ref, v_ref, fq_ref, fk_ref, lse_ref, do_ref, dq_ref, dk_out, dv_out, dfk_ref, dk_ref, dv_ref):
        row, col = _iota2((QB, QB))
        dk_ref[...] = jnp.zeros_like(dk_ref)
        dv_ref[...] = jnp.zeros_like(dv_ref)
        dfk_ref[...] = jnp.zeros_like(dfk_ref)

        def qblock(i, _):
            t0 = pl.multiple_of(i * QB, QB)
            qbs = [q_ref[h, pl.ds(t0, QB), :].astype(BF16) for h in HEADS]
            fqs = [fq_ref[h, pl.ds(t0, QB), :] for h in HEADS]
            lses = [lse_ref[h, pl.ds(t0, QB), :] for h in HEADS]
            dobs = [do_ref[h, pl.ds(t0, QB), :].astype(BF16) for h in HEADS]

            def probs(h, j):
                s0 = pl.multiple_of(j * QB, QB)
                kb = k_ref[h, pl.ds(s0, QB), :].astype(BF16)
                vb = v_ref[h, pl.ds(s0, QB), :].astype(BF16)
                z = _fox_logits(qbs[h], kb, fqs[h], fk_ref[h, pl.ds(j, 1), :], t0, s0, row, col)
                return s0, kb, jnp.exp(z - lses[h]), _dot_nt(dobs[h], vb)

            def row_dot(j, deltas):
                out = []
                for h in HEADS:
                    _, _, p, dp = probs(h, j)
                    out.append(deltas[h] + jnp.sum(p * dp, axis=1, keepdims=True))
                return tuple(out)

            deltas = lax.fori_loop(0, i + 1, row_dot, tuple(jnp.zeros((QB, 1), F32) for _ in HEADS))

            def kblock(j, dqs):
                out = []
                for h in HEADS:
                    s0, kb, p, dp = probs(h, j)
                    dz = p * (dp - deltas[h])
                    dzb = dz.astype(BF16)
                    dk_ref[h, pl.ds(s0, QB), :] += _dot_tn(dzb, qbs[h]) * SCALE
                    dv_ref[h, pl.ds(s0, QB), :] += _dot_tn(p.astype(BF16), dobs[h])
                    dfk_ref[h, pl.ds(j, 1), :] += -jnp.sum(dz, axis=0, keepdims=True)
                    out.append(dqs[h] + jnp.dot(dzb, kb, preferred_element_type=F32))
                return tuple(out)

            dqs = lax.fori_loop(0, i + 1, kblock, tuple(jnp.zeros((QB, HEAD_DIM), F32) for _ in HEADS))
            for h in HEADS:
                dq_ref[h, pl.ds(t0, QB), :] = (dqs[h] * SCALE).astype(BF16)
            return 0

        lax.fori_loop(0, nb, qblock, 0)
        dk_out[...] = dk_ref[...].astype(BF16)
        dv_out[...] = dv_ref[...].astype(BF16)

    spec = _head_spec(T)
    fq_spec, fk_spec = _fox_specs(T)
    acc = pltpu.VMEM((HEADS_PER_STEP, T, HEAD_DIM), F32)
    return _pcall_carrying(body, carry, name=name, grid=(H // HEADS_PER_STEP,),
                           in_specs=[spec] * 3 + [fq_spec, fk_spec, fq_spec, spec],
                           out_specs=[spec, spec, spec, fk_spec],
                           out_shape=[jax.ShapeDtypeStruct((H, T, HEAD_DIM), BF16)] * 3
                           + [jax.ShapeDtypeStruct((H, nb, QB), F32)],
                           scratch_shapes=[acc, acc])(q, k, v, fq, fk, lse, do)


def _forget_cumsum(flog, name):
    R, T = flog.shape
    nb = T // QB

    def body(x_ref, o_ref):
        row, col = _iota2((QB, QB))
        m_le = (row <= col).astype(BF16)
        carry = jnp.zeros((R, 1), F32)
        for b in range(nb):
            lf = _log_sigmoid(x_ref[:, b * QB:(b + 1) * QB])
            o_ref[:, b * QB:(b + 1) * QB] = _dot3(lf, m_le) + carry
            carry = carry + jnp.sum(lf, axis=1, keepdims=True)

    spec = pl.BlockSpec((R, T), lambda: (0, 0))
    return _pcall(body, name=name, in_specs=[spec], out_specs=spec,
                  out_shape=jax.ShapeDtypeStruct((R, T), F32))(flog)


def _forget_cumsum_bwd(flog, df, name):
    R, T = flog.shape
    nb = T // QB

    def body(x_ref, df_ref, o_ref):
        row, col = _iota2((QB, QB))
        m_ge = (row >= col).astype(BF16)
        carry = jnp.zeros((R, 1), F32)
        for b in reversed(range(nb)):
            dfb = df_ref[:, b * QB:(b + 1) * QB]
            dlog = _dot3(dfb, m_ge) + carry
            o_ref[:, b * QB:(b + 1) * QB] = dlog * _sigmoid(-x_ref[:, b * QB:(b + 1) * QB])
            carry = carry + jnp.sum(dfb, axis=1, keepdims=True)

    spec = pl.BlockSpec((R, T), lambda: (0, 0))
    return _pcall(body, name=name, in_specs=[spec, spec], out_specs=spec,
                  out_shape=jax.ShapeDtypeStruct((R, T), F32))(flog, df)


def _chunk_probs(qc, kb, bias, c, col):
    z = _dot_nt(qc, kb) * SCALE + bias
    z = jnp.where((c - (LEFT_CHUNKS + 1)) * CHUNK + col >= jnp.maximum(0, (c - LEFT_CHUNKS) * CHUNK), z, NEG)
    p = jnp.exp(z - jnp.max(z, axis=1, keepdims=True))
    return p, jnp.sum(p, axis=1, keepdims=True)


def _chunk_specs(T, n=CHUNK_HEADS_PER_STEP):
    return (_head_spec(T, heads=n), _head_spec(T, rows=T + BAND_PAD, heads=n),
            _head_spec(T, rows=CHUNK, width=BAND, heads=n))


def _chunk_fwd(q, kp, vp, bias, name, carry=None):
    H, T, _ = q.shape
    nc = T // CHUNK

    def body(q_ref, kp_ref, vp_ref, bias_ref, o_ref):
        _, col = _iota2((CHUNK, BAND))

        def chunk(c, _):
            t0 = pl.multiple_of(c * CHUNK, CHUNK)
            for h in HEADS:
                qc = q_ref[h, pl.ds(t0, CHUNK), :].astype(BF16)
                kb = kp_ref[h, pl.ds(t0, BAND), :].astype(BF16)
                vb = vp_ref[h, pl.ds(t0, BAND), :].astype(BF16)
                p, l = _chunk_probs(qc, kb, bias_ref[h], c, col)
                o = jnp.dot(p.astype(BF16), vb, preferred_element_type=F32) / l
                o_ref[h, pl.ds(t0, CHUNK), :] = o.astype(BF16)
            return 0

        lax.fori_loop(0, nc, chunk, 0)

    q_spec, kp_spec, b_spec = _chunk_specs(T, HEADS_PER_STEP)
    return _pcall_carrying(body, carry, name=name, grid=(H // HEADS_PER_STEP,),
                           in_specs=[q_spec, kp_spec, kp_spec, b_spec], out_specs=q_spec,
                           out_shape=jax.ShapeDtypeStruct((H, T, HEAD_DIM), BF16))(q, kp, vp, bias)


def _chunk_bwd(q, kp, vp, bias, do, name, carry=None):
    H, T, _ = q.shape
    nc = T // CHUNK

    def body(q_ref, kp_ref, vp_ref, bias_ref, do_ref, dq_ref, dk_out, dv_out, db_ref, dkp_ref, dvp_ref):
        _, col = _iota2((CHUNK, BAND))
        dkp_ref[...] = jnp.zeros_like(dkp_ref)
        dvp_ref[...] = jnp.zeros_like(dvp_ref)
        db_ref[...] = jnp.zeros_like(db_ref)

        def chunk(c, _):
            t0 = pl.multiple_of(c * CHUNK, CHUNK)
            for h in CHUNK_HEADS:
                qc = q_ref[h, pl.ds(t0, CHUNK), :].astype(BF16)
                kb = kp_ref[h, pl.ds(t0, BAND), :].astype(BF16)
                vb = vp_ref[h, pl.ds(t0, BAND), :].astype(BF16)
                dob = do_ref[h, pl.ds(t0, CHUNK), :].astype(BF16)
                p, l = _chunk_probs(qc, kb, bias_ref[h], c, col)
                p = p / l
                dp = _dot_nt(dob, vb)
                dz = p * (dp - jnp.sum(p * dp, axis=1, keepdims=True))
                dzb = dz.astype(BF16)
                dq = jnp.dot(dzb, kb, preferred_element_type=F32) * SCALE
                dq_ref[h, pl.ds(t0, CHUNK), :] = dq.astype(BF16)
                dkp_ref[h, pl.ds(t0, BAND), :] += _dot_tn(dzb, qc) * SCALE
                dvp_ref[h, pl.ds(t0, BAND), :] += _dot_tn(p.astype(BF16), dob)
                db_ref[h] += dz
            return 0

        lax.fori_loop(0, nc, chunk, 0)
        dk_out[...] = dkp_ref[:, BAND_PAD:, :].astype(BF16)
        dv_out[...] = dvp_ref[:, BAND_PAD:, :].astype(BF16)

    q_spec, kp_spec, b_spec = _chunk_specs(T)
    acc = pltpu.VMEM((CHUNK_HEADS_PER_STEP, T + BAND_PAD, HEAD_DIM), F32)
    return _pcall_carrying(body, carry, name=name, grid=(H // CHUNK_HEADS_PER_STEP,),
                           in_specs=[q_spec, kp_spec, kp_spec, b_spec, q_spec],
                           out_specs=[q_spec, q_spec, q_spec, b_spec],
                           out_shape=[jax.ShapeDtypeStruct((H, T, HEAD_DIM), BF16)] * 3
                           + [jax.ShapeDtypeStruct((H, CHUNK, BAND), F32)],
                           scratch_shapes=[acc, acc])(q, kp, vp, bias, do)


HBM_SPEC = pl.BlockSpec(memory_space=pltpu.HBM)


def _position():
    return lax.axis_index("x"), lax.axis_index("y"), lax.axis_index("c")


def _other_chips(x, y):
    chips = [(1 - x, y), (x, 1 - y), (1 - x, 1 - y)]
    return [(chip, 2 * chip[0] + chip[1]) for chip in chips]


def _remote_copy(src, dst, send_sems, recv_sems, k, to):
    return pltpu.make_async_remote_copy(src_ref=src, dst_ref=dst, send_sem=send_sems.at[k], recv_sem=recv_sems.at[k],
                                        device_id=to, device_id_type=MESH)


def _place_own(w, chip, name):
    _, r, c = w.shape
    tr = _pick(r, 256, 16)

    def body(chip_ref, w_ref, *out_refs):
        for l, o_ref in enumerate(out_refs):
            o_ref[...] = w_ref[l].astype(BF16)

    grid_spec = pltpu.PrefetchScalarGridSpec(
        num_scalar_prefetch=1, grid=(r // tr,), in_specs=[pl.BlockSpec((DEPTH, tr, c), lambda i, ch: (0, i, 0))],
        out_specs=[pl.BlockSpec((None, tr, c), lambda i, ch: (ch[0], i, 0))] * DEPTH)
    return _pcall(body, name=name, grid_spec=grid_spec,
                  out_shape=[jax.ShapeDtypeStruct((N_CHIPS, r, c), BF16)] * DEPTH)(chip, w)


def _gather_over_ici(srcs, bufs, new):
    x, y, c = _position()
    me = 2 * x + y
    return [(b.at[me, c], b.at[me, c], (*chip, c)) for b in bufs for chip, _ in _other_chips(x, y)]


def _gather_to_sibling(srcs, bufs, new):
    x, y, c = _position()
    return [(b.at[idx, c], b.at[idx, c], (x, y, 1 - c)) for b in bufs for _, idx in _other_chips(x, y)]


def _gather_layer(bufs, name):
    n = len(bufs)

    def body(*refs):
        dst = refs[n:2 * n]
        send_sems, recv_sems = refs[2 * n:]
        x, y, c = _position()
        me = 2 * x + y
        sibling = (x, y, 1 - c)
        others = _other_chips(x, y)
        first = [_remote_copy(dst[i].at[me, c], dst[i].at[me, c], send_sems, recv_sems, 3 * i + k, (*chip, c))
                 for i in range(n) for k, (chip, _) in enumerate(others)]
        for cp in first:
            cp.start()
        passed = []
        for i in range(n):
            for k, (_, idx) in enumerate(others):
                landed = dst[i].at[idx, c]
                _remote_copy(landed, landed, send_sems, recv_sems, 3 * i + k, sibling).wait_recv()
                passed.append(_remote_copy(landed, landed, send_sems, recv_sems, 3 * (n + i) + k, sibling))
                passed[-1].start()
        for i in range(n):
            for k, (_, idx) in enumerate(others):
                from_sibling = dst[i].at[idx, 1 - c]
                _remote_copy(from_sibling, from_sibling, send_sems, recv_sems, 3 * (n + i) + k, sibling).wait_recv()
        for cp in first + passed:
            cp.wait_send()

    return _pcall(
        body, name=name, in_specs=[HBM_SPEC] * n, out_specs=[HBM_SPEC] * n,
        out_shape=[jax.ShapeDtypeStruct(b.shape, b.dtype) for b in bufs],
        scratch_shapes=[pltpu.SemaphoreType.DMA((6 * n,)), pltpu.SemaphoreType.DMA((6 * n,))],
        input_output_aliases={i: i for i in range(n)},
    )(*bufs)


def _send_other_half(parts, name):
    n = len(parts)

    def body(*refs):
        src, got = refs[:n], refs[n:2 * n]
        send_sems, recv_sems = refs[2 * n:]
        x, y, c = _position()
        copies = [_remote_copy(src[i].at[1 - c], got[i], send_sems, recv_sems, i, (x, y, 1 - c)) for i in range(n)]
        for cp in copies:
            cp.start()
        for cp in copies:
            cp.wait()

    return _pcall(
        body, name=name, in_specs=[HBM_SPEC] * n, out_specs=[HBM_SPEC] * n,
        out_shape=[jax.ShapeDtypeStruct(p.shape[1:], p.dtype) for p in parts],
        scratch_shapes=[pltpu.SemaphoreType.DMA((n,)), pltpu.SemaphoreType.DMA((n,))],
    )(*parts)


def _scatter_chips(parts, name):
    n = len(parts)

    def body(*refs):
        src, dst = refs[:n], refs[n:2 * n]
        send_sems, recv_sems = refs[2 * n:]
        x, y, c = _position()
        others = _other_chips(x, y)
        sends = [_remote_copy(src[i].at[idx], dst[i].at[k], send_sems, recv_sems, 3 * i + k, (*chip, c))
                 for i in range(n) for k, (chip, idx) in enumerate(others)]
        for cp in sends:
            cp.start()
        for cp in sends:
            cp.wait()

    return _pcall(
        body, name=name, in_specs=[HBM_SPEC] * n, out_specs=[HBM_SPEC] * n,
        out_shape=[jax.ShapeDtypeStruct((3,) + p.shape[1:], p.dtype) for p in parts],
        scratch_shapes=[pltpu.SemaphoreType.DMA((3 * n,)), pltpu.SemaphoreType.DMA((3 * n,))],
    )(*parts)


def _swap_with_sibling(arrs, name):
    n = len(arrs)

    def body(*refs):
        src, dst = refs[:n], refs[n:2 * n]
        send_sems, recv_sems = refs[2 * n:]
        x, y, c = _position()
        copies = [_remote_copy(src[i], dst[i], send_sems, recv_sems, i, (x, y, 1 - c)) for i in range(n)]
        for cp in copies:
            cp.start()
        for cp in copies:
            cp.wait()

    return _pcall(
        body, name=name, in_specs=[HBM_SPEC] * n, out_specs=[HBM_SPEC] * n,
        out_shape=[jax.ShapeDtypeStruct(a.shape, a.dtype) for a in arrs],
        scratch_shapes=[pltpu.SemaphoreType.DMA((n,)), pltpu.SemaphoreType.DMA((n,))],
    )(*arrs)


def _allreduce_small(v, name):
    R, C = v.shape

    def body(v_ref, o_ref, slots, send_sems, recv_sems):
        x, y, c = _position()
        me = 4 * x + 2 * y + c
        slots[0] = v_ref[...]
        sends = []
        for r in range(1, 8):
            fx, fy, fc = (r >> 2) & 1, (r >> 1) & 1, r & 1
            to = (x ^ fx, y ^ fy, c ^ fc)
            cp = pltpu.make_async_remote_copy(src_ref=v_ref, dst_ref=slots.at[r], send_sem=send_sems.at[r - 1],
                                              recv_sem=recv_sems.at[r - 1], device_id=to, device_id_type=MESH)
            cp.start()
            sends.append(cp)
        for cp in sends:
            cp.wait()
        acc = slots[me]
        for d in range(1, 8):
            acc = acc + slots[d ^ me]
        o_ref[...] = acc

    vmem = pl.BlockSpec(memory_space=pltpu.VMEM)
    return _pcall(
        body, name=name, in_specs=[vmem], out_specs=vmem, out_shape=jax.ShapeDtypeStruct((R, C), F32),
        scratch_shapes=[pltpu.VMEM((8, R, C), F32), pltpu.SemaphoreType.DMA((7,)), pltpu.SemaphoreType.DMA((7,))],
    )(v)


def _add_pair(which, both, got, name):
    _, N, R, C = both.shape
    tr = _pick(R, 512, 16)

    def body(which_ref, a_ref, b_ref, o_ref):
        o_ref[...] = (a_ref[...].astype(F32) + b_ref[...].astype(F32)).astype(BF16)

    spec = pl.BlockSpec((None, tr, C), lambda n, i, w: (n, i, 0))
    grid_spec = pltpu.PrefetchScalarGridSpec(
        num_scalar_prefetch=1, grid=(N, R // tr),
        in_specs=[pl.BlockSpec((None, None, tr, C), lambda n, i, w: (w[0], n, i, 0)), spec], out_specs=spec)
    return _pcall(body, name=name, grid_spec=grid_spec,
                  out_shape=jax.ShapeDtypeStruct((N, R, C), BF16))(which, both, got)


def _sum_chips(which, own, others, name):
    N, R, C = others.shape
    tr = _pick(R, 512, 16)

    def body(which_ref, own_ref, p_ref, o_ref):
        acc = own_ref[...].astype(F32)
        for n in range(N):
            acc = acc + p_ref[n].astype(F32)
        o_ref[...] = acc

    grid_spec = pltpu.PrefetchScalarGridSpec(
        num_scalar_prefetch=1, grid=(R // tr,),
        in_specs=[pl.BlockSpec((None, tr, C), lambda i, w: (w[0], i, 0)), pl.BlockSpec((N, tr, C), lambda i, w: (0, i, 0))],
        out_specs=pl.BlockSpec((tr, C), lambda i, w: (i, 0)))
    return _pcall(body, name=name, grid_spec=grid_spec,
                  out_shape=jax.ShapeDtypeStruct((R, C), F32))(which, own, others)


BIG = ("w_ffn1_in", "w_ffn1_out", "w_in", "w_br_sb", "w_br_ch", "w_br_fox", "w_out", "w_ffn2_in", "w_ffn2_out")
ROW_SHARDED = ("w_ffn1_out", "w_out", "w_ffn2_out")
SMALL = ("g_ffn1", "g_mix", "b_in", "rel_bias", "g_ffn2", "g_final")


def _pair_sums(split, tag):
    core = lax.axis_index("c").astype(jnp.int32)[None]
    got = _send_other_half(split, f"grad_split_{tag}")
    return [_add_pair(core, a, b, f"grad_pair_sum_{tag}_{i}") for i, (a, b) in enumerate(zip(split, got))]


def _scatter_plan(srcs, bufs, new):
    x, y, c = _position()
    return [(s.at[idx], d.at[k], (*chip, c)) for s, d in zip(srcs, new) for k, (chip, idx) in enumerate(_other_chips(x, y))]


def _scatter_carry(pair):
    landing = [jax.ShapeDtypeStruct((3,) + p.shape[1:], p.dtype) for p in pair]
    return _Carry(pair, [], landing, 3 * len(pair), _scatter_plan)


def _finish_grads(pair, landed, tag):
    chip = (2 * lax.axis_index("x") + lax.axis_index("y")).astype(jnp.int32)[None]
    mine = [_sum_chips(chip, p, q, f"grad_chip_sum_{tag}_{i}") for i, (p, q) in enumerate(zip(pair, landed))]
    return mine, _swap_with_sibling(mine, f"grad_share_{tag}")


SMALL_SIZES = {"g_ffn1": DEPTH * D_MODEL, "g_mix": DEPTH * D_MODEL, "b_in": DEPTH * IN_WIDTH,
               "rel_bias": DEPTH * N_REL * H_CH, "g_ffn2": DEPTH * D_MODEL, "g_final": D_MODEL}
SMALL_TOTAL = sum(SMALL_SIZES.values()) + 1
SMALL_ROWS = -(-SMALL_TOTAL // (8 * 128)) * 8


def _pack_small(vals, extra):
    flat = [vals[n].reshape(-1) for n in SMALL] + [extra.reshape(-1)]
    flat.append(jnp.zeros((SMALL_ROWS * 128 - SMALL_TOTAL,), F32))
    return jnp.concatenate(flat).reshape(SMALL_ROWS, 128)


def _unpack_small(pack, shapes):
    flat, out, off = pack.reshape(-1), {}, 0
    for n in SMALL:
        out[n] = flat[off:off + SMALL_SIZES[n]].reshape(shapes[n])
        off += SMALL_SIZES[n]
    return out, flat[off]


def _to_heads(t, n_heads):
    return jnp.transpose(t.reshape(t.shape[0], n_heads, HEAD_DIM), (1, 0, 2)).astype(BF16)


def _from_heads(t):
    return jnp.transpose(t, (1, 0, 2)).reshape(t.shape[1], t.shape[0] * HEAD_DIM)


def _pad_keys(t):
    return jnp.pad(t, ((0, 0), (BAND_PAD, 0), (0, 0)))


DIAGONALS = CHUNK + BAND - 1


def _band_bias(table):
    n_far = (LEFT_CHUNKS + 1) * CHUNK + CHUNK - MAX_REL
    near = table[N_REL - 1 - (DIAGONALS - n_far):N_REL - 1][::-1]
    diag = jnp.concatenate([jnp.broadcast_to(table[N_REL - 1], (n_far, H_CH)), near], axis=0).T
    diag = jnp.pad(diag, ((0, 0), (0, 1)))
    skew = jnp.tile(diag, (1, CHUNK))[:, :CHUNK * DIAGONALS].reshape(H_CH, CHUNK, DIAGONALS)
    return skew[:, :, CHUNK - 1:]


def _pad_w_in(w):
    qkv, f, gates = w[:, :QKV_WIDTH], w[:, QKV_WIDTH:QKV_WIDTH + H_FOX], w[:, QKV_WIDTH + H_FOX:]
    return jnp.concatenate([qkv, gates, f, jnp.zeros((w.shape[0], F_PAD - H_FOX), w.dtype)], axis=1)


def _unpad_w_in(w):
    return jnp.concatenate([w[:, :QKV_WIDTH], w[:, QKV_WIDTH + GATE_WIDTH:QKV_WIDTH + GATE_WIDTH + H_FOX],
                            w[:, QKV_WIDTH:QKV_WIDTH + GATE_WIDTH]], axis=1)


QKV_SPLITS = np.cumsum([0, W_SB, W_SB, W_SB, W_CH, W_CH, W_CH, W_FOX, W_FOX, W_FOX])


def _by_chip_rows(g):
    return g.reshape(2, N_CHIPS, g.shape[1] // N_CHIPS, g.shape[2])


def _ffn_fwd(x, g, w_in, w_out, tag, carries):
    h, gate, up, a = _ffn_in_swiglu(x, g, w_in, f"{tag}_in", _carry_of(carries, "in"))
    y = _mm(a, w_out, mode="nn", name=f"{tag}_out", alpha=0.5, res=x, gathered="row",
            carry=_carry_of(carries, "out"))
    return y, (h, gate, up, a)


def _ffn_bwd(x, g, w_in, w_out, saved, dy, tag, carries):
    h, gate, up, a = saved
    da = _mm(dy, w_out, mode="nt", name=f"{tag}_da", alpha=0.5, gathered="row", out_dtype=BF16,
             carry=_carry_of(carries, "da"))
    dw_out = _mm(a, dy, mode="tn", name=f"{tag}_dwout", alpha=0.5, tm_cap=1408, out_dtype=BF16, out_split="row",
                 carry=_carry_of(carries, "dwout"))
    dgu = _swiglu_bwd(gate, up, da, f"{tag}_dact")
    dw_in = _mm(h, dgu, mode="tn", name=f"{tag}_dwin", tm_cap=512, out_dtype=BF16, out_split="col",
                carry=_carry_of(carries, "dwin"))
    dx, dg = _mm(dgu, w_in, mode="nt", name=f"{tag}_dh", gathered="col", tn_cap=1024, carry=_carry_of(carries, "dh"),
                 norm_bwd=(x, g, dy))
    return dx, dg, dw_in, _by_chip_rows(dw_out)


def _mixer_fwd(x, lw, tag, carries):
    T = x.shape[0]
    h = _rmsnorm_fwd(x, lw["g_mix"], f"{tag}_norm")
    p = _mm(h, lw["w_in"], mode="nn", name=f"{tag}_proj", bias=lw["b_in"], tn_cap=896, out_dtype=BF16,
            carry=_carry_of(carries, "proj"))
    parts = [p[:, QKV_SPLITS[i]:QKV_SPLITS[i + 1]] for i in range(9)]
    qa, ka, va = (_to_heads(t, H_SB) for t in parts[0:3])
    qb, kb, vb = (_to_heads(t, H_CH) for t in parts[3:6])
    qc, kc, vc = (_to_heads(t, H_FOX) for t in parts[6:9])
    flog = _mm(lw["w_forget_t"], h, mode="nt", name=f"{tag}_flog")[:8] + lw["b_forget"]
    fcum = _forget_cumsum(flog, f"{tag}_fcum")[:H_FOX]
    fq, fk = fcum.reshape(H_FOX, T, 1), fcum.reshape(H_FOX, T // QB, QB)
    kbp, vbp = _pad_keys(kb), _pad_keys(vb)
    oa = _sb_fwd(qa, ka, va, f"{tag}_sb", _carry_of(carries, "sb"))
    ob = _chunk_fwd(qb, kbp, vbp, lw["bias"], f"{tag}_ch", _carry_of(carries, "ch"))
    oc, lse = _fox_fwd(qc, kc, vc, fq, fk, f"{tag}_fox", _carry_of(carries, "fox"))
    oam, obm, ocm = _from_heads(oa), _from_heads(ob), _from_heads(oc)
    ya, yb, yc, merged = _branches_fwd(p, (oam, obm, ocm), (lw["w_br_sb"], lw["w_br_ch"], lw["w_br_fox"]),
                                       f"{tag}_branches")
    y = _mm(merged, lw["w_out"], mode="nn", name=f"{tag}_out", res=x, gathered="row")
    saved = (h, p, (qa, ka, va), (qb, kbp, vbp), (qc, kc, vc), flog, fq, fk, lse, (oam, obm, ocm),
             (ya, yb, yc), merged)
    return y, saved


def _mixer_bwd(x, lw, saved, dy, tag, carries):
    (h, p, (qa, ka, va), (qb, kbp, vbp), (qc, kc, vc), flog, fq, fk, lse, (oam, obm, ocm),
     (ya, yb, yc), merged) = saved
    T = x.shape[0]
    grads = {}
    dmerged = _mm(dy, lw["w_out"], mode="nt", name=f"{tag}_dmerged", gathered="row",
                  carry=_carry_of(carries, "dmerged"))
    grads["w_out"] = _by_chip_rows(_mm(merged, dy, mode="tn", name=f"{tag}_dwout", tm_cap=1024, out_dtype=BF16,
                                       out_split="row"))
    (dya, dyb, dyc), dgates, (doa, dob, doc) = _branches_bwd(
        p, (ya, yb, yc), dmerged, (lw["w_br_sb"], lw["w_br_ch"], lw["w_br_fox"]), f"{tag}_dbranches")
    grads["w_br_sb"], grads["w_br_ch"], grads["w_br_fox"] = _branches_dw(
        (oam, obm, ocm), (dya, dyb, dyc), f"{tag}_dwbranches")
    dqa, dka, dva = _sb_bwd(qa, ka, va, _to_heads(doa, H_SB), f"{tag}_dsb", _carry_of(carries, "dsb"))
    dqb, dkb, dvb, dbias = _chunk_bwd(qb, kbp, vbp, lw["bias"], _to_heads(dob, H_CH), f"{tag}_dch",
                                      _carry_of(carries, "dch"))
    dqc, dkc, dvc, dfk = _fox_bwd(qc, kc, vc, fq, fk, lse, _to_heads(doc, H_FOX), f"{tag}_dfox",
                                  _carry_of(carries, "dfox"))
    dflog = _forget_cumsum_bwd(flog, jnp.pad(dfk.reshape(H_FOX, T), ((0, 8 - H_FOX), (0, 0))), f"{tag}_dfcum")
    dqkv = [_from_heads(t) for t in (dqa, dka, dva, dqb, dkb, dvb, dqc, dkc, dvc)]
    dforget = jnp.pad(dflog[:H_FOX].T, ((0, 0), (0, F_PAD - H_FOX))).astype(BF16)
    dpb = jnp.concatenate(dqkv + list(dgates) + [dforget], axis=1)
    grads["b_in"] = _colsum(dpb, f"{tag}_dbin")
    dw_in =_unpad_w_in(_mm(h, dpb, mode="tn", name=f"{tag}_dwin", tm_cap=512, tn_cap=896, out_dtype=BF16))
    grads["w_in"] = jnp.transpose(dw_in.reshape(2, D_MODEL // 2, N_CHIPS, IN_WIDTH // N_CHIPS), (0, 2, 1, 3))
    dx, grads["g_mix"] = _mm(dpb, lw["w_in"], mode="nt", name=f"{tag}_dh", tk_cap=896, tn_cap=1024,
                             norm_bwd=(x, lw["g_mix"], dy))
    grads["rel_bias"] = lw["bias_vjp"](dbias)[0]
    return dx, grads


def kernel(x, g_ffn1, w_ffn1_in, w_ffn1_out, g_mix, w_in, b_in, rel_bias, w_br_sb, w_br_ch, w_br_fox, w_out, g_ffn2, w_ffn2_in, w_ffn2_out, g_final, loss_target, m_g_ffn1, m_w_ffn1_in, m_w_ffn1_out, m_g_mix, m_w_in, m_b_in, m_rel_bias, m_w_br_sb, m_w_br_ch, m_w_br_fox, m_w_out, m_g_ffn2, m_w_ffn2_in, m_w_ffn2_out, m_g_final, v_g_ffn1, v_w_ffn1_in, v_w_ffn1_out, v_g_mix, v_w_in, v_b_in, v_rel_bias, v_w_br_sb, v_w_br_ch, v_w_br_fox, v_w_out, v_g_ffn2, v_w_ffn2_in, v_w_ffn2_out, v_g_final):
    names = ("g_ffn1", "w_ffn1_in", "w_ffn1_out", "g_mix", "w_in", "b_in", "rel_bias", "w_br_sb", "w_br_ch",
             "w_br_fox", "w_out", "g_ffn2", "w_ffn2_in", "w_ffn2_out", "g_final")
    w = dict(zip(names, (g_ffn1, w_ffn1_in, w_ffn1_out, g_mix, w_in, b_in, rel_bias, w_br_sb, w_br_ch,
                         w_br_fox, w_out, g_ffn2, w_ffn2_in, w_ffn2_out, g_final)))
    m = dict(zip(names, (m_g_ffn1, m_w_ffn1_in, m_w_ffn1_out, m_g_mix, m_w_in, m_b_in, m_rel_bias, m_w_br_sb,
                         m_w_br_ch, m_w_br_fox, m_w_out, m_g_ffn2, m_w_ffn2_in, m_w_ffn2_out, m_g_final)))
    v = dict(zip(names, (v_g_ffn1, v_w_ffn1_in, v_w_ffn1_out, v_g_mix, v_w_in, v_b_in, v_rel_bias, v_w_br_sb,
                         v_w_br_ch, v_w_br_fox, v_w_out, v_g_ffn2, v_w_ffn2_in, v_w_ffn2_out, v_g_final)))
    xs = x[0]
    tgt = loss_target[0]

    chip = (2 * lax.axis_index("x") + lax.axis_index("y")).astype(jnp.int32)[None]
    placed = [_place_own(w[n], chip, f"place_{n}") for n in BIG]
    bufs = [[p[l].reshape(N_CHIPS, 2, p[l].shape[1] // 2, p[l].shape[2]) for p in placed] for l in range(DEPTH)]
    padded_w_in = {}

    def layer_weights(l):
        lw = {n: b.reshape((N_CHIPS,) + w[n].shape[1:]) for n, b in zip(BIG, bufs[l])}
        if l not in padded_w_in:
            whole = jnp.concatenate([lw["w_in"][j] for j in range(N_CHIPS)], axis=1)
            forget_t = jnp.pad(whole[:, QKV_WIDTH:QKV_WIDTH + H_FOX].T, ((0, F_PAD - H_FOX), (0, 0)))
            padded_w_in[l] = (_pad_w_in(whole), forget_t)
        lw["w_in"], lw["w_forget_t"] = padded_w_in[l]
        lw["b_forget"] = jnp.pad(w["b_in"][l][QKV_WIDTH:QKV_WIDTH + H_FOX], (0, 8 - H_FOX))[:, None]
        lw["b_in"] = _pad_w_in(w["b_in"][l][None, :])
        lw["bias"], lw["bias_vjp"] = jax.vjp(_band_bias, w["rel_bias"][l])
        for n in ("g_ffn1", "g_mix", "g_ffn2"):
            lw[n] = w[n][l][None, :]
        return lw

    def landed_in(l, idx):
        def done(carry):
            for i, b in zip(idx, carry.out[0]):
                bufs[l][i] = b
        return done

    def over_ici(l, idx):
        return lambda: _Carry([], [bufs[l][i] for i in idx], [], 3 * len(idx), _gather_over_ici, landed_in(l, idx))

    def to_sibling(l, idx):
        return lambda: _Carry([], [bufs[l][i] for i in idx], [], 3 * len(idx), _gather_to_sibling, landed_in(l, idx))

    def ffn_fwd(x_in, lw, which, tag, carries):
        return _ffn_fwd(x_in, lw[f"g_{which}"], lw[f"w_{which}_in"], lw[f"w_{which}_out"], tag, carries)

    def ffn_bwd(x_in, lw, which, saved_acts, dy, tag, carries):
        return _ffn_bwd(x_in, lw[f"g_{which}"], lw[f"w_{which}_in"], lw[f"w_{which}_out"], saved_acts, dy, tag,
                        carries)

    FFN1, W_IN, MIXER_REST, FFN2_IN, FFN2_OUT = (0, 1), (2,), (3, 4, 5, 6), (7,), (8,)
    first = FFN1 + W_IN
    for i, b in zip(first, _gather_layer([bufs[0][i] for i in first], "gather_l0")):
        bufs[0][i] = b
    fwd_carries = [
        {"ffn1": {"in": [over_ici(0, MIXER_REST)], "out": [to_sibling(0, MIXER_REST), over_ici(0, FFN2_OUT)]},
         "mix": {"proj": [to_sibling(0, FFN2_OUT), over_ici(1, MIXER_REST)],
                 "sb": [over_ici(0, FFN2_IN), over_ici(1, FFN2_OUT)],
                 "ch": [to_sibling(0, FFN2_IN), over_ici(1, FFN1)],
                 "fox": [over_ici(1, W_IN)]},
         "ffn2": {"in": [to_sibling(1, MIXER_REST + FFN2_OUT + FFN1 + W_IN)]}},
        {"ffn1": {}, "mix": {"sb": [over_ici(1, FFN2_IN)], "ch": [to_sibling(1, FFN2_IN)]}, "ffn2": {}},
    ]

    saved = []
    act = xs
    for l in range(DEPTH):
        x0 = act
        x1, s1 = ffn_fwd(x0, layer_weights(l), "ffn1", f"l{l}_ffn1", fwd_carries[l]["ffn1"])
        x2, s2 = _mixer_fwd(x1, layer_weights(l), f"l{l}_mix", fwd_carries[l]["mix"])
        act, s3 = ffn_fwd(x2, layer_weights(l), "ffn2", f"l{l}_ffn2", fwd_carries[l]["ffn2"])
        saved.append((x0, s1, x1, s2, x2, s3))
    layers = [layer_weights(l) for l in range(DEPTH)]

    dact, dg_final, loss_row = _final_loss(act, w["g_final"][None, :], tgt, "final_loss")

    big_grads = {n: [None] * DEPTH for n in BIG}
    small_grads = {n: [None] * DEPTH for n in ("g_ffn1", "g_mix", "b_in", "rel_bias", "g_ffn2")}
    pair = [[None] * len(BIG) for _ in range(DEPTH)]
    landed = [[None] * len(BIG) for _ in range(DEPTH)]

    def pair_up(l, idx, tag):
        for i, p in zip(idx, _pair_sums([big_grads[BIG[i]][l] for i in idx], tag)):
            pair[l][i] = p

    core = lax.axis_index("c").astype(jnp.int32)[None]

    def to_sibling_half(l, idx, tag):
        def plan(srcs, bufs, new):
            x, y, c = _position()
            return [(s.at[1 - c], d, (x, y, 1 - c)) for s, d in zip(srcs, new)]
        def done(carry):
            for j, (i, got) in enumerate(zip(idx, carry.out[1])):
                pair[l][i] = _add_pair(core, big_grads[BIG[i]][l], got, f"grad_pair_sum_{tag}_{j}")
        def make():
            split = [big_grads[BIG[i]][l] for i in idx]
            return _Carry(split, [], [jax.ShapeDtypeStruct(s.shape[1:], s.dtype) for s in split], len(idx), plan, done)
        return make

    def exchange(l, idx):
        def done(carry):
            for i, a in zip(idx, carry.out[1]):
                landed[l][i] = a
        def make():
            carry = _scatter_carry([pair[l][i] for i in idx])
            carry.done = done
            return carry
        return make

    def exchange_one_way(l, i, k):
        def plan(srcs, bufs, new):
            x, y, c = _position()
            chip_k, idx_k = _other_chips(x, y)[k]
            return [(srcs[0].at[idx_k], bufs[0].at[k], (*chip_k, c))]
        def done(carry):
            landed[l][i] = carry.out[0][0]
        def make():
            if landed[l][i] is None:
                landed[l][i] = lax.empty((3,) + pair[l][i].shape[1:], BF16)
            return _Carry([pair[l][i]], [landed[l][i]], [], 1, plan, done)
        return make

    bwd_carries = [
        {"ffn2": {},
         "mix": {"dmerged": [to_sibling_half(0, FFN2_IN + FFN2_OUT, "l0_ffn2")],
                 "dsb": [exchange(1, FFN1 + W_IN)], "dch": [exchange(1, MIXER_REST + FFN2_IN + FFN2_OUT)],
                 "dfox": [exchange(0, FFN2_IN + FFN2_OUT)]},
         "ffn1": {"da": [exchange(0, MIXER_REST)], "dwout": [exchange_one_way(0, 2, 0)],
                  "dwin": [exchange_one_way(0, 2, 1)], "dh": [exchange_one_way(0, 2, 2)]}},
        {"ffn2": {}, "mix": {},
         "ffn1": {"da": [to_sibling_half(1, W_IN + MIXER_REST + FFN2_IN + FFN2_OUT, "l1_rest")]}},
    ]
    for l in reversed(range(DEPTH)):
        lw = layers[l]
        x0, s1, x1, s2, x2, s3 = saved[l]
        dx2, small_grads["g_ffn2"][l], big_grads["w_ffn2_in"][l], big_grads["w_ffn2_out"][l] = ffn_bwd(
            x2, lw, "ffn2", s3, dact, f"l{l}_ffn2", bwd_carries[l]["ffn2"])
        dx1, mg = _mixer_bwd(x1, lw, s2, dx2, f"l{l}_mix", bwd_carries[l]["mix"])
        for n in ("w_in", "w_br_sb", "w_br_ch", "w_br_fox", "w_out"):
            big_grads[n][l] = mg[n]
        small_grads["b_in"][l] = _unpad_w_in(mg["b_in"])[0]
        small_grads["g_mix"][l] = mg["g_mix"][0]
        small_grads["rel_bias"][l] = mg["rel_bias"]
        if l == 0:
            pair_up(0, W_IN + MIXER_REST, "l0_mix")
        dact, dg1, big_grads["w_ffn1_in"][l], big_grads["w_ffn1_out"][l] = ffn_bwd(
            x0, lw, "ffn1", s1, dx1, f"l{l}_ffn1", bwd_carries[l]["ffn1"])
        small_grads["g_ffn1"][l] = dg1[0]
        small_grads["g_ffn2"][l] = small_grads["g_ffn2"][l][0]
        if l == 1:
            pair_up(1, FFN1, "l1_ffn1")
    grad_x = dact[None]
    pair_up(0, FFN1, "l0_ffn1")
    for i, a in zip(FFN1, _scatter_chips([pair[0][i] for i in FFN1], "grad_scatter_l0")):
        landed[0][i] = a

    small = {n: jnp.stack(small_grads[n]) for n in small_grads}
    small["g_final"] = dg_final[0]
    small_sum, loss = _unpack_small(_allreduce_small(_pack_small(small, loss_row[0, :1]), "allreduce_small"),
                                    {n: w[n].shape for n in SMALL})

    mine, theirs = _finish_grads(pair[0] + pair[1], landed[0] + landed[1], "all")

    grads, delta, new_m, new_v = dict(small_sum), {}, {}, {}
    for i, n in enumerate(BIG):
        halves = [(mine[l * len(BIG) + i], theirs[l * len(BIG) + i]) for l in range(DEPTH)]
        grads[n], delta[n], new_m[n], new_v[n] = _adamw_from_halves(
            w[n], m[n], v[n], halves, 1 if n in ROW_SHARDED else 0, core, f"adamw_{n}")
    zero = jnp.zeros((1,), F32)
    sd, sm, sv = _adamw(_pack_small(w, zero), _pack_small(grads, zero), _pack_small(m, zero), _pack_small(v, zero),
                        "adamw_small")
    shapes = {n: w[n].shape for n in SMALL}
    for dst, src in ((delta, sd), (new_m, sm), (new_v, sv)):
        dst.update(_unpack_small(src, shapes)[0])

    return (loss, grad_x, *[grads[n] for n in names], *[delta[n] for n in names],
            *[new_m[n] for n in names], *[new_v[n] for n in names])
```

```python
import functools

import numpy as np
import jax
import jax.numpy as jnp
from jax import lax
from jax.experimental import pallas as pl
from jax.experimental.pallas import tpu as pltpu

F32 = jnp.float32
BF16 = jnp.bfloat16

D_MODEL = 1024
DEPTH = 2
CHUNK = 64
HEAD_DIM = 64
H_SB, H_CH, H_FOX = 4, 8, 4
W_SB, W_CH, W_FOX = H_SB * HEAD_DIM, H_CH * HEAD_DIM, H_FOX * HEAD_DIM
LEFT_CHUNKS = 8
MAX_REL = 128
N_REL = 2 * MAX_REL + 1
D_FF = 2816
QKV_WIDTH = 3 * (W_SB + W_CH + W_FOX)
GATE_WIDTH = 3 * D_MODEL
IN_WIDTH = QKV_WIDTH + H_FOX + GATE_WIDTH
F_PAD = 128
P_WIDTH = QKV_WIDTH + GATE_WIDTH + F_PAD
RMS_EPS = 1e-6
NEG = -1e30
SCALE = HEAD_DIM ** -0.5
QB = 256
BAND = (LEFT_CHUNKS + 2) * CHUNK
BAND_PAD = BAND - CHUNK

ADAM_LR, ADAM_B1, ADAM_B2, ADAM_EPS, ADAM_WD, ADAM_STEP = 0.001, 0.9, 0.999, 1e-08, 0.01, 10

N_CHIPS = 4
VMEM_BUDGET = 52 * 1024 * 1024

NT = (((1,), (1,)), ((), ()))
TN = (((0,), (0,)), ((), ()))
NN = (((1,), (0,)), ((), ()))
MESH = pl.DeviceIdType.MESH


def _pcall(body, **kw):
    return pl.pallas_call(body, **kw)


class _Carry:
    def __init__(self, srcs, bufs, new, count, plan, done=None):
        self.srcs, self.bufs, self.new, self.count, self.plan = list(srcs), list(bufs), list(new), count, plan
        self.out, self.done = None, done

    @staticmethod
    def join(parts):
        parts = [p for p in parts if p is not None]
        if not parts:
            return None

        def plan(srcs, bufs, new):
            copies, s, b, n = [], 0, 0, 0
            for p in parts:
                copies += p.plan(srcs[s:s + len(p.srcs)], bufs[b:b + len(p.bufs)], new[n:n + len(p.new)])
                s, b, n = s + len(p.srcs), b + len(p.bufs), n + len(p.new)
            return copies

        whole = _Carry(sum((p.srcs for p in parts), []), sum((p.bufs for p in parts), []),
                       sum((p.new for p in parts), []), sum(p.count for p in parts), plan)
        whole.parts = parts
        return whole

    def share_out(self):
        b, n = 0, 0
        for p in getattr(self, "parts", []):
            p.out = (self.out[0][b:b + len(p.bufs)], self.out[1][n:n + len(p.new)])
            b, n = b + len(p.bufs), n + len(p.new)
            p.share_out()
        if self.done is not None:
            self.done(self)


def _carry_of(carries, key):
    return _Carry.join([make() for make in carries.get(key, [])])


def _pcall_carrying(body, carry, **kw):
    if carry is None:
        return _pcall(body, **kw)
    in_specs = list(kw.pop("in_specs"))
    out_specs, out_shape = kw.pop("out_specs"), kw.pop("out_shape")
    single = not isinstance(out_shape, (list, tuple))
    out_specs = [out_specs] if single else list(out_specs)
    out_shape = [out_shape] if single else list(out_shape)
    scratch = list(kw.pop("scratch_shapes", []))
    grid = tuple(kw.get("grid", ()))
    n_in, n_out, n_scr = len(in_specs), len(out_specs), len(scratch)
    ns, nb, nn = len(carry.srcs), len(carry.bufs), len(carry.new)

    def body2(*refs):
        ins, srcs = refs[:n_in], refs[n_in:n_in + ns]
        at = n_in + ns + nb
        outs, bufs, new = refs[at:at + n_out], refs[at + n_out:at + n_out + nb], refs[at + n_out + nb:at + n_out + nb + nn]
        at += n_out + nb + nn
        scr, (send_sems, recv_sems) = refs[at:at + n_scr], refs[at + n_scr:]

        def copies():
            return [_remote_copy(s, d, send_sems, recv_sems, k, to)
                    for k, (s, d, to) in enumerate(carry.plan(srcs, bufs, new))]

        def start():
            for cp in copies():
                cp.start()

        def wait():
            for cp in copies():
                cp.wait()

        if grid:
            ids = [pl.program_id(a) for a in range(len(grid))]
            pl.when(functools.reduce(jnp.logical_and, [i == 0 for i in ids]))(start)
        else:
            start()
        body(*ins, *outs, *scr)
        if grid:
            pl.when(functools.reduce(jnp.logical_and, [i == g - 1 for i, g in zip(ids, grid)]))(wait)
        else:
            wait()

    call = _pcall(
        body2, in_specs=in_specs + [HBM_SPEC] * (ns + nb), out_specs=out_specs + [HBM_SPEC] * (nb + nn),
        out_shape=out_shape + [jax.ShapeDtypeStruct(b.shape, b.dtype) for b in carry.bufs] + carry.new,
        scratch_shapes=scratch + [pltpu.SemaphoreType.DMA((carry.count,)), pltpu.SemaphoreType.DMA((carry.count,))],
        input_output_aliases={n_in + ns + j: n_out + j for j in range(nb)}, **kw)

    def apply(*args):
        res = call(*args, *carry.srcs, *carry.bufs)
        carry.out = (list(res[n_out:n_out + nb]), list(res[n_out + nb:]))
        carry.share_out()
        return res[0] if single else list(res[:n_out])

    return apply


def _pick(n, cap, mult=128):
    best = None
    d = mult
    while d <= min(n, cap):
        if n % d == 0:
            best = d
        d += mult
    return n if best is None else best


def _nbytes(shape, dtype):
    return int(np.prod(shape)) * jnp.dtype(dtype).itemsize


def _mm(a, b, *, mode, name, out_dtype=F32, alpha=1.0, bias=None, res=None, tm_cap=1024, tn_cap=512,
        tk_cap=2816, gathered=None, out_split=None, carry=None, norm_bwd=None):
    if mode == "tn":
        K, M = a.shape
    else:
        M, K = a.shape
    dn = {"nn": NN, "nt": NT, "tn": TN}[mode]
    b_rows = None
    if gathered is None:
        N = b.shape[0] if mode == "nt" else b.shape[1]
        tn = {None: _pick(N, tn_cap), "col": N // N_CHIPS, "row": N // 2}[out_split]
        if out_split == "col":
            tm_cap = min(tm_cap, M // 2)
        tk = K if K <= tk_cap else _pick(K, tk_cap)
        b_spec = (pl.BlockSpec((tn, tk), lambda i, j, k: (j, k)) if mode == "nt"
                  else pl.BlockSpec((tk, tn), lambda i, j, k: (k, j)))
    else:
        r, c = b.shape[1:]
        rows, cols = (r, N_CHIPS * c) if gathered == "col" else (N_CHIPS * r, c)
        N = rows if mode == "nt" else cols
        assert K == (cols if mode == "nt" else rows), (name, K, rows, cols)
        if gathered == "col" and mode == "nn":
            tn, tk = c, (r if r <= tk_cap else _pick(r, tk_cap))
            b_spec = pl.BlockSpec((None, tk, c), lambda i, j, k: (j, k, 0))
        elif gathered == "col":
            tn, tk = _pick(r, tn_cap), c
            b_spec = pl.BlockSpec((None, tn, c), lambda i, j, k: (k, j, 0))
        elif mode == "nn":
            tn, tk, b_rows = _pick(c, tn_cap), K, N_CHIPS
            b_spec = pl.BlockSpec((N_CHIPS, r, tn), lambda i, j, k: (0, 0, j))
        else:
            tn, tk, b_rows = 2 * r, K, 2
            b_spec = pl.BlockSpec((2, r, c), lambda i, j, k: (j, 0, 0))
    tm = _pick(M, tm_cap)
    nk = K // tk

    a_spec = (pl.BlockSpec((tk, tm), lambda i, j, k: (k, i)) if mode == "tn"
              else pl.BlockSpec((tm, tk), lambda i, j, k: (i, k)))
    in_specs = [a_spec, b_spec]
    args = [a, b]
    if bias is not None:
        in_specs.append(pl.BlockSpec((1, tn), lambda i, j, k: (0, j)))
        args.append(bias)
    if res is not None:
        in_specs.append(pl.BlockSpec((tm, tn), lambda i, j, k: (i, j)))
        args.append(res)
    if norm_bwd is not None:
        assert tn == N and alpha == 1.0 and out_split is None and bias is None and res is None, name
        x_in, g_in, dres_in = norm_bwd
        in_specs += [pl.BlockSpec((tm, tn), lambda i, j, k: (i, 0)), pl.BlockSpec((1, tn), lambda i, j, k: (0, 0)),
                     pl.BlockSpec((tm, tn), lambda i, j, k: (i, 0))]
        args += [x_in, g_in, dres_in]

    est = 2 * (_nbytes((tm, tk), a.dtype) + _nbytes((tk, tn), b.dtype) + _nbytes((tm, tn), out_dtype))
    est += _nbytes((tm, tn), F32) * (3 if res is not None else 1)
    est += _nbytes((tm, tn), F32) * (4 if norm_bwd is not None else 0)
    assert est < VMEM_BUDGET, (name, est)

    def body(*refs):
        a_ref, b_ref = refs[0], refs[1]
        pos = 2
        bias_ref = res_ref = None
        if bias is not None:
            bias_ref = refs[pos]
            pos += 1
        if res is not None:
            res_ref = refs[pos]
            pos += 1
        if norm_bwd is not None:
            x_ref, g_ref, dres_ref = refs[pos:pos + 3]
            pos += 3
        o_ref = refs[pos]
        if norm_bwd is not None:
            dg_ref = refs[pos + 1]
            pos += 1
        acc_ref = refs[pos + 1] if nk > 1 else None

        bv = b_ref[...]
        if b_rows is not None:
            bv = bv.reshape(b_rows * bv.shape[1], bv.shape[2])
        part = lax.dot_general(a_ref[...].astype(BF16), bv.astype(BF16), dn, preferred_element_type=F32)

        def finish(acc):
            if alpha != 1.0:
                acc = acc * alpha
            if bias_ref is not None:
                acc = acc + bias_ref[...]
            if res_ref is not None:
                acc = acc + res_ref[...]
            if norm_bwd is not None:
                xv = x_ref[...]
                rstd = lax.rsqrt(jnp.mean(xv * xv, axis=-1, keepdims=True) + RMS_EPS)
                xhat = xv * rstd
                dyg = acc * g_ref[...]
                part_g = jnp.sum(acc * xhat, axis=0, keepdims=True)
                acc = dres_ref[...] + rstd * (dyg - xhat * jnp.mean(dyg * xhat, axis=-1, keepdims=True))
                first = pl.program_id(0) == 0

                @pl.when(first)
                def _():
                    dg_ref[...] = part_g

                @pl.when(jnp.logical_not(first))
                def _():
                    dg_ref[...] += part_g
            o_ref[...] = acc.astype(out_dtype)

        if nk == 1:
            finish(part)
        else:
            k = pl.program_id(2)

            @pl.when(k == 0)
            def _():
                acc_ref[...] = part

            @pl.when(k > 0)
            def _():
                acc_ref[...] += part

            @pl.when(k == nk - 1)
            def _():
                finish(acc_ref[...])

    if out_split == "col":
        per_half = M // 2 // tm
        out_spec = pl.BlockSpec((None, None, tm, tn), lambda i, j, k: (i // per_half, j, i % per_half, 0))
        out_shape = jax.ShapeDtypeStruct((2, N_CHIPS, M // 2, tn), out_dtype)
    elif out_split == "row":
        out_spec = pl.BlockSpec((None, tm, tn), lambda i, j, k: (j, i, 0))
        out_shape = jax.ShapeDtypeStruct((2, M, tn), out_dtype)
    else:
        out_spec = pl.BlockSpec((tm, tn), lambda i, j, k: (i, j))
        out_shape = jax.ShapeDtypeStruct((M, N), out_dtype)
    semantics = ("parallel", "parallel", "arbitrary")
    if norm_bwd is not None:
        out_spec = [out_spec, pl.BlockSpec((1, tn), lambda i, j, k: (0, 0))]
        out_shape = [out_shape, jax.ShapeDtypeStruct((1, N), F32)]
        semantics = ("arbitrary", "arbitrary", "arbitrary")
    return _pcall_carrying(
        body, carry, name=name, grid=(M // tm, N // tn, nk), in_specs=in_specs, out_specs=out_spec,
        out_shape=out_shape,
        scratch_shapes=[pltpu.VMEM((tm, tn), F32)] if nk > 1 else [],
        compiler_params=pltpu.CompilerParams(dimension_semantics=semantics),
    )(*args)


def _rmsnorm_fwd(x, g, name):
    T, Dm = x.shape
    tm = _pick(T, 256, 8)

    def body(x_ref, g_ref, h_ref):
        xv = x_ref[...]
        rstd = lax.rsqrt(jnp.mean(xv * xv, axis=-1, keepdims=True) + RMS_EPS)
        h_ref[...] = (xv * rstd * g_ref[...]).astype(BF16)

    return _pcall(
        body, name=name, grid=(T // tm,),
        in_specs=[pl.BlockSpec((tm, Dm), lambda i: (i, 0)), pl.BlockSpec((1, Dm), lambda i: (0, 0))],
        out_specs=pl.BlockSpec((tm, Dm), lambda i: (i, 0)),
        out_shape=jax.ShapeDtypeStruct((T, Dm), BF16),
    )(x, g)


def _final_loss(x, g, tgt, name):
    T, Dm = x.shape
    tm = _pick(T, 256, 8)

    def body(x_ref, g_ref, t_ref, dx_ref, dg_ref, loss_ref):
        xv = x_ref[...]
        gv = g_ref[...]
        rstd = lax.rsqrt(jnp.mean(xv * xv, axis=-1, keepdims=True) + RMS_EPS)
        xhat = xv * rstd
        err = xhat * gv - t_ref[...]
        part_loss = 0.5 * jnp.sum(jnp.mean(err * err, axis=-1, keepdims=True), axis=0, keepdims=True)
        dy = err * (1.0 / Dm)
        dyg = dy * gv
        dx_ref[...] = rstd * (dyg - xhat * jnp.mean(dyg * xhat, axis=-1, keepdims=True))
        part_g = jnp.sum(dy * xhat, axis=0, keepdims=True)
        part_l = jnp.broadcast_to(part_loss, (1, 128))

        @pl.when(pl.program_id(0) == 0)
        def _():
            dg_ref[...] = part_g
            loss_ref[...] = part_l

        @pl.when(pl.program_id(0) > 0)
        def _():
            dg_ref[...] += part_g
            loss_ref[...] += part_l

    row = pl.BlockSpec((tm, Dm), lambda i: (i, 0))
    vec = pl.BlockSpec((1, Dm), lambda i: (0, 0))
    return _pcall(
        body, name=name, grid=(T // tm,), in_specs=[row, vec, row],
        out_specs=[row, vec, pl.BlockSpec((1, 128), lambda i: (0, 0))],
        out_shape=[jax.ShapeDtypeStruct((T, Dm), F32), jax.ShapeDtypeStruct((1, Dm), F32),
                   jax.ShapeDtypeStruct((1, 128), F32)],
        compiler_params=pltpu.CompilerParams(dimension_semantics=("arbitrary",)),
    )(x, g, tgt)


def _sigmoid(z):
    return 1.0 / (1.0 + jnp.exp(-z))


def _ffn_in_swiglu(x, g, w_in, name, carry=None):
    T, Dm = x.shape
    cw = w_in.shape[2]
    tm = _pick(T, 512, 16)

    def body(x_ref, gain_ref, wg_ref, wu_ref, h_ref, g_ref, u_ref, a_ref):
        @pl.when(pl.program_id(1) == 0)
        def _():
            xv = x_ref[...]
            rstd = lax.rsqrt(jnp.mean(xv * xv, axis=-1, keepdims=True) + RMS_EPS)
            h_ref[...] = (xv * rstd * gain_ref[...]).astype(BF16)

        hv = h_ref[...]
        gv = jnp.dot(hv, wg_ref[...], preferred_element_type=F32)
        uv = jnp.dot(hv, wu_ref[...], preferred_element_type=F32)
        g_ref[...] = gv.astype(BF16)
        u_ref[...] = uv.astype(BF16)
        a_ref[...] = (gv * _sigmoid(gv) * uv).astype(BF16)

    row = pl.BlockSpec((tm, Dm), lambda i, j: (i, 0))
    out = pl.BlockSpec((tm, cw), lambda i, j: (i, j))
    return _pcall_carrying(
        body, carry, name=name, grid=(T // tm, 2),
        in_specs=[row, pl.BlockSpec((1, Dm), lambda i, j: (0, 0)), pl.BlockSpec((None, Dm, cw), lambda i, j: (j, 0, 0)),
                  pl.BlockSpec((None, Dm, cw), lambda i, j: (j + 2, 0, 0))],
        out_specs=[row, out, out, out],
        out_shape=[jax.ShapeDtypeStruct((T, Dm), BF16)] + [jax.ShapeDtypeStruct((T, D_FF), BF16)] * 3,
        compiler_params=pltpu.CompilerParams(dimension_semantics=("parallel", "arbitrary")),
    )(x, g, w_in, w_in)


def _swiglu_bwd(g, u, da, name):
    T = g.shape[0]
    tm = _pick(T, 256, 16)

    def body(g_ref, u_ref, da_ref, d_ref):
        gv = g_ref[...].astype(F32)
        uv = u_ref[...].astype(F32)
        dav = da_ref[...].astype(F32)
        sg = _sigmoid(gv)
        d_ref[:, :D_FF] = (dav * uv * (sg + gv * sg * (1.0 - sg))).astype(BF16)
        d_ref[:, D_FF:] = (dav * gv * sg).astype(BF16)

    row = pl.BlockSpec((tm, D_FF), lambda i: (i, 0))
    return _pcall(
        body, name=name, grid=(T // tm,), in_specs=[row, row, row],
        out_specs=pl.BlockSpec((tm, 2 * D_FF), lambda i: (i, 0)),
        out_shape=jax.ShapeDtypeStruct((T, 2 * D_FF), BF16),
    )(g, u, da)


def _branches_fwd(p, outs, weights, name):
    T = p.shape[0]
    cw = weights[0].shape[2]
    tm = _pick(T, 1024, 16)
    first_gate, per_branch = QKV_WIDTH // cw, D_MODEL // cw

    def body(*refs):
        gates, o_refs, w_refs, y_refs, m_ref = refs[0:3], refs[3:6], refs[6:9], refs[9:12], refs[12]
        merged = None
        for g_ref, o_ref, w_ref, y_ref in zip(gates, o_refs, w_refs, y_refs):
            y = jnp.dot(o_ref[...], w_ref[...], preferred_element_type=F32)
            y_ref[...] = y
            term = _sigmoid(g_ref[...].astype(F32)) * y
            merged = term if merged is None else merged + term
        m_ref[...] = merged.astype(BF16)

    gate_specs = [pl.BlockSpec((tm, cw), functools.partial(
        lambda i, j, kk: (i, first_gate + per_branch * kk + j), kk=kk)) for kk in range(3)]
    o_specs = [pl.BlockSpec((tm, o.shape[1]), lambda i, j: (i, 0)) for o in outs]
    w_specs = [pl.BlockSpec((None, wk.shape[1], cw), lambda i, j: (j, 0, 0)) for wk in weights]
    tile = pl.BlockSpec((tm, cw), lambda i, j: (i, j))
    return _pcall(
        body, name=name, grid=(T // tm, N_CHIPS), in_specs=gate_specs + o_specs + w_specs, out_specs=[tile] * 4,
        out_shape=[jax.ShapeDtypeStruct((T, D_MODEL), F32)] * 3 + [jax.ShapeDtypeStruct((T, D_MODEL), BF16)],
    )(p, p, p, *outs, *weights)


def _branches_bwd(p, ys, dm, weights, name):
    T = p.shape[0]
    cw = weights[0].shape[2]
    tm = _pick(T, 1024, 16)
    first_gate, per_branch = QKV_WIDTH // cw, D_MODEL // cw
    widths = [wk.shape[1] for wk in weights]

    def body(*refs):
        gates, y_refs, dm_ref, w_refs = refs[0:3], refs[3:6], refs[6], refs[7:10]
        dy_refs, dg_refs, do_refs, acc_refs = refs[10:13], refs[13:16], refs[16:19], refs[19:22]
        j = pl.program_id(1)
        dmv = dm_ref[...]
        for g_ref, y_ref, w_ref, dy_ref, dg_ref, do_ref, acc_ref in zip(gates, y_refs, w_refs, dy_refs, dg_refs,
                                                                       do_refs, acc_refs):
            s = _sigmoid(g_ref[...].astype(F32))
            dy = (s * dmv).astype(BF16)
            dy_ref[...] = dy
            dg_ref[...] = (dmv * y_ref[...] * s * (1.0 - s)).astype(BF16)
            part = _dot_nt(dy, w_ref[...])

            @pl.when(j == 0)
            def _():
                acc_ref[...] = part

            @pl.when(j > 0)
            def _():
                acc_ref[...] += part

            @pl.when(j == N_CHIPS - 1)
            def _():
                do_ref[...] = acc_ref[...].astype(BF16)

    gate_specs = [pl.BlockSpec((tm, cw), functools.partial(
        lambda i, j, kk: (i, first_gate + per_branch * kk + j), kk=kk)) for kk in range(3)]
    tile = pl.BlockSpec((tm, cw), lambda i, j: (i, j))
    w_specs = [pl.BlockSpec((None, r, cw), lambda i, j: (j, 0, 0)) for r in widths]
    do_specs = [pl.BlockSpec((tm, r), lambda i, j: (i, 0)) for r in widths]
    wide = jax.ShapeDtypeStruct((T, D_MODEL), BF16)
    out = _pcall(
        body, name=name, grid=(T // tm, N_CHIPS), in_specs=gate_specs + [tile] * 4 + w_specs,
        out_specs=[tile] * 6 + do_specs,
        out_shape=[wide] * 6 + [jax.ShapeDtypeStruct((T, r), BF16) for r in widths],
        scratch_shapes=[pltpu.VMEM((tm, r), F32) for r in widths],
        compiler_params=pltpu.CompilerParams(dimension_semantics=("parallel", "arbitrary")),
    )(p, p, p, *ys, dm, *weights)
    return out[0:3], out[3:6], out[6:9]


def _branches_dw(outs, dys, name):
    T = dys[0].shape[0]
    cw = D_MODEL // N_CHIPS
    widths = [o.shape[1] for o in outs]

    def body(*refs):
        o_refs, dy_refs, dw_refs = refs[0:3], refs[3:6], refs[6:9]
        for o_ref, dy_ref, dw_ref, r in zip(o_refs, dy_refs, dw_refs, widths):
            dw = _dot_tn(o_ref[...], dy_ref[...]).astype(BF16)
            dw_ref[0] = dw[:r // 2]
            dw_ref[1] = dw[r // 2:]

    return _pcall(
        body, name=name, grid=(N_CHIPS,),
        in_specs=[pl.BlockSpec((T, r), lambda j: (0, 0)) for r in widths] + [pl.BlockSpec((T, cw), lambda j: (0, j))] * 3,
        out_specs=[pl.BlockSpec((2, None, r // 2, cw), lambda j: (0, j, 0, 0)) for r in widths],
        out_shape=[jax.ShapeDtypeStruct((2, N_CHIPS, r // 2, cw), BF16) for r in widths],
    )(*outs, *dys)


def _colsum(a, name):
    ones = jnp.ones((8, a.shape[0]), BF16)
    return _mm(ones, a, mode="nn", name=name, tn_cap=896)[:1]


def _adamw(w, g, m, v, name):
    R, C = w.shape
    tr = 256 if R % 256 == 0 else R
    c1 = 1.0 / (1.0 - ADAM_B1 ** ADAM_STEP)
    c2 = 1.0 / (1.0 - ADAM_B2 ** ADAM_STEP)

    def body(w_ref, g_ref, m_ref, v_ref, d_ref, nm_ref, nv_ref):
        gv = g_ref[...]
        mn = ADAM_B1 * m_ref[...] + (1.0 - ADAM_B1) * gv
        vn = ADAM_B2 * v_ref[...] + (1.0 - ADAM_B2) * (gv * gv)
        nm_ref[...] = mn
        nv_ref[...] = vn
        d_ref[...] = -ADAM_LR * ((mn * c1) / (jnp.sqrt(vn * c2) + ADAM_EPS) + ADAM_WD * w_ref[...])

    spec = pl.BlockSpec((tr, C), lambda i: (i, 0))
    return _pcall(
        body, name=name, grid=(R // tr,), in_specs=[spec] * 4, out_specs=[spec] * 3,
        out_shape=[jax.ShapeDtypeStruct((R, C), F32)] * 3,
    )(w, g, m, v)


def _adamw_from_halves(w, m, v, halves, axis, core, name):
    L, r, c = w.shape
    assert L == len(halves) == 2
    c1 = 1.0 / (1.0 - ADAM_B1 ** ADAM_STEP)
    c2 = 1.0 / (1.0 - ADAM_B2 ** ADAM_STEP)
    if axis == 0:
        tr = _pick(r // 2, 256, 8)
        per_half = r // 2 // tr
        steps = 2 * per_half

        def half_spec(layer, is_mine):
            def index(l, i, core_ref):
                read = jnp.logical_and(l == layer, (i // per_half == core_ref[0]) == is_mine)
                return jnp.where(read, i % per_half, 0), 0
            return pl.BlockSpec((tr, c), index)
    else:
        tr = _pick(r, 256, 8)
        steps = r // tr

        def half_spec(layer, is_mine):
            return pl.BlockSpec((tr, c // 2), lambda l, i, core_ref: (jnp.where(l == layer, i, 0), 0))
    whole = pl.BlockSpec((None, tr, c), lambda l, i, core_ref: (l, i, 0))
    half_specs = [half_spec(layer, is_mine) for layer in range(L) for is_mine in (True, False)]

    def body(core_ref, w_ref, m_ref, v_ref, mine0, theirs0, mine1, theirs1, g_ref, d_ref, nm_ref, nv_ref):
        shape = mine0.shape
        layer0 = jnp.full(shape, pl.program_id(0), jnp.int32) == 0
        mine = jnp.where(layer0, mine0[...], mine1[...])
        theirs = jnp.where(layer0, theirs0[...], theirs1[...])
        core_v = jnp.full(shape, core_ref[0], jnp.int32)

        def update(gv, cols):
            mn = ADAM_B1 * m_ref[:, cols] + (1.0 - ADAM_B1) * gv
            vn = ADAM_B2 * v_ref[:, cols] + (1.0 - ADAM_B2) * (gv * gv)
            g_ref[:, cols] = gv
            nm_ref[:, cols] = mn
            nv_ref[:, cols] = vn
            d_ref[:, cols] = -ADAM_LR * ((mn * c1) / (jnp.sqrt(vn * c2) + ADAM_EPS) + ADAM_WD * w_ref[:, cols])

        if axis == 0:
            here = jnp.full(shape, pl.program_id(1) // per_half, jnp.int32)
            update(jnp.where(here == core_v, mine, theirs), slice(None))
        else:
            update(jnp.where(core_v == 0, mine, theirs), slice(0, c // 2))
            update(jnp.where(core_v == 0, theirs, mine), slice(c // 2, c))

    grid_spec = pltpu.PrefetchScalarGridSpec(
        num_scalar_prefetch=1, grid=(L, steps), in_specs=[whole] * 3 + half_specs, out_specs=[whole] * 4)
    return _pcall(body, name=name, grid_spec=grid_spec, out_shape=[jax.ShapeDtypeStruct((L, r, c), F32)] * 4,
                  )(core, w, m, v, *halves[0], *halves[1])


def _log_sigmoid(z):
    return jnp.minimum(z, 0.0) - jnp.log(1.0 + jnp.exp(-jnp.abs(z)))


def _dot3(x, m01):
    x1 = x.astype(BF16)
    r1 = x - x1.astype(F32)
    x2 = r1.astype(BF16)
    x3 = (r1 - x2.astype(F32)).astype(BF16)
    n = x.shape[0]
    if n % 16:
        return (jnp.dot(x1, m01, preferred_element_type=F32) + jnp.dot(x2, m01, preferred_element_type=F32)
                + jnp.dot(x3, m01, preferred_element_type=F32))
    y = jnp.dot(jnp.concatenate([x1, x2], axis=0), m01, preferred_element_type=F32)
    return y[:n] + y[n:]


def _dot_nt(a, b):
    return lax.dot_general(a, b, NT, preferred_element_type=F32)


def _dot_tn(a, b):
    return lax.dot_general(a, b, TN, preferred_element_type=F32)


def _iota2(shape):
    return lax.broadcasted_iota(jnp.int32, shape, 0), lax.broadcasted_iota(jnp.int32, shape, 1)


HEADS_PER_STEP = 4
HEADS = range(HEADS_PER_STEP)


CHUNK_HEADS_PER_STEP = 4
CHUNK_HEADS = range(CHUNK_HEADS_PER_STEP)


def _head_spec(T, rows=None, width=HEAD_DIM, heads=HEADS_PER_STEP):
    return pl.BlockSpec((heads, T if rows is None else rows, width), lambda g: (g, 0, 0))


def _sb_weights(qb, kb, t0, s0, racc, m_gt, row, col, diagonal):
    z = _dot_nt(qb, kb) * SCALE
    lb = _log_sigmoid(z)
    strict = (s0 + col) < (t0 + row) if diagonal else None
    lf = jnp.where(strict, lb - z, 0.0) if diagonal else lb - z
    w = jnp.exp(lb + _dot3(lf, m_gt) + racc)
    return strict, lb, lf, (jnp.where(strict, w, 0.0) if diagonal else w)


def _sb_fwd(q, k, v, name, carry=None):
    H, T, _ = q.shape
    nb = T // QB

    def body(q_ref, k_ref, v_ref, o_ref):
        row, col = _iota2((QB, QB))
        m_gt = (row > col).astype(BF16)

        def qblock(i, _):
            t0 = pl.multiple_of(i * QB, QB)
            qbs = [q_ref[h, pl.ds(t0, QB), :].astype(BF16) for h in HEADS]

            def kblock(j, carry, diagonal):
                s0 = pl.multiple_of(j * QB, QB)
                out = []
                for h in HEADS:
                    acc, racc = carry[h]
                    kb = k_ref[h, pl.ds(s0, QB), :].astype(BF16)
                    vb = v_ref[h, pl.ds(s0, QB), :].astype(BF16)
                    _, _, lf, w = _sb_weights(qbs[h], kb, t0, s0, racc, m_gt, row, col, diagonal)
                    acc = acc + jnp.dot(w.astype(BF16), vb, preferred_element_type=F32)
                    out.append((acc, racc + jnp.sum(lf, axis=1, keepdims=True)))
                return tuple(out)

            res = kblock(i, tuple((jnp.zeros((QB, HEAD_DIM), F32), jnp.zeros((QB, 1), F32)) for _ in HEADS), True)
            res = lax.fori_loop(1, i + 1, lambda jj, carry: kblock(i - jj, carry, False), res)
            for h in HEADS:
                o_ref[h, pl.ds(t0, QB), :] = res[h][0].astype(BF16)
            return 0

        lax.fori_loop(0, nb, qblock, 0)

    spec = _head_spec(T)
    return _pcall_carrying(body, carry, name=name, grid=(H // HEADS_PER_STEP,), in_specs=[spec] * 3, out_specs=spec,
                           out_shape=jax.ShapeDtypeStruct((H, T, HEAD_DIM), BF16))(q, k, v)


def _sb_bwd(q, k, v, do, name, carry=None):
    H, T, _ = q.shape
    nb = T // QB

    def body(q_ref, k_ref, v_ref, do_ref, dq_ref, dk_out, dv_out, racc_ref, dk_ref, dv_ref):
        row, col = _iota2((QB, QB))
        m_gt = (row > col).astype(BF16)
        m_lt = (row < col).astype(BF16)
        dk_ref[...] = jnp.zeros_like(dk_ref)
        dv_ref[...] = jnp.zeros_like(dv_ref)

        def qblock(i, _):
            t0 = pl.multiple_of(i * QB, QB)
            qbs = [q_ref[h, pl.ds(t0, QB), :].astype(BF16) for h in HEADS]
            dobs = [do_ref[h, pl.ds(t0, QB), :].astype(BF16) for h in HEADS]

            def suffix(j, raccs, diagonal):
                s0 = pl.multiple_of(j * QB, QB)
                out = []
                for h in HEADS:
                    kb = k_ref[h, pl.ds(s0, QB), :].astype(BF16)
                    z = _dot_nt(qbs[h], kb) * SCALE
                    lf = _log_sigmoid(z) - z
                    if diagonal:
                        lf = jnp.where((s0 + col) < (t0 + row), lf, 0.0)
                    racc_ref[h, j] = raccs[h]
                    out.append(raccs[h] + jnp.sum(lf, axis=1, keepdims=True))
                return tuple(out)

            raccs = suffix(i, tuple(jnp.zeros((QB, 1), F32) for _ in HEADS), True)
            lax.fori_loop(1, i + 1, lambda jj, r: suffix(i - jj, r, False), raccs)

            def kblock(j, carry, diagonal):
                s0 = pl.multiple_of(j * QB, QB)
                out = []
                for h in HEADS:
                    dq, cacc = carry[h]
                    kb = k_ref[h, pl.ds(s0, QB), :].astype(BF16)
                    vb = v_ref[h, pl.ds(s0, QB), :].astype(BF16)
                    strict, lb, _, w = _sb_weights(qbs[h], kb, t0, s0, racc_ref[h, j], m_gt, row, col, diagonal)
                    e = _dot_nt(dobs[h], vb) * w
                    c_left = _dot3(e, m_lt) + cacc
                    sig = jnp.exp(lb)
                    dz = e * (1.0 - sig) - c_left * sig
                    dz = (jnp.where(strict, dz, 0.0) if diagonal else dz).astype(BF16)
                    dq = dq + jnp.dot(dz, kb, preferred_element_type=F32)
                    dk_ref[h, pl.ds(s0, QB), :] += _dot_tn(dz, qbs[h]) * SCALE
                    dv_ref[h, pl.ds(s0, QB), :] += _dot_tn(w.astype(BF16), dobs[h])
                    out.append((dq, cacc + jnp.sum(e, axis=1, keepdims=True)))
                return tuple(out)

            res = lax.fori_loop(0, i, lambda j, carry: kblock(j, carry, False),
                                tuple((jnp.zeros((QB, HEAD_DIM), F32), jnp.zeros((QB, 1), F32)) for _ in HEADS))
            res = kblock(i, res, True)
            for h in HEADS:
                dq_ref[h, pl.ds(t0, QB), :] = (res[h][0] * SCALE).astype(BF16)
            return 0

        lax.fori_loop(0, nb, qblock, 0)
        dk_out[...] = dk_ref[...].astype(BF16)
        dv_out[...] = dv_ref[...].astype(BF16)

    spec = _head_spec(T)
    acc = pltpu.VMEM((HEADS_PER_STEP, T, HEAD_DIM), F32)
    return _pcall_carrying(body, carry, name=name, grid=(H // HEADS_PER_STEP,), in_specs=[spec] * 4,
                           out_specs=[spec] * 3, out_shape=[jax.ShapeDtypeStruct((H, T, HEAD_DIM), BF16)] * 3,
                           scratch_shapes=[pltpu.VMEM((HEADS_PER_STEP, nb, QB, 1), F32), acc, acc])(q, k, v, do)


def _fox_logits(qb, kb, fq, fk, t0, s0, row, col):
    z = _dot_nt(qb, kb) * SCALE + fq - fk
    return jnp.where((s0 + col) <= (t0 + row), z, NEG)


def _fox_specs(T):
    return _head_spec(T, width=1), _head_spec(T, rows=T // QB, width=QB)


def _fox_fwd(q, k, v, fq, fk, name, carry=None):
    H, T, _ = q.shape
    nb = T // QB

    def body(q_ref, k_ref, v_ref, fq_ref, fk_ref, o_ref, lse_ref):
        row, col = _iota2((QB, QB))

        def qblock(i, _):
            t0 = pl.multiple_of(i * QB, QB)
            qbs = [q_ref[h, pl.ds(t0, QB), :].astype(BF16) for h in HEADS]
            fqs = [fq_ref[h, pl.ds(t0, QB), :] for h in HEADS]

            def kblock(j, carry):
                s0 = pl.multiple_of(j * QB, QB)
                out = []
                for h in HEADS:
                    acc, m, l = carry[h]
                    kb = k_ref[h, pl.ds(s0, QB), :].astype(BF16)
                    vb = v_ref[h, pl.ds(s0, QB), :].astype(BF16)
                    z = _fox_logits(qbs[h], kb, fqs[h], fk_ref[h, pl.ds(j, 1), :], t0, s0, row, col)
                    m_new = jnp.maximum(m, jnp.max(z, axis=1, keepdims=True))
                    a = jnp.exp(m - m_new)
                    p = jnp.exp(z - m_new)
                    acc = a * acc + jnp.dot(p.astype(BF16), vb, preferred_element_type=F32)
                    out.append((acc, m_new, a * l + jnp.sum(p, axis=1, keepdims=True)))
                return tuple(out)

            res = lax.fori_loop(0, i + 1, kblock,
                                tuple((jnp.zeros((QB, HEAD_DIM), F32), jnp.full((QB, 1), NEG, F32),
                                       jnp.zeros((QB, 1), F32)) for _ in HEADS))
            for h in HEADS:
                acc, m, l = res[h]
                o_ref[h, pl.ds(t0, QB), :] = (acc / l).astype(BF16)
                lse_ref[h, pl.ds(t0, QB), :] = m + jnp.log(l)
            return 0

        lax.fori_loop(0, nb, qblock, 0)

    spec = _head_spec(T)
    fq_spec, fk_spec = _fox_specs(T)
    return _pcall_carrying(
        body, carry, name=name, grid=(H // HEADS_PER_STEP,), in_specs=[spec] * 3 + [fq_spec, fk_spec],
        out_specs=[spec, fq_spec],
        out_shape=[jax.ShapeDtypeStruct((H, T, HEAD_DIM), BF16), jax.ShapeDtypeStruct((H, T, 1), F32)],
    )(q, k, v, fq, fk)


def _fox_bwd(q, k, v, fq, fk, lse, do, name, carry=None):
    H, T, _ = q.shape
    nb = T // QB

    def body(q_ref, k_ref, v_ref, fq_ref, fk_ref, lse_ref, do_ref, dq_ref, dk_out, dv_out, dfk_ref, dk_ref, dv_ref):
        row, col = _iota2((QB, QB))
        dk_ref[...] = jnp.zeros_like(dk_ref)
        dv_ref[...] = jnp.zeros_like(dv_ref)
        dfk_ref[...] = jnp.zeros_like(dfk_ref)

        def qblock(i, _):
            t0 = pl.multiple_of(i * QB, QB)
            qbs = [q_ref[h, pl.ds(t0, QB), :].astype(BF16) for h in HEADS]
            fqs = [fq_ref[h, pl.ds(t0, QB), :] for h in HEADS]
            lses = [lse_ref[h, pl.ds(t0, QB), :] for h in HEADS]
            dobs = [do_ref[h, pl.ds(t0, QB), :].astype(BF16) for h in HEADS]

            def probs(h, j):
                s0 = pl.multiple_of(j * QB, QB)
                kb = k_ref[h, pl.ds(s0, QB), :].astype(BF16)
                vb = v_ref[h, pl.ds(s0, QB), :].astype(BF16)
                z = _fox_logits(qbs[h], kb, fqs[h], fk_ref[h, pl.ds(j, 1), :], t0, s0, row, col)
                return s0, kb, jnp.exp(z - lses[h]), _dot_nt(dobs[h], vb)

            def row_dot(j, deltas):
                out = []
                for h in HEADS:
                    _, _, p, dp = probs(h, j)
                    out.append(deltas[h] + jnp.sum(p * dp, axis=1, keepdims=True))
                return tuple(out)

            deltas = lax.fori_loop(0, i + 1, row_dot, tuple(jnp.zeros((QB, 1), F32) for _ in HEADS))

            def kblock(j, dqs):
                out = []
                for h in HEADS:
                    s0, kb, p, dp = probs(h, j)
                    dz = p * (dp - deltas[h])
                    dzb = dz.astype(BF16)
                    dk_ref[h, pl.ds(s0, QB), :] += _dot_tn(dzb, qbs[h]) * SCALE
                    dv_ref[h, pl.ds(s0, QB), :] += _dot_tn(p.astype(BF16), dobs[h])
                    dfk_ref[h, pl.ds(j, 1), :] += -jnp.sum(dz, axis=0, keepdims=True)
                    out.append(dqs[h] + jnp.dot(dzb, kb, preferred_element_type=F32))
                return tuple(out)

            dqs = lax.fori_loop(0, i + 1, kblock, tuple(jnp.zeros((QB, HEAD_DIM), F32) for _ in HEADS))
            for h in HEADS:
                dq_ref[h, pl.ds(t0, QB), :] = (dqs[h] * SCALE).astype(BF16)
            return 0

        lax.fori_loop(0, nb, qblock, 0)
        dk_out[...] = dk_ref[...].astype(BF16)
        dv_out[...] = dv_ref[...].astype(BF16)

    spec = _head_spec(T)
    fq_spec, fk_spec = _fox_specs(T)
    acc = pltpu.VMEM((HEADS_PER_STEP, T, HEAD_DIM), F32)
    return _pcall_carrying(body, carry, name=name, grid=(H // HEADS_PER_STEP,),
                           in_specs=[spec] * 3 + [fq_spec, fk_spec, fq_spec, spec],
                           out_specs=[spec, spec, spec, fk_spec],
                           out_shape=[jax.ShapeDtypeStruct((H, T, HEAD_DIM), BF16)] * 3
                           + [jax.ShapeDtypeStruct((H, nb, QB), F32)],
                           scratch_shapes=[acc, acc])(q, k, v, fq, fk, lse, do)


def _forget_cumsum(flog, name):
    R, T = flog.shape
    nb = T // QB

    def body(x_ref, o_ref):
        row, col = _iota2((QB, QB))
        m_le = (row <= col).astype(BF16)
        carry = jnp.zeros((R, 1), F32)
        for b in range(nb):
            lf = _log_sigmoid(x_ref[:, b * QB:(b + 1) * QB])
            o_ref[:, b * QB:(b + 1) * QB] = _dot3(lf, m_le) + carry
            carry = carry + jnp.sum(lf, axis=1, keepdims=True)

    spec = pl.BlockSpec((R, T), lambda: (0, 0))
    return _pcall(body, name=name, in_specs=[spec], out_specs=spec,
                  out_shape=jax.ShapeDtypeStruct((R, T), F32))(flog)


def _forget_cumsum_bwd(flog, df, name):
    R, T = flog.shape
    nb = T // QB

    def body(x_ref, df_ref, o_ref):
        row, col = _iota2((QB, QB))
        m_ge = (row >= col).astype(BF16)
        carry = jnp.zeros((R, 1), F32)
        for b in reversed(range(nb)):
            dfb = df_ref[:, b * QB:(b + 1) * QB]
            dlog = _dot3(dfb, m_ge) + carry
            o_ref[:, b * QB:(b + 1) * QB] = dlog * _sigmoid(-x_ref[:, b * QB:(b + 1) * QB])
            carry = carry + jnp.sum(dfb, axis=1, keepdims=True)

    spec = pl.BlockSpec((R, T), lambda: (0, 0))
    return _pcall(body, name=name, in_specs=[spec, spec], out_specs=spec,
                  out_shape=jax.ShapeDtypeStruct((R, T), F32))(flog, df)


def _chunk_probs(qc, kb, bias, c, col):
    z = _dot_nt(qc, kb) * SCALE + bias
    z = jnp.where((c - (LEFT_CHUNKS + 1)) * CHUNK + col >= jnp.maximum(0, (c - LEFT_CHUNKS) * CHUNK), z, NEG)
    p = jnp.exp(z - jnp.max(z, axis=1, keepdims=True))
    return p, jnp.sum(p, axis=1, keepdims=True)


def _chunk_specs(T, n=CHUNK_HEADS_PER_STEP):
    return (_head_spec(T, heads=n), _head_spec(T, rows=T + BAND_PAD, heads=n),
            _head_spec(T, rows=CHUNK, width=BAND, heads=n))


def _chunk_fwd(q, kp, vp, bias, name, carry=None):
    H, T, _ = q.shape
    nc = T // CHUNK

    def body(q_ref, kp_ref, vp_ref, bias_ref, o_ref):
        _, col = _iota2((CHUNK, BAND))

        def chunk(c, _):
            t0 = pl.multiple_of(c * CHUNK, CHUNK)
            for h in HEADS:
                qc = q_ref[h, pl.ds(t0, CHUNK), :].astype(BF16)
                kb = kp_ref[h, pl.ds(t0, BAND), :].astype(BF16)
                vb = vp_ref[h, pl.ds(t0, BAND), :].astype(BF16)
                p, l = _chunk_probs(qc, kb, bias_ref[h], c, col)
                o = jnp.dot(p.astype(BF16), vb, preferred_element_type=F32) / l
                o_ref[h, pl.ds(t0, CHUNK), :] = o.astype(BF16)
            return 0

        lax.fori_loop(0, nc, chunk, 0)

    q_spec, kp_spec, b_spec = _chunk_specs(T, HEADS_PER_STEP)
    return _pcall_carrying(body, carry, name=name, grid=(H // HEADS_PER_STEP,),
                           in_specs=[q_spec, kp_spec, kp_spec, b_spec], out_specs=q_spec,
                           out_shape=jax.ShapeDtypeStruct((H, T, HEAD_DIM), BF16))(q, kp, vp, bias)


def _chunk_bwd(q, kp, vp, bias, do, name, carry=None):
    H, T, _ = q.shape
    nc = T // CHUNK

    def body(q_ref, kp_ref, vp_ref, bias_ref, do_ref, dq_ref, dk_out, dv_out, db_ref, dkp_ref, dvp_ref):
        _, col = _iota2((CHUNK, BAND))
        dkp_ref[...] = jnp.zeros_like(dkp_ref)
        dvp_ref[...] = jnp.zeros_like(dvp_ref)
        db_ref[...] = jnp.zeros_like(db_ref)

        def chunk(c, _):
            t0 = pl.multiple_of(c * CHUNK, CHUNK)
            for h in CHUNK_HEADS:
                qc = q_ref[h, pl.ds(t0, CHUNK), :].astype(BF16)
                kb = kp_ref[h, pl.ds(t0, BAND), :].astype(BF16)
                vb = vp_ref[h, pl.ds(t0, BAND), :].astype(BF16)
                dob = do_ref[h, pl.ds(t0, CHUNK), :].astype(BF16)
                p, l = _chunk_probs(qc, kb, bias_ref[h], c, col)
                p = p / l
                dp = _dot_nt(dob, vb)
                dz = p * (dp - jnp.sum(p * dp, axis=1, keepdims=True))
                dzb = dz.astype(BF16)
                dq = jnp.dot(dzb, kb, preferred_element_type=F32) * SCALE
                dq_ref[h, pl.ds(t0, CHUNK), :] = dq.astype(BF16)
                dkp_ref[h, pl.ds(t0, BAND), :] += _dot_tn(dzb, qc) * SCALE
                dvp_ref[h, pl.ds(t0, BAND), :] += _dot_tn(p.astype(BF16), dob)
                db_ref[h] += dz
            return 0

        lax.fori_loop(0, nc, chunk, 0)
        dk_out[...] = dkp_ref[:, BAND_PAD:, :].astype(BF16)
        dv_out[...] = dvp_ref[:, BAND_PAD:, :].astype(BF16)

    q_spec, kp_spec, b_spec = _chunk_specs(T)
    acc = pltpu.VMEM((CHUNK_HEADS_PER_STEP, T + BAND_PAD, HEAD_DIM), F32)
    return _pcall_carrying(body, carry, name=name, grid=(H // CHUNK_HEADS_PER_STEP,),
                           in_specs=[q_spec, kp_spec, kp_spec, b_spec, q_spec],
                           out_specs=[q_spec, q_spec, q_spec, b_spec],
                           out_shape=[jax.ShapeDtypeStruct((H, T, HEAD_DIM), BF16)] * 3
                           + [jax.ShapeDtypeStruct((H, CHUNK, BAND), F32)],
                           scratch_shapes=[acc, acc])(q, kp, vp, bias, do)


HBM_SPEC = pl.BlockSpec(memory_space=pltpu.HBM)


def _position():
    return lax.axis_index("x"), lax.axis_index("y"), lax.axis_index("c")


def _other_chips(x, y):
    chips = [(1 - x, y), (x, 1 - y), (1 - x, 1 - y)]
    return [(chip, 2 * chip[0] + chip[1]) for chip in chips]


def _remote_copy(src, dst, send_sems, recv_sems, k, to):
    return pltpu.make_async_remote_copy(src_ref=src, dst_ref=dst, send_sem=send_sems.at[k], recv_sem=recv_sems.at[k],
                                        device_id=to, device_id_type=MESH)


def _place_own(w, chip, name):
    _, r, c = w.shape
    tr = _pick(r, 256, 16)

    def body(chip_ref, w_ref, *out_refs):
        for l, o_ref in enumerate(out_refs):
            o_ref[...] = w_ref[l].astype(BF16)

    grid_spec = pltpu.PrefetchScalarGridSpec(
        num_scalar_prefetch=1, grid=(r // tr,), in_specs=[pl.BlockSpec((DEPTH, tr, c), lambda i, ch: (0, i, 0))],
        out_specs=[pl.BlockSpec((None, tr, c), lambda i, ch: (ch[0], i, 0))] * DEPTH)
    return _pcall(body, name=name, grid_spec=grid_spec,
                  out_shape=[jax.ShapeDtypeStruct((N_CHIPS, r, c), BF16)] * DEPTH)(chip, w)


def _gather_over_ici(srcs, bufs, new):
    x, y, c = _position()
    me = 2 * x + y
    return [(b.at[me, c], b.at[me, c], (*chip, c)) for b in bufs for chip, _ in _other_chips(x, y)]


def _gather_to_sibling(srcs, bufs, new):
    x, y, c = _position()
    return [(b.at[idx, c], b.at[idx, c], (x, y, 1 - c)) for b in bufs for _, idx in _other_chips(x, y)]


def _gather_layer(bufs, name):
    n = len(bufs)

    def body(*refs):
        dst = refs[n:2 * n]
        send_sems, recv_sems = refs[2 * n:]
        x, y, c = _position()
        me = 2 * x + y
        sibling = (x, y, 1 - c)
        others = _other_chips(x, y)
        first = [_remote_copy(dst[i].at[me, c], dst[i].at[me, c], send_sems, recv_sems, 3 * i + k, (*chip, c))
                 for i in range(n) for k, (chip, _) in enumerate(others)]
        for cp in first:
            cp.start()
        passed = []
        for i in range(n):
            for k, (_, idx) in enumerate(others):
                landed = dst[i].at[idx, c]
                _remote_copy(landed, landed, send_sems, recv_sems, 3 * i + k, sibling).wait_recv()
                passed.append(_remote_copy(landed, landed, send_sems, recv_sems, 3 * (n + i) + k, sibling))
                passed[-1].start()
        for i in range(n):
            for k, (_, idx) in enumerate(others):
                from_sibling = dst[i].at[idx, 1 - c]
                _remote_copy(from_sibling, from_sibling, send_sems, recv_sems, 3 * (n + i) + k, sibling).wait_recv()
        for cp in first + passed:
            cp.wait_send()

    return _pcall(
        body, name=name, in_specs=[HBM_SPEC] * n, out_specs=[HBM_SPEC] * n,
        out_shape=[jax.ShapeDtypeStruct(b.shape, b.dtype) for b in bufs],
        scratch_shapes=[pltpu.SemaphoreType.DMA((6 * n,)), pltpu.SemaphoreType.DMA((6 * n,))],
        input_output_aliases={i: i for i in range(n)},
    )(*bufs)


def _send_other_half(parts, name):
    n = len(parts)

    def body(*refs):
        src, got = refs[:n], refs[n:2 * n]
        send_sems, recv_sems = refs[2 * n:]
        x, y, c = _position()
        copies = [_remote_copy(src[i].at[1 - c], got[i], send_sems, recv_sems, i, (x, y, 1 - c)) for i in range(n)]
        for cp in copies:
            cp.start()
        for cp in copies:
            cp.wait()

    return _pcall(
        body, name=name, in_specs=[HBM_SPEC] * n, out_specs=[HBM_SPEC] * n,
        out_shape=[jax.ShapeDtypeStruct(p.shape[1:], p.dtype) for p in parts],
        scratch_shapes=[pltpu.SemaphoreType.DMA((n,)), pltpu.SemaphoreType.DMA((n,))],
    )(*parts)


def _scatter_chips(parts, name):
    n = len(parts)

    def body(*refs):
        src, dst = refs[:n], refs[n:2 * n]
        send_sems, recv_sems = refs[2 * n:]
        x, y, c = _position()
        others = _other_chips(x, y)
        sends = [_remote_copy(src[i].at[idx], dst[i].at[k], send_sems, recv_sems, 3 * i + k, (*chip, c))
                 for i in range(n) for k, (chip, idx) in enumerate(others)]
        for cp in sends:
            cp.start()
        for cp in sends:
            cp.wait()

    return _pcall(
        body, name=name, in_specs=[HBM_SPEC] * n, out_specs=[HBM_SPEC] * n,
        out_shape=[jax.ShapeDtypeStruct((3,) + p.shape[1:], p.dtype) for p in parts],
        scratch_shapes=[pltpu.SemaphoreType.DMA((3 * n,)), pltpu.SemaphoreType.DMA((3 * n,))],
    )(*parts)


def _swap_with_sibling(arrs, name):
    n = len(arrs)

    def body(*refs):
        src, dst = refs[:n], refs[n:2 * n]
        send_sems, recv_sems = refs[2 * n:]
        x, y, c = _position()
        copies = [_remote_copy(src[i], dst[i], send_sems, recv_sems, i, (x, y, 1 - c)) for i in range(n)]
        for cp in copies:
            cp.start()
        for cp in copies:
            cp.wait()

    return _pcall(
        body, name=name, in_specs=[HBM_SPEC] * n, out_specs=[HBM_SPEC] * n,
        out_shape=[jax.ShapeDtypeStruct(a.shape, a.dtype) for a in arrs],
        scratch_shapes=[pltpu.SemaphoreType.DMA((n,)), pltpu.SemaphoreType.DMA((n,))],
    )(*arrs)


def _allreduce_small(v, name):
    R, C = v.shape

    def body(v_ref, o_ref, slots, send_sems, recv_sems):
        x, y, c = _position()
        me = 4 * x + 2 * y + c
        slots[0] = v_ref[...]
        sends = []
        for r in range(1, 8):
            fx, fy, fc = (r >> 2) & 1, (r >> 1) & 1, r & 1
            to = (x ^ fx, y ^ fy, c ^ fc)
            cp = pltpu.make_async_remote_copy(src_ref=v_ref, dst_ref=slots.at[r], send_sem=send_sems.at[r - 1],
                                              recv_sem=recv_sems.at[r - 1], device_id=to, device_id_type=MESH)
            cp.start()
            sends.append(cp)
        for cp in sends:
            cp.wait()
        acc = slots[me]
        for d in range(1, 8):
            acc = acc + slots[d ^ me]
        o_ref[...] = acc

    vmem = pl.BlockSpec(memory_space=pltpu.VMEM)
    return _pcall(
        body, name=name, in_specs=[vmem], out_specs=vmem, out_shape=jax.ShapeDtypeStruct((R, C), F32),
        scratch_shapes=[pltpu.VMEM((8, R, C), F32), pltpu.SemaphoreType.DMA((7,)), pltpu.SemaphoreType.DMA((7,))],
    )(v)


def _add_pair(which, both, got, name):
    _, N, R, C = both.shape
    tr = _pick(R, 512, 16)

    def body(which_ref, a_ref, b_ref, o_ref):
        o_ref[...] = (a_ref[...].astype(F32) + b_ref[...].astype(F32)).astype(BF16)

    spec = pl.BlockSpec((None, tr, C), lambda n, i, w: (n, i, 0))
    grid_spec = pltpu.PrefetchScalarGridSpec(
        num_scalar_prefetch=1, grid=(N, R // tr),
        in_specs=[pl.BlockSpec((None, None, tr, C), lambda n, i, w: (w[0], n, i, 0)), spec], out_specs=spec)
    return _pcall(body, name=name, grid_spec=grid_spec,
                  out_shape=jax.ShapeDtypeStruct((N, R, C), BF16))(which, both, got)


def _sum_chips(which, own, others, name):
    N, R, C = others.shape
    tr = _pick(R, 512, 16)

    def body(which_ref, own_ref, p_ref, o_ref):
        acc = own_ref[...].astype(F32)
        for n in range(N):
            acc = acc + p_ref[n].astype(F32)
        o_ref[...] = acc

    grid_spec = pltpu.PrefetchScalarGridSpec(
        num_scalar_prefetch=1, grid=(R // tr,),
        in_specs=[pl.BlockSpec((None, tr, C), lambda i, w: (w[0], i, 0)), pl.BlockSpec((N, tr, C), lambda i, w: (0, i, 0))],
        out_specs=pl.BlockSpec((tr, C), lambda i, w: (i, 0)))
    return _pcall(body, name=name, grid_spec=grid_spec,
                  out_shape=jax.ShapeDtypeStruct((R, C), F32))(which, own, others)


BIG = ("w_ffn1_in", "w_ffn1_out", "w_in", "w_br_sb", "w_br_ch", "w_br_fox", "w_out", "w_ffn2_in", "w_ffn2_out")
ROW_SHARDED = ("w_ffn1_out", "w_out", "w_ffn2_out")
SMALL = ("g_ffn1", "g_mix", "b_in", "rel_bias", "g_ffn2", "g_final")


def _pair_sums(split, tag):
    core = lax.axis_index("c").astype(jnp.int32)[None]
    got = _send_other_half(split, f"grad_split_{tag}")
    return [_add_pair(core, a, b, f"grad_pair_sum_{tag}_{i}") for i, (a, b) in enumerate(zip(split, got))]


def _scatter_plan(srcs, bufs, new):
    x, y, c = _position()
    return [(s.at[idx], d.at[k], (*chip, c)) for s, d in zip(srcs, new) for k, (chip, idx) in enumerate(_other_chips(x, y))]


def _scatter_carry(pair):
    landing = [jax.ShapeDtypeStruct((3,) + p.shape[1:], p.dtype) for p in pair]
    return _Carry(pair, [], landing, 3 * len(pair), _scatter_plan)


def _finish_grads(pair, landed, tag):
    chip = (2 * lax.axis_index("x") + lax.axis_index("y")).astype(jnp.int32)[None]
    mine = [_sum_chips(chip, p, q, f"grad_chip_sum_{tag}_{i}") for i, (p, q) in enumerate(zip(pair, landed))]
    return mine, _swap_with_sibling(mine, f"grad_share_{tag}")


SMALL_SIZES = {"g_ffn1": DEPTH * D_MODEL, "g_mix": DEPTH * D_MODEL, "b_in": DEPTH * IN_WIDTH,
               "rel_bias": DEPTH * N_REL * H_CH, "g_ffn2": DEPTH * D_MODEL, "g_final": D_MODEL}
SMALL_TOTAL = sum(SMALL_SIZES.values()) + 1
SMALL_ROWS = -(-SMALL_TOTAL // (8 * 128)) * 8


def _pack_small(vals, extra):
    flat = [vals[n].reshape(-1) for n in SMALL] + [extra.reshape(-1)]
    flat.append(jnp.zeros((SMALL_ROWS * 128 - SMALL_TOTAL,), F32))
    return jnp.concatenate(flat).reshape(SMALL_ROWS, 128)


def _unpack_small(pack, shapes):
    flat, out, off = pack.reshape(-1), {}, 0
    for n in SMALL:
        out[n] = flat[off:off + SMALL_SIZES[n]].reshape(shapes[n])
        off += SMALL_SIZES[n]
    return out, flat[off]


def _to_heads(t, n_heads):
    return jnp.transpose(t.reshape(t.shape[0], n_heads, HEAD_DIM), (1, 0, 2)).astype(BF16)


def _from_heads(t):
    return jnp.transpose(t, (1, 0, 2)).reshape(t.shape[1], t.shape[0] * HEAD_DIM)


def _pad_keys(t):
    return jnp.pad(t, ((0, 0), (BAND_PAD, 0), (0, 0)))


DIAGONALS = CHUNK + BAND - 1


def _band_bias(table):
    n_far = (LEFT_CHUNKS + 1) * CHUNK + CHUNK - MAX_REL
    near = table[N_REL - 1 - (DIAGONALS - n_far):N_REL - 1][::-1]
    diag = jnp.concatenate([jnp.broadcast_to(table[N_REL - 1], (n_far, H_CH)), near], axis=0).T
    diag = jnp.pad(diag, ((0, 0), (0, 1)))
    skew = jnp.tile(diag, (1, CHUNK))[:, :CHUNK * DIAGONALS].reshape(H_CH, CHUNK, DIAGONALS)
    return skew[:, :, CHUNK - 1:]


def _pad_w_in(w):
    qkv, f, gates = w[:, :QKV_WIDTH], w[:, QKV_WIDTH:QKV_WIDTH + H_FOX], w[:, QKV_WIDTH + H_FOX:]
    return jnp.concatenate([qkv, gates, f, jnp.zeros((w.shape[0], F_PAD - H_FOX), w.dtype)], axis=1)


def _unpad_w_in(w):
    return jnp.concatenate([w[:, :QKV_WIDTH], w[:, QKV_WIDTH + GATE_WIDTH:QKV_WIDTH + GATE_WIDTH + H_FOX],
                            w[:, QKV_WIDTH:QKV_WIDTH + GATE_WIDTH]], axis=1)


QKV_SPLITS = np.cumsum([0, W_SB, W_SB, W_SB, W_CH, W_CH, W_CH, W_FOX, W_FOX, W_FOX])


def _by_chip_rows(g):
    return g.reshape(2, N_CHIPS, g.shape[1] // N_CHIPS, g.shape[2])


def _ffn_fwd(x, g, w_in, w_out, tag, carries):
    h, gate, up, a = _ffn_in_swiglu(x, g, w_in, f"{tag}_in", _carry_of(carries, "in"))
    y = _mm(a, w_out, mode="nn", name=f"{tag}_out", alpha=0.5, res=x, gathered="row",
            carry=_carry_of(carries, "out"))
    return y, (h, gate, up, a)


def _ffn_bwd(x, g, w_in, w_out, saved, dy, tag, carries):
    h, gate, up, a = saved
    da = _mm(dy, w_out, mode="nt", name=f"{tag}_da", alpha=0.5, gathered="row", out_dtype=BF16,
             carry=_carry_of(carries, "da"))
    dw_out = _mm(a, dy, mode="tn", name=f"{tag}_dwout", alpha=0.5, tm_cap=1408, out_dtype=BF16, out_split="row",
                 carry=_carry_of(carries, "dwout"))
    dgu = _swiglu_bwd(gate, up, da, f"{tag}_dact")
    dw_in = _mm(h, dgu, mode="tn", name=f"{tag}_dwin", tm_cap=512, out_dtype=BF16, out_split="col",
                carry=_carry_of(carries, "dwin"))
    dx, dg = _mm(dgu, w_in, mode="nt", name=f"{tag}_dh", gathered="col", tn_cap=1024, carry=_carry_of(carries, "dh"),
                 norm_bwd=(x, g, dy))
    return dx, dg, dw_in, _by_chip_rows(dw_out)


def _mixer_fwd(x, lw, tag, carries):
    T = x.shape[0]
    h = _rmsnorm_fwd(x, lw["g_mix"], f"{tag}_norm")
    p = _mm(h, lw["w_in"], mode="nn", name=f"{tag}_proj", bias=lw["b_in"], tn_cap=896, out_dtype=BF16,
            carry=_carry_of(carries, "proj"))
    parts = [p[:, QKV_SPLITS[i]:QKV_SPLITS[i + 1]] for i in range(9)]
    qa, ka, va = (_to_heads(t, H_SB) for t in parts[0:3])
    qb, kb, vb = (_to_heads(t, H_CH) for t in parts[3:6])
    qc, kc, vc = (_to_heads(t, H_FOX) for t in parts[6:9])
    flog = _mm(lw["w_forget_t"], h, mode="nt", name=f"{tag}_flog")[:8] + lw["b_forget"]
    fcum = _forget_cumsum(flog, f"{tag}_fcum")[:H_FOX]
    fq, fk = fcum.reshape(H_FOX, T, 1), fcum.reshape(H_FOX, T // QB, QB)
    kbp, vbp = _pad_keys(kb), _pad_keys(vb)
    oa = _sb_fwd(qa, ka, va, f"{tag}_sb", _carry_of(carries, "sb"))
    ob = _chunk_fwd(qb, kbp, vbp, lw["bias"], f"{tag}_ch", _carry_of(carries, "ch"))
    oc, lse = _fox_fwd(qc, kc, vc, fq, fk, f"{tag}_fox", _carry_of(carries, "fox"))
    oam, obm, ocm = _from_heads(oa), _from_heads(ob), _from_heads(oc)
    ya, yb, yc, merged = _branches_fwd(p, (oam, obm, ocm), (lw["w_br_sb"], lw["w_br_ch"], lw["w_br_fox"]),
                                       f"{tag}_branches")
    y = _mm(merged, lw["w_out"], mode="nn", name=f"{tag}_out", res=x, gathered="row")
    saved = (h, p, (qa, ka, va), (qb, kbp, vbp), (qc, kc, vc), flog, fq, fk, lse, (oam, obm, ocm),
             (ya, yb, yc), merged)
    return y, saved


def _mixer_bwd(x, lw, saved, dy, tag, carries):
    (h, p, (qa, ka, va), (qb, kbp, vbp), (qc, kc, vc), flog, fq, fk, lse, (oam, obm, ocm),
     (ya, yb, yc), merged) = saved
    T = x.shape[0]
    grads = {}
    dmerged = _mm(dy, lw["w_out"], mode="nt", name=f"{tag}_dmerged", gathered="row",
                  carry=_carry_of(carries, "dmerged"))
    grads["w_out"] = _by_chip_rows(_mm(merged, dy, mode="tn", name=f"{tag}_dwout", tm_cap=1024, out_dtype=BF16,
                                       out_split="row"))
    (dya, dyb, dyc), dgates, (doa, dob, doc) = _branches_bwd(
        p, (ya, yb, yc), dmerged, (lw["w_br_sb"], lw["w_br_ch"], lw["w_br_fox"]), f"{tag}_dbranches")
    grads["w_br_sb"], grads["w_br_ch"], grads["w_br_fox"] = _branches_dw(
        (oam, obm, ocm), (dya, dyb, dyc), f"{tag}_dwbranches")
    dqa, dka, dva = _sb_bwd(qa, ka, va, _to_heads(doa, H_SB), f"{tag}_dsb", _carry_of(carries, "dsb"))
    dqb, dkb, dvb, dbias = _chunk_bwd(qb, kbp, vbp, lw["bias"], _to_heads(dob, H_CH), f"{tag}_dch",
                                      _carry_of(carries, "dch"))
    dqc, dkc, dvc, dfk = _fox_bwd(qc, kc, vc, fq, fk, lse, _to_heads(doc, H_FOX), f"{tag}_dfox",
                                  _carry_of(carries, "dfox"))
    dflog = _forget_cumsum_bwd(flog, jnp.pad(dfk.reshape(H_FOX, T), ((0, 8 - H_FOX), (0, 0))), f"{tag}_dfcum")
    dqkv = [_from_heads(t) for t in (dqa, dka, dva, dqb, dkb, dvb, dqc, dkc, dvc)]
    dforget = jnp.pad(dflog[:H_FOX].T, ((0, 0), (0, F_PAD - H_FOX))).astype(BF16)
    dpb = jnp.concatenate(dqkv + list(dgates) + [dforget], axis=1)
    grads["b_in"] = _colsum(dpb, f"{tag}_dbin")
    dw_in =_unpad_w_in(_mm(h, dpb, mode="tn", name=f"{tag}_dwin", tm_cap=512, tn_cap=896, out_dtype=BF16))
    grads["w_in"] = jnp.transpose(dw_in.reshape(2, D_MODEL // 2, N_CHIPS, IN_WIDTH // N_CHIPS), (0, 2, 1, 3))
    dx, grads["g_mix"] = _mm(dpb, lw["w_in"], mode="nt", name=f"{tag}_dh", tk_cap=896, tn_cap=1024,
                             norm_bwd=(x, lw["g_mix"], dy))
    grads["rel_bias"] = lw["bias_vjp"](dbias)[0]
    return dx, grads


def kernel(x, g_ffn1, w_ffn1_in, w_ffn1_out, g_mix, w_in, b_in, rel_bias, w_br_sb, w_br_ch, w_br_fox, w_out, g_ffn2, w_ffn2_in, w_ffn2_out, g_final, loss_target, m_g_ffn1, m_w_ffn1_in, m_w_ffn1_out, m_g_mix, m_w_in, m_b_in, m_rel_bias, m_w_br_sb, m_w_br_ch, m_w_br_fox, m_w_out, m_g_ffn2, m_w_ffn2_in, m_w_ffn2_out, m_g_final, v_g_ffn1, v_w_ffn1_in, v_w_ffn1_out, v_g_mix, v_w_in, v_b_in, v_rel_bias, v_w_br_sb, v_w_br_ch, v_w_br_fox, v_w_out, v_g_ffn2, v_w_ffn2_in, v_w_ffn2_out, v_g_final):
    names = ("g_ffn1", "w_ffn1_in", "w_ffn1_out", "g_mix", "w_in", "b_in", "rel_bias", "w_br_sb", "w_br_ch",
             "w_br_fox", "w_out", "g_ffn2", "w_ffn2_in", "w_ffn2_out", "g_final")
    w = dict(zip(names, (g_ffn1, w_ffn1_in, w_ffn1_out, g_mix, w_in, b_in, rel_bias, w_br_sb, w_br_ch,
                         w_br_fox, w_out, g_ffn2, w_ffn2_in, w_ffn2_out, g_final)))
    m = dict(zip(names, (m_g_ffn1, m_w_ffn1_in, m_w_ffn1_out, m_g_mix, m_w_in, m_b_in, m_rel_bias, m_w_br_sb,
                         m_w_br_ch, m_w_br_fox, m_w_out, m_g_ffn2, m_w_ffn2_in, m_w_ffn2_out, m_g_final)))
    v = dict(zip(names, (v_g_ffn1, v_w_ffn1_in, v_w_ffn1_out, v_g_mix, v_w_in, v_b_in, v_rel_bias, v_w_br_sb,
                         v_w_br_ch, v_w_br_fox, v_w_out, v_g_ffn2, v_w_ffn2_in, v_w_ffn2_out, v_g_final)))
    xs = x[0]
    tgt = loss_target[0]

    chip = (2 * lax.axis_index("x") + lax.axis_index("y")).astype(jnp.int32)[None]
    placed = [_place_own(w[n], chip, f"place_{n}") for n in BIG]
    bufs = [[p[l].reshape(N_CHIPS, 2, p[l].shape[1] // 2, p[l].shape[2]) for p in placed] for l in range(DEPTH)]
    padded_w_in = {}

    def layer_weights(l):
        lw = {n: b.reshape((N_CHIPS,) + w[n].shape[1:]) for n, b in zip(BIG, bufs[l])}
        if l not in padded_w_in:
            whole = jnp.concatenate([lw["w_in"][j] for j in range(N_CHIPS)], axis=1)
            forget_t = jnp.pad(whole[:, QKV_WIDTH:QKV_WIDTH + H_FOX].T, ((0, F_PAD - H_FOX), (0, 0)))
            padded_w_in[l] = (_pad_w_in(whole), forget_t)
        lw["w_in"], lw["w_forget_t"] = padded_w_in[l]
        lw["b_forget"] = jnp.pad(w["b_in"][l][QKV_WIDTH:QKV_WIDTH + H_FOX], (0, 8 - H_FOX))[:, None]
        lw["b_in"] = _pad_w_in(w["b_in"][l][None, :])
        lw["bias"], lw["bias_vjp"] = jax.vjp(_band_bias, w["rel_bias"][l])
        for n in ("g_ffn1", "g_mix", "g_ffn2"):
            lw[n] = w[n][l][None, :]
        return lw

    def landed_in(l, idx):
        def done(carry):
            for i, b in zip(idx, carry.out[0]):
                bufs[l][i] = b
        return done

    def over_ici(l, idx):
        return lambda: _Carry([], [bufs[l][i] for i in idx], [], 3 * len(idx), _gather_over_ici, landed_in(l, idx))

    def to_sibling(l, idx):
        return lambda: _Carry([], [bufs[l][i] for i in idx], [], 3 * len(idx), _gather_to_sibling, landed_in(l, idx))

    def ffn_fwd(x_in, lw, which, tag, carries):
        return _ffn_fwd(x_in, lw[f"g_{which}"], lw[f"w_{which}_in"], lw[f"w_{which}_out"], tag, carries)

    def ffn_bwd(x_in, lw, which, saved_acts, dy, tag, carries):
        return _ffn_bwd(x_in, lw[f"g_{which}"], lw[f"w_{which}_in"], lw[f"w_{which}_out"], saved_acts, dy, tag,
                        carries)

    FFN1, W_IN, MIXER_REST, FFN2_IN, FFN2_OUT = (0, 1), (2,), (3, 4, 5, 6), (7,), (8,)
    first = FFN1 + W_IN
    for i, b in zip(first, _gather_layer([bufs[0][i] for i in first], "gather_l0")):
        bufs[0][i] = b
    fwd_carries = [
        {"ffn1": {"in": [over_ici(0, MIXER_REST)], "out": [to_sibling(0, MIXER_REST), over_ici(0, FFN2_OUT)]},
         "mix": {"proj": [to_sibling(0, FFN2_OUT), over_ici(1, MIXER_REST)],
                 "sb": [over_ici(0, FFN2_IN), over_ici(1, FFN2_OUT)],
                 "ch": [to_sibling(0, FFN2_IN), over_ici(1, FFN1)],
                 "fox": [over_ici(1, W_IN)]},
         "ffn2": {"in": [to_sibling(1, MIXER_REST + FFN2_OUT + FFN1 + W_IN)]}},
        {"ffn1": {}, "mix": {"sb": [over_ici(1, FFN2_IN)], "ch": [to_sibling(1, FFN2_IN)]}, "ffn2": {}},
    ]

    saved = []
    act = xs
    for l in range(DEPTH):
        x0 = act
        x1, s1 = ffn_fwd(x0, layer_weights(l), "ffn1", f"l{l}_ffn1", fwd_carries[l]["ffn1"])
        x2, s2 = _mixer_fwd(x1, layer_weights(l), f"l{l}_mix", fwd_carries[l]["mix"])
        act, s3 = ffn_fwd(x2, layer_weights(l), "ffn2", f"l{l}_ffn2", fwd_carries[l]["ffn2"])
        saved.append((x0, s1, x1, s2, x2, s3))
    layers = [layer_weights(l) for l in range(DEPTH)]

    dact, dg_final, loss_row = _final_loss(act, w["g_final"][None, :], tgt, "final_loss")

    big_grads = {n: [None] * DEPTH for n in BIG}
    small_grads = {n: [None] * DEPTH for n in ("g_ffn1", "g_mix", "b_in", "rel_bias", "g_ffn2")}
    pair = [[None] * len(BIG) for _ in range(DEPTH)]
    landed = [[None] * len(BIG) for _ in range(DEPTH)]

    def pair_up(l, idx, tag):
        for i, p in zip(idx, _pair_sums([big_grads[BIG[i]][l] for i in idx], tag)):
            pair[l][i] = p

    core = lax.axis_index("c").astype(jnp.int32)[None]

    def to_sibling_half(l, idx, tag):
        def plan(srcs, bufs, new):
            x, y, c = _position()
            return [(s.at[1 - c], d, (x, y, 1 - c)) for s, d in zip(srcs, new)]
        def done(carry):
            for j, (i, got) in enumerate(zip(idx, carry.out[1])):
                pair[l][i] = _add_pair(core, big_grads[BIG[i]][l], got, f"grad_pair_sum_{tag}_{j}")
        def make():
            split = [big_grads[BIG[i]][l] for i in idx]
            return _Carry(split, [], [jax.ShapeDtypeStruct(s.shape[1:], s.dtype) for s in split], len(idx), plan, done)
        return make

    def exchange(l, idx):
        def done(carry):
            for i, a in zip(idx, carry.out[1]):
                landed[l][i] = a
        def make():
            carry = _scatter_carry([pair[l][i] for i in idx])
            carry.done = done
            return carry
        return make

    def exchange_one_way(l, i, k):
        def plan(srcs, bufs, new):
            x, y, c = _position()
            chip_k, idx_k = _other_chips(x, y)[k]
            return [(srcs[0].at[idx_k], bufs[0].at[k], (*chip_k, c))]
        def done(carry):
            landed[l][i] = carry.out[0][0]
        def make():
            if landed[l][i] is None:
                landed[l][i] = lax.empty((3,) + pair[l][i].shape[1:], BF16)
            return _Carry([pair[l][i]], [landed[l][i]], [], 1, plan, done)
        return make

    bwd_carries = [
        {"ffn2": {},
         "mix": {"dmerged": [to_sibling_half(0, FFN2_IN + FFN2_OUT, "l0_ffn2")],
                 "dsb": [exchange(1, FFN1 + W_IN)], "dch": [exchange(1, MIXER_REST + FFN2_IN + FFN2_OUT)],
                 "dfox": [exchange(0, FFN2_IN + FFN2_OUT)]},
         "ffn1": {"da": [exchange(0, MIXER_REST)], "dwout": [exchange_one_way(0, 2, 0)],
                  "dwin": [exchange_one_way(0, 2, 1)], "dh": [exchange_one_way(0, 2, 2)]}},
        {"ffn2": {}, "mix": {},
         "ffn1": {"da": [to_sibling_half(1, W_IN + MIXER_REST + FFN2_IN + FFN2_OUT, "l1_rest")]}},
    ]
    for l in reversed(range(DEPTH)):
        lw = layers[l]
        x0, s1, x1, s2, x2, s3 = saved[l]
        dx2, small_grads["g_ffn2"][l], big_grads["w_ffn2_in"][l], big_grads["w_ffn2_out"][l] = ffn_bwd(
            x2, lw, "ffn2", s3, dact, f"l{l}_ffn2", bwd_carries[l]["ffn2"])
        dx1, mg = _mixer_bwd(x1, lw, s2, dx2, f"l{l}_mix", bwd_carries[l]["mix"])
        for n in ("w_in", "w_br_sb", "w_br_ch", "w_br_fox", "w_out"):
            big_grads[n][l] = mg[n]
        small_grads["b_in"][l] = _unpad_w_in(mg["b_in"])[0]
        small_grads["g_mix"][l] = mg["g_mix"][0]
        small_grads["rel_bias"][l] = mg["rel_bias"]
        if l == 0:
            pair_up(0, W_IN + MIXER_REST, "l0_mix")
        dact, dg1, big_grads["w_ffn1_in"][l], big_grads["w_ffn1_out"][l] = ffn_bwd(
            x0, lw, "ffn1", s1, dx1, f"l{l}_ffn1", bwd_carries[l]["ffn1"])
        small_grads["g_ffn1"][l] = dg1[0]
        small_grads["g_ffn2"][l] = small_grads["g_ffn2"][l][0]
        if l == 1:
            pair_up(1, FFN1, "l1_ffn1")
    grad_x = dact[None]
    pair_up(0, FFN1, "l0_ffn1")
    for i, a in zip(FFN1, _scatter_chips([pair[0][i] for i in FFN1], "grad_scatter_l0")):
        landed[0][i] = a

    small = {n: jnp.stack(small_grads[n]) for n in small_grads}
    small["g_final"] = dg_final[0]
    small_sum, loss = _unpack_small(_allreduce_small(_pack_small(small, loss_row[0, :1]), "allreduce_small"),
                                    {n: w[n].shape for n in SMALL})

    mine, theirs = _finish_grads(pair[0] + pair[1], landed[0] + landed[1], "all")

    grads, delta, new_m, new_v = dict(small_sum), {}, {}, {}
    for i, n in enumerate(BIG):
        halves = [(mine[l * len(BIG) + i], theirs[l * len(BIG) + i]) for l in range(DEPTH)]
        grads[n], delta[n], new_m[n], new_v[n] = _adamw_from_halves(
            w[n], m[n], v[n], halves, 1 if n in ROW_SHARDED else 0, core, f"adamw_{n}")
    zero = jnp.zeros((1,), F32)
    sd, sm, sv = _adamw(_pack_small(w, zero), _pack_small(grads, zero), _pack_small(m, zero), _pack_small(v, zero),
                        "adamw_small")
    shapes = {n: w[n].shape for n in SMALL}
    for dst, src in ((delta, sd), (new_m, sm), (new_v, sv)):
        dst.update(_unpack_small(src, shapes)[0])

    return (loss, grad_x, *[grads[n] for n in names], *[delta[n] for n in names],
            *[new_m[n] for n in names], *[new_v[n] for n in names])
```

```python
import functools

import numpy as np
import jax
import jax.numpy as jnp
from jax import lax
from jax.experimental import pallas as pl
from jax.experimental.pallas import tpu as pltpu

F32 = jnp.float32
BF16 = jnp.bfloat16

D_MODEL = 1024
DEPTH = 2
CHUNK = 64
HEAD_DIM = 64
H_SB, H_CH, H_FOX = 4, 8, 4
W_SB, W_CH, W_FOX = H_SB * HEAD_DIM, H_CH * HEAD_DIM, H_FOX * HEAD_DIM
LEFT_CHUNKS = 8
MAX_REL = 128
N_REL = 2 * MAX_REL + 1
D_FF = 2816
QKV_WIDTH = 3 * (W_SB + W_CH + W_FOX)
GATE_WIDTH = 3 * D_MODEL
IN_WIDTH = QKV_WIDTH + H_FOX + GATE_WIDTH
F_PAD = 128
P_WIDTH = QKV_WIDTH + GATE_WIDTH + F_PAD
RMS_EPS = 1e-6
NEG = -1e30
SCALE = HEAD_DIM ** -0.5
QB = 256
BAND = (LEFT_CHUNKS + 2) * CHUNK
BAND_PAD = BAND - CHUNK

ADAM_LR, ADAM_B1, ADAM_B2, ADAM_EPS, ADAM_WD, ADAM_STEP = 0.001, 0.9, 0.999, 1e-08, 0.01, 10

N_CHIPS = 4
VMEM_BUDGET = 52 * 1024 * 1024

NT = (((1,), (1,)), ((), ()))
TN = (((0,), (0,)), ((), ()))
NN = (((1,), (0,)), ((), ()))
MESH = pl.DeviceIdType.MESH


def _pcall(body, **kw):
    return pl.pallas_call(body, **kw)


class _Carry:
    def __init__(self, srcs, bufs, new, count, plan, done=None):
        self.srcs, self.bufs, self.new, self.count, self.plan = list(srcs), list(bufs), list(new), count, plan
        self.out, self.done = None, done

    @staticmethod
    def join(parts):
        parts = [p for p in parts if p is not None]
        if not parts:
            return None

        def plan(srcs, bufs, new):
            copies, s, b, n = [], 0, 0, 0
            for p in parts:
                copies += p.plan(srcs[s:s + len(p.srcs)], bufs[b:b + len(p.bufs)], new[n:n + len(p.new)])
                s, b, n = s + len(p.srcs), b + len(p.bufs), n + len(p.new)
            return copies

        whole = _Carry(sum((p.srcs for p in parts), []), sum((p.bufs for p in parts), []),
                       sum((p.new for p in parts), []), sum(p.count for p in parts), plan)
        whole.parts = parts
        return whole

    def share_out(self):
        b, n = 0, 0
        for p in getattr(self, "parts", []):
            p.out = (self.out[0][b:b + len(p.bufs)], self.out[1][n:n + len(p.new)])
            b, n = b + len(p.bufs), n + len(p.new)
            p.share_out()
        if self.done is not None:
            self.done(self)


def _carry_of(carries, key):
    return _Carry.join([make() for make in carries.get(key, [])])


def _pcall_carrying(body, carry, **kw):
    if carry is None:
        return _pcall(body, **kw)
    in_specs = list(kw.pop("in_specs"))
    out_specs, out_shape = kw.pop("out_specs"), kw.pop("out_shape")
    single = not isinstance(out_shape, (list, tuple))
    out_specs = [out_specs] if single else list(out_specs)
    out_shape = [out_shape] if single else list(out_shape)
    scratch = list(kw.pop("scratch_shapes", []))
    grid = tuple(kw.get("grid", ()))
    n_in, n_out, n_scr = len(in_specs), len(out_specs), len(scratch)
    ns, nb, nn = len(carry.srcs), len(carry.bufs), len(carry.new)

    def body2(*refs):
        ins, srcs = refs[:n_in], refs[n_in:n_in + ns]
        at = n_in + ns + nb
        outs, bufs, new = refs[at:at + n_out], refs[at + n_out:at + n_out + nb], refs[at + n_out + nb:at + n_out + nb + nn]
        at += n_out + nb + nn
        scr, (send_sems, recv_sems) = refs[at:at + n_scr], refs[at + n_scr:]

        def copies():
            return [_remote_copy(s, d, send_sems, recv_sems, k, to)
                    for k, (s, d, to) in enumerate(carry.plan(srcs, bufs, new))]

        def start():
            for cp in copies():
                cp.start()

        def wait():
            for cp in copies():
                cp.wait()

        if grid:
            ids = [pl.program_id(a) for a in range(len(grid))]
            pl.when(functools.reduce(jnp.logical_and, [i == 0 for i in ids]))(start)
        else:
            start()
        body(*ins, *outs, *scr)
        if grid:
            pl.when(functools.reduce(jnp.logical_and, [i == g - 1 for i, g in zip(ids, grid)]))(wait)
        else:
            wait()

    call = _pcall(
        body2, in_specs=in_specs + [HBM_SPEC] * (ns + nb), out_specs=out_specs + [HBM_SPEC] * (nb + nn),
        out_shape=out_shape + [jax.ShapeDtypeStruct(b.shape, b.dtype) for b in carry.bufs] + carry.new,
        scratch_shapes=scratch + [pltpu.SemaphoreType.DMA((carry.count,)), pltpu.SemaphoreType.DMA((carry.count,))],
        input_output_aliases={n_in + ns + j: n_out + j for j in range(nb)}, **kw)

    def apply(*args):
        res = call(*args, *carry.srcs, *carry.bufs)
        carry.out = (list(res[n_out:n_out + nb]), list(res[n_out + nb:]))
        carry.share_out()
        return res[0] if single else list(res[:n_out])

    return apply


def _pick(n, cap, mult=128):
    best = None
    d = mult
    while d <= min(n, cap):
        if n % d == 0:
            best = d
        d += mult
    return n if best is None else best


def _nbytes(shape, dtype):
    return int(np.prod(shape)) * jnp.dtype(dtype).itemsize


def _mm(a, b, *, mode, name, out_dtype=F32, alpha=1.0, bias=None, res=None, tm_cap=1024, tn_cap=512,
        tk_cap=2816, gathered=None, out_split=None, carry=None, norm_bwd=None):
    if mode == "tn":
        K, M = a.shape
    else:
        M, K = a.shape
    dn = {"nn": NN, "nt": NT, "tn": TN}[mode]
    b_rows = None
    if gathered is None:
        N = b.shape[0] if mode == "nt" else b.shape[1]
        tn = {None: _pick(N, tn_cap), "col": N // N_CHIPS, "row": N // 2}[out_split]
        if out_split == "col":
            tm_cap = min(tm_cap, M // 2)
        tk = K if K <= tk_cap else _pick(K, tk_cap)
        b_spec = (pl.BlockSpec((tn, tk), lambda i, j, k: (j, k)) if mode == "nt"
                  else pl.BlockSpec((tk, tn), lambda i, j, k: (k, j)))
    else:
        r, c = b.shape[1:]
        rows, cols = (r, N_CHIPS * c) if gathered == "col" else (N_CHIPS * r, c)
        N = rows if mode == "nt" else cols
        assert K == (cols if mode == "nt" else rows), (name, K, rows, cols)
        if gathered == "col" and mode == "nn":
            tn, tk = c, (r if r <= tk_cap else _pick(r, tk_cap))
            b_spec = pl.BlockSpec((None, tk, c), lambda i, j, k: (j, k, 0))
        elif gathered == "col":
            tn, tk = _pick(r, tn_cap), c
            b_spec = pl.BlockSpec((None, tn, c), lambda i, j, k: (k, j, 0))
        elif mode == "nn":
            tn, tk, b_rows = _pick(c, tn_cap), K, N_CHIPS
            b_spec = pl.BlockSpec((N_CHIPS, r, tn), lambda i, j, k: (0, 0, j))
        else:
            tn, tk, b_rows = 2 * r, K, 2
            b_spec = pl.BlockSpec((2, r, c), lambda i, j, k: (j, 0, 0))
    tm = _pick(M, tm_cap)
    nk = K // tk

    a_spec = (pl.BlockSpec((tk, tm), lambda i, j, k: (k, i)) if mode == "tn"
              else pl.BlockSpec((tm, tk), lambda i, j, k: (i, k)))
    in_specs = [a_spec, b_spec]
    args = [a, b]
    if bias is not None:
        in_specs.append(pl.BlockSpec((1, tn), lambda i, j, k: (0, j)))
        args.append(bias)
    if res is not None:
        in_specs.append(pl.BlockSpec((tm, tn), lambda i, j, k: (i, j)))
        args.append(res)
    if norm_bwd is not None:
        assert tn == N and alpha == 1.0 and out_split is None and bias is None and res is None, name
        x_in, g_in, dres_in = norm_bwd
        in_specs += [pl.BlockSpec((tm, tn), lambda i, j, k: (i, 0)), pl.BlockSpec((1, tn), lambda i, j, k: (0, 0)),
                     pl.BlockSpec((tm, tn), lambda i, j, k: (i, 0))]
        args += [x_in, g_in, dres_in]

    est = 2 * (_nbytes((tm, tk), a.dtype) + _nbytes((tk, tn), b.dtype) + _nbytes((tm, tn), out_dtype))
    est += _nbytes((tm, tn), F32) * (3 if res is not None else 1)
    est += _nbytes((tm, tn), F32) * (4 if norm_bwd is not None else 0)
    assert est < VMEM_BUDGET, (name, est)

    def body(*refs):
        a_ref, b_ref = refs[0], refs[1]
        pos = 2
        bias_ref = res_ref = None
        if bias is not None:
            bias_ref = refs[pos]
            pos += 1
        if res is not None:
            res_ref = refs[pos]
            pos += 1
        if norm_bwd is not None:
            x_ref, g_ref, dres_ref = refs[pos:pos + 3]
            pos += 3
        o_ref = refs[pos]
        if norm_bwd is not None:
            dg_ref = refs[pos + 1]
            pos += 1
        acc_ref = refs[pos + 1] if nk > 1 else None

        bv = b_ref[...]
        if b_rows is not None:
            bv = bv.reshape(b_rows * bv.shape[1], bv.shape[2])
        part = lax.dot_general(a_ref[...].astype(BF16), bv.astype(BF16), dn, preferred_element_type=F32)

        def finish(acc):
            if alpha != 1.0:
                acc = acc * alpha
            if bias_ref is not None:
                acc = acc + bias_ref[...]
            if res_ref is not None:
                acc = acc + res_ref[...]
            if norm_bwd is not None:
                xv = x_ref[...]
                rstd = lax.rsqrt(jnp.mean(xv * xv, axis=-1, keepdims=True) + RMS_EPS)
                xhat = xv * rstd
                dyg = acc * g_ref[...]
                part_g = jnp.sum(acc * xhat, axis=0, keepdims=True)
                acc = dres_ref[...] + rstd * (dyg - xhat * jnp.mean(dyg * xhat, axis=-1, keepdims=True))
                first = pl.program_id(0) == 0

                @pl.when(first)
                def _():
                    dg_ref[...] = part_g

                @pl.when(jnp.logical_not(first))
                def _():
                    dg_ref[...] += part_g
            o_ref[...] = acc.astype(out_dtype)

        if nk == 1:
            finish(part)
        else:
            k = pl.program_id(2)

            @pl.when(k == 0)
            def _():
                acc_ref[...] = part

            @pl.when(k > 0)
            def _():
                acc_ref[...] += part

            @pl.when(k == nk - 1)
            def _():
                finish(acc_ref[...])

    if out_split == "col":
        per_half = M // 2 // tm
        out_spec = pl.BlockSpec((None, None, tm, tn), lambda i, j, k: (i // per_half, j, i % per_half, 0))
        out_shape = jax.ShapeDtypeStruct((2, N_CHIPS, M // 2, tn), out_dtype)
    elif out_split == "row":
        out_spec = pl.BlockSpec((None, tm, tn), lambda i, j, k: (j, i, 0))
        out_shape = jax.ShapeDtypeStruct((2, M, tn), out_dtype)
    else:
        out_spec = pl.BlockSpec((tm, tn), lambda i, j, k: (i, j))
        out_shape = jax.ShapeDtypeStruct((M, N), out_dtype)
    semantics = ("parallel", "parallel", "arbitrary")
    if norm_bwd is not None:
        out_spec = [out_spec, pl.BlockSpec((1, tn), lambda i, j, k: (0, 0))]
        out_shape = [out_shape, jax.ShapeDtypeStruct((1, N), F32)]
        semantics = ("arbitrary", "arbitrary", "arbitrary")
    return _pcall_carrying(
        body, carry, name=name, grid=(M // tm, N // tn, nk), in_specs=in_specs, out_specs=out_spec,
        out_shape=out_shape,
        scratch_shapes=[pltpu.VMEM((tm, tn), F32)] if nk > 1 else [],
        compiler_params=pltpu.CompilerParams(dimension_semantics=semantics),
    )(*args)


def _rmsnorm_fwd(x, g, name):
    T, Dm = x.shape
    tm = _pick(T, 256, 8)

    def body(x_ref, g_ref, h_ref):
        xv = x_ref[...]
        rstd = lax.rsqrt(jnp.mean(xv * xv, axis=-1, keepdims=True) + RMS_EPS)
        h_ref[...] = (xv * rstd * g_ref[...]).astype(BF16)

    return _pcall(
        body, name=name, grid=(T // tm,),
        in_specs=[pl.BlockSpec((tm, Dm), lambda i: (i, 0)), pl.BlockSpec((1, Dm), lambda i: (0, 0))],
        out_specs=pl.BlockSpec((tm, Dm), lambda i: (i, 0)),
        out_shape=jax.ShapeDtypeStruct((T, Dm), BF16),
    )(x, g)


def _final_loss(x, g, tgt, name):
    T, Dm = x.shape
    tm = _pick(T, 256, 8)

    def body(x_ref, g_ref, t_ref, dx_ref, dg_ref, loss_ref):
        xv = x_ref[...]
        gv = g_ref[...]
        rstd = lax.rsqrt(jnp.mean(xv * xv, axis=-1, keepdims=True) + RMS_EPS)
        xhat = xv * rstd
        err = xhat * gv - t_ref[...]
        part_loss = 0.5 * jnp.sum(jnp.mean(err * err, axis=-1, keepdims=True), axis=0, keepdims=True)
        dy = err * (1.0 / Dm)
        dyg = dy * gv
        dx_ref[...] = rstd * (dyg - xhat * jnp.mean(dyg * xhat, axis=-1, keepdims=True))
        part_g = jnp.sum(dy * xhat, axis=0, keepdims=True)
        part_l = jnp.broadcast_to(part_loss, (1, 128))

        @pl.when(pl.program_id(0) == 0)
        def _():
            dg_ref[...] = part_g
            loss_ref[...] = part_l

        @pl.when(pl.program_id(0) > 0)
        def _():
            dg_ref[...] += part_g
            loss_ref[...] += part_l

    row = pl.BlockSpec((tm, Dm), lambda i: (i, 0))
    vec = pl.BlockSpec((1, Dm), lambda i: (0, 0))
    return _pcall(
        body, name=name, grid=(T // tm,), in_specs=[row, vec, row],
        out_specs=[row, vec, pl.BlockSpec((1, 128), lambda i: (0, 0))],
        out_shape=[jax.ShapeDtypeStruct((T, Dm), F32), jax.ShapeDtypeStruct((1, Dm), F32),
                   jax.ShapeDtypeStruct((1, 128), F32)],
        compiler_params=pltpu.CompilerParams(dimension_semantics=("arbitrary",)),
    )(x, g, tgt)


def _sigmoid(z):
    return 1.0 / (1.0 + jnp.exp(-z))


def _ffn_in_swiglu(x, g, w_in, name, carry=None):
    T, Dm = x.shape
    cw = w_in.shape[2]
    tm = _pick(T, 512, 16)

    def body(x_ref, gain_ref, wg_ref, wu_ref, h_ref, g_ref, u_ref, a_ref):
        @pl.when(pl.program_id(1) == 0)
        def _():
            xv = x_ref[...]
            rstd = lax.rsqrt(jnp.mean(xv * xv, axis=-1, keepdims=True) + RMS_EPS)
            h_ref[...] = (xv * rstd * gain_ref[...]).astype(BF16)

        hv = h_ref[...]
        gv = jnp.dot(hv, wg_ref[...], preferred_element_type=F32)
        uv = jnp.dot(hv, wu_ref[...], preferred_element_type=F32)
        g_ref[...] = gv.astype(BF16)
        u_ref[...] = uv.astype(BF16)
        a_ref[...] = (gv * _sigmoid(gv) * uv).astype(BF16)

    row = pl.BlockSpec((tm, Dm), lambda i, j: (i, 0))
    out = pl.BlockSpec((tm, cw), lambda i, j: (i, j))
    return _pcall_carrying(
        body, carry, name=name, grid=(T // tm, 2),
        in_specs=[row, pl.BlockSpec((1, Dm), lambda i, j: (0, 0)), pl.BlockSpec((None, Dm, cw), lambda i, j: (j, 0, 0)),
                  pl.BlockSpec((None, Dm, cw), lambda i, j: (j + 2, 0, 0))],
        out_specs=[row, out, out, out],
        out_shape=[jax.ShapeDtypeStruct((T, Dm), BF16)] + [jax.ShapeDtypeStruct((T, D_FF), BF16)] * 3,
        compiler_params=pltpu.CompilerParams(dimension_semantics=("parallel", "arbitrary")),
    )(x, g, w_in, w_in)


def _swiglu_bwd(g, u, da, name):
    T = g.shape[0]
    tm = _pick(T, 256, 16)

    def body(g_ref, u_ref, da_ref, d_ref):
        gv = g_ref[...].astype(F32)
        uv = u_ref[...].astype(F32)
        dav = da_ref[...].astype(F32)
        sg = _sigmoid(gv)
        d_ref[:, :D_FF] = (dav * uv * (sg + gv * sg * (1.0 - sg))).astype(BF16)
        d_ref[:, D_FF:] = (dav * gv * sg).astype(BF16)

    row = pl.BlockSpec((tm, D_FF), lambda i: (i, 0))
    return _pcall(
        body, name=name, grid=(T // tm,), in_specs=[row, row, row],
        out_specs=pl.BlockSpec((tm, 2 * D_FF), lambda i: (i, 0)),
        out_shape=jax.ShapeDtypeStruct((T, 2 * D_FF), BF16),
    )(g, u, da)


def _branches_fwd(p, outs, weights, name):
    T = p.shape[0]
    cw = weights[0].shape[2]
    tm = _pick(T, 1024, 16)
    first_gate, per_branch = QKV_WIDTH // cw, D_MODEL // cw

    def body(*refs):
        gates, o_refs, w_refs, y_refs, m_ref = refs[0:3], refs[3:6], refs[6:9], refs[9:12], refs[12]
        merged = None
        for g_ref, o_ref, w_ref, y_ref in zip(gates, o_refs, w_refs, y_refs):
            y = jnp.dot(o_ref[...], w_ref[...], preferred_element_type=F32)
            y_ref[...] = y
            term = _sigmoid(g_ref[...].astype(F32)) * y
            merged = term if merged is None else merged + term
        m_ref[...] = merged.astype(BF16)

    gate_specs = [pl.BlockSpec((tm, cw), functools.partial(
        lambda i, j, kk: (i, first_gate + per_branch * kk + j), kk=kk)) for kk in range(3)]
    o_specs = [pl.BlockSpec((tm, o.shape[1]), lambda i, j: (i, 0)) for o in outs]
    w_specs = [pl.BlockSpec((None, wk.shape[1], cw), lambda i, j: (j, 0, 0)) for wk in weights]
    tile = pl.BlockSpec((tm, cw), lambda i, j: (i, j))
    return _pcall(
        body, name=name, grid=(T // tm, N_CHIPS), in_specs=gate_specs + o_specs + w_specs, out_specs=[tile] * 4,
        out_shape=[jax.ShapeDtypeStruct((T, D_MODEL), F32)] * 3 + [jax.ShapeDtypeStruct((T, D_MODEL), BF16)],
    )(p, p, p, *outs, *weights)


def _branches_bwd(p, ys, dm, weights, name):
    T = p.shape[0]
    cw = weights[0].shape[2]
    tm = _pick(T, 1024, 16)
    first_gate, per_branch = QKV_WIDTH // cw, D_MODEL // cw
    widths = [wk.shape[1] for wk in weights]

    def body(*refs):
        gates, y_refs, dm_ref, w_refs = refs[0:3], refs[3:6], refs[6], refs[7:10]
        dy_refs, dg_refs, do_refs, acc_refs = refs[10:13], refs[13:16], refs[16:19], refs[19:22]
        j = pl.program_id(1)
        dmv = dm_ref[...]
        for g_ref, y_ref, w_ref, dy_ref, dg_ref, do_ref, acc_ref in zip(gates, y_refs, w_refs, dy_refs, dg_refs,
                                                                       do_refs, acc_refs):
            s = _sigmoid(g_ref[...].astype(F32))
            dy = (s * dmv).astype(BF16)
            dy_ref[...] = dy
            dg_ref[...] = (dmv * y_ref[...] * s * (1.0 - s)).astype(BF16)
            part = _dot_nt(dy, w_ref[...])

            @pl.when(j == 0)
            def _():
                acc_ref[...] = part

            @pl.when(j > 0)
            def _():
                acc_ref[...] += part

            @pl.when(j == N_CHIPS - 1)
            def _():
                do_ref[...] = acc_ref[...].astype(BF16)

    gate_specs = [pl.BlockSpec((tm, cw), functools.partial(
        lambda i, j, kk: (i, first_gate + per_branch * kk + j), kk=kk)) for kk in range(3)]
    tile = pl.BlockSpec((tm, cw), lambda i, j: (i, j))
    w_specs = [pl.BlockSpec((None, r, cw), lambda i, j: (j, 0, 0)) for r in widths]
    do_specs = [pl.BlockSpec((tm, r), lambda i, j: (i, 0)) for r in widths]
    wide = jax.ShapeDtypeStruct((T, D_MODEL), BF16)
    out = _pcall(
        body, name=name, grid=(T // tm, N_CHIPS), in_specs=gate_specs + [tile] * 4 + w_specs,
        out_specs=[tile] * 6 + do_specs,
        out_shape=[wide] * 6 + [jax.ShapeDtypeStruct((T, r), BF16) for r in widths],
        scratch_shapes=[pltpu.VMEM((tm, r), F32) for r in widths],
        compiler_params=pltpu.CompilerParams(dimension_semantics=("parallel", "arbitrary")),
    )(p, p, p, *ys, dm, *weights)
    return out[0:3], out[3:6], out[6:9]


def _branches_dw(outs, dys, name):
    T = dys[0].shape[0]
    cw = D_MODEL // N_CHIPS
    widths = [o.shape[1] for o in outs]

    def body(*refs):
        o_refs, dy_refs, dw_refs = refs[0:3], refs[3:6], refs[6:9]
        for o_ref, dy_ref, dw_ref, r in zip(o_refs, dy_refs, dw_refs, widths):
            dw = _dot_tn(o_ref[...], dy_ref[...]).astype(BF16)
            dw_ref[0] = dw[:r // 2]
            dw_ref[1] = dw[r // 2:]

    return _pcall(
        body, name=name, grid=(N_CHIPS,),
        in_specs=[pl.BlockSpec((T, r), lambda j: (0, 0)) for r in widths] + [pl.BlockSpec((T, cw), lambda j: (0, j))] * 3,
        out_specs=[pl.BlockSpec((2, None, r // 2, cw), lambda j: (0, j, 0, 0)) for r in widths],
        out_shape=[jax.ShapeDtypeStruct((2, N_CHIPS, r // 2, cw), BF16) for r in widths],
    )(*outs, *dys)


def _colsum(a, name):
    ones = jnp.ones((8, a.shape[0]), BF16)
    return _mm(ones, a, mode="nn", name=name, tn_cap=896)[:1]


def _adamw(w, g, m, v, name):
    R, C = w.shape
    tr = 256 if R % 256 == 0 else R
    c1 = 1.0 / (1.0 - ADAM_B1 ** ADAM_STEP)
    c2 = 1.0 / (1.0 - ADAM_B2 ** ADAM_STEP)

    def body(w_ref, g_ref, m_ref, v_ref, d_ref, nm_ref, nv_ref):
        gv = g_ref[...]
        mn = ADAM_B1 * m_ref[...] + (1.0 - ADAM_B1) * gv
        vn = ADAM_B2 * v_ref[...] + (1.0 - ADAM_B2) * (gv * gv)
        nm_ref[...] = mn
        nv_ref[...] = vn
        d_ref[...] = -ADAM_LR * ((mn * c1) / (jnp.sqrt(vn * c2) + ADAM_EPS) + ADAM_WD * w_ref[...])

    spec = pl.BlockSpec((tr, C), lambda i: (i, 0))
    return _pcall(
        body, name=name, grid=(R // tr,), in_specs=[spec] * 4, out_specs=[spec] * 3,
        out_shape=[jax.ShapeDtypeStruct((R, C), F32)] * 3,
    )(w, g, m, v)


def _adamw_from_halves(w, m, v, halves, axis, core, name):
    L, r, c = w.shape
    assert L == len(halves) == 2
    c1 = 1.0 / (1.0 - ADAM_B1 ** ADAM_STEP)
    c2 = 1.0 / (1.0 - ADAM_B2 ** ADAM_STEP)
    if axis == 0:
        tr = _pick(r // 2, 256, 8)
        per_half = r // 2 // tr
        steps = 2 * per_half

        def half_spec(layer, is_mine):
            def index(l, i, core_ref):
                read = jnp.logical_and(l == layer, (i // per_half == core_ref[0]) == is_mine)
                return jnp.where(read, i % per_half, 0), 0
            return pl.BlockSpec((tr, c), index)
    else:
        tr = _pick(r, 256, 8)
        steps = r // tr

        def half_spec(layer, is_mine):
            return pl.BlockSpec((tr, c // 2), lambda l, i, core_ref: (jnp.where(l == layer, i, 0), 0))
    whole = pl.BlockSpec((None, tr, c), lambda l, i, core_ref: (l, i, 0))
    half_specs = [half_spec(layer, is_mine) for layer in range(L) for is_mine in (True, False)]

    def body(core_ref, w_ref, m_ref, v_ref, mine0, theirs0, mine1, theirs1, g_ref, d_ref, nm_ref, nv_ref):
        shape = mine0.shape
        layer0 = jnp.full(shape, pl.program_id(0), jnp.int32) == 0
        mine = jnp.where(layer0, mine0[...], mine1[...])
        theirs = jnp.where(layer0, theirs0[...], theirs1[...])
        core_v = jnp.full(shape, core_ref[0], jnp.int32)

        def update(gv, cols):
            mn = ADAM_B1 * m_ref[:, cols] + (1.0 - ADAM_B1) * gv
            vn = ADAM_B2 * v_ref[:, cols] + (1.0 - ADAM_B2) * (gv * gv)
            g_ref[:, cols] = gv
            nm_ref[:, cols] = mn
            nv_ref[:, cols] = vn
            d_ref[:, cols] = -ADAM_LR * ((mn * c1) / (jnp.sqrt(vn * c2) + ADAM_EPS) + ADAM_WD * w_ref[:, cols])

        if axis == 0:
            here = jnp.full(shape, pl.program_id(1) // per_half, jnp.int32)
            update(jnp.where(here == core_v, mine, theirs), slice(None))
        else:
            update(jnp.where(core_v == 0, mine, theirs), slice(0, c // 2))
            update(jnp.where(core_v == 0, theirs, mine), slice(c // 2, c))

    grid_spec = pltpu.PrefetchScalarGridSpec(
        num_scalar_prefetch=1, grid=(L, steps), in_specs=[whole] * 3 + half_specs, out_specs=[whole] * 4)
    return _pcall(body, name=name, grid_spec=grid_spec, out_shape=[jax.ShapeDtypeStruct((L, r, c), F32)] * 4,
                  )(core, w, m, v, *halves[0], *halves[1])


def _log_sigmoid(z):
    return jnp.minimum(z, 0.0) - jnp.log(1.0 + jnp.exp(-jnp.abs(z)))


def _dot3(x, m01):
    x1 = x.astype(BF16)
    r1 = x - x1.astype(F32)
    x2 = r1.astype(BF16)
    x3 = (r1 - x2.astype(F32)).astype(BF16)
    n = x.shape[0]
    if n % 16:
        return (jnp.dot(x1, m01, preferred_element_type=F32) + jnp.dot(x2, m01, preferred_element_type=F32)
                + jnp.dot(x3, m01, preferred_element_type=F32))
    y = jnp.dot(jnp.concatenate([x1, x2], axis=0), m01, preferred_element_type=F32)
    return y[:n] + y[n:]


def _dot_nt(a, b):
    return lax.dot_general(a, b, NT, preferred_element_type=F32)


def _dot_tn(a, b):
    return lax.dot_general(a, b, TN, preferred_element_type=F32)


def _iota2(shape):
    return lax.broadcasted_iota(jnp.int32, shape, 0), lax.broadcasted_iota(jnp.int32, shape, 1)


HEADS_PER_STEP = 4
HEADS = range(HEADS_PER_STEP)


CHUNK_HEADS_PER_STEP = 4
CHUNK_HEADS = range(CHUNK_HEADS_PER_STEP)


def _head_spec(T, rows=None, width=HEAD_DIM, heads=HEADS_PER_STEP):
    return pl.BlockSpec((heads, T if rows is None else rows, width), lambda g: (g, 0, 0))


def _sb_weights(qb, kb, t0, s0, racc, m_gt, row, col, diagonal):
    z = _dot_nt(qb, kb) * SCALE
    lb = _log_sigmoid(z)
    strict = (s0 + col) < (t0 + row) if diagonal else None
    lf = jnp.where(strict, lb - z, 0.0) if diagonal else lb - z
    w = jnp.exp(lb + _dot3(lf, m_gt) + racc)
    return strict, lb, lf, (jnp.where(strict, w, 0.0) if diagonal else w)


def _sb_fwd(q, k, v, name, carry=None):
    H, T, _ = q.shape
    nb = T // QB

    def body(q_ref, k_ref, v_ref, o_ref):
        row, col = _iota2((QB, QB))
        m_gt = (row > col).astype(BF16)

        def qblock(i, _):
            t0 = pl.multiple_of(i * QB, QB)
            qbs = [q_ref[h, pl.ds(t0, QB), :].astype(BF16) for h in HEADS]

            def kblock(j, carry, diagonal):
                s0 = pl.multiple_of(j * QB, QB)
                out = []
                for h in HEADS:
                    acc, racc = carry[h]
                    kb = k_ref[h, pl.ds(s0, QB), :].astype(BF16)
                    vb = v_ref[h, pl.ds(s0, QB), :].astype(BF16)
                    _, _, lf, w = _sb_weights(qbs[h], kb, t0, s0, racc, m_gt, row, col, diagonal)
                    acc = acc + jnp.dot(w.astype(BF16), vb, preferred_element_type=F32)
                    out.append((acc, racc + jnp.sum(lf, axis=1, keepdims=True)))
                return tuple(out)

            res = kblock(i, tuple((jnp.zeros((QB, HEAD_DIM), F32), jnp.zeros((QB, 1), F32)) for _ in HEADS), True)
            res = lax.fori_loop(1, i + 1, lambda jj, carry: kblock(i - jj, carry, False), res)
            for h in HEADS:
                o_ref[h, pl.ds(t0, QB), :] = res[h][0].astype(BF16)
            return 0

        lax.fori_loop(0, nb, qblock, 0)

    spec = _head_spec(T)
    return _pcall_carrying(body, carry, name=name, grid=(H // HEADS_PER_STEP,), in_specs=[spec] * 3, out_specs=spec,
                           out_shape=jax.ShapeDtypeStruct((H, T, HEAD_DIM), BF16))(q, k, v)


def _sb_bwd(q, k, v, do, name, carry=None):
    H, T, _ = q.shape
    nb = T // QB

    def body(q_ref, k_ref, v_ref, do_ref, dq_ref, dk_out, dv_out, racc_ref, dk_ref, dv_ref):
        row, col = _iota2((QB, QB))
        m_gt = (row > col).astype(BF16)
        m_lt = (row < col).astype(BF16)
        dk_ref[...] = jnp.zeros_like(dk_ref)
        dv_ref[...] = jnp.zeros_like(dv_ref)

        def qblock(i, _):
            t0 = pl.multiple_of(i * QB, QB)
            qbs = [q_ref[h, pl.ds(t0, QB), :].astype(BF16) for h in HEADS]
            dobs = [do_ref[h, pl.ds(t0, QB), :].astype(BF16) for h in HEADS]

            def suffix(j, raccs, diagonal):
                s0 = pl.multiple_of(j * QB, QB)
                out = []
                for h in HEADS:
                    kb = k_ref[h, pl.ds(s0, QB), :].astype(BF16)
                    z = _dot_nt(qbs[h], kb) * SCALE
                    lf = _log_sigmoid(z) - z
                    if diagonal:
                        lf = jnp.where((s0 + col) < (t0 + row), lf, 0.0)
                    racc_ref[h, j] = raccs[h]
                    out.append(raccs[h] + jnp.sum(lf, axis=1, keepdims=True))
                return tuple(out)

            raccs = suffix(i, tuple(jnp.zeros((QB, 1), F32) for _ in HEADS), True)
            lax.fori_loop(1, i + 1, lambda jj, r: suffix(i - jj, r, False), raccs)

            def kblock(j, carry, diagonal):
                s0 = pl.multiple_of(j * QB, QB)
                out = []
                for h in HEADS:
                    dq, cacc = carry[h]
                    kb = k_ref[h, pl.ds(s0, QB), :].astype(BF16)
                    vb = v_ref[h, pl.ds(s0, QB), :].astype(BF16)
                    strict, lb, _, w = _sb_weights(qbs[h], kb, t0, s0, racc_ref[h, j], m_gt, row, col, diagonal)
                    e = _dot_nt(dobs[h], vb) * w
                    c_left = _dot3(e, m_lt) + cacc
                    sig = jnp.exp(lb)
                    dz = e * (1.0 - sig) - c_left * sig
                    dz = (jnp.where(strict, dz, 0.0) if diagonal else dz).astype(BF16)
                    dq = dq + jnp.dot(dz, kb, preferred_element_type=F32)
                    dk_ref[h, pl.ds(s0, QB), :] += _dot_tn(dz, qbs[h]) * SCALE
                    dv_ref[h, pl.ds(s0, QB), :] += _dot_tn(w.astype(BF16), dobs[h])
                    out.append((dq, cacc + jnp.sum(e, axis=1, keepdims=True)))
                return tuple(out)

            res = lax.fori_loop(0, i, lambda j, carry: kblock(j, carry, False),
                                tuple((jnp.zeros((QB, HEAD_DIM), F32), jnp.zeros((QB, 1), F32)) for _ in HEADS))
            res = kblock(i, res, True)
            for h in HEADS:
                dq_ref[h, pl.ds(t0, QB), :] = (res[h][0] * SCALE).astype(BF16)
            return 0

        lax.fori_loop(0, nb, qblock, 0)
        dk_out[...] = dk_ref[...].astype(BF16)
        dv_out[...] = dv_ref[...].astype(BF16)

    spec = _head_spec(T)
    acc = pltpu.VMEM((HEADS_PER_STEP, T, HEAD_DIM), F32)
    return _pcall_carrying(body, carry, name=name, grid=(H // HEADS_PER_STEP,), in_specs=[spec] * 4,
                           out_specs=[spec] * 3, out_shape=[jax.ShapeDtypeStruct((H, T, HEAD_DIM), BF16)] * 3,
                           scratch_shapes=[pltpu.VMEM((HEADS_PER_STEP, nb, QB, 1), F32), acc, acc])(q, k, v, do)


def _fox_logits(qb, kb, fq, fk, t0, s0, row, col):
    z = _dot_nt(qb, kb) * SCALE + fq - fk
    return jnp.where((s0 + col) <= (t0 + row), z, NEG)


def _fox_specs(T):
    return _head_spec(T, width=1), _head_spec(T, rows=T // QB, width=QB)


def _fox_fwd(q, k, v, fq, fk, name, carry=None):
    H, T, _ = q.shape
    nb = T // QB

    def body(q_ref, k_ref, v_ref, fq_ref, fk_ref, o_ref, lse_ref):
        row, col = _iota2((QB, QB))

        def qblock(i, _):
            t0 = pl.multiple_of(i * QB, QB)
            qbs = [q_ref[h, pl.ds(t0, QB), :].astype(BF16) for h in HEADS]
            fqs = [fq_ref[h, pl.ds(t0, QB), :] for h in HEADS]

            def kblock(j, carry):
                s0 = pl.multiple_of(j * QB, QB)
                out = []
                for h in HEADS:
                    acc, m, l = carry[h]
                    kb = k_ref[h, pl.ds(s0, QB), :].astype(BF16)
                    vb = v_ref[h, pl.ds(s0, QB), :].astype(BF16)
                    z = _fox_logits(qbs[h], kb, fqs[h], fk_ref[h, pl.ds(j, 1), :], t0, s0, row, col)
                    m_new = jnp.maximum(m, jnp.max(z, axis=1, keepdims=True))
                    a = jnp.exp(m - m_new)
                    p = jnp.exp(z - m_new)
                    acc = a * acc + jnp.dot(p.astype(BF16), vb, preferred_element_type=F32)
                    out.append((acc, m_new, a * l + jnp.sum(p, axis=1, keepdims=True)))
                return tuple(out)

            res = lax.fori_loop(0, i + 1, kblock,
                                tuple((jnp.zeros((QB, HEAD_DIM), F32), jnp.full((QB, 1), NEG, F32),
                                       jnp.zeros((QB, 1), F32)) for _ in HEADS))
            for h in HEADS:
                acc, m, l = res[h]
                o_ref[h, pl.ds(t0, QB), :] = (acc / l).astype(BF16)
                lse_ref[h, pl.ds(t0, QB), :] = m + jnp.log(l)
            return 0

        lax.fori_loop(0, nb, qblock, 0)

    spec = _head_spec(T)
    fq_spec, fk_spec = _fox_specs(T)
    return _pcall_carrying(
        body, carry, name=name, grid=(H // HEADS_PER_STEP,), in_specs=[spec] * 3 + [fq_spec, fk_spec],
        out_specs=[spec, fq_spec],
        out_shape=[jax.ShapeDtypeStruct((H, T, HEAD_DIM), BF16), jax.ShapeDtypeStruct((H, T, 1), F32)],
    )(q, k, v, fq, fk)


def _fox_bwd(q, k, v, fq, fk, lse, do, name, carry=None):
    H, T, _ = q.shape
    nb = T // QB

    def body(q_ref, k_ref, v_ref, fq_ref, fk_ref, lse_ref, do_ref, dq_ref, dk_out, dv_out, dfk_ref, dk_ref, dv_ref):
        row, col = _iota2((QB, QB))
        dk_ref[...] = jnp.zeros_like(dk_ref)
        dv_ref[...] = jnp.zeros_like(dv_ref)
        dfk_ref[...] = jnp.zeros_like(dfk_ref)

        def qblock(i, _):
            t0 = pl.multiple_of(i * QB, QB)
            qbs = [q_ref[h, pl.ds(t0, QB), :].astype(BF16) for h in HEADS]
            fqs = [fq_ref[h, pl.ds(t0, QB), :] for h in HEADS]
            lses = [lse_ref[h, pl.ds(t0, QB), :] for h in HEADS]
            dobs = [do_ref[h, pl.ds(t0, QB), :].astype(BF16) for h in HEADS]

            def probs(h, j):
                s0 = pl.multiple_of(j * QB, QB)
                kb = k_ref[h, pl.ds(s0, QB), :].astype(BF16)
                vb = v_ref[h, pl.ds(s0, QB), :].astype(BF16)
                z = _fox_logits(qbs[h], kb, fqs[h], fk_ref[h, pl.ds(j, 1), :], t0, s0, row, col)
                return s0, kb, jnp.exp(z - lses[h]), _dot_nt(dobs[h], vb)

            def row_dot(j, deltas):
                out = []
                for h in HEADS:
                    _, _, p, dp = probs(h, j)
                    out.append(deltas[h] + jnp.sum(p * dp, axis=1, keepdims=True))
                return tuple(out)

            deltas = lax.fori_loop(0, i + 1, row_dot, tuple(jnp.zeros((QB, 1), F32) for _ in HEADS))

            def kblock(j, dqs):
                out = []
                for h in HEADS:
                    s0, kb, p, dp = probs(h, j)
                    dz = p * (dp - deltas[h])
                    dzb = dz.astype(BF16)
                    dk_ref[h, pl.ds(s0, QB), :] += _dot_tn(dzb, qbs[h]) * SCALE
                    dv_ref[h, pl.ds(s0, QB), :] += _dot_tn(p.astype(BF16), dobs[h])
                    dfk_ref[h, pl.ds(j, 1), :] += -jnp.sum(dz, axis=0, keepdims=True)
                    out.append(dqs[h] + jnp.dot(dzb, kb, preferred_element_type=F32))
                return tuple(out)

            dqs = lax.fori_loop(0, i + 1, kblock, tuple(jnp.zeros((QB, HEAD_DIM), F32) for _ in HEADS))
            for h in HEADS:
                dq_ref[h, pl.ds(t0, QB), :] = (dqs[h] * SCALE).astype(BF16)
            return 0

        lax.fori_loop(0, nb, qblock, 0)
        dk_out[...] = dk_ref[...].astype(BF16)
        dv_out[...] = dv_ref[...].astype(BF16)

    spec = _head_spec(T)
    fq_spec, fk_spec = _fox_specs(T)
    acc = pltpu.VMEM((HEADS_PER_STEP, T, HEAD_DIM), F32)
    return _pcall_carrying(body, carry, name=name, grid=(H // HEADS_PER_STEP,),
                           in_specs=[spec] * 3 + [fq_spec, fk_spec, fq_spec, spec],
                           out_specs=[spec, spec, spec, fk_spec],
                           out_shape=[jax.ShapeDtypeStruct((H, T, HEAD_DIM), BF16)] * 3
                           + [jax.ShapeDtypeStruct((H, nb, QB), F32)],
                           scratch_shapes=[acc, acc])(q, k, v, fq, fk, lse, do)


def _forget_cumsum(flog, name):
    R, T = flog.shape
    nb = T // QB

    def body(x_ref, o_ref):
        row, col = _iota2((QB, QB))
        m_le = (row <= col).astype(BF16)
        carry = jnp.zeros((R, 1), F32)
        for b in range(nb):
            lf = _log_sigmoid(x_ref[:, b * QB:(b + 1) * QB])
            o_ref[:, b * QB:(b + 1) * QB] = _dot3(lf, m_le) + carry
            carry = carry + jnp.sum(lf, axis=1, keepdims=True)

    spec = pl.BlockSpec((R, T), lambda: (0, 0))
    return _pcall(body, name=name, in_specs=[spec], out_specs=spec,
                  out_shape=jax.ShapeDtypeStruct((R, T), F32))(flog)


def _forget_cumsum_bwd(flog, df, name):
    R, T = flog.shape
    nb = T // QB

    def body(x_ref, df_ref, o_ref):
        row, col = _iota2((QB, QB))
        m_ge = (row >= col).astype(BF16)
        carry = jnp.zeros((R, 1), F32)
        for b in reversed(range(nb)):
            dfb = df_ref[:, b * QB:(b + 1) * QB]
            dlog = _dot3(dfb, m_ge) + carry
            o_ref[:, b * QB:(b + 1) * QB] = dlog * _sigmoid(-x_ref[:, b * QB:(b + 1) * QB])
            carry = carry + jnp.sum(dfb, axis=1, keepdims=True)

    spec = pl.BlockSpec((R, T), lambda: (0, 0))
    return _pcall(body, name=name, in_specs=[spec, spec], out_specs=spec,
                  out_shape=jax.ShapeDtypeStruct((R, T), F32))(flog, df)


def _chunk_probs(qc, kb, bias, c, col):
    z = _dot_nt(qc, kb) * SCALE + bias
    z = jnp.where((c - (LEFT_CHUNKS + 1)) * CHUNK + col >= jnp.maximum(0, (c - LEFT_CHUNKS) * CHUNK), z, NEG)
    p = jnp.exp(z - jnp.max(z, axis=1, keepdims=True))
    return p, jnp.sum(p, axis=1, keepdims=True)


def _chunk_specs(T, n=CHUNK_HEADS_PER_STEP):
    return (_head_spec(T, heads=n), _head_spec(T, rows=T + BAND_PAD, heads=n),
            _head_spec(T, rows=CHUNK, width=BAND, heads=n))


def _chunk_fwd(q, kp, vp, bias, name, carry=None):
    H, T, _ = q.shape
    nc = T // CHUNK

    def body(q_ref, kp_ref, vp_ref, bias_ref, o_ref):
        _, col = _iota2((CHUNK, BAND))

        def chunk(c, _):
            t0 = pl.multiple_of(c * CHUNK, CHUNK)
            for h in HEADS:
                qc = q_ref[h, pl.ds(t0, CHUNK), :].astype(BF16)
                kb = kp_ref[h, pl.ds(t0, BAND), :].astype(BF16)
                vb = vp_ref[h, pl.ds(t0, BAND), :].astype(BF16)
                p, l = _chunk_probs(qc, kb, bias_ref[h], c, col)
                o = jnp.dot(p.astype(BF16), vb, preferred_element_type=F32) / l
                o_ref[h, pl.ds(t0, CHUNK), :] = o.astype(BF16)
            return 0

        lax.fori_loop(0, nc, chunk, 0)

    q_spec, kp_spec, b_spec = _chunk_specs(T, HEADS_PER_STEP)
    return _pcall_carrying(body, carry, name=name, grid=(H // HEADS_PER_STEP,),
                           in_specs=[q_spec, kp_spec, kp_spec, b_spec], out_specs=q_spec,
                           out_shape=jax.ShapeDtypeStruct((H, T, HEAD_DIM), BF16))(q, kp, vp, bias)


def _chunk_bwd(q, kp, vp, bias, do, name, carry=None):
    H, T, _ = q.shape
    nc = T // CHUNK

    def body(q_ref, kp_ref, vp_ref, bias_ref, do_ref, dq_ref, dk_out, dv_out, db_ref, dkp_ref, dvp_ref):
        _, col = _iota2((CHUNK, BAND))
        dkp_ref[...] = jnp.zeros_like(dkp_ref)
        dvp_ref[...] = jnp.zeros_like(dvp_ref)
        db_ref[...] = jnp.zeros_like(db_ref)

        def chunk(c, _):
            t0 = pl.multiple_of(c * CHUNK, CHUNK)
            for h in CHUNK_HEADS:
                qc = q_ref[h, pl.ds(t0, CHUNK), :].astype(BF16)
                kb = kp_ref[h, pl.ds(t0, BAND), :].astype(BF16)
                vb = vp_ref[h, pl.ds(t0, BAND), :].astype(BF16)
                dob = do_ref[h, pl.ds(t0, CHUNK), :].astype(BF16)
                p, l = _chunk_probs(qc, kb, bias_ref[h], c, col)
                p = p / l
                dp = _dot_nt(dob, vb)
                dz = p * (dp - jnp.sum(p * dp, axis=1, keepdims=True))
                dzb = dz.astype(BF16)
                dq = jnp.dot(dzb, kb, preferred_element_type=F32) * SCALE
                dq_ref[h, pl.ds(t0, CHUNK), :] = dq.astype(BF16)
                dkp_ref[h, pl.ds(t0, BAND), :] += _dot_tn(dzb, qc) * SCALE
                dvp_ref[h, pl.ds(t0, BAND), :] += _dot_tn(p.astype(BF16), dob)
                db_ref[h] += dz
            return 0

        lax.fori_loop(0, nc, chunk, 0)
        dk_out[...] = dkp_ref[:, BAND_PAD:, :].astype(BF16)
        dv_out[...] = dvp_ref[:, BAND_PAD:, :].astype(BF16)

    q_spec, kp_spec, b_spec = _chunk_specs(T)
    acc = pltpu.VMEM((CHUNK_HEADS_PER_STEP, T + BAND_PAD, HEAD_DIM), F32)
    return _pcall_carrying(body, carry, name=name, grid=(H // CHUNK_HEADS_PER_STEP,),
                           in_specs=[q_spec, kp_spec, kp_spec, b_spec, q_spec],
                           out_specs=[q_spec, q_spec, q_spec, b_spec],
                           out_shape=[jax.ShapeDtypeStruct((H, T, HEAD_DIM), BF16)] * 3
                           + [jax.ShapeDtypeStruct((H, CHUNK, BAND), F32)],
                           scratch_shapes=[acc, acc])(q, kp, vp, bias, do)


HBM_SPEC = pl.BlockSpec(memory_space=pltpu.HBM)


def _position():
    return lax.axis_index("x"), lax.axis_index("y"), lax.axis_index("c")


def _other_chips(x, y):
    chips = [(1 - x, y), (x, 1 - y), (1 - x, 1 - y)]
    return [(chip, 2 * chip[0] + chip[1]) for chip in chips]


def _remote_copy(src, dst, send_sems, recv_sems, k, to):
    return pltpu.make_async_remote_copy(src_ref=src, dst_ref=dst, send_sem=send_sems.at[k], recv_sem=recv_sems.at[k],
                                        device_id=to, device_id_type=MESH)


def _place_own(w, chip, name):
    _, r, c = w.shape
    tr = _pick(r, 256, 16)

    def body(chip_ref, w_ref, *out_refs):
        for l, o_ref in enumerate(out_refs):
            o_ref[...] = w_ref[l].astype(BF16)

    grid_spec = pltpu.PrefetchScalarGridSpec(
        num_scalar_prefetch=1, grid=(r // tr,), in_specs=[pl.BlockSpec((DEPTH, tr, c), lambda i, ch: (0, i, 0))],
        out_specs=[pl.BlockSpec((None, tr, c), lambda i, ch: (ch[0], i, 0))] * DEPTH)
    return _pcall(body, name=name, grid_spec=grid_spec,
                  out_shape=[jax.ShapeDtypeStruct((N_CHIPS, r, c), BF16)] * DEPTH)(chip, w)


def _gather_over_ici(srcs, bufs, new):
    x, y, c = _position()
    me = 2 * x + y
    return [(b.at[me, c], b.at[me, c], (*chip, c)) for b in bufs for chip, _ in _other_chips(x, y)]


def _gather_to_sibling(srcs, bufs, new):
    x, y, c = _position()
    return [(b.at[idx, c], b.at[idx, c], (x, y, 1 - c)) for b in bufs for _, idx in _other_chips(x, y)]


def _gather_layer(bufs, name):
    n = len(bufs)

    def body(*refs):
        dst = refs[n:2 * n]
        send_sems, recv_sems = refs[2 * n:]
        x, y, c = _position()
        me = 2 * x + y
        sibling = (x, y, 1 - c)
        others = _other_chips(x, y)
        first = [_remote_copy(dst[i].at[me, c], dst[i].at[me, c], send_sems, recv_sems, 3 * i + k, (*chip, c))
                 for i in range(n) for k, (chip, _) in enumerate(others)]
        for cp in first:
            cp.start()
        passed = []
        for i in range(n):
            for k, (_, idx) in enumerate(others):
                landed = dst[i].at[idx, c]
                _remote_copy(landed, landed, send_sems, recv_sems, 3 * i + k, sibling).wait_recv()
                passed.append(_remote_copy(landed, landed, send_sems, recv_sems, 3 * (n + i) + k, sibling))
                passed[-1].start()
        for i in range(n):
            for k, (_, idx) in enumerate(others):
                from_sibling = dst[i].at[idx, 1 - c]
                _remote_copy(from_sibling, from_sibling, send_sems, recv_sems, 3 * (n + i) + k, sibling).wait_recv()
        for cp in first + passed:
            cp.wait_send()

    return _pcall(
        body, name=name, in_specs=[HBM_SPEC] * n, out_specs=[HBM_SPEC] * n,
        out_shape=[jax.ShapeDtypeStruct(b.shape, b.dtype) for b in bufs],
        scratch_shapes=[pltpu.SemaphoreType.DMA((6 * n,)), pltpu.SemaphoreType.DMA((6 * n,))],
        input_output_aliases={i: i for i in range(n)},
    )(*bufs)


def _send_other_half(parts, name):
    n = len(parts)

    def body(*refs):
        src, got = refs[:n], refs[n:2 * n]
        send_sems, recv_sems = refs[2 * n:]
        x, y, c = _position()
        copies = [_remote_copy(src[i].at[1 - c], got[i], send_sems, recv_sems, i, (x, y, 1 - c)) for i in range(n)]
        for cp in copies:
            cp.start()
        for cp in copies:
            cp.wait()

    return _pcall(
        body, name=name, in_specs=[HBM_SPEC] * n, out_specs=[HBM_SPEC] * n,
        out_shape=[jax.ShapeDtypeStruct(p.shape[1:], p.dtype) for p in parts],
        scratch_shapes=[pltpu.SemaphoreType.DMA((n,)), pltpu.SemaphoreType.DMA((n,))],
    )(*parts)


def _scatter_chips(parts, name):
    n = len(parts)

    def body(*refs):
        src, dst = refs[:n], refs[n:2 * n]
        send_sems, recv_sems = refs[2 * n:]
        x, y, c = _position()
        others = _other_chips(x, y)
        sends = [_remote_copy(src[i].at[idx], dst[i].at[k], send_sems, recv_sems, 3 * i + k, (*chip, c))
                 for i in range(n) for k, (chip, idx) in enumerate(others)]
        for cp in sends:
            cp.start()
        for cp in sends:
            cp.wait()

    return _pcall(
        body, name=name, in_specs=[HBM_SPEC] * n, out_specs=[HBM_SPEC] * n,
        out_shape=[jax.ShapeDtypeStruct((3,) + p.shape[1:], p.dtype) for p in parts],
        scratch_shapes=[pltpu.SemaphoreType.DMA((3 * n,)), pltpu.SemaphoreType.DMA((3 * n,))],
    )(*parts)


def _swap_with_sibling(arrs, name):
    n = len(arrs)

    def body(*refs):
        src, dst = refs[:n], refs[n:2 * n]
        send_sems, recv_sems = refs[2 * n:]
        x, y, c = _position()
        copies = [_remote_copy(src[i], dst[i], send_sems, recv_sems, i, (x, y, 1 - c)) for i in range(n)]
        for cp in copies:
            cp.start()
        for cp in copies:
            cp.wait()

    return _pcall(
        body, name=name, in_specs=[HBM_SPEC] * n, out_specs=[HBM_SPEC] * n,
        out_shape=[jax.ShapeDtypeStruct(a.shape, a.dtype) for a in arrs],
        scratch_shapes=[pltpu.SemaphoreType.DMA((n,)), pltpu.SemaphoreType.DMA((n,))],
    )(*arrs)


def _allreduce_small(v, name):
    R, C = v.shape

    def body(v_ref, o_ref, slots, send_sems, recv_sems):
        x, y, c = _position()
        me = 4 * x + 2 * y + c
        slots[0] = v_ref[...]
        sends = []
        for r in range(1, 8):
            fx, fy, fc = (r >> 2) & 1, (r >> 1) & 1, r & 1
            to = (x ^ fx, y ^ fy, c ^ fc)
            cp = pltpu.make_async_remote_copy(src_ref=v_ref, dst_ref=slots.at[r], send_sem=send_sems.at[r - 1],
                                              recv_sem=recv_sems.at[r - 1], device_id=to, device_id_type=MESH)
            cp.start()
            sends.append(cp)
        for cp in sends:
            cp.wait()
        acc = slots[me]
        for d in range(1, 8):
            acc = acc + slots[d ^ me]
        o_ref[...] = acc

    vmem = pl.BlockSpec(memory_space=pltpu.VMEM)
    return _pcall(
        body, name=name, in_specs=[vmem], out_specs=vmem, out_shape=jax.ShapeDtypeStruct((R, C), F32),
        scratch_shapes=[pltpu.VMEM((8, R, C), F32), pltpu.SemaphoreType.DMA((7,)), pltpu.SemaphoreType.DMA((7,))],
    )(v)


def _add_pair(which, both, got, name):
    _, N, R, C = both.shape
    tr = _pick(R, 512, 16)

    def body(which_ref, a_ref, b_ref, o_ref):
        o_ref[...] = (a_ref[...].astype(F32) + b_ref[...].astype(F32)).astype(BF16)

    spec = pl.BlockSpec((None, tr, C), lambda n, i, w: (n, i, 0))
    grid_spec = pltpu.PrefetchScalarGridSpec(
        num_scalar_prefetch=1, grid=(N, R // tr),
        in_specs=[pl.BlockSpec((None, None, tr, C), lambda n, i, w: (w[0], n, i, 0)), spec], out_specs=spec)
    return _pcall(body, name=name, grid_spec=grid_spec,
                  out_shape=jax.ShapeDtypeStruct((N, R, C), BF16))(which, both, got)


def _sum_chips(which, own, others, name):
    N, R, C = others.shape
    tr = _pick(R, 512, 16)

    def body(which_ref, own_ref, p_ref, o_ref):
        acc = own_ref[...].astype(F32)
        for n in range(N):
            acc = acc + p_ref[n].astype(F32)
        o_ref[...] = acc

    grid_spec = pltpu.PrefetchScalarGridSpec(
        num_scalar_prefetch=1, grid=(R // tr,),
        in_specs=[pl.BlockSpec((None, tr, C), lambda i, w: (w[0], i, 0)), pl.BlockSpec((N, tr, C), lambda i, w: (0, i, 0))],
        out_specs=pl.BlockSpec((tr, C), lambda i, w: (i, 0)))
    return _pcall(body, name=name, grid_spec=grid_spec,
                  out_shape=jax.ShapeDtypeStruct((R, C), F32))(which, own, others)


BIG = ("w_ffn1_in", "w_ffn1_out", "w_in", "w_br_sb", "w_br_ch", "w_br_fox", "w_out", "w_ffn2_in", "w_ffn2_out")
ROW_SHARDED = ("w_ffn1_out", "w_out", "w_ffn2_out")
SMALL = ("g_ffn1", "g_mix", "b_in", "rel_bias", "g_ffn2", "g_final")


def _pair_sums(split, tag):
    core = lax.axis_index("c").astype(jnp.int32)[None]
    got = _send_other_half(split, f"grad_split_{tag}")
    return [_add_pair(core, a, b, f"grad_pair_sum_{tag}_{i}") for i, (a, b) in enumerate(zip(split, got))]


def _scatter_plan(srcs, bufs, new):
    x, y, c = _position()
    return [(s.at[idx], d.at[k], (*chip, c)) for s, d in zip(srcs, new) for k, (chip, idx) in enumerate(_other_chips(x, y))]


def _scatter_carry(pair):
    landing = [jax.ShapeDtypeStruct((3,) + p.shape[1:], p.dtype) for p in pair]
    return _Carry(pair, [], landing, 3 * len(pair), _scatter_plan)


def _finish_grads(pair, landed, tag):
    chip = (2 * lax.axis_index("x") + lax.axis_index("y")).astype(jnp.int32)[None]
    mine = [_sum_chips(chip, p, q, f"grad_chip_sum_{tag}_{i}") for i, (p, q) in enumerate(zip(pair, landed))]
    return mine, _swap_with_sibling(mine, f"grad_share_{tag}")


SMALL_SIZES = {"g_ffn1": DEPTH * D_MODEL, "g_mix": DEPTH * D_MODEL, "b_in": DEPTH * IN_WIDTH,
               "rel_bias": DEPTH * N_REL * H_CH, "g_ffn2": DEPTH * D_MODEL, "g_final": D_MODEL}
SMALL_TOTAL = sum(SMALL_SIZES.values()) + 1
SMALL_ROWS = -(-SMALL_TOTAL // (8 * 128)) * 8


def _pack_small(vals, extra):
    flat = [vals[n].reshape(-1) for n in SMALL] + [extra.reshape(-1)]
    flat.append(jnp.zeros((SMALL_ROWS * 128 - SMALL_TOTAL,), F32))
    return jnp.concatenate(flat).reshape(SMALL_ROWS, 128)


def _unpack_small(pack, shapes):
    flat, out, off = pack.reshape(-1), {}, 0
    for n in SMALL:
        out[n] = flat[off:off + SMALL_SIZES[n]].reshape(shapes[n])
        off += SMALL_SIZES[n]
    return out, flat[off]


def _to_heads(t, n_heads):
    return jnp.transpose(t.reshape(t.shape[0], n_heads, HEAD_DIM), (1, 0, 2)).astype(BF16)


def _from_heads(t):
    return jnp.transpose(t, (1, 0, 2)).reshape(t.shape[1], t.shape[0] * HEAD_DIM)


def _pad_keys(t):
    return jnp.pad(t, ((0, 0), (BAND_PAD, 0), (0, 0)))


DIAGONALS = CHUNK + BAND - 1


def _band_bias(table):
    n_far = (LEFT_CHUNKS + 1) * CHUNK + CHUNK - MAX_REL
    near = table[N_REL - 1 - (DIAGONALS - n_far):N_REL - 1][::-1]
    diag = jnp.concatenate([jnp.broadcast_to(table[N_REL - 1], (n_far, H_CH)), near], axis=0).T
    diag = jnp.pad(diag, ((0, 0), (0, 1)))
    skew = jnp.tile(diag, (1, CHUNK))[:, :CHUNK * DIAGONALS].reshape(H_CH, CHUNK, DIAGONALS)
    return skew[:, :, CHUNK - 1:]


def _pad_w_in(w):
    qkv, f, gates = w[:, :QKV_WIDTH], w[:, QKV_WIDTH:QKV_WIDTH + H_FOX], w[:, QKV_WIDTH + H_FOX:]
    return jnp.concatenate([qkv, gates, f, jnp.zeros((w.shape[0], F_PAD - H_FOX), w.dtype)], axis=1)


def _unpad_w_in(w):
    return jnp.concatenate([w[:, :QKV_WIDTH], w[:, QKV_WIDTH + GATE_WIDTH:QKV_WIDTH + GATE_WIDTH + H_FOX],
                            w[:, QKV_WIDTH:QKV_WIDTH + GATE_WIDTH]], axis=1)


QKV_SPLITS = np.cumsum([0, W_SB, W_SB, W_SB, W_CH, W_CH, W_CH, W_FOX, W_FOX, W_FOX])


def _by_chip_rows(g):
    return g.reshape(2, N_CHIPS, g.shape[1] // N_CHIPS, g.shape[2])


def _ffn_fwd(x, g, w_in, w_out, tag, carries):
    h, gate, up, a = _ffn_in_swiglu(x, g, w_in, f"{tag}_in", _carry_of(carries, "in"))
    y = _mm(a, w_out, mode="nn", name=f"{tag}_out", alpha=0.5, res=x, gathered="row",
            carry=_carry_of(carries, "out"))
    return y, (h, gate, up, a)


def _ffn_bwd(x, g, w_in, w_out, saved, dy, tag, carries):
    h, gate, up, a = saved
    da = _mm(dy, w_out, mode="nt", name=f"{tag}_da", alpha=0.5, gathered="row", out_dtype=BF16,
             carry=_carry_of(carries, "da"))
    dw_out = _mm(a, dy, mode="tn", name=f"{tag}_dwout", alpha=0.5, tm_cap=1408, out_dtype=BF16, out_split="row",
                 carry=_carry_of(carries, "dwout"))
    dgu = _swiglu_bwd(gate, up, da, f"{tag}_dact")
    dw_in = _mm(h, dgu, mode="tn", name=f"{tag}_dwin", tm_cap=512, out_dtype=BF16, out_split="col",
                carry=_carry_of(carries, "dwin"))
    dx, dg = _mm(dgu, w_in, mode="nt", name=f"{tag}_dh", gathered="col", tn_cap=1024, carry=_carry_of(carries, "dh"),
                 norm_bwd=(x, g, dy))
    return dx, dg, dw_in, _by_chip_rows(dw_out)


def _mixer_fwd(x, lw, tag, carries):
    T = x.shape[0]
    h = _rmsnorm_fwd(x, lw["g_mix"], f"{tag}_norm")
    p = _mm(h, lw["w_in"], mode="nn", name=f"{tag}_proj", bias=lw["b_in"], tn_cap=896, out_dtype=BF16,
            carry=_carry_of(carries, "proj"))
    parts = [p[:, QKV_SPLITS[i]:QKV_SPLITS[i + 1]] for i in range(9)]
    qa, ka, va = (_to_heads(t, H_SB) for t in parts[0:3])
    qb, kb, vb = (_to_heads(t, H_CH) for t in parts[3:6])
    qc, kc, vc = (_to_heads(t, H_FOX) for t in parts[6:9])
    flog = _mm(lw["w_forget_t"], h, mode="nt", name=f"{tag}_flog")[:8] + lw["b_forget"]
    fcum = _forget_cumsum(flog, f"{tag}_fcum")[:H_FOX]
    fq, fk = fcum.reshape(H_FOX, T, 1), fcum.reshape(H_FOX, T // QB, QB)
    kbp, vbp = _pad_keys(kb), _pad_keys(vb)
    oa = _sb_fwd(qa, ka, va, f"{tag}_sb", _carry_of(carries, "sb"))
    ob = _chunk_fwd(qb, kbp, vbp, lw["bias"], f"{tag}_ch", _carry_of(carries, "ch"))
    oc, lse = _fox_fwd(qc, kc, vc, fq, fk, f"{tag}_fox", _carry_of(carries, "fox"))
    oam, obm, ocm = _from_heads(oa), _from_heads(ob), _from_heads(oc)
    ya, yb, yc, merged = _branches_fwd(p, (oam, obm, ocm), (lw["w_br_sb"], lw["w_br_ch"], lw["w_br_fox"]),
                                       f"{tag}_branches")
    y = _mm(merged, lw["w_out"], mode="nn", name=f"{tag}_out", res=x, gathered="row")
    saved = (h, p, (qa, ka, va), (qb, kbp, vbp), (qc, kc, vc), flog, fq, fk, lse, (oam, obm, ocm),
             (ya, yb, yc), merged)
    return y, saved


def _mixer_bwd(x, lw, saved, dy, tag, carries):
    (h, p, (qa, ka, va), (qb, kbp, vbp), (qc, kc, vc), flog, fq, fk, lse, (oam, obm, ocm),
     (ya, yb, yc), merged) = saved
    T = x.shape[0]
    grads = {}
    dmerged = _mm(dy, lw["w_out"], mode="nt", name=f"{tag}_dmerged", gathered="row",
                  carry=_carry_of(carries, "dmerged"))
    grads["w_out"] = _by_chip_rows(_mm(merged, dy, mode="tn", name=f"{tag}_dwout", tm_cap=1024, out_dtype=BF16,
                                       out_split="row"))
    (dya, dyb, dyc), dgates, (doa, dob, doc) = _branches_bwd(
        p, (ya, yb, yc), dmerged, (lw["w_br_sb"], lw["w_br_ch"], lw["w_br_fox"]), f"{tag}_dbranches")
    grads["w_br_sb"], grads["w_br_ch"], grads["w_br_fox"] = _branches_dw(
        (oam, obm, ocm), (dya, dyb, dyc), f"{tag}_dwbranches")
    dqa, dka, dva = _sb_bwd(qa, ka, va, _to_heads(doa, H_SB), f"{tag}_dsb", _carry_of(carries, "dsb"))
    dqb, dkb, dvb, dbias = _chunk_bwd(qb, kbp, vbp, lw["bias"], _to_heads(dob, H_CH), f"{tag}_dch",
                                      _carry_of(carries, "dch"))
    dqc, dkc, dvc, dfk = _fox_bwd(qc, kc, vc, fq, fk, lse, _to_heads(doc, H_FOX), f"{tag}_dfox",
                                  _carry_of(carries, "dfox"))
    dflog = _forget_cumsum_bwd(flog, jnp.pad(dfk.reshape(H_FOX, T), ((0, 8 - H_FOX), (0, 0))), f"{tag}_dfcum")
    dqkv = [_from_heads(t) for t in (dqa, dka, dva, dqb, dkb, dvb, dqc, dkc, dvc)]
    dforget = jnp.pad(dflog[:H_FOX].T, ((0, 0), (0, F_PAD - H_FOX))).astype(BF16)
    dpb = jnp.concatenate(dqkv + list(dgates) + [dforget], axis=1)
    grads["b_in"] = _colsum(dpb, f"{tag}_dbin")
    dw_in = _unpad_w_in(_mm(h, dpb, mode="tn", name=f"{tag}_dwin", tm_cap=512, tn_cap=896, out_dtype=BF16,
                            carry=_carry_of(carries, "dwin")))
    grads["w_in"] = jnp.transpose(dw_in.reshape(2, D_MODEL // 2, N_CHIPS, IN_WIDTH // N_CHIPS), (0, 2, 1, 3))
    dx, grads["g_mix"] = _mm(dpb, lw["w_in"], mode="nt", name=f"{tag}_dh", tk_cap=896, tn_cap=1024,
                             norm_bwd=(x, lw["g_mix"], dy))
    grads["rel_bias"] = lw["bias_vjp"](dbias)[0]
    return dx, grads


def kernel(x, g_ffn1, w_ffn1_in, w_ffn1_out, g_mix, w_in, b_in, rel_bias, w_br_sb, w_br_ch, w_br_fox, w_out, g_ffn2, w_ffn2_in, w_ffn2_out, g_final, loss_target, m_g_ffn1, m_w_ffn1_in, m_w_ffn1_out, m_g_mix, m_w_in, m_b_in, m_rel_bias, m_w_br_sb, m_w_br_ch, m_w_br_fox, m_w_out, m_g_ffn2, m_w_ffn2_in, m_w_ffn2_out, m_g_final, v_g_ffn1, v_w_ffn1_in, v_w_ffn1_out, v_g_mix, v_w_in, v_b_in, v_rel_bias, v_w_br_sb, v_w_br_ch, v_w_br_fox, v_w_out, v_g_ffn2, v_w_ffn2_in, v_w_ffn2_out, v_g_final):
    names = ("g_ffn1", "w_ffn1_in", "w_ffn1_out", "g_mix", "w_in", "b_in", "rel_bias", "w_br_sb", "w_br_ch",
             "w_br_fox", "w_out", "g_ffn2", "w_ffn2_in", "w_ffn2_out", "g_final")
    w = dict(zip(names, (g_ffn1, w_ffn1_in, w_ffn1_out, g_mix, w_in, b_in, rel_bias, w_br_sb, w_br_ch,
                         w_br_fox, w_out, g_ffn2, w_ffn2_in, w_ffn2_out, g_final)))
    m = dict(zip(names, (m_g_ffn1, m_w_ffn1_in, m_w_ffn1_out, m_g_mix, m_w_in, m_b_in, m_rel_bias, m_w_br_sb,
                         m_w_br_ch, m_w_br_fox, m_w_out, m_g_ffn2, m_w_ffn2_in, m_w_ffn2_out, m_g_final)))
    v = dict(zip(names, (v_g_ffn1, v_w_ffn1_in, v_w_ffn1_out, v_g_mix, v_w_in, v_b_in, v_rel_bias, v_w_br_sb,
                         v_w_br_ch, v_w_br_fox, v_w_out, v_g_ffn2, v_w_ffn2_in, v_w_ffn2_out, v_g_final)))
    xs = x[0]
    tgt = loss_target[0]

    chip = (2 * lax.axis_index("x") + lax.axis_index("y")).astype(jnp.int32)[None]
    placed = [_place_own(w[n], chip, f"place_{n}") for n in BIG]
    bufs = [[p[l].reshape(N_CHIPS, 2, p[l].shape[1] // 2, p[l].shape[2]) for p in placed] for l in range(DEPTH)]
    padded_w_in = {}

    def layer_weights(l):
        lw = {n: b.reshape((N_CHIPS,) + w[n].shape[1:]) for n, b in zip(BIG, bufs[l])}
        if l not in padded_w_in:
            whole = jnp.concatenate([lw["w_in"][j] for j in range(N_CHIPS)], axis=1)
            forget_t = jnp.pad(whole[:, QKV_WIDTH:QKV_WIDTH + H_FOX].T, ((0, F_PAD - H_FOX), (0, 0)))
            padded_w_in[l] = (_pad_w_in(whole), forget_t)
        lw["w_in"], lw["w_forget_t"] = padded_w_in[l]
        lw["b_forget"] = jnp.pad(w["b_in"][l][QKV_WIDTH:QKV_WIDTH + H_FOX], (0, 8 - H_FOX))[:, None]
        lw["b_in"] = _pad_w_in(w["b_in"][l][None, :])
        lw["bias"], lw["bias_vjp"] = jax.vjp(_band_bias, w["rel_bias"][l])
        for n in ("g_ffn1", "g_mix", "g_ffn2"):
            lw[n] = w[n][l][None, :]
        return lw

    def landed_in(l, idx):
        def done(carry):
            for i, b in zip(idx, carry.out[0]):
                bufs[l][i] = b
        return done

    def over_ici(l, idx):
        return lambda: _Carry([], [bufs[l][i] for i in idx], [], 3 * len(idx), _gather_over_ici, landed_in(l, idx))

    def to_sibling(l, idx):
        return lambda: _Carry([], [bufs[l][i] for i in idx], [], 3 * len(idx), _gather_to_sibling, landed_in(l, idx))

    def ffn_fwd(x_in, lw, which, tag, carries):
        return _ffn_fwd(x_in, lw[f"g_{which}"], lw[f"w_{which}_in"], lw[f"w_{which}_out"], tag, carries)

    def ffn_bwd(x_in, lw, which, saved_acts, dy, tag, carries):
        return _ffn_bwd(x_in, lw[f"g_{which}"], lw[f"w_{which}_in"], lw[f"w_{which}_out"], saved_acts, dy, tag,
                        carries)

    FFN1, W_IN, MIXER_REST, FFN2_IN, FFN2_OUT = (0, 1), (2,), (3, 4, 5, 6), (7,), (8,)
    first = FFN1 + W_IN
    for i, b in zip(first, _gather_layer([bufs[0][i] for i in first], "gather_l0")):
        bufs[0][i] = b
    fwd_carries = [
        {"ffn1": {"in": [over_ici(0, MIXER_REST)], "out": [to_sibling(0, MIXER_REST), over_ici(0, FFN2_OUT)]},
         "mix": {"proj": [to_sibling(0, FFN2_OUT), over_ici(1, MIXER_REST)],
                 "sb": [over_ici(0, FFN2_IN), over_ici(1, FFN2_OUT)],
                 "ch": [to_sibling(0, FFN2_IN), over_ici(1, FFN1)],
                 "fox": [over_ici(1, W_IN)]},
         "ffn2": {"in": [to_sibling(1, MIXER_REST + FFN2_OUT + FFN1 + W_IN)]}},
        {"ffn1": {}, "mix": {"sb": [over_ici(1, FFN2_IN)], "ch": [to_sibling(1, FFN2_IN)]}, "ffn2": {}},
    ]

    saved = []
    act = xs
    for l in range(DEPTH):
        x0 = act
        x1, s1 = ffn_fwd(x0, layer_weights(l), "ffn1", f"l{l}_ffn1", fwd_carries[l]["ffn1"])
        x2, s2 = _mixer_fwd(x1, layer_weights(l), f"l{l}_mix", fwd_carries[l]["mix"])
        act, s3 = ffn_fwd(x2, layer_weights(l), "ffn2", f"l{l}_ffn2", fwd_carries[l]["ffn2"])
        saved.append((x0, s1, x1, s2, x2, s3))
    layers = [layer_weights(l) for l in range(DEPTH)]

    dact, dg_final, loss_row = _final_loss(act, w["g_final"][None, :], tgt, "final_loss")

    big_grads = {n: [None] * DEPTH for n in BIG}
    small_grads = {n: [None] * DEPTH for n in ("g_ffn1", "g_mix", "b_in", "rel_bias", "g_ffn2")}
    pair = [[None] * len(BIG) for _ in range(DEPTH)]
    landed = [[None] * len(BIG) for _ in range(DEPTH)]

    def pair_up(l, idx, tag):
        for i, p in zip(idx, _pair_sums([big_grads[BIG[i]][l] for i in idx], tag)):
            pair[l][i] = p

    core = lax.axis_index("c").astype(jnp.int32)[None]

    def to_sibling_half(l, idx, tag):
        def plan(srcs, bufs, new):
            x, y, c = _position()
            return [(s.at[1 - c], d, (x, y, 1 - c)) for s, d in zip(srcs, new)]
        def done(carry):
            for j, (i, got) in enumerate(zip(idx, carry.out[1])):
                pair[l][i] = _add_pair(core, big_grads[BIG[i]][l], got, f"grad_pair_sum_{tag}_{j}")
        def make():
            split = [big_grads[BIG[i]][l] for i in idx]
            return _Carry(split, [], [jax.ShapeDtypeStruct(s.shape[1:], s.dtype) for s in split], len(idx), plan, done)
        return make

    def exchange(l, idx):
        def done(carry):
            for i, a in zip(idx, carry.out[1]):
                landed[l][i] = a
        def make():
            carry = _scatter_carry([pair[l][i] for i in idx])
            carry.done = done
            return carry
        return make

    shared = {}

    def share_layer(l, tag):
        def plan(srcs, bufs, new):
            x, y, c = _position()
            return [(s, d, (x, y, 1 - c)) for s, d in zip(srcs, new)]
        def done(carry):
            shared[l] = (carry.srcs, carry.out[1])
        def make():
            sums = [_sum_chips(chip, p, q, f"grad_chip_sum_{tag}_{i}") for i, (p, q) in enumerate(zip(pair[l], landed[l]))]
            return _Carry(sums, [], [jax.ShapeDtypeStruct(a.shape, a.dtype) for a in sums], len(sums), plan, done)
        return make

    def exchange_one_way(l, i, k):
        def plan(srcs, bufs, new):
            x, y, c = _position()
            chip_k, idx_k = _other_chips(x, y)[k]
            return [(srcs[0].at[idx_k], bufs[0].at[k], (*chip_k, c))]
        def done(carry):
            landed[l][i] = carry.out[0][0]
        def make():
            if landed[l][i] is None:
                landed[l][i] = lax.empty((3,) + pair[l][i].shape[1:], BF16)
            return _Carry([pair[l][i]], [landed[l][i]], [], 1, plan, done)
        return make

    bwd_carries = [
        {"ffn2": {},
         "mix": {"dmerged": [to_sibling_half(0, FFN2_IN + FFN2_OUT, "l0_ffn2")],
                 "dsb": [exchange(1, FFN1 + W_IN)], "dch": [exchange(1, MIXER_REST + FFN2_IN + FFN2_OUT)],
                 "dfox": [exchange(0, FFN2_IN + FFN2_OUT)], "dwin": [share_layer(1, "l1")]},
         "ffn1": {"da": [exchange(0, MIXER_REST)], "dwout": [exchange_one_way(0, 2, 0)],
                  "dwin": [exchange_one_way(0, 2, 1)], "dh": [exchange_one_way(0, 2, 2)]}},
        {"ffn2": {}, "mix": {},
         "ffn1": {"da": [to_sibling_half(1, W_IN + MIXER_REST + FFN2_IN + FFN2_OUT, "l1_rest")]}},
    ]
    for l in reversed(range(DEPTH)):
        lw = layers[l]
        x0, s1, x1, s2, x2, s3 = saved[l]
        dx2, small_grads["g_ffn2"][l], big_grads["w_ffn2_in"][l], big_grads["w_ffn2_out"][l] = ffn_bwd(
            x2, lw, "ffn2", s3, dact, f"l{l}_ffn2", bwd_carries[l]["ffn2"])
        dx1, mg = _mixer_bwd(x1, lw, s2, dx2, f"l{l}_mix", bwd_carries[l]["mix"])
        for n in ("w_in", "w_br_sb", "w_br_ch", "w_br_fox", "w_out"):
            big_grads[n][l] = mg[n]
        small_grads["b_in"][l] = _unpad_w_in(mg["b_in"])[0]
        small_grads["g_mix"][l] = mg["g_mix"][0]
        small_grads["rel_bias"][l] = mg["rel_bias"]
        if l == 0:
            pair_up(0, W_IN + MIXER_REST, "l0_mix")
        dact, dg1, big_grads["w_ffn1_in"][l], big_grads["w_ffn1_out"][l] = ffn_bwd(
            x0, lw, "ffn1", s1, dx1, f"l{l}_ffn1", bwd_carries[l]["ffn1"])
        small_grads["g_ffn1"][l] = dg1[0]
        small_grads["g_ffn2"][l] = small_grads["g_ffn2"][l][0]
        if l == 1:
            pair_up(1, FFN1, "l1_ffn1")
    grad_x = dact[None]
    pair_up(0, FFN1, "l0_ffn1")
    for i, a in zip(FFN1, _scatter_chips([pair[0][i] for i in FFN1], "grad_scatter_l0")):
        landed[0][i] = a

    small = {n: jnp.stack(small_grads[n]) for n in small_grads}
    small["g_final"] = dg_final[0]
    small_sum, loss = _unpack_small(_allreduce_small(_pack_small(small, loss_row[0, :1]), "allreduce_small"),
                                    {n: w[n].shape for n in SMALL})

    mine, theirs = _finish_grads(pair[0], landed[0], "l0")
    mine, theirs = mine + shared[1][0], list(theirs) + shared[1][1]

    grads, delta, new_m, new_v = dict(small_sum), {}, {}, {}
    for i, n in enumerate(BIG):
        halves = [(mine[l * len(BIG) + i], theirs[l * len(BIG) + i]) for l in range(DEPTH)]
        grads[n], delta[n], new_m[n], new_v[n] = _adamw_from_halves(
            w[n], m[n], v[n], halves, 1 if n in ROW_SHARDED else 0, core, f"adamw_{n}")
    zero = jnp.zeros((1,), F32)
    sd, sm, sv = _adamw(_pack_small(w, zero), _pack_small(grads, zero), _pack_small(m, zero), _pack_small(v, zero),
                        "adamw_small")
    shapes = {n: w[n].shape for n in SMALL}
    for dst, src in ((delta, sd), (new_m, sm), (new_v, sv)):
        dst.update(_unpack_small(src, shapes)[0])

    return (loss, grad_x, *[grads[n] for n in names], *[delta[n] for n in names],
            *[new_m[n] for n in names], *[new_v[n] for n in names])
```
